```python
import jax, jax.numpy as jnp
from jax import lax
import numpy as np

D_MODEL = 1024
BATCH = 8
SEQ = 2048
DEPTH = 1

GRID_W = 64
N_ATT_HEADS = 8
ATT_HEAD_DIM = 64
D_ATT = N_ATT_HEADS * ATT_HEAD_DIM
WIN_H_MAX = 8
WIN_W = 16
Q_BLOCK_W = 16
KV_BLOCK_W = Q_BLOCK_W + WIN_W
N_COL_BLOCKS = GRID_W // Q_BLOCK_W
D_REC = D_MODEL
N_REC_BLOCKS = 16
REC_BLOCK = D_REC // N_REC_BLOCKS
CONV_W = 4
LRU_C = 8.0
N_DIR = 2
D_FF = 4 * D_MODEL
EPS = 1e-6
D_IN = 3 * D_ATT + 2 * D_REC + 2 * D_MODEL
SPLITS = [int(v) for v in np.cumsum([D_ATT, D_ATT, D_ATT, D_REC, D_REC, D_MODEL])]

kernel_name = "hybrid_natten_rglru_gated_encoder"


def rms_norm(x, g):
    x32 = x.astype(jnp.float32)
    y = x32 * lax.rsqrt(jnp.mean(x32 * x32, axis=-1, keepdims=True) + EPS)
    return (y * g.astype(jnp.float32)).astype(x.dtype)


def neighbourhood_attention(q, k, v, rpb):
    b, s, h, dh = q.shape
    rows = s // GRID_W
    kh = min(WIN_H_MAX, rows)
    r = np.arange(rows)
    row_start = np.clip(r - kh // 2, 0, rows - kh)
    row_idx = row_start[:, None] + np.arange(kh)[None, :]
    n = np.arange(N_COL_BLOCKS)
    col_start = np.clip(n * Q_BLOCK_W - WIN_W // 2, 0, GRID_W - KV_BLOCK_W)
    col_idx = col_start[:, None] + np.arange(KV_BLOCK_W)[None, :]
    qc = n[:, None] * Q_BLOCK_W + np.arange(Q_BLOCK_W)[None, :]
    win_start = np.clip(qc - WIN_W // 2, 0, GRID_W - WIN_W)
    kc = col_idx[:, None, :]
    valid = (kc >= win_start[..., None]) & (kc < win_start[..., None] + WIN_W)
    d_row = row_idx - r[:, None] + (WIN_H_MAX - 1)
    d_col = np.clip(kc - qc[..., None], -(WIN_W - 1), WIN_W - 1) + (WIN_W - 1)

    scale = ATT_HEAD_DIM ** -0.5
    q_blk = (q * scale).reshape(b, rows, N_COL_BLOCKS, Q_BLOCK_W, h, dh).transpose(0, 4, 1, 2, 3, 5)
    k_grid = k.reshape(b, rows, GRID_W, h, dh).transpose(0, 3, 1, 2, 4)
    v_grid = v.reshape(b, rows, GRID_W, h, dh).transpose(0, 3, 1, 2, 4)
    ri = row_idx[:, None, :, None]
    ci = col_idx[None, :, None, :]
    kg = k_grid[:, :, ri, ci]
    vg = v_grid[:, :, ri, ci]

    scores = jnp.einsum('bhrnqd,bhrnikd->bhrnqik', q_blk, kg).astype(jnp.float32)
    bias = rpb.astype(jnp.float32)[:, d_row[:, None, None, :, None], d_col[None, :, :, None, :]]
    scores = scores + bias[None]
    scores = jnp.where(valid[:, :, None, :], scores, -1e30)
    probs = jax.nn.softmax(scores, axis=(-2, -1)).astype(v.dtype)
    out = jnp.einsum('bhrnqik,bhrnikd->bhrnqd', probs, vg)
    return out.transpose(0, 2, 3, 4, 1, 5).reshape(b, s, h * dh)


def centred_depthwise_conv(u, w, bias):
    s = u.shape[1]
    left = CONV_W // 2
    right = CONV_W - 1 - left
    up = jnp.pad(u, ((0, 0), (left, right), (0, 0)))
    out = bias
    for j in range(CONV_W):
        out = out + up[:, j:j + s] * w[j]
    return out


def block_diag_linear(u, w, b):
    bsz, s, c = u.shape
    ub = u.reshape(bsz, s, N_REC_BLOCKS, REC_BLOCK)
    return jnp.einsum('bsnc,ncd->bsnd', ub, w).reshape(bsz, s, c) + b


def rg_lru(u, w_a, b_a, w_i, b_i, lam, reverse):
    r_gate = jax.nn.sigmoid(block_diag_linear(u, w_a, b_a)).astype(jnp.float32)
    i_gate = jax.nn.sigmoid(block_diag_linear(u, w_i, b_i))
    log_a = -LRU_C * r_gate * jax.nn.softplus(-lam.astype(jnp.float32))
    a = jnp.exp(log_a)
    mult = jnp.sqrt(jnp.maximum(-jnp.expm1(2.0 * log_a), 0.0))
    bx = mult * (i_gate * u).astype(jnp.float32)

    def combine(c1, c2):
        a1, b1 = c1
        a2, b2 = c2
        return a1 * a2, a2 * b1 + b2

    _, h = lax.associative_scan(combine, (a, bx), axis=1, reverse=reverse)
    return h.astype(u.dtype)


def _fwd_setup_inputs(seed: int = 0) -> dict:
    key = jax.random.key(seed)
    ks = jax.random.split(key, 20)
    f32 = jnp.float32
    nrm = lambda k, shape, fan_in: jax.random.normal(k, shape, f32) * (fan_in ** -0.5)
    x = jax.random.normal(ks[0], (BATCH, SEQ, D_MODEL), f32)
    ln1_g = 1.0 + 0.05 * jax.random.normal(ks[1], (DEPTH, D_MODEL), f32)
    w_in = nrm(ks[2], (DEPTH, D_MODEL, D_IN), D_MODEL)
    b_in = 0.02 * jax.random.normal(ks[3], (DEPTH, D_IN), f32)
    rpb = 0.02 * jax.random.normal(ks[4], (DEPTH, N_ATT_HEADS, 2 * WIN_H_MAX - 1, 2 * WIN_W - 1), f32)
    w_att_o = nrm(ks[5], (DEPTH, D_ATT, D_MODEL), D_ATT)
    conv_w = nrm(ks[6], (DEPTH, CONV_W, D_REC), CONV_W)
    conv_b = 0.02 * jax.random.normal(ks[7], (DEPTH, D_REC), f32)
    w_rg_a = nrm(ks[8], (DEPTH, N_DIR, N_REC_BLOCKS, REC_BLOCK, REC_BLOCK), REC_BLOCK)
    b_rg_a = 0.02 * jax.random.normal(ks[9], (DEPTH, N_DIR, D_REC), f32)
    w_rg_i = nrm(ks[10], (DEPTH, N_DIR, N_REC_BLOCKS, REC_BLOCK, REC_BLOCK), REC_BLOCK)
    b_rg_i = 0.02 * jax.random.normal(ks[11], (DEPTH, N_DIR, D_REC), f32)
    a_c = jax.random.uniform(ks[12], (DEPTH, N_DIR, D_REC), f32, 0.9, 0.999)
    a0 = a_c ** (1.0 / LRU_C)
    lru_lambda = jnp.log(a0) - jnp.log1p(-a0)
    w_rec_o = nrm(ks[13], (DEPTH, D_REC, D_MODEL), D_REC)
    w_out = nrm(ks[14], (DEPTH, D_MODEL, D_MODEL), D_MODEL)
    ln2_g = 1.0 + 0.05 * jax.random.normal(ks[15], (DEPTH, D_MODEL), f32)
    w_ff1 = nrm(ks[16], (DEPTH, D_MODEL, D_FF), D_MODEL)
    w_ff2 = nrm(ks[17], (DEPTH, D_FF, D_MODEL), D_FF)
    lnf_g = 1.0 + 0.05 * jax.random.normal(ks[18], (D_MODEL,), f32)
    return {"x": x, "ln1_g": ln1_g, "w_in": w_in, "b_in": b_in, "rpb": rpb,
            "w_att_o": w_att_o, "conv_w": conv_w, "conv_b": conv_b,
            "w_rg_a": w_rg_a, "b_rg_a": b_rg_a, "w_rg_i": w_rg_i, "b_rg_i": b_rg_i,
            "lru_lambda": lru_lambda, "w_rec_o": w_rec_o, "w_out": w_out,
            "ln2_g": ln2_g, "w_ff1": w_ff1, "w_ff2": w_ff2, "lnf_g": lnf_g}


def _fwd_reference(x, ln1_g, w_in, b_in, rpb, w_att_o, conv_w, conv_b, w_rg_a, b_rg_a,
              w_rg_i, b_rg_i, lru_lambda, w_rec_o, w_out, ln2_g, w_ff1, w_ff2, lnf_g):
    b, s, _ = x.shape
    for l in range(DEPTH):
        h = rms_norm(x, ln1_g[l])
        z = h @ w_in[l] + b_in[l]
        q, k, v, u, y_branch, g_att, g_rec = jnp.split(z, SPLITS, axis=-1)

        q = q.reshape(b, s, N_ATT_HEADS, ATT_HEAD_DIM)
        k = k.reshape(b, s, N_ATT_HEADS, ATT_HEAD_DIM)
        v = v.reshape(b, s, N_ATT_HEADS, ATT_HEAD_DIM)
        y_att = neighbourhood_attention(q, k, v, rpb[l]) @ w_att_o[l]

        u = centred_depthwise_conv(u, conv_w[l], conv_b[l])
        h_fwd = rg_lru(u, w_rg_a[l, 0], b_rg_a[l, 0], w_rg_i[l, 0], b_rg_i[l, 0], lru_lambda[l, 0], False)
        h_bwd = rg_lru(u, w_rg_a[l, 1], b_rg_a[l, 1], w_rg_i[l, 1], b_rg_i[l, 1], lru_lambda[l, 1], True)
        y_rec = ((h_fwd + h_bwd) * jax.nn.gelu(y_branch)) @ w_rec_o[l]

        mixed = jax.nn.sigmoid(g_att) * y_att + jax.nn.sigmoid(g_rec) * y_rec
        x = x + mixed @ w_out[l]

        h2 = rms_norm(x, ln2_g[l])
        x = x + jnp.square(jax.nn.relu(h2 @ w_ff1[l])) @ w_ff2[l]
    return rms_norm(x, lnf_g)


import jax as _jax
import jax.numpy as _jnp

TWIN_FORMAT = 'train_step'
FWD_PARAMS = ['x', 'ln1_g', 'w_in', 'b_in', 'rpb', 'w_att_o', 'conv_w', 'conv_b', 'w_rg_a', 'b_rg_a', 'w_rg_i', 'b_rg_i', 'lru_lambda', 'w_rec_o', 'w_out', 'ln2_g', 'w_ff1', 'w_ff2', 'lnf_g']
TWIN_WEIGHTS = ['ln1_g', 'w_in', 'b_in', 'rpb', 'w_att_o', 'conv_w', 'conv_b', 'w_rg_a', 'b_rg_a', 'w_rg_i', 'b_rg_i', 'lru_lambda', 'w_rec_o', 'w_out', 'ln2_g', 'w_ff1', 'w_ff2', 'lnf_g']
TWIN_DIFF_INPUT = 'x'
TWIN_INPUTS = ['x', 'ln1_g', 'w_in', 'b_in', 'rpb', 'w_att_o', 'conv_w', 'conv_b', 'w_rg_a', 'b_rg_a', 'w_rg_i', 'b_rg_i', 'lru_lambda', 'w_rec_o', 'w_out', 'ln2_g', 'w_ff1', 'w_ff2', 'lnf_g', 'loss_target', 'm_ln1_g', 'm_w_in', 'm_b_in', 'm_rpb', 'm_w_att_o', 'm_conv_w', 'm_conv_b', 'm_w_rg_a', 'm_b_rg_a', 'm_w_rg_i', 'm_b_rg_i', 'm_lru_lambda', 'm_w_rec_o', 'm_w_out', 'm_ln2_g', 'm_w_ff1', 'm_w_ff2', 'm_lnf_g', 'v_ln1_g', 'v_w_in', 'v_b_in', 'v_rpb', 'v_w_att_o', 'v_conv_w', 'v_conv_b', 'v_w_rg_a', 'v_b_rg_a', 'v_w_rg_i', 'v_b_rg_i', 'v_lru_lambda', 'v_w_rec_o', 'v_w_out', 'v_ln2_g', 'v_w_ff1', 'v_w_ff2', 'v_lnf_g']
TWIN_OUTPUTS = ['loss', 'grad_x', 'grad_ln1_g', 'grad_w_in', 'grad_b_in', 'grad_rpb', 'grad_w_att_o', 'grad_conv_w', 'grad_conv_b', 'grad_w_rg_a', 'grad_b_rg_a', 'grad_w_rg_i', 'grad_b_rg_i', 'grad_lru_lambda', 'grad_w_rec_o', 'grad_w_out', 'grad_ln2_g', 'grad_w_ff1', 'grad_w_ff2', 'grad_lnf_g', 'delta_ln1_g', 'delta_w_in', 'delta_b_in', 'delta_rpb', 'delta_w_att_o', 'delta_conv_w', 'delta_conv_b', 'delta_w_rg_a', 'delta_b_rg_a', 'delta_w_rg_i', 'delta_b_rg_i', 'delta_lru_lambda', 'delta_w_rec_o', 'delta_w_out', 'delta_ln2_g', 'delta_w_ff1', 'delta_w_ff2', 'delta_lnf_g', 'new_m_ln1_g', 'new_m_w_in', 'new_m_b_in', 'new_m_rpb', 'new_m_w_att_o', 'new_m_conv_w', 'new_m_conv_b', 'new_m_w_rg_a', 'new_m_b_rg_a', 'new_m_w_rg_i', 'new_m_b_rg_i', 'new_m_lru_lambda', 'new_m_w_rec_o', 'new_m_w_out', 'new_m_ln2_g', 'new_m_w_ff1', 'new_m_w_ff2', 'new_m_lnf_g', 'new_v_ln1_g', 'new_v_w_in', 'new_v_b_in', 'new_v_rpb', 'new_v_w_att_o', 'new_v_conv_w', 'new_v_conv_b', 'new_v_w_rg_a', 'new_v_b_rg_a', 'new_v_w_rg_i', 'new_v_b_rg_i', 'new_v_lru_lambda', 'new_v_w_rec_o', 'new_v_w_out', 'new_v_ln2_g', 'new_v_w_ff1', 'new_v_w_ff2', 'new_v_lnf_g']
TWIN_LEAF_KINDS = {'loss': 'loss', 'grad_x': 'grad_x', 'grad_ln1_g': 'grad_w', 'grad_w_in': 'grad_w', 'grad_b_in': 'grad_w', 'grad_rpb': 'grad_w', 'grad_w_att_o': 'grad_w', 'grad_conv_w': 'grad_w', 'grad_conv_b': 'grad_w', 'grad_w_rg_a': 'grad_w', 'grad_b_rg_a': 'grad_w', 'grad_w_rg_i': 'grad_w', 'grad_b_rg_i': 'grad_w', 'grad_lru_lambda': 'grad_w', 'grad_w_rec_o': 'grad_w', 'grad_w_out': 'grad_w', 'grad_ln2_g': 'grad_w', 'grad_w_ff1': 'grad_w', 'grad_w_ff2': 'grad_w', 'grad_lnf_g': 'grad_w', 'delta_ln1_g': 'delta_w', 'delta_w_in': 'delta_w', 'delta_b_in': 'delta_w', 'delta_rpb': 'delta_w', 'delta_w_att_o': 'delta_w', 'delta_conv_w': 'delta_w', 'delta_conv_b': 'delta_w', 'delta_w_rg_a': 'delta_w', 'delta_b_rg_a': 'delta_w', 'delta_w_rg_i': 'delta_w', 'delta_b_rg_i': 'delta_w', 'delta_lru_lambda': 'delta_w', 'delta_w_rec_o': 'delta_w', 'delta_w_out': 'delta_w', 'delta_ln2_g': 'delta_w', 'delta_w_ff1': 'delta_w', 'delta_w_ff2': 'delta_w', 'delta_lnf_g': 'delta_w', 'new_m_ln1_g': 'new_m', 'new_m_w_in': 'new_m', 'new_m_b_in': 'new_m', 'new_m_rpb': 'new_m', 'new_m_w_att_o': 'new_m', 'new_m_conv_w': 'new_m', 'new_m_conv_b': 'new_m', 'new_m_w_rg_a': 'new_m', 'new_m_b_rg_a': 'new_m', 'new_m_w_rg_i': 'new_m', 'new_m_b_rg_i': 'new_m', 'new_m_lru_lambda': 'new_m', 'new_m_w_rec_o': 'new_m', 'new_m_w_out': 'new_m', 'new_m_ln2_g': 'new_m', 'new_m_w_ff1': 'new_m', 'new_m_w_ff2': 'new_m', 'new_m_lnf_g': 'new_m', 'new_v_ln1_g': 'new_v', 'new_v_w_in': 'new_v', 'new_v_b_in': 'new_v', 'new_v_rpb': 'new_v', 'new_v_w_att_o': 'new_v', 'new_v_conv_w': 'new_v', 'new_v_conv_b': 'new_v', 'new_v_w_rg_a': 'new_v', 'new_v_b_rg_a': 'new_v', 'new_v_w_rg_i': 'new_v', 'new_v_b_rg_i': 'new_v', 'new_v_lru_lambda': 'new_v', 'new_v_w_rec_o': 'new_v', 'new_v_w_out': 'new_v', 'new_v_ln2_g': 'new_v', 'new_v_w_ff1': 'new_v', 'new_v_w_ff2': 'new_v', 'new_v_lnf_g': 'new_v'}


def _forward(args):
    return _fwd_reference(*[args[k] for k in FWD_PARAMS])


def _output_shape():
    out = _jax.eval_shape(lambda: _forward(_fwd_setup_inputs(0)))
    return out.shape, out.dtype

N_MICROBATCH = 1
ADAM_LR = 0.001
ADAM_B1 = 0.9
ADAM_B2 = 0.999
ADAM_EPS = 1e-08
ADAM_WD = 0.01
ADAM_STEP = 10
PER_EXAMPLE_BATCH_AXIS = {'x': 0, 'loss_target': 0}
SHARED_INPUTS = []
_WEIGHT_DTYPES = {'ln1_g': _jnp.float32, 'w_in': _jnp.float32, 'b_in': _jnp.float32, 'rpb': _jnp.float32, 'w_att_o': _jnp.float32, 'conv_w': _jnp.float32, 'conv_b': _jnp.float32, 'w_rg_a': _jnp.float32, 'b_rg_a': _jnp.float32, 'w_rg_i': _jnp.float32, 'b_rg_i': _jnp.float32, 'lru_lambda': _jnp.float32, 'w_rec_o': _jnp.float32, 'w_out': _jnp.float32, 'ln2_g': _jnp.float32, 'w_ff1': _jnp.float32, 'w_ff2': _jnp.float32, 'lnf_g': _jnp.float32}
MOMENT_SCALE = {'ln1_g': 6.984153e-02, 'w_in': 2.921700e-02, 'b_in': 3.822597e-01, 'rpb': 6.870551e-03, 'w_att_o': 1.443887e-02, 'conv_w': 5.426976e-02, 'conv_b': 8.207950e-01, 'w_rg_a': 1.623999e-02, 'b_rg_a': 1.180455e-02, 'w_rg_i': 2.982966e-02, 'b_rg_i': 1.184269e-02, 'lru_lambda': 1.910629e-02, 'w_rec_o': 5.506531e-02, 'w_out': 5.527076e-02, 'ln2_g': 1.107899e-01, 'w_ff1': 5.568195e-02, 'w_ff2': 1.408999e-01, 'lnf_g': 1.619869e+01}


def _to_microbatches(a, axis):
    t = _jnp.moveaxis(a, axis, 0)
    t = t.reshape((N_MICROBATCH, t.shape[0] // N_MICROBATCH) + t.shape[1:])
    return _jnp.moveaxis(t, 1, axis + 1)


def setup_inputs(seed: int = 0) -> dict:
    inp = _fwd_setup_inputs(seed)
    key = _jax.random.fold_in(_jax.random.key(seed), 7919)
    shape, _ = _output_shape()
    out = dict(inp)
    out["loss_target"] = _jax.random.normal(_jax.random.fold_in(key, 0), shape, _jnp.float32)
    for i, name in enumerate(TWIN_WEIGHTS):
        w = inp[name].astype(_jnp.float32)
        if MOMENT_SCALE is None:
            s = _jnp.sqrt(_jnp.mean(_jnp.square(w)) + 1e-30)
        else:
            s = MOMENT_SCALE[name]
        km, kv = _jax.random.split(_jax.random.fold_in(key, i + 1))
        out[name] = w
        out["m_" + name] = s * _jax.random.normal(km, w.shape, _jnp.float32)
        out["v_" + name] = (s * s) * _jax.random.uniform(kv, w.shape, _jnp.float32, 0.5, 1.5)
    if N_MICROBATCH > 1:
        for name, axis in PER_EXAMPLE_BATCH_AXIS.items():
            out[name] = _to_microbatches(out[name], axis)
    return {'x': out['x'], 'ln1_g': out['ln1_g'], 'w_in': out['w_in'], 'b_in': out['b_in'], 'rpb': out['rpb'], 'w_att_o': out['w_att_o'], 'conv_w': out['conv_w'], 'conv_b': out['conv_b'], 'w_rg_a': out['w_rg_a'], 'b_rg_a': out['b_rg_a'], 'w_rg_i': out['w_rg_i'], 'b_rg_i': out['b_rg_i'], 'lru_lambda': out['lru_lambda'], 'w_rec_o': out['w_rec_o'], 'w_out': out['w_out'], 'ln2_g': out['ln2_g'], 'w_ff1': out['w_ff1'], 'w_ff2': out['w_ff2'], 'lnf_g': out['lnf_g'], 'loss_target': out['loss_target'], 'm_ln1_g': out['m_ln1_g'], 'm_w_in': out['m_w_in'], 'm_b_in': out['m_b_in'], 'm_rpb': out['m_rpb'], 'm_w_att_o': out['m_w_att_o'], 'm_conv_w': out['m_conv_w'], 'm_conv_b': out['m_conv_b'], 'm_w_rg_a': out['m_w_rg_a'], 'm_b_rg_a': out['m_b_rg_a'], 'm_w_rg_i': out['m_w_rg_i'], 'm_b_rg_i': out['m_b_rg_i'], 'm_lru_lambda': out['m_lru_lambda'], 'm_w_rec_o': out['m_w_rec_o'], 'm_w_out': out['m_w_out'], 'm_ln2_g': out['m_ln2_g'], 'm_w_ff1': out['m_w_ff1'], 'm_w_ff2': out['m_w_ff2'], 'm_lnf_g': out['m_lnf_g'], 'v_ln1_g': out['v_ln1_g'], 'v_w_in': out['v_w_in'], 'v_b_in': out['v_b_in'], 'v_rpb': out['v_rpb'], 'v_w_att_o': out['v_w_att_o'], 'v_conv_w': out['v_conv_w'], 'v_conv_b': out['v_conv_b'], 'v_w_rg_a': out['v_w_rg_a'], 'v_b_rg_a': out['v_b_rg_a'], 'v_w_rg_i': out['v_w_rg_i'], 'v_b_rg_i': out['v_b_rg_i'], 'v_lru_lambda': out['v_lru_lambda'], 'v_w_rec_o': out['v_w_rec_o'], 'v_w_out': out['v_w_out'], 'v_ln2_g': out['v_ln2_g'], 'v_w_ff1': out['v_w_ff1'], 'v_w_ff2': out['v_w_ff2'], 'v_lnf_g': out['v_lnf_g']}


def _loss(weights, diff, rest, loss_target):
    with _jax.named_scope("forward"):
        args = {**rest, TWIN_DIFF_INPUT: diff, **{k: w.astype(_WEIGHT_DTYPES[k]) for k, w in weights.items()}}
        y = _forward(args)
    with _jax.named_scope("loss_head"):
        err = _jnp.square(y.astype(_jnp.float32) - loss_target)
        return 0.5 * _jnp.sum(_jnp.mean(err, axis=-1)) if err.ndim else 0.5 * err


def _adamw(w, g, m, v):
    m = ADAM_B1 * m + (1.0 - ADAM_B1) * g
    v = ADAM_B2 * v + (1.0 - ADAM_B2) * _jnp.square(g)
    m_hat = m / (1.0 - ADAM_B1 ** ADAM_STEP)
    v_hat = v / (1.0 - ADAM_B2 ** ADAM_STEP)
    delta = -ADAM_LR * (m_hat / (_jnp.sqrt(v_hat) + ADAM_EPS) + ADAM_WD * w)
    return delta, m, v


def reference(x, ln1_g, w_in, b_in, rpb, w_att_o, conv_w, conv_b, w_rg_a, b_rg_a, w_rg_i, b_rg_i, lru_lambda, w_rec_o, w_out, ln2_g, w_ff1, w_ff2, lnf_g, loss_target, m_ln1_g, m_w_in, m_b_in, m_rpb, m_w_att_o, m_conv_w, m_conv_b, m_w_rg_a, m_b_rg_a, m_w_rg_i, m_b_rg_i, m_lru_lambda, m_w_rec_o, m_w_out, m_ln2_g, m_w_ff1, m_w_ff2, m_lnf_g, v_ln1_g, v_w_in, v_b_in, v_rpb, v_w_att_o, v_conv_w, v_conv_b, v_w_rg_a, v_b_rg_a, v_w_rg_i, v_b_rg_i, v_lru_lambda, v_w_rec_o, v_w_out, v_ln2_g, v_w_ff1, v_w_ff2, v_lnf_g):
    given = dict(x=x, ln1_g=ln1_g, w_in=w_in, b_in=b_in, rpb=rpb, w_att_o=w_att_o, conv_w=conv_w, conv_b=conv_b, w_rg_a=w_rg_a, b_rg_a=b_rg_a, w_rg_i=w_rg_i, b_rg_i=b_rg_i, lru_lambda=lru_lambda, w_rec_o=w_rec_o, w_out=w_out, ln2_g=ln2_g, w_ff1=w_ff1, w_ff2=w_ff2, lnf_g=lnf_g, loss_target=loss_target, m_ln1_g=m_ln1_g, m_w_in=m_w_in, m_b_in=m_b_in, m_rpb=m_rpb, m_w_att_o=m_w_att_o, m_conv_w=m_conv_w, m_conv_b=m_conv_b, m_w_rg_a=m_w_rg_a, m_b_rg_a=m_b_rg_a, m_w_rg_i=m_w_rg_i, m_b_rg_i=m_b_rg_i, m_lru_lambda=m_lru_lambda, m_w_rec_o=m_w_rec_o, m_w_out=m_w_out, m_ln2_g=m_ln2_g, m_w_ff1=m_w_ff1, m_w_ff2=m_w_ff2, m_lnf_g=m_lnf_g, v_ln1_g=v_ln1_g, v_w_in=v_w_in, v_b_in=v_b_in, v_rpb=v_rpb, v_w_att_o=v_w_att_o, v_conv_w=v_conv_w, v_conv_b=v_conv_b, v_w_rg_a=v_w_rg_a, v_b_rg_a=v_b_rg_a, v_w_rg_i=v_w_rg_i, v_b_rg_i=v_b_rg_i, v_lru_lambda=v_lru_lambda, v_w_rec_o=v_w_rec_o, v_w_out=v_w_out, v_ln2_g=v_ln2_g, v_w_ff1=v_w_ff1, v_w_ff2=v_w_ff2, v_lnf_g=v_lnf_g)
    weights = {n: given[n] for n in TWIN_WEIGHTS}
    shared = {n: given[n] for n in SHARED_INPUTS}
    per_example = {n: given[n] for n in ['x']}
    grad_fn = _jax.value_and_grad(_loss, argnums=(0, 1))

    def one_microbatch(ex, loss_target):
        ex = dict(ex)
        diff = ex.pop(TWIN_DIFF_INPUT)
        return grad_fn(weights, diff, {**shared, **ex}, loss_target)

    if N_MICROBATCH == 1:
        loss, (grad_w, grad_x) = one_microbatch(per_example, given["loss_target"])
    else:
        def body(carry, xs):
            loss_sum, grad_sum = carry
            l_k, (gw_k, gx_k) = one_microbatch(xs[0], xs[1])
            with _jax.named_scope("update"):
                return (loss_sum + l_k, _jax.tree.map(_jnp.add, grad_sum, gw_k)), gx_k

        init = (_jnp.zeros((), _jnp.float32), _jax.tree.map(_jnp.zeros_like, weights))
        (loss, grad_w), grad_x = _jax.lax.scan(body, init, (per_example, given["loss_target"]))
    with _jax.named_scope("update"):
        delta_w, new_m, new_v = {}, {}, {}
        for n in TWIN_WEIGHTS:
            delta_w[n], new_m[n], new_v[n] = _adamw(weights[n], grad_w[n], given["m_" + n], given["v_" + n])
    return (loss, grad_x, *[grad_w[n] for n in TWIN_WEIGHTS], *[delta_w[n] for n in TWIN_WEIGHTS],
            *[new_m[n] for n in TWIN_WEIGHTS], *[new_v[n] for n in TWIN_WEIGHTS])
```

```python
import functools

import numpy as np
import jax
import jax.numpy as jnp
from jax import lax
from jax.experimental import pallas as pl
from jax.experimental.pallas import tpu as pltpu

F32 = jnp.float32
BF16 = jnp.bfloat16

T = 2048
D = 1024
D_ATT = 512
D_IN = 5632
D_FF = 4096
N_HEADS = 8
HEAD_DIM = 64
GRID_W = 64
N_ROWS = T // GRID_W
WIN_H = 8
WIN_W = 16
KEYS = WIN_H * GRID_W
N_CHIPS = 4
EPS = 1e-6
LRU_C = 8.0
SCALE = HEAD_DIM ** -0.5
REC_CB = 256
REC_CHUNK = 256
PAD = 8

ADAM_LR = 0.001
ADAM_B1 = 0.9
ADAM_B2 = 0.999
ADAM_EPS = 1e-08
ADAM_WD = 0.01
ADAM_STEP = 10

VMEM_LIMIT = 56 * 1024 * 1024

NN = (((1,), (0,)), ((), ()))
NT = (((1,), (1,)), ((), ()))
TN = (((0,), (0,)), ((), ()))
MESH = pl.DeviceIdType.MESH


def _params(sem=None):
    return pltpu.CompilerParams(dimension_semantics=sem, vmem_limit_bytes=VMEM_LIMIT)


def _dot(a, b, dims):
    return lax.dot_general(a, b, dims, preferred_element_type=F32)


def _sigmoid(x):
    return 1.0 / (1.0 + jnp.exp(-x))


def _matmul(name, a, b, *, dims, grid, a_spec, b_spec, out_shapes, out_specs, acc_shape,
            extras=(), extra_specs=(), epilogue=None, colsum_spec=None, colsum_shape=None):
    nk = grid[2]
    n_extra = len(extras)
    n_out = len(out_shapes)
    with_colsum = colsum_spec is not None

    def body(a_ref, b_ref, *rest):
        ex = rest[:n_extra]
        outs = rest[n_extra:n_extra + n_out]
        pos = n_extra + n_out
        cs_out = rest[pos] if with_colsum else None
        pos += 1 if with_colsum else 0
        acc = rest[pos]
        cs_acc = rest[pos + 1] if with_colsum else None
        k = pl.program_id(2)

        @pl.when(k == 0)
        def _():
            acc[...] = jnp.zeros_like(acc)
            if with_colsum:
                cs_acc[...] = jnp.zeros_like(cs_acc)

        bv = b_ref[...]
        acc[...] += _dot(a_ref[...].astype(BF16), bv.astype(BF16), dims)
        if with_colsum:
            cs_acc[...] += jnp.sum(bv.astype(F32), axis=0, keepdims=True)

        @pl.when(k == nk - 1)
        def _():
            r = acc[...]
            if epilogue is None:
                outs[0][...] = r.astype(outs[0].dtype)
            else:
                epilogue(r, ex, outs)
            if with_colsum:
                cs_out[...] = cs_acc[...]

    shapes = list(out_shapes)
    specs = list(out_specs)
    scratch = [pltpu.VMEM(acc_shape, F32)]
    if with_colsum:
        shapes.append(colsum_shape)
        specs.append(colsum_spec)
        scratch.append(pltpu.VMEM((1, acc_shape[1]), F32))
    res = pl.pallas_call(
        body, name=name, grid=grid,
        in_specs=[a_spec, b_spec, *extra_specs],
        out_specs=specs, out_shape=shapes, scratch_shapes=scratch,
        compiler_params=_params(("parallel", "parallel", "arbitrary")),
    )(a, b, *extras)
    return res


def _sds(shape, dtype):
    return jax.ShapeDtypeStruct(shape, dtype)


TM = 512
NI = T // TM


def _mm_nn_cols(name, a, wg, out_dtype, *, bias=None, extras=(), extra_specs=(), epilogue=None,
                out_shapes=None, out_specs=None):
    k_dim, n4 = wg.shape[1], wg.shape[2]
    ex, exs = list(extras), list(extra_specs)
    if bias is not None:
        ex = [bias] + ex
        exs = [pl.BlockSpec((1, n4), lambda j, i, k: (0, j))] + exs
        user_ep = epilogue

        def epilogue(r, e, outs):
            r = r + e[0][...]
            if user_ep is None:
                outs[0][...] = r.astype(outs[0].dtype)
            else:
                user_ep(r, e[1:], outs)
    if out_shapes is None:
        out_shapes = [_sds((T, N_CHIPS * n4), out_dtype)]
        out_specs = [pl.BlockSpec((TM, n4), lambda j, i, k: (i, j))]
    return _matmul(
        name, a, wg, dims=NN, grid=(N_CHIPS, NI, 1),
        a_spec=pl.BlockSpec((TM, k_dim), lambda j, i, k: (i, 0)),
        b_spec=pl.BlockSpec((None, k_dim, n4), lambda j, i, k: (j, 0, 0)),
        out_shapes=out_shapes, out_specs=out_specs, acc_shape=(TM, n4),
        extras=ex, extra_specs=exs, epilogue=epilogue)


def _mm_nn_rows(name, a, w, out_dtype, *, tk, extras=(), extra_specs=(), epilogue=None):
    k_dim, n = w.shape
    return _matmul(
        name, a, w, dims=NN, grid=(NI, 1, k_dim // tk),
        a_spec=pl.BlockSpec((TM, tk), lambda i, j, k: (i, k)),
        b_spec=pl.BlockSpec((tk, n), lambda i, j, k: (k, 0)),
        out_shapes=[_sds((T, n), out_dtype)],
        out_specs=[pl.BlockSpec((TM, n), lambda i, j, k: (i, 0))], acc_shape=(TM, n),
        extras=extras, extra_specs=extra_specs, epilogue=epilogue)


def _mm_nt_cols(name, a, wg, out_dtype):
    k_dim, n4 = wg.shape[1], wg.shape[2]
    return _matmul(
        name, a, wg, dims=NT, grid=(NI, 1, N_CHIPS),
        a_spec=pl.BlockSpec((TM, n4), lambda i, j, k: (i, k)),
        b_spec=pl.BlockSpec((None, k_dim, n4), lambda i, j, k: (k, 0, 0)),
        out_shapes=[_sds((T, k_dim), out_dtype)],
        out_specs=[pl.BlockSpec((TM, k_dim), lambda i, j, k: (i, 0))], acc_shape=(TM, k_dim))


def _mm_nt_rows(name, a, w, out_dtype, *, tn, extras=(), extra_specs=(), epilogue=None):
    k_dim, n = w.shape
    return _matmul(
        name, a, w, dims=NT, grid=(k_dim // tn, NI, 1),
        a_spec=pl.BlockSpec((TM, n), lambda j, i, k: (i, 0)),
        b_spec=pl.BlockSpec((tn, n), lambda j, i, k: (j, 0)),
        out_shapes=[_sds((T, k_dim), out_dtype)],
        out_specs=[pl.BlockSpec((TM, tn), lambda j, i, k: (i, j))], acc_shape=(TM, tn),
        extras=extras, extra_specs=extra_specs, epilogue=epilogue)


def _mm_tn_cols(name, a, g, n4, *, colsum=False):
    k_dim = a.shape[1]
    kw = {}
    if colsum:
        kw = dict(colsum_spec=pl.BlockSpec((1, n4), lambda j, i, k: (0, j)),
                  colsum_shape=_sds((1, N_CHIPS * n4), F32))
    return _matmul(
        name, a, g, dims=TN, grid=(N_CHIPS, 1, NI),
        a_spec=pl.BlockSpec((TM, k_dim), lambda j, i, k: (k, 0)),
        b_spec=pl.BlockSpec((TM, n4), lambda j, i, k: (k, j)),
        out_shapes=[_sds((N_CHIPS, k_dim, n4), F32)],
        out_specs=[pl.BlockSpec((None, k_dim, n4), lambda j, i, k: (j, 0, 0))],
        acc_shape=(k_dim, n4), **kw)


def _mm_tn_rows(name, a, g, *, tm):
    k_dim, n = a.shape[1], g.shape[1]
    return _matmul(
        name, a, g, dims=TN, grid=(k_dim // tm, 1, NI),
        a_spec=pl.BlockSpec((TM, tm), lambda j, i, k: (k, j)),
        b_spec=pl.BlockSpec((TM, n), lambda j, i, k: (k, 0)),
        out_shapes=[_sds((k_dim, n), F32)],
        out_specs=[pl.BlockSpec((tm, n), lambda j, i, k: (j, 0))], acc_shape=(tm, n))


TE = 256
NE = T // TE
_ROW = pl.BlockSpec((TE, D), lambda i: (i, 0))
_VEC = pl.BlockSpec((1, D), lambda i: (0, 0))


def _rms_fwd(name, x, g):
    def body(x_ref, g_ref, h_ref):
        xv = x_ref[...]
        rstd = lax.rsqrt(jnp.mean(xv * xv, axis=-1, keepdims=True) + EPS)
        h_ref[...] = (xv * rstd * g_ref[...]).astype(BF16)

    return pl.pallas_call(body, name=name, grid=(NE,), in_specs=[_ROW, _VEC], out_specs=_ROW,
                          out_shape=_sds((T, D), BF16), compiler_params=_params(("parallel",)))(x, g)


def _rms_bwd(name, dh, x, g, dres):
    def body(dh_ref, x_ref, g_ref, dres_ref, dx_ref, dg_ref):
        xv = x_ref[...]
        rstd = lax.rsqrt(jnp.mean(xv * xv, axis=-1, keepdims=True) + EPS)
        xhat = xv * rstd
        dhv = dh_ref[...]
        dy = dhv * g_ref[...]
        dx_ref[...] = dres_ref[...] + rstd * (dy - xhat * jnp.mean(dy * xhat, axis=-1, keepdims=True))

        @pl.when(pl.program_id(0) == 0)
        def _():
            dg_ref[...] = jnp.zeros_like(dg_ref)

        dg_ref[...] += jnp.sum(dhv * xhat, axis=0, keepdims=True)

    return pl.pallas_call(body, name=name, grid=(NE,), in_specs=[_ROW, _ROW, _VEC, _ROW],
                          out_specs=[_ROW, _VEC], out_shape=[_sds((T, D), F32), _sds((1, D), F32)],
                          compiler_params=_params(("arbitrary",)))(dh, x, g, dres)


def _loss_head(x2, target, g):
    def body(x_ref, t_ref, g_ref, loss_ref, dx_ref, dg_ref):
        xv = x_ref[...]
        rstd = lax.rsqrt(jnp.mean(xv * xv, axis=-1, keepdims=True) + EPS)
        xhat = xv * rstd
        gv = g_ref[...]
        err = xhat * gv - t_ref[...]
        dy = err * (1.0 / D)
        dxh = dy * gv
        dx_ref[...] = rstd * (dxh - xhat * jnp.mean(dxh * xhat, axis=-1, keepdims=True))

        @pl.when(pl.program_id(0) == 0)
        def _():
            dg_ref[...] = jnp.zeros_like(dg_ref)
            loss_ref[...] = jnp.zeros_like(loss_ref)

        dg_ref[...] += jnp.sum(dy * xhat, axis=0, keepdims=True)
        loss_ref[...] += (0.5 / D) * jnp.sum(jnp.sum(err * err, axis=1, keepdims=True), axis=0, keepdims=True)

    return pl.pallas_call(
        body, name="loss_head", grid=(NE,), in_specs=[_ROW, _ROW, _VEC],
        out_specs=[pl.BlockSpec((1, 1), lambda i: (0, 0)), _ROW, _VEC],
        out_shape=[_sds((1, 1), F32), _sds((T, D), F32), _sds((1, D), F32)],
        compiler_params=_params(("arbitrary",)))(x2, target, g)


MW = 512
_G_ATT_BLK = 3584 // MW
_G_REC_BLK = 4608 // MW


def _merge_specs():
    y = pl.BlockSpec((TM, MW), lambda i, j: (i, j))
    ga = pl.BlockSpec((TM, MW), lambda i, j: (i, _G_ATT_BLK + j))
    gr = pl.BlockSpec((TM, MW), lambda i, j: (i, _G_REC_BLK + j))
    return y, ga, gr


def _merge_fwd(y_att, y_rec, z):
    y, ga, gr = _merge_specs()

    def body(ya_ref, yr_ref, ga_ref, gr_ref, m_ref):
        m = _sigmoid(ga_ref[...]) * ya_ref[...] + _sigmoid(gr_ref[...]) * yr_ref[...]
        m_ref[...] = m.astype(BF16)

    return pl.pallas_call(body, name="merge_fwd", grid=(NI, D // MW), in_specs=[y, y, ga, gr], out_specs=y,
                          out_shape=_sds((T, D), BF16),
                          compiler_params=_params(("parallel", "parallel")))(y_att, y_rec, z, z)


def _merge_bwd(dm, y_att, y_rec, z):
    y, ga, gr = _merge_specs()

    def body(dm_ref, ya_ref, yr_ref, ga_ref, gr_ref, dya_ref, dyr_ref, dga_ref, dgr_ref):
        dmv = dm_ref[...]
        sa = _sigmoid(ga_ref[...])
        sr = _sigmoid(gr_ref[...])
        dya_ref[...] = (dmv * sa).astype(BF16)
        dyr_ref[...] = (dmv * sr).astype(BF16)
        dga_ref[...] = (dmv * ya_ref[...] * sa * (1.0 - sa)).astype(BF16)
        dgr_ref[...] = (dmv * yr_ref[...] * sr * (1.0 - sr)).astype(BF16)

    return pl.pallas_call(body, name="merge_bwd", grid=(NI, D // MW), in_specs=[y, y, y, ga, gr],
                          out_specs=[y, y, y, y], out_shape=[_sds((T, D), BF16)] * 4,
                          compiler_params=_params(("parallel", "parallel")))(dm, y_att, y_rec, z, z)


HP = 2 * HEAD_DIM
N_HP = N_HEADS // 2


def _bias_tables(rpb):
    qc = np.arange(GRID_W)[:, None]
    kc = np.arange(GRID_W)[None, :]
    win0 = np.clip(qc - WIN_W // 2, 0, GRID_W - WIN_W)
    valid = (kc >= win0) & (kc < win0 + WIN_W)
    dcol = np.clip(kc - qc, -(WIN_W - 1), WIN_W - 1) + (WIN_W - 1)
    drow = np.arange(WIN_H)[:, None] + np.arange(WIN_H)[None, :]
    tb = rpb[:, drow[:, None, :, None], dcol[None, :, None, :]]
    tb = jnp.where(valid[None, None, :, None, :], tb, -1e30)
    return tb.reshape(N_HEADS, WIN_H, GRID_W, KEYS)


def _row_window(r):
    rs = jnp.clip(r - WIN_H // 2, 0, N_ROWS - WIN_H)
    return pl.multiple_of(r * GRID_W, GRID_W), pl.multiple_of(rs * GRID_W, GRID_W), rs - r + (WIN_H - 1)


def _attn_probs(qh, kh, bias):
    s = _dot(qh, kh, NT) + bias
    e = jnp.exp(s - jnp.max(s, axis=-1, keepdims=True))
    return e / jnp.sum(e, axis=-1, keepdims=True)


def _attn_in_specs():
    q = pl.BlockSpec((T, HP), lambda p: (0, p))
    k = pl.BlockSpec((T, HP), lambda p: (0, N_HP + p))
    v = pl.BlockSpec((T, HP), lambda p: (0, 2 * N_HP + p))
    tb = pl.BlockSpec((2, WIN_H, GRID_W, KEYS), lambda p: (p, 0, 0, 0))
    return q, k, v, tb


def _attn_fwd(z, tb):
    def body(q_ref, k_ref, v_ref, tb_ref, o_ref, kb_ref, vb_ref):
        kb_ref[...] = k_ref[...].astype(BF16)
        vb_ref[...] = v_ref[...].astype(BF16)

        def row(r, carry):
            q0, k0, d0 = _row_window(r)
            q = (q_ref[pl.ds(q0, GRID_W), :] * SCALE).astype(BF16)
            kw = kb_ref[pl.ds(k0, KEYS), :]
            vw = vb_ref[pl.ds(k0, KEYS), :]
            outs = []
            for hh in range(2):
                ln = slice(hh * HEAD_DIM, (hh + 1) * HEAD_DIM)
                p = _attn_probs(q[:, ln], kw[:, ln], tb_ref[hh, d0])
                outs.append(_dot(p.astype(BF16), vw[:, ln], NN))
            o_ref[pl.ds(q0, GRID_W), :] = jnp.concatenate(outs, axis=1).astype(BF16)
            return carry

        lax.fori_loop(0, N_ROWS, row, 0)

    blk = pl.BlockSpec((T, HP), lambda p: (0, p))
    return pl.pallas_call(
        body, name="attn_fwd", grid=(N_HP,), in_specs=list(_attn_in_specs()), out_specs=blk,
        out_shape=_sds((T, D_ATT), BF16),
        scratch_shapes=[pltpu.VMEM((T, HP), BF16), pltpu.VMEM((T, HP), BF16)],
        compiler_params=_params(("parallel",)))(z, z, z, tb)


def _attn_bwd(z, tb, d_att):
    def body(q_ref, k_ref, v_ref, tb_ref, do_ref, dq_ref, dk_ref, dv_ref, ds_ref, kb_ref, vb_ref, dka_ref, dva_ref):
        kb_ref[...] = k_ref[...].astype(BF16)
        vb_ref[...] = v_ref[...].astype(BF16)
        dka_ref[...] = jnp.zeros_like(dka_ref)
        dva_ref[...] = jnp.zeros_like(dva_ref)
        ds_ref[...] = jnp.zeros_like(ds_ref)

        def row(r, carry):
            q0, k0, d0 = _row_window(r)
            q = (q_ref[pl.ds(q0, GRID_W), :] * SCALE).astype(BF16)
            kw = kb_ref[pl.ds(k0, KEYS), :]
            vw = vb_ref[pl.ds(k0, KEYS), :]
            do = do_ref[pl.ds(q0, GRID_W), :]
            dqs, dks, dvs = [], [], []
            for hh in range(2):
                ln = slice(hh * HEAD_DIM, (hh + 1) * HEAD_DIM)
                qh, kh, vh, doh = q[:, ln], kw[:, ln], vw[:, ln], do[:, ln]
                p = _attn_probs(qh, kh, tb_ref[hh, d0])
                dvs.append(_dot(p.astype(BF16), doh, TN))
                dp = _dot(doh, vh, NT)
                ds = p * (dp - jnp.sum(dp * p, axis=-1, keepdims=True))
                ds_ref[hh, d0] += ds
                dsb = ds.astype(BF16)
                dqs.append(_dot(dsb, kh, NN) * SCALE)
                dks.append(_dot(dsb, qh, TN))
            dq_ref[pl.ds(q0, GRID_W), :] = jnp.concatenate(dqs, axis=1).astype(BF16)
            dka_ref[pl.ds(k0, KEYS), :] += jnp.concatenate(dks, axis=1)
            dva_ref[pl.ds(k0, KEYS), :] += jnp.concatenate(dvs, axis=1)
            return carry

        lax.fori_loop(0, N_ROWS, row, 0)
        dk_ref[...] = dka_ref[...].astype(BF16)
        dv_ref[...] = dva_ref[...].astype(BF16)

    blk = pl.BlockSpec((T, HP), lambda p: (0, p))
    q, k, v, tbs = _attn_in_specs()
    return pl.pallas_call(
        body, name="attn_bwd", grid=(N_HP,), in_specs=[q, k, v, tbs, blk],
        out_specs=[blk, blk, blk, tbs],
        out_shape=[_sds((T, D_ATT), BF16)] * 3 + [_sds((N_HEADS, WIN_H, GRID_W, KEYS), F32)],
        scratch_shapes=[pltpu.VMEM((T, HP), BF16), pltpu.VMEM((T, HP), BF16),
                        pltpu.VMEM((T, HP), F32), pltpu.VMEM((T, HP), F32)],
        compiler_params=_params(("parallel",)))(z, z, z, tb, d_att)


def _rpb_selectors():
    sel = np.zeros((128, N_HEADS * WIN_H * WIN_H), np.float32)
    for h in range(N_HEADS):
        for d0 in range(WIN_H):
            for i in range(WIN_H):
                sel[h * 15 + d0 + i, (h * WIN_H + d0) * WIN_H + i] = 1.0
    diag = np.zeros((GRID_W * GRID_W, 128), np.float32)
    for qc in range(GRID_W):
        w0 = min(max(qc - WIN_W // 2, 0), GRID_W - WIN_W)
        for kc in range(w0, w0 + WIN_W):
            diag[qc * GRID_W + kc, kc - qc + WIN_W - 1] = 1.0
    return sel, diag


def _split3(x):
    a = x.astype(BF16)
    r = x - a.astype(F32)
    b = r.astype(BF16)
    c = (r - b.astype(F32)).astype(BF16)
    return a, b, c


def _rpb_grad(ds_acc):
    a = ds_acc.reshape(N_HEADS, WIN_H, GRID_W, WIN_H, GRID_W).transpose(0, 1, 3, 2, 4)
    a = a.reshape(N_HEADS * WIN_H * WIN_H, GRID_W * GRID_W)
    sel, diag = _rpb_selectors()

    def body(a_ref, sel_ref, diag_ref, o_ref):
        selv = sel_ref[...]
        g = sum(_dot(selv, part, NN) for part in _split3(a_ref[...]))
        diagv = diag_ref[...]
        o_ref[...] = sum(_dot(part, diagv, NN) for part in _split3(g))

    out = pl.pallas_call(body, name="rpb_grad", out_shape=_sds((128, 128), F32),
                         compiler_params=_params())(a, jnp.asarray(sel, BF16), jnp.asarray(diag, BF16))
    return out[:N_HEADS * 15, :2 * WIN_W - 1].reshape(N_HEADS, 15, 2 * WIN_W - 1)


N_CB = D // REC_CB
N_CHUNK = T // REC_CHUNK
N_TILE = T // 8
_U_BLK = 1536 // REC_CB
_Y_BLK = 2560 // REC_CB


def _block_diag(w):
    per = REC_CB // 64
    wt = w.reshape(2, N_CB, per, 64, 64)
    eye = jnp.eye(per, dtype=w.dtype)
    full = wt[:, :, :, :, None, :] * eye[None, None, :, None, :, None]
    return full.reshape(2, N_CB, REC_CB, REC_CB).astype(BF16)


def _block_diag_grad(g):
    per = REC_CB // 64
    g6 = g.reshape(2, N_CB, per, 64, per, 64)
    return jnp.stack([g6[:, :, p, :, p, :] for p in range(per)], axis=2).reshape(2, 16, 64, 64)


def _gelu(x):
    c = 0.7978845608028654
    return 0.5 * x * (1.0 + jnp.tanh(c * (x + 0.044715 * x * x * x)))


def _gelu_grad(x):
    c = 0.7978845608028654
    th = jnp.tanh(c * (x + 0.044715 * x * x * x))
    return 0.5 * (1.0 + th) + 0.5 * x * (1.0 - th * th) * c * (1.0 + 3.0 * 0.044715 * x * x)


def _softplus_neg(lam):
    x = -lam
    e = jnp.exp(-jnp.abs(x))
    w = 1.0 + e
    l1p = jnp.where(w == 1.0, e, jnp.log(w) * e / (w - 1.0))
    return jnp.maximum(x, 0.0) + l1p


def _one_minus_exp(x):
    poly = x * (1.0 + x * (1 / 2 + x * (1 / 6 + x * (1 / 24 + x * (1 / 120 + x * (1 / 720 + x * (1 / 5040)))))))
    return jnp.where(x > -0.25, -poly, 1.0 - jnp.exp(x))


def _conv_taps(pad_ref, t0, w, sign):
    out = None
    for j in range(4):
        term = w[j:j + 1, :] * pad_ref[pl.ds(PAD + t0 + sign * (j - 2), REC_CHUNK), :]
        out = term if out is None else out + term
    return out


def _gates(u, wa, wi, ba, bi, sp):
    ub = u.astype(BF16)
    r = _sigmoid(_dot(ub, wa, NN) + ba)
    i = _sigmoid(_dot(ub, wi, NN) + bi)
    log_a = -LRU_C * r * sp
    a = jnp.exp(log_a)
    mult = jnp.sqrt(jnp.maximum(_one_minus_exp(2.0 * log_a), 0.0))
    return r, i, a, mult


def _tile_scan(a, b, sub, reverse):
    for s in (1, 2, 4):
        if reverse:
            a_s, b_s, m = pltpu.roll(a, 8 - s, 0), pltpu.roll(b, 8 - s, 0), sub < 8 - s
        else:
            a_s, b_s, m = pltpu.roll(a, s, 0), pltpu.roll(b, s, 0), sub >= s
        b = jnp.where(m, a * b_s + b, b)
        a = jnp.where(m, a * a_s, a)
    return a, b


def _last_row(x, sub, row):
    return jnp.broadcast_to(jnp.sum(jnp.where(sub == row, x, 0.0), axis=0, keepdims=True), x.shape)


def _rec_prologue(up_ref, cw_ref, cb_ref, wa_ref, wi_ref, ba_ref, bi_ref, lam_ref,
                  upad_ref, u_ref, a_refs, h_refs):
    cb = up_ref.shape[1]
    zeros = jnp.zeros((PAD, cb), F32)
    upad_ref[pl.ds(0, PAD), :] = zeros
    upad_ref[pl.ds(PAD + T, PAD), :] = zeros
    upad_ref[pl.ds(PAD, T), :] = up_ref[...]
    cw = cw_ref[...]
    sp = _softplus_neg(lam_ref[...])
    for c in range(N_CHUNK):
        t0 = c * REC_CHUNK
        u = cb_ref[...] + _conv_taps(upad_ref, t0, cw, 1)
        u_ref[pl.ds(t0, REC_CHUNK), :] = u
        for d in range(2):
            _, i, a, mult = _gates(u, wa_ref[d], wi_ref[d], ba_ref[d:d + 1, :], bi_ref[d:d + 1, :], sp[d:d + 1, :])
            a_refs[d][pl.ds(t0, REC_CHUNK), :] = a
            h_refs[d][pl.ds(t0, REC_CHUNK), :] = mult * (i * u)

    sub = lax.broadcasted_iota(jnp.int32, (8, cb), 0)

    def tile(k, carry):
        cf, cr = carry
        tf = pl.multiple_of(k * 8, 8)
        tr = pl.multiple_of((N_TILE - 1 - k) * 8, 8)
        af, bf = _tile_scan(a_refs[0][pl.ds(tf, 8), :], h_refs[0][pl.ds(tf, 8), :], sub, False)
        hf = af * cf + bf
        h_refs[0][pl.ds(tf, 8), :] = hf
        ar, br = _tile_scan(a_refs[1][pl.ds(tr, 8), :], h_refs[1][pl.ds(tr, 8), :], sub, True)
        hr = ar * cr + br
        h_refs[1][pl.ds(tr, 8), :] = hr
        return _last_row(hf, sub, 7), _last_row(hr, sub, 0)

    z8 = jnp.zeros((8, cb), F32)
    lax.fori_loop(0, N_TILE, tile, (z8, z8))
    return sp


def _rec_specs():
    up = pl.BlockSpec((T, REC_CB), lambda c: (0, _U_BLK + c))
    yb = pl.BlockSpec((T, REC_CB), lambda c: (0, _Y_BLK + c))
    cw = pl.BlockSpec((4, REC_CB), lambda c: (0, c))
    cbias = pl.BlockSpec((1, REC_CB), lambda c: (0, c))
    wbd = pl.BlockSpec((2, None, REC_CB, REC_CB), lambda c: (0, c, 0, 0))
    vec2 = pl.BlockSpec((2, REC_CB), lambda c: (0, c))
    col = pl.BlockSpec((T, REC_CB), lambda c: (0, c))
    return up, yb, cw, cbias, wbd, vec2, col


def _rec_fwd(z, conv_w, conv_b, wa, wi, ba, bi, lam):
    up, yb, cw, cbias, wbd, vec2, col = _rec_specs()

    def body(up_ref, yb_ref, cw_ref, cb_ref, wa_ref, wi_ref, ba_ref, bi_ref, lam_ref, g_ref,
             upad_ref, u_ref, af_ref, ar_ref, hf_ref, hr_ref):
        _rec_prologue(up_ref, cw_ref, cb_ref, wa_ref, wi_ref, ba_ref, bi_ref, lam_ref,
                      upad_ref, u_ref, (af_ref, ar_ref), (hf_ref, hr_ref))

        def chunk(c, carry):
            t0 = pl.multiple_of(c * REC_CHUNK, REC_CHUNK)
            rows = pl.ds(t0, REC_CHUNK)
            g_ref[rows, :] = ((hf_ref[rows, :] + hr_ref[rows, :]) * _gelu(yb_ref[rows, :])).astype(BF16)
            return carry

        lax.fori_loop(0, N_CHUNK, chunk, 0)

    full = pltpu.VMEM((T, REC_CB), F32)
    return pl.pallas_call(
        body, name="rec_fwd", grid=(N_CB,),
        in_specs=[up, yb, cw, cbias, wbd, wbd, vec2, vec2, vec2], out_specs=col,
        out_shape=_sds((T, D), BF16),
        scratch_shapes=[pltpu.VMEM((T + 2 * PAD, REC_CB), F32), full, full, full, full, full],
        compiler_params=_params(("parallel",)))(z, z, conv_w, conv_b, wa, wi, ba, bi, lam)


def _rec_bwd(z, dg, conv_w, conv_b, wa, wi, ba, bi, lam):
    up, yb, cw, cbias, wbd, vec2, col = _rec_specs()

    def body(up_ref, yb_ref, dg_ref, cw_ref, cb_ref, wa_ref, wi_ref, ba_ref, bi_ref, lam_ref,
             dup_ref, dyb_ref, dcw_ref, dcb_ref, dwa_ref, dwi_ref, dba_ref, dbi_ref, dlam_ref,
             upad_ref, u_ref, af_ref, ar_ref, hf_ref, hr_ref, dh_ref, gf_ref, gr_ref, daf_ref, dar_ref, dupad_ref):
        a_refs, h_refs = (af_ref, ar_ref), (hf_ref, hr_ref)
        g_refs, da_refs = (gf_ref, gr_ref), (daf_ref, dar_ref)
        sp = _rec_prologue(up_ref, cw_ref, cb_ref, wa_ref, wi_ref, ba_ref, bi_ref, lam_ref,
                           upad_ref, u_ref, a_refs, h_refs)
        cb = up_ref.shape[1]

        def gate_chunk(c, carry):
            t0 = pl.multiple_of(c * REC_CHUNK, REC_CHUNK)
            rows = pl.ds(t0, REC_CHUNK)
            y = yb_ref[rows, :]
            dgv = dg_ref[rows, :].astype(F32)
            dh_ref[rows, :] = dgv * _gelu(y)
            dyb_ref[rows, :] = (dgv * (hf_ref[rows, :] + hr_ref[rows, :]) * _gelu_grad(y)).astype(BF16)
            return carry

        lax.fori_loop(0, N_CHUNK, gate_chunk, 0)

        sub = lax.broadcasted_iota(jnp.int32, (8, cb), 0)

        def tile(k, carry):
            cf, cr = carry
            kf = N_TILE - 1 - k
            tf = pl.multiple_of(kf * 8, 8)
            tnext = pl.multiple_of(jnp.minimum(kf + 1, N_TILE - 1) * 8, 8)
            tprev = pl.multiple_of(jnp.maximum(kf - 1, 0) * 8, 8)
            a_t = af_ref[pl.ds(tf, 8), :]
            a_n = jnp.where(kf < N_TILE - 1, af_ref[pl.ds(tnext, 8), :], 0.0)
            a_sh = jnp.where(sub == 7, pltpu.roll(a_n, 7, 0), pltpu.roll(a_t, 7, 0))
            ca, cbb = _tile_scan(a_sh, dh_ref[pl.ds(tf, 8), :], sub, True)
            gf = ca * cf + cbb
            h_t = hf_ref[pl.ds(tf, 8), :]
            h_p = jnp.where(kf > 0, hf_ref[pl.ds(tprev, 8), :], 0.0)
            h_sh = jnp.where(sub == 0, pltpu.roll(h_p, 1, 0), pltpu.roll(h_t, 1, 0))
            gf_ref[pl.ds(tf, 8), :] = gf
            daf_ref[pl.ds(tf, 8), :] = gf * h_sh
            tr = pl.multiple_of(k * 8, 8)
            rnext = pl.multiple_of(jnp.minimum(k + 1, N_TILE - 1) * 8, 8)
            rprev = pl.multiple_of(jnp.maximum(k - 1, 0) * 8, 8)
            b_t = ar_ref[pl.ds(tr, 8), :]
            b_p = jnp.where(k > 0, ar_ref[pl.ds(rprev, 8), :], 0.0)
            b_sh = jnp.where(sub == 0, pltpu.roll(b_p, 1, 0), pltpu.roll(b_t, 1, 0))
            ra, rb = _tile_scan(b_sh, dh_ref[pl.ds(tr, 8), :], sub, False)
            gr = ra * cr + rb
            hr_t = hr_ref[pl.ds(tr, 8), :]
            hr_n = jnp.where(k < N_TILE - 1, hr_ref[pl.ds(rnext, 8), :], 0.0)
            hr_sh = jnp.where(sub == 7, pltpu.roll(hr_n, 7, 0), pltpu.roll(hr_t, 7, 0))
            gr_ref[pl.ds(tr, 8), :] = gr
            dar_ref[pl.ds(tr, 8), :] = gr * hr_sh
            return _last_row(gf, sub, 0), _last_row(gr, sub, 7)

        z8 = jnp.zeros((8, cb), F32)
        lax.fori_loop(0, N_TILE, tile, (z8, z8))

        zeros = jnp.zeros((PAD, cb), F32)
        dupad_ref[pl.ds(0, PAD), :] = zeros
        dupad_ref[pl.ds(PAD + T, PAD), :] = zeros
        dwa_ref[...] = jnp.zeros_like(dwa_ref)
        dwi_ref[...] = jnp.zeros_like(dwi_ref)
        dba_ref[...] = jnp.zeros_like(dba_ref)
        dbi_ref[...] = jnp.zeros_like(dbi_ref)
        dlam_ref[...] = jnp.zeros_like(dlam_ref)

        def grad_chunk(c, carry):
            t0 = pl.multiple_of(c * REC_CHUNK, REC_CHUNK)
            rows = pl.ds(t0, REC_CHUNK)
            u = u_ref[rows, :]
            ub = u.astype(BF16)
            du = jnp.zeros((REC_CHUNK, cb), F32)
            for d in range(2):
                r, i, a, mult = _gates(u, wa_ref[d], wi_ref[d], ba_ref[d:d + 1, :], bi_ref[d:d + 1, :], sp[d:d + 1, :])
                dbx = g_refs[d][rows, :]
                dmult = dbx * (i * u)
                diu = dbx * mult
                a2 = a * a
                dlog = da_refs[d][rows, :] * a - dmult * jnp.where(mult > 0.0, a2 / mult, 0.0)
                dpa = (dlog * (-LRU_C) * sp[d:d + 1, :]) * r * (1.0 - r)
                dpi = (diu * u) * i * (1.0 - i)
                dpab, dpib = dpa.astype(BF16), dpi.astype(BF16)
                du = du + diu * i + _dot(dpab, wa_ref[d], NT) + _dot(dpib, wi_ref[d], NT)
                dwa_ref[d] += _dot(ub, dpab, TN)
                dwi_ref[d] += _dot(ub, dpib, TN)
                dba_ref[d:d + 1, :] += jnp.sum(dpa, axis=0, keepdims=True)
                dbi_ref[d:d + 1, :] += jnp.sum(dpi, axis=0, keepdims=True)
                dlam_ref[d:d + 1, :] += jnp.sum(dlog * r, axis=0, keepdims=True)
            dupad_ref[pl.ds(PAD + t0, REC_CHUNK), :] = du
            return carry

        lax.fori_loop(0, N_CHUNK, grad_chunk, 0)
        dlam_ref[...] = dlam_ref[...] * (LRU_C * _sigmoid(-lam_ref[...]))

        cw = cw_ref[...]
        dcb = jnp.zeros((1, cb), F32)
        dcw = [jnp.zeros((1, cb), F32) for _ in range(4)]
        for c in range(N_CHUNK):
            t0 = c * REC_CHUNK
            du = dupad_ref[pl.ds(PAD + t0, REC_CHUNK), :]
            dcb = dcb + jnp.sum(du, axis=0, keepdims=True)
            for j in range(4):
                dcw[j] = dcw[j] + jnp.sum(du * upad_ref[pl.ds(PAD + t0 + j - 2, REC_CHUNK), :], axis=0, keepdims=True)
            dup_ref[pl.ds(t0, REC_CHUNK), :] = _conv_taps(dupad_ref, t0, cw, -1).astype(BF16)
        dcb_ref[...] = dcb
        dcw_ref[...] = jnp.concatenate(dcw, axis=0)

    full = pltpu.VMEM((T, REC_CB), F32)
    padded = pltpu.VMEM((T + 2 * PAD, REC_CB), F32)
    return pl.pallas_call(
        body, name="rec_bwd", grid=(N_CB,),
        in_specs=[up, yb, col, cw, cbias, wbd, wbd, vec2, vec2, vec2],
        out_specs=[col, col, cw, cbias, wbd, wbd, vec2, vec2, vec2],
        out_shape=[_sds((T, D), BF16), _sds((T, D), BF16), _sds((4, D), F32), _sds((1, D), F32),
                   _sds((2, N_CB, REC_CB, REC_CB), F32), _sds((2, N_CB, REC_CB, REC_CB), F32),
                   _sds((2, D), F32), _sds((2, D), F32), _sds((2, D), F32)],
        scratch_shapes=[padded, full, full, full, full, full, full, full, full, full, full, padded],
        compiler_params=_params(("parallel",)))(z, z, dg, conv_w, conv_b, wa, wi, ba, bi, lam)


def _local_step(x, target, p):
    x = x.reshape(T, D)
    target = target.reshape(T, D)
    tb = _bias_tables(p["rpb"])
    wa, wi = _block_diag(p["w_rg_a"]), _block_diag(p["w_rg_i"])
    rec_params = (p["conv_w"], p["conv_b"], wa, wi, p["b_rg_a"], p["b_rg_i"], p["lru_lambda"])

    h1 = _rms_fwd("rms1_fwd", x, p["ln1_g"])
    (z,) = _mm_nn_cols("mm_z", h1, p["w_in"], F32, bias=p["b_in"])
    att = _attn_fwd(z, tb)
    g = _rec_fwd(z, *rec_params)
    (y_att,) = _mm_nn_cols("mm_y_att", att, p["w_att_o"], F32)
    (y_rec,) = _mm_nn_rows("mm_y_rec", g, p["w_rec_o"], F32, tk=D)
    mixed = _merge_fwd(y_att, y_rec, z)

    def add_res(r, ex, outs):
        outs[0][...] = ex[0][...] + r

    res_spec = pl.BlockSpec((TM, D), lambda i, j, k: (i, 0))
    (x1,) = _mm_nn_rows("mm_x1", mixed, p["w_out"], F32, tk=D, extras=[x], extra_specs=[res_spec], epilogue=add_res)
    h2 = _rms_fwd("rms2_fwd", x1, p["ln2_g"])

    def relu2(r, ex, outs):
        outs[0][...] = r
        rp = jnp.maximum(r, 0.0)
        outs[1][...] = (rp * rp).astype(BF16)

    ff_blk = pl.BlockSpec((TM, D), lambda j, i, k: (i, j))
    f, s = _mm_nn_cols("mm_ff1", h2, p["w_ff1"], F32, epilogue=relu2,
                       out_shapes=[_sds((T, D_FF), F32), _sds((T, D_FF), BF16)], out_specs=[ff_blk, ff_blk])
    (x2,) = _mm_nn_rows("mm_x2", s, p["w_ff2"], F32, tk=D, extras=[x1], extra_specs=[res_spec], epilogue=add_res)
    loss, dx2, g_lnf = _loss_head(x2, target, p["lnf_g"])

    def relu2_bwd(r, ex, outs):
        outs[0][...] = (r * 2.0 * jnp.maximum(ex[0][...], 0.0)).astype(BF16)

    (df,) = _mm_nt_rows("mm_df", dx2, p["w_ff2"], BF16, tn=D, extras=[f],
                        extra_specs=[pl.BlockSpec((TM, D), lambda j, i, k: (i, j))], epilogue=relu2_bwd)
    (g_w_ff2,) = _mm_tn_rows("mm_g_ff2", s, dx2, tm=D)
    (g_w_ff1,) = _mm_tn_cols("mm_g_ff1", h2, df, D)
    (dh2,) = _mm_nt_cols("mm_dh2", df, p["w_ff1"], F32)
    dx1, g_ln2 = _rms_bwd("rms2_bwd", dh2, x1, p["ln2_g"], dx2)

    (dmixed,) = _mm_nt_rows("mm_dmixed", dx1, p["w_out"], F32, tn=D)
    (g_w_out,) = _mm_tn_rows("mm_g_out", mixed, dx1, tm=D)
    dy_att, dy_rec, dg_att, dg_rec = _merge_bwd(dmixed, y_att, y_rec, z)
    (d_att,) = _mm_nt_cols("mm_d_att", dy_att, p["w_att_o"], BF16)
    (g_w_att_o,) = _mm_tn_cols("mm_g_att_o", att, dy_att, D // N_CHIPS)
    (d_g,) = _mm_nt_rows("mm_d_g", dy_rec, p["w_rec_o"], BF16, tn=D)
    (g_w_rec_o,) = _mm_tn_rows("mm_g_rec_o", g, dy_rec, tm=D)

    dq, dk, dv, ds_acc = _attn_bwd(z, tb, d_att)
    g_rpb = _rpb_grad(ds_acc)
    d_up, d_yb, g_conv_w, g_conv_b, g_wa, g_wi, g_ba, g_bi, g_lam = _rec_bwd(z, d_g, *rec_params)
    dz = jnp.concatenate([dq, dk, dv, d_up, d_yb, dg_att, dg_rec], axis=1)

    g_w_in, g_b_in = _mm_tn_cols("mm_g_in", h1, dz, D_IN // N_CHIPS, colsum=True)
    (dh1,) = _mm_nt_cols("mm_dh1", dz, p["w_in"], F32)
    grad_x, g_ln1 = _rms_bwd("rms1_bwd", dh1, x, p["ln1_g"], dx1)

    grads = dict(ln1_g=g_ln1, w_in=g_w_in, b_in=g_b_in, rpb=g_rpb, w_att_o=g_w_att_o, conv_w=g_conv_w,
                 conv_b=g_conv_b, w_rg_a=_block_diag_grad(g_wa), b_rg_a=g_ba, w_rg_i=_block_diag_grad(g_wi),
                 b_rg_i=g_bi, lru_lambda=g_lam, w_rec_o=g_w_rec_o, w_out=g_w_out, ln2_g=g_ln2,
                 w_ff1=g_w_ff1, w_ff2=g_w_ff2, lnf_g=g_lnf)
    return loss, grad_x.reshape(1, T, D), grads


_ANY = pl.BlockSpec(memory_space=pl.ANY)
N_PEERS = N_CHIPS - 1


def _place():
    x, y, c = lax.axis_index("x"), lax.axis_index("y"), lax.axis_index("c")
    peers = [(1 - x, y), (x, 1 - y), (1 - x, 1 - y)]
    return x, y, c, 2 * x + y, peers


def _remote(src, dst, send_sem, recv_sem, dev):
    return pltpu.make_async_remote_copy(src_ref=src, dst_ref=dst, send_sem=send_sem, recv_sem=recv_sem,
                                        device_id=dev, device_id_type=MESH)


def _cast_bf16(name, w):
    rows, cols = w.shape
    rb = min(rows, 256)
    blk = pl.BlockSpec((rb, cols), lambda i: (i, 0))

    def body(w_ref, o_ref):
        o_ref[...] = w_ref[...].astype(BF16)

    return pl.pallas_call(body, name=name, grid=(rows // rb,), in_specs=[blk], out_specs=blk,
                          out_shape=_sds((rows, cols), BF16), compiler_params=_params(("parallel",)))(w)


def _all_gather(shards):
    n = len(shards)

    def body(*refs):
        ins, outs = refs[:n], refs[n:2 * n]
        ici_send, ici_recv, d2d_send, d2d_recv, local_sem = refs[2 * n:]
        x, y, c, chip, peers = _place()
        sibling = (x, y, 1 - c)
        locals_ = [pltpu.make_async_copy(ins[t], outs[t].at[chip], local_sem.at[t]) for t in range(n)]
        for cp in locals_:
            cp.start()

        def halves(t):
            half = ins[t].shape[0] // 2
            return pl.ds(c * half, half), pl.ds((1 - c) * half, half)

        sends = []
        for t in range(n):
            mine, _ = halves(t)
            for r, (px, py) in enumerate(peers):
                k = t * N_PEERS + r
                sends.append(_remote(ins[t].at[mine], outs[t].at[chip, mine], ici_send.at[k], ici_recv.at[k], (px, py, c)))
                sends[-1].start()
        for t in range(n):
            mine, _ = halves(t)
            for r, (px, py) in enumerate(peers):
                k = t * N_PEERS + r
                landed = outs[t].at[2 * px + py, mine]
                _remote(ins[t].at[mine], landed, ici_send.at[k], ici_recv.at[k], (px, py, c)).wait_recv()
                sends.append(_remote(landed, landed, d2d_send.at[k], d2d_recv.at[k], sibling))
                sends[-1].start()
        for t in range(n):
            _, theirs = halves(t)
            for r, (px, py) in enumerate(peers):
                k = t * N_PEERS + r
                landed = outs[t].at[2 * px + py, theirs]
                _remote(landed, landed, d2d_send.at[k], d2d_recv.at[k], sibling).wait_recv()
        for cp in sends:
            cp.wait_send()
        for cp in locals_:
            cp.wait()

    nk = n * N_PEERS
    return pl.pallas_call(
        body, name="all_gather_weights", in_specs=[_ANY] * n, out_specs=[_ANY] * n,
        out_shape=[_sds((N_CHIPS,) + s.shape, s.dtype) for s in shards],
        scratch_shapes=[pltpu.SemaphoreType.DMA((nk,)), pltpu.SemaphoreType.DMA((nk,)),
                        pltpu.SemaphoreType.DMA((nk,)), pltpu.SemaphoreType.DMA((nk,)),
                        pltpu.SemaphoreType.DMA((n,))],
        )(*shards)


def _pair_exchange(grads, small):
    n = len(grads)

    def body(*refs):
        ins, small_in = refs[:n], refs[n]
        own, got, small_got = refs[n + 1:2 * n + 1], refs[2 * n + 1:3 * n + 1], refs[3 * n + 1]
        send_sem, recv_sem, local_sem = refs[3 * n + 2:]
        x, y, c, _, _ = _place()
        sibling = (x, y, 1 - c)
        copies, locals_ = [], []
        for t in range(n):
            half = ins[t].shape[1] // 2
            for j in range(N_CHIPS):
                k = t * N_CHIPS + j
                locals_.append(pltpu.make_async_copy(ins[t].at[j, pl.ds(c * half, half)], own[t].at[j], local_sem.at[k]))
                copies.append(_remote(ins[t].at[j, pl.ds((1 - c) * half, half)], got[t].at[j],
                                      send_sem.at[k], recv_sem.at[k], sibling))
        copies.append(_remote(small_in, small_got, send_sem.at[n * N_CHIPS], recv_sem.at[n * N_CHIPS], sibling))
        for cp in copies + locals_:
            cp.start()
        for cp in copies:
            cp.wait_recv()
        for cp in copies:
            cp.wait_send()
        for cp in locals_:
            cp.wait()

    half_shapes = [_sds((N_CHIPS, g.shape[1] // 2, g.shape[2]), F32) for g in grads]
    ns = n * N_CHIPS + 1
    res = pl.pallas_call(
        body, name="grad_pair_exchange", in_specs=[_ANY] * (n + 1), out_specs=[_ANY] * (2 * n + 1),
        out_shape=half_shapes + half_shapes + [_sds(small.shape, F32)],
        scratch_shapes=[pltpu.SemaphoreType.DMA((ns,)), pltpu.SemaphoreType.DMA((ns,)),
                        pltpu.SemaphoreType.DMA((n * N_CHIPS,))],
        )(*grads, small)
    return res[:n], res[n:2 * n], res[2 * n]


def _chip_exchange(sums_f32, sums_bf16, small):
    n = len(sums_f32)

    def body(*refs):
        f32_in, bf_in, small_in = refs[:n], refs[n:2 * n], refs[2 * n]
        own, got = refs[2 * n + 1:3 * n + 1], refs[3 * n + 1:4 * n + 1]
        small_own, small_got = refs[4 * n + 1], refs[4 * n + 2]
        send_sem, recv_sem, local_sem = refs[4 * n + 3:]
        x, y, c, chip, peers = _place()
        half_s = small_in.shape[0] // 2
        locals_ = [pltpu.make_async_copy(f32_in[t].at[chip], own[t], local_sem.at[t]) for t in range(n)]
        locals_.append(pltpu.make_async_copy(small_in.at[pl.ds(c * half_s, half_s)], small_own, local_sem.at[n]))
        copies = []
        for r, (px, py) in enumerate(peers):
            for t in range(n):
                k = t * N_PEERS + r
                copies.append(_remote(bf_in[t].at[2 * px + py], got[t].at[r], send_sem.at[k], recv_sem.at[k], (px, py, c)))
            k = n * N_PEERS + r
            copies.append(_remote(small_in.at[pl.ds(c * half_s, half_s)], small_got.at[r],
                                  send_sem.at[k], recv_sem.at[k], (px, py, c)))
        for cp in copies + locals_:
            cp.start()
        for cp in copies:
            cp.wait_recv()
        for cp in copies:
            cp.wait_send()
        for cp in locals_:
            cp.wait()

    ns = (n + 1) * N_PEERS
    half_s = small.shape[0] // 2
    res = pl.pallas_call(
        body, name="grad_chip_exchange", in_specs=[_ANY] * (2 * n + 1), out_specs=[_ANY] * (2 * n + 2),
        out_shape=([_sds(s.shape[1:], F32) for s in sums_f32]
                   + [_sds((N_PEERS,) + s.shape[1:], BF16) for s in sums_f32]
                   + [_sds((half_s, 128), F32), _sds((N_PEERS, half_s, 128), F32)]),
        scratch_shapes=[pltpu.SemaphoreType.DMA((ns,)), pltpu.SemaphoreType.DMA((ns,)),
                        pltpu.SemaphoreType.DMA((n + 1,))],
        )(*sums_f32, *sums_bf16, small)
    return res[:n], res[n:2 * n], res[2 * n], res[2 * n + 1]


def _half_swap(halves):
    n = len(halves)

    def body(*refs):
        ins, outs = refs[:n], refs[n:2 * n]
        send_sem, recv_sem, local_sem = refs[2 * n:]
        x, y, c, _, _ = _place()
        sibling = (x, y, 1 - c)
        copies, locals_ = [], []
        for t in range(n):
            h = ins[t].shape[0]
            mine = outs[t].at[pl.ds(c * h, h)]
            locals_.append(pltpu.make_async_copy(ins[t], mine, local_sem.at[t]))
            copies.append(_remote(ins[t], mine, send_sem.at[t], recv_sem.at[t], sibling))
        for cp in copies + locals_:
            cp.start()
        for t in range(n):
            h = ins[t].shape[0]
            theirs = outs[t].at[pl.ds((1 - c) * h, h)]
            _remote(ins[t], theirs, send_sem.at[t], recv_sem.at[t], sibling).wait_recv()
        for cp in copies:
            cp.wait_send()
        for cp in locals_:
            cp.wait()

    return pl.pallas_call(
        body, name="grad_half_swap", in_specs=[_ANY] * n, out_specs=[_ANY] * n,
        out_shape=[_sds((2 * h.shape[0], h.shape[1]), h.dtype) for h in halves],
        scratch_shapes=[pltpu.SemaphoreType.DMA((n,)), pltpu.SemaphoreType.DMA((n,)), pltpu.SemaphoreType.DMA((n,))],
        )(*halves)


def _pair_sum(name, own, got):
    _, rows, cols = own.shape
    rb = min(rows, 256)
    blk = pl.BlockSpec((None, rb, cols), lambda j, i: (j, i, 0))

    def body(a_ref, b_ref, s_ref, sb_ref):
        s = a_ref[...] + b_ref[...]
        s_ref[...] = s
        sb_ref[...] = s.astype(BF16)

    return pl.pallas_call(body, name=name, grid=(N_CHIPS, rows // rb), in_specs=[blk, blk], out_specs=[blk, blk],
                          out_shape=[_sds(own.shape, F32), _sds(own.shape, BF16)],
                          compiler_params=_params(("parallel", "parallel")))(own, got)


def _chip_sum(name, own, got):
    rows, cols = own.shape
    rb = min(rows, 256)
    blk = pl.BlockSpec((rb, cols), lambda i: (i, 0))
    blk3 = pl.BlockSpec((N_PEERS, rb, cols), lambda i: (0, i, 0))

    def body(a_ref, b_ref, o_ref):
        o_ref[...] = ((a_ref[...] + b_ref[0].astype(F32)) + b_ref[1].astype(F32)) + b_ref[2].astype(F32)

    return pl.pallas_call(body, name=name, grid=(rows // rb,), in_specs=[blk, blk3], out_specs=blk,
                          out_shape=_sds(own.shape, F32), compiler_params=_params(("parallel",)))(own, got)


SMALL_RB = 280


def _small_pair_sum(own, got):
    blk = pl.BlockSpec((SMALL_RB, 128), lambda i: (i, 0))

    def body(a_ref, b_ref, o_ref):
        o_ref[...] = a_ref[...] + b_ref[...]

    return pl.pallas_call(body, name="small_pair_sum", grid=(own.shape[0] // SMALL_RB,), in_specs=[blk, blk],
                          out_specs=blk, out_shape=_sds(own.shape, F32),
                          compiler_params=_params(("parallel",)))(own, got)


def _small_chip_sum(own, got):
    blk = pl.BlockSpec((SMALL_RB, 128), lambda i: (i, 0))
    blk3 = pl.BlockSpec((N_PEERS, SMALL_RB, 128), lambda i: (0, i, 0))

    def body(a_ref, b_ref, o_ref):
        o_ref[...] = (a_ref[...] + b_ref[1]) + (b_ref[0] + b_ref[2])

    return pl.pallas_call(body, name="small_chip_sum", grid=(own.shape[0] // SMALL_RB,), in_specs=[blk, blk3],
                          out_specs=blk, out_shape=_sds(own.shape, F32),
                          compiler_params=_params(("parallel",)))(own, got)


def _adamw_math(w, g, m, v):
    m = ADAM_B1 * m + (1.0 - ADAM_B1) * g
    v = ADAM_B2 * v + (1.0 - ADAM_B2) * (g * g)
    m_hat = m / (1.0 - ADAM_B1 ** ADAM_STEP)
    v_hat = v / (1.0 - ADAM_B2 ** ADAM_STEP)
    delta = -ADAM_LR * (m_hat / (jnp.sqrt(v_hat) + ADAM_EPS) + ADAM_WD * w)
    return delta, m, v


def _adamw(name, w, g, m, v, rb=None):
    rows, cols = w.shape
    rb = rows if rb is None else rb
    blk = pl.BlockSpec((rb, cols), lambda i: (i, 0))

    def body(w_ref, g_ref, m_ref, v_ref, d_ref, nm_ref, nv_ref):
        d, nm, nv = _adamw_math(w_ref[...], g_ref[...], m_ref[...], v_ref[...])
        d_ref[...] = d
        nm_ref[...] = nm
        nv_ref[...] = nv

    return pl.pallas_call(body, name=name, grid=(rows // rb,), in_specs=[blk] * 4, out_specs=[blk] * 3,
                          out_shape=[_sds(w.shape, F32)] * 3, compiler_params=_params(("parallel",)))(w, g, m, v)


BIG = ("w_in", "w_att_o", "w_rec_o", "w_out", "w_ff1", "w_ff2")
COL_SHARDED = ("w_in", "w_att_o", "w_ff1")
SHARDED_VECS = ("conv_w", "b_rg_a", "b_rg_i", "lru_lambda")
SMALL = ("ln1_g", "b_in", "rpb", "conv_w", "conv_b", "w_rg_a", "b_rg_a", "w_rg_i", "b_rg_i", "lru_lambda",
         "ln2_g", "lnf_g")
SMALL_ROWS = 2240
ORDER = ("ln1_g", "w_in", "b_in", "rpb", "w_att_o", "conv_w", "conv_b", "w_rg_a", "b_rg_a", "w_rg_i", "b_rg_i",
         "lru_lambda", "w_rec_o", "w_out", "ln2_g", "w_ff1", "w_ff2", "lnf_g")


def _pack_small(grads):
    parts, sizes = [], {}
    for n in SMALL:
        flat = grads[n].reshape(-1)
        pad = (-flat.shape[0]) % 128
        sizes[n] = (flat.shape[0], flat.shape[0] + pad)
        parts.append(jnp.pad(flat, (0, pad)))
    total = sum(s[1] for s in sizes.values())
    parts.append(jnp.zeros((SMALL_ROWS * 128 - total,), F32))
    return jnp.concatenate(parts).reshape(SMALL_ROWS, 128), sizes


def _unpack_small(buf, sizes, shapes):
    flat = buf.reshape(-1)
    out, pos = {}, 0
    for n in SMALL:
        size, padded = sizes[n]
        out[n] = flat[pos:pos + size].reshape(shapes[n])
        pos += padded
    return out


def kernel(x, ln1_g, w_in, b_in, rpb, w_att_o, conv_w, conv_b, w_rg_a, b_rg_a, w_rg_i, b_rg_i, lru_lambda, w_rec_o, w_out, ln2_g, w_ff1, w_ff2, lnf_g, loss_target, m_ln1_g, m_w_in, m_b_in, m_rpb, m_w_att_o, m_conv_w, m_conv_b, m_w_rg_a, m_b_rg_a, m_w_rg_i, m_b_rg_i, m_lru_lambda, m_w_rec_o, m_w_out, m_ln2_g, m_w_ff1, m_w_ff2, m_lnf_g, v_ln1_g, v_w_in, v_b_in, v_rpb, v_w_att_o, v_conv_w, v_conv_b, v_w_rg_a, v_b_rg_a, v_w_rg_i, v_b_rg_i, v_lru_lambda, v_w_rec_o, v_w_out, v_ln2_g, v_w_ff1, v_w_ff2, v_lnf_g):
    w = dict(ln1_g=ln1_g, w_in=w_in, b_in=b_in, rpb=rpb, w_att_o=w_att_o, conv_w=conv_w, conv_b=conv_b,
             w_rg_a=w_rg_a, b_rg_a=b_rg_a, w_rg_i=w_rg_i, b_rg_i=b_rg_i, lru_lambda=lru_lambda, w_rec_o=w_rec_o,
             w_out=w_out, ln2_g=ln2_g, w_ff1=w_ff1, w_ff2=w_ff2, lnf_g=lnf_g)
    m = dict(ln1_g=m_ln1_g, w_in=m_w_in, b_in=m_b_in, rpb=m_rpb, w_att_o=m_w_att_o, conv_w=m_conv_w,
             conv_b=m_conv_b, w_rg_a=m_w_rg_a, b_rg_a=m_b_rg_a, w_rg_i=m_w_rg_i, b_rg_i=m_b_rg_i,
             lru_lambda=m_lru_lambda, w_rec_o=m_w_rec_o, w_out=m_w_out, ln2_g=m_ln2_g, w_ff1=m_w_ff1,
             w_ff2=m_w_ff2, lnf_g=m_lnf_g)
    v = dict(ln1_g=v_ln1_g, w_in=v_w_in, b_in=v_b_in, rpb=v_rpb, w_att_o=v_w_att_o, conv_w=v_conv_w,
             conv_b=v_conv_b, w_rg_a=v_w_rg_a, b_rg_a=v_b_rg_a, w_rg_i=v_w_rg_i, b_rg_i=v_b_rg_i,
             lru_lambda=v_lru_lambda, w_rec_o=v_w_rec_o, w_out=v_w_out, ln2_g=v_ln2_g, w_ff1=v_w_ff1,
             w_ff2=v_w_ff2, lnf_g=v_lnf_g)
    chip = 2 * lax.axis_index("x") + lax.axis_index("y")

    shards = [_cast_bf16("cast_" + n, w[n][0]) for n in BIG]
    vec_rows = [w[n][0] for n in SHARDED_VECS]
    vec_shard = jnp.concatenate(vec_rows + [jnp.zeros((16 - 10, D // N_CHIPS), F32)], axis=0)
    gathered = _all_gather(shards + [vec_shard])
    full = dict(zip(BIG, gathered[:len(BIG)]))
    vecs = gathered[len(BIG)].transpose(1, 0, 2).reshape(16, D)
    p = dict(
        w_in=full["w_in"], w_att_o=full["w_att_o"], w_ff1=full["w_ff1"],
        w_rec_o=full["w_rec_o"].reshape(D, D), w_out=full["w_out"].reshape(D, D), w_ff2=full["w_ff2"].reshape(D_FF, D),
        ln1_g=ln1_g, b_in=b_in, rpb=rpb[0], conv_w=vecs[0:4], conv_b=conv_b, w_rg_a=w_rg_a[0], b_rg_a=vecs[4:6],
        w_rg_i=w_rg_i[0], b_rg_i=vecs[6:8], lru_lambda=vecs[8:10], ln2_g=ln2_g, lnf_g=lnf_g.reshape(1, D))

    loss, grad_x, g = _local_step(x, loss_target, p)
    loss = lax.psum(loss[0, 0], ("x", "y", "c"))

    big = [g["w_in"], g["w_att_o"], g["w_rec_o"].reshape(N_CHIPS, D // N_CHIPS, D),
           g["w_out"].reshape(N_CHIPS, D // N_CHIPS, D), g["w_ff1"], g["w_ff2"].reshape(N_CHIPS, D_FF // N_CHIPS, D)]
    small, sizes = _pack_small(g)
    own1, got1, small_got1 = _pair_exchange(big, small)
    sums = [_pair_sum("pair_sum_" + n, a, b) for n, a, b in zip(BIG, own1, got1)]
    small_sum = _small_pair_sum(small, small_got1)
    own2, got2, small_own2, small_got2 = _chip_exchange([s[0] for s in sums], [s[1] for s in sums], small_sum)
    halves = [_chip_sum("chip_sum_" + n, a, b) for n, a, b in zip(BIG, own2, got2)]
    halves.append(_small_chip_sum(small_own2, small_got2))
    swapped = _half_swap(halves)
    shapes = {n: g[n].shape for n in SMALL}
    gsmall = _unpack_small(swapped[len(BIG)], sizes, shapes)

    grad = {}
    for n, red in zip(BIG, swapped[:len(BIG)]):
        if n in COL_SHARDED:
            grad[n] = red
        else:
            grad[n] = red
    for n in SMALL:
        gn = gsmall[n]
        if n in SHARDED_VECS:
            gn = lax.dynamic_slice_in_dim(gn, chip * (D // N_CHIPS), D // N_CHIPS, axis=1)
        grad[n] = gn

    out_grad, out_delta, out_m, out_v = {}, {}, {}, {}
    for n in ORDER:
        shape = w[n].shape
        if n in BIG:
            two_d = grad[n].shape
            rb = 256
        else:
            two_d = (int(np.prod(shape[:-1])), shape[-1])
            rb = None
        gn = grad[n].reshape(two_d)
        d, nm, nv = _adamw("adamw_" + n, w[n].reshape(two_d), gn, m[n].reshape(two_d), v[n].reshape(two_d), rb)
        out_grad[n], out_delta[n], out_m[n], out_v[n] = (gn.reshape(shape), d.reshape(shape), nm.reshape(shape),
                                                         nv.reshape(shape))
    return (loss, grad_x, *[out_grad[n] for n in ORDER], *[out_delta[n] for n in ORDER],
            *[out_m[n] for n in ORDER], *[out_v[n] for n in ORDER])
```

```python
import functools

import numpy as np
import jax
import jax.numpy as jnp
from jax import lax
from jax.experimental import pallas as pl
from jax.experimental.pallas import tpu as pltpu

F32 = jnp.float32
BF16 = jnp.bfloat16

T = 2048
D = 1024
D_ATT = 512
D_IN = 5632
D_FF = 4096
N_HEADS = 8
HEAD_DIM = 64
GRID_W = 64
N_ROWS = T // GRID_W
WIN_H = 8
WIN_W = 16
KEYS = WIN_H * GRID_W
N_CHIPS = 4
EPS = 1e-6
LRU_C = 8.0
SCALE = HEAD_DIM ** -0.5
REC_CB = 256
REC_CHUNK = 256
PAD = 8

ADAM_LR = 0.001
ADAM_B1 = 0.9
ADAM_B2 = 0.999
ADAM_EPS = 1e-08
ADAM_WD = 0.01
ADAM_STEP = 10

VMEM_LIMIT = 56 * 1024 * 1024

NN = (((1,), (0,)), ((), ()))
NT = (((1,), (1,)), ((), ()))
TN = (((0,), (0,)), ((), ()))
MESH = pl.DeviceIdType.MESH


def _params(sem=None):
    return pltpu.CompilerParams(dimension_semantics=sem, vmem_limit_bytes=VMEM_LIMIT)


def _dot(a, b, dims):
    return lax.dot_general(a, b, dims, preferred_element_type=F32)


def _sigmoid(x):
    return 1.0 / (1.0 + jnp.exp(-x))


def _matmul(name, a, b, *, dims, grid, a_spec, b_spec, out_shapes, out_specs, acc_shape,
            extras=(), extra_specs=(), epilogue=None, colsum_spec=None, colsum_shape=None):
    nk = grid[2]
    n_extra = len(extras)
    n_out = len(out_shapes)
    with_colsum = colsum_spec is not None

    def body(a_ref, b_ref, *rest):
        ex = rest[:n_extra]
        outs = rest[n_extra:n_extra + n_out]
        pos = n_extra + n_out
        cs_out = rest[pos] if with_colsum else None
        pos += 1 if with_colsum else 0
        acc = rest[pos]
        cs_acc = rest[pos + 1] if with_colsum else None
        k = pl.program_id(2)

        @pl.when(k == 0)
        def _():
            acc[...] = jnp.zeros_like(acc)
            if with_colsum:
                cs_acc[...] = jnp.zeros_like(cs_acc)

        bv = b_ref[...]
        acc[...] += _dot(a_ref[...].astype(BF16), bv.astype(BF16), dims)
        if with_colsum:
            cs_acc[...] += jnp.sum(bv.astype(F32), axis=0, keepdims=True)

        @pl.when(k == nk - 1)
        def _():
            r = acc[...]
            if epilogue is None:
                outs[0][...] = r.astype(outs[0].dtype)
            else:
                epilogue(r, ex, outs)
            if with_colsum:
                cs_out[...] = cs_acc[...]

    shapes = list(out_shapes)
    specs = list(out_specs)
    scratch = [pltpu.VMEM(acc_shape, F32)]
    if with_colsum:
        shapes.append(colsum_shape)
        specs.append(colsum_spec)
        scratch.append(pltpu.VMEM((1, acc_shape[1]), F32))
    res = pl.pallas_call(
        body, name=name, grid=grid,
        in_specs=[a_spec, b_spec, *extra_specs],
        out_specs=specs, out_shape=shapes, scratch_shapes=scratch,
        compiler_params=_params(("parallel", "parallel", "arbitrary")),
    )(a, b, *extras)
    return res


def _sds(shape, dtype):
    return jax.ShapeDtypeStruct(shape, dtype)


TM = 512
NI = T // TM


def _mm_nn_cols(name, a, wg, out_dtype, *, bias=None, extras=(), extra_specs=(), epilogue=None,
                out_shapes=None, out_specs=None):
    k_dim, n4 = wg.shape[1], wg.shape[2]
    ex, exs = list(extras), list(extra_specs)
    if bias is not None:
        ex = [bias] + ex
        exs = [pl.BlockSpec((1, n4), lambda j, i, k: (0, j))] + exs
        user_ep = epilogue

        def epilogue(r, e, outs):
            r = r + e[0][...]
            if user_ep is None:
                outs[0][...] = r.astype(outs[0].dtype)
            else:
                user_ep(r, e[1:], outs)
    if out_shapes is None:
        out_shapes = [_sds((T, N_CHIPS * n4), out_dtype)]
        out_specs = [pl.BlockSpec((TM, n4), lambda j, i, k: (i, j))]
    return _matmul(
        name, a, wg, dims=NN, grid=(N_CHIPS, NI, 1),
        a_spec=pl.BlockSpec((TM, k_dim), lambda j, i, k: (i, 0)),
        b_spec=pl.BlockSpec((None, k_dim, n4), lambda j, i, k: (j, 0, 0)),
        out_shapes=out_shapes, out_specs=out_specs, acc_shape=(TM, n4),
        extras=ex, extra_specs=exs, epilogue=epilogue)


def _mm_nn_rows(name, a, w, out_dtype, *, tk, extras=(), extra_specs=(), epilogue=None):
    k_dim, n = w.shape
    return _matmul(
        name, a, w, dims=NN, grid=(NI, 1, k_dim // tk),
        a_spec=pl.BlockSpec((TM, tk), lambda i, j, k: (i, k)),
        b_spec=pl.BlockSpec((tk, n), lambda i, j, k: (k, 0)),
        out_shapes=[_sds((T, n), out_dtype)],
        out_specs=[pl.BlockSpec((TM, n), lambda i, j, k: (i, 0))], acc_shape=(TM, n),
        extras=extras, extra_specs=extra_specs, epilogue=epilogue)


def _mm_nt_cols(name, a, wg, out_dtype):
    k_dim, n4 = wg.shape[1], wg.shape[2]
    return _matmul(
        name, a, wg, dims=NT, grid=(NI, 1, N_CHIPS),
        a_spec=pl.BlockSpec((TM, n4), lambda i, j, k: (i, k)),
        b_spec=pl.BlockSpec((None, k_dim, n4), lambda i, j, k: (k, 0, 0)),
        out_shapes=[_sds((T, k_dim), out_dtype)],
        out_specs=[pl.BlockSpec((TM, k_dim), lambda i, j, k: (i, 0))], acc_shape=(TM, k_dim))


def _mm_nt_rows(name, a, w, out_dtype, *, tn, extras=(), extra_specs=(), epilogue=None):
    k_dim, n = w.shape
    return _matmul(
        name, a, w, dims=NT, grid=(k_dim // tn, NI, 1),
        a_spec=pl.BlockSpec((TM, n), lambda j, i, k: (i, 0)),
        b_spec=pl.BlockSpec((tn, n), lambda j, i, k: (j, 0)),
        out_shapes=[_sds((T, k_dim), out_dtype)],
        out_specs=[pl.BlockSpec((TM, tn), lambda j, i, k: (i, j))], acc_shape=(TM, tn),
        extras=extras, extra_specs=extra_specs, epilogue=epilogue)


def _mm_tn_cols(name, a, g, n4, *, colsum=False):
    k_dim = a.shape[1]
    kw = {}
    if colsum:
        kw = dict(colsum_spec=pl.BlockSpec((1, n4), lambda j, i, k: (0, j)),
                  colsum_shape=_sds((1, N_CHIPS * n4), F32))
    return _matmul(
        name, a, g, dims=TN, grid=(N_CHIPS, 1, NI),
        a_spec=pl.BlockSpec((TM, k_dim), lambda j, i, k: (k, 0)),
        b_spec=pl.BlockSpec((TM, n4), lambda j, i, k: (k, j)),
        out_shapes=[_sds((N_CHIPS, k_dim, n4), F32)],
        out_specs=[pl.BlockSpec((None, k_dim, n4), lambda j, i, k: (j, 0, 0))],
        acc_shape=(k_dim, n4), **kw)


def _mm_tn_rows(name, a, g, *, tm):
    k_dim, n = a.shape[1], g.shape[1]
    return _matmul(
        name, a, g, dims=TN, grid=(k_dim // tm, 1, NI),
        a_spec=pl.BlockSpec((TM, tm), lambda j, i, k: (k, j)),
        b_spec=pl.BlockSpec((TM, n), lambda j, i, k: (k, 0)),
        out_shapes=[_sds((k_dim, n), F32)],
        out_specs=[pl.BlockSpec((tm, n), lambda j, i, k: (j, 0))], acc_shape=(tm, n))


TE = 256
NE = T // TE
_ROW = pl.BlockSpec((TE, D), lambda i: (i, 0))
_VEC = pl.BlockSpec((1, D), lambda i: (0, 0))


def _rms_fwd(name, x, g):
    def body(x_ref, g_ref, h_ref):
        xv = x_ref[...]
        rstd = lax.rsqrt(jnp.mean(xv * xv, axis=-1, keepdims=True) + EPS)
        h_ref[...] = (xv * rstd * g_ref[...]).astype(BF16)

    return pl.pallas_call(body, name=name, grid=(NE,), in_specs=[_ROW, _VEC], out_specs=_ROW,
                          out_shape=_sds((T, D), BF16), compiler_params=_params(("parallel",)))(x, g)


def _rms_bwd(name, dh, x, g, dres):
    def body(dh_ref, x_ref, g_ref, dres_ref, dx_ref, dg_ref):
        xv = x_ref[...]
        rstd = lax.rsqrt(jnp.mean(xv * xv, axis=-1, keepdims=True) + EPS)
        xhat = xv * rstd
        dhv = dh_ref[...]
        dy = dhv * g_ref[...]
        dx_ref[...] = dres_ref[...] + rstd * (dy - xhat * jnp.mean(dy * xhat, axis=-1, keepdims=True))

        @pl.when(pl.program_id(0) == 0)
        def _():
            dg_ref[...] = jnp.zeros_like(dg_ref)

        dg_ref[...] += jnp.sum(dhv * xhat, axis=0, keepdims=True)

    return pl.pallas_call(body, name=name, grid=(NE,), in_specs=[_ROW, _ROW, _VEC, _ROW],
                          out_specs=[_ROW, _VEC], out_shape=[_sds((T, D), F32), _sds((1, D), F32)],
                          compiler_params=_params(("arbitrary",)))(dh, x, g, dres)


def _loss_head(x2, target, g):
    def body(x_ref, t_ref, g_ref, loss_ref, dx_ref, dg_ref):
        xv = x_ref[...]
        rstd = lax.rsqrt(jnp.mean(xv * xv, axis=-1, keepdims=True) + EPS)
        xhat = xv * rstd
        gv = g_ref[...]
        err = xhat * gv - t_ref[...]
        dy = err * (1.0 / D)
        dxh = dy * gv
        dx_ref[...] = rstd * (dxh - xhat * jnp.mean(dxh * xhat, axis=-1, keepdims=True))

        @pl.when(pl.program_id(0) == 0)
        def _():
            dg_ref[...] = jnp.zeros_like(dg_ref)
            loss_ref[...] = jnp.zeros_like(loss_ref)

        dg_ref[...] += jnp.sum(dy * xhat, axis=0, keepdims=True)
        loss_ref[...] += (0.5 / D) * jnp.sum(jnp.sum(err * err, axis=1, keepdims=True), axis=0, keepdims=True)

    return pl.pallas_call(
        body, name="loss_head", grid=(NE,), in_specs=[_ROW, _ROW, _VEC],
        out_specs=[pl.BlockSpec((1, 1), lambda i: (0, 0)), _ROW, _VEC],
        out_shape=[_sds((1, 1), F32), _sds((T, D), F32), _sds((1, D), F32)],
        compiler_params=_params(("arbitrary",)))(x2, target, g)


MW = 512
_G_ATT_BLK = 3584 // MW
_G_REC_BLK = 4608 // MW


def _merge_specs():
    y = pl.BlockSpec((TM, MW), lambda i, j: (i, j))
    ga = pl.BlockSpec((TM, MW), lambda i, j: (i, _G_ATT_BLK + j))
    gr = pl.BlockSpec((TM, MW), lambda i, j: (i, _G_REC_BLK + j))
    return y, ga, gr


def _merge_fwd(y_att, y_rec, z):
    y, ga, gr = _merge_specs()

    def body(ya_ref, yr_ref, ga_ref, gr_ref, m_ref):
        m = _sigmoid(ga_ref[...]) * ya_ref[...] + _sigmoid(gr_ref[...]) * yr_ref[...]
        m_ref[...] = m.astype(BF16)

    return pl.pallas_call(body, name="merge_fwd", grid=(NI, D // MW), in_specs=[y, y, ga, gr], out_specs=y,
                          out_shape=_sds((T, D), BF16),
                          compiler_params=_params(("parallel", "parallel")))(y_att, y_rec, z, z)


def _merge_bwd(dm, y_att, y_rec, z):
    y, ga, gr = _merge_specs()

    def body(dm_ref, ya_ref, yr_ref, ga_ref, gr_ref, dya_ref, dyr_ref, dga_ref, dgr_ref):
        dmv = dm_ref[...]
        sa = _sigmoid(ga_ref[...])
        sr = _sigmoid(gr_ref[...])
        dya_ref[...] = (dmv * sa).astype(BF16)
        dyr_ref[...] = (dmv * sr).astype(BF16)
        dga_ref[...] = (dmv * ya_ref[...] * sa * (1.0 - sa)).astype(BF16)
        dgr_ref[...] = (dmv * yr_ref[...] * sr * (1.0 - sr)).astype(BF16)

    return pl.pallas_call(body, name="merge_bwd", grid=(NI, D // MW), in_specs=[y, y, y, ga, gr],
                          out_specs=[y, y, y, y], out_shape=[_sds((T, D), BF16)] * 4,
                          compiler_params=_params(("parallel", "parallel")))(dm, y_att, y_rec, z, z)


HP = 2 * HEAD_DIM
N_HP = N_HEADS // 2


def _window_maps():
    diag = np.zeros((GRID_W * GRID_W, 128), np.float32)
    for qc in range(GRID_W):
        w0 = min(max(qc - WIN_W // 2, 0), GRID_W - WIN_W)
        for kc in range(w0, w0 + WIN_W):
            diag[qc * GRID_W + kc, kc - qc + WIN_W - 1] = 1.0
    return diag, diag.sum(axis=1)[None, :]


def _split3(x):
    a = x.astype(BF16)
    r = x - a.astype(F32)
    b = r.astype(BF16)
    c = (r - b.astype(F32)).astype(BF16)
    return a, b, c


N_DROW = 2 * WIN_H - 1
N_DPAIR = N_DROW - 1


def _bias_pairs(rpb):
    diag, valid = _window_maps()
    r2 = jnp.pad(rpb.reshape(N_HEADS * N_DROW, 2 * WIN_W - 1),
                 ((0, 128 - N_HEADS * N_DROW), (0, 128 - (2 * WIN_W - 1))))

    def body(r_ref, d_ref, v_ref, o_ref):
        dv = d_ref[...]
        t = sum(_dot(part, dv, NN) for part in _split3(r_ref[...]))
        o_ref[...] = jnp.where(v_ref[...] > 0.0, t, -1e30)

    t = pl.pallas_call(body, name="rpb_expand", out_shape=_sds((128, GRID_W * GRID_W), F32),
                       compiler_params=_params())(r2, jnp.asarray(diag.T, BF16), jnp.asarray(valid, F32))
    t = t[:N_HEADS * N_DROW].reshape(N_HEADS, N_DROW, GRID_W, GRID_W)
    return jnp.concatenate([t[:, :N_DPAIR], t[:, 1:]], axis=-1)


def _row_bias(tb_ref, hh, d0):
    return jnp.concatenate([tb_ref[hh, d0 + 2 * ii] for ii in range(WIN_H // 2)], axis=1)


def _row_window(r):
    rs = jnp.clip(r - WIN_H // 2, 0, N_ROWS - WIN_H)
    return pl.multiple_of(r * GRID_W, GRID_W), pl.multiple_of(rs * GRID_W, GRID_W), rs - r + (WIN_H - 1)


def _attn_probs(qh, kh, bias):
    s = _dot(qh, kh, NT) + bias
    e = jnp.exp(s - jnp.max(s, axis=-1, keepdims=True))
    return e / jnp.sum(e, axis=-1, keepdims=True)


def _attn_in_specs():
    q = pl.BlockSpec((T, HP), lambda p: (0, p))
    k = pl.BlockSpec((T, HP), lambda p: (0, N_HP + p))
    v = pl.BlockSpec((T, HP), lambda p: (0, 2 * N_HP + p))
    tb = pl.BlockSpec((2, N_DPAIR, GRID_W, HP), lambda p: (p, 0, 0, 0))
    return q, k, v, tb


def _attn_fwd(z, tb):
    def body(q_ref, k_ref, v_ref, tb_ref, o_ref, kb_ref, vb_ref):
        kb_ref[...] = k_ref[...].astype(BF16)
        vb_ref[...] = v_ref[...].astype(BF16)

        def row(r, carry):
            q0, k0, d0 = _row_window(r)
            q = (q_ref[pl.ds(q0, GRID_W), :] * SCALE).astype(BF16)
            kw = kb_ref[pl.ds(k0, KEYS), :]
            vw = vb_ref[pl.ds(k0, KEYS), :]
            outs = []
            for hh in range(2):
                ln = slice(hh * HEAD_DIM, (hh + 1) * HEAD_DIM)
                p = _attn_probs(q[:, ln], kw[:, ln], _row_bias(tb_ref, hh, d0))
                outs.append(_dot(p.astype(BF16), vw[:, ln], NN))
            o_ref[pl.ds(q0, GRID_W), :] = jnp.concatenate(outs, axis=1).astype(BF16)
            return carry

        lax.fori_loop(0, N_ROWS, row, 0)

    blk = pl.BlockSpec((T, HP), lambda p: (0, p))
    return pl.pallas_call(
        body, name="attn_fwd", grid=(N_HP,), in_specs=list(_attn_in_specs()), out_specs=blk,
        out_shape=_sds((T, D_ATT), BF16),
        scratch_shapes=[pltpu.VMEM((T, HP), BF16), pltpu.VMEM((T, HP), BF16)],
        compiler_params=_params(("parallel",)))(z, z, z, tb)


def _attn_bwd(z, tb, d_att):
    def body(q_ref, k_ref, v_ref, tb_ref, do_ref, dq_ref, dk_ref, dv_ref, ds_ref, kb_ref, vb_ref, dka_ref, dva_ref):
        kb_ref[...] = k_ref[...].astype(BF16)
        vb_ref[...] = v_ref[...].astype(BF16)
        dka_ref[...] = jnp.zeros_like(dka_ref)
        dva_ref[...] = jnp.zeros_like(dva_ref)
        ds_ref[...] = jnp.zeros_like(ds_ref)

        def row(r, carry):
            q0, k0, d0 = _row_window(r)
            q = (q_ref[pl.ds(q0, GRID_W), :] * SCALE).astype(BF16)
            kw = kb_ref[pl.ds(k0, KEYS), :]
            vw = vb_ref[pl.ds(k0, KEYS), :]
            do = do_ref[pl.ds(q0, GRID_W), :]
            dqs, dks, dvs = [], [], []
            for hh in range(2):
                ln = slice(hh * HEAD_DIM, (hh + 1) * HEAD_DIM)
                qh, kh, vh, doh = q[:, ln], kw[:, ln], vw[:, ln], do[:, ln]
                p = _attn_probs(qh, kh, _row_bias(tb_ref, hh, d0))
                dvs.append(_dot(p.astype(BF16), doh, TN))
                dp = _dot(doh, vh, NT)
                ds = p * (dp - jnp.sum(dp * p, axis=-1, keepdims=True))
                for ii in range(WIN_H // 2):
                    ds_ref[hh, d0 + 2 * ii] += ds[:, ii * HP:(ii + 1) * HP]
                dsb = ds.astype(BF16)
                dqs.append(_dot(dsb, kh, NN) * SCALE)
                dks.append(_dot(dsb, qh, TN))
            dq_ref[pl.ds(q0, GRID_W), :] = jnp.concatenate(dqs, axis=1).astype(BF16)
            dka_ref[pl.ds(k0, KEYS), :] += jnp.concatenate(dks, axis=1)
            dva_ref[pl.ds(k0, KEYS), :] += jnp.concatenate(dvs, axis=1)
            return carry

        lax.fori_loop(0, N_ROWS, row, 0)
        dk_ref[...] = dka_ref[...].astype(BF16)
        dv_ref[...] = dva_ref[...].astype(BF16)

    blk = pl.BlockSpec((T, HP), lambda p: (0, p))
    q, k, v, tbs = _attn_in_specs()
    return pl.pallas_call(
        body, name="attn_bwd", grid=(N_HP,), in_specs=[q, k, v, tbs, blk],
        out_specs=[blk, blk, blk, tbs],
        out_shape=[_sds((T, D_ATT), BF16)] * 3 + [_sds((N_HEADS, N_DPAIR, GRID_W, HP), F32)],
        scratch_shapes=[pltpu.VMEM((T, HP), BF16), pltpu.VMEM((T, HP), BF16),
                        pltpu.VMEM((T, HP), F32), pltpu.VMEM((T, HP), F32)],
        compiler_params=_params(("parallel",)))(z, z, z, tb, d_att)


def _rpb_grad(ds_acc):
    a = ds_acc.reshape(N_HEADS, N_DPAIR, GRID_W, 2, GRID_W).transpose(0, 1, 3, 2, 4)
    a = jnp.pad(a.reshape(N_HEADS * N_DPAIR * 2, GRID_W * GRID_W), ((0, 256 - N_HEADS * N_DPAIR * 2), (0, 0)))
    sel = np.zeros((128, 256), np.float32)
    for h in range(N_HEADS):
        for pair in range(N_DPAIR):
            for half in range(2):
                sel[h * N_DROW + pair + half, (h * N_DPAIR + pair) * 2 + half] = 1.0
    diag, _ = _window_maps()

    def body(a_ref, sel_ref, diag_ref, o_ref):
        selv = sel_ref[...]
        g = sum(_dot(selv, part, NN) for part in _split3(a_ref[...]))
        diagv = diag_ref[...]
        o_ref[...] = sum(_dot(part, diagv, NN) for part in _split3(g))

    out = pl.pallas_call(body, name="rpb_grad", out_shape=_sds((128, 128), F32),
                         compiler_params=_params())(a, jnp.asarray(sel, BF16), jnp.asarray(diag, BF16))
    return out[:N_HEADS * N_DROW, :2 * WIN_W - 1].reshape(N_HEADS, N_DROW, 2 * WIN_W - 1)


N_CB = D // REC_CB
N_CHUNK = T // REC_CHUNK
N_TILE = T // 8
_U_BLK = 1536 // REC_CB
_Y_BLK = 2560 // REC_CB


def _block_diag(w):
    per = REC_CB // 64
    wt = w.reshape(2, N_CB, per, 64, 64)
    eye = jnp.eye(per, dtype=w.dtype)
    full = wt[:, :, :, :, None, :] * eye[None, None, :, None, :, None]
    return full.reshape(2, N_CB, REC_CB, REC_CB).astype(BF16)


def _block_diag_grad(g):
    per = REC_CB // 64
    g6 = g.reshape(2, N_CB, per, 64, per, 64)
    return jnp.stack([g6[:, :, p, :, p, :] for p in range(per)], axis=2).reshape(2, 16, 64, 64)


def _gelu(x):
    c = 0.7978845608028654
    return 0.5 * x * (1.0 + jnp.tanh(c * (x + 0.044715 * x * x * x)))


def _gelu_grad(x):
    c = 0.7978845608028654
    th = jnp.tanh(c * (x + 0.044715 * x * x * x))
    return 0.5 * (1.0 + th) + 0.5 * x * (1.0 - th * th) * c * (1.0 + 3.0 * 0.044715 * x * x)


def _softplus_neg(lam):
    x = -lam
    e = jnp.exp(-jnp.abs(x))
    w = 1.0 + e
    l1p = jnp.where(w == 1.0, e, jnp.log(w) * e / (w - 1.0))
    return jnp.maximum(x, 0.0) + l1p


def _one_minus_exp(x):
    poly = x * (1.0 + x * (1 / 2 + x * (1 / 6 + x * (1 / 24 + x * (1 / 120 + x * (1 / 720 + x * (1 / 5040)))))))
    return jnp.where(x > -0.25, -poly, 1.0 - jnp.exp(x))


def _conv_taps(pad_ref, t0, w, sign):
    out = None
    for j in range(4):
        term = w[j:j + 1, :] * pad_ref[pl.ds(PAD + t0 + sign * (j - 2), REC_CHUNK), :]
        out = term if out is None else out + term
    return out


def _gates(u, wa, wi, ba, bi, sp):
    ub = u.astype(BF16)
    r = _sigmoid(_dot(ub, wa, NN) + ba)
    i = _sigmoid(_dot(ub, wi, NN) + bi)
    log_a = -LRU_C * r * sp
    a = jnp.exp(log_a)
    mult = jnp.sqrt(jnp.maximum(_one_minus_exp(2.0 * log_a), 0.0))
    return r, i, a, mult


def _tile_scan(a, b, sub, reverse):
    for s in (1, 2, 4):
        if reverse:
            a_s, b_s, m = pltpu.roll(a, 8 - s, 0), pltpu.roll(b, 8 - s, 0), sub < 8 - s
        else:
            a_s, b_s, m = pltpu.roll(a, s, 0), pltpu.roll(b, s, 0), sub >= s
        b = jnp.where(m, a * b_s + b, b)
        a = jnp.where(m, a * a_s, a)
    return a, b


def _last_row(x, sub, row):
    return jnp.broadcast_to(jnp.sum(jnp.where(sub == row, x, 0.0), axis=0, keepdims=True), x.shape)


def _rec_prologue(up_ref, cw_ref, cb_ref, wa_ref, wi_ref, ba_ref, bi_ref, lam_ref,
                  upad_ref, u_ref, a_refs, h_refs):
    cb = up_ref.shape[1]
    zeros = jnp.zeros((PAD, cb), F32)
    upad_ref[pl.ds(0, PAD), :] = zeros
    upad_ref[pl.ds(PAD + T, PAD), :] = zeros
    upad_ref[pl.ds(PAD, T), :] = up_ref[...]
    cw = cw_ref[...]
    sp = _softplus_neg(lam_ref[...])
    for c in range(N_CHUNK):
        t0 = c * REC_CHUNK
        u = cb_ref[...] + _conv_taps(upad_ref, t0, cw, 1)
        u_ref[pl.ds(t0, REC_CHUNK), :] = u
        for d in range(2):
            _, i, a, mult = _gates(u, wa_ref[d], wi_ref[d], ba_ref[d:d + 1, :], bi_ref[d:d + 1, :], sp[d:d + 1, :])
            a_refs[d][pl.ds(t0, REC_CHUNK), :] = a
            h_refs[d][pl.ds(t0, REC_CHUNK), :] = mult * (i * u)

    sub = lax.broadcasted_iota(jnp.int32, (8, cb), 0)

    def tile(k, carry):
        cf, cr = carry
        tf = pl.multiple_of(k * 8, 8)
        tr = pl.multiple_of((N_TILE - 1 - k) * 8, 8)
        af, bf = _tile_scan(a_refs[0][pl.ds(tf, 8), :], h_refs[0][pl.ds(tf, 8), :], sub, False)
        hf = af * cf + bf
        h_refs[0][pl.ds(tf, 8), :] = hf
        ar, br = _tile_scan(a_refs[1][pl.ds(tr, 8), :], h_refs[1][pl.ds(tr, 8), :], sub, True)
        hr = ar * cr + br
        h_refs[1][pl.ds(tr, 8), :] = hr
        return _last_row(hf, sub, 7), _last_row(hr, sub, 0)

    z8 = jnp.zeros((8, cb), F32)
    lax.fori_loop(0, N_TILE, tile, (z8, z8))
    return sp


def _rec_specs():
    up = pl.BlockSpec((T, REC_CB), lambda c: (0, _U_BLK + c))
    yb = pl.BlockSpec((T, REC_CB), lambda c: (0, _Y_BLK + c))
    cw = pl.BlockSpec((4, REC_CB), lambda c: (0, c))
    cbias = pl.BlockSpec((1, REC_CB), lambda c: (0, c))
    wbd = pl.BlockSpec((2, None, REC_CB, REC_CB), lambda c: (0, c, 0, 0))
    vec2 = pl.BlockSpec((2, REC_CB), lambda c: (0, c))
    col = pl.BlockSpec((T, REC_CB), lambda c: (0, c))
    return up, yb, cw, cbias, wbd, vec2, col


def _rec_fwd(z, conv_w, conv_b, wa, wi, ba, bi, lam):
    up, yb, cw, cbias, wbd, vec2, col = _rec_specs()

    def body(up_ref, yb_ref, cw_ref, cb_ref, wa_ref, wi_ref, ba_ref, bi_ref, lam_ref, g_ref,
             upad_ref, u_ref, af_ref, ar_ref, hf_ref, hr_ref):
        _rec_prologue(up_ref, cw_ref, cb_ref, wa_ref, wi_ref, ba_ref, bi_ref, lam_ref,
                      upad_ref, u_ref, (af_ref, ar_ref), (hf_ref, hr_ref))

        def chunk(c, carry):
            t0 = pl.multiple_of(c * REC_CHUNK, REC_CHUNK)
            rows = pl.ds(t0, REC_CHUNK)
            g_ref[rows, :] = ((hf_ref[rows, :] + hr_ref[rows, :]) * _gelu(yb_ref[rows, :])).astype(BF16)
            return carry

        lax.fori_loop(0, N_CHUNK, chunk, 0)

    full = pltpu.VMEM((T, REC_CB), F32)
    return pl.pallas_call(
        body, name="rec_fwd", grid=(N_CB,),
        in_specs=[up, yb, cw, cbias, wbd, wbd, vec2, vec2, vec2], out_specs=col,
        out_shape=_sds((T, D), BF16),
        scratch_shapes=[pltpu.VMEM((T + 2 * PAD, REC_CB), F32), full, full, full, full, full],
        compiler_params=_params(("parallel",)))(z, z, conv_w, conv_b, wa, wi, ba, bi, lam)


def _rec_bwd(z, dg, conv_w, conv_b, wa, wi, ba, bi, lam):
    up, yb, cw, cbias, wbd, vec2, col = _rec_specs()

    def body(up_ref, yb_ref, dg_ref, cw_ref, cb_ref, wa_ref, wi_ref, ba_ref, bi_ref, lam_ref,
             dup_ref, dyb_ref, dcw_ref, dcb_ref, dwa_ref, dwi_ref, dba_ref, dbi_ref, dlam_ref,
             upad_ref, u_ref, af_ref, ar_ref, hf_ref, hr_ref, dh_ref, gf_ref, gr_ref, daf_ref, dar_ref, dupad_ref):
        a_refs, h_refs = (af_ref, ar_ref), (hf_ref, hr_ref)
        g_refs, da_refs = (gf_ref, gr_ref), (daf_ref, dar_ref)
        sp = _rec_prologue(up_ref, cw_ref, cb_ref, wa_ref, wi_ref, ba_ref, bi_ref, lam_ref,
                           upad_ref, u_ref, a_refs, h_refs)
        cb = up_ref.shape[1]

        def gate_chunk(c, carry):
            t0 = pl.multiple_of(c * REC_CHUNK, REC_CHUNK)
            rows = pl.ds(t0, REC_CHUNK)
            y = yb_ref[rows, :]
            dgv = dg_ref[rows, :].astype(F32)
            dh_ref[rows, :] = dgv * _gelu(y)
            dyb_ref[rows, :] = (dgv * (hf_ref[rows, :] + hr_ref[rows, :]) * _gelu_grad(y)).astype(BF16)
            return carry

        lax.fori_loop(0, N_CHUNK, gate_chunk, 0)

        sub = lax.broadcasted_iota(jnp.int32, (8, cb), 0)

        def tile(k, carry):
            cf, cr = carry
            kf = N_TILE - 1 - k
            tf = pl.multiple_of(kf * 8, 8)
            tnext = pl.multiple_of(jnp.minimum(kf + 1, N_TILE - 1) * 8, 8)
            tprev = pl.multiple_of(jnp.maximum(kf - 1, 0) * 8, 8)
            a_t = af_ref[pl.ds(tf, 8), :]
            a_n = jnp.where(kf < N_TILE - 1, af_ref[pl.ds(tnext, 8), :], 0.0)
            a_sh = jnp.where(sub == 7, pltpu.roll(a_n, 7, 0), pltpu.roll(a_t, 7, 0))
            ca, cbb = _tile_scan(a_sh, dh_ref[pl.ds(tf, 8), :], sub, True)
            gf = ca * cf + cbb
            h_t = hf_ref[pl.ds(tf, 8), :]
            h_p = jnp.where(kf > 0, hf_ref[pl.ds(tprev, 8), :], 0.0)
            h_sh = jnp.where(sub == 0, pltpu.roll(h_p, 1, 0), pltpu.roll(h_t, 1, 0))
            gf_ref[pl.ds(tf, 8), :] = gf
            daf_ref[pl.ds(tf, 8), :] = gf * h_sh
            tr = pl.multiple_of(k * 8, 8)
            rnext = pl.multiple_of(jnp.minimum(k + 1, N_TILE - 1) * 8, 8)
            rprev = pl.multiple_of(jnp.maximum(k - 1, 0) * 8, 8)
            b_t = ar_ref[pl.ds(tr, 8), :]
            b_p = jnp.where(k > 0, ar_ref[pl.ds(rprev, 8), :], 0.0)
            b_sh = jnp.where(sub == 0, pltpu.roll(b_p, 1, 0), pltpu.roll(b_t, 1, 0))
            ra, rb = _tile_scan(b_sh, dh_ref[pl.ds(tr, 8), :], sub, False)
            gr = ra * cr + rb
            hr_t = hr_ref[pl.ds(tr, 8), :]
            hr_n = jnp.where(k < N_TILE - 1, hr_ref[pl.ds(rnext, 8), :], 0.0)
            hr_sh = jnp.where(sub == 7, pltpu.roll(hr_n, 7, 0), pltpu.roll(hr_t, 7, 0))
            gr_ref[pl.ds(tr, 8), :] = gr
            dar_ref[pl.ds(tr, 8), :] = gr * hr_sh
            return _last_row(gf, sub, 0), _last_row(gr, sub, 7)

        z8 = jnp.zeros((8, cb), F32)
        lax.fori_loop(0, N_TILE, tile, (z8, z8))

        zeros = jnp.zeros((PAD, cb), F32)
        dupad_ref[pl.ds(0, PAD), :] = zeros
        dupad_ref[pl.ds(PAD + T, PAD), :] = zeros
        dwa_ref[...] = jnp.zeros_like(dwa_ref)
        dwi_ref[...] = jnp.zeros_like(dwi_ref)
        dba_ref[...] = jnp.zeros_like(dba_ref)
        dbi_ref[...] = jnp.zeros_like(dbi_ref)
        dlam_ref[...] = jnp.zeros_like(dlam_ref)

        def grad_chunk(c, carry):
            t0 = pl.multiple_of(c * REC_CHUNK, REC_CHUNK)
            rows = pl.ds(t0, REC_CHUNK)
            u = u_ref[rows, :]
            ub = u.astype(BF16)
            du = jnp.zeros((REC_CHUNK, cb), F32)
            for d in range(2):
                r, i, a, mult = _gates(u, wa_ref[d], wi_ref[d], ba_ref[d:d + 1, :], bi_ref[d:d + 1, :], sp[d:d + 1, :])
                dbx = g_refs[d][rows, :]
                dmult = dbx * (i * u)
                diu = dbx * mult
                a2 = a * a
                dlog = da_refs[d][rows, :] * a - dmult * jnp.where(mult > 0.0, a2 / mult, 0.0)
                dpa = (dlog * (-LRU_C) * sp[d:d + 1, :]) * r * (1.0 - r)
                dpi = (diu * u) * i * (1.0 - i)
                dpab, dpib = dpa.astype(BF16), dpi.astype(BF16)
                du = du + diu * i + _dot(dpab, wa_ref[d], NT) + _dot(dpib, wi_ref[d], NT)
                dwa_ref[d] += _dot(ub, dpab, TN)
                dwi_ref[d] += _dot(ub, dpib, TN)
                dba_ref[d:d + 1, :] += jnp.sum(dpa, axis=0, keepdims=True)
                dbi_ref[d:d + 1, :] += jnp.sum(dpi, axis=0, keepdims=True)
                dlam_ref[d:d + 1, :] += jnp.sum(dlog * r, axis=0, keepdims=True)
            dupad_ref[pl.ds(PAD + t0, REC_CHUNK), :] = du
            return carry

        lax.fori_loop(0, N_CHUNK, grad_chunk, 0)
        dlam_ref[...] = dlam_ref[...] * (LRU_C * _sigmoid(-lam_ref[...]))

        cw = cw_ref[...]
        dcb = jnp.zeros((1, cb), F32)
        dcw = [jnp.zeros((1, cb), F32) for _ in range(4)]
        for c in range(N_CHUNK):
            t0 = c * REC_CHUNK
            du = dupad_ref[pl.ds(PAD + t0, REC_CHUNK), :]
            dcb = dcb + jnp.sum(du, axis=0, keepdims=True)
            for j in range(4):
                dcw[j] = dcw[j] + jnp.sum(du * upad_ref[pl.ds(PAD + t0 + j - 2, REC_CHUNK), :], axis=0, keepdims=True)
            dup_ref[pl.ds(t0, REC_CHUNK), :] = _conv_taps(dupad_ref, t0, cw, -1).astype(BF16)
        dcb_ref[...] = dcb
        dcw_ref[...] = jnp.concatenate(dcw, axis=0)

    full = pltpu.VMEM((T, REC_CB), F32)
    padded = pltpu.VMEM((T + 2 * PAD, REC_CB), F32)
    return pl.pallas_call(
        body, name="rec_bwd", grid=(N_CB,),
        in_specs=[up, yb, col, cw, cbias, wbd, wbd, vec2, vec2, vec2],
        out_specs=[col, col, cw, cbias, wbd, wbd, vec2, vec2, vec2],
        out_shape=[_sds((T, D), BF16), _sds((T, D), BF16), _sds((4, D), F32), _sds((1, D), F32),
                   _sds((2, N_CB, REC_CB, REC_CB), F32), _sds((2, N_CB, REC_CB, REC_CB), F32),
                   _sds((2, D), F32), _sds((2, D), F32), _sds((2, D), F32)],
        scratch_shapes=[padded, full, full, full, full, full, full, full, full, full, full, padded],
        compiler_params=_params(("parallel",)))(z, z, dg, conv_w, conv_b, wa, wi, ba, bi, lam)


def _local_step(x, target, p):
    x = x.reshape(T, D)
    target = target.reshape(T, D)
    tb = _bias_pairs(p["rpb"])
    wa, wi = _block_diag(p["w_rg_a"]), _block_diag(p["w_rg_i"])
    rec_params = (p["conv_w"], p["conv_b"], wa, wi, p["b_rg_a"], p["b_rg_i"], p["lru_lambda"])

    h1 = _rms_fwd("rms1_fwd", x, p["ln1_g"])
    (z,) = _mm_nn_cols("mm_z", h1, p["w_in"], F32, bias=p["b_in"])
    att = _attn_fwd(z, tb)
    g = _rec_fwd(z, *rec_params)
    (y_att,) = _mm_nn_cols("mm_y_att", att, p["w_att_o"], F32)
    (y_rec,) = _mm_nn_rows("mm_y_rec", g, p["w_rec_o"], F32, tk=D)
    mixed = _merge_fwd(y_att, y_rec, z)

    def add_res(r, ex, outs):
        outs[0][...] = ex[0][...] + r

    res_spec = pl.BlockSpec((TM, D), lambda i, j, k: (i, 0))
    (x1,) = _mm_nn_rows("mm_x1", mixed, p["w_out"], F32, tk=D, extras=[x], extra_specs=[res_spec], epilogue=add_res)
    h2 = _rms_fwd("rms2_fwd", x1, p["ln2_g"])

    def relu2(r, ex, outs):
        outs[0][...] = r
        rp = jnp.maximum(r, 0.0)
        outs[1][...] = (rp * rp).astype(BF16)

    ff_blk = pl.BlockSpec((TM, D), lambda j, i, k: (i, j))
    f, s = _mm_nn_cols("mm_ff1", h2, p["w_ff1"], F32, epilogue=relu2,
                       out_shapes=[_sds((T, D_FF), F32), _sds((T, D_FF), BF16)], out_specs=[ff_blk, ff_blk])
    (x2,) = _mm_nn_rows("mm_x2", s, p["w_ff2"], F32, tk=D, extras=[x1], extra_specs=[res_spec], epilogue=add_res)
    loss, dx2, g_lnf = _loss_head(x2, target, p["lnf_g"])

    def relu2_bwd(r, ex, outs):
        outs[0][...] = (r * 2.0 * jnp.maximum(ex[0][...], 0.0)).astype(BF16)

    (df,) = _mm_nt_rows("mm_df", dx2, p["w_ff2"], BF16, tn=D, extras=[f],
                        extra_specs=[pl.BlockSpec((TM, D), lambda j, i, k: (i, j))], epilogue=relu2_bwd)
    (g_w_ff2,) = _mm_tn_rows("mm_g_ff2", s, dx2, tm=D)
    (g_w_ff1,) = _mm_tn_cols("mm_g_ff1", h2, df, D)
    (dh2,) = _mm_nt_cols("mm_dh2", df, p["w_ff1"], F32)
    dx1, g_ln2 = _rms_bwd("rms2_bwd", dh2, x1, p["ln2_g"], dx2)

    (dmixed,) = _mm_nt_rows("mm_dmixed", dx1, p["w_out"], F32, tn=D)
    (g_w_out,) = _mm_tn_rows("mm_g_out", mixed, dx1, tm=D)
    dy_att, dy_rec, dg_att, dg_rec = _merge_bwd(dmixed, y_att, y_rec, z)
    (d_att,) = _mm_nt_cols("mm_d_att", dy_att, p["w_att_o"], BF16)
    (g_w_att_o,) = _mm_tn_cols("mm_g_att_o", att, dy_att, D // N_CHIPS)
    (d_g,) = _mm_nt_rows("mm_d_g", dy_rec, p["w_rec_o"], BF16, tn=D)
    (g_w_rec_o,) = _mm_tn_rows("mm_g_rec_o", g, dy_rec, tm=D)

    dq, dk, dv, ds_acc = _attn_bwd(z, tb, d_att)
    g_rpb = _rpb_grad(ds_acc)
    d_up, d_yb, g_conv_w, g_conv_b, g_wa, g_wi, g_ba, g_bi, g_lam = _rec_bwd(z, d_g, *rec_params)
    dz = jnp.concatenate([dq, dk, dv, d_up, d_yb, dg_att, dg_rec], axis=1)

    g_w_in, g_b_in = _mm_tn_cols("mm_g_in", h1, dz, D_IN // N_CHIPS, colsum=True)
    (dh1,) = _mm_nt_cols("mm_dh1", dz, p["w_in"], F32)
    grad_x, g_ln1 = _rms_bwd("rms1_bwd", dh1, x, p["ln1_g"], dx1)

    grads = dict(ln1_g=g_ln1, w_in=g_w_in, b_in=g_b_in, rpb=g_rpb, w_att_o=g_w_att_o, conv_w=g_conv_w,
                 conv_b=g_conv_b, w_rg_a=_block_diag_grad(g_wa), b_rg_a=g_ba, w_rg_i=_block_diag_grad(g_wi),
                 b_rg_i=g_bi, lru_lambda=g_lam, w_rec_o=g_w_rec_o, w_out=g_w_out, ln2_g=g_ln2,
                 w_ff1=g_w_ff1, w_ff2=g_w_ff2, lnf_g=g_lnf)
    return loss, grad_x.reshape(1, T, D), grads


_ANY = pl.BlockSpec(memory_space=pl.ANY)
N_PEERS = N_CHIPS - 1


def _place():
    x, y, c = lax.axis_index("x"), lax.axis_index("y"), lax.axis_index("c")
    peers = [(1 - x, y), (x, 1 - y), (1 - x, 1 - y)]
    return x, y, c, 2 * x + y, peers


def _remote(src, dst, send_sem, recv_sem, dev):
    return pltpu.make_async_remote_copy(src_ref=src, dst_ref=dst, send_sem=send_sem, recv_sem=recv_sem,
                                        device_id=dev, device_id_type=MESH)


def _prefetch_call(body, name, ids, grid, in_specs, out_specs, out_shape, args):
    spec = pltpu.PrefetchScalarGridSpec(num_scalar_prefetch=1, grid=grid, in_specs=in_specs, out_specs=out_specs)
    return pl.pallas_call(body, name=name, grid_spec=spec, out_shape=out_shape,
                          compiler_params=_params(("parallel",) * len(grid)))(ids, *args)


def _cast_bf16(name, w, chip_id):
    rows, cols = w.shape
    rb = min(rows, 256)

    def body(ids_ref, w_ref, o_ref):
        o_ref[...] = w_ref[...].astype(BF16)

    return _prefetch_call(body, name, chip_id, (rows // rb,),
                          [pl.BlockSpec((rb, cols), lambda i, ids: (i, 0))],
                          pl.BlockSpec((None, rb, cols), lambda i, ids: (ids[0], i, 0)),
                          _sds((N_CHIPS, rows, cols), BF16), (w,))


def _dma_sems(*counts):
    return [pltpu.SemaphoreType.DMA((k,)) for k in counts]


def _all_gather(slots):
    n = len(slots)

    def body(*refs):
        bufs = refs[n:2 * n]
        ici_send, ici_recv, d2d_send, d2d_recv = refs[2 * n:]
        x, y, c, chip, peers = _place()
        sibling = (x, y, 1 - c)

        def halves(t):
            half = bufs[t].shape[1] // 2
            return pl.ds(c * half, half), pl.ds((1 - c) * half, half)

        sends = []
        for t in range(n):
            mine, _ = halves(t)
            for r, (px, py) in enumerate(peers):
                k = t * N_PEERS + r
                own = bufs[t].at[chip, mine]
                sends.append(_remote(own, own, ici_send.at[k], ici_recv.at[k], (px, py, c)))
                sends[-1].start()
        for t in range(n):
            mine, _ = halves(t)
            for r, (px, py) in enumerate(peers):
                k = t * N_PEERS + r
                landed = bufs[t].at[2 * px + py, mine]
                _remote(landed, landed, ici_send.at[k], ici_recv.at[k], (px, py, c)).wait_recv()
                sends.append(_remote(landed, landed, d2d_send.at[k], d2d_recv.at[k], sibling))
                sends[-1].start()
        for t in range(n):
            _, theirs = halves(t)
            for r, (px, py) in enumerate(peers):
                k = t * N_PEERS + r
                landed = bufs[t].at[2 * px + py, theirs]
                _remote(landed, landed, d2d_send.at[k], d2d_recv.at[k], sibling).wait_recv()
        for cp in sends:
            cp.wait_send()

    nk = n * N_PEERS
    return pl.pallas_call(
        body, name="all_gather_weights", in_specs=[_ANY] * n, out_specs=[_ANY] * n,
        out_shape=[_sds(s.shape, s.dtype) for s in slots], input_output_aliases={t: t for t in range(n)},
        scratch_shapes=_dma_sems(nk, nk, nk, nk))(*slots)


def _pair_exchange(grads, small):
    n = len(grads)

    def body(*refs):
        ins, small_in = refs[:n], refs[n]
        got, small_got = refs[n + 1:2 * n + 1], refs[2 * n + 1]
        send_sem, recv_sem = refs[2 * n + 2:]
        x, y, c, _, _ = _place()
        sibling = (x, y, 1 - c)
        copies = []
        for t in range(n):
            half = ins[t].shape[1] // 2
            for j in range(N_CHIPS):
                k = t * N_CHIPS + j
                copies.append(_remote(ins[t].at[j, pl.ds((1 - c) * half, half)], got[t].at[j],
                                      send_sem.at[k], recv_sem.at[k], sibling))
        copies.append(_remote(small_in, small_got, send_sem.at[n * N_CHIPS], recv_sem.at[n * N_CHIPS], sibling))
        for cp in copies:
            cp.start()
        for cp in copies:
            cp.wait_recv()
        for cp in copies:
            cp.wait_send()

    ns = n * N_CHIPS + 1
    res = pl.pallas_call(
        body, name="grad_pair_exchange", in_specs=[_ANY] * (n + 1), out_specs=[_ANY] * (n + 1),
        out_shape=[_sds((N_CHIPS, g.shape[1] // 2, g.shape[2]), F32) for g in grads] + [_sds(small.shape, F32)],
        scratch_shapes=_dma_sems(ns, ns))(*grads, small)
    return res[:n], res[n]


def _chip_exchange(sums_bf16, small):
    n = len(sums_bf16)

    def body(*refs):
        bf_in, small_in = refs[:n], refs[n]
        got, small_got = refs[n + 1:2 * n + 1], refs[2 * n + 1]
        send_sem, recv_sem = refs[2 * n + 2:]
        x, y, c, chip, peers = _place()
        half_s = small_in.shape[0] // 2
        copies = []
        for r, (px, py) in enumerate(peers):
            for t in range(n):
                k = t * N_PEERS + r
                copies.append(_remote(bf_in[t].at[2 * px + py], got[t].at[r], send_sem.at[k], recv_sem.at[k], (px, py, c)))
            k = n * N_PEERS + r
            copies.append(_remote(small_in.at[pl.ds(c * half_s, half_s)], small_got.at[r],
                                  send_sem.at[k], recv_sem.at[k], (px, py, c)))
        for cp in copies:
            cp.start()
        for cp in copies:
            cp.wait_recv()
        for cp in copies:
            cp.wait_send()

    ns = (n + 1) * N_PEERS
    res = pl.pallas_call(
        body, name="grad_chip_exchange", in_specs=[_ANY] * (n + 1), out_specs=[_ANY] * (n + 1),
        out_shape=([_sds((N_PEERS,) + s.shape[1:], BF16) for s in sums_bf16]
                   + [_sds((N_PEERS, small.shape[0] // 2, 128), F32)]),
        scratch_shapes=_dma_sems(ns, ns))(*sums_bf16, small)
    return res[:n], res[n]


def _half_swap(bufs):
    n = len(bufs)

    def body(*refs):
        outs = refs[n:2 * n]
        send_sem, recv_sem = refs[2 * n:]
        x, y, c, _, _ = _place()
        sibling = (x, y, 1 - c)
        copies = []
        for t in range(n):
            h = outs[t].shape[0] // 2
            mine = outs[t].at[pl.ds(c * h, h)]
            copies.append(_remote(mine, mine, send_sem.at[t], recv_sem.at[t], sibling))
            copies[-1].start()
        for t in range(n):
            h = outs[t].shape[0] // 2
            theirs = outs[t].at[pl.ds((1 - c) * h, h)]
            _remote(theirs, theirs, send_sem.at[t], recv_sem.at[t], sibling).wait_recv()
        for cp in copies:
            cp.wait_send()

    return pl.pallas_call(
        body, name="grad_half_swap", in_specs=[_ANY] * n, out_specs=[_ANY] * n,
        out_shape=[_sds(b.shape, b.dtype) for b in bufs], input_output_aliases={t: t for t in range(n)},
        scratch_shapes=_dma_sems(n, n))(*bufs)


def _pair_sum(name, grad, got, ids):
    _, rows, cols = got.shape
    rb = min(rows, 256)
    nb = rows // rb
    blk = pl.BlockSpec((None, rb, cols), lambda j, i, ids: (j, i, 0))
    mine = pl.BlockSpec((None, rb, cols), lambda j, i, ids: (j, ids[1] * nb + i, 0))

    def body(ids_ref, a_ref, b_ref, s_ref, sb_ref):
        s = a_ref[...] + b_ref[...]
        s_ref[...] = s
        sb_ref[...] = s.astype(BF16)

    return _prefetch_call(body, name, ids, (N_CHIPS, nb), [mine, blk], [blk, blk],
                          [_sds(got.shape, F32), _sds(got.shape, BF16)], (grad, got))


def _chip_sum(name, sums, got, ids):
    _, rows, cols = sums.shape
    rb = min(rows, 256)
    nb = rows // rb
    own = pl.BlockSpec((None, rb, cols), lambda i, ids: (ids[0], i, 0))
    blk3 = pl.BlockSpec((N_PEERS, rb, cols), lambda i, ids: (0, i, 0))
    out = pl.BlockSpec((rb, cols), lambda i, ids: (ids[1] * nb + i, 0))

    def body(ids_ref, a_ref, b_ref, o_ref):
        o_ref[...] = ((a_ref[...] + b_ref[0].astype(F32)) + b_ref[1].astype(F32)) + b_ref[2].astype(F32)

    return _prefetch_call(body, name, ids, (nb,), [own, blk3], out, _sds((2 * rows, cols), F32), (sums, got))


SMALL_RB = 280


def _small_pair_sum(own, got):
    blk = pl.BlockSpec((SMALL_RB, 128), lambda i: (i, 0))

    def body(a_ref, b_ref, o_ref):
        o_ref[...] = a_ref[...] + b_ref[...]

    return pl.pallas_call(body, name="small_pair_sum", grid=(own.shape[0] // SMALL_RB,), in_specs=[blk, blk],
                          out_specs=blk, out_shape=_sds(own.shape, F32),
                          compiler_params=_params(("parallel",)))(own, got)


def _small_chip_sum(pair, got, ids):
    nb = pair.shape[0] // 2 // SMALL_RB
    half = pl.BlockSpec((SMALL_RB, 128), lambda i, ids: (ids[1] * nb + i, 0))
    blk3 = pl.BlockSpec((N_PEERS, SMALL_RB, 128), lambda i, ids: (0, i, 0))

    def body(ids_ref, a_ref, b_ref, o_ref):
        o_ref[...] = (a_ref[...] + b_ref[1]) + (b_ref[0] + b_ref[2])

    return _prefetch_call(body, "small_chip_sum", ids, (nb,), [half, blk3], half, _sds(pair.shape, F32), (pair, got))


def _adamw_math(w, g, m, v):
    m = ADAM_B1 * m + (1.0 - ADAM_B1) * g
    v = ADAM_B2 * v + (1.0 - ADAM_B2) * (g * g)
    m_hat = m / (1.0 - ADAM_B1 ** ADAM_STEP)
    v_hat = v / (1.0 - ADAM_B2 ** ADAM_STEP)
    delta = -ADAM_LR * (m_hat / (jnp.sqrt(v_hat) + ADAM_EPS) + ADAM_WD * w)
    return delta, m, v


def _adamw(name, w, g, m, v, rb=None):
    rows, cols = w.shape
    rb = rows if rb is None else rb
    blk = pl.BlockSpec((rb, cols), lambda i: (i, 0))

    def body(w_ref, g_ref, m_ref, v_ref, d_ref, nm_ref, nv_ref):
        d, nm, nv = _adamw_math(w_ref[...], g_ref[...], m_ref[...], v_ref[...])
        d_ref[...] = d
        nm_ref[...] = nm
        nv_ref[...] = nv

    return pl.pallas_call(body, name=name, grid=(rows // rb,), in_specs=[blk] * 4, out_specs=[blk] * 3,
                          out_shape=[_sds(w.shape, F32)] * 3, compiler_params=_params(("parallel",)))(w, g, m, v)


BIG = ("w_in", "w_att_o", "w_rec_o", "w_out", "w_ff1", "w_ff2")
COL_SHARDED = ("w_in", "w_att_o", "w_ff1")
SHARDED_VECS = ("conv_w", "b_rg_a", "b_rg_i", "lru_lambda")
SMALL = ("ln1_g", "b_in", "rpb", "conv_w", "conv_b", "w_rg_a", "b_rg_a", "w_rg_i", "b_rg_i", "lru_lambda",
         "ln2_g", "lnf_g")
SMALL_ROWS = 2240
ORDER = ("ln1_g", "w_in", "b_in", "rpb", "w_att_o", "conv_w", "conv_b", "w_rg_a", "b_rg_a", "w_rg_i", "b_rg_i",
         "lru_lambda", "w_rec_o", "w_out", "ln2_g", "w_ff1", "w_ff2", "lnf_g")


def _pack_small(grads, loss):
    parts, sizes = [], {}
    for n in SMALL:
        flat = grads[n].reshape(-1)
        pad = (-flat.shape[0]) % 128
        sizes[n] = (flat.shape[0], flat.shape[0] + pad)
        parts.append(jnp.pad(flat, (0, pad)))
    total = sum(s[1] for s in sizes.values())
    parts.append(jnp.pad(loss.reshape(1), (0, SMALL_ROWS * 128 - total - 1)))
    return jnp.concatenate(parts).reshape(SMALL_ROWS, 128), sizes


def _unpack_small(buf, sizes, shapes):
    flat = buf.reshape(-1)
    out, pos = {}, 0
    for n in SMALL:
        size, padded = sizes[n]
        out[n] = flat[pos:pos + size].reshape(shapes[n])
        pos += padded
    return out, flat[pos]


def _gather_weights(w, chip):
    chip_id = chip.astype(jnp.int32).reshape(1)
    slots = [_cast_bf16("cast_" + n, w[n][0], chip_id) for n in BIG]
    vec_rows = [w[n][0] for n in SHARDED_VECS]
    vec_shard = jnp.concatenate(vec_rows + [jnp.zeros((16 - 10, D // N_CHIPS), F32)], axis=0)
    vec_slots = lax.dynamic_update_slice(jnp.zeros((N_CHIPS, 16, D // N_CHIPS), F32), vec_shard[None], (chip, 0, 0))
    gathered = _all_gather(slots + [vec_slots])
    full = dict(zip(BIG, gathered[:len(BIG)]))
    vecs = gathered[len(BIG)].transpose(1, 0, 2).reshape(16, D)
    return dict(
        w_in=full["w_in"], w_att_o=full["w_att_o"], w_ff1=full["w_ff1"],
        w_rec_o=full["w_rec_o"].reshape(D, D), w_out=full["w_out"].reshape(D, D), w_ff2=full["w_ff2"].reshape(D_FF, D),
        ln1_g=w["ln1_g"], b_in=w["b_in"], rpb=w["rpb"][0], conv_w=vecs[0:4], conv_b=w["conv_b"], w_rg_a=w["w_rg_a"][0],
        b_rg_a=vecs[4:6], w_rg_i=w["w_rg_i"][0], b_rg_i=vecs[6:8], lru_lambda=vecs[8:10], ln2_g=w["ln2_g"],
        lnf_g=w["lnf_g"].reshape(1, D))


def _reduce_grads(big, small, ids):
    got1, small_got1 = _pair_exchange(big, small)
    sums = [_pair_sum("pair_sum_" + n, a, b, ids) for n, a, b in zip(BIG, big, got1)]
    small_sum = _small_pair_sum(small, small_got1)
    got2, small_got2 = _chip_exchange([s[1] for s in sums], small_sum)
    halves = [_chip_sum("chip_sum_" + n, s[0], b, ids) for n, s, b in zip(BIG, sums, got2)]
    halves.append(_small_chip_sum(small_sum, small_got2, ids))
    return _half_swap(halves)


def kernel(x, ln1_g, w_in, b_in, rpb, w_att_o, conv_w, conv_b, w_rg_a, b_rg_a, w_rg_i, b_rg_i, lru_lambda, w_rec_o, w_out, ln2_g, w_ff1, w_ff2, lnf_g, loss_target, m_ln1_g, m_w_in, m_b_in, m_rpb, m_w_att_o, m_conv_w, m_conv_b, m_w_rg_a, m_b_rg_a, m_w_rg_i, m_b_rg_i, m_lru_lambda, m_w_rec_o, m_w_out, m_ln2_g, m_w_ff1, m_w_ff2, m_lnf_g, v_ln1_g, v_w_in, v_b_in, v_rpb, v_w_att_o, v_conv_w, v_conv_b, v_w_rg_a, v_b_rg_a, v_w_rg_i, v_b_rg_i, v_lru_lambda, v_w_rec_o, v_w_out, v_ln2_g, v_w_ff1, v_w_ff2, v_lnf_g):
    w = dict(ln1_g=ln1_g, w_in=w_in, b_in=b_in, rpb=rpb, w_att_o=w_att_o, conv_w=conv_w, conv_b=conv_b,
             w_rg_a=w_rg_a, b_rg_a=b_rg_a, w_rg_i=w_rg_i, b_rg_i=b_rg_i, lru_lambda=lru_lambda, w_rec_o=w_rec_o,
             w_out=w_out, ln2_g=ln2_g, w_ff1=w_ff1, w_ff2=w_ff2, lnf_g=lnf_g)
    m = dict(ln1_g=m_ln1_g, w_in=m_w_in, b_in=m_b_in, rpb=m_rpb, w_att_o=m_w_att_o, conv_w=m_conv_w,
             conv_b=m_conv_b, w_rg_a=m_w_rg_a, b_rg_a=m_b_rg_a, w_rg_i=m_w_rg_i, b_rg_i=m_b_rg_i,
             lru_lambda=m_lru_lambda, w_rec_o=m_w_rec_o, w_out=m_w_out, ln2_g=m_ln2_g, w_ff1=m_w_ff1,
             w_ff2=m_w_ff2, lnf_g=m_lnf_g)
    v = dict(ln1_g=v_ln1_g, w_in=v_w_in, b_in=v_b_in, rpb=v_rpb, w_att_o=v_w_att_o, conv_w=v_conv_w,
             conv_b=v_conv_b, w_rg_a=v_w_rg_a, b_rg_a=v_b_rg_a, w_rg_i=v_w_rg_i, b_rg_i=v_b_rg_i,
             lru_lambda=v_lru_lambda, w_rec_o=v_w_rec_o, w_out=v_w_out, ln2_g=v_ln2_g, w_ff1=v_w_ff1,
             w_ff2=v_w_ff2, lnf_g=v_lnf_g)
    chip = 2 * lax.axis_index("x") + lax.axis_index("y")
    ids = jnp.stack([chip, lax.axis_index("c")]).astype(jnp.int32)

    p = _gather_weights(w, chip)
    loss, grad_x, g = _local_step(x, loss_target, p)

    big = [g["w_in"], g["w_att_o"], g["w_rec_o"].reshape(N_CHIPS, D // N_CHIPS, D),
           g["w_out"].reshape(N_CHIPS, D // N_CHIPS, D), g["w_ff1"], g["w_ff2"].reshape(N_CHIPS, D_FF // N_CHIPS, D)]
    small, sizes = _pack_small(g, loss)
    swapped = _reduce_grads(big, small, ids)
    shapes = {n: g[n].shape for n in SMALL}
    gsmall, loss = _unpack_small(swapped[len(BIG)], sizes, shapes)

    grad = dict(zip(BIG, swapped[:len(BIG)]))
    for n in SMALL:
        gn = gsmall[n]
        if n in SHARDED_VECS:
            gn = lax.dynamic_slice_in_dim(gn, chip * (D // N_CHIPS), D // N_CHIPS, axis=1)
        grad[n] = gn

    out_grad, out_delta, out_m, out_v = {}, {}, {}, {}
    for n in ORDER:
        shape = w[n].shape
        if n in BIG:
            two_d = grad[n].shape
            rb = 256
        else:
            two_d = (int(np.prod(shape[:-1])), shape[-1])
            rb = None
        gn = grad[n].reshape(two_d)
        d, nm, nv = _adamw("adamw_" + n, w[n].reshape(two_d), gn, m[n].reshape(two_d), v[n].reshape(two_d), rb)
        out_grad[n], out_delta[n], out_m[n], out_v[n] = (gn.reshape(shape), d.reshape(shape), nm.reshape(shape),
                                                         nv.reshape(shape))
    return (loss, grad_x, *[out_grad[n] for n in ORDER], *[out_delta[n] for n in ORDER],
            *[out_m[n] for n in ORDER], *[out_v[n] for n in ORDER])
```

```python
import functools

import numpy as np
import jax
import jax.numpy as jnp
from jax import lax
from jax.experimental import pallas as pl
from jax.experimental.pallas import tpu as pltpu

F32 = jnp.float32
BF16 = jnp.bfloat16

T = 2048
D = 1024
D_ATT = 512
D_IN = 5632
D_FF = 4096
N_HEADS = 8
HEAD_DIM = 64
GRID_W = 64
N_ROWS = T // GRID_W
WIN_H = 8
WIN_W = 16
KEYS = WIN_H * GRID_W
N_CHIPS = 4
EPS = 1e-6
LRU_C = 8.0
SCALE = HEAD_DIM ** -0.5
REC_CB = 256
REC_CHUNK = 256
PAD = 8

ADAM_LR = 0.001
ADAM_B1 = 0.9
ADAM_B2 = 0.999
ADAM_EPS = 1e-08
ADAM_WD = 0.01
ADAM_STEP = 10

VMEM_LIMIT = 56 * 1024 * 1024

NN = (((1,), (0,)), ((), ()))
NT = (((1,), (1,)), ((), ()))
TN = (((0,), (0,)), ((), ()))
MESH = pl.DeviceIdType.MESH


def _params(sem=None):
    return pltpu.CompilerParams(dimension_semantics=sem, vmem_limit_bytes=VMEM_LIMIT)


def _dot(a, b, dims):
    return lax.dot_general(a, b, dims, preferred_element_type=F32)


def _sigmoid(x):
    return 0.5 * jnp.tanh(0.5 * x) + 0.5


def _matmul(name, a, b, *, dims, grid, a_spec, b_spec, out_shapes, out_specs, acc_shape,
            extras=(), extra_specs=(), epilogue=None, colsum_spec=None, colsum_shape=None):
    nk = grid[2]
    n_extra = len(extras)
    n_out = len(out_shapes)
    with_colsum = colsum_spec is not None

    def body(a_ref, b_ref, *rest):
        ex = rest[:n_extra]
        outs = rest[n_extra:n_extra + n_out]
        pos = n_extra + n_out
        cs_out = rest[pos] if with_colsum else None
        pos += 1 if with_colsum else 0
        acc = rest[pos]
        cs_acc = rest[pos + 1] if with_colsum else None
        k = pl.program_id(2)

        @pl.when(k == 0)
        def _():
            acc[...] = jnp.zeros_like(acc)
            if with_colsum:
                cs_acc[...] = jnp.zeros_like(cs_acc)

        bv = b_ref[...]
        acc[...] += _dot(a_ref[...].astype(BF16), bv.astype(BF16), dims)
        if with_colsum:
            cs_acc[...] += jnp.sum(bv.astype(F32), axis=0, keepdims=True)

        @pl.when(k == nk - 1)
        def _():
            r = acc[...]
            if epilogue is None:
                outs[0][...] = r.astype(outs[0].dtype)
            else:
                epilogue(r, ex, outs)
            if with_colsum:
                cs_out[...] = cs_acc[...]

    shapes = list(out_shapes)
    specs = list(out_specs)
    scratch = [pltpu.VMEM(acc_shape, F32)]
    if with_colsum:
        shapes.append(colsum_shape)
        specs.append(colsum_spec)
        scratch.append(pltpu.VMEM((1, acc_shape[1]), F32))
    res = pl.pallas_call(
        body, name=name, grid=grid,
        in_specs=[a_spec, b_spec, *extra_specs],
        out_specs=specs, out_shape=shapes, scratch_shapes=scratch,
        compiler_params=_params(("parallel", "parallel", "arbitrary")),
    )(a, b, *extras)
    return res


def _sds(shape, dtype):
    return jax.ShapeDtypeStruct(shape, dtype)


TM = 512
NI = T // TM


def _mm_nn_cols(name, a, wg, out_dtype, *, bias=None, extras=(), extra_specs=(), epilogue=None,
                out_shapes=None, out_specs=None):
    k_dim, n4 = wg.shape[1], wg.shape[2]
    ex, exs = list(extras), list(extra_specs)
    if bias is not None:
        ex = [bias] + ex
        exs = [pl.BlockSpec((1, n4), lambda j, i, k: (0, j))] + exs
        user_ep = epilogue

        def epilogue(r, e, outs):
            r = r + e[0][...]
            if user_ep is None:
                outs[0][...] = r.astype(outs[0].dtype)
            else:
                user_ep(r, e[1:], outs)
    if out_shapes is None:
        out_shapes = [_sds((T, N_CHIPS * n4), out_dtype)]
        out_specs = [pl.BlockSpec((TM, n4), lambda j, i, k: (i, j))]
    return _matmul(
        name, a, wg, dims=NN, grid=(N_CHIPS, NI, 1),
        a_spec=pl.BlockSpec((TM, k_dim), lambda j, i, k: (i, 0)),
        b_spec=pl.BlockSpec((None, k_dim, n4), lambda j, i, k: (j, 0, 0)),
        out_shapes=out_shapes, out_specs=out_specs, acc_shape=(TM, n4),
        extras=ex, extra_specs=exs, epilogue=epilogue)


def _mm_nn_rows(name, a, w, out_dtype, *, tk, extras=(), extra_specs=(), epilogue=None):
    k_dim, n = w.shape
    return _matmul(
        name, a, w, dims=NN, grid=(NI, 1, k_dim // tk),
        a_spec=pl.BlockSpec((TM, tk), lambda i, j, k: (i, k)),
        b_spec=pl.BlockSpec((tk, n), lambda i, j, k: (k, 0)),
        out_shapes=[_sds((T, n), out_dtype)],
        out_specs=[pl.BlockSpec((TM, n), lambda i, j, k: (i, 0))], acc_shape=(TM, n),
        extras=extras, extra_specs=extra_specs, epilogue=epilogue)


def _mm_nt_cols(name, a, wg, out_dtype):
    k_dim, n4 = wg.shape[1], wg.shape[2]
    return _matmul(
        name, a, wg, dims=NT, grid=(NI, 1, N_CHIPS),
        a_spec=pl.BlockSpec((TM, n4), lambda i, j, k: (i, k)),
        b_spec=pl.BlockSpec((None, k_dim, n4), lambda i, j, k: (k, 0, 0)),
        out_shapes=[_sds((T, k_dim), out_dtype)],
        out_specs=[pl.BlockSpec((TM, k_dim), lambda i, j, k: (i, 0))], acc_shape=(TM, k_dim))


def _mm_nt_rows(name, a, w, out_dtype, *, tn, extras=(), extra_specs=(), epilogue=None):
    k_dim, n = w.shape
    return _matmul(
        name, a, w, dims=NT, grid=(k_dim // tn, NI, 1),
        a_spec=pl.BlockSpec((TM, n), lambda j, i, k: (i, 0)),
        b_spec=pl.BlockSpec((tn, n), lambda j, i, k: (j, 0)),
        out_shapes=[_sds((T, k_dim), out_dtype)],
        out_specs=[pl.BlockSpec((TM, tn), lambda j, i, k: (i, j))], acc_shape=(TM, tn),
        extras=extras, extra_specs=extra_specs, epilogue=epilogue)


def _mm_tn_cols(name, a, g, n4, *, colsum=False):
    k_dim = a.shape[1]
    kw = {}
    if colsum:
        kw = dict(colsum_spec=pl.BlockSpec((1, n4), lambda j, i, k: (0, j)),
                  colsum_shape=_sds((1, N_CHIPS * n4), F32))
    return _matmul(
        name, a, g, dims=TN, grid=(N_CHIPS, 1, NI),
        a_spec=pl.BlockSpec((TM, k_dim), lambda j, i, k: (k, 0)),
        b_spec=pl.BlockSpec((TM, n4), lambda j, i, k: (k, j)),
        out_shapes=[_sds((N_CHIPS, k_dim, n4), F32)],
        out_specs=[pl.BlockSpec((None, k_dim, n4), lambda j, i, k: (j, 0, 0))],
        acc_shape=(k_dim, n4), **kw)


def _mm_tn_rows(name, a, g, *, tm):
    k_dim, n = a.shape[1], g.shape[1]
    return _matmul(
        name, a, g, dims=TN, grid=(k_dim // tm, 1, NI),
        a_spec=pl.BlockSpec((TM, tm), lambda j, i, k: (k, j)),
        b_spec=pl.BlockSpec((TM, n), lambda j, i, k: (k, 0)),
        out_shapes=[_sds((k_dim, n), F32)],
        out_specs=[pl.BlockSpec((tm, n), lambda j, i, k: (j, 0))], acc_shape=(tm, n))


TE = 256
NE = T // TE
_ROW = pl.BlockSpec((TE, D), lambda i: (i, 0))
_VEC = pl.BlockSpec((1, D), lambda i: (0, 0))


def _rms_fwd(name, x, g):
    def body(x_ref, g_ref, h_ref):
        xv = x_ref[...]
        rstd = lax.rsqrt(jnp.mean(xv * xv, axis=-1, keepdims=True) + EPS)
        h_ref[...] = (xv * rstd * g_ref[...]).astype(BF16)

    return pl.pallas_call(body, name=name, grid=(NE,), in_specs=[_ROW, _VEC], out_specs=_ROW,
                          out_shape=_sds((T, D), BF16), compiler_params=_params(("parallel",)))(x, g)


def _rms_bwd(name, dh, x, g, dres):
    def body(dh_ref, x_ref, g_ref, dres_ref, dx_ref, dg_ref):
        xv = x_ref[...]
        rstd = lax.rsqrt(jnp.mean(xv * xv, axis=-1, keepdims=True) + EPS)
        xhat = xv * rstd
        dhv = dh_ref[...]
        dy = dhv * g_ref[...]
        dx_ref[...] = dres_ref[...] + rstd * (dy - xhat * jnp.mean(dy * xhat, axis=-1, keepdims=True))

        @pl.when(pl.program_id(0) == 0)
        def _():
            dg_ref[...] = jnp.zeros_like(dg_ref)

        dg_ref[...] += jnp.sum(dhv * xhat, axis=0, keepdims=True)

    return pl.pallas_call(body, name=name, grid=(NE,), in_specs=[_ROW, _ROW, _VEC, _ROW],
                          out_specs=[_ROW, _VEC], out_shape=[_sds((T, D), F32), _sds((1, D), F32)],
                          compiler_params=_params(("arbitrary",)))(dh, x, g, dres)


def _loss_head(x2, target, g):
    def body(x_ref, t_ref, g_ref, loss_ref, dx_ref, dg_ref):
        xv = x_ref[...]
        rstd = lax.rsqrt(jnp.mean(xv * xv, axis=-1, keepdims=True) + EPS)
        xhat = xv * rstd
        gv = g_ref[...]
        err = xhat * gv - t_ref[...]
        dy = err * (1.0 / D)
        dxh = dy * gv
        dx_ref[...] = rstd * (dxh - xhat * jnp.mean(dxh * xhat, axis=-1, keepdims=True))

        @pl.when(pl.program_id(0) == 0)
        def _():
            dg_ref[...] = jnp.zeros_like(dg_ref)
            loss_ref[...] = jnp.zeros_like(loss_ref)

        dg_ref[...] += jnp.sum(dy * xhat, axis=0, keepdims=True)
        loss_ref[...] += (0.5 / D) * jnp.sum(jnp.sum(err * err, axis=1, keepdims=True), axis=0, keepdims=True)

    return pl.pallas_call(
        body, name="loss_head", grid=(NE,), in_specs=[_ROW, _ROW, _VEC],
        out_specs=[pl.BlockSpec((1, 1), lambda i: (0, 0)), _ROW, _VEC],
        out_shape=[_sds((1, 1), F32), _sds((T, D), F32), _sds((1, D), F32)],
        compiler_params=_params(("arbitrary",)))(x2, target, g)


MW = 512
_G_ATT_BLK = 3584 // MW
_G_REC_BLK = 4608 // MW


def _merge_specs():
    y = pl.BlockSpec((TM, MW), lambda i, j: (i, j))
    ga = pl.BlockSpec((TM, MW), lambda i, j: (i, _G_ATT_BLK + j))
    gr = pl.BlockSpec((TM, MW), lambda i, j: (i, _G_REC_BLK + j))
    return y, ga, gr


def _merge_fwd(y_att, y_rec, z):
    y, ga, gr = _merge_specs()

    def body(ya_ref, yr_ref, ga_ref, gr_ref, m_ref):
        m = _sigmoid(ga_ref[...]) * ya_ref[...] + _sigmoid(gr_ref[...]) * yr_ref[...]
        m_ref[...] = m.astype(BF16)

    return pl.pallas_call(body, name="merge_fwd", grid=(NI, D // MW), in_specs=[y, y, ga, gr], out_specs=y,
                          out_shape=_sds((T, D), BF16),
                          compiler_params=_params(("parallel", "parallel")))(y_att, y_rec, z, z)


def _merge_bwd(dm, y_att, y_rec, z):
    y, ga, gr = _merge_specs()

    def body(dm_ref, ya_ref, yr_ref, ga_ref, gr_ref, dya_ref, dyr_ref, dga_ref, dgr_ref):
        dmv = dm_ref[...]
        sa = _sigmoid(ga_ref[...])
        sr = _sigmoid(gr_ref[...])
        dya_ref[...] = (dmv * sa).astype(BF16)
        dyr_ref[...] = (dmv * sr).astype(BF16)
        dga_ref[...] = (dmv * ya_ref[...] * sa * (1.0 - sa)).astype(BF16)
        dgr_ref[...] = (dmv * yr_ref[...] * sr * (1.0 - sr)).astype(BF16)

    return pl.pallas_call(body, name="merge_bwd", grid=(NI, D // MW), in_specs=[y, y, y, ga, gr],
                          out_specs=[y, y, y, y], out_shape=[_sds((T, D), BF16)] * 4,
                          compiler_params=_params(("parallel", "parallel")))(dm, y_att, y_rec, z, z)


HP = 2 * HEAD_DIM
N_HP = N_HEADS // 2
ATT_UNROLL_FWD = 8
ATT_UNROLL_BWD = 4


def _window_maps():
    diag = np.zeros((GRID_W * GRID_W, 128), np.float32)
    for qc in range(GRID_W):
        w0 = min(max(qc - WIN_W // 2, 0), GRID_W - WIN_W)
        for kc in range(w0, w0 + WIN_W):
            diag[qc * GRID_W + kc, kc - qc + WIN_W - 1] = 1.0
    return diag, diag.sum(axis=1)[None, :]


def _split3(x):
    a = x.astype(BF16)
    r = x - a.astype(F32)
    b = r.astype(BF16)
    c = (r - b.astype(F32)).astype(BF16)
    return a, b, c


N_DROW = 2 * WIN_H - 1
N_DPAIR = N_DROW - 1


def _bias_pairs(rpb):
    diag, valid = _window_maps()
    r2 = jnp.pad(rpb.reshape(N_HEADS * N_DROW, 2 * WIN_W - 1),
                 ((0, 128 - N_HEADS * N_DROW), (0, 128 - (2 * WIN_W - 1))))

    def body(r_ref, d_ref, v_ref, o_ref):
        dv = d_ref[...]
        t = sum(_dot(part, dv, NN) for part in _split3(r_ref[...]))
        o_ref[...] = jnp.where(v_ref[...] > 0.0, t, -1e30)

    t = pl.pallas_call(body, name="rpb_expand", out_shape=_sds((128, GRID_W * GRID_W), F32),
                       compiler_params=_params())(r2, jnp.asarray(diag.T, BF16), jnp.asarray(valid, F32))
    t = t[:N_HEADS * N_DROW].reshape(N_HEADS, N_DROW, GRID_W, GRID_W)
    return jnp.concatenate([t[:, :N_DPAIR], t[:, 1:]], axis=-1)


def _row_bias(tb_ref, hh, d0):
    return jnp.concatenate([tb_ref[hh, d0 + 2 * ii] for ii in range(WIN_H // 2)], axis=1)


def _row_window(r):
    rs = jnp.clip(r - WIN_H // 2, 0, N_ROWS - WIN_H)
    return pl.multiple_of(r * GRID_W, GRID_W), pl.multiple_of(rs * GRID_W, GRID_W), rs - r + (WIN_H - 1)


def _split_heads(src_ref, dst_ref, scale=None):
    for hh in range(2):
        v = src_ref[:, hh * HEAD_DIM:(hh + 1) * HEAD_DIM]
        dst_ref[hh] = (v if scale is None else v * scale).astype(BF16)


def _attn_items(qb_ref, kb_ref, vb_ref, tb_ref, first_row, n_rows):
    wins = [_row_window(first_row + u) for u in range(n_rows)]
    items = [(u, hh) for u in range(n_rows) for hh in range(2)]
    q = [qb_ref[hh, pl.ds(wins[u][0], GRID_W), :] for u, hh in items]
    k = [kb_ref[hh, pl.ds(wins[u][1], KEYS), :] for u, hh in items]
    v = [vb_ref[hh, pl.ds(wins[u][1], KEYS), :] for u, hh in items]
    s = [_dot(qi, ki, NT) + _row_bias(tb_ref, hh, wins[u][2]) for qi, ki, (u, hh) in zip(q, k, items)]
    m = [jnp.max(si, axis=-1, keepdims=True) for si in s]
    e = [jnp.exp(si - mi) for si, mi in zip(s, m)]
    inv = [1.0 / jnp.sum(ei, axis=-1, keepdims=True) for ei in e]
    p = [ei * li for ei, li in zip(e, inv)]
    return wins, items, q, k, v, p


def _attn_in_specs():
    q = pl.BlockSpec((T, HP), lambda p: (0, p))
    k = pl.BlockSpec((T, HP), lambda p: (0, N_HP + p))
    v = pl.BlockSpec((T, HP), lambda p: (0, 2 * N_HP + p))
    tb = pl.BlockSpec((2, N_DPAIR, GRID_W, HP), lambda p: (p, 0, 0, 0))
    return q, k, v, tb


_HEAD_SCRATCH = pltpu.VMEM((2, T, HEAD_DIM), BF16)


def _attn_fwd(z, tb):
    def body(q_ref, k_ref, v_ref, tb_ref, o_ref, qb_ref, kb_ref, vb_ref):
        _split_heads(q_ref, qb_ref, SCALE)
        _split_heads(k_ref, kb_ref)
        _split_heads(v_ref, vb_ref)

        def rows(it, carry):
            wins, items, _, _, v, p = _attn_items(qb_ref, kb_ref, vb_ref, tb_ref, it * ATT_UNROLL_FWD, ATT_UNROLL_FWD)
            o = [_dot(pi.astype(BF16), vi, NN) for pi, vi in zip(p, v)]
            for u, (q0, _, _) in enumerate(wins):
                o_ref[pl.ds(q0, GRID_W), :] = jnp.concatenate(o[2 * u:2 * u + 2], axis=1).astype(BF16)
            return carry

        lax.fori_loop(0, N_ROWS // ATT_UNROLL_FWD, rows, 0)

    blk = pl.BlockSpec((T, HP), lambda p: (0, p))
    return pl.pallas_call(
        body, name="attn_fwd", grid=(N_HP,), in_specs=list(_attn_in_specs()), out_specs=blk,
        out_shape=_sds((T, D_ATT), BF16), scratch_shapes=[_HEAD_SCRATCH] * 3,
        compiler_params=_params(("parallel",)))(z, z, z, tb)


def _attn_bwd(z, tb, d_att):
    def body(q_ref, k_ref, v_ref, tb_ref, do_ref, dq_ref, dk_ref, dv_ref, ds_ref,
             qb_ref, kb_ref, vb_ref, dob_ref, dka_ref, dva_ref):
        _split_heads(q_ref, qb_ref, SCALE)
        _split_heads(k_ref, kb_ref)
        _split_heads(v_ref, vb_ref)
        _split_heads(do_ref, dob_ref)
        dka_ref[...] = jnp.zeros_like(dka_ref)
        dva_ref[...] = jnp.zeros_like(dva_ref)
        ds_ref[...] = jnp.zeros_like(ds_ref)

        def rows(it, carry):
            wins, items, q, k, v, p = _attn_items(qb_ref, kb_ref, vb_ref, tb_ref, it * ATT_UNROLL_BWD, ATT_UNROLL_BWD)
            do = [dob_ref[hh, pl.ds(wins[u][0], GRID_W), :] for u, hh in items]
            dv = [_dot(pi.astype(BF16), di, TN) for pi, di in zip(p, do)]
            dp = [_dot(di, vi, NT) for di, vi in zip(do, v)]
            ds = [pi * (dpi - jnp.sum(dpi * pi, axis=-1, keepdims=True)) for pi, dpi in zip(p, dp)]
            dsb = [d.astype(BF16) for d in ds]
            dq = [_dot(d, ki, NN) * SCALE for d, ki in zip(dsb, k)]
            dk = [_dot(d, qi, TN) for d, qi in zip(dsb, q)]
            for d, (u, hh) in zip(ds, items):
                for ii in range(WIN_H // 2):
                    ds_ref[hh, wins[u][2] + 2 * ii] += d[:, ii * HP:(ii + 1) * HP]
            for u, (q0, k0, _) in enumerate(wins):
                dq_ref[pl.ds(q0, GRID_W), :] = jnp.concatenate(dq[2 * u:2 * u + 2], axis=1).astype(BF16)
                dka_ref[pl.ds(k0, KEYS), :] += jnp.concatenate(dk[2 * u:2 * u + 2], axis=1)
                dva_ref[pl.ds(k0, KEYS), :] += jnp.concatenate(dv[2 * u:2 * u + 2], axis=1)
            return carry

        lax.fori_loop(0, N_ROWS // ATT_UNROLL_BWD, rows, 0)
        dk_ref[...] = dka_ref[...].astype(BF16)
        dv_ref[...] = dva_ref[...].astype(BF16)

    blk = pl.BlockSpec((T, HP), lambda p: (0, p))
    q, k, v, tbs = _attn_in_specs()
    return pl.pallas_call(
        body, name="attn_bwd", grid=(N_HP,), in_specs=[q, k, v, tbs, blk],
        out_specs=[blk, blk, blk, tbs],
        out_shape=[_sds((T, D_ATT), BF16)] * 3 + [_sds((N_HEADS, N_DPAIR, GRID_W, HP), F32)],
        scratch_shapes=[_HEAD_SCRATCH] * 4 + [pltpu.VMEM((T, HP), F32), pltpu.VMEM((T, HP), F32)],
        compiler_params=_params(("parallel",)))(z, z, z, tb, d_att)


def _rpb_grad(ds_acc):
    a = ds_acc.reshape(N_HEADS, N_DPAIR, GRID_W, 2, GRID_W).transpose(0, 1, 3, 2, 4)
    a = jnp.pad(a.reshape(N_HEADS * N_DPAIR * 2, GRID_W * GRID_W), ((0, 256 - N_HEADS * N_DPAIR * 2), (0, 0)))
    sel = np.zeros((128, 256), np.float32)
    for h in range(N_HEADS):
        for pair in range(N_DPAIR):
            for half in range(2):
                sel[h * N_DROW + pair + half, (h * N_DPAIR + pair) * 2 + half] = 1.0
    diag, _ = _window_maps()

    def body(a_ref, sel_ref, diag_ref, o_ref):
        selv = sel_ref[...]
        g = sum(_dot(selv, part, NN) for part in _split3(a_ref[...]))
        diagv = diag_ref[...]
        o_ref[...] = sum(_dot(part, diagv, NN) for part in _split3(g))

    out = pl.pallas_call(body, name="rpb_grad", out_shape=_sds((128, 128), F32),
                         compiler_params=_params())(a, jnp.asarray(sel, BF16), jnp.asarray(diag, BF16))
    return out[:N_HEADS * N_DROW, :2 * WIN_W - 1].reshape(N_HEADS, N_DROW, 2 * WIN_W - 1)


N_CB = D // REC_CB
N_CHUNK = T // REC_CHUNK
N_TILE = T // 8
_U_BLK = 1536 // REC_CB
_Y_BLK = 2560 // REC_CB


def _block_diag(w):
    per = REC_CB // 64
    wt = w.reshape(2, N_CB, per, 64, 64)
    eye = jnp.eye(per, dtype=w.dtype)
    full = wt[:, :, :, :, None, :] * eye[None, None, :, None, :, None]
    return full.reshape(2, N_CB, REC_CB, REC_CB).astype(BF16)


def _block_diag_grad(g):
    per = REC_CB // 64
    g6 = g.reshape(2, N_CB, per, 64, per, 64)
    return jnp.stack([g6[:, :, p, :, p, :] for p in range(per)], axis=2).reshape(2, 16, 64, 64)


def _gelu(x):
    c = 0.7978845608028654
    return 0.5 * x * (1.0 + jnp.tanh(c * (x + 0.044715 * x * x * x)))


def _gelu_grad(x):
    c = 0.7978845608028654
    th = jnp.tanh(c * (x + 0.044715 * x * x * x))
    return 0.5 * (1.0 + th) + 0.5 * x * (1.0 - th * th) * c * (1.0 + 3.0 * 0.044715 * x * x)


def _softplus_neg(lam):
    x = -lam
    e = jnp.exp(-jnp.abs(x))
    w = 1.0 + e
    l1p = jnp.where(w == 1.0, e, jnp.log(w) * e / (w - 1.0))
    return jnp.maximum(x, 0.0) + l1p


def _one_minus_exp(x):
    poly = x * (1.0 + x * (1 / 2 + x * (1 / 6 + x * (1 / 24 + x * (1 / 120 + x * (1 / 720))))))
    return jnp.where(x > -0.125, -poly, 1.0 - jnp.exp(x))


def _conv_taps(pad_ref, t0, w, sign):
    out = None
    for j in range(4):
        term = w[j:j + 1, :] * pad_ref[pl.ds(PAD + t0 + sign * (j - 2), REC_CHUNK), :]
        out = term if out is None else out + term
    return out


def _gates(u, wa, wi, ba, bi, sp):
    ub = u.astype(BF16)
    r = _sigmoid(_dot(ub, wa, NN) + ba)
    i = _sigmoid(_dot(ub, wi, NN) + bi)
    log_a = -LRU_C * r * sp
    a = jnp.exp(log_a)
    mult = jnp.sqrt(jnp.maximum(_one_minus_exp(2.0 * log_a), 0.0))
    return r, i, a, mult


def _tile_scan(a, b, sub, reverse):
    for s in (1, 2, 4):
        if reverse:
            a_s, b_s, m = pltpu.roll(a, 8 - s, 0), pltpu.roll(b, 8 - s, 0), sub < 8 - s
        else:
            a_s, b_s, m = pltpu.roll(a, s, 0), pltpu.roll(b, s, 0), sub >= s
        b = jnp.where(m, a * b_s + b, b)
        a = jnp.where(m, a * a_s, a)
    return a, b


def _last_row(x, sub, row):
    return jnp.broadcast_to(jnp.sum(jnp.where(sub == row, x, 0.0), axis=0, keepdims=True), x.shape)


def _rec_prologue(up_ref, cw_ref, cb_ref, wa_ref, wi_ref, ba_ref, bi_ref, lam_ref,
                  upad_ref, u_ref, a_refs, h_refs):
    cb = up_ref.shape[1]
    zeros = jnp.zeros((PAD, cb), F32)
    upad_ref[pl.ds(0, PAD), :] = zeros
    upad_ref[pl.ds(PAD + T, PAD), :] = zeros
    upad_ref[pl.ds(PAD, T), :] = up_ref[...]
    cw = cw_ref[...]
    sp = _softplus_neg(lam_ref[...])
    for c in range(N_CHUNK):
        t0 = c * REC_CHUNK
        u = cb_ref[...] + _conv_taps(upad_ref, t0, cw, 1)
        u_ref[pl.ds(t0, REC_CHUNK), :] = u
        for d in range(2):
            _, i, a, mult = _gates(u, wa_ref[d], wi_ref[d], ba_ref[d:d + 1, :], bi_ref[d:d + 1, :], sp[d:d + 1, :])
            a_refs[d][pl.ds(t0, REC_CHUNK), :] = a
            h_refs[d][pl.ds(t0, REC_CHUNK), :] = mult * (i * u)

    sub = lax.broadcasted_iota(jnp.int32, (8, cb), 0)

    def tile(k, carry):
        cf, cr = carry
        tf = pl.multiple_of(k * 8, 8)
        tr = pl.multiple_of((N_TILE - 1 - k) * 8, 8)
        af, bf = _tile_scan(a_refs[0][pl.ds(tf, 8), :], h_refs[0][pl.ds(tf, 8), :], sub, False)
        hf = af * cf + bf
        h_refs[0][pl.ds(tf, 8), :] = hf
        ar, br = _tile_scan(a_refs[1][pl.ds(tr, 8), :], h_refs[1][pl.ds(tr, 8), :], sub, True)
        hr = ar * cr + br
        h_refs[1][pl.ds(tr, 8), :] = hr
        return _last_row(hf, sub, 7), _last_row(hr, sub, 0)

    z8 = jnp.zeros((8, cb), F32)
    lax.fori_loop(0, N_TILE, tile, (z8, z8))
    return sp


def _rec_specs():
    up = pl.BlockSpec((T, REC_CB), lambda c: (0, _U_BLK + c))
    yb = pl.BlockSpec((T, REC_CB), lambda c: (0, _Y_BLK + c))
    cw = pl.BlockSpec((4, REC_CB), lambda c: (0, c))
    cbias = pl.BlockSpec((1, REC_CB), lambda c: (0, c))
    wbd = pl.BlockSpec((2, None, REC_CB, REC_CB), lambda c: (0, c, 0, 0))
    vec2 = pl.BlockSpec((2, REC_CB), lambda c: (0, c))
    col = pl.BlockSpec((T, REC_CB), lambda c: (0, c))
    return up, yb, cw, cbias, wbd, vec2, col


def _rec_fwd(z, conv_w, conv_b, wa, wi, ba, bi, lam):
    up, yb, cw, cbias, wbd, vec2, col = _rec_specs()

    def body(up_ref, yb_ref, cw_ref, cb_ref, wa_ref, wi_ref, ba_ref, bi_ref, lam_ref, g_ref,
             upad_ref, u_ref, af_ref, ar_ref, hf_ref, hr_ref):
        _rec_prologue(up_ref, cw_ref, cb_ref, wa_ref, wi_ref, ba_ref, bi_ref, lam_ref,
                      upad_ref, u_ref, (af_ref, ar_ref), (hf_ref, hr_ref))

        def chunk(c, carry):
            t0 = pl.multiple_of(c * REC_CHUNK, REC_CHUNK)
            rows = pl.ds(t0, REC_CHUNK)
            g_ref[rows, :] = ((hf_ref[rows, :] + hr_ref[rows, :]) * _gelu(yb_ref[rows, :])).astype(BF16)
            return carry

        lax.fori_loop(0, N_CHUNK, chunk, 0)

    full = pltpu.VMEM((T, REC_CB), F32)
    return pl.pallas_call(
        body, name="rec_fwd", grid=(N_CB,),
        in_specs=[up, yb, cw, cbias, wbd, wbd, vec2, vec2, vec2], out_specs=col,
        out_shape=_sds((T, D), BF16),
        scratch_shapes=[pltpu.VMEM((T + 2 * PAD, REC_CB), F32), full, full, full, full, full],
        compiler_params=_params(("parallel",)))(z, z, conv_w, conv_b, wa, wi, ba, bi, lam)


def _rec_bwd(z, dg, conv_w, conv_b, wa, wi, ba, bi, lam):
    up, yb, cw, cbias, wbd, vec2, col = _rec_specs()

    def body(up_ref, yb_ref, dg_ref, cw_ref, cb_ref, wa_ref, wi_ref, ba_ref, bi_ref, lam_ref,
             dup_ref, dyb_ref, dcw_ref, dcb_ref, dwa_ref, dwi_ref, dba_ref, dbi_ref, dlam_ref,
             upad_ref, u_ref, af_ref, ar_ref, hf_ref, hr_ref, dh_ref, gf_ref, gr_ref, daf_ref, dar_ref, dupad_ref):
        a_refs, h_refs = (af_ref, ar_ref), (hf_ref, hr_ref)
        g_refs, da_refs = (gf_ref, gr_ref), (daf_ref, dar_ref)
        sp = _rec_prologue(up_ref, cw_ref, cb_ref, wa_ref, wi_ref, ba_ref, bi_ref, lam_ref,
                           upad_ref, u_ref, a_refs, h_refs)
        cb = up_ref.shape[1]

        def gate_chunk(c, carry):
            t0 = pl.multiple_of(c * REC_CHUNK, REC_CHUNK)
            rows = pl.ds(t0, REC_CHUNK)
            y = yb_ref[rows, :]
            dgv = dg_ref[rows, :].astype(F32)
            dh_ref[rows, :] = dgv * _gelu(y)
            dyb_ref[rows, :] = (dgv * (hf_ref[rows, :] + hr_ref[rows, :]) * _gelu_grad(y)).astype(BF16)
            return carry

        lax.fori_loop(0, N_CHUNK, gate_chunk, 0)

        sub = lax.broadcasted_iota(jnp.int32, (8, cb), 0)

        def tile(k, carry):
            cf, cr = carry
            kf = N_TILE - 1 - k
            tf = pl.multiple_of(kf * 8, 8)
            tnext = pl.multiple_of(jnp.minimum(kf + 1, N_TILE - 1) * 8, 8)
            tprev = pl.multiple_of(jnp.maximum(kf - 1, 0) * 8, 8)
            a_t = af_ref[pl.ds(tf, 8), :]
            a_n = jnp.where(kf < N_TILE - 1, af_ref[pl.ds(tnext, 8), :], 0.0)
            a_sh = jnp.where(sub == 7, pltpu.roll(a_n, 7, 0), pltpu.roll(a_t, 7, 0))
            ca, cbb = _tile_scan(a_sh, dh_ref[pl.ds(tf, 8), :], sub, True)
            gf = ca * cf + cbb
            h_t = hf_ref[pl.ds(tf, 8), :]
            h_p = jnp.where(kf > 0, hf_ref[pl.ds(tprev, 8), :], 0.0)
            h_sh = jnp.where(sub == 0, pltpu.roll(h_p, 1, 0), pltpu.roll(h_t, 1, 0))
            gf_ref[pl.ds(tf, 8), :] = gf
            daf_ref[pl.ds(tf, 8), :] = gf * h_sh
            tr = pl.multiple_of(k * 8, 8)
            rnext = pl.multiple_of(jnp.minimum(k + 1, N_TILE - 1) * 8, 8)
            rprev = pl.multiple_of(jnp.maximum(k - 1, 0) * 8, 8)
            b_t = ar_ref[pl.ds(tr, 8), :]
            b_p = jnp.where(k > 0, ar_ref[pl.ds(rprev, 8), :], 0.0)
            b_sh = jnp.where(sub == 0, pltpu.roll(b_p, 1, 0), pltpu.roll(b_t, 1, 0))
            ra, rb = _tile_scan(b_sh, dh_ref[pl.ds(tr, 8), :], sub, False)
            gr = ra * cr + rb
            hr_t = hr_ref[pl.ds(tr, 8), :]
            hr_n = jnp.where(k < N_TILE - 1, hr_ref[pl.ds(rnext, 8), :], 0.0)
            hr_sh = jnp.where(sub == 7, pltpu.roll(hr_n, 7, 0), pltpu.roll(hr_t, 7, 0))
            gr_ref[pl.ds(tr, 8), :] = gr
            dar_ref[pl.ds(tr, 8), :] = gr * hr_sh
            return _last_row(gf, sub, 0), _last_row(gr, sub, 7)

        z8 = jnp.zeros((8, cb), F32)
        lax.fori_loop(0, N_TILE, tile, (z8, z8))

        zeros = jnp.zeros((PAD, cb), F32)
        dupad_ref[pl.ds(0, PAD), :] = zeros
        dupad_ref[pl.ds(PAD + T, PAD), :] = zeros
        dwa_ref[...] = jnp.zeros_like(dwa_ref)
        dwi_ref[...] = jnp.zeros_like(dwi_ref)
        dba_ref[...] = jnp.zeros_like(dba_ref)
        dbi_ref[...] = jnp.zeros_like(dbi_ref)
        dlam_ref[...] = jnp.zeros_like(dlam_ref)

        def grad_chunk(c, carry):
            t0 = pl.multiple_of(c * REC_CHUNK, REC_CHUNK)
            rows = pl.ds(t0, REC_CHUNK)
            u = u_ref[rows, :]
            ub = u.astype(BF16)
            du = jnp.zeros((REC_CHUNK, cb), F32)
            for d in range(2):
                r, i, a, mult = _gates(u, wa_ref[d], wi_ref[d], ba_ref[d:d + 1, :], bi_ref[d:d + 1, :], sp[d:d + 1, :])
                dbx = g_refs[d][rows, :]
                dmult = dbx * (i * u)
                diu = dbx * mult
                a2 = a * a
                dlog = da_refs[d][rows, :] * a - dmult * jnp.where(mult > 0.0, a2 / mult, 0.0)
                dpa = (dlog * (-LRU_C) * sp[d:d + 1, :]) * r * (1.0 - r)
                dpi = (diu * u) * i * (1.0 - i)
                dpab, dpib = dpa.astype(BF16), dpi.astype(BF16)
                du = du + diu * i + _dot(dpab, wa_ref[d], NT) + _dot(dpib, wi_ref[d], NT)
                dwa_ref[d] += _dot(ub, dpab, TN)
                dwi_ref[d] += _dot(ub, dpib, TN)
                dba_ref[d:d + 1, :] += jnp.sum(dpa, axis=0, keepdims=True)
                dbi_ref[d:d + 1, :] += jnp.sum(dpi, axis=0, keepdims=True)
                dlam_ref[d:d + 1, :] += jnp.sum(dlog * r, axis=0, keepdims=True)
            dupad_ref[pl.ds(PAD + t0, REC_CHUNK), :] = du
            return carry

        lax.fori_loop(0, N_CHUNK, grad_chunk, 0)
        dlam_ref[...] = dlam_ref[...] * (LRU_C * _sigmoid(-lam_ref[...]))

        cw = cw_ref[...]
        dcb = jnp.zeros((1, cb), F32)
        dcw = [jnp.zeros((1, cb), F32) for _ in range(4)]
        for c in range(N_CHUNK):
            t0 = c * REC_CHUNK
            du = dupad_ref[pl.ds(PAD + t0, REC_CHUNK), :]
            dcb = dcb + jnp.sum(du, axis=0, keepdims=True)
            for j in range(4):
                dcw[j] = dcw[j] + jnp.sum(du * upad_ref[pl.ds(PAD + t0 + j - 2, REC_CHUNK), :], axis=0, keepdims=True)
            dup_ref[pl.ds(t0, REC_CHUNK), :] = _conv_taps(dupad_ref, t0, cw, -1).astype(BF16)
        dcb_ref[...] = dcb
        dcw_ref[...] = jnp.concatenate(dcw, axis=0)

    full = pltpu.VMEM((T, REC_CB), F32)
    padded = pltpu.VMEM((T + 2 * PAD, REC_CB), F32)
    return pl.pallas_call(
        body, name="rec_bwd", grid=(N_CB,),
        in_specs=[up, yb, col, cw, cbias, wbd, wbd, vec2, vec2, vec2],
        out_specs=[col, col, cw, cbias, wbd, wbd, vec2, vec2, vec2],
        out_shape=[_sds((T, D), BF16), _sds((T, D), BF16), _sds((4, D), F32), _sds((1, D), F32),
                   _sds((2, N_CB, REC_CB, REC_CB), F32), _sds((2, N_CB, REC_CB, REC_CB), F32),
                   _sds((2, D), F32), _sds((2, D), F32), _sds((2, D), F32)],
        scratch_shapes=[padded, full, full, full, full, full, full, full, full, full, full, padded],
        compiler_params=_params(("parallel",)))(z, z, dg, conv_w, conv_b, wa, wi, ba, bi, lam)


def _local_step(x, target, p):
    x = x.reshape(T, D)
    target = target.reshape(T, D)
    tb = _bias_pairs(p["rpb"])
    wa, wi = _block_diag(p["w_rg_a"]), _block_diag(p["w_rg_i"])
    rec_params = (p["conv_w"], p["conv_b"], wa, wi, p["b_rg_a"], p["b_rg_i"], p["lru_lambda"])

    h1 = _rms_fwd("rms1_fwd", x, p["ln1_g"])
    (z,) = _mm_nn_cols("mm_z", h1, p["w_in"], F32, bias=p["b_in"])
    att = _attn_fwd(z, tb)
    g = _rec_fwd(z, *rec_params)
    (y_att,) = _mm_nn_cols("mm_y_att", att, p["w_att_o"], F32)
    (y_rec,) = _mm_nn_rows("mm_y_rec", g, p["w_rec_o"], F32, tk=D)
    mixed = _merge_fwd(y_att, y_rec, z)

    def add_res(r, ex, outs):
        outs[0][...] = ex[0][...] + r

    res_spec = pl.BlockSpec((TM, D), lambda i, j, k: (i, 0))
    (x1,) = _mm_nn_rows("mm_x1", mixed, p["w_out"], F32, tk=D, extras=[x], extra_specs=[res_spec], epilogue=add_res)
    h2 = _rms_fwd("rms2_fwd", x1, p["ln2_g"])

    def relu2(r, ex, outs):
        outs[0][...] = r
        rp = jnp.maximum(r, 0.0)
        outs[1][...] = (rp * rp).astype(BF16)

    ff_blk = pl.BlockSpec((TM, D), lambda j, i, k: (i, j))
    f, s = _mm_nn_cols("mm_ff1", h2, p["w_ff1"], F32, epilogue=relu2,
                       out_shapes=[_sds((T, D_FF), F32), _sds((T, D_FF), BF16)], out_specs=[ff_blk, ff_blk])
    (x2,) = _mm_nn_rows("mm_x2", s, p["w_ff2"], F32, tk=D, extras=[x1], extra_specs=[res_spec], epilogue=add_res)
    loss, dx2, g_lnf = _loss_head(x2, target, p["lnf_g"])

    def relu2_bwd(r, ex, outs):
        outs[0][...] = (r * 2.0 * jnp.maximum(ex[0][...], 0.0)).astype(BF16)

    (df,) = _mm_nt_rows("mm_df", dx2, p["w_ff2"], BF16, tn=D, extras=[f],
                        extra_specs=[pl.BlockSpec((TM, D), lambda j, i, k: (i, j))], epilogue=relu2_bwd)
    (g_w_ff2,) = _mm_tn_rows("mm_g_ff2", s, dx2, tm=D)
    (g_w_ff1,) = _mm_tn_cols("mm_g_ff1", h2, df, D)
    (dh2,) = _mm_nt_cols("mm_dh2", df, p["w_ff1"], F32)
    dx1, g_ln2 = _rms_bwd("rms2_bwd", dh2, x1, p["ln2_g"], dx2)

    (dmixed,) = _mm_nt_rows("mm_dmixed", dx1, p["w_out"], F32, tn=D)
    (g_w_out,) = _mm_tn_rows("mm_g_out", mixed, dx1, tm=D)
    dy_att, dy_rec, dg_att, dg_rec = _merge_bwd(dmixed, y_att, y_rec, z)
    (d_att,) = _mm_nt_cols("mm_d_att", dy_att, p["w_att_o"], BF16)
    (g_w_att_o,) = _mm_tn_cols("mm_g_att_o", att, dy_att, D // N_CHIPS)
    (d_g,) = _mm_nt_rows("mm_d_g", dy_rec, p["w_rec_o"], BF16, tn=D)
    (g_w_rec_o,) = _mm_tn_rows("mm_g_rec_o", g, dy_rec, tm=D)

    dq, dk, dv, ds_acc = _attn_bwd(z, tb, d_att)
    g_rpb = _rpb_grad(ds_acc)
    d_up, d_yb, g_conv_w, g_conv_b, g_wa, g_wi, g_ba, g_bi, g_lam = _rec_bwd(z, d_g, *rec_params)
    dz = jnp.concatenate([dq, dk, dv, d_up, d_yb, dg_att, dg_rec], axis=1)

    g_w_in, g_b_in = _mm_tn_cols("mm_g_in", h1, dz, D_IN // N_CHIPS, colsum=True)
    (dh1,) = _mm_nt_cols("mm_dh1", dz, p["w_in"], F32)
    grad_x, g_ln1 = _rms_bwd("rms1_bwd", dh1, x, p["ln1_g"], dx1)

    grads = dict(ln1_g=g_ln1, w_in=g_w_in, b_in=g_b_in, rpb=g_rpb, w_att_o=g_w_att_o, conv_w=g_conv_w,
                 conv_b=g_conv_b, w_rg_a=_block_diag_grad(g_wa), b_rg_a=g_ba, w_rg_i=_block_diag_grad(g_wi),
                 b_rg_i=g_bi, lru_lambda=g_lam, w_rec_o=g_w_rec_o, w_out=g_w_out, ln2_g=g_ln2,
                 w_ff1=g_w_ff1, w_ff2=g_w_ff2, lnf_g=g_lnf)
    return loss, grad_x.reshape(1, T, D), grads


_ANY = pl.BlockSpec(memory_space=pl.ANY)
N_PEERS = N_CHIPS - 1


def _place():
    x, y, c = lax.axis_index("x"), lax.axis_index("y"), lax.axis_index("c")
    peers = [(1 - x, y), (x, 1 - y), (1 - x, 1 - y)]
    return x, y, c, 2 * x + y, peers


def _remote(src, dst, send_sem, recv_sem, dev):
    return pltpu.make_async_remote_copy(src_ref=src, dst_ref=dst, send_sem=send_sem, recv_sem=recv_sem,
                                        device_id=dev, device_id_type=MESH)


def _prefetch_call(body, name, ids, grid, in_specs, out_specs, out_shape, args):
    spec = pltpu.PrefetchScalarGridSpec(num_scalar_prefetch=1, grid=grid, in_specs=in_specs, out_specs=out_specs)
    return pl.pallas_call(body, name=name, grid_spec=spec, out_shape=out_shape,
                          compiler_params=_params(("parallel",) * len(grid)))(ids, *args)


def _cast_bf16(name, w, chip_id):
    rows, cols = w.shape
    rb = min(rows, 256)

    def body(ids_ref, w_ref, o_ref):
        o_ref[...] = w_ref[...].astype(BF16)

    return _prefetch_call(body, name, chip_id, (rows // rb,),
                          [pl.BlockSpec((rb, cols), lambda i, ids: (i, 0))],
                          pl.BlockSpec((None, rb, cols), lambda i, ids: (ids[0], i, 0)),
                          _sds((N_CHIPS, rows, cols), BF16), (w,))


def _dma_sems(*counts):
    return [pltpu.SemaphoreType.DMA((k,)) for k in counts]


def _all_gather(slots):
    n = len(slots)

    def body(*refs):
        bufs = refs[n:2 * n]
        ici_send, ici_recv, d2d_send, d2d_recv = refs[2 * n:]
        x, y, c, chip, peers = _place()
        sibling = (x, y, 1 - c)

        def halves(t):
            half = bufs[t].shape[1] // 2
            return pl.ds(c * half, half), pl.ds((1 - c) * half, half)

        sends = []
        for t in range(n):
            mine, _ = halves(t)
            for r, (px, py) in enumerate(peers):
                k = t * N_PEERS + r
                own = bufs[t].at[chip, mine]
                sends.append(_remote(own, own, ici_send.at[k], ici_recv.at[k], (px, py, c)))
                sends[-1].start()
        for t in range(n):
            mine, _ = halves(t)
            for r, (px, py) in enumerate(peers):
                k = t * N_PEERS + r
                landed = bufs[t].at[2 * px + py, mine]
                _remote(landed, landed, ici_send.at[k], ici_recv.at[k], (px, py, c)).wait_recv()
                sends.append(_remote(landed, landed, d2d_send.at[k], d2d_recv.at[k], sibling))
                sends[-1].start()
        for t in range(n):
            _, theirs = halves(t)
            for r, (px, py) in enumerate(peers):
                k = t * N_PEERS + r
                landed = bufs[t].at[2 * px + py, theirs]
                _remote(landed, landed, d2d_send.at[k], d2d_recv.at[k], sibling).wait_recv()
        for cp in sends:
            cp.wait_send()

    nk = n * N_PEERS
    return pl.pallas_call(
        body, name="all_gather_weights", in_specs=[_ANY] * n, out_specs=[_ANY] * n,
        out_shape=[_sds(s.shape, s.dtype) for s in slots], input_output_aliases={t: t for t in range(n)},
        scratch_shapes=_dma_sems(nk, nk, nk, nk))(*slots)


def _pair_exchange(grads, small):
    n = len(grads)

    def body(*refs):
        ins, small_in = refs[:n], refs[n]
        got, small_got = refs[n + 1:2 * n + 1], refs[2 * n + 1]
        send_sem, recv_sem = refs[2 * n + 2:]
        x, y, c, _, _ = _place()
        sibling = (x, y, 1 - c)
        copies = []
        for t in range(n):
            half = ins[t].shape[1] // 2
            for j in range(N_CHIPS):
                k = t * N_CHIPS + j
                copies.append(_remote(ins[t].at[j, pl.ds((1 - c) * half, half)], got[t].at[j],
                                      send_sem.at[k], recv_sem.at[k], sibling))
        copies.append(_remote(small_in, small_got, send_sem.at[n * N_CHIPS], recv_sem.at[n * N_CHIPS], sibling))
        for cp in copies:
            cp.start()
        for cp in copies:
            cp.wait_recv()
        for cp in copies:
            cp.wait_send()

    ns = n * N_CHIPS + 1
    res = pl.pallas_call(
        body, name="grad_pair_exchange", in_specs=[_ANY] * (n + 1), out_specs=[_ANY] * (n + 1),
        out_shape=[_sds((N_CHIPS, g.shape[1] // 2, g.shape[2]), F32) for g in grads] + [_sds(small.shape, F32)],
        scratch_shapes=_dma_sems(ns, ns))(*grads, small)
    return res[:n], res[n]


def _chip_exchange(sums_bf16, small):
    n = len(sums_bf16)

    def body(*refs):
        bf_in, small_in = refs[:n], refs[n]
        got, small_got = refs[n + 1:2 * n + 1], refs[2 * n + 1]
        send_sem, recv_sem = refs[2 * n + 2:]
        x, y, c, chip, peers = _place()
        half_s = small_in.shape[0] // 2
        copies = []
        for r, (px, py) in enumerate(peers):
            for t in range(n):
                k = t * N_PEERS + r
                copies.append(_remote(bf_in[t].at[2 * px + py], got[t].at[r], send_sem.at[k], recv_sem.at[k], (px, py, c)))
            k = n * N_PEERS + r
            copies.append(_remote(small_in.at[pl.ds(c * half_s, half_s)], small_got.at[r],
                                  send_sem.at[k], recv_sem.at[k], (px, py, c)))
        for cp in copies:
            cp.start()
        for cp in copies:
            cp.wait_recv()
        for cp in copies:
            cp.wait_send()

    ns = (n + 1) * N_PEERS
    res = pl.pallas_call(
        body, name="grad_chip_exchange", in_specs=[_ANY] * (n + 1), out_specs=[_ANY] * (n + 1),
        out_shape=([_sds((N_PEERS,) + s.shape[1:], BF16) for s in sums_bf16]
                   + [_sds((N_PEERS, small.shape[0] // 2, 128), F32)]),
        scratch_shapes=_dma_sems(ns, ns))(*sums_bf16, small)
    return res[:n], res[n]


def _half_swap(bufs):
    n = len(bufs)

    def body(*refs):
        outs = refs[n:2 * n]
        send_sem, recv_sem = refs[2 * n:]
        x, y, c, _, _ = _place()
        sibling = (x, y, 1 - c)
        copies = []
        for t in range(n):
            h = outs[t].shape[0] // 2
            mine = outs[t].at[pl.ds(c * h, h)]
            copies.append(_remote(mine, mine, send_sem.at[t], recv_sem.at[t], sibling))
            copies[-1].start()
        for t in range(n):
            h = outs[t].shape[0] // 2
            theirs = outs[t].at[pl.ds((1 - c) * h, h)]
            _remote(theirs, theirs, send_sem.at[t], recv_sem.at[t], sibling).wait_recv()
        for cp in copies:
            cp.wait_send()

    return pl.pallas_call(
        body, name="grad_half_swap", in_specs=[_ANY] * n, out_specs=[_ANY] * n,
        out_shape=[_sds(b.shape, b.dtype) for b in bufs], input_output_aliases={t: t for t in range(n)},
        scratch_shapes=_dma_sems(n, n))(*bufs)


def _pair_sum(name, grad, got, ids):
    _, rows, cols = got.shape
    rb = min(rows, 256)
    nb = rows // rb
    blk = pl.BlockSpec((None, rb, cols), lambda j, i, ids: (j, i, 0))
    mine = pl.BlockSpec((None, rb, cols), lambda j, i, ids: (j, ids[1] * nb + i, 0))

    def body(ids_ref, a_ref, b_ref, s_ref, sb_ref):
        s = a_ref[...] + b_ref[...]
        s_ref[...] = s
        sb_ref[...] = s.astype(BF16)

    return _prefetch_call(body, name, ids, (N_CHIPS, nb), [mine, blk], [blk, blk],
                          [_sds(got.shape, F32), _sds(got.shape, BF16)], (grad, got))


def _chip_sum(name, sums, got, ids):
    _, rows, cols = sums.shape
    rb = min(rows, 256)
    nb = rows // rb
    own = pl.BlockSpec((None, rb, cols), lambda i, ids: (ids[0], i, 0))
    blk3 = pl.BlockSpec((N_PEERS, rb, cols), lambda i, ids: (0, i, 0))
    out = pl.BlockSpec((rb, cols), lambda i, ids: (ids[1] * nb + i, 0))

    def body(ids_ref, a_ref, b_ref, o_ref):
        o_ref[...] = ((a_ref[...] + b_ref[0].astype(F32)) + b_ref[1].astype(F32)) + b_ref[2].astype(F32)

    return _prefetch_call(body, name, ids, (nb,), [own, blk3], out, _sds((2 * rows, cols), F32), (sums, got))


SMALL_RB = 280


def _small_pair_sum(own, got):
    blk = pl.BlockSpec((SMALL_RB, 128), lambda i: (i, 0))

    def body(a_ref, b_ref, o_ref):
        o_ref[...] = a_ref[...] + b_ref[...]

    return pl.pallas_call(body, name="small_pair_sum", grid=(own.shape[0] // SMALL_RB,), in_specs=[blk, blk],
                          out_specs=blk, out_shape=_sds(own.shape, F32),
                          compiler_params=_params(("parallel",)))(own, got)


def _small_chip_sum(pair, got, ids):
    nb = pair.shape[0] // 2 // SMALL_RB
    half = pl.BlockSpec((SMALL_RB, 128), lambda i, ids: (ids[1] * nb + i, 0))
    blk3 = pl.BlockSpec((N_PEERS, SMALL_RB, 128), lambda i, ids: (0, i, 0))

    def body(ids_ref, a_ref, b_ref, o_ref):
        o_ref[...] = (a_ref[...] + b_ref[1]) + (b_ref[0] + b_ref[2])

    return _prefetch_call(body, "small_chip_sum", ids, (nb,), [half, blk3], half, _sds(pair.shape, F32), (pair, got))


def _adamw_math(w, g, m, v):
    m = ADAM_B1 * m + (1.0 - ADAM_B1) * g
    v = ADAM_B2 * v + (1.0 - ADAM_B2) * (g * g)
    m_hat = m / (1.0 - ADAM_B1 ** ADAM_STEP)
    v_hat = v / (1.0 - ADAM_B2 ** ADAM_STEP)
    delta = -ADAM_LR * (m_hat / (jnp.sqrt(v_hat) + ADAM_EPS) + ADAM_WD * w)
    return delta, m, v


def _adamw(name, w, g, m, v, rb=None):
    rows, cols = w.shape
    rb = rows if rb is None else rb
    blk = pl.BlockSpec((rb, cols), lambda i: (i, 0))

    def body(w_ref, g_ref, m_ref, v_ref, d_ref, nm_ref, nv_ref):
        d, nm, nv = _adamw_math(w_ref[...], g_ref[...], m_ref[...], v_ref[...])
        d_ref[...] = d
        nm_ref[...] = nm
        nv_ref[...] = nv

    return pl.pallas_call(body, name=name, grid=(rows // rb,), in_specs=[blk] * 4, out_specs=[blk] * 3,
                          out_shape=[_sds(w.shape, F32)] * 3, compiler_params=_params(("parallel",)))(w, g, m, v)


BIG = ("w_in", "w_att_o", "w_rec_o", "w_out", "w_ff1", "w_ff2")
COL_SHARDED = ("w_in", "w_att_o", "w_ff1")
SHARDED_VECS = ("conv_w", "b_rg_a", "b_rg_i", "lru_lambda")
SMALL = ("ln1_g", "b_in", "rpb", "conv_w", "conv_b", "w_rg_a", "b_rg_a", "w_rg_i", "b_rg_i", "lru_lambda",
         "ln2_g", "lnf_g")
SMALL_ROWS = 2240
ORDER = ("ln1_g", "w_in", "b_in", "rpb", "w_att_o", "conv_w", "conv_b", "w_rg_a", "b_rg_a", "w_rg_i", "b_rg_i",
         "lru_lambda", "w_rec_o", "w_out", "ln2_g", "w_ff1", "w_ff2", "lnf_g")


def _pack_small(grads, loss):
    parts, sizes = [], {}
    for n in SMALL:
        flat = grads[n].reshape(-1)
        pad = (-flat.shape[0]) % 128
        sizes[n] = (flat.shape[0], flat.shape[0] + pad)
        parts.append(jnp.pad(flat, (0, pad)))
    total = sum(s[1] for s in sizes.values())
    parts.append(jnp.pad(loss.reshape(1), (0, SMALL_ROWS * 128 - total - 1)))
    return jnp.concatenate(parts).reshape(SMALL_ROWS, 128), sizes


def _unpack_small(buf, sizes, shapes):
    flat = buf.reshape(-1)
    out, pos = {}, 0
    for n in SMALL:
        size, padded = sizes[n]
        out[n] = flat[pos:pos + size].reshape(shapes[n])
        pos += padded
    return out, flat[pos]


def _gather_weights(w, chip):
    chip_id = chip.astype(jnp.int32).reshape(1)
    slots = [_cast_bf16("cast_" + n, w[n][0], chip_id) for n in BIG]
    vec_rows = [w[n][0] for n in SHARDED_VECS]
    vec_shard = jnp.concatenate(vec_rows + [jnp.zeros((16 - 10, D // N_CHIPS), F32)], axis=0)
    vec_slots = lax.dynamic_update_slice(jnp.zeros((N_CHIPS, 16, D // N_CHIPS), F32), vec_shard[None], (chip, 0, 0))
    gathered = _all_gather(slots + [vec_slots])
    full = dict(zip(BIG, gathered[:len(BIG)]))
    vecs = gathered[len(BIG)].transpose(1, 0, 2).reshape(16, D)
    return dict(
        w_in=full["w_in"], w_att_o=full["w_att_o"], w_ff1=full["w_ff1"],
        w_rec_o=full["w_rec_o"].reshape(D, D), w_out=full["w_out"].reshape(D, D), w_ff2=full["w_ff2"].reshape(D_FF, D),
        ln1_g=w["ln1_g"], b_in=w["b_in"], rpb=w["rpb"][0], conv_w=vecs[0:4], conv_b=w["conv_b"], w_rg_a=w["w_rg_a"][0],
        b_rg_a=vecs[4:6], w_rg_i=w["w_rg_i"][0], b_rg_i=vecs[6:8], lru_lambda=vecs[8:10], ln2_g=w["ln2_g"],
        lnf_g=w["lnf_g"].reshape(1, D))


def _reduce_grads(big, small, ids):
    got1, small_got1 = _pair_exchange(big, small)
    sums = [_pair_sum("pair_sum_" + n, a, b, ids) for n, a, b in zip(BIG, big, got1)]
    small_sum = _small_pair_sum(small, small_got1)
    got2, small_got2 = _chip_exchange([s[1] for s in sums], small_sum)
    halves = [_chip_sum("chip_sum_" + n, s[0], b, ids) for n, s, b in zip(BIG, sums, got2)]
    halves.append(_small_chip_sum(small_sum, small_got2, ids))
    return _half_swap(halves)


def kernel(x, ln1_g, w_in, b_in, rpb, w_att_o, conv_w, conv_b, w_rg_a, b_rg_a, w_rg_i, b_rg_i, lru_lambda, w_rec_o, w_out, ln2_g, w_ff1, w_ff2, lnf_g, loss_target, m_ln1_g, m_w_in, m_b_in, m_rpb, m_w_att_o, m_conv_w, m_conv_b, m_w_rg_a, m_b_rg_a, m_w_rg_i, m_b_rg_i, m_lru_lambda, m_w_rec_o, m_w_out, m_ln2_g, m_w_ff1, m_w_ff2, m_lnf_g, v_ln1_g, v_w_in, v_b_in, v_rpb, v_w_att_o, v_conv_w, v_conv_b, v_w_rg_a, v_b_rg_a, v_w_rg_i, v_b_rg_i, v_lru_lambda, v_w_rec_o, v_w_out, v_ln2_g, v_w_ff1, v_w_ff2, v_lnf_g):
    w = dict(ln1_g=ln1_g, w_in=w_in, b_in=b_in, rpb=rpb, w_att_o=w_att_o, conv_w=conv_w, conv_b=conv_b,
             w_rg_a=w_rg_a, b_rg_a=b_rg_a, w_rg_i=w_rg_i, b_rg_i=b_rg_i, lru_lambda=lru_lambda, w_rec_o=w_rec_o,
             w_out=w_out, ln2_g=ln2_g, w_ff1=w_ff1, w_ff2=w_ff2, lnf_g=lnf_g)
    m = dict(ln1_g=m_ln1_g, w_in=m_w_in, b_in=m_b_in, rpb=m_rpb, w_att_o=m_w_att_o, conv_w=m_conv_w,
             conv_b=m_conv_b, w_rg_a=m_w_rg_a, b_rg_a=m_b_rg_a, w_rg_i=m_w_rg_i, b_rg_i=m_b_rg_i,
             lru_lambda=m_lru_lambda, w_rec_o=m_w_rec_o, w_out=m_w_out, ln2_g=m_ln2_g, w_ff1=m_w_ff1,
             w_ff2=m_w_ff2, lnf_g=m_lnf_g)
    v = dict(ln1_g=v_ln1_g, w_in=v_w_in, b_in=v_b_in, rpb=v_rpb, w_att_o=v_w_att_o, conv_w=v_conv_w,
             conv_b=v_conv_b, w_rg_a=v_w_rg_a, b_rg_a=v_b_rg_a, w_rg_i=v_w_rg_i, b_rg_i=v_b_rg_i,
             lru_lambda=v_lru_lambda, w_rec_o=v_w_rec_o, w_out=v_w_out, ln2_g=v_ln2_g, w_ff1=v_w_ff1,
             w_ff2=v_w_ff2, lnf_g=v_lnf_g)
    chip = 2 * lax.axis_index("x") + lax.axis_index("y")
    ids = jnp.stack([chip, lax.axis_index("c")]).astype(jnp.int32)

    p = _gather_weights(w, chip)
    loss, grad_x, g = _local_step(x, loss_target, p)

    big = [g["w_in"], g["w_att_o"], g["w_rec_o"].reshape(N_CHIPS, D // N_CHIPS, D),
           g["w_out"].reshape(N_CHIPS, D // N_CHIPS, D), g["w_ff1"], g["w_ff2"].reshape(N_CHIPS, D_FF // N_CHIPS, D)]
    small, sizes = _pack_small(g, loss)
    swapped = _reduce_grads(big, small, ids)
    shapes = {n: g[n].shape for n in SMALL}
    gsmall, loss = _unpack_small(swapped[len(BIG)], sizes, shapes)

    grad = dict(zip(BIG, swapped[:len(BIG)]))
    for n in SMALL:
        gn = gsmall[n]
        if n in SHARDED_VECS:
            gn = lax.dynamic_slice_in_dim(gn, chip * (D // N_CHIPS), D // N_CHIPS, axis=1)
        grad[n] = gn

    out_grad, out_delta, out_m, out_v = {}, {}, {}, {}
    for n in ORDER:
        shape = w[n].shape
        if n in BIG:
            two_d = grad[n].shape
            rb = 256
        else:
            two_d = (int(np.prod(shape[:-1])), shape[-1])
            rb = None
        gn = grad[n].reshape(two_d)
        d, nm, nv = _adamw("adamw_" + n, w[n].reshape(two_d), gn, m[n].reshape(two_d), v[n].reshape(two_d), rb)
        out_grad[n], out_delta[n], out_m[n], out_v[n] = (gn.reshape(shape), d.reshape(shape), nm.reshape(shape),
                                                         nv.reshape(shape))
    return (loss, grad_x, *[out_grad[n] for n in ORDER], *[out_delta[n] for n in ORDER],
            *[out_m[n] for n in ORDER], *[out_v[n] for n in ORDER])
```

```python
import functools

import numpy as np
import jax
import jax.numpy as jnp
from jax import lax
from jax.experimental import pallas as pl
from jax.experimental.pallas import tpu as pltpu

F32 = jnp.float32
BF16 = jnp.bfloat16

T = 2048
D = 1024
D_ATT = 512
D_IN = 5632
D_FF = 4096
N_HEADS = 8
HEAD_DIM = 64
GRID_W = 64
N_ROWS = T // GRID_W
WIN_H = 8
WIN_W = 16
KEYS = WIN_H * GRID_W
N_CHIPS = 4
EPS = 1e-6
LRU_C = 8.0
SCALE = HEAD_DIM ** -0.5
REC_CB = 256
REC_CHUNK = 256
PAD = 8

ADAM_LR = 0.001
ADAM_B1 = 0.9
ADAM_B2 = 0.999
ADAM_EPS = 1e-08
ADAM_WD = 0.01
ADAM_STEP = 10

VMEM_LIMIT = 56 * 1024 * 1024

NN = (((1,), (0,)), ((), ()))
NT = (((1,), (1,)), ((), ()))
TN = (((0,), (0,)), ((), ()))
MESH = pl.DeviceIdType.MESH


def _params(sem=None):
    return pltpu.CompilerParams(dimension_semantics=sem, vmem_limit_bytes=VMEM_LIMIT)


def _dot(a, b, dims):
    return lax.dot_general(a, b, dims, preferred_element_type=F32)


def _sigmoid(x):
    return 0.5 * jnp.tanh(0.5 * x) + 0.5


def _matmul(name, a, b, *, dims, grid, a_spec, b_spec, out_shapes, out_specs, acc_shape,
            extras=(), extra_specs=(), epilogue=None, colsum_spec=None, colsum_shape=None):
    nk = grid[2]
    n_extra = len(extras)
    n_out = len(out_shapes)
    with_colsum = colsum_spec is not None

    def body(a_ref, b_ref, *rest):
        ex = rest[:n_extra]
        outs = rest[n_extra:n_extra + n_out]
        pos = n_extra + n_out
        cs_out = rest[pos] if with_colsum else None
        pos += 1 if with_colsum else 0
        acc = rest[pos]
        cs_acc = rest[pos + 1] if with_colsum else None
        k = pl.program_id(2)

        @pl.when(k == 0)
        def _():
            acc[...] = jnp.zeros_like(acc)
            if with_colsum:
                cs_acc[...] = jnp.zeros_like(cs_acc)

        bv = b_ref[...]
        acc[...] += _dot(a_ref[...].astype(BF16), bv.astype(BF16), dims)
        if with_colsum:
            cs_acc[...] += jnp.sum(bv.astype(F32), axis=0, keepdims=True)

        @pl.when(k == nk - 1)
        def _():
            r = acc[...]
            if epilogue is None:
                outs[0][...] = r.astype(outs[0].dtype)
            else:
                epilogue(r, ex, outs)
            if with_colsum:
                cs_out[...] = cs_acc[...]

    shapes = list(out_shapes)
    specs = list(out_specs)
    scratch = [pltpu.VMEM(acc_shape, F32)]
    if with_colsum:
        shapes.append(colsum_shape)
        specs.append(colsum_spec)
        scratch.append(pltpu.VMEM((1, acc_shape[1]), F32))
    res = pl.pallas_call(
        body, name=name, grid=grid,
        in_specs=[a_spec, b_spec, *extra_specs],
        out_specs=specs, out_shape=shapes, scratch_shapes=scratch,
        compiler_params=_params(("parallel", "parallel", "arbitrary")),
    )(a, b, *extras)
    return res


def _sds(shape, dtype):
    return jax.ShapeDtypeStruct(shape, dtype)


TM = 512
NI = T // TM


def _mm_nn_cols(name, a, wg, out_dtype, *, bias=None, extras=(), extra_specs=(), epilogue=None,
                out_shapes=None, out_specs=None):
    k_dim, n4 = wg.shape[1], wg.shape[2]
    ex, exs = list(extras), list(extra_specs)
    if bias is not None:
        ex = [bias] + ex
        exs = [pl.BlockSpec((1, n4), lambda j, i, k: (0, j))] + exs
        user_ep = epilogue

        def epilogue(r, e, outs):
            r = r + e[0][...]
            if user_ep is None:
                outs[0][...] = r.astype(outs[0].dtype)
            else:
                user_ep(r, e[1:], outs)
    if out_shapes is None:
        out_shapes = [_sds((T, N_CHIPS * n4), out_dtype)]
        out_specs = [pl.BlockSpec((TM, n4), lambda j, i, k: (i, j))]
    return _matmul(
        name, a, wg, dims=NN, grid=(N_CHIPS, NI, 1),
        a_spec=pl.BlockSpec((TM, k_dim), lambda j, i, k: (i, 0)),
        b_spec=pl.BlockSpec((None, k_dim, n4), lambda j, i, k: (j, 0, 0)),
        out_shapes=out_shapes, out_specs=out_specs, acc_shape=(TM, n4),
        extras=ex, extra_specs=exs, epilogue=epilogue)


def _mm_nn_rows(name, a, w, out_dtype, *, tk, extras=(), extra_specs=(), epilogue=None):
    k_dim, n = w.shape
    return _matmul(
        name, a, w, dims=NN, grid=(NI, 1, k_dim // tk),
        a_spec=pl.BlockSpec((TM, tk), lambda i, j, k: (i, k)),
        b_spec=pl.BlockSpec((tk, n), lambda i, j, k: (k, 0)),
        out_shapes=[_sds((T, n), out_dtype)],
        out_specs=[pl.BlockSpec((TM, n), lambda i, j, k: (i, 0))], acc_shape=(TM, n),
        extras=extras, extra_specs=extra_specs, epilogue=epilogue)


def _mm_nt_cols(name, a, wg, out_dtype):
    k_dim, n4 = wg.shape[1], wg.shape[2]
    return _matmul(
        name, a, wg, dims=NT, grid=(NI, 1, N_CHIPS),
        a_spec=pl.BlockSpec((TM, n4), lambda i, j, k: (i, k)),
        b_spec=pl.BlockSpec((None, k_dim, n4), lambda i, j, k: (k, 0, 0)),
        out_shapes=[_sds((T, k_dim), out_dtype)],
        out_specs=[pl.BlockSpec((TM, k_dim), lambda i, j, k: (i, 0))], acc_shape=(TM, k_dim))


def _mm_nt_rows(name, a, w, out_dtype, *, tn, extras=(), extra_specs=(), epilogue=None):
    k_dim, n = w.shape
    return _matmul(
        name, a, w, dims=NT, grid=(k_dim // tn, NI, 1),
        a_spec=pl.BlockSpec((TM, n), lambda j, i, k: (i, 0)),
        b_spec=pl.BlockSpec((tn, n), lambda j, i, k: (j, 0)),
        out_shapes=[_sds((T, k_dim), out_dtype)],
        out_specs=[pl.BlockSpec((TM, tn), lambda j, i, k: (i, j))], acc_shape=(TM, tn),
        extras=extras, extra_specs=extra_specs, epilogue=epilogue)


def _mm_tn_cols(name, a, g, n4, *, colsum=False):
    k_dim = a.shape[1]
    kw = {}
    if colsum:
        kw = dict(colsum_spec=pl.BlockSpec((1, n4), lambda j, i, k: (0, j)),
                  colsum_shape=_sds((1, N_CHIPS * n4), F32))
    return _matmul(
        name, a, g, dims=TN, grid=(N_CHIPS, 1, NI),
        a_spec=pl.BlockSpec((TM, k_dim), lambda j, i, k: (k, 0)),
        b_spec=pl.BlockSpec((TM, n4), lambda j, i, k: (k, j)),
        out_shapes=[_sds((N_CHIPS, k_dim, n4), F32)],
        out_specs=[pl.BlockSpec((None, k_dim, n4), lambda j, i, k: (j, 0, 0))],
        acc_shape=(k_dim, n4), **kw)


def _mm_tn_rows(name, a, g, *, tm):
    k_dim, n = a.shape[1], g.shape[1]
    return _matmul(
        name, a, g, dims=TN, grid=(k_dim // tm, 1, NI),
        a_spec=pl.BlockSpec((TM, tm), lambda j, i, k: (k, j)),
        b_spec=pl.BlockSpec((TM, n), lambda j, i, k: (k, 0)),
        out_shapes=[_sds((k_dim, n), F32)],
        out_specs=[pl.BlockSpec((tm, n), lambda j, i, k: (j, 0))], acc_shape=(tm, n))


TE = 256
NE = T // TE
_ROW = pl.BlockSpec((TE, D), lambda i: (i, 0))
_VEC = pl.BlockSpec((1, D), lambda i: (0, 0))


def _rms_fwd(name, x, g, after=()):
    def body(x_ref, g_ref, *rest):
        h_ref = rest[-1]
        xv = x_ref[...]
        rstd = lax.rsqrt(jnp.mean(xv * xv, axis=-1, keepdims=True) + EPS)
        h_ref[...] = (xv * rstd * g_ref[...]).astype(BF16)

    return pl.pallas_call(body, name=name, grid=(NE,), in_specs=[_ROW, _VEC] + [_ANY] * len(after), out_specs=_ROW,
                          out_shape=_sds((T, D), BF16), compiler_params=_params(("parallel",)))(x, g, *after)


def _rms_bwd(name, dh, x, g, dres):
    def body(dh_ref, x_ref, g_ref, dres_ref, dx_ref, dg_ref):
        xv = x_ref[...]
        rstd = lax.rsqrt(jnp.mean(xv * xv, axis=-1, keepdims=True) + EPS)
        xhat = xv * rstd
        dhv = dh_ref[...]
        dy = dhv * g_ref[...]
        dx_ref[...] = dres_ref[...] + rstd * (dy - xhat * jnp.mean(dy * xhat, axis=-1, keepdims=True))

        @pl.when(pl.program_id(0) == 0)
        def _():
            dg_ref[...] = jnp.zeros_like(dg_ref)

        dg_ref[...] += jnp.sum(dhv * xhat, axis=0, keepdims=True)

    return pl.pallas_call(body, name=name, grid=(NE,), in_specs=[_ROW, _ROW, _VEC, _ROW],
                          out_specs=[_ROW, _VEC], out_shape=[_sds((T, D), F32), _sds((1, D), F32)],
                          compiler_params=_params(("arbitrary",)))(dh, x, g, dres)


def _loss_head(x2, target, g):
    def body(x_ref, t_ref, g_ref, loss_ref, dx_ref, dg_ref):
        xv = x_ref[...]
        rstd = lax.rsqrt(jnp.mean(xv * xv, axis=-1, keepdims=True) + EPS)
        xhat = xv * rstd
        gv = g_ref[...]
        err = xhat * gv - t_ref[...]
        dy = err * (1.0 / D)
        dxh = dy * gv
        dx_ref[...] = rstd * (dxh - xhat * jnp.mean(dxh * xhat, axis=-1, keepdims=True))

        @pl.when(pl.program_id(0) == 0)
        def _():
            dg_ref[...] = jnp.zeros_like(dg_ref)
            loss_ref[...] = jnp.zeros_like(loss_ref)

        dg_ref[...] += jnp.sum(dy * xhat, axis=0, keepdims=True)
        loss_ref[...] += (0.5 / D) * jnp.sum(jnp.sum(err * err, axis=1, keepdims=True), axis=0, keepdims=True)

    return pl.pallas_call(
        body, name="loss_head", grid=(NE,), in_specs=[_ROW, _ROW, _VEC],
        out_specs=[pl.BlockSpec((1, 1), lambda i: (0, 0)), _ROW, _VEC],
        out_shape=[_sds((1, 1), F32), _sds((T, D), F32), _sds((1, D), F32)],
        compiler_params=_params(("arbitrary",)))(x2, target, g)


MW = 512
_G_ATT_BLK = 3584 // MW
_G_REC_BLK = 4608 // MW


def _merge_specs():
    y = pl.BlockSpec((TM, MW), lambda i, j: (i, j))
    ga = pl.BlockSpec((TM, MW), lambda i, j: (i, _G_ATT_BLK + j))
    gr = pl.BlockSpec((TM, MW), lambda i, j: (i, _G_REC_BLK + j))
    return y, ga, gr


def _merge_fwd(y_att, y_rec, z):
    y, ga, gr = _merge_specs()

    def body(ya_ref, yr_ref, ga_ref, gr_ref, m_ref):
        m = _sigmoid(ga_ref[...]) * ya_ref[...] + _sigmoid(gr_ref[...]) * yr_ref[...]
        m_ref[...] = m.astype(BF16)

    return pl.pallas_call(body, name="merge_fwd", grid=(NI, D // MW), in_specs=[y, y, ga, gr], out_specs=y,
                          out_shape=_sds((T, D), BF16),
                          compiler_params=_params(("parallel", "parallel")))(y_att, y_rec, z, z)


def _merge_bwd(dm, y_att, y_rec, z):
    y, ga, gr = _merge_specs()

    def body(dm_ref, ya_ref, yr_ref, ga_ref, gr_ref, dya_ref, dyr_ref, dga_ref, dgr_ref):
        dmv = dm_ref[...]
        sa = _sigmoid(ga_ref[...])
        sr = _sigmoid(gr_ref[...])
        dya_ref[...] = (dmv * sa).astype(BF16)
        dyr_ref[...] = (dmv * sr).astype(BF16)
        dga_ref[...] = (dmv * ya_ref[...] * sa * (1.0 - sa)).astype(BF16)
        dgr_ref[...] = (dmv * yr_ref[...] * sr * (1.0 - sr)).astype(BF16)

    return pl.pallas_call(body, name="merge_bwd", grid=(NI, D // MW), in_specs=[y, y, y, ga, gr],
                          out_specs=[y, y, y, y], out_shape=[_sds((T, D), BF16)] * 4,
                          compiler_params=_params(("parallel", "parallel")))(dm, y_att, y_rec, z, z)


HP = 2 * HEAD_DIM
N_HP = N_HEADS // 2
ATT_UNROLL_FWD = 8
ATT_UNROLL_BWD = 4


def _window_maps():
    diag = np.zeros((GRID_W * GRID_W, 128), np.float32)
    for qc in range(GRID_W):
        w0 = min(max(qc - WIN_W // 2, 0), GRID_W - WIN_W)
        for kc in range(w0, w0 + WIN_W):
            diag[qc * GRID_W + kc, kc - qc + WIN_W - 1] = 1.0
    return diag, diag.sum(axis=1)[None, :]


def _split3(x):
    a = x.astype(BF16)
    r = x - a.astype(F32)
    b = r.astype(BF16)
    c = (r - b.astype(F32)).astype(BF16)
    return a, b, c


N_DROW = 2 * WIN_H - 1
N_DPAIR = N_DROW - 1


def _bias_pairs(rpb):
    diag, valid = _window_maps()
    r2 = jnp.pad(rpb.reshape(N_HEADS * N_DROW, 2 * WIN_W - 1),
                 ((0, 128 - N_HEADS * N_DROW), (0, 128 - (2 * WIN_W - 1))))

    def body(r_ref, d_ref, v_ref, o_ref):
        dv = d_ref[...]
        t = sum(_dot(part, dv, NN) for part in _split3(r_ref[...]))
        o_ref[...] = jnp.where(v_ref[...] > 0.0, t, -1e30)

    t = pl.pallas_call(body, name="rpb_expand", out_shape=_sds((128, GRID_W * GRID_W), F32),
                       compiler_params=_params())(r2, jnp.asarray(diag.T, BF16), jnp.asarray(valid, F32))
    t = t[:N_HEADS * N_DROW].reshape(N_HEADS, N_DROW, GRID_W, GRID_W)
    return jnp.concatenate([t[:, :N_DPAIR], t[:, 1:]], axis=-1)


def _row_bias(tb_ref, hh, d0):
    return jnp.concatenate([tb_ref[hh, d0 + 2 * ii] for ii in range(WIN_H // 2)], axis=1)


def _row_window(r):
    rs = jnp.clip(r - WIN_H // 2, 0, N_ROWS - WIN_H)
    return pl.multiple_of(r * GRID_W, GRID_W), pl.multiple_of(rs * GRID_W, GRID_W), rs - r + (WIN_H - 1)


def _split_heads(src_ref, dst_ref, scale=None):
    for hh in range(2):
        v = src_ref[:, hh * HEAD_DIM:(hh + 1) * HEAD_DIM]
        dst_ref[hh] = (v if scale is None else v * scale).astype(BF16)


def _attn_items(qb_ref, kb_ref, vb_ref, tb_ref, first_row, n_rows):
    wins = [_row_window(first_row + u) for u in range(n_rows)]
    items = [(u, hh) for u in range(n_rows) for hh in range(2)]
    q = [qb_ref[hh, pl.ds(wins[u][0], GRID_W), :] for u, hh in items]
    k = [kb_ref[hh, pl.ds(wins[u][1], KEYS), :] for u, hh in items]
    v = [vb_ref[hh, pl.ds(wins[u][1], KEYS), :] for u, hh in items]
    s = [_dot(qi, ki, NT) + _row_bias(tb_ref, hh, wins[u][2]) for qi, ki, (u, hh) in zip(q, k, items)]
    m = [jnp.max(si, axis=-1, keepdims=True) for si in s]
    e = [jnp.exp(si - mi) for si, mi in zip(s, m)]
    inv = [1.0 / jnp.sum(ei, axis=-1, keepdims=True) for ei in e]
    p = [ei * li for ei, li in zip(e, inv)]
    return wins, items, q, k, v, p


def _attn_in_specs():
    q = pl.BlockSpec((T, HP), lambda p: (0, p))
    k = pl.BlockSpec((T, HP), lambda p: (0, N_HP + p))
    v = pl.BlockSpec((T, HP), lambda p: (0, 2 * N_HP + p))
    tb = pl.BlockSpec((2, N_DPAIR, GRID_W, HP), lambda p: (p, 0, 0, 0))
    return q, k, v, tb


_HEAD_SCRATCH = pltpu.VMEM((2, T, HEAD_DIM), BF16)


def _attn_fwd(z, tb):
    def body(q_ref, k_ref, v_ref, tb_ref, o_ref, qb_ref, kb_ref, vb_ref):
        _split_heads(q_ref, qb_ref, SCALE)
        _split_heads(k_ref, kb_ref)
        _split_heads(v_ref, vb_ref)

        def rows(it, carry):
            wins, items, _, _, v, p = _attn_items(qb_ref, kb_ref, vb_ref, tb_ref, it * ATT_UNROLL_FWD, ATT_UNROLL_FWD)
            o = [_dot(pi.astype(BF16), vi, NN) for pi, vi in zip(p, v)]
            for u, (q0, _, _) in enumerate(wins):
                o_ref[pl.ds(q0, GRID_W), :] = jnp.concatenate(o[2 * u:2 * u + 2], axis=1).astype(BF16)
            return carry

        lax.fori_loop(0, N_ROWS // ATT_UNROLL_FWD, rows, 0)

    blk = pl.BlockSpec((T, HP), lambda p: (0, p))
    return pl.pallas_call(
        body, name="attn_fwd", grid=(N_HP,), in_specs=list(_attn_in_specs()), out_specs=blk,
        out_shape=_sds((T, D_ATT), BF16), scratch_shapes=[_HEAD_SCRATCH] * 3,
        compiler_params=_params(("parallel",)))(z, z, z, tb)


def _attn_bwd(z, tb, d_att):
    def body(q_ref, k_ref, v_ref, tb_ref, do_ref, dq_ref, dk_ref, dv_ref, ds_ref,
             qb_ref, kb_ref, vb_ref, dob_ref, dka_ref, dva_ref):
        _split_heads(q_ref, qb_ref, SCALE)
        _split_heads(k_ref, kb_ref)
        _split_heads(v_ref, vb_ref)
        _split_heads(do_ref, dob_ref)
        dka_ref[...] = jnp.zeros_like(dka_ref)
        dva_ref[...] = jnp.zeros_like(dva_ref)
        ds_ref[...] = jnp.zeros_like(ds_ref)

        def rows(it, carry):
            wins, items, q, k, v, p = _attn_items(qb_ref, kb_ref, vb_ref, tb_ref, it * ATT_UNROLL_BWD, ATT_UNROLL_BWD)
            do = [dob_ref[hh, pl.ds(wins[u][0], GRID_W), :] for u, hh in items]
            dv = [_dot(pi.astype(BF16), di, TN) for pi, di in zip(p, do)]
            dp = [_dot(di, vi, NT) for di, vi in zip(do, v)]
            ds = [pi * (dpi - jnp.sum(dpi * pi, axis=-1, keepdims=True)) for pi, dpi in zip(p, dp)]
            dsb = [d.astype(BF16) for d in ds]
            dq = [_dot(d, ki, NN) * SCALE for d, ki in zip(dsb, k)]
            dk = [_dot(d, qi, TN) for d, qi in zip(dsb, q)]
            for d, (u, hh) in zip(ds, items):
                for ii in range(WIN_H // 2):
                    ds_ref[hh, wins[u][2] + 2 * ii] += d[:, ii * HP:(ii + 1) * HP]
            for u, (q0, k0, _) in enumerate(wins):
                dq_ref[pl.ds(q0, GRID_W), :] = jnp.concatenate(dq[2 * u:2 * u + 2], axis=1).astype(BF16)
                dka_ref[pl.ds(k0, KEYS), :] += jnp.concatenate(dk[2 * u:2 * u + 2], axis=1)
                dva_ref[pl.ds(k0, KEYS), :] += jnp.concatenate(dv[2 * u:2 * u + 2], axis=1)
            return carry

        lax.fori_loop(0, N_ROWS // ATT_UNROLL_BWD, rows, 0)
        dk_ref[...] = dka_ref[...].astype(BF16)
        dv_ref[...] = dva_ref[...].astype(BF16)

    blk = pl.BlockSpec((T, HP), lambda p: (0, p))
    q, k, v, tbs = _attn_in_specs()
    return pl.pallas_call(
        body, name="attn_bwd", grid=(N_HP,), in_specs=[q, k, v, tbs, blk],
        out_specs=[blk, blk, blk, tbs],
        out_shape=[_sds((T, D_ATT), BF16)] * 3 + [_sds((N_HEADS, N_DPAIR, GRID_W, HP), F32)],
        scratch_shapes=[_HEAD_SCRATCH] * 4 + [pltpu.VMEM((T, HP), F32), pltpu.VMEM((T, HP), F32)],
        compiler_params=_params(("parallel",)))(z, z, z, tb, d_att)


def _rpb_grad(ds_acc):
    a = ds_acc.reshape(N_HEADS, N_DPAIR, GRID_W, 2, GRID_W).transpose(0, 1, 3, 2, 4)
    a = jnp.pad(a.reshape(N_HEADS * N_DPAIR * 2, GRID_W * GRID_W), ((0, 256 - N_HEADS * N_DPAIR * 2), (0, 0)))
    sel = np.zeros((128, 256), np.float32)
    for h in range(N_HEADS):
        for pair in range(N_DPAIR):
            for half in range(2):
                sel[h * N_DROW + pair + half, (h * N_DPAIR + pair) * 2 + half] = 1.0
    diag, _ = _window_maps()

    def body(a_ref, sel_ref, diag_ref, o_ref):
        selv = sel_ref[...]
        g = sum(_dot(selv, part, NN) for part in _split3(a_ref[...]))
        diagv = diag_ref[...]
        o_ref[...] = sum(_dot(part, diagv, NN) for part in _split3(g))

    out = pl.pallas_call(body, name="rpb_grad", out_shape=_sds((128, 128), F32),
                         compiler_params=_params())(a, jnp.asarray(sel, BF16), jnp.asarray(diag, BF16))
    return out[:N_HEADS * N_DROW, :2 * WIN_W - 1].reshape(N_HEADS, N_DROW, 2 * WIN_W - 1)


N_CB = D // REC_CB
N_CHUNK = T // REC_CHUNK
N_TILE = T // 8
_U_BLK = 1536 // REC_CB
_Y_BLK = 2560 // REC_CB


def _block_diag(w):
    per = REC_CB // 64
    wt = w.reshape(2, N_CB, per, 64, 64)
    eye = jnp.eye(per, dtype=w.dtype)
    full = wt[:, :, :, :, None, :] * eye[None, None, :, None, :, None]
    return full.reshape(2, N_CB, REC_CB, REC_CB).astype(BF16)


def _block_diag_grad(g):
    per = REC_CB // 64
    g6 = g.reshape(2, N_CB, per, 64, per, 64)
    return jnp.stack([g6[:, :, p, :, p, :] for p in range(per)], axis=2).reshape(2, 16, 64, 64)


def _gelu(x):
    c = 0.7978845608028654
    return 0.5 * x * (1.0 + jnp.tanh(c * (x + 0.044715 * x * x * x)))


def _gelu_grad(x):
    c = 0.7978845608028654
    th = jnp.tanh(c * (x + 0.044715 * x * x * x))
    return 0.5 * (1.0 + th) + 0.5 * x * (1.0 - th * th) * c * (1.0 + 3.0 * 0.044715 * x * x)


def _softplus_neg(lam):
    x = -lam
    e = jnp.exp(-jnp.abs(x))
    w = 1.0 + e
    l1p = jnp.where(w == 1.0, e, jnp.log(w) * e / (w - 1.0))
    return jnp.maximum(x, 0.0) + l1p


def _one_minus_exp(x):
    poly = x * (1.0 + x * (1 / 2 + x * (1 / 6 + x * (1 / 24 + x * (1 / 120 + x * (1 / 720))))))
    return jnp.where(x > -0.125, -poly, 1.0 - jnp.exp(x))


def _conv_taps(pad_ref, t0, w, sign):
    out = None
    for j in range(4):
        term = w[j:j + 1, :] * pad_ref[pl.ds(PAD + t0 + sign * (j - 2), REC_CHUNK), :]
        out = term if out is None else out + term
    return out


def _gates(u, wa, wi, ba, bi, sp):
    ub = u.astype(BF16)
    r = _sigmoid(_dot(ub, wa, NN) + ba)
    i = _sigmoid(_dot(ub, wi, NN) + bi)
    log_a = -LRU_C * r * sp
    a = jnp.exp(log_a)
    mult = jnp.sqrt(jnp.maximum(_one_minus_exp(2.0 * log_a), 0.0))
    return r, i, a, mult


def _tile_scan(a, b, sub, reverse):
    for s in (1, 2, 4):
        if reverse:
            a_s, b_s, m = pltpu.roll(a, 8 - s, 0), pltpu.roll(b, 8 - s, 0), sub < 8 - s
        else:
            a_s, b_s, m = pltpu.roll(a, s, 0), pltpu.roll(b, s, 0), sub >= s
        b = jnp.where(m, a * b_s + b, b)
        a = jnp.where(m, a * a_s, a)
    return a, b


def _last_row(x, sub, row):
    return jnp.broadcast_to(jnp.sum(jnp.where(sub == row, x, 0.0), axis=0, keepdims=True), x.shape)


def _rec_prologue(up_ref, cw_ref, cb_ref, wa_ref, wi_ref, ba_ref, bi_ref, lam_ref,
                  upad_ref, u_ref, a_refs, h_refs):
    cb = up_ref.shape[1]
    zeros = jnp.zeros((PAD, cb), F32)
    upad_ref[pl.ds(0, PAD), :] = zeros
    upad_ref[pl.ds(PAD + T, PAD), :] = zeros
    upad_ref[pl.ds(PAD, T), :] = up_ref[...]
    cw = cw_ref[...]
    sp = _softplus_neg(lam_ref[...])
    for c in range(N_CHUNK):
        t0 = c * REC_CHUNK
        u = cb_ref[...] + _conv_taps(upad_ref, t0, cw, 1)
        u_ref[pl.ds(t0, REC_CHUNK), :] = u
        for d in range(2):
            _, i, a, mult = _gates(u, wa_ref[d], wi_ref[d], ba_ref[d:d + 1, :], bi_ref[d:d + 1, :], sp[d:d + 1, :])
            a_refs[d][pl.ds(t0, REC_CHUNK), :] = a
            h_refs[d][pl.ds(t0, REC_CHUNK), :] = mult * (i * u)

    sub = lax.broadcasted_iota(jnp.int32, (8, cb), 0)

    def tile(k, carry):
        cf, cr = carry
        tf = pl.multiple_of(k * 8, 8)
        tr = pl.multiple_of((N_TILE - 1 - k) * 8, 8)
        af, bf = _tile_scan(a_refs[0][pl.ds(tf, 8), :], h_refs[0][pl.ds(tf, 8), :], sub, False)
        hf = af * cf + bf
        h_refs[0][pl.ds(tf, 8), :] = hf
        ar, br = _tile_scan(a_refs[1][pl.ds(tr, 8), :], h_refs[1][pl.ds(tr, 8), :], sub, True)
        hr = ar * cr + br
        h_refs[1][pl.ds(tr, 8), :] = hr
        return _last_row(hf, sub, 7), _last_row(hr, sub, 0)

    z8 = jnp.zeros((8, cb), F32)
    lax.fori_loop(0, N_TILE, tile, (z8, z8))
    return sp


def _rec_specs():
    up = pl.BlockSpec((T, REC_CB), lambda c: (0, _U_BLK + c))
    yb = pl.BlockSpec((T, REC_CB), lambda c: (0, _Y_BLK + c))
    cw = pl.BlockSpec((4, REC_CB), lambda c: (0, c))
    cbias = pl.BlockSpec((1, REC_CB), lambda c: (0, c))
    wbd = pl.BlockSpec((2, None, REC_CB, REC_CB), lambda c: (0, c, 0, 0))
    vec2 = pl.BlockSpec((2, REC_CB), lambda c: (0, c))
    col = pl.BlockSpec((T, REC_CB), lambda c: (0, c))
    return up, yb, cw, cbias, wbd, vec2, col


def _rec_fwd(z, conv_w, conv_b, wa, wi, ba, bi, lam):
    up, yb, cw, cbias, wbd, vec2, col = _rec_specs()

    def body(up_ref, yb_ref, cw_ref, cb_ref, wa_ref, wi_ref, ba_ref, bi_ref, lam_ref, g_ref,
             upad_ref, u_ref, af_ref, ar_ref, hf_ref, hr_ref):
        _rec_prologue(up_ref, cw_ref, cb_ref, wa_ref, wi_ref, ba_ref, bi_ref, lam_ref,
                      upad_ref, u_ref, (af_ref, ar_ref), (hf_ref, hr_ref))

        def chunk(c, carry):
            t0 = pl.multiple_of(c * REC_CHUNK, REC_CHUNK)
            rows = pl.ds(t0, REC_CHUNK)
            g_ref[rows, :] = ((hf_ref[rows, :] + hr_ref[rows, :]) * _gelu(yb_ref[rows, :])).astype(BF16)
            return carry

        lax.fori_loop(0, N_CHUNK, chunk, 0)

    full = pltpu.VMEM((T, REC_CB), F32)
    return pl.pallas_call(
        body, name="rec_fwd", grid=(N_CB,),
        in_specs=[up, yb, cw, cbias, wbd, wbd, vec2, vec2, vec2], out_specs=col,
        out_shape=_sds((T, D), BF16),
        scratch_shapes=[pltpu.VMEM((T + 2 * PAD, REC_CB), F32), full, full, full, full, full],
        compiler_params=_params(("parallel",)))(z, z, conv_w, conv_b, wa, wi, ba, bi, lam)


def _rec_bwd(z, dg, conv_w, conv_b, wa, wi, ba, bi, lam):
    up, yb, cw, cbias, wbd, vec2, col = _rec_specs()

    def body(up_ref, yb_ref, dg_ref, cw_ref, cb_ref, wa_ref, wi_ref, ba_ref, bi_ref, lam_ref,
             dup_ref, dyb_ref, dcw_ref, dcb_ref, dwa_ref, dwi_ref, dba_ref, dbi_ref, dlam_ref,
             upad_ref, u_ref, af_ref, ar_ref, hf_ref, hr_ref, dh_ref, gf_ref, gr_ref, daf_ref, dar_ref, dupad_ref):
        a_refs, h_refs = (af_ref, ar_ref), (hf_ref, hr_ref)
        g_refs, da_refs = (gf_ref, gr_ref), (daf_ref, dar_ref)
        sp = _rec_prologue(up_ref, cw_ref, cb_ref, wa_ref, wi_ref, ba_ref, bi_ref, lam_ref,
                           upad_ref, u_ref, a_refs, h_refs)
        cb = up_ref.shape[1]

        def gate_chunk(c, carry):
            t0 = pl.multiple_of(c * REC_CHUNK, REC_CHUNK)
            rows = pl.ds(t0, REC_CHUNK)
            y = yb_ref[rows, :]
            dgv = dg_ref[rows, :].astype(F32)
            dh_ref[rows, :] = dgv * _gelu(y)
            dyb_ref[rows, :] = (dgv * (hf_ref[rows, :] + hr_ref[rows, :]) * _gelu_grad(y)).astype(BF16)
            return carry

        lax.fori_loop(0, N_CHUNK, gate_chunk, 0)

        sub = lax.broadcasted_iota(jnp.int32, (8, cb), 0)

        def tile(k, carry):
            cf, cr = carry
            kf = N_TILE - 1 - k
            tf = pl.multiple_of(kf * 8, 8)
            tnext = pl.multiple_of(jnp.minimum(kf + 1, N_TILE - 1) * 8, 8)
            tprev = pl.multiple_of(jnp.maximum(kf - 1, 0) * 8, 8)
            a_t = af_ref[pl.ds(tf, 8), :]
            a_n = jnp.where(kf < N_TILE - 1, af_ref[pl.ds(tnext, 8), :], 0.0)
            a_sh = jnp.where(sub == 7, pltpu.roll(a_n, 7, 0), pltpu.roll(a_t, 7, 0))
            ca, cbb = _tile_scan(a_sh, dh_ref[pl.ds(tf, 8), :], sub, True)
            gf = ca * cf + cbb
            h_t = hf_ref[pl.ds(tf, 8), :]
            h_p = jnp.where(kf > 0, hf_ref[pl.ds(tprev, 8), :], 0.0)
            h_sh = jnp.where(sub == 0, pltpu.roll(h_p, 1, 0), pltpu.roll(h_t, 1, 0))
            gf_ref[pl.ds(tf, 8), :] = gf
            daf_ref[pl.ds(tf, 8), :] = gf * h_sh
            tr = pl.multiple_of(k * 8, 8)
            rnext = pl.multiple_of(jnp.minimum(k + 1, N_TILE - 1) * 8, 8)
            rprev = pl.multiple_of(jnp.maximum(k - 1, 0) * 8, 8)
            b_t = ar_ref[pl.ds(tr, 8), :]
            b_p = jnp.where(k > 0, ar_ref[pl.ds(rprev, 8), :], 0.0)
            b_sh = jnp.where(sub == 0, pltpu.roll(b_p, 1, 0), pltpu.roll(b_t, 1, 0))
            ra, rb = _tile_scan(b_sh, dh_ref[pl.ds(tr, 8), :], sub, False)
            gr = ra * cr + rb
            hr_t = hr_ref[pl.ds(tr, 8), :]
            hr_n = jnp.where(k < N_TILE - 1, hr_ref[pl.ds(rnext, 8), :], 0.0)
            hr_sh = jnp.where(sub == 7, pltpu.roll(hr_n, 7, 0), pltpu.roll(hr_t, 7, 0))
            gr_ref[pl.ds(tr, 8), :] = gr
            dar_ref[pl.ds(tr, 8), :] = gr * hr_sh
            return _last_row(gf, sub, 0), _last_row(gr, sub, 7)

        z8 = jnp.zeros((8, cb), F32)
        lax.fori_loop(0, N_TILE, tile, (z8, z8))

        zeros = jnp.zeros((PAD, cb), F32)
        dupad_ref[pl.ds(0, PAD), :] = zeros
        dupad_ref[pl.ds(PAD + T, PAD), :] = zeros
        dwa_ref[...] = jnp.zeros_like(dwa_ref)
        dwi_ref[...] = jnp.zeros_like(dwi_ref)
        dba_ref[...] = jnp.zeros_like(dba_ref)
        dbi_ref[...] = jnp.zeros_like(dbi_ref)
        dlam_ref[...] = jnp.zeros_like(dlam_ref)

        def grad_chunk(c, carry):
            t0 = pl.multiple_of(c * REC_CHUNK, REC_CHUNK)
            rows = pl.ds(t0, REC_CHUNK)
            u = u_ref[rows, :]
            ub = u.astype(BF16)
            du = jnp.zeros((REC_CHUNK, cb), F32)
            for d in range(2):
                r, i, a, mult = _gates(u, wa_ref[d], wi_ref[d], ba_ref[d:d + 1, :], bi_ref[d:d + 1, :], sp[d:d + 1, :])
                dbx = g_refs[d][rows, :]
                dmult = dbx * (i * u)
                diu = dbx * mult
                a2 = a * a
                dlog = da_refs[d][rows, :] * a - dmult * jnp.where(mult > 0.0, a2 / mult, 0.0)
                dpa = (dlog * (-LRU_C) * sp[d:d + 1, :]) * r * (1.0 - r)
                dpi = (diu * u) * i * (1.0 - i)
                dpab, dpib = dpa.astype(BF16), dpi.astype(BF16)
                du = du + diu * i + _dot(dpab, wa_ref[d], NT) + _dot(dpib, wi_ref[d], NT)
                dwa_ref[d] += _dot(ub, dpab, TN)
                dwi_ref[d] += _dot(ub, dpib, TN)
                dba_ref[d:d + 1, :] += jnp.sum(dpa, axis=0, keepdims=True)
                dbi_ref[d:d + 1, :] += jnp.sum(dpi, axis=0, keepdims=True)
                dlam_ref[d:d + 1, :] += jnp.sum(dlog * r, axis=0, keepdims=True)
            dupad_ref[pl.ds(PAD + t0, REC_CHUNK), :] = du
            return carry

        lax.fori_loop(0, N_CHUNK, grad_chunk, 0)
        dlam_ref[...] = dlam_ref[...] * (LRU_C * _sigmoid(-lam_ref[...]))

        cw = cw_ref[...]
        dcb = jnp.zeros((1, cb), F32)
        dcw = [jnp.zeros((1, cb), F32) for _ in range(4)]
        for c in range(N_CHUNK):
            t0 = c * REC_CHUNK
            du = dupad_ref[pl.ds(PAD + t0, REC_CHUNK), :]
            dcb = dcb + jnp.sum(du, axis=0, keepdims=True)
            for j in range(4):
                dcw[j] = dcw[j] + jnp.sum(du * upad_ref[pl.ds(PAD + t0 + j - 2, REC_CHUNK), :], axis=0, keepdims=True)
            dup_ref[pl.ds(t0, REC_CHUNK), :] = _conv_taps(dupad_ref, t0, cw, -1).astype(BF16)
        dcb_ref[...] = dcb
        dcw_ref[...] = jnp.concatenate(dcw, axis=0)

    full = pltpu.VMEM((T, REC_CB), F32)
    padded = pltpu.VMEM((T + 2 * PAD, REC_CB), F32)
    return pl.pallas_call(
        body, name="rec_bwd", grid=(N_CB,),
        in_specs=[up, yb, col, cw, cbias, wbd, wbd, vec2, vec2, vec2],
        out_specs=[col, col, cw, cbias, wbd, wbd, vec2, vec2, vec2],
        out_shape=[_sds((T, D), BF16), _sds((T, D), BF16), _sds((4, D), F32), _sds((1, D), F32),
                   _sds((2, N_CB, REC_CB, REC_CB), F32), _sds((2, N_CB, REC_CB, REC_CB), F32),
                   _sds((2, D), F32), _sds((2, D), F32), _sds((2, D), F32)],
        scratch_shapes=[padded, full, full, full, full, full, full, full, full, full, full, padded],
        compiler_params=_params(("parallel",)))(z, z, dg, conv_w, conv_b, wa, wi, ba, bi, lam)


def _local_step(x, target, p, late=None):
    x = x.reshape(T, D)
    target = target.reshape(T, D)
    tb = _bias_pairs(p["rpb"])
    wa, wi = _block_diag(p["w_rg_a"]), _block_diag(p["w_rg_i"])

    h1 = _rms_fwd("rms1_fwd", x, p["ln1_g"], after=late[0] if late else ())
    if late:
        p = {**p, **late[1](h1)}
    rec_params = (p["conv_w"], p["conv_b"], wa, wi, p["b_rg_a"], p["b_rg_i"], p["lru_lambda"])
    (z,) = _mm_nn_cols("mm_z", h1, p["w_in"], F32, bias=p["b_in"])
    att = _attn_fwd(z, tb)
    g = _rec_fwd(z, *rec_params)
    if late:
        p = {**p, **late[2](g)}
    (y_att,) = _mm_nn_cols("mm_y_att", att, p["w_att_o"], F32)
    (y_rec,) = _mm_nn_rows("mm_y_rec", g, p["w_rec_o"], F32, tk=D)
    mixed = _merge_fwd(y_att, y_rec, z)

    def add_res(r, ex, outs):
        outs[0][...] = ex[0][...] + r

    res_spec = pl.BlockSpec((TM, D), lambda i, j, k: (i, 0))
    (x1,) = _mm_nn_rows("mm_x1", mixed, p["w_out"], F32, tk=D, extras=[x], extra_specs=[res_spec], epilogue=add_res)
    h2 = _rms_fwd("rms2_fwd", x1, p["ln2_g"])

    def relu2(r, ex, outs):
        outs[0][...] = r
        rp = jnp.maximum(r, 0.0)
        outs[1][...] = (rp * rp).astype(BF16)

    ff_blk = pl.BlockSpec((TM, D), lambda j, i, k: (i, j))
    f, s = _mm_nn_cols("mm_ff1", h2, p["w_ff1"], F32, epilogue=relu2,
                       out_shapes=[_sds((T, D_FF), F32), _sds((T, D_FF), BF16)], out_specs=[ff_blk, ff_blk])
    (x2,) = _mm_nn_rows("mm_x2", s, p["w_ff2"], F32, tk=D, extras=[x1], extra_specs=[res_spec], epilogue=add_res)
    loss, dx2, g_lnf = _loss_head(x2, target, p["lnf_g"])

    def relu2_bwd(r, ex, outs):
        outs[0][...] = (r * 2.0 * jnp.maximum(ex[0][...], 0.0)).astype(BF16)

    (df,) = _mm_nt_rows("mm_df", dx2, p["w_ff2"], BF16, tn=D, extras=[f],
                        extra_specs=[pl.BlockSpec((TM, D), lambda j, i, k: (i, j))], epilogue=relu2_bwd)
    (g_w_ff2,) = _mm_tn_rows("mm_g_ff2", s, dx2, tm=D)
    (g_w_ff1,) = _mm_tn_cols("mm_g_ff1", h2, df, D)
    (dh2,) = _mm_nt_cols("mm_dh2", df, p["w_ff1"], F32)
    dx1, g_ln2 = _rms_bwd("rms2_bwd", dh2, x1, p["ln2_g"], dx2)

    (dmixed,) = _mm_nt_rows("mm_dmixed", dx1, p["w_out"], F32, tn=D)
    (g_w_out,) = _mm_tn_rows("mm_g_out", mixed, dx1, tm=D)
    dy_att, dy_rec, dg_att, dg_rec = _merge_bwd(dmixed, y_att, y_rec, z)
    (d_att,) = _mm_nt_cols("mm_d_att", dy_att, p["w_att_o"], BF16)
    (g_w_att_o,) = _mm_tn_cols("mm_g_att_o", att, dy_att, D // N_CHIPS)
    (d_g,) = _mm_nt_rows("mm_d_g", dy_rec, p["w_rec_o"], BF16, tn=D)
    (g_w_rec_o,) = _mm_tn_rows("mm_g_rec_o", g, dy_rec, tm=D)

    dq, dk, dv, ds_acc = _attn_bwd(z, tb, d_att)
    g_rpb = _rpb_grad(ds_acc)
    d_up, d_yb, g_conv_w, g_conv_b, g_wa, g_wi, g_ba, g_bi, g_lam = _rec_bwd(z, d_g, *rec_params)
    dz = jnp.concatenate([dq, dk, dv, d_up, d_yb, dg_att, dg_rec], axis=1)

    g_w_in, g_b_in = _mm_tn_cols("mm_g_in", h1, dz, D_IN // N_CHIPS, colsum=True)
    (dh1,) = _mm_nt_cols("mm_dh1", dz, p["w_in"], F32)
    grad_x, g_ln1 = _rms_bwd("rms1_bwd", dh1, x, p["ln1_g"], dx1)

    grads = dict(ln1_g=g_ln1, w_in=g_w_in, b_in=g_b_in, rpb=g_rpb, w_att_o=g_w_att_o, conv_w=g_conv_w,
                 conv_b=g_conv_b, w_rg_a=_block_diag_grad(g_wa), b_rg_a=g_ba, w_rg_i=_block_diag_grad(g_wi),
                 b_rg_i=g_bi, lru_lambda=g_lam, w_rec_o=g_w_rec_o, w_out=g_w_out, ln2_g=g_ln2,
                 w_ff1=g_w_ff1, w_ff2=g_w_ff2, lnf_g=g_lnf)
    return loss, grad_x.reshape(1, T, D), grads


_ANY = pl.BlockSpec(memory_space=pl.ANY)
N_PEERS = N_CHIPS - 1


def _place():
    x, y, c = lax.axis_index("x"), lax.axis_index("y"), lax.axis_index("c")
    peers = [(1 - x, y), (x, 1 - y), (1 - x, 1 - y)]
    return x, y, c, 2 * x + y, peers


def _remote(src, dst, send_sem, recv_sem, dev):
    return pltpu.make_async_remote_copy(src_ref=src, dst_ref=dst, send_sem=send_sem, recv_sem=recv_sem,
                                        device_id=dev, device_id_type=MESH)


def _prefetch_call(body, name, ids, grid, in_specs, out_specs, out_shape, args):
    spec = pltpu.PrefetchScalarGridSpec(num_scalar_prefetch=1, grid=grid, in_specs=in_specs, out_specs=out_specs)
    return pl.pallas_call(body, name=name, grid_spec=spec, out_shape=out_shape,
                          compiler_params=_params(("parallel",) * len(grid)))(ids, *args)


def _cast_bf16(name, w, chip_id, after=()):
    rows, cols = w.shape
    rb = min(rows, 256)

    def body(ids_ref, w_ref, *rest):
        rest[-1][...] = w_ref[...].astype(BF16)

    return _prefetch_call(body, name, chip_id, (rows // rb,),
                          [pl.BlockSpec((rb, cols), lambda i, ids: (i, 0))] + [_ANY] * len(after),
                          pl.BlockSpec((None, rb, cols), lambda i, ids: (ids[0], i, 0)),
                          _sds((N_CHIPS, rows, cols), BF16), (w, *after))


def _dma_sems(*counts):
    return [pltpu.SemaphoreType.DMA((k,)) for k in counts]


_HBM = pl.BlockSpec(memory_space=pltpu.HBM)
_SEM = pl.BlockSpec(memory_space=pltpu.SEMAPHORE)
_SPLIT_COPY = pltpu.CompilerParams(has_side_effects=pltpu.SideEffectType.DATAFLOW_SIDE_EFFECTING)


def _hbm(arrays):
    return [pltpu.with_memory_space_constraint(a, pltpu.HBM) for a in arrays]


def _hbm_like(arrays):
    return [pltpu.HBM(a.shape, a.dtype) for a in arrays]


def _halves(buf, c):
    half = buf.shape[1] // 2
    return pl.ds(c * half, half), pl.ds((1 - c) * half, half)


def _gather_start(name, slots):
    n = len(slots)
    nk = n * N_PEERS

    def body(*refs):
        bufs = refs[n:2 * n]
        send_sems, recv_sems, token = refs[2 * n:]
        x, y, c, chip, peers = _place()
        for t in range(n):
            mine, _ = _halves(bufs[t], c)
            for r, (px, py) in enumerate(peers):
                k = t * N_PEERS + r
                own = bufs[t].at[chip, mine]
                _remote(own, own, send_sems.at[k], recv_sems.at[k], (px, py, c)).start()
        token[...] = jnp.zeros_like(token)

    res = pl.pallas_call(
        body, name=name, in_specs=[_HBM] * n, out_specs=[_HBM] * n + [_SEM, _SEM, pl.BlockSpec(memory_space=pltpu.VMEM)],
        out_shape=_hbm_like(slots) + [pltpu.SemaphoreType.DMA((nk,)), pltpu.SemaphoreType.DMA((nk,)),
                                      _sds((8, 128), F32)],
        input_output_aliases={t: t for t in range(n)}, compiler_params=_SPLIT_COPY)(*_hbm(slots))
    return res[:n], (res[n], res[n + 1]), res[n + 2]


def _gather_wait(name, bufs, sems, after):
    n = len(bufs)

    def body(*refs):
        ins = refs[:n]
        send_sems, recv_sems = refs[n], refs[n + 1]
        x, y, c, chip, peers = _place()
        for t in range(n):
            mine, _ = _halves(ins[t], c)
            for r, (px, py) in enumerate(peers):
                k = t * N_PEERS + r
                cp = _remote(ins[t].at[chip, mine], ins[t].at[2 * px + py, mine], send_sems.at[k], recv_sems.at[k],
                             (px, py, c))
                cp.wait_send()
                cp.wait_recv()

    return pl.pallas_call(
        body, name=name, in_specs=[_HBM] * n + [_SEM, _SEM, _ANY], out_specs=[_HBM] * n, out_shape=_hbm_like(bufs),
        input_output_aliases={t: t for t in range(n)}, compiler_params=_SPLIT_COPY)(*bufs, *sems, after)


def _gather_forward(name, bufs):
    n = len(bufs)
    nk = n * N_PEERS

    def body(*refs):
        outs = refs[n:2 * n]
        send_sems, recv_sems = refs[2 * n:]
        x, y, c, chip, peers = _place()
        sibling = (x, y, 1 - c)
        sends = []
        for t in range(n):
            mine, _ = _halves(outs[t], c)
            for r, (px, py) in enumerate(peers):
                k = t * N_PEERS + r
                landed = outs[t].at[2 * px + py, mine]
                sends.append(_remote(landed, landed, send_sems.at[k], recv_sems.at[k], sibling))
                sends[-1].start()
        for t in range(n):
            _, theirs = _halves(outs[t], c)
            for r, (px, py) in enumerate(peers):
                k = t * N_PEERS + r
                landed = outs[t].at[2 * px + py, theirs]
                _remote(landed, landed, send_sems.at[k], recv_sems.at[k], sibling).wait_recv()
        for cp in sends:
            cp.wait_send()

    return pl.pallas_call(
        body, name=name, in_specs=[_ANY] * n, out_specs=[_ANY] * n, out_shape=[_sds(b.shape, b.dtype) for b in bufs],
        input_output_aliases={t: t for t in range(n)}, scratch_shapes=_dma_sems(nk, nk))(*bufs)


def _pair_exchange(grads, small):
    n = len(grads)

    def body(*refs):
        ins, small_in = refs[:n], refs[n]
        got, small_got = refs[n + 1:2 * n + 1], refs[2 * n + 1]
        send_sem, recv_sem = refs[2 * n + 2:]
        x, y, c, _, _ = _place()
        sibling = (x, y, 1 - c)
        copies = []
        for t in range(n):
            half = ins[t].shape[1] // 2
            for j in range(N_CHIPS):
                k = t * N_CHIPS + j
                copies.append(_remote(ins[t].at[j, pl.ds((1 - c) * half, half)], got[t].at[j],
                                      send_sem.at[k], recv_sem.at[k], sibling))
        copies.append(_remote(small_in, small_got, send_sem.at[n * N_CHIPS], recv_sem.at[n * N_CHIPS], sibling))
        for cp in copies:
            cp.start()
        for cp in copies:
            cp.wait_recv()
        for cp in copies:
            cp.wait_send()

    ns = n * N_CHIPS + 1
    res = pl.pallas_call(
        body, name="grad_pair_exchange", in_specs=[_ANY] * (n + 1), out_specs=[_ANY] * (n + 1),
        out_shape=[_sds((N_CHIPS, g.shape[1] // 2, g.shape[2]), F32) for g in grads] + [_sds(small.shape, F32)],
        scratch_shapes=_dma_sems(ns, ns))(*grads, small)
    return res[:n], res[n]


def _chip_exchange(sums_bf16, small):
    n = len(sums_bf16)

    def body(*refs):
        bf_in, small_in = refs[:n], refs[n]
        got, small_got = refs[n + 1:2 * n + 1], refs[2 * n + 1]
        send_sem, recv_sem = refs[2 * n + 2:]
        x, y, c, chip, peers = _place()
        half_s = small_in.shape[0] // 2
        copies = []
        for r, (px, py) in enumerate(peers):
            for t in range(n):
                k = t * N_PEERS + r
                copies.append(_remote(bf_in[t].at[2 * px + py], got[t].at[r], send_sem.at[k], recv_sem.at[k], (px, py, c)))
            k = n * N_PEERS + r
            copies.append(_remote(small_in.at[pl.ds(c * half_s, half_s)], small_got.at[r],
                                  send_sem.at[k], recv_sem.at[k], (px, py, c)))
        for cp in copies:
            cp.start()
        for cp in copies:
            cp.wait_recv()
        for cp in copies:
            cp.wait_send()

    ns = (n + 1) * N_PEERS
    res = pl.pallas_call(
        body, name="grad_chip_exchange", in_specs=[_ANY] * (n + 1), out_specs=[_ANY] * (n + 1),
        out_shape=([_sds((N_PEERS,) + s.shape[1:], BF16) for s in sums_bf16]
                   + [_sds((N_PEERS, small.shape[0] // 2, 128), F32)]),
        scratch_shapes=_dma_sems(ns, ns))(*sums_bf16, small)
    return res[:n], res[n]


def _half_swap(bufs):
    n = len(bufs)

    def body(*refs):
        outs = refs[n:2 * n]
        send_sem, recv_sem = refs[2 * n:]
        x, y, c, _, _ = _place()
        sibling = (x, y, 1 - c)
        copies = []
        for t in range(n):
            h = outs[t].shape[0] // 2
            mine = outs[t].at[pl.ds(c * h, h)]
            copies.append(_remote(mine, mine, send_sem.at[t], recv_sem.at[t], sibling))
            copies[-1].start()
        for t in range(n):
            h = outs[t].shape[0] // 2
            theirs = outs[t].at[pl.ds((1 - c) * h, h)]
            _remote(theirs, theirs, send_sem.at[t], recv_sem.at[t], sibling).wait_recv()
        for cp in copies:
            cp.wait_send()

    return pl.pallas_call(
        body, name="grad_half_swap", in_specs=[_ANY] * n, out_specs=[_ANY] * n,
        out_shape=[_sds(b.shape, b.dtype) for b in bufs], input_output_aliases={t: t for t in range(n)},
        scratch_shapes=_dma_sems(n, n))(*bufs)


def _pair_sum(name, grad, got, ids):
    _, rows, cols = got.shape
    rb = min(rows, 256)
    nb = rows // rb
    blk = pl.BlockSpec((None, rb, cols), lambda j, i, ids: (j, i, 0))
    mine = pl.BlockSpec((None, rb, cols), lambda j, i, ids: (j, ids[1] * nb + i, 0))

    def body(ids_ref, a_ref, b_ref, s_ref, sb_ref):
        s = a_ref[...] + b_ref[...]
        s_ref[...] = s
        sb_ref[...] = s.astype(BF16)

    return _prefetch_call(body, name, ids, (N_CHIPS, nb), [mine, blk], [blk, blk],
                          [_sds(got.shape, F32), _sds(got.shape, BF16)], (grad, got))


def _chip_sum(name, sums, got, ids):
    _, rows, cols = sums.shape
    rb = min(rows, 256)
    nb = rows // rb
    own = pl.BlockSpec((None, rb, cols), lambda i, ids: (ids[0], i, 0))
    blk3 = pl.BlockSpec((N_PEERS, rb, cols), lambda i, ids: (0, i, 0))
    out = pl.BlockSpec((rb, cols), lambda i, ids: (ids[1] * nb + i, 0))

    def body(ids_ref, a_ref, b_ref, o_ref):
        o_ref[...] = ((a_ref[...] + b_ref[0].astype(F32)) + b_ref[1].astype(F32)) + b_ref[2].astype(F32)

    return _prefetch_call(body, name, ids, (nb,), [own, blk3], out, _sds((2 * rows, cols), F32), (sums, got))


SMALL_RB = 280


def _small_pair_sum(own, got):
    blk = pl.BlockSpec((SMALL_RB, 128), lambda i: (i, 0))

    def body(a_ref, b_ref, o_ref):
        o_ref[...] = a_ref[...] + b_ref[...]

    return pl.pallas_call(body, name="small_pair_sum", grid=(own.shape[0] // SMALL_RB,), in_specs=[blk, blk],
                          out_specs=blk, out_shape=_sds(own.shape, F32),
                          compiler_params=_params(("parallel",)))(own, got)


def _small_chip_sum(pair, got, ids):
    nb = pair.shape[0] // 2 // SMALL_RB
    half = pl.BlockSpec((SMALL_RB, 128), lambda i, ids: (ids[1] * nb + i, 0))
    blk3 = pl.BlockSpec((N_PEERS, SMALL_RB, 128), lambda i, ids: (0, i, 0))

    def body(ids_ref, a_ref, b_ref, o_ref):
        o_ref[...] = (a_ref[...] + b_ref[1]) + (b_ref[0] + b_ref[2])

    return _prefetch_call(body, "small_chip_sum", ids, (nb,), [half, blk3], half, _sds(pair.shape, F32), (pair, got))


def _adamw_math(w, g, m, v):
    m = ADAM_B1 * m + (1.0 - ADAM_B1) * g
    v = ADAM_B2 * v + (1.0 - ADAM_B2) * (g * g)
    m_hat = m / (1.0 - ADAM_B1 ** ADAM_STEP)
    v_hat = v / (1.0 - ADAM_B2 ** ADAM_STEP)
    delta = -ADAM_LR * (m_hat / (jnp.sqrt(v_hat) + ADAM_EPS) + ADAM_WD * w)
    return delta, m, v


def _adamw(name, w, g, m, v, rb=None):
    rows, cols = w.shape
    rb = rows if rb is None else rb
    blk = pl.BlockSpec((rb, cols), lambda i: (i, 0))

    def body(w_ref, g_ref, m_ref, v_ref, d_ref, nm_ref, nv_ref):
        d, nm, nv = _adamw_math(w_ref[...], g_ref[...], m_ref[...], v_ref[...])
        d_ref[...] = d
        nm_ref[...] = nm
        nv_ref[...] = nv

    return pl.pallas_call(body, name=name, grid=(rows // rb,), in_specs=[blk] * 4, out_specs=[blk] * 3,
                          out_shape=[_sds(w.shape, F32)] * 3, compiler_params=_params(("parallel",)))(w, g, m, v)


BIG = ("w_in", "w_att_o", "w_rec_o", "w_out", "w_ff1", "w_ff2")
COL_SHARDED = ("w_in", "w_att_o", "w_ff1")
SHARDED_VECS = ("conv_w", "b_rg_a", "b_rg_i", "lru_lambda")
SMALL = ("ln1_g", "b_in", "rpb", "conv_w", "conv_b", "w_rg_a", "b_rg_a", "w_rg_i", "b_rg_i", "lru_lambda",
         "ln2_g", "lnf_g")
SMALL_ROWS = 2240
ORDER = ("ln1_g", "w_in", "b_in", "rpb", "w_att_o", "conv_w", "conv_b", "w_rg_a", "b_rg_a", "w_rg_i", "b_rg_i",
         "lru_lambda", "w_rec_o", "w_out", "ln2_g", "w_ff1", "w_ff2", "lnf_g")


def _pack_small(grads, loss):
    parts, sizes = [], {}
    for n in SMALL:
        flat = grads[n].reshape(-1)
        pad = (-flat.shape[0]) % 128
        sizes[n] = (flat.shape[0], flat.shape[0] + pad)
        parts.append(jnp.pad(flat, (0, pad)))
    total = sum(s[1] for s in sizes.values())
    parts.append(jnp.pad(loss.reshape(1), (0, SMALL_ROWS * 128 - total - 1)))
    return jnp.concatenate(parts).reshape(SMALL_ROWS, 128), sizes


def _unpack_small(buf, sizes, shapes):
    flat = buf.reshape(-1)
    out, pos = {}, 0
    for n in SMALL:
        size, padded = sizes[n]
        out[n] = flat[pos:pos + size].reshape(shapes[n])
        pos += padded
    return out, flat[pos]


def _gather_weights(w, chip):
    chip_id = chip.astype(jnp.int32).reshape(1)
    vec_rows = [w[n][0] for n in SHARDED_VECS]
    vec_shard = jnp.concatenate(vec_rows + [jnp.zeros((16 - 10, D // N_CHIPS), F32)], axis=0)
    vec_slots = lax.dynamic_update_slice(jnp.zeros((N_CHIPS, 16, D // N_CHIPS), F32), vec_shard[None], (chip, 0, 0))
    bufs_a, sems_a, token_a = _gather_start("gather_start_first", [_cast_bf16("cast_w_in", w["w_in"][0], chip_id), vec_slots])
    rest_names = BIG[1:]
    bufs_b, sems_b, token_b = _gather_start(
        "gather_start_rest", [_cast_bf16("cast_" + n, w[n][0], chip_id, after=(token_a,)) for n in rest_names])

    def first(after):
        w_in_full, vec_full = _gather_forward("gather_forward_first", _gather_wait("gather_wait_first", bufs_a, sems_a, after))
        vecs = vec_full.transpose(1, 0, 2).reshape(16, D)
        return dict(w_in=w_in_full, conv_w=vecs[0:4], b_rg_a=vecs[4:6], b_rg_i=vecs[6:8], lru_lambda=vecs[8:10])

    def rest(after):
        full = dict(zip(rest_names, _gather_forward("gather_forward_rest",
                                                    _gather_wait("gather_wait_rest", bufs_b, sems_b, after))))
        return dict(w_att_o=full["w_att_o"], w_ff1=full["w_ff1"], w_rec_o=full["w_rec_o"].reshape(D, D),
                    w_out=full["w_out"].reshape(D, D), w_ff2=full["w_ff2"].reshape(D_FF, D))

    p = dict(ln1_g=w["ln1_g"], b_in=w["b_in"], rpb=w["rpb"][0], conv_b=w["conv_b"], w_rg_a=w["w_rg_a"][0],
             w_rg_i=w["w_rg_i"][0], ln2_g=w["ln2_g"], lnf_g=w["lnf_g"].reshape(1, D))
    return p, ((token_b,), first, rest)


def _reduce_grads(big, small, ids):
    got1, small_got1 = _pair_exchange(big, small)
    sums = [_pair_sum("pair_sum_" + n, a, b, ids) for n, a, b in zip(BIG, big, got1)]
    small_sum = _small_pair_sum(small, small_got1)
    got2, small_got2 = _chip_exchange([s[1] for s in sums], small_sum)
    halves = [_chip_sum("chip_sum_" + n, s[0], b, ids) for n, s, b in zip(BIG, sums, got2)]
    halves.append(_small_chip_sum(small_sum, small_got2, ids))
    return _half_swap(halves)


def kernel(x, ln1_g, w_in, b_in, rpb, w_att_o, conv_w, conv_b, w_rg_a, b_rg_a, w_rg_i, b_rg_i, lru_lambda, w_rec_o, w_out, ln2_g, w_ff1, w_ff2, lnf_g, loss_target, m_ln1_g, m_w_in, m_b_in, m_rpb, m_w_att_o, m_conv_w, m_conv_b, m_w_rg_a, m_b_rg_a, m_w_rg_i, m_b_rg_i, m_lru_lambda, m_w_rec_o, m_w_out, m_ln2_g, m_w_ff1, m_w_ff2, m_lnf_g, v_ln1_g, v_w_in, v_b_in, v_rpb, v_w_att_o, v_conv_w, v_conv_b, v_w_rg_a, v_b_rg_a, v_w_rg_i, v_b_rg_i, v_lru_lambda, v_w_rec_o, v_w_out, v_ln2_g, v_w_ff1, v_w_ff2, v_lnf_g):
    w = dict(ln1_g=ln1_g, w_in=w_in, b_in=b_in, rpb=rpb, w_att_o=w_att_o, conv_w=conv_w, conv_b=conv_b,
             w_rg_a=w_rg_a, b_rg_a=b_rg_a, w_rg_i=w_rg_i, b_rg_i=b_rg_i, lru_lambda=lru_lambda, w_rec_o=w_rec_o,
             w_out=w_out, ln2_g=ln2_g, w_ff1=w_ff1, w_ff2=w_ff2, lnf_g=lnf_g)
    m = dict(ln1_g=m_ln1_g, w_in=m_w_in, b_in=m_b_in, rpb=m_rpb, w_att_o=m_w_att_o, conv_w=m_conv_w,
             conv_b=m_conv_b, w_rg_a=m_w_rg_a, b_rg_a=m_b_rg_a, w_rg_i=m_w_rg_i, b_rg_i=m_b_rg_i,
             lru_lambda=m_lru_lambda, w_rec_o=m_w_rec_o, w_out=m_w_out, ln2_g=m_ln2_g, w_ff1=m_w_ff1,
             w_ff2=m_w_ff2, lnf_g=m_lnf_g)
    v = dict(ln1_g=v_ln1_g, w_in=v_w_in, b_in=v_b_in, rpb=v_rpb, w_att_o=v_w_att_o, conv_w=v_conv_w,
             conv_b=v_conv_b, w_rg_a=v_w_rg_a, b_rg_a=v_b_rg_a, w_rg_i=v_w_rg_i, b_rg_i=v_b_rg_i,
             lru_lambda=v_lru_lambda, w_rec_o=v_w_rec_o, w_out=v_w_out, ln2_g=v_ln2_g, w_ff1=v_w_ff1,
             w_ff2=v_w_ff2, lnf_g=v_lnf_g)
    chip = 2 * lax.axis_index("x") + lax.axis_index("y")
    ids = jnp.stack([chip, lax.axis_index("c")]).astype(jnp.int32)

    p, late = _gather_weights(w, chip)
    loss, grad_x, g = _local_step(x, loss_target, p, late)

    big = [g["w_in"], g["w_att_o"], g["w_rec_o"].reshape(N_CHIPS, D // N_CHIPS, D),
           g["w_out"].reshape(N_CHIPS, D // N_CHIPS, D), g["w_ff1"], g["w_ff2"].reshape(N_CHIPS, D_FF // N_CHIPS, D)]
    small, sizes = _pack_small(g, loss)
    swapped = _reduce_grads(big, small, ids)
    shapes = {n: g[n].shape for n in SMALL}
    gsmall, loss = _unpack_small(swapped[len(BIG)], sizes, shapes)

    grad = dict(zip(BIG, swapped[:len(BIG)]))
    for n in SMALL:
        gn = gsmall[n]
        if n in SHARDED_VECS:
            gn = lax.dynamic_slice_in_dim(gn, chip * (D // N_CHIPS), D // N_CHIPS, axis=1)
        grad[n] = gn

    out_grad, out_delta, out_m, out_v = {}, {}, {}, {}
    for n in ORDER:
        shape = w[n].shape
        if n in BIG:
            two_d = grad[n].shape
            rb = 256
        else:
            two_d = (int(np.prod(shape[:-1])), shape[-1])
            rb = None
        gn = grad[n].reshape(two_d)
        d, nm, nv = _adamw("adamw_" + n, w[n].reshape(two_d), gn, m[n].reshape(two_d), v[n].reshape(two_d), rb)
        out_grad[n], out_delta[n], out_m[n], out_v[n] = (gn.reshape(shape), d.reshape(shape), nm.reshape(shape),
                                                         nv.reshape(shape))
    return (loss, grad_x, *[out_grad[n] for n in ORDER], *[out_delta[n] for n in ORDER],
            *[out_m[n] for n in ORDER], *[out_v[n] for n in ORDER])
```

```python
import functools

import numpy as np
import jax
import jax.numpy as jnp
from jax import lax
from jax.experimental import pallas as pl
from jax.experimental.pallas import tpu as pltpu

F32 = jnp.float32
BF16 = jnp.bfloat16

T = 2048
D = 1024
D_ATT = 512
D_IN = 5632
D_FF = 4096
N_HEADS = 8
HEAD_DIM = 64
GRID_W = 64
N_ROWS = T // GRID_W
WIN_H = 8
WIN_W = 16
KEYS = WIN_H * GRID_W
N_CHIPS = 4
EPS = 1e-6
LRU_C = 8.0
SCALE = HEAD_DIM ** -0.5
REC_CB = 256
REC_CHUNK = 256
PAD = 8

ADAM_LR = 0.001
ADAM_B1 = 0.9
ADAM_B2 = 0.999
ADAM_EPS = 1e-08
ADAM_WD = 0.01
ADAM_STEP = 10

VMEM_LIMIT = 56 * 1024 * 1024

NN = (((1,), (0,)), ((), ()))
NT = (((1,), (1,)), ((), ()))
TN = (((0,), (0,)), ((), ()))
MESH = pl.DeviceIdType.MESH


def _params(sem=None):
    return pltpu.CompilerParams(dimension_semantics=sem, vmem_limit_bytes=VMEM_LIMIT)


def _dot(a, b, dims):
    return lax.dot_general(a, b, dims, preferred_element_type=F32)


def _sigmoid(x):
    return 0.5 * jnp.tanh(0.5 * x) + 0.5


def _matmul(name, a, b, *, dims, grid, a_spec, b_spec, out_shapes, out_specs, acc_shape,
            extras=(), extra_specs=(), epilogue=None, colsum_spec=None, colsum_shape=None):
    nk = grid[2]
    n_extra = len(extras)
    n_out = len(out_shapes)
    with_colsum = colsum_spec is not None

    def body(a_ref, b_ref, *rest):
        ex = rest[:n_extra]
        outs = rest[n_extra:n_extra + n_out]
        pos = n_extra + n_out
        cs_out = rest[pos] if with_colsum else None
        pos += 1 if with_colsum else 0
        acc = rest[pos]
        cs_acc = rest[pos + 1] if with_colsum else None
        k = pl.program_id(2)

        @pl.when(k == 0)
        def _():
            acc[...] = jnp.zeros_like(acc)
            if with_colsum:
                cs_acc[...] = jnp.zeros_like(cs_acc)

        bv = b_ref[...]
        acc[...] += _dot(a_ref[...].astype(BF16), bv.astype(BF16), dims)
        if with_colsum:
            cs_acc[...] += jnp.sum(bv.astype(F32), axis=0, keepdims=True)

        @pl.when(k == nk - 1)
        def _():
            r = acc[...]
            if epilogue is None:
                outs[0][...] = r.astype(outs[0].dtype)
            else:
                epilogue(r, ex, outs)
            if with_colsum:
                cs_out[...] = cs_acc[...]

    shapes = list(out_shapes)
    specs = list(out_specs)
    scratch = [pltpu.VMEM(acc_shape, F32)]
    if with_colsum:
        shapes.append(colsum_shape)
        specs.append(colsum_spec)
        scratch.append(pltpu.VMEM((1, acc_shape[1]), F32))
    res = pl.pallas_call(
        body, name=name, grid=grid,
        in_specs=[a_spec, b_spec, *extra_specs],
        out_specs=specs, out_shape=shapes, scratch_shapes=scratch,
        compiler_params=_params(("parallel", "parallel", "arbitrary")),
    )(a, b, *extras)
    return res


def _sds(shape, dtype):
    return jax.ShapeDtypeStruct(shape, dtype)


TM = 512
NI = T // TM


def _mm_nn_cols(name, a, wg, out_dtype, *, bias=None, extras=(), extra_specs=(), epilogue=None,
                out_shapes=None, out_specs=None):
    k_dim, n4 = wg.shape[1], wg.shape[2]
    ex, exs = list(extras), list(extra_specs)
    if bias is not None:
        ex = [bias] + ex
        exs = [pl.BlockSpec((1, n4), lambda j, i, k: (0, j))] + exs
        user_ep = epilogue

        def epilogue(r, e, outs):
            r = r + e[0][...]
            if user_ep is None:
                outs[0][...] = r.astype(outs[0].dtype)
            else:
                user_ep(r, e[1:], outs)
    if out_shapes is None:
        out_shapes = [_sds((T, N_CHIPS * n4), out_dtype)]
        out_specs = [pl.BlockSpec((TM, n4), lambda j, i, k: (i, j))]
    return _matmul(
        name, a, wg, dims=NN, grid=(N_CHIPS, NI, 1),
        a_spec=pl.BlockSpec((TM, k_dim), lambda j, i, k: (i, 0)),
        b_spec=pl.BlockSpec((None, k_dim, n4), lambda j, i, k: (j, 0, 0)),
        out_shapes=out_shapes, out_specs=out_specs, acc_shape=(TM, n4),
        extras=ex, extra_specs=exs, epilogue=epilogue)


def _mm_nn_rows(name, a, w, out_dtype, *, tk, extras=(), extra_specs=(), epilogue=None):
    k_dim, n = w.shape
    return _matmul(
        name, a, w, dims=NN, grid=(NI, 1, k_dim // tk),
        a_spec=pl.BlockSpec((TM, tk), lambda i, j, k: (i, k)),
        b_spec=pl.BlockSpec((tk, n), lambda i, j, k: (k, 0)),
        out_shapes=[_sds((T, n), out_dtype)],
        out_specs=[pl.BlockSpec((TM, n), lambda i, j, k: (i, 0))], acc_shape=(TM, n),
        extras=extras, extra_specs=extra_specs, epilogue=epilogue)


def _mm_nt_cols(name, a, wg, out_dtype):
    k_dim, n4 = wg.shape[1], wg.shape[2]
    return _matmul(
        name, a, wg, dims=NT, grid=(NI, 1, N_CHIPS),
        a_spec=pl.BlockSpec((TM, n4), lambda i, j, k: (i, k)),
        b_spec=pl.BlockSpec((None, k_dim, n4), lambda i, j, k: (k, 0, 0)),
        out_shapes=[_sds((T, k_dim), out_dtype)],
        out_specs=[pl.BlockSpec((TM, k_dim), lambda i, j, k: (i, 0))], acc_shape=(TM, k_dim))


def _mm_nt_rows(name, a, w, out_dtype, *, tn, extras=(), extra_specs=(), epilogue=None):
    k_dim, n = w.shape
    return _matmul(
        name, a, w, dims=NT, grid=(k_dim // tn, NI, 1),
        a_spec=pl.BlockSpec((TM, n), lambda j, i, k: (i, 0)),
        b_spec=pl.BlockSpec((tn, n), lambda j, i, k: (j, 0)),
        out_shapes=[_sds((T, k_dim), out_dtype)],
        out_specs=[pl.BlockSpec((TM, tn), lambda j, i, k: (i, j))], acc_shape=(TM, tn),
        extras=extras, extra_specs=extra_specs, epilogue=epilogue)


def _mm_tn_cols(name, a, g, n4, *, colsum=False):
    k_dim = a.shape[1]
    kw = {}
    if colsum:
        kw = dict(colsum_spec=pl.BlockSpec((1, n4), lambda j, i, k: (0, j)),
                  colsum_shape=_sds((1, N_CHIPS * n4), F32))
    return _matmul(
        name, a, g, dims=TN, grid=(N_CHIPS, 1, NI),
        a_spec=pl.BlockSpec((TM, k_dim), lambda j, i, k: (k, 0)),
        b_spec=pl.BlockSpec((TM, n4), lambda j, i, k: (k, j)),
        out_shapes=[_sds((N_CHIPS, k_dim, n4), F32)],
        out_specs=[pl.BlockSpec((None, k_dim, n4), lambda j, i, k: (j, 0, 0))],
        acc_shape=(k_dim, n4), **kw)


def _mm_tn_rows(name, a, g, *, tm):
    k_dim, n = a.shape[1], g.shape[1]
    return _matmul(
        name, a, g, dims=TN, grid=(k_dim // tm, 1, NI),
        a_spec=pl.BlockSpec((TM, tm), lambda j, i, k: (k, j)),
        b_spec=pl.BlockSpec((TM, n), lambda j, i, k: (k, 0)),
        out_shapes=[_sds((k_dim, n), F32)],
        out_specs=[pl.BlockSpec((tm, n), lambda j, i, k: (j, 0))], acc_shape=(tm, n))


TE = 256
NE = T // TE
_ROW = pl.BlockSpec((TE, D), lambda i: (i, 0))
_VEC = pl.BlockSpec((1, D), lambda i: (0, 0))


def _rms_fwd(name, x, g, after=()):
    def body(x_ref, g_ref, *rest):
        h_ref = rest[-1]
        xv = x_ref[...]
        rstd = lax.rsqrt(jnp.mean(xv * xv, axis=-1, keepdims=True) + EPS)
        h_ref[...] = (xv * rstd * g_ref[...]).astype(BF16)

    return pl.pallas_call(body, name=name, grid=(NE,), in_specs=[_ROW, _VEC] + [_ANY] * len(after), out_specs=_ROW,
                          out_shape=_sds((T, D), BF16), compiler_params=_params(("parallel",)))(x, g, *after)


def _rms_bwd(name, dh, x, g, dres, after=()):
    def body(dh_ref, x_ref, g_ref, dres_ref, *rest):
        dx_ref, dg_ref = rest[-2:]
        xv = x_ref[...]
        rstd = lax.rsqrt(jnp.mean(xv * xv, axis=-1, keepdims=True) + EPS)
        xhat = xv * rstd
        dhv = dh_ref[...]
        dy = dhv * g_ref[...]
        dx_ref[...] = dres_ref[...] + rstd * (dy - xhat * jnp.mean(dy * xhat, axis=-1, keepdims=True))

        @pl.when(pl.program_id(0) == 0)
        def _():
            dg_ref[...] = jnp.zeros_like(dg_ref)

        dg_ref[...] += jnp.sum(dhv * xhat, axis=0, keepdims=True)

    return pl.pallas_call(body, name=name, grid=(NE,), in_specs=[_ROW, _ROW, _VEC, _ROW] + [_ANY] * len(after),
                          out_specs=[_ROW, _VEC], out_shape=[_sds((T, D), F32), _sds((1, D), F32)],
                          compiler_params=_params(("arbitrary",)))(dh, x, g, dres, *after)


def _loss_head(x2, target, g):
    def body(x_ref, t_ref, g_ref, loss_ref, dx_ref, dg_ref):
        xv = x_ref[...]
        rstd = lax.rsqrt(jnp.mean(xv * xv, axis=-1, keepdims=True) + EPS)
        xhat = xv * rstd
        gv = g_ref[...]
        err = xhat * gv - t_ref[...]
        dy = err * (1.0 / D)
        dxh = dy * gv
        dx_ref[...] = rstd * (dxh - xhat * jnp.mean(dxh * xhat, axis=-1, keepdims=True))

        @pl.when(pl.program_id(0) == 0)
        def _():
            dg_ref[...] = jnp.zeros_like(dg_ref)
            loss_ref[...] = jnp.zeros_like(loss_ref)

        dg_ref[...] += jnp.sum(dy * xhat, axis=0, keepdims=True)
        loss_ref[...] += (0.5 / D) * jnp.sum(jnp.sum(err * err, axis=1, keepdims=True), axis=0, keepdims=True)

    return pl.pallas_call(
        body, name="loss_head", grid=(NE,), in_specs=[_ROW, _ROW, _VEC],
        out_specs=[pl.BlockSpec((1, 1), lambda i: (0, 0)), _ROW, _VEC],
        out_shape=[_sds((1, 1), F32), _sds((T, D), F32), _sds((1, D), F32)],
        compiler_params=_params(("arbitrary",)))(x2, target, g)


MW = 512
_G_ATT_BLK = 3584 // MW
_G_REC_BLK = 4608 // MW


def _merge_specs():
    y = pl.BlockSpec((TM, MW), lambda i, j: (i, j))
    ga = pl.BlockSpec((TM, MW), lambda i, j: (i, _G_ATT_BLK + j))
    gr = pl.BlockSpec((TM, MW), lambda i, j: (i, _G_REC_BLK + j))
    return y, ga, gr


def _merge_fwd(y_att, y_rec, z):
    y, ga, gr = _merge_specs()

    def body(ya_ref, yr_ref, ga_ref, gr_ref, m_ref):
        m = _sigmoid(ga_ref[...]) * ya_ref[...] + _sigmoid(gr_ref[...]) * yr_ref[...]
        m_ref[...] = m.astype(BF16)

    return pl.pallas_call(body, name="merge_fwd", grid=(NI, D // MW), in_specs=[y, y, ga, gr], out_specs=y,
                          out_shape=_sds((T, D), BF16),
                          compiler_params=_params(("parallel", "parallel")))(y_att, y_rec, z, z)


def _merge_bwd(dm, y_att, y_rec, z):
    y, ga, gr = _merge_specs()

    def body(dm_ref, ya_ref, yr_ref, ga_ref, gr_ref, dya_ref, dyr_ref, dga_ref, dgr_ref):
        dmv = dm_ref[...]
        sa = _sigmoid(ga_ref[...])
        sr = _sigmoid(gr_ref[...])
        dya_ref[...] = (dmv * sa).astype(BF16)
        dyr_ref[...] = (dmv * sr).astype(BF16)
        dga_ref[...] = (dmv * ya_ref[...] * sa * (1.0 - sa)).astype(BF16)
        dgr_ref[...] = (dmv * yr_ref[...] * sr * (1.0 - sr)).astype(BF16)

    return pl.pallas_call(body, name="merge_bwd", grid=(NI, D // MW), in_specs=[y, y, y, ga, gr],
                          out_specs=[y, y, y, y], out_shape=[_sds((T, D), BF16)] * 4,
                          compiler_params=_params(("parallel", "parallel")))(dm, y_att, y_rec, z, z)


HP = 2 * HEAD_DIM
N_HP = N_HEADS // 2
ATT_UNROLL_FWD = 8
ATT_UNROLL_BWD = 4


def _window_maps():
    diag = np.zeros((GRID_W * GRID_W, 128), np.float32)
    for qc in range(GRID_W):
        w0 = min(max(qc - WIN_W // 2, 0), GRID_W - WIN_W)
        for kc in range(w0, w0 + WIN_W):
            diag[qc * GRID_W + kc, kc - qc + WIN_W - 1] = 1.0
    return diag, diag.sum(axis=1)[None, :]


def _split3(x):
    a = x.astype(BF16)
    r = x - a.astype(F32)
    b = r.astype(BF16)
    c = (r - b.astype(F32)).astype(BF16)
    return a, b, c


N_DROW = 2 * WIN_H - 1
N_DPAIR = N_DROW - 1


def _bias_pairs(rpb):
    diag, valid = _window_maps()
    r2 = jnp.pad(rpb.reshape(N_HEADS * N_DROW, 2 * WIN_W - 1),
                 ((0, 128 - N_HEADS * N_DROW), (0, 128 - (2 * WIN_W - 1))))

    def body(r_ref, d_ref, v_ref, o_ref):
        dv = d_ref[...]
        t = sum(_dot(part, dv, NN) for part in _split3(r_ref[...]))
        o_ref[...] = jnp.where(v_ref[...] > 0.0, t, -1e30)

    t = pl.pallas_call(body, name="rpb_expand", out_shape=_sds((128, GRID_W * GRID_W), F32),
                       compiler_params=_params())(r2, jnp.asarray(diag.T, BF16), jnp.asarray(valid, F32))
    t = t[:N_HEADS * N_DROW].reshape(N_HEADS, N_DROW, GRID_W, GRID_W)
    return jnp.concatenate([t[:, :N_DPAIR], t[:, 1:]], axis=-1)


def _row_bias(tb_ref, hh, d0):
    return jnp.concatenate([tb_ref[hh, d0 + 2 * ii] for ii in range(WIN_H // 2)], axis=1)


def _row_window(r):
    rs = jnp.clip(r - WIN_H // 2, 0, N_ROWS - WIN_H)
    return pl.multiple_of(r * GRID_W, GRID_W), pl.multiple_of(rs * GRID_W, GRID_W), rs - r + (WIN_H - 1)


def _split_heads(src_ref, dst_ref, scale=None):
    for hh in range(2):
        v = src_ref[:, hh * HEAD_DIM:(hh + 1) * HEAD_DIM]
        dst_ref[hh] = (v if scale is None else v * scale).astype(BF16)


def _attn_items(qb_ref, kb_ref, vb_ref, tb_ref, first_row, n_rows):
    wins = [_row_window(first_row + u) for u in range(n_rows)]
    items = [(u, hh) for u in range(n_rows) for hh in range(2)]
    q = [qb_ref[hh, pl.ds(wins[u][0], GRID_W), :] for u, hh in items]
    k = [kb_ref[hh, pl.ds(wins[u][1], KEYS), :] for u, hh in items]
    v = [vb_ref[hh, pl.ds(wins[u][1], KEYS), :] for u, hh in items]
    s = [_dot(qi, ki, NT) + _row_bias(tb_ref, hh, wins[u][2]) for qi, ki, (u, hh) in zip(q, k, items)]
    m = [jnp.max(si, axis=-1, keepdims=True) for si in s]
    e = [jnp.exp(si - mi) for si, mi in zip(s, m)]
    inv = [1.0 / jnp.sum(ei, axis=-1, keepdims=True) for ei in e]
    p = [ei * li for ei, li in zip(e, inv)]
    return wins, items, q, k, v, p


def _attn_in_specs():
    q = pl.BlockSpec((T, HP), lambda p: (0, p))
    k = pl.BlockSpec((T, HP), lambda p: (0, N_HP + p))
    v = pl.BlockSpec((T, HP), lambda p: (0, 2 * N_HP + p))
    tb = pl.BlockSpec((2, N_DPAIR, GRID_W, HP), lambda p: (p, 0, 0, 0))
    return q, k, v, tb


_HEAD_SCRATCH = pltpu.VMEM((2, T, HEAD_DIM), BF16)


def _attn_fwd(z, tb):
    def body(q_ref, k_ref, v_ref, tb_ref, o_ref, qb_ref, kb_ref, vb_ref):
        _split_heads(q_ref, qb_ref, SCALE)
        _split_heads(k_ref, kb_ref)
        _split_heads(v_ref, vb_ref)

        def rows(it, carry):
            wins, items, _, _, v, p = _attn_items(qb_ref, kb_ref, vb_ref, tb_ref, it * ATT_UNROLL_FWD, ATT_UNROLL_FWD)
            o = [_dot(pi.astype(BF16), vi, NN) for pi, vi in zip(p, v)]
            for u, (q0, _, _) in enumerate(wins):
                o_ref[pl.ds(q0, GRID_W), :] = jnp.concatenate(o[2 * u:2 * u + 2], axis=1).astype(BF16)
            return carry

        lax.fori_loop(0, N_ROWS // ATT_UNROLL_FWD, rows, 0)

    blk = pl.BlockSpec((T, HP), lambda p: (0, p))
    return pl.pallas_call(
        body, name="attn_fwd", grid=(N_HP,), in_specs=list(_attn_in_specs()), out_specs=blk,
        out_shape=_sds((T, D_ATT), BF16), scratch_shapes=[_HEAD_SCRATCH] * 3,
        compiler_params=_params(("parallel",)))(z, z, z, tb)


def _attn_bwd(z, tb, d_att, after=()):
    def body(q_ref, k_ref, v_ref, tb_ref, do_ref, *rest):
        dq_ref, dk_ref, dv_ref, ds_ref, qb_ref, kb_ref, vb_ref, dob_ref, dka_ref, dva_ref = rest[len(after):]
        _split_heads(q_ref, qb_ref, SCALE)
        _split_heads(k_ref, kb_ref)
        _split_heads(v_ref, vb_ref)
        _split_heads(do_ref, dob_ref)
        dka_ref[...] = jnp.zeros_like(dka_ref)
        dva_ref[...] = jnp.zeros_like(dva_ref)
        ds_ref[...] = jnp.zeros_like(ds_ref)

        def rows(it, carry):
            wins, items, q, k, v, p = _attn_items(qb_ref, kb_ref, vb_ref, tb_ref, it * ATT_UNROLL_BWD, ATT_UNROLL_BWD)
            do = [dob_ref[hh, pl.ds(wins[u][0], GRID_W), :] for u, hh in items]
            dv = [_dot(pi.astype(BF16), di, TN) for pi, di in zip(p, do)]
            dp = [_dot(di, vi, NT) for di, vi in zip(do, v)]
            ds = [pi * (dpi - jnp.sum(dpi * pi, axis=-1, keepdims=True)) for pi, dpi in zip(p, dp)]
            dsb = [d.astype(BF16) for d in ds]
            dq = [_dot(d, ki, NN) * SCALE for d, ki in zip(dsb, k)]
            dk = [_dot(d, qi, TN) for d, qi in zip(dsb, q)]
            for d, (u, hh) in zip(ds, items):
                for ii in range(WIN_H // 2):
                    ds_ref[hh, wins[u][2] + 2 * ii] += d[:, ii * HP:(ii + 1) * HP]
            for u, (q0, k0, _) in enumerate(wins):
                dq_ref[pl.ds(q0, GRID_W), :] = jnp.concatenate(dq[2 * u:2 * u + 2], axis=1).astype(BF16)
                dka_ref[pl.ds(k0, KEYS), :] += jnp.concatenate(dk[2 * u:2 * u + 2], axis=1)
                dva_ref[pl.ds(k0, KEYS), :] += jnp.concatenate(dv[2 * u:2 * u + 2], axis=1)
            return carry

        lax.fori_loop(0, N_ROWS // ATT_UNROLL_BWD, rows, 0)
        dk_ref[...] = dka_ref[...].astype(BF16)
        dv_ref[...] = dva_ref[...].astype(BF16)

    blk = pl.BlockSpec((T, HP), lambda p: (0, p))
    q, k, v, tbs = _attn_in_specs()
    return pl.pallas_call(
        body, name="attn_bwd", grid=(N_HP,), in_specs=[q, k, v, tbs, blk] + [_ANY] * len(after),
        out_specs=[blk, blk, blk, tbs],
        out_shape=[_sds((T, D_ATT), BF16)] * 3 + [_sds((N_HEADS, N_DPAIR, GRID_W, HP), F32)],
        scratch_shapes=[_HEAD_SCRATCH] * 4 + [pltpu.VMEM((T, HP), F32), pltpu.VMEM((T, HP), F32)],
        compiler_params=_params(("parallel",)))(z, z, z, tb, d_att, *after)


def _rpb_grad(ds_acc):
    a = ds_acc.reshape(N_HEADS, N_DPAIR, GRID_W, 2, GRID_W).transpose(0, 1, 3, 2, 4)
    a = jnp.pad(a.reshape(N_HEADS * N_DPAIR * 2, GRID_W * GRID_W), ((0, 256 - N_HEADS * N_DPAIR * 2), (0, 0)))
    sel = np.zeros((128, 256), np.float32)
    for h in range(N_HEADS):
        for pair in range(N_DPAIR):
            for half in range(2):
                sel[h * N_DROW + pair + half, (h * N_DPAIR + pair) * 2 + half] = 1.0
    diag, _ = _window_maps()

    def body(a_ref, sel_ref, diag_ref, o_ref):
        selv = sel_ref[...]
        g = sum(_dot(selv, part, NN) for part in _split3(a_ref[...]))
        diagv = diag_ref[...]
        o_ref[...] = sum(_dot(part, diagv, NN) for part in _split3(g))

    out = pl.pallas_call(body, name="rpb_grad", out_shape=_sds((128, 128), F32),
                         compiler_params=_params())(a, jnp.asarray(sel, BF16), jnp.asarray(diag, BF16))
    return out[:N_HEADS * N_DROW, :2 * WIN_W - 1].reshape(N_HEADS, N_DROW, 2 * WIN_W - 1)


N_CB = D // REC_CB
N_CHUNK = T // REC_CHUNK
N_TILE = T // 8
_U_BLK = 1536 // REC_CB
_Y_BLK = 2560 // REC_CB


def _block_diag(w):
    per = REC_CB // 64
    wt = w.reshape(2, N_CB, per, 64, 64)
    eye = jnp.eye(per, dtype=w.dtype)
    full = wt[:, :, :, :, None, :] * eye[None, None, :, None, :, None]
    return full.reshape(2, N_CB, REC_CB, REC_CB).astype(BF16)


def _block_diag_grad(g):
    per = REC_CB // 64
    g6 = g.reshape(2, N_CB, per, 64, per, 64)
    return jnp.stack([g6[:, :, p, :, p, :] for p in range(per)], axis=2).reshape(2, 16, 64, 64)


def _gelu(x):
    c = 0.7978845608028654
    return 0.5 * x * (1.0 + jnp.tanh(c * (x + 0.044715 * x * x * x)))


def _gelu_grad(x):
    c = 0.7978845608028654
    th = jnp.tanh(c * (x + 0.044715 * x * x * x))
    return 0.5 * (1.0 + th) + 0.5 * x * (1.0 - th * th) * c * (1.0 + 3.0 * 0.044715 * x * x)


def _softplus_neg(lam):
    x = -lam
    e = jnp.exp(-jnp.abs(x))
    w = 1.0 + e
    l1p = jnp.where(w == 1.0, e, jnp.log(w) * e / (w - 1.0))
    return jnp.maximum(x, 0.0) + l1p


def _one_minus_exp(x):
    poly = x * (1.0 + x * (1 / 2 + x * (1 / 6 + x * (1 / 24 + x * (1 / 120 + x * (1 / 720))))))
    return jnp.where(x > -0.125, -poly, 1.0 - jnp.exp(x))


def _conv_taps(pad_ref, t0, w, sign):
    out = None
    for j in range(4):
        term = w[j:j + 1, :] * pad_ref[pl.ds(PAD + t0 + sign * (j - 2), REC_CHUNK), :]
        out = term if out is None else out + term
    return out


def _gates(u, wa, wi, ba, bi, sp):
    ub = u.astype(BF16)
    r = _sigmoid(_dot(ub, wa, NN) + ba)
    i = _sigmoid(_dot(ub, wi, NN) + bi)
    log_a = -LRU_C * r * sp
    a = jnp.exp(log_a)
    mult = jnp.sqrt(jnp.maximum(_one_minus_exp(2.0 * log_a), 0.0))
    return r, i, a, mult


def _tile_scan(a, b, sub, reverse):
    for s in (1, 2, 4):
        if reverse:
            a_s, b_s, m = pltpu.roll(a, 8 - s, 0), pltpu.roll(b, 8 - s, 0), sub < 8 - s
        else:
            a_s, b_s, m = pltpu.roll(a, s, 0), pltpu.roll(b, s, 0), sub >= s
        b = jnp.where(m, a * b_s + b, b)
        a = jnp.where(m, a * a_s, a)
    return a, b


def _last_row(x, sub, row):
    return jnp.broadcast_to(jnp.sum(jnp.where(sub == row, x, 0.0), axis=0, keepdims=True), x.shape)


def _rec_prologue(up_ref, cw_ref, cb_ref, wa_ref, wi_ref, ba_ref, bi_ref, lam_ref,
                  upad_ref, u_ref, a_refs, h_refs):
    cb = up_ref.shape[1]
    zeros = jnp.zeros((PAD, cb), F32)
    upad_ref[pl.ds(0, PAD), :] = zeros
    upad_ref[pl.ds(PAD + T, PAD), :] = zeros
    upad_ref[pl.ds(PAD, T), :] = up_ref[...]
    cw = cw_ref[...]
    sp = _softplus_neg(lam_ref[...])
    for c in range(N_CHUNK):
        t0 = c * REC_CHUNK
        u = cb_ref[...] + _conv_taps(upad_ref, t0, cw, 1)
        u_ref[pl.ds(t0, REC_CHUNK), :] = u
        for d in range(2):
            _, i, a, mult = _gates(u, wa_ref[d], wi_ref[d], ba_ref[d:d + 1, :], bi_ref[d:d + 1, :], sp[d:d + 1, :])
            a_refs[d][pl.ds(t0, REC_CHUNK), :] = a
            h_refs[d][pl.ds(t0, REC_CHUNK), :] = mult * (i * u)

    sub = lax.broadcasted_iota(jnp.int32, (8, cb), 0)

    def tile(k, carry):
        cf, cr = carry
        tf = pl.multiple_of(k * 8, 8)
        tr = pl.multiple_of((N_TILE - 1 - k) * 8, 8)
        af, bf = _tile_scan(a_refs[0][pl.ds(tf, 8), :], h_refs[0][pl.ds(tf, 8), :], sub, False)
        hf = af * cf + bf
        h_refs[0][pl.ds(tf, 8), :] = hf
        ar, br = _tile_scan(a_refs[1][pl.ds(tr, 8), :], h_refs[1][pl.ds(tr, 8), :], sub, True)
        hr = ar * cr + br
        h_refs[1][pl.ds(tr, 8), :] = hr
        return _last_row(hf, sub, 7), _last_row(hr, sub, 0)

    z8 = jnp.zeros((8, cb), F32)
    lax.fori_loop(0, N_TILE, tile, (z8, z8))
    return sp


def _rec_specs():
    up = pl.BlockSpec((T, REC_CB), lambda c: (0, _U_BLK + c))
    yb = pl.BlockSpec((T, REC_CB), lambda c: (0, _Y_BLK + c))
    cw = pl.BlockSpec((4, REC_CB), lambda c: (0, c))
    cbias = pl.BlockSpec((1, REC_CB), lambda c: (0, c))
    wbd = pl.BlockSpec((2, None, REC_CB, REC_CB), lambda c: (0, c, 0, 0))
    vec2 = pl.BlockSpec((2, REC_CB), lambda c: (0, c))
    col = pl.BlockSpec((T, REC_CB), lambda c: (0, c))
    return up, yb, cw, cbias, wbd, vec2, col


def _rec_fwd(z, conv_w, conv_b, wa, wi, ba, bi, lam):
    up, yb, cw, cbias, wbd, vec2, col = _rec_specs()

    def body(up_ref, yb_ref, cw_ref, cb_ref, wa_ref, wi_ref, ba_ref, bi_ref, lam_ref, g_ref,
             upad_ref, u_ref, af_ref, ar_ref, hf_ref, hr_ref):
        _rec_prologue(up_ref, cw_ref, cb_ref, wa_ref, wi_ref, ba_ref, bi_ref, lam_ref,
                      upad_ref, u_ref, (af_ref, ar_ref), (hf_ref, hr_ref))

        def chunk(c, carry):
            t0 = pl.multiple_of(c * REC_CHUNK, REC_CHUNK)
            rows = pl.ds(t0, REC_CHUNK)
            g_ref[rows, :] = ((hf_ref[rows, :] + hr_ref[rows, :]) * _gelu(yb_ref[rows, :])).astype(BF16)
            return carry

        lax.fori_loop(0, N_CHUNK, chunk, 0)

    full = pltpu.VMEM((T, REC_CB), F32)
    return pl.pallas_call(
        body, name="rec_fwd", grid=(N_CB,),
        in_specs=[up, yb, cw, cbias, wbd, wbd, vec2, vec2, vec2], out_specs=col,
        out_shape=_sds((T, D), BF16),
        scratch_shapes=[pltpu.VMEM((T + 2 * PAD, REC_CB), F32), full, full, full, full, full],
        compiler_params=_params(("parallel",)))(z, z, conv_w, conv_b, wa, wi, ba, bi, lam)


def _rec_bwd(z, dg, conv_w, conv_b, wa, wi, ba, bi, lam):
    up, yb, cw, cbias, wbd, vec2, col = _rec_specs()

    def body(up_ref, yb_ref, dg_ref, cw_ref, cb_ref, wa_ref, wi_ref, ba_ref, bi_ref, lam_ref,
             dup_ref, dyb_ref, dcw_ref, dcb_ref, dwa_ref, dwi_ref, dba_ref, dbi_ref, dlam_ref,
             upad_ref, u_ref, af_ref, ar_ref, hf_ref, hr_ref, dh_ref, gf_ref, gr_ref, daf_ref, dar_ref, dupad_ref):
        a_refs, h_refs = (af_ref, ar_ref), (hf_ref, hr_ref)
        g_refs, da_refs = (gf_ref, gr_ref), (daf_ref, dar_ref)
        sp = _rec_prologue(up_ref, cw_ref, cb_ref, wa_ref, wi_ref, ba_ref, bi_ref, lam_ref,
                           upad_ref, u_ref, a_refs, h_refs)
        cb = up_ref.shape[1]

        def gate_chunk(c, carry):
            t0 = pl.multiple_of(c * REC_CHUNK, REC_CHUNK)
            rows = pl.ds(t0, REC_CHUNK)
            y = yb_ref[rows, :]
            dgv = dg_ref[rows, :].astype(F32)
            dh_ref[rows, :] = dgv * _gelu(y)
            dyb_ref[rows, :] = (dgv * (hf_ref[rows, :] + hr_ref[rows, :]) * _gelu_grad(y)).astype(BF16)
            return carry

        lax.fori_loop(0, N_CHUNK, gate_chunk, 0)

        sub = lax.broadcasted_iota(jnp.int32, (8, cb), 0)

        def tile(k, carry):
            cf, cr = carry
            kf = N_TILE - 1 - k
            tf = pl.multiple_of(kf * 8, 8)
            tnext = pl.multiple_of(jnp.minimum(kf + 1, N_TILE - 1) * 8, 8)
            tprev = pl.multiple_of(jnp.maximum(kf - 1, 0) * 8, 8)
            a_t = af_ref[pl.ds(tf, 8), :]
            a_n = jnp.where(kf < N_TILE - 1, af_ref[pl.ds(tnext, 8), :], 0.0)
            a_sh = jnp.where(sub == 7, pltpu.roll(a_n, 7, 0), pltpu.roll(a_t, 7, 0))
            ca, cbb = _tile_scan(a_sh, dh_ref[pl.ds(tf, 8), :], sub, True)
            gf = ca * cf + cbb
            h_t = hf_ref[pl.ds(tf, 8), :]
            h_p = jnp.where(kf > 0, hf_ref[pl.ds(tprev, 8), :], 0.0)
            h_sh = jnp.where(sub == 0, pltpu.roll(h_p, 1, 0), pltpu.roll(h_t, 1, 0))
            gf_ref[pl.ds(tf, 8), :] = gf
            daf_ref[pl.ds(tf, 8), :] = gf * h_sh
            tr = pl.multiple_of(k * 8, 8)
            rnext = pl.multiple_of(jnp.minimum(k + 1, N_TILE - 1) * 8, 8)
            rprev = pl.multiple_of(jnp.maximum(k - 1, 0) * 8, 8)
            b_t = ar_ref[pl.ds(tr, 8), :]
            b_p = jnp.where(k > 0, ar_ref[pl.ds(rprev, 8), :], 0.0)
            b_sh = jnp.where(sub == 0, pltpu.roll(b_p, 1, 0), pltpu.roll(b_t, 1, 0))
            ra, rb = _tile_scan(b_sh, dh_ref[pl.ds(tr, 8), :], sub, False)
            gr = ra * cr + rb
            hr_t = hr_ref[pl.ds(tr, 8), :]
            hr_n = jnp.where(k < N_TILE - 1, hr_ref[pl.ds(rnext, 8), :], 0.0)
            hr_sh = jnp.where(sub == 7, pltpu.roll(hr_n, 7, 0), pltpu.roll(hr_t, 7, 0))
            gr_ref[pl.ds(tr, 8), :] = gr
            dar_ref[pl.ds(tr, 8), :] = gr * hr_sh
            return _last_row(gf, sub, 0), _last_row(gr, sub, 7)

        z8 = jnp.zeros((8, cb), F32)
        lax.fori_loop(0, N_TILE, tile, (z8, z8))

        zeros = jnp.zeros((PAD, cb), F32)
        dupad_ref[pl.ds(0, PAD), :] = zeros
        dupad_ref[pl.ds(PAD + T, PAD), :] = zeros
        dwa_ref[...] = jnp.zeros_like(dwa_ref)
        dwi_ref[...] = jnp.zeros_like(dwi_ref)
        dba_ref[...] = jnp.zeros_like(dba_ref)
        dbi_ref[...] = jnp.zeros_like(dbi_ref)
        dlam_ref[...] = jnp.zeros_like(dlam_ref)

        def grad_chunk(c, carry):
            t0 = pl.multiple_of(c * REC_CHUNK, REC_CHUNK)
            rows = pl.ds(t0, REC_CHUNK)
            u = u_ref[rows, :]
            ub = u.astype(BF16)
            du = jnp.zeros((REC_CHUNK, cb), F32)
            for d in range(2):
                r, i, a, mult = _gates(u, wa_ref[d], wi_ref[d], ba_ref[d:d + 1, :], bi_ref[d:d + 1, :], sp[d:d + 1, :])
                dbx = g_refs[d][rows, :]
                dmult = dbx * (i * u)
                diu = dbx * mult
                a2 = a * a
                dlog = da_refs[d][rows, :] * a - dmult * jnp.where(mult > 0.0, a2 / mult, 0.0)
                dpa = (dlog * (-LRU_C) * sp[d:d + 1, :]) * r * (1.0 - r)
                dpi = (diu * u) * i * (1.0 - i)
                dpab, dpib = dpa.astype(BF16), dpi.astype(BF16)
                du = du + diu * i + _dot(dpab, wa_ref[d], NT) + _dot(dpib, wi_ref[d], NT)
                dwa_ref[d] += _dot(ub, dpab, TN)
                dwi_ref[d] += _dot(ub, dpib, TN)
                dba_ref[d:d + 1, :] += jnp.sum(dpa, axis=0, keepdims=True)
                dbi_ref[d:d + 1, :] += jnp.sum(dpi, axis=0, keepdims=True)
                dlam_ref[d:d + 1, :] += jnp.sum(dlog * r, axis=0, keepdims=True)
            dupad_ref[pl.ds(PAD + t0, REC_CHUNK), :] = du
            return carry

        lax.fori_loop(0, N_CHUNK, grad_chunk, 0)
        dlam_ref[...] = dlam_ref[...] * (LRU_C * _sigmoid(-lam_ref[...]))

        cw = cw_ref[...]
        dcb = jnp.zeros((1, cb), F32)
        dcw = [jnp.zeros((1, cb), F32) for _ in range(4)]
        for c in range(N_CHUNK):
            t0 = c * REC_CHUNK
            du = dupad_ref[pl.ds(PAD + t0, REC_CHUNK), :]
            dcb = dcb + jnp.sum(du, axis=0, keepdims=True)
            for j in range(4):
                dcw[j] = dcw[j] + jnp.sum(du * upad_ref[pl.ds(PAD + t0 + j - 2, REC_CHUNK), :], axis=0, keepdims=True)
            dup_ref[pl.ds(t0, REC_CHUNK), :] = _conv_taps(dupad_ref, t0, cw, -1).astype(BF16)
        dcb_ref[...] = dcb
        dcw_ref[...] = jnp.concatenate(dcw, axis=0)

    full = pltpu.VMEM((T, REC_CB), F32)
    padded = pltpu.VMEM((T + 2 * PAD, REC_CB), F32)
    return pl.pallas_call(
        body, name="rec_bwd", grid=(N_CB,),
        in_specs=[up, yb, col, cw, cbias, wbd, wbd, vec2, vec2, vec2],
        out_specs=[col, col, cw, cbias, wbd, wbd, vec2, vec2, vec2],
        out_shape=[_sds((T, D), BF16), _sds((T, D), BF16), _sds((4, D), F32), _sds((1, D), F32),
                   _sds((2, N_CB, REC_CB, REC_CB), F32), _sds((2, N_CB, REC_CB, REC_CB), F32),
                   _sds((2, D), F32), _sds((2, D), F32), _sds((2, D), F32)],
        scratch_shapes=[padded, full, full, full, full, full, full, full, full, full, full, padded],
        compiler_params=_params(("parallel",)))(z, z, dg, conv_w, conv_b, wa, wi, ba, bi, lam)


def _local_step(x, target, p, late=None, on_grads=lambda tag, grads: ()):
    x = x.reshape(T, D)
    target = target.reshape(T, D)
    tb = _bias_pairs(p["rpb"])
    wa, wi = _block_diag(p["w_rg_a"]), _block_diag(p["w_rg_i"])

    h1 = _rms_fwd("rms1_fwd", x, p["ln1_g"], after=late[0] if late else ())
    if late:
        p = {**p, **late[1](h1)}
    rec_params = (p["conv_w"], p["conv_b"], wa, wi, p["b_rg_a"], p["b_rg_i"], p["lru_lambda"])
    (z,) = _mm_nn_cols("mm_z", h1, p["w_in"], F32, bias=p["b_in"])
    att = _attn_fwd(z, tb)
    g = _rec_fwd(z, *rec_params)
    if late:
        p = {**p, **late[2](g)}
    (y_att,) = _mm_nn_cols("mm_y_att", att, p["w_att_o"], F32)
    (y_rec,) = _mm_nn_rows("mm_y_rec", g, p["w_rec_o"], F32, tk=D)
    mixed = _merge_fwd(y_att, y_rec, z)

    def add_res(r, ex, outs):
        outs[0][...] = ex[0][...] + r

    res_spec = pl.BlockSpec((TM, D), lambda i, j, k: (i, 0))
    (x1,) = _mm_nn_rows("mm_x1", mixed, p["w_out"], F32, tk=D, extras=[x], extra_specs=[res_spec], epilogue=add_res)
    h2 = _rms_fwd("rms2_fwd", x1, p["ln2_g"])

    def relu2(r, ex, outs):
        outs[0][...] = r
        rp = jnp.maximum(r, 0.0)
        outs[1][...] = (rp * rp).astype(BF16)

    ff_blk = pl.BlockSpec((TM, D), lambda j, i, k: (i, j))
    f, s = _mm_nn_cols("mm_ff1", h2, p["w_ff1"], F32, epilogue=relu2,
                       out_shapes=[_sds((T, D_FF), F32), _sds((T, D_FF), BF16)], out_specs=[ff_blk, ff_blk])
    (x2,) = _mm_nn_rows("mm_x2", s, p["w_ff2"], F32, tk=D, extras=[x1], extra_specs=[res_spec], epilogue=add_res)
    loss, dx2, g_lnf = _loss_head(x2, target, p["lnf_g"])

    def relu2_bwd(r, ex, outs):
        outs[0][...] = (r * 2.0 * jnp.maximum(ex[0][...], 0.0)).astype(BF16)

    (df,) = _mm_nt_rows("mm_df", dx2, p["w_ff2"], BF16, tn=D, extras=[f],
                        extra_specs=[pl.BlockSpec((TM, D), lambda j, i, k: (i, j))], epilogue=relu2_bwd)
    (g_w_ff2,) = _mm_tn_rows("mm_g_ff2", s, dx2, tm=D)
    (g_w_ff1,) = _mm_tn_cols("mm_g_ff1", h2, df, D)
    sent = on_grads("ff", dict(w_ff2=g_w_ff2, w_ff1=g_w_ff1))
    (dh2,) = _mm_nt_cols("mm_dh2", df, p["w_ff1"], F32)
    dx1, g_ln2 = _rms_bwd("rms2_bwd", dh2, x1, p["ln2_g"], dx2, after=sent)

    (dmixed,) = _mm_nt_rows("mm_dmixed", dx1, p["w_out"], F32, tn=D)
    (g_w_out,) = _mm_tn_rows("mm_g_out", mixed, dx1, tm=D)
    dy_att, dy_rec, dg_att, dg_rec = _merge_bwd(dmixed, y_att, y_rec, z)
    (d_att,) = _mm_nt_cols("mm_d_att", dy_att, p["w_att_o"], BF16)
    (g_w_att_o,) = _mm_tn_cols("mm_g_att_o", att, dy_att, D // N_CHIPS)
    (d_g,) = _mm_nt_rows("mm_d_g", dy_rec, p["w_rec_o"], BF16, tn=D)
    (g_w_rec_o,) = _mm_tn_rows("mm_g_rec_o", g, dy_rec, tm=D)
    sent = on_grads("proj", dict(w_out=g_w_out, w_att_o=g_w_att_o, w_rec_o=g_w_rec_o))

    dq, dk, dv, ds_acc = _attn_bwd(z, tb, d_att, after=sent)
    g_rpb = _rpb_grad(ds_acc)
    d_up, d_yb, g_conv_w, g_conv_b, g_wa, g_wi, g_ba, g_bi, g_lam = _rec_bwd(z, d_g, *rec_params)
    dz = jnp.concatenate([dq, dk, dv, d_up, d_yb, dg_att, dg_rec], axis=1)

    g_w_in, g_b_in = _mm_tn_cols("mm_g_in", h1, dz, D_IN // N_CHIPS, colsum=True)
    sent = on_grads("in", dict(w_in=g_w_in))
    (dh1,) = _mm_nt_cols("mm_dh1", dz, p["w_in"], F32)
    grad_x, g_ln1 = _rms_bwd("rms1_bwd", dh1, x, p["ln1_g"], dx1, after=sent)

    grads = dict(ln1_g=g_ln1, w_in=g_w_in, b_in=g_b_in, rpb=g_rpb, w_att_o=g_w_att_o, conv_w=g_conv_w,
                 conv_b=g_conv_b, w_rg_a=_block_diag_grad(g_wa), b_rg_a=g_ba, w_rg_i=_block_diag_grad(g_wi),
                 b_rg_i=g_bi, lru_lambda=g_lam, w_rec_o=g_w_rec_o, w_out=g_w_out, ln2_g=g_ln2,
                 w_ff1=g_w_ff1, w_ff2=g_w_ff2, lnf_g=g_lnf)
    return loss, grad_x.reshape(1, T, D), grads


_ANY = pl.BlockSpec(memory_space=pl.ANY)
N_PEERS = N_CHIPS - 1


def _place():
    x, y, c = lax.axis_index("x"), lax.axis_index("y"), lax.axis_index("c")
    peers = [(1 - x, y), (x, 1 - y), (1 - x, 1 - y)]
    return x, y, c, 2 * x + y, peers


def _remote(src, dst, send_sem, recv_sem, dev):
    return pltpu.make_async_remote_copy(src_ref=src, dst_ref=dst, send_sem=send_sem, recv_sem=recv_sem,
                                        device_id=dev, device_id_type=MESH)


def _prefetch_call(body, name, ids, grid, in_specs, out_specs, out_shape, args):
    spec = pltpu.PrefetchScalarGridSpec(num_scalar_prefetch=1, grid=grid, in_specs=in_specs, out_specs=out_specs)
    return pl.pallas_call(body, name=name, grid_spec=spec, out_shape=out_shape,
                          compiler_params=_params(("parallel",) * len(grid)))(ids, *args)


def _cast_bf16(name, w, chip_id, after=()):
    rows, cols = w.shape
    rb = min(rows, 256)

    def body(ids_ref, w_ref, *rest):
        rest[-1][...] = w_ref[...].astype(BF16)

    return _prefetch_call(body, name, chip_id, (rows // rb,),
                          [pl.BlockSpec((rb, cols), lambda i, ids: (i, 0))] + [_ANY] * len(after),
                          pl.BlockSpec((None, rb, cols), lambda i, ids: (ids[0], i, 0)),
                          _sds((N_CHIPS, rows, cols), BF16), (w, *after))


def _dma_sems(*counts):
    return [pltpu.SemaphoreType.DMA((k,)) for k in counts]


_HBM = pl.BlockSpec(memory_space=pltpu.HBM)
_SEM = pl.BlockSpec(memory_space=pltpu.SEMAPHORE)
_SPLIT_COPY = pltpu.CompilerParams(has_side_effects=pltpu.SideEffectType.DATAFLOW_SIDE_EFFECTING)


def _hbm(arrays):
    return [pltpu.with_memory_space_constraint(a, pltpu.HBM) for a in arrays]


def _hbm_like(arrays):
    return [pltpu.HBM(a.shape, a.dtype) for a in arrays]


def _halves(buf, c):
    half = buf.shape[1] // 2
    return pl.ds(c * half, half), pl.ds((1 - c) * half, half)


def _gather_start(name, slots):
    n = len(slots)
    nk = n * N_PEERS

    def body(*refs):
        bufs = refs[n:2 * n]
        send_sems, recv_sems, token = refs[2 * n:]
        x, y, c, chip, peers = _place()
        for t in range(n):
            mine, _ = _halves(bufs[t], c)
            for r, (px, py) in enumerate(peers):
                k = t * N_PEERS + r
                own = bufs[t].at[chip, mine]
                _remote(own, own, send_sems.at[k], recv_sems.at[k], (px, py, c)).start()
        token[...] = jnp.zeros_like(token)

    res = pl.pallas_call(
        body, name=name, in_specs=[_HBM] * n, out_specs=[_HBM] * n + [_SEM, _SEM, pl.BlockSpec(memory_space=pltpu.VMEM)],
        out_shape=_hbm_like(slots) + [pltpu.SemaphoreType.DMA((nk,)), pltpu.SemaphoreType.DMA((nk,)),
                                      _sds((8, 128), F32)],
        input_output_aliases={t: t for t in range(n)}, compiler_params=_SPLIT_COPY)(*_hbm(slots))
    return res[:n], (res[n], res[n + 1]), res[n + 2]


def _gather_wait(name, bufs, sems, after):
    n = len(bufs)

    def body(*refs):
        ins = refs[:n]
        send_sems, recv_sems = refs[n], refs[n + 1]
        x, y, c, chip, peers = _place()
        for t in range(n):
            mine, _ = _halves(ins[t], c)
            for r, (px, py) in enumerate(peers):
                k = t * N_PEERS + r
                cp = _remote(ins[t].at[chip, mine], ins[t].at[2 * px + py, mine], send_sems.at[k], recv_sems.at[k],
                             (px, py, c))
                cp.wait_send()
                cp.wait_recv()

    return pl.pallas_call(
        body, name=name, in_specs=[_HBM] * n + [_SEM, _SEM, _ANY], out_specs=[_HBM] * n, out_shape=_hbm_like(bufs),
        input_output_aliases={t: t for t in range(n)}, compiler_params=_SPLIT_COPY)(*bufs, *sems, after)


def _gather_forward(name, bufs):
    n = len(bufs)
    nk = n * N_PEERS

    def body(*refs):
        outs = refs[n:2 * n]
        send_sems, recv_sems = refs[2 * n:]
        x, y, c, chip, peers = _place()
        sibling = (x, y, 1 - c)
        sends = []
        for t in range(n):
            mine, _ = _halves(outs[t], c)
            for r, (px, py) in enumerate(peers):
                k = t * N_PEERS + r
                landed = outs[t].at[2 * px + py, mine]
                sends.append(_remote(landed, landed, send_sems.at[k], recv_sems.at[k], sibling))
                sends[-1].start()
        for t in range(n):
            _, theirs = _halves(outs[t], c)
            for r, (px, py) in enumerate(peers):
                k = t * N_PEERS + r
                landed = outs[t].at[2 * px + py, theirs]
                _remote(landed, landed, send_sems.at[k], recv_sems.at[k], sibling).wait_recv()
        for cp in sends:
            cp.wait_send()

    return pl.pallas_call(
        body, name=name, in_specs=[_ANY] * n, out_specs=[_ANY] * n, out_shape=[_sds(b.shape, b.dtype) for b in bufs],
        input_output_aliases={t: t for t in range(n)}, scratch_shapes=_dma_sems(nk, nk))(*bufs)


def _pair_exchange(name, grads, wholes=()):
    n, nw = len(grads), len(wholes)

    def body(*refs):
        ins, whole_in = refs[:n], refs[n:n + nw]
        got, whole_got = refs[n + nw:2 * n + nw], refs[2 * n + nw:2 * (n + nw)]
        send_sem, recv_sem = refs[2 * (n + nw):]
        x, y, c, _, _ = _place()
        sibling = (x, y, 1 - c)
        copies = []
        for t in range(n):
            half = ins[t].shape[1] // 2
            for j in range(N_CHIPS):
                k = t * N_CHIPS + j
                copies.append(_remote(ins[t].at[j, pl.ds((1 - c) * half, half)], got[t].at[j],
                                      send_sem.at[k], recv_sem.at[k], sibling))
        for t in range(nw):
            k = n * N_CHIPS + t
            copies.append(_remote(whole_in[t], whole_got[t], send_sem.at[k], recv_sem.at[k], sibling))
        for cp in copies:
            cp.start()
        for cp in copies:
            cp.wait_recv()
        for cp in copies:
            cp.wait_send()

    ns = n * N_CHIPS + nw
    res = pl.pallas_call(
        body, name=name, in_specs=[_ANY] * (n + nw), out_specs=[_ANY] * (n + nw),
        out_shape=([_sds((N_CHIPS, g.shape[1] // 2, g.shape[2]), F32) for g in grads]
                   + [_sds(a.shape, a.dtype) for a in wholes]),
        scratch_shapes=_dma_sems(ns, ns))(*grads, *wholes)
    return res[:n], res[n:]


def _chip_copies(srcs, lands, small_src, small_land, send_sems, recv_sems):
    x, y, c, chip, peers = _place()
    n = len(srcs)
    copies = []
    for r, (px, py) in enumerate(peers):
        for t in range(n):
            k = t * N_PEERS + r
            copies.append(_remote(srcs[t].at[2 * px + py], lands[t].at[r], send_sems.at[k], recv_sems.at[k], (px, py, c)))
        if small_src is not None:
            k = n * N_PEERS + r
            half_s = small_src.shape[0] // 2
            copies.append(_remote(small_src.at[pl.ds(c * half_s, half_s)], small_land.at[r],
                                  send_sems.at[k], recv_sems.at[k], (px, py, c)))
    return copies


def _chip_start(name, sums_bf16, small=None):
    n = len(sums_bf16)
    srcs = list(sums_bf16) + ([small] if small is not None else [])
    m = len(srcs)
    lands = [pltpu.HBM((N_PEERS,) + s.shape[1:], BF16) for s in sums_bf16]
    if small is not None:
        lands.append(pltpu.HBM((N_PEERS, small.shape[0] // 2, 128), F32))
    nk = m * N_PEERS

    def body(*refs):
        src_refs, land_refs = refs[m:2 * m], refs[2 * m:3 * m]
        send_sems, recv_sems, token = refs[3 * m:]
        small_src, small_land = (src_refs[n], land_refs[n]) if small is not None else (None, None)
        for cp in _chip_copies(src_refs[:n], land_refs[:n], small_src, small_land, send_sems, recv_sems):
            cp.start()
        token[...] = jnp.zeros_like(token)

    res = pl.pallas_call(
        body, name=name, in_specs=[_HBM] * m,
        out_specs=[_HBM] * (2 * m) + [_SEM, _SEM, pl.BlockSpec(memory_space=pltpu.VMEM)],
        out_shape=_hbm_like(srcs) + lands + [pltpu.SemaphoreType.DMA((nk,)), pltpu.SemaphoreType.DMA((nk,)),
                                             _sds((8, 128), F32)],
        input_output_aliases={t: t for t in range(m)}, compiler_params=_SPLIT_COPY)(*_hbm(srcs))
    return (res[:m], res[m:2 * m], (res[2 * m], res[2 * m + 1])), res[2 * m + 2]


def _chip_wait(name, flight, with_small, after):
    srcs, lands, sems = flight
    m = len(srcs)
    n = m - 1 if with_small else m

    def body(*refs):
        src_refs, land_refs = refs[:m], refs[m:2 * m]
        send_sems, recv_sems = refs[2 * m], refs[2 * m + 1]
        small_src, small_land = (src_refs[n], land_refs[n]) if with_small else (None, None)
        for cp in _chip_copies(src_refs[:n], land_refs[:n], small_src, small_land, send_sems, recv_sems):
            cp.wait_send()
            cp.wait_recv()

    res = pl.pallas_call(
        body, name=name, in_specs=[_HBM] * (2 * m) + [_SEM, _SEM, _ANY], out_specs=[_HBM] * (2 * m),
        out_shape=_hbm_like(srcs) + _hbm_like(lands), input_output_aliases={t: t for t in range(2 * m)},
        compiler_params=_SPLIT_COPY)(*srcs, *lands, *sems, after)
    return res[:m], res[m:]


def _half_swap(name, bufs):
    n = len(bufs)

    def body(*refs):
        outs = refs[n:2 * n]
        send_sem, recv_sem = refs[2 * n:]
        x, y, c, _, _ = _place()
        sibling = (x, y, 1 - c)
        copies = []
        for t in range(n):
            h = outs[t].shape[0] // 2
            mine = outs[t].at[pl.ds(c * h, h)]
            copies.append(_remote(mine, mine, send_sem.at[t], recv_sem.at[t], sibling))
            copies[-1].start()
        for t in range(n):
            h = outs[t].shape[0] // 2
            theirs = outs[t].at[pl.ds((1 - c) * h, h)]
            _remote(theirs, theirs, send_sem.at[t], recv_sem.at[t], sibling).wait_recv()
        for cp in copies:
            cp.wait_send()

    return pl.pallas_call(
        body, name=name, in_specs=[_ANY] * n, out_specs=[_ANY] * n,
        out_shape=[_sds(b.shape, b.dtype) for b in bufs], input_output_aliases={t: t for t in range(n)},
        scratch_shapes=_dma_sems(n, n))(*bufs)


def _pair_sum(name, grad, got, ids):
    _, rows, cols = got.shape
    rb = min(rows, 256)
    nb = rows // rb
    blk = pl.BlockSpec((None, rb, cols), lambda j, i, ids: (j, i, 0))
    mine = pl.BlockSpec((None, rb, cols), lambda j, i, ids: (j, ids[1] * nb + i, 0))

    def body(ids_ref, a_ref, b_ref, s_ref, sb_ref):
        s = a_ref[...] + b_ref[...]
        s_ref[...] = s
        sb_ref[...] = s.astype(BF16)

    return _prefetch_call(body, name, ids, (N_CHIPS, nb), [mine, blk], [blk, blk],
                          [_sds(got.shape, F32), _sds(got.shape, BF16)], (grad, got))


def _chip_sum(name, sums, got, ids):
    _, rows, cols = sums.shape
    rb = min(rows, 256)
    nb = rows // rb
    own = pl.BlockSpec((None, rb, cols), lambda i, ids: (ids[0], i, 0))
    blk3 = pl.BlockSpec((N_PEERS, rb, cols), lambda i, ids: (0, i, 0))
    out = pl.BlockSpec((rb, cols), lambda i, ids: (ids[1] * nb + i, 0))

    def body(ids_ref, a_ref, b_ref, o_ref):
        o_ref[...] = ((a_ref[...] + b_ref[0].astype(F32)) + b_ref[1].astype(F32)) + b_ref[2].astype(F32)

    return _prefetch_call(body, name, ids, (nb,), [own, blk3], out, _sds((2 * rows, cols), F32), (sums, got))


SMALL_RB = 280


def _small_pair_sum(own, got):
    blk = pl.BlockSpec((SMALL_RB, 128), lambda i: (i, 0))

    def body(a_ref, b_ref, o_ref):
        o_ref[...] = a_ref[...] + b_ref[...]

    return pl.pallas_call(body, name="small_pair_sum", grid=(own.shape[0] // SMALL_RB,), in_specs=[blk, blk],
                          out_specs=blk, out_shape=_sds(own.shape, F32),
                          compiler_params=_params(("parallel",)))(own, got)


def _small_chip_sum(pair, got, ids):
    nb = pair.shape[0] // 2 // SMALL_RB
    half = pl.BlockSpec((SMALL_RB, 128), lambda i, ids: (ids[1] * nb + i, 0))
    blk3 = pl.BlockSpec((N_PEERS, SMALL_RB, 128), lambda i, ids: (0, i, 0))

    def body(ids_ref, a_ref, b_ref, o_ref):
        o_ref[...] = (a_ref[...] + b_ref[1]) + (b_ref[0] + b_ref[2])

    return _prefetch_call(body, "small_chip_sum", ids, (nb,), [half, blk3], half, _sds(pair.shape, F32), (pair, got))


def _adamw_math(w, g, m, v):
    m = ADAM_B1 * m + (1.0 - ADAM_B1) * g
    v = ADAM_B2 * v + (1.0 - ADAM_B2) * (g * g)
    m_hat = m / (1.0 - ADAM_B1 ** ADAM_STEP)
    v_hat = v / (1.0 - ADAM_B2 ** ADAM_STEP)
    delta = -ADAM_LR * (m_hat / (jnp.sqrt(v_hat) + ADAM_EPS) + ADAM_WD * w)
    return delta, m, v


def _adamw(name, w, g, m, v, rb=None):
    rows, cols = w.shape
    rb = rows if rb is None else rb
    blk = pl.BlockSpec((rb, cols), lambda i: (i, 0))

    def body(w_ref, g_ref, m_ref, v_ref, d_ref, nm_ref, nv_ref):
        d, nm, nv = _adamw_math(w_ref[...], g_ref[...], m_ref[...], v_ref[...])
        d_ref[...] = d
        nm_ref[...] = nm
        nv_ref[...] = nv

    return pl.pallas_call(body, name=name, grid=(rows // rb,), in_specs=[blk] * 4, out_specs=[blk] * 3,
                          out_shape=[_sds(w.shape, F32)] * 3, compiler_params=_params(("parallel",)))(w, g, m, v)


BIG = ("w_in", "w_att_o", "w_rec_o", "w_out", "w_ff1", "w_ff2")
COL_SHARDED = ("w_in", "w_att_o", "w_ff1")
SHARDED_VECS = ("conv_w", "b_rg_a", "b_rg_i", "lru_lambda")
SMALL = ("ln1_g", "b_in", "rpb", "conv_w", "conv_b", "w_rg_a", "b_rg_a", "w_rg_i", "b_rg_i", "lru_lambda",
         "ln2_g", "lnf_g")
SMALL_ROWS = 2240
ORDER = ("ln1_g", "w_in", "b_in", "rpb", "w_att_o", "conv_w", "conv_b", "w_rg_a", "b_rg_a", "w_rg_i", "b_rg_i",
         "lru_lambda", "w_rec_o", "w_out", "ln2_g", "w_ff1", "w_ff2", "lnf_g")


def _pack_small(grads, loss):
    parts, sizes = [], {}
    for n in SMALL:
        flat = grads[n].reshape(-1)
        pad = (-flat.shape[0]) % 128
        sizes[n] = (flat.shape[0], flat.shape[0] + pad)
        parts.append(jnp.pad(flat, (0, pad)))
    total = sum(s[1] for s in sizes.values())
    parts.append(jnp.pad(loss.reshape(1), (0, SMALL_ROWS * 128 - total - 1)))
    return jnp.concatenate(parts).reshape(SMALL_ROWS, 128), sizes


def _unpack_small(buf, sizes, shapes):
    flat = buf.reshape(-1)
    out, pos = {}, 0
    for n in SMALL:
        size, padded = sizes[n]
        out[n] = flat[pos:pos + size].reshape(shapes[n])
        pos += padded
    return out, flat[pos]


def _gather_weights(w, chip):
    chip_id = chip.astype(jnp.int32).reshape(1)
    vec_rows = [w[n][0] for n in SHARDED_VECS]
    vec_shard = jnp.concatenate(vec_rows + [jnp.zeros((16 - 10, D // N_CHIPS), F32)], axis=0)
    vec_slots = lax.dynamic_update_slice(jnp.zeros((N_CHIPS, 16, D // N_CHIPS), F32), vec_shard[None], (chip, 0, 0))
    bufs_a, sems_a, token_a = _gather_start("gather_start_first", [_cast_bf16("cast_w_in", w["w_in"][0], chip_id), vec_slots])
    rest_names = BIG[1:]
    bufs_b, sems_b, token_b = _gather_start(
        "gather_start_rest", [_cast_bf16("cast_" + n, w[n][0], chip_id, after=(token_a,)) for n in rest_names])

    def first(after):
        w_in_full, vec_full = _gather_forward("gather_forward_first", _gather_wait("gather_wait_first", bufs_a, sems_a, after))
        vecs = vec_full.transpose(1, 0, 2).reshape(16, D)
        return dict(w_in=w_in_full, conv_w=vecs[0:4], b_rg_a=vecs[4:6], b_rg_i=vecs[6:8], lru_lambda=vecs[8:10])

    def rest(after):
        full = dict(zip(rest_names, _gather_forward("gather_forward_rest",
                                                    _gather_wait("gather_wait_rest", bufs_b, sems_b, after))))
        return dict(w_att_o=full["w_att_o"], w_ff1=full["w_ff1"], w_rec_o=full["w_rec_o"].reshape(D, D),
                    w_out=full["w_out"].reshape(D, D), w_ff2=full["w_ff2"].reshape(D_FF, D))

    p = dict(ln1_g=w["ln1_g"], b_in=w["b_in"], rpb=w["rpb"][0], conv_b=w["conv_b"], w_rg_a=w["w_rg_a"][0],
             w_rg_i=w["w_rg_i"][0], ln2_g=w["ln2_g"], lnf_g=w["lnf_g"].reshape(1, D))
    return p, ((token_b,), first, rest)


def _reduce_start(tag, names, big, small, ids):
    got1, small_got1 = _pair_exchange("pair_exchange_" + tag, big, [] if small is None else [small])
    sums = [_pair_sum("pair_sum_" + n, a, b, ids) for n, a, b in zip(names, big, got1)]
    small_sum = None if small is None else _small_pair_sum(small, small_got1[0])
    flight, token = _chip_start("chip_start_" + tag, [s[1] for s in sums], small_sum)
    return ([s[0] for s in sums], flight), token


def _reduce_finish(tag, names, state, with_small, ids, after):
    sums_f32, flight = state
    srcs, lands = _chip_wait("chip_wait_" + tag, flight, with_small, after)
    halves = [_chip_sum("chip_sum_" + n, s, b, ids) for n, s, b in zip(names, sums_f32, lands)]
    if with_small:
        halves.append(_small_chip_sum(srcs[-1], lands[-1], ids))
    return _half_swap("half_swap_" + tag, halves)


def kernel(x, ln1_g, w_in, b_in, rpb, w_att_o, conv_w, conv_b, w_rg_a, b_rg_a, w_rg_i, b_rg_i, lru_lambda, w_rec_o, w_out, ln2_g, w_ff1, w_ff2, lnf_g, loss_target, m_ln1_g, m_w_in, m_b_in, m_rpb, m_w_att_o, m_conv_w, m_conv_b, m_w_rg_a, m_b_rg_a, m_w_rg_i, m_b_rg_i, m_lru_lambda, m_w_rec_o, m_w_out, m_ln2_g, m_w_ff1, m_w_ff2, m_lnf_g, v_ln1_g, v_w_in, v_b_in, v_rpb, v_w_att_o, v_conv_w, v_conv_b, v_w_rg_a, v_b_rg_a, v_w_rg_i, v_b_rg_i, v_lru_lambda, v_w_rec_o, v_w_out, v_ln2_g, v_w_ff1, v_w_ff2, v_lnf_g):
    w = dict(ln1_g=ln1_g, w_in=w_in, b_in=b_in, rpb=rpb, w_att_o=w_att_o, conv_w=conv_w, conv_b=conv_b,
             w_rg_a=w_rg_a, b_rg_a=b_rg_a, w_rg_i=w_rg_i, b_rg_i=b_rg_i, lru_lambda=lru_lambda, w_rec_o=w_rec_o,
             w_out=w_out, ln2_g=ln2_g, w_ff1=w_ff1, w_ff2=w_ff2, lnf_g=lnf_g)
    m = dict(ln1_g=m_ln1_g, w_in=m_w_in, b_in=m_b_in, rpb=m_rpb, w_att_o=m_w_att_o, conv_w=m_conv_w,
             conv_b=m_conv_b, w_rg_a=m_w_rg_a, b_rg_a=m_b_rg_a, w_rg_i=m_w_rg_i, b_rg_i=m_b_rg_i,
             lru_lambda=m_lru_lambda, w_rec_o=m_w_rec_o, w_out=m_w_out, ln2_g=m_ln2_g, w_ff1=m_w_ff1,
             w_ff2=m_w_ff2, lnf_g=m_lnf_g)
    v = dict(ln1_g=v_ln1_g, w_in=v_w_in, b_in=v_b_in, rpb=v_rpb, w_att_o=v_w_att_o, conv_w=v_conv_w,
             conv_b=v_conv_b, w_rg_a=v_w_rg_a, b_rg_a=v_b_rg_a, w_rg_i=v_w_rg_i, b_rg_i=v_b_rg_i,
             lru_lambda=v_lru_lambda, w_rec_o=v_w_rec_o, w_out=v_w_out, ln2_g=v_ln2_g, w_ff1=v_w_ff1,
             w_ff2=v_w_ff2, lnf_g=v_lnf_g)
    chip = 2 * lax.axis_index("x") + lax.axis_index("y")
    ids = jnp.stack([chip, lax.axis_index("c")]).astype(jnp.int32)

    out_grad, out_delta, out_m, out_v = {}, {}, {}, {}

    def update(n, gn):
        shape = w[n].shape
        two_d, rb = (gn.shape, 256) if n in BIG else ((int(np.prod(shape[:-1])), shape[-1]), None)
        gn = gn.reshape(two_d)
        d, nm, nv = _adamw("adamw_" + n, w[n].reshape(two_d), gn, m[n].reshape(two_d), v[n].reshape(two_d), rb)
        out_grad[n], out_delta[n], out_m[n], out_v[n] = (gn.reshape(shape), d.reshape(shape), nm.reshape(shape),
                                                         nv.reshape(shape))
        return d

    in_flight = []

    def on_grads(tag, grads):
        names = list(grads)
        big = [grads[n].reshape(N_CHIPS, -1, grads[n].shape[-1]) for n in names]
        flight, token = _reduce_start(tag, names, big, None, ids)
        in_flight.append((tag, names, flight))
        return (token,)

    p, late = _gather_weights(w, chip)
    loss, grad_x, g = _local_step(x, loss_target, p, late, on_grads)

    small, sizes = _pack_small(g, loss)
    small_flight, after = _reduce_start("small", [], [], small, ids)
    for tag, names, flight in in_flight:
        for n, red in zip(names, _reduce_finish(tag, names, flight, False, ids, after)):
            after = update(n, red)
    (small_red,) = _reduce_finish("small", [], small_flight, True, ids, after)
    gsmall, loss = _unpack_small(small_red, sizes, {n: g[n].shape for n in SMALL})
    for n in SMALL:
        gn = gsmall[n]
        if n in SHARDED_VECS:
            gn = lax.dynamic_slice_in_dim(gn, chip * (D // N_CHIPS), D // N_CHIPS, axis=1)
        update(n, gn)
    return (loss, grad_x, *[out_grad[n] for n in ORDER], *[out_delta[n] for n in ORDER],
            *[out_m[n] for n in ORDER], *[out_v[n] for n in ORDER])
```

```python
import functools

import numpy as np
import jax
import jax.numpy as jnp
from jax import lax
from jax.experimental import pallas as pl
from jax.experimental.pallas import tpu as pltpu

F32 = jnp.float32
BF16 = jnp.bfloat16

T = 2048
D = 1024
D_ATT = 512
D_IN = 5632
D_FF = 4096
N_HEADS = 8
HEAD_DIM = 64
GRID_W = 64
N_ROWS = T // GRID_W
WIN_H = 8
WIN_W = 16
KEYS = WIN_H * GRID_W
N_CHIPS = 4
EPS = 1e-6
LRU_C = 8.0
SCALE = HEAD_DIM ** -0.5
REC_CB = 256
REC_CHUNK = 256
PAD = 8

ADAM_LR = 0.001
ADAM_B1 = 0.9
ADAM_B2 = 0.999
ADAM_EPS = 1e-08
ADAM_WD = 0.01
ADAM_STEP = 10

VMEM_LIMIT = 56 * 1024 * 1024

NN = (((1,), (0,)), ((), ()))
NT = (((1,), (1,)), ((), ()))
TN = (((0,), (0,)), ((), ()))
MESH = pl.DeviceIdType.MESH


def _params(sem=None):
    return pltpu.CompilerParams(dimension_semantics=sem, vmem_limit_bytes=VMEM_LIMIT)


def _dot(a, b, dims):
    return lax.dot_general(a, b, dims, preferred_element_type=F32)


def _sigmoid(x):
    return 0.5 * jnp.tanh(0.5 * x) + 0.5


def _matmul(name, a, b, *, dims, grid, a_spec, b_spec, out_shapes, out_specs, acc_shape,
            extras=(), extra_specs=(), epilogue=None, colsum_spec=None, colsum_shape=None, after=()):
    nk = grid[2]
    n_extra = len(extras)
    n_out = len(out_shapes)
    with_colsum = colsum_spec is not None

    def body(a_ref, b_ref, *rest):
        ex = rest[:n_extra]
        rest = rest[:n_extra] + rest[n_extra + len(after):]
        outs = rest[n_extra:n_extra + n_out]
        pos = n_extra + n_out
        cs_out = rest[pos] if with_colsum else None
        pos += 1 if with_colsum else 0
        acc = rest[pos]
        cs_acc = rest[pos + 1] if with_colsum else None
        k = pl.program_id(2)

        @pl.when(k == 0)
        def _():
            acc[...] = jnp.zeros_like(acc)
            if with_colsum:
                cs_acc[...] = jnp.zeros_like(cs_acc)

        bv = b_ref[...]
        acc[...] += _dot(a_ref[...].astype(BF16), bv.astype(BF16), dims)
        if with_colsum:
            cs_acc[...] += jnp.sum(bv.astype(F32), axis=0, keepdims=True)

        @pl.when(k == nk - 1)
        def _():
            r = acc[...]
            if epilogue is None:
                outs[0][...] = r.astype(outs[0].dtype)
            else:
                epilogue(r, ex, outs)
            if with_colsum:
                cs_out[...] = cs_acc[...]

    shapes = list(out_shapes)
    specs = list(out_specs)
    scratch = [pltpu.VMEM(acc_shape, F32)]
    if with_colsum:
        shapes.append(colsum_shape)
        specs.append(colsum_spec)
        scratch.append(pltpu.VMEM((1, acc_shape[1]), F32))
    res = pl.pallas_call(
        body, name=name, grid=grid,
        in_specs=[a_spec, b_spec, *extra_specs] + [_ANY] * len(after),
        out_specs=specs, out_shape=shapes, scratch_shapes=scratch,
        compiler_params=_params(("parallel", "parallel", "arbitrary")),
    )(a, b, *extras, *after)
    return res


def _sds(shape, dtype):
    return jax.ShapeDtypeStruct(shape, dtype)


TM = 512
NI = T // TM


def _mm_nn_cols(name, a, wg, out_dtype, *, bias=None, extras=(), extra_specs=(), epilogue=None,
                out_shapes=None, out_specs=None):
    k_dim, n4 = wg.shape[1], wg.shape[2]
    ex, exs = list(extras), list(extra_specs)
    if bias is not None:
        ex = [bias] + ex
        exs = [pl.BlockSpec((1, n4), lambda j, i, k: (0, j))] + exs
        user_ep = epilogue

        def epilogue(r, e, outs):
            r = r + e[0][...]
            if user_ep is None:
                outs[0][...] = r.astype(outs[0].dtype)
            else:
                user_ep(r, e[1:], outs)
    if out_shapes is None:
        out_shapes = [_sds((T, N_CHIPS * n4), out_dtype)]
        out_specs = [pl.BlockSpec((TM, n4), lambda j, i, k: (i, j))]
    return _matmul(
        name, a, wg, dims=NN, grid=(N_CHIPS, NI, 1),
        a_spec=pl.BlockSpec((TM, k_dim), lambda j, i, k: (i, 0)),
        b_spec=pl.BlockSpec((None, k_dim, n4), lambda j, i, k: (j, 0, 0)),
        out_shapes=out_shapes, out_specs=out_specs, acc_shape=(TM, n4),
        extras=ex, extra_specs=exs, epilogue=epilogue)


def _mm_nn_rows(name, a, w, out_dtype, *, tk, extras=(), extra_specs=(), epilogue=None):
    k_dim, n = w.shape
    return _matmul(
        name, a, w, dims=NN, grid=(NI, 1, k_dim // tk),
        a_spec=pl.BlockSpec((TM, tk), lambda i, j, k: (i, k)),
        b_spec=pl.BlockSpec((tk, n), lambda i, j, k: (k, 0)),
        out_shapes=[_sds((T, n), out_dtype)],
        out_specs=[pl.BlockSpec((TM, n), lambda i, j, k: (i, 0))], acc_shape=(TM, n),
        extras=extras, extra_specs=extra_specs, epilogue=epilogue)


def _mm_nt_cols(name, a, wg, out_dtype, after=()):
    k_dim, n4 = wg.shape[1], wg.shape[2]
    return _matmul(
        name, a, wg, dims=NT, grid=(NI, 1, N_CHIPS),
        a_spec=pl.BlockSpec((TM, n4), lambda i, j, k: (i, k)),
        b_spec=pl.BlockSpec((None, k_dim, n4), lambda i, j, k: (k, 0, 0)),
        out_shapes=[_sds((T, k_dim), out_dtype)],
        out_specs=[pl.BlockSpec((TM, k_dim), lambda i, j, k: (i, 0))], acc_shape=(TM, k_dim), after=after)


def _mm_nt_rows(name, a, w, out_dtype, *, tn, extras=(), extra_specs=(), epilogue=None):
    k_dim, n = w.shape
    return _matmul(
        name, a, w, dims=NT, grid=(k_dim // tn, NI, 1),
        a_spec=pl.BlockSpec((TM, n), lambda j, i, k: (i, 0)),
        b_spec=pl.BlockSpec((tn, n), lambda j, i, k: (j, 0)),
        out_shapes=[_sds((T, k_dim), out_dtype)],
        out_specs=[pl.BlockSpec((TM, tn), lambda j, i, k: (i, j))], acc_shape=(TM, tn),
        extras=extras, extra_specs=extra_specs, epilogue=epilogue)


def _mm_tn_cols(name, a, g, n4, *, colsum=False):
    k_dim = a.shape[1]
    kw = {}
    if colsum:
        kw = dict(colsum_spec=pl.BlockSpec((1, n4), lambda j, i, k: (0, j)),
                  colsum_shape=_sds((1, N_CHIPS * n4), F32))
    return _matmul(
        name, a, g, dims=TN, grid=(N_CHIPS, 1, NI),
        a_spec=pl.BlockSpec((TM, k_dim), lambda j, i, k: (k, 0)),
        b_spec=pl.BlockSpec((TM, n4), lambda j, i, k: (k, j)),
        out_shapes=[_sds((N_CHIPS, k_dim, n4), F32)],
        out_specs=[pl.BlockSpec((None, k_dim, n4), lambda j, i, k: (j, 0, 0))],
        acc_shape=(k_dim, n4), **kw)


def _mm_tn_rows(name, a, g, *, tm):
    k_dim, n = a.shape[1], g.shape[1]
    return _matmul(
        name, a, g, dims=TN, grid=(k_dim // tm, 1, NI),
        a_spec=pl.BlockSpec((TM, tm), lambda j, i, k: (k, j)),
        b_spec=pl.BlockSpec((TM, n), lambda j, i, k: (k, 0)),
        out_shapes=[_sds((k_dim, n), F32)],
        out_specs=[pl.BlockSpec((tm, n), lambda j, i, k: (j, 0))], acc_shape=(tm, n))


TE = 256
NE = T // TE
_ROW = pl.BlockSpec((TE, D), lambda i: (i, 0))
_VEC = pl.BlockSpec((1, D), lambda i: (0, 0))


def _rms_fwd(name, x, g, after=()):
    def body(x_ref, g_ref, *rest):
        h_ref = rest[-1]
        xv = x_ref[...]
        rstd = lax.rsqrt(jnp.mean(xv * xv, axis=-1, keepdims=True) + EPS)
        h_ref[...] = (xv * rstd * g_ref[...]).astype(BF16)

    return pl.pallas_call(body, name=name, grid=(NE,), in_specs=[_ROW, _VEC] + [_ANY] * len(after), out_specs=_ROW,
                          out_shape=_sds((T, D), BF16), compiler_params=_params(("parallel",)))(x, g, *after)


def _rms_bwd(name, dh, x, g, dres, after=()):
    def body(dh_ref, x_ref, g_ref, dres_ref, *rest):
        dx_ref, dg_ref = rest[-2:]
        xv = x_ref[...]
        rstd = lax.rsqrt(jnp.mean(xv * xv, axis=-1, keepdims=True) + EPS)
        xhat = xv * rstd
        dhv = dh_ref[...]
        dy = dhv * g_ref[...]
        dx_ref[...] = dres_ref[...] + rstd * (dy - xhat * jnp.mean(dy * xhat, axis=-1, keepdims=True))

        @pl.when(pl.program_id(0) == 0)
        def _():
            dg_ref[...] = jnp.zeros_like(dg_ref)

        dg_ref[...] += jnp.sum(dhv * xhat, axis=0, keepdims=True)

    return pl.pallas_call(body, name=name, grid=(NE,), in_specs=[_ROW, _ROW, _VEC, _ROW] + [_ANY] * len(after),
                          out_specs=[_ROW, _VEC], out_shape=[_sds((T, D), F32), _sds((1, D), F32)],
                          compiler_params=_params(("arbitrary",)))(dh, x, g, dres, *after)


def _loss_head(x2, target, g):
    def body(x_ref, t_ref, g_ref, loss_ref, dx_ref, dg_ref):
        xv = x_ref[...]
        rstd = lax.rsqrt(jnp.mean(xv * xv, axis=-1, keepdims=True) + EPS)
        xhat = xv * rstd
        gv = g_ref[...]
        err = xhat * gv - t_ref[...]
        dy = err * (1.0 / D)
        dxh = dy * gv
        dx_ref[...] = rstd * (dxh - xhat * jnp.mean(dxh * xhat, axis=-1, keepdims=True))

        @pl.when(pl.program_id(0) == 0)
        def _():
            dg_ref[...] = jnp.zeros_like(dg_ref)
            loss_ref[...] = jnp.zeros_like(loss_ref)

        dg_ref[...] += jnp.sum(dy * xhat, axis=0, keepdims=True)
        loss_ref[...] += (0.5 / D) * jnp.sum(jnp.sum(err * err, axis=1, keepdims=True), axis=0, keepdims=True)

    return pl.pallas_call(
        body, name="loss_head", grid=(NE,), in_specs=[_ROW, _ROW, _VEC],
        out_specs=[pl.BlockSpec((1, 1), lambda i: (0, 0)), _ROW, _VEC],
        out_shape=[_sds((1, 1), F32), _sds((T, D), F32), _sds((1, D), F32)],
        compiler_params=_params(("arbitrary",)))(x2, target, g)


MW = 512
_G_ATT_BLK = 3584 // MW
_G_REC_BLK = 4608 // MW


def _merge_specs():
    y = pl.BlockSpec((TM, MW), lambda i, j: (i, j))
    ga = pl.BlockSpec((TM, MW), lambda i, j: (i, _G_ATT_BLK + j))
    gr = pl.BlockSpec((TM, MW), lambda i, j: (i, _G_REC_BLK + j))
    return y, ga, gr


def _merge_fwd(y_att, y_rec, z):
    y, ga, gr = _merge_specs()

    def body(ya_ref, yr_ref, ga_ref, gr_ref, m_ref):
        m = _sigmoid(ga_ref[...]) * ya_ref[...] + _sigmoid(gr_ref[...]) * yr_ref[...]
        m_ref[...] = m.astype(BF16)

    return pl.pallas_call(body, name="merge_fwd", grid=(NI, D // MW), in_specs=[y, y, ga, gr], out_specs=y,
                          out_shape=_sds((T, D), BF16),
                          compiler_params=_params(("parallel", "parallel")))(y_att, y_rec, z, z)


def _merge_bwd(dm, y_att, y_rec, z):
    y, ga, gr = _merge_specs()

    def body(dm_ref, ya_ref, yr_ref, ga_ref, gr_ref, dya_ref, dyr_ref, dga_ref, dgr_ref):
        dmv = dm_ref[...]
        sa = _sigmoid(ga_ref[...])
        sr = _sigmoid(gr_ref[...])
        dya_ref[...] = (dmv * sa).astype(BF16)
        dyr_ref[...] = (dmv * sr).astype(BF16)
        dga_ref[...] = (dmv * ya_ref[...] * sa * (1.0 - sa)).astype(BF16)
        dgr_ref[...] = (dmv * yr_ref[...] * sr * (1.0 - sr)).astype(BF16)

    return pl.pallas_call(body, name="merge_bwd", grid=(NI, D // MW), in_specs=[y, y, y, ga, gr],
                          out_specs=[y, y, y, y], out_shape=[_sds((T, D), BF16)] * 4,
                          compiler_params=_params(("parallel", "parallel")))(dm, y_att, y_rec, z, z)


HP = 2 * HEAD_DIM
N_HP = N_HEADS // 2
ATT_UNROLL_FWD = 8
ATT_UNROLL_BWD = 4


def _window_maps():
    diag = np.zeros((GRID_W * GRID_W, 128), np.float32)
    for qc in range(GRID_W):
        w0 = min(max(qc - WIN_W // 2, 0), GRID_W - WIN_W)
        for kc in range(w0, w0 + WIN_W):
            diag[qc * GRID_W + kc, kc - qc + WIN_W - 1] = 1.0
    return diag, diag.sum(axis=1)[None, :]


def _split3(x):
    a = x.astype(BF16)
    r = x - a.astype(F32)
    b = r.astype(BF16)
    c = (r - b.astype(F32)).astype(BF16)
    return a, b, c


N_DROW = 2 * WIN_H - 1
N_DPAIR = N_DROW - 1


def _bias_pairs(rpb):
    diag, valid = _window_maps()
    r2 = jnp.pad(rpb.reshape(N_HEADS * N_DROW, 2 * WIN_W - 1),
                 ((0, 128 - N_HEADS * N_DROW), (0, 128 - (2 * WIN_W - 1))))

    def body(r_ref, d_ref, v_ref, o_ref):
        dv = d_ref[...]
        t = sum(_dot(part, dv, NN) for part in _split3(r_ref[...]))
        o_ref[...] = jnp.where(v_ref[...] > 0.0, t, -1e30)

    t = pl.pallas_call(body, name="rpb_expand", out_shape=_sds((128, GRID_W * GRID_W), F32),
                       compiler_params=_params())(r2, jnp.asarray(diag.T, BF16), jnp.asarray(valid, F32))
    t = t[:N_HEADS * N_DROW].reshape(N_HEADS, N_DROW, GRID_W, GRID_W)
    return jnp.concatenate([t[:, :N_DPAIR], t[:, 1:]], axis=-1)


def _row_bias(tb_ref, hh, d0):
    return jnp.concatenate([tb_ref[hh, d0 + 2 * ii] for ii in range(WIN_H // 2)], axis=1)


def _row_window(r):
    rs = jnp.clip(r - WIN_H // 2, 0, N_ROWS - WIN_H)
    return pl.multiple_of(r * GRID_W, GRID_W), pl.multiple_of(rs * GRID_W, GRID_W), rs - r + (WIN_H - 1)


def _split_heads(src_ref, dst_ref, scale=None):
    for hh in range(2):
        v = src_ref[:, hh * HEAD_DIM:(hh + 1) * HEAD_DIM]
        dst_ref[hh] = (v if scale is None else v * scale).astype(BF16)


def _attn_items(qb_ref, kb_ref, vb_ref, tb_ref, first_row, n_rows):
    wins = [_row_window(first_row + u) for u in range(n_rows)]
    items = [(u, hh) for u in range(n_rows) for hh in range(2)]
    q = [qb_ref[hh, pl.ds(wins[u][0], GRID_W), :] for u, hh in items]
    k = [kb_ref[hh, pl.ds(wins[u][1], KEYS), :] for u, hh in items]
    v = [vb_ref[hh, pl.ds(wins[u][1], KEYS), :] for u, hh in items]
    s = [_dot(qi, ki, NT) + _row_bias(tb_ref, hh, wins[u][2]) for qi, ki, (u, hh) in zip(q, k, items)]
    m = [jnp.max(si, axis=-1, keepdims=True) for si in s]
    e = [jnp.exp(si - mi) for si, mi in zip(s, m)]
    inv = [1.0 / jnp.sum(ei, axis=-1, keepdims=True) for ei in e]
    p = [ei * li for ei, li in zip(e, inv)]
    return wins, items, q, k, v, p


def _attn_in_specs():
    q = pl.BlockSpec((T, HP), lambda p: (0, p))
    k = pl.BlockSpec((T, HP), lambda p: (0, N_HP + p))
    v = pl.BlockSpec((T, HP), lambda p: (0, 2 * N_HP + p))
    tb = pl.BlockSpec((2, N_DPAIR, GRID_W, HP), lambda p: (p, 0, 0, 0))
    return q, k, v, tb


_HEAD_SCRATCH = pltpu.VMEM((2, T, HEAD_DIM), BF16)


def _attn_fwd(z, tb):
    def body(q_ref, k_ref, v_ref, tb_ref, o_ref, qb_ref, kb_ref, vb_ref):
        _split_heads(q_ref, qb_ref, SCALE)
        _split_heads(k_ref, kb_ref)
        _split_heads(v_ref, vb_ref)

        def rows(it, carry):
            wins, items, _, _, v, p = _attn_items(qb_ref, kb_ref, vb_ref, tb_ref, it * ATT_UNROLL_FWD, ATT_UNROLL_FWD)
            o = [_dot(pi.astype(BF16), vi, NN) for pi, vi in zip(p, v)]
            for u, (q0, _, _) in enumerate(wins):
                o_ref[pl.ds(q0, GRID_W), :] = jnp.concatenate(o[2 * u:2 * u + 2], axis=1).astype(BF16)
            return carry

        lax.fori_loop(0, N_ROWS // ATT_UNROLL_FWD, rows, 0)

    blk = pl.BlockSpec((T, HP), lambda p: (0, p))
    return pl.pallas_call(
        body, name="attn_fwd", grid=(N_HP,), in_specs=list(_attn_in_specs()), out_specs=blk,
        out_shape=_sds((T, D_ATT), BF16), scratch_shapes=[_HEAD_SCRATCH] * 3,
        compiler_params=_params(("parallel",)))(z, z, z, tb)


def _attn_bwd(z, tb, d_att, after=()):
    def body(q_ref, k_ref, v_ref, tb_ref, do_ref, *rest):
        dq_ref, dk_ref, dv_ref, ds_ref, qb_ref, kb_ref, vb_ref, dob_ref, dka_ref, dva_ref = rest[len(after):]
        _split_heads(q_ref, qb_ref, SCALE)
        _split_heads(k_ref, kb_ref)
        _split_heads(v_ref, vb_ref)
        _split_heads(do_ref, dob_ref)
        dka_ref[...] = jnp.zeros_like(dka_ref)
        dva_ref[...] = jnp.zeros_like(dva_ref)
        ds_ref[...] = jnp.zeros_like(ds_ref)

        def rows(it, carry):
            wins, items, q, k, v, p = _attn_items(qb_ref, kb_ref, vb_ref, tb_ref, it * ATT_UNROLL_BWD, ATT_UNROLL_BWD)
            do = [dob_ref[hh, pl.ds(wins[u][0], GRID_W), :] for u, hh in items]
            dv = [_dot(pi.astype(BF16), di, TN) for pi, di in zip(p, do)]
            dp = [_dot(di, vi, NT) for di, vi in zip(do, v)]
            ds = [pi * (dpi - jnp.sum(dpi * pi, axis=-1, keepdims=True)) for pi, dpi in zip(p, dp)]
            dsb = [d.astype(BF16) for d in ds]
            dq = [_dot(d, ki, NN) * SCALE for d, ki in zip(dsb, k)]
            dk = [_dot(d, qi, TN) for d, qi in zip(dsb, q)]
            for d, (u, hh) in zip(ds, items):
                for ii in range(WIN_H // 2):
                    ds_ref[hh, wins[u][2] + 2 * ii] += d[:, ii * HP:(ii + 1) * HP]
            for u, (q0, k0, _) in enumerate(wins):
                dq_ref[pl.ds(q0, GRID_W), :] = jnp.concatenate(dq[2 * u:2 * u + 2], axis=1).astype(BF16)
                dka_ref[pl.ds(k0, KEYS), :] += jnp.concatenate(dk[2 * u:2 * u + 2], axis=1)
                dva_ref[pl.ds(k0, KEYS), :] += jnp.concatenate(dv[2 * u:2 * u + 2], axis=1)
            return carry

        lax.fori_loop(0, N_ROWS // ATT_UNROLL_BWD, rows, 0)
        dk_ref[...] = dka_ref[...].astype(BF16)
        dv_ref[...] = dva_ref[...].astype(BF16)

    blk = pl.BlockSpec((T, HP), lambda p: (0, p))
    q, k, v, tbs = _attn_in_specs()
    return pl.pallas_call(
        body, name="attn_bwd", grid=(N_HP,), in_specs=[q, k, v, tbs, blk] + [_ANY] * len(after),
        out_specs=[blk, blk, blk, tbs],
        out_shape=[_sds((T, D_ATT), BF16)] * 3 + [_sds((N_HEADS, N_DPAIR, GRID_W, HP), F32)],
        scratch_shapes=[_HEAD_SCRATCH] * 4 + [pltpu.VMEM((T, HP), F32), pltpu.VMEM((T, HP), F32)],
        compiler_params=_params(("parallel",)))(z, z, z, tb, d_att, *after)


def _rpb_grad(ds_acc):
    a = ds_acc.reshape(N_HEADS, N_DPAIR, GRID_W, 2, GRID_W).transpose(0, 1, 3, 2, 4)
    a = jnp.pad(a.reshape(N_HEADS * N_DPAIR * 2, GRID_W * GRID_W), ((0, 256 - N_HEADS * N_DPAIR * 2), (0, 0)))
    sel = np.zeros((128, 256), np.float32)
    for h in range(N_HEADS):
        for pair in range(N_DPAIR):
            for half in range(2):
                sel[h * N_DROW + pair + half, (h * N_DPAIR + pair) * 2 + half] = 1.0
    diag, _ = _window_maps()

    def body(a_ref, sel_ref, diag_ref, o_ref):
        selv = sel_ref[...]
        g = sum(_dot(selv, part, NN) for part in _split3(a_ref[...]))
        diagv = diag_ref[...]
        o_ref[...] = sum(_dot(part, diagv, NN) for part in _split3(g))

    out = pl.pallas_call(body, name="rpb_grad", out_shape=_sds((128, 128), F32),
                         compiler_params=_params())(a, jnp.asarray(sel, BF16), jnp.asarray(diag, BF16))
    return out[:N_HEADS * N_DROW, :2 * WIN_W - 1].reshape(N_HEADS, N_DROW, 2 * WIN_W - 1)


N_CB = D // REC_CB
N_CHUNK = T // REC_CHUNK
N_TILE = T // 8
_U_BLK = 1536 // REC_CB
_Y_BLK = 2560 // REC_CB


def _block_diag(w):
    per = REC_CB // 64
    wt = w.reshape(2, N_CB, per, 64, 64)
    eye = jnp.eye(per, dtype=w.dtype)
    full = wt[:, :, :, :, None, :] * eye[None, None, :, None, :, None]
    return full.reshape(2, N_CB, REC_CB, REC_CB).astype(BF16)


def _block_diag_grad(g):
    per = REC_CB // 64
    g6 = g.reshape(2, N_CB, per, 64, per, 64)
    return jnp.stack([g6[:, :, p, :, p, :] for p in range(per)], axis=2).reshape(2, 16, 64, 64)


def _gelu(x):
    c = 0.7978845608028654
    return 0.5 * x * (1.0 + jnp.tanh(c * (x + 0.044715 * x * x * x)))


def _gelu_grad(x):
    c = 0.7978845608028654
    th = jnp.tanh(c * (x + 0.044715 * x * x * x))
    return 0.5 * (1.0 + th) + 0.5 * x * (1.0 - th * th) * c * (1.0 + 3.0 * 0.044715 * x * x)


def _softplus_neg(lam):
    x = -lam
    e = jnp.exp(-jnp.abs(x))
    w = 1.0 + e
    l1p = jnp.where(w == 1.0, e, jnp.log(w) * e / (w - 1.0))
    return jnp.maximum(x, 0.0) + l1p


def _one_minus_exp(x):
    poly = x * (1.0 + x * (1 / 2 + x * (1 / 6 + x * (1 / 24 + x * (1 / 120 + x * (1 / 720))))))
    return jnp.where(x > -0.125, -poly, 1.0 - jnp.exp(x))


def _conv_taps(pad_ref, t0, w, sign):
    out = None
    for j in range(4):
        term = w[j:j + 1, :] * pad_ref[pl.ds(PAD + t0 + sign * (j - 2), REC_CHUNK), :]
        out = term if out is None else out + term
    return out


def _gates(u, wa, wi, ba, bi, sp):
    ub = u.astype(BF16)
    r = _sigmoid(_dot(ub, wa, NN) + ba)
    i = _sigmoid(_dot(ub, wi, NN) + bi)
    log_a = -LRU_C * r * sp
    a = jnp.exp(log_a)
    mult = jnp.sqrt(jnp.maximum(_one_minus_exp(2.0 * log_a), 0.0))
    return r, i, a, mult


def _tile_scan(a, b, sub, reverse):
    for s in (1, 2, 4):
        if reverse:
            a_s, b_s, m = pltpu.roll(a, 8 - s, 0), pltpu.roll(b, 8 - s, 0), sub < 8 - s
        else:
            a_s, b_s, m = pltpu.roll(a, s, 0), pltpu.roll(b, s, 0), sub >= s
        b = jnp.where(m, a * b_s + b, b)
        a = jnp.where(m, a * a_s, a)
    return a, b


def _last_row(x, sub, row):
    return jnp.broadcast_to(jnp.sum(jnp.where(sub == row, x, 0.0), axis=0, keepdims=True), x.shape)


def _rec_prologue(up_ref, cw_ref, cb_ref, wa_ref, wi_ref, ba_ref, bi_ref, lam_ref,
                  upad_ref, u_ref, a_refs, h_refs):
    cb = up_ref.shape[1]
    zeros = jnp.zeros((PAD, cb), F32)
    upad_ref[pl.ds(0, PAD), :] = zeros
    upad_ref[pl.ds(PAD + T, PAD), :] = zeros
    upad_ref[pl.ds(PAD, T), :] = up_ref[...]
    cw = cw_ref[...]
    sp = _softplus_neg(lam_ref[...])
    for c in range(N_CHUNK):
        t0 = c * REC_CHUNK
        u = cb_ref[...] + _conv_taps(upad_ref, t0, cw, 1)
        u_ref[pl.ds(t0, REC_CHUNK), :] = u
        for d in range(2):
            _, i, a, mult = _gates(u, wa_ref[d], wi_ref[d], ba_ref[d:d + 1, :], bi_ref[d:d + 1, :], sp[d:d + 1, :])
            a_refs[d][pl.ds(t0, REC_CHUNK), :] = a
            h_refs[d][pl.ds(t0, REC_CHUNK), :] = mult * (i * u)

    sub = lax.broadcasted_iota(jnp.int32, (8, cb), 0)

    def tile(k, carry):
        cf, cr = carry
        tf = pl.multiple_of(k * 8, 8)
        tr = pl.multiple_of((N_TILE - 1 - k) * 8, 8)
        af, bf = _tile_scan(a_refs[0][pl.ds(tf, 8), :], h_refs[0][pl.ds(tf, 8), :], sub, False)
        hf = af * cf + bf
        h_refs[0][pl.ds(tf, 8), :] = hf
        ar, br = _tile_scan(a_refs[1][pl.ds(tr, 8), :], h_refs[1][pl.ds(tr, 8), :], sub, True)
        hr = ar * cr + br
        h_refs[1][pl.ds(tr, 8), :] = hr
        return _last_row(hf, sub, 7), _last_row(hr, sub, 0)

    z8 = jnp.zeros((8, cb), F32)
    lax.fori_loop(0, N_TILE, tile, (z8, z8))
    return sp


def _rec_specs():
    up = pl.BlockSpec((T, REC_CB), lambda c: (0, _U_BLK + c))
    yb = pl.BlockSpec((T, REC_CB), lambda c: (0, _Y_BLK + c))
    cw = pl.BlockSpec((4, REC_CB), lambda c: (0, c))
    cbias = pl.BlockSpec((1, REC_CB), lambda c: (0, c))
    wbd = pl.BlockSpec((2, None, REC_CB, REC_CB), lambda c: (0, c, 0, 0))
    vec2 = pl.BlockSpec((2, REC_CB), lambda c: (0, c))
    col = pl.BlockSpec((T, REC_CB), lambda c: (0, c))
    return up, yb, cw, cbias, wbd, vec2, col


def _rec_fwd(z, conv_w, conv_b, wa, wi, ba, bi, lam):
    up, yb, cw, cbias, wbd, vec2, col = _rec_specs()

    def body(up_ref, yb_ref, cw_ref, cb_ref, wa_ref, wi_ref, ba_ref, bi_ref, lam_ref, g_ref,
             upad_ref, u_ref, af_ref, ar_ref, hf_ref, hr_ref):
        _rec_prologue(up_ref, cw_ref, cb_ref, wa_ref, wi_ref, ba_ref, bi_ref, lam_ref,
                      upad_ref, u_ref, (af_ref, ar_ref), (hf_ref, hr_ref))

        def chunk(c, carry):
            t0 = pl.multiple_of(c * REC_CHUNK, REC_CHUNK)
            rows = pl.ds(t0, REC_CHUNK)
            g_ref[rows, :] = ((hf_ref[rows, :] + hr_ref[rows, :]) * _gelu(yb_ref[rows, :])).astype(BF16)
            return carry

        lax.fori_loop(0, N_CHUNK, chunk, 0)

    full = pltpu.VMEM((T, REC_CB), F32)
    return pl.pallas_call(
        body, name="rec_fwd", grid=(N_CB,),
        in_specs=[up, yb, cw, cbias, wbd, wbd, vec2, vec2, vec2], out_specs=col,
        out_shape=_sds((T, D), BF16),
        scratch_shapes=[pltpu.VMEM((T + 2 * PAD, REC_CB), F32), full, full, full, full, full],
        compiler_params=_params(("parallel",)))(z, z, conv_w, conv_b, wa, wi, ba, bi, lam)


def _rec_bwd(z, dg, conv_w, conv_b, wa, wi, ba, bi, lam, after=()):
    up, yb, cw, cbias, wbd, vec2, col = _rec_specs()

    def body(up_ref, yb_ref, dg_ref, cw_ref, cb_ref, wa_ref, wi_ref, ba_ref, bi_ref, lam_ref, *rest):
        (dup_ref, dyb_ref, dcw_ref, dcb_ref, dwa_ref, dwi_ref, dba_ref, dbi_ref, dlam_ref,
         upad_ref, u_ref, af_ref, ar_ref, hf_ref, hr_ref, dh_ref, gf_ref, gr_ref, daf_ref, dar_ref,
         dupad_ref) = rest[len(after):]
        a_refs, h_refs = (af_ref, ar_ref), (hf_ref, hr_ref)
        g_refs, da_refs = (gf_ref, gr_ref), (daf_ref, dar_ref)
        sp = _rec_prologue(up_ref, cw_ref, cb_ref, wa_ref, wi_ref, ba_ref, bi_ref, lam_ref,
                           upad_ref, u_ref, a_refs, h_refs)
        cb = up_ref.shape[1]

        def gate_chunk(c, carry):
            t0 = pl.multiple_of(c * REC_CHUNK, REC_CHUNK)
            rows = pl.ds(t0, REC_CHUNK)
            y = yb_ref[rows, :]
            dgv = dg_ref[rows, :].astype(F32)
            dh_ref[rows, :] = dgv * _gelu(y)
            dyb_ref[rows, :] = (dgv * (hf_ref[rows, :] + hr_ref[rows, :]) * _gelu_grad(y)).astype(BF16)
            return carry

        lax.fori_loop(0, N_CHUNK, gate_chunk, 0)

        sub = lax.broadcasted_iota(jnp.int32, (8, cb), 0)

        def tile(k, carry):
            cf, cr = carry
            kf = N_TILE - 1 - k
            tf = pl.multiple_of(kf * 8, 8)
            tnext = pl.multiple_of(jnp.minimum(kf + 1, N_TILE - 1) * 8, 8)
            tprev = pl.multiple_of(jnp.maximum(kf - 1, 0) * 8, 8)
            a_t = af_ref[pl.ds(tf, 8), :]
            a_n = jnp.where(kf < N_TILE - 1, af_ref[pl.ds(tnext, 8), :], 0.0)
            a_sh = jnp.where(sub == 7, pltpu.roll(a_n, 7, 0), pltpu.roll(a_t, 7, 0))
            ca, cbb = _tile_scan(a_sh, dh_ref[pl.ds(tf, 8), :], sub, True)
            gf = ca * cf + cbb
            h_t = hf_ref[pl.ds(tf, 8), :]
            h_p = jnp.where(kf > 0, hf_ref[pl.ds(tprev, 8), :], 0.0)
            h_sh = jnp.where(sub == 0, pltpu.roll(h_p, 1, 0), pltpu.roll(h_t, 1, 0))
            gf_ref[pl.ds(tf, 8), :] = gf
            daf_ref[pl.ds(tf, 8), :] = gf * h_sh
            tr = pl.multiple_of(k * 8, 8)
            rnext = pl.multiple_of(jnp.minimum(k + 1, N_TILE - 1) * 8, 8)
            rprev = pl.multiple_of(jnp.maximum(k - 1, 0) * 8, 8)
            b_t = ar_ref[pl.ds(tr, 8), :]
            b_p = jnp.where(k > 0, ar_ref[pl.ds(rprev, 8), :], 0.0)
            b_sh = jnp.where(sub == 0, pltpu.roll(b_p, 1, 0), pltpu.roll(b_t, 1, 0))
            ra, rb = _tile_scan(b_sh, dh_ref[pl.ds(tr, 8), :], sub, False)
            gr = ra * cr + rb
            hr_t = hr_ref[pl.ds(tr, 8), :]
            hr_n = jnp.where(k < N_TILE - 1, hr_ref[pl.ds(rnext, 8), :], 0.0)
            hr_sh = jnp.where(sub == 7, pltpu.roll(hr_n, 7, 0), pltpu.roll(hr_t, 7, 0))
            gr_ref[pl.ds(tr, 8), :] = gr
            dar_ref[pl.ds(tr, 8), :] = gr * hr_sh
            return _last_row(gf, sub, 0), _last_row(gr, sub, 7)

        z8 = jnp.zeros((8, cb), F32)
        lax.fori_loop(0, N_TILE, tile, (z8, z8))

        zeros = jnp.zeros((PAD, cb), F32)
        dupad_ref[pl.ds(0, PAD), :] = zeros
        dupad_ref[pl.ds(PAD + T, PAD), :] = zeros
        dwa_ref[...] = jnp.zeros_like(dwa_ref)
        dwi_ref[...] = jnp.zeros_like(dwi_ref)
        dba_ref[...] = jnp.zeros_like(dba_ref)
        dbi_ref[...] = jnp.zeros_like(dbi_ref)
        dlam_ref[...] = jnp.zeros_like(dlam_ref)

        def grad_chunk(c, carry):
            t0 = pl.multiple_of(c * REC_CHUNK, REC_CHUNK)
            rows = pl.ds(t0, REC_CHUNK)
            u = u_ref[rows, :]
            ub = u.astype(BF16)
            du = jnp.zeros((REC_CHUNK, cb), F32)
            for d in range(2):
                r, i, a, mult = _gates(u, wa_ref[d], wi_ref[d], ba_ref[d:d + 1, :], bi_ref[d:d + 1, :], sp[d:d + 1, :])
                dbx = g_refs[d][rows, :]
                dmult = dbx * (i * u)
                diu = dbx * mult
                a2 = a * a
                dlog = da_refs[d][rows, :] * a - dmult * jnp.where(mult > 0.0, a2 / mult, 0.0)
                dpa = (dlog * (-LRU_C) * sp[d:d + 1, :]) * r * (1.0 - r)
                dpi = (diu * u) * i * (1.0 - i)
                dpab, dpib = dpa.astype(BF16), dpi.astype(BF16)
                du = du + diu * i + _dot(dpab, wa_ref[d], NT) + _dot(dpib, wi_ref[d], NT)
                dwa_ref[d] += _dot(ub, dpab, TN)
                dwi_ref[d] += _dot(ub, dpib, TN)
                dba_ref[d:d + 1, :] += jnp.sum(dpa, axis=0, keepdims=True)
                dbi_ref[d:d + 1, :] += jnp.sum(dpi, axis=0, keepdims=True)
                dlam_ref[d:d + 1, :] += jnp.sum(dlog * r, axis=0, keepdims=True)
            dupad_ref[pl.ds(PAD + t0, REC_CHUNK), :] = du
            return carry

        lax.fori_loop(0, N_CHUNK, grad_chunk, 0)
        dlam_ref[...] = dlam_ref[...] * (LRU_C * _sigmoid(-lam_ref[...]))

        cw = cw_ref[...]
        dcb = jnp.zeros((1, cb), F32)
        dcw = [jnp.zeros((1, cb), F32) for _ in range(4)]
        for c in range(N_CHUNK):
            t0 = c * REC_CHUNK
            du = dupad_ref[pl.ds(PAD + t0, REC_CHUNK), :]
            dcb = dcb + jnp.sum(du, axis=0, keepdims=True)
            for j in range(4):
                dcw[j] = dcw[j] + jnp.sum(du * upad_ref[pl.ds(PAD + t0 + j - 2, REC_CHUNK), :], axis=0, keepdims=True)
            dup_ref[pl.ds(t0, REC_CHUNK), :] = _conv_taps(dupad_ref, t0, cw, -1).astype(BF16)
        dcb_ref[...] = dcb
        dcw_ref[...] = jnp.concatenate(dcw, axis=0)

    full = pltpu.VMEM((T, REC_CB), F32)
    padded = pltpu.VMEM((T + 2 * PAD, REC_CB), F32)
    return pl.pallas_call(
        body, name="rec_bwd", grid=(N_CB,),
        in_specs=[up, yb, col, cw, cbias, wbd, wbd, vec2, vec2, vec2] + [_ANY] * len(after),
        out_specs=[col, col, cw, cbias, wbd, wbd, vec2, vec2, vec2],
        out_shape=[_sds((T, D), BF16), _sds((T, D), BF16), _sds((4, D), F32), _sds((1, D), F32),
                   _sds((2, N_CB, REC_CB, REC_CB), F32), _sds((2, N_CB, REC_CB, REC_CB), F32),
                   _sds((2, D), F32), _sds((2, D), F32), _sds((2, D), F32)],
        scratch_shapes=[padded, full, full, full, full, full, full, full, full, full, full, padded],
        compiler_params=_params(("parallel",)))(z, z, dg, conv_w, conv_b, wa, wi, ba, bi, lam, *after)


class _NoReducer:
    def begin(self, tag, grads):
        return ()

    def advance(self, tag, after):
        return ()


def _local_step(x, target, p, late=None, reducer=_NoReducer()):
    x = x.reshape(T, D)
    target = target.reshape(T, D)
    tb = _bias_pairs(p["rpb"])
    wa, wi = _block_diag(p["w_rg_a"]), _block_diag(p["w_rg_i"])

    h1 = _rms_fwd("rms1_fwd", x, p["ln1_g"], after=late[0] if late else ())
    if late:
        p = {**p, **late[1](h1)}
    rec_params = (p["conv_w"], p["conv_b"], wa, wi, p["b_rg_a"], p["b_rg_i"], p["lru_lambda"])
    (z,) = _mm_nn_cols("mm_z", h1, p["w_in"], F32, bias=p["b_in"])
    att = _attn_fwd(z, tb)
    g = _rec_fwd(z, *rec_params)
    if late:
        p = {**p, **late[2](g)}
    (y_att,) = _mm_nn_cols("mm_y_att", att, p["w_att_o"], F32)
    (y_rec,) = _mm_nn_rows("mm_y_rec", g, p["w_rec_o"], F32, tk=D)
    mixed = _merge_fwd(y_att, y_rec, z)

    def add_res(r, ex, outs):
        outs[0][...] = ex[0][...] + r

    res_spec = pl.BlockSpec((TM, D), lambda i, j, k: (i, 0))
    (x1,) = _mm_nn_rows("mm_x1", mixed, p["w_out"], F32, tk=D, extras=[x], extra_specs=[res_spec], epilogue=add_res)
    h2 = _rms_fwd("rms2_fwd", x1, p["ln2_g"])

    def relu2(r, ex, outs):
        outs[0][...] = r
        rp = jnp.maximum(r, 0.0)
        outs[1][...] = (rp * rp).astype(BF16)

    ff_blk = pl.BlockSpec((TM, D), lambda j, i, k: (i, j))
    f, s = _mm_nn_cols("mm_ff1", h2, p["w_ff1"], F32, epilogue=relu2,
                       out_shapes=[_sds((T, D_FF), F32), _sds((T, D_FF), BF16)], out_specs=[ff_blk, ff_blk])
    (x2,) = _mm_nn_rows("mm_x2", s, p["w_ff2"], F32, tk=D, extras=[x1], extra_specs=[res_spec], epilogue=add_res)
    loss, dx2, g_lnf = _loss_head(x2, target, p["lnf_g"])

    def relu2_bwd(r, ex, outs):
        outs[0][...] = (r * 2.0 * jnp.maximum(ex[0][...], 0.0)).astype(BF16)

    (df,) = _mm_nt_rows("mm_df", dx2, p["w_ff2"], BF16, tn=D, extras=[f],
                        extra_specs=[pl.BlockSpec((TM, D), lambda j, i, k: (i, j))], epilogue=relu2_bwd)
    (g_w_ff2,) = _mm_tn_rows("mm_g_ff2", s, dx2, tm=D)
    (g_w_ff1,) = _mm_tn_cols("mm_g_ff1", h2, df, D)
    tok = reducer.begin("ff", dict(w_ff2=g_w_ff2, w_ff1=g_w_ff1))
    (dh2,) = _mm_nt_cols("mm_dh2", df, p["w_ff1"], F32)
    dx1, g_ln2 = _rms_bwd("rms2_bwd", dh2, x1, p["ln2_g"], dx2, after=tok)

    (dmixed,) = _mm_nt_rows("mm_dmixed", dx1, p["w_out"], F32, tn=D)
    (g_w_out,) = _mm_tn_rows("mm_g_out", mixed, dx1, tm=D)
    dy_att, dy_rec, dg_att, dg_rec = _merge_bwd(dmixed, y_att, y_rec, z)
    (d_att,) = _mm_nt_cols("mm_d_att", dy_att, p["w_att_o"], BF16)
    (g_w_att_o,) = _mm_tn_cols("mm_g_att_o", att, dy_att, D // N_CHIPS)
    (d_g,) = _mm_nt_rows("mm_d_g", dy_rec, p["w_rec_o"], BF16, tn=D)
    (g_w_rec_o,) = _mm_tn_rows("mm_g_rec_o", g, dy_rec, tm=D)
    tok = reducer.advance("ff", g_w_rec_o) + reducer.begin("proj", dict(w_out=g_w_out, w_att_o=g_w_att_o, w_rec_o=g_w_rec_o))

    dq, dk, dv, ds_acc = _attn_bwd(z, tb, d_att, after=tok)
    g_rpb = _rpb_grad(ds_acc)
    tok = reducer.advance("proj", dq)
    d_up, d_yb, g_conv_w, g_conv_b, g_wa, g_wi, g_ba, g_bi, g_lam = _rec_bwd(z, d_g, *rec_params, after=tok)
    dz = jnp.concatenate([dq, dk, dv, d_up, d_yb, dg_att, dg_rec], axis=1)

    g_w_in, g_b_in = _mm_tn_cols("mm_g_in", h1, dz, D_IN // N_CHIPS, colsum=True)
    tok = reducer.begin("in", dict(w_in=g_w_in))
    (dh1,) = _mm_nt_cols("mm_dh1", dz, p["w_in"], F32, after=tok)
    tok = reducer.advance("in", dh1)
    grad_x, g_ln1 = _rms_bwd("rms1_bwd", dh1, x, p["ln1_g"], dx1, after=tok)

    grads = dict(ln1_g=g_ln1, w_in=g_w_in, b_in=g_b_in, rpb=g_rpb, w_att_o=g_w_att_o, conv_w=g_conv_w,
                 conv_b=g_conv_b, w_rg_a=_block_diag_grad(g_wa), b_rg_a=g_ba, w_rg_i=_block_diag_grad(g_wi),
                 b_rg_i=g_bi, lru_lambda=g_lam, w_rec_o=g_w_rec_o, w_out=g_w_out, ln2_g=g_ln2,
                 w_ff1=g_w_ff1, w_ff2=g_w_ff2, lnf_g=g_lnf)
    return loss, grad_x.reshape(1, T, D), grads


_ANY = pl.BlockSpec(memory_space=pl.ANY)
N_PEERS = N_CHIPS - 1


def _place():
    x, y, c = lax.axis_index("x"), lax.axis_index("y"), lax.axis_index("c")
    peers = [(1 - x, y), (x, 1 - y), (1 - x, 1 - y)]
    return x, y, c, 2 * x + y, peers


def _remote(src, dst, send_sem, recv_sem, dev):
    return pltpu.make_async_remote_copy(src_ref=src, dst_ref=dst, send_sem=send_sem, recv_sem=recv_sem,
                                        device_id=dev, device_id_type=MESH)


def _prefetch_call(body, name, ids, grid, in_specs, out_specs, out_shape, args):
    spec = pltpu.PrefetchScalarGridSpec(num_scalar_prefetch=1, grid=grid, in_specs=in_specs, out_specs=out_specs)
    return pl.pallas_call(body, name=name, grid_spec=spec, out_shape=out_shape,
                          compiler_params=_params(("parallel",) * len(grid)))(ids, *args)


def _cast_bf16(name, w, chip_id, after=()):
    rows, cols = w.shape
    rb = min(rows, 256)

    def body(ids_ref, w_ref, *rest):
        rest[-1][...] = w_ref[...].astype(BF16)

    return _prefetch_call(body, name, chip_id, (rows // rb,),
                          [pl.BlockSpec((rb, cols), lambda i, ids: (i, 0))] + [_ANY] * len(after),
                          pl.BlockSpec((None, rb, cols), lambda i, ids: (ids[0], i, 0)),
                          _sds((N_CHIPS, rows, cols), BF16), (w, *after))


def _dma_sems(*counts):
    return [pltpu.SemaphoreType.DMA((k,)) for k in counts]


_HBM = pl.BlockSpec(memory_space=pltpu.HBM)
_SEM = pl.BlockSpec(memory_space=pltpu.SEMAPHORE)
_SPLIT_COPY = pltpu.CompilerParams(has_side_effects=pltpu.SideEffectType.DATAFLOW_SIDE_EFFECTING)


def _hbm(arrays):
    return [pltpu.with_memory_space_constraint(a, pltpu.HBM) for a in arrays]


def _hbm_like(arrays):
    return [pltpu.HBM(a.shape, a.dtype) for a in arrays]


def _halves(buf, c):
    half = buf.shape[1] // 2
    return pl.ds(c * half, half), pl.ds((1 - c) * half, half)


def _gather_start(name, slots):
    n = len(slots)
    nk = n * N_PEERS

    def body(*refs):
        bufs = refs[n:2 * n]
        send_sems, recv_sems, token = refs[2 * n:]
        x, y, c, chip, peers = _place()
        for t in range(n):
            mine, _ = _halves(bufs[t], c)
            for r, (px, py) in enumerate(peers):
                k = t * N_PEERS + r
                own = bufs[t].at[chip, mine]
                _remote(own, own, send_sems.at[k], recv_sems.at[k], (px, py, c)).start()
        token[...] = jnp.zeros_like(token)

    res = pl.pallas_call(
        body, name=name, in_specs=[_HBM] * n, out_specs=[_HBM] * n + [_SEM, _SEM, pl.BlockSpec(memory_space=pltpu.VMEM)],
        out_shape=_hbm_like(slots) + [pltpu.SemaphoreType.DMA((nk,)), pltpu.SemaphoreType.DMA((nk,)),
                                      _sds((8, 128), F32)],
        input_output_aliases={t: t for t in range(n)}, compiler_params=_SPLIT_COPY)(*_hbm(slots))
    return res[:n], (res[n], res[n + 1]), res[n + 2]


def _gather_wait(name, bufs, sems, after):
    n = len(bufs)

    def body(*refs):
        ins = refs[:n]
        send_sems, recv_sems = refs[n], refs[n + 1]
        x, y, c, chip, peers = _place()
        for t in range(n):
            mine, _ = _halves(ins[t], c)
            for r, (px, py) in enumerate(peers):
                k = t * N_PEERS + r
                cp = _remote(ins[t].at[chip, mine], ins[t].at[2 * px + py, mine], send_sems.at[k], recv_sems.at[k],
                             (px, py, c))
                cp.wait_send()
                cp.wait_recv()

    return pl.pallas_call(
        body, name=name, in_specs=[_HBM] * n + [_SEM, _SEM, _ANY], out_specs=[_HBM] * n, out_shape=_hbm_like(bufs),
        input_output_aliases={t: t for t in range(n)}, compiler_params=_SPLIT_COPY)(*bufs, *sems, after)


def _gather_forward(name, bufs):
    n = len(bufs)
    nk = n * N_PEERS

    def body(*refs):
        outs = refs[n:2 * n]
        send_sems, recv_sems = refs[2 * n:]
        x, y, c, chip, peers = _place()
        sibling = (x, y, 1 - c)
        sends = []
        for t in range(n):
            mine, _ = _halves(outs[t], c)
            for r, (px, py) in enumerate(peers):
                k = t * N_PEERS + r
                landed = outs[t].at[2 * px + py, mine]
                sends.append(_remote(landed, landed, send_sems.at[k], recv_sems.at[k], sibling))
                sends[-1].start()
        for t in range(n):
            _, theirs = _halves(outs[t], c)
            for r, (px, py) in enumerate(peers):
                k = t * N_PEERS + r
                landed = outs[t].at[2 * px + py, theirs]
                _remote(landed, landed, send_sems.at[k], recv_sems.at[k], sibling).wait_recv()
        for cp in sends:
            cp.wait_send()

    return pl.pallas_call(
        body, name=name, in_specs=[_ANY] * n, out_specs=[_ANY] * n, out_shape=[_sds(b.shape, b.dtype) for b in bufs],
        input_output_aliases={t: t for t in range(n)}, scratch_shapes=_dma_sems(nk, nk))(*bufs)


def _pair_copies(n, srcs, lands, send_sems, recv_sems):
    x, y, c, _, _ = _place()
    sibling = (x, y, 1 - c)
    copies = []
    for t in range(n):
        half = srcs[t].shape[1] // 2
        for j in range(N_CHIPS):
            k = t * N_CHIPS + j
            copies.append(_remote(srcs[t].at[j, pl.ds((1 - c) * half, half)], lands[t].at[j],
                                  send_sems.at[k], recv_sems.at[k], sibling))
    for t in range(n, len(srcs)):
        k = n * N_CHIPS + t - n
        copies.append(_remote(srcs[t], lands[t], send_sems.at[k], recv_sems.at[k], sibling))
    return copies


def _pair_start(name, grads, wholes=()):
    n = len(grads)
    srcs = list(grads) + list(wholes)
    m = len(srcs)
    lands = [pltpu.HBM((N_CHIPS, g.shape[1] // 2, g.shape[2]), F32) for g in grads] + _hbm_like(wholes)
    ns = n * N_CHIPS + len(wholes)

    def body(*refs):
        src_refs, land_refs = refs[m:2 * m], refs[2 * m:3 * m]
        send_sems, recv_sems, token = refs[3 * m:]
        for cp in _pair_copies(n, src_refs, land_refs, send_sems, recv_sems):
            cp.start()
        token[...] = jnp.zeros_like(token)

    res = pl.pallas_call(
        body, name=name, in_specs=[_HBM] * m,
        out_specs=[_HBM] * (2 * m) + [_SEM, _SEM, pl.BlockSpec(memory_space=pltpu.VMEM)],
        out_shape=_hbm_like(srcs) + lands + [pltpu.SemaphoreType.DMA((ns,)), pltpu.SemaphoreType.DMA((ns,)),
                                             _sds((8, 128), F32)],
        input_output_aliases={t: t for t in range(m)}, compiler_params=_SPLIT_COPY)(*_hbm(srcs))
    return (res[:m], res[m:2 * m], (res[2 * m], res[2 * m + 1])), res[2 * m + 2]


def _pair_wait(name, flight, n, after):
    srcs, lands, sems = flight
    m = len(srcs)

    def body(*refs):
        for cp in _pair_copies(n, refs[:m], refs[m:2 * m], refs[2 * m], refs[2 * m + 1]):
            cp.wait_send()
            cp.wait_recv()

    res = pl.pallas_call(
        body, name=name, in_specs=[_HBM] * (2 * m) + [_SEM, _SEM, _ANY], out_specs=[_HBM] * (2 * m),
        out_shape=_hbm_like(srcs) + _hbm_like(lands), input_output_aliases={t: t for t in range(2 * m)},
        compiler_params=_SPLIT_COPY)(*srcs, *lands, *sems, after)
    return res[:m], res[m:]


def _chip_copies(srcs, lands, small_src, small_land, send_sems, recv_sems):
    x, y, c, chip, peers = _place()
    n = len(srcs)
    copies = []
    for r, (px, py) in enumerate(peers):
        for t in range(n):
            k = t * N_PEERS + r
            copies.append(_remote(srcs[t].at[2 * px + py], lands[t].at[r], send_sems.at[k], recv_sems.at[k], (px, py, c)))
        if small_src is not None:
            k = n * N_PEERS + r
            half_s = small_src.shape[0] // 2
            copies.append(_remote(small_src.at[pl.ds(c * half_s, half_s)], small_land.at[r],
                                  send_sems.at[k], recv_sems.at[k], (px, py, c)))
    return copies


def _chip_start(name, sums_bf16, small=None):
    n = len(sums_bf16)
    srcs = list(sums_bf16) + ([small] if small is not None else [])
    m = len(srcs)
    lands = [pltpu.HBM((N_PEERS,) + s.shape[1:], BF16) for s in sums_bf16]
    if small is not None:
        lands.append(pltpu.HBM((N_PEERS, small.shape[0] // 2, 128), F32))
    nk = m * N_PEERS

    def body(*refs):
        src_refs, land_refs = refs[m:2 * m], refs[2 * m:3 * m]
        send_sems, recv_sems, token = refs[3 * m:]
        small_src, small_land = (src_refs[n], land_refs[n]) if small is not None else (None, None)
        for cp in _chip_copies(src_refs[:n], land_refs[:n], small_src, small_land, send_sems, recv_sems):
            cp.start()
        token[...] = jnp.zeros_like(token)

    res = pl.pallas_call(
        body, name=name, in_specs=[_HBM] * m,
        out_specs=[_HBM] * (2 * m) + [_SEM, _SEM, pl.BlockSpec(memory_space=pltpu.VMEM)],
        out_shape=_hbm_like(srcs) + lands + [pltpu.SemaphoreType.DMA((nk,)), pltpu.SemaphoreType.DMA((nk,)),
                                             _sds((8, 128), F32)],
        input_output_aliases={t: t for t in range(m)}, compiler_params=_SPLIT_COPY)(*_hbm(srcs))
    return (res[:m], res[m:2 * m], (res[2 * m], res[2 * m + 1])), res[2 * m + 2]


def _chip_wait(name, flight, with_small, after):
    srcs, lands, sems = flight
    m = len(srcs)
    n = m - 1 if with_small else m

    def body(*refs):
        src_refs, land_refs = refs[:m], refs[m:2 * m]
        send_sems, recv_sems = refs[2 * m], refs[2 * m + 1]
        small_src, small_land = (src_refs[n], land_refs[n]) if with_small else (None, None)
        for cp in _chip_copies(src_refs[:n], land_refs[:n], small_src, small_land, send_sems, recv_sems):
            cp.wait_send()
            cp.wait_recv()

    res = pl.pallas_call(
        body, name=name, in_specs=[_HBM] * (2 * m) + [_SEM, _SEM, _ANY], out_specs=[_HBM] * (2 * m),
        out_shape=_hbm_like(srcs) + _hbm_like(lands), input_output_aliases={t: t for t in range(2 * m)},
        compiler_params=_SPLIT_COPY)(*srcs, *lands, *sems, after)
    return res[:m], res[m:]


def _half_swap(name, bufs):
    n = len(bufs)

    def body(*refs):
        outs = refs[n:2 * n]
        send_sem, recv_sem = refs[2 * n:]
        x, y, c, _, _ = _place()
        sibling = (x, y, 1 - c)
        copies = []
        for t in range(n):
            h = outs[t].shape[0] // 2
            mine = outs[t].at[pl.ds(c * h, h)]
            copies.append(_remote(mine, mine, send_sem.at[t], recv_sem.at[t], sibling))
            copies[-1].start()
        for t in range(n):
            h = outs[t].shape[0] // 2
            theirs = outs[t].at[pl.ds((1 - c) * h, h)]
            _remote(theirs, theirs, send_sem.at[t], recv_sem.at[t], sibling).wait_recv()
        for cp in copies:
            cp.wait_send()

    return pl.pallas_call(
        body, name=name, in_specs=[_ANY] * n, out_specs=[_ANY] * n,
        out_shape=[_sds(b.shape, b.dtype) for b in bufs], input_output_aliases={t: t for t in range(n)},
        scratch_shapes=_dma_sems(n, n))(*bufs)


def _pair_sum(name, grad, got, ids):
    _, rows, cols = got.shape
    rb = min(rows, 256)
    nb = rows // rb
    blk = pl.BlockSpec((None, rb, cols), lambda j, i, ids: (j, i, 0))
    mine = pl.BlockSpec((None, rb, cols), lambda j, i, ids: (j, ids[1] * nb + i, 0))

    def body(ids_ref, a_ref, b_ref, s_ref, sb_ref):
        s = a_ref[...] + b_ref[...]
        s_ref[...] = s
        sb_ref[...] = s.astype(BF16)

    return _prefetch_call(body, name, ids, (N_CHIPS, nb), [mine, blk], [blk, blk],
                          [_sds(got.shape, F32), _sds(got.shape, BF16)], (grad, got))


def _chip_sum(name, sums, got, ids):
    _, rows, cols = sums.shape
    rb = min(rows, 256)
    nb = rows // rb
    own = pl.BlockSpec((None, rb, cols), lambda i, ids: (ids[0], i, 0))
    blk3 = pl.BlockSpec((N_PEERS, rb, cols), lambda i, ids: (0, i, 0))
    out = pl.BlockSpec((rb, cols), lambda i, ids: (ids[1] * nb + i, 0))

    def body(ids_ref, a_ref, b_ref, o_ref):
        o_ref[...] = ((a_ref[...] + b_ref[0].astype(F32)) + b_ref[1].astype(F32)) + b_ref[2].astype(F32)

    return _prefetch_call(body, name, ids, (nb,), [own, blk3], out, _sds((2 * rows, cols), F32), (sums, got))


SMALL_RB = 280


def _small_pair_sum(own, got):
    blk = pl.BlockSpec((SMALL_RB, 128), lambda i: (i, 0))

    def body(a_ref, b_ref, o_ref):
        o_ref[...] = a_ref[...] + b_ref[...]

    return pl.pallas_call(body, name="small_pair_sum", grid=(own.shape[0] // SMALL_RB,), in_specs=[blk, blk],
                          out_specs=blk, out_shape=_sds(own.shape, F32),
                          compiler_params=_params(("parallel",)))(own, got)


def _small_chip_sum(pair, got, ids):
    nb = pair.shape[0] // 2 // SMALL_RB
    half = pl.BlockSpec((SMALL_RB, 128), lambda i, ids: (ids[1] * nb + i, 0))
    blk3 = pl.BlockSpec((N_PEERS, SMALL_RB, 128), lambda i, ids: (0, i, 0))

    def body(ids_ref, a_ref, b_ref, o_ref):
        o_ref[...] = (a_ref[...] + b_ref[1]) + (b_ref[0] + b_ref[2])

    return _prefetch_call(body, "small_chip_sum", ids, (nb,), [half, blk3], half, _sds(pair.shape, F32), (pair, got))


def _adamw_math(w, g, m, v):
    m = ADAM_B1 * m + (1.0 - ADAM_B1) * g
    v = ADAM_B2 * v + (1.0 - ADAM_B2) * (g * g)
    m_hat = m / (1.0 - ADAM_B1 ** ADAM_STEP)
    v_hat = v / (1.0 - ADAM_B2 ** ADAM_STEP)
    delta = -ADAM_LR * (m_hat / (jnp.sqrt(v_hat) + ADAM_EPS) + ADAM_WD * w)
    return delta, m, v


def _adamw(name, w, g, m, v, rb=None):
    rows, cols = w.shape
    rb = rows if rb is None else rb
    blk = pl.BlockSpec((rb, cols), lambda i: (i, 0))

    def body(w_ref, g_ref, m_ref, v_ref, d_ref, nm_ref, nv_ref):
        d, nm, nv = _adamw_math(w_ref[...], g_ref[...], m_ref[...], v_ref[...])
        d_ref[...] = d
        nm_ref[...] = nm
        nv_ref[...] = nv

    return pl.pallas_call(body, name=name, grid=(rows // rb,), in_specs=[blk] * 4, out_specs=[blk] * 3,
                          out_shape=[_sds(w.shape, F32)] * 3, compiler_params=_params(("parallel",)))(w, g, m, v)


BIG = ("w_in", "w_att_o", "w_rec_o", "w_out", "w_ff1", "w_ff2")
SHARDED_VECS = ("conv_w", "b_rg_a", "b_rg_i", "lru_lambda")
SMALL = ("ln1_g", "b_in", "rpb", "conv_w", "conv_b", "w_rg_a", "b_rg_a", "w_rg_i", "b_rg_i", "lru_lambda",
         "ln2_g", "lnf_g")
SMALL_ROWS = 2240
ORDER = ("ln1_g", "w_in", "b_in", "rpb", "w_att_o", "conv_w", "conv_b", "w_rg_a", "b_rg_a", "w_rg_i", "b_rg_i",
         "lru_lambda", "w_rec_o", "w_out", "ln2_g", "w_ff1", "w_ff2", "lnf_g")


def _pack_small(grads, loss):
    parts, sizes = [], {}
    for n in SMALL:
        flat = grads[n].reshape(-1)
        pad = (-flat.shape[0]) % 128
        sizes[n] = (flat.shape[0], flat.shape[0] + pad)
        parts.append(jnp.pad(flat, (0, pad)))
    total = sum(s[1] for s in sizes.values())
    parts.append(jnp.pad(loss.reshape(1), (0, SMALL_ROWS * 128 - total - 1)))
    return jnp.concatenate(parts).reshape(SMALL_ROWS, 128), sizes


def _unpack_small(buf, sizes, shapes):
    flat = buf.reshape(-1)
    out, pos = {}, 0
    for n in SMALL:
        size, padded = sizes[n]
        out[n] = flat[pos:pos + size].reshape(shapes[n])
        pos += padded
    return out, flat[pos]


def _gather_weights(w, chip):
    chip_id = chip.astype(jnp.int32).reshape(1)
    vec_rows = [w[n][0] for n in SHARDED_VECS]
    vec_shard = jnp.concatenate(vec_rows + [jnp.zeros((16 - 10, D // N_CHIPS), F32)], axis=0)
    vec_slots = lax.dynamic_update_slice(jnp.zeros((N_CHIPS, 16, D // N_CHIPS), F32), vec_shard[None], (chip, 0, 0))
    bufs_a, sems_a, token_a = _gather_start("gather_start_first", [_cast_bf16("cast_w_in", w["w_in"][0], chip_id), vec_slots])
    rest_names = BIG[1:]
    bufs_b, sems_b, token_b = _gather_start(
        "gather_start_rest", [_cast_bf16("cast_" + n, w[n][0], chip_id, after=(token_a,)) for n in rest_names])

    def first(after):
        w_in_full, vec_full = _gather_forward("gather_forward_first", _gather_wait("gather_wait_first", bufs_a, sems_a, after))
        vecs = vec_full.transpose(1, 0, 2).reshape(16, D)
        return dict(w_in=w_in_full, conv_w=vecs[0:4], b_rg_a=vecs[4:6], b_rg_i=vecs[6:8], lru_lambda=vecs[8:10])

    def rest(after):
        full = dict(zip(rest_names, _gather_forward("gather_forward_rest",
                                                    _gather_wait("gather_wait_rest", bufs_b, sems_b, after))))
        return dict(w_att_o=full["w_att_o"], w_ff1=full["w_ff1"], w_rec_o=full["w_rec_o"].reshape(D, D),
                    w_out=full["w_out"].reshape(D, D), w_ff2=full["w_ff2"].reshape(D_FF, D))

    p = dict(ln1_g=w["ln1_g"], b_in=w["b_in"], rpb=w["rpb"][0], conv_b=w["conv_b"], w_rg_a=w["w_rg_a"][0],
             w_rg_i=w["w_rg_i"][0], ln2_g=w["ln2_g"], lnf_g=w["lnf_g"].reshape(1, D))
    return p, ((token_b,), first, rest)


class _Reducer:
    def __init__(self, ids):
        self.ids = ids
        self.groups = {}

    def begin(self, tag, grads, small=None):
        names = list(grads)
        big = [grads[n].reshape(N_CHIPS, -1, grads[n].shape[-1]) for n in names]
        flight, token = _pair_start("pair_start_" + tag, big, [] if small is None else [small])
        self.groups[tag] = dict(names=names, pair=flight, small=small is not None)
        return (token,)

    def advance(self, tag, after):
        grp = self.groups[tag]
        n = len(grp["names"])
        mine, got = _pair_wait("pair_wait_" + tag, grp["pair"], n, after)
        sums = [_pair_sum("pair_sum_" + name, a, b, self.ids) for name, a, b in zip(grp["names"], mine, got)]
        small_sum = _small_pair_sum(mine[n], got[n]) if grp["small"] else None
        grp["chip"], token = _chip_start("chip_start_" + tag, [s[1] for s in sums], small_sum)
        grp["sums"] = [s[0] for s in sums]
        return (token,)

    def finish(self, tag, after):
        grp = self.groups[tag]
        srcs, lands = _chip_wait("chip_wait_" + tag, grp["chip"], grp["small"], after)
        halves = [_chip_sum("chip_sum_" + name, s, b, self.ids) for name, s, b in zip(grp["names"], grp["sums"], lands)]
        if grp["small"]:
            halves.append(_small_chip_sum(srcs[-1], lands[-1], self.ids))
        return _half_swap("half_swap_" + tag, halves)


def kernel(x, ln1_g, w_in, b_in, rpb, w_att_o, conv_w, conv_b, w_rg_a, b_rg_a, w_rg_i, b_rg_i, lru_lambda, w_rec_o, w_out, ln2_g, w_ff1, w_ff2, lnf_g, loss_target, m_ln1_g, m_w_in, m_b_in, m_rpb, m_w_att_o, m_conv_w, m_conv_b, m_w_rg_a, m_b_rg_a, m_w_rg_i, m_b_rg_i, m_lru_lambda, m_w_rec_o, m_w_out, m_ln2_g, m_w_ff1, m_w_ff2, m_lnf_g, v_ln1_g, v_w_in, v_b_in, v_rpb, v_w_att_o, v_conv_w, v_conv_b, v_w_rg_a, v_b_rg_a, v_w_rg_i, v_b_rg_i, v_lru_lambda, v_w_rec_o, v_w_out, v_ln2_g, v_w_ff1, v_w_ff2, v_lnf_g):
    w = dict(ln1_g=ln1_g, w_in=w_in, b_in=b_in, rpb=rpb, w_att_o=w_att_o, conv_w=conv_w, conv_b=conv_b,
             w_rg_a=w_rg_a, b_rg_a=b_rg_a, w_rg_i=w_rg_i, b_rg_i=b_rg_i, lru_lambda=lru_lambda, w_rec_o=w_rec_o,
             w_out=w_out, ln2_g=ln2_g, w_ff1=w_ff1, w_ff2=w_ff2, lnf_g=lnf_g)
    m = dict(ln1_g=m_ln1_g, w_in=m_w_in, b_in=m_b_in, rpb=m_rpb, w_att_o=m_w_att_o, conv_w=m_conv_w,
             conv_b=m_conv_b, w_rg_a=m_w_rg_a, b_rg_a=m_b_rg_a, w_rg_i=m_w_rg_i, b_rg_i=m_b_rg_i,
             lru_lambda=m_lru_lambda, w_rec_o=m_w_rec_o, w_out=m_w_out, ln2_g=m_ln2_g, w_ff1=m_w_ff1,
             w_ff2=m_w_ff2, lnf_g=m_lnf_g)
    v = dict(ln1_g=v_ln1_g, w_in=v_w_in, b_in=v_b_in, rpb=v_rpb, w_att_o=v_w_att_o, conv_w=v_conv_w,
             conv_b=v_conv_b, w_rg_a=v_w_rg_a, b_rg_a=v_b_rg_a, w_rg_i=v_w_rg_i, b_rg_i=v_b_rg_i,
             lru_lambda=v_lru_lambda, w_rec_o=v_w_rec_o, w_out=v_w_out, ln2_g=v_ln2_g, w_ff1=v_w_ff1,
             w_ff2=v_w_ff2, lnf_g=v_lnf_g)
    chip = 2 * lax.axis_index("x") + lax.axis_index("y")
    ids = jnp.stack([chip, lax.axis_index("c")]).astype(jnp.int32)

    out_grad, out_delta, out_m, out_v = {}, {}, {}, {}

    def update(n, gn):
        shape = w[n].shape
        two_d, rb = (gn.shape, 256) if n in BIG else ((int(np.prod(shape[:-1])), shape[-1]), None)
        gn = gn.reshape(two_d)
        d, nm, nv = _adamw("adamw_" + n, w[n].reshape(two_d), gn, m[n].reshape(two_d), v[n].reshape(two_d), rb)
        out_grad[n], out_delta[n], out_m[n], out_v[n] = (gn.reshape(shape), d.reshape(shape), nm.reshape(shape),
                                                         nv.reshape(shape))
        return d

    reducer = _Reducer(ids)
    p, late = _gather_weights(w, chip)
    loss, grad_x, g = _local_step(x, loss_target, p, late, reducer)
    small, sizes = _pack_small(g, loss)
    after = reducer.begin("small", {}, small)[0]
    for tag in ("ff", "proj", "in"):
        for n, red in zip(reducer.groups[tag]["names"], reducer.finish(tag, after)):
            after = update(n, red)
        if tag == "ff":
            after = reducer.advance("small", after)[0]
    (small_red,) = reducer.finish("small", after)
    gsmall, loss = _unpack_small(small_red, sizes, {n: g[n].shape for n in SMALL})
    for n in SMALL:
        gn = gsmall[n]
        if n in SHARDED_VECS:
            gn = lax.dynamic_slice_in_dim(gn, chip * (D // N_CHIPS), D // N_CHIPS, axis=1)
        update(n, gn)
    return (loss, grad_x, *[out_grad[n] for n in ORDER], *[out_delta[n] for n in ORDER],
            *[out_m[n] for n in ORDER], *[out_v[n] for n in ORDER])
```

```python
import functools

import numpy as np
import jax
import jax.numpy as jnp
from jax import lax
from jax.experimental import pallas as pl
from jax.experimental.pallas import tpu as pltpu

F32 = jnp.float32
BF16 = jnp.bfloat16

T = 2048
D = 1024
D_ATT = 512
D_IN = 5632
D_FF = 4096
N_HEADS = 8
HEAD_DIM = 64
GRID_W = 64
N_ROWS = T // GRID_W
WIN_H = 8
WIN_W = 16
KEYS = WIN_H * GRID_W
N_CHIPS = 4
EPS = 1e-6
LRU_C = 8.0
SCALE = HEAD_DIM ** -0.5
REC_CB = 256
REC_CHUNK = 256
PAD = 8

ADAM_LR = 0.001
ADAM_B1 = 0.9
ADAM_B2 = 0.999
ADAM_EPS = 1e-08
ADAM_WD = 0.01
ADAM_STEP = 10

VMEM_LIMIT = 56 * 1024 * 1024

NN = (((1,), (0,)), ((), ()))
NT = (((1,), (1,)), ((), ()))
TN = (((0,), (0,)), ((), ()))
MESH = pl.DeviceIdType.MESH


def _params(sem=None):
    return pltpu.CompilerParams(dimension_semantics=sem, vmem_limit_bytes=VMEM_LIMIT)


def _dot(a, b, dims):
    return lax.dot_general(a, b, dims, preferred_element_type=F32)


def _sigmoid(x):
    return 0.5 * jnp.tanh(0.5 * x) + 0.5


def _matmul(name, a, b, *, dims, grid, a_spec, b_spec, out_shapes, out_specs, acc_shape,
            extras=(), extra_specs=(), epilogue=None, colsum_spec=None, colsum_shape=None, after=()):
    nk = grid[2]
    n_extra = len(extras)
    n_out = len(out_shapes)
    with_colsum = colsum_spec is not None

    def body(a_ref, b_ref, *rest):
        ex = rest[:n_extra]
        rest = rest[:n_extra] + rest[n_extra + len(after):]
        outs = rest[n_extra:n_extra + n_out]
        pos = n_extra + n_out
        cs_out = rest[pos] if with_colsum else None
        pos += 1 if with_colsum else 0
        acc = rest[pos]
        cs_acc = rest[pos + 1] if with_colsum else None
        k = pl.program_id(2)

        @pl.when(k == 0)
        def _():
            acc[...] = jnp.zeros_like(acc)
            if with_colsum:
                cs_acc[...] = jnp.zeros_like(cs_acc)

        bv = b_ref[...]
        acc[...] += _dot(a_ref[...].astype(BF16), bv.astype(BF16), dims)
        if with_colsum:
            cs_acc[...] += jnp.sum(bv.astype(F32), axis=0, keepdims=True)

        @pl.when(k == nk - 1)
        def _():
            r = acc[...]
            if epilogue is None:
                outs[0][...] = r.astype(outs[0].dtype)
            else:
                epilogue(r, ex, outs)
            if with_colsum:
                cs_out[...] = cs_acc[...]

    shapes = list(out_shapes)
    specs = list(out_specs)
    scratch = [pltpu.VMEM(acc_shape, F32)]
    if with_colsum:
        shapes.append(colsum_shape)
        specs.append(colsum_spec)
        scratch.append(pltpu.VMEM((1, acc_shape[1]), F32))
    res = pl.pallas_call(
        body, name=name, grid=grid,
        in_specs=[a_spec, b_spec, *extra_specs] + [_ANY] * len(after),
        out_specs=specs, out_shape=shapes, scratch_shapes=scratch,
        compiler_params=_params(("parallel", "parallel", "arbitrary")),
    )(a, b, *extras, *after)
    return res


def _sds(shape, dtype):
    return jax.ShapeDtypeStruct(shape, dtype)


TM = 512
NI = T // TM


def _mm_nn_cols(name, a, wg, out_dtype, *, bias=None, extras=(), extra_specs=(), epilogue=None,
                out_shapes=None, out_specs=None):
    k_dim, n4 = wg.shape[1], wg.shape[2]
    ex, exs = list(extras), list(extra_specs)
    if bias is not None:
        ex = [bias] + ex
        exs = [pl.BlockSpec((1, n4), lambda j, i, k: (0, j))] + exs
        user_ep = epilogue

        def epilogue(r, e, outs):
            r = r + e[0][...]
            if user_ep is None:
                outs[0][...] = r.astype(outs[0].dtype)
            else:
                user_ep(r, e[1:], outs)
    if out_shapes is None:
        out_shapes = [_sds((T, N_CHIPS * n4), out_dtype)]
        out_specs = [pl.BlockSpec((TM, n4), lambda j, i, k: (i, j))]
    return _matmul(
        name, a, wg, dims=NN, grid=(N_CHIPS, NI, 1),
        a_spec=pl.BlockSpec((TM, k_dim), lambda j, i, k: (i, 0)),
        b_spec=pl.BlockSpec((None, k_dim, n4), lambda j, i, k: (j, 0, 0)),
        out_shapes=out_shapes, out_specs=out_specs, acc_shape=(TM, n4),
        extras=ex, extra_specs=exs, epilogue=epilogue)


def _mm_nn_rows(name, a, w, out_dtype, *, tk, extras=(), extra_specs=(), epilogue=None):
    k_dim, n = w.shape
    return _matmul(
        name, a, w, dims=NN, grid=(NI, 1, k_dim // tk),
        a_spec=pl.BlockSpec((TM, tk), lambda i, j, k: (i, k)),
        b_spec=pl.BlockSpec((tk, n), lambda i, j, k: (k, 0)),
        out_shapes=[_sds((T, n), out_dtype)],
        out_specs=[pl.BlockSpec((TM, n), lambda i, j, k: (i, 0))], acc_shape=(TM, n),
        extras=extras, extra_specs=extra_specs, epilogue=epilogue)


def _mm_nt_cols(name, a, wg, out_dtype, after=()):
    k_dim, n4 = wg.shape[1], wg.shape[2]
    return _matmul(
        name, a, wg, dims=NT, grid=(NI, 1, N_CHIPS),
        a_spec=pl.BlockSpec((TM, n4), lambda i, j, k: (i, k)),
        b_spec=pl.BlockSpec((None, k_dim, n4), lambda i, j, k: (k, 0, 0)),
        out_shapes=[_sds((T, k_dim), out_dtype)],
        out_specs=[pl.BlockSpec((TM, k_dim), lambda i, j, k: (i, 0))], acc_shape=(TM, k_dim), after=after)


def _mm_nt_rows(name, a, w, out_dtype, *, tn, extras=(), extra_specs=(), epilogue=None):
    k_dim, n = w.shape
    return _matmul(
        name, a, w, dims=NT, grid=(k_dim // tn, NI, 1),
        a_spec=pl.BlockSpec((TM, n), lambda j, i, k: (i, 0)),
        b_spec=pl.BlockSpec((tn, n), lambda j, i, k: (j, 0)),
        out_shapes=[_sds((T, k_dim), out_dtype)],
        out_specs=[pl.BlockSpec((TM, tn), lambda j, i, k: (i, j))], acc_shape=(TM, tn),
        extras=extras, extra_specs=extra_specs, epilogue=epilogue)


def _mm_tn_cols(name, a, g, n4, *, colsum=False):
    k_dim = a.shape[1]
    kw = {}
    if colsum:
        kw = dict(colsum_spec=pl.BlockSpec((1, n4), lambda j, i, k: (0, j)),
                  colsum_shape=_sds((1, N_CHIPS * n4), F32))
    return _matmul(
        name, a, g, dims=TN, grid=(N_CHIPS, 1, NI),
        a_spec=pl.BlockSpec((TM, k_dim), lambda j, i, k: (k, 0)),
        b_spec=pl.BlockSpec((TM, n4), lambda j, i, k: (k, j)),
        out_shapes=[_sds((N_CHIPS, k_dim, n4), F32)],
        out_specs=[pl.BlockSpec((None, k_dim, n4), lambda j, i, k: (j, 0, 0))],
        acc_shape=(k_dim, n4), **kw)


def _mm_tn_rows(name, a, g, *, tm):
    k_dim, n = a.shape[1], g.shape[1]
    return _matmul(
        name, a, g, dims=TN, grid=(k_dim // tm, 1, NI),
        a_spec=pl.BlockSpec((TM, tm), lambda j, i, k: (k, j)),
        b_spec=pl.BlockSpec((TM, n), lambda j, i, k: (k, 0)),
        out_shapes=[_sds((k_dim, n), F32)],
        out_specs=[pl.BlockSpec((tm, n), lambda j, i, k: (j, 0))], acc_shape=(tm, n))


TE = 256
NE = T // TE
_ROW = pl.BlockSpec((TE, D), lambda i: (i, 0))
_VEC = pl.BlockSpec((1, D), lambda i: (0, 0))


def _rms_fwd(name, x, g, after=()):
    def body(x_ref, g_ref, *rest):
        h_ref = rest[-1]
        xv = x_ref[...]
        rstd = lax.rsqrt(jnp.mean(xv * xv, axis=-1, keepdims=True) + EPS)
        h_ref[...] = (xv * rstd * g_ref[...]).astype(BF16)

    return pl.pallas_call(body, name=name, grid=(NE,), in_specs=[_ROW, _VEC] + [_ANY] * len(after), out_specs=_ROW,
                          out_shape=_sds((T, D), BF16), compiler_params=_params(("parallel",)))(x, g, *after)


def _rms_bwd(name, dh, x, g, dres, after=()):
    def body(dh_ref, x_ref, g_ref, dres_ref, *rest):
        dx_ref, dg_ref = rest[-2:]
        xv = x_ref[...]
        rstd = lax.rsqrt(jnp.mean(xv * xv, axis=-1, keepdims=True) + EPS)
        xhat = xv * rstd
        dhv = dh_ref[...]
        dy = dhv * g_ref[...]
        dx_ref[...] = dres_ref[...] + rstd * (dy - xhat * jnp.mean(dy * xhat, axis=-1, keepdims=True))

        @pl.when(pl.program_id(0) == 0)
        def _():
            dg_ref[...] = jnp.zeros_like(dg_ref)

        dg_ref[...] += jnp.sum(dhv * xhat, axis=0, keepdims=True)

    return pl.pallas_call(body, name=name, grid=(NE,), in_specs=[_ROW, _ROW, _VEC, _ROW] + [_ANY] * len(after),
                          out_specs=[_ROW, _VEC], out_shape=[_sds((T, D), F32), _sds((1, D), F32)],
                          compiler_params=_params(("arbitrary",)))(dh, x, g, dres, *after)


def _loss_head(x2, target, g):
    def body(x_ref, t_ref, g_ref, loss_ref, dx_ref, dg_ref):
        xv = x_ref[...]
        rstd = lax.rsqrt(jnp.mean(xv * xv, axis=-1, keepdims=True) + EPS)
        xhat = xv * rstd
        gv = g_ref[...]
        err = xhat * gv - t_ref[...]
        dy = err * (1.0 / D)
        dxh = dy * gv
        dx_ref[...] = rstd * (dxh - xhat * jnp.mean(dxh * xhat, axis=-1, keepdims=True))

        @pl.when(pl.program_id(0) == 0)
        def _():
            dg_ref[...] = jnp.zeros_like(dg_ref)
            loss_ref[...] = jnp.zeros_like(loss_ref)

        dg_ref[...] += jnp.sum(dy * xhat, axis=0, keepdims=True)
        loss_ref[...] += (0.5 / D) * jnp.sum(jnp.sum(err * err, axis=1, keepdims=True), axis=0, keepdims=True)

    return pl.pallas_call(
        body, name="loss_head", grid=(NE,), in_specs=[_ROW, _ROW, _VEC],
        out_specs=[pl.BlockSpec((1, 1), lambda i: (0, 0)), _ROW, _VEC],
        out_shape=[_sds((1, 1), F32), _sds((T, D), F32), _sds((1, D), F32)],
        compiler_params=_params(("arbitrary",)))(x2, target, g)


MW = 512
_G_ATT_BLK = 3584 // MW
_G_REC_BLK = 4608 // MW


def _merge_specs():
    y = pl.BlockSpec((TM, MW), lambda i, j: (i, j))
    ga = pl.BlockSpec((TM, MW), lambda i, j: (i, _G_ATT_BLK + j))
    gr = pl.BlockSpec((TM, MW), lambda i, j: (i, _G_REC_BLK + j))
    return y, ga, gr


def _merge_fwd(y_att, y_rec, z):
    y, ga, gr = _merge_specs()

    def body(ya_ref, yr_ref, ga_ref, gr_ref, m_ref):
        m = _sigmoid(ga_ref[...]) * ya_ref[...] + _sigmoid(gr_ref[...]) * yr_ref[...]
        m_ref[...] = m.astype(BF16)

    return pl.pallas_call(body, name="merge_fwd", grid=(NI, D // MW), in_specs=[y, y, ga, gr], out_specs=y,
                          out_shape=_sds((T, D), BF16),
                          compiler_params=_params(("parallel", "parallel")))(y_att, y_rec, z, z)


def _merge_bwd(dm, y_att, y_rec, z):
    y, ga, gr = _merge_specs()

    def body(dm_ref, ya_ref, yr_ref, ga_ref, gr_ref, dya_ref, dyr_ref, dga_ref, dgr_ref):
        dmv = dm_ref[...]
        sa = _sigmoid(ga_ref[...])
        sr = _sigmoid(gr_ref[...])
        dya_ref[...] = (dmv * sa).astype(BF16)
        dyr_ref[...] = (dmv * sr).astype(BF16)
        dga_ref[...] = (dmv * ya_ref[...] * sa * (1.0 - sa)).astype(BF16)
        dgr_ref[...] = (dmv * yr_ref[...] * sr * (1.0 - sr)).astype(BF16)

    return pl.pallas_call(body, name="merge_bwd", grid=(NI, D // MW), in_specs=[y, y, y, ga, gr],
                          out_specs=[y, y, y, y], out_shape=[_sds((T, D), BF16)] * 4,
                          compiler_params=_params(("parallel", "parallel")))(dm, y_att, y_rec, z, z)


HP = 2 * HEAD_DIM
N_HP = N_HEADS // 2
ATT_UNROLL_FWD = 8
ATT_UNROLL_BWD = 4


def _window_maps():
    diag = np.zeros((GRID_W * GRID_W, 128), np.float32)
    for qc in range(GRID_W):
        w0 = min(max(qc - WIN_W // 2, 0), GRID_W - WIN_W)
        for kc in range(w0, w0 + WIN_W):
            diag[qc * GRID_W + kc, kc - qc + WIN_W - 1] = 1.0
    return diag, diag.sum(axis=1)[None, :]


def _split3(x):
    a = x.astype(BF16)
    r = x - a.astype(F32)
    b = r.astype(BF16)
    c = (r - b.astype(F32)).astype(BF16)
    return a, b, c


N_DROW = 2 * WIN_H - 1
N_DPAIR = N_DROW - 1


def _bias_pairs(rpb):
    diag, valid = _window_maps()
    r2 = jnp.pad(rpb.reshape(N_HEADS * N_DROW, 2 * WIN_W - 1),
                 ((0, 128 - N_HEADS * N_DROW), (0, 128 - (2 * WIN_W - 1))))

    def body(r_ref, d_ref, v_ref, o_ref):
        dv = d_ref[...]
        t = sum(_dot(part, dv, NN) for part in _split3(r_ref[...]))
        o_ref[...] = jnp.where(v_ref[...] > 0.0, t, -1e30)

    t = pl.pallas_call(body, name="rpb_expand", out_shape=_sds((128, GRID_W * GRID_W), F32),
                       compiler_params=_params())(r2, jnp.asarray(diag.T, BF16), jnp.asarray(valid, F32))
    t = t[:N_HEADS * N_DROW].reshape(N_HEADS, N_DROW, GRID_W, GRID_W)
    return jnp.concatenate([t[:, :N_DPAIR], t[:, 1:]], axis=-1)


def _row_bias(tb_ref, hh, d0):
    return jnp.concatenate([tb_ref[hh, d0 + 2 * ii] for ii in range(WIN_H // 2)], axis=1)


def _row_window(r):
    rs = jnp.clip(r - WIN_H // 2, 0, N_ROWS - WIN_H)
    return pl.multiple_of(r * GRID_W, GRID_W), pl.multiple_of(rs * GRID_W, GRID_W), rs - r + (WIN_H - 1)


def _split_heads(src_ref, dst_ref, scale=None):
    for hh in range(2):
        v = src_ref[:, hh * HEAD_DIM:(hh + 1) * HEAD_DIM]
        dst_ref[hh] = (v if scale is None else v * scale).astype(BF16)


def _attn_items(qb_ref, kb_ref, vb_ref, tb_ref, first_row, n_rows):
    wins = [_row_window(first_row + u) for u in range(n_rows)]
    items = [(u, hh) for u in range(n_rows) for hh in range(2)]
    q = [qb_ref[hh, pl.ds(wins[u][0], GRID_W), :] for u, hh in items]
    k = [kb_ref[hh, pl.ds(wins[u][1], KEYS), :] for u, hh in items]
    v = [vb_ref[hh, pl.ds(wins[u][1], KEYS), :] for u, hh in items]
    s = [_dot(qi, ki, NT) + _row_bias(tb_ref, hh, wins[u][2]) for qi, ki, (u, hh) in zip(q, k, items)]
    m = [jnp.max(si, axis=-1, keepdims=True) for si in s]
    e = [jnp.exp(si - mi) for si, mi in zip(s, m)]
    inv = [1.0 / jnp.sum(ei, axis=-1, keepdims=True) for ei in e]
    p = [ei * li for ei, li in zip(e, inv)]
    return wins, items, q, k, v, p


def _attn_in_specs():
    q = pl.BlockSpec((T, HP), lambda p: (0, p))
    k = pl.BlockSpec((T, HP), lambda p: (0, N_HP + p))
    v = pl.BlockSpec((T, HP), lambda p: (0, 2 * N_HP + p))
    tb = pl.BlockSpec((2, N_DPAIR, GRID_W, HP), lambda p: (p, 0, 0, 0))
    return q, k, v, tb


_HEAD_SCRATCH = pltpu.VMEM((2, T, HEAD_DIM), BF16)


def _attn_fwd(z, tb):
    def body(q_ref, k_ref, v_ref, tb_ref, o_ref, qb_ref, kb_ref, vb_ref):
        _split_heads(q_ref, qb_ref, SCALE)
        _split_heads(k_ref, kb_ref)
        _split_heads(v_ref, vb_ref)

        def rows(it, carry):
            wins, items, _, _, v, p = _attn_items(qb_ref, kb_ref, vb_ref, tb_ref, it * ATT_UNROLL_FWD, ATT_UNROLL_FWD)
            o = [_dot(pi.astype(BF16), vi, NN) for pi, vi in zip(p, v)]
            for u, (q0, _, _) in enumerate(wins):
                o_ref[pl.ds(q0, GRID_W), :] = jnp.concatenate(o[2 * u:2 * u + 2], axis=1).astype(BF16)
            return carry

        lax.fori_loop(0, N_ROWS // ATT_UNROLL_FWD, rows, 0)

    blk = pl.BlockSpec((T, HP), lambda p: (0, p))
    return pl.pallas_call(
        body, name="attn_fwd", grid=(N_HP,), in_specs=list(_attn_in_specs()), out_specs=blk,
        out_shape=_sds((T, D_ATT), BF16), scratch_shapes=[_HEAD_SCRATCH] * 3,
        compiler_params=_params(("parallel",)))(z, z, z, tb)


def _attn_bwd(z, tb, d_att, after=()):
    def body(q_ref, k_ref, v_ref, tb_ref, do_ref, *rest):
        dq_ref, dk_ref, dv_ref, ds_ref, qb_ref, kb_ref, vb_ref, dob_ref, dka_ref, dva_ref = rest[len(after):]
        _split_heads(q_ref, qb_ref, SCALE)
        _split_heads(k_ref, kb_ref)
        _split_heads(v_ref, vb_ref)
        _split_heads(do_ref, dob_ref)
        dka_ref[...] = jnp.zeros_like(dka_ref)
        dva_ref[...] = jnp.zeros_like(dva_ref)
        ds_ref[...] = jnp.zeros_like(ds_ref)

        def rows(it, carry):
            wins, items, q, k, v, p = _attn_items(qb_ref, kb_ref, vb_ref, tb_ref, it * ATT_UNROLL_BWD, ATT_UNROLL_BWD)
            do = [dob_ref[hh, pl.ds(wins[u][0], GRID_W), :] for u, hh in items]
            dv = [_dot(pi.astype(BF16), di, TN) for pi, di in zip(p, do)]
            dp = [_dot(di, vi, NT) for di, vi in zip(do, v)]
            ds = [pi * (dpi - jnp.sum(dpi * pi, axis=-1, keepdims=True)) for pi, dpi in zip(p, dp)]
            dsb = [d.astype(BF16) for d in ds]
            dq = [_dot(d, ki, NN) * SCALE for d, ki in zip(dsb, k)]
            dk = [_dot(d, qi, TN) for d, qi in zip(dsb, q)]
            for d, (u, hh) in zip(ds, items):
                for ii in range(WIN_H // 2):
                    ds_ref[hh, wins[u][2] + 2 * ii] += d[:, ii * HP:(ii + 1) * HP]
            for u, (q0, k0, _) in enumerate(wins):
                dq_ref[pl.ds(q0, GRID_W), :] = jnp.concatenate(dq[2 * u:2 * u + 2], axis=1).astype(BF16)
                dka_ref[pl.ds(k0, KEYS), :] += jnp.concatenate(dk[2 * u:2 * u + 2], axis=1)
                dva_ref[pl.ds(k0, KEYS), :] += jnp.concatenate(dv[2 * u:2 * u + 2], axis=1)
            return carry

        lax.fori_loop(0, N_ROWS // ATT_UNROLL_BWD, rows, 0)
        dk_ref[...] = dka_ref[...].astype(BF16)
        dv_ref[...] = dva_ref[...].astype(BF16)

    blk = pl.BlockSpec((T, HP), lambda p: (0, p))
    q, k, v, tbs = _attn_in_specs()
    return pl.pallas_call(
        body, name="attn_bwd", grid=(N_HP,), in_specs=[q, k, v, tbs, blk] + [_ANY] * len(after),
        out_specs=[blk, blk, blk, tbs],
        out_shape=[_sds((T, D_ATT), BF16)] * 3 + [_sds((N_HEADS, N_DPAIR, GRID_W, HP), F32)],
        scratch_shapes=[_HEAD_SCRATCH] * 4 + [pltpu.VMEM((T, HP), F32), pltpu.VMEM((T, HP), F32)],
        compiler_params=_params(("parallel",)))(z, z, z, tb, d_att, *after)


def _rpb_grad(ds_acc):
    a = ds_acc.reshape(N_HEADS, N_DPAIR, GRID_W, 2, GRID_W).transpose(0, 1, 3, 2, 4)
    a = jnp.pad(a.reshape(N_HEADS * N_DPAIR * 2, GRID_W * GRID_W), ((0, 256 - N_HEADS * N_DPAIR * 2), (0, 0)))
    sel = np.zeros((128, 256), np.float32)
    for h in range(N_HEADS):
        for pair in range(N_DPAIR):
            for half in range(2):
                sel[h * N_DROW + pair + half, (h * N_DPAIR + pair) * 2 + half] = 1.0
    diag, _ = _window_maps()

    def body(a_ref, sel_ref, diag_ref, o_ref):
        selv = sel_ref[...]
        g = sum(_dot(selv, part, NN) for part in _split3(a_ref[...]))
        diagv = diag_ref[...]
        o_ref[...] = sum(_dot(part, diagv, NN) for part in _split3(g))

    out = pl.pallas_call(body, name="rpb_grad", out_shape=_sds((128, 128), F32),
                         compiler_params=_params())(a, jnp.asarray(sel, BF16), jnp.asarray(diag, BF16))
    return out[:N_HEADS * N_DROW, :2 * WIN_W - 1].reshape(N_HEADS, N_DROW, 2 * WIN_W - 1)


N_CB = D // REC_CB
N_CHUNK = T // REC_CHUNK
N_TILE = T // 8
_U_BLK = 1536 // REC_CB
_Y_BLK = 2560 // REC_CB


def _block_diag(w):
    per = REC_CB // 64
    wt = w.reshape(2, N_CB, per, 64, 64)
    eye = jnp.eye(per, dtype=w.dtype)
    full = wt[:, :, :, :, None, :] * eye[None, None, :, None, :, None]
    return full.reshape(2, N_CB, REC_CB, REC_CB).astype(BF16)


def _block_diag_grad(g):
    per = REC_CB // 64
    g6 = g.reshape(2, N_CB, per, 64, per, 64)
    return jnp.stack([g6[:, :, p, :, p, :] for p in range(per)], axis=2).reshape(2, 16, 64, 64)


def _gelu(x):
    c = 0.7978845608028654
    return 0.5 * x * (1.0 + jnp.tanh(c * (x + 0.044715 * x * x * x)))


def _gelu_grad(x):
    c = 0.7978845608028654
    th = jnp.tanh(c * (x + 0.044715 * x * x * x))
    return 0.5 * (1.0 + th) + 0.5 * x * (1.0 - th * th) * c * (1.0 + 3.0 * 0.044715 * x * x)


def _softplus_neg(lam):
    x = -lam
    e = jnp.exp(-jnp.abs(x))
    w = 1.0 + e
    l1p = jnp.where(w == 1.0, e, jnp.log(w) * e / (w - 1.0))
    return jnp.maximum(x, 0.0) + l1p


def _one_minus_exp(x):
    poly = x * (1.0 + x * (1 / 2 + x * (1 / 6 + x * (1 / 24 + x * (1 / 120 + x * (1 / 720))))))
    return jnp.where(x > -0.125, -poly, 1.0 - jnp.exp(x))


def _conv_taps(pad_ref, t0, w, sign):
    out = None
    for j in range(4):
        term = w[j:j + 1, :] * pad_ref[pl.ds(PAD + t0 + sign * (j - 2), REC_CHUNK), :]
        out = term if out is None else out + term
    return out


def _gates(u, wa, wi, ba, bi, sp):
    ub = u.astype(BF16)
    r = _sigmoid(_dot(ub, wa, NN) + ba)
    i = _sigmoid(_dot(ub, wi, NN) + bi)
    log_a = -LRU_C * r * sp
    a = jnp.exp(log_a)
    mult = jnp.sqrt(jnp.maximum(_one_minus_exp(2.0 * log_a), 0.0))
    return r, i, a, mult


def _tile_scan(a, b, sub, reverse):
    for s in (1, 2, 4):
        if reverse:
            a_s, b_s, m = pltpu.roll(a, 8 - s, 0), pltpu.roll(b, 8 - s, 0), sub < 8 - s
        else:
            a_s, b_s, m = pltpu.roll(a, s, 0), pltpu.roll(b, s, 0), sub >= s
        b = jnp.where(m, a * b_s + b, b)
        a = jnp.where(m, a * a_s, a)
    return a, b


def _last_row(x, sub, row):
    return jnp.broadcast_to(jnp.sum(jnp.where(sub == row, x, 0.0), axis=0, keepdims=True), x.shape)


def _rec_prologue(up_ref, cw_ref, cb_ref, wa_ref, wi_ref, ba_ref, bi_ref, lam_ref,
                  upad_ref, u_ref, a_refs, h_refs):
    cb = up_ref.shape[1]
    zeros = jnp.zeros((PAD, cb), F32)
    upad_ref[pl.ds(0, PAD), :] = zeros
    upad_ref[pl.ds(PAD + T, PAD), :] = zeros
    upad_ref[pl.ds(PAD, T), :] = up_ref[...]
    cw = cw_ref[...]
    sp = _softplus_neg(lam_ref[...])
    for c in range(N_CHUNK):
        t0 = c * REC_CHUNK
        u = cb_ref[...] + _conv_taps(upad_ref, t0, cw, 1)
        u_ref[pl.ds(t0, REC_CHUNK), :] = u
        for d in range(2):
            _, i, a, mult = _gates(u, wa_ref[d], wi_ref[d], ba_ref[d:d + 1, :], bi_ref[d:d + 1, :], sp[d:d + 1, :])
            a_refs[d][pl.ds(t0, REC_CHUNK), :] = a
            h_refs[d][pl.ds(t0, REC_CHUNK), :] = mult * (i * u)

    sub = lax.broadcasted_iota(jnp.int32, (8, cb), 0)

    def tile(k, carry):
        cf, cr = carry
        tf = pl.multiple_of(k * 8, 8)
        tr = pl.multiple_of((N_TILE - 1 - k) * 8, 8)
        af, bf = _tile_scan(a_refs[0][pl.ds(tf, 8), :], h_refs[0][pl.ds(tf, 8), :], sub, False)
        hf = af * cf + bf
        h_refs[0][pl.ds(tf, 8), :] = hf
        ar, br = _tile_scan(a_refs[1][pl.ds(tr, 8), :], h_refs[1][pl.ds(tr, 8), :], sub, True)
        hr = ar * cr + br
        h_refs[1][pl.ds(tr, 8), :] = hr
        return _last_row(hf, sub, 7), _last_row(hr, sub, 0)

    z8 = jnp.zeros((8, cb), F32)
    lax.fori_loop(0, N_TILE, tile, (z8, z8))
    return sp


def _rec_specs():
    up = pl.BlockSpec((T, REC_CB), lambda c: (0, _U_BLK + c))
    yb = pl.BlockSpec((T, REC_CB), lambda c: (0, _Y_BLK + c))
    cw = pl.BlockSpec((4, REC_CB), lambda c: (0, c))
    cbias = pl.BlockSpec((1, REC_CB), lambda c: (0, c))
    wbd = pl.BlockSpec((2, None, REC_CB, REC_CB), lambda c: (0, c, 0, 0))
    vec2 = pl.BlockSpec((2, REC_CB), lambda c: (0, c))
    col = pl.BlockSpec((T, REC_CB), lambda c: (0, c))
    return up, yb, cw, cbias, wbd, vec2, col


def _rec_fwd(z, conv_w, conv_b, wa, wi, ba, bi, lam):
    up, yb, cw, cbias, wbd, vec2, col = _rec_specs()

    def body(up_ref, yb_ref, cw_ref, cb_ref, wa_ref, wi_ref, ba_ref, bi_ref, lam_ref, g_ref,
             u_ref, af_ref, ar_ref, hf_ref, hr_ref, upad_ref):
        _rec_prologue(up_ref, cw_ref, cb_ref, wa_ref, wi_ref, ba_ref, bi_ref, lam_ref,
                      upad_ref, u_ref, (af_ref, ar_ref), (hf_ref, hr_ref))

        def chunk(c, carry):
            t0 = pl.multiple_of(c * REC_CHUNK, REC_CHUNK)
            rows = pl.ds(t0, REC_CHUNK)
            g_ref[rows, :] = ((hf_ref[rows, :] + hr_ref[rows, :]) * _gelu(yb_ref[rows, :])).astype(BF16)
            return carry

        lax.fori_loop(0, N_CHUNK, chunk, 0)

    res = pl.pallas_call(
        body, name="rec_fwd", grid=(N_CB,),
        in_specs=[up, yb, cw, cbias, wbd, wbd, vec2, vec2, vec2], out_specs=[col] * 6,
        out_shape=[_sds((T, D), BF16)] + [_sds((T, D), F32)] * 5,
        scratch_shapes=[pltpu.VMEM((T + 2 * PAD, REC_CB), F32)],
        compiler_params=_params(("parallel",)))(z, z, conv_w, conv_b, wa, wi, ba, bi, lam)
    return res[0], tuple(res[1:])


def _rec_bwd(z, dg, saved, conv_w, conv_b, wa, wi, ba, bi, lam, after=()):
    up, yb, cw, cbias, wbd, vec2, col = _rec_specs()

    def body(up_ref, yb_ref, dg_ref, u_ref, af_ref, ar_ref, hf_ref, hr_ref,
             cw_ref, cb_ref, wa_ref, wi_ref, ba_ref, bi_ref, lam_ref, *rest):
        (dup_ref, dyb_ref, dcw_ref, dcb_ref, dwa_ref, dwi_ref, dba_ref, dbi_ref, dlam_ref,
         upad_ref, dh_ref, gf_ref, gr_ref, daf_ref, dar_ref, dupad_ref) = rest[len(after):]
        g_refs, da_refs = (gf_ref, gr_ref), (daf_ref, dar_ref)
        cb = up_ref.shape[1]
        zeros = jnp.zeros((PAD, cb), F32)
        upad_ref[pl.ds(0, PAD), :] = zeros
        upad_ref[pl.ds(PAD + T, PAD), :] = zeros
        upad_ref[pl.ds(PAD, T), :] = up_ref[...]
        sp = _softplus_neg(lam_ref[...])

        def gate_chunk(c, carry):
            t0 = pl.multiple_of(c * REC_CHUNK, REC_CHUNK)
            rows = pl.ds(t0, REC_CHUNK)
            y = yb_ref[rows, :]
            dgv = dg_ref[rows, :].astype(F32)
            dh_ref[rows, :] = dgv * _gelu(y)
            dyb_ref[rows, :] = (dgv * (hf_ref[rows, :] + hr_ref[rows, :]) * _gelu_grad(y)).astype(BF16)
            return carry

        lax.fori_loop(0, N_CHUNK, gate_chunk, 0)

        sub = lax.broadcasted_iota(jnp.int32, (8, cb), 0)

        def tile(k, carry):
            cf, cr = carry
            kf = N_TILE - 1 - k
            tf = pl.multiple_of(kf * 8, 8)
            tnext = pl.multiple_of(jnp.minimum(kf + 1, N_TILE - 1) * 8, 8)
            tprev = pl.multiple_of(jnp.maximum(kf - 1, 0) * 8, 8)
            a_t = af_ref[pl.ds(tf, 8), :]
            a_n = jnp.where(kf < N_TILE - 1, af_ref[pl.ds(tnext, 8), :], 0.0)
            a_sh = jnp.where(sub == 7, pltpu.roll(a_n, 7, 0), pltpu.roll(a_t, 7, 0))
            ca, cbb = _tile_scan(a_sh, dh_ref[pl.ds(tf, 8), :], sub, True)
            gf = ca * cf + cbb
            h_t = hf_ref[pl.ds(tf, 8), :]
            h_p = jnp.where(kf > 0, hf_ref[pl.ds(tprev, 8), :], 0.0)
            h_sh = jnp.where(sub == 0, pltpu.roll(h_p, 1, 0), pltpu.roll(h_t, 1, 0))
            gf_ref[pl.ds(tf, 8), :] = gf
            daf_ref[pl.ds(tf, 8), :] = gf * h_sh
            tr = pl.multiple_of(k * 8, 8)
            rnext = pl.multiple_of(jnp.minimum(k + 1, N_TILE - 1) * 8, 8)
            rprev = pl.multiple_of(jnp.maximum(k - 1, 0) * 8, 8)
            b_t = ar_ref[pl.ds(tr, 8), :]
            b_p = jnp.where(k > 0, ar_ref[pl.ds(rprev, 8), :], 0.0)
            b_sh = jnp.where(sub == 0, pltpu.roll(b_p, 1, 0), pltpu.roll(b_t, 1, 0))
            ra, rb = _tile_scan(b_sh, dh_ref[pl.ds(tr, 8), :], sub, False)
            gr = ra * cr + rb
            hr_t = hr_ref[pl.ds(tr, 8), :]
            hr_n = jnp.where(k < N_TILE - 1, hr_ref[pl.ds(rnext, 8), :], 0.0)
            hr_sh = jnp.where(sub == 7, pltpu.roll(hr_n, 7, 0), pltpu.roll(hr_t, 7, 0))
            gr_ref[pl.ds(tr, 8), :] = gr
            dar_ref[pl.ds(tr, 8), :] = gr * hr_sh
            return _last_row(gf, sub, 0), _last_row(gr, sub, 7)

        z8 = jnp.zeros((8, cb), F32)
        lax.fori_loop(0, N_TILE, tile, (z8, z8))

        dupad_ref[pl.ds(0, PAD), :] = zeros
        dupad_ref[pl.ds(PAD + T, PAD), :] = zeros
        dwa_ref[...] = jnp.zeros_like(dwa_ref)
        dwi_ref[...] = jnp.zeros_like(dwi_ref)
        dba_ref[...] = jnp.zeros_like(dba_ref)
        dbi_ref[...] = jnp.zeros_like(dbi_ref)
        dlam_ref[...] = jnp.zeros_like(dlam_ref)

        def grad_chunk(c, carry):
            t0 = pl.multiple_of(c * REC_CHUNK, REC_CHUNK)
            rows = pl.ds(t0, REC_CHUNK)
            u = u_ref[rows, :]
            ub = u.astype(BF16)
            du = jnp.zeros((REC_CHUNK, cb), F32)
            for d in range(2):
                r, i, a, mult = _gates(u, wa_ref[d], wi_ref[d], ba_ref[d:d + 1, :], bi_ref[d:d + 1, :], sp[d:d + 1, :])
                dbx = g_refs[d][rows, :]
                dmult = dbx * (i * u)
                diu = dbx * mult
                a2 = a * a
                dlog = da_refs[d][rows, :] * a - dmult * jnp.where(mult > 0.0, a2 / mult, 0.0)
                dpa = (dlog * (-LRU_C) * sp[d:d + 1, :]) * r * (1.0 - r)
                dpi = (diu * u) * i * (1.0 - i)
                dpab, dpib = dpa.astype(BF16), dpi.astype(BF16)
                du = du + diu * i + _dot(dpab, wa_ref[d], NT) + _dot(dpib, wi_ref[d], NT)
                dwa_ref[d] += _dot(ub, dpab, TN)
                dwi_ref[d] += _dot(ub, dpib, TN)
                dba_ref[d:d + 1, :] += jnp.sum(dpa, axis=0, keepdims=True)
                dbi_ref[d:d + 1, :] += jnp.sum(dpi, axis=0, keepdims=True)
                dlam_ref[d:d + 1, :] += jnp.sum(dlog * r, axis=0, keepdims=True)
            dupad_ref[pl.ds(PAD + t0, REC_CHUNK), :] = du
            return carry

        lax.fori_loop(0, N_CHUNK, grad_chunk, 0)
        dlam_ref[...] = dlam_ref[...] * (LRU_C * _sigmoid(-lam_ref[...]))

        cw = cw_ref[...]
        dcb = jnp.zeros((1, cb), F32)
        dcw = [jnp.zeros((1, cb), F32) for _ in range(4)]
        for c in range(N_CHUNK):
            t0 = c * REC_CHUNK
            du = dupad_ref[pl.ds(PAD + t0, REC_CHUNK), :]
            dcb = dcb + jnp.sum(du, axis=0, keepdims=True)
            for j in range(4):
                dcw[j] = dcw[j] + jnp.sum(du * upad_ref[pl.ds(PAD + t0 + j - 2, REC_CHUNK), :], axis=0, keepdims=True)
            dup_ref[pl.ds(t0, REC_CHUNK), :] = _conv_taps(dupad_ref, t0, cw, -1).astype(BF16)
        dcb_ref[...] = dcb
        dcw_ref[...] = jnp.concatenate(dcw, axis=0)

    full = pltpu.VMEM((T, REC_CB), F32)
    padded = pltpu.VMEM((T + 2 * PAD, REC_CB), F32)
    return pl.pallas_call(
        body, name="rec_bwd", grid=(N_CB,),
        in_specs=[up, yb] + [col] * 6 + [cw, cbias, wbd, wbd, vec2, vec2, vec2] + [_ANY] * len(after),
        out_specs=[col, col, cw, cbias, wbd, wbd, vec2, vec2, vec2],
        out_shape=[_sds((T, D), BF16), _sds((T, D), BF16), _sds((4, D), F32), _sds((1, D), F32),
                   _sds((2, N_CB, REC_CB, REC_CB), F32), _sds((2, N_CB, REC_CB, REC_CB), F32),
                   _sds((2, D), F32), _sds((2, D), F32), _sds((2, D), F32)],
        scratch_shapes=[padded, full, full, full, full, full, padded],
        compiler_params=_params(("parallel",)))(z, z, dg, *saved, conv_w, conv_b, wa, wi, ba, bi, lam, *after)


class _NoReducer:
    def begin(self, tag, grads):
        return ()

    def advance(self, tag, after):
        return ()


def _local_step(x, target, p, late=None, reducer=_NoReducer()):
    x = x.reshape(T, D)
    target = target.reshape(T, D)
    tb = _bias_pairs(p["rpb"])
    wa, wi = _block_diag(p["w_rg_a"]), _block_diag(p["w_rg_i"])

    h1 = _rms_fwd("rms1_fwd", x, p["ln1_g"], after=late[0] if late else ())
    if late:
        p = {**p, **late[1](h1)}
    rec_params = (p["conv_w"], p["conv_b"], wa, wi, p["b_rg_a"], p["b_rg_i"], p["lru_lambda"])
    (z,) = _mm_nn_cols("mm_z", h1, p["w_in"], F32, bias=p["b_in"])
    att = _attn_fwd(z, tb)
    g, rec_saved = _rec_fwd(z, *rec_params)
    if late:
        p = {**p, **late[2](g)}
    (y_att,) = _mm_nn_cols("mm_y_att", att, p["w_att_o"], F32)
    (y_rec,) = _mm_nn_rows("mm_y_rec", g, p["w_rec_o"], F32, tk=D)
    mixed = _merge_fwd(y_att, y_rec, z)

    def add_res(r, ex, outs):
        outs[0][...] = ex[0][...] + r

    res_spec = pl.BlockSpec((TM, D), lambda i, j, k: (i, 0))
    (x1,) = _mm_nn_rows("mm_x1", mixed, p["w_out"], F32, tk=D, extras=[x], extra_specs=[res_spec], epilogue=add_res)
    h2 = _rms_fwd("rms2_fwd", x1, p["ln2_g"])

    def relu2(r, ex, outs):
        rp = jnp.maximum(r, 0.0)
        outs[0][...] = (rp * rp).astype(BF16)

    (s,) = _mm_nn_cols("mm_ff1", h2, p["w_ff1"], BF16, epilogue=relu2)
    (x2,) = _mm_nn_rows("mm_x2", s, p["w_ff2"], F32, tk=D, extras=[x1], extra_specs=[res_spec], epilogue=add_res)
    loss, dx2, g_lnf = _loss_head(x2, target, p["lnf_g"])

    def relu2_bwd(r, ex, outs):
        outs[0][...] = (r * 2.0 * jnp.sqrt(ex[0][...].astype(F32))).astype(BF16)

    (df,) = _mm_nt_rows("mm_df", dx2, p["w_ff2"], BF16, tn=D, extras=[s],
                        extra_specs=[pl.BlockSpec((TM, D), lambda j, i, k: (i, j))], epilogue=relu2_bwd)
    (g_w_ff2,) = _mm_tn_rows("mm_g_ff2", s, dx2, tm=D)
    (g_w_ff1,) = _mm_tn_cols("mm_g_ff1", h2, df, D)
    tok = reducer.begin("ff", dict(w_ff2=g_w_ff2, w_ff1=g_w_ff1))
    (dh2,) = _mm_nt_cols("mm_dh2", df, p["w_ff1"], F32)
    dx1, g_ln2 = _rms_bwd("rms2_bwd", dh2, x1, p["ln2_g"], dx2, after=tok)

    (dmixed,) = _mm_nt_rows("mm_dmixed", dx1, p["w_out"], F32, tn=D)
    (g_w_out,) = _mm_tn_rows("mm_g_out", mixed, dx1, tm=D)
    dy_att, dy_rec, dg_att, dg_rec = _merge_bwd(dmixed, y_att, y_rec, z)
    (d_att,) = _mm_nt_cols("mm_d_att", dy_att, p["w_att_o"], BF16)
    (g_w_att_o,) = _mm_tn_cols("mm_g_att_o", att, dy_att, D // N_CHIPS)
    (d_g,) = _mm_nt_rows("mm_d_g", dy_rec, p["w_rec_o"], BF16, tn=D)
    (g_w_rec_o,) = _mm_tn_rows("mm_g_rec_o", g, dy_rec, tm=D)
    tok = reducer.advance("ff", g_w_rec_o) + reducer.begin("proj", dict(w_out=g_w_out, w_att_o=g_w_att_o, w_rec_o=g_w_rec_o))

    dq, dk, dv, ds_acc = _attn_bwd(z, tb, d_att, after=tok)
    g_rpb = _rpb_grad(ds_acc)
    tok = reducer.advance("proj", dq)
    d_up, d_yb, g_conv_w, g_conv_b, g_wa, g_wi, g_ba, g_bi, g_lam = _rec_bwd(z, d_g, rec_saved, *rec_params, after=tok)
    dz = jnp.concatenate([dq, dk, dv, d_up, d_yb, dg_att, dg_rec], axis=1)

    g_w_in, g_b_in = _mm_tn_cols("mm_g_in", h1, dz, D_IN // N_CHIPS, colsum=True)
    tok = reducer.begin("in", dict(w_in=g_w_in))
    (dh1,) = _mm_nt_cols("mm_dh1", dz, p["w_in"], F32, after=tok)
    tok = reducer.advance("in", dh1)
    grad_x, g_ln1 = _rms_bwd("rms1_bwd", dh1, x, p["ln1_g"], dx1, after=tok)

    grads = dict(ln1_g=g_ln1, w_in=g_w_in, b_in=g_b_in, rpb=g_rpb, w_att_o=g_w_att_o, conv_w=g_conv_w,
                 conv_b=g_conv_b, w_rg_a=_block_diag_grad(g_wa), b_rg_a=g_ba, w_rg_i=_block_diag_grad(g_wi),
                 b_rg_i=g_bi, lru_lambda=g_lam, w_rec_o=g_w_rec_o, w_out=g_w_out, ln2_g=g_ln2,
                 w_ff1=g_w_ff1, w_ff2=g_w_ff2, lnf_g=g_lnf)
    return loss, grad_x.reshape(1, T, D), grads


_ANY = pl.BlockSpec(memory_space=pl.ANY)
N_PEERS = N_CHIPS - 1


def _place():
    x, y, c = lax.axis_index("x"), lax.axis_index("y"), lax.axis_index("c")
    peers = [(1 - x, y), (x, 1 - y), (1 - x, 1 - y)]
    return x, y, c, 2 * x + y, peers


def _remote(src, dst, send_sem, recv_sem, dev):
    return pltpu.make_async_remote_copy(src_ref=src, dst_ref=dst, send_sem=send_sem, recv_sem=recv_sem,
                                        device_id=dev, device_id_type=MESH)


def _prefetch_call(body, name, ids, grid, in_specs, out_specs, out_shape, args):
    spec = pltpu.PrefetchScalarGridSpec(num_scalar_prefetch=1, grid=grid, in_specs=in_specs, out_specs=out_specs)
    return pl.pallas_call(body, name=name, grid_spec=spec, out_shape=out_shape,
                          compiler_params=_params(("parallel",) * len(grid)))(ids, *args)


def _cast_bf16(name, w, chip_id, after=()):
    rows, cols = w.shape
    rb = min(rows, 256)

    def body(ids_ref, w_ref, *rest):
        rest[-1][...] = w_ref[...].astype(BF16)

    return _prefetch_call(body, name, chip_id, (rows // rb,),
                          [pl.BlockSpec((rb, cols), lambda i, ids: (i, 0))] + [_ANY] * len(after),
                          pl.BlockSpec((None, rb, cols), lambda i, ids: (ids[0], i, 0)),
                          _sds((N_CHIPS, rows, cols), BF16), (w, *after))


def _dma_sems(*counts):
    return [pltpu.SemaphoreType.DMA((k,)) for k in counts]


_HBM = pl.BlockSpec(memory_space=pltpu.HBM)
_SEM = pl.BlockSpec(memory_space=pltpu.SEMAPHORE)
_SPLIT_COPY = pltpu.CompilerParams(has_side_effects=pltpu.SideEffectType.DATAFLOW_SIDE_EFFECTING)


def _hbm(arrays):
    return [pltpu.with_memory_space_constraint(a, pltpu.HBM) for a in arrays]


def _hbm_like(arrays):
    return [pltpu.HBM(a.shape, a.dtype) for a in arrays]


def _halves(buf, c):
    half = buf.shape[1] // 2
    return pl.ds(c * half, half), pl.ds((1 - c) * half, half)


def _gather_start(name, slots):
    n = len(slots)
    nk = n * N_PEERS

    def body(*refs):
        bufs = refs[n:2 * n]
        send_sems, recv_sems, token = refs[2 * n:]
        x, y, c, chip, peers = _place()
        for t in range(n):
            mine, _ = _halves(bufs[t], c)
            for r, (px, py) in enumerate(peers):
                k = t * N_PEERS + r
                own = bufs[t].at[chip, mine]
                _remote(own, own, send_sems.at[k], recv_sems.at[k], (px, py, c)).start()
        token[...] = jnp.zeros_like(token)

    res = pl.pallas_call(
        body, name=name, in_specs=[_HBM] * n, out_specs=[_HBM] * n + [_SEM, _SEM, pl.BlockSpec(memory_space=pltpu.VMEM)],
        out_shape=_hbm_like(slots) + [pltpu.SemaphoreType.DMA((nk,)), pltpu.SemaphoreType.DMA((nk,)),
                                      _sds((8, 128), F32)],
        input_output_aliases={t: t for t in range(n)}, compiler_params=_SPLIT_COPY)(*_hbm(slots))
    return res[:n], (res[n], res[n + 1]), res[n + 2]


def _gather_wait(name, bufs, sems, after):
    n = len(bufs)

    def body(*refs):
        ins = refs[:n]
        send_sems, recv_sems = refs[n], refs[n + 1]
        x, y, c, chip, peers = _place()
        for t in range(n):
            mine, _ = _halves(ins[t], c)
            for r, (px, py) in enumerate(peers):
                k = t * N_PEERS + r
                cp = _remote(ins[t].at[chip, mine], ins[t].at[2 * px + py, mine], send_sems.at[k], recv_sems.at[k],
                             (px, py, c))
                cp.wait_send()
                cp.wait_recv()

    return pl.pallas_call(
        body, name=name, in_specs=[_HBM] * n + [_SEM, _SEM, _ANY], out_specs=[_HBM] * n, out_shape=_hbm_like(bufs),
        input_output_aliases={t: t for t in range(n)}, compiler_params=_SPLIT_COPY)(*bufs, *sems, after)


def _gather_forward(name, bufs):
    n = len(bufs)
    nk = n * N_PEERS

    def body(*refs):
        outs = refs[n:2 * n]
        send_sems, recv_sems = refs[2 * n:]
        x, y, c, chip, peers = _place()
        sibling = (x, y, 1 - c)
        sends = []
        for t in range(n):
            mine, _ = _halves(outs[t], c)
            for r, (px, py) in enumerate(peers):
                k = t * N_PEERS + r
                landed = outs[t].at[2 * px + py, mine]
                sends.append(_remote(landed, landed, send_sems.at[k], recv_sems.at[k], sibling))
                sends[-1].start()
        for t in range(n):
            _, theirs = _halves(outs[t], c)
            for r, (px, py) in enumerate(peers):
                k = t * N_PEERS + r
                landed = outs[t].at[2 * px + py, theirs]
                _remote(landed, landed, send_sems.at[k], recv_sems.at[k], sibling).wait_recv()
        for cp in sends:
            cp.wait_send()

    return pl.pallas_call(
        body, name=name, in_specs=[_ANY] * n, out_specs=[_ANY] * n, out_shape=[_sds(b.shape, b.dtype) for b in bufs],
        input_output_aliases={t: t for t in range(n)}, scratch_shapes=_dma_sems(nk, nk))(*bufs)


def _pair_copies(n, srcs, lands, send_sems, recv_sems):
    x, y, c, _, _ = _place()
    sibling = (x, y, 1 - c)
    copies = []
    for t in range(n):
        half = srcs[t].shape[1] // 2
        for j in range(N_CHIPS):
            k = t * N_CHIPS + j
            copies.append(_remote(srcs[t].at[j, pl.ds((1 - c) * half, half)], lands[t].at[j],
                                  send_sems.at[k], recv_sems.at[k], sibling))
    for t in range(n, len(srcs)):
        k = n * N_CHIPS + t - n
        copies.append(_remote(srcs[t], lands[t], send_sems.at[k], recv_sems.at[k], sibling))
    return copies


def _pair_start(name, grads, wholes=()):
    n = len(grads)
    srcs = list(grads) + list(wholes)
    m = len(srcs)
    lands = [pltpu.HBM((N_CHIPS, g.shape[1] // 2, g.shape[2]), F32) for g in grads] + _hbm_like(wholes)
    ns = n * N_CHIPS + len(wholes)

    def body(*refs):
        src_refs, land_refs = refs[m:2 * m], refs[2 * m:3 * m]
        send_sems, recv_sems, token = refs[3 * m:]
        for cp in _pair_copies(n, src_refs, land_refs, send_sems, recv_sems):
            cp.start()
        token[...] = jnp.zeros_like(token)

    res = pl.pallas_call(
        body, name=name, in_specs=[_HBM] * m,
        out_specs=[_HBM] * (2 * m) + [_SEM, _SEM, pl.BlockSpec(memory_space=pltpu.VMEM)],
        out_shape=_hbm_like(srcs) + lands + [pltpu.SemaphoreType.DMA((ns,)), pltpu.SemaphoreType.DMA((ns,)),
                                             _sds((8, 128), F32)],
        input_output_aliases={t: t for t in range(m)}, compiler_params=_SPLIT_COPY)(*_hbm(srcs))
    return (res[:m], res[m:2 * m], (res[2 * m], res[2 * m + 1])), res[2 * m + 2]


def _pair_wait(name, flight, n, after):
    srcs, lands, sems = flight
    m = len(srcs)

    def body(*refs):
        for cp in _pair_copies(n, refs[:m], refs[m:2 * m], refs[2 * m], refs[2 * m + 1]):
            cp.wait_send()
            cp.wait_recv()

    res = pl.pallas_call(
        body, name=name, in_specs=[_HBM] * (2 * m) + [_SEM, _SEM, _ANY], out_specs=[_HBM] * (2 * m),
        out_shape=_hbm_like(srcs) + _hbm_like(lands), input_output_aliases={t: t for t in range(2 * m)},
        compiler_params=_SPLIT_COPY)(*srcs, *lands, *sems, after)
    return res[:m], res[m:]


def _chip_copies(srcs, lands, small_src, small_land, send_sems, recv_sems):
    x, y, c, chip, peers = _place()
    n = len(srcs)
    copies = []
    for r, (px, py) in enumerate(peers):
        for t in range(n):
            k = t * N_PEERS + r
            copies.append(_remote(srcs[t].at[2 * px + py], lands[t].at[r], send_sems.at[k], recv_sems.at[k], (px, py, c)))
        if small_src is not None:
            k = n * N_PEERS + r
            half_s = small_src.shape[0] // 2
            copies.append(_remote(small_src.at[pl.ds(c * half_s, half_s)], small_land.at[r],
                                  send_sems.at[k], recv_sems.at[k], (px, py, c)))
    return copies


def _chip_start(name, sums_bf16, small=None):
    n = len(sums_bf16)
    srcs = list(sums_bf16) + ([small] if small is not None else [])
    m = len(srcs)
    lands = [pltpu.HBM((N_PEERS,) + s.shape[1:], BF16) for s in sums_bf16]
    if small is not None:
        lands.append(pltpu.HBM((N_PEERS, small.shape[0] // 2, 128), F32))
    nk = m * N_PEERS

    def body(*refs):
        src_refs, land_refs = refs[m:2 * m], refs[2 * m:3 * m]
        send_sems, recv_sems, token = refs[3 * m:]
        small_src, small_land = (src_refs[n], land_refs[n]) if small is not None else (None, None)
        for cp in _chip_copies(src_refs[:n], land_refs[:n], small_src, small_land, send_sems, recv_sems):
            cp.start()
        token[...] = jnp.zeros_like(token)

    res = pl.pallas_call(
        body, name=name, in_specs=[_HBM] * m,
        out_specs=[_HBM] * (2 * m) + [_SEM, _SEM, pl.BlockSpec(memory_space=pltpu.VMEM)],
        out_shape=_hbm_like(srcs) + lands + [pltpu.SemaphoreType.DMA((nk,)), pltpu.SemaphoreType.DMA((nk,)),
                                             _sds((8, 128), F32)],
        input_output_aliases={t: t for t in range(m)}, compiler_params=_SPLIT_COPY)(*_hbm(srcs))
    return (res[:m], res[m:2 * m], (res[2 * m], res[2 * m + 1])), res[2 * m + 2]


def _chip_wait(name, flight, with_small, after):
    srcs, lands, sems = flight
    m = len(srcs)
    n = m - 1 if with_small else m

    def body(*refs):
        src_refs, land_refs = refs[:m], refs[m:2 * m]
        send_sems, recv_sems = refs[2 * m], refs[2 * m + 1]
        small_src, small_land = (src_refs[n], land_refs[n]) if with_small else (None, None)
        for cp in _chip_copies(src_refs[:n], land_refs[:n], small_src, small_land, send_sems, recv_sems):
            cp.wait_send()
            cp.wait_recv()

    res = pl.pallas_call(
        body, name=name, in_specs=[_HBM] * (2 * m) + [_SEM, _SEM, _ANY], out_specs=[_HBM] * (2 * m),
        out_shape=_hbm_like(srcs) + _hbm_like(lands), input_output_aliases={t: t for t in range(2 * m)},
        compiler_params=_SPLIT_COPY)(*srcs, *lands, *sems, after)
    return res[:m], res[m:]


def _half_swap(name, bufs):
    n = len(bufs)

    def body(*refs):
        outs = refs[n:2 * n]
        send_sem, recv_sem = refs[2 * n:]
        x, y, c, _, _ = _place()
        sibling = (x, y, 1 - c)
        copies = []
        for t in range(n):
            h = outs[t].shape[0] // 2
            mine = outs[t].at[pl.ds(c * h, h)]
            copies.append(_remote(mine, mine, send_sem.at[t], recv_sem.at[t], sibling))
            copies[-1].start()
        for t in range(n):
            h = outs[t].shape[0] // 2
            theirs = outs[t].at[pl.ds((1 - c) * h, h)]
            _remote(theirs, theirs, send_sem.at[t], recv_sem.at[t], sibling).wait_recv()
        for cp in copies:
            cp.wait_send()

    return pl.pallas_call(
        body, name=name, in_specs=[_ANY] * n, out_specs=[_ANY] * n,
        out_shape=[_sds(b.shape, b.dtype) for b in bufs], input_output_aliases={t: t for t in range(n)},
        scratch_shapes=_dma_sems(n, n))(*bufs)


def _pair_sum(name, grad, got, ids):
    _, rows, cols = got.shape
    rb = min(rows, 256)
    nb = rows // rb
    blk = pl.BlockSpec((None, rb, cols), lambda j, i, ids: (j, i, 0))
    mine = pl.BlockSpec((None, rb, cols), lambda j, i, ids: (j, ids[1] * nb + i, 0))

    def body(ids_ref, a_ref, b_ref, s_ref, sb_ref):
        s = a_ref[...] + b_ref[...]
        s_ref[...] = s
        sb_ref[...] = s.astype(BF16)

    return _prefetch_call(body, name, ids, (N_CHIPS, nb), [mine, blk], [blk, blk],
                          [_sds(got.shape, F32), _sds(got.shape, BF16)], (grad, got))


def _chip_sum(name, sums, got, ids):
    _, rows, cols = sums.shape
    rb = min(rows, 256)
    nb = rows // rb
    own = pl.BlockSpec((None, rb, cols), lambda i, ids: (ids[0], i, 0))
    blk3 = pl.BlockSpec((N_PEERS, rb, cols), lambda i, ids: (0, i, 0))
    out = pl.BlockSpec((rb, cols), lambda i, ids: (ids[1] * nb + i, 0))

    def body(ids_ref, a_ref, b_ref, o_ref):
        o_ref[...] = ((a_ref[...] + b_ref[0].astype(F32)) + b_ref[1].astype(F32)) + b_ref[2].astype(F32)

    return _prefetch_call(body, name, ids, (nb,), [own, blk3], out, _sds((2 * rows, cols), F32), (sums, got))


SMALL_RB = 280


def _small_pair_sum(own, got):
    blk = pl.BlockSpec((SMALL_RB, 128), lambda i: (i, 0))

    def body(a_ref, b_ref, o_ref):
        o_ref[...] = a_ref[...] + b_ref[...]

    return pl.pallas_call(body, name="small_pair_sum", grid=(own.shape[0] // SMALL_RB,), in_specs=[blk, blk],
                          out_specs=blk, out_shape=_sds(own.shape, F32),
                          compiler_params=_params(("parallel",)))(own, got)


def _small_chip_sum(pair, got, ids):
    nb = pair.shape[0] // 2 // SMALL_RB
    half = pl.BlockSpec((SMALL_RB, 128), lambda i, ids: (ids[1] * nb + i, 0))
    blk3 = pl.BlockSpec((N_PEERS, SMALL_RB, 128), lambda i, ids: (0, i, 0))

    def body(ids_ref, a_ref, b_ref, o_ref):
        o_ref[...] = (a_ref[...] + b_ref[1]) + (b_ref[0] + b_ref[2])

    return _prefetch_call(body, "small_chip_sum", ids, (nb,), [half, blk3], half, _sds(pair.shape, F32), (pair, got))


def _adamw_math(w, g, m, v):
    m = ADAM_B1 * m + (1.0 - ADAM_B1) * g
    v = ADAM_B2 * v + (1.0 - ADAM_B2) * (g * g)
    m_hat = m / (1.0 - ADAM_B1 ** ADAM_STEP)
    v_hat = v / (1.0 - ADAM_B2 ** ADAM_STEP)
    delta = -ADAM_LR * (m_hat / (jnp.sqrt(v_hat) + ADAM_EPS) + ADAM_WD * w)
    return delta, m, v


def _adamw(name, w, g, m, v, rb=None):
    rows, cols = w.shape
    rb = rows if rb is None else rb
    blk = pl.BlockSpec((rb, cols), lambda i: (i, 0))

    def body(w_ref, g_ref, m_ref, v_ref, d_ref, nm_ref, nv_ref):
        d, nm, nv = _adamw_math(w_ref[...], g_ref[...], m_ref[...], v_ref[...])
        d_ref[...] = d
        nm_ref[...] = nm
        nv_ref[...] = nv

    return pl.pallas_call(body, name=name, grid=(rows // rb,), in_specs=[blk] * 4, out_specs=[blk] * 3,
                          out_shape=[_sds(w.shape, F32)] * 3, compiler_params=_params(("parallel",)))(w, g, m, v)


BIG = ("w_in", "w_att_o", "w_rec_o", "w_out", "w_ff1", "w_ff2")
SHARDED_VECS = ("conv_w", "b_rg_a", "b_rg_i", "lru_lambda")
SMALL = ("ln1_g", "b_in", "rpb", "conv_w", "conv_b", "w_rg_a", "b_rg_a", "w_rg_i", "b_rg_i", "lru_lambda",
         "ln2_g", "lnf_g")
SMALL_ROWS = 2240
ORDER = ("ln1_g", "w_in", "b_in", "rpb", "w_att_o", "conv_w", "conv_b", "w_rg_a", "b_rg_a", "w_rg_i", "b_rg_i",
         "lru_lambda", "w_rec_o", "w_out", "ln2_g", "w_ff1", "w_ff2", "lnf_g")


def _pack_small(grads, loss):
    parts, sizes = [], {}
    for n in SMALL:
        flat = grads[n].reshape(-1)
        pad = (-flat.shape[0]) % 128
        sizes[n] = (flat.shape[0], flat.shape[0] + pad)
        parts.append(jnp.pad(flat, (0, pad)))
    total = sum(s[1] for s in sizes.values())
    parts.append(jnp.pad(loss.reshape(1), (0, SMALL_ROWS * 128 - total - 1)))
    return jnp.concatenate(parts).reshape(SMALL_ROWS, 128), sizes


def _unpack_small(buf, sizes, shapes):
    flat = buf.reshape(-1)
    out, pos = {}, 0
    for n in SMALL:
        size, padded = sizes[n]
        out[n] = flat[pos:pos + size].reshape(shapes[n])
        pos += padded
    return out, flat[pos]


def _gather_weights(w, chip):
    chip_id = chip.astype(jnp.int32).reshape(1)
    vec_rows = [w[n][0] for n in SHARDED_VECS]
    vec_shard = jnp.concatenate(vec_rows + [jnp.zeros((16 - 10, D // N_CHIPS), F32)], axis=0)
    vec_slots = lax.dynamic_update_slice(jnp.zeros((N_CHIPS, 16, D // N_CHIPS), F32), vec_shard[None], (chip, 0, 0))
    bufs_a, sems_a, token_a = _gather_start("gather_start_first", [_cast_bf16("cast_w_in", w["w_in"][0], chip_id), vec_slots])
    rest_names = BIG[1:]
    bufs_b, sems_b, token_b = _gather_start(
        "gather_start_rest", [_cast_bf16("cast_" + n, w[n][0], chip_id, after=(token_a,)) for n in rest_names])

    def first(after):
        w_in_full, vec_full = _gather_forward("gather_forward_first", _gather_wait("gather_wait_first", bufs_a, sems_a, after))
        vecs = vec_full.transpose(1, 0, 2).reshape(16, D)
        return dict(w_in=w_in_full, conv_w=vecs[0:4], b_rg_a=vecs[4:6], b_rg_i=vecs[6:8], lru_lambda=vecs[8:10])

    def rest(after):
        full = dict(zip(rest_names, _gather_forward("gather_forward_rest",
                                                    _gather_wait("gather_wait_rest", bufs_b, sems_b, after))))
        return dict(w_att_o=full["w_att_o"], w_ff1=full["w_ff1"], w_rec_o=full["w_rec_o"].reshape(D, D),
                    w_out=full["w_out"].reshape(D, D), w_ff2=full["w_ff2"].reshape(D_FF, D))

    p = dict(ln1_g=w["ln1_g"], b_in=w["b_in"], rpb=w["rpb"][0], conv_b=w["conv_b"], w_rg_a=w["w_rg_a"][0],
             w_rg_i=w["w_rg_i"][0], ln2_g=w["ln2_g"], lnf_g=w["lnf_g"].reshape(1, D))
    return p, ((token_b,), first, rest)


class _Reducer:
    def __init__(self, ids):
        self.ids = ids
        self.groups = {}

    def begin(self, tag, grads, small=None):
        names = list(grads)
        big = [grads[n].reshape(N_CHIPS, -1, grads[n].shape[-1]) for n in names]
        flight, token = _pair_start("pair_start_" + tag, big, [] if small is None else [small])
        self.groups[tag] = dict(names=names, pair=flight, small=small is not None)
        return (token,)

    def advance(self, tag, after):
        grp = self.groups[tag]
        n = len(grp["names"])
        mine, got = _pair_wait("pair_wait_" + tag, grp["pair"], n, after)
        sums = [_pair_sum("pair_sum_" + name, a, b, self.ids) for name, a, b in zip(grp["names"], mine, got)]
        small_sum = _small_pair_sum(mine[n], got[n]) if grp["small"] else None
        grp["chip"], token = _chip_start("chip_start_" + tag, [s[1] for s in sums], small_sum)
        grp["sums"] = [s[0] for s in sums]
        return (token,)

    def finish(self, tag, after):
        grp = self.groups[tag]
        srcs, lands = _chip_wait("chip_wait_" + tag, grp["chip"], grp["small"], after)
        halves = [_chip_sum("chip_sum_" + name, s, b, self.ids) for name, s, b in zip(grp["names"], grp["sums"], lands)]
        if grp["small"]:
            halves.append(_small_chip_sum(srcs[-1], lands[-1], self.ids))
        return _half_swap("half_swap_" + tag, halves)


def kernel(x, ln1_g, w_in, b_in, rpb, w_att_o, conv_w, conv_b, w_rg_a, b_rg_a, w_rg_i, b_rg_i, lru_lambda, w_rec_o, w_out, ln2_g, w_ff1, w_ff2, lnf_g, loss_target, m_ln1_g, m_w_in, m_b_in, m_rpb, m_w_att_o, m_conv_w, m_conv_b, m_w_rg_a, m_b_rg_a, m_w_rg_i, m_b_rg_i, m_lru_lambda, m_w_rec_o, m_w_out, m_ln2_g, m_w_ff1, m_w_ff2, m_lnf_g, v_ln1_g, v_w_in, v_b_in, v_rpb, v_w_att_o, v_conv_w, v_conv_b, v_w_rg_a, v_b_rg_a, v_w_rg_i, v_b_rg_i, v_lru_lambda, v_w_rec_o, v_w_out, v_ln2_g, v_w_ff1, v_w_ff2, v_lnf_g):
    w = dict(ln1_g=ln1_g, w_in=w_in, b_in=b_in, rpb=rpb, w_att_o=w_att_o, conv_w=conv_w, conv_b=conv_b,
             w_rg_a=w_rg_a, b_rg_a=b_rg_a, w_rg_i=w_rg_i, b_rg_i=b_rg_i, lru_lambda=lru_lambda, w_rec_o=w_rec_o,
             w_out=w_out, ln2_g=ln2_g, w_ff1=w_ff1, w_ff2=w_ff2, lnf_g=lnf_g)
    m = dict(ln1_g=m_ln1_g, w_in=m_w_in, b_in=m_b_in, rpb=m_rpb, w_att_o=m_w_att_o, conv_w=m_conv_w,
             conv_b=m_conv_b, w_rg_a=m_w_rg_a, b_rg_a=m_b_rg_a, w_rg_i=m_w_rg_i, b_rg_i=m_b_rg_i,
             lru_lambda=m_lru_lambda, w_rec_o=m_w_rec_o, w_out=m_w_out, ln2_g=m_ln2_g, w_ff1=m_w_ff1,
             w_ff2=m_w_ff2, lnf_g=m_lnf_g)
    v = dict(ln1_g=v_ln1_g, w_in=v_w_in, b_in=v_b_in, rpb=v_rpb, w_att_o=v_w_att_o, conv_w=v_conv_w,
             conv_b=v_conv_b, w_rg_a=v_w_rg_a, b_rg_a=v_b_rg_a, w_rg_i=v_w_rg_i, b_rg_i=v_b_rg_i,
             lru_lambda=v_lru_lambda, w_rec_o=v_w_rec_o, w_out=v_w_out, ln2_g=v_ln2_g, w_ff1=v_w_ff1,
             w_ff2=v_w_ff2, lnf_g=v_lnf_g)
    chip = 2 * lax.axis_index("x") + lax.axis_index("y")
    ids = jnp.stack([chip, lax.axis_index("c")]).astype(jnp.int32)

    out_grad, out_delta, out_m, out_v = {}, {}, {}, {}

    def update(n, gn):
        shape = w[n].shape
        two_d, rb = (gn.shape, 256) if n in BIG else ((int(np.prod(shape[:-1])), shape[-1]), None)
        gn = gn.reshape(two_d)
        d, nm, nv = _adamw("adamw_" + n, w[n].reshape(two_d), gn, m[n].reshape(two_d), v[n].reshape(two_d), rb)
        out_grad[n], out_delta[n], out_m[n], out_v[n] = (gn.reshape(shape), d.reshape(shape), nm.reshape(shape),
                                                         nv.reshape(shape))
        return d

    reducer = _Reducer(ids)
    p, late = _gather_weights(w, chip)
    loss, grad_x, g = _local_step(x, loss_target, p, late, reducer)
    small, sizes = _pack_small(g, loss)
    after = reducer.begin("small", {}, small)[0]
    for tag in ("ff", "proj", "in"):
        for n, red in zip(reducer.groups[tag]["names"], reducer.finish(tag, after)):
            after = update(n, red)
        if tag == "ff":
            after = reducer.advance("small", after)[0]
    (small_red,) = reducer.finish("small", after)
    gsmall, loss = _unpack_small(small_red, sizes, {n: g[n].shape for n in SMALL})
    for n in SMALL:
        gn = gsmall[n]
        if n in SHARDED_VECS:
            gn = lax.dynamic_slice_in_dim(gn, chip * (D // N_CHIPS), D // N_CHIPS, axis=1)
        update(n, gn)
    return (loss, grad_x, *[out_grad[n] for n in ORDER], *[out_delta[n] for n in ORDER],
            *[out_m[n] for n in ORDER], *[out_v[n] for n in ORDER])
```

```python
import functools

import numpy as np
import jax
import jax.numpy as jnp
from jax import lax
from jax.experimental import pallas as pl
from jax.experimental.pallas import tpu as pltpu

F32 = jnp.float32
BF16 = jnp.bfloat16

T = 2048
D = 1024
D_ATT = 512
D_IN = 5632
D_FF = 4096
N_HEADS = 8
HEAD_DIM = 64
GRID_W = 64
N_ROWS = T // GRID_W
WIN_H = 8
WIN_W = 16
KEYS = WIN_H * GRID_W
N_CHIPS = 4
EPS = 1e-6
LRU_C = 8.0
SCALE = HEAD_DIM ** -0.5
REC_CB = 256
REC_CHUNK = 256
PAD = 8

ADAM_LR = 0.001
ADAM_B1 = 0.9
ADAM_B2 = 0.999
ADAM_EPS = 1e-08
ADAM_WD = 0.01
ADAM_STEP = 10

VMEM_LIMIT = 56 * 1024 * 1024

NN = (((1,), (0,)), ((), ()))
NT = (((1,), (1,)), ((), ()))
TN = (((0,), (0,)), ((), ()))
MESH = pl.DeviceIdType.MESH


def _params(sem=None):
    return pltpu.CompilerParams(dimension_semantics=sem, vmem_limit_bytes=VMEM_LIMIT)


def _dot(a, b, dims):
    return lax.dot_general(a, b, dims, preferred_element_type=F32)


def _sigmoid(x):
    return 0.5 * jnp.tanh(0.5 * x) + 0.5


def _matmul(name, a, b, *, dims, grid, a_spec, b_spec, out_shapes, out_specs, acc_shape,
            extras=(), extra_specs=(), epilogue=None, colsum_spec=None, colsum_shape=None, after=()):
    nk = grid[2]
    n_extra = len(extras)
    n_out = len(out_shapes)
    with_colsum = colsum_spec is not None

    def body(a_ref, b_ref, *rest):
        ex = rest[:n_extra]
        rest = rest[:n_extra] + rest[n_extra + len(after):]
        outs = rest[n_extra:n_extra + n_out]
        pos = n_extra + n_out
        cs_out = rest[pos] if with_colsum else None
        pos += 1 if with_colsum else 0
        acc = rest[pos]
        cs_acc = rest[pos + 1] if with_colsum else None
        k = pl.program_id(2)

        @pl.when(k == 0)
        def _():
            acc[...] = jnp.zeros_like(acc)
            if with_colsum:
                cs_acc[...] = jnp.zeros_like(cs_acc)

        bv = b_ref[...]
        acc[...] += _dot(a_ref[...].astype(BF16), bv.astype(BF16), dims)
        if with_colsum:
            cs_acc[...] += jnp.sum(bv.astype(F32), axis=0, keepdims=True)

        @pl.when(k == nk - 1)
        def _():
            r = acc[...]
            if epilogue is None:
                outs[0][...] = r.astype(outs[0].dtype)
            else:
                epilogue(r, ex, outs)
            if with_colsum:
                cs_out[...] = cs_acc[...]

    shapes = list(out_shapes)
    specs = list(out_specs)
    scratch = [pltpu.VMEM(acc_shape, F32)]
    if with_colsum:
        shapes.append(colsum_shape)
        specs.append(colsum_spec)
        scratch.append(pltpu.VMEM((1, acc_shape[1]), F32))
    res = pl.pallas_call(
        body, name=name, grid=grid,
        in_specs=[a_spec, b_spec, *extra_specs] + [_ANY] * len(after),
        out_specs=specs, out_shape=shapes, scratch_shapes=scratch,
        compiler_params=_params(("parallel", "parallel", "arbitrary")),
    )(a, b, *extras, *after)
    return res


def _sds(shape, dtype):
    return jax.ShapeDtypeStruct(shape, dtype)


TM = 1024
NI = T // TM


def _mm_nn_cols(name, a, wg, out_dtype, *, bias=None, extras=(), extra_specs=(), epilogue=None,
                out_shapes=None, out_specs=None):
    k_dim, n4 = wg.shape[1], wg.shape[2]
    ex, exs = list(extras), list(extra_specs)
    if bias is not None:
        ex = [bias] + ex
        exs = [pl.BlockSpec((1, n4), lambda j, i, k: (0, j))] + exs
        user_ep = epilogue

        def epilogue(r, e, outs):
            r = r + e[0][...]
            if user_ep is None:
                outs[0][...] = r.astype(outs[0].dtype)
            else:
                user_ep(r, e[1:], outs)
    if out_shapes is None:
        out_shapes = [_sds((T, N_CHIPS * n4), out_dtype)]
        out_specs = [pl.BlockSpec((TM, n4), lambda j, i, k: (i, j))]
    return _matmul(
        name, a, wg, dims=NN, grid=(N_CHIPS, NI, 1),
        a_spec=pl.BlockSpec((TM, k_dim), lambda j, i, k: (i, 0)),
        b_spec=pl.BlockSpec((None, k_dim, n4), lambda j, i, k: (j, 0, 0)),
        out_shapes=out_shapes, out_specs=out_specs, acc_shape=(TM, n4),
        extras=ex, extra_specs=exs, epilogue=epilogue)


def _mm_nn_rows(name, a, w, out_dtype, *, tk, extras=(), extra_specs=(), epilogue=None):
    k_dim, n = w.shape
    return _matmul(
        name, a, w, dims=NN, grid=(NI, 1, k_dim // tk),
        a_spec=pl.BlockSpec((TM, tk), lambda i, j, k: (i, k)),
        b_spec=pl.BlockSpec((tk, n), lambda i, j, k: (k, 0)),
        out_shapes=[_sds((T, n), out_dtype)],
        out_specs=[pl.BlockSpec((TM, n), lambda i, j, k: (i, 0))], acc_shape=(TM, n),
        extras=extras, extra_specs=extra_specs, epilogue=epilogue)


def _mm_nt_cols(name, a, wg, out_dtype, after=()):
    k_dim, n4 = wg.shape[1], wg.shape[2]
    return _matmul(
        name, a, wg, dims=NT, grid=(NI, 1, N_CHIPS),
        a_spec=pl.BlockSpec((TM, n4), lambda i, j, k: (i, k)),
        b_spec=pl.BlockSpec((None, k_dim, n4), lambda i, j, k: (k, 0, 0)),
        out_shapes=[_sds((T, k_dim), out_dtype)],
        out_specs=[pl.BlockSpec((TM, k_dim), lambda i, j, k: (i, 0))], acc_shape=(TM, k_dim), after=after)


def _mm_nt_rows(name, a, w, out_dtype, *, tn, extras=(), extra_specs=(), epilogue=None):
    k_dim, n = w.shape
    return _matmul(
        name, a, w, dims=NT, grid=(k_dim // tn, NI, 1),
        a_spec=pl.BlockSpec((TM, n), lambda j, i, k: (i, 0)),
        b_spec=pl.BlockSpec((tn, n), lambda j, i, k: (j, 0)),
        out_shapes=[_sds((T, k_dim), out_dtype)],
        out_specs=[pl.BlockSpec((TM, tn), lambda j, i, k: (i, j))], acc_shape=(TM, tn),
        extras=extras, extra_specs=extra_specs, epilogue=epilogue)


def _mm_tn_cols(name, a, g, n4, *, colsum=False):
    k_dim = a.shape[1]
    kw = {}
    if colsum:
        kw = dict(colsum_spec=pl.BlockSpec((1, n4), lambda j, i, k: (0, j)),
                  colsum_shape=_sds((1, N_CHIPS * n4), F32))
    return _matmul(
        name, a, g, dims=TN, grid=(N_CHIPS, 1, NI),
        a_spec=pl.BlockSpec((TM, k_dim), lambda j, i, k: (k, 0)),
        b_spec=pl.BlockSpec((TM, n4), lambda j, i, k: (k, j)),
        out_shapes=[_sds((N_CHIPS, k_dim, n4), F32)],
        out_specs=[pl.BlockSpec((None, k_dim, n4), lambda j, i, k: (j, 0, 0))],
        acc_shape=(k_dim, n4), **kw)


def _mm_tn_rows(name, a, g, *, tm):
    k_dim, n = a.shape[1], g.shape[1]
    return _matmul(
        name, a, g, dims=TN, grid=(k_dim // tm, 1, NI),
        a_spec=pl.BlockSpec((TM, tm), lambda j, i, k: (k, j)),
        b_spec=pl.BlockSpec((TM, n), lambda j, i, k: (k, 0)),
        out_shapes=[_sds((k_dim, n), F32)],
        out_specs=[pl.BlockSpec((tm, n), lambda j, i, k: (j, 0))], acc_shape=(tm, n))


TE = 256
NE = T // TE
_ROW = pl.BlockSpec((TE, D), lambda i: (i, 0))
_VEC = pl.BlockSpec((1, D), lambda i: (0, 0))


def _rms_fwd(name, x, g, after=()):
    def body(x_ref, g_ref, *rest):
        h_ref = rest[-1]
        xv = x_ref[...]
        rstd = lax.rsqrt(jnp.mean(xv * xv, axis=-1, keepdims=True) + EPS)
        h_ref[...] = (xv * rstd * g_ref[...]).astype(BF16)

    return pl.pallas_call(body, name=name, grid=(NE,), in_specs=[_ROW, _VEC] + [_ANY] * len(after), out_specs=_ROW,
                          out_shape=_sds((T, D), BF16), compiler_params=_params(("parallel",)))(x, g, *after)


def _rms_bwd(name, dh, x, g, dres, after=()):
    def body(dh_ref, x_ref, g_ref, dres_ref, *rest):
        dx_ref, dg_ref = rest[-2:]
        xv = x_ref[...]
        rstd = lax.rsqrt(jnp.mean(xv * xv, axis=-1, keepdims=True) + EPS)
        xhat = xv * rstd
        dhv = dh_ref[...]
        dy = dhv * g_ref[...]
        dx_ref[...] = dres_ref[...] + rstd * (dy - xhat * jnp.mean(dy * xhat, axis=-1, keepdims=True))

        @pl.when(pl.program_id(0) == 0)
        def _():
            dg_ref[...] = jnp.zeros_like(dg_ref)

        dg_ref[...] += jnp.sum(dhv * xhat, axis=0, keepdims=True)

    return pl.pallas_call(body, name=name, grid=(NE,), in_specs=[_ROW, _ROW, _VEC, _ROW] + [_ANY] * len(after),
                          out_specs=[_ROW, _VEC], out_shape=[_sds((T, D), F32), _sds((1, D), F32)],
                          compiler_params=_params(("arbitrary",)))(dh, x, g, dres, *after)


def _loss_head(x2, target, g):
    def body(x_ref, t_ref, g_ref, loss_ref, dx_ref, dg_ref):
        xv = x_ref[...]
        rstd = lax.rsqrt(jnp.mean(xv * xv, axis=-1, keepdims=True) + EPS)
        xhat = xv * rstd
        gv = g_ref[...]
        err = xhat * gv - t_ref[...]
        dy = err * (1.0 / D)
        dxh = dy * gv
        dx_ref[...] = rstd * (dxh - xhat * jnp.mean(dxh * xhat, axis=-1, keepdims=True))

        @pl.when(pl.program_id(0) == 0)
        def _():
            dg_ref[...] = jnp.zeros_like(dg_ref)
            loss_ref[...] = jnp.zeros_like(loss_ref)

        dg_ref[...] += jnp.sum(dy * xhat, axis=0, keepdims=True)
        loss_ref[...] += (0.5 / D) * jnp.sum(jnp.sum(err * err, axis=1, keepdims=True), axis=0, keepdims=True)

    return pl.pallas_call(
        body, name="loss_head", grid=(NE,), in_specs=[_ROW, _ROW, _VEC],
        out_specs=[pl.BlockSpec((1, 1), lambda i: (0, 0)), _ROW, _VEC],
        out_shape=[_sds((1, 1), F32), _sds((T, D), F32), _sds((1, D), F32)],
        compiler_params=_params(("arbitrary",)))(x2, target, g)


MW = 512
_G_ATT_BLK = 3584 // MW
_G_REC_BLK = 4608 // MW


def _merge_specs():
    y = pl.BlockSpec((TM, MW), lambda i, j: (i, j))
    ga = pl.BlockSpec((TM, MW), lambda i, j: (i, _G_ATT_BLK + j))
    gr = pl.BlockSpec((TM, MW), lambda i, j: (i, _G_REC_BLK + j))
    return y, ga, gr


def _merge_fwd(y_att, y_rec, z):
    y, ga, gr = _merge_specs()

    def body(ya_ref, yr_ref, ga_ref, gr_ref, m_ref):
        m = _sigmoid(ga_ref[...]) * ya_ref[...] + _sigmoid(gr_ref[...]) * yr_ref[...]
        m_ref[...] = m.astype(BF16)

    return pl.pallas_call(body, name="merge_fwd", grid=(NI, D // MW), in_specs=[y, y, ga, gr], out_specs=y,
                          out_shape=_sds((T, D), BF16),
                          compiler_params=_params(("parallel", "parallel")))(y_att, y_rec, z, z)


def _merge_bwd(dm, y_att, y_rec, z):
    y, ga, gr = _merge_specs()

    def body(dm_ref, ya_ref, yr_ref, ga_ref, gr_ref, dya_ref, dyr_ref, dga_ref, dgr_ref):
        dmv = dm_ref[...]
        sa = _sigmoid(ga_ref[...])
        sr = _sigmoid(gr_ref[...])
        dya_ref[...] = (dmv * sa).astype(BF16)
        dyr_ref[...] = (dmv * sr).astype(BF16)
        dga_ref[...] = (dmv * ya_ref[...] * sa * (1.0 - sa)).astype(BF16)
        dgr_ref[...] = (dmv * yr_ref[...] * sr * (1.0 - sr)).astype(BF16)

    return pl.pallas_call(body, name="merge_bwd", grid=(NI, D // MW), in_specs=[y, y, y, ga, gr],
                          out_specs=[y, y, y, y], out_shape=[_sds((T, D), BF16)] * 4,
                          compiler_params=_params(("parallel", "parallel")))(dm, y_att, y_rec, z, z)


HP = 2 * HEAD_DIM
N_HP = N_HEADS // 2
ATT_UNROLL_FWD = 8
ATT_UNROLL_BWD = 4


def _window_maps():
    diag = np.zeros((GRID_W * GRID_W, 128), np.float32)
    for qc in range(GRID_W):
        w0 = min(max(qc - WIN_W // 2, 0), GRID_W - WIN_W)
        for kc in range(w0, w0 + WIN_W):
            diag[qc * GRID_W + kc, kc - qc + WIN_W - 1] = 1.0
    return diag, diag.sum(axis=1)[None, :]


def _split3(x):
    a = x.astype(BF16)
    r = x - a.astype(F32)
    b = r.astype(BF16)
    c = (r - b.astype(F32)).astype(BF16)
    return a, b, c


N_DROW = 2 * WIN_H - 1
N_DPAIR = N_DROW - 1


def _bias_pairs(rpb):
    diag, valid = _window_maps()
    r2 = jnp.pad(rpb.reshape(N_HEADS * N_DROW, 2 * WIN_W - 1),
                 ((0, 128 - N_HEADS * N_DROW), (0, 128 - (2 * WIN_W - 1))))

    def body(r_ref, d_ref, v_ref, o_ref):
        dv = d_ref[...]
        t = sum(_dot(part, dv, NN) for part in _split3(r_ref[...]))
        o_ref[...] = jnp.where(v_ref[...] > 0.0, t, -1e30)

    t = pl.pallas_call(body, name="rpb_expand", out_shape=_sds((128, GRID_W * GRID_W), F32),
                       compiler_params=_params())(r2, jnp.asarray(diag.T, BF16), jnp.asarray(valid, F32))
    t = t[:N_HEADS * N_DROW].reshape(N_HEADS, N_DROW, GRID_W, GRID_W)
    return jnp.concatenate([t[:, :N_DPAIR], t[:, 1:]], axis=-1)


def _row_bias(tb_ref, hh, d0):
    return jnp.concatenate([tb_ref[hh, d0 + 2 * ii] for ii in range(WIN_H // 2)], axis=1)


def _row_window(r):
    rs = jnp.clip(r - WIN_H // 2, 0, N_ROWS - WIN_H)
    return pl.multiple_of(r * GRID_W, GRID_W), pl.multiple_of(rs * GRID_W, GRID_W), rs - r + (WIN_H - 1)


def _split_heads(src_ref, dst_ref, scale=None):
    for hh in range(2):
        v = src_ref[:, hh * HEAD_DIM:(hh + 1) * HEAD_DIM]
        dst_ref[hh] = (v if scale is None else v * scale).astype(BF16)


def _attn_items(qb_ref, kb_ref, vb_ref, tb_ref, first_row, n_rows):
    wins = [_row_window(first_row + u) for u in range(n_rows)]
    items = [(u, hh) for u in range(n_rows) for hh in range(2)]
    q = [qb_ref[hh, pl.ds(wins[u][0], GRID_W), :] for u, hh in items]
    k = [kb_ref[hh, pl.ds(wins[u][1], KEYS), :] for u, hh in items]
    v = [vb_ref[hh, pl.ds(wins[u][1], KEYS), :] for u, hh in items]
    s = [_dot(qi, ki, NT) + _row_bias(tb_ref, hh, wins[u][2]) for qi, ki, (u, hh) in zip(q, k, items)]
    m = [jnp.max(si, axis=-1, keepdims=True) for si in s]
    e = [jnp.exp(si - mi) for si, mi in zip(s, m)]
    inv = [1.0 / jnp.sum(ei, axis=-1, keepdims=True) for ei in e]
    p = [ei * li for ei, li in zip(e, inv)]
    return wins, items, q, k, v, p


def _attn_in_specs():
    q = pl.BlockSpec((T, HP), lambda p: (0, p))
    k = pl.BlockSpec((T, HP), lambda p: (0, N_HP + p))
    v = pl.BlockSpec((T, HP), lambda p: (0, 2 * N_HP + p))
    tb = pl.BlockSpec((2, N_DPAIR, GRID_W, HP), lambda p: (p, 0, 0, 0))
    return q, k, v, tb


_HEAD_SCRATCH = pltpu.VMEM((2, T, HEAD_DIM), BF16)


def _attn_fwd(z, tb):
    def body(q_ref, k_ref, v_ref, tb_ref, o_ref, qb_ref, kb_ref, vb_ref):
        _split_heads(q_ref, qb_ref, SCALE)
        _split_heads(k_ref, kb_ref)
        _split_heads(v_ref, vb_ref)

        def rows(it, carry):
            wins, items, _, _, v, p = _attn_items(qb_ref, kb_ref, vb_ref, tb_ref, it * ATT_UNROLL_FWD, ATT_UNROLL_FWD)
            o = [_dot(pi.astype(BF16), vi, NN) for pi, vi in zip(p, v)]
            for u, (q0, _, _) in enumerate(wins):
                o_ref[pl.ds(q0, GRID_W), :] = jnp.concatenate(o[2 * u:2 * u + 2], axis=1).astype(BF16)
            return carry

        lax.fori_loop(0, N_ROWS // ATT_UNROLL_FWD, rows, 0)

    blk = pl.BlockSpec((T, HP), lambda p: (0, p))
    return pl.pallas_call(
        body, name="attn_fwd", grid=(N_HP,), in_specs=list(_attn_in_specs()), out_specs=blk,
        out_shape=_sds((T, D_ATT), BF16), scratch_shapes=[_HEAD_SCRATCH] * 3,
        compiler_params=_params(("parallel",)))(z, z, z, tb)


def _attn_bwd(z, tb, d_att, after=()):
    def body(q_ref, k_ref, v_ref, tb_ref, do_ref, *rest):
        dq_ref, dk_ref, dv_ref, ds_ref, qb_ref, kb_ref, vb_ref, dob_ref, dka_ref, dva_ref = rest[len(after):]
        _split_heads(q_ref, qb_ref, SCALE)
        _split_heads(k_ref, kb_ref)
        _split_heads(v_ref, vb_ref)
        _split_heads(do_ref, dob_ref)
        dka_ref[...] = jnp.zeros_like(dka_ref)
        dva_ref[...] = jnp.zeros_like(dva_ref)
        ds_ref[...] = jnp.zeros_like(ds_ref)

        def rows(it, carry):
            wins, items, q, k, v, p = _attn_items(qb_ref, kb_ref, vb_ref, tb_ref, it * ATT_UNROLL_BWD, ATT_UNROLL_BWD)
            do = [dob_ref[hh, pl.ds(wins[u][0], GRID_W), :] for u, hh in items]
            dv = [_dot(pi.astype(BF16), di, TN) for pi, di in zip(p, do)]
            dp = [_dot(di, vi, NT) for di, vi in zip(do, v)]
            ds = [pi * (dpi - jnp.sum(dpi * pi, axis=-1, keepdims=True)) for pi, dpi in zip(p, dp)]
            dsb = [d.astype(BF16) for d in ds]
            dq = [_dot(d, ki, NN) * SCALE for d, ki in zip(dsb, k)]
            dk = [_dot(d, qi, TN) for d, qi in zip(dsb, q)]
            for d, (u, hh) in zip(ds, items):
                for ii in range(WIN_H // 2):
                    ds_ref[hh, wins[u][2] + 2 * ii] += d[:, ii * HP:(ii + 1) * HP]
            for u, (q0, k0, _) in enumerate(wins):
                dq_ref[pl.ds(q0, GRID_W), :] = jnp.concatenate(dq[2 * u:2 * u + 2], axis=1).astype(BF16)
                dka_ref[pl.ds(k0, KEYS), :] += jnp.concatenate(dk[2 * u:2 * u + 2], axis=1)
                dva_ref[pl.ds(k0, KEYS), :] += jnp.concatenate(dv[2 * u:2 * u + 2], axis=1)
            return carry

        lax.fori_loop(0, N_ROWS // ATT_UNROLL_BWD, rows, 0)
        dk_ref[...] = dka_ref[...].astype(BF16)
        dv_ref[...] = dva_ref[...].astype(BF16)

    blk = pl.BlockSpec((T, HP), lambda p: (0, p))
    q, k, v, tbs = _attn_in_specs()
    return pl.pallas_call(
        body, name="attn_bwd", grid=(N_HP,), in_specs=[q, k, v, tbs, blk] + [_ANY] * len(after),
        out_specs=[blk, blk, blk, tbs],
        out_shape=[_sds((T, D_ATT), BF16)] * 3 + [_sds((N_HEADS, N_DPAIR, GRID_W, HP), F32)],
        scratch_shapes=[_HEAD_SCRATCH] * 4 + [pltpu.VMEM((T, HP), F32), pltpu.VMEM((T, HP), F32)],
        compiler_params=_params(("parallel",)))(z, z, z, tb, d_att, *after)


def _rpb_grad(ds_acc):
    a = ds_acc.reshape(N_HEADS, N_DPAIR, GRID_W, 2, GRID_W).transpose(0, 1, 3, 2, 4)
    a = jnp.pad(a.reshape(N_HEADS * N_DPAIR * 2, GRID_W * GRID_W), ((0, 256 - N_HEADS * N_DPAIR * 2), (0, 0)))
    sel = np.zeros((128, 256), np.float32)
    for h in range(N_HEADS):
        for pair in range(N_DPAIR):
            for half in range(2):
                sel[h * N_DROW + pair + half, (h * N_DPAIR + pair) * 2 + half] = 1.0
    diag, _ = _window_maps()

    def body(a_ref, sel_ref, diag_ref, o_ref):
        selv = sel_ref[...]
        g = sum(_dot(selv, part, NN) for part in _split3(a_ref[...]))
        diagv = diag_ref[...]
        o_ref[...] = sum(_dot(part, diagv, NN) for part in _split3(g))

    out = pl.pallas_call(body, name="rpb_grad", out_shape=_sds((128, 128), F32),
                         compiler_params=_params())(a, jnp.asarray(sel, BF16), jnp.asarray(diag, BF16))
    return out[:N_HEADS * N_DROW, :2 * WIN_W - 1].reshape(N_HEADS, N_DROW, 2 * WIN_W - 1)


N_CB = D // REC_CB
N_CHUNK = T // REC_CHUNK
N_TILE = T // 8
_U_BLK = 1536 // REC_CB
_Y_BLK = 2560 // REC_CB


def _block_diag(w):
    per = REC_CB // 64
    wt = w.reshape(2, N_CB, per, 64, 64)
    eye = jnp.eye(per, dtype=w.dtype)
    full = wt[:, :, :, :, None, :] * eye[None, None, :, None, :, None]
    return full.reshape(2, N_CB, REC_CB, REC_CB).astype(BF16)


def _block_diag_grad(g):
    per = REC_CB // 64
    g6 = g.reshape(2, N_CB, per, 64, per, 64)
    return jnp.stack([g6[:, :, p, :, p, :] for p in range(per)], axis=2).reshape(2, 16, 64, 64)


def _gelu(x):
    c = 0.7978845608028654
    return 0.5 * x * (1.0 + jnp.tanh(c * (x + 0.044715 * x * x * x)))


def _gelu_grad(x):
    c = 0.7978845608028654
    th = jnp.tanh(c * (x + 0.044715 * x * x * x))
    return 0.5 * (1.0 + th) + 0.5 * x * (1.0 - th * th) * c * (1.0 + 3.0 * 0.044715 * x * x)


def _softplus_neg(lam):
    x = -lam
    e = jnp.exp(-jnp.abs(x))
    w = 1.0 + e
    l1p = jnp.where(w == 1.0, e, jnp.log(w) * e / (w - 1.0))
    return jnp.maximum(x, 0.0) + l1p


def _one_minus_exp(x):
    poly = x * (1.0 + x * (1 / 2 + x * (1 / 6 + x * (1 / 24 + x * (1 / 120 + x * (1 / 720))))))
    return jnp.where(x > -0.125, -poly, 1.0 - jnp.exp(x))


def _conv_taps(pad_ref, t0, w, sign):
    out = None
    for j in range(4):
        term = w[j:j + 1, :] * pad_ref[pl.ds(PAD + t0 + sign * (j - 2), REC_CHUNK), :]
        out = term if out is None else out + term
    return out


def _gates(u, wa, wi, ba, bi, sp):
    ub = u.astype(BF16)
    r = _sigmoid(_dot(ub, wa, NN) + ba)
    i = _sigmoid(_dot(ub, wi, NN) + bi)
    log_a = -LRU_C * r * sp
    a = jnp.exp(log_a)
    mult = jnp.sqrt(jnp.maximum(_one_minus_exp(2.0 * log_a), 0.0))
    return r, i, a, mult


def _tile_scan(a, b, sub, reverse):
    for s in (1, 2, 4):
        if reverse:
            a_s, b_s, m = pltpu.roll(a, 8 - s, 0), pltpu.roll(b, 8 - s, 0), sub < 8 - s
        else:
            a_s, b_s, m = pltpu.roll(a, s, 0), pltpu.roll(b, s, 0), sub >= s
        b = jnp.where(m, a * b_s + b, b)
        a = jnp.where(m, a * a_s, a)
    return a, b


def _last_row(x, sub, row):
    return jnp.broadcast_to(jnp.sum(jnp.where(sub == row, x, 0.0), axis=0, keepdims=True), x.shape)


def _rec_prologue(up_ref, cw_ref, cb_ref, wa_ref, wi_ref, ba_ref, bi_ref, lam_ref,
                  upad_ref, u_ref, a_refs, h_refs):
    cb = up_ref.shape[1]
    zeros = jnp.zeros((PAD, cb), F32)
    upad_ref[pl.ds(0, PAD), :] = zeros
    upad_ref[pl.ds(PAD + T, PAD), :] = zeros
    upad_ref[pl.ds(PAD, T), :] = up_ref[...]
    cw = cw_ref[...]
    sp = _softplus_neg(lam_ref[...])
    for c in range(N_CHUNK):
        t0 = c * REC_CHUNK
        u = cb_ref[...] + _conv_taps(upad_ref, t0, cw, 1)
        u_ref[pl.ds(t0, REC_CHUNK), :] = u
        for d in range(2):
            _, i, a, mult = _gates(u, wa_ref[d], wi_ref[d], ba_ref[d:d + 1, :], bi_ref[d:d + 1, :], sp[d:d + 1, :])
            a_refs[d][pl.ds(t0, REC_CHUNK), :] = a
            h_refs[d][pl.ds(t0, REC_CHUNK), :] = mult * (i * u)

    sub = lax.broadcasted_iota(jnp.int32, (8, cb), 0)

    def tile(k, carry):
        cf, cr = carry
        tf = pl.multiple_of(k * 8, 8)
        tr = pl.multiple_of((N_TILE - 1 - k) * 8, 8)
        af, bf = _tile_scan(a_refs[0][pl.ds(tf, 8), :], h_refs[0][pl.ds(tf, 8), :], sub, False)
        hf = af * cf + bf
        h_refs[0][pl.ds(tf, 8), :] = hf
        ar, br = _tile_scan(a_refs[1][pl.ds(tr, 8), :], h_refs[1][pl.ds(tr, 8), :], sub, True)
        hr = ar * cr + br
        h_refs[1][pl.ds(tr, 8), :] = hr
        return _last_row(hf, sub, 7), _last_row(hr, sub, 0)

    z8 = jnp.zeros((8, cb), F32)
    lax.fori_loop(0, N_TILE, tile, (z8, z8))
    return sp


def _rec_specs():
    up = pl.BlockSpec((T, REC_CB), lambda c: (0, _U_BLK + c))
    yb = pl.BlockSpec((T, REC_CB), lambda c: (0, _Y_BLK + c))
    cw = pl.BlockSpec((4, REC_CB), lambda c: (0, c))
    cbias = pl.BlockSpec((1, REC_CB), lambda c: (0, c))
    wbd = pl.BlockSpec((2, None, REC_CB, REC_CB), lambda c: (0, c, 0, 0))
    vec2 = pl.BlockSpec((2, REC_CB), lambda c: (0, c))
    col = pl.BlockSpec((T, REC_CB), lambda c: (0, c))
    return up, yb, cw, cbias, wbd, vec2, col


def _rec_fwd(z, conv_w, conv_b, wa, wi, ba, bi, lam):
    up, yb, cw, cbias, wbd, vec2, col = _rec_specs()

    def body(up_ref, yb_ref, cw_ref, cb_ref, wa_ref, wi_ref, ba_ref, bi_ref, lam_ref, g_ref,
             u_ref, af_ref, ar_ref, hf_ref, hr_ref, upad_ref):
        _rec_prologue(up_ref, cw_ref, cb_ref, wa_ref, wi_ref, ba_ref, bi_ref, lam_ref,
                      upad_ref, u_ref, (af_ref, ar_ref), (hf_ref, hr_ref))

        def chunk(c, carry):
            t0 = pl.multiple_of(c * REC_CHUNK, REC_CHUNK)
            rows = pl.ds(t0, REC_CHUNK)
            g_ref[rows, :] = ((hf_ref[rows, :] + hr_ref[rows, :]) * _gelu(yb_ref[rows, :])).astype(BF16)
            return carry

        lax.fori_loop(0, N_CHUNK, chunk, 0)

    res = pl.pallas_call(
        body, name="rec_fwd", grid=(N_CB,),
        in_specs=[up, yb, cw, cbias, wbd, wbd, vec2, vec2, vec2], out_specs=[col] * 6,
        out_shape=[_sds((T, D), BF16)] + [_sds((T, D), F32)] * 5,
        scratch_shapes=[pltpu.VMEM((T + 2 * PAD, REC_CB), F32)],
        compiler_params=_params(("parallel",)))(z, z, conv_w, conv_b, wa, wi, ba, bi, lam)
    return res[0], tuple(res[1:])


def _rec_bwd(z, dg, saved, conv_w, conv_b, wa, wi, ba, bi, lam, after=()):
    up, yb, cw, cbias, wbd, vec2, col = _rec_specs()

    def body(up_ref, yb_ref, dg_ref, u_ref, af_ref, ar_ref, hf_ref, hr_ref,
             cw_ref, cb_ref, wa_ref, wi_ref, ba_ref, bi_ref, lam_ref, *rest):
        (dup_ref, dyb_ref, dcw_ref, dcb_ref, dwa_ref, dwi_ref, dba_ref, dbi_ref, dlam_ref,
         upad_ref, dh_ref, gf_ref, gr_ref, daf_ref, dar_ref, dupad_ref) = rest[len(after):]
        g_refs, da_refs = (gf_ref, gr_ref), (daf_ref, dar_ref)
        cb = up_ref.shape[1]
        zeros = jnp.zeros((PAD, cb), F32)
        upad_ref[pl.ds(0, PAD), :] = zeros
        upad_ref[pl.ds(PAD + T, PAD), :] = zeros
        upad_ref[pl.ds(PAD, T), :] = up_ref[...]
        sp = _softplus_neg(lam_ref[...])

        def gate_chunk(c, carry):
            t0 = pl.multiple_of(c * REC_CHUNK, REC_CHUNK)
            rows = pl.ds(t0, REC_CHUNK)
            y = yb_ref[rows, :]
            dgv = dg_ref[rows, :].astype(F32)
            dh_ref[rows, :] = dgv * _gelu(y)
            dyb_ref[rows, :] = (dgv * (hf_ref[rows, :] + hr_ref[rows, :]) * _gelu_grad(y)).astype(BF16)
            return carry

        lax.fori_loop(0, N_CHUNK, gate_chunk, 0)

        sub = lax.broadcasted_iota(jnp.int32, (8, cb), 0)

        def tile(k, carry):
            cf, cr = carry
            kf = N_TILE - 1 - k
            tf = pl.multiple_of(kf * 8, 8)
            tnext = pl.multiple_of(jnp.minimum(kf + 1, N_TILE - 1) * 8, 8)
            tprev = pl.multiple_of(jnp.maximum(kf - 1, 0) * 8, 8)
            a_t = af_ref[pl.ds(tf, 8), :]
            a_n = jnp.where(kf < N_TILE - 1, af_ref[pl.ds(tnext, 8), :], 0.0)
            a_sh = jnp.where(sub == 7, pltpu.roll(a_n, 7, 0), pltpu.roll(a_t, 7, 0))
            ca, cbb = _tile_scan(a_sh, dh_ref[pl.ds(tf, 8), :], sub, True)
            gf = ca * cf + cbb
            h_t = hf_ref[pl.ds(tf, 8), :]
            h_p = jnp.where(kf > 0, hf_ref[pl.ds(tprev, 8), :], 0.0)
            h_sh = jnp.where(sub == 0, pltpu.roll(h_p, 1, 0), pltpu.roll(h_t, 1, 0))
            gf_ref[pl.ds(tf, 8), :] = gf
            daf_ref[pl.ds(tf, 8), :] = gf * h_sh
            tr = pl.multiple_of(k * 8, 8)
            rnext = pl.multiple_of(jnp.minimum(k + 1, N_TILE - 1) * 8, 8)
            rprev = pl.multiple_of(jnp.maximum(k - 1, 0) * 8, 8)
            b_t = ar_ref[pl.ds(tr, 8), :]
            b_p = jnp.where(k > 0, ar_ref[pl.ds(rprev, 8), :], 0.0)
            b_sh = jnp.where(sub == 0, pltpu.roll(b_p, 1, 0), pltpu.roll(b_t, 1, 0))
            ra, rb = _tile_scan(b_sh, dh_ref[pl.ds(tr, 8), :], sub, False)
            gr = ra * cr + rb
            hr_t = hr_ref[pl.ds(tr, 8), :]
            hr_n = jnp.where(k < N_TILE - 1, hr_ref[pl.ds(rnext, 8), :], 0.0)
            hr_sh = jnp.where(sub == 7, pltpu.roll(hr_n, 7, 0), pltpu.roll(hr_t, 7, 0))
            gr_ref[pl.ds(tr, 8), :] = gr
            dar_ref[pl.ds(tr, 8), :] = gr * hr_sh
            return _last_row(gf, sub, 0), _last_row(gr, sub, 7)

        z8 = jnp.zeros((8, cb), F32)
        lax.fori_loop(0, N_TILE, tile, (z8, z8))

        dupad_ref[pl.ds(0, PAD), :] = zeros
        dupad_ref[pl.ds(PAD + T, PAD), :] = zeros
        dwa_ref[...] = jnp.zeros_like(dwa_ref)
        dwi_ref[...] = jnp.zeros_like(dwi_ref)
        dba_ref[...] = jnp.zeros_like(dba_ref)
        dbi_ref[...] = jnp.zeros_like(dbi_ref)
        dlam_ref[...] = jnp.zeros_like(dlam_ref)

        def grad_chunk(c, carry):
            t0 = pl.multiple_of(c * REC_CHUNK, REC_CHUNK)
            rows = pl.ds(t0, REC_CHUNK)
            u = u_ref[rows, :]
            ub = u.astype(BF16)
            du = jnp.zeros((REC_CHUNK, cb), F32)
            for d in range(2):
                r, i, a, mult = _gates(u, wa_ref[d], wi_ref[d], ba_ref[d:d + 1, :], bi_ref[d:d + 1, :], sp[d:d + 1, :])
                dbx = g_refs[d][rows, :]
                dmult = dbx * (i * u)
                diu = dbx * mult
                a2 = a * a
                dlog = da_refs[d][rows, :] * a - dmult * jnp.where(mult > 0.0, a2 / mult, 0.0)
                dpa = (dlog * (-LRU_C) * sp[d:d + 1, :]) * r * (1.0 - r)
                dpi = (diu * u) * i * (1.0 - i)
                dpab, dpib = dpa.astype(BF16), dpi.astype(BF16)
                du = du + diu * i + _dot(dpab, wa_ref[d], NT) + _dot(dpib, wi_ref[d], NT)
                dwa_ref[d] += _dot(ub, dpab, TN)
                dwi_ref[d] += _dot(ub, dpib, TN)
                dba_ref[d:d + 1, :] += jnp.sum(dpa, axis=0, keepdims=True)
                dbi_ref[d:d + 1, :] += jnp.sum(dpi, axis=0, keepdims=True)
                dlam_ref[d:d + 1, :] += jnp.sum(dlog * r, axis=0, keepdims=True)
            dupad_ref[pl.ds(PAD + t0, REC_CHUNK), :] = du
            return carry

        lax.fori_loop(0, N_CHUNK, grad_chunk, 0)
        dlam_ref[...] = dlam_ref[...] * (LRU_C * _sigmoid(-lam_ref[...]))

        cw = cw_ref[...]
        dcb = jnp.zeros((1, cb), F32)
        dcw = [jnp.zeros((1, cb), F32) for _ in range(4)]
        for c in range(N_CHUNK):
            t0 = c * REC_CHUNK
            du = dupad_ref[pl.ds(PAD + t0, REC_CHUNK), :]
            dcb = dcb + jnp.sum(du, axis=0, keepdims=True)
            for j in range(4):
                dcw[j] = dcw[j] + jnp.sum(du * upad_ref[pl.ds(PAD + t0 + j - 2, REC_CHUNK), :], axis=0, keepdims=True)
            dup_ref[pl.ds(t0, REC_CHUNK), :] = _conv_taps(dupad_ref, t0, cw, -1).astype(BF16)
        dcb_ref[...] = dcb
        dcw_ref[...] = jnp.concatenate(dcw, axis=0)

    full = pltpu.VMEM((T, REC_CB), F32)
    padded = pltpu.VMEM((T + 2 * PAD, REC_CB), F32)
    return pl.pallas_call(
        body, name="rec_bwd", grid=(N_CB,),
        in_specs=[up, yb] + [col] * 6 + [cw, cbias, wbd, wbd, vec2, vec2, vec2] + [_ANY] * len(after),
        out_specs=[col, col, cw, cbias, wbd, wbd, vec2, vec2, vec2],
        out_shape=[_sds((T, D), BF16), _sds((T, D), BF16), _sds((4, D), F32), _sds((1, D), F32),
                   _sds((2, N_CB, REC_CB, REC_CB), F32), _sds((2, N_CB, REC_CB, REC_CB), F32),
                   _sds((2, D), F32), _sds((2, D), F32), _sds((2, D), F32)],
        scratch_shapes=[padded, full, full, full, full, full, padded],
        compiler_params=_params(("parallel",)))(z, z, dg, *saved, conv_w, conv_b, wa, wi, ba, bi, lam, *after)


class _NoReducer:
    def begin(self, tag, grads):
        return ()

    def advance(self, tag, after):
        return ()


def _local_step(x, target, p, late=None, reducer=_NoReducer()):
    x = x.reshape(T, D)
    target = target.reshape(T, D)
    tb = _bias_pairs(p["rpb"])
    wa, wi = _block_diag(p["w_rg_a"]), _block_diag(p["w_rg_i"])

    h1 = _rms_fwd("rms1_fwd", x, p["ln1_g"], after=late[0] if late else ())
    if late:
        p = {**p, **late[1](h1)}
    rec_params = (p["conv_w"], p["conv_b"], wa, wi, p["b_rg_a"], p["b_rg_i"], p["lru_lambda"])
    (z,) = _mm_nn_cols("mm_z", h1, p["w_in"], F32, bias=p["b_in"])
    att = _attn_fwd(z, tb)
    g, rec_saved = _rec_fwd(z, *rec_params)
    if late:
        p = {**p, **late[2](g)}
    (y_att,) = _mm_nn_cols("mm_y_att", att, p["w_att_o"], F32)
    (y_rec,) = _mm_nn_rows("mm_y_rec", g, p["w_rec_o"], F32, tk=D)
    mixed = _merge_fwd(y_att, y_rec, z)

    def add_res(r, ex, outs):
        outs[0][...] = ex[0][...] + r

    res_spec = pl.BlockSpec((TM, D), lambda i, j, k: (i, 0))
    (x1,) = _mm_nn_rows("mm_x1", mixed, p["w_out"], F32, tk=D, extras=[x], extra_specs=[res_spec], epilogue=add_res)
    h2 = _rms_fwd("rms2_fwd", x1, p["ln2_g"])

    def relu2(r, ex, outs):
        rp = jnp.maximum(r, 0.0)
        outs[0][...] = (rp * rp).astype(BF16)

    (s,) = _mm_nn_cols("mm_ff1", h2, p["w_ff1"], BF16, epilogue=relu2)
    (x2,) = _mm_nn_rows("mm_x2", s, p["w_ff2"], F32, tk=D, extras=[x1], extra_specs=[res_spec], epilogue=add_res)
    loss, dx2, g_lnf = _loss_head(x2, target, p["lnf_g"])

    def relu2_bwd(r, ex, outs):
        outs[0][...] = (r * 2.0 * jnp.sqrt(ex[0][...].astype(F32))).astype(BF16)

    (df,) = _mm_nt_rows("mm_df", dx2, p["w_ff2"], BF16, tn=D, extras=[s],
                        extra_specs=[pl.BlockSpec((TM, D), lambda j, i, k: (i, j))], epilogue=relu2_bwd)
    (g_w_ff2,) = _mm_tn_rows("mm_g_ff2", s, dx2, tm=D)
    (g_w_ff1,) = _mm_tn_cols("mm_g_ff1", h2, df, D)
    tok = reducer.begin("ff", dict(w_ff2=g_w_ff2, w_ff1=g_w_ff1))
    (dh2,) = _mm_nt_cols("mm_dh2", df, p["w_ff1"], F32)
    dx1, g_ln2 = _rms_bwd("rms2_bwd", dh2, x1, p["ln2_g"], dx2, after=tok)

    (dmixed,) = _mm_nt_rows("mm_dmixed", dx1, p["w_out"], F32, tn=D)
    (g_w_out,) = _mm_tn_rows("mm_g_out", mixed, dx1, tm=D)
    dy_att, dy_rec, dg_att, dg_rec = _merge_bwd(dmixed, y_att, y_rec, z)
    (d_att,) = _mm_nt_cols("mm_d_att", dy_att, p["w_att_o"], BF16)
    (g_w_att_o,) = _mm_tn_cols("mm_g_att_o", att, dy_att, D // N_CHIPS)
    (d_g,) = _mm_nt_rows("mm_d_g", dy_rec, p["w_rec_o"], BF16, tn=D)
    (g_w_rec_o,) = _mm_tn_rows("mm_g_rec_o", g, dy_rec, tm=D)
    tok = reducer.advance("ff", g_w_rec_o) + reducer.begin("proj", dict(w_out=g_w_out, w_att_o=g_w_att_o, w_rec_o=g_w_rec_o))

    dq, dk, dv, ds_acc = _attn_bwd(z, tb, d_att, after=tok)
    g_rpb = _rpb_grad(ds_acc)
    tok = reducer.advance("proj", dq)
    d_up, d_yb, g_conv_w, g_conv_b, g_wa, g_wi, g_ba, g_bi, g_lam = _rec_bwd(z, d_g, rec_saved, *rec_params, after=tok)
    dz = jnp.concatenate([dq, dk, dv, d_up, d_yb, dg_att, dg_rec], axis=1)

    g_w_in, g_b_in = _mm_tn_cols("mm_g_in", h1, dz, D_IN // N_CHIPS, colsum=True)
    tok = reducer.begin("in", dict(w_in=g_w_in))
    (dh1,) = _mm_nt_cols("mm_dh1", dz, p["w_in"], F32, after=tok)
    tok = reducer.advance("in", dh1)
    grad_x, g_ln1 = _rms_bwd("rms1_bwd", dh1, x, p["ln1_g"], dx1, after=tok)

    grads = dict(ln1_g=g_ln1, w_in=g_w_in, b_in=g_b_in, rpb=g_rpb, w_att_o=g_w_att_o, conv_w=g_conv_w,
                 conv_b=g_conv_b, w_rg_a=_block_diag_grad(g_wa), b_rg_a=g_ba, w_rg_i=_block_diag_grad(g_wi),
                 b_rg_i=g_bi, lru_lambda=g_lam, w_rec_o=g_w_rec_o, w_out=g_w_out, ln2_g=g_ln2,
                 w_ff1=g_w_ff1, w_ff2=g_w_ff2, lnf_g=g_lnf)
    return loss, grad_x.reshape(1, T, D), grads


_ANY = pl.BlockSpec(memory_space=pl.ANY)
N_PEERS = N_CHIPS - 1


def _place():
    x, y, c = lax.axis_index("x"), lax.axis_index("y"), lax.axis_index("c")
    peers = [(1 - x, y), (x, 1 - y), (1 - x, 1 - y)]
    return x, y, c, 2 * x + y, peers


def _remote(src, dst, send_sem, recv_sem, dev):
    return pltpu.make_async_remote_copy(src_ref=src, dst_ref=dst, send_sem=send_sem, recv_sem=recv_sem,
                                        device_id=dev, device_id_type=MESH)


def _prefetch_call(body, name, ids, grid, in_specs, out_specs, out_shape, args):
    spec = pltpu.PrefetchScalarGridSpec(num_scalar_prefetch=1, grid=grid, in_specs=in_specs, out_specs=out_specs)
    return pl.pallas_call(body, name=name, grid_spec=spec, out_shape=out_shape,
                          compiler_params=_params(("parallel",) * len(grid)))(ids, *args)


def _cast_bf16(name, w, chip_id, after=()):
    rows, cols = w.shape
    rb = min(rows, 256)

    def body(ids_ref, w_ref, *rest):
        rest[-1][...] = w_ref[...].astype(BF16)

    return _prefetch_call(body, name, chip_id, (rows // rb,),
                          [pl.BlockSpec((rb, cols), lambda i, ids: (i, 0))] + [_ANY] * len(after),
                          pl.BlockSpec((None, rb, cols), lambda i, ids: (ids[0], i, 0)),
                          _sds((N_CHIPS, rows, cols), BF16), (w, *after))


def _dma_sems(*counts):
    return [pltpu.SemaphoreType.DMA((k,)) for k in counts]


_HBM = pl.BlockSpec(memory_space=pltpu.HBM)
_SEM = pl.BlockSpec(memory_space=pltpu.SEMAPHORE)
_SPLIT_COPY = pltpu.CompilerParams(has_side_effects=pltpu.SideEffectType.DATAFLOW_SIDE_EFFECTING)


def _hbm(arrays):
    return [pltpu.with_memory_space_constraint(a, pltpu.HBM) for a in arrays]


def _hbm_like(arrays):
    return [pltpu.HBM(a.shape, a.dtype) for a in arrays]


def _halves(buf, c):
    half = buf.shape[1] // 2
    return pl.ds(c * half, half), pl.ds((1 - c) * half, half)


def _gather_start(name, slots):
    n = len(slots)
    nk = n * N_PEERS

    def body(*refs):
        bufs = refs[n:2 * n]
        send_sems, recv_sems, token = refs[2 * n:]
        x, y, c, chip, peers = _place()
        for t in range(n):
            mine, _ = _halves(bufs[t], c)
            for r, (px, py) in enumerate(peers):
                k = t * N_PEERS + r
                own = bufs[t].at[chip, mine]
                _remote(own, own, send_sems.at[k], recv_sems.at[k], (px, py, c)).start()
        token[...] = jnp.zeros_like(token)

    res = pl.pallas_call(
        body, name=name, in_specs=[_HBM] * n, out_specs=[_HBM] * n + [_SEM, _SEM, pl.BlockSpec(memory_space=pltpu.VMEM)],
        out_shape=_hbm_like(slots) + [pltpu.SemaphoreType.DMA((nk,)), pltpu.SemaphoreType.DMA((nk,)),
                                      _sds((8, 128), F32)],
        input_output_aliases={t: t for t in range(n)}, compiler_params=_SPLIT_COPY)(*_hbm(slots))
    return res[:n], (res[n], res[n + 1]), res[n + 2]


def _gather_wait(name, bufs, sems, after):
    n = len(bufs)

    def body(*refs):
        ins = refs[:n]
        send_sems, recv_sems = refs[n], refs[n + 1]
        x, y, c, chip, peers = _place()
        for t in range(n):
            mine, _ = _halves(ins[t], c)
            for r, (px, py) in enumerate(peers):
                k = t * N_PEERS + r
                cp = _remote(ins[t].at[chip, mine], ins[t].at[2 * px + py, mine], send_sems.at[k], recv_sems.at[k],
                             (px, py, c))
                cp.wait_send()
                cp.wait_recv()

    return pl.pallas_call(
        body, name=name, in_specs=[_HBM] * n + [_SEM, _SEM, _ANY], out_specs=[_HBM] * n, out_shape=_hbm_like(bufs),
        input_output_aliases={t: t for t in range(n)}, compiler_params=_SPLIT_COPY)(*bufs, *sems, after)


def _gather_forward(name, bufs):
    n = len(bufs)
    nk = n * N_PEERS

    def body(*refs):
        outs = refs[n:2 * n]
        send_sems, recv_sems = refs[2 * n:]
        x, y, c, chip, peers = _place()
        sibling = (x, y, 1 - c)
        sends = []
        for t in range(n):
            mine, _ = _halves(outs[t], c)
            for r, (px, py) in enumerate(peers):
                k = t * N_PEERS + r
                landed = outs[t].at[2 * px + py, mine]
                sends.append(_remote(landed, landed, send_sems.at[k], recv_sems.at[k], sibling))
                sends[-1].start()
        for t in range(n):
            _, theirs = _halves(outs[t], c)
            for r, (px, py) in enumerate(peers):
                k = t * N_PEERS + r
                landed = outs[t].at[2 * px + py, theirs]
                _remote(landed, landed, send_sems.at[k], recv_sems.at[k], sibling).wait_recv()
        for cp in sends:
            cp.wait_send()

    return pl.pallas_call(
        body, name=name, in_specs=[_ANY] * n, out_specs=[_ANY] * n, out_shape=[_sds(b.shape, b.dtype) for b in bufs],
        input_output_aliases={t: t for t in range(n)}, scratch_shapes=_dma_sems(nk, nk))(*bufs)


def _pair_copies(n, srcs, lands, send_sems, recv_sems):
    x, y, c, _, _ = _place()
    sibling = (x, y, 1 - c)
    copies = []
    for t in range(n):
        half = srcs[t].shape[1] // 2
        for j in range(N_CHIPS):
            k = t * N_CHIPS + j
            copies.append(_remote(srcs[t].at[j, pl.ds((1 - c) * half, half)], lands[t].at[j],
                                  send_sems.at[k], recv_sems.at[k], sibling))
    for t in range(n, len(srcs)):
        k = n * N_CHIPS + t - n
        copies.append(_remote(srcs[t], lands[t], send_sems.at[k], recv_sems.at[k], sibling))
    return copies


def _pair_start(name, grads, wholes=()):
    n = len(grads)
    srcs = list(grads) + list(wholes)
    m = len(srcs)
    lands = [pltpu.HBM((N_CHIPS, g.shape[1] // 2, g.shape[2]), F32) for g in grads] + _hbm_like(wholes)
    ns = n * N_CHIPS + len(wholes)

    def body(*refs):
        src_refs, land_refs = refs[m:2 * m], refs[2 * m:3 * m]
        send_sems, recv_sems, token = refs[3 * m:]
        for cp in _pair_copies(n, src_refs, land_refs, send_sems, recv_sems):
            cp.start()
        token[...] = jnp.zeros_like(token)

    res = pl.pallas_call(
        body, name=name, in_specs=[_HBM] * m,
        out_specs=[_HBM] * (2 * m) + [_SEM, _SEM, pl.BlockSpec(memory_space=pltpu.VMEM)],
        out_shape=_hbm_like(srcs) + lands + [pltpu.SemaphoreType.DMA((ns,)), pltpu.SemaphoreType.DMA((ns,)),
                                             _sds((8, 128), F32)],
        input_output_aliases={t: t for t in range(m)}, compiler_params=_SPLIT_COPY)(*_hbm(srcs))
    return (res[:m], res[m:2 * m], (res[2 * m], res[2 * m + 1])), res[2 * m + 2]


def _pair_wait(name, flight, n, after):
    srcs, lands, sems = flight
    m = len(srcs)

    def body(*refs):
        for cp in _pair_copies(n, refs[:m], refs[m:2 * m], refs[2 * m], refs[2 * m + 1]):
            cp.wait_send()
            cp.wait_recv()

    res = pl.pallas_call(
        body, name=name, in_specs=[_HBM] * (2 * m) + [_SEM, _SEM, _ANY], out_specs=[_HBM] * (2 * m),
        out_shape=_hbm_like(srcs) + _hbm_like(lands), input_output_aliases={t: t for t in range(2 * m)},
        compiler_params=_SPLIT_COPY)(*srcs, *lands, *sems, after)
    return res[:m], res[m:]


def _chip_copies(srcs, lands, small_src, small_land, send_sems, recv_sems):
    x, y, c, chip, peers = _place()
    n = len(srcs)
    copies = []
    for r, (px, py) in enumerate(peers):
        for t in range(n):
            k = t * N_PEERS + r
            copies.append(_remote(srcs[t].at[2 * px + py], lands[t].at[r], send_sems.at[k], recv_sems.at[k], (px, py, c)))
        if small_src is not None:
            k = n * N_PEERS + r
            half_s = small_src.shape[0] // 2
            copies.append(_remote(small_src.at[pl.ds(c * half_s, half_s)], small_land.at[r],
                                  send_sems.at[k], recv_sems.at[k], (px, py, c)))
    return copies


def _chip_start(name, sums_bf16, small=None):
    n = len(sums_bf16)
    srcs = list(sums_bf16) + ([small] if small is not None else [])
    m = len(srcs)
    lands = [pltpu.HBM((N_PEERS,) + s.shape[1:], BF16) for s in sums_bf16]
    if small is not None:
        lands.append(pltpu.HBM((N_PEERS, small.shape[0] // 2, 128), F32))
    nk = m * N_PEERS

    def body(*refs):
        src_refs, land_refs = refs[m:2 * m], refs[2 * m:3 * m]
        send_sems, recv_sems, token = refs[3 * m:]
        small_src, small_land = (src_refs[n], land_refs[n]) if small is not None else (None, None)
        for cp in _chip_copies(src_refs[:n], land_refs[:n], small_src, small_land, send_sems, recv_sems):
            cp.start()
        token[...] = jnp.zeros_like(token)

    res = pl.pallas_call(
        body, name=name, in_specs=[_HBM] * m,
        out_specs=[_HBM] * (2 * m) + [_SEM, _SEM, pl.BlockSpec(memory_space=pltpu.VMEM)],
        out_shape=_hbm_like(srcs) + lands + [pltpu.SemaphoreType.DMA((nk,)), pltpu.SemaphoreType.DMA((nk,)),
                                             _sds((8, 128), F32)],
        input_output_aliases={t: t for t in range(m)}, compiler_params=_SPLIT_COPY)(*_hbm(srcs))
    return (res[:m], res[m:2 * m], (res[2 * m], res[2 * m + 1])), res[2 * m + 2]


def _chip_wait(name, flight, with_small, after):
    srcs, lands, sems = flight
    m = len(srcs)
    n = m - 1 if with_small else m

    def body(*refs):
        src_refs, land_refs = refs[:m], refs[m:2 * m]
        send_sems, recv_sems = refs[2 * m], refs[2 * m + 1]
        small_src, small_land = (src_refs[n], land_refs[n]) if with_small else (None, None)
        for cp in _chip_copies(src_refs[:n], land_refs[:n], small_src, small_land, send_sems, recv_sems):
            cp.wait_send()
            cp.wait_recv()

    res = pl.pallas_call(
        body, name=name, in_specs=[_HBM] * (2 * m) + [_SEM, _SEM, _ANY], out_specs=[_HBM] * (2 * m),
        out_shape=_hbm_like(srcs) + _hbm_like(lands), input_output_aliases={t: t for t in range(2 * m)},
        compiler_params=_SPLIT_COPY)(*srcs, *lands, *sems, after)
    return res[:m], res[m:]


def _half_swap(name, bufs):
    n = len(bufs)

    def body(*refs):
        outs = refs[n:2 * n]
        send_sem, recv_sem = refs[2 * n:]
        x, y, c, _, _ = _place()
        sibling = (x, y, 1 - c)
        copies = []
        for t in range(n):
            h = outs[t].shape[0] // 2
            mine = outs[t].at[pl.ds(c * h, h)]
            copies.append(_remote(mine, mine, send_sem.at[t], recv_sem.at[t], sibling))
            copies[-1].start()
        for t in range(n):
            h = outs[t].shape[0] // 2
            theirs = outs[t].at[pl.ds((1 - c) * h, h)]
            _remote(theirs, theirs, send_sem.at[t], recv_sem.at[t], sibling).wait_recv()
        for cp in copies:
            cp.wait_send()

    return pl.pallas_call(
        body, name=name, in_specs=[_ANY] * n, out_specs=[_ANY] * n,
        out_shape=[_sds(b.shape, b.dtype) for b in bufs], input_output_aliases={t: t for t in range(n)},
        scratch_shapes=_dma_sems(n, n))(*bufs)


def _pair_sum(name, grad, got, ids):
    _, rows, cols = got.shape
    rb = min(rows, 256)
    nb = rows // rb
    blk = pl.BlockSpec((None, rb, cols), lambda j, i, ids: (j, i, 0))
    mine = pl.BlockSpec((None, rb, cols), lambda j, i, ids: (j, ids[1] * nb + i, 0))

    def body(ids_ref, a_ref, b_ref, s_ref, sb_ref):
        s = a_ref[...] + b_ref[...]
        s_ref[...] = s
        sb_ref[...] = s.astype(BF16)

    return _prefetch_call(body, name, ids, (N_CHIPS, nb), [mine, blk], [blk, blk],
                          [_sds(got.shape, F32), _sds(got.shape, BF16)], (grad, got))


def _chip_sum(name, sums, got, ids):
    _, rows, cols = sums.shape
    rb = min(rows, 256)
    nb = rows // rb
    own = pl.BlockSpec((None, rb, cols), lambda i, ids: (ids[0], i, 0))
    blk3 = pl.BlockSpec((N_PEERS, rb, cols), lambda i, ids: (0, i, 0))
    out = pl.BlockSpec((rb, cols), lambda i, ids: (ids[1] * nb + i, 0))

    def body(ids_ref, a_ref, b_ref, o_ref):
        o_ref[...] = ((a_ref[...] + b_ref[0].astype(F32)) + b_ref[1].astype(F32)) + b_ref[2].astype(F32)

    return _prefetch_call(body, name, ids, (nb,), [own, blk3], out, _sds((2 * rows, cols), F32), (sums, got))


SMALL_RB = 280


def _small_pair_sum(own, got):
    blk = pl.BlockSpec((SMALL_RB, 128), lambda i: (i, 0))

    def body(a_ref, b_ref, o_ref):
        o_ref[...] = a_ref[...] + b_ref[...]

    return pl.pallas_call(body, name="small_pair_sum", grid=(own.shape[0] // SMALL_RB,), in_specs=[blk, blk],
                          out_specs=blk, out_shape=_sds(own.shape, F32),
                          compiler_params=_params(("parallel",)))(own, got)


def _small_chip_sum(pair, got, ids):
    nb = pair.shape[0] // 2 // SMALL_RB
    half = pl.BlockSpec((SMALL_RB, 128), lambda i, ids: (ids[1] * nb + i, 0))
    blk3 = pl.BlockSpec((N_PEERS, SMALL_RB, 128), lambda i, ids: (0, i, 0))

    def body(ids_ref, a_ref, b_ref, o_ref):
        o_ref[...] = (a_ref[...] + b_ref[1]) + (b_ref[0] + b_ref[2])

    return _prefetch_call(body, "small_chip_sum", ids, (nb,), [half, blk3], half, _sds(pair.shape, F32), (pair, got))


def _adamw_math(w, g, m, v):
    m = ADAM_B1 * m + (1.0 - ADAM_B1) * g
    v = ADAM_B2 * v + (1.0 - ADAM_B2) * (g * g)
    m_hat = m / (1.0 - ADAM_B1 ** ADAM_STEP)
    v_hat = v / (1.0 - ADAM_B2 ** ADAM_STEP)
    delta = -ADAM_LR * (m_hat / (jnp.sqrt(v_hat) + ADAM_EPS) + ADAM_WD * w)
    return delta, m, v


def _adamw(name, w, g, m, v, rb=None):
    rows, cols = w.shape
    rb = rows if rb is None else rb
    blk = pl.BlockSpec((rb, cols), lambda i: (i, 0))

    def body(w_ref, g_ref, m_ref, v_ref, d_ref, nm_ref, nv_ref):
        d, nm, nv = _adamw_math(w_ref[...], g_ref[...], m_ref[...], v_ref[...])
        d_ref[...] = d
        nm_ref[...] = nm
        nv_ref[...] = nv

    return pl.pallas_call(body, name=name, grid=(rows // rb,), in_specs=[blk] * 4, out_specs=[blk] * 3,
                          out_shape=[_sds(w.shape, F32)] * 3, compiler_params=_params(("parallel",)))(w, g, m, v)


BIG = ("w_in", "w_att_o", "w_rec_o", "w_out", "w_ff1", "w_ff2")
SHARDED_VECS = ("conv_w", "b_rg_a", "b_rg_i", "lru_lambda")
SMALL = ("ln1_g", "b_in", "rpb", "conv_w", "conv_b", "w_rg_a", "b_rg_a", "w_rg_i", "b_rg_i", "lru_lambda",
         "ln2_g", "lnf_g")
SMALL_ROWS = 2240
ORDER = ("ln1_g", "w_in", "b_in", "rpb", "w_att_o", "conv_w", "conv_b", "w_rg_a", "b_rg_a", "w_rg_i", "b_rg_i",
         "lru_lambda", "w_rec_o", "w_out", "ln2_g", "w_ff1", "w_ff2", "lnf_g")


def _pack_small(grads, loss):
    parts, sizes = [], {}
    for n in SMALL:
        flat = grads[n].reshape(-1)
        pad = (-flat.shape[0]) % 128
        sizes[n] = (flat.shape[0], flat.shape[0] + pad)
        parts.append(jnp.pad(flat, (0, pad)))
    total = sum(s[1] for s in sizes.values())
    parts.append(jnp.pad(loss.reshape(1), (0, SMALL_ROWS * 128 - total - 1)))
    return jnp.concatenate(parts).reshape(SMALL_ROWS, 128), sizes


def _unpack_small(buf, sizes, shapes):
    flat = buf.reshape(-1)
    out, pos = {}, 0
    for n in SMALL:
        size, padded = sizes[n]
        out[n] = flat[pos:pos + size].reshape(shapes[n])
        pos += padded
    return out, flat[pos]


def _gather_weights(w, chip):
    chip_id = chip.astype(jnp.int32).reshape(1)
    vec_rows = [w[n][0] for n in SHARDED_VECS]
    vec_shard = jnp.concatenate(vec_rows + [jnp.zeros((16 - 10, D // N_CHIPS), F32)], axis=0)
    vec_slots = lax.dynamic_update_slice(jnp.zeros((N_CHIPS, 16, D // N_CHIPS), F32), vec_shard[None], (chip, 0, 0))
    bufs_a, sems_a, token_a = _gather_start("gather_start_first", [_cast_bf16("cast_w_in", w["w_in"][0], chip_id), vec_slots])
    rest_names = BIG[1:]
    bufs_b, sems_b, token_b = _gather_start(
        "gather_start_rest", [_cast_bf16("cast_" + n, w[n][0], chip_id, after=(token_a,)) for n in rest_names])

    def first(after):
        w_in_full, vec_full = _gather_forward("gather_forward_first", _gather_wait("gather_wait_first", bufs_a, sems_a, after))
        vecs = vec_full.transpose(1, 0, 2).reshape(16, D)
        return dict(w_in=w_in_full, conv_w=vecs[0:4], b_rg_a=vecs[4:6], b_rg_i=vecs[6:8], lru_lambda=vecs[8:10])

    def rest(after):
        full = dict(zip(rest_names, _gather_forward("gather_forward_rest",
                                                    _gather_wait("gather_wait_rest", bufs_b, sems_b, after))))
        return dict(w_att_o=full["w_att_o"], w_ff1=full["w_ff1"], w_rec_o=full["w_rec_o"].reshape(D, D),
                    w_out=full["w_out"].reshape(D, D), w_ff2=full["w_ff2"].reshape(D_FF, D))

    p = dict(ln1_g=w["ln1_g"], b_in=w["b_in"], rpb=w["rpb"][0], conv_b=w["conv_b"], w_rg_a=w["w_rg_a"][0],
             w_rg_i=w["w_rg_i"][0], ln2_g=w["ln2_g"], lnf_g=w["lnf_g"].reshape(1, D))
    return p, ((token_b,), first, rest)


class _Reducer:
    def __init__(self, ids):
        self.ids = ids
        self.groups = {}

    def begin(self, tag, grads, small=None):
        names = list(grads)
        big = [grads[n].reshape(N_CHIPS, -1, grads[n].shape[-1]) for n in names]
        flight, token = _pair_start("pair_start_" + tag, big, [] if small is None else [small])
        self.groups[tag] = dict(names=names, pair=flight, small=small is not None)
        return (token,)

    def advance(self, tag, after):
        grp = self.groups[tag]
        n = len(grp["names"])
        mine, got = _pair_wait("pair_wait_" + tag, grp["pair"], n, after)
        sums = [_pair_sum("pair_sum_" + name, a, b, self.ids) for name, a, b in zip(grp["names"], mine, got)]
        small_sum = _small_pair_sum(mine[n], got[n]) if grp["small"] else None
        grp["chip"], token = _chip_start("chip_start_" + tag, [s[1] for s in sums], small_sum)
        grp["sums"] = [s[0] for s in sums]
        return (token,)

    def finish(self, tag, after):
        grp = self.groups[tag]
        srcs, lands = _chip_wait("chip_wait_" + tag, grp["chip"], grp["small"], after)
        halves = [_chip_sum("chip_sum_" + name, s, b, self.ids) for name, s, b in zip(grp["names"], grp["sums"], lands)]
        if grp["small"]:
            halves.append(_small_chip_sum(srcs[-1], lands[-1], self.ids))
        return _half_swap("half_swap_" + tag, halves)


def kernel(x, ln1_g, w_in, b_in, rpb, w_att_o, conv_w, conv_b, w_rg_a, b_rg_a, w_rg_i, b_rg_i, lru_lambda, w_rec_o, w_out, ln2_g, w_ff1, w_ff2, lnf_g, loss_target, m_ln1_g, m_w_in, m_b_in, m_rpb, m_w_att_o, m_conv_w, m_conv_b, m_w_rg_a, m_b_rg_a, m_w_rg_i, m_b_rg_i, m_lru_lambda, m_w_rec_o, m_w_out, m_ln2_g, m_w_ff1, m_w_ff2, m_lnf_g, v_ln1_g, v_w_in, v_b_in, v_rpb, v_w_att_o, v_conv_w, v_conv_b, v_w_rg_a, v_b_rg_a, v_w_rg_i, v_b_rg_i, v_lru_lambda, v_w_rec_o, v_w_out, v_ln2_g, v_w_ff1, v_w_ff2, v_lnf_g):
    w = dict(ln1_g=ln1_g, w_in=w_in, b_in=b_in, rpb=rpb, w_att_o=w_att_o, conv_w=conv_w, conv_b=conv_b,
             w_rg_a=w_rg_a, b_rg_a=b_rg_a, w_rg_i=w_rg_i, b_rg_i=b_rg_i, lru_lambda=lru_lambda, w_rec_o=w_rec_o,
             w_out=w_out, ln2_g=ln2_g, w_ff1=w_ff1, w_ff2=w_ff2, lnf_g=lnf_g)
    m = dict(ln1_g=m_ln1_g, w_in=m_w_in, b_in=m_b_in, rpb=m_rpb, w_att_o=m_w_att_o, conv_w=m_conv_w,
             conv_b=m_conv_b, w_rg_a=m_w_rg_a, b_rg_a=m_b_rg_a, w_rg_i=m_w_rg_i, b_rg_i=m_b_rg_i,
             lru_lambda=m_lru_lambda, w_rec_o=m_w_rec_o, w_out=m_w_out, ln2_g=m_ln2_g, w_ff1=m_w_ff1,
             w_ff2=m_w_ff2, lnf_g=m_lnf_g)
    v = dict(ln1_g=v_ln1_g, w_in=v_w_in, b_in=v_b_in, rpb=v_rpb, w_att_o=v_w_att_o, conv_w=v_conv_w,
             conv_b=v_conv_b, w_rg_a=v_w_rg_a, b_rg_a=v_b_rg_a, w_rg_i=v_w_rg_i, b_rg_i=v_b_rg_i,
             lru_lambda=v_lru_lambda, w_rec_o=v_w_rec_o, w_out=v_w_out, ln2_g=v_ln2_g, w_ff1=v_w_ff1,
             w_ff2=v_w_ff2, lnf_g=v_lnf_g)
    chip = 2 * lax.axis_index("x") + lax.axis_index("y")
    ids = jnp.stack([chip, lax.axis_index("c")]).astype(jnp.int32)

    out_grad, out_delta, out_m, out_v = {}, {}, {}, {}

    def update(n, gn):
        shape = w[n].shape
        two_d, rb = (gn.shape, 256) if n in BIG else ((int(np.prod(shape[:-1])), shape[-1]), None)
        gn = gn.reshape(two_d)
        d, nm, nv = _adamw("adamw_" + n, w[n].reshape(two_d), gn, m[n].reshape(two_d), v[n].reshape(two_d), rb)
        out_grad[n], out_delta[n], out_m[n], out_v[n] = (gn.reshape(shape), d.reshape(shape), nm.reshape(shape),
                                                         nv.reshape(shape))
        return d

    reducer = _Reducer(ids)
    p, late = _gather_weights(w, chip)
    loss, grad_x, g = _local_step(x, loss_target, p, late, reducer)
    small, sizes = _pack_small(g, loss)
    after = reducer.begin("small", {}, small)[0]
    for tag in ("ff", "proj", "in"):
        for n, red in zip(reducer.groups[tag]["names"], reducer.finish(tag, after)):
            after = update(n, red)
        if tag == "ff":
            after = reducer.advance("small", after)[0]
    (small_red,) = reducer.finish("small", after)
    gsmall, loss = _unpack_small(small_red, sizes, {n: g[n].shape for n in SMALL})
    for n in SMALL:
        gn = gsmall[n]
        if n in SHARDED_VECS:
            gn = lax.dynamic_slice_in_dim(gn, chip * (D // N_CHIPS), D // N_CHIPS, axis=1)
        update(n, gn)
    return (loss, grad_x, *[out_grad[n] for n in ORDER], *[out_delta[n] for n in ORDER],
            *[out_m[n] for n in ORDER], *[out_v[n] for n in ORDER])
```

```python
import functools

import numpy as np
import jax
import jax.numpy as jnp
from jax import lax
from jax.experimental import pallas as pl
from jax.experimental.pallas import tpu as pltpu

F32 = jnp.float32
BF16 = jnp.bfloat16

T = 2048
D = 1024
D_ATT = 512
D_IN = 5632
D_FF = 4096
N_HEADS = 8
HEAD_DIM = 64
GRID_W = 64
N_ROWS = T // GRID_W
WIN_H = 8
WIN_W = 16
KEYS = WIN_H * GRID_W
N_CHIPS = 4
EPS = 1e-6
LRU_C = 8.0
SCALE = HEAD_DIM ** -0.5
REC_CB = 256
REC_CHUNK = 256
PAD = 8

ADAM_LR = 0.001
ADAM_B1 = 0.9
ADAM_B2 = 0.999
ADAM_EPS = 1e-08
ADAM_WD = 0.01
ADAM_STEP = 10

VMEM_LIMIT = 56 * 1024 * 1024

NN = (((1,), (0,)), ((), ()))
NT = (((1,), (1,)), ((), ()))
TN = (((0,), (0,)), ((), ()))
MESH = pl.DeviceIdType.MESH


def _params(sem=None):
    return pltpu.CompilerParams(dimension_semantics=sem, vmem_limit_bytes=VMEM_LIMIT)


def _dot(a, b, dims):
    return lax.dot_general(a, b, dims, preferred_element_type=F32)


def _sigmoid(x):
    return 0.5 * jnp.tanh(0.5 * x) + 0.5


def _matmul(name, a, b, *, dims, grid, a_spec, b_spec, out_shapes, out_specs, acc_shape,
            extras=(), extra_specs=(), epilogue=None, colsum_spec=None, colsum_shape=None, after=(),
            semantics=("parallel", "parallel", "arbitrary"), epilogue_takes_first=False):
    nk = grid[2]
    n_extra = len(extras)
    n_out = len(out_shapes)
    with_colsum = colsum_spec is not None

    def body(a_ref, b_ref, *rest):
        ex = rest[:n_extra]
        rest = rest[:n_extra] + rest[n_extra + len(after):]
        outs = rest[n_extra:n_extra + n_out]
        pos = n_extra + n_out
        cs_out = rest[pos] if with_colsum else None
        pos += 1 if with_colsum else 0
        acc = rest[pos]
        cs_acc = rest[pos + 1] if with_colsum else None
        k = pl.program_id(2)
        first_tile = pl.program_id(0) == 0

        @pl.when(k == 0)
        def _():
            acc[...] = jnp.zeros_like(acc)
            if with_colsum:
                cs_acc[...] = jnp.zeros_like(cs_acc)

        bv = b_ref[...]
        acc[...] += _dot(a_ref[...].astype(BF16), bv.astype(BF16), dims)
        if with_colsum:
            cs_acc[...] += jnp.sum(bv.astype(F32), axis=0, keepdims=True)

        @pl.when(k == nk - 1)
        def _():
            r = acc[...]
            if epilogue is None:
                outs[0][...] = r.astype(outs[0].dtype)
            elif epilogue_takes_first:
                epilogue(r, ex, outs, first_tile)
            else:
                epilogue(r, ex, outs)
            if with_colsum:
                cs_out[...] = cs_acc[...]

    shapes = list(out_shapes)
    specs = list(out_specs)
    scratch = [pltpu.VMEM(acc_shape, F32)]
    if with_colsum:
        shapes.append(colsum_shape)
        specs.append(colsum_spec)
        scratch.append(pltpu.VMEM((1, acc_shape[1]), F32))
    res = pl.pallas_call(
        body, name=name, grid=grid,
        in_specs=[a_spec, b_spec, *extra_specs] + [_ANY] * len(after),
        out_specs=specs, out_shape=shapes, scratch_shapes=scratch,
        compiler_params=_params(semantics),
    )(a, b, *extras, *after)
    return res


def _sds(shape, dtype):
    return jax.ShapeDtypeStruct(shape, dtype)


TM = 1024
NI = T // TM


def _mm_nn_cols(name, a, wg, out_dtype, *, bias=None, extras=(), extra_specs=(), epilogue=None,
                out_shapes=None, out_specs=None):
    k_dim, n4 = wg.shape[1], wg.shape[2]
    ex, exs = list(extras), list(extra_specs)
    if bias is not None:
        ex = [bias] + ex
        exs = [pl.BlockSpec((1, n4), lambda j, i, k: (0, j))] + exs
        user_ep = epilogue

        def epilogue(r, e, outs):
            r = r + e[0][...]
            if user_ep is None:
                outs[0][...] = r.astype(outs[0].dtype)
            else:
                user_ep(r, e[1:], outs)
    if out_shapes is None:
        out_shapes = [_sds((T, N_CHIPS * n4), out_dtype)]
        out_specs = [pl.BlockSpec((TM, n4), lambda j, i, k: (i, j))]
    return _matmul(
        name, a, wg, dims=NN, grid=(N_CHIPS, NI, 1),
        a_spec=pl.BlockSpec((TM, k_dim), lambda j, i, k: (i, 0)),
        b_spec=pl.BlockSpec((None, k_dim, n4), lambda j, i, k: (j, 0, 0)),
        out_shapes=out_shapes, out_specs=out_specs, acc_shape=(TM, n4),
        extras=ex, extra_specs=exs, epilogue=epilogue)


def _mm_nn_rows(name, a, w, out_dtype, *, tk, extras=(), extra_specs=(), epilogue=None):
    k_dim, n = w.shape
    return _matmul(
        name, a, w, dims=NN, grid=(NI, 1, k_dim // tk),
        a_spec=pl.BlockSpec((TM, tk), lambda i, j, k: (i, k)),
        b_spec=pl.BlockSpec((tk, n), lambda i, j, k: (k, 0)),
        out_shapes=[_sds((T, n), out_dtype)],
        out_specs=[pl.BlockSpec((TM, n), lambda i, j, k: (i, 0))], acc_shape=(TM, n),
        extras=extras, extra_specs=extra_specs, epilogue=epilogue)


def _mm_nt_cols(name, a, wg, out_dtype, after=()):
    k_dim, n4 = wg.shape[1], wg.shape[2]
    return _matmul(
        name, a, wg, dims=NT, grid=(NI, 1, N_CHIPS),
        a_spec=pl.BlockSpec((TM, n4), lambda i, j, k: (i, k)),
        b_spec=pl.BlockSpec((None, k_dim, n4), lambda i, j, k: (k, 0, 0)),
        out_shapes=[_sds((T, k_dim), out_dtype)],
        out_specs=[pl.BlockSpec((TM, k_dim), lambda i, j, k: (i, 0))], acc_shape=(TM, k_dim), after=after)


def _mm_nt_cols_rms_bwd(name, a, wg, x, g, dres, after=(), bf16_copy=False):
    n4 = wg.shape[2]
    row = pl.BlockSpec((TM, D), lambda i, j, k: (i, 0))
    vec = pl.BlockSpec((1, D), lambda i, j, k: (0, 0))

    def epilogue(dhv, ex, outs, first):
        x_ref, g_ref, dres_ref = ex
        dx_ref, dg_ref = outs[0], outs[-1]
        xv = x_ref[...]
        rstd = lax.rsqrt(jnp.mean(xv * xv, axis=-1, keepdims=True) + EPS)
        xhat = xv * rstd
        dy = dhv * g_ref[...]
        dx = dres_ref[...] + rstd * (dy - xhat * jnp.mean(dy * xhat, axis=-1, keepdims=True))
        dx_ref[...] = dx
        if bf16_copy:
            outs[1][...] = dx.astype(BF16)
        part = jnp.sum(dhv * xhat, axis=0, keepdims=True)

        @pl.when(first)
        def _():
            dg_ref[...] = part

        @pl.when(jnp.logical_not(first))
        def _():
            dg_ref[...] += part

    return _matmul(
        name, a, wg, dims=NT, grid=(NI, 1, N_CHIPS),
        a_spec=pl.BlockSpec((TM, n4), lambda i, j, k: (i, k)),
        b_spec=pl.BlockSpec((None, D, n4), lambda i, j, k: (k, 0, 0)),
        out_shapes=[_sds((T, D), F32)] + [_sds((T, D), BF16)] * bf16_copy + [_sds((1, D), F32)],
        out_specs=[row] + [row] * bf16_copy + [vec], acc_shape=(TM, D),
        extras=[x, g, dres], extra_specs=[row, vec, row], epilogue=epilogue, after=after,
        semantics=("arbitrary", "arbitrary", "arbitrary"), epilogue_takes_first=True)


def _mm_nt_rows(name, a, w, out_dtype, *, tn, extras=(), extra_specs=(), epilogue=None):
    k_dim, n = w.shape
    return _matmul(
        name, a, w, dims=NT, grid=(k_dim // tn, NI, 1),
        a_spec=pl.BlockSpec((TM, n), lambda j, i, k: (i, 0)),
        b_spec=pl.BlockSpec((tn, n), lambda j, i, k: (j, 0)),
        out_shapes=[_sds((T, k_dim), out_dtype)],
        out_specs=[pl.BlockSpec((TM, tn), lambda j, i, k: (i, j))], acc_shape=(TM, tn),
        extras=extras, extra_specs=extra_specs, epilogue=epilogue)


def _mm_tn_cols(name, a, g, n4, *, colsum=False):
    k_dim = a.shape[1]
    kw = {}
    if colsum:
        kw = dict(colsum_spec=pl.BlockSpec((1, n4), lambda j, i, k: (0, j)),
                  colsum_shape=_sds((1, N_CHIPS * n4), F32))
    return _matmul(
        name, a, g, dims=TN, grid=(N_CHIPS, 1, NI),
        a_spec=pl.BlockSpec((TM, k_dim), lambda j, i, k: (k, 0)),
        b_spec=pl.BlockSpec((TM, n4), lambda j, i, k: (k, j)),
        out_shapes=[_sds((N_CHIPS, k_dim, n4), F32)],
        out_specs=[pl.BlockSpec((None, k_dim, n4), lambda j, i, k: (j, 0, 0))],
        acc_shape=(k_dim, n4), **kw)


def _mm_tn_rows(name, a, g, *, tm):
    k_dim, n = a.shape[1], g.shape[1]
    return _matmul(
        name, a, g, dims=TN, grid=(k_dim // tm, 1, NI),
        a_spec=pl.BlockSpec((TM, tm), lambda j, i, k: (k, j)),
        b_spec=pl.BlockSpec((TM, n), lambda j, i, k: (k, 0)),
        out_shapes=[_sds((k_dim, n), F32)],
        out_specs=[pl.BlockSpec((tm, n), lambda j, i, k: (j, 0))], acc_shape=(tm, n))


TE = 256
NE = T // TE
_ROW = pl.BlockSpec((TE, D), lambda i: (i, 0))
_VEC = pl.BlockSpec((1, D), lambda i: (0, 0))


def _rms_fwd(name, x, g, after=()):
    def body(x_ref, g_ref, *rest):
        h_ref = rest[-1]
        xv = x_ref[...]
        rstd = lax.rsqrt(jnp.mean(xv * xv, axis=-1, keepdims=True) + EPS)
        h_ref[...] = (xv * rstd * g_ref[...]).astype(BF16)

    return pl.pallas_call(body, name=name, grid=(NE,), in_specs=[_ROW, _VEC] + [_ANY] * len(after), out_specs=_ROW,
                          out_shape=_sds((T, D), BF16), compiler_params=_params(("parallel",)))(x, g, *after)


def _loss_head(x2, target, g):
    def body(x_ref, t_ref, g_ref, loss_ref, dx_ref, dxb_ref, dg_ref):
        xv = x_ref[...]
        rstd = lax.rsqrt(jnp.mean(xv * xv, axis=-1, keepdims=True) + EPS)
        xhat = xv * rstd
        gv = g_ref[...]
        err = xhat * gv - t_ref[...]
        dy = err * (1.0 / D)
        dxh = dy * gv
        dx = rstd * (dxh - xhat * jnp.mean(dxh * xhat, axis=-1, keepdims=True))
        dx_ref[...] = dx
        dxb_ref[...] = dx.astype(BF16)

        @pl.when(pl.program_id(0) == 0)
        def _():
            dg_ref[...] = jnp.zeros_like(dg_ref)
            loss_ref[...] = jnp.zeros_like(loss_ref)

        dg_ref[...] += jnp.sum(dy * xhat, axis=0, keepdims=True)
        loss_ref[...] += (0.5 / D) * jnp.sum(jnp.sum(err * err, axis=1, keepdims=True), axis=0, keepdims=True)

    return pl.pallas_call(
        body, name="loss_head", grid=(NE,), in_specs=[_ROW, _ROW, _VEC],
        out_specs=[pl.BlockSpec((1, 1), lambda i: (0, 0)), _ROW, _ROW, _VEC],
        out_shape=[_sds((1, 1), F32), _sds((T, D), F32), _sds((T, D), BF16), _sds((1, D), F32)],
        compiler_params=_params(("arbitrary",)))(x2, target, g)


MW = 512
_G_ATT_BLK = 3584 // MW
_G_REC_BLK = 4608 // MW


def _merge_specs():
    y = pl.BlockSpec((TM, MW), lambda i, j: (i, j))
    ga = pl.BlockSpec((TM, MW), lambda i, j: (i, _G_ATT_BLK + j))
    gr = pl.BlockSpec((TM, MW), lambda i, j: (i, _G_REC_BLK + j))
    return y, ga, gr


def _merge_fwd(y_att, y_rec, z):
    y, ga, gr = _merge_specs()

    def body(ya_ref, yr_ref, ga_ref, gr_ref, m_ref):
        m = _sigmoid(ga_ref[...]) * ya_ref[...] + _sigmoid(gr_ref[...]) * yr_ref[...]
        m_ref[...] = m.astype(BF16)

    return pl.pallas_call(body, name="merge_fwd", grid=(NI, D // MW), in_specs=[y, y, ga, gr], out_specs=y,
                          out_shape=_sds((T, D), BF16),
                          compiler_params=_params(("parallel", "parallel")))(y_att, y_rec, z, z)


def _merge_bwd(dm, y_att, y_rec, z):
    y, ga, gr = _merge_specs()

    def body(dm_ref, ya_ref, yr_ref, ga_ref, gr_ref, dya_ref, dyr_ref, dga_ref, dgr_ref):
        dmv = dm_ref[...]
        sa = _sigmoid(ga_ref[...])
        sr = _sigmoid(gr_ref[...])
        dya_ref[...] = (dmv * sa).astype(BF16)
        dyr_ref[...] = (dmv * sr).astype(BF16)
        dga_ref[...] = (dmv * ya_ref[...] * sa * (1.0 - sa)).astype(BF16)
        dgr_ref[...] = (dmv * yr_ref[...] * sr * (1.0 - sr)).astype(BF16)

    return pl.pallas_call(body, name="merge_bwd", grid=(NI, D // MW), in_specs=[y, y, y, ga, gr],
                          out_specs=[y, y, y, y], out_shape=[_sds((T, D), BF16)] * 4,
                          compiler_params=_params(("parallel", "parallel")))(dm, y_att, y_rec, z, z)


HP = 2 * HEAD_DIM
N_HP = N_HEADS // 2
ATT_UNROLL_FWD = 8
ATT_UNROLL_BWD = 4


def _window_maps():
    diag = np.zeros((GRID_W * GRID_W, 128), np.float32)
    for qc in range(GRID_W):
        w0 = min(max(qc - WIN_W // 2, 0), GRID_W - WIN_W)
        for kc in range(w0, w0 + WIN_W):
            diag[qc * GRID_W + kc, kc - qc + WIN_W - 1] = 1.0
    return diag, diag.sum(axis=1)[None, :]


def _split3(x):
    a = x.astype(BF16)
    r = x - a.astype(F32)
    b = r.astype(BF16)
    c = (r - b.astype(F32)).astype(BF16)
    return a, b, c


N_DROW = 2 * WIN_H - 1
N_DPAIR = N_DROW - 1


def _bias_pairs(rpb):
    diag, valid = _window_maps()
    r2 = jnp.pad(rpb.reshape(N_HEADS * N_DROW, 2 * WIN_W - 1),
                 ((0, 128 - N_HEADS * N_DROW), (0, 128 - (2 * WIN_W - 1))))

    def body(r_ref, d_ref, v_ref, o_ref):
        dv = d_ref[...]
        t = sum(_dot(part, dv, NN) for part in _split3(r_ref[...]))
        o_ref[...] = jnp.where(v_ref[...] > 0.0, t, -1e30)

    t = pl.pallas_call(body, name="rpb_expand", out_shape=_sds((128, GRID_W * GRID_W), F32),
                       compiler_params=_params())(r2, jnp.asarray(diag.T, BF16), jnp.asarray(valid, F32))
    t = t[:N_HEADS * N_DROW].reshape(N_HEADS, N_DROW, GRID_W, GRID_W)
    return jnp.concatenate([t[:, :N_DPAIR], t[:, 1:]], axis=-1)


def _row_bias(tb_ref, hh, d0):
    return jnp.concatenate([tb_ref[hh, d0 + 2 * ii] for ii in range(WIN_H // 2)], axis=1)


def _row_window(r):
    rs = jnp.clip(r - WIN_H // 2, 0, N_ROWS - WIN_H)
    return pl.multiple_of(r * GRID_W, GRID_W), pl.multiple_of(rs * GRID_W, GRID_W), rs - r + (WIN_H - 1)


def _split_heads(src_ref, dst_ref, scale=None):
    for hh in range(2):
        v = src_ref[:, hh * HEAD_DIM:(hh + 1) * HEAD_DIM]
        dst_ref[hh] = (v if scale is None else v * scale).astype(BF16)


def _attn_items(qb_ref, kb_ref, vb_ref, tb_ref, first_row, n_rows):
    wins = [_row_window(first_row + u) for u in range(n_rows)]
    items = [(u, hh) for u in range(n_rows) for hh in range(2)]
    q = [qb_ref[hh, pl.ds(wins[u][0], GRID_W), :] for u, hh in items]
    k = [kb_ref[hh, pl.ds(wins[u][1], KEYS), :] for u, hh in items]
    v = [vb_ref[hh, pl.ds(wins[u][1], KEYS), :] for u, hh in items]
    s = [_dot(qi, ki, NT) + _row_bias(tb_ref, hh, wins[u][2]) for qi, ki, (u, hh) in zip(q, k, items)]
    m = [jnp.max(si, axis=-1, keepdims=True) for si in s]
    e = [jnp.exp(si - mi) for si, mi in zip(s, m)]
    inv = [1.0 / jnp.sum(ei, axis=-1, keepdims=True) for ei in e]
    p = [ei * li for ei, li in zip(e, inv)]
    return wins, items, q, k, v, p


def _attn_in_specs():
    q = pl.BlockSpec((T, HP), lambda p: (0, p))
    k = pl.BlockSpec((T, HP), lambda p: (0, N_HP + p))
    v = pl.BlockSpec((T, HP), lambda p: (0, 2 * N_HP + p))
    tb = pl.BlockSpec((2, N_DPAIR, GRID_W, HP), lambda p: (p, 0, 0, 0))
    return q, k, v, tb


_HEAD_SCRATCH = pltpu.VMEM((2, T, HEAD_DIM), BF16)


def _attn_fwd(z, tb):
    def body(q_ref, k_ref, v_ref, tb_ref, o_ref, qb_ref, kb_ref, vb_ref):
        _split_heads(q_ref, qb_ref, SCALE)
        _split_heads(k_ref, kb_ref)
        _split_heads(v_ref, vb_ref)

        def rows(it, carry):
            wins, items, _, _, v, p = _attn_items(qb_ref, kb_ref, vb_ref, tb_ref, it * ATT_UNROLL_FWD, ATT_UNROLL_FWD)
            o = [_dot(pi.astype(BF16), vi, NN) for pi, vi in zip(p, v)]
            for u, (q0, _, _) in enumerate(wins):
                o_ref[pl.ds(q0, GRID_W), :] = jnp.concatenate(o[2 * u:2 * u + 2], axis=1).astype(BF16)
            return carry

        lax.fori_loop(0, N_ROWS // ATT_UNROLL_FWD, rows, 0)

    blk = pl.BlockSpec((T, HP), lambda p: (0, p))
    return pl.pallas_call(
        body, name="attn_fwd", grid=(N_HP,), in_specs=list(_attn_in_specs()), out_specs=blk,
        out_shape=_sds((T, D_ATT), BF16), scratch_shapes=[_HEAD_SCRATCH] * 3,
        compiler_params=_params(("parallel",)))(z, z, z, tb)


def _attn_bwd(z, tb, d_att, after=()):
    def body(q_ref, k_ref, v_ref, tb_ref, do_ref, *rest):
        dq_ref, dk_ref, dv_ref, ds_ref, qb_ref, kb_ref, vb_ref, dob_ref, dka_ref, dva_ref = rest[len(after):]
        _split_heads(q_ref, qb_ref, SCALE)
        _split_heads(k_ref, kb_ref)
        _split_heads(v_ref, vb_ref)
        _split_heads(do_ref, dob_ref)
        dka_ref[...] = jnp.zeros_like(dka_ref)
        dva_ref[...] = jnp.zeros_like(dva_ref)
        ds_ref[...] = jnp.zeros_like(ds_ref)

        def rows(it, carry):
            wins, items, q, k, v, p = _attn_items(qb_ref, kb_ref, vb_ref, tb_ref, it * ATT_UNROLL_BWD, ATT_UNROLL_BWD)
            do = [dob_ref[hh, pl.ds(wins[u][0], GRID_W), :] for u, hh in items]
            dv = [_dot(pi.astype(BF16), di, TN) for pi, di in zip(p, do)]
            dp = [_dot(di, vi, NT) for di, vi in zip(do, v)]
            ds = [pi * (dpi - jnp.sum(dpi * pi, axis=-1, keepdims=True)) for pi, dpi in zip(p, dp)]
            dsb = [d.astype(BF16) for d in ds]
            dq = [_dot(d, ki, NN) * SCALE for d, ki in zip(dsb, k)]
            dk = [_dot(d, qi, TN) for d, qi in zip(dsb, q)]
            for d, (u, hh) in zip(ds, items):
                for ii in range(WIN_H // 2):
                    ds_ref[hh, wins[u][2] + 2 * ii] += d[:, ii * HP:(ii + 1) * HP]
            for u, (q0, k0, _) in enumerate(wins):
                dq_ref[pl.ds(q0, GRID_W), :] = jnp.concatenate(dq[2 * u:2 * u + 2], axis=1).astype(BF16)
                dka_ref[pl.ds(k0, KEYS), :] += jnp.concatenate(dk[2 * u:2 * u + 2], axis=1)
                dva_ref[pl.ds(k0, KEYS), :] += jnp.concatenate(dv[2 * u:2 * u + 2], axis=1)
            return carry

        lax.fori_loop(0, N_ROWS // ATT_UNROLL_BWD, rows, 0)
        dk_ref[...] = dka_ref[...].astype(BF16)
        dv_ref[...] = dva_ref[...].astype(BF16)

    blk = pl.BlockSpec((T, HP), lambda p: (0, p))
    q, k, v, tbs = _attn_in_specs()
    return pl.pallas_call(
        body, name="attn_bwd", grid=(N_HP,), in_specs=[q, k, v, tbs, blk] + [_ANY] * len(after),
        out_specs=[blk, blk, blk, tbs],
        out_shape=[_sds((T, D_ATT), BF16)] * 3 + [_sds((N_HEADS, N_DPAIR, GRID_W, HP), F32)],
        scratch_shapes=[_HEAD_SCRATCH] * 4 + [pltpu.VMEM((T, HP), F32), pltpu.VMEM((T, HP), F32)],
        compiler_params=_params(("parallel",)))(z, z, z, tb, d_att, *after)


def _rpb_grad(ds_acc):
    a = ds_acc.reshape(N_HEADS, N_DPAIR, GRID_W, 2, GRID_W).transpose(0, 1, 3, 2, 4)
    a = jnp.pad(a.reshape(N_HEADS * N_DPAIR * 2, GRID_W * GRID_W), ((0, 256 - N_HEADS * N_DPAIR * 2), (0, 0)))
    sel = np.zeros((128, 256), np.float32)
    for h in range(N_HEADS):
        for pair in range(N_DPAIR):
            for half in range(2):
                sel[h * N_DROW + pair + half, (h * N_DPAIR + pair) * 2 + half] = 1.0
    diag, _ = _window_maps()

    def body(a_ref, sel_ref, diag_ref, o_ref):
        selv = sel_ref[...]
        g = sum(_dot(selv, part, NN) for part in _split3(a_ref[...]))
        diagv = diag_ref[...]
        o_ref[...] = sum(_dot(part, diagv, NN) for part in _split3(g))

    out = pl.pallas_call(body, name="rpb_grad", out_shape=_sds((128, 128), F32),
                         compiler_params=_params())(a, jnp.asarray(sel, BF16), jnp.asarray(diag, BF16))
    return out[:N_HEADS * N_DROW, :2 * WIN_W - 1].reshape(N_HEADS, N_DROW, 2 * WIN_W - 1)


N_CB = D // REC_CB
N_CHUNK = T // REC_CHUNK
N_TILE = T // 8
_U_BLK = 1536 // REC_CB
_Y_BLK = 2560 // REC_CB


def _block_diag(w):
    per = REC_CB // 64
    wt = w.reshape(2, N_CB, per, 64, 64)
    eye = jnp.eye(per, dtype=w.dtype)
    full = wt[:, :, :, :, None, :] * eye[None, None, :, None, :, None]
    return full.reshape(2, N_CB, REC_CB, REC_CB).astype(BF16)


def _block_diag_grad(g):
    per = REC_CB // 64
    g6 = g.reshape(2, N_CB, per, 64, per, 64)
    return jnp.stack([g6[:, :, p, :, p, :] for p in range(per)], axis=2).reshape(2, 16, 64, 64)


def _gelu(x):
    c = 0.7978845608028654
    return 0.5 * x * (1.0 + jnp.tanh(c * (x + 0.044715 * x * x * x)))


def _gelu_grad(x):
    c = 0.7978845608028654
    th = jnp.tanh(c * (x + 0.044715 * x * x * x))
    return 0.5 * (1.0 + th) + 0.5 * x * (1.0 - th * th) * c * (1.0 + 3.0 * 0.044715 * x * x)


def _softplus_neg(lam):
    x = -lam
    e = jnp.exp(-jnp.abs(x))
    w = 1.0 + e
    l1p = jnp.where(w == 1.0, e, jnp.log(w) * e / (w - 1.0))
    return jnp.maximum(x, 0.0) + l1p


def _one_minus_exp(x):
    poly = x * (1.0 + x * (1 / 2 + x * (1 / 6 + x * (1 / 24 + x * (1 / 120 + x * (1 / 720))))))
    return jnp.where(x > -0.125, -poly, 1.0 - jnp.exp(x))


def _conv_taps(pad_ref, t0, w, sign):
    out = None
    for j in range(4):
        term = w[j:j + 1, :] * pad_ref[pl.ds(PAD + t0 + sign * (j - 2), REC_CHUNK), :]
        out = term if out is None else out + term
    return out


def _gates(u, wa, wi, ba, bi, sp):
    ub = u.astype(BF16)
    r = _sigmoid(_dot(ub, wa, NN) + ba)
    i = _sigmoid(_dot(ub, wi, NN) + bi)
    log_a = -LRU_C * r * sp
    a = jnp.exp(log_a)
    mult = jnp.sqrt(jnp.maximum(_one_minus_exp(2.0 * log_a), 0.0))
    return r, i, a, mult


def _tile_scan(a, b, sub, reverse):
    for s in (1, 2, 4):
        if reverse:
            a_s, b_s, m = pltpu.roll(a, 8 - s, 0), pltpu.roll(b, 8 - s, 0), sub < 8 - s
        else:
            a_s, b_s, m = pltpu.roll(a, s, 0), pltpu.roll(b, s, 0), sub >= s
        b = jnp.where(m, a * b_s + b, b)
        a = jnp.where(m, a * a_s, a)
    return a, b


def _last_row(x, sub, row):
    return jnp.broadcast_to(jnp.sum(jnp.where(sub == row, x, 0.0), axis=0, keepdims=True), x.shape)


def _rec_prologue(up_ref, cw_ref, cb_ref, wa_ref, wi_ref, ba_ref, bi_ref, lam_ref,
                  upad_ref, u_ref, a_refs, h_refs):
    cb = up_ref.shape[1]
    zeros = jnp.zeros((PAD, cb), F32)
    upad_ref[pl.ds(0, PAD), :] = zeros
    upad_ref[pl.ds(PAD + T, PAD), :] = zeros
    upad_ref[pl.ds(PAD, T), :] = up_ref[...]
    cw = cw_ref[...]
    sp = _softplus_neg(lam_ref[...])
    for c in range(N_CHUNK):
        t0 = c * REC_CHUNK
        u = cb_ref[...] + _conv_taps(upad_ref, t0, cw, 1)
        u_ref[pl.ds(t0, REC_CHUNK), :] = u
        for d in range(2):
            _, i, a, mult = _gates(u, wa_ref[d], wi_ref[d], ba_ref[d:d + 1, :], bi_ref[d:d + 1, :], sp[d:d + 1, :])
            a_refs[d][pl.ds(t0, REC_CHUNK), :] = a
            h_refs[d][pl.ds(t0, REC_CHUNK), :] = mult * (i * u)

    sub = lax.broadcasted_iota(jnp.int32, (8, cb), 0)

    def tile(k, carry):
        cf, cr = carry
        tf = pl.multiple_of(k * 8, 8)
        tr = pl.multiple_of((N_TILE - 1 - k) * 8, 8)
        af, bf = _tile_scan(a_refs[0][pl.ds(tf, 8), :], h_refs[0][pl.ds(tf, 8), :], sub, False)
        hf = af * cf + bf
        h_refs[0][pl.ds(tf, 8), :] = hf
        ar, br = _tile_scan(a_refs[1][pl.ds(tr, 8), :], h_refs[1][pl.ds(tr, 8), :], sub, True)
        hr = ar * cr + br
        h_refs[1][pl.ds(tr, 8), :] = hr
        return _last_row(hf, sub, 7), _last_row(hr, sub, 0)

    z8 = jnp.zeros((8, cb), F32)
    lax.fori_loop(0, N_TILE, tile, (z8, z8))
    return sp


def _rec_specs():
    up = pl.BlockSpec((T, REC_CB), lambda c: (0, _U_BLK + c))
    yb = pl.BlockSpec((T, REC_CB), lambda c: (0, _Y_BLK + c))
    cw = pl.BlockSpec((4, REC_CB), lambda c: (0, c))
    cbias = pl.BlockSpec((1, REC_CB), lambda c: (0, c))
    wbd = pl.BlockSpec((2, None, REC_CB, REC_CB), lambda c: (0, c, 0, 0))
    vec2 = pl.BlockSpec((2, REC_CB), lambda c: (0, c))
    col = pl.BlockSpec((T, REC_CB), lambda c: (0, c))
    return up, yb, cw, cbias, wbd, vec2, col


def _rec_fwd(z, conv_w, conv_b, wa, wi, ba, bi, lam):
    up, yb, cw, cbias, wbd, vec2, col = _rec_specs()

    def body(up_ref, yb_ref, cw_ref, cb_ref, wa_ref, wi_ref, ba_ref, bi_ref, lam_ref, g_ref,
             u_ref, af_ref, ar_ref, hf_ref, hr_ref, upad_ref):
        _rec_prologue(up_ref, cw_ref, cb_ref, wa_ref, wi_ref, ba_ref, bi_ref, lam_ref,
                      upad_ref, u_ref, (af_ref, ar_ref), (hf_ref, hr_ref))

        def chunk(c, carry):
            t0 = pl.multiple_of(c * REC_CHUNK, REC_CHUNK)
            rows = pl.ds(t0, REC_CHUNK)
            g_ref[rows, :] = ((hf_ref[rows, :] + hr_ref[rows, :]) * _gelu(yb_ref[rows, :])).astype(BF16)
            return carry

        lax.fori_loop(0, N_CHUNK, chunk, 0)

    res = pl.pallas_call(
        body, name="rec_fwd", grid=(N_CB,),
        in_specs=[up, yb, cw, cbias, wbd, wbd, vec2, vec2, vec2], out_specs=[col] * 6,
        out_shape=[_sds((T, D), BF16)] + [_sds((T, D), F32)] * 5,
        scratch_shapes=[pltpu.VMEM((T + 2 * PAD, REC_CB), F32)],
        compiler_params=_params(("parallel",)))(z, z, conv_w, conv_b, wa, wi, ba, bi, lam)
    return res[0], tuple(res[1:])


def _rec_bwd(z, dg, saved, conv_w, conv_b, wa, wi, ba, bi, lam, after=()):
    up, yb, cw, cbias, wbd, vec2, col = _rec_specs()

    def body(up_ref, yb_ref, dg_ref, u_ref, af_ref, ar_ref, hf_ref, hr_ref,
             cw_ref, cb_ref, wa_ref, wi_ref, ba_ref, bi_ref, lam_ref, *rest):
        (dup_ref, dyb_ref, dcw_ref, dcb_ref, dwa_ref, dwi_ref, dba_ref, dbi_ref, dlam_ref,
         upad_ref, dh_ref, gf_ref, gr_ref, daf_ref, dar_ref, dupad_ref) = rest[len(after):]
        g_refs, da_refs = (gf_ref, gr_ref), (daf_ref, dar_ref)
        cb = up_ref.shape[1]
        zeros = jnp.zeros((PAD, cb), F32)
        upad_ref[pl.ds(0, PAD), :] = zeros
        upad_ref[pl.ds(PAD + T, PAD), :] = zeros
        upad_ref[pl.ds(PAD, T), :] = up_ref[...]
        sp = _softplus_neg(lam_ref[...])

        def gate_chunk(c, carry):
            t0 = pl.multiple_of(c * REC_CHUNK, REC_CHUNK)
            rows = pl.ds(t0, REC_CHUNK)
            y = yb_ref[rows, :]
            dgv = dg_ref[rows, :].astype(F32)
            dh_ref[rows, :] = dgv * _gelu(y)
            dyb_ref[rows, :] = (dgv * (hf_ref[rows, :] + hr_ref[rows, :]) * _gelu_grad(y)).astype(BF16)
            return carry

        lax.fori_loop(0, N_CHUNK, gate_chunk, 0)

        sub = lax.broadcasted_iota(jnp.int32, (8, cb), 0)

        def tile(k, carry):
            cf, cr = carry
            kf = N_TILE - 1 - k
            tf = pl.multiple_of(kf * 8, 8)
            tnext = pl.multiple_of(jnp.minimum(kf + 1, N_TILE - 1) * 8, 8)
            tprev = pl.multiple_of(jnp.maximum(kf - 1, 0) * 8, 8)
            a_t = af_ref[pl.ds(tf, 8), :]
            a_n = jnp.where(kf < N_TILE - 1, af_ref[pl.ds(tnext, 8), :], 0.0)
            a_sh = jnp.where(sub == 7, pltpu.roll(a_n, 7, 0), pltpu.roll(a_t, 7, 0))
            ca, cbb = _tile_scan(a_sh, dh_ref[pl.ds(tf, 8), :], sub, True)
            gf = ca * cf + cbb
            h_t = hf_ref[pl.ds(tf, 8), :]
            h_p = jnp.where(kf > 0, hf_ref[pl.ds(tprev, 8), :], 0.0)
            h_sh = jnp.where(sub == 0, pltpu.roll(h_p, 1, 0), pltpu.roll(h_t, 1, 0))
            gf_ref[pl.ds(tf, 8), :] = gf
            daf_ref[pl.ds(tf, 8), :] = gf * h_sh
            tr = pl.multiple_of(k * 8, 8)
            rnext = pl.multiple_of(jnp.minimum(k + 1, N_TILE - 1) * 8, 8)
            rprev = pl.multiple_of(jnp.maximum(k - 1, 0) * 8, 8)
            b_t = ar_ref[pl.ds(tr, 8), :]
            b_p = jnp.where(k > 0, ar_ref[pl.ds(rprev, 8), :], 0.0)
            b_sh = jnp.where(sub == 0, pltpu.roll(b_p, 1, 0), pltpu.roll(b_t, 1, 0))
            ra, rb = _tile_scan(b_sh, dh_ref[pl.ds(tr, 8), :], sub, False)
            gr = ra * cr + rb
            hr_t = hr_ref[pl.ds(tr, 8), :]
            hr_n = jnp.where(k < N_TILE - 1, hr_ref[pl.ds(rnext, 8), :], 0.0)
            hr_sh = jnp.where(sub == 7, pltpu.roll(hr_n, 7, 0), pltpu.roll(hr_t, 7, 0))
            gr_ref[pl.ds(tr, 8), :] = gr
            dar_ref[pl.ds(tr, 8), :] = gr * hr_sh
            return _last_row(gf, sub, 0), _last_row(gr, sub, 7)

        z8 = jnp.zeros((8, cb), F32)
        lax.fori_loop(0, N_TILE, tile, (z8, z8))

        dupad_ref[pl.ds(0, PAD), :] = zeros
        dupad_ref[pl.ds(PAD + T, PAD), :] = zeros
        dwa_ref[...] = jnp.zeros_like(dwa_ref)
        dwi_ref[...] = jnp.zeros_like(dwi_ref)
        dba_ref[...] = jnp.zeros_like(dba_ref)
        dbi_ref[...] = jnp.zeros_like(dbi_ref)
        dlam_ref[...] = jnp.zeros_like(dlam_ref)

        def grad_chunk(c, carry):
            t0 = pl.multiple_of(c * REC_CHUNK, REC_CHUNK)
            rows = pl.ds(t0, REC_CHUNK)
            u = u_ref[rows, :]
            ub = u.astype(BF16)
            du = jnp.zeros((REC_CHUNK, cb), F32)
            for d in range(2):
                r, i, a, mult = _gates(u, wa_ref[d], wi_ref[d], ba_ref[d:d + 1, :], bi_ref[d:d + 1, :], sp[d:d + 1, :])
                dbx = g_refs[d][rows, :]
                dmult = dbx * (i * u)
                diu = dbx * mult
                a2 = a * a
                dlog = da_refs[d][rows, :] * a - dmult * jnp.where(mult > 0.0, a2 / mult, 0.0)
                dpa = (dlog * (-LRU_C) * sp[d:d + 1, :]) * r * (1.0 - r)
                dpi = (diu * u) * i * (1.0 - i)
                dpab, dpib = dpa.astype(BF16), dpi.astype(BF16)
                du = du + diu * i + _dot(dpab, wa_ref[d], NT) + _dot(dpib, wi_ref[d], NT)
                dwa_ref[d] += _dot(ub, dpab, TN)
                dwi_ref[d] += _dot(ub, dpib, TN)
                dba_ref[d:d + 1, :] += jnp.sum(dpa, axis=0, keepdims=True)
                dbi_ref[d:d + 1, :] += jnp.sum(dpi, axis=0, keepdims=True)
                dlam_ref[d:d + 1, :] += jnp.sum(dlog * r, axis=0, keepdims=True)
            dupad_ref[pl.ds(PAD + t0, REC_CHUNK), :] = du
            return carry

        lax.fori_loop(0, N_CHUNK, grad_chunk, 0)
        dlam_ref[...] = dlam_ref[...] * (LRU_C * _sigmoid(-lam_ref[...]))

        cw = cw_ref[...]
        dcb = jnp.zeros((1, cb), F32)
        dcw = [jnp.zeros((1, cb), F32) for _ in range(4)]
        for c in range(N_CHUNK):
            t0 = c * REC_CHUNK
            du = dupad_ref[pl.ds(PAD + t0, REC_CHUNK), :]
            dcb = dcb + jnp.sum(du, axis=0, keepdims=True)
            for j in range(4):
                dcw[j] = dcw[j] + jnp.sum(du * upad_ref[pl.ds(PAD + t0 + j - 2, REC_CHUNK), :], axis=0, keepdims=True)
            dup_ref[pl.ds(t0, REC_CHUNK), :] = _conv_taps(dupad_ref, t0, cw, -1).astype(BF16)
        dcb_ref[...] = dcb
        dcw_ref[...] = jnp.concatenate(dcw, axis=0)

    full = pltpu.VMEM((T, REC_CB), F32)
    padded = pltpu.VMEM((T + 2 * PAD, REC_CB), F32)
    return pl.pallas_call(
        body, name="rec_bwd", grid=(N_CB,),
        in_specs=[up, yb] + [col] * 6 + [cw, cbias, wbd, wbd, vec2, vec2, vec2] + [_ANY] * len(after),
        out_specs=[col, col, cw, cbias, wbd, wbd, vec2, vec2, vec2],
        out_shape=[_sds((T, D), BF16), _sds((T, D), BF16), _sds((4, D), F32), _sds((1, D), F32),
                   _sds((2, N_CB, REC_CB, REC_CB), F32), _sds((2, N_CB, REC_CB, REC_CB), F32),
                   _sds((2, D), F32), _sds((2, D), F32), _sds((2, D), F32)],
        scratch_shapes=[padded, full, full, full, full, full, padded],
        compiler_params=_params(("parallel",)))(z, z, dg, *saved, conv_w, conv_b, wa, wi, ba, bi, lam, *after)


class _NoReducer:
    def begin(self, tag, grads):
        return ()

    def advance(self, tag, after):
        return ()


def _local_step(x, target, p, late=None, reducer=_NoReducer()):
    x = x.reshape(T, D)
    target = target.reshape(T, D)
    tb = _bias_pairs(p["rpb"])
    wa, wi = _block_diag(p["w_rg_a"]), _block_diag(p["w_rg_i"])

    h1 = _rms_fwd("rms1_fwd", x, p["ln1_g"], after=late[0] if late else ())
    if late:
        p = {**p, **late[1](h1)}
    rec_params = (p["conv_w"], p["conv_b"], wa, wi, p["b_rg_a"], p["b_rg_i"], p["lru_lambda"])
    (z,) = _mm_nn_cols("mm_z", h1, p["w_in"], F32, bias=p["b_in"])
    att = _attn_fwd(z, tb)
    g, rec_saved = _rec_fwd(z, *rec_params)
    if late:
        p = {**p, **late[2](g)}
    (y_att,) = _mm_nn_cols("mm_y_att", att, p["w_att_o"], F32)
    (y_rec,) = _mm_nn_rows("mm_y_rec", g, p["w_rec_o"], F32, tk=D)
    mixed = _merge_fwd(y_att, y_rec, z)

    def add_res(r, ex, outs):
        outs[0][...] = ex[0][...] + r

    res_spec = pl.BlockSpec((TM, D), lambda i, j, k: (i, 0))
    (x1,) = _mm_nn_rows("mm_x1", mixed, p["w_out"], F32, tk=D, extras=[x], extra_specs=[res_spec], epilogue=add_res)
    h2 = _rms_fwd("rms2_fwd", x1, p["ln2_g"])

    def relu2(r, ex, outs):
        rp = jnp.maximum(r, 0.0)
        outs[0][...] = (rp * rp).astype(BF16)

    (s,) = _mm_nn_cols("mm_ff1", h2, p["w_ff1"], BF16, epilogue=relu2)
    (x2,) = _mm_nn_rows("mm_x2", s, p["w_ff2"], F32, tk=D, extras=[x1], extra_specs=[res_spec], epilogue=add_res)
    loss, dx2, dx2_b, g_lnf = _loss_head(x2, target, p["lnf_g"])

    def relu2_bwd(r, ex, outs):
        outs[0][...] = (r * 2.0 * jnp.sqrt(ex[0][...].astype(F32))).astype(BF16)

    (df,) = _mm_nt_rows("mm_df", dx2_b, p["w_ff2"], BF16, tn=D, extras=[s],
                        extra_specs=[pl.BlockSpec((TM, D), lambda j, i, k: (i, j))], epilogue=relu2_bwd)
    (g_w_ff2,) = _mm_tn_rows("mm_g_ff2", s, dx2_b, tm=D)
    (g_w_ff1,) = _mm_tn_cols("mm_g_ff1", h2, df, D)
    tok = reducer.begin("ff", dict(w_ff2=g_w_ff2, w_ff1=g_w_ff1))
    dx1, dx1_b, g_ln2 = _mm_nt_cols_rms_bwd("mm_dh2_rms2_bwd", df, p["w_ff1"], x1, p["ln2_g"], dx2, after=tok,
                                            bf16_copy=True)

    (dmixed,) = _mm_nt_rows("mm_dmixed", dx1_b, p["w_out"], F32, tn=D)
    (g_w_out,) = _mm_tn_rows("mm_g_out", mixed, dx1_b, tm=D)
    dy_att, dy_rec, dg_att, dg_rec = _merge_bwd(dmixed, y_att, y_rec, z)
    (d_att,) = _mm_nt_cols("mm_d_att", dy_att, p["w_att_o"], BF16)
    (g_w_att_o,) = _mm_tn_cols("mm_g_att_o", att, dy_att, D // N_CHIPS)
    (d_g,) = _mm_nt_rows("mm_d_g", dy_rec, p["w_rec_o"], BF16, tn=D)
    (g_w_rec_o,) = _mm_tn_rows("mm_g_rec_o", g, dy_rec, tm=D)
    tok = reducer.advance("ff", g_w_rec_o) + reducer.begin("proj", dict(w_out=g_w_out, w_att_o=g_w_att_o, w_rec_o=g_w_rec_o))

    dq, dk, dv, ds_acc = _attn_bwd(z, tb, d_att, after=tok)
    g_rpb = _rpb_grad(ds_acc)
    tok = reducer.advance("proj", dq)
    d_up, d_yb, g_conv_w, g_conv_b, g_wa, g_wi, g_ba, g_bi, g_lam = _rec_bwd(z, d_g, rec_saved, *rec_params, after=tok)
    dz = jnp.concatenate([dq, dk, dv, d_up, d_yb, dg_att, dg_rec], axis=1)

    g_w_in, g_b_in = _mm_tn_cols("mm_g_in", h1, dz, D_IN // N_CHIPS, colsum=True)
    tok = reducer.begin("in", dict(w_in=g_w_in))
    grad_x, g_ln1 = _mm_nt_cols_rms_bwd("mm_dh1_rms1_bwd", dz, p["w_in"], x, p["ln1_g"], dx1, after=tok)
    reducer.advance("in", grad_x)

    grads = dict(ln1_g=g_ln1, w_in=g_w_in, b_in=g_b_in, rpb=g_rpb, w_att_o=g_w_att_o, conv_w=g_conv_w,
                 conv_b=g_conv_b, w_rg_a=_block_diag_grad(g_wa), b_rg_a=g_ba, w_rg_i=_block_diag_grad(g_wi),
                 b_rg_i=g_bi, lru_lambda=g_lam, w_rec_o=g_w_rec_o, w_out=g_w_out, ln2_g=g_ln2,
                 w_ff1=g_w_ff1, w_ff2=g_w_ff2, lnf_g=g_lnf)
    return loss, grad_x.reshape(1, T, D), grads


_ANY = pl.BlockSpec(memory_space=pl.ANY)
N_PEERS = N_CHIPS - 1


def _place():
    x, y, c = lax.axis_index("x"), lax.axis_index("y"), lax.axis_index("c")
    peers = [(1 - x, y), (x, 1 - y), (1 - x, 1 - y)]
    return x, y, c, 2 * x + y, peers


def _remote(src, dst, send_sem, recv_sem, dev):
    return pltpu.make_async_remote_copy(src_ref=src, dst_ref=dst, send_sem=send_sem, recv_sem=recv_sem,
                                        device_id=dev, device_id_type=MESH)


def _prefetch_call(body, name, ids, grid, in_specs, out_specs, out_shape, args):
    spec = pltpu.PrefetchScalarGridSpec(num_scalar_prefetch=1, grid=grid, in_specs=in_specs, out_specs=out_specs)
    return pl.pallas_call(body, name=name, grid_spec=spec, out_shape=out_shape,
                          compiler_params=_params(("parallel",) * len(grid)))(ids, *args)


def _cast_bf16(name, w, chip_id, after=()):
    rows, cols = w.shape
    rb = min(rows, 256)

    def body(ids_ref, w_ref, *rest):
        rest[-1][...] = w_ref[...].astype(BF16)

    return _prefetch_call(body, name, chip_id, (rows // rb,),
                          [pl.BlockSpec((rb, cols), lambda i, ids: (i, 0))] + [_ANY] * len(after),
                          pl.BlockSpec((None, rb, cols), lambda i, ids: (ids[0], i, 0)),
                          _sds((N_CHIPS, rows, cols), BF16), (w, *after))


def _dma_sems(*counts):
    return [pltpu.SemaphoreType.DMA((k,)) for k in counts]


_HBM = pl.BlockSpec(memory_space=pltpu.HBM)
_SEM = pl.BlockSpec(memory_space=pltpu.SEMAPHORE)
_SPLIT_COPY = pltpu.CompilerParams(has_side_effects=pltpu.SideEffectType.DATAFLOW_SIDE_EFFECTING)


def _hbm(arrays):
    return [pltpu.with_memory_space_constraint(a, pltpu.HBM) for a in arrays]


def _hbm_like(arrays):
    return [pltpu.HBM(a.shape, a.dtype) for a in arrays]


def _halves(buf, c):
    half = buf.shape[1] // 2
    return pl.ds(c * half, half), pl.ds((1 - c) * half, half)


def _gather_start(name, slots):
    n = len(slots)
    nk = n * N_PEERS

    def body(*refs):
        bufs = refs[n:2 * n]
        send_sems, recv_sems, token = refs[2 * n:]
        x, y, c, chip, peers = _place()
        for t in range(n):
            mine, _ = _halves(bufs[t], c)
            for r, (px, py) in enumerate(peers):
                k = t * N_PEERS + r
                own = bufs[t].at[chip, mine]
                _remote(own, own, send_sems.at[k], recv_sems.at[k], (px, py, c)).start()
        token[...] = jnp.zeros_like(token)

    res = pl.pallas_call(
        body, name=name, in_specs=[_HBM] * n, out_specs=[_HBM] * n + [_SEM, _SEM, pl.BlockSpec(memory_space=pltpu.VMEM)],
        out_shape=_hbm_like(slots) + [pltpu.SemaphoreType.DMA((nk,)), pltpu.SemaphoreType.DMA((nk,)),
                                      _sds((8, 128), F32)],
        input_output_aliases={t: t for t in range(n)}, compiler_params=_SPLIT_COPY)(*_hbm(slots))
    return res[:n], (res[n], res[n + 1]), res[n + 2]


def _gather_wait(name, bufs, sems, after):
    n = len(bufs)

    def body(*refs):
        ins = refs[:n]
        send_sems, recv_sems = refs[n], refs[n + 1]
        x, y, c, chip, peers = _place()
        for t in range(n):
            mine, _ = _halves(ins[t], c)
            for r, (px, py) in enumerate(peers):
                k = t * N_PEERS + r
                cp = _remote(ins[t].at[chip, mine], ins[t].at[2 * px + py, mine], send_sems.at[k], recv_sems.at[k],
                             (px, py, c))
                cp.wait_send()
                cp.wait_recv()

    return pl.pallas_call(
        body, name=name, in_specs=[_HBM] * n + [_SEM, _SEM, _ANY], out_specs=[_HBM] * n, out_shape=_hbm_like(bufs),
        input_output_aliases={t: t for t in range(n)}, compiler_params=_SPLIT_COPY)(*bufs, *sems, after)


def _gather_forward(name, bufs):
    n = len(bufs)
    nk = n * N_PEERS

    def body(*refs):
        outs = refs[n:2 * n]
        send_sems, recv_sems = refs[2 * n:]
        x, y, c, chip, peers = _place()
        sibling = (x, y, 1 - c)
        sends = []
        for t in range(n):
            mine, _ = _halves(outs[t], c)
            for r, (px, py) in enumerate(peers):
                k = t * N_PEERS + r
                landed = outs[t].at[2 * px + py, mine]
                sends.append(_remote(landed, landed, send_sems.at[k], recv_sems.at[k], sibling))
                sends[-1].start()
        for t in range(n):
            _, theirs = _halves(outs[t], c)
            for r, (px, py) in enumerate(peers):
                k = t * N_PEERS + r
                landed = outs[t].at[2 * px + py, theirs]
                _remote(landed, landed, send_sems.at[k], recv_sems.at[k], sibling).wait_recv()
        for cp in sends:
            cp.wait_send()

    return pl.pallas_call(
        body, name=name, in_specs=[_ANY] * n, out_specs=[_ANY] * n, out_shape=[_sds(b.shape, b.dtype) for b in bufs],
        input_output_aliases={t: t for t in range(n)}, scratch_shapes=_dma_sems(nk, nk))(*bufs)


def _pair_copies(n, srcs, lands, send_sems, recv_sems):
    x, y, c, _, _ = _place()
    sibling = (x, y, 1 - c)
    copies = []
    for t in range(n):
        half = srcs[t].shape[1] // 2
        for j in range(N_CHIPS):
            k = t * N_CHIPS + j
            copies.append(_remote(srcs[t].at[j, pl.ds((1 - c) * half, half)], lands[t].at[j],
                                  send_sems.at[k], recv_sems.at[k], sibling))
    for t in range(n, len(srcs)):
        k = n * N_CHIPS + t - n
        copies.append(_remote(srcs[t], lands[t], send_sems.at[k], recv_sems.at[k], sibling))
    return copies


def _pair_start(name, grads, wholes=()):
    n = len(grads)
    srcs = list(grads) + list(wholes)
    m = len(srcs)
    lands = [pltpu.HBM((N_CHIPS, g.shape[1] // 2, g.shape[2]), F32) for g in grads] + _hbm_like(wholes)
    ns = n * N_CHIPS + len(wholes)

    def body(*refs):
        src_refs, land_refs = refs[m:2 * m], refs[2 * m:3 * m]
        send_sems, recv_sems, token = refs[3 * m:]
        for cp in _pair_copies(n, src_refs, land_refs, send_sems, recv_sems):
            cp.start()
        token[...] = jnp.zeros_like(token)

    res = pl.pallas_call(
        body, name=name, in_specs=[_HBM] * m,
        out_specs=[_HBM] * (2 * m) + [_SEM, _SEM, pl.BlockSpec(memory_space=pltpu.VMEM)],
        out_shape=_hbm_like(srcs) + lands + [pltpu.SemaphoreType.DMA((ns,)), pltpu.SemaphoreType.DMA((ns,)),
                                             _sds((8, 128), F32)],
        input_output_aliases={t: t for t in range(m)}, compiler_params=_SPLIT_COPY)(*_hbm(srcs))
    return (res[:m], res[m:2 * m], (res[2 * m], res[2 * m + 1])), res[2 * m + 2]


def _pair_wait(name, flight, n, after):
    srcs, lands, sems = flight
    m = len(srcs)

    def body(*refs):
        for cp in _pair_copies(n, refs[:m], refs[m:2 * m], refs[2 * m], refs[2 * m + 1]):
            cp.wait_send()
            cp.wait_recv()

    res = pl.pallas_call(
        body, name=name, in_specs=[_HBM] * (2 * m) + [_SEM, _SEM, _ANY], out_specs=[_HBM] * (2 * m),
        out_shape=_hbm_like(srcs) + _hbm_like(lands), input_output_aliases={t: t for t in range(2 * m)},
        compiler_params=_SPLIT_COPY)(*srcs, *lands, *sems, after)
    return res[:m], res[m:]


def _chip_copies(srcs, lands, small_src, small_land, send_sems, recv_sems):
    x, y, c, chip, peers = _place()
    n = len(srcs)
    copies = []
    for r, (px, py) in enumerate(peers):
        for t in range(n):
            k = t * N_PEERS + r
            copies.append(_remote(srcs[t].at[2 * px + py], lands[t].at[r], send_sems.at[k], recv_sems.at[k], (px, py, c)))
        if small_src is not None:
            k = n * N_PEERS + r
            half_s = small_src.shape[0] // 2
            copies.append(_remote(small_src.at[pl.ds(c * half_s, half_s)], small_land.at[r],
                                  send_sems.at[k], recv_sems.at[k], (px, py, c)))
    return copies


def _chip_start(name, sums_bf16, small=None):
    n = len(sums_bf16)
    srcs = list(sums_bf16) + ([small] if small is not None else [])
    m = len(srcs)
    lands = [pltpu.HBM((N_PEERS,) + s.shape[1:], BF16) for s in sums_bf16]
    if small is not None:
        lands.append(pltpu.HBM((N_PEERS, small.shape[0] // 2, 128), F32))
    nk = m * N_PEERS

    def body(*refs):
        src_refs, land_refs = refs[m:2 * m], refs[2 * m:3 * m]
        send_sems, recv_sems, token = refs[3 * m:]
        small_src, small_land = (src_refs[n], land_refs[n]) if small is not None else (None, None)
        for cp in _chip_copies(src_refs[:n], land_refs[:n], small_src, small_land, send_sems, recv_sems):
            cp.start()
        token[...] = jnp.zeros_like(token)

    res = pl.pallas_call(
        body, name=name, in_specs=[_HBM] * m,
        out_specs=[_HBM] * (2 * m) + [_SEM, _SEM, pl.BlockSpec(memory_space=pltpu.VMEM)],
        out_shape=_hbm_like(srcs) + lands + [pltpu.SemaphoreType.DMA((nk,)), pltpu.SemaphoreType.DMA((nk,)),
                                             _sds((8, 128), F32)],
        input_output_aliases={t: t for t in range(m)}, compiler_params=_SPLIT_COPY)(*_hbm(srcs))
    return (res[:m], res[m:2 * m], (res[2 * m], res[2 * m + 1])), res[2 * m + 2]


def _chip_wait(name, flight, with_small, after):
    srcs, lands, sems = flight
    m = len(srcs)
    n = m - 1 if with_small else m

    def body(*refs):
        src_refs, land_refs = refs[:m], refs[m:2 * m]
        send_sems, recv_sems = refs[2 * m], refs[2 * m + 1]
        small_src, small_land = (src_refs[n], land_refs[n]) if with_small else (None, None)
        for cp in _chip_copies(src_refs[:n], land_refs[:n], small_src, small_land, send_sems, recv_sems):
            cp.wait_send()
            cp.wait_recv()

    res = pl.pallas_call(
        body, name=name, in_specs=[_HBM] * (2 * m) + [_SEM, _SEM, _ANY], out_specs=[_HBM] * (2 * m),
        out_shape=_hbm_like(srcs) + _hbm_like(lands), input_output_aliases={t: t for t in range(2 * m)},
        compiler_params=_SPLIT_COPY)(*srcs, *lands, *sems, after)
    return res[:m], res[m:]


def _half_swap(name, bufs):
    n = len(bufs)

    def body(*refs):
        outs = refs[n:2 * n]
        send_sem, recv_sem = refs[2 * n:]
        x, y, c, _, _ = _place()
        sibling = (x, y, 1 - c)
        copies = []
        for t in range(n):
            h = outs[t].shape[0] // 2
            mine = outs[t].at[pl.ds(c * h, h)]
            copies.append(_remote(mine, mine, send_sem.at[t], recv_sem.at[t], sibling))
            copies[-1].start()
        for t in range(n):
            h = outs[t].shape[0] // 2
            theirs = outs[t].at[pl.ds((1 - c) * h, h)]
            _remote(theirs, theirs, send_sem.at[t], recv_sem.at[t], sibling).wait_recv()
        for cp in copies:
            cp.wait_send()

    return pl.pallas_call(
        body, name=name, in_specs=[_ANY] * n, out_specs=[_ANY] * n,
        out_shape=[_sds(b.shape, b.dtype) for b in bufs], input_output_aliases={t: t for t in range(n)},
        scratch_shapes=_dma_sems(n, n))(*bufs)


def _pair_sum(name, grad, got, ids):
    _, rows, cols = got.shape
    rb = min(rows, 256)
    nb = rows // rb
    blk = pl.BlockSpec((None, rb, cols), lambda j, i, ids: (j, i, 0))
    mine = pl.BlockSpec((None, rb, cols), lambda j, i, ids: (j, ids[1] * nb + i, 0))

    def body(ids_ref, a_ref, b_ref, s_ref, sb_ref):
        s = a_ref[...] + b_ref[...]
        s_ref[...] = s
        sb_ref[...] = s.astype(BF16)

    return _prefetch_call(body, name, ids, (N_CHIPS, nb), [mine, blk], [blk, blk],
                          [_sds(got.shape, F32), _sds(got.shape, BF16)], (grad, got))


def _chip_sum(name, sums, got, ids):
    _, rows, cols = sums.shape
    rb = min(rows, 256)
    nb = rows // rb
    own = pl.BlockSpec((None, rb, cols), lambda i, ids: (ids[0], i, 0))
    blk3 = pl.BlockSpec((N_PEERS, rb, cols), lambda i, ids: (0, i, 0))
    out = pl.BlockSpec((rb, cols), lambda i, ids: (ids[1] * nb + i, 0))

    def body(ids_ref, a_ref, b_ref, o_ref):
        o_ref[...] = ((a_ref[...] + b_ref[0].astype(F32)) + b_ref[1].astype(F32)) + b_ref[2].astype(F32)

    return _prefetch_call(body, name, ids, (nb,), [own, blk3], out, _sds((2 * rows, cols), F32), (sums, got))


SMALL_RB = 280


def _small_pair_sum(own, got):
    blk = pl.BlockSpec((SMALL_RB, 128), lambda i: (i, 0))

    def body(a_ref, b_ref, o_ref):
        o_ref[...] = a_ref[...] + b_ref[...]

    return pl.pallas_call(body, name="small_pair_sum", grid=(own.shape[0] // SMALL_RB,), in_specs=[blk, blk],
                          out_specs=blk, out_shape=_sds(own.shape, F32),
                          compiler_params=_params(("parallel",)))(own, got)


def _small_chip_sum(pair, got, ids):
    nb = pair.shape[0] // 2 // SMALL_RB
    half = pl.BlockSpec((SMALL_RB, 128), lambda i, ids: (ids[1] * nb + i, 0))
    blk3 = pl.BlockSpec((N_PEERS, SMALL_RB, 128), lambda i, ids: (0, i, 0))

    def body(ids_ref, a_ref, b_ref, o_ref):
        o_ref[...] = (a_ref[...] + b_ref[1]) + (b_ref[0] + b_ref[2])

    return _prefetch_call(body, "small_chip_sum", ids, (nb,), [half, blk3], half, _sds(pair.shape, F32), (pair, got))


def _adamw_math(w, g, m, v):
    m = ADAM_B1 * m + (1.0 - ADAM_B1) * g
    v = ADAM_B2 * v + (1.0 - ADAM_B2) * (g * g)
    m_hat = m / (1.0 - ADAM_B1 ** ADAM_STEP)
    v_hat = v / (1.0 - ADAM_B2 ** ADAM_STEP)
    delta = -ADAM_LR * (m_hat / (jnp.sqrt(v_hat) + ADAM_EPS) + ADAM_WD * w)
    return delta, m, v


def _adamw(name, w, g, m, v, rb=None):
    rows, cols = w.shape
    rb = rows if rb is None else rb
    blk = pl.BlockSpec((rb, cols), lambda i: (i, 0))

    def body(w_ref, g_ref, m_ref, v_ref, d_ref, nm_ref, nv_ref):
        d, nm, nv = _adamw_math(w_ref[...], g_ref[...], m_ref[...], v_ref[...])
        d_ref[...] = d
        nm_ref[...] = nm
        nv_ref[...] = nv

    return pl.pallas_call(body, name=name, grid=(rows // rb,), in_specs=[blk] * 4, out_specs=[blk] * 3,
                          out_shape=[_sds(w.shape, F32)] * 3, compiler_params=_params(("parallel",)))(w, g, m, v)


BIG = ("w_in", "w_att_o", "w_rec_o", "w_out", "w_ff1", "w_ff2")
SHARDED_VECS = ("conv_w", "b_rg_a", "b_rg_i", "lru_lambda")
SMALL = ("ln1_g", "b_in", "rpb", "conv_w", "conv_b", "w_rg_a", "b_rg_a", "w_rg_i", "b_rg_i", "lru_lambda",
         "ln2_g", "lnf_g")
SMALL_ROWS = 2240
ORDER = ("ln1_g", "w_in", "b_in", "rpb", "w_att_o", "conv_w", "conv_b", "w_rg_a", "b_rg_a", "w_rg_i", "b_rg_i",
         "lru_lambda", "w_rec_o", "w_out", "ln2_g", "w_ff1", "w_ff2", "lnf_g")


def _pack_small(grads, loss):
    parts, sizes = [], {}
    for n in SMALL:
        flat = grads[n].reshape(-1)
        pad = (-flat.shape[0]) % 128
        sizes[n] = (flat.shape[0], flat.shape[0] + pad)
        parts.append(jnp.pad(flat, (0, pad)))
    total = sum(s[1] for s in sizes.values())
    parts.append(jnp.pad(loss.reshape(1), (0, SMALL_ROWS * 128 - total - 1)))
    return jnp.concatenate(parts).reshape(SMALL_ROWS, 128), sizes


def _unpack_small(buf, sizes, shapes):
    flat = buf.reshape(-1)
    out, pos = {}, 0
    for n in SMALL:
        size, padded = sizes[n]
        out[n] = flat[pos:pos + size].reshape(shapes[n])
        pos += padded
    return out, flat[pos]


def _gather_weights(w, chip):
    chip_id = chip.astype(jnp.int32).reshape(1)
    vec_rows = [w[n][0] for n in SHARDED_VECS]
    vec_shard = jnp.concatenate(vec_rows + [jnp.zeros((16 - 10, D // N_CHIPS), F32)], axis=0)
    vec_slots = lax.dynamic_update_slice(jnp.zeros((N_CHIPS, 16, D // N_CHIPS), F32), vec_shard[None], (chip, 0, 0))
    bufs_a, sems_a, token_a = _gather_start("gather_start_first", [_cast_bf16("cast_w_in", w["w_in"][0], chip_id), vec_slots])
    rest_names = BIG[1:]
    bufs_b, sems_b, token_b = _gather_start(
        "gather_start_rest", [_cast_bf16("cast_" + n, w[n][0], chip_id, after=(token_a,)) for n in rest_names])

    def first(after):
        w_in_full, vec_full = _gather_forward("gather_forward_first", _gather_wait("gather_wait_first", bufs_a, sems_a, after))
        vecs = vec_full.transpose(1, 0, 2).reshape(16, D)
        return dict(w_in=w_in_full, conv_w=vecs[0:4], b_rg_a=vecs[4:6], b_rg_i=vecs[6:8], lru_lambda=vecs[8:10])

    def rest(after):
        full = dict(zip(rest_names, _gather_forward("gather_forward_rest",
                                                    _gather_wait("gather_wait_rest", bufs_b, sems_b, after))))
        return dict(w_att_o=full["w_att_o"], w_ff1=full["w_ff1"], w_rec_o=full["w_rec_o"].reshape(D, D),
                    w_out=full["w_out"].reshape(D, D), w_ff2=full["w_ff2"].reshape(D_FF, D))

    p = dict(ln1_g=w["ln1_g"], b_in=w["b_in"], rpb=w["rpb"][0], conv_b=w["conv_b"], w_rg_a=w["w_rg_a"][0],
             w_rg_i=w["w_rg_i"][0], ln2_g=w["ln2_g"], lnf_g=w["lnf_g"].reshape(1, D))
    return p, ((token_b,), first, rest)


class _Reducer:
    def __init__(self, ids):
        self.ids = ids
        self.groups = {}

    def begin(self, tag, grads, small=None):
        names = list(grads)
        big = [grads[n].reshape(N_CHIPS, -1, grads[n].shape[-1]) for n in names]
        flight, token = _pair_start("pair_start_" + tag, big, [] if small is None else [small])
        self.groups[tag] = dict(names=names, pair=flight, small=small is not None)
        return (token,)

    def advance(self, tag, after):
        grp = self.groups[tag]
        n = len(grp["names"])
        mine, got = _pair_wait("pair_wait_" + tag, grp["pair"], n, after)
        sums = [_pair_sum("pair_sum_" + name, a, b, self.ids) for name, a, b in zip(grp["names"], mine, got)]
        small_sum = _small_pair_sum(mine[n], got[n]) if grp["small"] else None
        grp["chip"], token = _chip_start("chip_start_" + tag, [s[1] for s in sums], small_sum)
        grp["sums"] = [s[0] for s in sums]
        self.last_token = token
        return (token,)

    def finish(self, tag, after):
        grp = self.groups[tag]
        srcs, lands = _chip_wait("chip_wait_" + tag, grp["chip"], grp["small"], after)
        halves = [_chip_sum("chip_sum_" + name, s, b, self.ids) for name, s, b in zip(grp["names"], grp["sums"], lands)]
        if grp["small"]:
            halves.append(_small_chip_sum(srcs[-1], lands[-1], self.ids))
        return _half_swap("half_swap_" + tag, halves)


def kernel(x, ln1_g, w_in, b_in, rpb, w_att_o, conv_w, conv_b, w_rg_a, b_rg_a, w_rg_i, b_rg_i, lru_lambda, w_rec_o, w_out, ln2_g, w_ff1, w_ff2, lnf_g, loss_target, m_ln1_g, m_w_in, m_b_in, m_rpb, m_w_att_o, m_conv_w, m_conv_b, m_w_rg_a, m_b_rg_a, m_w_rg_i, m_b_rg_i, m_lru_lambda, m_w_rec_o, m_w_out, m_ln2_g, m_w_ff1, m_w_ff2, m_lnf_g, v_ln1_g, v_w_in, v_b_in, v_rpb, v_w_att_o, v_conv_w, v_conv_b, v_w_rg_a, v_b_rg_a, v_w_rg_i, v_b_rg_i, v_lru_lambda, v_w_rec_o, v_w_out, v_ln2_g, v_w_ff1, v_w_ff2, v_lnf_g):
    w = dict(ln1_g=ln1_g, w_in=w_in, b_in=b_in, rpb=rpb, w_att_o=w_att_o, conv_w=conv_w, conv_b=conv_b,
             w_rg_a=w_rg_a, b_rg_a=b_rg_a, w_rg_i=w_rg_i, b_rg_i=b_rg_i, lru_lambda=lru_lambda, w_rec_o=w_rec_o,
             w_out=w_out, ln2_g=ln2_g, w_ff1=w_ff1, w_ff2=w_ff2, lnf_g=lnf_g)
    m = dict(ln1_g=m_ln1_g, w_in=m_w_in, b_in=m_b_in, rpb=m_rpb, w_att_o=m_w_att_o, conv_w=m_conv_w,
             conv_b=m_conv_b, w_rg_a=m_w_rg_a, b_rg_a=m_b_rg_a, w_rg_i=m_w_rg_i, b_rg_i=m_b_rg_i,
             lru_lambda=m_lru_lambda, w_rec_o=m_w_rec_o, w_out=m_w_out, ln2_g=m_ln2_g, w_ff1=m_w_ff1,
             w_ff2=m_w_ff2, lnf_g=m_lnf_g)
    v = dict(ln1_g=v_ln1_g, w_in=v_w_in, b_in=v_b_in, rpb=v_rpb, w_att_o=v_w_att_o, conv_w=v_conv_w,
             conv_b=v_conv_b, w_rg_a=v_w_rg_a, b_rg_a=v_b_rg_a, w_rg_i=v_w_rg_i, b_rg_i=v_b_rg_i,
             lru_lambda=v_lru_lambda, w_rec_o=v_w_rec_o, w_out=v_w_out, ln2_g=v_ln2_g, w_ff1=v_w_ff1,
             w_ff2=v_w_ff2, lnf_g=v_lnf_g)
    chip = 2 * lax.axis_index("x") + lax.axis_index("y")
    ids = jnp.stack([chip, lax.axis_index("c")]).astype(jnp.int32)

    out_grad, out_delta, out_m, out_v = {}, {}, {}, {}

    def update(n, gn):
        shape = w[n].shape
        two_d, rb = (gn.shape, 256) if n in BIG else ((int(np.prod(shape[:-1])), shape[-1]), None)
        gn = gn.reshape(two_d)
        d, nm, nv = _adamw("adamw_" + n, w[n].reshape(two_d), gn, m[n].reshape(two_d), v[n].reshape(two_d), rb)
        out_grad[n], out_delta[n], out_m[n], out_v[n] = (gn.reshape(shape), d.reshape(shape), nm.reshape(shape),
                                                         nv.reshape(shape))
        return d

    reducer = _Reducer(ids)
    p, late = _gather_weights(w, chip)
    loss, grad_x, g = _local_step(x, loss_target, p, late, reducer)
    small, sizes = _pack_small(g, loss + reducer.last_token[:1, :1])
    after = reducer.begin("small", {}, small)[0]
    for tag in ("ff", "proj", "in"):
        for n, red in zip(reducer.groups[tag]["names"], reducer.finish(tag, after)):
            after = update(n, red)
        if tag == "ff":
            after = reducer.advance("small", after)[0]
    (small_red,) = reducer.finish("small", after)
    gsmall, loss = _unpack_small(small_red, sizes, {n: g[n].shape for n in SMALL})
    for n in SMALL:
        gn = gsmall[n]
        if n in SHARDED_VECS:
            gn = lax.dynamic_slice_in_dim(gn, chip * (D // N_CHIPS), D // N_CHIPS, axis=1)
        update(n, gn)
    return (loss, grad_x, *[out_grad[n] for n in ORDER], *[out_delta[n] for n in ORDER],
            *[out_m[n] for n in ORDER], *[out_v[n] for n in ORDER])
```

```python
import functools

import numpy as np
import jax
import jax.numpy as jnp
from jax import lax
from jax.experimental import pallas as pl
from jax.experimental.pallas import tpu as pltpu

F32 = jnp.float32
BF16 = jnp.bfloat16

T = 2048
D = 1024
D_ATT = 512
D_IN = 5632
D_FF = 4096
N_HEADS = 8
HEAD_DIM = 64
GRID_W = 64
N_ROWS = T // GRID_W
WIN_H = 8
WIN_W = 16
KEYS = WIN_H * GRID_W
N_CHIPS = 4
EPS = 1e-6
LRU_C = 8.0
SCALE = HEAD_DIM ** -0.5
REC_CB = 256
REC_CHUNK = 256
PAD = 8

ADAM_LR = 0.001
ADAM_B1 = 0.9
ADAM_B2 = 0.999
ADAM_EPS = 1e-08
ADAM_WD = 0.01
ADAM_STEP = 10

VMEM_LIMIT = 56 * 1024 * 1024

NN = (((1,), (0,)), ((), ()))
NT = (((1,), (1,)), ((), ()))
TN = (((0,), (0,)), ((), ()))
MESH = pl.DeviceIdType.MESH


def _params(sem=None):
    return pltpu.CompilerParams(dimension_semantics=sem, vmem_limit_bytes=VMEM_LIMIT)


def _dot(a, b, dims):
    return lax.dot_general(a, b, dims, preferred_element_type=F32)


def _sigmoid(x):
    return 0.5 * jnp.tanh(0.5 * x) + 0.5


def _matmul(name, a, b, *, dims, grid, a_spec, b_spec, out_shapes, out_specs, acc_shape,
            extras=(), extra_specs=(), epilogue=None, colsum_spec=None, colsum_shape=None, after=(),
            semantics=("parallel", "parallel", "arbitrary"), epilogue_takes_first=False):
    nk = grid[2]
    n_extra = len(extras)
    n_out = len(out_shapes)
    with_colsum = colsum_spec is not None

    def body(a_ref, b_ref, *rest):
        ex = rest[:n_extra]
        rest = rest[:n_extra] + rest[n_extra + len(after):]
        outs = rest[n_extra:n_extra + n_out]
        pos = n_extra + n_out
        cs_out = rest[pos] if with_colsum else None
        pos += 1 if with_colsum else 0
        acc = rest[pos]
        cs_acc = rest[pos + 1] if with_colsum else None
        k = pl.program_id(2)
        first_tile = pl.program_id(0) == 0

        @pl.when(k == 0)
        def _():
            acc[...] = jnp.zeros_like(acc)
            if with_colsum:
                cs_acc[...] = jnp.zeros_like(cs_acc)

        bv = b_ref[...]
        acc[...] += _dot(a_ref[...].astype(BF16), bv.astype(BF16), dims)
        if with_colsum:
            cs_acc[...] += jnp.sum(bv.astype(F32), axis=0, keepdims=True)

        @pl.when(k == nk - 1)
        def _():
            r = acc[...]
            if epilogue is None:
                outs[0][...] = r.astype(outs[0].dtype)
            elif epilogue_takes_first:
                epilogue(r, ex, outs, first_tile)
            else:
                epilogue(r, ex, outs)
            if with_colsum:
                cs_out[...] = cs_acc[...]

    shapes = list(out_shapes)
    specs = list(out_specs)
    scratch = [pltpu.VMEM(acc_shape, F32)]
    if with_colsum:
        shapes.append(colsum_shape)
        specs.append(colsum_spec)
        scratch.append(pltpu.VMEM((1, acc_shape[1]), F32))
    res = pl.pallas_call(
        body, name=name, grid=grid,
        in_specs=[a_spec, b_spec, *extra_specs] + [_ANY] * len(after),
        out_specs=specs, out_shape=shapes, scratch_shapes=scratch,
        compiler_params=_params(semantics),
    )(a, b, *extras, *after)
    return res


def _sds(shape, dtype):
    return jax.ShapeDtypeStruct(shape, dtype)


TM = 1024
NI = T // TM


def _mm_nn_cols(name, a, wg, out_dtype, *, bias=None, extras=(), extra_specs=(), epilogue=None,
                out_shapes=None, out_specs=None):
    k_dim, n4 = wg.shape[1], wg.shape[2]
    ex, exs = list(extras), list(extra_specs)
    if bias is not None:
        ex = [bias] + ex
        exs = [pl.BlockSpec((1, n4), lambda j, i, k: (0, j))] + exs
        user_ep = epilogue

        def epilogue(r, e, outs):
            r = r + e[0][...]
            if user_ep is None:
                outs[0][...] = r.astype(outs[0].dtype)
            else:
                user_ep(r, e[1:], outs)
    if out_shapes is None:
        out_shapes = [_sds((T, N_CHIPS * n4), out_dtype)]
        out_specs = [pl.BlockSpec((TM, n4), lambda j, i, k: (i, j))]
    return _matmul(
        name, a, wg, dims=NN, grid=(N_CHIPS, NI, 1),
        a_spec=pl.BlockSpec((TM, k_dim), lambda j, i, k: (i, 0)),
        b_spec=pl.BlockSpec((None, k_dim, n4), lambda j, i, k: (j, 0, 0)),
        out_shapes=out_shapes, out_specs=out_specs, acc_shape=(TM, n4),
        extras=ex, extra_specs=exs, epilogue=epilogue)


def _mm_nn_rows(name, a, w, out_dtype, *, tk, extras=(), extra_specs=(), epilogue=None):
    k_dim, n = w.shape
    return _matmul(
        name, a, w, dims=NN, grid=(NI, 1, k_dim // tk),
        a_spec=pl.BlockSpec((TM, tk), lambda i, j, k: (i, k)),
        b_spec=pl.BlockSpec((tk, n), lambda i, j, k: (k, 0)),
        out_shapes=[_sds((T, n), out_dtype)],
        out_specs=[pl.BlockSpec((TM, n), lambda i, j, k: (i, 0))], acc_shape=(TM, n),
        extras=extras, extra_specs=extra_specs, epilogue=epilogue)


def _mm_nt_cols(name, a, wg, out_dtype, after=()):
    k_dim, n4 = wg.shape[1], wg.shape[2]
    return _matmul(
        name, a, wg, dims=NT, grid=(NI, 1, N_CHIPS),
        a_spec=pl.BlockSpec((TM, n4), lambda i, j, k: (i, k)),
        b_spec=pl.BlockSpec((None, k_dim, n4), lambda i, j, k: (k, 0, 0)),
        out_shapes=[_sds((T, k_dim), out_dtype)],
        out_specs=[pl.BlockSpec((TM, k_dim), lambda i, j, k: (i, 0))], acc_shape=(TM, k_dim), after=after)


def _mm_nt_cols_rms_bwd(name, a, wg, x, g, dres, after=(), bf16_copy=False):
    n4 = wg.shape[2]
    row = pl.BlockSpec((TM, D), lambda i, j, k: (i, 0))
    vec = pl.BlockSpec((1, D), lambda i, j, k: (0, 0))

    def epilogue(dhv, ex, outs, first):
        x_ref, g_ref, dres_ref = ex
        dx_ref, dg_ref = outs[0], outs[-1]
        xv = x_ref[...]
        rstd = lax.rsqrt(jnp.mean(xv * xv, axis=-1, keepdims=True) + EPS)
        xhat = xv * rstd
        dy = dhv * g_ref[...]
        dx = dres_ref[...] + rstd * (dy - xhat * jnp.mean(dy * xhat, axis=-1, keepdims=True))
        dx_ref[...] = dx
        if bf16_copy:
            outs[1][...] = dx.astype(BF16)
        part = jnp.sum(dhv * xhat, axis=0, keepdims=True)

        @pl.when(first)
        def _():
            dg_ref[...] = part

        @pl.when(jnp.logical_not(first))
        def _():
            dg_ref[...] += part

    return _matmul(
        name, a, wg, dims=NT, grid=(NI, 1, N_CHIPS),
        a_spec=pl.BlockSpec((TM, n4), lambda i, j, k: (i, k)),
        b_spec=pl.BlockSpec((None, D, n4), lambda i, j, k: (k, 0, 0)),
        out_shapes=[_sds((T, D), F32)] + [_sds((T, D), BF16)] * bf16_copy + [_sds((1, D), F32)],
        out_specs=[row] + [row] * bf16_copy + [vec], acc_shape=(TM, D),
        extras=[x, g, dres], extra_specs=[row, vec, row], epilogue=epilogue, after=after,
        semantics=("arbitrary", "arbitrary", "arbitrary"), epilogue_takes_first=True)


def _mm_nt_rows(name, a, w, out_dtype, *, tn, extras=(), extra_specs=(), epilogue=None):
    k_dim, n = w.shape
    return _matmul(
        name, a, w, dims=NT, grid=(k_dim // tn, NI, 1),
        a_spec=pl.BlockSpec((TM, n), lambda j, i, k: (i, 0)),
        b_spec=pl.BlockSpec((tn, n), lambda j, i, k: (j, 0)),
        out_shapes=[_sds((T, k_dim), out_dtype)],
        out_specs=[pl.BlockSpec((TM, tn), lambda j, i, k: (i, j))], acc_shape=(TM, tn),
        extras=extras, extra_specs=extra_specs, epilogue=epilogue)


def _mm_tn_cols(name, a, g, n4, *, colsum=False):
    k_dim = a.shape[1]
    kw = {}
    if colsum:
        kw = dict(colsum_spec=pl.BlockSpec((1, n4), lambda j, i, k: (0, j)),
                  colsum_shape=_sds((1, N_CHIPS * n4), F32))
    return _matmul(
        name, a, g, dims=TN, grid=(N_CHIPS, 1, NI),
        a_spec=pl.BlockSpec((TM, k_dim), lambda j, i, k: (k, 0)),
        b_spec=pl.BlockSpec((TM, n4), lambda j, i, k: (k, j)),
        out_shapes=[_sds((N_CHIPS, k_dim, n4), F32)],
        out_specs=[pl.BlockSpec((None, k_dim, n4), lambda j, i, k: (j, 0, 0))],
        acc_shape=(k_dim, n4), **kw)


def _mm_tn_rows(name, a, g, *, tm):
    k_dim, n = a.shape[1], g.shape[1]
    return _matmul(
        name, a, g, dims=TN, grid=(k_dim // tm, 1, NI),
        a_spec=pl.BlockSpec((TM, tm), lambda j, i, k: (k, j)),
        b_spec=pl.BlockSpec((TM, n), lambda j, i, k: (k, 0)),
        out_shapes=[_sds((k_dim, n), F32)],
        out_specs=[pl.BlockSpec((tm, n), lambda j, i, k: (j, 0))], acc_shape=(tm, n))


TE = 256
NE = T // TE
_ROW = pl.BlockSpec((TE, D), lambda i: (i, 0))
_VEC = pl.BlockSpec((1, D), lambda i: (0, 0))


def _rms_fwd(name, x, g, after=()):
    def body(x_ref, g_ref, *rest):
        h_ref = rest[-1]
        xv = x_ref[...]
        rstd = lax.rsqrt(jnp.mean(xv * xv, axis=-1, keepdims=True) + EPS)
        h_ref[...] = (xv * rstd * g_ref[...]).astype(BF16)

    return pl.pallas_call(body, name=name, grid=(NE,), in_specs=[_ROW, _VEC] + [_ANY] * len(after), out_specs=_ROW,
                          out_shape=_sds((T, D), BF16), compiler_params=_params(("parallel",)))(x, g, *after)


def _loss_head(x2, target, g):
    def body(x_ref, t_ref, g_ref, loss_ref, dx_ref, dxb_ref, dg_ref):
        xv = x_ref[...]
        rstd = lax.rsqrt(jnp.mean(xv * xv, axis=-1, keepdims=True) + EPS)
        xhat = xv * rstd
        gv = g_ref[...]
        err = xhat * gv - t_ref[...]
        dy = err * (1.0 / D)
        dxh = dy * gv
        dx = rstd * (dxh - xhat * jnp.mean(dxh * xhat, axis=-1, keepdims=True))
        dx_ref[...] = dx
        dxb_ref[...] = dx.astype(BF16)

        @pl.when(pl.program_id(0) == 0)
        def _():
            dg_ref[...] = jnp.zeros_like(dg_ref)
            loss_ref[...] = jnp.zeros_like(loss_ref)

        dg_ref[...] += jnp.sum(dy * xhat, axis=0, keepdims=True)
        loss_ref[...] += (0.5 / D) * jnp.sum(jnp.sum(err * err, axis=1, keepdims=True), axis=0, keepdims=True)

    return pl.pallas_call(
        body, name="loss_head", grid=(NE,), in_specs=[_ROW, _ROW, _VEC],
        out_specs=[pl.BlockSpec((1, 1), lambda i: (0, 0)), _ROW, _ROW, _VEC],
        out_shape=[_sds((1, 1), F32), _sds((T, D), F32), _sds((T, D), BF16), _sds((1, D), F32)],
        compiler_params=_params(("arbitrary",)))(x2, target, g)


MW = 512
_G_ATT_BLK = 3584 // MW
_G_REC_BLK = 4608 // MW


def _merge_specs():
    y = pl.BlockSpec((TM, MW), lambda i, j: (i, j))
    ga = pl.BlockSpec((TM, MW), lambda i, j: (i, _G_ATT_BLK + j))
    gr = pl.BlockSpec((TM, MW), lambda i, j: (i, _G_REC_BLK + j))
    return y, ga, gr


def _merge_fwd(y_att, y_rec, z):
    y, ga, gr = _merge_specs()

    def body(ya_ref, yr_ref, ga_ref, gr_ref, m_ref):
        m = _sigmoid(ga_ref[...]) * ya_ref[...] + _sigmoid(gr_ref[...]) * yr_ref[...]
        m_ref[...] = m.astype(BF16)

    return pl.pallas_call(body, name="merge_fwd", grid=(NI, D // MW), in_specs=[y, y, ga, gr], out_specs=y,
                          out_shape=_sds((T, D), BF16),
                          compiler_params=_params(("parallel", "parallel")))(y_att, y_rec, z, z)


def _merge_bwd(dm, y_att, y_rec, z):
    y, ga, gr = _merge_specs()

    def body(dm_ref, ya_ref, yr_ref, ga_ref, gr_ref, dya_ref, dyr_ref, dga_ref, dgr_ref):
        dmv = dm_ref[...]
        sa = _sigmoid(ga_ref[...])
        sr = _sigmoid(gr_ref[...])
        dya_ref[...] = (dmv * sa).astype(BF16)
        dyr_ref[...] = (dmv * sr).astype(BF16)
        dga_ref[...] = (dmv * ya_ref[...] * sa * (1.0 - sa)).astype(BF16)
        dgr_ref[...] = (dmv * yr_ref[...] * sr * (1.0 - sr)).astype(BF16)

    return pl.pallas_call(body, name="merge_bwd", grid=(NI, D // MW), in_specs=[y, y, y, ga, gr],
                          out_specs=[y, y, y, y], out_shape=[_sds((T, D), BF16)] * 4,
                          compiler_params=_params(("parallel", "parallel")))(dm, y_att, y_rec, z, z)


HP = 2 * HEAD_DIM
N_HP = N_HEADS // 2
ATT_UNROLL_FWD = 8
ATT_UNROLL_BWD = 4
DIAG_ROWS = 32


def _window_maps():
    diag = np.zeros((GRID_W * GRID_W, 128), np.float32)
    for qc in range(GRID_W):
        w0 = min(max(qc - WIN_W // 2, 0), GRID_W - WIN_W)
        for kc in range(w0, w0 + WIN_W):
            diag[qc * GRID_W + kc, kc - qc + WIN_W - 1] = 1.0
    return diag, diag.sum(axis=1)[None, :]


def _split3(x):
    a = x.astype(BF16)
    r = x - a.astype(F32)
    b = r.astype(BF16)
    c = (r - b.astype(F32)).astype(BF16)
    return a, b, c


N_DROW = 2 * WIN_H - 1
N_DPAIR = N_DROW - 1


def _bias_pairs(rpb):
    diag, valid = _window_maps()
    r2 = jnp.pad(rpb.reshape(N_HEADS * N_DROW, 2 * WIN_W - 1),
                 ((0, 128 - N_HEADS * N_DROW), (0, 128 - (2 * WIN_W - 1))))

    def body(r_ref, d_ref, v_ref, o_ref):
        dv = d_ref[...]
        t = sum(_dot(part, dv, NN) for part in _split3(r_ref[...]))
        o_ref[...] = jnp.where(v_ref[...] > 0.0, t, -1e30)

    t = pl.pallas_call(body, name="rpb_expand", out_shape=_sds((128, GRID_W * GRID_W), F32),
                       compiler_params=_params())(r2, jnp.asarray(diag.T, BF16), jnp.asarray(valid, F32))
    t = t[:N_HEADS * N_DROW].reshape(N_HEADS, N_DROW, GRID_W, GRID_W)
    return jnp.concatenate([t[:, :N_DPAIR], t[:, 1:]], axis=-1)


def _row_bias(tb_ref, hh, d0):
    return jnp.concatenate([tb_ref[hh, d0 + 2 * ii] for ii in range(WIN_H // 2)], axis=1)


def _row_window(r):
    rs = jnp.clip(r - WIN_H // 2, 0, N_ROWS - WIN_H)
    return pl.multiple_of(r * GRID_W, GRID_W), pl.multiple_of(rs * GRID_W, GRID_W), rs - r + (WIN_H - 1)


def _split_heads(src_ref, dst_ref, scale=None):
    for hh in range(2):
        v = src_ref[:, hh * HEAD_DIM:(hh + 1) * HEAD_DIM]
        dst_ref[hh] = (v if scale is None else v * scale).astype(BF16)


def _attn_items(qb_ref, kb_ref, vb_ref, tb_ref, first_row, n_rows):
    wins = [_row_window(first_row + u) for u in range(n_rows)]
    items = [(u, hh) for u in range(n_rows) for hh in range(2)]
    q = [qb_ref[hh, pl.ds(wins[u][0], GRID_W), :] for u, hh in items]
    k = [kb_ref[hh, pl.ds(wins[u][1], KEYS), :] for u, hh in items]
    v = [vb_ref[hh, pl.ds(wins[u][1], KEYS), :] for u, hh in items]
    s = [_dot(qi, ki, NT) + _row_bias(tb_ref, hh, wins[u][2]) for qi, ki, (u, hh) in zip(q, k, items)]
    m = [jnp.max(si, axis=-1, keepdims=True) for si in s]
    e = [jnp.exp(si - mi) for si, mi in zip(s, m)]
    inv = [1.0 / jnp.sum(ei, axis=-1, keepdims=True) for ei in e]
    p = [ei * li for ei, li in zip(e, inv)]
    return wins, items, q, k, v, p


def _attn_in_specs():
    q = pl.BlockSpec((T, HP), lambda p: (0, p))
    k = pl.BlockSpec((T, HP), lambda p: (0, N_HP + p))
    v = pl.BlockSpec((T, HP), lambda p: (0, 2 * N_HP + p))
    tb = pl.BlockSpec((2, N_DPAIR, GRID_W, HP), lambda p: (p, 0, 0, 0))
    return q, k, v, tb


_HEAD_SCRATCH = pltpu.VMEM((2, T, HEAD_DIM), BF16)


def _attn_fwd(z, tb):
    def body(q_ref, k_ref, v_ref, tb_ref, o_ref, qb_ref, kb_ref, vb_ref):
        _split_heads(q_ref, qb_ref, SCALE)
        _split_heads(k_ref, kb_ref)
        _split_heads(v_ref, vb_ref)

        def rows(it, carry):
            wins, items, _, _, v, p = _attn_items(qb_ref, kb_ref, vb_ref, tb_ref, it * ATT_UNROLL_FWD, ATT_UNROLL_FWD)
            o = [_dot(pi.astype(BF16), vi, NN) for pi, vi in zip(p, v)]
            for u, (q0, _, _) in enumerate(wins):
                o_ref[pl.ds(q0, GRID_W), :] = jnp.concatenate(o[2 * u:2 * u + 2], axis=1).astype(BF16)
            return carry

        lax.fori_loop(0, N_ROWS // ATT_UNROLL_FWD, rows, 0)

    blk = pl.BlockSpec((T, HP), lambda p: (0, p))
    return pl.pallas_call(
        body, name="attn_fwd", grid=(N_HP,), in_specs=list(_attn_in_specs()), out_specs=blk,
        out_shape=_sds((T, D_ATT), BF16), scratch_shapes=[_HEAD_SCRATCH] * 3,
        compiler_params=_params(("parallel",)))(z, z, z, tb)


def _attn_bwd(z, tb, d_att, after=()):
    def body(q_ref, k_ref, v_ref, tb_ref, do_ref, flip_ref, *rest):
        (dq_ref, dk_ref, dv_ref, diag_ref, qb_ref, kb_ref, vb_ref, dob_ref, dka_ref, dva_ref,
         ds_ref) = rest[len(after):]
        _split_heads(q_ref, qb_ref, SCALE)
        _split_heads(k_ref, kb_ref)
        _split_heads(v_ref, vb_ref)
        _split_heads(do_ref, dob_ref)
        dka_ref[...] = jnp.zeros_like(dka_ref)
        dva_ref[...] = jnp.zeros_like(dva_ref)
        ds_ref[...] = jnp.zeros_like(ds_ref)

        def rows(it, carry):
            wins, items, q, k, v, p = _attn_items(qb_ref, kb_ref, vb_ref, tb_ref, it * ATT_UNROLL_BWD, ATT_UNROLL_BWD)
            do = [dob_ref[hh, pl.ds(wins[u][0], GRID_W), :] for u, hh in items]
            dv = [_dot(pi.astype(BF16), di, TN) for pi, di in zip(p, do)]
            dp = [_dot(di, vi, NT) for di, vi in zip(do, v)]
            ds = [pi * (dpi - jnp.sum(dpi * pi, axis=-1, keepdims=True)) for pi, dpi in zip(p, dp)]
            dsb = [d.astype(BF16) for d in ds]
            dq = [_dot(d, ki, NN) * SCALE for d, ki in zip(dsb, k)]
            dk = [_dot(d, qi, TN) for d, qi in zip(dsb, q)]
            for d, (u, hh) in zip(ds, items):
                for ii in range(WIN_H // 2):
                    ds_ref[hh, wins[u][2] + 2 * ii] += d[:, ii * HP:(ii + 1) * HP]
            for u, (q0, k0, _) in enumerate(wins):
                dq_ref[pl.ds(q0, GRID_W), :] = jnp.concatenate(dq[2 * u:2 * u + 2], axis=1).astype(BF16)
                dka_ref[pl.ds(k0, KEYS), :] += jnp.concatenate(dk[2 * u:2 * u + 2], axis=1)
                dva_ref[pl.ds(k0, KEYS), :] += jnp.concatenate(dv[2 * u:2 * u + 2], axis=1)
            return carry

        lax.fori_loop(0, N_ROWS // ATT_UNROLL_BWD, rows, 0)
        dk_ref[...] = dka_ref[...].astype(BF16)
        dv_ref[...] = dva_ref[...].astype(BF16)
        _diag_sums(ds_ref, flip_ref, diag_ref)

    blk = pl.BlockSpec((T, HP), lambda p: (0, p))
    q, k, v, tbs = _attn_in_specs()
    flip = jnp.asarray(np.eye(HP, dtype=np.float32)[::-1], BF16)
    return pl.pallas_call(
        body, name="attn_bwd", grid=(N_HP,),
        in_specs=[q, k, v, tbs, blk, pl.BlockSpec((HP, HP), lambda p: (0, 0))] + [_ANY] * len(after),
        out_specs=[blk, blk, blk, pl.BlockSpec((None, DIAG_ROWS, HP), lambda p: (p, 0, 0))],
        out_shape=[_sds((T, D_ATT), BF16)] * 3 + [_sds((N_HP, DIAG_ROWS, HP), F32)],
        scratch_shapes=[_HEAD_SCRATCH] * 4 + [pltpu.VMEM((T, HP), F32), pltpu.VMEM((T, HP), F32),
                                              pltpu.VMEM((2, N_DPAIR, GRID_W, HP), F32)],
        compiler_params=_params(("parallel",)))(z, z, z, tb, d_att, flip, *after)


def _diag_sums(acc_ref, flip_ref, out_ref):
    flip = flip_ref[...]
    rows = []
    for hh in range(2):
        for pair in range(N_DPAIR):
            reversed_lanes = sum(_dot(part, flip, NN) for part in _split3(acc_ref[hh, pair]))
            skewed = pltpu.roll(reversed_lanes, 0, 1, stride=1, stride_axis=0)
            rows.append(jnp.sum(skewed, axis=0, keepdims=True))
    rows.append(jnp.zeros((DIAG_ROWS - len(rows), HP), F32))
    out_ref[...] = jnp.concatenate(rows, axis=0)


def _rpb_grad(diag_sums):
    g = diag_sums.reshape(N_HP * DIAG_ROWS, HP)
    sel = np.zeros((2, 128, N_HP * DIAG_ROWS), np.float32)
    lane = np.zeros((2, HP, 128), np.float32)
    for h in range(N_HEADS):
        for pair in range(N_DPAIR):
            for half in range(2):
                sel[half, h * N_DROW + pair + half, (h // 2) * DIAG_ROWS + (h % 2) * N_DPAIR + pair] = 1.0
    for j in range(2 * WIN_W - 1):
        for half in range(2):
            lane[half, (HP - 1 - GRID_W * half - (j - (WIN_W - 1))) % HP, j] = 1.0

    def body(g_ref, sel_ref, lane_ref, o_ref):
        parts = _split3(g_ref[...])
        total = None
        for half in range(2):
            picked = sum(_dot(sel_ref[half], part, NN) for part in parts)
            term = sum(_dot(part, lane_ref[half], NN) for part in _split3(picked))
            total = term if total is None else total + term
        o_ref[...] = total

    out = pl.pallas_call(body, name="rpb_grad", out_shape=_sds((128, 128), F32),
                         compiler_params=_params())(g, jnp.asarray(sel, BF16), jnp.asarray(lane, BF16))
    return out[:N_HEADS * N_DROW, :2 * WIN_W - 1].reshape(N_HEADS, N_DROW, 2 * WIN_W - 1)


N_CB = D // REC_CB
N_CHUNK = T // REC_CHUNK
N_TILE = T // 8
_U_BLK = 1536 // REC_CB
_Y_BLK = 2560 // REC_CB


def _block_diag(w):
    per = REC_CB // 64
    wt = w.reshape(2, N_CB, per, 64, 64)
    eye = jnp.eye(per, dtype=w.dtype)
    full = wt[:, :, :, :, None, :] * eye[None, None, :, None, :, None]
    return full.reshape(2, N_CB, REC_CB, REC_CB).astype(BF16)


def _block_diag_grad(g):
    per = REC_CB // 64
    g6 = g.reshape(2, N_CB, per, 64, per, 64)
    return jnp.stack([g6[:, :, p, :, p, :] for p in range(per)], axis=2).reshape(2, 16, 64, 64)


def _gelu(x):
    c = 0.7978845608028654
    return 0.5 * x * (1.0 + jnp.tanh(c * (x + 0.044715 * x * x * x)))


def _gelu_grad(x):
    c = 0.7978845608028654
    th = jnp.tanh(c * (x + 0.044715 * x * x * x))
    return 0.5 * (1.0 + th) + 0.5 * x * (1.0 - th * th) * c * (1.0 + 3.0 * 0.044715 * x * x)


def _softplus_neg(lam):
    x = -lam
    e = jnp.exp(-jnp.abs(x))
    w = 1.0 + e
    l1p = jnp.where(w == 1.0, e, jnp.log(w) * e / (w - 1.0))
    return jnp.maximum(x, 0.0) + l1p


def _one_minus_exp(x):
    poly = x * (1.0 + x * (1 / 2 + x * (1 / 6 + x * (1 / 24 + x * (1 / 120 + x * (1 / 720))))))
    return jnp.where(x > -0.125, -poly, 1.0 - jnp.exp(x))


def _conv_taps(pad_ref, t0, w, sign):
    out = None
    for j in range(4):
        term = w[j:j + 1, :] * pad_ref[pl.ds(PAD + t0 + sign * (j - 2), REC_CHUNK), :]
        out = term if out is None else out + term
    return out


def _gates(u, wa, wi, ba, bi, sp):
    ub = u.astype(BF16)
    r = _sigmoid(_dot(ub, wa, NN) + ba)
    i = _sigmoid(_dot(ub, wi, NN) + bi)
    log_a = -LRU_C * r * sp
    a = jnp.exp(log_a)
    mult = jnp.sqrt(jnp.maximum(_one_minus_exp(2.0 * log_a), 0.0))
    return r, i, a, mult


def _tile_scan(a, b, sub, reverse):
    for s in (1, 2, 4):
        if reverse:
            a_s, b_s, m = pltpu.roll(a, 8 - s, 0), pltpu.roll(b, 8 - s, 0), sub < 8 - s
        else:
            a_s, b_s, m = pltpu.roll(a, s, 0), pltpu.roll(b, s, 0), sub >= s
        b = jnp.where(m, a * b_s + b, b)
        a = jnp.where(m, a * a_s, a)
    return a, b


def _last_row(x, sub, row):
    return jnp.broadcast_to(jnp.sum(jnp.where(sub == row, x, 0.0), axis=0, keepdims=True), x.shape)


def _rec_prologue(up_ref, cw_ref, cb_ref, wa_ref, wi_ref, ba_ref, bi_ref, lam_ref,
                  upad_ref, u_ref, a_refs, h_refs):
    cb = up_ref.shape[1]
    zeros = jnp.zeros((PAD, cb), F32)
    upad_ref[pl.ds(0, PAD), :] = zeros
    upad_ref[pl.ds(PAD + T, PAD), :] = zeros
    upad_ref[pl.ds(PAD, T), :] = up_ref[...]
    cw = cw_ref[...]
    sp = _softplus_neg(lam_ref[...])
    for c in range(N_CHUNK):
        t0 = c * REC_CHUNK
        u = cb_ref[...] + _conv_taps(upad_ref, t0, cw, 1)
        u_ref[pl.ds(t0, REC_CHUNK), :] = u
        for d in range(2):
            _, i, a, mult = _gates(u, wa_ref[d], wi_ref[d], ba_ref[d:d + 1, :], bi_ref[d:d + 1, :], sp[d:d + 1, :])
            a_refs[d][pl.ds(t0, REC_CHUNK), :] = a
            h_refs[d][pl.ds(t0, REC_CHUNK), :] = mult * (i * u)

    sub = lax.broadcasted_iota(jnp.int32, (8, cb), 0)

    def tile(k, carry):
        cf, cr = carry
        tf = pl.multiple_of(k * 8, 8)
        tr = pl.multiple_of((N_TILE - 1 - k) * 8, 8)
        af, bf = _tile_scan(a_refs[0][pl.ds(tf, 8), :], h_refs[0][pl.ds(tf, 8), :], sub, False)
        hf = af * cf + bf
        h_refs[0][pl.ds(tf, 8), :] = hf
        ar, br = _tile_scan(a_refs[1][pl.ds(tr, 8), :], h_refs[1][pl.ds(tr, 8), :], sub, True)
        hr = ar * cr + br
        h_refs[1][pl.ds(tr, 8), :] = hr
        return _last_row(hf, sub, 7), _last_row(hr, sub, 0)

    z8 = jnp.zeros((8, cb), F32)
    lax.fori_loop(0, N_TILE, tile, (z8, z8))
    return sp


def _rec_specs():
    up = pl.BlockSpec((T, REC_CB), lambda c: (0, _U_BLK + c))
    yb = pl.BlockSpec((T, REC_CB), lambda c: (0, _Y_BLK + c))
    cw = pl.BlockSpec((4, REC_CB), lambda c: (0, c))
    cbias = pl.BlockSpec((1, REC_CB), lambda c: (0, c))
    wbd = pl.BlockSpec((2, None, REC_CB, REC_CB), lambda c: (0, c, 0, 0))
    vec2 = pl.BlockSpec((2, REC_CB), lambda c: (0, c))
    col = pl.BlockSpec((T, REC_CB), lambda c: (0, c))
    return up, yb, cw, cbias, wbd, vec2, col


def _rec_fwd(z, conv_w, conv_b, wa, wi, ba, bi, lam):
    up, yb, cw, cbias, wbd, vec2, col = _rec_specs()

    def body(up_ref, yb_ref, cw_ref, cb_ref, wa_ref, wi_ref, ba_ref, bi_ref, lam_ref, g_ref,
             u_ref, af_ref, ar_ref, hf_ref, hr_ref, upad_ref):
        _rec_prologue(up_ref, cw_ref, cb_ref, wa_ref, wi_ref, ba_ref, bi_ref, lam_ref,
                      upad_ref, u_ref, (af_ref, ar_ref), (hf_ref, hr_ref))

        def chunk(c, carry):
            t0 = pl.multiple_of(c * REC_CHUNK, REC_CHUNK)
            rows = pl.ds(t0, REC_CHUNK)
            g_ref[rows, :] = ((hf_ref[rows, :] + hr_ref[rows, :]) * _gelu(yb_ref[rows, :])).astype(BF16)
            return carry

        lax.fori_loop(0, N_CHUNK, chunk, 0)

    res = pl.pallas_call(
        body, name="rec_fwd", grid=(N_CB,),
        in_specs=[up, yb, cw, cbias, wbd, wbd, vec2, vec2, vec2], out_specs=[col] * 6,
        out_shape=[_sds((T, D), BF16)] + [_sds((T, D), F32)] * 5,
        scratch_shapes=[pltpu.VMEM((T + 2 * PAD, REC_CB), F32)],
        compiler_params=_params(("parallel",)))(z, z, conv_w, conv_b, wa, wi, ba, bi, lam)
    return res[0], tuple(res[1:])


def _rec_bwd(z, dg, saved, conv_w, conv_b, wa, wi, ba, bi, lam, after=()):
    up, yb, cw, cbias, wbd, vec2, col = _rec_specs()

    def body(up_ref, yb_ref, dg_ref, u_ref, af_ref, ar_ref, hf_ref, hr_ref,
             cw_ref, cb_ref, wa_ref, wi_ref, ba_ref, bi_ref, lam_ref, *rest):
        (dup_ref, dyb_ref, dcw_ref, dcb_ref, dwa_ref, dwi_ref, dba_ref, dbi_ref, dlam_ref,
         upad_ref, dh_ref, gf_ref, gr_ref, daf_ref, dar_ref, dupad_ref) = rest[len(after):]
        g_refs, da_refs = (gf_ref, gr_ref), (daf_ref, dar_ref)
        cb = up_ref.shape[1]
        zeros = jnp.zeros((PAD, cb), F32)
        upad_ref[pl.ds(0, PAD), :] = zeros
        upad_ref[pl.ds(PAD + T, PAD), :] = zeros
        upad_ref[pl.ds(PAD, T), :] = up_ref[...]
        sp = _softplus_neg(lam_ref[...])

        def gate_chunk(c, carry):
            t0 = pl.multiple_of(c * REC_CHUNK, REC_CHUNK)
            rows = pl.ds(t0, REC_CHUNK)
            y = yb_ref[rows, :]
            dgv = dg_ref[rows, :].astype(F32)
            dh_ref[rows, :] = dgv * _gelu(y)
            dyb_ref[rows, :] = (dgv * (hf_ref[rows, :] + hr_ref[rows, :]) * _gelu_grad(y)).astype(BF16)
            return carry

        lax.fori_loop(0, N_CHUNK, gate_chunk, 0)

        sub = lax.broadcasted_iota(jnp.int32, (8, cb), 0)

        def tile(k, carry):
            cf, cr = carry
            kf = N_TILE - 1 - k
            tf = pl.multiple_of(kf * 8, 8)
            tnext = pl.multiple_of(jnp.minimum(kf + 1, N_TILE - 1) * 8, 8)
            tprev = pl.multiple_of(jnp.maximum(kf - 1, 0) * 8, 8)
            a_t = af_ref[pl.ds(tf, 8), :]
            a_n = jnp.where(kf < N_TILE - 1, af_ref[pl.ds(tnext, 8), :], 0.0)
            a_sh = jnp.where(sub == 7, pltpu.roll(a_n, 7, 0), pltpu.roll(a_t, 7, 0))
            ca, cbb = _tile_scan(a_sh, dh_ref[pl.ds(tf, 8), :], sub, True)
            gf = ca * cf + cbb
            h_t = hf_ref[pl.ds(tf, 8), :]
            h_p = jnp.where(kf > 0, hf_ref[pl.ds(tprev, 8), :], 0.0)
            h_sh = jnp.where(sub == 0, pltpu.roll(h_p, 1, 0), pltpu.roll(h_t, 1, 0))
            gf_ref[pl.ds(tf, 8), :] = gf
            daf_ref[pl.ds(tf, 8), :] = gf * h_sh
            tr = pl.multiple_of(k * 8, 8)
            rnext = pl.multiple_of(jnp.minimum(k + 1, N_TILE - 1) * 8, 8)
            rprev = pl.multiple_of(jnp.maximum(k - 1, 0) * 8, 8)
            b_t = ar_ref[pl.ds(tr, 8), :]
            b_p = jnp.where(k > 0, ar_ref[pl.ds(rprev, 8), :], 0.0)
            b_sh = jnp.where(sub == 0, pltpu.roll(b_p, 1, 0), pltpu.roll(b_t, 1, 0))
            ra, rb = _tile_scan(b_sh, dh_ref[pl.ds(tr, 8), :], sub, False)
            gr = ra * cr + rb
            hr_t = hr_ref[pl.ds(tr, 8), :]
            hr_n = jnp.where(k < N_TILE - 1, hr_ref[pl.ds(rnext, 8), :], 0.0)
            hr_sh = jnp.where(sub == 7, pltpu.roll(hr_n, 7, 0), pltpu.roll(hr_t, 7, 0))
            gr_ref[pl.ds(tr, 8), :] = gr
            dar_ref[pl.ds(tr, 8), :] = gr * hr_sh
            return _last_row(gf, sub, 0), _last_row(gr, sub, 7)

        z8 = jnp.zeros((8, cb), F32)
        lax.fori_loop(0, N_TILE, tile, (z8, z8))

        dupad_ref[pl.ds(0, PAD), :] = zeros
        dupad_ref[pl.ds(PAD + T, PAD), :] = zeros
        dwa_ref[...] = jnp.zeros_like(dwa_ref)
        dwi_ref[...] = jnp.zeros_like(dwi_ref)
        dba_ref[...] = jnp.zeros_like(dba_ref)
        dbi_ref[...] = jnp.zeros_like(dbi_ref)
        dlam_ref[...] = jnp.zeros_like(dlam_ref)

        def grad_chunk(c, carry):
            t0 = pl.multiple_of(c * REC_CHUNK, REC_CHUNK)
            rows = pl.ds(t0, REC_CHUNK)
            u = u_ref[rows, :]
            ub = u.astype(BF16)
            du = jnp.zeros((REC_CHUNK, cb), F32)
            for d in range(2):
                r, i, a, mult = _gates(u, wa_ref[d], wi_ref[d], ba_ref[d:d + 1, :], bi_ref[d:d + 1, :], sp[d:d + 1, :])
                dbx = g_refs[d][rows, :]
                dmult = dbx * (i * u)
                diu = dbx * mult
                a2 = a * a
                dlog = da_refs[d][rows, :] * a - dmult * jnp.where(mult > 0.0, a2 / mult, 0.0)
                dpa = (dlog * (-LRU_C) * sp[d:d + 1, :]) * r * (1.0 - r)
                dpi = (diu * u) * i * (1.0 - i)
                dpab, dpib = dpa.astype(BF16), dpi.astype(BF16)
                du = du + diu * i + _dot(dpab, wa_ref[d], NT) + _dot(dpib, wi_ref[d], NT)
                dwa_ref[d] += _dot(ub, dpab, TN)
                dwi_ref[d] += _dot(ub, dpib, TN)
                dba_ref[d:d + 1, :] += jnp.sum(dpa, axis=0, keepdims=True)
                dbi_ref[d:d + 1, :] += jnp.sum(dpi, axis=0, keepdims=True)
                dlam_ref[d:d + 1, :] += jnp.sum(dlog * r, axis=0, keepdims=True)
            dupad_ref[pl.ds(PAD + t0, REC_CHUNK), :] = du
            return carry

        lax.fori_loop(0, N_CHUNK, grad_chunk, 0)
        dlam_ref[...] = dlam_ref[...] * (LRU_C * _sigmoid(-lam_ref[...]))

        cw = cw_ref[...]
        dcb = jnp.zeros((1, cb), F32)
        dcw = [jnp.zeros((1, cb), F32) for _ in range(4)]
        for c in range(N_CHUNK):
            t0 = c * REC_CHUNK
            du = dupad_ref[pl.ds(PAD + t0, REC_CHUNK), :]
            dcb = dcb + jnp.sum(du, axis=0, keepdims=True)
            for j in range(4):
                dcw[j] = dcw[j] + jnp.sum(du * upad_ref[pl.ds(PAD + t0 + j - 2, REC_CHUNK), :], axis=0, keepdims=True)
            dup_ref[pl.ds(t0, REC_CHUNK), :] = _conv_taps(dupad_ref, t0, cw, -1).astype(BF16)
        dcb_ref[...] = dcb
        dcw_ref[...] = jnp.concatenate(dcw, axis=0)

    full = pltpu.VMEM((T, REC_CB), F32)
    padded = pltpu.VMEM((T + 2 * PAD, REC_CB), F32)
    return pl.pallas_call(
        body, name="rec_bwd", grid=(N_CB,),
        in_specs=[up, yb] + [col] * 6 + [cw, cbias, wbd, wbd, vec2, vec2, vec2] + [_ANY] * len(after),
        out_specs=[col, col, cw, cbias, wbd, wbd, vec2, vec2, vec2],
        out_shape=[_sds((T, D), BF16), _sds((T, D), BF16), _sds((4, D), F32), _sds((1, D), F32),
                   _sds((2, N_CB, REC_CB, REC_CB), F32), _sds((2, N_CB, REC_CB, REC_CB), F32),
                   _sds((2, D), F32), _sds((2, D), F32), _sds((2, D), F32)],
        scratch_shapes=[padded, full, full, full, full, full, padded],
        compiler_params=_params(("parallel",)))(z, z, dg, *saved, conv_w, conv_b, wa, wi, ba, bi, lam, *after)


class _NoReducer:
    def begin(self, tag, grads):
        return ()

    def advance(self, tag, after):
        return ()


def _local_step(x, target, p, late=None, reducer=_NoReducer()):
    x = x.reshape(T, D)
    target = target.reshape(T, D)
    tb = _bias_pairs(p["rpb"])
    wa, wi = _block_diag(p["w_rg_a"]), _block_diag(p["w_rg_i"])

    h1 = _rms_fwd("rms1_fwd", x, p["ln1_g"], after=late[0] if late else ())
    if late:
        p = {**p, **late[1](h1)}
    rec_params = (p["conv_w"], p["conv_b"], wa, wi, p["b_rg_a"], p["b_rg_i"], p["lru_lambda"])
    (z,) = _mm_nn_cols("mm_z", h1, p["w_in"], F32, bias=p["b_in"])
    att = _attn_fwd(z, tb)
    g, rec_saved = _rec_fwd(z, *rec_params)
    if late:
        p = {**p, **late[2](g)}
    (y_att,) = _mm_nn_cols("mm_y_att", att, p["w_att_o"], F32)
    (y_rec,) = _mm_nn_rows("mm_y_rec", g, p["w_rec_o"], F32, tk=D)
    mixed = _merge_fwd(y_att, y_rec, z)

    def add_res(r, ex, outs):
        outs[0][...] = ex[0][...] + r

    res_spec = pl.BlockSpec((TM, D), lambda i, j, k: (i, 0))
    (x1,) = _mm_nn_rows("mm_x1", mixed, p["w_out"], F32, tk=D, extras=[x], extra_specs=[res_spec], epilogue=add_res)
    h2 = _rms_fwd("rms2_fwd", x1, p["ln2_g"])

    def relu2(r, ex, outs):
        rp = jnp.maximum(r, 0.0)
        outs[0][...] = (rp * rp).astype(BF16)

    (s,) = _mm_nn_cols("mm_ff1", h2, p["w_ff1"], BF16, epilogue=relu2)
    (x2,) = _mm_nn_rows("mm_x2", s, p["w_ff2"], F32, tk=D, extras=[x1], extra_specs=[res_spec], epilogue=add_res)
    loss, dx2, dx2_b, g_lnf = _loss_head(x2, target, p["lnf_g"])

    def relu2_bwd(r, ex, outs):
        outs[0][...] = (r * 2.0 * jnp.sqrt(ex[0][...].astype(F32))).astype(BF16)

    (df,) = _mm_nt_rows("mm_df", dx2_b, p["w_ff2"], BF16, tn=D, extras=[s],
                        extra_specs=[pl.BlockSpec((TM, D), lambda j, i, k: (i, j))], epilogue=relu2_bwd)
    (g_w_ff2,) = _mm_tn_rows("mm_g_ff2", s, dx2_b, tm=D)
    (g_w_ff1,) = _mm_tn_cols("mm_g_ff1", h2, df, D)
    tok = reducer.begin("ff", dict(w_ff2=g_w_ff2, w_ff1=g_w_ff1))
    dx1, dx1_b, g_ln2 = _mm_nt_cols_rms_bwd("mm_dh2_rms2_bwd", df, p["w_ff1"], x1, p["ln2_g"], dx2, after=tok,
                                            bf16_copy=True)

    (dmixed,) = _mm_nt_rows("mm_dmixed", dx1_b, p["w_out"], F32, tn=D)
    (g_w_out,) = _mm_tn_rows("mm_g_out", mixed, dx1_b, tm=D)
    dy_att, dy_rec, dg_att, dg_rec = _merge_bwd(dmixed, y_att, y_rec, z)
    (d_att,) = _mm_nt_cols("mm_d_att", dy_att, p["w_att_o"], BF16)
    (g_w_att_o,) = _mm_tn_cols("mm_g_att_o", att, dy_att, D // N_CHIPS)
    (d_g,) = _mm_nt_rows("mm_d_g", dy_rec, p["w_rec_o"], BF16, tn=D)
    (g_w_rec_o,) = _mm_tn_rows("mm_g_rec_o", g, dy_rec, tm=D)
    tok = reducer.advance("ff", g_w_rec_o) + reducer.begin("proj", dict(w_out=g_w_out, w_att_o=g_w_att_o, w_rec_o=g_w_rec_o))

    dq, dk, dv, ds_acc = _attn_bwd(z, tb, d_att, after=tok)
    g_rpb = _rpb_grad(ds_acc)
    tok = reducer.advance("proj", dq)
    d_up, d_yb, g_conv_w, g_conv_b, g_wa, g_wi, g_ba, g_bi, g_lam = _rec_bwd(z, d_g, rec_saved, *rec_params, after=tok)
    dz = jnp.concatenate([dq, dk, dv, d_up, d_yb, dg_att, dg_rec], axis=1)

    g_w_in, g_b_in = _mm_tn_cols("mm_g_in", h1, dz, D_IN // N_CHIPS, colsum=True)
    tok = reducer.begin("in", dict(w_in=g_w_in))
    grad_x, g_ln1 = _mm_nt_cols_rms_bwd("mm_dh1_rms1_bwd", dz, p["w_in"], x, p["ln1_g"], dx1, after=tok)
    reducer.advance("in", grad_x)

    grads = dict(ln1_g=g_ln1, w_in=g_w_in, b_in=g_b_in, rpb=g_rpb, w_att_o=g_w_att_o, conv_w=g_conv_w,
                 conv_b=g_conv_b, w_rg_a=_block_diag_grad(g_wa), b_rg_a=g_ba, w_rg_i=_block_diag_grad(g_wi),
                 b_rg_i=g_bi, lru_lambda=g_lam, w_rec_o=g_w_rec_o, w_out=g_w_out, ln2_g=g_ln2,
                 w_ff1=g_w_ff1, w_ff2=g_w_ff2, lnf_g=g_lnf)
    return loss, grad_x.reshape(1, T, D), grads


_ANY = pl.BlockSpec(memory_space=pl.ANY)
N_PEERS = N_CHIPS - 1


def _place():
    x, y, c = lax.axis_index("x"), lax.axis_index("y"), lax.axis_index("c")
    peers = [(1 - x, y), (x, 1 - y), (1 - x, 1 - y)]
    return x, y, c, 2 * x + y, peers


def _remote(src, dst, send_sem, recv_sem, dev):
    return pltpu.make_async_remote_copy(src_ref=src, dst_ref=dst, send_sem=send_sem, recv_sem=recv_sem,
                                        device_id=dev, device_id_type=MESH)


def _prefetch_call(body, name, ids, grid, in_specs, out_specs, out_shape, args, semantics=None):
    spec = pltpu.PrefetchScalarGridSpec(num_scalar_prefetch=1, grid=grid, in_specs=in_specs, out_specs=out_specs)
    return pl.pallas_call(body, name=name, grid_spec=spec, out_shape=out_shape,
                          compiler_params=_params(semantics or ("parallel",) * len(grid)))(ids, *args)


def _cast_bf16(name, w, chip_id, after=()):
    rows, cols = w.shape
    rb = min(rows, 256)

    def body(ids_ref, w_ref, *rest):
        rest[-1][...] = w_ref[...].astype(BF16)

    return _prefetch_call(body, name, chip_id, (rows // rb,),
                          [pl.BlockSpec((rb, cols), lambda i, ids: (i, 0))] + [_ANY] * len(after),
                          pl.BlockSpec((None, rb, cols), lambda i, ids: (ids[0], i, 0)),
                          _sds((N_CHIPS, rows, cols), BF16), (w, *after))


def _dma_sems(*counts):
    return [pltpu.SemaphoreType.DMA((k,)) for k in counts]


_HBM = pl.BlockSpec(memory_space=pltpu.HBM)
_SEM = pl.BlockSpec(memory_space=pltpu.SEMAPHORE)
_SPLIT_COPY = pltpu.CompilerParams(has_side_effects=pltpu.SideEffectType.DATAFLOW_SIDE_EFFECTING)


def _hbm(arrays):
    return [pltpu.with_memory_space_constraint(a, pltpu.HBM) for a in arrays]


def _hbm_like(arrays):
    return [pltpu.HBM(a.shape, a.dtype) for a in arrays]


def _halves(buf, c):
    half = buf.shape[1] // 2
    return pl.ds(c * half, half), pl.ds((1 - c) * half, half)


def _gather_start(name, slots):
    n = len(slots)
    nk = n * N_PEERS

    def body(*refs):
        bufs = refs[n:2 * n]
        send_sems, recv_sems, token = refs[2 * n:]
        x, y, c, chip, peers = _place()
        for t in range(n):
            mine, _ = _halves(bufs[t], c)
            for r, (px, py) in enumerate(peers):
                k = t * N_PEERS + r
                own = bufs[t].at[chip, mine]
                _remote(own, own, send_sems.at[k], recv_sems.at[k], (px, py, c)).start()
        token[...] = jnp.zeros_like(token)

    res = pl.pallas_call(
        body, name=name, in_specs=[_HBM] * n, out_specs=[_HBM] * n + [_SEM, _SEM, pl.BlockSpec(memory_space=pltpu.VMEM)],
        out_shape=_hbm_like(slots) + [pltpu.SemaphoreType.DMA((nk,)), pltpu.SemaphoreType.DMA((nk,)),
                                      _sds((8, 128), F32)],
        input_output_aliases={t: t for t in range(n)}, compiler_params=_SPLIT_COPY)(*_hbm(slots))
    return res[:n], (res[n], res[n + 1]), res[n + 2]


def _gather_wait(name, bufs, sems, after):
    n = len(bufs)

    def body(*refs):
        ins = refs[:n]
        send_sems, recv_sems = refs[n], refs[n + 1]
        x, y, c, chip, peers = _place()
        for t in range(n):
            mine, _ = _halves(ins[t], c)
            for r, (px, py) in enumerate(peers):
                k = t * N_PEERS + r
                cp = _remote(ins[t].at[chip, mine], ins[t].at[2 * px + py, mine], send_sems.at[k], recv_sems.at[k],
                             (px, py, c))
                cp.wait_send()
                cp.wait_recv()

    return pl.pallas_call(
        body, name=name, in_specs=[_HBM] * n + [_SEM, _SEM, _ANY], out_specs=[_HBM] * n, out_shape=_hbm_like(bufs),
        input_output_aliases={t: t for t in range(n)}, compiler_params=_SPLIT_COPY)(*bufs, *sems, after)


def _gather_forward(name, bufs):
    n = len(bufs)
    nk = n * N_PEERS

    def body(*refs):
        outs = refs[n:2 * n]
        send_sems, recv_sems = refs[2 * n:]
        x, y, c, chip, peers = _place()
        sibling = (x, y, 1 - c)
        sends = []
        for t in range(n):
            mine, _ = _halves(outs[t], c)
            for r, (px, py) in enumerate(peers):
                k = t * N_PEERS + r
                landed = outs[t].at[2 * px + py, mine]
                sends.append(_remote(landed, landed, send_sems.at[k], recv_sems.at[k], sibling))
                sends[-1].start()
        for t in range(n):
            _, theirs = _halves(outs[t], c)
            for r, (px, py) in enumerate(peers):
                k = t * N_PEERS + r
                landed = outs[t].at[2 * px + py, theirs]
                _remote(landed, landed, send_sems.at[k], recv_sems.at[k], sibling).wait_recv()
        for cp in sends:
            cp.wait_send()

    return pl.pallas_call(
        body, name=name, in_specs=[_ANY] * n, out_specs=[_ANY] * n, out_shape=[_sds(b.shape, b.dtype) for b in bufs],
        input_output_aliases={t: t for t in range(n)}, scratch_shapes=_dma_sems(nk, nk))(*bufs)


def _pair_copies(n, srcs, lands, send_sems, recv_sems):
    x, y, c, _, _ = _place()
    sibling = (x, y, 1 - c)
    copies = []
    for t in range(n):
        half = srcs[t].shape[1] // 2
        for j in range(N_CHIPS):
            k = t * N_CHIPS + j
            copies.append(_remote(srcs[t].at[j, pl.ds((1 - c) * half, half)], lands[t].at[j],
                                  send_sems.at[k], recv_sems.at[k], sibling))
    for t in range(n, len(srcs)):
        k = n * N_CHIPS + t - n
        copies.append(_remote(srcs[t], lands[t], send_sems.at[k], recv_sems.at[k], sibling))
    return copies


def _pair_start(name, grads, wholes=()):
    n = len(grads)
    srcs = list(grads) + list(wholes)
    m = len(srcs)
    lands = [pltpu.HBM((N_CHIPS, g.shape[1] // 2, g.shape[2]), F32) for g in grads] + _hbm_like(wholes)
    ns = n * N_CHIPS + len(wholes)

    def body(*refs):
        src_refs, land_refs = refs[m:2 * m], refs[2 * m:3 * m]
        send_sems, recv_sems, token = refs[3 * m:]
        for cp in _pair_copies(n, src_refs, land_refs, send_sems, recv_sems):
            cp.start()
        token[...] = jnp.zeros_like(token)

    res = pl.pallas_call(
        body, name=name, in_specs=[_HBM] * m,
        out_specs=[_HBM] * (2 * m) + [_SEM, _SEM, pl.BlockSpec(memory_space=pltpu.VMEM)],
        out_shape=_hbm_like(srcs) + lands + [pltpu.SemaphoreType.DMA((ns,)), pltpu.SemaphoreType.DMA((ns,)),
                                             _sds((8, 128), F32)],
        input_output_aliases={t: t for t in range(m)}, compiler_params=_SPLIT_COPY)(*_hbm(srcs))
    return (res[:m], res[m:2 * m], (res[2 * m], res[2 * m + 1])), res[2 * m + 2]


def _pair_wait(name, flight, n, after):
    srcs, lands, sems = flight
    m = len(srcs)

    def body(*refs):
        for cp in _pair_copies(n, refs[:m], refs[m:2 * m], refs[2 * m], refs[2 * m + 1]):
            cp.wait_send()
            cp.wait_recv()

    res = pl.pallas_call(
        body, name=name, in_specs=[_HBM] * (2 * m) + [_SEM, _SEM, _ANY], out_specs=[_HBM] * (2 * m),
        out_shape=_hbm_like(srcs) + _hbm_like(lands), input_output_aliases={t: t for t in range(2 * m)},
        compiler_params=_SPLIT_COPY)(*srcs, *lands, *sems, after)
    return res[:m], res[m:]


def _chip_copies(srcs, lands, small_src, small_land, send_sems, recv_sems):
    x, y, c, chip, peers = _place()
    n = len(srcs)
    copies = []
    for r, (px, py) in enumerate(peers):
        for t in range(n):
            k = t * N_PEERS + r
            copies.append(_remote(srcs[t].at[2 * px + py], lands[t].at[r], send_sems.at[k], recv_sems.at[k], (px, py, c)))
        if small_src is not None:
            k = n * N_PEERS + r
            half_s = small_src.shape[0] // 2
            copies.append(_remote(small_src.at[pl.ds(c * half_s, half_s)], small_land.at[r],
                                  send_sems.at[k], recv_sems.at[k], (px, py, c)))
    return copies


def _chip_start(name, sums_bf16, small=None):
    n = len(sums_bf16)
    srcs = list(sums_bf16) + ([small] if small is not None else [])
    m = len(srcs)
    lands = [pltpu.HBM((N_PEERS,) + s.shape[1:], BF16) for s in sums_bf16]
    if small is not None:
        lands.append(pltpu.HBM((N_PEERS, small.shape[0] // 2, 128), F32))
    nk = m * N_PEERS

    def body(*refs):
        src_refs, land_refs = refs[m:2 * m], refs[2 * m:3 * m]
        send_sems, recv_sems, token = refs[3 * m:]
        small_src, small_land = (src_refs[n], land_refs[n]) if small is not None else (None, None)
        for cp in _chip_copies(src_refs[:n], land_refs[:n], small_src, small_land, send_sems, recv_sems):
            cp.start()
        token[...] = jnp.zeros_like(token)

    res = pl.pallas_call(
        body, name=name, in_specs=[_HBM] * m,
        out_specs=[_HBM] * (2 * m) + [_SEM, _SEM, pl.BlockSpec(memory_space=pltpu.VMEM)],
        out_shape=_hbm_like(srcs) + lands + [pltpu.SemaphoreType.DMA((nk,)), pltpu.SemaphoreType.DMA((nk,)),
                                             _sds((8, 128), F32)],
        input_output_aliases={t: t for t in range(m)}, compiler_params=_SPLIT_COPY)(*_hbm(srcs))
    return (res[:m], res[m:2 * m], (res[2 * m], res[2 * m + 1])), res[2 * m + 2]


def _chip_wait(name, flight, with_small, after):
    srcs, lands, sems = flight
    m = len(srcs)
    n = m - 1 if with_small else m

    def body(*refs):
        src_refs, land_refs = refs[:m], refs[m:2 * m]
        send_sems, recv_sems = refs[2 * m], refs[2 * m + 1]
        small_src, small_land = (src_refs[n], land_refs[n]) if with_small else (None, None)
        for cp in _chip_copies(src_refs[:n], land_refs[:n], small_src, small_land, send_sems, recv_sems):
            cp.wait_send()
            cp.wait_recv()

    res = pl.pallas_call(
        body, name=name, in_specs=[_HBM] * (2 * m) + [_SEM, _SEM, _ANY], out_specs=[_HBM] * (2 * m),
        out_shape=_hbm_like(srcs) + _hbm_like(lands), input_output_aliases={t: t for t in range(2 * m)},
        compiler_params=_SPLIT_COPY)(*srcs, *lands, *sems, after)
    return res[:m], res[m:]


def _half_swap(name, bufs):
    n = len(bufs)

    def body(*refs):
        outs = refs[n:2 * n]
        send_sem, recv_sem = refs[2 * n:]
        x, y, c, _, _ = _place()
        sibling = (x, y, 1 - c)
        copies = []
        for t in range(n):
            h = outs[t].shape[0] // 2
            mine = outs[t].at[pl.ds(c * h, h)]
            copies.append(_remote(mine, mine, send_sem.at[t], recv_sem.at[t], sibling))
            copies[-1].start()
        for t in range(n):
            h = outs[t].shape[0] // 2
            theirs = outs[t].at[pl.ds((1 - c) * h, h)]
            _remote(theirs, theirs, send_sem.at[t], recv_sem.at[t], sibling).wait_recv()
        for cp in copies:
            cp.wait_send()

    return pl.pallas_call(
        body, name=name, in_specs=[_ANY] * n, out_specs=[_ANY] * n,
        out_shape=[_sds(b.shape, b.dtype) for b in bufs], input_output_aliases={t: t for t in range(n)},
        scratch_shapes=_dma_sems(n, n))(*bufs)


def _pair_sum(name, grad, got, ids):
    _, rows, cols = got.shape
    rb = min(rows, 256)
    nb = rows // rb
    blk = pl.BlockSpec((None, rb, cols), lambda i, j, ids: (j, i, 0))
    mine = pl.BlockSpec((None, rb, cols), lambda i, j, ids: (j, ids[1] * nb + i, 0))
    own = pl.BlockSpec((rb, cols), lambda i, j, ids: (i, 0))

    def body(ids_ref, a_ref, b_ref, s_ref, sb_ref):
        s = a_ref[...] + b_ref[...]
        sb_ref[...] = s.astype(BF16)

        @pl.when(pl.program_id(1) == ids_ref[0])
        def _():
            s_ref[...] = s

    return _prefetch_call(body, name, ids, (nb, N_CHIPS), [mine, blk], [own, blk],
                          [_sds((rows, cols), F32), _sds(got.shape, BF16)], (grad, got),
                          semantics=("parallel", "arbitrary"))


def _chip_sum(name, own_sum, got, ids):
    rows, cols = own_sum.shape
    rb = min(rows, 256)
    nb = rows // rb
    own = pl.BlockSpec((rb, cols), lambda i, ids: (i, 0))
    blk3 = pl.BlockSpec((N_PEERS, rb, cols), lambda i, ids: (0, i, 0))
    out = pl.BlockSpec((rb, cols), lambda i, ids: (ids[1] * nb + i, 0))

    def body(ids_ref, a_ref, b_ref, o_ref):
        o_ref[...] = ((a_ref[...] + b_ref[0].astype(F32)) + b_ref[1].astype(F32)) + b_ref[2].astype(F32)

    return _prefetch_call(body, name, ids, (nb,), [own, blk3], out, _sds((2 * rows, cols), F32), (own_sum, got))


SMALL_RB = 280


def _small_pair_sum(own, got):
    blk = pl.BlockSpec((SMALL_RB, 128), lambda i: (i, 0))

    def body(a_ref, b_ref, o_ref):
        o_ref[...] = a_ref[...] + b_ref[...]

    return pl.pallas_call(body, name="small_pair_sum", grid=(own.shape[0] // SMALL_RB,), in_specs=[blk, blk],
                          out_specs=blk, out_shape=_sds(own.shape, F32),
                          compiler_params=_params(("parallel",)))(own, got)


def _small_chip_sum(pair, got, ids):
    nb = pair.shape[0] // 2 // SMALL_RB
    half = pl.BlockSpec((SMALL_RB, 128), lambda i, ids: (ids[1] * nb + i, 0))
    blk3 = pl.BlockSpec((N_PEERS, SMALL_RB, 128), lambda i, ids: (0, i, 0))

    def body(ids_ref, a_ref, b_ref, o_ref):
        o_ref[...] = (a_ref[...] + b_ref[1]) + (b_ref[0] + b_ref[2])

    return _prefetch_call(body, "small_chip_sum", ids, (nb,), [half, blk3], half, _sds(pair.shape, F32), (pair, got))


def _adamw_math(w, g, m, v):
    m = ADAM_B1 * m + (1.0 - ADAM_B1) * g
    v = ADAM_B2 * v + (1.0 - ADAM_B2) * (g * g)
    m_hat = m / (1.0 - ADAM_B1 ** ADAM_STEP)
    v_hat = v / (1.0 - ADAM_B2 ** ADAM_STEP)
    delta = -ADAM_LR * (m_hat / (jnp.sqrt(v_hat) + ADAM_EPS) + ADAM_WD * w)
    return delta, m, v


def _adamw(name, w, g, m, v, rb=None):
    rows, cols = w.shape
    rb = rows if rb is None else rb
    blk = pl.BlockSpec((rb, cols), lambda i: (i, 0))

    def body(w_ref, g_ref, m_ref, v_ref, d_ref, nm_ref, nv_ref):
        d, nm, nv = _adamw_math(w_ref[...], g_ref[...], m_ref[...], v_ref[...])
        d_ref[...] = d
        nm_ref[...] = nm
        nv_ref[...] = nv

    return pl.pallas_call(body, name=name, grid=(rows // rb,), in_specs=[blk] * 4, out_specs=[blk] * 3,
                          out_shape=[_sds(w.shape, F32)] * 3, compiler_params=_params(("parallel",)))(w, g, m, v)


BIG = ("w_in", "w_att_o", "w_rec_o", "w_out", "w_ff1", "w_ff2")
SHARDED_VECS = ("conv_w", "b_rg_a", "b_rg_i", "lru_lambda")
SMALL = ("ln1_g", "b_in", "rpb", "conv_w", "conv_b", "w_rg_a", "b_rg_a", "w_rg_i", "b_rg_i", "lru_lambda",
         "ln2_g", "lnf_g")
SMALL_ROWS = 2240
ORDER = ("ln1_g", "w_in", "b_in", "rpb", "w_att_o", "conv_w", "conv_b", "w_rg_a", "b_rg_a", "w_rg_i", "b_rg_i",
         "lru_lambda", "w_rec_o", "w_out", "ln2_g", "w_ff1", "w_ff2", "lnf_g")


def _pack_small(grads, loss):
    parts, sizes = [], {}
    for n in SMALL:
        flat = grads[n].reshape(-1)
        pad = (-flat.shape[0]) % 128
        sizes[n] = (flat.shape[0], flat.shape[0] + pad)
        parts.append(jnp.pad(flat, (0, pad)))
    total = sum(s[1] for s in sizes.values())
    parts.append(jnp.pad(loss.reshape(1), (0, SMALL_ROWS * 128 - total - 1)))
    return jnp.concatenate(parts).reshape(SMALL_ROWS, 128), sizes


def _unpack_small(buf, sizes, shapes):
    flat = buf.reshape(-1)
    out, pos = {}, 0
    for n in SMALL:
        size, padded = sizes[n]
        out[n] = flat[pos:pos + size].reshape(shapes[n])
        pos += padded
    return out, flat[pos]


def _gather_weights(w, chip):
    chip_id = chip.astype(jnp.int32).reshape(1)
    vec_rows = [w[n][0] for n in SHARDED_VECS]
    vec_shard = jnp.concatenate(vec_rows + [jnp.zeros((16 - 10, D // N_CHIPS), F32)], axis=0)
    vec_slots = lax.dynamic_update_slice(jnp.zeros((N_CHIPS, 16, D // N_CHIPS), F32), vec_shard[None], (chip, 0, 0))
    bufs_a, sems_a, token_a = _gather_start("gather_start_first", [_cast_bf16("cast_w_in", w["w_in"][0], chip_id), vec_slots])
    rest_names = BIG[1:]
    bufs_b, sems_b, token_b = _gather_start(
        "gather_start_rest", [_cast_bf16("cast_" + n, w[n][0], chip_id, after=(token_a,)) for n in rest_names])

    def first(after):
        w_in_full, vec_full = _gather_forward("gather_forward_first", _gather_wait("gather_wait_first", bufs_a, sems_a, after))
        vecs = vec_full.transpose(1, 0, 2).reshape(16, D)
        return dict(w_in=w_in_full, conv_w=vecs[0:4], b_rg_a=vecs[4:6], b_rg_i=vecs[6:8], lru_lambda=vecs[8:10])

    def rest(after):
        full = dict(zip(rest_names, _gather_forward("gather_forward_rest",
                                                    _gather_wait("gather_wait_rest", bufs_b, sems_b, after))))
        return dict(w_att_o=full["w_att_o"], w_ff1=full["w_ff1"], w_rec_o=full["w_rec_o"].reshape(D, D),
                    w_out=full["w_out"].reshape(D, D), w_ff2=full["w_ff2"].reshape(D_FF, D))

    p = dict(ln1_g=w["ln1_g"], b_in=w["b_in"], rpb=w["rpb"][0], conv_b=w["conv_b"], w_rg_a=w["w_rg_a"][0],
             w_rg_i=w["w_rg_i"][0], ln2_g=w["ln2_g"], lnf_g=w["lnf_g"].reshape(1, D))
    return p, ((token_b,), first, rest)


class _Reducer:
    def __init__(self, ids):
        self.ids = ids
        self.groups = {}

    def begin(self, tag, grads, small=None):
        names = list(grads)
        big = [grads[n].reshape(N_CHIPS, -1, grads[n].shape[-1]) for n in names]
        flight, token = _pair_start("pair_start_" + tag, big, [] if small is None else [small])
        self.groups[tag] = dict(names=names, pair=flight, small=small is not None)
        return (token,)

    def advance(self, tag, after):
        grp = self.groups[tag]
        n = len(grp["names"])
        mine, got = _pair_wait("pair_wait_" + tag, grp["pair"], n, after)
        sums = [_pair_sum("pair_sum_" + name, a, b, self.ids) for name, a, b in zip(grp["names"], mine, got)]
        small_sum = _small_pair_sum(mine[n], got[n]) if grp["small"] else None
        grp["chip"], token = _chip_start("chip_start_" + tag, [s[1] for s in sums], small_sum)
        grp["sums"] = [s[0] for s in sums]
        self.last_token = token
        return (token,)

    def finish(self, tag, after):
        grp = self.groups[tag]
        srcs, lands = _chip_wait("chip_wait_" + tag, grp["chip"], grp["small"], after)
        halves = [_chip_sum("chip_sum_" + name, s, b, self.ids) for name, s, b in zip(grp["names"], grp["sums"], lands)]
        if grp["small"]:
            halves.append(_small_chip_sum(srcs[-1], lands[-1], self.ids))
        return _half_swap("half_swap_" + tag, halves)


def kernel(x, ln1_g, w_in, b_in, rpb, w_att_o, conv_w, conv_b, w_rg_a, b_rg_a, w_rg_i, b_rg_i, lru_lambda, w_rec_o, w_out, ln2_g, w_ff1, w_ff2, lnf_g, loss_target, m_ln1_g, m_w_in, m_b_in, m_rpb, m_w_att_o, m_conv_w, m_conv_b, m_w_rg_a, m_b_rg_a, m_w_rg_i, m_b_rg_i, m_lru_lambda, m_w_rec_o, m_w_out, m_ln2_g, m_w_ff1, m_w_ff2, m_lnf_g, v_ln1_g, v_w_in, v_b_in, v_rpb, v_w_att_o, v_conv_w, v_conv_b, v_w_rg_a, v_b_rg_a, v_w_rg_i, v_b_rg_i, v_lru_lambda, v_w_rec_o, v_w_out, v_ln2_g, v_w_ff1, v_w_ff2, v_lnf_g):
    w = dict(ln1_g=ln1_g, w_in=w_in, b_in=b_in, rpb=rpb, w_att_o=w_att_o, conv_w=conv_w, conv_b=conv_b,
             w_rg_a=w_rg_a, b_rg_a=b_rg_a, w_rg_i=w_rg_i, b_rg_i=b_rg_i, lru_lambda=lru_lambda, w_rec_o=w_rec_o,
             w_out=w_out, ln2_g=ln2_g, w_ff1=w_ff1, w_ff2=w_ff2, lnf_g=lnf_g)
    m = dict(ln1_g=m_ln1_g, w_in=m_w_in, b_in=m_b_in, rpb=m_rpb, w_att_o=m_w_att_o, conv_w=m_conv_w,
             conv_b=m_conv_b, w_rg_a=m_w_rg_a, b_rg_a=m_b_rg_a, w_rg_i=m_w_rg_i, b_rg_i=m_b_rg_i,
             lru_lambda=m_lru_lambda, w_rec_o=m_w_rec_o, w_out=m_w_out, ln2_g=m_ln2_g, w_ff1=m_w_ff1,
             w_ff2=m_w_ff2, lnf_g=m_lnf_g)
    v = dict(ln1_g=v_ln1_g, w_in=v_w_in, b_in=v_b_in, rpb=v_rpb, w_att_o=v_w_att_o, conv_w=v_conv_w,
             conv_b=v_conv_b, w_rg_a=v_w_rg_a, b_rg_a=v_b_rg_a, w_rg_i=v_w_rg_i, b_rg_i=v_b_rg_i,
             lru_lambda=v_lru_lambda, w_rec_o=v_w_rec_o, w_out=v_w_out, ln2_g=v_ln2_g, w_ff1=v_w_ff1,
             w_ff2=v_w_ff2, lnf_g=v_lnf_g)
    chip = 2 * lax.axis_index("x") + lax.axis_index("y")
    ids = jnp.stack([chip, lax.axis_index("c")]).astype(jnp.int32)

    out_grad, out_delta, out_m, out_v = {}, {}, {}, {}

    def update(n, gn):
        shape = w[n].shape
        two_d, rb = (gn.shape, 256) if n in BIG else ((int(np.prod(shape[:-1])), shape[-1]), None)
        gn = gn.reshape(two_d)
        d, nm, nv = _adamw("adamw_" + n, w[n].reshape(two_d), gn, m[n].reshape(two_d), v[n].reshape(two_d), rb)
        out_grad[n], out_delta[n], out_m[n], out_v[n] = (gn.reshape(shape), d.reshape(shape), nm.reshape(shape),
                                                         nv.reshape(shape))
        return d

    reducer = _Reducer(ids)
    p, late = _gather_weights(w, chip)
    loss, grad_x, g = _local_step(x, loss_target, p, late, reducer)
    small, sizes = _pack_small(g, loss + reducer.last_token[:1, :1])
    after = reducer.begin("small", {}, small)[0]
    for tag in ("ff", "proj", "in"):
        for n, red in zip(reducer.groups[tag]["names"], reducer.finish(tag, after)):
            after = update(n, red)
        if tag == "ff":
            after = reducer.advance("small", after)[0]
    (small_red,) = reducer.finish("small", after)
    gsmall, loss = _unpack_small(small_red, sizes, {n: g[n].shape for n in SMALL})
    for n in SMALL:
        gn = gsmall[n]
        if n in SHARDED_VECS:
            gn = lax.dynamic_slice_in_dim(gn, chip * (D // N_CHIPS), D // N_CHIPS, axis=1)
        update(n, gn)
    return (loss, grad_x, *[out_grad[n] for n in ORDER], *[out_delta[n] for n in ORDER],
            *[out_m[n] for n in ORDER], *[out_v[n] for n in ORDER])
```

```python
import functools

import numpy as np
import jax
import jax.numpy as jnp
from jax import lax
from jax.experimental import pallas as pl
from jax.experimental.pallas import tpu as pltpu

F32 = jnp.float32
BF16 = jnp.bfloat16

T = 2048
D = 1024
D_ATT = 512
D_IN = 5632
D_FF = 4096
N_HEADS = 8
HEAD_DIM = 64
GRID_W = 64
N_ROWS = T // GRID_W
WIN_H = 8
WIN_W = 16
KEYS = WIN_H * GRID_W
N_CHIPS = 4
EPS = 1e-6
LRU_C = 8.0
SCALE = HEAD_DIM ** -0.5
REC_CB = 256
REC_CHUNK = 256
PAD = 8

ADAM_LR = 0.001
ADAM_B1 = 0.9
ADAM_B2 = 0.999
ADAM_EPS = 1e-08
ADAM_WD = 0.01
ADAM_STEP = 10

VMEM_LIMIT = 56 * 1024 * 1024

NN = (((1,), (0,)), ((), ()))
NT = (((1,), (1,)), ((), ()))
TN = (((0,), (0,)), ((), ()))
MESH = pl.DeviceIdType.MESH


def _params(sem=None):
    return pltpu.CompilerParams(dimension_semantics=sem, vmem_limit_bytes=VMEM_LIMIT)


def _dot(a, b, dims):
    return lax.dot_general(a, b, dims, preferred_element_type=F32)


def _sigmoid(x):
    return 0.5 * jnp.tanh(0.5 * x) + 0.5


def _matmul(name, a, b, *, dims, grid, a_spec, b_spec, out_shapes, out_specs, acc_shape,
            extras=(), extra_specs=(), epilogue=None, colsum_spec=None, colsum_shape=None, after=(),
            semantics=("parallel", "parallel", "arbitrary"), epilogue_takes_first=False):
    nk = grid[2]
    n_extra = len(extras)
    n_out = len(out_shapes)
    with_colsum = colsum_spec is not None

    def body(a_ref, b_ref, *rest):
        ex = rest[:n_extra]
        rest = rest[:n_extra] + rest[n_extra + len(after):]
        outs = rest[n_extra:n_extra + n_out]
        pos = n_extra + n_out
        cs_out = rest[pos] if with_colsum else None
        pos += 1 if with_colsum else 0
        acc = rest[pos]
        cs_acc = rest[pos + 1] if with_colsum else None
        k = pl.program_id(2)
        first_tile = pl.program_id(0) == 0

        @pl.when(k == 0)
        def _():
            acc[...] = jnp.zeros_like(acc)
            if with_colsum:
                cs_acc[...] = jnp.zeros_like(cs_acc)

        bv = b_ref[...]
        acc[...] += _dot(a_ref[...].astype(BF16), bv.astype(BF16), dims)
        if with_colsum:
            cs_acc[...] += jnp.sum(bv.astype(F32), axis=0, keepdims=True)

        @pl.when(k == nk - 1)
        def _():
            r = acc[...]
            if epilogue is None:
                outs[0][...] = r.astype(outs[0].dtype)
            elif epilogue_takes_first:
                epilogue(r, ex, outs, first_tile)
            else:
                epilogue(r, ex, outs)
            if with_colsum:
                cs_out[...] = cs_acc[...]

    shapes = list(out_shapes)
    specs = list(out_specs)
    scratch = [pltpu.VMEM(acc_shape, F32)]
    if with_colsum:
        shapes.append(colsum_shape)
        specs.append(colsum_spec)
        scratch.append(pltpu.VMEM((1, acc_shape[1]), F32))
    res = pl.pallas_call(
        body, name=name, grid=grid,
        in_specs=[a_spec, b_spec, *extra_specs] + [_ANY] * len(after),
        out_specs=specs, out_shape=shapes, scratch_shapes=scratch,
        compiler_params=_params(semantics),
    )(a, b, *extras, *after)
    return res


def _sds(shape, dtype):
    return jax.ShapeDtypeStruct(shape, dtype)


TM = 1024
NI = T // TM


def _mm_nn_cols(name, a, wg, out_dtype, *, bias=None, extras=(), extra_specs=(), epilogue=None,
                out_shapes=None, out_specs=None):
    k_dim, n4 = wg.shape[1], wg.shape[2]
    ex, exs = list(extras), list(extra_specs)
    if bias is not None:
        ex = [bias] + ex
        exs = [pl.BlockSpec((1, n4), lambda j, i, k: (0, j))] + exs
        user_ep = epilogue

        def epilogue(r, e, outs):
            r = r + e[0][...]
            if user_ep is None:
                outs[0][...] = r.astype(outs[0].dtype)
            else:
                user_ep(r, e[1:], outs)
    if out_shapes is None:
        out_shapes = [_sds((T, N_CHIPS * n4), out_dtype)]
        out_specs = [pl.BlockSpec((TM, n4), lambda j, i, k: (i, j))]
    return _matmul(
        name, a, wg, dims=NN, grid=(N_CHIPS, NI, 1),
        a_spec=pl.BlockSpec((TM, k_dim), lambda j, i, k: (i, 0)),
        b_spec=pl.BlockSpec((None, k_dim, n4), lambda j, i, k: (j, 0, 0)),
        out_shapes=out_shapes, out_specs=out_specs, acc_shape=(TM, n4),
        extras=ex, extra_specs=exs, epilogue=epilogue)


def _mm_nn_rows(name, a, w, out_dtype, *, tk, extras=(), extra_specs=(), epilogue=None):
    k_dim, n = w.shape
    return _matmul(
        name, a, w, dims=NN, grid=(NI, 1, k_dim // tk),
        a_spec=pl.BlockSpec((TM, tk), lambda i, j, k: (i, k)),
        b_spec=pl.BlockSpec((tk, n), lambda i, j, k: (k, 0)),
        out_shapes=[_sds((T, n), out_dtype)],
        out_specs=[pl.BlockSpec((TM, n), lambda i, j, k: (i, 0))], acc_shape=(TM, n),
        extras=extras, extra_specs=extra_specs, epilogue=epilogue)


def _mm_nt_cols_rms_bwd(name, a, wg, x, g, dres, after=(), bf16_copy=False):
    n4 = wg.shape[2]
    row = pl.BlockSpec((TM, D), lambda i, j, k: (i, 0))
    vec = pl.BlockSpec((1, D), lambda i, j, k: (0, 0))

    def epilogue(dhv, ex, outs, first):
        x_ref, g_ref, dres_ref = ex
        dx_ref, dg_ref = outs[0], outs[-1]
        xv = x_ref[...]
        rstd = lax.rsqrt(jnp.mean(xv * xv, axis=-1, keepdims=True) + EPS)
        xhat = xv * rstd
        dy = dhv * g_ref[...]
        dx = dres_ref[...] + rstd * (dy - xhat * jnp.mean(dy * xhat, axis=-1, keepdims=True))
        dx_ref[...] = dx
        if bf16_copy:
            outs[1][...] = dx.astype(BF16)
        part = jnp.sum(dhv * xhat, axis=0, keepdims=True)

        @pl.when(first)
        def _():
            dg_ref[...] = part

        @pl.when(jnp.logical_not(first))
        def _():
            dg_ref[...] += part

    return _matmul(
        name, a, wg, dims=NT, grid=(NI, 1, N_CHIPS),
        a_spec=pl.BlockSpec((TM, n4), lambda i, j, k: (i, k)),
        b_spec=pl.BlockSpec((None, D, n4), lambda i, j, k: (k, 0, 0)),
        out_shapes=[_sds((T, D), F32)] + [_sds((T, D), BF16)] * bf16_copy + [_sds((1, D), F32)],
        out_specs=[row] + [row] * bf16_copy + [vec], acc_shape=(TM, D),
        extras=[x, g, dres], extra_specs=[row, vec, row], epilogue=epilogue, after=after,
        semantics=("arbitrary", "arbitrary", "arbitrary"), epilogue_takes_first=True)


def _mm_nt_rows(name, a, w, out_dtype, *, tn, extras=(), extra_specs=(), epilogue=None):
    k_dim, n = w.shape
    return _matmul(
        name, a, w, dims=NT, grid=(k_dim // tn, NI, 1),
        a_spec=pl.BlockSpec((TM, n), lambda j, i, k: (i, 0)),
        b_spec=pl.BlockSpec((tn, n), lambda j, i, k: (j, 0)),
        out_shapes=[_sds((T, k_dim), out_dtype)],
        out_specs=[pl.BlockSpec((TM, tn), lambda j, i, k: (i, j))], acc_shape=(TM, tn),
        extras=extras, extra_specs=extra_specs, epilogue=epilogue)


def _mm_tn_cols(name, a, g, n4, *, colsum=False):
    k_dim = a.shape[1]
    kw = {}
    if colsum:
        kw = dict(colsum_spec=pl.BlockSpec((1, n4), lambda j, i, k: (0, j)),
                  colsum_shape=_sds((1, N_CHIPS * n4), F32))
    return _matmul(
        name, a, g, dims=TN, grid=(N_CHIPS, 1, NI),
        a_spec=pl.BlockSpec((TM, k_dim), lambda j, i, k: (k, 0)),
        b_spec=pl.BlockSpec((TM, n4), lambda j, i, k: (k, j)),
        out_shapes=[_sds((N_CHIPS, k_dim, n4), F32)],
        out_specs=[pl.BlockSpec((None, k_dim, n4), lambda j, i, k: (j, 0, 0))],
        acc_shape=(k_dim, n4), **kw)


def _mm_tn_rows(name, a, g, *, tm):
    k_dim, n = a.shape[1], g.shape[1]
    return _matmul(
        name, a, g, dims=TN, grid=(k_dim // tm, 1, NI),
        a_spec=pl.BlockSpec((TM, tm), lambda j, i, k: (k, j)),
        b_spec=pl.BlockSpec((TM, n), lambda j, i, k: (k, 0)),
        out_shapes=[_sds((k_dim, n), F32)],
        out_specs=[pl.BlockSpec((tm, n), lambda j, i, k: (j, 0))], acc_shape=(tm, n))


TE = 256
NE = T // TE
_ROW = pl.BlockSpec((TE, D), lambda i: (i, 0))
_VEC = pl.BlockSpec((1, D), lambda i: (0, 0))


def _rms_fwd(name, x, g, after=()):
    def body(x_ref, g_ref, *rest):
        h_ref = rest[-1]
        xv = x_ref[...]
        rstd = lax.rsqrt(jnp.mean(xv * xv, axis=-1, keepdims=True) + EPS)
        h_ref[...] = (xv * rstd * g_ref[...]).astype(BF16)

    return pl.pallas_call(body, name=name, grid=(NE,), in_specs=[_ROW, _VEC] + [_ANY] * len(after), out_specs=_ROW,
                          out_shape=_sds((T, D), BF16), compiler_params=_params(("parallel",)))(x, g, *after)


def _loss_head(x2, target, g):
    def body(x_ref, t_ref, g_ref, loss_ref, dx_ref, dxb_ref, dg_ref):
        xv = x_ref[...]
        rstd = lax.rsqrt(jnp.mean(xv * xv, axis=-1, keepdims=True) + EPS)
        xhat = xv * rstd
        gv = g_ref[...]
        err = xhat * gv - t_ref[...]
        dy = err * (1.0 / D)
        dxh = dy * gv
        dx = rstd * (dxh - xhat * jnp.mean(dxh * xhat, axis=-1, keepdims=True))
        dx_ref[...] = dx
        dxb_ref[...] = dx.astype(BF16)

        @pl.when(pl.program_id(0) == 0)
        def _():
            dg_ref[...] = jnp.zeros_like(dg_ref)
            loss_ref[...] = jnp.zeros_like(loss_ref)

        dg_ref[...] += jnp.sum(dy * xhat, axis=0, keepdims=True)
        loss_ref[...] += (0.5 / D) * jnp.sum(jnp.sum(err * err, axis=1, keepdims=True), axis=0, keepdims=True)

    return pl.pallas_call(
        body, name="loss_head", grid=(NE,), in_specs=[_ROW, _ROW, _VEC],
        out_specs=[pl.BlockSpec((1, 1), lambda i: (0, 0)), _ROW, _ROW, _VEC],
        out_shape=[_sds((1, 1), F32), _sds((T, D), F32), _sds((T, D), BF16), _sds((1, D), F32)],
        compiler_params=_params(("arbitrary",)))(x2, target, g)


MW = 512
_G_ATT_BLK = 3584 // MW
_G_REC_BLK = 4608 // MW


TB = 512


def _branch_specs():
    def row(cols):
        return pl.BlockSpec((TB, cols), lambda i: (i, 0))

    ga = pl.BlockSpec((TB, MW), lambda i: (i, _G_ATT_BLK))
    ga2 = pl.BlockSpec((TB, MW), lambda i: (i, _G_ATT_BLK + 1))
    gr = pl.BlockSpec((TB, MW), lambda i: (i, _G_REC_BLK))
    gr2 = pl.BlockSpec((TB, MW), lambda i: (i, _G_REC_BLK + 1))
    w_att = pl.BlockSpec((N_CHIPS, D_ATT, D // N_CHIPS), lambda i: (0, 0, 0))
    w_sq = pl.BlockSpec((D, D), lambda i: (0, 0))
    return row, (ga, ga2, gr, gr2), w_att, w_sq


def _gate_values(gate_refs):
    ga, ga2, gr, gr2 = (r[...] for r in gate_refs)
    return _sigmoid(jnp.concatenate([ga, ga2], axis=1)), _sigmoid(jnp.concatenate([gr, gr2], axis=1))


def _branches_fwd(att, g, z, x, w_att_o, w_rec_o, w_out):
    row, gate_specs, w_att, w_sq = _branch_specs()

    def body(att_ref, g_ref, ga_ref, ga2_ref, gr_ref, gr2_ref, x_ref, wa_ref, wr_ref, wo_ref,
             ya_ref, yr_ref, m_ref, x1_ref):
        attv = att_ref[...]
        ya = jnp.concatenate([_dot(attv, wa_ref[j], NN) for j in range(N_CHIPS)], axis=1)
        yr = _dot(g_ref[...], wr_ref[...], NN)
        sa, sr = _gate_values((ga_ref, ga2_ref, gr_ref, gr2_ref))
        mixed = (sa * ya + sr * yr).astype(BF16)
        ya_ref[...] = ya
        yr_ref[...] = yr
        m_ref[...] = mixed
        x1_ref[...] = x_ref[...] + _dot(mixed, wo_ref[...], NN)

    return pl.pallas_call(
        body, name="branches_fwd", grid=(T // TB,),
        in_specs=[row(D_ATT), row(D), *gate_specs, row(D), w_att, w_sq, w_sq],
        out_specs=[row(D)] * 4,
        out_shape=[_sds((T, D), F32), _sds((T, D), F32), _sds((T, D), BF16), _sds((T, D), F32)],
        compiler_params=_params(("parallel",)))(att, g, z, z, z, z, x, w_att_o, w_rec_o, w_out)


def _branches_bwd(dx1_b, y_att, y_rec, z, w_att_o, w_rec_o, w_out):
    row, gate_specs, w_att, w_sq = _branch_specs()
    n4 = D // N_CHIPS

    def body(dx_ref, ya_ref, yr_ref, ga_ref, ga2_ref, gr_ref, gr2_ref, wa_ref, wr_ref, wo_ref,
             dya_ref, dyr_ref, dga_ref, dgr_ref, datt_ref, dg_ref):
        dm = _dot(dx_ref[...], wo_ref[...], NT)
        sa, sr = _gate_values((ga_ref, ga2_ref, gr_ref, gr2_ref))
        dya = (dm * sa).astype(BF16)
        dyr = (dm * sr).astype(BF16)
        dya_ref[...] = dya
        dyr_ref[...] = dyr
        dga_ref[...] = (dm * ya_ref[...] * sa * (1.0 - sa)).astype(BF16)
        dgr_ref[...] = (dm * yr_ref[...] * sr * (1.0 - sr)).astype(BF16)
        datt = _dot(dya[:, 0:n4], wa_ref[0], NT)
        for j in range(1, N_CHIPS):
            datt = datt + _dot(dya[:, j * n4:(j + 1) * n4], wa_ref[j], NT)
        datt_ref[...] = datt.astype(BF16)
        dg_ref[...] = _dot(dyr, wr_ref[...], NT).astype(BF16)

    return pl.pallas_call(
        body, name="branches_bwd", grid=(T // TB,),
        in_specs=[row(D), row(D), row(D), *gate_specs, w_att, w_sq, w_sq],
        out_specs=[row(D)] * 4 + [row(D_ATT), row(D)],
        out_shape=[_sds((T, D), BF16)] * 4 + [_sds((T, D_ATT), BF16), _sds((T, D), BF16)],
        compiler_params=_params(("parallel",)))(dx1_b, y_att, y_rec, z, z, z, z, w_att_o, w_rec_o, w_out)


HP = 2 * HEAD_DIM
N_HP = N_HEADS // 2
ATT_UNROLL_FWD = 8
ATT_UNROLL_BWD = 4
DIAG_ROWS = 32


def _window_maps():
    diag = np.zeros((GRID_W * GRID_W, 128), np.float32)
    for qc in range(GRID_W):
        w0 = min(max(qc - WIN_W // 2, 0), GRID_W - WIN_W)
        for kc in range(w0, w0 + WIN_W):
            diag[qc * GRID_W + kc, kc - qc + WIN_W - 1] = 1.0
    return diag, diag.sum(axis=1)[None, :]


def _split3(x):
    a = x.astype(BF16)
    r = x - a.astype(F32)
    b = r.astype(BF16)
    c = (r - b.astype(F32)).astype(BF16)
    return a, b, c


N_DROW = 2 * WIN_H - 1
N_DPAIR = N_DROW - 1


def _bias_pairs(rpb):
    diag, valid = _window_maps()
    r2 = jnp.pad(rpb.reshape(N_HEADS * N_DROW, 2 * WIN_W - 1),
                 ((0, 128 - N_HEADS * N_DROW), (0, 128 - (2 * WIN_W - 1))))

    def body(r_ref, d_ref, v_ref, o_ref):
        dv = d_ref[...]
        t = sum(_dot(part, dv, NN) for part in _split3(r_ref[...]))
        o_ref[...] = jnp.where(v_ref[...] > 0.0, t, -1e30)

    t = pl.pallas_call(body, name="rpb_expand", out_shape=_sds((128, GRID_W * GRID_W), F32),
                       compiler_params=_params())(r2, jnp.asarray(diag.T, BF16), jnp.asarray(valid, F32))
    t = t[:N_HEADS * N_DROW].reshape(N_HEADS, N_DROW, GRID_W, GRID_W)
    return jnp.concatenate([t[:, :N_DPAIR], t[:, 1:]], axis=-1)


def _row_bias(tb_ref, hh, d0):
    return jnp.concatenate([tb_ref[hh, d0 + 2 * ii] for ii in range(WIN_H // 2)], axis=1)


def _row_window(r):
    rs = jnp.clip(r - WIN_H // 2, 0, N_ROWS - WIN_H)
    return pl.multiple_of(r * GRID_W, GRID_W), pl.multiple_of(rs * GRID_W, GRID_W), rs - r + (WIN_H - 1)


def _split_heads(src_ref, dst_ref, scale=None):
    for hh in range(2):
        v = src_ref[:, hh * HEAD_DIM:(hh + 1) * HEAD_DIM]
        dst_ref[hh] = (v if scale is None else v * scale).astype(BF16)


def _attn_items(qb_ref, kb_ref, vb_ref, tb_ref, first_row, n_rows):
    wins = [_row_window(first_row + u) for u in range(n_rows)]
    items = [(u, hh) for u in range(n_rows) for hh in range(2)]
    q = [qb_ref[hh, pl.ds(wins[u][0], GRID_W), :] for u, hh in items]
    k = [kb_ref[hh, pl.ds(wins[u][1], KEYS), :] for u, hh in items]
    v = [vb_ref[hh, pl.ds(wins[u][1], KEYS), :] for u, hh in items]
    s = [_dot(qi, ki, NT) + _row_bias(tb_ref, hh, wins[u][2]) for qi, ki, (u, hh) in zip(q, k, items)]
    m = [jnp.max(si, axis=-1, keepdims=True) for si in s]
    e = [jnp.exp(si - mi) for si, mi in zip(s, m)]
    inv = [1.0 / jnp.sum(ei, axis=-1, keepdims=True) for ei in e]
    p = [ei * li for ei, li in zip(e, inv)]
    return wins, items, q, k, v, p


def _attn_in_specs():
    q = pl.BlockSpec((T, HP), lambda p: (0, p))
    k = pl.BlockSpec((T, HP), lambda p: (0, N_HP + p))
    v = pl.BlockSpec((T, HP), lambda p: (0, 2 * N_HP + p))
    tb = pl.BlockSpec((2, N_DPAIR, GRID_W, HP), lambda p: (p, 0, 0, 0))
    return q, k, v, tb


_HEAD_SCRATCH = pltpu.VMEM((2, T, HEAD_DIM), BF16)


def _attn_fwd(z, tb):
    def body(q_ref, k_ref, v_ref, tb_ref, o_ref, qb_ref, kb_ref, vb_ref):
        _split_heads(q_ref, qb_ref, SCALE)
        _split_heads(k_ref, kb_ref)
        _split_heads(v_ref, vb_ref)

        def rows(it, carry):
            wins, items, _, _, v, p = _attn_items(qb_ref, kb_ref, vb_ref, tb_ref, it * ATT_UNROLL_FWD, ATT_UNROLL_FWD)
            o = [_dot(pi.astype(BF16), vi, NN) for pi, vi in zip(p, v)]
            for u, (q0, _, _) in enumerate(wins):
                o_ref[pl.ds(q0, GRID_W), :] = jnp.concatenate(o[2 * u:2 * u + 2], axis=1).astype(BF16)
            return carry

        lax.fori_loop(0, N_ROWS // ATT_UNROLL_FWD, rows, 0)

    blk = pl.BlockSpec((T, HP), lambda p: (0, p))
    return pl.pallas_call(
        body, name="attn_fwd", grid=(N_HP,), in_specs=list(_attn_in_specs()), out_specs=blk,
        out_shape=_sds((T, D_ATT), BF16), scratch_shapes=[_HEAD_SCRATCH] * 3,
        compiler_params=_params(("parallel",)))(z, z, z, tb)


def _attn_bwd(z, tb, d_att, after=()):
    def body(q_ref, k_ref, v_ref, tb_ref, do_ref, flip_ref, *rest):
        (dq_ref, dk_ref, dv_ref, diag_ref, qb_ref, kb_ref, vb_ref, dob_ref, dka_ref, dva_ref,
         ds_ref) = rest[len(after):]
        _split_heads(q_ref, qb_ref, SCALE)
        _split_heads(k_ref, kb_ref)
        _split_heads(v_ref, vb_ref)
        _split_heads(do_ref, dob_ref)
        dka_ref[...] = jnp.zeros_like(dka_ref)
        dva_ref[...] = jnp.zeros_like(dva_ref)
        ds_ref[...] = jnp.zeros_like(ds_ref)

        def rows(it, carry):
            wins, items, q, k, v, p = _attn_items(qb_ref, kb_ref, vb_ref, tb_ref, it * ATT_UNROLL_BWD, ATT_UNROLL_BWD)
            do = [dob_ref[hh, pl.ds(wins[u][0], GRID_W), :] for u, hh in items]
            dv = [_dot(pi.astype(BF16), di, TN) for pi, di in zip(p, do)]
            dp = [_dot(di, vi, NT) for di, vi in zip(do, v)]
            ds = [pi * (dpi - jnp.sum(dpi * pi, axis=-1, keepdims=True)) for pi, dpi in zip(p, dp)]
            dsb = [d.astype(BF16) for d in ds]
            dq = [_dot(d, ki, NN) * SCALE for d, ki in zip(dsb, k)]
            dk = [_dot(d, qi, TN) for d, qi in zip(dsb, q)]
            for d, (u, hh) in zip(ds, items):
                for ii in range(WIN_H // 2):
                    ds_ref[hh, wins[u][2] + 2 * ii] += d[:, ii * HP:(ii + 1) * HP]
            for u, (q0, k0, _) in enumerate(wins):
                dq_ref[pl.ds(q0, GRID_W), :] = jnp.concatenate(dq[2 * u:2 * u + 2], axis=1).astype(BF16)
                dka_ref[pl.ds(k0, KEYS), :] += jnp.concatenate(dk[2 * u:2 * u + 2], axis=1)
                dva_ref[pl.ds(k0, KEYS), :] += jnp.concatenate(dv[2 * u:2 * u + 2], axis=1)
            return carry

        lax.fori_loop(0, N_ROWS // ATT_UNROLL_BWD, rows, 0)
        dk_ref[...] = dka_ref[...].astype(BF16)
        dv_ref[...] = dva_ref[...].astype(BF16)
        _diag_sums(ds_ref, flip_ref, diag_ref)

    blk = pl.BlockSpec((T, HP), lambda p: (0, p))
    q, k, v, tbs = _attn_in_specs()
    flip = jnp.asarray(np.eye(HP, dtype=np.float32)[::-1], BF16)
    return pl.pallas_call(
        body, name="attn_bwd", grid=(N_HP,),
        in_specs=[q, k, v, tbs, blk, pl.BlockSpec((HP, HP), lambda p: (0, 0))] + [_ANY] * len(after),
        out_specs=[blk, blk, blk, pl.BlockSpec((None, DIAG_ROWS, HP), lambda p: (p, 0, 0))],
        out_shape=[_sds((T, D_ATT), BF16)] * 3 + [_sds((N_HP, DIAG_ROWS, HP), F32)],
        scratch_shapes=[_HEAD_SCRATCH] * 4 + [pltpu.VMEM((T, HP), F32), pltpu.VMEM((T, HP), F32),
                                              pltpu.VMEM((2, N_DPAIR, GRID_W, HP), F32)],
        compiler_params=_params(("parallel",)))(z, z, z, tb, d_att, flip, *after)


def _diag_sums(acc_ref, flip_ref, out_ref):
    flip = flip_ref[...]
    rows = []
    for hh in range(2):
        for pair in range(N_DPAIR):
            reversed_lanes = sum(_dot(part, flip, NN) for part in _split3(acc_ref[hh, pair]))
            skewed = pltpu.roll(reversed_lanes, 0, 1, stride=1, stride_axis=0)
            rows.append(jnp.sum(skewed, axis=0, keepdims=True))
    rows.append(jnp.zeros((DIAG_ROWS - len(rows), HP), F32))
    out_ref[...] = jnp.concatenate(rows, axis=0)


def _rpb_grad(diag_sums):
    g = diag_sums.reshape(N_HP * DIAG_ROWS, HP)
    sel = np.zeros((2, 128, N_HP * DIAG_ROWS), np.float32)
    lane = np.zeros((2, HP, 128), np.float32)
    for h in range(N_HEADS):
        for pair in range(N_DPAIR):
            for half in range(2):
                sel[half, h * N_DROW + pair + half, (h // 2) * DIAG_ROWS + (h % 2) * N_DPAIR + pair] = 1.0
    for j in range(2 * WIN_W - 1):
        for half in range(2):
            lane[half, (HP - 1 - GRID_W * half - (j - (WIN_W - 1))) % HP, j] = 1.0

    def body(g_ref, sel_ref, lane_ref, o_ref):
        parts = _split3(g_ref[...])
        total = None
        for half in range(2):
            picked = sum(_dot(sel_ref[half], part, NN) for part in parts)
            term = sum(_dot(part, lane_ref[half], NN) for part in _split3(picked))
            total = term if total is None else total + term
        o_ref[...] = total

    out = pl.pallas_call(body, name="rpb_grad", out_shape=_sds((128, 128), F32),
                         compiler_params=_params())(g, jnp.asarray(sel, BF16), jnp.asarray(lane, BF16))
    return out[:N_HEADS * N_DROW, :2 * WIN_W - 1].reshape(N_HEADS, N_DROW, 2 * WIN_W - 1)


N_CB = D // REC_CB
N_CHUNK = T // REC_CHUNK
N_TILE = T // 8
_U_BLK = 1536 // REC_CB
_Y_BLK = 2560 // REC_CB


def _block_diag(w):
    per = REC_CB // 64
    wt = w.reshape(2, N_CB, per, 64, 64)
    eye = jnp.eye(per, dtype=w.dtype)
    full = wt[:, :, :, :, None, :] * eye[None, None, :, None, :, None]
    return full.reshape(2, N_CB, REC_CB, REC_CB).astype(BF16)


def _block_diag_grad(g):
    per = REC_CB // 64
    g6 = g.reshape(2, N_CB, per, 64, per, 64)
    return jnp.stack([g6[:, :, p, :, p, :] for p in range(per)], axis=2).reshape(2, 16, 64, 64)


def _gelu(x):
    c = 0.7978845608028654
    return 0.5 * x * (1.0 + jnp.tanh(c * (x + 0.044715 * x * x * x)))


def _gelu_grad(x):
    c = 0.7978845608028654
    th = jnp.tanh(c * (x + 0.044715 * x * x * x))
    return 0.5 * (1.0 + th) + 0.5 * x * (1.0 - th * th) * c * (1.0 + 3.0 * 0.044715 * x * x)


def _softplus_neg(lam):
    x = -lam
    e = jnp.exp(-jnp.abs(x))
    w = 1.0 + e
    l1p = jnp.where(w == 1.0, e, jnp.log(w) * e / (w - 1.0))
    return jnp.maximum(x, 0.0) + l1p


def _one_minus_exp(x):
    poly = x * (1.0 + x * (1 / 2 + x * (1 / 6 + x * (1 / 24 + x * (1 / 120 + x * (1 / 720))))))
    return jnp.where(x > -0.125, -poly, 1.0 - jnp.exp(x))


def _conv_taps(pad_ref, t0, w, sign):
    out = None
    for j in range(4):
        term = w[j:j + 1, :] * pad_ref[pl.ds(PAD + t0 + sign * (j - 2), REC_CHUNK), :]
        out = term if out is None else out + term
    return out


def _gates(u, wa, wi, ba, bi, sp):
    ub = u.astype(BF16)
    r = _sigmoid(_dot(ub, wa, NN) + ba)
    i = _sigmoid(_dot(ub, wi, NN) + bi)
    log_a = -LRU_C * r * sp
    a = jnp.exp(log_a)
    x = jnp.maximum(_one_minus_exp(2.0 * log_a), 0.0)
    positive = x > 0.0
    inv = lax.rsqrt(jnp.where(positive, x, 1.0))
    mult = jnp.where(positive, x * inv, 0.0)
    return r, i, a, mult, jnp.where(positive, inv, 0.0)


def _tile_scan(a, b, sub, reverse):
    for s in (1, 2, 4):
        if reverse:
            a_s, b_s, m = pltpu.roll(a, 8 - s, 0), pltpu.roll(b, 8 - s, 0), sub < 8 - s
        else:
            a_s, b_s, m = pltpu.roll(a, s, 0), pltpu.roll(b, s, 0), sub >= s
        b = jnp.where(m, a * b_s + b, b)
        a = jnp.where(m, a * a_s, a)
    return a, b


def _last_row(x, sub, row):
    return jnp.broadcast_to(jnp.sum(jnp.where(sub == row, x, 0.0), axis=0, keepdims=True), x.shape)


def _rec_prologue(up_ref, cw_ref, cb_ref, wa_ref, wi_ref, ba_ref, bi_ref, lam_ref,
                  upad_ref, u_ref, a_refs, h_refs):
    cb = up_ref.shape[1]
    zeros = jnp.zeros((PAD, cb), F32)
    upad_ref[pl.ds(0, PAD), :] = zeros
    upad_ref[pl.ds(PAD + T, PAD), :] = zeros
    upad_ref[pl.ds(PAD, T), :] = up_ref[...]
    cw = cw_ref[...]
    sp = _softplus_neg(lam_ref[...])
    for c in range(N_CHUNK):
        t0 = c * REC_CHUNK
        u = cb_ref[...] + _conv_taps(upad_ref, t0, cw, 1)
        u_ref[pl.ds(t0, REC_CHUNK), :] = u
        for d in range(2):
            _, i, a, mult, _ = _gates(u, wa_ref[d], wi_ref[d], ba_ref[d:d + 1, :], bi_ref[d:d + 1, :], sp[d:d + 1, :])
            a_refs[d][pl.ds(t0, REC_CHUNK), :] = a
            h_refs[d][pl.ds(t0, REC_CHUNK), :] = mult * (i * u)

    sub = lax.broadcasted_iota(jnp.int32, (8, cb), 0)

    def tile(k, carry):
        cf, cr = carry
        tf = pl.multiple_of(k * 8, 8)
        tr = pl.multiple_of((N_TILE - 1 - k) * 8, 8)
        af, bf = _tile_scan(a_refs[0][pl.ds(tf, 8), :], h_refs[0][pl.ds(tf, 8), :], sub, False)
        hf = af * cf + bf
        h_refs[0][pl.ds(tf, 8), :] = hf
        ar, br = _tile_scan(a_refs[1][pl.ds(tr, 8), :], h_refs[1][pl.ds(tr, 8), :], sub, True)
        hr = ar * cr + br
        h_refs[1][pl.ds(tr, 8), :] = hr
        return _last_row(hf, sub, 7), _last_row(hr, sub, 0)

    z8 = jnp.zeros((8, cb), F32)
    lax.fori_loop(0, N_TILE, tile, (z8, z8))
    return sp


def _rec_specs():
    up = pl.BlockSpec((T, REC_CB), lambda c: (0, _U_BLK + c))
    yb = pl.BlockSpec((T, REC_CB), lambda c: (0, _Y_BLK + c))
    cw = pl.BlockSpec((4, REC_CB), lambda c: (0, c))
    cbias = pl.BlockSpec((1, REC_CB), lambda c: (0, c))
    wbd = pl.BlockSpec((2, None, REC_CB, REC_CB), lambda c: (0, c, 0, 0))
    vec2 = pl.BlockSpec((2, REC_CB), lambda c: (0, c))
    col = pl.BlockSpec((T, REC_CB), lambda c: (0, c))
    return up, yb, cw, cbias, wbd, vec2, col


def _rec_fwd(z, conv_w, conv_b, wa, wi, ba, bi, lam):
    up, yb, cw, cbias, wbd, vec2, col = _rec_specs()

    def body(up_ref, yb_ref, cw_ref, cb_ref, wa_ref, wi_ref, ba_ref, bi_ref, lam_ref, g_ref,
             u_ref, af_ref, ar_ref, hf_ref, hr_ref, upad_ref):
        _rec_prologue(up_ref, cw_ref, cb_ref, wa_ref, wi_ref, ba_ref, bi_ref, lam_ref,
                      upad_ref, u_ref, (af_ref, ar_ref), (hf_ref, hr_ref))

        def chunk(c, carry):
            t0 = pl.multiple_of(c * REC_CHUNK, REC_CHUNK)
            rows = pl.ds(t0, REC_CHUNK)
            g_ref[rows, :] = ((hf_ref[rows, :] + hr_ref[rows, :]) * _gelu(yb_ref[rows, :])).astype(BF16)
            return carry

        lax.fori_loop(0, N_CHUNK, chunk, 0)

    res = pl.pallas_call(
        body, name="rec_fwd", grid=(N_CB,),
        in_specs=[up, yb, cw, cbias, wbd, wbd, vec2, vec2, vec2], out_specs=[col] * 6,
        out_shape=[_sds((T, D), BF16)] + [_sds((T, D), F32)] * 5,
        scratch_shapes=[pltpu.VMEM((T + 2 * PAD, REC_CB), F32)],
        compiler_params=_params(("parallel",)))(z, z, conv_w, conv_b, wa, wi, ba, bi, lam)
    return res[0], tuple(res[1:])


def _rec_bwd(z, dg, saved, conv_w, conv_b, wa, wi, ba, bi, lam, after=()):
    up, yb, cw, cbias, wbd, vec2, col = _rec_specs()

    def body(up_ref, yb_ref, dg_ref, u_ref, af_ref, ar_ref, hf_ref, hr_ref,
             cw_ref, cb_ref, wa_ref, wi_ref, ba_ref, bi_ref, lam_ref, *rest):
        (dup_ref, dyb_ref, dcw_ref, dcb_ref, dwa_ref, dwi_ref, dba_ref, dbi_ref, dlam_ref,
         upad_ref, dh_ref, gf_ref, gr_ref, daf_ref, dar_ref, dupad_ref) = rest[len(after):]
        g_refs, da_refs = (gf_ref, gr_ref), (daf_ref, dar_ref)
        cb = up_ref.shape[1]
        zeros = jnp.zeros((PAD, cb), F32)
        upad_ref[pl.ds(0, PAD), :] = zeros
        upad_ref[pl.ds(PAD + T, PAD), :] = zeros
        upad_ref[pl.ds(PAD, T), :] = up_ref[...]
        sp = _softplus_neg(lam_ref[...])

        def gate_chunk(c, carry):
            t0 = pl.multiple_of(c * REC_CHUNK, REC_CHUNK)
            rows = pl.ds(t0, REC_CHUNK)
            y = yb_ref[rows, :]
            dgv = dg_ref[rows, :].astype(F32)
            dh_ref[rows, :] = dgv * _gelu(y)
            dyb_ref[rows, :] = (dgv * (hf_ref[rows, :] + hr_ref[rows, :]) * _gelu_grad(y)).astype(BF16)
            return carry

        lax.fori_loop(0, N_CHUNK, gate_chunk, 0)

        sub = lax.broadcasted_iota(jnp.int32, (8, cb), 0)

        def tile(k, carry):
            cf, cr = carry
            kf = N_TILE - 1 - k
            tf = pl.multiple_of(kf * 8, 8)
            tnext = pl.multiple_of(jnp.minimum(kf + 1, N_TILE - 1) * 8, 8)
            tprev = pl.multiple_of(jnp.maximum(kf - 1, 0) * 8, 8)
            a_t = af_ref[pl.ds(tf, 8), :]
            a_n = jnp.where(kf < N_TILE - 1, af_ref[pl.ds(tnext, 8), :], 0.0)
            a_sh = jnp.where(sub == 7, pltpu.roll(a_n, 7, 0), pltpu.roll(a_t, 7, 0))
            ca, cbb = _tile_scan(a_sh, dh_ref[pl.ds(tf, 8), :], sub, True)
            gf = ca * cf + cbb
            h_t = hf_ref[pl.ds(tf, 8), :]
            h_p = jnp.where(kf > 0, hf_ref[pl.ds(tprev, 8), :], 0.0)
            h_sh = jnp.where(sub == 0, pltpu.roll(h_p, 1, 0), pltpu.roll(h_t, 1, 0))
            gf_ref[pl.ds(tf, 8), :] = gf
            daf_ref[pl.ds(tf, 8), :] = gf * h_sh
            tr = pl.multiple_of(k * 8, 8)
            rnext = pl.multiple_of(jnp.minimum(k + 1, N_TILE - 1) * 8, 8)
            rprev = pl.multiple_of(jnp.maximum(k - 1, 0) * 8, 8)
            b_t = ar_ref[pl.ds(tr, 8), :]
            b_p = jnp.where(k > 0, ar_ref[pl.ds(rprev, 8), :], 0.0)
            b_sh = jnp.where(sub == 0, pltpu.roll(b_p, 1, 0), pltpu.roll(b_t, 1, 0))
            ra, rb = _tile_scan(b_sh, dh_ref[pl.ds(tr, 8), :], sub, False)
            gr = ra * cr + rb
            hr_t = hr_ref[pl.ds(tr, 8), :]
            hr_n = jnp.where(k < N_TILE - 1, hr_ref[pl.ds(rnext, 8), :], 0.0)
            hr_sh = jnp.where(sub == 7, pltpu.roll(hr_n, 7, 0), pltpu.roll(hr_t, 7, 0))
            gr_ref[pl.ds(tr, 8), :] = gr
            dar_ref[pl.ds(tr, 8), :] = gr * hr_sh
            return _last_row(gf, sub, 0), _last_row(gr, sub, 7)

        z8 = jnp.zeros((8, cb), F32)
        lax.fori_loop(0, N_TILE, tile, (z8, z8))

        dupad_ref[pl.ds(0, PAD), :] = zeros
        dupad_ref[pl.ds(PAD + T, PAD), :] = zeros
        dwa_ref[...] = jnp.zeros_like(dwa_ref)
        dwi_ref[...] = jnp.zeros_like(dwi_ref)
        dba_ref[...] = jnp.zeros_like(dba_ref)
        dbi_ref[...] = jnp.zeros_like(dbi_ref)
        dlam_ref[...] = jnp.zeros_like(dlam_ref)

        def grad_chunk(c, carry):
            t0 = pl.multiple_of(c * REC_CHUNK, REC_CHUNK)
            rows = pl.ds(t0, REC_CHUNK)
            u = u_ref[rows, :]
            ub = u.astype(BF16)
            du = jnp.zeros((REC_CHUNK, cb), F32)
            for d in range(2):
                r, i, a, mult, inv_mult = _gates(u, wa_ref[d], wi_ref[d], ba_ref[d:d + 1, :], bi_ref[d:d + 1, :],
                                                 sp[d:d + 1, :])
                dbx = g_refs[d][rows, :]
                dmult = dbx * (i * u)
                diu = dbx * mult
                a2 = a * a
                dlog = da_refs[d][rows, :] * a - dmult * (a2 * inv_mult)
                dpa = (dlog * (-LRU_C) * sp[d:d + 1, :]) * r * (1.0 - r)
                dpi = (diu * u) * i * (1.0 - i)
                dpab, dpib = dpa.astype(BF16), dpi.astype(BF16)
                du = du + diu * i + _dot(dpab, wa_ref[d], NT) + _dot(dpib, wi_ref[d], NT)
                dwa_ref[d] += _dot(ub, dpab, TN)
                dwi_ref[d] += _dot(ub, dpib, TN)
                dba_ref[d:d + 1, :] += jnp.sum(dpa, axis=0, keepdims=True)
                dbi_ref[d:d + 1, :] += jnp.sum(dpi, axis=0, keepdims=True)
                dlam_ref[d:d + 1, :] += jnp.sum(dlog * r, axis=0, keepdims=True)
            dupad_ref[pl.ds(PAD + t0, REC_CHUNK), :] = du
            return carry

        lax.fori_loop(0, N_CHUNK, grad_chunk, 0)
        dlam_ref[...] = dlam_ref[...] * (LRU_C * _sigmoid(-lam_ref[...]))

        cw = cw_ref[...]
        dcb = jnp.zeros((1, cb), F32)
        dcw = [jnp.zeros((1, cb), F32) for _ in range(4)]
        for c in range(N_CHUNK):
            t0 = c * REC_CHUNK
            du = dupad_ref[pl.ds(PAD + t0, REC_CHUNK), :]
            dcb = dcb + jnp.sum(du, axis=0, keepdims=True)
            for j in range(4):
                dcw[j] = dcw[j] + jnp.sum(du * upad_ref[pl.ds(PAD + t0 + j - 2, REC_CHUNK), :], axis=0, keepdims=True)
            dup_ref[pl.ds(t0, REC_CHUNK), :] = _conv_taps(dupad_ref, t0, cw, -1).astype(BF16)
        dcb_ref[...] = dcb
        dcw_ref[...] = jnp.concatenate(dcw, axis=0)

    full = pltpu.VMEM((T, REC_CB), F32)
    padded = pltpu.VMEM((T + 2 * PAD, REC_CB), F32)
    return pl.pallas_call(
        body, name="rec_bwd", grid=(N_CB,),
        in_specs=[up, yb] + [col] * 6 + [cw, cbias, wbd, wbd, vec2, vec2, vec2] + [_ANY] * len(after),
        out_specs=[col, col, cw, cbias, wbd, wbd, vec2, vec2, vec2],
        out_shape=[_sds((T, D), BF16), _sds((T, D), BF16), _sds((4, D), F32), _sds((1, D), F32),
                   _sds((2, N_CB, REC_CB, REC_CB), F32), _sds((2, N_CB, REC_CB, REC_CB), F32),
                   _sds((2, D), F32), _sds((2, D), F32), _sds((2, D), F32)],
        scratch_shapes=[padded, full, full, full, full, full, padded],
        compiler_params=_params(("parallel",)))(z, z, dg, *saved, conv_w, conv_b, wa, wi, ba, bi, lam, *after)


class _NoReducer:
    def begin(self, tag, grads):
        return ()

    def advance(self, tag, after):
        return ()


def _local_step(x, target, p, late=None, reducer=_NoReducer()):
    x = x.reshape(T, D)
    target = target.reshape(T, D)
    tb = _bias_pairs(p["rpb"])
    wa, wi = _block_diag(p["w_rg_a"]), _block_diag(p["w_rg_i"])

    h1 = _rms_fwd("rms1_fwd", x, p["ln1_g"], after=late[0] if late else ())
    if late:
        p = {**p, **late[1](h1)}
    rec_params = (p["conv_w"], p["conv_b"], wa, wi, p["b_rg_a"], p["b_rg_i"], p["lru_lambda"])
    (z,) = _mm_nn_cols("mm_z", h1, p["w_in"], F32, bias=p["b_in"])
    att = _attn_fwd(z, tb)
    g, rec_saved = _rec_fwd(z, *rec_params)
    if late:
        p = {**p, **late[2](g)}
    y_att, y_rec, mixed, x1 = _branches_fwd(att, g, z, x, p["w_att_o"], p["w_rec_o"], p["w_out"])

    def add_res(r, ex, outs):
        outs[0][...] = ex[0][...] + r

    res_spec = pl.BlockSpec((TM, D), lambda i, j, k: (i, 0))
    h2 = _rms_fwd("rms2_fwd", x1, p["ln2_g"])

    def relu2(r, ex, outs):
        rp = jnp.maximum(r, 0.0)
        outs[0][...] = (rp * rp).astype(BF16)

    (s,) = _mm_nn_cols("mm_ff1", h2, p["w_ff1"], BF16, epilogue=relu2)
    (x2,) = _mm_nn_rows("mm_x2", s, p["w_ff2"], F32, tk=D, extras=[x1], extra_specs=[res_spec], epilogue=add_res)
    loss, dx2, dx2_b, g_lnf = _loss_head(x2, target, p["lnf_g"])

    def relu2_bwd(r, ex, outs):
        outs[0][...] = (r * 2.0 * jnp.sqrt(ex[0][...].astype(F32))).astype(BF16)

    (df,) = _mm_nt_rows("mm_df", dx2_b, p["w_ff2"], BF16, tn=D, extras=[s],
                        extra_specs=[pl.BlockSpec((TM, D), lambda j, i, k: (i, j))], epilogue=relu2_bwd)
    (g_w_ff2,) = _mm_tn_rows("mm_g_ff2", s, dx2_b, tm=D)
    (g_w_ff1,) = _mm_tn_cols("mm_g_ff1", h2, df, D)
    tok = reducer.begin("ff", dict(w_ff2=g_w_ff2, w_ff1=g_w_ff1))
    dx1, dx1_b, g_ln2 = _mm_nt_cols_rms_bwd("mm_dh2_rms2_bwd", df, p["w_ff1"], x1, p["ln2_g"], dx2, after=tok,
                                            bf16_copy=True)

    dy_att, dy_rec, dg_att, dg_rec, d_att, d_g = _branches_bwd(dx1_b, y_att, y_rec, z, p["w_att_o"], p["w_rec_o"],
                                                               p["w_out"])
    (g_w_out,) = _mm_tn_rows("mm_g_out", mixed, dx1_b, tm=D)
    (g_w_att_o,) = _mm_tn_cols("mm_g_att_o", att, dy_att, D // N_CHIPS)
    (g_w_rec_o,) = _mm_tn_rows("mm_g_rec_o", g, dy_rec, tm=D)
    tok = reducer.advance("ff", g_w_rec_o) + reducer.begin("proj", dict(w_out=g_w_out, w_att_o=g_w_att_o, w_rec_o=g_w_rec_o))

    dq, dk, dv, ds_acc = _attn_bwd(z, tb, d_att, after=tok)
    g_rpb = _rpb_grad(ds_acc)
    tok = reducer.advance("proj", dq)
    d_up, d_yb, g_conv_w, g_conv_b, g_wa, g_wi, g_ba, g_bi, g_lam = _rec_bwd(z, d_g, rec_saved, *rec_params, after=tok)
    dz = jnp.concatenate([dq, dk, dv, d_up, d_yb, dg_att, dg_rec], axis=1)

    g_w_in, g_b_in = _mm_tn_cols("mm_g_in", h1, dz, D_IN // N_CHIPS, colsum=True)
    tok = reducer.begin("in", dict(w_in=g_w_in))
    grad_x, g_ln1 = _mm_nt_cols_rms_bwd("mm_dh1_rms1_bwd", dz, p["w_in"], x, p["ln1_g"], dx1, after=tok)
    reducer.advance("in", grad_x)

    grads = dict(ln1_g=g_ln1, w_in=g_w_in, b_in=g_b_in, rpb=g_rpb, w_att_o=g_w_att_o, conv_w=g_conv_w,
                 conv_b=g_conv_b, w_rg_a=_block_diag_grad(g_wa), b_rg_a=g_ba, w_rg_i=_block_diag_grad(g_wi),
                 b_rg_i=g_bi, lru_lambda=g_lam, w_rec_o=g_w_rec_o, w_out=g_w_out, ln2_g=g_ln2,
                 w_ff1=g_w_ff1, w_ff2=g_w_ff2, lnf_g=g_lnf)
    return loss, grad_x.reshape(1, T, D), grads


_ANY = pl.BlockSpec(memory_space=pl.ANY)
N_PEERS = N_CHIPS - 1


def _place():
    x, y, c = lax.axis_index("x"), lax.axis_index("y"), lax.axis_index("c")
    peers = [(1 - x, y), (x, 1 - y), (1 - x, 1 - y)]
    return x, y, c, 2 * x + y, peers


def _remote(src, dst, send_sem, recv_sem, dev):
    return pltpu.make_async_remote_copy(src_ref=src, dst_ref=dst, send_sem=send_sem, recv_sem=recv_sem,
                                        device_id=dev, device_id_type=MESH)


def _prefetch_call(body, name, ids, grid, in_specs, out_specs, out_shape, args, semantics=None):
    spec = pltpu.PrefetchScalarGridSpec(num_scalar_prefetch=1, grid=grid, in_specs=in_specs, out_specs=out_specs)
    return pl.pallas_call(body, name=name, grid_spec=spec, out_shape=out_shape,
                          compiler_params=_params(semantics or ("parallel",) * len(grid)))(ids, *args)


def _cast_bf16(name, w, chip_id, after=()):
    rows, cols = w.shape
    rb = min(rows, 256)

    def body(ids_ref, w_ref, *rest):
        rest[-1][...] = w_ref[...].astype(BF16)

    return _prefetch_call(body, name, chip_id, (rows // rb,),
                          [pl.BlockSpec((rb, cols), lambda i, ids: (i, 0))] + [_ANY] * len(after),
                          pl.BlockSpec((None, rb, cols), lambda i, ids: (ids[0], i, 0)),
                          _sds((N_CHIPS, rows, cols), BF16), (w, *after))


def _dma_sems(*counts):
    return [pltpu.SemaphoreType.DMA((k,)) for k in counts]


_HBM = pl.BlockSpec(memory_space=pltpu.HBM)
_SEM = pl.BlockSpec(memory_space=pltpu.SEMAPHORE)
_SPLIT_COPY = pltpu.CompilerParams(has_side_effects=pltpu.SideEffectType.DATAFLOW_SIDE_EFFECTING)


def _hbm(arrays):
    return [pltpu.with_memory_space_constraint(a, pltpu.HBM) for a in arrays]


def _hbm_like(arrays):
    return [pltpu.HBM(a.shape, a.dtype) for a in arrays]


def _halves(buf, c):
    half = buf.shape[1] // 2
    return pl.ds(c * half, half), pl.ds((1 - c) * half, half)


def _gather_start(name, slots):
    n = len(slots)
    nk = n * N_PEERS

    def body(*refs):
        bufs = refs[n:2 * n]
        send_sems, recv_sems, token = refs[2 * n:]
        x, y, c, chip, peers = _place()
        for t in range(n):
            mine, _ = _halves(bufs[t], c)
            for r, (px, py) in enumerate(peers):
                k = t * N_PEERS + r
                own = bufs[t].at[chip, mine]
                _remote(own, own, send_sems.at[k], recv_sems.at[k], (px, py, c)).start()
        token[...] = jnp.zeros_like(token)

    res = pl.pallas_call(
        body, name=name, in_specs=[_HBM] * n, out_specs=[_HBM] * n + [_SEM, _SEM, pl.BlockSpec(memory_space=pltpu.VMEM)],
        out_shape=_hbm_like(slots) + [pltpu.SemaphoreType.DMA((nk,)), pltpu.SemaphoreType.DMA((nk,)),
                                      _sds((8, 128), F32)],
        input_output_aliases={t: t for t in range(n)}, compiler_params=_SPLIT_COPY)(*_hbm(slots))
    return res[:n], (res[n], res[n + 1]), res[n + 2]


def _gather_wait(name, bufs, sems, after):
    n = len(bufs)

    def body(*refs):
        ins = refs[:n]
        send_sems, recv_sems = refs[n], refs[n + 1]
        x, y, c, chip, peers = _place()
        for t in range(n):
            mine, _ = _halves(ins[t], c)
            for r, (px, py) in enumerate(peers):
                k = t * N_PEERS + r
                cp = _remote(ins[t].at[chip, mine], ins[t].at[2 * px + py, mine], send_sems.at[k], recv_sems.at[k],
                             (px, py, c))
                cp.wait_send()
                cp.wait_recv()

    return pl.pallas_call(
        body, name=name, in_specs=[_HBM] * n + [_SEM, _SEM, _ANY], out_specs=[_HBM] * n, out_shape=_hbm_like(bufs),
        input_output_aliases={t: t for t in range(n)}, compiler_params=_SPLIT_COPY)(*bufs, *sems, after)


def _gather_forward(name, bufs):
    n = len(bufs)
    nk = n * N_PEERS

    def body(*refs):
        outs = refs[n:2 * n]
        send_sems, recv_sems = refs[2 * n:]
        x, y, c, chip, peers = _place()
        sibling = (x, y, 1 - c)
        sends = []
        for t in range(n):
            mine, _ = _halves(outs[t], c)
            for r, (px, py) in enumerate(peers):
                k = t * N_PEERS + r
                landed = outs[t].at[2 * px + py, mine]
                sends.append(_remote(landed, landed, send_sems.at[k], recv_sems.at[k], sibling))
                sends[-1].start()
        for t in range(n):
            _, theirs = _halves(outs[t], c)
            for r, (px, py) in enumerate(peers):
                k = t * N_PEERS + r
                landed = outs[t].at[2 * px + py, theirs]
                _remote(landed, landed, send_sems.at[k], recv_sems.at[k], sibling).wait_recv()
        for cp in sends:
            cp.wait_send()

    return pl.pallas_call(
        body, name=name, in_specs=[_ANY] * n, out_specs=[_ANY] * n, out_shape=[_sds(b.shape, b.dtype) for b in bufs],
        input_output_aliases={t: t for t in range(n)}, scratch_shapes=_dma_sems(nk, nk))(*bufs)


def _pair_copies(n, srcs, lands, send_sems, recv_sems):
    x, y, c, _, _ = _place()
    sibling = (x, y, 1 - c)
    copies = []
    for t in range(n):
        half = srcs[t].shape[1] // 2
        for j in range(N_CHIPS):
            k = t * N_CHIPS + j
            copies.append(_remote(srcs[t].at[j, pl.ds((1 - c) * half, half)], lands[t].at[j],
                                  send_sems.at[k], recv_sems.at[k], sibling))
    for t in range(n, len(srcs)):
        k = n * N_CHIPS + t - n
        copies.append(_remote(srcs[t], lands[t], send_sems.at[k], recv_sems.at[k], sibling))
    return copies


def _pair_start(name, grads, wholes=()):
    n = len(grads)
    srcs = list(grads) + list(wholes)
    m = len(srcs)
    lands = [pltpu.HBM((N_CHIPS, g.shape[1] // 2, g.shape[2]), F32) for g in grads] + _hbm_like(wholes)
    ns = n * N_CHIPS + len(wholes)

    def body(*refs):
        src_refs, land_refs = refs[m:2 * m], refs[2 * m:3 * m]
        send_sems, recv_sems, token = refs[3 * m:]
        for cp in _pair_copies(n, src_refs, land_refs, send_sems, recv_sems):
            cp.start()
        token[...] = jnp.zeros_like(token)

    res = pl.pallas_call(
        body, name=name, in_specs=[_HBM] * m,
        out_specs=[_HBM] * (2 * m) + [_SEM, _SEM, pl.BlockSpec(memory_space=pltpu.VMEM)],
        out_shape=_hbm_like(srcs) + lands + [pltpu.SemaphoreType.DMA((ns,)), pltpu.SemaphoreType.DMA((ns,)),
                                             _sds((8, 128), F32)],
        input_output_aliases={t: t for t in range(m)}, compiler_params=_SPLIT_COPY)(*_hbm(srcs))
    return (res[:m], res[m:2 * m], (res[2 * m], res[2 * m + 1])), res[2 * m + 2]


def _pair_wait(name, flight, n, after):
    srcs, lands, sems = flight
    m = len(srcs)

    def body(*refs):
        for cp in _pair_copies(n, refs[:m], refs[m:2 * m], refs[2 * m], refs[2 * m + 1]):
            cp.wait_send()
            cp.wait_recv()

    res = pl.pallas_call(
        body, name=name, in_specs=[_HBM] * (2 * m) + [_SEM, _SEM, _ANY], out_specs=[_HBM] * (2 * m),
        out_shape=_hbm_like(srcs) + _hbm_like(lands), input_output_aliases={t: t for t in range(2 * m)},
        compiler_params=_SPLIT_COPY)(*srcs, *lands, *sems, after)
    return res[:m], res[m:]


def _chip_copies(srcs, lands, small_src, small_land, send_sems, recv_sems):
    x, y, c, chip, peers = _place()
    n = len(srcs)
    copies = []
    for r, (px, py) in enumerate(peers):
        for t in range(n):
            k = t * N_PEERS + r
            copies.append(_remote(srcs[t].at[2 * px + py], lands[t].at[r], send_sems.at[k], recv_sems.at[k], (px, py, c)))
        if small_src is not None:
            k = n * N_PEERS + r
            half_s = small_src.shape[0] // 2
            copies.append(_remote(small_src.at[pl.ds(c * half_s, half_s)], small_land.at[r],
                                  send_sems.at[k], recv_sems.at[k], (px, py, c)))
    return copies


def _chip_start(name, sums_bf16, small=None):
    n = len(sums_bf16)
    srcs = list(sums_bf16) + ([small] if small is not None else [])
    m = len(srcs)
    lands = [pltpu.HBM((N_PEERS,) + s.shape[1:], BF16) for s in sums_bf16]
    if small is not None:
        lands.append(pltpu.HBM((N_PEERS, small.shape[0] // 2, 128), F32))
    nk = m * N_PEERS

    def body(*refs):
        src_refs, land_refs = refs[m:2 * m], refs[2 * m:3 * m]
        send_sems, recv_sems, token = refs[3 * m:]
        small_src, small_land = (src_refs[n], land_refs[n]) if small is not None else (None, None)
        for cp in _chip_copies(src_refs[:n], land_refs[:n], small_src, small_land, send_sems, recv_sems):
            cp.start()
        token[...] = jnp.zeros_like(token)

    res = pl.pallas_call(
        body, name=name, in_specs=[_HBM] * m,
        out_specs=[_HBM] * (2 * m) + [_SEM, _SEM, pl.BlockSpec(memory_space=pltpu.VMEM)],
        out_shape=_hbm_like(srcs) + lands + [pltpu.SemaphoreType.DMA((nk,)), pltpu.SemaphoreType.DMA((nk,)),
                                             _sds((8, 128), F32)],
        input_output_aliases={t: t for t in range(m)}, compiler_params=_SPLIT_COPY)(*_hbm(srcs))
    return (res[:m], res[m:2 * m], (res[2 * m], res[2 * m + 1])), res[2 * m + 2]


def _chip_wait(name, flight, with_small, after):
    srcs, lands, sems = flight
    m = len(srcs)
    n = m - 1 if with_small else m

    def body(*refs):
        src_refs, land_refs = refs[:m], refs[m:2 * m]
        send_sems, recv_sems = refs[2 * m], refs[2 * m + 1]
        small_src, small_land = (src_refs[n], land_refs[n]) if with_small else (None, None)
        for cp in _chip_copies(src_refs[:n], land_refs[:n], small_src, small_land, send_sems, recv_sems):
            cp.wait_send()
            cp.wait_recv()

    res = pl.pallas_call(
        body, name=name, in_specs=[_HBM] * (2 * m) + [_SEM, _SEM, _ANY], out_specs=[_HBM] * (2 * m),
        out_shape=_hbm_like(srcs) + _hbm_like(lands), input_output_aliases={t: t for t in range(2 * m)},
        compiler_params=_SPLIT_COPY)(*srcs, *lands, *sems, after)
    return res[:m], res[m:]


def _half_swap(name, bufs):
    n = len(bufs)

    def body(*refs):
        outs = refs[n:2 * n]
        send_sem, recv_sem = refs[2 * n:]
        x, y, c, _, _ = _place()
        sibling = (x, y, 1 - c)
        copies = []
        for t in range(n):
            h = outs[t].shape[0] // 2
            mine = outs[t].at[pl.ds(c * h, h)]
            copies.append(_remote(mine, mine, send_sem.at[t], recv_sem.at[t], sibling))
            copies[-1].start()
        for t in range(n):
            h = outs[t].shape[0] // 2
            theirs = outs[t].at[pl.ds((1 - c) * h, h)]
            _remote(theirs, theirs, send_sem.at[t], recv_sem.at[t], sibling).wait_recv()
        for cp in copies:
            cp.wait_send()

    return pl.pallas_call(
        body, name=name, in_specs=[_ANY] * n, out_specs=[_ANY] * n,
        out_shape=[_sds(b.shape, b.dtype) for b in bufs], input_output_aliases={t: t for t in range(n)},
        scratch_shapes=_dma_sems(n, n))(*bufs)


def _pair_sum(name, grad, got, ids):
    _, rows, cols = got.shape
    rb = min(rows, 256)
    nb = rows // rb
    blk = pl.BlockSpec((None, rb, cols), lambda i, j, ids: (j, i, 0))
    mine = pl.BlockSpec((None, rb, cols), lambda i, j, ids: (j, ids[1] * nb + i, 0))
    own = pl.BlockSpec((rb, cols), lambda i, j, ids: (i, 0))

    def body(ids_ref, a_ref, b_ref, s_ref, sb_ref):
        s = a_ref[...] + b_ref[...]
        sb_ref[...] = s.astype(BF16)

        @pl.when(pl.program_id(1) == ids_ref[0])
        def _():
            s_ref[...] = s

    return _prefetch_call(body, name, ids, (nb, N_CHIPS), [mine, blk], [own, blk],
                          [_sds((rows, cols), F32), _sds(got.shape, BF16)], (grad, got),
                          semantics=("parallel", "arbitrary"))


def _chip_sum(name, own_sum, got, ids):
    rows, cols = own_sum.shape
    rb = min(rows, 256)
    nb = rows // rb
    own = pl.BlockSpec((rb, cols), lambda i, ids: (i, 0))
    blk3 = pl.BlockSpec((N_PEERS, rb, cols), lambda i, ids: (0, i, 0))
    out = pl.BlockSpec((rb, cols), lambda i, ids: (ids[1] * nb + i, 0))

    def body(ids_ref, a_ref, b_ref, o_ref):
        o_ref[...] = ((a_ref[...] + b_ref[0].astype(F32)) + b_ref[1].astype(F32)) + b_ref[2].astype(F32)

    return _prefetch_call(body, name, ids, (nb,), [own, blk3], out, _sds((2 * rows, cols), F32), (own_sum, got))


SMALL_RB = 280


def _small_pair_sum(own, got):
    blk = pl.BlockSpec((SMALL_RB, 128), lambda i: (i, 0))

    def body(a_ref, b_ref, o_ref):
        o_ref[...] = a_ref[...] + b_ref[...]

    return pl.pallas_call(body, name="small_pair_sum", grid=(own.shape[0] // SMALL_RB,), in_specs=[blk, blk],
                          out_specs=blk, out_shape=_sds(own.shape, F32),
                          compiler_params=_params(("parallel",)))(own, got)


def _small_chip_sum(pair, got, ids):
    nb = pair.shape[0] // 2 // SMALL_RB
    half = pl.BlockSpec((SMALL_RB, 128), lambda i, ids: (ids[1] * nb + i, 0))
    blk3 = pl.BlockSpec((N_PEERS, SMALL_RB, 128), lambda i, ids: (0, i, 0))

    def body(ids_ref, a_ref, b_ref, o_ref):
        o_ref[...] = (a_ref[...] + b_ref[1]) + (b_ref[0] + b_ref[2])

    return _prefetch_call(body, "small_chip_sum", ids, (nb,), [half, blk3], half, _sds(pair.shape, F32), (pair, got))


def _adamw_math(w, g, m, v):
    m = ADAM_B1 * m + (1.0 - ADAM_B1) * g
    v = ADAM_B2 * v + (1.0 - ADAM_B2) * (g * g)
    m_hat = m / (1.0 - ADAM_B1 ** ADAM_STEP)
    v_hat = v / (1.0 - ADAM_B2 ** ADAM_STEP)
    delta = -ADAM_LR * (m_hat / (jnp.sqrt(v_hat) + ADAM_EPS) + ADAM_WD * w)
    return delta, m, v


def _adamw(name, w, g, m, v, rb=None):
    rows, cols = w.shape
    rb = rows if rb is None else rb
    blk = pl.BlockSpec((rb, cols), lambda i: (i, 0))

    def body(w_ref, g_ref, m_ref, v_ref, d_ref, nm_ref, nv_ref):
        d, nm, nv = _adamw_math(w_ref[...], g_ref[...], m_ref[...], v_ref[...])
        d_ref[...] = d
        nm_ref[...] = nm
        nv_ref[...] = nv

    return pl.pallas_call(body, name=name, grid=(rows // rb,), in_specs=[blk] * 4, out_specs=[blk] * 3,
                          out_shape=[_sds(w.shape, F32)] * 3, compiler_params=_params(("parallel",)))(w, g, m, v)


def _adamw_small(ws, gs, ms, vs):
    n = len(ws)

    def body(*refs):
        for t in range(n):
            w_ref, g_ref, m_ref, v_ref = (refs[k * n + t] for k in range(4))
            d, nm, nv = _adamw_math(w_ref[...], g_ref[...], m_ref[...], v_ref[...])
            for k, val in enumerate((d, nm, nv)):
                refs[(4 + k) * n + t][...] = val

    res = pl.pallas_call(body, name="adamw_small", out_shape=[_sds(a.shape, F32) for a in ws] * 3,
                         compiler_params=_params())(*ws, *gs, *ms, *vs)
    return [(res[t], res[n + t], res[2 * n + t]) for t in range(n)]


BIG = ("w_in", "w_att_o", "w_rec_o", "w_out", "w_ff1", "w_ff2")
SHARDED_VECS = ("conv_w", "b_rg_a", "b_rg_i", "lru_lambda")
SMALL = ("ln1_g", "b_in", "rpb", "conv_w", "conv_b", "w_rg_a", "b_rg_a", "w_rg_i", "b_rg_i", "lru_lambda",
         "ln2_g", "lnf_g")
SMALL_ROWS = 2240
ORDER = ("ln1_g", "w_in", "b_in", "rpb", "w_att_o", "conv_w", "conv_b", "w_rg_a", "b_rg_a", "w_rg_i", "b_rg_i",
         "lru_lambda", "w_rec_o", "w_out", "ln2_g", "w_ff1", "w_ff2", "lnf_g")


def _pack_small(grads, loss):
    parts, sizes = [], {}
    for n in SMALL:
        flat = grads[n].reshape(-1)
        pad = (-flat.shape[0]) % 128
        sizes[n] = (flat.shape[0], flat.shape[0] + pad)
        parts.append(jnp.pad(flat, (0, pad)))
    total = sum(s[1] for s in sizes.values())
    parts.append(jnp.pad(loss.reshape(1), (0, SMALL_ROWS * 128 - total - 1)))
    return jnp.concatenate(parts).reshape(SMALL_ROWS, 128), sizes


def _unpack_small(buf, sizes, shapes):
    flat = buf.reshape(-1)
    out, pos = {}, 0
    for n in SMALL:
        size, padded = sizes[n]
        out[n] = flat[pos:pos + size].reshape(shapes[n])
        pos += padded
    return out, flat[pos]


def _gather_weights(w, chip):
    chip_id = chip.astype(jnp.int32).reshape(1)
    vec_rows = [w[n][0] for n in SHARDED_VECS]
    vec_shard = jnp.concatenate(vec_rows + [jnp.zeros((16 - 10, D // N_CHIPS), F32)], axis=0)
    vec_slots = lax.dynamic_update_slice(jnp.zeros((N_CHIPS, 16, D // N_CHIPS), F32), vec_shard[None], (chip, 0, 0))
    bufs_a, sems_a, token_a = _gather_start("gather_start_first", [_cast_bf16("cast_w_in", w["w_in"][0], chip_id), vec_slots])
    rest_names = BIG[1:]
    bufs_b, sems_b, token_b = _gather_start(
        "gather_start_rest", [_cast_bf16("cast_" + n, w[n][0], chip_id, after=(token_a,)) for n in rest_names])

    def first(after):
        w_in_full, vec_full = _gather_forward("gather_forward_first", _gather_wait("gather_wait_first", bufs_a, sems_a, after))
        vecs = vec_full.transpose(1, 0, 2).reshape(16, D)
        return dict(w_in=w_in_full, conv_w=vecs[0:4], b_rg_a=vecs[4:6], b_rg_i=vecs[6:8], lru_lambda=vecs[8:10])

    def rest(after):
        full = dict(zip(rest_names, _gather_forward("gather_forward_rest",
                                                    _gather_wait("gather_wait_rest", bufs_b, sems_b, after))))
        return dict(w_att_o=full["w_att_o"], w_ff1=full["w_ff1"], w_rec_o=full["w_rec_o"].reshape(D, D),
                    w_out=full["w_out"].reshape(D, D), w_ff2=full["w_ff2"].reshape(D_FF, D))

    p = dict(ln1_g=w["ln1_g"], b_in=w["b_in"], rpb=w["rpb"][0], conv_b=w["conv_b"], w_rg_a=w["w_rg_a"][0],
             w_rg_i=w["w_rg_i"][0], ln2_g=w["ln2_g"], lnf_g=w["lnf_g"].reshape(1, D))
    return p, ((token_b,), first, rest)


class _Reducer:
    def __init__(self, ids):
        self.ids = ids
        self.groups = {}

    def begin(self, tag, grads, small=None):
        names = list(grads)
        big = [grads[n].reshape(N_CHIPS, -1, grads[n].shape[-1]) for n in names]
        flight, token = _pair_start("pair_start_" + tag, big, [] if small is None else [small])
        self.groups[tag] = dict(names=names, pair=flight, small=small is not None)
        return (token,)

    def advance(self, tag, after):
        grp = self.groups[tag]
        n = len(grp["names"])
        mine, got = _pair_wait("pair_wait_" + tag, grp["pair"], n, after)
        sums = [_pair_sum("pair_sum_" + name, a, b, self.ids) for name, a, b in zip(grp["names"], mine, got)]
        small_sum = _small_pair_sum(mine[n], got[n]) if grp["small"] else None
        grp["chip"], token = _chip_start("chip_start_" + tag, [s[1] for s in sums], small_sum)
        grp["sums"] = [s[0] for s in sums]
        self.last_token = token
        return (token,)

    def finish(self, tag, after):
        grp = self.groups[tag]
        srcs, lands = _chip_wait("chip_wait_" + tag, grp["chip"], grp["small"], after)
        halves = [_chip_sum("chip_sum_" + name, s, b, self.ids) for name, s, b in zip(grp["names"], grp["sums"], lands)]
        if grp["small"]:
            halves.append(_small_chip_sum(srcs[-1], lands[-1], self.ids))
        return _half_swap("half_swap_" + tag, halves)


def kernel(x, ln1_g, w_in, b_in, rpb, w_att_o, conv_w, conv_b, w_rg_a, b_rg_a, w_rg_i, b_rg_i, lru_lambda, w_rec_o, w_out, ln2_g, w_ff1, w_ff2, lnf_g, loss_target, m_ln1_g, m_w_in, m_b_in, m_rpb, m_w_att_o, m_conv_w, m_conv_b, m_w_rg_a, m_b_rg_a, m_w_rg_i, m_b_rg_i, m_lru_lambda, m_w_rec_o, m_w_out, m_ln2_g, m_w_ff1, m_w_ff2, m_lnf_g, v_ln1_g, v_w_in, v_b_in, v_rpb, v_w_att_o, v_conv_w, v_conv_b, v_w_rg_a, v_b_rg_a, v_w_rg_i, v_b_rg_i, v_lru_lambda, v_w_rec_o, v_w_out, v_ln2_g, v_w_ff1, v_w_ff2, v_lnf_g):
    w = dict(ln1_g=ln1_g, w_in=w_in, b_in=b_in, rpb=rpb, w_att_o=w_att_o, conv_w=conv_w, conv_b=conv_b,
             w_rg_a=w_rg_a, b_rg_a=b_rg_a, w_rg_i=w_rg_i, b_rg_i=b_rg_i, lru_lambda=lru_lambda, w_rec_o=w_rec_o,
             w_out=w_out, ln2_g=ln2_g, w_ff1=w_ff1, w_ff2=w_ff2, lnf_g=lnf_g)
    m = dict(ln1_g=m_ln1_g, w_in=m_w_in, b_in=m_b_in, rpb=m_rpb, w_att_o=m_w_att_o, conv_w=m_conv_w,
             conv_b=m_conv_b, w_rg_a=m_w_rg_a, b_rg_a=m_b_rg_a, w_rg_i=m_w_rg_i, b_rg_i=m_b_rg_i,
             lru_lambda=m_lru_lambda, w_rec_o=m_w_rec_o, w_out=m_w_out, ln2_g=m_ln2_g, w_ff1=m_w_ff1,
             w_ff2=m_w_ff2, lnf_g=m_lnf_g)
    v = dict(ln1_g=v_ln1_g, w_in=v_w_in, b_in=v_b_in, rpb=v_rpb, w_att_o=v_w_att_o, conv_w=v_conv_w,
             conv_b=v_conv_b, w_rg_a=v_w_rg_a, b_rg_a=v_b_rg_a, w_rg_i=v_w_rg_i, b_rg_i=v_b_rg_i,
             lru_lambda=v_lru_lambda, w_rec_o=v_w_rec_o, w_out=v_w_out, ln2_g=v_ln2_g, w_ff1=v_w_ff1,
             w_ff2=v_w_ff2, lnf_g=v_lnf_g)
    chip = 2 * lax.axis_index("x") + lax.axis_index("y")
    ids = jnp.stack([chip, lax.axis_index("c")]).astype(jnp.int32)

    out_grad, out_delta, out_m, out_v = {}, {}, {}, {}

    def update(n, gn):
        shape, two_d = w[n].shape, gn.shape
        d, nm, nv = _adamw("adamw_" + n, w[n].reshape(two_d), gn, m[n].reshape(two_d), v[n].reshape(two_d), 256)
        out_grad[n], out_delta[n], out_m[n], out_v[n] = (gn.reshape(shape), d.reshape(shape), nm.reshape(shape),
                                                         nv.reshape(shape))
        return d

    reducer = _Reducer(ids)
    p, late = _gather_weights(w, chip)
    loss, grad_x, g = _local_step(x, loss_target, p, late, reducer)
    small, sizes = _pack_small(g, loss + reducer.last_token[:1, :1])
    after = reducer.begin("small", {}, small)[0]
    for tag in ("ff", "proj", "in"):
        for n, red in zip(reducer.groups[tag]["names"], reducer.finish(tag, after)):
            after = update(n, red)
        if tag == "ff":
            after = reducer.advance("small", after)[0]
    (small_red,) = reducer.finish("small", after)
    gsmall, loss = _unpack_small(small_red, sizes, {n: g[n].shape for n in SMALL})
    two_d = {n: (int(np.prod(w[n].shape[:-1])), w[n].shape[-1]) for n in SMALL}
    for n in SHARDED_VECS:
        gsmall[n] = lax.dynamic_slice_in_dim(gsmall[n], chip * (D // N_CHIPS), D // N_CHIPS, axis=1)
    gs = [gsmall[n].reshape(two_d[n]) for n in SMALL]
    updates = _adamw_small([w[n].reshape(two_d[n]) for n in SMALL], gs, [m[n].reshape(two_d[n]) for n in SMALL],
                           [v[n].reshape(two_d[n]) for n in SMALL])
    for n, gn, (d, nm, nv) in zip(SMALL, gs, updates):
        shape = w[n].shape
        out_grad[n], out_delta[n], out_m[n], out_v[n] = (gn.reshape(shape), d.reshape(shape), nm.reshape(shape),
                                                         nv.reshape(shape))
    return (loss, grad_x, *[out_grad[n] for n in ORDER], *[out_delta[n] for n in ORDER],
            *[out_m[n] for n in ORDER], *[out_v[n] for n in ORDER])
```

```python
import functools

import numpy as np
import jax
import jax.numpy as jnp
from jax import lax
from jax.experimental import pallas as pl
from jax.experimental.pallas import tpu as pltpu

F32 = jnp.float32
BF16 = jnp.bfloat16

T = 2048
D = 1024
D_ATT = 512
D_IN = 5632
D_FF = 4096
N_HEADS = 8
HEAD_DIM = 64
GRID_W = 64
N_ROWS = T // GRID_W
WIN_H = 8
WIN_W = 16
KEYS = WIN_H * GRID_W
N_CHIPS = 4
EPS = 1e-6
LRU_C = 8.0
SCALE = HEAD_DIM ** -0.5
REC_CB = 256
REC_CHUNK = 256
PAD = 8

ADAM_LR = 0.001
ADAM_B1 = 0.9
ADAM_B2 = 0.999
ADAM_EPS = 1e-08
ADAM_WD = 0.01
ADAM_STEP = 10

VMEM_LIMIT = 56 * 1024 * 1024

NN = (((1,), (0,)), ((), ()))
NT = (((1,), (1,)), ((), ()))
TN = (((0,), (0,)), ((), ()))
MESH = pl.DeviceIdType.MESH


def _params(sem=None):
    return pltpu.CompilerParams(dimension_semantics=sem, vmem_limit_bytes=VMEM_LIMIT)


def _dot(a, b, dims):
    return lax.dot_general(a, b, dims, preferred_element_type=F32)


def _sigmoid(x):
    return 0.5 * jnp.tanh(0.5 * x) + 0.5


def _matmul(name, a, b, *, dims, grid, a_spec, b_spec, out_shapes, out_specs, acc_shape,
            extras=(), extra_specs=(), epilogue=None, colsum_spec=None, colsum_shape=None, after=(),
            semantics=("parallel", "parallel", "arbitrary"), epilogue_takes_first=False):
    nk = grid[2]
    n_extra = len(extras)
    n_out = len(out_shapes)
    with_colsum = colsum_spec is not None

    def body(a_ref, b_ref, *rest):
        ex = rest[:n_extra]
        rest = rest[:n_extra] + rest[n_extra + len(after):]
        outs = rest[n_extra:n_extra + n_out]
        pos = n_extra + n_out
        cs_out = rest[pos] if with_colsum else None
        pos += 1 if with_colsum else 0
        acc = rest[pos]
        cs_acc = rest[pos + 1] if with_colsum else None
        k = pl.program_id(2)
        first_tile = pl.program_id(0) == 0

        @pl.when(k == 0)
        def _():
            acc[...] = jnp.zeros_like(acc)
            if with_colsum:
                cs_acc[...] = jnp.zeros_like(cs_acc)

        bv = b_ref[...]
        acc[...] += _dot(a_ref[...].astype(BF16), bv.astype(BF16), dims)
        if with_colsum:
            cs_acc[...] += jnp.sum(bv.astype(F32), axis=0, keepdims=True)

        @pl.when(k == nk - 1)
        def _():
            r = acc[...]
            if epilogue is None:
                outs[0][...] = r.astype(outs[0].dtype)
            elif epilogue_takes_first:
                epilogue(r, ex, outs, first_tile)
            else:
                epilogue(r, ex, outs)
            if with_colsum:
                cs_out[...] = cs_acc[...]

    shapes = list(out_shapes)
    specs = list(out_specs)
    scratch = [pltpu.VMEM(acc_shape, F32)]
    if with_colsum:
        shapes.append(colsum_shape)
        specs.append(colsum_spec)
        scratch.append(pltpu.VMEM((1, acc_shape[1]), F32))
    res = pl.pallas_call(
        body, name=name, grid=grid,
        in_specs=[a_spec, b_spec, *extra_specs] + [_ANY] * len(after),
        out_specs=specs, out_shape=shapes, scratch_shapes=scratch,
        compiler_params=_params(semantics),
    )(a, b, *extras, *after)
    return res


def _sds(shape, dtype):
    return jax.ShapeDtypeStruct(shape, dtype)


TM = 1024
NI = T // TM


def _mm_nn_cols(name, a, wg, out_dtype, *, bias=None, extras=(), extra_specs=(), epilogue=None,
                out_shapes=None, out_specs=None):
    k_dim, n4 = wg.shape[1], wg.shape[2]
    ex, exs = list(extras), list(extra_specs)
    if bias is not None:
        ex = [bias] + ex
        exs = [pl.BlockSpec((1, n4), lambda j, i, k: (0, j))] + exs
        user_ep = epilogue

        def epilogue(r, e, outs):
            r = r + e[0][...]
            if user_ep is None:
                outs[0][...] = r.astype(outs[0].dtype)
            else:
                user_ep(r, e[1:], outs)
    if out_shapes is None:
        out_shapes = [_sds((T, N_CHIPS * n4), out_dtype)]
        out_specs = [pl.BlockSpec((TM, n4), lambda j, i, k: (i, j))]
    return _matmul(
        name, a, wg, dims=NN, grid=(N_CHIPS, NI, 1),
        a_spec=pl.BlockSpec((TM, k_dim), lambda j, i, k: (i, 0)),
        b_spec=pl.BlockSpec((None, k_dim, n4), lambda j, i, k: (j, 0, 0)),
        out_shapes=out_shapes, out_specs=out_specs, acc_shape=(TM, n4),
        extras=ex, extra_specs=exs, epilogue=epilogue)


def _mm_nt_cols_rms_bwd(name, a, wg, x, g, dres, after=(), bf16_copy=False):
    n4 = wg.shape[2]
    row = pl.BlockSpec((TM, D), lambda i, j, k: (i, 0))
    vec = pl.BlockSpec((1, D), lambda i, j, k: (0, 0))

    def epilogue(dhv, ex, outs, first):
        x_ref, g_ref, dres_ref = ex
        dx_ref, dg_ref = outs[0], outs[-1]
        xv = x_ref[...]
        rstd = lax.rsqrt(jnp.mean(xv * xv, axis=-1, keepdims=True) + EPS)
        xhat = xv * rstd
        dy = dhv * g_ref[...]
        dx = dres_ref[...] + rstd * (dy - xhat * jnp.mean(dy * xhat, axis=-1, keepdims=True))
        dx_ref[...] = dx
        if bf16_copy:
            outs[1][...] = dx.astype(BF16)
        part = jnp.sum(dhv * xhat, axis=0, keepdims=True)

        @pl.when(first)
        def _():
            dg_ref[...] = part

        @pl.when(jnp.logical_not(first))
        def _():
            dg_ref[...] += part

    return _matmul(
        name, a, wg, dims=NT, grid=(NI, 1, N_CHIPS),
        a_spec=pl.BlockSpec((TM, n4), lambda i, j, k: (i, k)),
        b_spec=pl.BlockSpec((None, D, n4), lambda i, j, k: (k, 0, 0)),
        out_shapes=[_sds((T, D), F32)] + [_sds((T, D), BF16)] * bf16_copy + [_sds((1, D), F32)],
        out_specs=[row] + [row] * bf16_copy + [vec], acc_shape=(TM, D),
        extras=[x, g, dres], extra_specs=[row, vec, row], epilogue=epilogue, after=after,
        semantics=("arbitrary", "arbitrary", "arbitrary"), epilogue_takes_first=True)


def _mm_nt_rows(name, a, w, out_dtype, *, tn, extras=(), extra_specs=(), epilogue=None):
    k_dim, n = w.shape
    return _matmul(
        name, a, w, dims=NT, grid=(k_dim // tn, NI, 1),
        a_spec=pl.BlockSpec((TM, n), lambda j, i, k: (i, 0)),
        b_spec=pl.BlockSpec((tn, n), lambda j, i, k: (j, 0)),
        out_shapes=[_sds((T, k_dim), out_dtype)],
        out_specs=[pl.BlockSpec((TM, tn), lambda j, i, k: (i, j))], acc_shape=(TM, tn),
        extras=extras, extra_specs=extra_specs, epilogue=epilogue)


def _mm_tn_cols(name, a, g, n4, *, colsum=False):
    k_dim = a.shape[1]
    kw = {}
    if colsum:
        kw = dict(colsum_spec=pl.BlockSpec((1, n4), lambda j, i, k: (0, j)),
                  colsum_shape=_sds((1, N_CHIPS * n4), F32))
    return _matmul(
        name, a, g, dims=TN, grid=(N_CHIPS, 1, NI),
        a_spec=pl.BlockSpec((TM, k_dim), lambda j, i, k: (k, 0)),
        b_spec=pl.BlockSpec((TM, n4), lambda j, i, k: (k, j)),
        out_shapes=[_sds((N_CHIPS, k_dim, n4), F32)],
        out_specs=[pl.BlockSpec((None, k_dim, n4), lambda j, i, k: (j, 0, 0))],
        acc_shape=(k_dim, n4), **kw)


def _mm_tn_rows(name, a, g, *, tm):
    k_dim, n = a.shape[1], g.shape[1]
    return _matmul(
        name, a, g, dims=TN, grid=(k_dim // tm, 1, NI),
        a_spec=pl.BlockSpec((TM, tm), lambda j, i, k: (k, j)),
        b_spec=pl.BlockSpec((TM, n), lambda j, i, k: (k, 0)),
        out_shapes=[_sds((k_dim, n), F32)],
        out_specs=[pl.BlockSpec((tm, n), lambda j, i, k: (j, 0))], acc_shape=(tm, n))


TE = 256
NE = T // TE
_ROW = pl.BlockSpec((TE, D), lambda i: (i, 0))
_VEC = pl.BlockSpec((1, D), lambda i: (0, 0))


def _rms_fwd(name, x, g, after=()):
    def body(x_ref, g_ref, *rest):
        h_ref = rest[-1]
        xv = x_ref[...]
        rstd = lax.rsqrt(jnp.mean(xv * xv, axis=-1, keepdims=True) + EPS)
        h_ref[...] = (xv * rstd * g_ref[...]).astype(BF16)

    return pl.pallas_call(body, name=name, grid=(NE,), in_specs=[_ROW, _VEC] + [_ANY] * len(after), out_specs=_ROW,
                          out_shape=_sds((T, D), BF16), compiler_params=_params(("parallel",)))(x, g, *after)


def _mm_x2_loss_head(s, w_ff2, x1, target, g):
    k_dim = w_ff2.shape[0]
    row = pl.BlockSpec((TM, D), lambda i, j, k: (i, 0))
    vec = pl.BlockSpec((1, D), lambda i, j, k: (0, 0))

    def epilogue(r, ex, outs, first):
        x1_ref, t_ref, g_ref = ex
        loss_ref, dx_ref, dxb_ref, dg_ref = outs
        xv = x1_ref[...] + r
        rstd = lax.rsqrt(jnp.mean(xv * xv, axis=-1, keepdims=True) + EPS)
        xhat = xv * rstd
        gv = g_ref[...]
        err = xhat * gv - t_ref[...]
        dy = err * (1.0 / D)
        dxh = dy * gv
        dx = rstd * (dxh - xhat * jnp.mean(dxh * xhat, axis=-1, keepdims=True))
        dx_ref[...] = dx
        dxb_ref[...] = dx.astype(BF16)
        dg_part = jnp.sum(dy * xhat, axis=0, keepdims=True)
        loss_part = (0.5 / D) * jnp.sum(jnp.sum(err * err, axis=1, keepdims=True), axis=0, keepdims=True)

        @pl.when(first)
        def _():
            dg_ref[...] = dg_part
            loss_ref[...] = loss_part

        @pl.when(jnp.logical_not(first))
        def _():
            dg_ref[...] += dg_part
            loss_ref[...] += loss_part

    return _matmul(
        "mm_x2_loss_head", s, w_ff2, dims=NN, grid=(NI, 1, k_dim // D),
        a_spec=pl.BlockSpec((TM, D), lambda i, j, k: (i, k)), b_spec=pl.BlockSpec((D, D), lambda i, j, k: (k, 0)),
        out_shapes=[_sds((1, 1), F32), _sds((T, D), F32), _sds((T, D), BF16), _sds((1, D), F32)],
        out_specs=[pl.BlockSpec((1, 1), lambda i, j, k: (0, 0)), row, row, vec], acc_shape=(TM, D),
        extras=[x1, target, g], extra_specs=[row, row, vec], epilogue=epilogue,
        semantics=("arbitrary", "arbitrary", "arbitrary"), epilogue_takes_first=True)


MW = 512
_G_ATT_BLK = 3584 // MW
_G_REC_BLK = 4608 // MW


TB = 512


def _branch_specs():
    def row(cols):
        return pl.BlockSpec((TB, cols), lambda i: (i, 0))

    ga = pl.BlockSpec((TB, MW), lambda i: (i, _G_ATT_BLK))
    ga2 = pl.BlockSpec((TB, MW), lambda i: (i, _G_ATT_BLK + 1))
    gr = pl.BlockSpec((TB, MW), lambda i: (i, _G_REC_BLK))
    gr2 = pl.BlockSpec((TB, MW), lambda i: (i, _G_REC_BLK + 1))
    w_att = pl.BlockSpec((N_CHIPS, D_ATT, D // N_CHIPS), lambda i: (0, 0, 0))
    w_sq = pl.BlockSpec((D, D), lambda i: (0, 0))
    return row, (ga, ga2, gr, gr2), w_att, w_sq


def _gate_values(gate_refs):
    ga, ga2, gr, gr2 = (r[...] for r in gate_refs)
    return _sigmoid(jnp.concatenate([ga, ga2], axis=1)), _sigmoid(jnp.concatenate([gr, gr2], axis=1))


def _branches_fwd(att, g, z, x, w_att_o, w_rec_o, w_out, ln2_g):
    row, gate_specs, w_att, w_sq = _branch_specs()

    def body(att_ref, g_ref, ga_ref, ga2_ref, gr_ref, gr2_ref, x_ref, wa_ref, wr_ref, wo_ref, g2_ref,
             ya_ref, yr_ref, m_ref, x1_ref, h2_ref):
        attv = att_ref[...]
        ya = jnp.concatenate([_dot(attv, wa_ref[j], NN) for j in range(N_CHIPS)], axis=1)
        yr = _dot(g_ref[...], wr_ref[...], NN)
        sa, sr = _gate_values((ga_ref, ga2_ref, gr_ref, gr2_ref))
        mixed = (sa * ya + sr * yr).astype(BF16)
        ya_ref[...] = ya
        yr_ref[...] = yr
        m_ref[...] = mixed
        x1 = x_ref[...] + _dot(mixed, wo_ref[...], NN)
        x1_ref[...] = x1
        rstd = lax.rsqrt(jnp.mean(x1 * x1, axis=-1, keepdims=True) + EPS)
        h2_ref[...] = (x1 * rstd * g2_ref[...]).astype(BF16)

    return pl.pallas_call(
        body, name="branches_fwd", grid=(T // TB,),
        in_specs=[row(D_ATT), row(D), *gate_specs, row(D), w_att, w_sq, w_sq, pl.BlockSpec((1, D), lambda i: (0, 0))],
        out_specs=[row(D)] * 5,
        out_shape=[_sds((T, D), F32), _sds((T, D), F32), _sds((T, D), BF16), _sds((T, D), F32), _sds((T, D), BF16)],
        compiler_params=_params(("parallel",)))(att, g, z, z, z, z, x, w_att_o, w_rec_o, w_out, ln2_g)


def _branches_bwd(dx1_b, y_att, y_rec, z, w_att_o, w_rec_o, w_out):
    row, gate_specs, w_att, w_sq = _branch_specs()
    n4 = D // N_CHIPS

    def body(dx_ref, ya_ref, yr_ref, ga_ref, ga2_ref, gr_ref, gr2_ref, wa_ref, wr_ref, wo_ref,
             dya_ref, dyr_ref, dga_ref, dgr_ref, datt_ref, dg_ref):
        dm = _dot(dx_ref[...], wo_ref[...], NT)
        sa, sr = _gate_values((ga_ref, ga2_ref, gr_ref, gr2_ref))
        dya = (dm * sa).astype(BF16)
        dyr = (dm * sr).astype(BF16)
        dya_ref[...] = dya
        dyr_ref[...] = dyr
        dga_ref[...] = (dm * ya_ref[...] * sa * (1.0 - sa)).astype(BF16)
        dgr_ref[...] = (dm * yr_ref[...] * sr * (1.0 - sr)).astype(BF16)
        datt = _dot(dya[:, 0:n4], wa_ref[0], NT)
        for j in range(1, N_CHIPS):
            datt = datt + _dot(dya[:, j * n4:(j + 1) * n4], wa_ref[j], NT)
        datt_ref[...] = datt.astype(BF16)
        dg_ref[...] = _dot(dyr, wr_ref[...], NT).astype(BF16)

    return pl.pallas_call(
        body, name="branches_bwd", grid=(T // TB,),
        in_specs=[row(D), row(D), row(D), *gate_specs, w_att, w_sq, w_sq],
        out_specs=[row(D)] * 4 + [row(D_ATT), row(D)],
        out_shape=[_sds((T, D), BF16)] * 4 + [_sds((T, D_ATT), BF16), _sds((T, D), BF16)],
        compiler_params=_params(("parallel",)))(dx1_b, y_att, y_rec, z, z, z, z, w_att_o, w_rec_o, w_out)


HP = 2 * HEAD_DIM
N_HP = N_HEADS // 2
ATT_UNROLL_FWD = 8
ATT_UNROLL_BWD = 4
DIAG_ROWS = 32


def _window_maps():
    diag = np.zeros((GRID_W * GRID_W, 128), np.float32)
    for qc in range(GRID_W):
        w0 = min(max(qc - WIN_W // 2, 0), GRID_W - WIN_W)
        for kc in range(w0, w0 + WIN_W):
            diag[qc * GRID_W + kc, kc - qc + WIN_W - 1] = 1.0
    return diag, diag.sum(axis=1)[None, :]


def _split3(x):
    a = x.astype(BF16)
    r = x - a.astype(F32)
    b = r.astype(BF16)
    c = (r - b.astype(F32)).astype(BF16)
    return a, b, c


N_DROW = 2 * WIN_H - 1
N_DPAIR = N_DROW - 1


def _bias_pairs(rpb):
    diag, valid = _window_maps()
    r2 = jnp.pad(rpb.reshape(N_HEADS * N_DROW, 2 * WIN_W - 1),
                 ((0, 128 - N_HEADS * N_DROW), (0, 128 - (2 * WIN_W - 1))))

    def body(r_ref, d_ref, v_ref, o_ref):
        dv = d_ref[...]
        t = sum(_dot(part, dv, NN) for part in _split3(r_ref[...]))
        o_ref[...] = jnp.where(v_ref[...] > 0.0, t, -1e30)

    t = pl.pallas_call(body, name="rpb_expand", out_shape=_sds((128, GRID_W * GRID_W), F32),
                       compiler_params=_params())(r2, jnp.asarray(diag.T, BF16), jnp.asarray(valid, F32))
    t = t[:N_HEADS * N_DROW].reshape(N_HEADS, N_DROW, GRID_W, GRID_W)
    return jnp.concatenate([t[:, :N_DPAIR], t[:, 1:]], axis=-1)


def _row_bias(tb_ref, hh, d0):
    return jnp.concatenate([tb_ref[hh, d0 + 2 * ii] for ii in range(WIN_H // 2)], axis=1)


def _row_window(r):
    rs = jnp.clip(r - WIN_H // 2, 0, N_ROWS - WIN_H)
    return pl.multiple_of(r * GRID_W, GRID_W), pl.multiple_of(rs * GRID_W, GRID_W), rs - r + (WIN_H - 1)


def _split_heads(src_ref, dst_ref, scale=None):
    for hh in range(2):
        v = src_ref[:, hh * HEAD_DIM:(hh + 1) * HEAD_DIM]
        dst_ref[hh] = (v if scale is None else v * scale).astype(BF16)


def _attn_items(qb_ref, kb_ref, vb_ref, tb_ref, first_row, n_rows):
    wins = [_row_window(first_row + u) for u in range(n_rows)]
    items = [(u, hh) for u in range(n_rows) for hh in range(2)]
    q = [qb_ref[hh, pl.ds(wins[u][0], GRID_W), :] for u, hh in items]
    k = [kb_ref[hh, pl.ds(wins[u][1], KEYS), :] for u, hh in items]
    v = [vb_ref[hh, pl.ds(wins[u][1], KEYS), :] for u, hh in items]
    s = [_dot(qi, ki, NT) + _row_bias(tb_ref, hh, wins[u][2]) for qi, ki, (u, hh) in zip(q, k, items)]
    m = [jnp.max(si, axis=-1, keepdims=True) for si in s]
    e = [jnp.exp(si - mi) for si, mi in zip(s, m)]
    inv = [1.0 / jnp.sum(ei, axis=-1, keepdims=True) for ei in e]
    p = [ei * li for ei, li in zip(e, inv)]
    return wins, items, q, k, v, p


def _attn_in_specs():
    q = pl.BlockSpec((T, HP), lambda p: (0, p))
    k = pl.BlockSpec((T, HP), lambda p: (0, N_HP + p))
    v = pl.BlockSpec((T, HP), lambda p: (0, 2 * N_HP + p))
    tb = pl.BlockSpec((2, N_DPAIR, GRID_W, HP), lambda p: (p, 0, 0, 0))
    return q, k, v, tb


_HEAD_SCRATCH = pltpu.VMEM((2, T, HEAD_DIM), BF16)


def _attn_fwd(z, tb):
    def body(q_ref, k_ref, v_ref, tb_ref, o_ref, qb_ref, kb_ref, vb_ref):
        _split_heads(q_ref, qb_ref, SCALE)
        _split_heads(k_ref, kb_ref)
        _split_heads(v_ref, vb_ref)

        def rows(it, carry):
            wins, items, _, _, v, p = _attn_items(qb_ref, kb_ref, vb_ref, tb_ref, it * ATT_UNROLL_FWD, ATT_UNROLL_FWD)
            o = [_dot(pi.astype(BF16), vi, NN) for pi, vi in zip(p, v)]
            for u, (q0, _, _) in enumerate(wins):
                o_ref[pl.ds(q0, GRID_W), :] = jnp.concatenate(o[2 * u:2 * u + 2], axis=1).astype(BF16)
            return carry

        lax.fori_loop(0, N_ROWS // ATT_UNROLL_FWD, rows, 0)

    blk = pl.BlockSpec((T, HP), lambda p: (0, p))
    return pl.pallas_call(
        body, name="attn_fwd", grid=(N_HP,), in_specs=list(_attn_in_specs()), out_specs=blk,
        out_shape=_sds((T, D_ATT), BF16), scratch_shapes=[_HEAD_SCRATCH] * 3,
        compiler_params=_params(("parallel",)))(z, z, z, tb)


def _attn_bwd(z, tb, d_att, after=()):
    def body(q_ref, k_ref, v_ref, tb_ref, do_ref, flip_ref, *rest):
        (dq_ref, dk_ref, dv_ref, diag_ref, qb_ref, kb_ref, vb_ref, dob_ref, dka_ref, dva_ref,
         ds_ref) = rest[len(after):]
        _split_heads(q_ref, qb_ref, SCALE)
        _split_heads(k_ref, kb_ref)
        _split_heads(v_ref, vb_ref)
        _split_heads(do_ref, dob_ref)
        dka_ref[...] = jnp.zeros_like(dka_ref)
        dva_ref[...] = jnp.zeros_like(dva_ref)
        ds_ref[...] = jnp.zeros_like(ds_ref)

        def rows(it, carry):
            wins, items, q, k, v, p = _attn_items(qb_ref, kb_ref, vb_ref, tb_ref, it * ATT_UNROLL_BWD, ATT_UNROLL_BWD)
            do = [dob_ref[hh, pl.ds(wins[u][0], GRID_W), :] for u, hh in items]
            dv = [_dot(pi.astype(BF16), di, TN) for pi, di in zip(p, do)]
            dp = [_dot(di, vi, NT) for di, vi in zip(do, v)]
            ds = [pi * (dpi - jnp.sum(dpi * pi, axis=-1, keepdims=True)) for pi, dpi in zip(p, dp)]
            dsb = [d.astype(BF16) for d in ds]
            dq = [_dot(d, ki, NN) * SCALE for d, ki in zip(dsb, k)]
            dk = [_dot(d, qi, TN) for d, qi in zip(dsb, q)]
            for d, (u, hh) in zip(ds, items):
                for ii in range(WIN_H // 2):
                    ds_ref[hh, wins[u][2] + 2 * ii] += d[:, ii * HP:(ii + 1) * HP]
            for u, (q0, k0, _) in enumerate(wins):
                dq_ref[pl.ds(q0, GRID_W), :] = jnp.concatenate(dq[2 * u:2 * u + 2], axis=1).astype(BF16)
                dka_ref[pl.ds(k0, KEYS), :] += jnp.concatenate(dk[2 * u:2 * u + 2], axis=1)
                dva_ref[pl.ds(k0, KEYS), :] += jnp.concatenate(dv[2 * u:2 * u + 2], axis=1)
            return carry

        lax.fori_loop(0, N_ROWS // ATT_UNROLL_BWD, rows, 0)
        dk_ref[...] = dka_ref[...].astype(BF16)
        dv_ref[...] = dva_ref[...].astype(BF16)
        _diag_sums(ds_ref, flip_ref, diag_ref)

    blk = pl.BlockSpec((T, HP), lambda p: (0, p))
    q, k, v, tbs = _attn_in_specs()
    flip = jnp.asarray(np.eye(HP, dtype=np.float32)[::-1], BF16)
    return pl.pallas_call(
        body, name="attn_bwd", grid=(N_HP,),
        in_specs=[q, k, v, tbs, blk, pl.BlockSpec((HP, HP), lambda p: (0, 0))] + [_ANY] * len(after),
        out_specs=[blk, blk, blk, pl.BlockSpec((None, DIAG_ROWS, HP), lambda p: (p, 0, 0))],
        out_shape=[_sds((T, D_ATT), BF16)] * 3 + [_sds((N_HP, DIAG_ROWS, HP), F32)],
        scratch_shapes=[_HEAD_SCRATCH] * 4 + [pltpu.VMEM((T, HP), F32), pltpu.VMEM((T, HP), F32),
                                              pltpu.VMEM((2, N_DPAIR, GRID_W, HP), F32)],
        compiler_params=_params(("parallel",)))(z, z, z, tb, d_att, flip, *after)


def _diag_sums(acc_ref, flip_ref, out_ref):
    flip = flip_ref[...]
    rows = []
    for hh in range(2):
        for pair in range(N_DPAIR):
            reversed_lanes = sum(_dot(part, flip, NN) for part in _split3(acc_ref[hh, pair]))
            skewed = pltpu.roll(reversed_lanes, 0, 1, stride=1, stride_axis=0)
            rows.append(jnp.sum(skewed, axis=0, keepdims=True))
    rows.append(jnp.zeros((DIAG_ROWS - len(rows), HP), F32))
    out_ref[...] = jnp.concatenate(rows, axis=0)


def _rpb_grad(diag_sums):
    g = diag_sums.reshape(N_HP * DIAG_ROWS, HP)
    sel = np.zeros((2, 128, N_HP * DIAG_ROWS), np.float32)
    lane = np.zeros((2, HP, 128), np.float32)
    for h in range(N_HEADS):
        for pair in range(N_DPAIR):
            for half in range(2):
                sel[half, h * N_DROW + pair + half, (h // 2) * DIAG_ROWS + (h % 2) * N_DPAIR + pair] = 1.0
    for j in range(2 * WIN_W - 1):
        for half in range(2):
            lane[half, (HP - 1 - GRID_W * half - (j - (WIN_W - 1))) % HP, j] = 1.0

    def body(g_ref, sel_ref, lane_ref, o_ref):
        parts = _split3(g_ref[...])
        total = None
        for half in range(2):
            picked = sum(_dot(sel_ref[half], part, NN) for part in parts)
            term = sum(_dot(part, lane_ref[half], NN) for part in _split3(picked))
            total = term if total is None else total + term
        o_ref[...] = total

    out = pl.pallas_call(body, name="rpb_grad", out_shape=_sds((128, 128), F32),
                         compiler_params=_params())(g, jnp.asarray(sel, BF16), jnp.asarray(lane, BF16))
    return out[:N_HEADS * N_DROW, :2 * WIN_W - 1].reshape(N_HEADS, N_DROW, 2 * WIN_W - 1)


N_CB = D // REC_CB
N_CHUNK = T // REC_CHUNK
N_TILE = T // 8
_U_BLK = 1536 // REC_CB
_Y_BLK = 2560 // REC_CB


def _block_diag(w):
    per = REC_CB // 64
    wt = w.reshape(2, N_CB, per, 64, 64)
    eye = jnp.eye(per, dtype=w.dtype)
    full = wt[:, :, :, :, None, :] * eye[None, None, :, None, :, None]
    return full.reshape(2, N_CB, REC_CB, REC_CB).astype(BF16)


def _block_diag_grad(g):
    per = REC_CB // 64
    g6 = g.reshape(2, N_CB, per, 64, per, 64)
    return jnp.stack([g6[:, :, p, :, p, :] for p in range(per)], axis=2).reshape(2, 16, 64, 64)


def _gelu(x):
    c = 0.7978845608028654
    return 0.5 * x * (1.0 + jnp.tanh(c * (x + 0.044715 * x * x * x)))


def _gelu_grad(x):
    c = 0.7978845608028654
    th = jnp.tanh(c * (x + 0.044715 * x * x * x))
    return 0.5 * (1.0 + th) + 0.5 * x * (1.0 - th * th) * c * (1.0 + 3.0 * 0.044715 * x * x)


def _softplus_neg(lam):
    x = -lam
    e = jnp.exp(-jnp.abs(x))
    w = 1.0 + e
    l1p = jnp.where(w == 1.0, e, jnp.log(w) * e / (w - 1.0))
    return jnp.maximum(x, 0.0) + l1p


def _one_minus_exp(x):
    poly = x * (1.0 + x * (1 / 2 + x * (1 / 6 + x * (1 / 24 + x * (1 / 120 + x * (1 / 720))))))
    return jnp.where(x > -0.125, -poly, 1.0 - jnp.exp(x))


def _conv_taps(pad_ref, t0, w, sign):
    out = None
    for j in range(4):
        term = w[j:j + 1, :] * pad_ref[pl.ds(PAD + t0 + sign * (j - 2), REC_CHUNK), :]
        out = term if out is None else out + term
    return out


def _gates(u, wa, wi, ba, bi, sp):
    ub = u.astype(BF16)
    r = _sigmoid(_dot(ub, wa, NN) + ba)
    i = _sigmoid(_dot(ub, wi, NN) + bi)
    log_a = -LRU_C * r * sp
    a = jnp.exp(log_a)
    x = jnp.maximum(_one_minus_exp(2.0 * log_a), 0.0)
    positive = x > 0.0
    inv = lax.rsqrt(jnp.where(positive, x, 1.0))
    mult = jnp.where(positive, x * inv, 0.0)
    return r, i, a, mult, jnp.where(positive, inv, 0.0)


def _tile_scan(a, b, sub, reverse):
    for s in (1, 2, 4):
        if reverse:
            a_s, b_s, m = pltpu.roll(a, 8 - s, 0), pltpu.roll(b, 8 - s, 0), sub < 8 - s
        else:
            a_s, b_s, m = pltpu.roll(a, s, 0), pltpu.roll(b, s, 0), sub >= s
        b = jnp.where(m, a * b_s + b, b)
        a = jnp.where(m, a * a_s, a)
    return a, b


def _last_row(x, sub, row):
    return jnp.broadcast_to(jnp.sum(jnp.where(sub == row, x, 0.0), axis=0, keepdims=True), x.shape)


def _rec_prologue(up_ref, cw_ref, cb_ref, wa_ref, wi_ref, ba_ref, bi_ref, lam_ref,
                  upad_ref, u_ref, a_refs, h_refs):
    cb = up_ref.shape[1]
    zeros = jnp.zeros((PAD, cb), F32)
    upad_ref[pl.ds(0, PAD), :] = zeros
    upad_ref[pl.ds(PAD + T, PAD), :] = zeros
    upad_ref[pl.ds(PAD, T), :] = up_ref[...]
    cw = cw_ref[...]
    sp = _softplus_neg(lam_ref[...])
    for c in range(N_CHUNK):
        t0 = c * REC_CHUNK
        u = cb_ref[...] + _conv_taps(upad_ref, t0, cw, 1)
        u_ref[pl.ds(t0, REC_CHUNK), :] = u
        for d in range(2):
            _, i, a, mult, _ = _gates(u, wa_ref[d], wi_ref[d], ba_ref[d:d + 1, :], bi_ref[d:d + 1, :], sp[d:d + 1, :])
            a_refs[d][pl.ds(t0, REC_CHUNK), :] = a
            h_refs[d][pl.ds(t0, REC_CHUNK), :] = mult * (i * u)

    sub = lax.broadcasted_iota(jnp.int32, (8, cb), 0)

    def tile(k, carry):
        cf, cr = carry
        tf = pl.multiple_of(k * 8, 8)
        tr = pl.multiple_of((N_TILE - 1 - k) * 8, 8)
        af, bf = _tile_scan(a_refs[0][pl.ds(tf, 8), :], h_refs[0][pl.ds(tf, 8), :], sub, False)
        hf = af * cf + bf
        h_refs[0][pl.ds(tf, 8), :] = hf
        ar, br = _tile_scan(a_refs[1][pl.ds(tr, 8), :], h_refs[1][pl.ds(tr, 8), :], sub, True)
        hr = ar * cr + br
        h_refs[1][pl.ds(tr, 8), :] = hr
        return _last_row(hf, sub, 7), _last_row(hr, sub, 0)

    z8 = jnp.zeros((8, cb), F32)
    lax.fori_loop(0, N_TILE, tile, (z8, z8))
    return sp


def _rec_specs():
    up = pl.BlockSpec((T, REC_CB), lambda c: (0, _U_BLK + c))
    yb = pl.BlockSpec((T, REC_CB), lambda c: (0, _Y_BLK + c))
    cw = pl.BlockSpec((4, REC_CB), lambda c: (0, c))
    cbias = pl.BlockSpec((1, REC_CB), lambda c: (0, c))
    wbd = pl.BlockSpec((2, None, REC_CB, REC_CB), lambda c: (0, c, 0, 0))
    vec2 = pl.BlockSpec((2, REC_CB), lambda c: (0, c))
    col = pl.BlockSpec((T, REC_CB), lambda c: (0, c))
    return up, yb, cw, cbias, wbd, vec2, col


def _rec_fwd(z, conv_w, conv_b, wa, wi, ba, bi, lam):
    up, yb, cw, cbias, wbd, vec2, col = _rec_specs()

    def body(up_ref, yb_ref, cw_ref, cb_ref, wa_ref, wi_ref, ba_ref, bi_ref, lam_ref, g_ref,
             u_ref, af_ref, ar_ref, hf_ref, hr_ref, upad_ref):
        _rec_prologue(up_ref, cw_ref, cb_ref, wa_ref, wi_ref, ba_ref, bi_ref, lam_ref,
                      upad_ref, u_ref, (af_ref, ar_ref), (hf_ref, hr_ref))

        def chunk(c, carry):
            t0 = pl.multiple_of(c * REC_CHUNK, REC_CHUNK)
            rows = pl.ds(t0, REC_CHUNK)
            g_ref[rows, :] = ((hf_ref[rows, :] + hr_ref[rows, :]) * _gelu(yb_ref[rows, :])).astype(BF16)
            return carry

        lax.fori_loop(0, N_CHUNK, chunk, 0)

    res = pl.pallas_call(
        body, name="rec_fwd", grid=(N_CB,),
        in_specs=[up, yb, cw, cbias, wbd, wbd, vec2, vec2, vec2], out_specs=[col] * 6,
        out_shape=[_sds((T, D), BF16)] + [_sds((T, D), F32)] * 5,
        scratch_shapes=[pltpu.VMEM((T + 2 * PAD, REC_CB), F32)],
        compiler_params=_params(("parallel",)))(z, z, conv_w, conv_b, wa, wi, ba, bi, lam)
    return res[0], tuple(res[1:])


def _rec_bwd(z, dg, saved, conv_w, conv_b, wa, wi, ba, bi, lam, after=()):
    up, yb, cw, cbias, wbd, vec2, col = _rec_specs()

    def body(up_ref, yb_ref, dg_ref, u_ref, af_ref, ar_ref, hf_ref, hr_ref,
             cw_ref, cb_ref, wa_ref, wi_ref, ba_ref, bi_ref, lam_ref, *rest):
        (dup_ref, dyb_ref, dcw_ref, dcb_ref, dwa_ref, dwi_ref, dba_ref, dbi_ref, dlam_ref,
         upad_ref, dh_ref, gf_ref, gr_ref, daf_ref, dar_ref, dupad_ref) = rest[len(after):]
        g_refs, da_refs = (gf_ref, gr_ref), (daf_ref, dar_ref)
        cb = up_ref.shape[1]
        zeros = jnp.zeros((PAD, cb), F32)
        upad_ref[pl.ds(0, PAD), :] = zeros
        upad_ref[pl.ds(PAD + T, PAD), :] = zeros
        upad_ref[pl.ds(PAD, T), :] = up_ref[...]
        sp = _softplus_neg(lam_ref[...])

        def gate_chunk(c, carry):
            t0 = pl.multiple_of(c * REC_CHUNK, REC_CHUNK)
            rows = pl.ds(t0, REC_CHUNK)
            y = yb_ref[rows, :]
            dgv = dg_ref[rows, :].astype(F32)
            dh_ref[rows, :] = dgv * _gelu(y)
            dyb_ref[rows, :] = (dgv * (hf_ref[rows, :] + hr_ref[rows, :]) * _gelu_grad(y)).astype(BF16)
            return carry

        lax.fori_loop(0, N_CHUNK, gate_chunk, 0)

        sub = lax.broadcasted_iota(jnp.int32, (8, cb), 0)

        def tile(k, carry):
            cf, cr = carry
            kf = N_TILE - 1 - k
            tf = pl.multiple_of(kf * 8, 8)
            tnext = pl.multiple_of(jnp.minimum(kf + 1, N_TILE - 1) * 8, 8)
            tprev = pl.multiple_of(jnp.maximum(kf - 1, 0) * 8, 8)
            a_t = af_ref[pl.ds(tf, 8), :]
            a_n = jnp.where(kf < N_TILE - 1, af_ref[pl.ds(tnext, 8), :], 0.0)
            a_sh = jnp.where(sub == 7, pltpu.roll(a_n, 7, 0), pltpu.roll(a_t, 7, 0))
            ca, cbb = _tile_scan(a_sh, dh_ref[pl.ds(tf, 8), :], sub, True)
            gf = ca * cf + cbb
            h_t = hf_ref[pl.ds(tf, 8), :]
            h_p = jnp.where(kf > 0, hf_ref[pl.ds(tprev, 8), :], 0.0)
            h_sh = jnp.where(sub == 0, pltpu.roll(h_p, 1, 0), pltpu.roll(h_t, 1, 0))
            gf_ref[pl.ds(tf, 8), :] = gf
            daf_ref[pl.ds(tf, 8), :] = gf * h_sh
            tr = pl.multiple_of(k * 8, 8)
            rnext = pl.multiple_of(jnp.minimum(k + 1, N_TILE - 1) * 8, 8)
            rprev = pl.multiple_of(jnp.maximum(k - 1, 0) * 8, 8)
            b_t = ar_ref[pl.ds(tr, 8), :]
            b_p = jnp.where(k > 0, ar_ref[pl.ds(rprev, 8), :], 0.0)
            b_sh = jnp.where(sub == 0, pltpu.roll(b_p, 1, 0), pltpu.roll(b_t, 1, 0))
            ra, rb = _tile_scan(b_sh, dh_ref[pl.ds(tr, 8), :], sub, False)
            gr = ra * cr + rb
            hr_t = hr_ref[pl.ds(tr, 8), :]
            hr_n = jnp.where(k < N_TILE - 1, hr_ref[pl.ds(rnext, 8), :], 0.0)
            hr_sh = jnp.where(sub == 7, pltpu.roll(hr_n, 7, 0), pltpu.roll(hr_t, 7, 0))
            gr_ref[pl.ds(tr, 8), :] = gr
            dar_ref[pl.ds(tr, 8), :] = gr * hr_sh
            return _last_row(gf, sub, 0), _last_row(gr, sub, 7)

        z8 = jnp.zeros((8, cb), F32)
        lax.fori_loop(0, N_TILE, tile, (z8, z8))

        dupad_ref[pl.ds(0, PAD), :] = zeros
        dupad_ref[pl.ds(PAD + T, PAD), :] = zeros
        dwa_ref[...] = jnp.zeros_like(dwa_ref)
        dwi_ref[...] = jnp.zeros_like(dwi_ref)
        dba_ref[...] = jnp.zeros_like(dba_ref)
        dbi_ref[...] = jnp.zeros_like(dbi_ref)
        dlam_ref[...] = jnp.zeros_like(dlam_ref)

        def grad_chunk(c, carry):
            t0 = pl.multiple_of(c * REC_CHUNK, REC_CHUNK)
            rows = pl.ds(t0, REC_CHUNK)
            u = u_ref[rows, :]
            ub = u.astype(BF16)
            du = jnp.zeros((REC_CHUNK, cb), F32)
            for d in range(2):
                r, i, a, mult, inv_mult = _gates(u, wa_ref[d], wi_ref[d], ba_ref[d:d + 1, :], bi_ref[d:d + 1, :],
                                                 sp[d:d + 1, :])
                dbx = g_refs[d][rows, :]
                dmult = dbx * (i * u)
                diu = dbx * mult
                a2 = a * a
                dlog = da_refs[d][rows, :] * a - dmult * (a2 * inv_mult)
                dpa = (dlog * (-LRU_C) * sp[d:d + 1, :]) * r * (1.0 - r)
                dpi = (diu * u) * i * (1.0 - i)
                dpab, dpib = dpa.astype(BF16), dpi.astype(BF16)
                du = du + diu * i + _dot(dpab, wa_ref[d], NT) + _dot(dpib, wi_ref[d], NT)
                dwa_ref[d] += _dot(ub, dpab, TN)
                dwi_ref[d] += _dot(ub, dpib, TN)
                dba_ref[d:d + 1, :] += jnp.sum(dpa, axis=0, keepdims=True)
                dbi_ref[d:d + 1, :] += jnp.sum(dpi, axis=0, keepdims=True)
                dlam_ref[d:d + 1, :] += jnp.sum(dlog * r, axis=0, keepdims=True)
            dupad_ref[pl.ds(PAD + t0, REC_CHUNK), :] = du
            return carry

        lax.fori_loop(0, N_CHUNK, grad_chunk, 0)
        dlam_ref[...] = dlam_ref[...] * (LRU_C * _sigmoid(-lam_ref[...]))

        cw = cw_ref[...]
        dcb = jnp.zeros((1, cb), F32)
        dcw = [jnp.zeros((1, cb), F32) for _ in range(4)]
        for c in range(N_CHUNK):
            t0 = c * REC_CHUNK
            du = dupad_ref[pl.ds(PAD + t0, REC_CHUNK), :]
            dcb = dcb + jnp.sum(du, axis=0, keepdims=True)
            for j in range(4):
                dcw[j] = dcw[j] + jnp.sum(du * upad_ref[pl.ds(PAD + t0 + j - 2, REC_CHUNK), :], axis=0, keepdims=True)
            dup_ref[pl.ds(t0, REC_CHUNK), :] = _conv_taps(dupad_ref, t0, cw, -1).astype(BF16)
        dcb_ref[...] = dcb
        dcw_ref[...] = jnp.concatenate(dcw, axis=0)

    full = pltpu.VMEM((T, REC_CB), F32)
    padded = pltpu.VMEM((T + 2 * PAD, REC_CB), F32)
    return pl.pallas_call(
        body, name="rec_bwd", grid=(N_CB,),
        in_specs=[up, yb] + [col] * 6 + [cw, cbias, wbd, wbd, vec2, vec2, vec2] + [_ANY] * len(after),
        out_specs=[col, col, cw, cbias, wbd, wbd, vec2, vec2, vec2],
        out_shape=[_sds((T, D), BF16), _sds((T, D), BF16), _sds((4, D), F32), _sds((1, D), F32),
                   _sds((2, N_CB, REC_CB, REC_CB), F32), _sds((2, N_CB, REC_CB, REC_CB), F32),
                   _sds((2, D), F32), _sds((2, D), F32), _sds((2, D), F32)],
        scratch_shapes=[padded, full, full, full, full, full, padded],
        compiler_params=_params(("parallel",)))(z, z, dg, *saved, conv_w, conv_b, wa, wi, ba, bi, lam, *after)


class _NoReducer:
    def begin(self, tag, grads):
        return ()

    def advance(self, tag, after):
        return ()


def _local_step(x, target, p, late=None, reducer=_NoReducer()):
    x = x.reshape(T, D)
    target = target.reshape(T, D)
    tb = _bias_pairs(p["rpb"])
    wa, wi = _block_diag(p["w_rg_a"]), _block_diag(p["w_rg_i"])

    h1 = _rms_fwd("rms1_fwd", x, p["ln1_g"], after=late[0] if late else ())
    if late:
        p = {**p, **late[1]((h1, tb, wa, wi))}
    rec_params = (p["conv_w"], p["conv_b"], wa, wi, p["b_rg_a"], p["b_rg_i"], p["lru_lambda"])
    (z,) = _mm_nn_cols("mm_z", h1, p["w_in"], F32, bias=p["b_in"])
    att = _attn_fwd(z, tb)
    g, rec_saved = _rec_fwd(z, *rec_params)
    if late:
        p = {**p, **late[2](g)}
    y_att, y_rec, mixed, x1, h2 = _branches_fwd(att, g, z, x, p["w_att_o"], p["w_rec_o"], p["w_out"], p["ln2_g"])

    def relu2(r, ex, outs):
        rp = jnp.maximum(r, 0.0)
        outs[0][...] = (rp * rp).astype(BF16)

    (s,) = _mm_nn_cols("mm_ff1", h2, p["w_ff1"], BF16, epilogue=relu2)
    loss, dx2, dx2_b, g_lnf = _mm_x2_loss_head(s, p["w_ff2"], x1, target, p["lnf_g"])

    def relu2_bwd(r, ex, outs):
        outs[0][...] = (r * 2.0 * jnp.sqrt(ex[0][...].astype(F32))).astype(BF16)

    (df,) = _mm_nt_rows("mm_df", dx2_b, p["w_ff2"], BF16, tn=D, extras=[s],
                        extra_specs=[pl.BlockSpec((TM, D), lambda j, i, k: (i, j))], epilogue=relu2_bwd)
    (g_w_ff2,) = _mm_tn_rows("mm_g_ff2", s, dx2_b, tm=D)
    (g_w_ff1,) = _mm_tn_cols("mm_g_ff1", h2, df, D)
    tok = reducer.begin("ff", dict(w_ff2=g_w_ff2, w_ff1=g_w_ff1))
    dx1, dx1_b, g_ln2 = _mm_nt_cols_rms_bwd("mm_dh2_rms2_bwd", df, p["w_ff1"], x1, p["ln2_g"], dx2, after=tok,
                                            bf16_copy=True)

    dy_att, dy_rec, dg_att, dg_rec, d_att, d_g = _branches_bwd(dx1_b, y_att, y_rec, z, p["w_att_o"], p["w_rec_o"],
                                                               p["w_out"])
    (g_w_out,) = _mm_tn_rows("mm_g_out", mixed, dx1_b, tm=D)
    (g_w_att_o,) = _mm_tn_cols("mm_g_att_o", att, dy_att, D // N_CHIPS)
    (g_w_rec_o,) = _mm_tn_rows("mm_g_rec_o", g, dy_rec, tm=D)
    tok = reducer.advance("ff", g_w_rec_o) + reducer.begin("proj", dict(w_out=g_w_out, w_att_o=g_w_att_o, w_rec_o=g_w_rec_o))

    dq, dk, dv, ds_acc = _attn_bwd(z, tb, d_att, after=tok)
    g_rpb = _rpb_grad(ds_acc)
    tok = reducer.advance("proj", dq)
    d_up, d_yb, g_conv_w, g_conv_b, g_wa, g_wi, g_ba, g_bi, g_lam = _rec_bwd(z, d_g, rec_saved, *rec_params, after=tok)
    dz = jnp.concatenate([dq, dk, dv, d_up, d_yb, dg_att, dg_rec], axis=1)

    g_w_in, g_b_in = _mm_tn_cols("mm_g_in", h1, dz, D_IN // N_CHIPS, colsum=True)
    tok = reducer.begin("in", dict(w_in=g_w_in))
    grad_x, g_ln1 = _mm_nt_cols_rms_bwd("mm_dh1_rms1_bwd", dz, p["w_in"], x, p["ln1_g"], dx1, after=tok)
    reducer.advance("in", grad_x)

    grads = dict(ln1_g=g_ln1, w_in=g_w_in, b_in=g_b_in, rpb=g_rpb, w_att_o=g_w_att_o, conv_w=g_conv_w,
                 conv_b=g_conv_b, w_rg_a=_block_diag_grad(g_wa), b_rg_a=g_ba, w_rg_i=_block_diag_grad(g_wi),
                 b_rg_i=g_bi, lru_lambda=g_lam, w_rec_o=g_w_rec_o, w_out=g_w_out, ln2_g=g_ln2,
                 w_ff1=g_w_ff1, w_ff2=g_w_ff2, lnf_g=g_lnf)
    return loss, grad_x.reshape(1, T, D), grads


_ANY = pl.BlockSpec(memory_space=pl.ANY)
N_PEERS = N_CHIPS - 1


def _place():
    x, y, c = lax.axis_index("x"), lax.axis_index("y"), lax.axis_index("c")
    peers = [(1 - x, y), (x, 1 - y), (1 - x, 1 - y)]
    return x, y, c, 2 * x + y, peers


def _remote(src, dst, send_sem, recv_sem, dev):
    return pltpu.make_async_remote_copy(src_ref=src, dst_ref=dst, send_sem=send_sem, recv_sem=recv_sem,
                                        device_id=dev, device_id_type=MESH)


def _prefetch_call(body, name, ids, grid, in_specs, out_specs, out_shape, args, semantics=None):
    spec = pltpu.PrefetchScalarGridSpec(num_scalar_prefetch=1, grid=grid, in_specs=in_specs, out_specs=out_specs)
    return pl.pallas_call(body, name=name, grid_spec=spec, out_shape=out_shape,
                          compiler_params=_params(semantics or ("parallel",) * len(grid)))(ids, *args)


def _cast_bf16(name, w, chip_id, after=()):
    rows, cols = w.shape
    rb = min(rows, 256)

    def body(ids_ref, w_ref, *rest):
        rest[-1][...] = w_ref[...].astype(BF16)

    return _prefetch_call(body, name, chip_id, (rows // rb,),
                          [pl.BlockSpec((rb, cols), lambda i, ids: (i, 0))] + [_ANY] * len(after),
                          pl.BlockSpec((None, rb, cols), lambda i, ids: (ids[0], i, 0)),
                          _sds((N_CHIPS, rows, cols), BF16), (w, *after))


def _dma_sems(*counts):
    return [pltpu.SemaphoreType.DMA((k,)) for k in counts]


_HBM = pl.BlockSpec(memory_space=pltpu.HBM)
_SEM = pl.BlockSpec(memory_space=pltpu.SEMAPHORE)
_SPLIT_COPY = pltpu.CompilerParams(has_side_effects=pltpu.SideEffectType.DATAFLOW_SIDE_EFFECTING)


def _hbm(arrays):
    return [pltpu.with_memory_space_constraint(a, pltpu.HBM) for a in arrays]


def _hbm_like(arrays):
    return [pltpu.HBM(a.shape, a.dtype) for a in arrays]


def _halves(buf, c):
    half = buf.shape[1] // 2
    return pl.ds(c * half, half), pl.ds((1 - c) * half, half)


def _gather_start(name, slots):
    n = len(slots)
    nk = n * N_PEERS

    def body(*refs):
        bufs = refs[n:2 * n]
        send_sems, recv_sems, token = refs[2 * n:]
        x, y, c, chip, peers = _place()
        for t in range(n):
            mine, _ = _halves(bufs[t], c)
            for r, (px, py) in enumerate(peers):
                k = t * N_PEERS + r
                own = bufs[t].at[chip, mine]
                _remote(own, own, send_sems.at[k], recv_sems.at[k], (px, py, c)).start()
        token[...] = jnp.zeros_like(token)

    res = pl.pallas_call(
        body, name=name, in_specs=[_HBM] * n, out_specs=[_HBM] * n + [_SEM, _SEM, pl.BlockSpec(memory_space=pltpu.VMEM)],
        out_shape=_hbm_like(slots) + [pltpu.SemaphoreType.DMA((nk,)), pltpu.SemaphoreType.DMA((nk,)),
                                      _sds((8, 128), F32)],
        input_output_aliases={t: t for t in range(n)}, compiler_params=_SPLIT_COPY)(*_hbm(slots))
    return res[:n], (res[n], res[n + 1]), res[n + 2]


def _gather_wait(name, bufs, sems, after):
    n = len(bufs)
    after = tuple(after) if isinstance(after, (tuple, list)) else (after,)

    def body(*refs):
        ins = refs[:n]
        send_sems, recv_sems = refs[n], refs[n + 1]
        x, y, c, chip, peers = _place()
        for t in range(n):
            mine, _ = _halves(ins[t], c)
            for r, (px, py) in enumerate(peers):
                k = t * N_PEERS + r
                cp = _remote(ins[t].at[chip, mine], ins[t].at[2 * px + py, mine], send_sems.at[k], recv_sems.at[k],
                             (px, py, c))
                cp.wait_send()
                cp.wait_recv()

    return pl.pallas_call(
        body, name=name, in_specs=[_HBM] * n + [_SEM, _SEM] + [_ANY] * len(after), out_specs=[_HBM] * n,
        out_shape=_hbm_like(bufs), input_output_aliases={t: t for t in range(n)},
        compiler_params=_SPLIT_COPY)(*bufs, *sems, *after)


def _gather_forward(name, bufs):
    n = len(bufs)
    nk = n * N_PEERS

    def body(*refs):
        outs = refs[n:2 * n]
        send_sems, recv_sems = refs[2 * n:]
        x, y, c, chip, peers = _place()
        sibling = (x, y, 1 - c)
        sends = []
        for t in range(n):
            mine, _ = _halves(outs[t], c)
            for r, (px, py) in enumerate(peers):
                k = t * N_PEERS + r
                landed = outs[t].at[2 * px + py, mine]
                sends.append(_remote(landed, landed, send_sems.at[k], recv_sems.at[k], sibling))
                sends[-1].start()
        for t in range(n):
            _, theirs = _halves(outs[t], c)
            for r, (px, py) in enumerate(peers):
                k = t * N_PEERS + r
                landed = outs[t].at[2 * px + py, theirs]
                _remote(landed, landed, send_sems.at[k], recv_sems.at[k], sibling).wait_recv()
        for cp in sends:
            cp.wait_send()

    return pl.pallas_call(
        body, name=name, in_specs=[_ANY] * n, out_specs=[_ANY] * n, out_shape=[_sds(b.shape, b.dtype) for b in bufs],
        input_output_aliases={t: t for t in range(n)}, scratch_shapes=_dma_sems(nk, nk))(*bufs)


def _pair_copies(n, srcs, lands, send_sems, recv_sems):
    x, y, c, _, _ = _place()
    sibling = (x, y, 1 - c)
    copies = []
    for t in range(n):
        half = srcs[t].shape[1] // 2
        for j in range(N_CHIPS):
            k = t * N_CHIPS + j
            copies.append(_remote(srcs[t].at[j, pl.ds((1 - c) * half, half)], lands[t].at[j],
                                  send_sems.at[k], recv_sems.at[k], sibling))
    for t in range(n, len(srcs)):
        k = n * N_CHIPS + t - n
        copies.append(_remote(srcs[t], lands[t], send_sems.at[k], recv_sems.at[k], sibling))
    return copies


def _pair_start(name, grads, wholes=()):
    n = len(grads)
    srcs = list(grads) + list(wholes)
    m = len(srcs)
    lands = [pltpu.HBM((N_CHIPS, g.shape[1] // 2, g.shape[2]), F32) for g in grads] + _hbm_like(wholes)
    ns = n * N_CHIPS + len(wholes)

    def body(*refs):
        src_refs, land_refs = refs[m:2 * m], refs[2 * m:3 * m]
        send_sems, recv_sems, token = refs[3 * m:]
        for cp in _pair_copies(n, src_refs, land_refs, send_sems, recv_sems):
            cp.start()
        token[...] = jnp.zeros_like(token)

    res = pl.pallas_call(
        body, name=name, in_specs=[_HBM] * m,
        out_specs=[_HBM] * (2 * m) + [_SEM, _SEM, pl.BlockSpec(memory_space=pltpu.VMEM)],
        out_shape=_hbm_like(srcs) + lands + [pltpu.SemaphoreType.DMA((ns,)), pltpu.SemaphoreType.DMA((ns,)),
                                             _sds((8, 128), F32)],
        input_output_aliases={t: t for t in range(m)}, compiler_params=_SPLIT_COPY)(*_hbm(srcs))
    return (res[:m], res[m:2 * m], (res[2 * m], res[2 * m + 1])), res[2 * m + 2]


def _pair_wait(name, flight, n, after):
    srcs, lands, sems = flight
    m = len(srcs)

    def body(*refs):
        for cp in _pair_copies(n, refs[:m], refs[m:2 * m], refs[2 * m], refs[2 * m + 1]):
            cp.wait_send()
            cp.wait_recv()

    res = pl.pallas_call(
        body, name=name, in_specs=[_HBM] * (2 * m) + [_SEM, _SEM, _ANY], out_specs=[_HBM] * (2 * m),
        out_shape=_hbm_like(srcs) + _hbm_like(lands), input_output_aliases={t: t for t in range(2 * m)},
        compiler_params=_SPLIT_COPY)(*srcs, *lands, *sems, after)
    return res[:m], res[m:]


def _chip_copies(srcs, lands, small_src, small_land, send_sems, recv_sems):
    x, y, c, chip, peers = _place()
    n = len(srcs)
    copies = []
    for r, (px, py) in enumerate(peers):
        for t in range(n):
            k = t * N_PEERS + r
            copies.append(_remote(srcs[t].at[2 * px + py], lands[t].at[r], send_sems.at[k], recv_sems.at[k], (px, py, c)))
        if small_src is not None:
            k = n * N_PEERS + r
            half_s = small_src.shape[0] // 2
            copies.append(_remote(small_src.at[pl.ds(c * half_s, half_s)], small_land.at[r],
                                  send_sems.at[k], recv_sems.at[k], (px, py, c)))
    return copies


def _chip_start(name, sums_bf16, small=None):
    n = len(sums_bf16)
    srcs = list(sums_bf16) + ([small] if small is not None else [])
    m = len(srcs)
    lands = [pltpu.HBM((N_PEERS,) + s.shape[1:], BF16) for s in sums_bf16]
    if small is not None:
        lands.append(pltpu.HBM((N_PEERS, small.shape[0] // 2, 128), F32))
    nk = m * N_PEERS

    def body(*refs):
        src_refs, land_refs = refs[m:2 * m], refs[2 * m:3 * m]
        send_sems, recv_sems, token = refs[3 * m:]
        small_src, small_land = (src_refs[n], land_refs[n]) if small is not None else (None, None)
        for cp in _chip_copies(src_refs[:n], land_refs[:n], small_src, small_land, send_sems, recv_sems):
            cp.start()
        token[...] = jnp.zeros_like(token)

    res = pl.pallas_call(
        body, name=name, in_specs=[_HBM] * m,
        out_specs=[_HBM] * (2 * m) + [_SEM, _SEM, pl.BlockSpec(memory_space=pltpu.VMEM)],
        out_shape=_hbm_like(srcs) + lands + [pltpu.SemaphoreType.DMA((nk,)), pltpu.SemaphoreType.DMA((nk,)),
                                             _sds((8, 128), F32)],
        input_output_aliases={t: t for t in range(m)}, compiler_params=_SPLIT_COPY)(*_hbm(srcs))
    return (res[:m], res[m:2 * m], (res[2 * m], res[2 * m + 1])), res[2 * m + 2]


def _chip_wait(name, flight, with_small, after):
    srcs, lands, sems = flight
    m = len(srcs)
    n = m - 1 if with_small else m

    def body(*refs):
        src_refs, land_refs = refs[:m], refs[m:2 * m]
        send_sems, recv_sems = refs[2 * m], refs[2 * m + 1]
        small_src, small_land = (src_refs[n], land_refs[n]) if with_small else (None, None)
        for cp in _chip_copies(src_refs[:n], land_refs[:n], small_src, small_land, send_sems, recv_sems):
            cp.wait_send()
            cp.wait_recv()

    res = pl.pallas_call(
        body, name=name, in_specs=[_HBM] * (2 * m) + [_SEM, _SEM, _ANY], out_specs=[_HBM] * (2 * m),
        out_shape=_hbm_like(srcs) + _hbm_like(lands), input_output_aliases={t: t for t in range(2 * m)},
        compiler_params=_SPLIT_COPY)(*srcs, *lands, *sems, after)
    return res[:m], res[m:]


def _half_swap(name, bufs):
    n = len(bufs)

    def body(*refs):
        outs = refs[n:2 * n]
        send_sem, recv_sem = refs[2 * n:]
        x, y, c, _, _ = _place()
        sibling = (x, y, 1 - c)
        copies = []
        for t in range(n):
            h = outs[t].shape[0] // 2
            mine = outs[t].at[pl.ds(c * h, h)]
            copies.append(_remote(mine, mine, send_sem.at[t], recv_sem.at[t], sibling))
            copies[-1].start()
        for t in range(n):
            h = outs[t].shape[0] // 2
            theirs = outs[t].at[pl.ds((1 - c) * h, h)]
            _remote(theirs, theirs, send_sem.at[t], recv_sem.at[t], sibling).wait_recv()
        for cp in copies:
            cp.wait_send()

    return pl.pallas_call(
        body, name=name, in_specs=[_ANY] * n, out_specs=[_ANY] * n,
        out_shape=[_sds(b.shape, b.dtype) for b in bufs], input_output_aliases={t: t for t in range(n)},
        scratch_shapes=_dma_sems(n, n))(*bufs)


def _pair_sum(name, grad, got, ids):
    _, rows, cols = got.shape
    rb = min(rows, 256)
    nb = rows // rb
    blk = pl.BlockSpec((None, rb, cols), lambda i, j, ids: (j, i, 0))
    mine = pl.BlockSpec((None, rb, cols), lambda i, j, ids: (j, ids[1] * nb + i, 0))
    own = pl.BlockSpec((rb, cols), lambda i, j, ids: (i, 0))

    def body(ids_ref, a_ref, b_ref, s_ref, sb_ref):
        s = a_ref[...] + b_ref[...]
        sb_ref[...] = s.astype(BF16)

        @pl.when(pl.program_id(1) == ids_ref[0])
        def _():
            s_ref[...] = s

    return _prefetch_call(body, name, ids, (nb, N_CHIPS), [mine, blk], [own, blk],
                          [_sds((rows, cols), F32), _sds(got.shape, BF16)], (grad, got),
                          semantics=("parallel", "arbitrary"))


def _chip_sum(name, own_sum, got, ids):
    rows, cols = own_sum.shape
    rb = min(rows, 256)
    nb = rows // rb
    own = pl.BlockSpec((rb, cols), lambda i, ids: (i, 0))
    blk3 = pl.BlockSpec((N_PEERS, rb, cols), lambda i, ids: (0, i, 0))
    out = pl.BlockSpec((rb, cols), lambda i, ids: (ids[1] * nb + i, 0))

    def body(ids_ref, a_ref, b_ref, o_ref):
        o_ref[...] = ((a_ref[...] + b_ref[0].astype(F32)) + b_ref[1].astype(F32)) + b_ref[2].astype(F32)

    return _prefetch_call(body, name, ids, (nb,), [own, blk3], out, _sds((2 * rows, cols), F32), (own_sum, got))


SMALL_RB = 280


def _small_pair_sum(own, got):
    blk = pl.BlockSpec((SMALL_RB, 128), lambda i: (i, 0))

    def body(a_ref, b_ref, o_ref):
        o_ref[...] = a_ref[...] + b_ref[...]

    return pl.pallas_call(body, name="small_pair_sum", grid=(own.shape[0] // SMALL_RB,), in_specs=[blk, blk],
                          out_specs=blk, out_shape=_sds(own.shape, F32),
                          compiler_params=_params(("parallel",)))(own, got)


def _small_chip_sum(pair, got, ids):
    nb = pair.shape[0] // 2 // SMALL_RB
    half = pl.BlockSpec((SMALL_RB, 128), lambda i, ids: (ids[1] * nb + i, 0))
    blk3 = pl.BlockSpec((N_PEERS, SMALL_RB, 128), lambda i, ids: (0, i, 0))

    def body(ids_ref, a_ref, b_ref, o_ref):
        o_ref[...] = (a_ref[...] + b_ref[1]) + (b_ref[0] + b_ref[2])

    return _prefetch_call(body, "small_chip_sum", ids, (nb,), [half, blk3], half, _sds(pair.shape, F32), (pair, got))


def _adamw_math(w, g, m, v):
    m = ADAM_B1 * m + (1.0 - ADAM_B1) * g
    v = ADAM_B2 * v + (1.0 - ADAM_B2) * (g * g)
    m_hat = m / (1.0 - ADAM_B1 ** ADAM_STEP)
    v_hat = v / (1.0 - ADAM_B2 ** ADAM_STEP)
    delta = -ADAM_LR * (m_hat / (jnp.sqrt(v_hat) + ADAM_EPS) + ADAM_WD * w)
    return delta, m, v


def _adamw(name, w, g, m, v, rb=None):
    rows, cols = w.shape
    rb = rows if rb is None else rb
    blk = pl.BlockSpec((rb, cols), lambda i: (i, 0))

    def body(w_ref, g_ref, m_ref, v_ref, d_ref, nm_ref, nv_ref):
        d, nm, nv = _adamw_math(w_ref[...], g_ref[...], m_ref[...], v_ref[...])
        d_ref[...] = d
        nm_ref[...] = nm
        nv_ref[...] = nv

    return pl.pallas_call(body, name=name, grid=(rows // rb,), in_specs=[blk] * 4, out_specs=[blk] * 3,
                          out_shape=[_sds(w.shape, F32)] * 3, compiler_params=_params(("parallel",)))(w, g, m, v)


def _adamw_small(ws, gs, ms, vs):
    n = len(ws)

    def body(*refs):
        for t in range(n):
            w_ref, g_ref, m_ref, v_ref = (refs[k * n + t] for k in range(4))
            d, nm, nv = _adamw_math(w_ref[...], g_ref[...], m_ref[...], v_ref[...])
            for k, val in enumerate((d, nm, nv)):
                refs[(4 + k) * n + t][...] = val

    res = pl.pallas_call(body, name="adamw_small", out_shape=[_sds(a.shape, F32) for a in ws] * 3,
                         compiler_params=_params())(*ws, *gs, *ms, *vs)
    return [(res[t], res[n + t], res[2 * n + t]) for t in range(n)]


BIG = ("w_in", "w_att_o", "w_rec_o", "w_out", "w_ff1", "w_ff2")
SHARDED_VECS = ("conv_w", "b_rg_a", "b_rg_i", "lru_lambda")
SMALL = ("ln1_g", "b_in", "rpb", "conv_w", "conv_b", "w_rg_a", "b_rg_a", "w_rg_i", "b_rg_i", "lru_lambda",
         "ln2_g", "lnf_g")
SMALL_ROWS = 2240
ORDER = ("ln1_g", "w_in", "b_in", "rpb", "w_att_o", "conv_w", "conv_b", "w_rg_a", "b_rg_a", "w_rg_i", "b_rg_i",
         "lru_lambda", "w_rec_o", "w_out", "ln2_g", "w_ff1", "w_ff2", "lnf_g")


def _pack_small(grads, loss):
    parts, sizes = [], {}
    for n in SMALL:
        flat = grads[n].reshape(-1)
        pad = (-flat.shape[0]) % 128
        sizes[n] = (flat.shape[0], flat.shape[0] + pad)
        parts.append(jnp.pad(flat, (0, pad)))
    total = sum(s[1] for s in sizes.values())
    parts.append(jnp.pad(loss.reshape(1), (0, SMALL_ROWS * 128 - total - 1)))
    return jnp.concatenate(parts).reshape(SMALL_ROWS, 128), sizes


def _unpack_small(buf, sizes, shapes):
    flat = buf.reshape(-1)
    out, pos = {}, 0
    for n in SMALL:
        size, padded = sizes[n]
        out[n] = flat[pos:pos + size].reshape(shapes[n])
        pos += padded
    return out, flat[pos]


def _gather_weights(w, chip):
    chip_id = chip.astype(jnp.int32).reshape(1)
    vec_rows = [w[n][0] for n in SHARDED_VECS]
    vec_shard = jnp.concatenate(vec_rows + [jnp.zeros((16 - 10, D // N_CHIPS), F32)], axis=0)
    vec_slots = lax.dynamic_update_slice(jnp.zeros((N_CHIPS, 16, D // N_CHIPS), F32), vec_shard[None], (chip, 0, 0))
    bufs_a, sems_a, token_a = _gather_start("gather_start_first", [_cast_bf16("cast_w_in", w["w_in"][0], chip_id), vec_slots])
    rest_names = BIG[1:]
    bufs_b, sems_b, token_b = _gather_start(
        "gather_start_rest", [_cast_bf16("cast_" + n, w[n][0], chip_id, after=(token_a,)) for n in rest_names])

    def first(after):
        w_in_full, vec_full = _gather_forward("gather_forward_first", _gather_wait("gather_wait_first", bufs_a, sems_a, after))
        vecs = vec_full.transpose(1, 0, 2).reshape(16, D)
        return dict(w_in=w_in_full, conv_w=vecs[0:4], b_rg_a=vecs[4:6], b_rg_i=vecs[6:8], lru_lambda=vecs[8:10])

    def rest(after):
        full = dict(zip(rest_names, _gather_forward("gather_forward_rest",
                                                    _gather_wait("gather_wait_rest", bufs_b, sems_b, after))))
        return dict(w_att_o=full["w_att_o"], w_ff1=full["w_ff1"], w_rec_o=full["w_rec_o"].reshape(D, D),
                    w_out=full["w_out"].reshape(D, D), w_ff2=full["w_ff2"].reshape(D_FF, D))

    p = dict(ln1_g=w["ln1_g"], b_in=w["b_in"], rpb=w["rpb"][0], conv_b=w["conv_b"], w_rg_a=w["w_rg_a"][0],
             w_rg_i=w["w_rg_i"][0], ln2_g=w["ln2_g"], lnf_g=w["lnf_g"].reshape(1, D))
    return p, ((token_b,), first, rest)


class _Reducer:
    def __init__(self, ids):
        self.ids = ids
        self.groups = {}

    def begin(self, tag, grads, small=None):
        names = list(grads)
        big = [grads[n].reshape(N_CHIPS, -1, grads[n].shape[-1]) for n in names]
        flight, token = _pair_start("pair_start_" + tag, big, [] if small is None else [small])
        self.groups[tag] = dict(names=names, pair=flight, small=small is not None)
        return (token,)

    def advance(self, tag, after):
        grp = self.groups[tag]
        n = len(grp["names"])
        mine, got = _pair_wait("pair_wait_" + tag, grp["pair"], n, after)
        sums = [_pair_sum("pair_sum_" + name, a, b, self.ids) for name, a, b in zip(grp["names"], mine, got)]
        small_sum = _small_pair_sum(mine[n], got[n]) if grp["small"] else None
        grp["chip"], token = _chip_start("chip_start_" + tag, [s[1] for s in sums], small_sum)
        grp["sums"] = [s[0] for s in sums]
        self.last_token = token
        return (token,)

    def finish(self, tag, after):
        grp = self.groups[tag]
        srcs, lands = _chip_wait("chip_wait_" + tag, grp["chip"], grp["small"], after)
        halves = [_chip_sum("chip_sum_" + name, s, b, self.ids) for name, s, b in zip(grp["names"], grp["sums"], lands)]
        if grp["small"]:
            halves.append(_small_chip_sum(srcs[-1], lands[-1], self.ids))
        return _half_swap("half_swap_" + tag, halves)


def kernel(x, ln1_g, w_in, b_in, rpb, w_att_o, conv_w, conv_b, w_rg_a, b_rg_a, w_rg_i, b_rg_i, lru_lambda, w_rec_o, w_out, ln2_g, w_ff1, w_ff2, lnf_g, loss_target, m_ln1_g, m_w_in, m_b_in, m_rpb, m_w_att_o, m_conv_w, m_conv_b, m_w_rg_a, m_b_rg_a, m_w_rg_i, m_b_rg_i, m_lru_lambda, m_w_rec_o, m_w_out, m_ln2_g, m_w_ff1, m_w_ff2, m_lnf_g, v_ln1_g, v_w_in, v_b_in, v_rpb, v_w_att_o, v_conv_w, v_conv_b, v_w_rg_a, v_b_rg_a, v_w_rg_i, v_b_rg_i, v_lru_lambda, v_w_rec_o, v_w_out, v_ln2_g, v_w_ff1, v_w_ff2, v_lnf_g):
    w = dict(ln1_g=ln1_g, w_in=w_in, b_in=b_in, rpb=rpb, w_att_o=w_att_o, conv_w=conv_w, conv_b=conv_b,
             w_rg_a=w_rg_a, b_rg_a=b_rg_a, w_rg_i=w_rg_i, b_rg_i=b_rg_i, lru_lambda=lru_lambda, w_rec_o=w_rec_o,
             w_out=w_out, ln2_g=ln2_g, w_ff1=w_ff1, w_ff2=w_ff2, lnf_g=lnf_g)
    m = dict(ln1_g=m_ln1_g, w_in=m_w_in, b_in=m_b_in, rpb=m_rpb, w_att_o=m_w_att_o, conv_w=m_conv_w,
             conv_b=m_conv_b, w_rg_a=m_w_rg_a, b_rg_a=m_b_rg_a, w_rg_i=m_w_rg_i, b_rg_i=m_b_rg_i,
             lru_lambda=m_lru_lambda, w_rec_o=m_w_rec_o, w_out=m_w_out, ln2_g=m_ln2_g, w_ff1=m_w_ff1,
             w_ff2=m_w_ff2, lnf_g=m_lnf_g)
    v = dict(ln1_g=v_ln1_g, w_in=v_w_in, b_in=v_b_in, rpb=v_rpb, w_att_o=v_w_att_o, conv_w=v_conv_w,
             conv_b=v_conv_b, w_rg_a=v_w_rg_a, b_rg_a=v_b_rg_a, w_rg_i=v_w_rg_i, b_rg_i=v_b_rg_i,
             lru_lambda=v_lru_lambda, w_rec_o=v_w_rec_o, w_out=v_w_out, ln2_g=v_ln2_g, w_ff1=v_w_ff1,
             w_ff2=v_w_ff2, lnf_g=v_lnf_g)
    chip = 2 * lax.axis_index("x") + lax.axis_index("y")
    ids = jnp.stack([chip, lax.axis_index("c")]).astype(jnp.int32)

    out_grad, out_delta, out_m, out_v = {}, {}, {}, {}

    def update(n, gn):
        shape, two_d = w[n].shape, gn.shape
        d, nm, nv = _adamw("adamw_" + n, w[n].reshape(two_d), gn, m[n].reshape(two_d), v[n].reshape(two_d), 256)
        out_grad[n], out_delta[n], out_m[n], out_v[n] = (gn.reshape(shape), d.reshape(shape), nm.reshape(shape),
                                                         nv.reshape(shape))
        return d

    reducer = _Reducer(ids)
    p, late = _gather_weights(w, chip)
    loss, grad_x, g = _local_step(x, loss_target, p, late, reducer)
    small, sizes = _pack_small(g, loss + reducer.last_token[:1, :1])
    after = reducer.begin("small", {}, small)[0]
    for tag in ("ff", "proj", "in"):
        for n, red in zip(reducer.groups[tag]["names"], reducer.finish(tag, after)):
            after = update(n, red)
        if tag == "ff":
            after = reducer.advance("small", after)[0]
    (small_red,) = reducer.finish("small", after)
    gsmall, loss = _unpack_small(small_red, sizes, {n: g[n].shape for n in SMALL})
    two_d = {n: (int(np.prod(w[n].shape[:-1])), w[n].shape[-1]) for n in SMALL}
    for n in SHARDED_VECS:
        gsmall[n] = lax.dynamic_slice_in_dim(gsmall[n], chip * (D // N_CHIPS), D // N_CHIPS, axis=1)
    gs = [gsmall[n].reshape(two_d[n]) for n in SMALL]
    updates = _adamw_small([w[n].reshape(two_d[n]) for n in SMALL], gs, [m[n].reshape(two_d[n]) for n in SMALL],
                           [v[n].reshape(two_d[n]) for n in SMALL])
    for n, gn, (d, nm, nv) in zip(SMALL, gs, updates):
        shape = w[n].shape
        out_grad[n], out_delta[n], out_m[n], out_v[n] = (gn.reshape(shape), d.reshape(shape), nm.reshape(shape),
                                                         nv.reshape(shape))
    return (loss, grad_x, *[out_grad[n] for n in ORDER], *[out_delta[n] for n in ORDER],
            *[out_m[n] for n in ORDER], *[out_v[n] for n in ORDER])
```

```python
import functools

import numpy as np
import jax
import jax.numpy as jnp
from jax import lax
from jax.experimental import pallas as pl
from jax.experimental.pallas import tpu as pltpu

F32 = jnp.float32
BF16 = jnp.bfloat16

T = 2048
D = 1024
D_ATT = 512
D_IN = 5632
D_FF = 4096
N_HEADS = 8
HEAD_DIM = 64
GRID_W = 64
N_ROWS = T // GRID_W
WIN_H = 8
WIN_W = 16
KEYS = WIN_H * GRID_W
N_CHIPS = 4
EPS = 1e-6
LRU_C = 8.0
SCALE = HEAD_DIM ** -0.5
REC_CB = 256
REC_CHUNK = 256
PAD = 8

ADAM_LR = 0.001
ADAM_B1 = 0.9
ADAM_B2 = 0.999
ADAM_EPS = 1e-08
ADAM_WD = 0.01
ADAM_STEP = 10

VMEM_LIMIT = 56 * 1024 * 1024

NN = (((1,), (0,)), ((), ()))
NT = (((1,), (1,)), ((), ()))
TN = (((0,), (0,)), ((), ()))
MESH = pl.DeviceIdType.MESH


def _params(sem=None):
    return pltpu.CompilerParams(dimension_semantics=sem, vmem_limit_bytes=VMEM_LIMIT)


def _dot(a, b, dims):
    return lax.dot_general(a, b, dims, preferred_element_type=F32)


def _sigmoid(x):
    return 0.5 * jnp.tanh(0.5 * x) + 0.5


def _matmul(name, a, b, *, dims, grid, a_spec, b_spec, out_shapes, out_specs, acc_shape,
            extras=(), extra_specs=(), epilogue=None, colsum_spec=None, colsum_shape=None, after=(),
            semantics=("parallel", "parallel", "arbitrary"), epilogue_takes_first=False):
    nk = grid[2]
    n_extra = len(extras)
    n_out = len(out_shapes)
    with_colsum = colsum_spec is not None

    def body(a_ref, b_ref, *rest):
        ex = rest[:n_extra]
        rest = rest[:n_extra] + rest[n_extra + len(after):]
        outs = rest[n_extra:n_extra + n_out]
        pos = n_extra + n_out
        cs_out = rest[pos] if with_colsum else None
        pos += 1 if with_colsum else 0
        acc = rest[pos]
        cs_acc = rest[pos + 1] if with_colsum else None
        k = pl.program_id(2)
        first_tile = pl.program_id(0) == 0

        @pl.when(k == 0)
        def _():
            acc[...] = jnp.zeros_like(acc)
            if with_colsum:
                cs_acc[...] = jnp.zeros_like(cs_acc)

        bv = b_ref[...]
        acc[...] += _dot(a_ref[...].astype(BF16), bv.astype(BF16), dims)
        if with_colsum:
            cs_acc[...] += jnp.sum(bv.astype(F32), axis=0, keepdims=True)

        @pl.when(k == nk - 1)
        def _():
            r = acc[...]
            if epilogue is None:
                outs[0][...] = r.astype(outs[0].dtype)
            elif epilogue_takes_first:
                epilogue(r, ex, outs, first_tile)
            else:
                epilogue(r, ex, outs)
            if with_colsum:
                cs_out[...] = cs_acc[...]

    shapes = list(out_shapes)
    specs = list(out_specs)
    scratch = [pltpu.VMEM(acc_shape, F32)]
    if with_colsum:
        shapes.append(colsum_shape)
        specs.append(colsum_spec)
        scratch.append(pltpu.VMEM((1, acc_shape[1]), F32))
    res = pl.pallas_call(
        body, name=name, grid=grid,
        in_specs=[a_spec, b_spec, *extra_specs] + [_ANY] * len(after),
        out_specs=specs, out_shape=shapes, scratch_shapes=scratch,
        compiler_params=_params(semantics),
    )(a, b, *extras, *after)
    return res


def _sds(shape, dtype):
    return jax.ShapeDtypeStruct(shape, dtype)


TM = 1024
NI = T // TM


def _mm_nn_cols(name, a, wg, out_dtype, *, bias=None, extras=(), extra_specs=(), epilogue=None,
                out_shapes=None, out_specs=None):
    k_dim, n4 = wg.shape[1], wg.shape[2]
    ex, exs = list(extras), list(extra_specs)
    if bias is not None:
        ex = [bias] + ex
        exs = [pl.BlockSpec((1, n4), lambda j, i, k: (0, j))] + exs
        user_ep = epilogue

        def epilogue(r, e, outs):
            r = r + e[0][...]
            if user_ep is None:
                outs[0][...] = r.astype(outs[0].dtype)
            else:
                user_ep(r, e[1:], outs)
    if out_shapes is None:
        out_shapes = [_sds((T, N_CHIPS * n4), out_dtype)]
        out_specs = [pl.BlockSpec((TM, n4), lambda j, i, k: (i, j))]
    return _matmul(
        name, a, wg, dims=NN, grid=(N_CHIPS, NI, 1),
        a_spec=pl.BlockSpec((TM, k_dim), lambda j, i, k: (i, 0)),
        b_spec=pl.BlockSpec((None, k_dim, n4), lambda j, i, k: (j, 0, 0)),
        out_shapes=out_shapes, out_specs=out_specs, acc_shape=(TM, n4),
        extras=ex, extra_specs=exs, epilogue=epilogue)


def _mm_nt_cols_rms_bwd(name, a, wg, x, g, dres, after=(), bf16_copy=False):
    n4 = wg.shape[2]
    row = pl.BlockSpec((TM, D), lambda i, j, k: (i, 0))
    vec = pl.BlockSpec((1, D), lambda i, j, k: (0, 0))

    def epilogue(dhv, ex, outs, first):
        x_ref, g_ref, dres_ref = ex
        dx_ref, dg_ref = outs[0], outs[-1]
        xv = x_ref[...]
        rstd = lax.rsqrt(jnp.mean(xv * xv, axis=-1, keepdims=True) + EPS)
        xhat = xv * rstd
        dy = dhv * g_ref[...]
        dx = dres_ref[...] + rstd * (dy - xhat * jnp.mean(dy * xhat, axis=-1, keepdims=True))
        dx_ref[...] = dx
        if bf16_copy:
            outs[1][...] = dx.astype(BF16)
        part = jnp.sum(dhv * xhat, axis=0, keepdims=True)

        @pl.when(first)
        def _():
            dg_ref[...] = part

        @pl.when(jnp.logical_not(first))
        def _():
            dg_ref[...] += part

    return _matmul(
        name, a, wg, dims=NT, grid=(NI, 1, N_CHIPS),
        a_spec=pl.BlockSpec((TM, n4), lambda i, j, k: (i, k)),
        b_spec=pl.BlockSpec((None, D, n4), lambda i, j, k: (k, 0, 0)),
        out_shapes=[_sds((T, D), F32)] + [_sds((T, D), BF16)] * bf16_copy + [_sds((1, D), F32)],
        out_specs=[row] + [row] * bf16_copy + [vec], acc_shape=(TM, D),
        extras=[x, g, dres], extra_specs=[row, vec, row], epilogue=epilogue, after=after,
        semantics=("arbitrary", "arbitrary", "arbitrary"), epilogue_takes_first=True)


def _mm_nt_rows(name, a, w, out_dtype, *, tn, extras=(), extra_specs=(), epilogue=None):
    k_dim, n = w.shape
    return _matmul(
        name, a, w, dims=NT, grid=(k_dim // tn, NI, 1),
        a_spec=pl.BlockSpec((TM, n), lambda j, i, k: (i, 0)),
        b_spec=pl.BlockSpec((tn, n), lambda j, i, k: (j, 0)),
        out_shapes=[_sds((T, k_dim), out_dtype)],
        out_specs=[pl.BlockSpec((TM, tn), lambda j, i, k: (i, j))], acc_shape=(TM, tn),
        extras=extras, extra_specs=extra_specs, epilogue=epilogue)


def _mm_tn_cols(name, a, g, n4, *, colsum=False):
    k_dim = a.shape[1]
    kw = {}
    if colsum:
        kw = dict(colsum_spec=pl.BlockSpec((1, n4), lambda j, i, k: (0, j)),
                  colsum_shape=_sds((1, N_CHIPS * n4), F32))
    return _matmul(
        name, a, g, dims=TN, grid=(N_CHIPS, 1, NI),
        a_spec=pl.BlockSpec((TM, k_dim), lambda j, i, k: (k, 0)),
        b_spec=pl.BlockSpec((TM, n4), lambda j, i, k: (k, j)),
        out_shapes=[_sds((N_CHIPS, k_dim, n4), F32)],
        out_specs=[pl.BlockSpec((None, k_dim, n4), lambda j, i, k: (j, 0, 0))],
        acc_shape=(k_dim, n4), **kw)


def _mm_tn_rows(name, a, g, *, tm):
    k_dim, n = a.shape[1], g.shape[1]
    return _matmul(
        name, a, g, dims=TN, grid=(k_dim // tm, 1, NI),
        a_spec=pl.BlockSpec((TM, tm), lambda j, i, k: (k, j)),
        b_spec=pl.BlockSpec((TM, n), lambda j, i, k: (k, 0)),
        out_shapes=[_sds((k_dim, n), F32)],
        out_specs=[pl.BlockSpec((tm, n), lambda j, i, k: (j, 0))], acc_shape=(tm, n))


TE = 256
NE = T // TE
_ROW = pl.BlockSpec((TE, D), lambda i: (i, 0))
_VEC = pl.BlockSpec((1, D), lambda i: (0, 0))


def _rms_fwd(name, x, g, after=()):
    def body(x_ref, g_ref, *rest):
        h_ref = rest[-1]
        xv = x_ref[...]
        rstd = lax.rsqrt(jnp.mean(xv * xv, axis=-1, keepdims=True) + EPS)
        h_ref[...] = (xv * rstd * g_ref[...]).astype(BF16)

    return pl.pallas_call(body, name=name, grid=(NE,), in_specs=[_ROW, _VEC] + [_ANY] * len(after), out_specs=_ROW,
                          out_shape=_sds((T, D), BF16), compiler_params=_params(("parallel",)))(x, g, *after)


def _mm_x2_loss_head(s, w_ff2, x1, target, g):
    k_dim = w_ff2.shape[0]
    row = pl.BlockSpec((TM, D), lambda i, j, k: (i, 0))
    vec = pl.BlockSpec((1, D), lambda i, j, k: (0, 0))

    def epilogue(r, ex, outs, first):
        x1_ref, t_ref, g_ref = ex
        loss_ref, dx_ref, dxb_ref, dg_ref = outs
        xv = x1_ref[...] + r
        rstd = lax.rsqrt(jnp.mean(xv * xv, axis=-1, keepdims=True) + EPS)
        xhat = xv * rstd
        gv = g_ref[...]
        err = xhat * gv - t_ref[...]
        dy = err * (1.0 / D)
        dxh = dy * gv
        dx = rstd * (dxh - xhat * jnp.mean(dxh * xhat, axis=-1, keepdims=True))
        dx_ref[...] = dx
        dxb_ref[...] = dx.astype(BF16)
        dg_part = jnp.sum(dy * xhat, axis=0, keepdims=True)
        loss_part = (0.5 / D) * jnp.sum(jnp.sum(err * err, axis=1, keepdims=True), axis=0, keepdims=True)

        @pl.when(first)
        def _():
            dg_ref[...] = dg_part
            loss_ref[...] = loss_part

        @pl.when(jnp.logical_not(first))
        def _():
            dg_ref[...] += dg_part
            loss_ref[...] += loss_part

    return _matmul(
        "mm_x2_loss_head", s, w_ff2, dims=NN, grid=(NI, 1, k_dim // D),
        a_spec=pl.BlockSpec((TM, D), lambda i, j, k: (i, k)), b_spec=pl.BlockSpec((D, D), lambda i, j, k: (k, 0)),
        out_shapes=[_sds((1, 1), F32), _sds((T, D), F32), _sds((T, D), BF16), _sds((1, D), F32)],
        out_specs=[pl.BlockSpec((1, 1), lambda i, j, k: (0, 0)), row, row, vec], acc_shape=(TM, D),
        extras=[x1, target, g], extra_specs=[row, row, vec], epilogue=epilogue,
        semantics=("arbitrary", "arbitrary", "arbitrary"), epilogue_takes_first=True)


MW = 512
_G_ATT_BLK = 3584 // MW
_G_REC_BLK = 4608 // MW


TB = 512


def _branch_specs():
    def row(cols):
        return pl.BlockSpec((TB, cols), lambda i: (i, 0))

    ga = pl.BlockSpec((TB, MW), lambda i: (i, _G_ATT_BLK))
    ga2 = pl.BlockSpec((TB, MW), lambda i: (i, _G_ATT_BLK + 1))
    gr = pl.BlockSpec((TB, MW), lambda i: (i, _G_REC_BLK))
    gr2 = pl.BlockSpec((TB, MW), lambda i: (i, _G_REC_BLK + 1))
    w_att = pl.BlockSpec((N_CHIPS, D_ATT, D // N_CHIPS), lambda i: (0, 0, 0))
    w_sq = pl.BlockSpec((D, D), lambda i: (0, 0))
    return row, (ga, ga2, gr, gr2), w_att, w_sq


def _gate_values(gate_refs):
    ga, ga2, gr, gr2 = (r[...] for r in gate_refs)
    return _sigmoid(jnp.concatenate([ga, ga2], axis=1)), _sigmoid(jnp.concatenate([gr, gr2], axis=1))


def _branches_fwd(att, g, z, x, w_att_o, w_rec_o, w_out, ln2_g):
    row, gate_specs, w_att, w_sq = _branch_specs()

    def body(att_ref, g_ref, ga_ref, ga2_ref, gr_ref, gr2_ref, x_ref, wa_ref, wr_ref, wo_ref, g2_ref,
             ya_ref, yr_ref, m_ref, x1_ref, h2_ref):
        attv = att_ref[...]
        ya = jnp.concatenate([_dot(attv, wa_ref[j], NN) for j in range(N_CHIPS)], axis=1)
        yr = _dot(g_ref[...], wr_ref[...], NN)
        sa, sr = _gate_values((ga_ref, ga2_ref, gr_ref, gr2_ref))
        mixed = (sa * ya + sr * yr).astype(BF16)
        ya_ref[...] = ya
        yr_ref[...] = yr
        m_ref[...] = mixed
        x1 = x_ref[...] + _dot(mixed, wo_ref[...], NN)
        x1_ref[...] = x1
        rstd = lax.rsqrt(jnp.mean(x1 * x1, axis=-1, keepdims=True) + EPS)
        h2_ref[...] = (x1 * rstd * g2_ref[...]).astype(BF16)

    return pl.pallas_call(
        body, name="branches_fwd", grid=(T // TB,),
        in_specs=[row(D_ATT), row(D), *gate_specs, row(D), w_att, w_sq, w_sq, pl.BlockSpec((1, D), lambda i: (0, 0))],
        out_specs=[row(D)] * 5,
        out_shape=[_sds((T, D), F32), _sds((T, D), F32), _sds((T, D), BF16), _sds((T, D), F32), _sds((T, D), BF16)],
        compiler_params=_params(("parallel",)))(att, g, z, z, z, z, x, w_att_o, w_rec_o, w_out, ln2_g)


def _branches_bwd(dx1_b, y_att, y_rec, z, w_att_o, w_rec_o, w_out):
    row, gate_specs, w_att, w_sq = _branch_specs()
    n4 = D // N_CHIPS

    def body(dx_ref, ya_ref, yr_ref, ga_ref, ga2_ref, gr_ref, gr2_ref, wa_ref, wr_ref, wo_ref,
             dya_ref, dyr_ref, dga_ref, dgr_ref, datt_ref, dg_ref):
        dm = _dot(dx_ref[...], wo_ref[...], NT)
        sa, sr = _gate_values((ga_ref, ga2_ref, gr_ref, gr2_ref))
        dya = (dm * sa).astype(BF16)
        dyr = (dm * sr).astype(BF16)
        dya_ref[...] = dya
        dyr_ref[...] = dyr
        dga_ref[...] = (dm * ya_ref[...] * sa * (1.0 - sa)).astype(BF16)
        dgr_ref[...] = (dm * yr_ref[...] * sr * (1.0 - sr)).astype(BF16)
        datt = _dot(dya[:, 0:n4], wa_ref[0], NT)
        for j in range(1, N_CHIPS):
            datt = datt + _dot(dya[:, j * n4:(j + 1) * n4], wa_ref[j], NT)
        datt_ref[...] = datt.astype(BF16)
        dg_ref[...] = _dot(dyr, wr_ref[...], NT).astype(BF16)

    return pl.pallas_call(
        body, name="branches_bwd", grid=(T // TB,),
        in_specs=[row(D), row(D), row(D), *gate_specs, w_att, w_sq, w_sq],
        out_specs=[row(D)] * 4 + [row(D_ATT), row(D)],
        out_shape=[_sds((T, D), BF16)] * 4 + [_sds((T, D_ATT), BF16), _sds((T, D), BF16)],
        compiler_params=_params(("parallel",)))(dx1_b, y_att, y_rec, z, z, z, z, w_att_o, w_rec_o, w_out)


HP = 2 * HEAD_DIM
N_HP = N_HEADS // 2
ATT_UNROLL_FWD = 8
ATT_UNROLL_BWD = 4
DIAG_ROWS = 32


def _window_maps():
    diag = np.zeros((GRID_W * GRID_W, 128), np.float32)
    for qc in range(GRID_W):
        w0 = min(max(qc - WIN_W // 2, 0), GRID_W - WIN_W)
        for kc in range(w0, w0 + WIN_W):
            diag[qc * GRID_W + kc, kc - qc + WIN_W - 1] = 1.0
    return diag, diag.sum(axis=1)[None, :]


def _split3(x):
    a = x.astype(BF16)
    r = x - a.astype(F32)
    b = r.astype(BF16)
    c = (r - b.astype(F32)).astype(BF16)
    return a, b, c


N_DROW = 2 * WIN_H - 1
N_DPAIR = N_DROW - 1


def _bias_pairs(rpb):
    diag, valid = _window_maps()
    r2 = jnp.pad(rpb.reshape(N_HEADS * N_DROW, 2 * WIN_W - 1),
                 ((0, 128 - N_HEADS * N_DROW), (0, 128 - (2 * WIN_W - 1))))

    def body(r_ref, d_ref, v_ref, o_ref):
        dv = d_ref[...]
        t = sum(_dot(part, dv, NN) for part in _split3(r_ref[...]))
        o_ref[...] = jnp.where(v_ref[...] > 0.0, t, -1e30)

    t = pl.pallas_call(body, name="rpb_expand", out_shape=_sds((128, GRID_W * GRID_W), F32),
                       compiler_params=_params())(r2, jnp.asarray(diag.T, BF16), jnp.asarray(valid, F32))
    t = t[:N_HEADS * N_DROW].reshape(N_HEADS, N_DROW, GRID_W, GRID_W)
    return jnp.concatenate([t[:, :N_DPAIR], t[:, 1:]], axis=-1)


def _row_bias(tb_ref, hh, d0):
    return jnp.concatenate([tb_ref[hh, d0 + 2 * ii] for ii in range(WIN_H // 2)], axis=1)


def _row_window(r):
    rs = jnp.clip(r - WIN_H // 2, 0, N_ROWS - WIN_H)
    return pl.multiple_of(r * GRID_W, GRID_W), pl.multiple_of(rs * GRID_W, GRID_W), rs - r + (WIN_H - 1)


def _split_heads(src_ref, dst_ref, scale=None):
    for hh in range(2):
        v = src_ref[:, hh * HEAD_DIM:(hh + 1) * HEAD_DIM]
        dst_ref[hh] = (v if scale is None else v * scale).astype(BF16)


def _attn_items(qb_ref, kb_ref, vb_ref, tb_ref, first_row, n_rows):
    wins = [_row_window(first_row + u) for u in range(n_rows)]
    items = [(u, hh) for u in range(n_rows) for hh in range(2)]
    q = [qb_ref[hh, pl.ds(wins[u][0], GRID_W), :] for u, hh in items]
    k = [kb_ref[hh, pl.ds(wins[u][1], KEYS), :] for u, hh in items]
    v = [vb_ref[hh, pl.ds(wins[u][1], KEYS), :] for u, hh in items]
    s = [_dot(qi, ki, NT) + _row_bias(tb_ref, hh, wins[u][2]) for qi, ki, (u, hh) in zip(q, k, items)]
    m = [jnp.max(si, axis=-1, keepdims=True) for si in s]
    e = [jnp.exp(si - mi) for si, mi in zip(s, m)]
    inv = [1.0 / jnp.sum(ei, axis=-1, keepdims=True) for ei in e]
    p = [ei * li for ei, li in zip(e, inv)]
    return wins, items, q, k, v, p


def _attn_in_specs():
    q = pl.BlockSpec((T, HP), lambda p: (0, p))
    k = pl.BlockSpec((T, HP), lambda p: (0, N_HP + p))
    v = pl.BlockSpec((T, HP), lambda p: (0, 2 * N_HP + p))
    tb = pl.BlockSpec((2, N_DPAIR, GRID_W, HP), lambda p: (p, 0, 0, 0))
    return q, k, v, tb


_HEAD_SCRATCH = pltpu.VMEM((2, T, HEAD_DIM), BF16)


def _attn_fwd(z, tb):
    def body(q_ref, k_ref, v_ref, tb_ref, o_ref, qb_ref, kb_ref, vb_ref):
        _split_heads(q_ref, qb_ref, SCALE)
        _split_heads(k_ref, kb_ref)
        _split_heads(v_ref, vb_ref)

        def rows(it, carry):
            wins, items, _, _, v, p = _attn_items(qb_ref, kb_ref, vb_ref, tb_ref, it * ATT_UNROLL_FWD, ATT_UNROLL_FWD)
            o = [_dot(pi.astype(BF16), vi, NN) for pi, vi in zip(p, v)]
            for u, (q0, _, _) in enumerate(wins):
                o_ref[pl.ds(q0, GRID_W), :] = jnp.concatenate(o[2 * u:2 * u + 2], axis=1).astype(BF16)
            return carry

        lax.fori_loop(0, N_ROWS // ATT_UNROLL_FWD, rows, 0)

    blk = pl.BlockSpec((T, HP), lambda p: (0, p))
    return pl.pallas_call(
        body, name="attn_fwd", grid=(N_HP,), in_specs=list(_attn_in_specs()), out_specs=blk,
        out_shape=_sds((T, D_ATT), BF16), scratch_shapes=[_HEAD_SCRATCH] * 3,
        compiler_params=_params(("parallel",)))(z, z, z, tb)


def _attn_bwd(z, tb, d_att, after=()):
    def body(q_ref, k_ref, v_ref, tb_ref, do_ref, flip_ref, *rest):
        (dq_ref, dk_ref, dv_ref, diag_ref, qb_ref, kb_ref, vb_ref, dob_ref, dka_ref, dva_ref,
         ds_ref) = rest[len(after):]
        _split_heads(q_ref, qb_ref, SCALE)
        _split_heads(k_ref, kb_ref)
        _split_heads(v_ref, vb_ref)
        _split_heads(do_ref, dob_ref)
        dka_ref[...] = jnp.zeros_like(dka_ref)
        dva_ref[...] = jnp.zeros_like(dva_ref)
        ds_ref[...] = jnp.zeros_like(ds_ref)

        def rows(it, carry):
            wins, items, q, k, v, p = _attn_items(qb_ref, kb_ref, vb_ref, tb_ref, it * ATT_UNROLL_BWD, ATT_UNROLL_BWD)
            do = [dob_ref[hh, pl.ds(wins[u][0], GRID_W), :] for u, hh in items]
            dv = [_dot(pi.astype(BF16), di, TN) for pi, di in zip(p, do)]
            dp = [_dot(di, vi, NT) for di, vi in zip(do, v)]
            ds = [pi * (dpi - jnp.sum(dpi * pi, axis=-1, keepdims=True)) for pi, dpi in zip(p, dp)]
            dsb = [d.astype(BF16) for d in ds]
            dq = [_dot(d, ki, NN) * SCALE for d, ki in zip(dsb, k)]
            dk = [_dot(d, qi, TN) for d, qi in zip(dsb, q)]
            for d, (u, hh) in zip(ds, items):
                for ii in range(WIN_H // 2):
                    ds_ref[hh, wins[u][2] + 2 * ii] += d[:, ii * HP:(ii + 1) * HP]
            for u, (q0, k0, _) in enumerate(wins):
                dq_ref[pl.ds(q0, GRID_W), :] = jnp.concatenate(dq[2 * u:2 * u + 2], axis=1).astype(BF16)
                dka_ref[pl.ds(k0, KEYS), :] += jnp.concatenate(dk[2 * u:2 * u + 2], axis=1)
                dva_ref[pl.ds(k0, KEYS), :] += jnp.concatenate(dv[2 * u:2 * u + 2], axis=1)
            return carry

        lax.fori_loop(0, N_ROWS // ATT_UNROLL_BWD, rows, 0)
        dk_ref[...] = dka_ref[...].astype(BF16)
        dv_ref[...] = dva_ref[...].astype(BF16)
        _diag_sums(ds_ref, flip_ref, diag_ref)

    blk = pl.BlockSpec((T, HP), lambda p: (0, p))
    q, k, v, tbs = _attn_in_specs()
    flip = jnp.asarray(np.eye(HP, dtype=np.float32)[::-1], BF16)
    return pl.pallas_call(
        body, name="attn_bwd", grid=(N_HP,),
        in_specs=[q, k, v, tbs, blk, pl.BlockSpec((HP, HP), lambda p: (0, 0))] + [_ANY] * len(after),
        out_specs=[blk, blk, blk, pl.BlockSpec((None, DIAG_ROWS, HP), lambda p: (p, 0, 0))],
        out_shape=[_sds((T, D_ATT), BF16)] * 3 + [_sds((N_HP, DIAG_ROWS, HP), F32)],
        scratch_shapes=[_HEAD_SCRATCH] * 4 + [pltpu.VMEM((T, HP), F32), pltpu.VMEM((T, HP), F32),
                                              pltpu.VMEM((2, N_DPAIR, GRID_W, HP), F32)],
        compiler_params=_params(("parallel",)))(z, z, z, tb, d_att, flip, *after)


def _diag_sums(acc_ref, flip_ref, out_ref):
    flip = flip_ref[...]
    rows = []
    for hh in range(2):
        for pair in range(N_DPAIR):
            reversed_lanes = sum(_dot(part, flip, NN) for part in _split3(acc_ref[hh, pair]))
            skewed = pltpu.roll(reversed_lanes, 0, 1, stride=1, stride_axis=0)
            rows.append(jnp.sum(skewed, axis=0, keepdims=True))
    rows.append(jnp.zeros((DIAG_ROWS - len(rows), HP), F32))
    out_ref[...] = jnp.concatenate(rows, axis=0)


def _rpb_grad(diag_sums):
    g = diag_sums.reshape(N_HP * DIAG_ROWS, HP)
    sel = np.zeros((2, 128, N_HP * DIAG_ROWS), np.float32)
    lane = np.zeros((2, HP, 128), np.float32)
    for h in range(N_HEADS):
        for pair in range(N_DPAIR):
            for half in range(2):
                sel[half, h * N_DROW + pair + half, (h // 2) * DIAG_ROWS + (h % 2) * N_DPAIR + pair] = 1.0
    for j in range(2 * WIN_W - 1):
        for half in range(2):
            lane[half, (HP - 1 - GRID_W * half - (j - (WIN_W - 1))) % HP, j] = 1.0

    def body(g_ref, sel_ref, lane_ref, o_ref):
        parts = _split3(g_ref[...])
        total = None
        for half in range(2):
            picked = sum(_dot(sel_ref[half], part, NN) for part in parts)
            term = sum(_dot(part, lane_ref[half], NN) for part in _split3(picked))
            total = term if total is None else total + term
        o_ref[...] = total

    out = pl.pallas_call(body, name="rpb_grad", out_shape=_sds((128, 128), F32),
                         compiler_params=_params())(g, jnp.asarray(sel, BF16), jnp.asarray(lane, BF16))
    return out[:N_HEADS * N_DROW, :2 * WIN_W - 1].reshape(N_HEADS, N_DROW, 2 * WIN_W - 1)


N_CB = D // REC_CB
N_CHUNK = T // REC_CHUNK
N_TILE = T // 8
_U_BLK = 1536 // REC_CB
_Y_BLK = 2560 // REC_CB


def _block_diag(w):
    per = REC_CB // 64
    wt = w.reshape(2, N_CB, per, 64, 64)
    eye = jnp.eye(per, dtype=w.dtype)
    full = wt[:, :, :, :, None, :] * eye[None, None, :, None, :, None]
    return full.reshape(2, N_CB, REC_CB, REC_CB).astype(BF16)


def _block_diag_grad(g):
    per = REC_CB // 64
    g6 = g.reshape(2, N_CB, per, 64, per, 64)
    return jnp.stack([g6[:, :, p, :, p, :] for p in range(per)], axis=2).reshape(2, 16, 64, 64)


def _gelu(x):
    c = 0.7978845608028654
    return 0.5 * x * (1.0 + jnp.tanh(c * (x + 0.044715 * x * x * x)))


def _gelu_grad(x):
    c = 0.7978845608028654
    th = jnp.tanh(c * (x + 0.044715 * x * x * x))
    return 0.5 * (1.0 + th) + 0.5 * x * (1.0 - th * th) * c * (1.0 + 3.0 * 0.044715 * x * x)


def _softplus_neg(lam):
    x = -lam
    e = jnp.exp(-jnp.abs(x))
    w = 1.0 + e
    l1p = jnp.where(w == 1.0, e, jnp.log(w) * e / (w - 1.0))
    return jnp.maximum(x, 0.0) + l1p


def _one_minus_exp(x):
    poly = x * (1.0 + x * (1 / 2 + x * (1 / 6 + x * (1 / 24 + x * (1 / 120 + x * (1 / 720))))))
    return jnp.where(x > -0.125, -poly, 1.0 - jnp.exp(x))


def _conv_taps(pad_ref, t0, w, sign):
    out = None
    for j in range(4):
        term = w[j:j + 1, :] * pad_ref[pl.ds(PAD + t0 + sign * (j - 2), REC_CHUNK), :]
        out = term if out is None else out + term
    return out


def _gates(u, wa, wi, ba, bi, sp):
    ub = u.astype(BF16)
    r = _sigmoid(_dot(ub, wa, NN) + ba)
    i = _sigmoid(_dot(ub, wi, NN) + bi)
    log_a = -LRU_C * r * sp
    a = jnp.exp(log_a)
    x = jnp.maximum(_one_minus_exp(2.0 * log_a), 0.0)
    positive = x > 0.0
    inv = lax.rsqrt(jnp.where(positive, x, 1.0))
    mult = jnp.where(positive, x * inv, 0.0)
    return r, i, a, mult, jnp.where(positive, inv, 0.0)


def _tile_scan(a, b, sub, reverse):
    for s in (1, 2, 4):
        if reverse:
            a_s, b_s, m = pltpu.roll(a, 8 - s, 0), pltpu.roll(b, 8 - s, 0), sub < 8 - s
        else:
            a_s, b_s, m = pltpu.roll(a, s, 0), pltpu.roll(b, s, 0), sub >= s
        b = jnp.where(m, a * b_s + b, b)
        a = jnp.where(m, a * a_s, a)
    return a, b


def _last_row(x, row):
    return jnp.broadcast_to(x[row:row + 1, :], x.shape)


def _rec_prologue(up_ref, cw_ref, cb_ref, wa_ref, wi_ref, ba_ref, bi_ref, lam_ref,
                  upad_ref, u_ref, a_refs, h_refs):
    cb = up_ref.shape[1]
    zeros = jnp.zeros((PAD, cb), F32)
    upad_ref[pl.ds(0, PAD), :] = zeros
    upad_ref[pl.ds(PAD + T, PAD), :] = zeros
    upad_ref[pl.ds(PAD, T), :] = up_ref[...]
    cw = cw_ref[...]
    sp = _softplus_neg(lam_ref[...])
    for c in range(N_CHUNK):
        t0 = c * REC_CHUNK
        u = cb_ref[...] + _conv_taps(upad_ref, t0, cw, 1)
        u_ref[pl.ds(t0, REC_CHUNK), :] = u
        for d in range(2):
            _, i, a, mult, _ = _gates(u, wa_ref[d], wi_ref[d], ba_ref[d:d + 1, :], bi_ref[d:d + 1, :], sp[d:d + 1, :])
            a_refs[d][pl.ds(t0, REC_CHUNK), :] = a
            h_refs[d][pl.ds(t0, REC_CHUNK), :] = mult * (i * u)

    sub = lax.broadcasted_iota(jnp.int32, (8, cb), 0)

    def tile(k, carry):
        cf, cr = carry
        tf = pl.multiple_of(k * 8, 8)
        tr = pl.multiple_of((N_TILE - 1 - k) * 8, 8)
        af, bf = _tile_scan(a_refs[0][pl.ds(tf, 8), :], h_refs[0][pl.ds(tf, 8), :], sub, False)
        hf = af * cf + bf
        h_refs[0][pl.ds(tf, 8), :] = hf
        ar, br = _tile_scan(a_refs[1][pl.ds(tr, 8), :], h_refs[1][pl.ds(tr, 8), :], sub, True)
        hr = ar * cr + br
        h_refs[1][pl.ds(tr, 8), :] = hr
        return _last_row(af, 7) * cf + _last_row(bf, 7), _last_row(ar, 0) * cr + _last_row(br, 0)

    z8 = jnp.zeros((8, cb), F32)
    lax.fori_loop(0, N_TILE, tile, (z8, z8))
    return sp


def _rec_specs():
    up = pl.BlockSpec((T, REC_CB), lambda c: (0, _U_BLK + c))
    yb = pl.BlockSpec((T, REC_CB), lambda c: (0, _Y_BLK + c))
    cw = pl.BlockSpec((4, REC_CB), lambda c: (0, c))
    cbias = pl.BlockSpec((1, REC_CB), lambda c: (0, c))
    wbd = pl.BlockSpec((2, None, REC_CB, REC_CB), lambda c: (0, c, 0, 0))
    vec2 = pl.BlockSpec((2, REC_CB), lambda c: (0, c))
    col = pl.BlockSpec((T, REC_CB), lambda c: (0, c))
    return up, yb, cw, cbias, wbd, vec2, col


def _rec_fwd(z, conv_w, conv_b, wa, wi, ba, bi, lam):
    up, yb, cw, cbias, wbd, vec2, col = _rec_specs()

    def body(up_ref, yb_ref, cw_ref, cb_ref, wa_ref, wi_ref, ba_ref, bi_ref, lam_ref, g_ref,
             u_ref, af_ref, ar_ref, hf_ref, hr_ref, upad_ref):
        _rec_prologue(up_ref, cw_ref, cb_ref, wa_ref, wi_ref, ba_ref, bi_ref, lam_ref,
                      upad_ref, u_ref, (af_ref, ar_ref), (hf_ref, hr_ref))

        def chunk(c, carry):
            t0 = pl.multiple_of(c * REC_CHUNK, REC_CHUNK)
            rows = pl.ds(t0, REC_CHUNK)
            g_ref[rows, :] = ((hf_ref[rows, :] + hr_ref[rows, :]) * _gelu(yb_ref[rows, :])).astype(BF16)
            return carry

        lax.fori_loop(0, N_CHUNK, chunk, 0)

    res = pl.pallas_call(
        body, name="rec_fwd", grid=(N_CB,),
        in_specs=[up, yb, cw, cbias, wbd, wbd, vec2, vec2, vec2], out_specs=[col] * 6,
        out_shape=[_sds((T, D), BF16)] + [_sds((T, D), F32)] * 5,
        scratch_shapes=[pltpu.VMEM((T + 2 * PAD, REC_CB), F32)],
        compiler_params=_params(("parallel",)))(z, z, conv_w, conv_b, wa, wi, ba, bi, lam)
    return res[0], tuple(res[1:])


def _rec_bwd(z, dg, saved, conv_w, conv_b, wa, wi, ba, bi, lam, after=()):
    up, yb, cw, cbias, wbd, vec2, col = _rec_specs()

    def body(up_ref, yb_ref, dg_ref, u_ref, af_ref, ar_ref, hf_ref, hr_ref,
             cw_ref, cb_ref, wa_ref, wi_ref, ba_ref, bi_ref, lam_ref, *rest):
        (dup_ref, dyb_ref, dcw_ref, dcb_ref, dwa_ref, dwi_ref, dba_ref, dbi_ref, dlam_ref,
         upad_ref, dh_ref, gf_ref, gr_ref, daf_ref, dar_ref, dupad_ref) = rest[len(after):]
        g_refs, da_refs = (gf_ref, gr_ref), (daf_ref, dar_ref)
        cb = up_ref.shape[1]
        zeros = jnp.zeros((PAD, cb), F32)
        upad_ref[pl.ds(0, PAD), :] = zeros
        upad_ref[pl.ds(PAD + T, PAD), :] = zeros
        upad_ref[pl.ds(PAD, T), :] = up_ref[...]
        sp = _softplus_neg(lam_ref[...])

        def gate_chunk(c, carry):
            t0 = pl.multiple_of(c * REC_CHUNK, REC_CHUNK)
            rows = pl.ds(t0, REC_CHUNK)
            y = yb_ref[rows, :]
            dgv = dg_ref[rows, :].astype(F32)
            dh_ref[rows, :] = dgv * _gelu(y)
            dyb_ref[rows, :] = (dgv * (hf_ref[rows, :] + hr_ref[rows, :]) * _gelu_grad(y)).astype(BF16)
            return carry

        lax.fori_loop(0, N_CHUNK, gate_chunk, 0)

        sub = lax.broadcasted_iota(jnp.int32, (8, cb), 0)

        def tile(k, carry):
            cf, cr = carry
            kf = N_TILE - 1 - k
            tf = pl.multiple_of(kf * 8, 8)
            tnext = pl.multiple_of(jnp.minimum(kf + 1, N_TILE - 1) * 8, 8)
            tprev = pl.multiple_of(jnp.maximum(kf - 1, 0) * 8, 8)
            a_t = af_ref[pl.ds(tf, 8), :]
            a_n = jnp.where(kf < N_TILE - 1, af_ref[pl.ds(tnext, 8), :], 0.0)
            a_sh = jnp.where(sub == 7, pltpu.roll(a_n, 7, 0), pltpu.roll(a_t, 7, 0))
            ca, cbb = _tile_scan(a_sh, dh_ref[pl.ds(tf, 8), :], sub, True)
            gf = ca * cf + cbb
            h_t = hf_ref[pl.ds(tf, 8), :]
            h_p = jnp.where(kf > 0, hf_ref[pl.ds(tprev, 8), :], 0.0)
            h_sh = jnp.where(sub == 0, pltpu.roll(h_p, 1, 0), pltpu.roll(h_t, 1, 0))
            gf_ref[pl.ds(tf, 8), :] = gf
            daf_ref[pl.ds(tf, 8), :] = gf * h_sh
            tr = pl.multiple_of(k * 8, 8)
            rnext = pl.multiple_of(jnp.minimum(k + 1, N_TILE - 1) * 8, 8)
            rprev = pl.multiple_of(jnp.maximum(k - 1, 0) * 8, 8)
            b_t = ar_ref[pl.ds(tr, 8), :]
            b_p = jnp.where(k > 0, ar_ref[pl.ds(rprev, 8), :], 0.0)
            b_sh = jnp.where(sub == 0, pltpu.roll(b_p, 1, 0), pltpu.roll(b_t, 1, 0))
            ra, rb = _tile_scan(b_sh, dh_ref[pl.ds(tr, 8), :], sub, False)
            gr = ra * cr + rb
            hr_t = hr_ref[pl.ds(tr, 8), :]
            hr_n = jnp.where(k < N_TILE - 1, hr_ref[pl.ds(rnext, 8), :], 0.0)
            hr_sh = jnp.where(sub == 7, pltpu.roll(hr_n, 7, 0), pltpu.roll(hr_t, 7, 0))
            gr_ref[pl.ds(tr, 8), :] = gr
            dar_ref[pl.ds(tr, 8), :] = gr * hr_sh
            return _last_row(gf, 0), _last_row(gr, 7)

        z8 = jnp.zeros((8, cb), F32)
        lax.fori_loop(0, N_TILE, tile, (z8, z8))

        dupad_ref[pl.ds(0, PAD), :] = zeros
        dupad_ref[pl.ds(PAD + T, PAD), :] = zeros
        dwa_ref[...] = jnp.zeros_like(dwa_ref)
        dwi_ref[...] = jnp.zeros_like(dwi_ref)
        dba_ref[...] = jnp.zeros_like(dba_ref)
        dbi_ref[...] = jnp.zeros_like(dbi_ref)
        dlam_ref[...] = jnp.zeros_like(dlam_ref)

        def grad_chunk(c, carry):
            t0 = pl.multiple_of(c * REC_CHUNK, REC_CHUNK)
            rows = pl.ds(t0, REC_CHUNK)
            u = u_ref[rows, :]
            ub = u.astype(BF16)
            du = jnp.zeros((REC_CHUNK, cb), F32)
            for d in range(2):
                r, i, a, mult, inv_mult = _gates(u, wa_ref[d], wi_ref[d], ba_ref[d:d + 1, :], bi_ref[d:d + 1, :],
                                                 sp[d:d + 1, :])
                dbx = g_refs[d][rows, :]
                dmult = dbx * (i * u)
                diu = dbx * mult
                a2 = a * a
                dlog = da_refs[d][rows, :] * a - dmult * (a2 * inv_mult)
                dpa = (dlog * (-LRU_C) * sp[d:d + 1, :]) * r * (1.0 - r)
                dpi = (diu * u) * i * (1.0 - i)
                dpab, dpib = dpa.astype(BF16), dpi.astype(BF16)
                du = du + diu * i + _dot(dpab, wa_ref[d], NT) + _dot(dpib, wi_ref[d], NT)
                dwa_ref[d] += _dot(ub, dpab, TN)
                dwi_ref[d] += _dot(ub, dpib, TN)
                dba_ref[d:d + 1, :] += jnp.sum(dpa, axis=0, keepdims=True)
                dbi_ref[d:d + 1, :] += jnp.sum(dpi, axis=0, keepdims=True)
                dlam_ref[d:d + 1, :] += jnp.sum(dlog * r, axis=0, keepdims=True)
            dupad_ref[pl.ds(PAD + t0, REC_CHUNK), :] = du
            return carry

        lax.fori_loop(0, N_CHUNK, grad_chunk, 0)
        dlam_ref[...] = dlam_ref[...] * (LRU_C * _sigmoid(-lam_ref[...]))

        cw = cw_ref[...]
        dcb = jnp.zeros((1, cb), F32)
        dcw = [jnp.zeros((1, cb), F32) for _ in range(4)]
        for c in range(N_CHUNK):
            t0 = c * REC_CHUNK
            du = dupad_ref[pl.ds(PAD + t0, REC_CHUNK), :]
            dcb = dcb + jnp.sum(du, axis=0, keepdims=True)
            for j in range(4):
                dcw[j] = dcw[j] + jnp.sum(du * upad_ref[pl.ds(PAD + t0 + j - 2, REC_CHUNK), :], axis=0, keepdims=True)
            dup_ref[pl.ds(t0, REC_CHUNK), :] = _conv_taps(dupad_ref, t0, cw, -1).astype(BF16)
        dcb_ref[...] = dcb
        dcw_ref[...] = jnp.concatenate(dcw, axis=0)

    full = pltpu.VMEM((T, REC_CB), F32)
    padded = pltpu.VMEM((T + 2 * PAD, REC_CB), F32)
    return pl.pallas_call(
        body, name="rec_bwd", grid=(N_CB,),
        in_specs=[up, yb] + [col] * 6 + [cw, cbias, wbd, wbd, vec2, vec2, vec2] + [_ANY] * len(after),
        out_specs=[col, col, cw, cbias, wbd, wbd, vec2, vec2, vec2],
        out_shape=[_sds((T, D), BF16), _sds((T, D), BF16), _sds((4, D), F32), _sds((1, D), F32),
                   _sds((2, N_CB, REC_CB, REC_CB), F32), _sds((2, N_CB, REC_CB, REC_CB), F32),
                   _sds((2, D), F32), _sds((2, D), F32), _sds((2, D), F32)],
        scratch_shapes=[padded, full, full, full, full, full, padded],
        compiler_params=_params(("parallel",)))(z, z, dg, *saved, conv_w, conv_b, wa, wi, ba, bi, lam, *after)


class _NoReducer:
    def begin(self, tag, grads):
        return ()

    def advance(self, tag, after):
        return ()


def _local_step(x, target, p, late=None, reducer=_NoReducer()):
    x = x.reshape(T, D)
    target = target.reshape(T, D)
    tb = _bias_pairs(p["rpb"])
    wa, wi = _block_diag(p["w_rg_a"]), _block_diag(p["w_rg_i"])

    h1 = _rms_fwd("rms1_fwd", x, p["ln1_g"], after=late[0] if late else ())
    if late:
        p = {**p, **late[1]((h1, tb, wa, wi))}
    rec_params = (p["conv_w"], p["conv_b"], wa, wi, p["b_rg_a"], p["b_rg_i"], p["lru_lambda"])
    (z,) = _mm_nn_cols("mm_z", h1, p["w_in"], F32, bias=p["b_in"])
    att = _attn_fwd(z, tb)
    g, rec_saved = _rec_fwd(z, *rec_params)
    if late:
        p = {**p, **late[2](g)}
    y_att, y_rec, mixed, x1, h2 = _branches_fwd(att, g, z, x, p["w_att_o"], p["w_rec_o"], p["w_out"], p["ln2_g"])

    def relu2(r, ex, outs):
        rp = jnp.maximum(r, 0.0)
        outs[0][...] = (rp * rp).astype(BF16)

    (s,) = _mm_nn_cols("mm_ff1", h2, p["w_ff1"], BF16, epilogue=relu2)
    loss, dx2, dx2_b, g_lnf = _mm_x2_loss_head(s, p["w_ff2"], x1, target, p["lnf_g"])

    def relu2_bwd(r, ex, outs):
        outs[0][...] = (r * 2.0 * jnp.sqrt(ex[0][...].astype(F32))).astype(BF16)

    (df,) = _mm_nt_rows("mm_df", dx2_b, p["w_ff2"], BF16, tn=D, extras=[s],
                        extra_specs=[pl.BlockSpec((TM, D), lambda j, i, k: (i, j))], epilogue=relu2_bwd)
    (g_w_ff2,) = _mm_tn_rows("mm_g_ff2", s, dx2_b, tm=D)
    (g_w_ff1,) = _mm_tn_cols("mm_g_ff1", h2, df, D)
    tok = reducer.begin("ff", dict(w_ff2=g_w_ff2, w_ff1=g_w_ff1))
    dx1, dx1_b, g_ln2 = _mm_nt_cols_rms_bwd("mm_dh2_rms2_bwd", df, p["w_ff1"], x1, p["ln2_g"], dx2, after=tok,
                                            bf16_copy=True)

    dy_att, dy_rec, dg_att, dg_rec, d_att, d_g = _branches_bwd(dx1_b, y_att, y_rec, z, p["w_att_o"], p["w_rec_o"],
                                                               p["w_out"])
    (g_w_out,) = _mm_tn_rows("mm_g_out", mixed, dx1_b, tm=D)
    (g_w_att_o,) = _mm_tn_cols("mm_g_att_o", att, dy_att, D // N_CHIPS)
    (g_w_rec_o,) = _mm_tn_rows("mm_g_rec_o", g, dy_rec, tm=D)
    tok = reducer.advance("ff", g_w_rec_o) + reducer.begin("proj", dict(w_out=g_w_out, w_att_o=g_w_att_o, w_rec_o=g_w_rec_o))

    dq, dk, dv, ds_acc = _attn_bwd(z, tb, d_att, after=tok)
    g_rpb = _rpb_grad(ds_acc)
    tok = reducer.advance("proj", dq)
    d_up, d_yb, g_conv_w, g_conv_b, g_wa, g_wi, g_ba, g_bi, g_lam = _rec_bwd(z, d_g, rec_saved, *rec_params, after=tok)
    dz = jnp.concatenate([dq, dk, dv, d_up, d_yb, dg_att, dg_rec], axis=1)

    g_w_in, g_b_in = _mm_tn_cols("mm_g_in", h1, dz, D_IN // N_CHIPS, colsum=True)
    tok = reducer.begin("in", dict(w_in=g_w_in))
    grad_x, g_ln1 = _mm_nt_cols_rms_bwd("mm_dh1_rms1_bwd", dz, p["w_in"], x, p["ln1_g"], dx1, after=tok)
    reducer.advance("in", grad_x)

    grads = dict(ln1_g=g_ln1, w_in=g_w_in, b_in=g_b_in, rpb=g_rpb, w_att_o=g_w_att_o, conv_w=g_conv_w,
                 conv_b=g_conv_b, w_rg_a=_block_diag_grad(g_wa), b_rg_a=g_ba, w_rg_i=_block_diag_grad(g_wi),
                 b_rg_i=g_bi, lru_lambda=g_lam, w_rec_o=g_w_rec_o, w_out=g_w_out, ln2_g=g_ln2,
                 w_ff1=g_w_ff1, w_ff2=g_w_ff2, lnf_g=g_lnf)
    return loss, grad_x.reshape(1, T, D), grads


_ANY = pl.BlockSpec(memory_space=pl.ANY)
N_PEERS = N_CHIPS - 1


def _place():
    x, y, c = lax.axis_index("x"), lax.axis_index("y"), lax.axis_index("c")
    peers = [(1 - x, y), (x, 1 - y), (1 - x, 1 - y)]
    return x, y, c, 2 * x + y, peers


def _remote(src, dst, send_sem, recv_sem, dev):
    return pltpu.make_async_remote_copy(src_ref=src, dst_ref=dst, send_sem=send_sem, recv_sem=recv_sem,
                                        device_id=dev, device_id_type=MESH)


def _prefetch_call(body, name, ids, grid, in_specs, out_specs, out_shape, args, semantics=None):
    spec = pltpu.PrefetchScalarGridSpec(num_scalar_prefetch=1, grid=grid, in_specs=in_specs, out_specs=out_specs)
    return pl.pallas_call(body, name=name, grid_spec=spec, out_shape=out_shape,
                          compiler_params=_params(semantics or ("parallel",) * len(grid)))(ids, *args)


def _cast_bf16(name, w, chip_id, after=()):
    rows, cols = w.shape
    rb = min(rows, 256)

    def body(ids_ref, w_ref, *rest):
        rest[-1][...] = w_ref[...].astype(BF16)

    return _prefetch_call(body, name, chip_id, (rows // rb,),
                          [pl.BlockSpec((rb, cols), lambda i, ids: (i, 0))] + [_ANY] * len(after),
                          pl.BlockSpec((None, rb, cols), lambda i, ids: (ids[0], i, 0)),
                          _sds((N_CHIPS, rows, cols), BF16), (w, *after))


def _dma_sems(*counts):
    return [pltpu.SemaphoreType.DMA((k,)) for k in counts]


_HBM = pl.BlockSpec(memory_space=pltpu.HBM)
_SEM = pl.BlockSpec(memory_space=pltpu.SEMAPHORE)
_SPLIT_COPY = pltpu.CompilerParams(has_side_effects=pltpu.SideEffectType.DATAFLOW_SIDE_EFFECTING)


def _hbm(arrays):
    return [pltpu.with_memory_space_constraint(a, pltpu.HBM) for a in arrays]


def _hbm_like(arrays):
    return [pltpu.HBM(a.shape, a.dtype) for a in arrays]


def _halves(buf, c):
    half = buf.shape[1] // 2
    return pl.ds(c * half, half), pl.ds((1 - c) * half, half)


def _gather_start(name, slots):
    n = len(slots)
    nk = n * N_PEERS

    def body(*refs):
        bufs = refs[n:2 * n]
        send_sems, recv_sems, token = refs[2 * n:]
        x, y, c, chip, peers = _place()
        for t in range(n):
            mine, _ = _halves(bufs[t], c)
            for r, (px, py) in enumerate(peers):
                k = t * N_PEERS + r
                own = bufs[t].at[chip, mine]
                _remote(own, own, send_sems.at[k], recv_sems.at[k], (px, py, c)).start()
        token[...] = jnp.zeros_like(token)

    res = pl.pallas_call(
        body, name=name, in_specs=[_HBM] * n, out_specs=[_HBM] * n + [_SEM, _SEM, pl.BlockSpec(memory_space=pltpu.VMEM)],
        out_shape=_hbm_like(slots) + [pltpu.SemaphoreType.DMA((nk,)), pltpu.SemaphoreType.DMA((nk,)),
                                      _sds((8, 128), F32)],
        input_output_aliases={t: t for t in range(n)}, compiler_params=_SPLIT_COPY)(*_hbm(slots))
    return res[:n], (res[n], res[n + 1]), res[n + 2]


def _gather_wait(name, bufs, sems, after):
    n = len(bufs)
    after = tuple(after) if isinstance(after, (tuple, list)) else (after,)

    def body(*refs):
        ins = refs[:n]
        send_sems, recv_sems = refs[n], refs[n + 1]
        x, y, c, chip, peers = _place()
        for t in range(n):
            mine, _ = _halves(ins[t], c)
            for r, (px, py) in enumerate(peers):
                k = t * N_PEERS + r
                cp = _remote(ins[t].at[chip, mine], ins[t].at[2 * px + py, mine], send_sems.at[k], recv_sems.at[k],
                             (px, py, c))
                cp.wait_send()
                cp.wait_recv()

    return pl.pallas_call(
        body, name=name, in_specs=[_HBM] * n + [_SEM, _SEM] + [_ANY] * len(after), out_specs=[_HBM] * n,
        out_shape=_hbm_like(bufs), input_output_aliases={t: t for t in range(n)},
        compiler_params=_SPLIT_COPY)(*bufs, *sems, *after)


def _gather_forward(name, bufs):
    n = len(bufs)
    nk = n * N_PEERS

    def body(*refs):
        outs = refs[n:2 * n]
        send_sems, recv_sems = refs[2 * n:]
        x, y, c, chip, peers = _place()
        sibling = (x, y, 1 - c)
        sends = []
        for t in range(n):
            mine, _ = _halves(outs[t], c)
            for r, (px, py) in enumerate(peers):
                k = t * N_PEERS + r
                landed = outs[t].at[2 * px + py, mine]
                sends.append(_remote(landed, landed, send_sems.at[k], recv_sems.at[k], sibling))
                sends[-1].start()
        for t in range(n):
            _, theirs = _halves(outs[t], c)
            for r, (px, py) in enumerate(peers):
                k = t * N_PEERS + r
                landed = outs[t].at[2 * px + py, theirs]
                _remote(landed, landed, send_sems.at[k], recv_sems.at[k], sibling).wait_recv()
        for cp in sends:
            cp.wait_send()

    return pl.pallas_call(
        body, name=name, in_specs=[_ANY] * n, out_specs=[_ANY] * n, out_shape=[_sds(b.shape, b.dtype) for b in bufs],
        input_output_aliases={t: t for t in range(n)}, scratch_shapes=_dma_sems(nk, nk))(*bufs)


def _pair_copies(n, srcs, lands, send_sems, recv_sems):
    x, y, c, _, _ = _place()
    sibling = (x, y, 1 - c)
    copies = []
    for t in range(n):
        half = srcs[t].shape[1] // 2
        for j in range(N_CHIPS):
            k = t * N_CHIPS + j
            copies.append(_remote(srcs[t].at[j, pl.ds((1 - c) * half, half)], lands[t].at[j],
                                  send_sems.at[k], recv_sems.at[k], sibling))
    for t in range(n, len(srcs)):
        k = n * N_CHIPS + t - n
        copies.append(_remote(srcs[t], lands[t], send_sems.at[k], recv_sems.at[k], sibling))
    return copies


def _pair_start(name, grads, wholes=()):
    n = len(grads)
    srcs = list(grads) + list(wholes)
    m = len(srcs)
    lands = [pltpu.HBM((N_CHIPS, g.shape[1] // 2, g.shape[2]), F32) for g in grads] + _hbm_like(wholes)
    ns = n * N_CHIPS + len(wholes)

    def body(*refs):
        src_refs, land_refs = refs[m:2 * m], refs[2 * m:3 * m]
        send_sems, recv_sems, token = refs[3 * m:]
        for cp in _pair_copies(n, src_refs, land_refs, send_sems, recv_sems):
            cp.start()
        token[...] = jnp.zeros_like(token)

    res = pl.pallas_call(
        body, name=name, in_specs=[_HBM] * m,
        out_specs=[_HBM] * (2 * m) + [_SEM, _SEM, pl.BlockSpec(memory_space=pltpu.VMEM)],
        out_shape=_hbm_like(srcs) + lands + [pltpu.SemaphoreType.DMA((ns,)), pltpu.SemaphoreType.DMA((ns,)),
                                             _sds((8, 128), F32)],
        input_output_aliases={t: t for t in range(m)}, compiler_params=_SPLIT_COPY)(*_hbm(srcs))
    return (res[:m], res[m:2 * m], (res[2 * m], res[2 * m + 1])), res[2 * m + 2]


def _pair_wait(name, flight, n, after):
    srcs, lands, sems = flight
    m = len(srcs)

    def body(*refs):
        for cp in _pair_copies(n, refs[:m], refs[m:2 * m], refs[2 * m], refs[2 * m + 1]):
            cp.wait_send()
            cp.wait_recv()

    res = pl.pallas_call(
        body, name=name, in_specs=[_HBM] * (2 * m) + [_SEM, _SEM, _ANY], out_specs=[_HBM] * (2 * m),
        out_shape=_hbm_like(srcs) + _hbm_like(lands), input_output_aliases={t: t for t in range(2 * m)},
        compiler_params=_SPLIT_COPY)(*srcs, *lands, *sems, after)
    return res[:m], res[m:]


def _chip_copies(srcs, lands, small_src, small_land, send_sems, recv_sems):
    x, y, c, chip, peers = _place()
    n = len(srcs)
    copies = []
    for r, (px, py) in enumerate(peers):
        for t in range(n):
            k = t * N_PEERS + r
            copies.append(_remote(srcs[t].at[2 * px + py], lands[t].at[r], send_sems.at[k], recv_sems.at[k], (px, py, c)))
        if small_src is not None:
            k = n * N_PEERS + r
            half_s = small_src.shape[0] // 2
            copies.append(_remote(small_src.at[pl.ds(c * half_s, half_s)], small_land.at[r],
                                  send_sems.at[k], recv_sems.at[k], (px, py, c)))
    return copies


def _chip_start(name, sums_bf16, small=None):
    n = len(sums_bf16)
    srcs = list(sums_bf16) + ([small] if small is not None else [])
    m = len(srcs)
    lands = [pltpu.HBM((N_PEERS,) + s.shape[1:], BF16) for s in sums_bf16]
    if small is not None:
        lands.append(pltpu.HBM((N_PEERS, small.shape[0] // 2, 128), F32))
    nk = m * N_PEERS

    def body(*refs):
        src_refs, land_refs = refs[m:2 * m], refs[2 * m:3 * m]
        send_sems, recv_sems, token = refs[3 * m:]
        small_src, small_land = (src_refs[n], land_refs[n]) if small is not None else (None, None)
        for cp in _chip_copies(src_refs[:n], land_refs[:n], small_src, small_land, send_sems, recv_sems):
            cp.start()
        token[...] = jnp.zeros_like(token)

    res = pl.pallas_call(
        body, name=name, in_specs=[_HBM] * m,
        out_specs=[_HBM] * (2 * m) + [_SEM, _SEM, pl.BlockSpec(memory_space=pltpu.VMEM)],
        out_shape=_hbm_like(srcs) + lands + [pltpu.SemaphoreType.DMA((nk,)), pltpu.SemaphoreType.DMA((nk,)),
                                             _sds((8, 128), F32)],
        input_output_aliases={t: t for t in range(m)}, compiler_params=_SPLIT_COPY)(*_hbm(srcs))
    return (res[:m], res[m:2 * m], (res[2 * m], res[2 * m + 1])), res[2 * m + 2]


def _chip_wait(name, flight, with_small, after):
    srcs, lands, sems = flight
    m = len(srcs)
    n = m - 1 if with_small else m

    def body(*refs):
        src_refs, land_refs = refs[:m], refs[m:2 * m]
        send_sems, recv_sems = refs[2 * m], refs[2 * m + 1]
        small_src, small_land = (src_refs[n], land_refs[n]) if with_small else (None, None)
        for cp in _chip_copies(src_refs[:n], land_refs[:n], small_src, small_land, send_sems, recv_sems):
            cp.wait_send()
            cp.wait_recv()

    res = pl.pallas_call(
        body, name=name, in_specs=[_HBM] * (2 * m) + [_SEM, _SEM, _ANY], out_specs=[_HBM] * (2 * m),
        out_shape=_hbm_like(srcs) + _hbm_like(lands), input_output_aliases={t: t for t in range(2 * m)},
        compiler_params=_SPLIT_COPY)(*srcs, *lands, *sems, after)
    return res[:m], res[m:]


def _half_swap(name, bufs):
    n = len(bufs)

    def body(*refs):
        outs = refs[n:2 * n]
        send_sem, recv_sem = refs[2 * n:]
        x, y, c, _, _ = _place()
        sibling = (x, y, 1 - c)
        copies = []
        for t in range(n):
            h = outs[t].shape[0] // 2
            mine = outs[t].at[pl.ds(c * h, h)]
            copies.append(_remote(mine, mine, send_sem.at[t], recv_sem.at[t], sibling))
            copies[-1].start()
        for t in range(n):
            h = outs[t].shape[0] // 2
            theirs = outs[t].at[pl.ds((1 - c) * h, h)]
            _remote(theirs, theirs, send_sem.at[t], recv_sem.at[t], sibling).wait_recv()
        for cp in copies:
            cp.wait_send()

    return pl.pallas_call(
        body, name=name, in_specs=[_ANY] * n, out_specs=[_ANY] * n,
        out_shape=[_sds(b.shape, b.dtype) for b in bufs], input_output_aliases={t: t for t in range(n)},
        scratch_shapes=_dma_sems(n, n))(*bufs)


def _pair_sum(name, grad, got, ids):
    _, rows, cols = got.shape
    rb = min(rows, 256)
    nb = rows // rb
    blk = pl.BlockSpec((None, rb, cols), lambda i, j, ids: (j, i, 0))
    mine = pl.BlockSpec((None, rb, cols), lambda i, j, ids: (j, ids[1] * nb + i, 0))
    own = pl.BlockSpec((rb, cols), lambda i, j, ids: (i, 0))

    def body(ids_ref, a_ref, b_ref, s_ref, sb_ref):
        s = a_ref[...] + b_ref[...]
        sb_ref[...] = s.astype(BF16)

        @pl.when(pl.program_id(1) == ids_ref[0])
        def _():
            s_ref[...] = s

    return _prefetch_call(body, name, ids, (nb, N_CHIPS), [mine, blk], [own, blk],
                          [_sds((rows, cols), F32), _sds(got.shape, BF16)], (grad, got),
                          semantics=("parallel", "arbitrary"))


def _chip_sum(name, own_sum, got, ids):
    rows, cols = own_sum.shape
    rb = min(rows, 256)
    nb = rows // rb
    own = pl.BlockSpec((rb, cols), lambda i, ids: (i, 0))
    blk3 = pl.BlockSpec((N_PEERS, rb, cols), lambda i, ids: (0, i, 0))
    out = pl.BlockSpec((rb, cols), lambda i, ids: (ids[1] * nb + i, 0))

    def body(ids_ref, a_ref, b_ref, o_ref):
        o_ref[...] = ((a_ref[...] + b_ref[0].astype(F32)) + b_ref[1].astype(F32)) + b_ref[2].astype(F32)

    return _prefetch_call(body, name, ids, (nb,), [own, blk3], out, _sds((2 * rows, cols), F32), (own_sum, got))


SMALL_RB = 280


def _small_pair_sum(own, got):
    blk = pl.BlockSpec((SMALL_RB, 128), lambda i: (i, 0))

    def body(a_ref, b_ref, o_ref):
        o_ref[...] = a_ref[...] + b_ref[...]

    return pl.pallas_call(body, name="small_pair_sum", grid=(own.shape[0] // SMALL_RB,), in_specs=[blk, blk],
                          out_specs=blk, out_shape=_sds(own.shape, F32),
                          compiler_params=_params(("parallel",)))(own, got)


def _small_chip_sum(pair, got, ids):
    nb = pair.shape[0] // 2 // SMALL_RB
    half = pl.BlockSpec((SMALL_RB, 128), lambda i, ids: (ids[1] * nb + i, 0))
    blk3 = pl.BlockSpec((N_PEERS, SMALL_RB, 128), lambda i, ids: (0, i, 0))

    def body(ids_ref, a_ref, b_ref, o_ref):
        o_ref[...] = (a_ref[...] + b_ref[1]) + (b_ref[0] + b_ref[2])

    return _prefetch_call(body, "small_chip_sum", ids, (nb,), [half, blk3], half, _sds(pair.shape, F32), (pair, got))


def _adamw_math(w, g, m, v):
    m = ADAM_B1 * m + (1.0 - ADAM_B1) * g
    v = ADAM_B2 * v + (1.0 - ADAM_B2) * (g * g)
    m_hat = m / (1.0 - ADAM_B1 ** ADAM_STEP)
    v_hat = v / (1.0 - ADAM_B2 ** ADAM_STEP)
    delta = -ADAM_LR * (m_hat / (jnp.sqrt(v_hat) + ADAM_EPS) + ADAM_WD * w)
    return delta, m, v


def _adamw(name, w, g, m, v, rb=None):
    rows, cols = w.shape
    rb = rows if rb is None else rb
    blk = pl.BlockSpec((rb, cols), lambda i: (i, 0))

    def body(w_ref, g_ref, m_ref, v_ref, d_ref, nm_ref, nv_ref):
        d, nm, nv = _adamw_math(w_ref[...], g_ref[...], m_ref[...], v_ref[...])
        d_ref[...] = d
        nm_ref[...] = nm
        nv_ref[...] = nv

    return pl.pallas_call(body, name=name, grid=(rows // rb,), in_specs=[blk] * 4, out_specs=[blk] * 3,
                          out_shape=[_sds(w.shape, F32)] * 3, compiler_params=_params(("parallel",)))(w, g, m, v)


def _adamw_small(ws, gs, ms, vs):
    n = len(ws)

    def body(*refs):
        for t in range(n):
            w_ref, g_ref, m_ref, v_ref = (refs[k * n + t] for k in range(4))
            d, nm, nv = _adamw_math(w_ref[...], g_ref[...], m_ref[...], v_ref[...])
            for k, val in enumerate((d, nm, nv)):
                refs[(4 + k) * n + t][...] = val

    res = pl.pallas_call(body, name="adamw_small", out_shape=[_sds(a.shape, F32) for a in ws] * 3,
                         compiler_params=_params())(*ws, *gs, *ms, *vs)
    return [(res[t], res[n + t], res[2 * n + t]) for t in range(n)]


BIG = ("w_in", "w_att_o", "w_rec_o", "w_out", "w_ff1", "w_ff2")
SHARDED_VECS = ("conv_w", "b_rg_a", "b_rg_i", "lru_lambda")
SMALL = ("ln1_g", "b_in", "rpb", "conv_w", "conv_b", "w_rg_a", "b_rg_a", "w_rg_i", "b_rg_i", "lru_lambda",
         "ln2_g", "lnf_g")
SMALL_ROWS = 2240
ORDER = ("ln1_g", "w_in", "b_in", "rpb", "w_att_o", "conv_w", "conv_b", "w_rg_a", "b_rg_a", "w_rg_i", "b_rg_i",
         "lru_lambda", "w_rec_o", "w_out", "ln2_g", "w_ff1", "w_ff2", "lnf_g")


def _pack_small(grads, loss):
    parts, sizes = [], {}
    for n in SMALL:
        flat = grads[n].reshape(-1)
        pad = (-flat.shape[0]) % 128
        sizes[n] = (flat.shape[0], flat.shape[0] + pad)
        parts.append(jnp.pad(flat, (0, pad)))
    total = sum(s[1] for s in sizes.values())
    parts.append(jnp.pad(loss.reshape(1), (0, SMALL_ROWS * 128 - total - 1)))
    return jnp.concatenate(parts).reshape(SMALL_ROWS, 128), sizes


def _unpack_small(buf, sizes, shapes):
    flat = buf.reshape(-1)
    out, pos = {}, 0
    for n in SMALL:
        size, padded = sizes[n]
        out[n] = flat[pos:pos + size].reshape(shapes[n])
        pos += padded
    return out, flat[pos]


def _gather_weights(w, chip):
    chip_id = chip.astype(jnp.int32).reshape(1)
    vec_rows = [w[n][0] for n in SHARDED_VECS]
    vec_shard = jnp.concatenate(vec_rows + [jnp.zeros((16 - 10, D // N_CHIPS), F32)], axis=0)
    vec_slots = lax.dynamic_update_slice(jnp.zeros((N_CHIPS, 16, D // N_CHIPS), F32), vec_shard[None], (chip, 0, 0))
    bufs_a, sems_a, token_a = _gather_start("gather_start_first", [_cast_bf16("cast_w_in", w["w_in"][0], chip_id), vec_slots])
    rest_names = BIG[1:]
    bufs_b, sems_b, token_b = _gather_start(
        "gather_start_rest", [_cast_bf16("cast_" + n, w[n][0], chip_id, after=(token_a,)) for n in rest_names])

    def first(after):
        w_in_full, vec_full = _gather_forward("gather_forward_first", _gather_wait("gather_wait_first", bufs_a, sems_a, after))
        vecs = vec_full.transpose(1, 0, 2).reshape(16, D)
        return dict(w_in=w_in_full, conv_w=vecs[0:4], b_rg_a=vecs[4:6], b_rg_i=vecs[6:8], lru_lambda=vecs[8:10])

    def rest(after):
        full = dict(zip(rest_names, _gather_forward("gather_forward_rest",
                                                    _gather_wait("gather_wait_rest", bufs_b, sems_b, after))))
        return dict(w_att_o=full["w_att_o"], w_ff1=full["w_ff1"], w_rec_o=full["w_rec_o"].reshape(D, D),
                    w_out=full["w_out"].reshape(D, D), w_ff2=full["w_ff2"].reshape(D_FF, D))

    p = dict(ln1_g=w["ln1_g"], b_in=w["b_in"], rpb=w["rpb"][0], conv_b=w["conv_b"], w_rg_a=w["w_rg_a"][0],
             w_rg_i=w["w_rg_i"][0], ln2_g=w["ln2_g"], lnf_g=w["lnf_g"].reshape(1, D))
    return p, ((token_b,), first, rest)


class _Reducer:
    def __init__(self, ids):
        self.ids = ids
        self.groups = {}

    def begin(self, tag, grads, small=None):
        names = list(grads)
        big = [grads[n].reshape(N_CHIPS, -1, grads[n].shape[-1]) for n in names]
        flight, token = _pair_start("pair_start_" + tag, big, [] if small is None else [small])
        self.groups[tag] = dict(names=names, pair=flight, small=small is not None)
        return (token,)

    def advance(self, tag, after):
        grp = self.groups[tag]
        n = len(grp["names"])
        mine, got = _pair_wait("pair_wait_" + tag, grp["pair"], n, after)
        sums = [_pair_sum("pair_sum_" + name, a, b, self.ids) for name, a, b in zip(grp["names"], mine, got)]
        small_sum = _small_pair_sum(mine[n], got[n]) if grp["small"] else None
        grp["chip"], token = _chip_start("chip_start_" + tag, [s[1] for s in sums], small_sum)
        grp["sums"] = [s[0] for s in sums]
        self.last_token = token
        return (token,)

    def finish(self, tag, after):
        grp = self.groups[tag]
        srcs, lands = _chip_wait("chip_wait_" + tag, grp["chip"], grp["small"], after)
        halves = [_chip_sum("chip_sum_" + name, s, b, self.ids) for name, s, b in zip(grp["names"], grp["sums"], lands)]
        if grp["small"]:
            halves.append(_small_chip_sum(srcs[-1], lands[-1], self.ids))
        return _half_swap("half_swap_" + tag, halves)


def kernel(x, ln1_g, w_in, b_in, rpb, w_att_o, conv_w, conv_b, w_rg_a, b_rg_a, w_rg_i, b_rg_i, lru_lambda, w_rec_o, w_out, ln2_g, w_ff1, w_ff2, lnf_g, loss_target, m_ln1_g, m_w_in, m_b_in, m_rpb, m_w_att_o, m_conv_w, m_conv_b, m_w_rg_a, m_b_rg_a, m_w_rg_i, m_b_rg_i, m_lru_lambda, m_w_rec_o, m_w_out, m_ln2_g, m_w_ff1, m_w_ff2, m_lnf_g, v_ln1_g, v_w_in, v_b_in, v_rpb, v_w_att_o, v_conv_w, v_conv_b, v_w_rg_a, v_b_rg_a, v_w_rg_i, v_b_rg_i, v_lru_lambda, v_w_rec_o, v_w_out, v_ln2_g, v_w_ff1, v_w_ff2, v_lnf_g):
    w = dict(ln1_g=ln1_g, w_in=w_in, b_in=b_in, rpb=rpb, w_att_o=w_att_o, conv_w=conv_w, conv_b=conv_b,
             w_rg_a=w_rg_a, b_rg_a=b_rg_a, w_rg_i=w_rg_i, b_rg_i=b_rg_i, lru_lambda=lru_lambda, w_rec_o=w_rec_o,
             w_out=w_out, ln2_g=ln2_g, w_ff1=w_ff1, w_ff2=w_ff2, lnf_g=lnf_g)
    m = dict(ln1_g=m_ln1_g, w_in=m_w_in, b_in=m_b_in, rpb=m_rpb, w_att_o=m_w_att_o, conv_w=m_conv_w,
             conv_b=m_conv_b, w_rg_a=m_w_rg_a, b_rg_a=m_b_rg_a, w_rg_i=m_w_rg_i, b_rg_i=m_b_rg_i,
             lru_lambda=m_lru_lambda, w_rec_o=m_w_rec_o, w_out=m_w_out, ln2_g=m_ln2_g, w_ff1=m_w_ff1,
             w_ff2=m_w_ff2, lnf_g=m_lnf_g)
    v = dict(ln1_g=v_ln1_g, w_in=v_w_in, b_in=v_b_in, rpb=v_rpb, w_att_o=v_w_att_o, conv_w=v_conv_w,
             conv_b=v_conv_b, w_rg_a=v_w_rg_a, b_rg_a=v_b_rg_a, w_rg_i=v_w_rg_i, b_rg_i=v_b_rg_i,
             lru_lambda=v_lru_lambda, w_rec_o=v_w_rec_o, w_out=v_w_out, ln2_g=v_ln2_g, w_ff1=v_w_ff1,
             w_ff2=v_w_ff2, lnf_g=v_lnf_g)
    chip = 2 * lax.axis_index("x") + lax.axis_index("y")
    ids = jnp.stack([chip, lax.axis_index("c")]).astype(jnp.int32)

    out_grad, out_delta, out_m, out_v = {}, {}, {}, {}

    def update(n, gn):
        shape, two_d = w[n].shape, gn.shape
        d, nm, nv = _adamw("adamw_" + n, w[n].reshape(two_d), gn, m[n].reshape(two_d), v[n].reshape(two_d), 256)
        out_grad[n], out_delta[n], out_m[n], out_v[n] = (gn.reshape(shape), d.reshape(shape), nm.reshape(shape),
                                                         nv.reshape(shape))
        return d

    reducer = _Reducer(ids)
    p, late = _gather_weights(w, chip)
    loss, grad_x, g = _local_step(x, loss_target, p, late, reducer)
    small, sizes = _pack_small(g, loss + reducer.last_token[:1, :1])
    after = reducer.begin("small", {}, small)[0]
    for tag in ("ff", "proj", "in"):
        for n, red in zip(reducer.groups[tag]["names"], reducer.finish(tag, after)):
            after = update(n, red)
        if tag == "ff":
            after = reducer.advance("small", after)[0]
    (small_red,) = reducer.finish("small", after)
    gsmall, loss = _unpack_small(small_red, sizes, {n: g[n].shape for n in SMALL})
    two_d = {n: (int(np.prod(w[n].shape[:-1])), w[n].shape[-1]) for n in SMALL}
    for n in SHARDED_VECS:
        gsmall[n] = lax.dynamic_slice_in_dim(gsmall[n], chip * (D // N_CHIPS), D // N_CHIPS, axis=1)
    gs = [gsmall[n].reshape(two_d[n]) for n in SMALL]
    updates = _adamw_small([w[n].reshape(two_d[n]) for n in SMALL], gs, [m[n].reshape(two_d[n]) for n in SMALL],
                           [v[n].reshape(two_d[n]) for n in SMALL])
    for n, gn, (d, nm, nv) in zip(SMALL, gs, updates):
        shape = w[n].shape
        out_grad[n], out_delta[n], out_m[n], out_v[n] = (gn.reshape(shape), d.reshape(shape), nm.reshape(shape),
                                                         nv.reshape(shape))
    return (loss, grad_x, *[out_grad[n] for n in ORDER], *[out_delta[n] for n in ORDER],
            *[out_m[n] for n in ORDER], *[out_v[n] for n in ORDER])
```

```python
import functools

import numpy as np
import jax
import jax.numpy as jnp
from jax import lax
from jax.experimental import pallas as pl
from jax.experimental.pallas import tpu as pltpu

F32 = jnp.float32
BF16 = jnp.bfloat16

T = 2048
D = 1024
D_ATT = 512
D_IN = 5632
D_FF = 4096
N_HEADS = 8
HEAD_DIM = 64
GRID_W = 64
N_ROWS = T // GRID_W
WIN_H = 8
WIN_W = 16
KEYS = WIN_H * GRID_W
N_CHIPS = 4
EPS = 1e-6
LRU_C = 8.0
SCALE = HEAD_DIM ** -0.5
REC_CB = 256
REC_CHUNK = 256
PAD = 8

ADAM_LR = 0.001
ADAM_B1 = 0.9
ADAM_B2 = 0.999
ADAM_EPS = 1e-08
ADAM_WD = 0.01
ADAM_STEP = 10

VMEM_LIMIT = 56 * 1024 * 1024

NN = (((1,), (0,)), ((), ()))
NT = (((1,), (1,)), ((), ()))
TN = (((0,), (0,)), ((), ()))
MESH = pl.DeviceIdType.MESH


def _params(sem=None):
    return pltpu.CompilerParams(dimension_semantics=sem, vmem_limit_bytes=VMEM_LIMIT)


def _dot(a, b, dims):
    return lax.dot_general(a, b, dims, preferred_element_type=F32)


def _sigmoid(x):
    return 0.5 * jnp.tanh(0.5 * x) + 0.5


def _matmul(name, a, b, *, dims, grid, a_spec, b_spec, out_shapes, out_specs, acc_shape,
            extras=(), extra_specs=(), epilogue=None, colsum_spec=None, colsum_shape=None, after=(),
            semantics=("parallel", "parallel", "arbitrary"), epilogue_takes_first=False):
    nk = grid[2]
    n_extra = len(extras)
    n_out = len(out_shapes)
    with_colsum = colsum_spec is not None

    def body(a_ref, b_ref, *rest):
        ex = rest[:n_extra]
        rest = rest[:n_extra] + rest[n_extra + len(after):]
        outs = rest[n_extra:n_extra + n_out]
        pos = n_extra + n_out
        cs_out = rest[pos] if with_colsum else None
        pos += 1 if with_colsum else 0
        acc = rest[pos]
        cs_acc = rest[pos + 1] if with_colsum else None
        k = pl.program_id(2)
        first_tile = pl.program_id(0) == 0

        @pl.when(k == 0)
        def _():
            acc[...] = jnp.zeros_like(acc)
            if with_colsum:
                cs_acc[...] = jnp.zeros_like(cs_acc)

        bv = b_ref[...]
        acc[...] += _dot(a_ref[...].astype(BF16), bv.astype(BF16), dims)
        if with_colsum:
            cs_acc[...] += jnp.sum(bv.astype(F32), axis=0, keepdims=True)

        @pl.when(k == nk - 1)
        def _():
            r = acc[...]
            if epilogue is None:
                outs[0][...] = r.astype(outs[0].dtype)
            elif epilogue_takes_first:
                epilogue(r, ex, outs, first_tile)
            else:
                epilogue(r, ex, outs)
            if with_colsum:
                cs_out[...] = cs_acc[...]

    shapes = list(out_shapes)
    specs = list(out_specs)
    scratch = [pltpu.VMEM(acc_shape, F32)]
    if with_colsum:
        shapes.append(colsum_shape)
        specs.append(colsum_spec)
        scratch.append(pltpu.VMEM((1, acc_shape[1]), F32))
    res = pl.pallas_call(
        body, name=name, grid=grid,
        in_specs=[a_spec, b_spec, *extra_specs] + [_ANY] * len(after),
        out_specs=specs, out_shape=shapes, scratch_shapes=scratch,
        compiler_params=_params(semantics),
    )(a, b, *extras, *after)
    return res


def _sds(shape, dtype):
    return jax.ShapeDtypeStruct(shape, dtype)


TM = 1024
NI = T // TM
TJ = T
NJ = T // TJ


def _mm_nn_cols(name, a, wg, out_dtype, *, bias=None, extras=(), extra_specs=(), epilogue=None,
                out_shapes=None, out_specs=None):
    k_dim, n4 = wg.shape[1], wg.shape[2]
    ex, exs = list(extras), list(extra_specs)
    if bias is not None:
        ex = [bias] + ex
        exs = [pl.BlockSpec((1, n4), lambda j, i, k: (0, j))] + exs
        user_ep = epilogue

        def epilogue(r, e, outs):
            r = r + e[0][...]
            if user_ep is None:
                outs[0][...] = r.astype(outs[0].dtype)
            else:
                user_ep(r, e[1:], outs)
    if out_shapes is None:
        out_shapes = [_sds((T, N_CHIPS * n4), out_dtype)]
        out_specs = [pl.BlockSpec((TJ, n4), lambda j, i, k: (i, j))]
    return _matmul(
        name, a, wg, dims=NN, grid=(N_CHIPS, NJ, 1),
        a_spec=pl.BlockSpec((TJ, k_dim), lambda j, i, k: (i, 0)),
        b_spec=pl.BlockSpec((None, k_dim, n4), lambda j, i, k: (j, 0, 0)),
        out_shapes=out_shapes, out_specs=out_specs, acc_shape=(TJ, n4),
        extras=ex, extra_specs=exs, epilogue=epilogue)


def _mm_nt_cols_rms_bwd(name, a, wg, x, g, dres, after=(), bf16_copy=False):
    n4 = wg.shape[2]
    row = pl.BlockSpec((TM, D), lambda i, j, k: (i, 0))
    vec = pl.BlockSpec((1, D), lambda i, j, k: (0, 0))

    def epilogue(dhv, ex, outs, first):
        x_ref, g_ref, dres_ref = ex
        dx_ref, dg_ref = outs[0], outs[-1]
        xv = x_ref[...]
        rstd = lax.rsqrt(jnp.mean(xv * xv, axis=-1, keepdims=True) + EPS)
        xhat = xv * rstd
        dy = dhv * g_ref[...]
        dx = dres_ref[...] + rstd * (dy - xhat * jnp.mean(dy * xhat, axis=-1, keepdims=True))
        dx_ref[...] = dx
        if bf16_copy:
            outs[1][...] = dx.astype(BF16)
        part = jnp.sum(dhv * xhat, axis=0, keepdims=True)

        @pl.when(first)
        def _():
            dg_ref[...] = part

        @pl.when(jnp.logical_not(first))
        def _():
            dg_ref[...] += part

    return _matmul(
        name, a, wg, dims=NT, grid=(NI, 1, N_CHIPS),
        a_spec=pl.BlockSpec((TM, n4), lambda i, j, k: (i, k)),
        b_spec=pl.BlockSpec((None, D, n4), lambda i, j, k: (k, 0, 0)),
        out_shapes=[_sds((T, D), F32)] + [_sds((T, D), BF16)] * bf16_copy + [_sds((1, D), F32)],
        out_specs=[row] + [row] * bf16_copy + [vec], acc_shape=(TM, D),
        extras=[x, g, dres], extra_specs=[row, vec, row], epilogue=epilogue, after=after,
        semantics=("arbitrary", "arbitrary", "arbitrary"), epilogue_takes_first=True)


def _mm_nt_rows(name, a, w, out_dtype, *, tn, extras=(), extra_specs=(), epilogue=None):
    k_dim, n = w.shape
    return _matmul(
        name, a, w, dims=NT, grid=(k_dim // tn, NJ, 1),
        a_spec=pl.BlockSpec((TJ, n), lambda j, i, k: (i, 0)),
        b_spec=pl.BlockSpec((tn, n), lambda j, i, k: (j, 0)),
        out_shapes=[_sds((T, k_dim), out_dtype)],
        out_specs=[pl.BlockSpec((TJ, tn), lambda j, i, k: (i, j))], acc_shape=(TJ, tn),
        extras=extras, extra_specs=extra_specs, epilogue=epilogue)


def _mm_tn_cols(name, a, g, n4, *, colsum=False):
    k_dim = a.shape[1]
    kw = {}
    if colsum:
        kw = dict(colsum_spec=pl.BlockSpec((1, n4), lambda j, i, k: (0, j)),
                  colsum_shape=_sds((1, N_CHIPS * n4), F32))
    return _matmul(
        name, a, g, dims=TN, grid=(N_CHIPS, 1, NJ),
        a_spec=pl.BlockSpec((TJ, k_dim), lambda j, i, k: (k, 0)),
        b_spec=pl.BlockSpec((TJ, n4), lambda j, i, k: (k, j)),
        out_shapes=[_sds((N_CHIPS, k_dim, n4), F32)],
        out_specs=[pl.BlockSpec((None, k_dim, n4), lambda j, i, k: (j, 0, 0))],
        acc_shape=(k_dim, n4), **kw)


def _mm_tn_rows(name, a, g, *, tm):
    k_dim, n = a.shape[1], g.shape[1]
    return _matmul(
        name, a, g, dims=TN, grid=(k_dim // tm, 1, NJ),
        a_spec=pl.BlockSpec((TJ, tm), lambda j, i, k: (k, j)),
        b_spec=pl.BlockSpec((TJ, n), lambda j, i, k: (k, 0)),
        out_shapes=[_sds((k_dim, n), F32)],
        out_specs=[pl.BlockSpec((tm, n), lambda j, i, k: (j, 0))], acc_shape=(tm, n))


TE = 256
NE = T // TE
_ROW = pl.BlockSpec((TE, D), lambda i: (i, 0))
_VEC = pl.BlockSpec((1, D), lambda i: (0, 0))


def _rms_fwd(name, x, g, after=()):
    def body(x_ref, g_ref, *rest):
        h_ref = rest[-1]
        xv = x_ref[...]
        rstd = lax.rsqrt(jnp.mean(xv * xv, axis=-1, keepdims=True) + EPS)
        h_ref[...] = (xv * rstd * g_ref[...]).astype(BF16)

    return pl.pallas_call(body, name=name, grid=(NE,), in_specs=[_ROW, _VEC] + [_ANY] * len(after), out_specs=_ROW,
                          out_shape=_sds((T, D), BF16), compiler_params=_params(("parallel",)))(x, g, *after)


def _mm_x2_loss_head(s, w_ff2, x1, target, g):
    k_dim = w_ff2.shape[0]
    row = pl.BlockSpec((TM, D), lambda i, j, k: (i, 0))
    vec = pl.BlockSpec((1, D), lambda i, j, k: (0, 0))

    def epilogue(r, ex, outs, first):
        x1_ref, t_ref, g_ref = ex
        loss_ref, dx_ref, dxb_ref, dg_ref = outs
        xv = x1_ref[...] + r
        rstd = lax.rsqrt(jnp.mean(xv * xv, axis=-1, keepdims=True) + EPS)
        xhat = xv * rstd
        gv = g_ref[...]
        err = xhat * gv - t_ref[...]
        dy = err * (1.0 / D)
        dxh = dy * gv
        dx = rstd * (dxh - xhat * jnp.mean(dxh * xhat, axis=-1, keepdims=True))
        dx_ref[...] = dx
        dxb_ref[...] = dx.astype(BF16)
        dg_part = jnp.sum(dy * xhat, axis=0, keepdims=True)
        loss_part = (0.5 / D) * jnp.sum(jnp.sum(err * err, axis=1, keepdims=True), axis=0, keepdims=True)

        @pl.when(first)
        def _():
            dg_ref[...] = dg_part
            loss_ref[...] = loss_part

        @pl.when(jnp.logical_not(first))
        def _():
            dg_ref[...] += dg_part
            loss_ref[...] += loss_part

    return _matmul(
        "mm_x2_loss_head", s, w_ff2, dims=NN, grid=(NI, 1, k_dim // D),
        a_spec=pl.BlockSpec((TM, D), lambda i, j, k: (i, k)), b_spec=pl.BlockSpec((D, D), lambda i, j, k: (k, 0)),
        out_shapes=[_sds((1, 1), F32), _sds((T, D), F32), _sds((T, D), BF16), _sds((1, D), F32)],
        out_specs=[pl.BlockSpec((1, 1), lambda i, j, k: (0, 0)), row, row, vec], acc_shape=(TM, D),
        extras=[x1, target, g], extra_specs=[row, row, vec], epilogue=epilogue,
        semantics=("arbitrary", "arbitrary", "arbitrary"), epilogue_takes_first=True)


MW = 512
_G_ATT_BLK = 3584 // MW
_G_REC_BLK = 4608 // MW


TB = 512


def _branch_specs():
    def row(cols):
        return pl.BlockSpec((TB, cols), lambda i: (i, 0))

    ga = pl.BlockSpec((TB, MW), lambda i: (i, _G_ATT_BLK))
    ga2 = pl.BlockSpec((TB, MW), lambda i: (i, _G_ATT_BLK + 1))
    gr = pl.BlockSpec((TB, MW), lambda i: (i, _G_REC_BLK))
    gr2 = pl.BlockSpec((TB, MW), lambda i: (i, _G_REC_BLK + 1))
    w_att = pl.BlockSpec((N_CHIPS, D_ATT, D // N_CHIPS), lambda i: (0, 0, 0))
    w_sq = pl.BlockSpec((D, D), lambda i: (0, 0))
    return row, (ga, ga2, gr, gr2), w_att, w_sq


def _gate_values(gate_refs):
    ga, ga2, gr, gr2 = (r[...] for r in gate_refs)
    return _sigmoid(jnp.concatenate([ga, ga2], axis=1)), _sigmoid(jnp.concatenate([gr, gr2], axis=1))


def _branches_fwd(att, g, z, x, w_att_o, w_rec_o, w_out, ln2_g):
    row, gate_specs, w_att, w_sq = _branch_specs()

    def body(att_ref, g_ref, ga_ref, ga2_ref, gr_ref, gr2_ref, x_ref, wa_ref, wr_ref, wo_ref, g2_ref,
             ya_ref, yr_ref, m_ref, x1_ref, h2_ref):
        attv = att_ref[...]
        ya = jnp.concatenate([_dot(attv, wa_ref[j], NN) for j in range(N_CHIPS)], axis=1)
        yr = _dot(g_ref[...], wr_ref[...], NN)
        sa, sr = _gate_values((ga_ref, ga2_ref, gr_ref, gr2_ref))
        mixed = (sa * ya + sr * yr).astype(BF16)
        ya_ref[...] = ya
        yr_ref[...] = yr
        m_ref[...] = mixed
        x1 = x_ref[...] + _dot(mixed, wo_ref[...], NN)
        x1_ref[...] = x1
        rstd = lax.rsqrt(jnp.mean(x1 * x1, axis=-1, keepdims=True) + EPS)
        h2_ref[...] = (x1 * rstd * g2_ref[...]).astype(BF16)

    return pl.pallas_call(
        body, name="branches_fwd", grid=(T // TB,),
        in_specs=[row(D_ATT), row(D), *gate_specs, row(D), w_att, w_sq, w_sq, pl.BlockSpec((1, D), lambda i: (0, 0))],
        out_specs=[row(D)] * 5,
        out_shape=[_sds((T, D), F32), _sds((T, D), F32), _sds((T, D), BF16), _sds((T, D), F32), _sds((T, D), BF16)],
        compiler_params=_params(("parallel",)))(att, g, z, z, z, z, x, w_att_o, w_rec_o, w_out, ln2_g)


def _branches_bwd(dx1_b, y_att, y_rec, z, w_att_o, w_rec_o, w_out):
    row, gate_specs, w_att, w_sq = _branch_specs()
    n4 = D // N_CHIPS

    def body(dx_ref, ya_ref, yr_ref, ga_ref, ga2_ref, gr_ref, gr2_ref, wa_ref, wr_ref, wo_ref,
             dya_ref, dyr_ref, dga_ref, dgr_ref, datt_ref, dg_ref):
        dm = _dot(dx_ref[...], wo_ref[...], NT)
        sa, sr = _gate_values((ga_ref, ga2_ref, gr_ref, gr2_ref))
        dya = (dm * sa).astype(BF16)
        dyr = (dm * sr).astype(BF16)
        dya_ref[...] = dya
        dyr_ref[...] = dyr
        dga_ref[...] = (dm * ya_ref[...] * sa * (1.0 - sa)).astype(BF16)
        dgr_ref[...] = (dm * yr_ref[...] * sr * (1.0 - sr)).astype(BF16)
        datt = _dot(dya[:, 0:n4], wa_ref[0], NT)
        for j in range(1, N_CHIPS):
            datt = datt + _dot(dya[:, j * n4:(j + 1) * n4], wa_ref[j], NT)
        datt_ref[...] = datt.astype(BF16)
        dg_ref[...] = _dot(dyr, wr_ref[...], NT).astype(BF16)

    return pl.pallas_call(
        body, name="branches_bwd", grid=(T // TB,),
        in_specs=[row(D), row(D), row(D), *gate_specs, w_att, w_sq, w_sq],
        out_specs=[row(D)] * 4 + [row(D_ATT), row(D)],
        out_shape=[_sds((T, D), BF16)] * 4 + [_sds((T, D_ATT), BF16), _sds((T, D), BF16)],
        compiler_params=_params(("parallel",)))(dx1_b, y_att, y_rec, z, z, z, z, w_att_o, w_rec_o, w_out)


HP = 2 * HEAD_DIM
N_HP = N_HEADS // 2
ATT_UNROLL_FWD = 16
ATT_UNROLL_BWD = 8
DIAG_ROWS = 32


def _window_maps():
    diag = np.zeros((GRID_W * GRID_W, 128), np.float32)
    for qc in range(GRID_W):
        w0 = min(max(qc - WIN_W // 2, 0), GRID_W - WIN_W)
        for kc in range(w0, w0 + WIN_W):
            diag[qc * GRID_W + kc, kc - qc + WIN_W - 1] = 1.0
    return diag, diag.sum(axis=1)[None, :]


def _split3(x):
    a = x.astype(BF16)
    r = x - a.astype(F32)
    b = r.astype(BF16)
    c = (r - b.astype(F32)).astype(BF16)
    return a, b, c


N_DROW = 2 * WIN_H - 1
N_DPAIR = N_DROW - 1


def _bias_pairs(rpb):
    diag, valid = _window_maps()
    r2 = jnp.pad(rpb.reshape(N_HEADS * N_DROW, 2 * WIN_W - 1),
                 ((0, 128 - N_HEADS * N_DROW), (0, 128 - (2 * WIN_W - 1))))

    def body(r_ref, d_ref, v_ref, o_ref):
        dv = d_ref[...]
        t = sum(_dot(part, dv, NN) for part in _split3(r_ref[...]))
        o_ref[...] = jnp.where(v_ref[...] > 0.0, t, -1e30)

    t = pl.pallas_call(body, name="rpb_expand", out_shape=_sds((128, GRID_W * GRID_W), F32),
                       compiler_params=_params())(r2, jnp.asarray(diag.T, BF16), jnp.asarray(valid, F32))
    t = t[:N_HEADS * N_DROW].reshape(N_HEADS, N_DROW, GRID_W, GRID_W)
    return jnp.concatenate([t[:, :N_DPAIR], t[:, 1:]], axis=-1)


def _row_bias(tb_ref, hh, d0):
    return jnp.concatenate([tb_ref[hh, d0 + 2 * ii] for ii in range(WIN_H // 2)], axis=1)


def _row_window(r):
    rs = jnp.clip(r - WIN_H // 2, 0, N_ROWS - WIN_H)
    return pl.multiple_of(r * GRID_W, GRID_W), pl.multiple_of(rs * GRID_W, GRID_W), rs - r + (WIN_H - 1)


def _split_heads(src_ref, dst_ref, scale=None):
    for hh in range(2):
        v = src_ref[:, hh * HEAD_DIM:(hh + 1) * HEAD_DIM]
        dst_ref[hh] = (v if scale is None else v * scale).astype(BF16)


def _attn_items(qb_ref, kb_ref, vb_ref, tb_ref, first_row, n_rows):
    wins = [_row_window(first_row + u) for u in range(n_rows)]
    items = [(u, hh) for u in range(n_rows) for hh in range(2)]
    q = [qb_ref[hh, pl.ds(wins[u][0], GRID_W), :] for u, hh in items]
    k = [kb_ref[hh, pl.ds(wins[u][1], KEYS), :] for u, hh in items]
    v = [vb_ref[hh, pl.ds(wins[u][1], KEYS), :] for u, hh in items]
    s = [_dot(qi, ki, NT) + _row_bias(tb_ref, hh, wins[u][2]) for qi, ki, (u, hh) in zip(q, k, items)]
    m = [jnp.max(si, axis=-1, keepdims=True) for si in s]
    e = [jnp.exp(si - mi) for si, mi in zip(s, m)]
    inv = [1.0 / jnp.sum(ei, axis=-1, keepdims=True) for ei in e]
    p = [ei * li for ei, li in zip(e, inv)]
    return wins, items, q, k, v, p


def _attn_in_specs():
    q = pl.BlockSpec((T, HP), lambda p: (0, p))
    k = pl.BlockSpec((T, HP), lambda p: (0, N_HP + p))
    v = pl.BlockSpec((T, HP), lambda p: (0, 2 * N_HP + p))
    tb = pl.BlockSpec((2, N_DPAIR, GRID_W, HP), lambda p: (p, 0, 0, 0))
    return q, k, v, tb


_HEAD_SCRATCH = pltpu.VMEM((2, T, HEAD_DIM), BF16)


def _attn_fwd(z, tb):
    def body(q_ref, k_ref, v_ref, tb_ref, o_ref, qb_ref, kb_ref, vb_ref):
        _split_heads(q_ref, qb_ref, SCALE)
        _split_heads(k_ref, kb_ref)
        _split_heads(v_ref, vb_ref)

        def rows(it, carry):
            wins, items, _, _, v, p = _attn_items(qb_ref, kb_ref, vb_ref, tb_ref, it * ATT_UNROLL_FWD, ATT_UNROLL_FWD)
            o = [_dot(pi.astype(BF16), vi, NN) for pi, vi in zip(p, v)]
            for u, (q0, _, _) in enumerate(wins):
                o_ref[pl.ds(q0, GRID_W), :] = jnp.concatenate(o[2 * u:2 * u + 2], axis=1).astype(BF16)
            return carry

        lax.fori_loop(0, N_ROWS // ATT_UNROLL_FWD, rows, 0)

    blk = pl.BlockSpec((T, HP), lambda p: (0, p))
    return pl.pallas_call(
        body, name="attn_fwd", grid=(N_HP,), in_specs=list(_attn_in_specs()), out_specs=blk,
        out_shape=_sds((T, D_ATT), BF16), scratch_shapes=[_HEAD_SCRATCH] * 3,
        compiler_params=_params(("parallel",)))(z, z, z, tb)


def _attn_bwd(z, tb, d_att, after=()):
    def body(q_ref, k_ref, v_ref, tb_ref, do_ref, flip_ref, *rest):
        (dq_ref, dk_ref, dv_ref, diag_ref, qb_ref, kb_ref, vb_ref, dob_ref, dka_ref, dva_ref,
         ds_ref) = rest[len(after):]
        _split_heads(q_ref, qb_ref, SCALE)
        _split_heads(k_ref, kb_ref)
        _split_heads(v_ref, vb_ref)
        _split_heads(do_ref, dob_ref)
        dka_ref[...] = jnp.zeros_like(dka_ref)
        dva_ref[...] = jnp.zeros_like(dva_ref)
        ds_ref[...] = jnp.zeros_like(ds_ref)

        def rows(it, carry):
            wins, items, q, k, v, p = _attn_items(qb_ref, kb_ref, vb_ref, tb_ref, it * ATT_UNROLL_BWD, ATT_UNROLL_BWD)
            do = [dob_ref[hh, pl.ds(wins[u][0], GRID_W), :] for u, hh in items]
            dv = [_dot(pi.astype(BF16), di, TN) for pi, di in zip(p, do)]
            dp = [_dot(di, vi, NT) for di, vi in zip(do, v)]
            ds = [pi * (dpi - jnp.sum(dpi * pi, axis=-1, keepdims=True)) for pi, dpi in zip(p, dp)]
            dsb = [d.astype(BF16) for d in ds]
            dq = [_dot(d, ki, NN) * SCALE for d, ki in zip(dsb, k)]
            dk = [_dot(d, qi, TN) for d, qi in zip(dsb, q)]
            for d, (u, hh) in zip(ds, items):
                for ii in range(WIN_H // 2):
                    ds_ref[hh, wins[u][2] + 2 * ii] += d[:, ii * HP:(ii + 1) * HP]
            for u, (q0, k0, _) in enumerate(wins):
                dq_ref[pl.ds(q0, GRID_W), :] = jnp.concatenate(dq[2 * u:2 * u + 2], axis=1).astype(BF16)
                dka_ref[pl.ds(k0, KEYS), :] += jnp.concatenate(dk[2 * u:2 * u + 2], axis=1)
                dva_ref[pl.ds(k0, KEYS), :] += jnp.concatenate(dv[2 * u:2 * u + 2], axis=1)
            return carry

        lax.fori_loop(0, N_ROWS // ATT_UNROLL_BWD, rows, 0)
        dk_ref[...] = dka_ref[...].astype(BF16)
        dv_ref[...] = dva_ref[...].astype(BF16)
        _diag_sums(ds_ref, flip_ref, diag_ref)

    blk = pl.BlockSpec((T, HP), lambda p: (0, p))
    q, k, v, tbs = _attn_in_specs()
    flip = jnp.asarray(np.eye(HP, dtype=np.float32)[::-1], BF16)
    return pl.pallas_call(
        body, name="attn_bwd", grid=(N_HP,),
        in_specs=[q, k, v, tbs, blk, pl.BlockSpec((HP, HP), lambda p: (0, 0))] + [_ANY] * len(after),
        out_specs=[blk, blk, blk, pl.BlockSpec((None, DIAG_ROWS, HP), lambda p: (p, 0, 0))],
        out_shape=[_sds((T, D_ATT), BF16)] * 3 + [_sds((N_HP, DIAG_ROWS, HP), F32)],
        scratch_shapes=[_HEAD_SCRATCH] * 4 + [pltpu.VMEM((T, HP), F32), pltpu.VMEM((T, HP), F32),
                                              pltpu.VMEM((2, N_DPAIR, GRID_W, HP), F32)],
        compiler_params=_params(("parallel",)))(z, z, z, tb, d_att, flip, *after)


def _diag_sums(acc_ref, flip_ref, out_ref):
    flip = flip_ref[...]
    rows = []
    for hh in range(2):
        for pair in range(N_DPAIR):
            reversed_lanes = sum(_dot(part, flip, NN) for part in _split3(acc_ref[hh, pair]))
            skewed = pltpu.roll(reversed_lanes, 0, 1, stride=1, stride_axis=0)
            rows.append(jnp.sum(skewed, axis=0, keepdims=True))
    rows.append(jnp.zeros((DIAG_ROWS - len(rows), HP), F32))
    out_ref[...] = jnp.concatenate(rows, axis=0)


def _rpb_grad(diag_sums):
    g = diag_sums.reshape(N_HP * DIAG_ROWS, HP)
    sel = np.zeros((2, 128, N_HP * DIAG_ROWS), np.float32)
    lane = np.zeros((2, HP, 128), np.float32)
    for h in range(N_HEADS):
        for pair in range(N_DPAIR):
            for half in range(2):
                sel[half, h * N_DROW + pair + half, (h // 2) * DIAG_ROWS + (h % 2) * N_DPAIR + pair] = 1.0
    for j in range(2 * WIN_W - 1):
        for half in range(2):
            lane[half, (HP - 1 - GRID_W * half - (j - (WIN_W - 1))) % HP, j] = 1.0

    def body(g_ref, sel_ref, lane_ref, o_ref):
        parts = _split3(g_ref[...])
        total = None
        for half in range(2):
            picked = sum(_dot(sel_ref[half], part, NN) for part in parts)
            term = sum(_dot(part, lane_ref[half], NN) for part in _split3(picked))
            total = term if total is None else total + term
        o_ref[...] = total

    out = pl.pallas_call(body, name="rpb_grad", out_shape=_sds((128, 128), F32),
                         compiler_params=_params())(g, jnp.asarray(sel, BF16), jnp.asarray(lane, BF16))
    return out[:N_HEADS * N_DROW, :2 * WIN_W - 1].reshape(N_HEADS, N_DROW, 2 * WIN_W - 1)


N_CB = D // REC_CB
N_CHUNK = T // REC_CHUNK
N_TILE = T // 8
_U_BLK = 1536 // REC_CB
_Y_BLK = 2560 // REC_CB


def _block_diag(w):
    per = REC_CB // 64
    wt = w.reshape(2, N_CB, per, 64, 64)
    eye = jnp.eye(per, dtype=w.dtype)
    full = wt[:, :, :, :, None, :] * eye[None, None, :, None, :, None]
    return full.reshape(2, N_CB, REC_CB, REC_CB).astype(BF16)


def _block_diag_grad(g):
    per = REC_CB // 64
    g6 = g.reshape(2, N_CB, per, 64, per, 64)
    return jnp.stack([g6[:, :, p, :, p, :] for p in range(per)], axis=2).reshape(2, 16, 64, 64)


def _gelu(x):
    c = 0.7978845608028654
    return 0.5 * x * (1.0 + jnp.tanh(c * (x + 0.044715 * x * x * x)))


def _gelu_grad(x):
    c = 0.7978845608028654
    th = jnp.tanh(c * (x + 0.044715 * x * x * x))
    return 0.5 * (1.0 + th) + 0.5 * x * (1.0 - th * th) * c * (1.0 + 3.0 * 0.044715 * x * x)


def _softplus_neg(lam):
    x = -lam
    e = jnp.exp(-jnp.abs(x))
    w = 1.0 + e
    l1p = jnp.where(w == 1.0, e, jnp.log(w) * e / (w - 1.0))
    return jnp.maximum(x, 0.0) + l1p


def _one_minus_exp(x):
    poly = x * (1.0 + x * (1 / 2 + x * (1 / 6 + x * (1 / 24 + x * (1 / 120 + x * (1 / 720))))))
    return jnp.where(x > -0.125, -poly, 1.0 - jnp.exp(x))


def _conv_taps(pad_ref, t0, w, sign):
    out = None
    for j in range(4):
        term = w[j:j + 1, :] * pad_ref[pl.ds(PAD + t0 + sign * (j - 2), REC_CHUNK), :]
        out = term if out is None else out + term
    return out


def _gates(u, wa, wi, ba, bi, sp):
    ub = u.astype(BF16)
    r = _sigmoid(_dot(ub, wa, NN) + ba)
    i = _sigmoid(_dot(ub, wi, NN) + bi)
    log_a = -LRU_C * r * sp
    a = jnp.exp(log_a)
    x = jnp.maximum(_one_minus_exp(2.0 * log_a), 0.0)
    positive = x > 0.0
    inv = lax.rsqrt(jnp.where(positive, x, 1.0))
    mult = jnp.where(positive, x * inv, 0.0)
    return r, i, a, mult, jnp.where(positive, inv, 0.0)


def _tile_scan(a, b, sub, reverse):
    for s in (1, 2, 4):
        if reverse:
            a_s, b_s, m = pltpu.roll(a, 8 - s, 0), pltpu.roll(b, 8 - s, 0), sub < 8 - s
        else:
            a_s, b_s, m = pltpu.roll(a, s, 0), pltpu.roll(b, s, 0), sub >= s
        b = jnp.where(m, a * b_s + b, b)
        a = jnp.where(m, a * a_s, a)
    return a, b


def _last_row(x, row):
    return jnp.broadcast_to(x[row:row + 1, :], x.shape)


def _rec_prologue(up_ref, cw_ref, cb_ref, wa_ref, wi_ref, ba_ref, bi_ref, lam_ref,
                  upad_ref, u_ref, a_refs, h_refs):
    cb = up_ref.shape[1]
    zeros = jnp.zeros((PAD, cb), F32)
    upad_ref[pl.ds(0, PAD), :] = zeros
    upad_ref[pl.ds(PAD + T, PAD), :] = zeros
    upad_ref[pl.ds(PAD, T), :] = up_ref[...]
    cw = cw_ref[...]
    sp = _softplus_neg(lam_ref[...])
    for c in range(N_CHUNK):
        t0 = c * REC_CHUNK
        u = cb_ref[...] + _conv_taps(upad_ref, t0, cw, 1)
        u_ref[pl.ds(t0, REC_CHUNK), :] = u
        for d in range(2):
            _, i, a, mult, _ = _gates(u, wa_ref[d], wi_ref[d], ba_ref[d:d + 1, :], bi_ref[d:d + 1, :], sp[d:d + 1, :])
            a_refs[d][pl.ds(t0, REC_CHUNK), :] = a
            h_refs[d][pl.ds(t0, REC_CHUNK), :] = mult * (i * u)

    sub = lax.broadcasted_iota(jnp.int32, (8, cb), 0)

    def tile(k, carry):
        cf, cr = carry
        tf = pl.multiple_of(k * 8, 8)
        tr = pl.multiple_of((N_TILE - 1 - k) * 8, 8)
        af, bf = _tile_scan(a_refs[0][pl.ds(tf, 8), :], h_refs[0][pl.ds(tf, 8), :], sub, False)
        hf = af * cf + bf
        h_refs[0][pl.ds(tf, 8), :] = hf
        ar, br = _tile_scan(a_refs[1][pl.ds(tr, 8), :], h_refs[1][pl.ds(tr, 8), :], sub, True)
        hr = ar * cr + br
        h_refs[1][pl.ds(tr, 8), :] = hr
        return _last_row(af, 7) * cf + _last_row(bf, 7), _last_row(ar, 0) * cr + _last_row(br, 0)

    z8 = jnp.zeros((8, cb), F32)
    lax.fori_loop(0, N_TILE, tile, (z8, z8))
    return sp


def _rec_specs():
    up = pl.BlockSpec((T, REC_CB), lambda c: (0, _U_BLK + c))
    yb = pl.BlockSpec((T, REC_CB), lambda c: (0, _Y_BLK + c))
    cw = pl.BlockSpec((4, REC_CB), lambda c: (0, c))
    cbias = pl.BlockSpec((1, REC_CB), lambda c: (0, c))
    wbd = pl.BlockSpec((2, None, REC_CB, REC_CB), lambda c: (0, c, 0, 0))
    vec2 = pl.BlockSpec((2, REC_CB), lambda c: (0, c))
    col = pl.BlockSpec((T, REC_CB), lambda c: (0, c))
    return up, yb, cw, cbias, wbd, vec2, col


def _rec_fwd(z, conv_w, conv_b, wa, wi, ba, bi, lam):
    up, yb, cw, cbias, wbd, vec2, col = _rec_specs()

    def body(up_ref, yb_ref, cw_ref, cb_ref, wa_ref, wi_ref, ba_ref, bi_ref, lam_ref, g_ref,
             u_ref, af_ref, ar_ref, hf_ref, hr_ref, upad_ref):
        _rec_prologue(up_ref, cw_ref, cb_ref, wa_ref, wi_ref, ba_ref, bi_ref, lam_ref,
                      upad_ref, u_ref, (af_ref, ar_ref), (hf_ref, hr_ref))

        def chunk(c, carry):
            t0 = pl.multiple_of(c * REC_CHUNK, REC_CHUNK)
            rows = pl.ds(t0, REC_CHUNK)
            g_ref[rows, :] = ((hf_ref[rows, :] + hr_ref[rows, :]) * _gelu(yb_ref[rows, :])).astype(BF16)
            return carry

        lax.fori_loop(0, N_CHUNK, chunk, 0)

    res = pl.pallas_call(
        body, name="rec_fwd", grid=(N_CB,),
        in_specs=[up, yb, cw, cbias, wbd, wbd, vec2, vec2, vec2], out_specs=[col] * 6,
        out_shape=[_sds((T, D), BF16)] + [_sds((T, D), F32)] * 5,
        scratch_shapes=[pltpu.VMEM((T + 2 * PAD, REC_CB), F32)],
        compiler_params=_params(("parallel",)))(z, z, conv_w, conv_b, wa, wi, ba, bi, lam)
    return res[0], tuple(res[1:])


def _rec_bwd(z, dg, saved, conv_w, conv_b, wa, wi, ba, bi, lam, after=()):
    up, yb, cw, cbias, wbd, vec2, col = _rec_specs()

    def body(up_ref, yb_ref, dg_ref, u_ref, af_ref, ar_ref, hf_ref, hr_ref,
             cw_ref, cb_ref, wa_ref, wi_ref, ba_ref, bi_ref, lam_ref, *rest):
        (dup_ref, dyb_ref, dcw_ref, dcb_ref, dwa_ref, dwi_ref, dba_ref, dbi_ref, dlam_ref,
         upad_ref, dh_ref, gf_ref, gr_ref, daf_ref, dar_ref, dupad_ref) = rest[len(after):]
        g_refs, da_refs = (gf_ref, gr_ref), (daf_ref, dar_ref)
        cb = up_ref.shape[1]
        zeros = jnp.zeros((PAD, cb), F32)
        upad_ref[pl.ds(0, PAD), :] = zeros
        upad_ref[pl.ds(PAD + T, PAD), :] = zeros
        upad_ref[pl.ds(PAD, T), :] = up_ref[...]
        sp = _softplus_neg(lam_ref[...])

        def gate_chunk(c, carry):
            t0 = pl.multiple_of(c * REC_CHUNK, REC_CHUNK)
            rows = pl.ds(t0, REC_CHUNK)
            y = yb_ref[rows, :]
            dgv = dg_ref[rows, :].astype(F32)
            dh_ref[rows, :] = dgv * _gelu(y)
            dyb_ref[rows, :] = (dgv * (hf_ref[rows, :] + hr_ref[rows, :]) * _gelu_grad(y)).astype(BF16)
            return carry

        lax.fori_loop(0, N_CHUNK, gate_chunk, 0)

        sub = lax.broadcasted_iota(jnp.int32, (8, cb), 0)

        def tile(k, carry):
            cf, cr = carry
            kf = N_TILE - 1 - k
            tf = pl.multiple_of(kf * 8, 8)
            tnext = pl.multiple_of(jnp.minimum(kf + 1, N_TILE - 1) * 8, 8)
            tprev = pl.multiple_of(jnp.maximum(kf - 1, 0) * 8, 8)
            a_t = af_ref[pl.ds(tf, 8), :]
            a_n = jnp.where(kf < N_TILE - 1, af_ref[pl.ds(tnext, 8), :], 0.0)
            a_sh = jnp.where(sub == 7, pltpu.roll(a_n, 7, 0), pltpu.roll(a_t, 7, 0))
            ca, cbb = _tile_scan(a_sh, dh_ref[pl.ds(tf, 8), :], sub, True)
            gf = ca * cf + cbb
            h_t = hf_ref[pl.ds(tf, 8), :]
            h_p = jnp.where(kf > 0, hf_ref[pl.ds(tprev, 8), :], 0.0)
            h_sh = jnp.where(sub == 0, pltpu.roll(h_p, 1, 0), pltpu.roll(h_t, 1, 0))
            gf_ref[pl.ds(tf, 8), :] = gf
            daf_ref[pl.ds(tf, 8), :] = gf * h_sh
            tr = pl.multiple_of(k * 8, 8)
            rnext = pl.multiple_of(jnp.minimum(k + 1, N_TILE - 1) * 8, 8)
            rprev = pl.multiple_of(jnp.maximum(k - 1, 0) * 8, 8)
            b_t = ar_ref[pl.ds(tr, 8), :]
            b_p = jnp.where(k > 0, ar_ref[pl.ds(rprev, 8), :], 0.0)
            b_sh = jnp.where(sub == 0, pltpu.roll(b_p, 1, 0), pltpu.roll(b_t, 1, 0))
            ra, rb = _tile_scan(b_sh, dh_ref[pl.ds(tr, 8), :], sub, False)
            gr = ra * cr + rb
            hr_t = hr_ref[pl.ds(tr, 8), :]
            hr_n = jnp.where(k < N_TILE - 1, hr_ref[pl.ds(rnext, 8), :], 0.0)
            hr_sh = jnp.where(sub == 7, pltpu.roll(hr_n, 7, 0), pltpu.roll(hr_t, 7, 0))
            gr_ref[pl.ds(tr, 8), :] = gr
            dar_ref[pl.ds(tr, 8), :] = gr * hr_sh
            return _last_row(gf, 0), _last_row(gr, 7)

        z8 = jnp.zeros((8, cb), F32)
        lax.fori_loop(0, N_TILE, tile, (z8, z8))

        dupad_ref[pl.ds(0, PAD), :] = zeros
        dupad_ref[pl.ds(PAD + T, PAD), :] = zeros
        dwa_ref[...] = jnp.zeros_like(dwa_ref)
        dwi_ref[...] = jnp.zeros_like(dwi_ref)
        dba_ref[...] = jnp.zeros_like(dba_ref)
        dbi_ref[...] = jnp.zeros_like(dbi_ref)
        dlam_ref[...] = jnp.zeros_like(dlam_ref)

        def grad_chunk(c, carry):
            t0 = pl.multiple_of(c * REC_CHUNK, REC_CHUNK)
            rows = pl.ds(t0, REC_CHUNK)
            u = u_ref[rows, :]
            ub = u.astype(BF16)
            du = jnp.zeros((REC_CHUNK, cb), F32)
            for d in range(2):
                r, i, a, mult, inv_mult = _gates(u, wa_ref[d], wi_ref[d], ba_ref[d:d + 1, :], bi_ref[d:d + 1, :],
                                                 sp[d:d + 1, :])
                dbx = g_refs[d][rows, :]
                dmult = dbx * (i * u)
                diu = dbx * mult
                a2 = a * a
                dlog = da_refs[d][rows, :] * a - dmult * (a2 * inv_mult)
                dpa = (dlog * (-LRU_C) * sp[d:d + 1, :]) * r * (1.0 - r)
                dpi = (diu * u) * i * (1.0 - i)
                dpab, dpib = dpa.astype(BF16), dpi.astype(BF16)
                du = du + diu * i + _dot(dpab, wa_ref[d], NT) + _dot(dpib, wi_ref[d], NT)
                dwa_ref[d] += _dot(ub, dpab, TN)
                dwi_ref[d] += _dot(ub, dpib, TN)
                dba_ref[d:d + 1, :] += jnp.sum(dpa, axis=0, keepdims=True)
                dbi_ref[d:d + 1, :] += jnp.sum(dpi, axis=0, keepdims=True)
                dlam_ref[d:d + 1, :] += jnp.sum(dlog * r, axis=0, keepdims=True)
            dupad_ref[pl.ds(PAD + t0, REC_CHUNK), :] = du
            return carry

        lax.fori_loop(0, N_CHUNK, grad_chunk, 0)
        dlam_ref[...] = dlam_ref[...] * (LRU_C * _sigmoid(-lam_ref[...]))

        cw = cw_ref[...]
        dcb = jnp.zeros((1, cb), F32)
        dcw = [jnp.zeros((1, cb), F32) for _ in range(4)]
        for c in range(N_CHUNK):
            t0 = c * REC_CHUNK
            du = dupad_ref[pl.ds(PAD + t0, REC_CHUNK), :]
            dcb = dcb + jnp.sum(du, axis=0, keepdims=True)
            for j in range(4):
                dcw[j] = dcw[j] + jnp.sum(du * upad_ref[pl.ds(PAD + t0 + j - 2, REC_CHUNK), :], axis=0, keepdims=True)
            dup_ref[pl.ds(t0, REC_CHUNK), :] = _conv_taps(dupad_ref, t0, cw, -1).astype(BF16)
        dcb_ref[...] = dcb
        dcw_ref[...] = jnp.concatenate(dcw, axis=0)

    full = pltpu.VMEM((T, REC_CB), F32)
    padded = pltpu.VMEM((T + 2 * PAD, REC_CB), F32)
    return pl.pallas_call(
        body, name="rec_bwd", grid=(N_CB,),
        in_specs=[up, yb] + [col] * 6 + [cw, cbias, wbd, wbd, vec2, vec2, vec2] + [_ANY] * len(after),
        out_specs=[col, col, cw, cbias, wbd, wbd, vec2, vec2, vec2],
        out_shape=[_sds((T, D), BF16), _sds((T, D), BF16), _sds((4, D), F32), _sds((1, D), F32),
                   _sds((2, N_CB, REC_CB, REC_CB), F32), _sds((2, N_CB, REC_CB, REC_CB), F32),
                   _sds((2, D), F32), _sds((2, D), F32), _sds((2, D), F32)],
        scratch_shapes=[padded, full, full, full, full, full, padded],
        compiler_params=_params(("parallel",)))(z, z, dg, *saved, conv_w, conv_b, wa, wi, ba, bi, lam, *after)


class _NoReducer:
    def begin(self, tag, grads):
        return ()

    def advance(self, tag, after):
        return ()


def _local_step(x, target, p, late=None, reducer=_NoReducer()):
    x = x.reshape(T, D)
    target = target.reshape(T, D)
    tb = _bias_pairs(p["rpb"])
    wa, wi = _block_diag(p["w_rg_a"]), _block_diag(p["w_rg_i"])

    h1 = _rms_fwd("rms1_fwd", x, p["ln1_g"], after=late[0] if late else ())
    if late:
        p = {**p, **late[1]((h1, tb, wa, wi))}
    rec_params = (p["conv_w"], p["conv_b"], wa, wi, p["b_rg_a"], p["b_rg_i"], p["lru_lambda"])
    (z,) = _mm_nn_cols("mm_z", h1, p["w_in"], F32, bias=p["b_in"])
    att = _attn_fwd(z, tb)
    g, rec_saved = _rec_fwd(z, *rec_params)
    if late:
        p = {**p, **late[2](g)}
    y_att, y_rec, mixed, x1, h2 = _branches_fwd(att, g, z, x, p["w_att_o"], p["w_rec_o"], p["w_out"], p["ln2_g"])

    def relu2(r, ex, outs):
        rp = jnp.maximum(r, 0.0)
        outs[0][...] = (rp * rp).astype(BF16)

    (s,) = _mm_nn_cols("mm_ff1", h2, p["w_ff1"], BF16, epilogue=relu2)
    loss, dx2, dx2_b, g_lnf = _mm_x2_loss_head(s, p["w_ff2"], x1, target, p["lnf_g"])

    def relu2_bwd(r, ex, outs):
        outs[0][...] = (r * 2.0 * jnp.sqrt(ex[0][...].astype(F32))).astype(BF16)

    (df,) = _mm_nt_rows("mm_df", dx2_b, p["w_ff2"], BF16, tn=D, extras=[s],
                        extra_specs=[pl.BlockSpec((TJ, D), lambda j, i, k: (i, j))], epilogue=relu2_bwd)
    (g_w_ff2,) = _mm_tn_rows("mm_g_ff2", s, dx2_b, tm=D)
    (g_w_ff1,) = _mm_tn_cols("mm_g_ff1", h2, df, D)
    tok = reducer.begin("ff", dict(w_ff2=g_w_ff2, w_ff1=g_w_ff1))
    dx1, dx1_b, g_ln2 = _mm_nt_cols_rms_bwd("mm_dh2_rms2_bwd", df, p["w_ff1"], x1, p["ln2_g"], dx2, after=tok,
                                            bf16_copy=True)

    dy_att, dy_rec, dg_att, dg_rec, d_att, d_g = _branches_bwd(dx1_b, y_att, y_rec, z, p["w_att_o"], p["w_rec_o"],
                                                               p["w_out"])
    (g_w_out,) = _mm_tn_rows("mm_g_out", mixed, dx1_b, tm=D)
    (g_w_att_o,) = _mm_tn_cols("mm_g_att_o", att, dy_att, D // N_CHIPS)
    (g_w_rec_o,) = _mm_tn_rows("mm_g_rec_o", g, dy_rec, tm=D)
    tok = reducer.advance("ff", g_w_rec_o) + reducer.begin("proj", dict(w_out=g_w_out, w_att_o=g_w_att_o, w_rec_o=g_w_rec_o))

    dq, dk, dv, ds_acc = _attn_bwd(z, tb, d_att, after=tok)
    g_rpb = _rpb_grad(ds_acc)
    tok = reducer.advance("proj", dq)
    d_up, d_yb, g_conv_w, g_conv_b, g_wa, g_wi, g_ba, g_bi, g_lam = _rec_bwd(z, d_g, rec_saved, *rec_params, after=tok)
    dz = jnp.concatenate([dq, dk, dv, d_up, d_yb, dg_att, dg_rec], axis=1)

    g_w_in, g_b_in = _mm_tn_cols("mm_g_in", h1, dz, D_IN // N_CHIPS, colsum=True)
    tok = reducer.begin("in", dict(w_in=g_w_in))
    grad_x, g_ln1 = _mm_nt_cols_rms_bwd("mm_dh1_rms1_bwd", dz, p["w_in"], x, p["ln1_g"], dx1, after=tok)
    reducer.advance("in", grad_x)

    grads = dict(ln1_g=g_ln1, w_in=g_w_in, b_in=g_b_in, rpb=g_rpb, w_att_o=g_w_att_o, conv_w=g_conv_w,
                 conv_b=g_conv_b, w_rg_a=_block_diag_grad(g_wa), b_rg_a=g_ba, w_rg_i=_block_diag_grad(g_wi),
                 b_rg_i=g_bi, lru_lambda=g_lam, w_rec_o=g_w_rec_o, w_out=g_w_out, ln2_g=g_ln2,
                 w_ff1=g_w_ff1, w_ff2=g_w_ff2, lnf_g=g_lnf)
    return loss, grad_x.reshape(1, T, D), grads


_ANY = pl.BlockSpec(memory_space=pl.ANY)
N_PEERS = N_CHIPS - 1


def _place():
    x, y, c = lax.axis_index("x"), lax.axis_index("y"), lax.axis_index("c")
    peers = [(1 - x, y), (x, 1 - y), (1 - x, 1 - y)]
    return x, y, c, 2 * x + y, peers


def _remote(src, dst, send_sem, recv_sem, dev):
    return pltpu.make_async_remote_copy(src_ref=src, dst_ref=dst, send_sem=send_sem, recv_sem=recv_sem,
                                        device_id=dev, device_id_type=MESH)


def _prefetch_call(body, name, ids, grid, in_specs, out_specs, out_shape, args, semantics=None):
    spec = pltpu.PrefetchScalarGridSpec(num_scalar_prefetch=1, grid=grid, in_specs=in_specs, out_specs=out_specs)
    return pl.pallas_call(body, name=name, grid_spec=spec, out_shape=out_shape,
                          compiler_params=_params(semantics or ("parallel",) * len(grid)))(ids, *args)


def _cast_bf16(name, w, chip_id, after=()):
    rows, cols = w.shape
    rb = min(rows, 256)

    def body(ids_ref, w_ref, *rest):
        rest[-1][...] = w_ref[...].astype(BF16)

    return _prefetch_call(body, name, chip_id, (rows // rb,),
                          [pl.BlockSpec((rb, cols), lambda i, ids: (i, 0))] + [_ANY] * len(after),
                          pl.BlockSpec((None, rb, cols), lambda i, ids: (ids[0], i, 0)),
                          _sds((N_CHIPS, rows, cols), BF16), (w, *after))


def _dma_sems(*counts):
    return [pltpu.SemaphoreType.DMA((k,)) for k in counts]


_HBM = pl.BlockSpec(memory_space=pltpu.HBM)
_SEM = pl.BlockSpec(memory_space=pltpu.SEMAPHORE)
_SPLIT_COPY = pltpu.CompilerParams(has_side_effects=pltpu.SideEffectType.DATAFLOW_SIDE_EFFECTING)


def _hbm(arrays):
    return [pltpu.with_memory_space_constraint(a, pltpu.HBM) for a in arrays]


def _hbm_like(arrays):
    return [pltpu.HBM(a.shape, a.dtype) for a in arrays]


def _halves(buf, c):
    half = buf.shape[1] // 2
    return pl.ds(c * half, half), pl.ds((1 - c) * half, half)


def _gather_start(name, slots):
    n = len(slots)
    nk = n * N_PEERS

    def body(*refs):
        bufs = refs[n:2 * n]
        send_sems, recv_sems, token = refs[2 * n:]
        x, y, c, chip, peers = _place()
        for t in range(n):
            mine, _ = _halves(bufs[t], c)
            for r, (px, py) in enumerate(peers):
                k = t * N_PEERS + r
                own = bufs[t].at[chip, mine]
                _remote(own, own, send_sems.at[k], recv_sems.at[k], (px, py, c)).start()
        token[...] = jnp.zeros_like(token)

    res = pl.pallas_call(
        body, name=name, in_specs=[_HBM] * n, out_specs=[_HBM] * n + [_SEM, _SEM, pl.BlockSpec(memory_space=pltpu.VMEM)],
        out_shape=_hbm_like(slots) + [pltpu.SemaphoreType.DMA((nk,)), pltpu.SemaphoreType.DMA((nk,)),
                                      _sds((8, 128), F32)],
        input_output_aliases={t: t for t in range(n)}, compiler_params=_SPLIT_COPY)(*_hbm(slots))
    return res[:n], (res[n], res[n + 1]), res[n + 2]


def _gather_wait(name, bufs, sems, after):
    n = len(bufs)
    after = tuple(after) if isinstance(after, (tuple, list)) else (after,)

    def body(*refs):
        ins = refs[:n]
        send_sems, recv_sems = refs[n], refs[n + 1]
        x, y, c, chip, peers = _place()
        for t in range(n):
            mine, _ = _halves(ins[t], c)
            for r, (px, py) in enumerate(peers):
                k = t * N_PEERS + r
                cp = _remote(ins[t].at[chip, mine], ins[t].at[2 * px + py, mine], send_sems.at[k], recv_sems.at[k],
                             (px, py, c))
                cp.wait_send()
                cp.wait_recv()

    return pl.pallas_call(
        body, name=name, in_specs=[_HBM] * n + [_SEM, _SEM] + [_ANY] * len(after), out_specs=[_HBM] * n,
        out_shape=_hbm_like(bufs), input_output_aliases={t: t for t in range(n)},
        compiler_params=_SPLIT_COPY)(*bufs, *sems, *after)


def _gather_forward(name, bufs):
    n = len(bufs)
    nk = n * N_PEERS

    def body(*refs):
        outs = refs[n:2 * n]
        send_sems, recv_sems = refs[2 * n:]
        x, y, c, chip, peers = _place()
        sibling = (x, y, 1 - c)
        sends = []
        for t in range(n):
            mine, _ = _halves(outs[t], c)
            for r, (px, py) in enumerate(peers):
                k = t * N_PEERS + r
                landed = outs[t].at[2 * px + py, mine]
                sends.append(_remote(landed, landed, send_sems.at[k], recv_sems.at[k], sibling))
                sends[-1].start()
        for t in range(n):
            _, theirs = _halves(outs[t], c)
            for r, (px, py) in enumerate(peers):
                k = t * N_PEERS + r
                landed = outs[t].at[2 * px + py, theirs]
                _remote(landed, landed, send_sems.at[k], recv_sems.at[k], sibling).wait_recv()
        for cp in sends:
            cp.wait_send()

    return pl.pallas_call(
        body, name=name, in_specs=[_ANY] * n, out_specs=[_ANY] * n, out_shape=[_sds(b.shape, b.dtype) for b in bufs],
        input_output_aliases={t: t for t in range(n)}, scratch_shapes=_dma_sems(nk, nk))(*bufs)


def _pair_copies(n, srcs, lands, send_sems, recv_sems):
    x, y, c, _, _ = _place()
    sibling = (x, y, 1 - c)
    copies = []
    for t in range(n):
        half = srcs[t].shape[1] // 2
        for j in range(N_CHIPS):
            k = t * N_CHIPS + j
            copies.append(_remote(srcs[t].at[j, pl.ds((1 - c) * half, half)], lands[t].at[j],
                                  send_sems.at[k], recv_sems.at[k], sibling))
    for t in range(n, len(srcs)):
        k = n * N_CHIPS + t - n
        copies.append(_remote(srcs[t], lands[t], send_sems.at[k], recv_sems.at[k], sibling))
    return copies


def _pair_start(name, grads, wholes=()):
    n = len(grads)
    srcs = list(grads) + list(wholes)
    m = len(srcs)
    lands = [pltpu.HBM((N_CHIPS, g.shape[1] // 2, g.shape[2]), F32) for g in grads] + _hbm_like(wholes)
    ns = n * N_CHIPS + len(wholes)

    def body(*refs):
        src_refs, land_refs = refs[m:2 * m], refs[2 * m:3 * m]
        send_sems, recv_sems, token = refs[3 * m:]
        for cp in _pair_copies(n, src_refs, land_refs, send_sems, recv_sems):
            cp.start()
        token[...] = jnp.zeros_like(token)

    res = pl.pallas_call(
        body, name=name, in_specs=[_HBM] * m,
        out_specs=[_HBM] * (2 * m) + [_SEM, _SEM, pl.BlockSpec(memory_space=pltpu.VMEM)],
        out_shape=_hbm_like(srcs) + lands + [pltpu.SemaphoreType.DMA((ns,)), pltpu.SemaphoreType.DMA((ns,)),
                                             _sds((8, 128), F32)],
        input_output_aliases={t: t for t in range(m)}, compiler_params=_SPLIT_COPY)(*_hbm(srcs))
    return (res[:m], res[m:2 * m], (res[2 * m], res[2 * m + 1])), res[2 * m + 2]


def _pair_wait(name, flight, n, after):
    srcs, lands, sems = flight
    m = len(srcs)

    def body(*refs):
        for cp in _pair_copies(n, refs[:m], refs[m:2 * m], refs[2 * m], refs[2 * m + 1]):
            cp.wait_send()
            cp.wait_recv()

    res = pl.pallas_call(
        body, name=name, in_specs=[_HBM] * (2 * m) + [_SEM, _SEM, _ANY], out_specs=[_HBM] * (2 * m),
        out_shape=_hbm_like(srcs) + _hbm_like(lands), input_output_aliases={t: t for t in range(2 * m)},
        compiler_params=_SPLIT_COPY)(*srcs, *lands, *sems, after)
    return res[:m], res[m:]


def _chip_copies(srcs, lands, small_src, small_land, send_sems, recv_sems):
    x, y, c, chip, peers = _place()
    n = len(srcs)
    copies = []
    for r, (px, py) in enumerate(peers):
        for t in range(n):
            k = t * N_PEERS + r
            copies.append(_remote(srcs[t].at[2 * px + py], lands[t].at[r], send_sems.at[k], recv_sems.at[k], (px, py, c)))
        if small_src is not None:
            k = n * N_PEERS + r
            half_s = small_src.shape[0] // 2
            copies.append(_remote(small_src.at[pl.ds(c * half_s, half_s)], small_land.at[r],
                                  send_sems.at[k], recv_sems.at[k], (px, py, c)))
    return copies


def _chip_start(name, sums_bf16, small=None):
    n = len(sums_bf16)
    srcs = list(sums_bf16) + ([small] if small is not None else [])
    m = len(srcs)
    lands = [pltpu.HBM((N_PEERS,) + s.shape[1:], BF16) for s in sums_bf16]
    if small is not None:
        lands.append(pltpu.HBM((N_PEERS, small.shape[0] // 2, 128), F32))
    nk = m * N_PEERS

    def body(*refs):
        src_refs, land_refs = refs[m:2 * m], refs[2 * m:3 * m]
        send_sems, recv_sems, token = refs[3 * m:]
        small_src, small_land = (src_refs[n], land_refs[n]) if small is not None else (None, None)
        for cp in _chip_copies(src_refs[:n], land_refs[:n], small_src, small_land, send_sems, recv_sems):
            cp.start()
        token[...] = jnp.zeros_like(token)

    res = pl.pallas_call(
        body, name=name, in_specs=[_HBM] * m,
        out_specs=[_HBM] * (2 * m) + [_SEM, _SEM, pl.BlockSpec(memory_space=pltpu.VMEM)],
        out_shape=_hbm_like(srcs) + lands + [pltpu.SemaphoreType.DMA((nk,)), pltpu.SemaphoreType.DMA((nk,)),
                                             _sds((8, 128), F32)],
        input_output_aliases={t: t for t in range(m)}, compiler_params=_SPLIT_COPY)(*_hbm(srcs))
    return (res[:m], res[m:2 * m], (res[2 * m], res[2 * m + 1])), res[2 * m + 2]


def _chip_wait(name, flight, with_small, after):
    srcs, lands, sems = flight
    m = len(srcs)
    n = m - 1 if with_small else m

    def body(*refs):
        src_refs, land_refs = refs[:m], refs[m:2 * m]
        send_sems, recv_sems = refs[2 * m], refs[2 * m + 1]
        small_src, small_land = (src_refs[n], land_refs[n]) if with_small else (None, None)
        for cp in _chip_copies(src_refs[:n], land_refs[:n], small_src, small_land, send_sems, recv_sems):
            cp.wait_send()
            cp.wait_recv()

    res = pl.pallas_call(
        body, name=name, in_specs=[_HBM] * (2 * m) + [_SEM, _SEM, _ANY], out_specs=[_HBM] * (2 * m),
        out_shape=_hbm_like(srcs) + _hbm_like(lands), input_output_aliases={t: t for t in range(2 * m)},
        compiler_params=_SPLIT_COPY)(*srcs, *lands, *sems, after)
    return res[:m], res[m:]


def _half_swap(name, bufs):
    n = len(bufs)

    def body(*refs):
        outs = refs[n:2 * n]
        send_sem, recv_sem = refs[2 * n:]
        x, y, c, _, _ = _place()
        sibling = (x, y, 1 - c)
        copies = []
        for t in range(n):
            h = outs[t].shape[0] // 2
            mine = outs[t].at[pl.ds(c * h, h)]
            copies.append(_remote(mine, mine, send_sem.at[t], recv_sem.at[t], sibling))
            copies[-1].start()
        for t in range(n):
            h = outs[t].shape[0] // 2
            theirs = outs[t].at[pl.ds((1 - c) * h, h)]
            _remote(theirs, theirs, send_sem.at[t], recv_sem.at[t], sibling).wait_recv()
        for cp in copies:
            cp.wait_send()

    return pl.pallas_call(
        body, name=name, in_specs=[_ANY] * n, out_specs=[_ANY] * n,
        out_shape=[_sds(b.shape, b.dtype) for b in bufs], input_output_aliases={t: t for t in range(n)},
        scratch_shapes=_dma_sems(n, n))(*bufs)


def _pair_sum(name, grad, got, ids):
    _, rows, cols = got.shape
    rb = min(rows, 256)
    nb = rows // rb
    blk = pl.BlockSpec((None, rb, cols), lambda i, j, ids: (j, i, 0))
    mine = pl.BlockSpec((None, rb, cols), lambda i, j, ids: (j, ids[1] * nb + i, 0))
    own = pl.BlockSpec((rb, cols), lambda i, j, ids: (i, 0))

    def body(ids_ref, a_ref, b_ref, s_ref, sb_ref):
        s = a_ref[...] + b_ref[...]
        sb_ref[...] = s.astype(BF16)

        @pl.when(pl.program_id(1) == ids_ref[0])
        def _():
            s_ref[...] = s

    return _prefetch_call(body, name, ids, (nb, N_CHIPS), [mine, blk], [own, blk],
                          [_sds((rows, cols), F32), _sds(got.shape, BF16)], (grad, got),
                          semantics=("parallel", "arbitrary"))


def _chip_sum(name, own_sum, got, ids):
    rows, cols = own_sum.shape
    rb = min(rows, 256)
    nb = rows // rb
    own = pl.BlockSpec((rb, cols), lambda i, ids: (i, 0))
    blk3 = pl.BlockSpec((N_PEERS, rb, cols), lambda i, ids: (0, i, 0))
    out = pl.BlockSpec((rb, cols), lambda i, ids: (ids[1] * nb + i, 0))

    def body(ids_ref, a_ref, b_ref, o_ref):
        o_ref[...] = ((a_ref[...] + b_ref[0].astype(F32)) + b_ref[1].astype(F32)) + b_ref[2].astype(F32)

    return _prefetch_call(body, name, ids, (nb,), [own, blk3], out, _sds((2 * rows, cols), F32), (own_sum, got))


SMALL_RB = 280


def _small_pair_sum(own, got):
    blk = pl.BlockSpec((SMALL_RB, 128), lambda i: (i, 0))

    def body(a_ref, b_ref, o_ref):
        o_ref[...] = a_ref[...] + b_ref[...]

    return pl.pallas_call(body, name="small_pair_sum", grid=(own.shape[0] // SMALL_RB,), in_specs=[blk, blk],
                          out_specs=blk, out_shape=_sds(own.shape, F32),
                          compiler_params=_params(("parallel",)))(own, got)


def _small_chip_sum(pair, got, ids):
    nb = pair.shape[0] // 2 // SMALL_RB
    half = pl.BlockSpec((SMALL_RB, 128), lambda i, ids: (ids[1] * nb + i, 0))
    blk3 = pl.BlockSpec((N_PEERS, SMALL_RB, 128), lambda i, ids: (0, i, 0))

    def body(ids_ref, a_ref, b_ref, o_ref):
        o_ref[...] = (a_ref[...] + b_ref[1]) + (b_ref[0] + b_ref[2])

    return _prefetch_call(body, "small_chip_sum", ids, (nb,), [half, blk3], half, _sds(pair.shape, F32), (pair, got))


def _adamw_math(w, g, m, v):
    m = ADAM_B1 * m + (1.0 - ADAM_B1) * g
    v = ADAM_B2 * v + (1.0 - ADAM_B2) * (g * g)
    m_hat = m / (1.0 - ADAM_B1 ** ADAM_STEP)
    v_hat = v / (1.0 - ADAM_B2 ** ADAM_STEP)
    delta = -ADAM_LR * (m_hat / (jnp.sqrt(v_hat) + ADAM_EPS) + ADAM_WD * w)
    return delta, m, v


def _adamw(name, w, g, m, v, rb=None):
    rows, cols = w.shape
    rb = rows if rb is None else rb
    blk = pl.BlockSpec((rb, cols), lambda i: (i, 0))

    def body(w_ref, g_ref, m_ref, v_ref, d_ref, nm_ref, nv_ref):
        d, nm, nv = _adamw_math(w_ref[...], g_ref[...], m_ref[...], v_ref[...])
        d_ref[...] = d
        nm_ref[...] = nm
        nv_ref[...] = nv

    return pl.pallas_call(body, name=name, grid=(rows // rb,), in_specs=[blk] * 4, out_specs=[blk] * 3,
                          out_shape=[_sds(w.shape, F32)] * 3, compiler_params=_params(("parallel",)))(w, g, m, v)


def _adamw_small(ws, gs, ms, vs):
    n = len(ws)

    def body(*refs):
        for t in range(n):
            w_ref, g_ref, m_ref, v_ref = (refs[k * n + t] for k in range(4))
            d, nm, nv = _adamw_math(w_ref[...], g_ref[...], m_ref[...], v_ref[...])
            for k, val in enumerate((d, nm, nv)):
                refs[(4 + k) * n + t][...] = val

    res = pl.pallas_call(body, name="adamw_small", out_shape=[_sds(a.shape, F32) for a in ws] * 3,
                         compiler_params=_params())(*ws, *gs, *ms, *vs)
    return [(res[t], res[n + t], res[2 * n + t]) for t in range(n)]


BIG = ("w_in", "w_att_o", "w_rec_o", "w_out", "w_ff1", "w_ff2")
SHARDED_VECS = ("conv_w", "b_rg_a", "b_rg_i", "lru_lambda")
SMALL = ("ln1_g", "b_in", "rpb", "conv_w", "conv_b", "w_rg_a", "b_rg_a", "w_rg_i", "b_rg_i", "lru_lambda",
         "ln2_g", "lnf_g")
SMALL_ROWS = 2240
ORDER = ("ln1_g", "w_in", "b_in", "rpb", "w_att_o", "conv_w", "conv_b", "w_rg_a", "b_rg_a", "w_rg_i", "b_rg_i",
         "lru_lambda", "w_rec_o", "w_out", "ln2_g", "w_ff1", "w_ff2", "lnf_g")


def _pack_small(grads, loss):
    parts, sizes = [], {}
    for n in SMALL:
        flat = grads[n].reshape(-1)
        pad = (-flat.shape[0]) % 128
        sizes[n] = (flat.shape[0], flat.shape[0] + pad)
        parts.append(jnp.pad(flat, (0, pad)))
    total = sum(s[1] for s in sizes.values())
    parts.append(jnp.pad(loss.reshape(1), (0, SMALL_ROWS * 128 - total - 1)))
    return jnp.concatenate(parts).reshape(SMALL_ROWS, 128), sizes


def _unpack_small(buf, sizes, shapes):
    flat = buf.reshape(-1)
    out, pos = {}, 0
    for n in SMALL:
        size, padded = sizes[n]
        out[n] = flat[pos:pos + size].reshape(shapes[n])
        pos += padded
    return out, flat[pos]


def _gather_weights(w, chip):
    chip_id = chip.astype(jnp.int32).reshape(1)
    vec_rows = [w[n][0] for n in SHARDED_VECS]
    vec_shard = jnp.concatenate(vec_rows + [jnp.zeros((16 - 10, D // N_CHIPS), F32)], axis=0)
    vec_slots = lax.dynamic_update_slice(jnp.zeros((N_CHIPS, 16, D // N_CHIPS), F32), vec_shard[None], (chip, 0, 0))
    bufs_a, sems_a, token_a = _gather_start("gather_start_first", [_cast_bf16("cast_w_in", w["w_in"][0], chip_id), vec_slots])
    rest_names = BIG[1:]
    bufs_b, sems_b, token_b = _gather_start(
        "gather_start_rest", [_cast_bf16("cast_" + n, w[n][0], chip_id, after=(token_a,)) for n in rest_names])

    def first(after):
        w_in_full, vec_full = _gather_forward("gather_forward_first", _gather_wait("gather_wait_first", bufs_a, sems_a, after))
        vecs = vec_full.transpose(1, 0, 2).reshape(16, D)
        return dict(w_in=w_in_full, conv_w=vecs[0:4], b_rg_a=vecs[4:6], b_rg_i=vecs[6:8], lru_lambda=vecs[8:10])

    def rest(after):
        full = dict(zip(rest_names, _gather_forward("gather_forward_rest",
                                                    _gather_wait("gather_wait_rest", bufs_b, sems_b, after))))
        return dict(w_att_o=full["w_att_o"], w_ff1=full["w_ff1"], w_rec_o=full["w_rec_o"].reshape(D, D),
                    w_out=full["w_out"].reshape(D, D), w_ff2=full["w_ff2"].reshape(D_FF, D))

    p = dict(ln1_g=w["ln1_g"], b_in=w["b_in"], rpb=w["rpb"][0], conv_b=w["conv_b"], w_rg_a=w["w_rg_a"][0],
             w_rg_i=w["w_rg_i"][0], ln2_g=w["ln2_g"], lnf_g=w["lnf_g"].reshape(1, D))
    return p, ((token_b,), first, rest)


class _Reducer:
    def __init__(self, ids):
        self.ids = ids
        self.groups = {}

    def begin(self, tag, grads, small=None):
        names = list(grads)
        big = [grads[n].reshape(N_CHIPS, -1, grads[n].shape[-1]) for n in names]
        flight, token = _pair_start("pair_start_" + tag, big, [] if small is None else [small])
        self.groups[tag] = dict(names=names, pair=flight, small=small is not None)
        return (token,)

    def advance(self, tag, after):
        grp = self.groups[tag]
        n = len(grp["names"])
        mine, got = _pair_wait("pair_wait_" + tag, grp["pair"], n, after)
        sums = [_pair_sum("pair_sum_" + name, a, b, self.ids) for name, a, b in zip(grp["names"], mine, got)]
        small_sum = _small_pair_sum(mine[n], got[n]) if grp["small"] else None
        grp["chip"], token = _chip_start("chip_start_" + tag, [s[1] for s in sums], small_sum)
        grp["sums"] = [s[0] for s in sums]
        self.last_token = token
        return (token,)

    def finish(self, tag, after):
        grp = self.groups[tag]
        srcs, lands = _chip_wait("chip_wait_" + tag, grp["chip"], grp["small"], after)
        halves = [_chip_sum("chip_sum_" + name, s, b, self.ids) for name, s, b in zip(grp["names"], grp["sums"], lands)]
        if grp["small"]:
            halves.append(_small_chip_sum(srcs[-1], lands[-1], self.ids))
        return _half_swap("half_swap_" + tag, halves)


def kernel(x, ln1_g, w_in, b_in, rpb, w_att_o, conv_w, conv_b, w_rg_a, b_rg_a, w_rg_i, b_rg_i, lru_lambda, w_rec_o, w_out, ln2_g, w_ff1, w_ff2, lnf_g, loss_target, m_ln1_g, m_w_in, m_b_in, m_rpb, m_w_att_o, m_conv_w, m_conv_b, m_w_rg_a, m_b_rg_a, m_w_rg_i, m_b_rg_i, m_lru_lambda, m_w_rec_o, m_w_out, m_ln2_g, m_w_ff1, m_w_ff2, m_lnf_g, v_ln1_g, v_w_in, v_b_in, v_rpb, v_w_att_o, v_conv_w, v_conv_b, v_w_rg_a, v_b_rg_a, v_w_rg_i, v_b_rg_i, v_lru_lambda, v_w_rec_o, v_w_out, v_ln2_g, v_w_ff1, v_w_ff2, v_lnf_g):
    w = dict(ln1_g=ln1_g, w_in=w_in, b_in=b_in, rpb=rpb, w_att_o=w_att_o, conv_w=conv_w, conv_b=conv_b,
             w_rg_a=w_rg_a, b_rg_a=b_rg_a, w_rg_i=w_rg_i, b_rg_i=b_rg_i, lru_lambda=lru_lambda, w_rec_o=w_rec_o,
             w_out=w_out, ln2_g=ln2_g, w_ff1=w_ff1, w_ff2=w_ff2, lnf_g=lnf_g)
    m = dict(ln1_g=m_ln1_g, w_in=m_w_in, b_in=m_b_in, rpb=m_rpb, w_att_o=m_w_att_o, conv_w=m_conv_w,
             conv_b=m_conv_b, w_rg_a=m_w_rg_a, b_rg_a=m_b_rg_a, w_rg_i=m_w_rg_i, b_rg_i=m_b_rg_i,
             lru_lambda=m_lru_lambda, w_rec_o=m_w_rec_o, w_out=m_w_out, ln2_g=m_ln2_g, w_ff1=m_w_ff1,
             w_ff2=m_w_ff2, lnf_g=m_lnf_g)
    v = dict(ln1_g=v_ln1_g, w_in=v_w_in, b_in=v_b_in, rpb=v_rpb, w_att_o=v_w_att_o, conv_w=v_conv_w,
             conv_b=v_conv_b, w_rg_a=v_w_rg_a, b_rg_a=v_b_rg_a, w_rg_i=v_w_rg_i, b_rg_i=v_b_rg_i,
             lru_lambda=v_lru_lambda, w_rec_o=v_w_rec_o, w_out=v_w_out, ln2_g=v_ln2_g, w_ff1=v_w_ff1,
             w_ff2=v_w_ff2, lnf_g=v_lnf_g)
    chip = 2 * lax.axis_index("x") + lax.axis_index("y")
    ids = jnp.stack([chip, lax.axis_index("c")]).astype(jnp.int32)

    out_grad, out_delta, out_m, out_v = {}, {}, {}, {}

    def update(n, gn):
        shape, two_d = w[n].shape, gn.shape
        d, nm, nv = _adamw("adamw_" + n, w[n].reshape(two_d), gn, m[n].reshape(two_d), v[n].reshape(two_d), 256)
        out_grad[n], out_delta[n], out_m[n], out_v[n] = (gn.reshape(shape), d.reshape(shape), nm.reshape(shape),
                                                         nv.reshape(shape))
        return d

    reducer = _Reducer(ids)
    p, late = _gather_weights(w, chip)
    loss, grad_x, g = _local_step(x, loss_target, p, late, reducer)
    small, sizes = _pack_small(g, loss + reducer.last_token[:1, :1])
    after = reducer.begin("small", {}, small)[0]
    for tag in ("ff", "proj", "in"):
        for n, red in zip(reducer.groups[tag]["names"], reducer.finish(tag, after)):
            after = update(n, red)
        if tag == "ff":
            after = reducer.advance("small", after)[0]
    (small_red,) = reducer.finish("small", after)
    gsmall, loss = _unpack_small(small_red, sizes, {n: g[n].shape for n in SMALL})
    two_d = {n: (int(np.prod(w[n].shape[:-1])), w[n].shape[-1]) for n in SMALL}
    for n in SHARDED_VECS:
        gsmall[n] = lax.dynamic_slice_in_dim(gsmall[n], chip * (D // N_CHIPS), D // N_CHIPS, axis=1)
    gs = [gsmall[n].reshape(two_d[n]) for n in SMALL]
    updates = _adamw_small([w[n].reshape(two_d[n]) for n in SMALL], gs, [m[n].reshape(two_d[n]) for n in SMALL],
                           [v[n].reshape(two_d[n]) for n in SMALL])
    for n, gn, (d, nm, nv) in zip(SMALL, gs, updates):
        shape = w[n].shape
        out_grad[n], out_delta[n], out_m[n], out_v[n] = (gn.reshape(shape), d.reshape(shape), nm.reshape(shape),
                                                         nv.reshape(shape))
    return (loss, grad_x, *[out_grad[n] for n in ORDER], *[out_delta[n] for n in ORDER],
            *[out_m[n] for n in ORDER], *[out_v[n] for n in ORDER])
```

```python
import functools

import numpy as np
import jax
import jax.numpy as jnp
from jax import lax
from jax.experimental import pallas as pl
from jax.experimental.pallas import tpu as pltpu

F32 = jnp.float32
BF16 = jnp.bfloat16

T = 2048
D = 1024
D_ATT = 512
D_IN = 5632
D_FF = 4096
N_HEADS = 8
HEAD_DIM = 64
GRID_W = 64
N_ROWS = T // GRID_W
WIN_H = 8
WIN_W = 16
KEYS = WIN_H * GRID_W
N_CHIPS = 4
EPS = 1e-6
LRU_C = 8.0
SCALE = HEAD_DIM ** -0.5
REC_CB = 256
REC_CHUNK = 256
PAD = 8

ADAM_LR = 0.001
ADAM_B1 = 0.9
ADAM_B2 = 0.999
ADAM_EPS = 1e-08
ADAM_WD = 0.01
ADAM_STEP = 10

VMEM_LIMIT = 56 * 1024 * 1024

NN = (((1,), (0,)), ((), ()))
NT = (((1,), (1,)), ((), ()))
TN = (((0,), (0,)), ((), ()))
MESH = pl.DeviceIdType.MESH


def _params(sem=None):
    return pltpu.CompilerParams(dimension_semantics=sem, vmem_limit_bytes=VMEM_LIMIT)


def _dot(a, b, dims):
    return lax.dot_general(a, b, dims, preferred_element_type=F32)


def _sigmoid(x):
    return 0.5 * jnp.tanh(0.5 * x) + 0.5


def _matmul(name, a, b, *, dims, grid, a_spec, b_spec, out_shapes, out_specs, acc_shape,
            extras=(), extra_specs=(), epilogue=None, colsum_spec=None, colsum_shape=None, after=(),
            semantics=("parallel", "parallel", "arbitrary"), epilogue_takes_first=False):
    nk = grid[2]
    n_extra = len(extras)
    n_out = len(out_shapes)
    with_colsum = colsum_spec is not None

    def body(a_ref, b_ref, *rest):
        ex = rest[:n_extra]
        rest = rest[:n_extra] + rest[n_extra + len(after):]
        outs = rest[n_extra:n_extra + n_out]
        pos = n_extra + n_out
        cs_out = rest[pos] if with_colsum else None
        pos += 1 if with_colsum else 0
        acc = rest[pos]
        cs_acc = rest[pos + 1] if with_colsum else None
        k = pl.program_id(2)
        first_tile = pl.program_id(0) == 0

        @pl.when(k == 0)
        def _():
            acc[...] = jnp.zeros_like(acc)
            if with_colsum:
                cs_acc[...] = jnp.zeros_like(cs_acc)

        bv = b_ref[...]
        acc[...] += _dot(a_ref[...].astype(BF16), bv.astype(BF16), dims)
        if with_colsum:
            cs_acc[...] += jnp.sum(bv.astype(F32), axis=0, keepdims=True)

        @pl.when(k == nk - 1)
        def _():
            r = acc[...]
            if epilogue is None:
                outs[0][...] = r.astype(outs[0].dtype)
            elif epilogue_takes_first:
                epilogue(r, ex, outs, first_tile)
            else:
                epilogue(r, ex, outs)
            if with_colsum:
                cs_out[...] = cs_acc[...]

    shapes = list(out_shapes)
    specs = list(out_specs)
    scratch = [pltpu.VMEM(acc_shape, F32)]
    if with_colsum:
        shapes.append(colsum_shape)
        specs.append(colsum_spec)
        scratch.append(pltpu.VMEM((1, acc_shape[1]), F32))
    res = pl.pallas_call(
        body, name=name, grid=grid,
        in_specs=[a_spec, b_spec, *extra_specs] + [_ANY] * len(after),
        out_specs=specs, out_shape=shapes, scratch_shapes=scratch,
        compiler_params=_params(semantics),
    )(a, b, *extras, *after)
    return res


def _sds(shape, dtype):
    return jax.ShapeDtypeStruct(shape, dtype)


TM = 1024
NI = T // TM
TJ = T
NJ = T // TJ


def _mm_nn_cols(name, a, wg, out_dtype, *, bias=None, extras=(), extra_specs=(), epilogue=None,
                out_shapes=None, out_specs=None):
    k_dim, n4 = wg.shape[1], wg.shape[2]
    ex, exs = list(extras), list(extra_specs)
    if bias is not None:
        ex = [bias] + ex
        exs = [pl.BlockSpec((1, n4), lambda j, i, k: (0, j))] + exs
        user_ep = epilogue

        def epilogue(r, e, outs):
            r = r + e[0][...]
            if user_ep is None:
                outs[0][...] = r.astype(outs[0].dtype)
            else:
                user_ep(r, e[1:], outs)
    if out_shapes is None:
        out_shapes = [_sds((T, N_CHIPS * n4), out_dtype)]
        out_specs = [pl.BlockSpec((TJ, n4), lambda j, i, k: (i, j))]
    return _matmul(
        name, a, wg, dims=NN, grid=(N_CHIPS, NJ, 1),
        a_spec=pl.BlockSpec((TJ, k_dim), lambda j, i, k: (i, 0)),
        b_spec=pl.BlockSpec((None, k_dim, n4), lambda j, i, k: (j, 0, 0)),
        out_shapes=out_shapes, out_specs=out_specs, acc_shape=(TJ, n4),
        extras=ex, extra_specs=exs, epilogue=epilogue)


def _mm_nt_cols_rms_bwd(name, a, wg, x, g, dres, after=(), bf16_copy=False):
    n4 = wg.shape[2]
    row = pl.BlockSpec((TM, D), lambda i, j, k: (i, 0))
    vec = pl.BlockSpec((1, D), lambda i, j, k: (0, 0))

    def epilogue(dhv, ex, outs, first):
        x_ref, g_ref, dres_ref = ex
        dx_ref, dg_ref = outs[0], outs[-1]
        xv = x_ref[...]
        rstd = lax.rsqrt(jnp.mean(xv * xv, axis=-1, keepdims=True) + EPS)
        xhat = xv * rstd
        dy = dhv * g_ref[...]
        dx = dres_ref[...] + rstd * (dy - xhat * jnp.mean(dy * xhat, axis=-1, keepdims=True))
        dx_ref[...] = dx
        if bf16_copy:
            outs[1][...] = dx.astype(BF16)
        part = jnp.sum(dhv * xhat, axis=0, keepdims=True)

        @pl.when(first)
        def _():
            dg_ref[...] = part

        @pl.when(jnp.logical_not(first))
        def _():
            dg_ref[...] += part

    return _matmul(
        name, a, wg, dims=NT, grid=(NI, 1, N_CHIPS),
        a_spec=pl.BlockSpec((TM, n4), lambda i, j, k: (i, k)),
        b_spec=pl.BlockSpec((None, D, n4), lambda i, j, k: (k, 0, 0)),
        out_shapes=[_sds((T, D), F32)] + [_sds((T, D), BF16)] * bf16_copy + [_sds((1, D), F32)],
        out_specs=[row] + [row] * bf16_copy + [vec], acc_shape=(TM, D),
        extras=[x, g, dres], extra_specs=[row, vec, row], epilogue=epilogue, after=after,
        semantics=("arbitrary", "arbitrary", "arbitrary"), epilogue_takes_first=True)


def _mm_nt_rows(name, a, w, out_dtype, *, tn, extras=(), extra_specs=(), epilogue=None):
    k_dim, n = w.shape
    return _matmul(
        name, a, w, dims=NT, grid=(k_dim // tn, NJ, 1),
        a_spec=pl.BlockSpec((TJ, n), lambda j, i, k: (i, 0)),
        b_spec=pl.BlockSpec((tn, n), lambda j, i, k: (j, 0)),
        out_shapes=[_sds((T, k_dim), out_dtype)],
        out_specs=[pl.BlockSpec((TJ, tn), lambda j, i, k: (i, j))], acc_shape=(TJ, tn),
        extras=extras, extra_specs=extra_specs, epilogue=epilogue)


def _mm_tn_cols(name, a, g, n4, *, colsum=False):
    k_dim = a.shape[1]
    kw = {}
    if colsum:
        kw = dict(colsum_spec=pl.BlockSpec((1, n4), lambda j, i, k: (0, j)),
                  colsum_shape=_sds((1, N_CHIPS * n4), F32))
    return _matmul(
        name, a, g, dims=TN, grid=(N_CHIPS, 1, NJ),
        a_spec=pl.BlockSpec((TJ, k_dim), lambda j, i, k: (k, 0)),
        b_spec=pl.BlockSpec((TJ, n4), lambda j, i, k: (k, j)),
        out_shapes=[_sds((N_CHIPS, k_dim, n4), F32)],
        out_specs=[pl.BlockSpec((None, k_dim, n4), lambda j, i, k: (j, 0, 0))],
        acc_shape=(k_dim, n4), **kw)


def _mm_tn_rows(name, a, g, *, tm):
    k_dim, n = a.shape[1], g.shape[1]
    return _matmul(
        name, a, g, dims=TN, grid=(k_dim // tm, 1, NJ),
        a_spec=pl.BlockSpec((TJ, tm), lambda j, i, k: (k, j)),
        b_spec=pl.BlockSpec((TJ, n), lambda j, i, k: (k, 0)),
        out_shapes=[_sds((k_dim, n), F32)],
        out_specs=[pl.BlockSpec((tm, n), lambda j, i, k: (j, 0))], acc_shape=(tm, n))


TE = 256
NE = T // TE
_ROW = pl.BlockSpec((TE, D), lambda i: (i, 0))
_VEC = pl.BlockSpec((1, D), lambda i: (0, 0))


def _rms_fwd(name, x, g, after=()):
    def body(x_ref, g_ref, *rest):
        h_ref = rest[-1]
        xv = x_ref[...]
        rstd = lax.rsqrt(jnp.mean(xv * xv, axis=-1, keepdims=True) + EPS)
        h_ref[...] = (xv * rstd * g_ref[...]).astype(BF16)

    return pl.pallas_call(body, name=name, grid=(NE,), in_specs=[_ROW, _VEC] + [_ANY] * len(after), out_specs=_ROW,
                          out_shape=_sds((T, D), BF16), compiler_params=_params(("parallel",)))(x, g, *after)


def _mm_x2_loss_head(s, w_ff2, x1, target, g):
    k_dim = w_ff2.shape[0]
    row = pl.BlockSpec((TM, D), lambda i, j, k: (i, 0))
    vec = pl.BlockSpec((1, D), lambda i, j, k: (0, 0))

    def epilogue(r, ex, outs, first):
        x1_ref, t_ref, g_ref = ex
        loss_ref, dx_ref, dxb_ref, dg_ref = outs
        xv = x1_ref[...] + r
        rstd = lax.rsqrt(jnp.mean(xv * xv, axis=-1, keepdims=True) + EPS)
        xhat = xv * rstd
        gv = g_ref[...]
        err = xhat * gv - t_ref[...]
        dy = err * (1.0 / D)
        dxh = dy * gv
        dx = rstd * (dxh - xhat * jnp.mean(dxh * xhat, axis=-1, keepdims=True))
        dx_ref[...] = dx
        dxb_ref[...] = dx.astype(BF16)
        dg_part = jnp.sum(dy * xhat, axis=0, keepdims=True)
        loss_part = (0.5 / D) * jnp.sum(jnp.sum(err * err, axis=1, keepdims=True), axis=0, keepdims=True)

        @pl.when(first)
        def _():
            dg_ref[...] = dg_part
            loss_ref[...] = loss_part

        @pl.when(jnp.logical_not(first))
        def _():
            dg_ref[...] += dg_part
            loss_ref[...] += loss_part

    return _matmul(
        "mm_x2_loss_head", s, w_ff2, dims=NN, grid=(NI, 1, k_dim // D),
        a_spec=pl.BlockSpec((TM, D), lambda i, j, k: (i, k)), b_spec=pl.BlockSpec((D, D), lambda i, j, k: (k, 0)),
        out_shapes=[_sds((1, 1), F32), _sds((T, D), F32), _sds((T, D), BF16), _sds((1, D), F32)],
        out_specs=[pl.BlockSpec((1, 1), lambda i, j, k: (0, 0)), row, row, vec], acc_shape=(TM, D),
        extras=[x1, target, g], extra_specs=[row, row, vec], epilogue=epilogue,
        semantics=("arbitrary", "arbitrary", "arbitrary"), epilogue_takes_first=True)


MW = 512
_G_ATT_BLK = 3584 // MW
_G_REC_BLK = 4608 // MW


TB = 512


def _branch_specs():
    def row(cols):
        return pl.BlockSpec((TB, cols), lambda i: (i, 0))

    ga = pl.BlockSpec((TB, MW), lambda i: (i, _G_ATT_BLK))
    ga2 = pl.BlockSpec((TB, MW), lambda i: (i, _G_ATT_BLK + 1))
    gr = pl.BlockSpec((TB, MW), lambda i: (i, _G_REC_BLK))
    gr2 = pl.BlockSpec((TB, MW), lambda i: (i, _G_REC_BLK + 1))
    w_att = pl.BlockSpec((N_CHIPS, D_ATT, D // N_CHIPS), lambda i: (0, 0, 0))
    w_sq = pl.BlockSpec((D, D), lambda i: (0, 0))
    return row, (ga, ga2, gr, gr2), w_att, w_sq


def _gate_values(gate_refs):
    ga, ga2, gr, gr2 = (r[...] for r in gate_refs)
    return _sigmoid(jnp.concatenate([ga, ga2], axis=1)), _sigmoid(jnp.concatenate([gr, gr2], axis=1))


def _branches_fwd(att, g, z, x, w_att_o, w_rec_o, w_out, ln2_g):
    row, gate_specs, w_att, w_sq = _branch_specs()

    def body(att_ref, g_ref, ga_ref, ga2_ref, gr_ref, gr2_ref, x_ref, wa_ref, wr_ref, wo_ref, g2_ref,
             ya_ref, yr_ref, m_ref, x1_ref, h2_ref):
        attv = att_ref[...]
        ya = jnp.concatenate([_dot(attv, wa_ref[j], NN) for j in range(N_CHIPS)], axis=1)
        yr = _dot(g_ref[...], wr_ref[...], NN)
        sa, sr = _gate_values((ga_ref, ga2_ref, gr_ref, gr2_ref))
        mixed = (sa * ya + sr * yr).astype(BF16)
        ya_ref[...] = ya
        yr_ref[...] = yr
        m_ref[...] = mixed
        x1 = x_ref[...] + _dot(mixed, wo_ref[...], NN)
        x1_ref[...] = x1
        rstd = lax.rsqrt(jnp.mean(x1 * x1, axis=-1, keepdims=True) + EPS)
        h2_ref[...] = (x1 * rstd * g2_ref[...]).astype(BF16)

    return pl.pallas_call(
        body, name="branches_fwd", grid=(T // TB,),
        in_specs=[row(D_ATT), row(D), *gate_specs, row(D), w_att, w_sq, w_sq, pl.BlockSpec((1, D), lambda i: (0, 0))],
        out_specs=[row(D)] * 5,
        out_shape=[_sds((T, D), F32), _sds((T, D), F32), _sds((T, D), BF16), _sds((T, D), F32), _sds((T, D), BF16)],
        compiler_params=_params(("parallel",)))(att, g, z, z, z, z, x, w_att_o, w_rec_o, w_out, ln2_g)


def _branches_bwd(dx1_b, y_att, y_rec, z, w_att_o, w_rec_o, w_out):
    row, gate_specs, w_att, w_sq = _branch_specs()
    n4 = D // N_CHIPS

    def body(dx_ref, ya_ref, yr_ref, ga_ref, ga2_ref, gr_ref, gr2_ref, wa_ref, wr_ref, wo_ref,
             dya_ref, dyr_ref, dga_ref, dgr_ref, datt_ref, dg_ref):
        dm = _dot(dx_ref[...], wo_ref[...], NT)
        sa, sr = _gate_values((ga_ref, ga2_ref, gr_ref, gr2_ref))
        dya = (dm * sa).astype(BF16)
        dyr = (dm * sr).astype(BF16)
        dya_ref[...] = dya
        dyr_ref[...] = dyr
        dga_ref[...] = (dm * ya_ref[...] * sa * (1.0 - sa)).astype(BF16)
        dgr_ref[...] = (dm * yr_ref[...] * sr * (1.0 - sr)).astype(BF16)
        datt = _dot(dya[:, 0:n4], wa_ref[0], NT)
        for j in range(1, N_CHIPS):
            datt = datt + _dot(dya[:, j * n4:(j + 1) * n4], wa_ref[j], NT)
        datt_ref[...] = datt.astype(BF16)
        dg_ref[...] = _dot(dyr, wr_ref[...], NT).astype(BF16)

    return pl.pallas_call(
        body, name="branches_bwd", grid=(T // TB,),
        in_specs=[row(D), row(D), row(D), *gate_specs, w_att, w_sq, w_sq],
        out_specs=[row(D)] * 4 + [row(D_ATT), row(D)],
        out_shape=[_sds((T, D), BF16)] * 4 + [_sds((T, D_ATT), BF16), _sds((T, D), BF16)],
        compiler_params=_params(("parallel",)))(dx1_b, y_att, y_rec, z, z, z, z, w_att_o, w_rec_o, w_out)


HP = 2 * HEAD_DIM
N_HP = N_HEADS // 2
ATT_UNROLL_FWD = 16
ATT_UNROLL_BWD = 8
DIAG_ROWS = 32


def _window_maps():
    diag = np.zeros((GRID_W * GRID_W, 128), np.float32)
    for qc in range(GRID_W):
        w0 = min(max(qc - WIN_W // 2, 0), GRID_W - WIN_W)
        for kc in range(w0, w0 + WIN_W):
            diag[qc * GRID_W + kc, kc - qc + WIN_W - 1] = 1.0
    return diag, diag.sum(axis=1)[None, :]


def _split3(x):
    a = x.astype(BF16)
    r = x - a.astype(F32)
    b = r.astype(BF16)
    c = (r - b.astype(F32)).astype(BF16)
    return a, b, c


N_DROW = 2 * WIN_H - 1
N_DPAIR = N_DROW - 1


def _bias_pairs(rpb):
    diag, valid = _window_maps()
    r2 = jnp.pad(rpb.reshape(N_HEADS * N_DROW, 2 * WIN_W - 1),
                 ((0, 128 - N_HEADS * N_DROW), (0, 128 - (2 * WIN_W - 1))))

    def body(r_ref, d_ref, v_ref, o_ref):
        dv = d_ref[...]
        t = sum(_dot(part, dv, NN) for part in _split3(r_ref[...]))
        o_ref[...] = jnp.where(v_ref[...] > 0.0, t, -1e30)

    t = pl.pallas_call(body, name="rpb_expand", out_shape=_sds((128, GRID_W * GRID_W), F32),
                       compiler_params=_params())(r2, jnp.asarray(diag.T, BF16), jnp.asarray(valid, F32))
    t = t[:N_HEADS * N_DROW].reshape(N_HEADS, N_DROW, GRID_W, GRID_W)
    return jnp.concatenate([t[:, :N_DPAIR], t[:, 1:]], axis=-1)


def _row_bias(tb_ref, hh, d0):
    return jnp.concatenate([tb_ref[hh, d0 + 2 * ii] for ii in range(WIN_H // 2)], axis=1)


def _row_window(r):
    rs = jnp.clip(r - WIN_H // 2, 0, N_ROWS - WIN_H)
    return pl.multiple_of(r * GRID_W, GRID_W), pl.multiple_of(rs * GRID_W, GRID_W), rs - r + (WIN_H - 1)


def _split_heads(src_ref, dst_ref, scale=None):
    for hh in range(2):
        v = src_ref[:, hh * HEAD_DIM:(hh + 1) * HEAD_DIM]
        dst_ref[hh] = (v if scale is None else v * scale).astype(BF16)


def _attn_items(qb_ref, kb_ref, vb_ref, tb_ref, first_row, n_rows):
    wins = [_row_window(first_row + u) for u in range(n_rows)]
    items = [(u, hh) for u in range(n_rows) for hh in range(2)]
    q = [qb_ref[hh, pl.ds(wins[u][0], GRID_W), :] for u, hh in items]
    k = [kb_ref[hh, pl.ds(wins[u][1], KEYS), :] for u, hh in items]
    v = [vb_ref[hh, pl.ds(wins[u][1], KEYS), :] for u, hh in items]
    s = [_dot(qi, ki, NT) + _row_bias(tb_ref, hh, wins[u][2]) for qi, ki, (u, hh) in zip(q, k, items)]
    m = [jnp.max(si, axis=-1, keepdims=True) for si in s]
    e = [jnp.exp(si - mi) for si, mi in zip(s, m)]
    inv = [1.0 / jnp.sum(ei, axis=-1, keepdims=True) for ei in e]
    p = [ei * li for ei, li in zip(e, inv)]
    return wins, items, q, k, v, p


def _attn_in_specs():
    q = pl.BlockSpec((T, HP), lambda p: (0, p))
    k = pl.BlockSpec((T, HP), lambda p: (0, N_HP + p))
    v = pl.BlockSpec((T, HP), lambda p: (0, 2 * N_HP + p))
    tb = pl.BlockSpec((2, N_DPAIR, GRID_W, HP), lambda p: (p, 0, 0, 0))
    return q, k, v, tb


_HEAD_SCRATCH = pltpu.VMEM((2, T, HEAD_DIM), BF16)


def _attn_fwd(z, tb):
    def body(q_ref, k_ref, v_ref, tb_ref, o_ref, qb_ref, kb_ref, vb_ref):
        _split_heads(q_ref, qb_ref, SCALE)
        _split_heads(k_ref, kb_ref)
        _split_heads(v_ref, vb_ref)

        def rows(it, carry):
            wins, items, _, _, v, p = _attn_items(qb_ref, kb_ref, vb_ref, tb_ref, it * ATT_UNROLL_FWD, ATT_UNROLL_FWD)
            o = [_dot(pi.astype(BF16), vi, NN) for pi, vi in zip(p, v)]
            for u, (q0, _, _) in enumerate(wins):
                o_ref[pl.ds(q0, GRID_W), :] = jnp.concatenate(o[2 * u:2 * u + 2], axis=1).astype(BF16)
            return carry

        lax.fori_loop(0, N_ROWS // ATT_UNROLL_FWD, rows, 0)

    blk = pl.BlockSpec((T, HP), lambda p: (0, p))
    return pl.pallas_call(
        body, name="attn_fwd", grid=(N_HP,), in_specs=list(_attn_in_specs()), out_specs=blk,
        out_shape=_sds((T, D_ATT), BF16), scratch_shapes=[_HEAD_SCRATCH] * 3,
        compiler_params=_params(("parallel",)))(z, z, z, tb)


def _attn_bwd(z, tb, d_att, after=()):
    def body(q_ref, k_ref, v_ref, tb_ref, do_ref, flip_ref, *rest):
        (dq_ref, dk_ref, dv_ref, diag_ref, qb_ref, kb_ref, vb_ref, dob_ref, dka_ref, dva_ref,
         ds_ref) = rest[len(after):]
        _split_heads(q_ref, qb_ref, SCALE)
        _split_heads(k_ref, kb_ref)
        _split_heads(v_ref, vb_ref)
        _split_heads(do_ref, dob_ref)
        dka_ref[...] = jnp.zeros_like(dka_ref)
        dva_ref[...] = jnp.zeros_like(dva_ref)
        ds_ref[...] = jnp.zeros_like(ds_ref)

        def rows(it, carry):
            wins, items, q, k, v, p = _attn_items(qb_ref, kb_ref, vb_ref, tb_ref, it * ATT_UNROLL_BWD, ATT_UNROLL_BWD)
            do = [dob_ref[hh, pl.ds(wins[u][0], GRID_W), :] for u, hh in items]
            dv = [_dot(pi.astype(BF16), di, TN) for pi, di in zip(p, do)]
            dp = [_dot(di, vi, NT) for di, vi in zip(do, v)]
            ds = [pi * (dpi - jnp.sum(dpi * pi, axis=-1, keepdims=True)) for pi, dpi in zip(p, dp)]
            dsb = [d.astype(BF16) for d in ds]
            dq = [_dot(d, ki, NN) * SCALE for d, ki in zip(dsb, k)]
            dk = [_dot(d, qi, TN) for d, qi in zip(dsb, q)]
            for d, (u, hh) in zip(ds, items):
                for ii in range(WIN_H // 2):
                    ds_ref[hh, wins[u][2] + 2 * ii] += d[:, ii * HP:(ii + 1) * HP]
            for u, (q0, k0, _) in enumerate(wins):
                dq_ref[pl.ds(q0, GRID_W), :] = jnp.concatenate(dq[2 * u:2 * u + 2], axis=1).astype(BF16)
                dka_ref[pl.ds(k0, KEYS), :] += jnp.concatenate(dk[2 * u:2 * u + 2], axis=1)
                dva_ref[pl.ds(k0, KEYS), :] += jnp.concatenate(dv[2 * u:2 * u + 2], axis=1)
            return carry

        lax.fori_loop(0, N_ROWS // ATT_UNROLL_BWD, rows, 0)
        dk_ref[...] = dka_ref[...].astype(BF16)
        dv_ref[...] = dva_ref[...].astype(BF16)
        _diag_sums(ds_ref, flip_ref, diag_ref)

    blk = pl.BlockSpec((T, HP), lambda p: (0, p))
    q, k, v, tbs = _attn_in_specs()
    flip = jnp.asarray(np.eye(HP, dtype=np.float32)[::-1], BF16)
    return pl.pallas_call(
        body, name="attn_bwd", grid=(N_HP,),
        in_specs=[q, k, v, tbs, blk, pl.BlockSpec((HP, HP), lambda p: (0, 0))] + [_ANY] * len(after),
        out_specs=[blk, blk, blk, pl.BlockSpec((None, DIAG_ROWS, HP), lambda p: (p, 0, 0))],
        out_shape=[_sds((T, D_ATT), BF16)] * 3 + [_sds((N_HP, DIAG_ROWS, HP), F32)],
        scratch_shapes=[_HEAD_SCRATCH] * 4 + [pltpu.VMEM((T, HP), F32), pltpu.VMEM((T, HP), F32),
                                              pltpu.VMEM((2, N_DPAIR, GRID_W, HP), F32)],
        compiler_params=_params(("parallel",)))(z, z, z, tb, d_att, flip, *after)


def _diag_sums(acc_ref, flip_ref, out_ref):
    flip = flip_ref[...]
    rows = []
    for hh in range(2):
        for pair in range(N_DPAIR):
            reversed_lanes = sum(_dot(part, flip, NN) for part in _split3(acc_ref[hh, pair]))
            skewed = pltpu.roll(reversed_lanes, 0, 1, stride=1, stride_axis=0)
            rows.append(jnp.sum(skewed, axis=0, keepdims=True))
    rows.append(jnp.zeros((DIAG_ROWS - len(rows), HP), F32))
    out_ref[...] = jnp.concatenate(rows, axis=0)


def _rpb_grad(diag_sums):
    g = diag_sums.reshape(N_HP * DIAG_ROWS, HP)
    sel = np.zeros((2, 128, N_HP * DIAG_ROWS), np.float32)
    lane = np.zeros((2, HP, 128), np.float32)
    for h in range(N_HEADS):
        for pair in range(N_DPAIR):
            for half in range(2):
                sel[half, h * N_DROW + pair + half, (h // 2) * DIAG_ROWS + (h % 2) * N_DPAIR + pair] = 1.0
    for j in range(2 * WIN_W - 1):
        for half in range(2):
            lane[half, (HP - 1 - GRID_W * half - (j - (WIN_W - 1))) % HP, j] = 1.0

    def body(g_ref, sel_ref, lane_ref, o_ref):
        parts = _split3(g_ref[...])
        total = None
        for half in range(2):
            picked = sum(_dot(sel_ref[half], part, NN) for part in parts)
            term = sum(_dot(part, lane_ref[half], NN) for part in _split3(picked))
            total = term if total is None else total + term
        o_ref[...] = total

    out = pl.pallas_call(body, name="rpb_grad", out_shape=_sds((128, 128), F32),
                         compiler_params=_params())(g, jnp.asarray(sel, BF16), jnp.asarray(lane, BF16))
    return out[:N_HEADS * N_DROW, :2 * WIN_W - 1].reshape(N_HEADS, N_DROW, 2 * WIN_W - 1)


N_CB = D // REC_CB
N_CHUNK = T // REC_CHUNK
N_TILE = T // 8
_U_BLK = 1536 // REC_CB
_Y_BLK = 2560 // REC_CB


def _block_diag(w):
    per = REC_CB // 64
    wt = w.reshape(2, N_CB, per, 64, 64)
    eye = jnp.eye(per, dtype=w.dtype)
    full = wt[:, :, :, :, None, :] * eye[None, None, :, None, :, None]
    return full.reshape(2, N_CB, REC_CB, REC_CB).astype(BF16)


def _block_diag_grad(g):
    per = REC_CB // 64
    g6 = g.reshape(2, N_CB, per, 64, per, 64)
    return jnp.stack([g6[:, :, p, :, p, :] for p in range(per)], axis=2).reshape(2, 16, 64, 64)


def _gelu(x):
    c = 0.7978845608028654
    return 0.5 * x * (1.0 + jnp.tanh(c * (x + 0.044715 * x * x * x)))


def _gelu_grad(x):
    c = 0.7978845608028654
    th = jnp.tanh(c * (x + 0.044715 * x * x * x))
    return 0.5 * (1.0 + th) + 0.5 * x * (1.0 - th * th) * c * (1.0 + 3.0 * 0.044715 * x * x)


def _softplus_neg(lam):
    x = -lam
    e = jnp.exp(-jnp.abs(x))
    w = 1.0 + e
    l1p = jnp.where(w == 1.0, e, jnp.log(w) * e / (w - 1.0))
    return jnp.maximum(x, 0.0) + l1p


def _one_minus_exp(x):
    poly = x * (1.0 + x * (1 / 2 + x * (1 / 6 + x * (1 / 24 + x * (1 / 120 + x * (1 / 720))))))
    return jnp.where(x > -0.125, -poly, 1.0 - jnp.exp(x))


def _conv_taps(pad_ref, t0, w, sign):
    out = None
    for j in range(4):
        term = w[j:j + 1, :] * pad_ref[pl.ds(PAD + t0 + sign * (j - 2), REC_CHUNK), :]
        out = term if out is None else out + term
    return out


def _gates(u, wa, wi, ba, bi, sp):
    ub = u.astype(BF16)
    r = _sigmoid(_dot(ub, wa, NN) + ba)
    i = _sigmoid(_dot(ub, wi, NN) + bi)
    log_a = -LRU_C * r * sp
    a = jnp.exp(log_a)
    x = jnp.maximum(_one_minus_exp(2.0 * log_a), 0.0)
    positive = x > 0.0
    inv = lax.rsqrt(jnp.where(positive, x, 1.0))
    mult = jnp.where(positive, x * inv, 0.0)
    return r, i, a, mult, jnp.where(positive, inv, 0.0)


def _tile_scan(a, b, sub, reverse):
    for s in (1, 2, 4):
        if reverse:
            a_s, b_s, m = pltpu.roll(a, 8 - s, 0), pltpu.roll(b, 8 - s, 0), sub < 8 - s
        else:
            a_s, b_s, m = pltpu.roll(a, s, 0), pltpu.roll(b, s, 0), sub >= s
        b = jnp.where(m, a * b_s + b, b)
        a = jnp.where(m, a * a_s, a)
    return a, b


def _last_row(x, row):
    return jnp.broadcast_to(x[row:row + 1, :], x.shape)


def _rec_prologue(up_ref, cw_ref, cb_ref, wa_ref, wi_ref, ba_ref, bi_ref, lam_ref,
                  upad_ref, u_ref, a_refs, h_refs):
    cb = up_ref.shape[1]
    zeros = jnp.zeros((PAD, cb), F32)
    upad_ref[pl.ds(0, PAD), :] = zeros
    upad_ref[pl.ds(PAD + T, PAD), :] = zeros
    upad_ref[pl.ds(PAD, T), :] = up_ref[...]
    cw = cw_ref[...]
    sp = _softplus_neg(lam_ref[...])
    for c in range(N_CHUNK):
        t0 = c * REC_CHUNK
        u = cb_ref[...] + _conv_taps(upad_ref, t0, cw, 1)
        u_ref[pl.ds(t0, REC_CHUNK), :] = u
        for d in range(2):
            _, i, a, mult, _ = _gates(u, wa_ref[d], wi_ref[d], ba_ref[d:d + 1, :], bi_ref[d:d + 1, :], sp[d:d + 1, :])
            a_refs[d][pl.ds(t0, REC_CHUNK), :] = a
            h_refs[d][pl.ds(t0, REC_CHUNK), :] = mult * (i * u)

    sub = lax.broadcasted_iota(jnp.int32, (8, cb), 0)

    def tile(k, carry):
        cf, cr = carry
        tf = pl.multiple_of(k * 8, 8)
        tr = pl.multiple_of((N_TILE - 1 - k) * 8, 8)
        af, bf = _tile_scan(a_refs[0][pl.ds(tf, 8), :], h_refs[0][pl.ds(tf, 8), :], sub, False)
        hf = af * cf + bf
        h_refs[0][pl.ds(tf, 8), :] = hf
        ar, br = _tile_scan(a_refs[1][pl.ds(tr, 8), :], h_refs[1][pl.ds(tr, 8), :], sub, True)
        hr = ar * cr + br
        h_refs[1][pl.ds(tr, 8), :] = hr
        return _last_row(af, 7) * cf + _last_row(bf, 7), _last_row(ar, 0) * cr + _last_row(br, 0)

    z8 = jnp.zeros((8, cb), F32)
    lax.fori_loop(0, N_TILE, tile, (z8, z8))
    return sp


def _rec_specs():
    up = pl.BlockSpec((T, REC_CB), lambda c: (0, _U_BLK + c))
    yb = pl.BlockSpec((T, REC_CB), lambda c: (0, _Y_BLK + c))
    cw = pl.BlockSpec((4, REC_CB), lambda c: (0, c))
    cbias = pl.BlockSpec((1, REC_CB), lambda c: (0, c))
    wbd = pl.BlockSpec((2, None, REC_CB, REC_CB), lambda c: (0, c, 0, 0))
    vec2 = pl.BlockSpec((2, REC_CB), lambda c: (0, c))
    col = pl.BlockSpec((T, REC_CB), lambda c: (0, c))
    return up, yb, cw, cbias, wbd, vec2, col


def _rec_fwd(z, conv_w, conv_b, wa, wi, ba, bi, lam):
    up, yb, cw, cbias, wbd, vec2, col = _rec_specs()

    def body(up_ref, yb_ref, cw_ref, cb_ref, wa_ref, wi_ref, ba_ref, bi_ref, lam_ref, g_ref,
             u_ref, af_ref, ar_ref, hf_ref, hr_ref, upad_ref):
        _rec_prologue(up_ref, cw_ref, cb_ref, wa_ref, wi_ref, ba_ref, bi_ref, lam_ref,
                      upad_ref, u_ref, (af_ref, ar_ref), (hf_ref, hr_ref))

        def chunk(c, carry):
            t0 = pl.multiple_of(c * REC_CHUNK, REC_CHUNK)
            rows = pl.ds(t0, REC_CHUNK)
            g_ref[rows, :] = ((hf_ref[rows, :] + hr_ref[rows, :]) * _gelu(yb_ref[rows, :])).astype(BF16)
            return carry

        lax.fori_loop(0, N_CHUNK, chunk, 0)

    res = pl.pallas_call(
        body, name="rec_fwd", grid=(N_CB,),
        in_specs=[up, yb, cw, cbias, wbd, wbd, vec2, vec2, vec2], out_specs=[col] * 6,
        out_shape=[_sds((T, D), BF16)] + [_sds((T, D), F32)] * 5,
        scratch_shapes=[pltpu.VMEM((T + 2 * PAD, REC_CB), F32)],
        compiler_params=_params(("parallel",)))(z, z, conv_w, conv_b, wa, wi, ba, bi, lam)
    return res[0], tuple(res[1:])


def _rec_bwd(z, dg, saved, conv_w, conv_b, wa, wi, ba, bi, lam, after=()):
    up, yb, cw, cbias, wbd, vec2, col = _rec_specs()

    def body(up_ref, yb_ref, dg_ref, u_ref, af_ref, ar_ref, hf_ref, hr_ref,
             cw_ref, cb_ref, wa_ref, wi_ref, ba_ref, bi_ref, lam_ref, *rest):
        (dup_ref, dyb_ref, dcw_ref, dcb_ref, dwa_ref, dwi_ref, dba_ref, dbi_ref, dlam_ref,
         upad_ref, dh_ref, gf_ref, gr_ref, daf_ref, dar_ref, dupad_ref) = rest[len(after):]
        g_refs, da_refs = (gf_ref, gr_ref), (daf_ref, dar_ref)
        cb = up_ref.shape[1]
        zeros = jnp.zeros((PAD, cb), F32)
        upad_ref[pl.ds(0, PAD), :] = zeros
        upad_ref[pl.ds(PAD + T, PAD), :] = zeros
        upad_ref[pl.ds(PAD, T), :] = up_ref[...]
        sp = _softplus_neg(lam_ref[...])

        def gate_chunk(c, carry):
            t0 = pl.multiple_of(c * REC_CHUNK, REC_CHUNK)
            rows = pl.ds(t0, REC_CHUNK)
            y = yb_ref[rows, :]
            dgv = dg_ref[rows, :].astype(F32)
            dh_ref[rows, :] = dgv * _gelu(y)
            dyb_ref[rows, :] = (dgv * (hf_ref[rows, :] + hr_ref[rows, :]) * _gelu_grad(y)).astype(BF16)
            return carry

        lax.fori_loop(0, N_CHUNK, gate_chunk, 0)

        sub = lax.broadcasted_iota(jnp.int32, (8, cb), 0)

        def tile(k, carry):
            cf, cr = carry
            kf = N_TILE - 1 - k
            tf = pl.multiple_of(kf * 8, 8)
            tnext = pl.multiple_of(jnp.minimum(kf + 1, N_TILE - 1) * 8, 8)
            tprev = pl.multiple_of(jnp.maximum(kf - 1, 0) * 8, 8)
            a_t = af_ref[pl.ds(tf, 8), :]
            a_n = jnp.where(kf < N_TILE - 1, af_ref[pl.ds(tnext, 8), :], 0.0)
            a_sh = jnp.where(sub == 7, pltpu.roll(a_n, 7, 0), pltpu.roll(a_t, 7, 0))
            ca, cbb = _tile_scan(a_sh, dh_ref[pl.ds(tf, 8), :], sub, True)
            gf = ca * cf + cbb
            h_t = hf_ref[pl.ds(tf, 8), :]
            h_p = jnp.where(kf > 0, hf_ref[pl.ds(tprev, 8), :], 0.0)
            h_sh = jnp.where(sub == 0, pltpu.roll(h_p, 1, 0), pltpu.roll(h_t, 1, 0))
            gf_ref[pl.ds(tf, 8), :] = gf
            daf_ref[pl.ds(tf, 8), :] = gf * h_sh
            tr = pl.multiple_of(k * 8, 8)
            rnext = pl.multiple_of(jnp.minimum(k + 1, N_TILE - 1) * 8, 8)
            rprev = pl.multiple_of(jnp.maximum(k - 1, 0) * 8, 8)
            b_t = ar_ref[pl.ds(tr, 8), :]
            b_p = jnp.where(k > 0, ar_ref[pl.ds(rprev, 8), :], 0.0)
            b_sh = jnp.where(sub == 0, pltpu.roll(b_p, 1, 0), pltpu.roll(b_t, 1, 0))
            ra, rb = _tile_scan(b_sh, dh_ref[pl.ds(tr, 8), :], sub, False)
            gr = ra * cr + rb
            hr_t = hr_ref[pl.ds(tr, 8), :]
            hr_n = jnp.where(k < N_TILE - 1, hr_ref[pl.ds(rnext, 8), :], 0.0)
            hr_sh = jnp.where(sub == 7, pltpu.roll(hr_n, 7, 0), pltpu.roll(hr_t, 7, 0))
            gr_ref[pl.ds(tr, 8), :] = gr
            dar_ref[pl.ds(tr, 8), :] = gr * hr_sh
            return _last_row(gf, 0), _last_row(gr, 7)

        z8 = jnp.zeros((8, cb), F32)
        lax.fori_loop(0, N_TILE, tile, (z8, z8))

        dupad_ref[pl.ds(0, PAD), :] = zeros
        dupad_ref[pl.ds(PAD + T, PAD), :] = zeros
        dwa_ref[...] = jnp.zeros_like(dwa_ref)
        dwi_ref[...] = jnp.zeros_like(dwi_ref)
        dba_ref[...] = jnp.zeros_like(dba_ref)
        dbi_ref[...] = jnp.zeros_like(dbi_ref)
        dlam_ref[...] = jnp.zeros_like(dlam_ref)

        def grad_chunk(c, carry):
            t0 = pl.multiple_of(c * REC_CHUNK, REC_CHUNK)
            rows = pl.ds(t0, REC_CHUNK)
            u = u_ref[rows, :]
            ub = u.astype(BF16)
            du = jnp.zeros((REC_CHUNK, cb), F32)
            for d in range(2):
                r, i, a, mult, inv_mult = _gates(u, wa_ref[d], wi_ref[d], ba_ref[d:d + 1, :], bi_ref[d:d + 1, :],
                                                 sp[d:d + 1, :])
                dbx = g_refs[d][rows, :]
                dmult = dbx * (i * u)
                diu = dbx * mult
                a2 = a * a
                dlog = da_refs[d][rows, :] * a - dmult * (a2 * inv_mult)
                dpa = (dlog * (-LRU_C) * sp[d:d + 1, :]) * r * (1.0 - r)
                dpi = (diu * u) * i * (1.0 - i)
                dpab, dpib = dpa.astype(BF16), dpi.astype(BF16)
                du = du + diu * i + _dot(dpab, wa_ref[d], NT) + _dot(dpib, wi_ref[d], NT)
                dwa_ref[d] += _dot(ub, dpab, TN)
                dwi_ref[d] += _dot(ub, dpib, TN)
                dba_ref[d:d + 1, :] += jnp.sum(dpa, axis=0, keepdims=True)
                dbi_ref[d:d + 1, :] += jnp.sum(dpi, axis=0, keepdims=True)
                dlam_ref[d:d + 1, :] += jnp.sum(dlog * r, axis=0, keepdims=True)
            dupad_ref[pl.ds(PAD + t0, REC_CHUNK), :] = du
            return carry

        lax.fori_loop(0, N_CHUNK, grad_chunk, 0)
        dlam_ref[...] = dlam_ref[...] * (LRU_C * _sigmoid(-lam_ref[...]))

        cw = cw_ref[...]
        dcb = jnp.zeros((1, cb), F32)
        dcw = [jnp.zeros((1, cb), F32) for _ in range(4)]
        for c in range(N_CHUNK):
            t0 = c * REC_CHUNK
            du = dupad_ref[pl.ds(PAD + t0, REC_CHUNK), :]
            dcb = dcb + jnp.sum(du, axis=0, keepdims=True)
            for j in range(4):
                dcw[j] = dcw[j] + jnp.sum(du * upad_ref[pl.ds(PAD + t0 + j - 2, REC_CHUNK), :], axis=0, keepdims=True)
            dup_ref[pl.ds(t0, REC_CHUNK), :] = _conv_taps(dupad_ref, t0, cw, -1).astype(BF16)
        dcb_ref[...] = dcb
        dcw_ref[...] = jnp.concatenate(dcw, axis=0)

    full = pltpu.VMEM((T, REC_CB), F32)
    padded = pltpu.VMEM((T + 2 * PAD, REC_CB), F32)
    return pl.pallas_call(
        body, name="rec_bwd", grid=(N_CB,),
        in_specs=[up, yb] + [col] * 6 + [cw, cbias, wbd, wbd, vec2, vec2, vec2] + [_ANY] * len(after),
        out_specs=[col, col, cw, cbias, wbd, wbd, vec2, vec2, vec2],
        out_shape=[_sds((T, D), BF16), _sds((T, D), BF16), _sds((4, D), F32), _sds((1, D), F32),
                   _sds((2, N_CB, REC_CB, REC_CB), F32), _sds((2, N_CB, REC_CB, REC_CB), F32),
                   _sds((2, D), F32), _sds((2, D), F32), _sds((2, D), F32)],
        scratch_shapes=[padded, full, full, full, full, full, padded],
        compiler_params=_params(("parallel",)))(z, z, dg, *saved, conv_w, conv_b, wa, wi, ba, bi, lam, *after)


class _NoReducer:
    def begin(self, tag, grads):
        return ()

    def advance(self, tag, after):
        return ()


def _local_step(x, target, p, late=None, reducer=_NoReducer()):
    x = x.reshape(T, D)
    target = target.reshape(T, D)
    tb = _bias_pairs(p["rpb"])
    wa, wi = _block_diag(p["w_rg_a"]), _block_diag(p["w_rg_i"])

    h1 = _rms_fwd("rms1_fwd", x, p["ln1_g"], after=late[0] if late else ())
    if late:
        p = {**p, **late[1]((h1, tb, wa, wi))}
    rec_params = (p["conv_w"], p["conv_b"], wa, wi, p["b_rg_a"], p["b_rg_i"], p["lru_lambda"])
    (z,) = _mm_nn_cols("mm_z", h1, p["w_in"], F32, bias=p["b_in"])
    att = _attn_fwd(z, tb)
    g, rec_saved = _rec_fwd(z, *rec_params)
    if late:
        p = {**p, **late[2](g)}
    y_att, y_rec, mixed, x1, h2 = _branches_fwd(att, g, z, x, p["w_att_o"], p["w_rec_o"], p["w_out"], p["ln2_g"])

    def relu2(r, ex, outs):
        rp = jnp.maximum(r, 0.0)
        outs[0][...] = (rp * rp).astype(BF16)

    (s,) = _mm_nn_cols("mm_ff1", h2, p["w_ff1"], BF16, epilogue=relu2)
    loss, dx2, dx2_b, g_lnf = _mm_x2_loss_head(s, p["w_ff2"], x1, target, p["lnf_g"])

    def relu2_bwd(r, ex, outs):
        outs[0][...] = (r * 2.0 * jnp.sqrt(ex[0][...].astype(F32))).astype(BF16)

    (df,) = _mm_nt_rows("mm_df", dx2_b, p["w_ff2"], BF16, tn=D, extras=[s],
                        extra_specs=[pl.BlockSpec((TJ, D), lambda j, i, k: (i, j))], epilogue=relu2_bwd)
    (g_w_ff2,) = _mm_tn_rows("mm_g_ff2", s, dx2_b, tm=D)
    (g_w_ff1,) = _mm_tn_cols("mm_g_ff1", h2, df, D)
    tok = reducer.begin("ff", dict(w_ff2=g_w_ff2, w_ff1=g_w_ff1))
    dx1, dx1_b, g_ln2 = _mm_nt_cols_rms_bwd("mm_dh2_rms2_bwd", df, p["w_ff1"], x1, p["ln2_g"], dx2, after=tok,
                                            bf16_copy=True)

    dy_att, dy_rec, dg_att, dg_rec, d_att, d_g = _branches_bwd(dx1_b, y_att, y_rec, z, p["w_att_o"], p["w_rec_o"],
                                                               p["w_out"])
    (g_w_out,) = _mm_tn_rows("mm_g_out", mixed, dx1_b, tm=D)
    (g_w_att_o,) = _mm_tn_cols("mm_g_att_o", att, dy_att, D // N_CHIPS)
    (g_w_rec_o,) = _mm_tn_rows("mm_g_rec_o", g, dy_rec, tm=D)
    tok = reducer.advance("ff", g_w_rec_o) + reducer.begin("proj", dict(w_out=g_w_out, w_att_o=g_w_att_o, w_rec_o=g_w_rec_o))

    dq, dk, dv, ds_acc = _attn_bwd(z, tb, d_att, after=tok)
    g_rpb = _rpb_grad(ds_acc)
    tok = reducer.advance("proj", dq)
    d_up, d_yb, g_conv_w, g_conv_b, g_wa, g_wi, g_ba, g_bi, g_lam = _rec_bwd(z, d_g, rec_saved, *rec_params, after=tok)
    dz = jnp.concatenate([dq, dk, dv, d_up, d_yb, dg_att, dg_rec], axis=1)

    g_w_in, g_b_in = _mm_tn_cols("mm_g_in", h1, dz, D_IN // N_CHIPS, colsum=True)
    tok = reducer.begin("in", dict(w_in=g_w_in))
    grad_x, g_ln1 = _mm_nt_cols_rms_bwd("mm_dh1_rms1_bwd", dz, p["w_in"], x, p["ln1_g"], dx1, after=tok)
    reducer.advance("in", grad_x)

    grads = dict(ln1_g=g_ln1, w_in=g_w_in, b_in=g_b_in, rpb=g_rpb, w_att_o=g_w_att_o, conv_w=g_conv_w,
                 conv_b=g_conv_b, w_rg_a=_block_diag_grad(g_wa), b_rg_a=g_ba, w_rg_i=_block_diag_grad(g_wi),
                 b_rg_i=g_bi, lru_lambda=g_lam, w_rec_o=g_w_rec_o, w_out=g_w_out, ln2_g=g_ln2,
                 w_ff1=g_w_ff1, w_ff2=g_w_ff2, lnf_g=g_lnf)
    return loss, grad_x.reshape(1, T, D), grads


_ANY = pl.BlockSpec(memory_space=pl.ANY)
N_PEERS = N_CHIPS - 1


def _place():
    x, y, c = lax.axis_index("x"), lax.axis_index("y"), lax.axis_index("c")
    peers = [(1 - x, y), (x, 1 - y), (1 - x, 1 - y)]
    return x, y, c, 2 * x + y, peers


def _remote(src, dst, send_sem, recv_sem, dev):
    return pltpu.make_async_remote_copy(src_ref=src, dst_ref=dst, send_sem=send_sem, recv_sem=recv_sem,
                                        device_id=dev, device_id_type=MESH)


def _prefetch_call(body, name, ids, grid, in_specs, out_specs, out_shape, args, semantics=None):
    spec = pltpu.PrefetchScalarGridSpec(num_scalar_prefetch=1, grid=grid, in_specs=in_specs, out_specs=out_specs)
    return pl.pallas_call(body, name=name, grid_spec=spec, out_shape=out_shape,
                          compiler_params=_params(semantics or ("parallel",) * len(grid)))(ids, *args)


def _cast_bf16(name, w, chip_id, after=()):
    rows, cols = w.shape
    rb = min(rows, 256)

    def body(ids_ref, w_ref, *rest):
        rest[-1][...] = w_ref[...].astype(BF16)

    return _prefetch_call(body, name, chip_id, (rows // rb,),
                          [pl.BlockSpec((rb, cols), lambda i, ids: (i, 0))] + [_ANY] * len(after),
                          pl.BlockSpec((None, rb, cols), lambda i, ids: (ids[0], i, 0)),
                          _sds((N_CHIPS, rows, cols), BF16), (w, *after))


def _dma_sems(*counts):
    return [pltpu.SemaphoreType.DMA((k,)) for k in counts]


_HBM = pl.BlockSpec(memory_space=pltpu.HBM)
_SEM = pl.BlockSpec(memory_space=pltpu.SEMAPHORE)
_SPLIT_COPY = pltpu.CompilerParams(has_side_effects=pltpu.SideEffectType.DATAFLOW_SIDE_EFFECTING)


def _hbm(arrays):
    return [pltpu.with_memory_space_constraint(a, pltpu.HBM) for a in arrays]


def _hbm_like(arrays):
    return [pltpu.HBM(a.shape, a.dtype) for a in arrays]


def _halves(buf, c):
    half = buf.shape[1] // 2
    return pl.ds(c * half, half), pl.ds((1 - c) * half, half)


def _gather_start(name, slots):
    n = len(slots)
    nk = n * N_PEERS

    def body(*refs):
        bufs = refs[n:2 * n]
        send_sems, recv_sems, token = refs[2 * n:]
        x, y, c, chip, peers = _place()
        for t in range(n):
            mine, _ = _halves(bufs[t], c)
            for r, (px, py) in enumerate(peers):
                k = t * N_PEERS + r
                own = bufs[t].at[chip, mine]
                _remote(own, own, send_sems.at[k], recv_sems.at[k], (px, py, c)).start()
        token[...] = jnp.zeros_like(token)

    res = pl.pallas_call(
        body, name=name, in_specs=[_HBM] * n, out_specs=[_HBM] * n + [_SEM, _SEM, pl.BlockSpec(memory_space=pltpu.VMEM)],
        out_shape=_hbm_like(slots) + [pltpu.SemaphoreType.DMA((nk,)), pltpu.SemaphoreType.DMA((nk,)),
                                      _sds((8, 128), F32)],
        input_output_aliases={t: t for t in range(n)}, compiler_params=_SPLIT_COPY)(*_hbm(slots))
    return res[:n], (res[n], res[n + 1]), res[n + 2]


def _gather_wait(name, bufs, sems, after):
    n = len(bufs)
    after = tuple(after) if isinstance(after, (tuple, list)) else (after,)

    def body(*refs):
        ins = refs[:n]
        send_sems, recv_sems = refs[n], refs[n + 1]
        x, y, c, chip, peers = _place()
        for t in range(n):
            mine, _ = _halves(ins[t], c)
            for r, (px, py) in enumerate(peers):
                k = t * N_PEERS + r
                cp = _remote(ins[t].at[chip, mine], ins[t].at[2 * px + py, mine], send_sems.at[k], recv_sems.at[k],
                             (px, py, c))
                cp.wait_send()
                cp.wait_recv()

    return pl.pallas_call(
        body, name=name, in_specs=[_HBM] * n + [_SEM, _SEM] + [_ANY] * len(after), out_specs=[_HBM] * n,
        out_shape=_hbm_like(bufs), input_output_aliases={t: t for t in range(n)},
        compiler_params=_SPLIT_COPY)(*bufs, *sems, *after)


def _gather_forward(name, bufs):
    n = len(bufs)
    nk = n * N_PEERS

    def body(*refs):
        outs = refs[n:2 * n]
        send_sems, recv_sems = refs[2 * n:]
        x, y, c, chip, peers = _place()
        sibling = (x, y, 1 - c)
        sends = []
        for t in range(n):
            mine, _ = _halves(outs[t], c)
            for r, (px, py) in enumerate(peers):
                k = t * N_PEERS + r
                landed = outs[t].at[2 * px + py, mine]
                sends.append(_remote(landed, landed, send_sems.at[k], recv_sems.at[k], sibling))
                sends[-1].start()
        for t in range(n):
            _, theirs = _halves(outs[t], c)
            for r, (px, py) in enumerate(peers):
                k = t * N_PEERS + r
                landed = outs[t].at[2 * px + py, theirs]
                _remote(landed, landed, send_sems.at[k], recv_sems.at[k], sibling).wait_recv()
        for cp in sends:
            cp.wait_send()

    return pl.pallas_call(
        body, name=name, in_specs=[_ANY] * n, out_specs=[_ANY] * n, out_shape=[_sds(b.shape, b.dtype) for b in bufs],
        input_output_aliases={t: t for t in range(n)}, scratch_shapes=_dma_sems(nk, nk))(*bufs)


def _pair_copies(n, srcs, lands, send_sems, recv_sems):
    x, y, c, _, _ = _place()
    sibling = (x, y, 1 - c)
    copies = []
    for t in range(n):
        half = srcs[t].shape[1] // 2
        for j in range(N_CHIPS):
            k = t * N_CHIPS + j
            copies.append(_remote(srcs[t].at[j, pl.ds((1 - c) * half, half)], lands[t].at[j],
                                  send_sems.at[k], recv_sems.at[k], sibling))
    for t in range(n, len(srcs)):
        k = n * N_CHIPS + t - n
        copies.append(_remote(srcs[t], lands[t], send_sems.at[k], recv_sems.at[k], sibling))
    return copies


def _pair_start(name, grads, wholes=()):
    n = len(grads)
    srcs = list(grads) + list(wholes)
    m = len(srcs)
    lands = [pltpu.HBM((N_CHIPS, g.shape[1] // 2, g.shape[2]), F32) for g in grads] + _hbm_like(wholes)
    ns = n * N_CHIPS + len(wholes)

    def body(*refs):
        src_refs, land_refs = refs[m:2 * m], refs[2 * m:3 * m]
        send_sems, recv_sems, token = refs[3 * m:]
        for cp in _pair_copies(n, src_refs, land_refs, send_sems, recv_sems):
            cp.start()
        token[...] = jnp.zeros_like(token)

    res = pl.pallas_call(
        body, name=name, in_specs=[_HBM] * m,
        out_specs=[_HBM] * (2 * m) + [_SEM, _SEM, pl.BlockSpec(memory_space=pltpu.VMEM)],
        out_shape=_hbm_like(srcs) + lands + [pltpu.SemaphoreType.DMA((ns,)), pltpu.SemaphoreType.DMA((ns,)),
                                             _sds((8, 128), F32)],
        input_output_aliases={t: t for t in range(m)}, compiler_params=_SPLIT_COPY)(*_hbm(srcs))
    return (res[:m], res[m:2 * m], (res[2 * m], res[2 * m + 1])), res[2 * m + 2]


def _pair_wait(name, flight, n, after):
    srcs, lands, sems = flight
    m = len(srcs)

    def body(*refs):
        for cp in _pair_copies(n, refs[:m], refs[m:2 * m], refs[2 * m], refs[2 * m + 1]):
            cp.wait_send()
            cp.wait_recv()

    res = pl.pallas_call(
        body, name=name, in_specs=[_HBM] * (2 * m) + [_SEM, _SEM, _ANY], out_specs=[_HBM] * (2 * m),
        out_shape=_hbm_like(srcs) + _hbm_like(lands), input_output_aliases={t: t for t in range(2 * m)},
        compiler_params=_SPLIT_COPY)(*srcs, *lands, *sems, after)
    return res[:m], res[m:]


def _chip_copies(srcs, lands, small_src, small_land, send_sems, recv_sems):
    x, y, c, chip, peers = _place()
    n = len(srcs)
    copies = []
    for r, (px, py) in enumerate(peers):
        for t in range(n):
            k = t * N_PEERS + r
            copies.append(_remote(srcs[t].at[2 * px + py], lands[t].at[r], send_sems.at[k], recv_sems.at[k], (px, py, c)))
        if small_src is not None:
            k = n * N_PEERS + r
            half_s = small_src.shape[0] // 2
            copies.append(_remote(small_src.at[pl.ds(c * half_s, half_s)], small_land.at[r],
                                  send_sems.at[k], recv_sems.at[k], (px, py, c)))
    return copies


def _chip_start(name, sums_bf16, small=None):
    n = len(sums_bf16)
    srcs = list(sums_bf16) + ([small] if small is not None else [])
    m = len(srcs)
    lands = [pltpu.HBM((N_PEERS,) + s.shape[1:], BF16) for s in sums_bf16]
    if small is not None:
        lands.append(pltpu.HBM((N_PEERS, small.shape[0] // 2, 128), F32))
    nk = m * N_PEERS

    def body(*refs):
        src_refs, land_refs = refs[m:2 * m], refs[2 * m:3 * m]
        send_sems, recv_sems, token = refs[3 * m:]
        small_src, small_land = (src_refs[n], land_refs[n]) if small is not None else (None, None)
        for cp in _chip_copies(src_refs[:n], land_refs[:n], small_src, small_land, send_sems, recv_sems):
            cp.start()
        token[...] = jnp.zeros_like(token)

    res = pl.pallas_call(
        body, name=name, in_specs=[_HBM] * m,
        out_specs=[_HBM] * (2 * m) + [_SEM, _SEM, pl.BlockSpec(memory_space=pltpu.VMEM)],
        out_shape=_hbm_like(srcs) + lands + [pltpu.SemaphoreType.DMA((nk,)), pltpu.SemaphoreType.DMA((nk,)),
                                             _sds((8, 128), F32)],
        input_output_aliases={t: t for t in range(m)}, compiler_params=_SPLIT_COPY)(*_hbm(srcs))
    return (res[:m], res[m:2 * m], (res[2 * m], res[2 * m + 1])), res[2 * m + 2]


def _chip_wait(name, flight, with_small, after):
    srcs, lands, sems = flight
    m = len(srcs)
    n = m - 1 if with_small else m

    def body(*refs):
        src_refs, land_refs = refs[:m], refs[m:2 * m]
        send_sems, recv_sems = refs[2 * m], refs[2 * m + 1]
        small_src, small_land = (src_refs[n], land_refs[n]) if with_small else (None, None)
        for cp in _chip_copies(src_refs[:n], land_refs[:n], small_src, small_land, send_sems, recv_sems):
            cp.wait_send()
            cp.wait_recv()

    res = pl.pallas_call(
        body, name=name, in_specs=[_HBM] * (2 * m) + [_SEM, _SEM, _ANY], out_specs=[_HBM] * (2 * m),
        out_shape=_hbm_like(srcs) + _hbm_like(lands), input_output_aliases={t: t for t in range(2 * m)},
        compiler_params=_SPLIT_COPY)(*srcs, *lands, *sems, after)
    return res[:m], res[m:]


def _swap_start(name, bufs):
    n = len(bufs)

    def body(*refs):
        outs = refs[n:2 * n]
        send_sems, recv_sems, token = refs[2 * n:]
        x, y, c, _, _ = _place()
        for t in range(n):
            h = outs[t].shape[0] // 2
            mine = outs[t].at[pl.ds(c * h, h)]
            _remote(mine, mine, send_sems.at[t], recv_sems.at[t], (x, y, 1 - c)).start()
        token[...] = jnp.zeros_like(token)

    res = pl.pallas_call(
        body, name=name, in_specs=[_HBM] * n, out_specs=[_HBM] * n + [_SEM, _SEM, pl.BlockSpec(memory_space=pltpu.VMEM)],
        out_shape=_hbm_like(bufs) + [pltpu.SemaphoreType.DMA((n,)), pltpu.SemaphoreType.DMA((n,)), _sds((8, 128), F32)],
        input_output_aliases={t: t for t in range(n)}, compiler_params=_SPLIT_COPY)(*_hbm(bufs))
    return (res[:n], (res[n], res[n + 1])), res[n + 2]


def _swap_wait(name, flight, after):
    bufs, sems = flight
    n = len(bufs)

    def body(*refs):
        ins = refs[:n]
        send_sems, recv_sems = refs[n], refs[n + 1]
        x, y, c, _, _ = _place()
        for t in range(n):
            h = ins[t].shape[0] // 2
            cp = _remote(ins[t].at[pl.ds(c * h, h)], ins[t].at[pl.ds((1 - c) * h, h)], send_sems.at[t],
                         recv_sems.at[t], (x, y, 1 - c))
            cp.wait_send()
            cp.wait_recv()

    return pl.pallas_call(
        body, name=name, in_specs=[_HBM] * n + [_SEM, _SEM, _ANY], out_specs=[_HBM] * n, out_shape=_hbm_like(bufs),
        input_output_aliases={t: t for t in range(n)}, compiler_params=_SPLIT_COPY)(*bufs, *sems, after)


def _pair_sum(name, grad, got, ids):
    _, rows, cols = got.shape
    rb = min(rows, 256)
    nb = rows // rb
    blk = pl.BlockSpec((None, rb, cols), lambda i, j, ids: (j, i, 0))
    mine = pl.BlockSpec((None, rb, cols), lambda i, j, ids: (j, ids[1] * nb + i, 0))
    own = pl.BlockSpec((rb, cols), lambda i, j, ids: (i, 0))

    def body(ids_ref, a_ref, b_ref, s_ref, sb_ref):
        s = a_ref[...] + b_ref[...]
        sb_ref[...] = s.astype(BF16)

        @pl.when(pl.program_id(1) == ids_ref[0])
        def _():
            s_ref[...] = s

    return _prefetch_call(body, name, ids, (nb, N_CHIPS), [mine, blk], [own, blk],
                          [_sds((rows, cols), F32), _sds(got.shape, BF16)], (grad, got),
                          semantics=("parallel", "arbitrary"))


def _chip_sum(name, own_sum, got, ids):
    rows, cols = own_sum.shape
    rb = min(rows, 256)
    nb = rows // rb
    own = pl.BlockSpec((rb, cols), lambda i, ids: (i, 0))
    blk3 = pl.BlockSpec((N_PEERS, rb, cols), lambda i, ids: (0, i, 0))
    out = pl.BlockSpec((rb, cols), lambda i, ids: (ids[1] * nb + i, 0))

    def body(ids_ref, a_ref, b_ref, o_ref):
        o_ref[...] = ((a_ref[...] + b_ref[0].astype(F32)) + b_ref[1].astype(F32)) + b_ref[2].astype(F32)

    return _prefetch_call(body, name, ids, (nb,), [own, blk3], out, _sds((2 * rows, cols), F32), (own_sum, got))


SMALL_RB = 280


def _small_pair_sum(own, got):
    blk = pl.BlockSpec((SMALL_RB, 128), lambda i: (i, 0))

    def body(a_ref, b_ref, o_ref):
        o_ref[...] = a_ref[...] + b_ref[...]

    return pl.pallas_call(body, name="small_pair_sum", grid=(own.shape[0] // SMALL_RB,), in_specs=[blk, blk],
                          out_specs=blk, out_shape=_sds(own.shape, F32),
                          compiler_params=_params(("parallel",)))(own, got)


def _small_chip_sum(pair, got, ids):
    nb = pair.shape[0] // 2 // SMALL_RB
    half = pl.BlockSpec((SMALL_RB, 128), lambda i, ids: (ids[1] * nb + i, 0))
    blk3 = pl.BlockSpec((N_PEERS, SMALL_RB, 128), lambda i, ids: (0, i, 0))

    def body(ids_ref, a_ref, b_ref, o_ref):
        o_ref[...] = (a_ref[...] + b_ref[1]) + (b_ref[0] + b_ref[2])

    return _prefetch_call(body, "small_chip_sum", ids, (nb,), [half, blk3], half, _sds(pair.shape, F32), (pair, got))


def _adamw_math(w, g, m, v):
    m = ADAM_B1 * m + (1.0 - ADAM_B1) * g
    v = ADAM_B2 * v + (1.0 - ADAM_B2) * (g * g)
    m_hat = m / (1.0 - ADAM_B1 ** ADAM_STEP)
    v_hat = v / (1.0 - ADAM_B2 ** ADAM_STEP)
    delta = -ADAM_LR * (m_hat / (jnp.sqrt(v_hat) + ADAM_EPS) + ADAM_WD * w)
    return delta, m, v


def _adamw(name, w, g, m, v, rb=None):
    rows, cols = w.shape
    rb = rows if rb is None else rb
    blk = pl.BlockSpec((rb, cols), lambda i: (i, 0))

    def body(w_ref, g_ref, m_ref, v_ref, d_ref, nm_ref, nv_ref):
        d, nm, nv = _adamw_math(w_ref[...], g_ref[...], m_ref[...], v_ref[...])
        d_ref[...] = d
        nm_ref[...] = nm
        nv_ref[...] = nv

    return pl.pallas_call(body, name=name, grid=(rows // rb,), in_specs=[blk] * 4, out_specs=[blk] * 3,
                          out_shape=[_sds(w.shape, F32)] * 3, compiler_params=_params(("parallel",)))(w, g, m, v)


def _adamw_small(ws, gs, ms, vs):
    n = len(ws)

    def body(*refs):
        for t in range(n):
            w_ref, g_ref, m_ref, v_ref = (refs[k * n + t] for k in range(4))
            d, nm, nv = _adamw_math(w_ref[...], g_ref[...], m_ref[...], v_ref[...])
            for k, val in enumerate((d, nm, nv)):
                refs[(4 + k) * n + t][...] = val

    res = pl.pallas_call(body, name="adamw_small", out_shape=[_sds(a.shape, F32) for a in ws] * 3,
                         compiler_params=_params())(*ws, *gs, *ms, *vs)
    return [(res[t], res[n + t], res[2 * n + t]) for t in range(n)]


BIG = ("w_in", "w_att_o", "w_rec_o", "w_out", "w_ff1", "w_ff2")
SHARDED_VECS = ("conv_w", "b_rg_a", "b_rg_i", "lru_lambda")
SMALL = ("ln1_g", "b_in", "rpb", "conv_w", "conv_b", "w_rg_a", "b_rg_a", "w_rg_i", "b_rg_i", "lru_lambda",
         "ln2_g", "lnf_g")
SMALL_ROWS = 2240
ORDER = ("ln1_g", "w_in", "b_in", "rpb", "w_att_o", "conv_w", "conv_b", "w_rg_a", "b_rg_a", "w_rg_i", "b_rg_i",
         "lru_lambda", "w_rec_o", "w_out", "ln2_g", "w_ff1", "w_ff2", "lnf_g")


def _pack_small(grads, loss):
    parts, sizes = [], {}
    for n in SMALL:
        flat = grads[n].reshape(-1)
        pad = (-flat.shape[0]) % 128
        sizes[n] = (flat.shape[0], flat.shape[0] + pad)
        parts.append(jnp.pad(flat, (0, pad)))
    total = sum(s[1] for s in sizes.values())
    parts.append(jnp.pad(loss.reshape(1), (0, SMALL_ROWS * 128 - total - 1)))
    return jnp.concatenate(parts).reshape(SMALL_ROWS, 128), sizes


def _unpack_small(buf, sizes, shapes):
    flat = buf.reshape(-1)
    out, pos = {}, 0
    for n in SMALL:
        size, padded = sizes[n]
        out[n] = flat[pos:pos + size].reshape(shapes[n])
        pos += padded
    return out, flat[pos]


def _gather_weights(w, chip):
    chip_id = chip.astype(jnp.int32).reshape(1)
    vec_rows = [w[n][0] for n in SHARDED_VECS]
    vec_shard = jnp.concatenate(vec_rows + [jnp.zeros((16 - 10, D // N_CHIPS), F32)], axis=0)
    vec_slots = lax.dynamic_update_slice(jnp.zeros((N_CHIPS, 16, D // N_CHIPS), F32), vec_shard[None], (chip, 0, 0))
    bufs_a, sems_a, token_a = _gather_start("gather_start_first", [_cast_bf16("cast_w_in", w["w_in"][0], chip_id), vec_slots])
    rest_names = BIG[1:]
    bufs_b, sems_b, token_b = _gather_start(
        "gather_start_rest", [_cast_bf16("cast_" + n, w[n][0], chip_id, after=(token_a,)) for n in rest_names])

    def first(after):
        w_in_full, vec_full = _gather_forward("gather_forward_first", _gather_wait("gather_wait_first", bufs_a, sems_a, after))
        vecs = vec_full.transpose(1, 0, 2).reshape(16, D)
        return dict(w_in=w_in_full, conv_w=vecs[0:4], b_rg_a=vecs[4:6], b_rg_i=vecs[6:8], lru_lambda=vecs[8:10])

    def rest(after):
        full = dict(zip(rest_names, _gather_forward("gather_forward_rest",
                                                    _gather_wait("gather_wait_rest", bufs_b, sems_b, after))))
        return dict(w_att_o=full["w_att_o"], w_ff1=full["w_ff1"], w_rec_o=full["w_rec_o"].reshape(D, D),
                    w_out=full["w_out"].reshape(D, D), w_ff2=full["w_ff2"].reshape(D_FF, D))

    p = dict(ln1_g=w["ln1_g"], b_in=w["b_in"], rpb=w["rpb"][0], conv_b=w["conv_b"], w_rg_a=w["w_rg_a"][0],
             w_rg_i=w["w_rg_i"][0], ln2_g=w["ln2_g"], lnf_g=w["lnf_g"].reshape(1, D))
    return p, ((token_b,), first, rest)


class _Reducer:
    def __init__(self, ids):
        self.ids = ids
        self.groups = {}

    def begin(self, tag, grads, small=None):
        names = list(grads)
        big = [grads[n].reshape(N_CHIPS, -1, grads[n].shape[-1]) for n in names]
        flight, token = _pair_start("pair_start_" + tag, big, [] if small is None else [small])
        self.groups[tag] = dict(names=names, pair=flight, small=small is not None)
        return (token,)

    def advance(self, tag, after):
        grp = self.groups[tag]
        n = len(grp["names"])
        mine, got = _pair_wait("pair_wait_" + tag, grp["pair"], n, after)
        sums = [_pair_sum("pair_sum_" + name, a, b, self.ids) for name, a, b in zip(grp["names"], mine, got)]
        small_sum = _small_pair_sum(mine[n], got[n]) if grp["small"] else None
        grp["chip"], token = _chip_start("chip_start_" + tag, [s[1] for s in sums], small_sum)
        grp["sums"] = [s[0] for s in sums]
        self.last_token = token
        return (token,)

    def finish(self, tag, after):
        grp = self.groups[tag]
        srcs, lands = _chip_wait("chip_wait_" + tag, grp["chip"], grp["small"], after)
        halves = [_chip_sum("chip_sum_" + name, s, b, self.ids) for name, s, b in zip(grp["names"], grp["sums"], lands)]
        if grp["small"]:
            halves.append(_small_chip_sum(srcs[-1], lands[-1], self.ids))
        grp["swap"], token = _swap_start("swap_start_" + tag, halves)
        return token

    def result(self, tag, after):
        return _swap_wait("swap_wait_" + tag, self.groups[tag]["swap"], after)


def kernel(x, ln1_g, w_in, b_in, rpb, w_att_o, conv_w, conv_b, w_rg_a, b_rg_a, w_rg_i, b_rg_i, lru_lambda, w_rec_o, w_out, ln2_g, w_ff1, w_ff2, lnf_g, loss_target, m_ln1_g, m_w_in, m_b_in, m_rpb, m_w_att_o, m_conv_w, m_conv_b, m_w_rg_a, m_b_rg_a, m_w_rg_i, m_b_rg_i, m_lru_lambda, m_w_rec_o, m_w_out, m_ln2_g, m_w_ff1, m_w_ff2, m_lnf_g, v_ln1_g, v_w_in, v_b_in, v_rpb, v_w_att_o, v_conv_w, v_conv_b, v_w_rg_a, v_b_rg_a, v_w_rg_i, v_b_rg_i, v_lru_lambda, v_w_rec_o, v_w_out, v_ln2_g, v_w_ff1, v_w_ff2, v_lnf_g):
    w = dict(ln1_g=ln1_g, w_in=w_in, b_in=b_in, rpb=rpb, w_att_o=w_att_o, conv_w=conv_w, conv_b=conv_b,
             w_rg_a=w_rg_a, b_rg_a=b_rg_a, w_rg_i=w_rg_i, b_rg_i=b_rg_i, lru_lambda=lru_lambda, w_rec_o=w_rec_o,
             w_out=w_out, ln2_g=ln2_g, w_ff1=w_ff1, w_ff2=w_ff2, lnf_g=lnf_g)
    m = dict(ln1_g=m_ln1_g, w_in=m_w_in, b_in=m_b_in, rpb=m_rpb, w_att_o=m_w_att_o, conv_w=m_conv_w,
             conv_b=m_conv_b, w_rg_a=m_w_rg_a, b_rg_a=m_b_rg_a, w_rg_i=m_w_rg_i, b_rg_i=m_b_rg_i,
             lru_lambda=m_lru_lambda, w_rec_o=m_w_rec_o, w_out=m_w_out, ln2_g=m_ln2_g, w_ff1=m_w_ff1,
             w_ff2=m_w_ff2, lnf_g=m_lnf_g)
    v = dict(ln1_g=v_ln1_g, w_in=v_w_in, b_in=v_b_in, rpb=v_rpb, w_att_o=v_w_att_o, conv_w=v_conv_w,
             conv_b=v_conv_b, w_rg_a=v_w_rg_a, b_rg_a=v_b_rg_a, w_rg_i=v_w_rg_i, b_rg_i=v_b_rg_i,
             lru_lambda=v_lru_lambda, w_rec_o=v_w_rec_o, w_out=v_w_out, ln2_g=v_ln2_g, w_ff1=v_w_ff1,
             w_ff2=v_w_ff2, lnf_g=v_lnf_g)
    chip = 2 * lax.axis_index("x") + lax.axis_index("y")
    ids = jnp.stack([chip, lax.axis_index("c")]).astype(jnp.int32)

    out_grad, out_delta, out_m, out_v = {}, {}, {}, {}

    def update(n, gn):
        shape, two_d = w[n].shape, gn.shape
        d, nm, nv = _adamw("adamw_" + n, w[n].reshape(two_d), gn, m[n].reshape(two_d), v[n].reshape(two_d), 256)
        out_grad[n], out_delta[n], out_m[n], out_v[n] = (gn.reshape(shape), d.reshape(shape), nm.reshape(shape),
                                                         nv.reshape(shape))
        return d

    reducer = _Reducer(ids)
    p, late = _gather_weights(w, chip)
    loss, grad_x, g = _local_step(x, loss_target, p, late, reducer)
    small, sizes = _pack_small(g, loss + reducer.last_token[:1, :1])
    after = reducer.begin("small", {}, small)[0]
    after = reducer.finish("ff", after)
    after = reducer.advance("small", after)[0]
    order = ("ff", "proj", "in", "small")
    for tag, following in zip(order[:-1], order[1:]):
        after = reducer.finish(following, after)
        for n, red in zip(reducer.groups[tag]["names"], reducer.result(tag, after)):
            after = update(n, red)
    (small_red,) = reducer.result("small", after)
    gsmall, loss = _unpack_small(small_red, sizes, {n: g[n].shape for n in SMALL})
    two_d = {n: (int(np.prod(w[n].shape[:-1])), w[n].shape[-1]) for n in SMALL}
    for n in SHARDED_VECS:
        gsmall[n] = lax.dynamic_slice_in_dim(gsmall[n], chip * (D // N_CHIPS), D // N_CHIPS, axis=1)
    gs = [gsmall[n].reshape(two_d[n]) for n in SMALL]
    updates = _adamw_small([w[n].reshape(two_d[n]) for n in SMALL], gs, [m[n].reshape(two_d[n]) for n in SMALL],
                           [v[n].reshape(two_d[n]) for n in SMALL])
    for n, gn, (d, nm, nv) in zip(SMALL, gs, updates):
        shape = w[n].shape
        out_grad[n], out_delta[n], out_m[n], out_v[n] = (gn.reshape(shape), d.reshape(shape), nm.reshape(shape),
                                                         nv.reshape(shape))
    return (loss, grad_x, *[out_grad[n] for n in ORDER], *[out_delta[n] for n in ORDER],
            *[out_m[n] for n in ORDER], *[out_v[n] for n in ORDER])
```

```python
import functools

import numpy as np
import jax
import jax.numpy as jnp
from jax import lax
from jax.experimental import pallas as pl
from jax.experimental.pallas import tpu as pltpu

F32 = jnp.float32
BF16 = jnp.bfloat16

T = 2048
D = 1024
D_ATT = 512
D_IN = 5632
D_FF = 4096
N_HEADS = 8
HEAD_DIM = 64
GRID_W = 64
N_ROWS = T // GRID_W
WIN_H = 8
WIN_W = 16
KEYS = WIN_H * GRID_W
N_CHIPS = 4
EPS = 1e-6
LRU_C = 8.0
SCALE = HEAD_DIM ** -0.5
REC_CB = 256
REC_CHUNK = 256
PAD = 8

ADAM_LR = 0.001
ADAM_B1 = 0.9
ADAM_B2 = 0.999
ADAM_EPS = 1e-08
ADAM_WD = 0.01
ADAM_STEP = 10

VMEM_LIMIT = 56 * 1024 * 1024

NN = (((1,), (0,)), ((), ()))
NT = (((1,), (1,)), ((), ()))
TN = (((0,), (0,)), ((), ()))
MESH = pl.DeviceIdType.MESH


def _params(sem=None):
    return pltpu.CompilerParams(dimension_semantics=sem, vmem_limit_bytes=VMEM_LIMIT)


def _dot(a, b, dims):
    return lax.dot_general(a, b, dims, preferred_element_type=F32)


def _sigmoid(x):
    return 0.5 * jnp.tanh(0.5 * x) + 0.5


def _matmul(name, a, b, *, dims, grid, a_spec, b_spec, out_shapes, out_specs, acc_shape,
            extras=(), extra_specs=(), epilogue=None, colsum_spec=None, colsum_shape=None, after=(),
            semantics=("parallel", "parallel", "arbitrary"), epilogue_takes_first=False):
    nk = grid[2]
    n_extra = len(extras)
    n_out = len(out_shapes)
    with_colsum = colsum_spec is not None

    def body(a_ref, b_ref, *rest):
        ex = rest[:n_extra]
        rest = rest[:n_extra] + rest[n_extra + len(after):]
        outs = rest[n_extra:n_extra + n_out]
        pos = n_extra + n_out
        cs_out = rest[pos] if with_colsum else None
        pos += 1 if with_colsum else 0
        acc = rest[pos]
        cs_acc = rest[pos + 1] if with_colsum else None
        k = pl.program_id(2)
        first_tile = pl.program_id(0) == 0

        @pl.when(k == 0)
        def _():
            acc[...] = jnp.zeros_like(acc)
            if with_colsum:
                cs_acc[...] = jnp.zeros_like(cs_acc)

        bv = b_ref[...]
        acc[...] += _dot(a_ref[...].astype(BF16), bv.astype(BF16), dims)
        if with_colsum:
            cs_acc[...] += jnp.sum(bv.astype(F32), axis=0, keepdims=True)

        @pl.when(k == nk - 1)
        def _():
            r = acc[...]
            if epilogue is None:
                outs[0][...] = r.astype(outs[0].dtype)
            elif epilogue_takes_first:
                epilogue(r, ex, outs, first_tile)
            else:
                epilogue(r, ex, outs)
            if with_colsum:
                cs_out[...] = cs_acc[...]

    shapes = list(out_shapes)
    specs = list(out_specs)
    scratch = [pltpu.VMEM(acc_shape, F32)]
    if with_colsum:
        shapes.append(colsum_shape)
        specs.append(colsum_spec)
        scratch.append(pltpu.VMEM((1, acc_shape[1]), F32))
    res = pl.pallas_call(
        body, name=name, grid=grid,
        in_specs=[a_spec, b_spec, *extra_specs] + [_ANY] * len(after),
        out_specs=specs, out_shape=shapes, scratch_shapes=scratch,
        compiler_params=_params(semantics),
    )(a, b, *extras, *after)
    return res


def _sds(shape, dtype):
    return jax.ShapeDtypeStruct(shape, dtype)


TM = 1024
NI = T // TM
TJ = T
NJ = T // TJ


def _mm_nn_cols(name, a, wg, out_dtype, *, bias=None, extras=(), extra_specs=(), epilogue=None,
                out_shapes=None, out_specs=None):
    k_dim, n4 = wg.shape[1], wg.shape[2]
    ex, exs = list(extras), list(extra_specs)
    if bias is not None:
        ex = [bias] + ex
        exs = [pl.BlockSpec((1, n4), lambda j, i, k: (0, j))] + exs
        user_ep = epilogue

        def epilogue(r, e, outs):
            r = r + e[0][...]
            if user_ep is None:
                outs[0][...] = r.astype(outs[0].dtype)
            else:
                user_ep(r, e[1:], outs)
    if out_shapes is None:
        out_shapes = [_sds((T, N_CHIPS * n4), out_dtype)]
        out_specs = [pl.BlockSpec((TJ, n4), lambda j, i, k: (i, j))]
    return _matmul(
        name, a, wg, dims=NN, grid=(N_CHIPS, NJ, 1),
        a_spec=pl.BlockSpec((TJ, k_dim), lambda j, i, k: (i, 0)),
        b_spec=pl.BlockSpec((None, k_dim, n4), lambda j, i, k: (j, 0, 0)),
        out_shapes=out_shapes, out_specs=out_specs, acc_shape=(TJ, n4),
        extras=ex, extra_specs=exs, epilogue=epilogue)


def _mm_nt_cols_rms_bwd(name, a, wg, x, g, dres, after=(), bf16_copy=False):
    n4 = wg.shape[2]
    row = pl.BlockSpec((TM, D), lambda i, j, k: (i, 0))
    vec = pl.BlockSpec((1, D), lambda i, j, k: (0, 0))

    def epilogue(dhv, ex, outs, first):
        x_ref, g_ref, dres_ref = ex
        dx_ref, dg_ref = outs[0], outs[-1]
        xv = x_ref[...]
        rstd = lax.rsqrt(jnp.mean(xv * xv, axis=-1, keepdims=True) + EPS)
        xhat = xv * rstd
        dy = dhv * g_ref[...]
        dx = dres_ref[...] + rstd * (dy - xhat * jnp.mean(dy * xhat, axis=-1, keepdims=True))
        dx_ref[...] = dx
        if bf16_copy:
            outs[1][...] = dx.astype(BF16)
        part = jnp.sum(dhv * xhat, axis=0, keepdims=True)

        @pl.when(first)
        def _():
            dg_ref[...] = part

        @pl.when(jnp.logical_not(first))
        def _():
            dg_ref[...] += part

    return _matmul(
        name, a, wg, dims=NT, grid=(NI, 1, N_CHIPS),
        a_spec=pl.BlockSpec((TM, n4), lambda i, j, k: (i, k)),
        b_spec=pl.BlockSpec((None, D, n4), lambda i, j, k: (k, 0, 0)),
        out_shapes=[_sds((T, D), F32)] + [_sds((T, D), BF16)] * bf16_copy + [_sds((1, D), F32)],
        out_specs=[row] + [row] * bf16_copy + [vec], acc_shape=(TM, D),
        extras=[x, g, dres], extra_specs=[row, vec, row], epilogue=epilogue, after=after,
        semantics=("arbitrary", "arbitrary", "arbitrary"), epilogue_takes_first=True)


def _mm_nt_rows(name, a, w, out_dtype, *, tn, extras=(), extra_specs=(), epilogue=None):
    k_dim, n = w.shape
    return _matmul(
        name, a, w, dims=NT, grid=(k_dim // tn, NJ, 1),
        a_spec=pl.BlockSpec((TJ, n), lambda j, i, k: (i, 0)),
        b_spec=pl.BlockSpec((tn, n), lambda j, i, k: (j, 0)),
        out_shapes=[_sds((T, k_dim), out_dtype)],
        out_specs=[pl.BlockSpec((TJ, tn), lambda j, i, k: (i, j))], acc_shape=(TJ, tn),
        extras=extras, extra_specs=extra_specs, epilogue=epilogue)


def _mm_tn_cols(name, a, g, n4, *, colsum=False):
    k_dim = a.shape[1]
    kw = {}
    if colsum:
        kw = dict(colsum_spec=pl.BlockSpec((1, n4), lambda j, i, k: (0, j)),
                  colsum_shape=_sds((1, N_CHIPS * n4), F32))
    return _matmul(
        name, a, g, dims=TN, grid=(N_CHIPS, 1, NJ),
        a_spec=pl.BlockSpec((TJ, k_dim), lambda j, i, k: (k, 0)),
        b_spec=pl.BlockSpec((TJ, n4), lambda j, i, k: (k, j)),
        out_shapes=[_sds((N_CHIPS, k_dim, n4), F32)],
        out_specs=[pl.BlockSpec((None, k_dim, n4), lambda j, i, k: (j, 0, 0))],
        acc_shape=(k_dim, n4), **kw)


def _mm_tn_rows(name, a, g, *, tm):
    k_dim, n = a.shape[1], g.shape[1]
    return _matmul(
        name, a, g, dims=TN, grid=(k_dim // tm, 1, NJ),
        a_spec=pl.BlockSpec((TJ, tm), lambda j, i, k: (k, j)),
        b_spec=pl.BlockSpec((TJ, n), lambda j, i, k: (k, 0)),
        out_shapes=[_sds((k_dim, n), F32)],
        out_specs=[pl.BlockSpec((tm, n), lambda j, i, k: (j, 0))], acc_shape=(tm, n))


TE = 256
NE = T // TE
_ROW = pl.BlockSpec((TE, D), lambda i: (i, 0))
_VEC = pl.BlockSpec((1, D), lambda i: (0, 0))


def _rms_fwd(name, x, g, after=()):
    def body(x_ref, g_ref, *rest):
        h_ref = rest[-1]
        xv = x_ref[...]
        rstd = lax.rsqrt(jnp.mean(xv * xv, axis=-1, keepdims=True) + EPS)
        h_ref[...] = (xv * rstd * g_ref[...]).astype(BF16)

    return pl.pallas_call(body, name=name, grid=(NE,), in_specs=[_ROW, _VEC] + [_ANY] * len(after), out_specs=_ROW,
                          out_shape=_sds((T, D), BF16), compiler_params=_params(("parallel",)))(x, g, *after)


def _mm_x2_loss_head(s, w_ff2, x1, target, g):
    k_dim = w_ff2.shape[0]
    row = pl.BlockSpec((TM, D), lambda i, j, k: (i, 0))
    vec = pl.BlockSpec((1, D), lambda i, j, k: (0, 0))

    def epilogue(r, ex, outs, first):
        x1_ref, t_ref, g_ref = ex
        loss_ref, dx_ref, dxb_ref, dg_ref = outs
        xv = x1_ref[...] + r
        rstd = lax.rsqrt(jnp.mean(xv * xv, axis=-1, keepdims=True) + EPS)
        xhat = xv * rstd
        gv = g_ref[...]
        err = xhat * gv - t_ref[...]
        dy = err * (1.0 / D)
        dxh = dy * gv
        dx = rstd * (dxh - xhat * jnp.mean(dxh * xhat, axis=-1, keepdims=True))
        dx_ref[...] = dx
        dxb_ref[...] = dx.astype(BF16)
        dg_part = jnp.sum(dy * xhat, axis=0, keepdims=True)
        loss_part = (0.5 / D) * jnp.sum(jnp.sum(err * err, axis=1, keepdims=True), axis=0, keepdims=True)

        @pl.when(first)
        def _():
            dg_ref[...] = dg_part
            loss_ref[...] = loss_part

        @pl.when(jnp.logical_not(first))
        def _():
            dg_ref[...] += dg_part
            loss_ref[...] += loss_part

    return _matmul(
        "mm_x2_loss_head", s, w_ff2, dims=NN, grid=(NI, 1, k_dim // D),
        a_spec=pl.BlockSpec((TM, D), lambda i, j, k: (i, k)), b_spec=pl.BlockSpec((D, D), lambda i, j, k: (k, 0)),
        out_shapes=[_sds((1, 1), F32), _sds((T, D), F32), _sds((T, D), BF16), _sds((1, D), F32)],
        out_specs=[pl.BlockSpec((1, 1), lambda i, j, k: (0, 0)), row, row, vec], acc_shape=(TM, D),
        extras=[x1, target, g], extra_specs=[row, row, vec], epilogue=epilogue,
        semantics=("arbitrary", "arbitrary", "arbitrary"), epilogue_takes_first=True)


MW = 512
_G_ATT_BLK = 3584 // MW
_G_REC_BLK = 4608 // MW


TB = 512


def _branch_specs():
    def row(cols):
        return pl.BlockSpec((TB, cols), lambda i: (i, 0))

    ga = pl.BlockSpec((TB, MW), lambda i: (i, _G_ATT_BLK))
    ga2 = pl.BlockSpec((TB, MW), lambda i: (i, _G_ATT_BLK + 1))
    gr = pl.BlockSpec((TB, MW), lambda i: (i, _G_REC_BLK))
    gr2 = pl.BlockSpec((TB, MW), lambda i: (i, _G_REC_BLK + 1))
    w_att = pl.BlockSpec((N_CHIPS, D_ATT, D // N_CHIPS), lambda i: (0, 0, 0))
    w_sq = pl.BlockSpec((D, D), lambda i: (0, 0))
    return row, (ga, ga2, gr, gr2), w_att, w_sq


def _gate_values(gate_refs):
    ga, ga2, gr, gr2 = (r[...] for r in gate_refs)
    return _sigmoid(jnp.concatenate([ga, ga2], axis=1)), _sigmoid(jnp.concatenate([gr, gr2], axis=1))


def _branches_fwd(att, g, z, x, w_att_o, w_rec_o, w_out, ln2_g):
    row, gate_specs, w_att, w_sq = _branch_specs()

    def body(att_ref, g_ref, ga_ref, ga2_ref, gr_ref, gr2_ref, x_ref, wa_ref, wr_ref, wo_ref, g2_ref,
             ya_ref, yr_ref, m_ref, x1_ref, h2_ref):
        attv = att_ref[...]
        ya = jnp.concatenate([_dot(attv, wa_ref[j], NN) for j in range(N_CHIPS)], axis=1)
        yr = _dot(g_ref[...], wr_ref[...], NN)
        sa, sr = _gate_values((ga_ref, ga2_ref, gr_ref, gr2_ref))
        mixed = (sa * ya + sr * yr).astype(BF16)
        ya_ref[...] = ya
        yr_ref[...] = yr
        m_ref[...] = mixed
        x1 = x_ref[...] + _dot(mixed, wo_ref[...], NN)
        x1_ref[...] = x1
        rstd = lax.rsqrt(jnp.mean(x1 * x1, axis=-1, keepdims=True) + EPS)
        h2_ref[...] = (x1 * rstd * g2_ref[...]).astype(BF16)

    return pl.pallas_call(
        body, name="branches_fwd", grid=(T // TB,),
        in_specs=[row(D_ATT), row(D), *gate_specs, row(D), w_att, w_sq, w_sq, pl.BlockSpec((1, D), lambda i: (0, 0))],
        out_specs=[row(D)] * 5,
        out_shape=[_sds((T, D), F32), _sds((T, D), F32), _sds((T, D), BF16), _sds((T, D), F32), _sds((T, D), BF16)],
        compiler_params=_params(("parallel",)))(att, g, z, z, z, z, x, w_att_o, w_rec_o, w_out, ln2_g)


def _branches_bwd(dx1_b, y_att, y_rec, z, w_att_o, w_rec_o, w_out):
    row, gate_specs, w_att, w_sq = _branch_specs()
    n4 = D // N_CHIPS

    def body(dx_ref, ya_ref, yr_ref, ga_ref, ga2_ref, gr_ref, gr2_ref, wa_ref, wr_ref, wo_ref,
             dya_ref, dyr_ref, dga_ref, dgr_ref, datt_ref, dg_ref):
        dm = _dot(dx_ref[...], wo_ref[...], NT)
        sa, sr = _gate_values((ga_ref, ga2_ref, gr_ref, gr2_ref))
        dya = (dm * sa).astype(BF16)
        dyr = (dm * sr).astype(BF16)
        dya_ref[...] = dya
        dyr_ref[...] = dyr
        dga_ref[...] = (dm * ya_ref[...] * sa * (1.0 - sa)).astype(BF16)
        dgr_ref[...] = (dm * yr_ref[...] * sr * (1.0 - sr)).astype(BF16)
        datt = _dot(dya[:, 0:n4], wa_ref[0], NT)
        for j in range(1, N_CHIPS):
            datt = datt + _dot(dya[:, j * n4:(j + 1) * n4], wa_ref[j], NT)
        datt_ref[...] = datt.astype(BF16)
        dg_ref[...] = _dot(dyr, wr_ref[...], NT).astype(BF16)

    return pl.pallas_call(
        body, name="branches_bwd", grid=(T // TB,),
        in_specs=[row(D), row(D), row(D), *gate_specs, w_att, w_sq, w_sq],
        out_specs=[row(D)] * 4 + [row(D_ATT), row(D)],
        out_shape=[_sds((T, D), BF16)] * 4 + [_sds((T, D_ATT), BF16), _sds((T, D), BF16)],
        compiler_params=_params(("parallel",)))(dx1_b, y_att, y_rec, z, z, z, z, w_att_o, w_rec_o, w_out)


HP = 2 * HEAD_DIM
N_HP = N_HEADS // 2
ATT_UNROLL_FWD = 16
ATT_UNROLL_BWD = 8
DIAG_ROWS = 32


def _window_maps():
    diag = np.zeros((GRID_W * GRID_W, 128), np.float32)
    for qc in range(GRID_W):
        w0 = min(max(qc - WIN_W // 2, 0), GRID_W - WIN_W)
        for kc in range(w0, w0 + WIN_W):
            diag[qc * GRID_W + kc, kc - qc + WIN_W - 1] = 1.0
    return diag, diag.sum(axis=1)[None, :]


def _split3(x):
    a = x.astype(BF16)
    r = x - a.astype(F32)
    b = r.astype(BF16)
    c = (r - b.astype(F32)).astype(BF16)
    return a, b, c


N_DROW = 2 * WIN_H - 1
N_DPAIR = N_DROW - 1


def _bias_pairs(rpb):
    diag, valid = _window_maps()
    r2 = jnp.pad(rpb.reshape(N_HEADS * N_DROW, 2 * WIN_W - 1),
                 ((0, 128 - N_HEADS * N_DROW), (0, 128 - (2 * WIN_W - 1))))

    def body(r_ref, d_ref, v_ref, o_ref):
        dv = d_ref[...]
        t = sum(_dot(part, dv, NN) for part in _split3(r_ref[...]))
        o_ref[...] = jnp.where(v_ref[...] > 0.0, t, -1e30)

    t = pl.pallas_call(body, name="rpb_expand", out_shape=_sds((128, GRID_W * GRID_W), F32),
                       compiler_params=_params())(r2, jnp.asarray(diag.T, BF16), jnp.asarray(valid, F32))
    t = t[:N_HEADS * N_DROW].reshape(N_HEADS, N_DROW, GRID_W, GRID_W)
    return jnp.concatenate([t[:, :N_DPAIR], t[:, 1:]], axis=-1)


def _row_bias(tb_ref, hh, d0):
    return jnp.concatenate([tb_ref[hh, d0 + 2 * ii] for ii in range(WIN_H // 2)], axis=1)


def _row_window(r):
    rs = jnp.clip(r - WIN_H // 2, 0, N_ROWS - WIN_H)
    return pl.multiple_of(r * GRID_W, GRID_W), pl.multiple_of(rs * GRID_W, GRID_W), rs - r + (WIN_H - 1)


def _split_heads(src_ref, dst_ref, scale=None):
    for hh in range(2):
        v = src_ref[:, hh * HEAD_DIM:(hh + 1) * HEAD_DIM]
        dst_ref[hh] = (v if scale is None else v * scale).astype(BF16)


def _attn_items(qb_ref, kb_ref, vb_ref, tb_ref, first_row, n_rows):
    wins = [_row_window(first_row + u) for u in range(n_rows)]
    items = [(u, hh) for u in range(n_rows) for hh in range(2)]
    q = [qb_ref[hh, pl.ds(wins[u][0], GRID_W), :] for u, hh in items]
    k = [kb_ref[hh, pl.ds(wins[u][1], KEYS), :] for u, hh in items]
    v = [vb_ref[hh, pl.ds(wins[u][1], KEYS), :] for u, hh in items]
    s = [_dot(qi, ki, NT) + _row_bias(tb_ref, hh, wins[u][2]) for qi, ki, (u, hh) in zip(q, k, items)]
    m = [jnp.max(si, axis=-1, keepdims=True) for si in s]
    e = [jnp.exp(si - mi) for si, mi in zip(s, m)]
    inv = [1.0 / jnp.sum(ei, axis=-1, keepdims=True) for ei in e]
    p = [ei * li for ei, li in zip(e, inv)]
    return wins, items, q, k, v, p


def _attn_in_specs():
    q = pl.BlockSpec((T, HP), lambda p: (0, p))
    k = pl.BlockSpec((T, HP), lambda p: (0, N_HP + p))
    v = pl.BlockSpec((T, HP), lambda p: (0, 2 * N_HP + p))
    tb = pl.BlockSpec((2, N_DPAIR, GRID_W, HP), lambda p: (p, 0, 0, 0))
    return q, k, v, tb


_HEAD_SCRATCH = pltpu.VMEM((2, T, HEAD_DIM), BF16)


def _attn_fwd(z, tb):
    def body(q_ref, k_ref, v_ref, tb_ref, o_ref, qb_ref, kb_ref, vb_ref):
        _split_heads(q_ref, qb_ref, SCALE)
        _split_heads(k_ref, kb_ref)
        _split_heads(v_ref, vb_ref)

        def rows(it, carry):
            wins, items, _, _, v, p = _attn_items(qb_ref, kb_ref, vb_ref, tb_ref, it * ATT_UNROLL_FWD, ATT_UNROLL_FWD)
            o = [_dot(pi.astype(BF16), vi, NN) for pi, vi in zip(p, v)]
            for u, (q0, _, _) in enumerate(wins):
                o_ref[pl.ds(q0, GRID_W), :] = jnp.concatenate(o[2 * u:2 * u + 2], axis=1).astype(BF16)
            return carry

        lax.fori_loop(0, N_ROWS // ATT_UNROLL_FWD, rows, 0)

    blk = pl.BlockSpec((T, HP), lambda p: (0, p))
    return pl.pallas_call(
        body, name="attn_fwd", grid=(N_HP,), in_specs=list(_attn_in_specs()), out_specs=blk,
        out_shape=_sds((T, D_ATT), BF16), scratch_shapes=[_HEAD_SCRATCH] * 3,
        compiler_params=_params(("parallel",)))(z, z, z, tb)


def _attn_bwd(z, tb, d_att, after=()):
    def body(q_ref, k_ref, v_ref, tb_ref, do_ref, flip_ref, *rest):
        (dq_ref, dk_ref, dv_ref, diag_ref, qb_ref, kb_ref, vb_ref, dob_ref, dka_ref, dva_ref,
         ds_ref) = rest[len(after):]
        _split_heads(q_ref, qb_ref, SCALE)
        _split_heads(k_ref, kb_ref)
        _split_heads(v_ref, vb_ref)
        _split_heads(do_ref, dob_ref)
        dka_ref[...] = jnp.zeros_like(dka_ref)
        dva_ref[...] = jnp.zeros_like(dva_ref)
        ds_ref[...] = jnp.zeros_like(ds_ref)

        def rows(it, carry):
            wins, items, q, k, v, p = _attn_items(qb_ref, kb_ref, vb_ref, tb_ref, it * ATT_UNROLL_BWD, ATT_UNROLL_BWD)
            do = [dob_ref[hh, pl.ds(wins[u][0], GRID_W), :] for u, hh in items]
            dv = [_dot(pi.astype(BF16), di, TN) for pi, di in zip(p, do)]
            dp = [_dot(di, vi, NT) for di, vi in zip(do, v)]
            ds = [pi * (dpi - jnp.sum(dpi * pi, axis=-1, keepdims=True)) for pi, dpi in zip(p, dp)]
            dsb = [d.astype(BF16) for d in ds]
            dq = [_dot(d, ki, NN) * SCALE for d, ki in zip(dsb, k)]
            dk = [_dot(d, qi, TN) for d, qi in zip(dsb, q)]
            for d, (u, hh) in zip(ds, items):
                for ii in range(WIN_H // 2):
                    ds_ref[hh, wins[u][2] + 2 * ii] += d[:, ii * HP:(ii + 1) * HP]
            for u, (q0, k0, _) in enumerate(wins):
                dq_ref[pl.ds(q0, GRID_W), :] = jnp.concatenate(dq[2 * u:2 * u + 2], axis=1).astype(BF16)
                dka_ref[pl.ds(k0, KEYS), :] += jnp.concatenate(dk[2 * u:2 * u + 2], axis=1)
                dva_ref[pl.ds(k0, KEYS), :] += jnp.concatenate(dv[2 * u:2 * u + 2], axis=1)
            return carry

        lax.fori_loop(0, N_ROWS // ATT_UNROLL_BWD, rows, 0)
        dk_ref[...] = dka_ref[...].astype(BF16)
        dv_ref[...] = dva_ref[...].astype(BF16)
        _diag_sums(ds_ref, flip_ref, diag_ref)

    blk = pl.BlockSpec((T, HP), lambda p: (0, p))
    q, k, v, tbs = _attn_in_specs()
    flip = jnp.asarray(np.eye(HP, dtype=np.float32)[::-1], BF16)
    return pl.pallas_call(
        body, name="attn_bwd", grid=(N_HP,),
        in_specs=[q, k, v, tbs, blk, pl.BlockSpec((HP, HP), lambda p: (0, 0))] + [_ANY] * len(after),
        out_specs=[blk, blk, blk, pl.BlockSpec((None, DIAG_ROWS, HP), lambda p: (p, 0, 0))],
        out_shape=[_sds((T, D_ATT), BF16)] * 3 + [_sds((N_HP, DIAG_ROWS, HP), F32)],
        scratch_shapes=[_HEAD_SCRATCH] * 4 + [pltpu.VMEM((T, HP), F32), pltpu.VMEM((T, HP), F32),
                                              pltpu.VMEM((2, N_DPAIR, GRID_W, HP), F32)],
        compiler_params=_params(("parallel",)))(z, z, z, tb, d_att, flip, *after)


def _diag_sums(acc_ref, flip_ref, out_ref):
    flip = flip_ref[...]
    rows = []
    for hh in range(2):
        for pair in range(N_DPAIR):
            reversed_lanes = sum(_dot(part, flip, NN) for part in _split3(acc_ref[hh, pair]))
            skewed = pltpu.roll(reversed_lanes, 0, 1, stride=1, stride_axis=0)
            rows.append(jnp.sum(skewed, axis=0, keepdims=True))
    rows.append(jnp.zeros((DIAG_ROWS - len(rows), HP), F32))
    out_ref[...] = jnp.concatenate(rows, axis=0)


def _rpb_grad(diag_sums):
    g = diag_sums.reshape(N_HP * DIAG_ROWS, HP)
    sel = np.zeros((2, 128, N_HP * DIAG_ROWS), np.float32)
    lane = np.zeros((2, HP, 128), np.float32)
    for h in range(N_HEADS):
        for pair in range(N_DPAIR):
            for half in range(2):
                sel[half, h * N_DROW + pair + half, (h // 2) * DIAG_ROWS + (h % 2) * N_DPAIR + pair] = 1.0
    for j in range(2 * WIN_W - 1):
        for half in range(2):
            lane[half, (HP - 1 - GRID_W * half - (j - (WIN_W - 1))) % HP, j] = 1.0

    def body(g_ref, sel_ref, lane_ref, o_ref):
        parts = _split3(g_ref[...])
        total = None
        for half in range(2):
            picked = sum(_dot(sel_ref[half], part, NN) for part in parts)
            term = sum(_dot(part, lane_ref[half], NN) for part in _split3(picked))
            total = term if total is None else total + term
        o_ref[...] = total

    out = pl.pallas_call(body, name="rpb_grad", out_shape=_sds((128, 128), F32),
                         compiler_params=_params())(g, jnp.asarray(sel, BF16), jnp.asarray(lane, BF16))
    return out[:N_HEADS * N_DROW, :2 * WIN_W - 1].reshape(N_HEADS, N_DROW, 2 * WIN_W - 1)


N_CB = D // REC_CB
N_CHUNK = T // REC_CHUNK
N_TILE = T // 8
_U_BLK = 1536 // REC_CB
_Y_BLK = 2560 // REC_CB


def _block_diag(w):
    per = REC_CB // 64
    wt = w.reshape(2, N_CB, per, 64, 64)
    eye = jnp.eye(per, dtype=w.dtype)
    full = wt[:, :, :, :, None, :] * eye[None, None, :, None, :, None]
    return full.reshape(2, N_CB, REC_CB, REC_CB).astype(BF16)


def _block_diag_grad(g):
    per = REC_CB // 64
    g6 = g.reshape(2, N_CB, per, 64, per, 64)
    return jnp.stack([g6[:, :, p, :, p, :] for p in range(per)], axis=2).reshape(2, 16, 64, 64)


def _gelu(x):
    c = 0.7978845608028654
    return 0.5 * x * (1.0 + jnp.tanh(c * (x + 0.044715 * x * x * x)))


def _gelu_grad(x):
    c = 0.7978845608028654
    th = jnp.tanh(c * (x + 0.044715 * x * x * x))
    return 0.5 * (1.0 + th) + 0.5 * x * (1.0 - th * th) * c * (1.0 + 3.0 * 0.044715 * x * x)


def _softplus_neg(lam):
    x = -lam
    e = jnp.exp(-jnp.abs(x))
    w = 1.0 + e
    l1p = jnp.where(w == 1.0, e, jnp.log(w) * e / (w - 1.0))
    return jnp.maximum(x, 0.0) + l1p


def _one_minus_exp(x):
    poly = x * (1.0 + x * (1 / 2 + x * (1 / 6 + x * (1 / 24 + x * (1 / 120 + x * (1 / 720))))))
    return jnp.where(x > -0.125, -poly, 1.0 - jnp.exp(x))


def _conv_taps(pad_ref, t0, w, sign):
    out = None
    for j in range(4):
        term = w[j:j + 1, :] * pad_ref[pl.ds(PAD + t0 + sign * (j - 2), REC_CHUNK), :]
        out = term if out is None else out + term
    return out


def _gates(u, wa, wi, ba, bi, sp):
    ub = u.astype(BF16)
    r = _sigmoid(_dot(ub, wa, NN) + ba)
    i = _sigmoid(_dot(ub, wi, NN) + bi)
    log_a = -LRU_C * r * sp
    a = jnp.exp(log_a)
    x = jnp.maximum(_one_minus_exp(2.0 * log_a), 0.0)
    positive = x > 0.0
    inv = lax.rsqrt(jnp.where(positive, x, 1.0))
    mult = jnp.where(positive, x * inv, 0.0)
    return r, i, a, mult, jnp.where(positive, inv, 0.0)


def _tile_scan(a, b, sub, reverse):
    for s in (1, 2, 4):
        if reverse:
            a_s, b_s, m = pltpu.roll(a, 8 - s, 0), pltpu.roll(b, 8 - s, 0), sub < 8 - s
        else:
            a_s, b_s, m = pltpu.roll(a, s, 0), pltpu.roll(b, s, 0), sub >= s
        b = jnp.where(m, a * b_s + b, b)
        a = jnp.where(m, a * a_s, a)
    return a, b


def _last_row(x, row):
    return jnp.broadcast_to(x[row:row + 1, :], x.shape)


def _rec_prologue(up_ref, cw_ref, cb_ref, wa_ref, wi_ref, ba_ref, bi_ref, lam_ref,
                  upad_ref, u_ref, a_refs, h_refs):
    cb = up_ref.shape[1]
    zeros = jnp.zeros((PAD, cb), F32)
    upad_ref[pl.ds(0, PAD), :] = zeros
    upad_ref[pl.ds(PAD + T, PAD), :] = zeros
    upad_ref[pl.ds(PAD, T), :] = up_ref[...]
    cw = cw_ref[...]
    sp = _softplus_neg(lam_ref[...])
    for c in range(N_CHUNK):
        t0 = c * REC_CHUNK
        u = cb_ref[...] + _conv_taps(upad_ref, t0, cw, 1)
        u_ref[pl.ds(t0, REC_CHUNK), :] = u
        for d in range(2):
            _, i, a, mult, _ = _gates(u, wa_ref[d], wi_ref[d], ba_ref[d:d + 1, :], bi_ref[d:d + 1, :], sp[d:d + 1, :])
            a_refs[d][pl.ds(t0, REC_CHUNK), :] = a
            h_refs[d][pl.ds(t0, REC_CHUNK), :] = mult * (i * u)

    sub = lax.broadcasted_iota(jnp.int32, (8, cb), 0)

    def tile(k, carry):
        cf, cr = carry
        tf = pl.multiple_of(k * 8, 8)
        tr = pl.multiple_of((N_TILE - 1 - k) * 8, 8)
        af, bf = _tile_scan(a_refs[0][pl.ds(tf, 8), :], h_refs[0][pl.ds(tf, 8), :], sub, False)
        hf = af * cf + bf
        h_refs[0][pl.ds(tf, 8), :] = hf
        ar, br = _tile_scan(a_refs[1][pl.ds(tr, 8), :], h_refs[1][pl.ds(tr, 8), :], sub, True)
        hr = ar * cr + br
        h_refs[1][pl.ds(tr, 8), :] = hr
        return _last_row(af, 7) * cf + _last_row(bf, 7), _last_row(ar, 0) * cr + _last_row(br, 0)

    z8 = jnp.zeros((8, cb), F32)
    lax.fori_loop(0, N_TILE, tile, (z8, z8))
    return sp


def _rec_specs():
    up = pl.BlockSpec((T, REC_CB), lambda c: (0, _U_BLK + c))
    yb = pl.BlockSpec((T, REC_CB), lambda c: (0, _Y_BLK + c))
    cw = pl.BlockSpec((4, REC_CB), lambda c: (0, c))
    cbias = pl.BlockSpec((1, REC_CB), lambda c: (0, c))
    wbd = pl.BlockSpec((2, None, REC_CB, REC_CB), lambda c: (0, c, 0, 0))
    vec2 = pl.BlockSpec((2, REC_CB), lambda c: (0, c))
    col = pl.BlockSpec((T, REC_CB), lambda c: (0, c))
    return up, yb, cw, cbias, wbd, vec2, col


def _rec_fwd(z, conv_w, conv_b, wa, wi, ba, bi, lam):
    up, yb, cw, cbias, wbd, vec2, col = _rec_specs()

    def body(up_ref, yb_ref, cw_ref, cb_ref, wa_ref, wi_ref, ba_ref, bi_ref, lam_ref, g_ref,
             u_ref, af_ref, ar_ref, hf_ref, hr_ref, upad_ref):
        _rec_prologue(up_ref, cw_ref, cb_ref, wa_ref, wi_ref, ba_ref, bi_ref, lam_ref,
                      upad_ref, u_ref, (af_ref, ar_ref), (hf_ref, hr_ref))

        def chunk(c, carry):
            t0 = pl.multiple_of(c * REC_CHUNK, REC_CHUNK)
            rows = pl.ds(t0, REC_CHUNK)
            g_ref[rows, :] = ((hf_ref[rows, :] + hr_ref[rows, :]) * _gelu(yb_ref[rows, :])).astype(BF16)
            return carry

        lax.fori_loop(0, N_CHUNK, chunk, 0)

    res = pl.pallas_call(
        body, name="rec_fwd", grid=(N_CB,),
        in_specs=[up, yb, cw, cbias, wbd, wbd, vec2, vec2, vec2], out_specs=[col] * 6,
        out_shape=[_sds((T, D), BF16)] + [_sds((T, D), F32)] * 5,
        scratch_shapes=[pltpu.VMEM((T + 2 * PAD, REC_CB), F32)],
        compiler_params=_params(("parallel",)))(z, z, conv_w, conv_b, wa, wi, ba, bi, lam)
    return res[0], tuple(res[1:])


def _rec_bwd(z, dg, saved, conv_w, conv_b, wa, wi, ba, bi, lam, after=()):
    up, yb, cw, cbias, wbd, vec2, col = _rec_specs()

    def body(up_ref, yb_ref, dg_ref, u_ref, af_ref, ar_ref, hf_ref, hr_ref,
             cw_ref, cb_ref, wa_ref, wi_ref, ba_ref, bi_ref, lam_ref, *rest):
        (dup_ref, dyb_ref, dcw_ref, dcb_ref, dwa_ref, dwi_ref, dba_ref, dbi_ref, dlam_ref,
         upad_ref, dh_ref, gf_ref, gr_ref, daf_ref, dar_ref, dupad_ref) = rest[len(after):]
        g_refs, da_refs = (gf_ref, gr_ref), (daf_ref, dar_ref)
        cb = up_ref.shape[1]
        zeros = jnp.zeros((PAD, cb), F32)
        upad_ref[pl.ds(0, PAD), :] = zeros
        upad_ref[pl.ds(PAD + T, PAD), :] = zeros
        upad_ref[pl.ds(PAD, T), :] = up_ref[...]
        sp = _softplus_neg(lam_ref[...])

        def gate_chunk(c, carry):
            t0 = pl.multiple_of(c * REC_CHUNK, REC_CHUNK)
            rows = pl.ds(t0, REC_CHUNK)
            y = yb_ref[rows, :]
            dgv = dg_ref[rows, :].astype(F32)
            dh_ref[rows, :] = dgv * _gelu(y)
            dyb_ref[rows, :] = (dgv * (hf_ref[rows, :] + hr_ref[rows, :]) * _gelu_grad(y)).astype(BF16)
            return carry

        lax.fori_loop(0, N_CHUNK, gate_chunk, 0)

        sub = lax.broadcasted_iota(jnp.int32, (8, cb), 0)

        def tile(k, carry):
            cf, cr = carry
            kf = N_TILE - 1 - k
            tf = pl.multiple_of(kf * 8, 8)
            tnext = pl.multiple_of(jnp.minimum(kf + 1, N_TILE - 1) * 8, 8)
            tprev = pl.multiple_of(jnp.maximum(kf - 1, 0) * 8, 8)
            a_t = af_ref[pl.ds(tf, 8), :]
            a_n = jnp.where(kf < N_TILE - 1, af_ref[pl.ds(tnext, 8), :], 0.0)
            a_sh = jnp.where(sub == 7, pltpu.roll(a_n, 7, 0), pltpu.roll(a_t, 7, 0))
            ca, cbb = _tile_scan(a_sh, dh_ref[pl.ds(tf, 8), :], sub, True)
            gf = ca * cf + cbb
            h_t = hf_ref[pl.ds(tf, 8), :]
            h_p = jnp.where(kf > 0, hf_ref[pl.ds(tprev, 8), :], 0.0)
            h_sh = jnp.where(sub == 0, pltpu.roll(h_p, 1, 0), pltpu.roll(h_t, 1, 0))
            gf_ref[pl.ds(tf, 8), :] = gf
            daf_ref[pl.ds(tf, 8), :] = gf * h_sh
            tr = pl.multiple_of(k * 8, 8)
            rnext = pl.multiple_of(jnp.minimum(k + 1, N_TILE - 1) * 8, 8)
            rprev = pl.multiple_of(jnp.maximum(k - 1, 0) * 8, 8)
            b_t = ar_ref[pl.ds(tr, 8), :]
            b_p = jnp.where(k > 0, ar_ref[pl.ds(rprev, 8), :], 0.0)
            b_sh = jnp.where(sub == 0, pltpu.roll(b_p, 1, 0), pltpu.roll(b_t, 1, 0))
            ra, rb = _tile_scan(b_sh, dh_ref[pl.ds(tr, 8), :], sub, False)
            gr = ra * cr + rb
            hr_t = hr_ref[pl.ds(tr, 8), :]
            hr_n = jnp.where(k < N_TILE - 1, hr_ref[pl.ds(rnext, 8), :], 0.0)
            hr_sh = jnp.where(sub == 7, pltpu.roll(hr_n, 7, 0), pltpu.roll(hr_t, 7, 0))
            gr_ref[pl.ds(tr, 8), :] = gr
            dar_ref[pl.ds(tr, 8), :] = gr * hr_sh
            return _last_row(gf, 0), _last_row(gr, 7)

        z8 = jnp.zeros((8, cb), F32)
        lax.fori_loop(0, N_TILE, tile, (z8, z8))

        dupad_ref[pl.ds(0, PAD), :] = zeros
        dupad_ref[pl.ds(PAD + T, PAD), :] = zeros
        dwa_ref[...] = jnp.zeros_like(dwa_ref)
        dwi_ref[...] = jnp.zeros_like(dwi_ref)
        dba_ref[...] = jnp.zeros_like(dba_ref)
        dbi_ref[...] = jnp.zeros_like(dbi_ref)
        dlam_ref[...] = jnp.zeros_like(dlam_ref)

        def grad_chunk(c, carry):
            t0 = pl.multiple_of(c * REC_CHUNK, REC_CHUNK)
            rows = pl.ds(t0, REC_CHUNK)
            u = u_ref[rows, :]
            ub = u.astype(BF16)
            du = jnp.zeros((REC_CHUNK, cb), F32)
            for d in range(2):
                r, i, a, mult, inv_mult = _gates(u, wa_ref[d], wi_ref[d], ba_ref[d:d + 1, :], bi_ref[d:d + 1, :],
                                                 sp[d:d + 1, :])
                dbx = g_refs[d][rows, :]
                dmult = dbx * (i * u)
                diu = dbx * mult
                a2 = a * a
                dlog = da_refs[d][rows, :] * a - dmult * (a2 * inv_mult)
                dpa = (dlog * (-LRU_C) * sp[d:d + 1, :]) * r * (1.0 - r)
                dpi = (diu * u) * i * (1.0 - i)
                dpab, dpib = dpa.astype(BF16), dpi.astype(BF16)
                du = du + diu * i + _dot(dpab, wa_ref[d], NT) + _dot(dpib, wi_ref[d], NT)
                dwa_ref[d] += _dot(ub, dpab, TN)
                dwi_ref[d] += _dot(ub, dpib, TN)
                dba_ref[d:d + 1, :] += jnp.sum(dpa, axis=0, keepdims=True)
                dbi_ref[d:d + 1, :] += jnp.sum(dpi, axis=0, keepdims=True)
                dlam_ref[d:d + 1, :] += jnp.sum(dlog * r, axis=0, keepdims=True)
            dupad_ref[pl.ds(PAD + t0, REC_CHUNK), :] = du
            return carry

        lax.fori_loop(0, N_CHUNK, grad_chunk, 0)
        dlam_ref[...] = dlam_ref[...] * (LRU_C * _sigmoid(-lam_ref[...]))

        cw = cw_ref[...]
        dcb = jnp.zeros((1, cb), F32)
        dcw = [jnp.zeros((1, cb), F32) for _ in range(4)]
        for c in range(N_CHUNK):
            t0 = c * REC_CHUNK
            du = dupad_ref[pl.ds(PAD + t0, REC_CHUNK), :]
            dcb = dcb + jnp.sum(du, axis=0, keepdims=True)
            for j in range(4):
                dcw[j] = dcw[j] + jnp.sum(du * upad_ref[pl.ds(PAD + t0 + j - 2, REC_CHUNK), :], axis=0, keepdims=True)
            dup_ref[pl.ds(t0, REC_CHUNK), :] = _conv_taps(dupad_ref, t0, cw, -1).astype(BF16)
        dcb_ref[...] = dcb
        dcw_ref[...] = jnp.concatenate(dcw, axis=0)

    full = pltpu.VMEM((T, REC_CB), F32)
    padded = pltpu.VMEM((T + 2 * PAD, REC_CB), F32)
    return pl.pallas_call(
        body, name="rec_bwd", grid=(N_CB,),
        in_specs=[up, yb] + [col] * 6 + [cw, cbias, wbd, wbd, vec2, vec2, vec2] + [_ANY] * len(after),
        out_specs=[col, col, cw, cbias, wbd, wbd, vec2, vec2, vec2],
        out_shape=[_sds((T, D), BF16), _sds((T, D), BF16), _sds((4, D), F32), _sds((1, D), F32),
                   _sds((2, N_CB, REC_CB, REC_CB), F32), _sds((2, N_CB, REC_CB, REC_CB), F32),
                   _sds((2, D), F32), _sds((2, D), F32), _sds((2, D), F32)],
        scratch_shapes=[padded, full, full, full, full, full, padded],
        compiler_params=_params(("parallel",)))(z, z, dg, *saved, conv_w, conv_b, wa, wi, ba, bi, lam, *after)


class _NoReducer:
    def begin(self, tag, grads):
        return ()

    def advance(self, tag, after):
        return ()


def _local_step(x, target, p, late=None, reducer=_NoReducer()):
    x = x.reshape(T, D)
    target = target.reshape(T, D)
    tb = _bias_pairs(p["rpb"])
    wa, wi = _block_diag(p["w_rg_a"]), _block_diag(p["w_rg_i"])

    h1 = _rms_fwd("rms1_fwd", x, p["ln1_g"], after=late[0] if late else ())
    if late:
        p = {**p, **late[1]((h1, tb, wa, wi))}
    rec_params = (p["conv_w"], p["conv_b"], wa, wi, p["b_rg_a"], p["b_rg_i"], p["lru_lambda"])
    (z,) = _mm_nn_cols("mm_z", h1, p["w_in"], F32, bias=p["b_in"])
    att = _attn_fwd(z, tb)
    g, rec_saved = _rec_fwd(z, *rec_params)
    if late:
        p = {**p, **late[2](g)}
    y_att, y_rec, mixed, x1, h2 = _branches_fwd(att, g, z, x, p["w_att_o"], p["w_rec_o"], p["w_out"], p["ln2_g"])

    def relu2(r, ex, outs):
        rp = jnp.maximum(r, 0.0)
        outs[0][...] = (rp * rp).astype(BF16)

    (s,) = _mm_nn_cols("mm_ff1", h2, p["w_ff1"], BF16, epilogue=relu2)
    loss, dx2, dx2_b, g_lnf = _mm_x2_loss_head(s, p["w_ff2"], x1, target, p["lnf_g"])

    def relu2_bwd(r, ex, outs):
        outs[0][...] = (r * 2.0 * jnp.sqrt(ex[0][...].astype(F32))).astype(BF16)

    (df,) = _mm_nt_rows("mm_df", dx2_b, p["w_ff2"], BF16, tn=D, extras=[s],
                        extra_specs=[pl.BlockSpec((TJ, D), lambda j, i, k: (i, j))], epilogue=relu2_bwd)
    (g_w_ff2,) = _mm_tn_rows("mm_g_ff2", s, dx2_b, tm=D)
    (g_w_ff1,) = _mm_tn_cols("mm_g_ff1", h2, df, D)
    tok = reducer.begin("ff", dict(w_ff2=g_w_ff2, w_ff1=g_w_ff1))
    dx1, dx1_b, g_ln2 = _mm_nt_cols_rms_bwd("mm_dh2_rms2_bwd", df, p["w_ff1"], x1, p["ln2_g"], dx2, after=tok,
                                            bf16_copy=True)

    dy_att, dy_rec, dg_att, dg_rec, d_att, d_g = _branches_bwd(dx1_b, y_att, y_rec, z, p["w_att_o"], p["w_rec_o"],
                                                               p["w_out"])
    (g_w_out,) = _mm_tn_rows("mm_g_out", mixed, dx1_b, tm=D)
    (g_w_att_o,) = _mm_tn_cols("mm_g_att_o", att, dy_att, D // N_CHIPS)
    (g_w_rec_o,) = _mm_tn_rows("mm_g_rec_o", g, dy_rec, tm=D)
    tok = reducer.advance("ff", g_w_rec_o) + reducer.begin("proj", dict(w_out=g_w_out, w_att_o=g_w_att_o, w_rec_o=g_w_rec_o))

    dq, dk, dv, ds_acc = _attn_bwd(z, tb, d_att, after=tok)
    g_rpb = _rpb_grad(ds_acc)
    tok = reducer.advance("proj", dq)
    d_up, d_yb, g_conv_w, g_conv_b, g_wa, g_wi, g_ba, g_bi, g_lam = _rec_bwd(z, d_g, rec_saved, *rec_params, after=tok)
    dz = jnp.concatenate([dq, dk, dv, d_up, d_yb, dg_att, dg_rec], axis=1)

    g_w_in, g_b_in = _mm_tn_cols("mm_g_in", h1, dz, D_IN // N_CHIPS, colsum=True)
    tok = reducer.begin("in", dict(w_in=g_w_in))
    grad_x, g_ln1 = _mm_nt_cols_rms_bwd("mm_dh1_rms1_bwd", dz, p["w_in"], x, p["ln1_g"], dx1, after=tok)
    reducer.advance("in", grad_x)

    grads = dict(ln1_g=g_ln1, w_in=g_w_in, b_in=g_b_in, rpb=g_rpb, w_att_o=g_w_att_o, conv_w=g_conv_w,
                 conv_b=g_conv_b, w_rg_a=_block_diag_grad(g_wa), b_rg_a=g_ba, w_rg_i=_block_diag_grad(g_wi),
                 b_rg_i=g_bi, lru_lambda=g_lam, w_rec_o=g_w_rec_o, w_out=g_w_out, ln2_g=g_ln2,
                 w_ff1=g_w_ff1, w_ff2=g_w_ff2, lnf_g=g_lnf)
    return loss, grad_x.reshape(1, T, D), grads


_ANY = pl.BlockSpec(memory_space=pl.ANY)
N_PEERS = N_CHIPS - 1


def _place():
    x, y, c = lax.axis_index("x"), lax.axis_index("y"), lax.axis_index("c")
    peers = [(1 - x, y), (x, 1 - y), (1 - x, 1 - y)]
    return x, y, c, 2 * x + y, peers


def _remote(src, dst, send_sem, recv_sem, dev):
    return pltpu.make_async_remote_copy(src_ref=src, dst_ref=dst, send_sem=send_sem, recv_sem=recv_sem,
                                        device_id=dev, device_id_type=MESH)


def _prefetch_call(body, name, ids, grid, in_specs, out_specs, out_shape, args, semantics=None):
    spec = pltpu.PrefetchScalarGridSpec(num_scalar_prefetch=1, grid=grid, in_specs=in_specs, out_specs=out_specs)
    return pl.pallas_call(body, name=name, grid_spec=spec, out_shape=out_shape,
                          compiler_params=_params(semantics or ("parallel",) * len(grid)))(ids, *args)


def _cast_bf16(name, w, chip_id, after=()):
    rows, cols = w.shape
    rb = min(rows, 256)

    def body(ids_ref, w_ref, *rest):
        rest[-1][...] = w_ref[...].astype(BF16)

    return _prefetch_call(body, name, chip_id, (rows // rb,),
                          [pl.BlockSpec((rb, cols), lambda i, ids: (i, 0))] + [_ANY] * len(after),
                          pl.BlockSpec((None, rb, cols), lambda i, ids: (ids[0], i, 0)),
                          _sds((N_CHIPS, rows, cols), BF16), (w, *after))


def _dma_sems(*counts):
    return [pltpu.SemaphoreType.DMA((k,)) for k in counts]


_HBM = pl.BlockSpec(memory_space=pltpu.HBM)
_SEM = pl.BlockSpec(memory_space=pltpu.SEMAPHORE)
_SPLIT_COPY = pltpu.CompilerParams(has_side_effects=pltpu.SideEffectType.DATAFLOW_SIDE_EFFECTING)
SIBLING_ID = 0
_SPLIT_COPY_SIBLING = pltpu.CompilerParams(has_side_effects=pltpu.SideEffectType.DATAFLOW_SIDE_EFFECTING,
                                           collective_id=SIBLING_ID)


def _sibling_handshake():
    x, y, c = lax.axis_index("x"), lax.axis_index("y"), lax.axis_index("c")
    barrier = pltpu.get_barrier_semaphore()
    pl.semaphore_signal(barrier, inc=1, device_id=(x, y, 1 - c), device_id_type=MESH)
    pl.semaphore_wait(barrier, 1)


def _hbm(arrays):
    return [pltpu.with_memory_space_constraint(a, pltpu.HBM) for a in arrays]


def _hbm_like(arrays):
    return [pltpu.HBM(a.shape, a.dtype) for a in arrays]


def _halves(buf, c):
    half = buf.shape[1] // 2
    return pl.ds(c * half, half), pl.ds((1 - c) * half, half)


def _gather_start(name, slots):
    n = len(slots)
    nk = n * N_PEERS

    def body(*refs):
        bufs = refs[n:2 * n]
        send_sems, recv_sems, token = refs[2 * n:]
        x, y, c, chip, peers = _place()
        for t in range(n):
            mine, _ = _halves(bufs[t], c)
            for r, (px, py) in enumerate(peers):
                k = t * N_PEERS + r
                own = bufs[t].at[chip, mine]
                _remote(own, own, send_sems.at[k], recv_sems.at[k], (px, py, c)).start()
        token[...] = jnp.zeros_like(token)

    res = pl.pallas_call(
        body, name=name, in_specs=[_HBM] * n, out_specs=[_HBM] * n + [_SEM, _SEM, pl.BlockSpec(memory_space=pltpu.VMEM)],
        out_shape=_hbm_like(slots) + [pltpu.SemaphoreType.DMA((nk,)), pltpu.SemaphoreType.DMA((nk,)),
                                      _sds((8, 128), F32)],
        input_output_aliases={t: t for t in range(n)}, compiler_params=_SPLIT_COPY)(*_hbm(slots))
    return res[:n], (res[n], res[n + 1]), res[n + 2]


def _gather_wait(name, bufs, sems, after):
    n = len(bufs)
    after = tuple(after) if isinstance(after, (tuple, list)) else (after,)

    def body(*refs):
        ins = refs[:n]
        send_sems, recv_sems = refs[n], refs[n + 1]
        x, y, c, chip, peers = _place()
        for t in range(n):
            mine, _ = _halves(ins[t], c)
            for r, (px, py) in enumerate(peers):
                k = t * N_PEERS + r
                cp = _remote(ins[t].at[chip, mine], ins[t].at[2 * px + py, mine], send_sems.at[k], recv_sems.at[k],
                             (px, py, c))
                cp.wait_send()
                cp.wait_recv()

    return pl.pallas_call(
        body, name=name, in_specs=[_HBM] * n + [_SEM, _SEM] + [_ANY] * len(after), out_specs=[_HBM] * n,
        out_shape=_hbm_like(bufs), input_output_aliases={t: t for t in range(n)},
        compiler_params=_SPLIT_COPY)(*bufs, *sems, *after)


def _gather_forward(name, bufs):
    n = len(bufs)
    nk = n * N_PEERS

    def body(*refs):
        _sibling_handshake()
        outs = refs[n:2 * n]
        send_sems, recv_sems = refs[2 * n:]
        x, y, c, chip, peers = _place()
        sibling = (x, y, 1 - c)
        sends = []
        for t in range(n):
            mine, _ = _halves(outs[t], c)
            for r, (px, py) in enumerate(peers):
                k = t * N_PEERS + r
                landed = outs[t].at[2 * px + py, mine]
                sends.append(_remote(landed, landed, send_sems.at[k], recv_sems.at[k], sibling))
                sends[-1].start()
        for t in range(n):
            _, theirs = _halves(outs[t], c)
            for r, (px, py) in enumerate(peers):
                k = t * N_PEERS + r
                landed = outs[t].at[2 * px + py, theirs]
                _remote(landed, landed, send_sems.at[k], recv_sems.at[k], sibling).wait_recv()
        for cp in sends:
            cp.wait_send()

    return pl.pallas_call(
        body, name=name, in_specs=[_ANY] * n, out_specs=[_ANY] * n, out_shape=[_sds(b.shape, b.dtype) for b in bufs],
        input_output_aliases={t: t for t in range(n)}, scratch_shapes=_dma_sems(nk, nk),
        compiler_params=pltpu.CompilerParams(collective_id=SIBLING_ID))(*bufs)


def _pair_copies(n, srcs, lands, send_sems, recv_sems):
    x, y, c, _, _ = _place()
    sibling = (x, y, 1 - c)
    copies = []
    for t in range(n):
        half = srcs[t].shape[1] // 2
        for j in range(N_CHIPS):
            k = t * N_CHIPS + j
            copies.append(_remote(srcs[t].at[j, pl.ds((1 - c) * half, half)], lands[t].at[j],
                                  send_sems.at[k], recv_sems.at[k], sibling))
    for t in range(n, len(srcs)):
        k = n * N_CHIPS + t - n
        copies.append(_remote(srcs[t], lands[t], send_sems.at[k], recv_sems.at[k], sibling))
    return copies


def _pair_start(name, grads, wholes=()):
    n = len(grads)
    srcs = list(grads) + list(wholes)
    m = len(srcs)
    lands = [pltpu.HBM((N_CHIPS, g.shape[1] // 2, g.shape[2]), F32) for g in grads] + _hbm_like(wholes)
    ns = n * N_CHIPS + len(wholes)

    def body(*refs):
        _sibling_handshake()
        src_refs, land_refs = refs[m:2 * m], refs[2 * m:3 * m]
        send_sems, recv_sems, token = refs[3 * m:]
        for cp in _pair_copies(n, src_refs, land_refs, send_sems, recv_sems):
            cp.start()
        token[...] = jnp.zeros_like(token)

    res = pl.pallas_call(
        body, name=name, in_specs=[_HBM] * m,
        out_specs=[_HBM] * (2 * m) + [_SEM, _SEM, pl.BlockSpec(memory_space=pltpu.VMEM)],
        out_shape=_hbm_like(srcs) + lands + [pltpu.SemaphoreType.DMA((ns,)), pltpu.SemaphoreType.DMA((ns,)),
                                             _sds((8, 128), F32)],
        input_output_aliases={t: t for t in range(m)}, compiler_params=_SPLIT_COPY_SIBLING)(*_hbm(srcs))
    return (res[:m], res[m:2 * m], (res[2 * m], res[2 * m + 1])), res[2 * m + 2]


def _pair_wait(name, flight, n, after):
    srcs, lands, sems = flight
    m = len(srcs)

    def body(*refs):
        for cp in _pair_copies(n, refs[:m], refs[m:2 * m], refs[2 * m], refs[2 * m + 1]):
            cp.wait_send()
            cp.wait_recv()

    res = pl.pallas_call(
        body, name=name, in_specs=[_HBM] * (2 * m) + [_SEM, _SEM, _ANY], out_specs=[_HBM] * (2 * m),
        out_shape=_hbm_like(srcs) + _hbm_like(lands), input_output_aliases={t: t for t in range(2 * m)},
        compiler_params=_SPLIT_COPY)(*srcs, *lands, *sems, after)
    return res[:m], res[m:]


def _chip_copies(srcs, lands, small_src, small_land, send_sems, recv_sems):
    x, y, c, chip, peers = _place()
    n = len(srcs)
    copies = []
    for r, (px, py) in enumerate(peers):
        for t in range(n):
            k = t * N_PEERS + r
            copies.append(_remote(srcs[t].at[2 * px + py], lands[t].at[r], send_sems.at[k], recv_sems.at[k], (px, py, c)))
        if small_src is not None:
            k = n * N_PEERS + r
            half_s = small_src.shape[0] // 2
            copies.append(_remote(small_src.at[pl.ds(c * half_s, half_s)], small_land.at[r],
                                  send_sems.at[k], recv_sems.at[k], (px, py, c)))
    return copies


def _chip_start(name, sums_bf16, small=None):
    n = len(sums_bf16)
    srcs = list(sums_bf16) + ([small] if small is not None else [])
    m = len(srcs)
    lands = [pltpu.HBM((N_PEERS,) + s.shape[1:], BF16) for s in sums_bf16]
    if small is not None:
        lands.append(pltpu.HBM((N_PEERS, small.shape[0] // 2, 128), F32))
    nk = m * N_PEERS

    def body(*refs):
        src_refs, land_refs = refs[m:2 * m], refs[2 * m:3 * m]
        send_sems, recv_sems, token = refs[3 * m:]
        small_src, small_land = (src_refs[n], land_refs[n]) if small is not None else (None, None)
        for cp in _chip_copies(src_refs[:n], land_refs[:n], small_src, small_land, send_sems, recv_sems):
            cp.start()
        token[...] = jnp.zeros_like(token)

    res = pl.pallas_call(
        body, name=name, in_specs=[_HBM] * m,
        out_specs=[_HBM] * (2 * m) + [_SEM, _SEM, pl.BlockSpec(memory_space=pltpu.VMEM)],
        out_shape=_hbm_like(srcs) + lands + [pltpu.SemaphoreType.DMA((nk,)), pltpu.SemaphoreType.DMA((nk,)),
                                             _sds((8, 128), F32)],
        input_output_aliases={t: t for t in range(m)}, compiler_params=_SPLIT_COPY)(*_hbm(srcs))
    return (res[:m], res[m:2 * m], (res[2 * m], res[2 * m + 1])), res[2 * m + 2]


def _chip_wait(name, flight, with_small, after):
    srcs, lands, sems = flight
    m = len(srcs)
    n = m - 1 if with_small else m

    def body(*refs):
        src_refs, land_refs = refs[:m], refs[m:2 * m]
        send_sems, recv_sems = refs[2 * m], refs[2 * m + 1]
        small_src, small_land = (src_refs[n], land_refs[n]) if with_small else (None, None)
        for cp in _chip_copies(src_refs[:n], land_refs[:n], small_src, small_land, send_sems, recv_sems):
            cp.wait_send()
            cp.wait_recv()

    res = pl.pallas_call(
        body, name=name, in_specs=[_HBM] * (2 * m) + [_SEM, _SEM, _ANY], out_specs=[_HBM] * (2 * m),
        out_shape=_hbm_like(srcs) + _hbm_like(lands), input_output_aliases={t: t for t in range(2 * m)},
        compiler_params=_SPLIT_COPY)(*srcs, *lands, *sems, after)
    return res[:m], res[m:]


def _swap_start(name, bufs):
    n = len(bufs)

    def body(*refs):
        _sibling_handshake()
        outs = refs[n:2 * n]
        send_sems, recv_sems, token = refs[2 * n:]
        x, y, c, _, _ = _place()
        for t in range(n):
            h = outs[t].shape[0] // 2
            mine = outs[t].at[pl.ds(c * h, h)]
            _remote(mine, mine, send_sems.at[t], recv_sems.at[t], (x, y, 1 - c)).start()
        token[...] = jnp.zeros_like(token)

    res = pl.pallas_call(
        body, name=name, in_specs=[_HBM] * n, out_specs=[_HBM] * n + [_SEM, _SEM, pl.BlockSpec(memory_space=pltpu.VMEM)],
        out_shape=_hbm_like(bufs) + [pltpu.SemaphoreType.DMA((n,)), pltpu.SemaphoreType.DMA((n,)), _sds((8, 128), F32)],
        input_output_aliases={t: t for t in range(n)}, compiler_params=_SPLIT_COPY_SIBLING)(*_hbm(bufs))
    return (res[:n], (res[n], res[n + 1])), res[n + 2]


def _swap_wait(name, flight, after):
    bufs, sems = flight
    n = len(bufs)

    def body(*refs):
        ins = refs[:n]
        send_sems, recv_sems = refs[n], refs[n + 1]
        x, y, c, _, _ = _place()
        for t in range(n):
            h = ins[t].shape[0] // 2
            cp = _remote(ins[t].at[pl.ds(c * h, h)], ins[t].at[pl.ds((1 - c) * h, h)], send_sems.at[t],
                         recv_sems.at[t], (x, y, 1 - c))
            cp.wait_send()
            cp.wait_recv()

    return pl.pallas_call(
        body, name=name, in_specs=[_HBM] * n + [_SEM, _SEM, _ANY], out_specs=[_HBM] * n, out_shape=_hbm_like(bufs),
        input_output_aliases={t: t for t in range(n)}, compiler_params=_SPLIT_COPY)(*bufs, *sems, after)


def _pair_sum(name, grad, got, ids):
    _, rows, cols = got.shape
    rb = min(rows, 256)
    nb = rows // rb
    blk = pl.BlockSpec((None, rb, cols), lambda i, j, ids: (j, i, 0))
    mine = pl.BlockSpec((None, rb, cols), lambda i, j, ids: (j, ids[1] * nb + i, 0))
    own = pl.BlockSpec((rb, cols), lambda i, j, ids: (i, 0))

    def body(ids_ref, a_ref, b_ref, s_ref, sb_ref):
        s = a_ref[...] + b_ref[...]
        sb_ref[...] = s.astype(BF16)

        @pl.when(pl.program_id(1) == ids_ref[0])
        def _():
            s_ref[...] = s

    return _prefetch_call(body, name, ids, (nb, N_CHIPS), [mine, blk], [own, blk],
                          [_sds((rows, cols), F32), _sds(got.shape, BF16)], (grad, got),
                          semantics=("parallel", "arbitrary"))


def _chip_sum(name, own_sum, got, ids):
    rows, cols = own_sum.shape
    rb = min(rows, 256)
    nb = rows // rb
    own = pl.BlockSpec((rb, cols), lambda i, ids: (i, 0))
    blk3 = pl.BlockSpec((N_PEERS, rb, cols), lambda i, ids: (0, i, 0))
    out = pl.BlockSpec((rb, cols), lambda i, ids: (ids[1] * nb + i, 0))

    def body(ids_ref, a_ref, b_ref, o_ref):
        o_ref[...] = ((a_ref[...] + b_ref[0].astype(F32)) + b_ref[1].astype(F32)) + b_ref[2].astype(F32)

    return _prefetch_call(body, name, ids, (nb,), [own, blk3], out, _sds((2 * rows, cols), F32), (own_sum, got))


SMALL_RB = 280


def _small_pair_sum(own, got):
    blk = pl.BlockSpec((SMALL_RB, 128), lambda i: (i, 0))

    def body(a_ref, b_ref, o_ref):
        o_ref[...] = a_ref[...] + b_ref[...]

    return pl.pallas_call(body, name="small_pair_sum", grid=(own.shape[0] // SMALL_RB,), in_specs=[blk, blk],
                          out_specs=blk, out_shape=_sds(own.shape, F32),
                          compiler_params=_params(("parallel",)))(own, got)


def _small_chip_sum(pair, got, ids):
    nb = pair.shape[0] // 2 // SMALL_RB
    half = pl.BlockSpec((SMALL_RB, 128), lambda i, ids: (ids[1] * nb + i, 0))
    blk3 = pl.BlockSpec((N_PEERS, SMALL_RB, 128), lambda i, ids: (0, i, 0))

    def body(ids_ref, a_ref, b_ref, o_ref):
        o_ref[...] = (a_ref[...] + b_ref[1]) + (b_ref[0] + b_ref[2])

    return _prefetch_call(body, "small_chip_sum", ids, (nb,), [half, blk3], half, _sds(pair.shape, F32), (pair, got))


def _adamw_math(w, g, m, v):
    m = ADAM_B1 * m + (1.0 - ADAM_B1) * g
    v = ADAM_B2 * v + (1.0 - ADAM_B2) * (g * g)
    m_hat = m / (1.0 - ADAM_B1 ** ADAM_STEP)
    v_hat = v / (1.0 - ADAM_B2 ** ADAM_STEP)
    delta = -ADAM_LR * (m_hat / (jnp.sqrt(v_hat) + ADAM_EPS) + ADAM_WD * w)
    return delta, m, v


def _adamw(name, w, g, m, v, rb=None):
    rows, cols = w.shape
    rb = rows if rb is None else rb
    blk = pl.BlockSpec((rb, cols), lambda i: (i, 0))

    def body(w_ref, g_ref, m_ref, v_ref, d_ref, nm_ref, nv_ref):
        d, nm, nv = _adamw_math(w_ref[...], g_ref[...], m_ref[...], v_ref[...])
        d_ref[...] = d
        nm_ref[...] = nm
        nv_ref[...] = nv

    return pl.pallas_call(body, name=name, grid=(rows // rb,), in_specs=[blk] * 4, out_specs=[blk] * 3,
                          out_shape=[_sds(w.shape, F32)] * 3, compiler_params=_params(("parallel",)))(w, g, m, v)


def _adamw_small(ws, gs, ms, vs):
    n = len(ws)

    def body(*refs):
        for t in range(n):
            w_ref, g_ref, m_ref, v_ref = (refs[k * n + t] for k in range(4))
            d, nm, nv = _adamw_math(w_ref[...], g_ref[...], m_ref[...], v_ref[...])
            for k, val in enumerate((d, nm, nv)):
                refs[(4 + k) * n + t][...] = val

    res = pl.pallas_call(body, name="adamw_small", out_shape=[_sds(a.shape, F32) for a in ws] * 3,
                         compiler_params=_params())(*ws, *gs, *ms, *vs)
    return [(res[t], res[n + t], res[2 * n + t]) for t in range(n)]


BIG = ("w_in", "w_att_o", "w_rec_o", "w_out", "w_ff1", "w_ff2")
SHARDED_VECS = ("conv_w", "b_rg_a", "b_rg_i", "lru_lambda")
SMALL = ("ln1_g", "b_in", "rpb", "conv_w", "conv_b", "w_rg_a", "b_rg_a", "w_rg_i", "b_rg_i", "lru_lambda",
         "ln2_g", "lnf_g")
SMALL_ROWS = 2240
ORDER = ("ln1_g", "w_in", "b_in", "rpb", "w_att_o", "conv_w", "conv_b", "w_rg_a", "b_rg_a", "w_rg_i", "b_rg_i",
         "lru_lambda", "w_rec_o", "w_out", "ln2_g", "w_ff1", "w_ff2", "lnf_g")


def _pack_small(grads, loss):
    parts, sizes = [], {}
    for n in SMALL:
        flat = grads[n].reshape(-1)
        pad = (-flat.shape[0]) % 128
        sizes[n] = (flat.shape[0], flat.shape[0] + pad)
        parts.append(jnp.pad(flat, (0, pad)))
    total = sum(s[1] for s in sizes.values())
    parts.append(jnp.pad(loss.reshape(1), (0, SMALL_ROWS * 128 - total - 1)))
    return jnp.concatenate(parts).reshape(SMALL_ROWS, 128), sizes


def _unpack_small(buf, sizes, shapes):
    flat = buf.reshape(-1)
    out, pos = {}, 0
    for n in SMALL:
        size, padded = sizes[n]
        out[n] = flat[pos:pos + size].reshape(shapes[n])
        pos += padded
    return out, flat[pos]


def _gather_weights(w, chip):
    chip_id = chip.astype(jnp.int32).reshape(1)
    vec_rows = [w[n][0] for n in SHARDED_VECS]
    vec_shard = jnp.concatenate(vec_rows + [jnp.zeros((16 - 10, D // N_CHIPS), F32)], axis=0)
    vec_slots = lax.dynamic_update_slice(jnp.zeros((N_CHIPS, 16, D // N_CHIPS), F32), vec_shard[None], (chip, 0, 0))
    bufs_a, sems_a, token_a = _gather_start("gather_start_first", [_cast_bf16("cast_w_in", w["w_in"][0], chip_id), vec_slots])
    rest_names = BIG[1:]
    bufs_b, sems_b, token_b = _gather_start(
        "gather_start_rest", [_cast_bf16("cast_" + n, w[n][0], chip_id, after=(token_a,)) for n in rest_names])

    def first(after):
        w_in_full, vec_full = _gather_forward("gather_forward_first", _gather_wait("gather_wait_first", bufs_a, sems_a, after))
        vecs = vec_full.transpose(1, 0, 2).reshape(16, D)
        return dict(w_in=w_in_full, conv_w=vecs[0:4], b_rg_a=vecs[4:6], b_rg_i=vecs[6:8], lru_lambda=vecs[8:10])

    def rest(after):
        full = dict(zip(rest_names, _gather_forward("gather_forward_rest",
                                                    _gather_wait("gather_wait_rest", bufs_b, sems_b, after))))
        return dict(w_att_o=full["w_att_o"], w_ff1=full["w_ff1"], w_rec_o=full["w_rec_o"].reshape(D, D),
                    w_out=full["w_out"].reshape(D, D), w_ff2=full["w_ff2"].reshape(D_FF, D))

    p = dict(ln1_g=w["ln1_g"], b_in=w["b_in"], rpb=w["rpb"][0], conv_b=w["conv_b"], w_rg_a=w["w_rg_a"][0],
             w_rg_i=w["w_rg_i"][0], ln2_g=w["ln2_g"], lnf_g=w["lnf_g"].reshape(1, D))
    return p, ((token_b,), first, rest)


class _Reducer:
    def __init__(self, ids):
        self.ids = ids
        self.groups = {}

    def begin(self, tag, grads, small=None):
        names = list(grads)
        big = [grads[n].reshape(N_CHIPS, -1, grads[n].shape[-1]) for n in names]
        flight, token = _pair_start("pair_start_" + tag, big, [] if small is None else [small])
        self.groups[tag] = dict(names=names, pair=flight, small=small is not None)
        return (token,)

    def advance(self, tag, after):
        grp = self.groups[tag]
        n = len(grp["names"])
        mine, got = _pair_wait("pair_wait_" + tag, grp["pair"], n, after)
        sums = [_pair_sum("pair_sum_" + name, a, b, self.ids) for name, a, b in zip(grp["names"], mine, got)]
        small_sum = _small_pair_sum(mine[n], got[n]) if grp["small"] else None
        grp["chip"], token = _chip_start("chip_start_" + tag, [s[1] for s in sums], small_sum)
        grp["sums"] = [s[0] for s in sums]
        self.last_token = token
        return (token,)

    def finish(self, tag, after):
        grp = self.groups[tag]
        srcs, lands = _chip_wait("chip_wait_" + tag, grp["chip"], grp["small"], after)
        halves = [_chip_sum("chip_sum_" + name, s, b, self.ids) for name, s, b in zip(grp["names"], grp["sums"], lands)]
        if grp["small"]:
            halves.append(_small_chip_sum(srcs[-1], lands[-1], self.ids))
        grp["swap"], token = _swap_start("swap_start_" + tag, halves)
        return token

    def result(self, tag, after):
        return _swap_wait("swap_wait_" + tag, self.groups[tag]["swap"], after)


def kernel(x, ln1_g, w_in, b_in, rpb, w_att_o, conv_w, conv_b, w_rg_a, b_rg_a, w_rg_i, b_rg_i, lru_lambda, w_rec_o, w_out, ln2_g, w_ff1, w_ff2, lnf_g, loss_target, m_ln1_g, m_w_in, m_b_in, m_rpb, m_w_att_o, m_conv_w, m_conv_b, m_w_rg_a, m_b_rg_a, m_w_rg_i, m_b_rg_i, m_lru_lambda, m_w_rec_o, m_w_out, m_ln2_g, m_w_ff1, m_w_ff2, m_lnf_g, v_ln1_g, v_w_in, v_b_in, v_rpb, v_w_att_o, v_conv_w, v_conv_b, v_w_rg_a, v_b_rg_a, v_w_rg_i, v_b_rg_i, v_lru_lambda, v_w_rec_o, v_w_out, v_ln2_g, v_w_ff1, v_w_ff2, v_lnf_g):
    w = dict(ln1_g=ln1_g, w_in=w_in, b_in=b_in, rpb=rpb, w_att_o=w_att_o, conv_w=conv_w, conv_b=conv_b,
             w_rg_a=w_rg_a, b_rg_a=b_rg_a, w_rg_i=w_rg_i, b_rg_i=b_rg_i, lru_lambda=lru_lambda, w_rec_o=w_rec_o,
             w_out=w_out, ln2_g=ln2_g, w_ff1=w_ff1, w_ff2=w_ff2, lnf_g=lnf_g)
    m = dict(ln1_g=m_ln1_g, w_in=m_w_in, b_in=m_b_in, rpb=m_rpb, w_att_o=m_w_att_o, conv_w=m_conv_w,
             conv_b=m_conv_b, w_rg_a=m_w_rg_a, b_rg_a=m_b_rg_a, w_rg_i=m_w_rg_i, b_rg_i=m_b_rg_i,
             lru_lambda=m_lru_lambda, w_rec_o=m_w_rec_o, w_out=m_w_out, ln2_g=m_ln2_g, w_ff1=m_w_ff1,
             w_ff2=m_w_ff2, lnf_g=m_lnf_g)
    v = dict(ln1_g=v_ln1_g, w_in=v_w_in, b_in=v_b_in, rpb=v_rpb, w_att_o=v_w_att_o, conv_w=v_conv_w,
             conv_b=v_conv_b, w_rg_a=v_w_rg_a, b_rg_a=v_b_rg_a, w_rg_i=v_w_rg_i, b_rg_i=v_b_rg_i,
             lru_lambda=v_lru_lambda, w_rec_o=v_w_rec_o, w_out=v_w_out, ln2_g=v_ln2_g, w_ff1=v_w_ff1,
             w_ff2=v_w_ff2, lnf_g=v_lnf_g)
    chip = 2 * lax.axis_index("x") + lax.axis_index("y")
    ids = jnp.stack([chip, lax.axis_index("c")]).astype(jnp.int32)

    out_grad, out_delta, out_m, out_v = {}, {}, {}, {}

    def update(n, gn):
        shape, two_d = w[n].shape, gn.shape
        d, nm, nv = _adamw("adamw_" + n, w[n].reshape(two_d), gn, m[n].reshape(two_d), v[n].reshape(two_d), 256)
        out_grad[n], out_delta[n], out_m[n], out_v[n] = (gn.reshape(shape), d.reshape(shape), nm.reshape(shape),
                                                         nv.reshape(shape))
        return d

    reducer = _Reducer(ids)
    p, late = _gather_weights(w, chip)
    loss, grad_x, g = _local_step(x, loss_target, p, late, reducer)
    small, sizes = _pack_small(g, loss + reducer.last_token[:1, :1])
    after = reducer.begin("small", {}, small)[0]
    after = reducer.finish("ff", after)
    after = reducer.advance("small", after)[0]
    order = ("ff", "proj", "in", "small")
    for tag, following in zip(order[:-1], order[1:]):
        after = reducer.finish(following, after)
        for n, red in zip(reducer.groups[tag]["names"], reducer.result(tag, after)):
            after = update(n, red)
    (small_red,) = reducer.result("small", after)
    gsmall, loss = _unpack_small(small_red, sizes, {n: g[n].shape for n in SMALL})
    two_d = {n: (int(np.prod(w[n].shape[:-1])), w[n].shape[-1]) for n in SMALL}
    for n in SHARDED_VECS:
        gsmall[n] = lax.dynamic_slice_in_dim(gsmall[n], chip * (D // N_CHIPS), D // N_CHIPS, axis=1)
    gs = [gsmall[n].reshape(two_d[n]) for n in SMALL]
    updates = _adamw_small([w[n].reshape(two_d[n]) for n in SMALL], gs, [m[n].reshape(two_d[n]) for n in SMALL],
                           [v[n].reshape(two_d[n]) for n in SMALL])
    for n, gn, (d, nm, nv) in zip(SMALL, gs, updates):
        shape = w[n].shape
        out_grad[n], out_delta[n], out_m[n], out_v[n] = (gn.reshape(shape), d.reshape(shape), nm.reshape(shape),
                                                         nv.reshape(shape))
    return (loss, grad_x, *[out_grad[n] for n in ORDER], *[out_delta[n] for n in ORDER],
            *[out_m[n] for n in ORDER], *[out_v[n] for n in ORDER])
```

```python
import numpy as np
import jax
import jax.numpy as jnp
from jax import lax
from jax.experimental import pallas as pl
from jax.experimental.pallas import tpu as pltpu

F32 = jnp.float32
BF16 = jnp.bfloat16

T = 2048
D = 1024
D_ATT = 512
D_IN = 5632
D_FF = 4096
N_HEADS = 8
HEAD_DIM = 64
GRID_W = 64
N_ROWS = T // GRID_W
WIN_H = 8
WIN_W = 16
KEYS = WIN_H * GRID_W
N_CHIPS = 4
EPS = 1e-6
LRU_C = 8.0
SCALE = HEAD_DIM ** -0.5
REC_CB = 256
REC_CHUNK = 256
PAD = 8

ADAM_LR = 0.001
ADAM_B1 = 0.9
ADAM_B2 = 0.999
ADAM_EPS = 1e-08
ADAM_WD = 0.01
ADAM_STEP = 10

VMEM_LIMIT = 56 * 1024 * 1024

NN = (((1,), (0,)), ((), ()))
NT = (((1,), (1,)), ((), ()))
TN = (((0,), (0,)), ((), ()))
MESH = pl.DeviceIdType.MESH


def _params(sem=None):
    return pltpu.CompilerParams(dimension_semantics=sem, vmem_limit_bytes=VMEM_LIMIT)


def _dot(a, b, dims):
    return lax.dot_general(a, b, dims, preferred_element_type=F32)


def _sigmoid(x):
    return 0.5 * jnp.tanh(0.5 * x) + 0.5


def _matmul(name, a, b, *, dims, grid, a_spec, b_spec, out_shapes, out_specs, acc_shape,
            extras=(), extra_specs=(), epilogue=None, colsum_spec=None, colsum_shape=None, after=(),
            semantics=("parallel", "parallel", "arbitrary"), epilogue_takes_first=False):
    nk = grid[2]
    n_extra = len(extras)
    n_out = len(out_shapes)
    with_colsum = colsum_spec is not None

    def body(a_ref, b_ref, *rest):
        ex = rest[:n_extra]
        rest = rest[:n_extra] + rest[n_extra + len(after):]
        outs = rest[n_extra:n_extra + n_out]
        pos = n_extra + n_out
        cs_out = rest[pos] if with_colsum else None
        pos += 1 if with_colsum else 0
        acc = rest[pos]
        cs_acc = rest[pos + 1] if with_colsum else None
        k = pl.program_id(2)
        first_tile = pl.program_id(0) == 0

        @pl.when(k == 0)
        def _():
            acc[...] = jnp.zeros_like(acc)
            if with_colsum:
                cs_acc[...] = jnp.zeros_like(cs_acc)

        bv = b_ref[...]
        acc[...] += _dot(a_ref[...].astype(BF16), bv.astype(BF16), dims)
        if with_colsum:
            cs_acc[...] += jnp.sum(bv.astype(F32), axis=0, keepdims=True)

        @pl.when(k == nk - 1)
        def _():
            r = acc[...]
            if epilogue is None:
                outs[0][...] = r.astype(outs[0].dtype)
            elif epilogue_takes_first:
                epilogue(r, ex, outs, first_tile)
            else:
                epilogue(r, ex, outs)
            if with_colsum:
                cs_out[...] = cs_acc[...]

    shapes = list(out_shapes)
    specs = list(out_specs)
    scratch = [pltpu.VMEM(acc_shape, F32)]
    if with_colsum:
        shapes.append(colsum_shape)
        specs.append(colsum_spec)
        scratch.append(pltpu.VMEM((1, acc_shape[1]), F32))
    res = pl.pallas_call(
        body, name=name, grid=grid,
        in_specs=[a_spec, b_spec, *extra_specs] + [_ANY] * len(after),
        out_specs=specs, out_shape=shapes, scratch_shapes=scratch,
        compiler_params=_params(semantics),
    )(a, b, *extras, *after)
    return res


def _sds(shape, dtype):
    return jax.ShapeDtypeStruct(shape, dtype)


TM = 1024
NI = T // TM
TJ = T
NJ = T // TJ


def _mm_nn_cols(name, a, wg, out_dtype, *, bias=None, extras=(), extra_specs=(), epilogue=None,
                out_shapes=None, out_specs=None):
    k_dim, n4 = wg.shape[1], wg.shape[2]
    ex, exs = list(extras), list(extra_specs)
    if bias is not None:
        ex = [bias] + ex
        exs = [pl.BlockSpec((1, n4), lambda j, i, k: (0, j))] + exs
        user_ep = epilogue

        def epilogue(r, e, outs):
            r = r + e[0][...]
            if user_ep is None:
                outs[0][...] = r.astype(outs[0].dtype)
            else:
                user_ep(r, e[1:], outs)
    if out_shapes is None:
        out_shapes = [_sds((T, N_CHIPS * n4), out_dtype)]
        out_specs = [pl.BlockSpec((TJ, n4), lambda j, i, k: (i, j))]
    return _matmul(
        name, a, wg, dims=NN, grid=(N_CHIPS, NJ, 1),
        a_spec=pl.BlockSpec((TJ, k_dim), lambda j, i, k: (i, 0)),
        b_spec=pl.BlockSpec((None, k_dim, n4), lambda j, i, k: (j, 0, 0)),
        out_shapes=out_shapes, out_specs=out_specs, acc_shape=(TJ, n4),
        extras=ex, extra_specs=exs, epilogue=epilogue)


def _mm_nt_cols_rms_bwd(name, a, wg, x, g, dres, after=(), bf16_copy=False):
    n4 = wg.shape[2]
    row = pl.BlockSpec((TM, D), lambda i, j, k: (i, 0))
    vec = pl.BlockSpec((1, D), lambda i, j, k: (0, 0))

    def epilogue(dhv, ex, outs, first):
        x_ref, g_ref, dres_ref = ex
        dx_ref, dg_ref = outs[0], outs[-1]
        xv = x_ref[...]
        rstd = lax.rsqrt(jnp.mean(xv * xv, axis=-1, keepdims=True) + EPS)
        xhat = xv * rstd
        dy = dhv * g_ref[...]
        dx = dres_ref[...] + rstd * (dy - xhat * jnp.mean(dy * xhat, axis=-1, keepdims=True))
        dx_ref[...] = dx
        if bf16_copy:
            outs[1][...] = dx.astype(BF16)
        part = jnp.sum(dhv * xhat, axis=0, keepdims=True)

        @pl.when(first)
        def _():
            dg_ref[...] = part

        @pl.when(jnp.logical_not(first))
        def _():
            dg_ref[...] += part

    return _matmul(
        name, a, wg, dims=NT, grid=(NI, 1, N_CHIPS),
        a_spec=pl.BlockSpec((TM, n4), lambda i, j, k: (i, k)),
        b_spec=pl.BlockSpec((None, D, n4), lambda i, j, k: (k, 0, 0)),
        out_shapes=[_sds((T, D), F32)] + [_sds((T, D), BF16)] * bf16_copy + [_sds((1, D), F32)],
        out_specs=[row] + [row] * bf16_copy + [vec], acc_shape=(TM, D),
        extras=[x, g, dres], extra_specs=[row, vec, row], epilogue=epilogue, after=after,
        semantics=("arbitrary", "arbitrary", "arbitrary"), epilogue_takes_first=True)


def _mm_nt_rows(name, a, w, out_dtype, *, tn, extras=(), extra_specs=(), epilogue=None):
    k_dim, n = w.shape
    return _matmul(
        name, a, w, dims=NT, grid=(k_dim // tn, NJ, 1),
        a_spec=pl.BlockSpec((TJ, n), lambda j, i, k: (i, 0)),
        b_spec=pl.BlockSpec((tn, n), lambda j, i, k: (j, 0)),
        out_shapes=[_sds((T, k_dim), out_dtype)],
        out_specs=[pl.BlockSpec((TJ, tn), lambda j, i, k: (i, j))], acc_shape=(TJ, tn),
        extras=extras, extra_specs=extra_specs, epilogue=epilogue)


def _mm_tn_cols(name, a, g, n4, *, colsum=False):
    k_dim = a.shape[1]
    kw = {}
    if colsum:
        kw = dict(colsum_spec=pl.BlockSpec((1, n4), lambda j, i, k: (0, j)),
                  colsum_shape=_sds((1, N_CHIPS * n4), F32))
    return _matmul(
        name, a, g, dims=TN, grid=(N_CHIPS, 1, NJ),
        a_spec=pl.BlockSpec((TJ, k_dim), lambda j, i, k: (k, 0)),
        b_spec=pl.BlockSpec((TJ, n4), lambda j, i, k: (k, j)),
        out_shapes=[_sds((N_CHIPS, k_dim, n4), F32)],
        out_specs=[pl.BlockSpec((None, k_dim, n4), lambda j, i, k: (j, 0, 0))],
        acc_shape=(k_dim, n4), **kw)


def _mm_tn_rows(name, a, g, *, tm):
    k_dim, n = a.shape[1], g.shape[1]
    return _matmul(
        name, a, g, dims=TN, grid=(k_dim // tm, 1, NJ),
        a_spec=pl.BlockSpec((TJ, tm), lambda j, i, k: (k, j)),
        b_spec=pl.BlockSpec((TJ, n), lambda j, i, k: (k, 0)),
        out_shapes=[_sds((k_dim, n), F32)],
        out_specs=[pl.BlockSpec((tm, n), lambda j, i, k: (j, 0))], acc_shape=(tm, n))


TE = 256
NE = T // TE
_ROW = pl.BlockSpec((TE, D), lambda i: (i, 0))
_VEC = pl.BlockSpec((1, D), lambda i: (0, 0))


def _rms_fwd(name, x, g, after=()):
    def body(x_ref, g_ref, *rest):
        h_ref = rest[-1]
        xv = x_ref[...]
        rstd = lax.rsqrt(jnp.mean(xv * xv, axis=-1, keepdims=True) + EPS)
        h_ref[...] = (xv * rstd * g_ref[...]).astype(BF16)

    return pl.pallas_call(body, name=name, grid=(NE,), in_specs=[_ROW, _VEC] + [_ANY] * len(after), out_specs=_ROW,
                          out_shape=_sds((T, D), BF16), compiler_params=_params(("parallel",)))(x, g, *after)


def _mm_x2_loss_head(s, w_ff2, x1, target, g):
    k_dim = w_ff2.shape[0]
    row = pl.BlockSpec((TM, D), lambda i, j, k: (i, 0))
    vec = pl.BlockSpec((1, D), lambda i, j, k: (0, 0))

    def epilogue(r, ex, outs, first):
        x1_ref, t_ref, g_ref = ex
        loss_ref, dx_ref, dxb_ref, dg_ref = outs
        xv = x1_ref[...] + r
        rstd = lax.rsqrt(jnp.mean(xv * xv, axis=-1, keepdims=True) + EPS)
        xhat = xv * rstd
        gv = g_ref[...]
        err = xhat * gv - t_ref[...]
        dy = err * (1.0 / D)
        dxh = dy * gv
        dx = rstd * (dxh - xhat * jnp.mean(dxh * xhat, axis=-1, keepdims=True))
        dx_ref[...] = dx
        dxb_ref[...] = dx.astype(BF16)
        dg_part = jnp.sum(dy * xhat, axis=0, keepdims=True)
        loss_part = (0.5 / D) * jnp.sum(jnp.sum(err * err, axis=1, keepdims=True), axis=0, keepdims=True)

        @pl.when(first)
        def _():
            dg_ref[...] = dg_part
            loss_ref[...] = loss_part

        @pl.when(jnp.logical_not(first))
        def _():
            dg_ref[...] += dg_part
            loss_ref[...] += loss_part

    return _matmul(
        "mm_x2_loss_head", s, w_ff2, dims=NN, grid=(NI, 1, k_dim // D),
        a_spec=pl.BlockSpec((TM, D), lambda i, j, k: (i, k)), b_spec=pl.BlockSpec((D, D), lambda i, j, k: (k, 0)),
        out_shapes=[_sds((1, 1), F32), _sds((T, D), F32), _sds((T, D), BF16), _sds((1, D), F32)],
        out_specs=[pl.BlockSpec((1, 1), lambda i, j, k: (0, 0)), row, row, vec], acc_shape=(TM, D),
        extras=[x1, target, g], extra_specs=[row, row, vec], epilogue=epilogue,
        semantics=("arbitrary", "arbitrary", "arbitrary"), epilogue_takes_first=True)


MW = 512
_G_ATT_BLK = 3584 // MW
_G_REC_BLK = 4608 // MW


TB = 512


def _branch_specs():
    def row(cols):
        return pl.BlockSpec((TB, cols), lambda i: (i, 0))

    ga = pl.BlockSpec((TB, MW), lambda i: (i, _G_ATT_BLK))
    ga2 = pl.BlockSpec((TB, MW), lambda i: (i, _G_ATT_BLK + 1))
    gr = pl.BlockSpec((TB, MW), lambda i: (i, _G_REC_BLK))
    gr2 = pl.BlockSpec((TB, MW), lambda i: (i, _G_REC_BLK + 1))
    w_att = pl.BlockSpec((N_CHIPS, D_ATT, D // N_CHIPS), lambda i: (0, 0, 0))
    w_sq = pl.BlockSpec((D, D), lambda i: (0, 0))
    return row, (ga, ga2, gr, gr2), w_att, w_sq


def _gate_values(gate_refs):
    ga, ga2, gr, gr2 = (r[...] for r in gate_refs)
    return _sigmoid(jnp.concatenate([ga, ga2], axis=1)), _sigmoid(jnp.concatenate([gr, gr2], axis=1))


def _branches_fwd(att, g, z, x, w_att_o, w_rec_o, w_out, ln2_g, after=()):
    row, gate_specs, w_att, w_sq = _branch_specs()

    def body(att_ref, g_ref, ga_ref, ga2_ref, gr_ref, gr2_ref, x_ref, wa_ref, wr_ref, wo_ref, g2_ref, *rest):
        ya_ref, yr_ref, m_ref, x1_ref, h2_ref = rest[len(after):]
        attv = att_ref[...]
        ya = jnp.concatenate([_dot(attv, wa_ref[j], NN) for j in range(N_CHIPS)], axis=1)
        yr = _dot(g_ref[...], wr_ref[...], NN)
        sa, sr = _gate_values((ga_ref, ga2_ref, gr_ref, gr2_ref))
        mixed = (sa * ya + sr * yr).astype(BF16)
        ya_ref[...] = ya
        yr_ref[...] = yr
        m_ref[...] = mixed
        x1 = x_ref[...] + _dot(mixed, wo_ref[...], NN)
        x1_ref[...] = x1
        rstd = lax.rsqrt(jnp.mean(x1 * x1, axis=-1, keepdims=True) + EPS)
        h2_ref[...] = (x1 * rstd * g2_ref[...]).astype(BF16)

    return pl.pallas_call(
        body, name="branches_fwd", grid=(T // TB,),
        in_specs=[row(D_ATT), row(D), *gate_specs, row(D), w_att, w_sq, w_sq, pl.BlockSpec((1, D), lambda i: (0, 0))]
        + [_ANY] * len(after),
        out_specs=[row(D)] * 5,
        out_shape=[_sds((T, D), F32), _sds((T, D), F32), _sds((T, D), BF16), _sds((T, D), F32), _sds((T, D), BF16)],
        compiler_params=_params(("parallel",)))(att, g, z, z, z, z, x, w_att_o, w_rec_o, w_out, ln2_g, *after)


def _branches_bwd(dx1_b, y_att, y_rec, z, w_att_o, w_rec_o, w_out):
    row, gate_specs, w_att, w_sq = _branch_specs()
    n4 = D // N_CHIPS

    def body(dx_ref, ya_ref, yr_ref, ga_ref, ga2_ref, gr_ref, gr2_ref, wa_ref, wr_ref, wo_ref,
             dya_ref, dyr_ref, dga_ref, dgr_ref, datt_ref, dg_ref):
        dm = _dot(dx_ref[...], wo_ref[...], NT)
        sa, sr = _gate_values((ga_ref, ga2_ref, gr_ref, gr2_ref))
        dya = (dm * sa).astype(BF16)
        dyr = (dm * sr).astype(BF16)
        dya_ref[...] = dya
        dyr_ref[...] = dyr
        dga_ref[...] = (dm * ya_ref[...] * sa * (1.0 - sa)).astype(BF16)
        dgr_ref[...] = (dm * yr_ref[...] * sr * (1.0 - sr)).astype(BF16)
        datt = _dot(dya[:, 0:n4], wa_ref[0], NT)
        for j in range(1, N_CHIPS):
            datt = datt + _dot(dya[:, j * n4:(j + 1) * n4], wa_ref[j], NT)
        datt_ref[...] = datt.astype(BF16)
        dg_ref[...] = _dot(dyr, wr_ref[...], NT).astype(BF16)

    return pl.pallas_call(
        body, name="branches_bwd", grid=(T // TB,),
        in_specs=[row(D), row(D), row(D), *gate_specs, w_att, w_sq, w_sq],
        out_specs=[row(D)] * 4 + [row(D_ATT), row(D)],
        out_shape=[_sds((T, D), BF16)] * 4 + [_sds((T, D_ATT), BF16), _sds((T, D), BF16)],
        compiler_params=_params(("parallel",)))(dx1_b, y_att, y_rec, z, z, z, z, w_att_o, w_rec_o, w_out)


HP = 2 * HEAD_DIM
N_HP = N_HEADS // 2
ATT_UNROLL_FWD = 16
ATT_UNROLL_BWD = 8
DIAG_ROWS = 32


def _window_maps():
    diag = np.zeros((GRID_W * GRID_W, 128), np.float32)
    for qc in range(GRID_W):
        w0 = min(max(qc - WIN_W // 2, 0), GRID_W - WIN_W)
        for kc in range(w0, w0 + WIN_W):
            diag[qc * GRID_W + kc, kc - qc + WIN_W - 1] = 1.0
    return diag, diag.sum(axis=1)[None, :]


def _split3(x):
    a = x.astype(BF16)
    r = x - a.astype(F32)
    b = r.astype(BF16)
    c = (r - b.astype(F32)).astype(BF16)
    return a, b, c


N_DROW = 2 * WIN_H - 1
N_DPAIR = N_DROW - 1


def _bias_pairs(rpb):
    diag, valid = _window_maps()
    r2 = jnp.pad(rpb.reshape(N_HEADS * N_DROW, 2 * WIN_W - 1),
                 ((0, 128 - N_HEADS * N_DROW), (0, 128 - (2 * WIN_W - 1))))

    def body(r_ref, d_ref, v_ref, o_ref):
        dv = d_ref[...]
        t = sum(_dot(part, dv, NN) for part in _split3(r_ref[...]))
        o_ref[...] = jnp.where(v_ref[...] > 0.0, t, -1e30)

    t = pl.pallas_call(body, name="rpb_expand", out_shape=_sds((128, GRID_W * GRID_W), F32),
                       compiler_params=_params())(r2, jnp.asarray(diag.T, BF16), jnp.asarray(valid, F32))
    t = t[:N_HEADS * N_DROW].reshape(N_HEADS, N_DROW, GRID_W, GRID_W)
    return jnp.concatenate([t[:, :N_DPAIR], t[:, 1:]], axis=-1)


def _row_bias(tb_ref, hh, d0):
    return jnp.concatenate([tb_ref[hh, d0 + 2 * ii] for ii in range(WIN_H // 2)], axis=1)


def _row_window(r):
    rs = jnp.clip(r - WIN_H // 2, 0, N_ROWS - WIN_H)
    return pl.multiple_of(r * GRID_W, GRID_W), pl.multiple_of(rs * GRID_W, GRID_W), rs - r + (WIN_H - 1)


def _split_heads(src_ref, dst_ref, scale=None):
    for hh in range(2):
        v = src_ref[:, hh * HEAD_DIM:(hh + 1) * HEAD_DIM]
        dst_ref[hh] = (v if scale is None else v * scale).astype(BF16)


def _attn_items(qb_ref, kb_ref, vb_ref, tb_ref, first_row, n_rows):
    wins = [_row_window(first_row + u) for u in range(n_rows)]
    items = [(u, hh) for u in range(n_rows) for hh in range(2)]
    q = [qb_ref[hh, pl.ds(wins[u][0], GRID_W), :] for u, hh in items]
    k = [kb_ref[hh, pl.ds(wins[u][1], KEYS), :] for u, hh in items]
    v = [vb_ref[hh, pl.ds(wins[u][1], KEYS), :] for u, hh in items]
    s = [_dot(qi, ki, NT) + _row_bias(tb_ref, hh, wins[u][2]) for qi, ki, (u, hh) in zip(q, k, items)]
    m = [jnp.max(si, axis=-1, keepdims=True) for si in s]
    e = [jnp.exp(si - mi) for si, mi in zip(s, m)]
    inv = [1.0 / jnp.sum(ei, axis=-1, keepdims=True) for ei in e]
    p = [ei * li for ei, li in zip(e, inv)]
    return wins, items, q, k, v, p


def _attn_in_specs():
    q = pl.BlockSpec((T, HP), lambda p: (0, p))
    k = pl.BlockSpec((T, HP), lambda p: (0, N_HP + p))
    v = pl.BlockSpec((T, HP), lambda p: (0, 2 * N_HP + p))
    tb = pl.BlockSpec((2, N_DPAIR, GRID_W, HP), lambda p: (p, 0, 0, 0))
    return q, k, v, tb


_HEAD_SCRATCH = pltpu.VMEM((2, T, HEAD_DIM), BF16)


def _attn_fwd(z, tb):
    def body(q_ref, k_ref, v_ref, tb_ref, o_ref, qb_ref, kb_ref, vb_ref):
        _split_heads(q_ref, qb_ref, SCALE)
        _split_heads(k_ref, kb_ref)
        _split_heads(v_ref, vb_ref)

        def rows(it, carry):
            wins, items, _, _, v, p = _attn_items(qb_ref, kb_ref, vb_ref, tb_ref, it * ATT_UNROLL_FWD, ATT_UNROLL_FWD)
            o = [_dot(pi.astype(BF16), vi, NN) for pi, vi in zip(p, v)]
            for u, (q0, _, _) in enumerate(wins):
                o_ref[pl.ds(q0, GRID_W), :] = jnp.concatenate(o[2 * u:2 * u + 2], axis=1).astype(BF16)
            return carry

        lax.fori_loop(0, N_ROWS // ATT_UNROLL_FWD, rows, 0)

    blk = pl.BlockSpec((T, HP), lambda p: (0, p))
    return pl.pallas_call(
        body, name="attn_fwd", grid=(N_HP,), in_specs=list(_attn_in_specs()), out_specs=blk,
        out_shape=_sds((T, D_ATT), BF16), scratch_shapes=[_HEAD_SCRATCH] * 3,
        compiler_params=_params(("parallel",)))(z, z, z, tb)


def _attn_bwd(z, tb, d_att, after=()):
    def body(q_ref, k_ref, v_ref, tb_ref, do_ref, flip_ref, *rest):
        (dq_ref, dk_ref, dv_ref, diag_ref, qb_ref, kb_ref, vb_ref, dob_ref, dka_ref, dva_ref,
         ds_ref) = rest[len(after):]
        _split_heads(q_ref, qb_ref, SCALE)
        _split_heads(k_ref, kb_ref)
        _split_heads(v_ref, vb_ref)
        _split_heads(do_ref, dob_ref)
        dka_ref[...] = jnp.zeros_like(dka_ref)
        dva_ref[...] = jnp.zeros_like(dva_ref)
        ds_ref[...] = jnp.zeros_like(ds_ref)

        def rows(it, carry):
            wins, items, q, k, v, p = _attn_items(qb_ref, kb_ref, vb_ref, tb_ref, it * ATT_UNROLL_BWD, ATT_UNROLL_BWD)
            do = [dob_ref[hh, pl.ds(wins[u][0], GRID_W), :] for u, hh in items]
            dv = [_dot(pi.astype(BF16), di, TN) for pi, di in zip(p, do)]
            dp = [_dot(di, vi, NT) for di, vi in zip(do, v)]
            ds = [pi * (dpi - jnp.sum(dpi * pi, axis=-1, keepdims=True)) for pi, dpi in zip(p, dp)]
            dsb = [d.astype(BF16) for d in ds]
            dq = [_dot(d, ki, NN) * SCALE for d, ki in zip(dsb, k)]
            dk = [_dot(d, qi, TN) for d, qi in zip(dsb, q)]
            for d, (u, hh) in zip(ds, items):
                for ii in range(WIN_H // 2):
                    ds_ref[hh, wins[u][2] + 2 * ii] += d[:, ii * HP:(ii + 1) * HP]
            for u, (q0, k0, _) in enumerate(wins):
                dq_ref[pl.ds(q0, GRID_W), :] = jnp.concatenate(dq[2 * u:2 * u + 2], axis=1).astype(BF16)
                dka_ref[pl.ds(k0, KEYS), :] += jnp.concatenate(dk[2 * u:2 * u + 2], axis=1)
                dva_ref[pl.ds(k0, KEYS), :] += jnp.concatenate(dv[2 * u:2 * u + 2], axis=1)
            return carry

        lax.fori_loop(0, N_ROWS // ATT_UNROLL_BWD, rows, 0)
        dk_ref[...] = dka_ref[...].astype(BF16)
        dv_ref[...] = dva_ref[...].astype(BF16)
        _diag_sums(ds_ref, flip_ref, diag_ref)

    blk = pl.BlockSpec((T, HP), lambda p: (0, p))
    q, k, v, tbs = _attn_in_specs()
    flip = jnp.asarray(np.eye(HP, dtype=np.float32)[::-1], BF16)
    return pl.pallas_call(
        body, name="attn_bwd", grid=(N_HP,),
        in_specs=[q, k, v, tbs, blk, pl.BlockSpec((HP, HP), lambda p: (0, 0))] + [_ANY] * len(after),
        out_specs=[blk, blk, blk, pl.BlockSpec((None, DIAG_ROWS, HP), lambda p: (p, 0, 0))],
        out_shape=[_sds((T, D_ATT), BF16)] * 3 + [_sds((N_HP, DIAG_ROWS, HP), F32)],
        scratch_shapes=[_HEAD_SCRATCH] * 4 + [pltpu.VMEM((T, HP), F32), pltpu.VMEM((T, HP), F32),
                                              pltpu.VMEM((2, N_DPAIR, GRID_W, HP), F32)],
        compiler_params=_params(("parallel",)))(z, z, z, tb, d_att, flip, *after)


def _diag_sums(acc_ref, flip_ref, out_ref):
    flip = flip_ref[...]
    rows = []
    for hh in range(2):
        for pair in range(N_DPAIR):
            reversed_lanes = sum(_dot(part, flip, NN) for part in _split3(acc_ref[hh, pair]))
            skewed = pltpu.roll(reversed_lanes, 0, 1, stride=1, stride_axis=0)
            rows.append(jnp.sum(skewed, axis=0, keepdims=True))
    rows.append(jnp.zeros((DIAG_ROWS - len(rows), HP), F32))
    out_ref[...] = jnp.concatenate(rows, axis=0)


def _rpb_grad(diag_sums):
    g = diag_sums.reshape(N_HP * DIAG_ROWS, HP)
    sel = np.zeros((2, 128, N_HP * DIAG_ROWS), np.float32)
    lane = np.zeros((2, HP, 128), np.float32)
    for h in range(N_HEADS):
        for pair in range(N_DPAIR):
            for half in range(2):
                sel[half, h * N_DROW + pair + half, (h // 2) * DIAG_ROWS + (h % 2) * N_DPAIR + pair] = 1.0
    for j in range(2 * WIN_W - 1):
        for half in range(2):
            lane[half, (HP - 1 - GRID_W * half - (j - (WIN_W - 1))) % HP, j] = 1.0

    def body(g_ref, sel_ref, lane_ref, o_ref):
        parts = _split3(g_ref[...])
        total = None
        for half in range(2):
            picked = sum(_dot(sel_ref[half], part, NN) for part in parts)
            term = sum(_dot(part, lane_ref[half], NN) for part in _split3(picked))
            total = term if total is None else total + term
        o_ref[...] = total

    out = pl.pallas_call(body, name="rpb_grad", out_shape=_sds((128, 128), F32),
                         compiler_params=_params())(g, jnp.asarray(sel, BF16), jnp.asarray(lane, BF16))
    return out[:N_HEADS * N_DROW, :2 * WIN_W - 1].reshape(N_HEADS, N_DROW, 2 * WIN_W - 1)


N_CB = D // REC_CB
N_CHUNK = T // REC_CHUNK
N_TILE = T // 8
_U_BLK = 1536 // REC_CB
_Y_BLK = 2560 // REC_CB


def _block_diag(w):
    per = REC_CB // 64
    wt = w.reshape(2, N_CB, per, 64, 64)
    eye = jnp.eye(per, dtype=w.dtype)
    full = wt[:, :, :, :, None, :] * eye[None, None, :, None, :, None]
    return full.reshape(2, N_CB, REC_CB, REC_CB).astype(BF16)


def _block_diag_grad(g):
    per = REC_CB // 64
    g6 = g.reshape(2, N_CB, per, 64, per, 64)
    return jnp.stack([g6[:, :, p, :, p, :] for p in range(per)], axis=2).reshape(2, 16, 64, 64)


def _gelu(x):
    c = 0.7978845608028654
    return 0.5 * x * (1.0 + jnp.tanh(c * (x + 0.044715 * x * x * x)))


def _gelu_grad(x):
    c = 0.7978845608028654
    th = jnp.tanh(c * (x + 0.044715 * x * x * x))
    return 0.5 * (1.0 + th) + 0.5 * x * (1.0 - th * th) * c * (1.0 + 3.0 * 0.044715 * x * x)


def _softplus_neg(lam):
    x = -lam
    e = jnp.exp(-jnp.abs(x))
    w = 1.0 + e
    l1p = jnp.where(w == 1.0, e, jnp.log(w) * e / (w - 1.0))
    return jnp.maximum(x, 0.0) + l1p


def _one_minus_exp(x):
    poly = x * (1.0 + x * (1 / 2 + x * (1 / 6 + x * (1 / 24 + x * (1 / 120 + x * (1 / 720))))))
    return jnp.where(x > -0.125, -poly, 1.0 - jnp.exp(x))


def _conv_taps(pad_ref, t0, w, sign):
    out = None
    for j in range(4):
        term = w[j:j + 1, :] * pad_ref[pl.ds(PAD + t0 + sign * (j - 2), REC_CHUNK), :]
        out = term if out is None else out + term
    return out


def _gates(u, wa, wi, ba, bi, sp):
    ub = u.astype(BF16)
    r = _sigmoid(_dot(ub, wa, NN) + ba)
    i = _sigmoid(_dot(ub, wi, NN) + bi)
    log_a = -LRU_C * r * sp
    a = jnp.exp(log_a)
    x = jnp.maximum(_one_minus_exp(2.0 * log_a), 0.0)
    positive = x > 0.0
    inv = lax.rsqrt(jnp.where(positive, x, 1.0))
    mult = jnp.where(positive, x * inv, 0.0)
    return r, i, a, mult, jnp.where(positive, inv, 0.0)


def _tile_scan(a, b, sub, reverse):
    for s in (1, 2, 4):
        if reverse:
            a_s, b_s, m = pltpu.roll(a, 8 - s, 0), pltpu.roll(b, 8 - s, 0), sub < 8 - s
        else:
            a_s, b_s, m = pltpu.roll(a, s, 0), pltpu.roll(b, s, 0), sub >= s
        b = jnp.where(m, a * b_s + b, b)
        a = jnp.where(m, a * a_s, a)
    return a, b


def _last_row(x, row):
    return jnp.broadcast_to(x[row:row + 1, :], x.shape)


def _rec_prologue(up_ref, cw_ref, cb_ref, wa_ref, wi_ref, ba_ref, bi_ref, lam_ref,
                  upad_ref, u_ref, a_refs, h_refs):
    cb = up_ref.shape[1]
    zeros = jnp.zeros((PAD, cb), F32)
    upad_ref[pl.ds(0, PAD), :] = zeros
    upad_ref[pl.ds(PAD + T, PAD), :] = zeros
    upad_ref[pl.ds(PAD, T), :] = up_ref[...]
    cw = cw_ref[...]
    sp = _softplus_neg(lam_ref[...])
    for c in range(N_CHUNK):
        t0 = c * REC_CHUNK
        u = cb_ref[...] + _conv_taps(upad_ref, t0, cw, 1)
        u_ref[pl.ds(t0, REC_CHUNK), :] = u
        for d in range(2):
            _, i, a, mult, _ = _gates(u, wa_ref[d], wi_ref[d], ba_ref[d:d + 1, :], bi_ref[d:d + 1, :], sp[d:d + 1, :])
            a_refs[d][pl.ds(t0, REC_CHUNK), :] = a
            h_refs[d][pl.ds(t0, REC_CHUNK), :] = mult * (i * u)

    sub = lax.broadcasted_iota(jnp.int32, (8, cb), 0)

    def tile(k, carry):
        cf, cr = carry
        tf = pl.multiple_of(k * 8, 8)
        tr = pl.multiple_of((N_TILE - 1 - k) * 8, 8)
        af, bf = _tile_scan(a_refs[0][pl.ds(tf, 8), :], h_refs[0][pl.ds(tf, 8), :], sub, False)
        hf = af * cf + bf
        h_refs[0][pl.ds(tf, 8), :] = hf
        ar, br = _tile_scan(a_refs[1][pl.ds(tr, 8), :], h_refs[1][pl.ds(tr, 8), :], sub, True)
        hr = ar * cr + br
        h_refs[1][pl.ds(tr, 8), :] = hr
        return _last_row(af, 7) * cf + _last_row(bf, 7), _last_row(ar, 0) * cr + _last_row(br, 0)

    z8 = jnp.zeros((8, cb), F32)
    lax.fori_loop(0, N_TILE, tile, (z8, z8))
    return sp


def _rec_specs():
    up = pl.BlockSpec((T, REC_CB), lambda c: (0, _U_BLK + c))
    yb = pl.BlockSpec((T, REC_CB), lambda c: (0, _Y_BLK + c))
    cw = pl.BlockSpec((4, REC_CB), lambda c: (0, c))
    cbias = pl.BlockSpec((1, REC_CB), lambda c: (0, c))
    wbd = pl.BlockSpec((2, None, REC_CB, REC_CB), lambda c: (0, c, 0, 0))
    vec2 = pl.BlockSpec((2, REC_CB), lambda c: (0, c))
    col = pl.BlockSpec((T, REC_CB), lambda c: (0, c))
    return up, yb, cw, cbias, wbd, vec2, col


def _rec_fwd(z, conv_w, conv_b, wa, wi, ba, bi, lam):
    up, yb, cw, cbias, wbd, vec2, col = _rec_specs()

    def body(up_ref, yb_ref, cw_ref, cb_ref, wa_ref, wi_ref, ba_ref, bi_ref, lam_ref, g_ref,
             u_ref, af_ref, ar_ref, hf_ref, hr_ref, upad_ref):
        _rec_prologue(up_ref, cw_ref, cb_ref, wa_ref, wi_ref, ba_ref, bi_ref, lam_ref,
                      upad_ref, u_ref, (af_ref, ar_ref), (hf_ref, hr_ref))

        def chunk(c, carry):
            t0 = pl.multiple_of(c * REC_CHUNK, REC_CHUNK)
            rows = pl.ds(t0, REC_CHUNK)
            g_ref[rows, :] = ((hf_ref[rows, :] + hr_ref[rows, :]) * _gelu(yb_ref[rows, :])).astype(BF16)
            return carry

        lax.fori_loop(0, N_CHUNK, chunk, 0)

    res = pl.pallas_call(
        body, name="rec_fwd", grid=(N_CB,),
        in_specs=[up, yb, cw, cbias, wbd, wbd, vec2, vec2, vec2], out_specs=[col] * 6,
        out_shape=[_sds((T, D), BF16)] + [_sds((T, D), F32)] * 5,
        scratch_shapes=[pltpu.VMEM((T + 2 * PAD, REC_CB), F32)],
        compiler_params=_params(("parallel",)))(z, z, conv_w, conv_b, wa, wi, ba, bi, lam)
    return res[0], tuple(res[1:])


def _rec_bwd(z, dg, saved, conv_w, conv_b, wa, wi, ba, bi, lam, after=()):
    up, yb, cw, cbias, wbd, vec2, col = _rec_specs()

    def body(up_ref, yb_ref, dg_ref, u_ref, af_ref, ar_ref, hf_ref, hr_ref,
             cw_ref, cb_ref, wa_ref, wi_ref, ba_ref, bi_ref, lam_ref, *rest):
        (dup_ref, dyb_ref, dcw_ref, dcb_ref, dwa_ref, dwi_ref, dba_ref, dbi_ref, dlam_ref,
         upad_ref, dh_ref, gf_ref, gr_ref, daf_ref, dar_ref, dupad_ref) = rest[len(after):]
        g_refs, da_refs = (gf_ref, gr_ref), (daf_ref, dar_ref)
        cb = up_ref.shape[1]
        zeros = jnp.zeros((PAD, cb), F32)
        upad_ref[pl.ds(0, PAD), :] = zeros
        upad_ref[pl.ds(PAD + T, PAD), :] = zeros
        upad_ref[pl.ds(PAD, T), :] = up_ref[...]
        sp = _softplus_neg(lam_ref[...])

        def gate_chunk(c, carry):
            t0 = pl.multiple_of(c * REC_CHUNK, REC_CHUNK)
            rows = pl.ds(t0, REC_CHUNK)
            y = yb_ref[rows, :]
            dgv = dg_ref[rows, :].astype(F32)
            dh_ref[rows, :] = dgv * _gelu(y)
            dyb_ref[rows, :] = (dgv * (hf_ref[rows, :] + hr_ref[rows, :]) * _gelu_grad(y)).astype(BF16)
            return carry

        lax.fori_loop(0, N_CHUNK, gate_chunk, 0)

        sub = lax.broadcasted_iota(jnp.int32, (8, cb), 0)

        def tile(k, carry):
            cf, cr = carry
            kf = N_TILE - 1 - k
            tf = pl.multiple_of(kf * 8, 8)
            tnext = pl.multiple_of(jnp.minimum(kf + 1, N_TILE - 1) * 8, 8)
            tprev = pl.multiple_of(jnp.maximum(kf - 1, 0) * 8, 8)
            a_t = af_ref[pl.ds(tf, 8), :]
            a_n = jnp.where(kf < N_TILE - 1, af_ref[pl.ds(tnext, 8), :], 0.0)
            a_sh = jnp.where(sub == 7, pltpu.roll(a_n, 7, 0), pltpu.roll(a_t, 7, 0))
            ca, cbb = _tile_scan(a_sh, dh_ref[pl.ds(tf, 8), :], sub, True)
            gf = ca * cf + cbb
            h_t = hf_ref[pl.ds(tf, 8), :]
            h_p = jnp.where(kf > 0, hf_ref[pl.ds(tprev, 8), :], 0.0)
            h_sh = jnp.where(sub == 0, pltpu.roll(h_p, 1, 0), pltpu.roll(h_t, 1, 0))
            gf_ref[pl.ds(tf, 8), :] = gf
            daf_ref[pl.ds(tf, 8), :] = gf * h_sh
            tr = pl.multiple_of(k * 8, 8)
            rnext = pl.multiple_of(jnp.minimum(k + 1, N_TILE - 1) * 8, 8)
            rprev = pl.multiple_of(jnp.maximum(k - 1, 0) * 8, 8)
            b_t = ar_ref[pl.ds(tr, 8), :]
            b_p = jnp.where(k > 0, ar_ref[pl.ds(rprev, 8), :], 0.0)
            b_sh = jnp.where(sub == 0, pltpu.roll(b_p, 1, 0), pltpu.roll(b_t, 1, 0))
            ra, rb = _tile_scan(b_sh, dh_ref[pl.ds(tr, 8), :], sub, False)
            gr = ra * cr + rb
            hr_t = hr_ref[pl.ds(tr, 8), :]
            hr_n = jnp.where(k < N_TILE - 1, hr_ref[pl.ds(rnext, 8), :], 0.0)
            hr_sh = jnp.where(sub == 7, pltpu.roll(hr_n, 7, 0), pltpu.roll(hr_t, 7, 0))
            gr_ref[pl.ds(tr, 8), :] = gr
            dar_ref[pl.ds(tr, 8), :] = gr * hr_sh
            return _last_row(gf, 0), _last_row(gr, 7)

        z8 = jnp.zeros((8, cb), F32)
        lax.fori_loop(0, N_TILE, tile, (z8, z8))

        dupad_ref[pl.ds(0, PAD), :] = zeros
        dupad_ref[pl.ds(PAD + T, PAD), :] = zeros
        dwa_ref[...] = jnp.zeros_like(dwa_ref)
        dwi_ref[...] = jnp.zeros_like(dwi_ref)
        dba_ref[...] = jnp.zeros_like(dba_ref)
        dbi_ref[...] = jnp.zeros_like(dbi_ref)
        dlam_ref[...] = jnp.zeros_like(dlam_ref)

        def grad_chunk(c, carry):
            t0 = pl.multiple_of(c * REC_CHUNK, REC_CHUNK)
            rows = pl.ds(t0, REC_CHUNK)
            u = u_ref[rows, :]
            ub = u.astype(BF16)
            du = jnp.zeros((REC_CHUNK, cb), F32)
            for d in range(2):
                r, i, a, mult, inv_mult = _gates(u, wa_ref[d], wi_ref[d], ba_ref[d:d + 1, :], bi_ref[d:d + 1, :],
                                                 sp[d:d + 1, :])
                dbx = g_refs[d][rows, :]
                dmult = dbx * (i * u)
                diu = dbx * mult
                a2 = a * a
                dlog = da_refs[d][rows, :] * a - dmult * (a2 * inv_mult)
                dpa = (dlog * (-LRU_C) * sp[d:d + 1, :]) * r * (1.0 - r)
                dpi = (diu * u) * i * (1.0 - i)
                dpab, dpib = dpa.astype(BF16), dpi.astype(BF16)
                du = du + diu * i + _dot(dpab, wa_ref[d], NT) + _dot(dpib, wi_ref[d], NT)
                dwa_ref[d] += _dot(ub, dpab, TN)
                dwi_ref[d] += _dot(ub, dpib, TN)
                dba_ref[d:d + 1, :] += jnp.sum(dpa, axis=0, keepdims=True)
                dbi_ref[d:d + 1, :] += jnp.sum(dpi, axis=0, keepdims=True)
                dlam_ref[d:d + 1, :] += jnp.sum(dlog * r, axis=0, keepdims=True)
            dupad_ref[pl.ds(PAD + t0, REC_CHUNK), :] = du
            return carry

        lax.fori_loop(0, N_CHUNK, grad_chunk, 0)
        dlam_ref[...] = dlam_ref[...] * (LRU_C * _sigmoid(-lam_ref[...]))

        cw = cw_ref[...]
        dcb = jnp.zeros((1, cb), F32)
        dcw = [jnp.zeros((1, cb), F32) for _ in range(4)]
        for c in range(N_CHUNK):
            t0 = c * REC_CHUNK
            du = dupad_ref[pl.ds(PAD + t0, REC_CHUNK), :]
            dcb = dcb + jnp.sum(du, axis=0, keepdims=True)
            for j in range(4):
                dcw[j] = dcw[j] + jnp.sum(du * upad_ref[pl.ds(PAD + t0 + j - 2, REC_CHUNK), :], axis=0, keepdims=True)
            dup_ref[pl.ds(t0, REC_CHUNK), :] = _conv_taps(dupad_ref, t0, cw, -1).astype(BF16)
        dcb_ref[...] = dcb
        dcw_ref[...] = jnp.concatenate(dcw, axis=0)

    full = pltpu.VMEM((T, REC_CB), F32)
    padded = pltpu.VMEM((T + 2 * PAD, REC_CB), F32)
    return pl.pallas_call(
        body, name="rec_bwd", grid=(N_CB,),
        in_specs=[up, yb] + [col] * 6 + [cw, cbias, wbd, wbd, vec2, vec2, vec2] + [_ANY] * len(after),
        out_specs=[col, col, cw, cbias, wbd, wbd, vec2, vec2, vec2],
        out_shape=[_sds((T, D), BF16), _sds((T, D), BF16), _sds((4, D), F32), _sds((1, D), F32),
                   _sds((2, N_CB, REC_CB, REC_CB), F32), _sds((2, N_CB, REC_CB, REC_CB), F32),
                   _sds((2, D), F32), _sds((2, D), F32), _sds((2, D), F32)],
        scratch_shapes=[padded, full, full, full, full, full, padded],
        compiler_params=_params(("parallel",)))(z, z, dg, *saved, conv_w, conv_b, wa, wi, ba, bi, lam, *after)


class _NoReducer:
    def begin(self, tag, grads):
        return ()

    def advance(self, tag, after):
        return ()


def _local_step(x, target, p, late=None, reducer=_NoReducer()):
    x = x.reshape(T, D)
    target = target.reshape(T, D)
    tb = _bias_pairs(p["rpb"])
    wa, wi = _block_diag(p["w_rg_a"]), _block_diag(p["w_rg_i"])

    h1 = _rms_fwd("rms1_fwd", x, p["ln1_g"], after=late[0] if late else ())
    if late:
        p = {**p, **late[1]((h1, tb, wa, wi))}
    rec_params = (p["conv_w"], p["conv_b"], wa, wi, p["b_rg_a"], p["b_rg_i"], p["lru_lambda"])
    (z,) = _mm_nn_cols("mm_z", h1, p["w_in"], F32, bias=p["b_in"])
    att = _attn_fwd(z, tb)
    g, rec_saved = _rec_fwd(z, *rec_params)
    tok = ()
    if late:
        branch_weights, tok = late[2](g)
        p = {**p, **branch_weights}
    y_att, y_rec, mixed, x1, h2 = _branches_fwd(att, g, z, x, p["w_att_o"], p["w_rec_o"], p["w_out"], p["ln2_g"],
                                                after=tok)
    if late:
        p = {**p, **late[3](h2)}

    def relu2(r, ex, outs):
        rp = jnp.maximum(r, 0.0)
        outs[0][...] = (rp * rp).astype(BF16)

    (s,) = _mm_nn_cols("mm_ff1", h2, p["w_ff1"], BF16, epilogue=relu2)
    loss, dx2, dx2_b, g_lnf = _mm_x2_loss_head(s, p["w_ff2"], x1, target, p["lnf_g"])

    def relu2_bwd(r, ex, outs):
        outs[0][...] = (r * 2.0 * jnp.sqrt(ex[0][...].astype(F32))).astype(BF16)

    (df,) = _mm_nt_rows("mm_df", dx2_b, p["w_ff2"], BF16, tn=D, extras=[s],
                        extra_specs=[pl.BlockSpec((TJ, D), lambda j, i, k: (i, j))], epilogue=relu2_bwd)
    (g_w_ff2,) = _mm_tn_rows("mm_g_ff2", s, dx2_b, tm=D)
    (g_w_ff1,) = _mm_tn_cols("mm_g_ff1", h2, df, D)
    tok = reducer.begin("ff", dict(w_ff2=g_w_ff2, w_ff1=g_w_ff1))
    dx1, dx1_b, g_ln2 = _mm_nt_cols_rms_bwd("mm_dh2_rms2_bwd", df, p["w_ff1"], x1, p["ln2_g"], dx2, after=tok,
                                            bf16_copy=True)

    dy_att, dy_rec, dg_att, dg_rec, d_att, d_g = _branches_bwd(dx1_b, y_att, y_rec, z, p["w_att_o"], p["w_rec_o"],
                                                               p["w_out"])
    (g_w_out,) = _mm_tn_rows("mm_g_out", mixed, dx1_b, tm=D)
    (g_w_att_o,) = _mm_tn_cols("mm_g_att_o", att, dy_att, D // N_CHIPS)
    (g_w_rec_o,) = _mm_tn_rows("mm_g_rec_o", g, dy_rec, tm=D)
    tok = reducer.advance("ff", g_w_rec_o) + reducer.begin("proj", dict(w_out=g_w_out, w_att_o=g_w_att_o, w_rec_o=g_w_rec_o))

    dq, dk, dv, ds_acc = _attn_bwd(z, tb, d_att, after=tok)
    g_rpb = _rpb_grad(ds_acc)
    tok = reducer.advance("proj", dq)
    d_up, d_yb, g_conv_w, g_conv_b, g_wa, g_wi, g_ba, g_bi, g_lam = _rec_bwd(z, d_g, rec_saved, *rec_params, after=tok)
    dz = jnp.concatenate([dq, dk, dv, d_up, d_yb, dg_att, dg_rec], axis=1)

    g_w_in, g_b_in = _mm_tn_cols("mm_g_in", h1, dz, D_IN // N_CHIPS, colsum=True)
    tok = reducer.begin("in", dict(w_in=g_w_in))
    grad_x, g_ln1 = _mm_nt_cols_rms_bwd("mm_dh1_rms1_bwd", dz, p["w_in"], x, p["ln1_g"], dx1, after=tok)
    reducer.advance("in", grad_x)

    grads = dict(ln1_g=g_ln1, w_in=g_w_in, b_in=g_b_in, rpb=g_rpb, w_att_o=g_w_att_o, conv_w=g_conv_w,
                 conv_b=g_conv_b, w_rg_a=_block_diag_grad(g_wa), b_rg_a=g_ba, w_rg_i=_block_diag_grad(g_wi),
                 b_rg_i=g_bi, lru_lambda=g_lam, w_rec_o=g_w_rec_o, w_out=g_w_out, ln2_g=g_ln2,
                 w_ff1=g_w_ff1, w_ff2=g_w_ff2, lnf_g=g_lnf)
    return loss, grad_x.reshape(1, T, D), grads


_ANY = pl.BlockSpec(memory_space=pl.ANY)
N_PEERS = N_CHIPS - 1


def _place():
    x, y, c = lax.axis_index("x"), lax.axis_index("y"), lax.axis_index("c")
    peers = [(1 - x, y), (x, 1 - y), (1 - x, 1 - y)]
    return x, y, c, 2 * x + y, peers


def _remote(src, dst, send_sem, recv_sem, dev):
    return pltpu.make_async_remote_copy(src_ref=src, dst_ref=dst, send_sem=send_sem, recv_sem=recv_sem,
                                        device_id=dev, device_id_type=MESH)


def _prefetch_call(body, name, ids, grid, in_specs, out_specs, out_shape, args, semantics=None):
    spec = pltpu.PrefetchScalarGridSpec(num_scalar_prefetch=1, grid=grid, in_specs=in_specs, out_specs=out_specs)
    return pl.pallas_call(body, name=name, grid_spec=spec, out_shape=out_shape,
                          compiler_params=_params(semantics or ("parallel",) * len(grid)))(ids, *args)


def _cast_bf16(name, w, chip_id, after=()):
    rows, cols = w.shape
    rb = min(rows, 256)

    def body(ids_ref, w_ref, *rest):
        rest[-1][...] = w_ref[...].astype(BF16)

    return _prefetch_call(body, name, chip_id, (rows // rb,),
                          [pl.BlockSpec((rb, cols), lambda i, ids: (i, 0))] + [_ANY] * len(after),
                          pl.BlockSpec((None, rb, cols), lambda i, ids: (ids[0], i, 0)),
                          _sds((N_CHIPS, rows, cols), BF16), (w, *after))


def _dma_sems(*counts):
    return [pltpu.SemaphoreType.DMA((k,)) for k in counts]


_HBM = pl.BlockSpec(memory_space=pltpu.HBM)
_SEM = pl.BlockSpec(memory_space=pltpu.SEMAPHORE)
_SPLIT_COPY = pltpu.CompilerParams(has_side_effects=pltpu.SideEffectType.DATAFLOW_SIDE_EFFECTING)
SIBLING_ID = 0
_SPLIT_COPY_SIBLING = pltpu.CompilerParams(has_side_effects=pltpu.SideEffectType.DATAFLOW_SIDE_EFFECTING,
                                           collective_id=SIBLING_ID)


def _sibling_handshake():
    x, y, c = lax.axis_index("x"), lax.axis_index("y"), lax.axis_index("c")
    barrier = pltpu.get_barrier_semaphore()
    pl.semaphore_signal(barrier, inc=1, device_id=(x, y, 1 - c), device_id_type=MESH)
    pl.semaphore_wait(barrier, 1)


def _hbm(arrays):
    return [pltpu.with_memory_space_constraint(a, pltpu.HBM) for a in arrays]


def _hbm_like(arrays):
    return [pltpu.HBM(a.shape, a.dtype) for a in arrays]


def _halves(buf, c):
    half = buf.shape[1] // 2
    return pl.ds(c * half, half), pl.ds((1 - c) * half, half)


def _gather_start(name, slots):
    n = len(slots)
    nk = n * N_PEERS

    def body(*refs):
        bufs = refs[n:2 * n]
        send_sems, recv_sems, token = refs[2 * n:]
        x, y, c, chip, peers = _place()
        for t in range(n):
            mine, _ = _halves(bufs[t], c)
            for r, (px, py) in enumerate(peers):
                k = t * N_PEERS + r
                own = bufs[t].at[chip, mine]
                _remote(own, own, send_sems.at[k], recv_sems.at[k], (px, py, c)).start()
        token[...] = jnp.zeros_like(token)

    res = pl.pallas_call(
        body, name=name, in_specs=[_HBM] * n, out_specs=[_HBM] * n + [_SEM, _SEM, pl.BlockSpec(memory_space=pltpu.VMEM)],
        out_shape=_hbm_like(slots) + [pltpu.SemaphoreType.DMA((nk,)), pltpu.SemaphoreType.DMA((nk,)),
                                      _sds((8, 128), F32)],
        input_output_aliases={t: t for t in range(n)}, compiler_params=_SPLIT_COPY)(*_hbm(slots))
    return res[:n], (res[n], res[n + 1]), res[n + 2]


def _gather_wait(name, bufs, sems, after):
    n = len(bufs)
    after = tuple(after) if isinstance(after, (tuple, list)) else (after,)

    def body(*refs):
        ins = refs[:n]
        send_sems, recv_sems = refs[n], refs[n + 1]
        x, y, c, chip, peers = _place()
        for t in range(n):
            mine, _ = _halves(ins[t], c)
            for r, (px, py) in enumerate(peers):
                k = t * N_PEERS + r
                cp = _remote(ins[t].at[chip, mine], ins[t].at[2 * px + py, mine], send_sems.at[k], recv_sems.at[k],
                             (px, py, c))
                cp.wait_send()
                cp.wait_recv()

    return pl.pallas_call(
        body, name=name, in_specs=[_HBM] * n + [_SEM, _SEM] + [_ANY] * len(after), out_specs=[_HBM] * n,
        out_shape=_hbm_like(bufs), input_output_aliases={t: t for t in range(n)},
        compiler_params=_SPLIT_COPY)(*bufs, *sems, *after)


def _gather_forward(name, bufs):
    n = len(bufs)
    nk = n * N_PEERS

    def body(*refs):
        _sibling_handshake()
        outs = refs[n:2 * n]
        send_sems, recv_sems = refs[2 * n:]
        x, y, c, chip, peers = _place()
        sibling = (x, y, 1 - c)
        sends = []
        for t in range(n):
            mine, _ = _halves(outs[t], c)
            for r, (px, py) in enumerate(peers):
                k = t * N_PEERS + r
                landed = outs[t].at[2 * px + py, mine]
                sends.append(_remote(landed, landed, send_sems.at[k], recv_sems.at[k], sibling))
                sends[-1].start()
        for t in range(n):
            _, theirs = _halves(outs[t], c)
            for r, (px, py) in enumerate(peers):
                k = t * N_PEERS + r
                landed = outs[t].at[2 * px + py, theirs]
                _remote(landed, landed, send_sems.at[k], recv_sems.at[k], sibling).wait_recv()
        for cp in sends:
            cp.wait_send()

    return pl.pallas_call(
        body, name=name, in_specs=[_ANY] * n, out_specs=[_ANY] * n, out_shape=[_sds(b.shape, b.dtype) for b in bufs],
        input_output_aliases={t: t for t in range(n)}, scratch_shapes=_dma_sems(nk, nk),
        compiler_params=pltpu.CompilerParams(collective_id=SIBLING_ID))(*bufs)


def _forward_copies(bufs, send_sems, recv_sems):
    x, y, c, chip, peers = _place()
    copies = []
    for t, buf in enumerate(bufs):
        mine, theirs = _halves(buf, c)
        for r, (px, py) in enumerate(peers):
            k = t * N_PEERS + r
            copies.append((_remote(buf.at[2 * px + py, mine], buf.at[2 * px + py, mine], send_sems.at[k],
                                   recv_sems.at[k], (x, y, 1 - c)),
                           _remote(buf.at[2 * px + py, mine], buf.at[2 * px + py, theirs], send_sems.at[k],
                                   recv_sems.at[k], (x, y, 1 - c))))
    return copies


def _gather_forward_start(name, bufs):
    n = len(bufs)
    nk = n * N_PEERS

    def body(*refs):
        _sibling_handshake()
        send_sems, recv_sems, token = refs[2 * n:]
        for send, _ in _forward_copies(refs[n:2 * n], send_sems, recv_sems):
            send.start()
        token[...] = jnp.zeros_like(token)

    res = pl.pallas_call(
        body, name=name, in_specs=[_HBM] * n, out_specs=[_HBM] * n + [_SEM, _SEM, pl.BlockSpec(memory_space=pltpu.VMEM)],
        out_shape=_hbm_like(bufs) + [pltpu.SemaphoreType.DMA((nk,)), pltpu.SemaphoreType.DMA((nk,)), _sds((8, 128), F32)],
        input_output_aliases={t: t for t in range(n)}, compiler_params=_SPLIT_COPY_SIBLING)(*_hbm(bufs))
    return (res[:n], (res[n], res[n + 1])), res[n + 2]


def _gather_forward_wait(name, flight, after):
    bufs, sems = flight
    n = len(bufs)

    def body(*refs):
        for _, landing in _forward_copies(refs[:n], refs[n], refs[n + 1]):
            landing.wait_send()
            landing.wait_recv()

    return pl.pallas_call(
        body, name=name, in_specs=[_HBM] * n + [_SEM, _SEM, _ANY], out_specs=[_HBM] * n, out_shape=_hbm_like(bufs),
        input_output_aliases={t: t for t in range(n)}, compiler_params=_SPLIT_COPY)(*bufs, *sems, after)


def _pair_copies(n, srcs, lands, send_sems, recv_sems):
    x, y, c, _, _ = _place()
    sibling = (x, y, 1 - c)
    copies = []
    for t in range(n):
        half = srcs[t].shape[1] // 2
        for j in range(N_CHIPS):
            k = t * N_CHIPS + j
            copies.append(_remote(srcs[t].at[j, pl.ds((1 - c) * half, half)], lands[t].at[j],
                                  send_sems.at[k], recv_sems.at[k], sibling))
    for t in range(n, len(srcs)):
        k = n * N_CHIPS + t - n
        copies.append(_remote(srcs[t], lands[t], send_sems.at[k], recv_sems.at[k], sibling))
    return copies


def _pair_start(name, grads, wholes=()):
    n = len(grads)
    srcs = list(grads) + list(wholes)
    m = len(srcs)
    lands = [pltpu.HBM((N_CHIPS, g.shape[1] // 2, g.shape[2]), F32) for g in grads] + _hbm_like(wholes)
    ns = n * N_CHIPS + len(wholes)

    def body(*refs):
        _sibling_handshake()
        src_refs, land_refs = refs[m:2 * m], refs[2 * m:3 * m]
        send_sems, recv_sems, token = refs[3 * m:]
        for cp in _pair_copies(n, src_refs, land_refs, send_sems, recv_sems):
            cp.start()
        token[...] = jnp.zeros_like(token)

    res = pl.pallas_call(
        body, name=name, in_specs=[_HBM] * m,
        out_specs=[_HBM] * (2 * m) + [_SEM, _SEM, pl.BlockSpec(memory_space=pltpu.VMEM)],
        out_shape=_hbm_like(srcs) + lands + [pltpu.SemaphoreType.DMA((ns,)), pltpu.SemaphoreType.DMA((ns,)),
                                             _sds((8, 128), F32)],
        input_output_aliases={t: t for t in range(m)}, compiler_params=_SPLIT_COPY_SIBLING)(*_hbm(srcs))
    return (res[:m], res[m:2 * m], (res[2 * m], res[2 * m + 1])), res[2 * m + 2]


def _pair_wait(name, flight, n, after):
    srcs, lands, sems = flight
    m = len(srcs)

    def body(*refs):
        for cp in _pair_copies(n, refs[:m], refs[m:2 * m], refs[2 * m], refs[2 * m + 1]):
            cp.wait_send()
            cp.wait_recv()

    res = pl.pallas_call(
        body, name=name, in_specs=[_HBM] * (2 * m) + [_SEM, _SEM, _ANY], out_specs=[_HBM] * (2 * m),
        out_shape=_hbm_like(srcs) + _hbm_like(lands), input_output_aliases={t: t for t in range(2 * m)},
        compiler_params=_SPLIT_COPY)(*srcs, *lands, *sems, after)
    return res[:m], res[m:]


def _chip_copies(srcs, lands, small_src, small_land, send_sems, recv_sems):
    x, y, c, chip, peers = _place()
    n = len(srcs)
    copies = []
    for r, (px, py) in enumerate(peers):
        for t in range(n):
            k = t * N_PEERS + r
            copies.append(_remote(srcs[t].at[2 * px + py], lands[t].at[r], send_sems.at[k], recv_sems.at[k], (px, py, c)))
        if small_src is not None:
            k = n * N_PEERS + r
            half_s = small_src.shape[0] // 2
            copies.append(_remote(small_src.at[pl.ds(c * half_s, half_s)], small_land.at[r],
                                  send_sems.at[k], recv_sems.at[k], (px, py, c)))
    return copies


def _chip_start(name, sums_bf16, small=None):
    n = len(sums_bf16)
    srcs = list(sums_bf16) + ([small] if small is not None else [])
    m = len(srcs)
    lands = [pltpu.HBM((N_PEERS,) + s.shape[1:], BF16) for s in sums_bf16]
    if small is not None:
        lands.append(pltpu.HBM((N_PEERS, small.shape[0] // 2, 128), F32))
    nk = m * N_PEERS

    def body(*refs):
        src_refs, land_refs = refs[m:2 * m], refs[2 * m:3 * m]
        send_sems, recv_sems, token = refs[3 * m:]
        small_src, small_land = (src_refs[n], land_refs[n]) if small is not None else (None, None)
        for cp in _chip_copies(src_refs[:n], land_refs[:n], small_src, small_land, send_sems, recv_sems):
            cp.start()
        token[...] = jnp.zeros_like(token)

    res = pl.pallas_call(
        body, name=name, in_specs=[_HBM] * m,
        out_specs=[_HBM] * (2 * m) + [_SEM, _SEM, pl.BlockSpec(memory_space=pltpu.VMEM)],
        out_shape=_hbm_like(srcs) + lands + [pltpu.SemaphoreType.DMA((nk,)), pltpu.SemaphoreType.DMA((nk,)),
                                             _sds((8, 128), F32)],
        input_output_aliases={t: t for t in range(m)}, compiler_params=_SPLIT_COPY)(*_hbm(srcs))
    return (res[:m], res[m:2 * m], (res[2 * m], res[2 * m + 1])), res[2 * m + 2]


def _chip_wait(name, flight, with_small, after):
    srcs, lands, sems = flight
    m = len(srcs)
    n = m - 1 if with_small else m

    def body(*refs):
        src_refs, land_refs = refs[:m], refs[m:2 * m]
        send_sems, recv_sems = refs[2 * m], refs[2 * m + 1]
        small_src, small_land = (src_refs[n], land_refs[n]) if with_small else (None, None)
        for cp in _chip_copies(src_refs[:n], land_refs[:n], small_src, small_land, send_sems, recv_sems):
            cp.wait_send()
            cp.wait_recv()

    res = pl.pallas_call(
        body, name=name, in_specs=[_HBM] * (2 * m) + [_SEM, _SEM, _ANY], out_specs=[_HBM] * (2 * m),
        out_shape=_hbm_like(srcs) + _hbm_like(lands), input_output_aliases={t: t for t in range(2 * m)},
        compiler_params=_SPLIT_COPY)(*srcs, *lands, *sems, after)
    return res[:m], res[m:]


def _swap_start(name, bufs):
    n = len(bufs)

    def body(*refs):
        _sibling_handshake()
        outs = refs[n:2 * n]
        send_sems, recv_sems, token = refs[2 * n:]
        x, y, c, _, _ = _place()
        for t in range(n):
            h = outs[t].shape[0] // 2
            mine = outs[t].at[pl.ds(c * h, h)]
            _remote(mine, mine, send_sems.at[t], recv_sems.at[t], (x, y, 1 - c)).start()
        token[...] = jnp.zeros_like(token)

    res = pl.pallas_call(
        body, name=name, in_specs=[_HBM] * n, out_specs=[_HBM] * n + [_SEM, _SEM, pl.BlockSpec(memory_space=pltpu.VMEM)],
        out_shape=_hbm_like(bufs) + [pltpu.SemaphoreType.DMA((n,)), pltpu.SemaphoreType.DMA((n,)), _sds((8, 128), F32)],
        input_output_aliases={t: t for t in range(n)}, compiler_params=_SPLIT_COPY_SIBLING)(*_hbm(bufs))
    return (res[:n], (res[n], res[n + 1])), res[n + 2]


def _swap_wait(name, flight, after):
    bufs, sems = flight
    n = len(bufs)

    def body(*refs):
        ins = refs[:n]
        send_sems, recv_sems = refs[n], refs[n + 1]
        x, y, c, _, _ = _place()
        for t in range(n):
            h = ins[t].shape[0] // 2
            cp = _remote(ins[t].at[pl.ds(c * h, h)], ins[t].at[pl.ds((1 - c) * h, h)], send_sems.at[t],
                         recv_sems.at[t], (x, y, 1 - c))
            cp.wait_send()
            cp.wait_recv()

    return pl.pallas_call(
        body, name=name, in_specs=[_HBM] * n + [_SEM, _SEM, _ANY], out_specs=[_HBM] * n, out_shape=_hbm_like(bufs),
        input_output_aliases={t: t for t in range(n)}, compiler_params=_SPLIT_COPY)(*bufs, *sems, after)


def _pair_sum(name, grad, got, ids):
    _, rows, cols = got.shape
    rb = min(rows, 256)
    nb = rows // rb
    blk = pl.BlockSpec((None, rb, cols), lambda i, j, ids: (j, i, 0))
    mine = pl.BlockSpec((None, rb, cols), lambda i, j, ids: (j, ids[1] * nb + i, 0))
    own = pl.BlockSpec((rb, cols), lambda i, j, ids: (i, 0))

    def body(ids_ref, a_ref, b_ref, s_ref, sb_ref):
        s = a_ref[...] + b_ref[...]
        sb_ref[...] = s.astype(BF16)

        @pl.when(pl.program_id(1) == ids_ref[0])
        def _():
            s_ref[...] = s

    return _prefetch_call(body, name, ids, (nb, N_CHIPS), [mine, blk], [own, blk],
                          [_sds((rows, cols), F32), _sds(got.shape, BF16)], (grad, got),
                          semantics=("parallel", "arbitrary"))


def _chip_sum(name, own_sum, got, ids):
    rows, cols = own_sum.shape
    rb = min(rows, 256)
    nb = rows // rb
    own = pl.BlockSpec((rb, cols), lambda i, ids: (i, 0))
    blk3 = pl.BlockSpec((N_PEERS, rb, cols), lambda i, ids: (0, i, 0))
    out = pl.BlockSpec((rb, cols), lambda i, ids: (ids[1] * nb + i, 0))

    def body(ids_ref, a_ref, b_ref, o_ref):
        o_ref[...] = ((a_ref[...] + b_ref[0].astype(F32)) + b_ref[1].astype(F32)) + b_ref[2].astype(F32)

    return _prefetch_call(body, name, ids, (nb,), [own, blk3], out, _sds((2 * rows, cols), F32), (own_sum, got))


SMALL_RB = 280


def _small_pair_sum(own, got):
    blk = pl.BlockSpec((SMALL_RB, 128), lambda i: (i, 0))

    def body(a_ref, b_ref, o_ref):
        o_ref[...] = a_ref[...] + b_ref[...]

    return pl.pallas_call(body, name="small_pair_sum", grid=(own.shape[0] // SMALL_RB,), in_specs=[blk, blk],
                          out_specs=blk, out_shape=_sds(own.shape, F32),
                          compiler_params=_params(("parallel",)))(own, got)


def _small_chip_sum(pair, got, ids):
    nb = pair.shape[0] // 2 // SMALL_RB
    half = pl.BlockSpec((SMALL_RB, 128), lambda i, ids: (ids[1] * nb + i, 0))
    blk3 = pl.BlockSpec((N_PEERS, SMALL_RB, 128), lambda i, ids: (0, i, 0))

    def body(ids_ref, a_ref, b_ref, o_ref):
        o_ref[...] = (a_ref[...] + b_ref[1]) + (b_ref[0] + b_ref[2])

    return _prefetch_call(body, "small_chip_sum", ids, (nb,), [half, blk3], half, _sds(pair.shape, F32), (pair, got))


def _adamw_math(w, g, m, v):
    m = ADAM_B1 * m + (1.0 - ADAM_B1) * g
    v = ADAM_B2 * v + (1.0 - ADAM_B2) * (g * g)
    m_hat = m / (1.0 - ADAM_B1 ** ADAM_STEP)
    v_hat = v / (1.0 - ADAM_B2 ** ADAM_STEP)
    delta = -ADAM_LR * (m_hat / (jnp.sqrt(v_hat) + ADAM_EPS) + ADAM_WD * w)
    return delta, m, v


def _adamw(name, w, g, m, v, rb=None):
    rows, cols = w.shape
    rb = rows if rb is None else rb
    blk = pl.BlockSpec((rb, cols), lambda i: (i, 0))

    def body(w_ref, g_ref, m_ref, v_ref, d_ref, nm_ref, nv_ref):
        d, nm, nv = _adamw_math(w_ref[...], g_ref[...], m_ref[...], v_ref[...])
        d_ref[...] = d
        nm_ref[...] = nm
        nv_ref[...] = nv

    return pl.pallas_call(body, name=name, grid=(rows // rb,), in_specs=[blk] * 4, out_specs=[blk] * 3,
                          out_shape=[_sds(w.shape, F32)] * 3, compiler_params=_params(("parallel",)))(w, g, m, v)


def _adamw_small(ws, gs, ms, vs):
    n = len(ws)

    def body(*refs):
        for t in range(n):
            w_ref, g_ref, m_ref, v_ref = (refs[k * n + t] for k in range(4))
            d, nm, nv = _adamw_math(w_ref[...], g_ref[...], m_ref[...], v_ref[...])
            for k, val in enumerate((d, nm, nv)):
                refs[(4 + k) * n + t][...] = val

    res = pl.pallas_call(body, name="adamw_small", out_shape=[_sds(a.shape, F32) for a in ws] * 3,
                         compiler_params=_params())(*ws, *gs, *ms, *vs)
    return [(res[t], res[n + t], res[2 * n + t]) for t in range(n)]


BIG = ("w_in", "w_att_o", "w_rec_o", "w_out", "w_ff1", "w_ff2")
SHARDED_VECS = ("conv_w", "b_rg_a", "b_rg_i", "lru_lambda")
SMALL = ("ln1_g", "b_in", "rpb", "conv_w", "conv_b", "w_rg_a", "b_rg_a", "w_rg_i", "b_rg_i", "lru_lambda",
         "ln2_g", "lnf_g")
SMALL_ROWS = 2240
ORDER = ("ln1_g", "w_in", "b_in", "rpb", "w_att_o", "conv_w", "conv_b", "w_rg_a", "b_rg_a", "w_rg_i", "b_rg_i",
         "lru_lambda", "w_rec_o", "w_out", "ln2_g", "w_ff1", "w_ff2", "lnf_g")


def _pack_small(grads, loss):
    parts, sizes = [], {}
    for n in SMALL:
        flat = grads[n].reshape(-1)
        pad = (-flat.shape[0]) % 128
        sizes[n] = (flat.shape[0], flat.shape[0] + pad)
        parts.append(jnp.pad(flat, (0, pad)))
    total = sum(s[1] for s in sizes.values())
    parts.append(jnp.pad(loss.reshape(1), (0, SMALL_ROWS * 128 - total - 1)))
    return jnp.concatenate(parts).reshape(SMALL_ROWS, 128), sizes


def _unpack_small(buf, sizes, shapes):
    flat = buf.reshape(-1)
    out, pos = {}, 0
    for n in SMALL:
        size, padded = sizes[n]
        out[n] = flat[pos:pos + size].reshape(shapes[n])
        pos += padded
    return out, flat[pos]


def _gather_weights(w, chip):
    chip_id = chip.astype(jnp.int32).reshape(1)
    vec_rows = [w[n][0] for n in SHARDED_VECS]
    vec_shard = jnp.concatenate(vec_rows + [jnp.zeros((16 - 10, D // N_CHIPS), F32)], axis=0)
    vec_slots = lax.dynamic_update_slice(jnp.zeros((N_CHIPS, 16, D // N_CHIPS), F32), vec_shard[None], (chip, 0, 0))
    bufs_a, sems_a, token_a = _gather_start("gather_start_first", [_cast_bf16("cast_w_in", w["w_in"][0], chip_id), vec_slots])
    rest_names = BIG[1:]
    bufs_b, sems_b, token_b = _gather_start(
        "gather_start_rest", [_cast_bf16("cast_" + n, w[n][0], chip_id, after=(token_a,)) for n in rest_names])

    def first(after):
        w_in_full, vec_full = _gather_forward("gather_forward_first", _gather_wait("gather_wait_first", bufs_a, sems_a, after))
        vecs = vec_full.transpose(1, 0, 2).reshape(16, D)
        return dict(w_in=w_in_full, conv_w=vecs[0:4], b_rg_a=vecs[4:6], b_rg_i=vecs[6:8], lru_lambda=vecs[8:10])

    ffn_flight = []

    def rest(after):
        landed = dict(zip(rest_names, _gather_wait("gather_wait_rest", bufs_b, sems_b, after)))
        branch = _gather_forward("gather_forward_branch", [landed[n] for n in ("w_att_o", "w_rec_o", "w_out")])
        flight, token = _gather_forward_start("gather_forward_ffn_start", [landed["w_ff1"], landed["w_ff2"]])
        ffn_flight.append(flight)
        return dict(w_att_o=branch[0], w_rec_o=branch[1].reshape(D, D), w_out=branch[2].reshape(D, D)), (token,)

    def ffn(after):
        w_ff1_full, w_ff2_full = _gather_forward_wait("gather_forward_ffn_wait", ffn_flight[0], after)
        return dict(w_ff1=w_ff1_full, w_ff2=w_ff2_full.reshape(D_FF, D))

    p = dict(ln1_g=w["ln1_g"], b_in=w["b_in"], rpb=w["rpb"][0], conv_b=w["conv_b"], w_rg_a=w["w_rg_a"][0],
             w_rg_i=w["w_rg_i"][0], ln2_g=w["ln2_g"], lnf_g=w["lnf_g"].reshape(1, D))
    return p, ((token_b,), first, rest, ffn)


class _Reducer:
    def __init__(self, ids):
        self.ids = ids
        self.groups = {}

    def begin(self, tag, grads, small=None):
        names = list(grads)
        big = [grads[n].reshape(N_CHIPS, -1, grads[n].shape[-1]) for n in names]
        flight, token = _pair_start("pair_start_" + tag, big, [] if small is None else [small])
        self.groups[tag] = dict(names=names, pair=flight, small=small is not None)
        return (token,)

    def advance(self, tag, after):
        grp = self.groups[tag]
        n = len(grp["names"])
        mine, got = _pair_wait("pair_wait_" + tag, grp["pair"], n, after)
        sums = [_pair_sum("pair_sum_" + name, a, b, self.ids) for name, a, b in zip(grp["names"], mine, got)]
        small_sum = _small_pair_sum(mine[n], got[n]) if grp["small"] else None
        grp["chip"], token = _chip_start("chip_start_" + tag, [s[1] for s in sums], small_sum)
        grp["sums"] = [s[0] for s in sums]
        self.last_token = token
        return (token,)

    def finish(self, tag, after):
        grp = self.groups[tag]
        srcs, lands = _chip_wait("chip_wait_" + tag, grp["chip"], grp["small"], after)
        halves = [_chip_sum("chip_sum_" + name, s, b, self.ids) for name, s, b in zip(grp["names"], grp["sums"], lands)]
        if grp["small"]:
            halves.append(_small_chip_sum(srcs[-1], lands[-1], self.ids))
        grp["swap"], token = _swap_start("swap_start_" + tag, halves)
        return token

    def result(self, tag, after):
        return _swap_wait("swap_wait_" + tag, self.groups[tag]["swap"], after)


def kernel(x, ln1_g, w_in, b_in, rpb, w_att_o, conv_w, conv_b, w_rg_a, b_rg_a, w_rg_i, b_rg_i, lru_lambda, w_rec_o, w_out, ln2_g, w_ff1, w_ff2, lnf_g, loss_target, m_ln1_g, m_w_in, m_b_in, m_rpb, m_w_att_o, m_conv_w, m_conv_b, m_w_rg_a, m_b_rg_a, m_w_rg_i, m_b_rg_i, m_lru_lambda, m_w_rec_o, m_w_out, m_ln2_g, m_w_ff1, m_w_ff2, m_lnf_g, v_ln1_g, v_w_in, v_b_in, v_rpb, v_w_att_o, v_conv_w, v_conv_b, v_w_rg_a, v_b_rg_a, v_w_rg_i, v_b_rg_i, v_lru_lambda, v_w_rec_o, v_w_out, v_ln2_g, v_w_ff1, v_w_ff2, v_lnf_g):
    w = dict(ln1_g=ln1_g, w_in=w_in, b_in=b_in, rpb=rpb, w_att_o=w_att_o, conv_w=conv_w, conv_b=conv_b,
             w_rg_a=w_rg_a, b_rg_a=b_rg_a, w_rg_i=w_rg_i, b_rg_i=b_rg_i, lru_lambda=lru_lambda, w_rec_o=w_rec_o,
             w_out=w_out, ln2_g=ln2_g, w_ff1=w_ff1, w_ff2=w_ff2, lnf_g=lnf_g)
    m = dict(ln1_g=m_ln1_g, w_in=m_w_in, b_in=m_b_in, rpb=m_rpb, w_att_o=m_w_att_o, conv_w=m_conv_w,
             conv_b=m_conv_b, w_rg_a=m_w_rg_a, b_rg_a=m_b_rg_a, w_rg_i=m_w_rg_i, b_rg_i=m_b_rg_i,
             lru_lambda=m_lru_lambda, w_rec_o=m_w_rec_o, w_out=m_w_out, ln2_g=m_ln2_g, w_ff1=m_w_ff1,
             w_ff2=m_w_ff2, lnf_g=m_lnf_g)
    v = dict(ln1_g=v_ln1_g, w_in=v_w_in, b_in=v_b_in, rpb=v_rpb, w_att_o=v_w_att_o, conv_w=v_conv_w,
             conv_b=v_conv_b, w_rg_a=v_w_rg_a, b_rg_a=v_b_rg_a, w_rg_i=v_w_rg_i, b_rg_i=v_b_rg_i,
             lru_lambda=v_lru_lambda, w_rec_o=v_w_rec_o, w_out=v_w_out, ln2_g=v_ln2_g, w_ff1=v_w_ff1,
             w_ff2=v_w_ff2, lnf_g=v_lnf_g)
    chip = 2 * lax.axis_index("x") + lax.axis_index("y")
    ids = jnp.stack([chip, lax.axis_index("c")]).astype(jnp.int32)

    out_grad, out_delta, out_m, out_v = {}, {}, {}, {}

    def update(n, gn):
        shape, two_d = w[n].shape, gn.shape
        d, nm, nv = _adamw("adamw_" + n, w[n].reshape(two_d), gn, m[n].reshape(two_d), v[n].reshape(two_d), 256)
        out_grad[n], out_delta[n], out_m[n], out_v[n] = (gn.reshape(shape), d.reshape(shape), nm.reshape(shape),
                                                         nv.reshape(shape))
        return d

    reducer = _Reducer(ids)
    p, late = _gather_weights(w, chip)
    loss, grad_x, g = _local_step(x, loss_target, p, late, reducer)
    small, sizes = _pack_small(g, loss + reducer.last_token[:1, :1])
    after = reducer.begin("small", {}, small)[0]
    after = reducer.finish("ff", after)
    after = reducer.advance("small", after)[0]
    order = ("ff", "proj", "in", "small")
    for tag, following in zip(order[:-1], order[1:]):
        after = reducer.finish(following, after)
        for n, red in zip(reducer.groups[tag]["names"], reducer.result(tag, after)):
            after = update(n, red)
    (small_red,) = reducer.result("small", after)
    gsmall, loss = _unpack_small(small_red, sizes, {n: g[n].shape for n in SMALL})
    two_d = {n: (int(np.prod(w[n].shape[:-1])), w[n].shape[-1]) for n in SMALL}
    for n in SHARDED_VECS:
        gsmall[n] = lax.dynamic_slice_in_dim(gsmall[n], chip * (D // N_CHIPS), D // N_CHIPS, axis=1)
    gs = [gsmall[n].reshape(two_d[n]) for n in SMALL]
    updates = _adamw_small([w[n].reshape(two_d[n]) for n in SMALL], gs, [m[n].reshape(two_d[n]) for n in SMALL],
                           [v[n].reshape(two_d[n]) for n in SMALL])
    for n, gn, (d, nm, nv) in zip(SMALL, gs, updates):
        shape = w[n].shape
        out_grad[n], out_delta[n], out_m[n], out_v[n] = (gn.reshape(shape), d.reshape(shape), nm.reshape(shape),
                                                         nv.reshape(shape))
    return (loss, grad_x, *[out_grad[n] for n in ORDER], *[out_delta[n] for n in ORDER],
            *[out_m[n] for n in ORDER], *[out_v[n] for n in ORDER])
```

```python
import numpy as np
import jax
import jax.numpy as jnp
from jax import lax
from jax.experimental import pallas as pl
from jax.experimental.pallas import tpu as pltpu

F32 = jnp.float32
BF16 = jnp.bfloat16

T = 2048
D = 1024
D_ATT = 512
D_IN = 5632
D_FF = 4096
N_HEADS = 8
HEAD_DIM = 64
GRID_W = 64
N_ROWS = T // GRID_W
WIN_H = 8
WIN_W = 16
KEYS = WIN_H * GRID_W
N_CHIPS = 4
EPS = 1e-6
LRU_C = 8.0
SCALE = HEAD_DIM ** -0.5
REC_CB = 256
REC_CHUNK = 256
PAD = 8

ADAM_LR = 0.001
ADAM_B1 = 0.9
ADAM_B2 = 0.999
ADAM_EPS = 1e-08
ADAM_WD = 0.01
ADAM_STEP = 10

VMEM_LIMIT = 56 * 1024 * 1024

NN = (((1,), (0,)), ((), ()))
NT = (((1,), (1,)), ((), ()))
TN = (((0,), (0,)), ((), ()))
MESH = pl.DeviceIdType.MESH


def _params(sem=None):
    return pltpu.CompilerParams(dimension_semantics=sem, vmem_limit_bytes=VMEM_LIMIT)


def _dot(a, b, dims):
    return lax.dot_general(a, b, dims, preferred_element_type=F32)


def _sigmoid(x):
    return 0.5 * jnp.tanh(0.5 * x) + 0.5


def _matmul(name, a, b, *, dims, grid, a_spec, b_spec, out_shapes, out_specs, acc_shape,
            extras=(), extra_specs=(), epilogue=None, colsum_spec=None, colsum_shape=None, after=(),
            semantics=("parallel", "parallel", "arbitrary"), epilogue_takes_first=False):
    nk = grid[2]
    n_extra = len(extras)
    n_out = len(out_shapes)
    with_colsum = colsum_spec is not None

    def body(a_ref, b_ref, *rest):
        ex = rest[:n_extra]
        rest = rest[:n_extra] + rest[n_extra + len(after):]
        outs = rest[n_extra:n_extra + n_out]
        pos = n_extra + n_out
        cs_out = rest[pos] if with_colsum else None
        pos += 1 if with_colsum else 0
        acc = rest[pos]
        cs_acc = rest[pos + 1] if with_colsum else None
        k = pl.program_id(2)
        first_tile = pl.program_id(0) == 0

        @pl.when(k == 0)
        def _():
            acc[...] = jnp.zeros_like(acc)
            if with_colsum:
                cs_acc[...] = jnp.zeros_like(cs_acc)

        bv = b_ref[...]
        acc[...] += _dot(a_ref[...].astype(BF16), bv.astype(BF16), dims)
        if with_colsum:
            cs_acc[...] += jnp.sum(bv.astype(F32), axis=0, keepdims=True)

        @pl.when(k == nk - 1)
        def _():
            r = acc[...]
            if epilogue is None:
                outs[0][...] = r.astype(outs[0].dtype)
            elif epilogue_takes_first:
                epilogue(r, ex, outs, first_tile)
            else:
                epilogue(r, ex, outs)
            if with_colsum:
                cs_out[...] = cs_acc[...]

    shapes = list(out_shapes)
    specs = list(out_specs)
    scratch = [pltpu.VMEM(acc_shape, F32)]
    if with_colsum:
        shapes.append(colsum_shape)
        specs.append(colsum_spec)
        scratch.append(pltpu.VMEM((1, acc_shape[1]), F32))
    res = pl.pallas_call(
        body, name=name, grid=grid,
        in_specs=[a_spec, b_spec, *extra_specs] + [_ANY] * len(after),
        out_specs=specs, out_shape=shapes, scratch_shapes=scratch,
        compiler_params=_params(semantics),
    )(a, b, *extras, *after)
    return res


def _sds(shape, dtype):
    return jax.ShapeDtypeStruct(shape, dtype)


TM = 1024
NI = T // TM
TJ = T
NJ = T // TJ


def _mm_nn_cols(name, a, wg, out_dtype, *, bias=None, extras=(), extra_specs=(), epilogue=None,
                out_shapes=None, out_specs=None):
    k_dim, n4 = wg.shape[1], wg.shape[2]
    ex, exs = list(extras), list(extra_specs)
    if bias is not None:
        ex = [bias] + ex
        exs = [pl.BlockSpec((1, n4), lambda j, i, k: (0, j))] + exs
        user_ep = epilogue

        def epilogue(r, e, outs):
            r = r + e[0][...]
            if user_ep is None:
                outs[0][...] = r.astype(outs[0].dtype)
            else:
                user_ep(r, e[1:], outs)
    if out_shapes is None:
        out_shapes = [_sds((T, N_CHIPS * n4), out_dtype)]
        out_specs = [pl.BlockSpec((TJ, n4), lambda j, i, k: (i, j))]
    return _matmul(
        name, a, wg, dims=NN, grid=(N_CHIPS, NJ, 1),
        a_spec=pl.BlockSpec((TJ, k_dim), lambda j, i, k: (i, 0)),
        b_spec=pl.BlockSpec((None, k_dim, n4), lambda j, i, k: (j, 0, 0)),
        out_shapes=out_shapes, out_specs=out_specs, acc_shape=(TJ, n4),
        extras=ex, extra_specs=exs, epilogue=epilogue)


def _mm_nt_cols_rms_bwd(name, a, wg, x, g, dres, after=(), bf16_copy=False):
    n4 = wg.shape[2]
    row = pl.BlockSpec((TM, D), lambda i, j, k: (i, 0))
    vec = pl.BlockSpec((1, D), lambda i, j, k: (0, 0))

    def epilogue(dhv, ex, outs, first):
        x_ref, g_ref, dres_ref = ex
        dx_ref, dg_ref = outs[0], outs[-1]
        xv = x_ref[...]
        rstd = lax.rsqrt(jnp.mean(xv * xv, axis=-1, keepdims=True) + EPS)
        xhat = xv * rstd
        dy = dhv * g_ref[...]
        dx = dres_ref[...] + rstd * (dy - xhat * jnp.mean(dy * xhat, axis=-1, keepdims=True))
        dx_ref[...] = dx
        if bf16_copy:
            outs[1][...] = dx.astype(BF16)
        part = jnp.sum(dhv * xhat, axis=0, keepdims=True)

        @pl.when(first)
        def _():
            dg_ref[...] = part

        @pl.when(jnp.logical_not(first))
        def _():
            dg_ref[...] += part

    return _matmul(
        name, a, wg, dims=NT, grid=(NI, 1, N_CHIPS),
        a_spec=pl.BlockSpec((TM, n4), lambda i, j, k: (i, k)),
        b_spec=pl.BlockSpec((None, D, n4), lambda i, j, k: (k, 0, 0)),
        out_shapes=[_sds((T, D), F32)] + [_sds((T, D), BF16)] * bf16_copy + [_sds((1, D), F32)],
        out_specs=[row] + [row] * bf16_copy + [vec], acc_shape=(TM, D),
        extras=[x, g, dres], extra_specs=[row, vec, row], epilogue=epilogue, after=after,
        semantics=("arbitrary", "arbitrary", "arbitrary"), epilogue_takes_first=True)


def _mm_nt_rows(name, a, w, out_dtype, *, tn, extras=(), extra_specs=(), epilogue=None):
    k_dim, n = w.shape
    return _matmul(
        name, a, w, dims=NT, grid=(k_dim // tn, NJ, 1),
        a_spec=pl.BlockSpec((TJ, n), lambda j, i, k: (i, 0)),
        b_spec=pl.BlockSpec((tn, n), lambda j, i, k: (j, 0)),
        out_shapes=[_sds((T, k_dim), out_dtype)],
        out_specs=[pl.BlockSpec((TJ, tn), lambda j, i, k: (i, j))], acc_shape=(TJ, tn),
        extras=extras, extra_specs=extra_specs, epilogue=epilogue)


def _mm_tn_cols(name, a, g, n4, *, colsum=False):
    k_dim = a.shape[1]
    kw = {}
    if colsum:
        kw = dict(colsum_spec=pl.BlockSpec((1, n4), lambda j, i, k: (0, j)),
                  colsum_shape=_sds((1, N_CHIPS * n4), F32))
    return _matmul(
        name, a, g, dims=TN, grid=(N_CHIPS, 1, NJ),
        a_spec=pl.BlockSpec((TJ, k_dim), lambda j, i, k: (k, 0)),
        b_spec=pl.BlockSpec((TJ, n4), lambda j, i, k: (k, j)),
        out_shapes=[_sds((N_CHIPS, k_dim, n4), F32)],
        out_specs=[pl.BlockSpec((None, k_dim, n4), lambda j, i, k: (j, 0, 0))],
        acc_shape=(k_dim, n4), **kw)


def _mm_tn_rows(name, a, g, *, tm):
    k_dim, n = a.shape[1], g.shape[1]
    return _matmul(
        name, a, g, dims=TN, grid=(k_dim // tm, 1, NJ),
        a_spec=pl.BlockSpec((TJ, tm), lambda j, i, k: (k, j)),
        b_spec=pl.BlockSpec((TJ, n), lambda j, i, k: (k, 0)),
        out_shapes=[_sds((k_dim, n), F32)],
        out_specs=[pl.BlockSpec((tm, n), lambda j, i, k: (j, 0))], acc_shape=(tm, n))


TE = 256
NE = T // TE
_ROW = pl.BlockSpec((TE, D), lambda i: (i, 0))
_VEC = pl.BlockSpec((1, D), lambda i: (0, 0))


def _rms_fwd(name, x, g, after=()):
    def body(x_ref, g_ref, *rest):
        h_ref = rest[-1]
        xv = x_ref[...]
        rstd = lax.rsqrt(jnp.mean(xv * xv, axis=-1, keepdims=True) + EPS)
        h_ref[...] = (xv * rstd * g_ref[...]).astype(BF16)

    return pl.pallas_call(body, name=name, grid=(NE,), in_specs=[_ROW, _VEC] + [_ANY] * len(after), out_specs=_ROW,
                          out_shape=_sds((T, D), BF16), compiler_params=_params(("parallel",)))(x, g, *after)


def _mm_x2_loss_head(s, w_ff2, x1, target, g):
    k_dim = w_ff2.shape[0]
    row = pl.BlockSpec((TM, D), lambda i, j, k: (i, 0))
    vec = pl.BlockSpec((1, D), lambda i, j, k: (0, 0))

    def epilogue(r, ex, outs, first):
        x1_ref, t_ref, g_ref = ex
        loss_ref, dx_ref, dxb_ref, dg_ref = outs
        xv = x1_ref[...] + r
        rstd = lax.rsqrt(jnp.mean(xv * xv, axis=-1, keepdims=True) + EPS)
        xhat = xv * rstd
        gv = g_ref[...]
        err = xhat * gv - t_ref[...]
        dy = err * (1.0 / D)
        dxh = dy * gv
        dx = rstd * (dxh - xhat * jnp.mean(dxh * xhat, axis=-1, keepdims=True))
        dx_ref[...] = dx
        dxb_ref[...] = dx.astype(BF16)
        dg_part = jnp.sum(dy * xhat, axis=0, keepdims=True)
        loss_part = (0.5 / D) * jnp.sum(jnp.sum(err * err, axis=1, keepdims=True), axis=0, keepdims=True)

        @pl.when(first)
        def _():
            dg_ref[...] = dg_part
            loss_ref[...] = loss_part

        @pl.when(jnp.logical_not(first))
        def _():
            dg_ref[...] += dg_part
            loss_ref[...] += loss_part

    return _matmul(
        "mm_x2_loss_head", s, w_ff2, dims=NN, grid=(NI, 1, k_dim // D),
        a_spec=pl.BlockSpec((TM, D), lambda i, j, k: (i, k)), b_spec=pl.BlockSpec((D, D), lambda i, j, k: (k, 0)),
        out_shapes=[_sds((1, 1), F32), _sds((T, D), F32), _sds((T, D), BF16), _sds((1, D), F32)],
        out_specs=[pl.BlockSpec((1, 1), lambda i, j, k: (0, 0)), row, row, vec], acc_shape=(TM, D),
        extras=[x1, target, g], extra_specs=[row, row, vec], epilogue=epilogue,
        semantics=("arbitrary", "arbitrary", "arbitrary"), epilogue_takes_first=True)


MW = 512
_G_ATT_BLK = 3584 // MW
_G_REC_BLK = 4608 // MW


TB = 512


def _branch_specs():
    def row(cols):
        return pl.BlockSpec((TB, cols), lambda i: (i, 0))

    ga = pl.BlockSpec((TB, MW), lambda i: (i, _G_ATT_BLK))
    ga2 = pl.BlockSpec((TB, MW), lambda i: (i, _G_ATT_BLK + 1))
    gr = pl.BlockSpec((TB, MW), lambda i: (i, _G_REC_BLK))
    gr2 = pl.BlockSpec((TB, MW), lambda i: (i, _G_REC_BLK + 1))
    w_att = pl.BlockSpec((N_CHIPS, D_ATT, D // N_CHIPS), lambda i: (0, 0, 0))
    w_sq = pl.BlockSpec((D, D), lambda i: (0, 0))
    return row, (ga, ga2, gr, gr2), w_att, w_sq


def _gate_values(gate_refs):
    ga, ga2, gr, gr2 = (r[...] for r in gate_refs)
    return _sigmoid(jnp.concatenate([ga, ga2], axis=1)), _sigmoid(jnp.concatenate([gr, gr2], axis=1))


def _branches_fwd(att, g, z, x, w_att_o, w_rec_o, w_out, ln2_g):
    row, gate_specs, w_att, w_sq = _branch_specs()

    def body(att_ref, g_ref, ga_ref, ga2_ref, gr_ref, gr2_ref, x_ref, wa_ref, wr_ref, wo_ref, g2_ref,
             ya_ref, yr_ref, m_ref, x1_ref, h2_ref):
        attv = att_ref[...]
        ya = jnp.concatenate([_dot(attv, wa_ref[j], NN) for j in range(N_CHIPS)], axis=1)
        yr = _dot(g_ref[...], wr_ref[...], NN)
        sa, sr = _gate_values((ga_ref, ga2_ref, gr_ref, gr2_ref))
        mixed = (sa * ya + sr * yr).astype(BF16)
        ya_ref[...] = ya
        yr_ref[...] = yr
        m_ref[...] = mixed
        x1 = x_ref[...] + _dot(mixed, wo_ref[...], NN)
        x1_ref[...] = x1
        rstd = lax.rsqrt(jnp.mean(x1 * x1, axis=-1, keepdims=True) + EPS)
        h2_ref[...] = (x1 * rstd * g2_ref[...]).astype(BF16)

    return pl.pallas_call(
        body, name="branches_fwd", grid=(T // TB,),
        in_specs=[row(D_ATT), row(D), *gate_specs, row(D), w_att, w_sq, w_sq, pl.BlockSpec((1, D), lambda i: (0, 0))],
        out_specs=[row(D)] * 5,
        out_shape=[_sds((T, D), F32), _sds((T, D), F32), _sds((T, D), BF16), _sds((T, D), F32), _sds((T, D), BF16)],
        compiler_params=_params(("parallel",)))(att, g, z, z, z, z, x, w_att_o, w_rec_o, w_out, ln2_g)


def _branches_bwd(dx1_b, y_att, y_rec, z, w_att_o, w_rec_o, w_out):
    row, gate_specs, w_att, w_sq = _branch_specs()
    n4 = D // N_CHIPS

    def body(dx_ref, ya_ref, yr_ref, ga_ref, ga2_ref, gr_ref, gr2_ref, wa_ref, wr_ref, wo_ref,
             dya_ref, dyr_ref, dga_ref, dgr_ref, datt_ref, dg_ref):
        dm = _dot(dx_ref[...], wo_ref[...], NT)
        sa, sr = _gate_values((ga_ref, ga2_ref, gr_ref, gr2_ref))
        dya = (dm * sa).astype(BF16)
        dyr = (dm * sr).astype(BF16)
        dya_ref[...] = dya
        dyr_ref[...] = dyr
        dga_ref[...] = (dm * ya_ref[...] * sa * (1.0 - sa)).astype(BF16)
        dgr_ref[...] = (dm * yr_ref[...] * sr * (1.0 - sr)).astype(BF16)
        datt = _dot(dya[:, 0:n4], wa_ref[0], NT)
        for j in range(1, N_CHIPS):
            datt = datt + _dot(dya[:, j * n4:(j + 1) * n4], wa_ref[j], NT)
        datt_ref[...] = datt.astype(BF16)
        dg_ref[...] = _dot(dyr, wr_ref[...], NT).astype(BF16)

    return pl.pallas_call(
        body, name="branches_bwd", grid=(T // TB,),
        in_specs=[row(D), row(D), row(D), *gate_specs, w_att, w_sq, w_sq],
        out_specs=[row(D)] * 4 + [row(D_ATT), row(D)],
        out_shape=[_sds((T, D), BF16)] * 4 + [_sds((T, D_ATT), BF16), _sds((T, D), BF16)],
        compiler_params=_params(("parallel",)))(dx1_b, y_att, y_rec, z, z, z, z, w_att_o, w_rec_o, w_out)


HP = 2 * HEAD_DIM
N_HP = N_HEADS // 2
ATT_UNROLL_FWD = 16
ATT_UNROLL_BWD = 8
DIAG_ROWS = 32


def _window_maps():
    diag = np.zeros((GRID_W * GRID_W, 128), np.float32)
    for qc in range(GRID_W):
        w0 = min(max(qc - WIN_W // 2, 0), GRID_W - WIN_W)
        for kc in range(w0, w0 + WIN_W):
            diag[qc * GRID_W + kc, kc - qc + WIN_W - 1] = 1.0
    return diag, diag.sum(axis=1)[None, :]


def _split3(x):
    a = x.astype(BF16)
    r = x - a.astype(F32)
    b = r.astype(BF16)
    c = (r - b.astype(F32)).astype(BF16)
    return a, b, c


N_DROW = 2 * WIN_H - 1
N_DPAIR = N_DROW - 1


def _bias_pairs(rpb):
    diag, valid = _window_maps()
    r2 = jnp.pad(rpb.reshape(N_HEADS * N_DROW, 2 * WIN_W - 1),
                 ((0, 128 - N_HEADS * N_DROW), (0, 128 - (2 * WIN_W - 1))))

    def body(r_ref, d_ref, v_ref, o_ref):
        dv = d_ref[...]
        t = sum(_dot(part, dv, NN) for part in _split3(r_ref[...]))
        o_ref[...] = jnp.where(v_ref[...] > 0.0, t, -1e30)

    t = pl.pallas_call(body, name="rpb_expand", out_shape=_sds((128, GRID_W * GRID_W), F32),
                       compiler_params=_params())(r2, jnp.asarray(diag.T, BF16), jnp.asarray(valid, F32))
    t = t[:N_HEADS * N_DROW].reshape(N_HEADS, N_DROW, GRID_W, GRID_W)
    return jnp.concatenate([t[:, :N_DPAIR], t[:, 1:]], axis=-1)


def _row_bias(tb_ref, hh, d0):
    return jnp.concatenate([tb_ref[hh, d0 + 2 * ii] for ii in range(WIN_H // 2)], axis=1)


def _row_window(r):
    rs = jnp.clip(r - WIN_H // 2, 0, N_ROWS - WIN_H)
    return pl.multiple_of(r * GRID_W, GRID_W), pl.multiple_of(rs * GRID_W, GRID_W), rs - r + (WIN_H - 1)


def _split_heads(src_ref, dst_ref, scale=None):
    for hh in range(2):
        v = src_ref[:, hh * HEAD_DIM:(hh + 1) * HEAD_DIM]
        dst_ref[hh] = (v if scale is None else v * scale).astype(BF16)


def _attn_items(qb_ref, kb_ref, vb_ref, tb_ref, first_row, n_rows):
    wins = [_row_window(first_row + u) for u in range(n_rows)]
    items = [(u, hh) for u in range(n_rows) for hh in range(2)]
    q = [qb_ref[hh, pl.ds(wins[u][0], GRID_W), :] for u, hh in items]
    k = [kb_ref[hh, pl.ds(wins[u][1], KEYS), :] for u, hh in items]
    v = [vb_ref[hh, pl.ds(wins[u][1], KEYS), :] for u, hh in items]
    s = [_dot(qi, ki, NT) + _row_bias(tb_ref, hh, wins[u][2]) for qi, ki, (u, hh) in zip(q, k, items)]
    m = [jnp.max(si, axis=-1, keepdims=True) for si in s]
    e = [jnp.exp(si - mi) for si, mi in zip(s, m)]
    inv = [1.0 / jnp.sum(ei, axis=-1, keepdims=True) for ei in e]
    p = [ei * li for ei, li in zip(e, inv)]
    return wins, items, q, k, v, p


def _attn_in_specs():
    q = pl.BlockSpec((T, HP), lambda p: (0, p))
    k = pl.BlockSpec((T, HP), lambda p: (0, N_HP + p))
    v = pl.BlockSpec((T, HP), lambda p: (0, 2 * N_HP + p))
    tb = pl.BlockSpec((2, N_DPAIR, GRID_W, HP), lambda p: (p, 0, 0, 0))
    return q, k, v, tb


_HEAD_SCRATCH = pltpu.VMEM((2, T, HEAD_DIM), BF16)


def _attn_fwd(z, tb):
    def body(q_ref, k_ref, v_ref, tb_ref, o_ref, qb_ref, kb_ref, vb_ref):
        _split_heads(q_ref, qb_ref, SCALE)
        _split_heads(k_ref, kb_ref)
        _split_heads(v_ref, vb_ref)

        def rows(it, carry):
            wins, items, _, _, v, p = _attn_items(qb_ref, kb_ref, vb_ref, tb_ref, it * ATT_UNROLL_FWD, ATT_UNROLL_FWD)
            o = [_dot(pi.astype(BF16), vi, NN) for pi, vi in zip(p, v)]
            for u, (q0, _, _) in enumerate(wins):
                o_ref[pl.ds(q0, GRID_W), :] = jnp.concatenate(o[2 * u:2 * u + 2], axis=1).astype(BF16)
            return carry

        lax.fori_loop(0, N_ROWS // ATT_UNROLL_FWD, rows, 0)

    blk = pl.BlockSpec((T, HP), lambda p: (0, p))
    return pl.pallas_call(
        body, name="attn_fwd", grid=(N_HP,), in_specs=list(_attn_in_specs()), out_specs=blk,
        out_shape=_sds((T, D_ATT), BF16), scratch_shapes=[_HEAD_SCRATCH] * 3,
        compiler_params=_params(("parallel",)))(z, z, z, tb)


def _attn_bwd(z, tb, d_att, after=()):
    def body(q_ref, k_ref, v_ref, tb_ref, do_ref, flip_ref, *rest):
        (dq_ref, dk_ref, dv_ref, diag_ref, qb_ref, kb_ref, vb_ref, dob_ref, dka_ref, dva_ref,
         ds_ref) = rest[len(after):]
        _split_heads(q_ref, qb_ref, SCALE)
        _split_heads(k_ref, kb_ref)
        _split_heads(v_ref, vb_ref)
        _split_heads(do_ref, dob_ref)
        dka_ref[...] = jnp.zeros_like(dka_ref)
        dva_ref[...] = jnp.zeros_like(dva_ref)
        ds_ref[...] = jnp.zeros_like(ds_ref)

        def rows(it, carry):
            wins, items, q, k, v, p = _attn_items(qb_ref, kb_ref, vb_ref, tb_ref, it * ATT_UNROLL_BWD, ATT_UNROLL_BWD)
            do = [dob_ref[hh, pl.ds(wins[u][0], GRID_W), :] for u, hh in items]
            dv = [_dot(pi.astype(BF16), di, TN) for pi, di in zip(p, do)]
            dp = [_dot(di, vi, NT) for di, vi in zip(do, v)]
            ds = [pi * (dpi - jnp.sum(dpi * pi, axis=-1, keepdims=True)) for pi, dpi in zip(p, dp)]
            dsb = [d.astype(BF16) for d in ds]
            dq = [_dot(d, ki, NN) * SCALE for d, ki in zip(dsb, k)]
            dk = [_dot(d, qi, TN) for d, qi in zip(dsb, q)]
            for d, (u, hh) in zip(ds, items):
                for ii in range(WIN_H // 2):
                    ds_ref[hh, wins[u][2] + 2 * ii] += d[:, ii * HP:(ii + 1) * HP]
            for u, (q0, k0, _) in enumerate(wins):
                dq_ref[pl.ds(q0, GRID_W), :] = jnp.concatenate(dq[2 * u:2 * u + 2], axis=1).astype(BF16)
                dka_ref[pl.ds(k0, KEYS), :] += jnp.concatenate(dk[2 * u:2 * u + 2], axis=1)
                dva_ref[pl.ds(k0, KEYS), :] += jnp.concatenate(dv[2 * u:2 * u + 2], axis=1)
            return carry

        lax.fori_loop(0, N_ROWS // ATT_UNROLL_BWD, rows, 0)
        dk_ref[...] = dka_ref[...].astype(BF16)
        dv_ref[...] = dva_ref[...].astype(BF16)
        _diag_sums(ds_ref, flip_ref, diag_ref)

    blk = pl.BlockSpec((T, HP), lambda p: (0, p))
    q, k, v, tbs = _attn_in_specs()
    flip = jnp.asarray(np.eye(HP, dtype=np.float32)[::-1], BF16)
    return pl.pallas_call(
        body, name="attn_bwd", grid=(N_HP,),
        in_specs=[q, k, v, tbs, blk, pl.BlockSpec((HP, HP), lambda p: (0, 0))] + [_ANY] * len(after),
        out_specs=[blk, blk, blk, pl.BlockSpec((None, DIAG_ROWS, HP), lambda p: (p, 0, 0))],
        out_shape=[_sds((T, D_ATT), BF16)] * 3 + [_sds((N_HP, DIAG_ROWS, HP), F32)],
        scratch_shapes=[_HEAD_SCRATCH] * 4 + [pltpu.VMEM((T, HP), F32), pltpu.VMEM((T, HP), F32),
                                              pltpu.VMEM((2, N_DPAIR, GRID_W, HP), F32)],
        compiler_params=_params(("parallel",)))(z, z, z, tb, d_att, flip, *after)


def _diag_sums(acc_ref, flip_ref, out_ref):
    flip = flip_ref[...]
    rows = []
    for hh in range(2):
        for pair in range(N_DPAIR):
            reversed_lanes = sum(_dot(part, flip, NN) for part in _split3(acc_ref[hh, pair]))
            skewed = pltpu.roll(reversed_lanes, 0, 1, stride=1, stride_axis=0)
            rows.append(jnp.sum(skewed, axis=0, keepdims=True))
    rows.append(jnp.zeros((DIAG_ROWS - len(rows), HP), F32))
    out_ref[...] = jnp.concatenate(rows, axis=0)


def _rpb_grad(diag_sums):
    g = diag_sums.reshape(N_HP * DIAG_ROWS, HP)
    sel = np.zeros((2, 128, N_HP * DIAG_ROWS), np.float32)
    lane = np.zeros((2, HP, 128), np.float32)
    for h in range(N_HEADS):
        for pair in range(N_DPAIR):
            for half in range(2):
                sel[half, h * N_DROW + pair + half, (h // 2) * DIAG_ROWS + (h % 2) * N_DPAIR + pair] = 1.0
    for j in range(2 * WIN_W - 1):
        for half in range(2):
            lane[half, (HP - 1 - GRID_W * half - (j - (WIN_W - 1))) % HP, j] = 1.0

    def body(g_ref, sel_ref, lane_ref, o_ref):
        parts = _split3(g_ref[...])
        total = None
        for half in range(2):
            picked = sum(_dot(sel_ref[half], part, NN) for part in parts)
            term = sum(_dot(part, lane_ref[half], NN) for part in _split3(picked))
            total = term if total is None else total + term
        o_ref[...] = total

    out = pl.pallas_call(body, name="rpb_grad", out_shape=_sds((128, 128), F32),
                         compiler_params=_params())(g, jnp.asarray(sel, BF16), jnp.asarray(lane, BF16))
    return out[:N_HEADS * N_DROW, :2 * WIN_W - 1].reshape(N_HEADS, N_DROW, 2 * WIN_W - 1)


N_CB = D // REC_CB
N_CHUNK = T // REC_CHUNK
N_TILE = T // 8
_U_BLK = 1536 // REC_CB
_Y_BLK = 2560 // REC_CB


def _block_diag(w):
    per = REC_CB // 64
    wt = w.reshape(2, N_CB, per, 64, 64)
    eye = jnp.eye(per, dtype=w.dtype)
    full = wt[:, :, :, :, None, :] * eye[None, None, :, None, :, None]
    return full.reshape(2, N_CB, REC_CB, REC_CB).astype(BF16)


def _block_diag_grad(g):
    per = REC_CB // 64
    g6 = g.reshape(2, N_CB, per, 64, per, 64)
    return jnp.stack([g6[:, :, p, :, p, :] for p in range(per)], axis=2).reshape(2, 16, 64, 64)


def _gelu(x):
    c = 0.7978845608028654
    return 0.5 * x * (1.0 + jnp.tanh(c * (x + 0.044715 * x * x * x)))


def _gelu_grad(x):
    c = 0.7978845608028654
    th = jnp.tanh(c * (x + 0.044715 * x * x * x))
    return 0.5 * (1.0 + th) + 0.5 * x * (1.0 - th * th) * c * (1.0 + 3.0 * 0.044715 * x * x)


def _softplus_neg(lam):
    x = -lam
    e = jnp.exp(-jnp.abs(x))
    w = 1.0 + e
    l1p = jnp.where(w == 1.0, e, jnp.log(w) * e / (w - 1.0))
    return jnp.maximum(x, 0.0) + l1p


def _one_minus_exp(x):
    poly = x * (1.0 + x * (1 / 2 + x * (1 / 6 + x * (1 / 24 + x * (1 / 120 + x * (1 / 720))))))
    return jnp.where(x > -0.125, -poly, 1.0 - jnp.exp(x))


def _conv_taps(pad_ref, t0, w, sign):
    out = None
    for j in range(4):
        term = w[j:j + 1, :] * pad_ref[pl.ds(PAD + t0 + sign * (j - 2), REC_CHUNK), :]
        out = term if out is None else out + term
    return out


def _gates(u, wa, wi, ba, bi, sp):
    ub = u.astype(BF16)
    r = _sigmoid(_dot(ub, wa, NN) + ba)
    i = _sigmoid(_dot(ub, wi, NN) + bi)
    log_a = -LRU_C * r * sp
    a = jnp.exp(log_a)
    x = jnp.maximum(_one_minus_exp(2.0 * log_a), 0.0)
    positive = x > 0.0
    inv = lax.rsqrt(jnp.where(positive, x, 1.0))
    mult = jnp.where(positive, x * inv, 0.0)
    return r, i, a, mult, jnp.where(positive, inv, 0.0)


def _tile_scan(a, b, sub, reverse):
    for s in (1, 2, 4):
        if reverse:
            a_s, b_s, m = pltpu.roll(a, 8 - s, 0), pltpu.roll(b, 8 - s, 0), sub < 8 - s
        else:
            a_s, b_s, m = pltpu.roll(a, s, 0), pltpu.roll(b, s, 0), sub >= s
        b = jnp.where(m, a * b_s + b, b)
        a = jnp.where(m, a * a_s, a)
    return a, b


def _last_row(x, row):
    return jnp.broadcast_to(x[row:row + 1, :], x.shape)


def _rec_prologue(up_ref, cw_ref, cb_ref, wa_ref, wi_ref, ba_ref, bi_ref, lam_ref,
                  upad_ref, u_ref, a_refs, h_refs):
    cb = up_ref.shape[1]
    zeros = jnp.zeros((PAD, cb), F32)
    upad_ref[pl.ds(0, PAD), :] = zeros
    upad_ref[pl.ds(PAD + T, PAD), :] = zeros
    upad_ref[pl.ds(PAD, T), :] = up_ref[...]
    cw = cw_ref[...]
    sp = _softplus_neg(lam_ref[...])
    for c in range(N_CHUNK):
        t0 = c * REC_CHUNK
        u = cb_ref[...] + _conv_taps(upad_ref, t0, cw, 1)
        u_ref[pl.ds(t0, REC_CHUNK), :] = u
        for d in range(2):
            _, i, a, mult, _ = _gates(u, wa_ref[d], wi_ref[d], ba_ref[d:d + 1, :], bi_ref[d:d + 1, :], sp[d:d + 1, :])
            a_refs[d][pl.ds(t0, REC_CHUNK), :] = a
            h_refs[d][pl.ds(t0, REC_CHUNK), :] = mult * (i * u)

    sub = lax.broadcasted_iota(jnp.int32, (8, cb), 0)

    def tile(k, carry):
        cf, cr = carry
        tf = pl.multiple_of(k * 8, 8)
        tr = pl.multiple_of((N_TILE - 1 - k) * 8, 8)
        af, bf = _tile_scan(a_refs[0][pl.ds(tf, 8), :], h_refs[0][pl.ds(tf, 8), :], sub, False)
        hf = af * cf + bf
        h_refs[0][pl.ds(tf, 8), :] = hf
        ar, br = _tile_scan(a_refs[1][pl.ds(tr, 8), :], h_refs[1][pl.ds(tr, 8), :], sub, True)
        hr = ar * cr + br
        h_refs[1][pl.ds(tr, 8), :] = hr
        return _last_row(af, 7) * cf + _last_row(bf, 7), _last_row(ar, 0) * cr + _last_row(br, 0)

    z8 = jnp.zeros((8, cb), F32)
    lax.fori_loop(0, N_TILE, tile, (z8, z8))
    return sp


def _rec_specs():
    up = pl.BlockSpec((T, REC_CB), lambda c: (0, _U_BLK + c))
    yb = pl.BlockSpec((T, REC_CB), lambda c: (0, _Y_BLK + c))
    cw = pl.BlockSpec((4, REC_CB), lambda c: (0, c))
    cbias = pl.BlockSpec((1, REC_CB), lambda c: (0, c))
    wbd = pl.BlockSpec((2, None, REC_CB, REC_CB), lambda c: (0, c, 0, 0))
    vec2 = pl.BlockSpec((2, REC_CB), lambda c: (0, c))
    col = pl.BlockSpec((T, REC_CB), lambda c: (0, c))
    return up, yb, cw, cbias, wbd, vec2, col


def _rec_fwd(z, conv_w, conv_b, wa, wi, ba, bi, lam):
    up, yb, cw, cbias, wbd, vec2, col = _rec_specs()

    def body(up_ref, yb_ref, cw_ref, cb_ref, wa_ref, wi_ref, ba_ref, bi_ref, lam_ref, g_ref,
             u_ref, af_ref, ar_ref, hf_ref, hr_ref, upad_ref):
        _rec_prologue(up_ref, cw_ref, cb_ref, wa_ref, wi_ref, ba_ref, bi_ref, lam_ref,
                      upad_ref, u_ref, (af_ref, ar_ref), (hf_ref, hr_ref))

        def chunk(c, carry):
            t0 = pl.multiple_of(c * REC_CHUNK, REC_CHUNK)
            rows = pl.ds(t0, REC_CHUNK)
            g_ref[rows, :] = ((hf_ref[rows, :] + hr_ref[rows, :]) * _gelu(yb_ref[rows, :])).astype(BF16)
            return carry

        lax.fori_loop(0, N_CHUNK, chunk, 0)

    res = pl.pallas_call(
        body, name="rec_fwd", grid=(N_CB,),
        in_specs=[up, yb, cw, cbias, wbd, wbd, vec2, vec2, vec2], out_specs=[col] * 6,
        out_shape=[_sds((T, D), BF16)] + [_sds((T, D), F32)] * 5,
        scratch_shapes=[pltpu.VMEM((T + 2 * PAD, REC_CB), F32)],
        compiler_params=_params(("parallel",)))(z, z, conv_w, conv_b, wa, wi, ba, bi, lam)
    return res[0], tuple(res[1:])


def _rec_bwd(z, dg, saved, conv_w, conv_b, wa, wi, ba, bi, lam, after=()):
    up, yb, cw, cbias, wbd, vec2, col = _rec_specs()

    def body(up_ref, yb_ref, dg_ref, u_ref, af_ref, ar_ref, hf_ref, hr_ref,
             cw_ref, cb_ref, wa_ref, wi_ref, ba_ref, bi_ref, lam_ref, *rest):
        (dup_ref, dyb_ref, dcw_ref, dcb_ref, dwa_ref, dwi_ref, dba_ref, dbi_ref, dlam_ref,
         upad_ref, dh_ref, gf_ref, gr_ref, daf_ref, dar_ref, dupad_ref) = rest[len(after):]
        g_refs, da_refs = (gf_ref, gr_ref), (daf_ref, dar_ref)
        cb = up_ref.shape[1]
        zeros = jnp.zeros((PAD, cb), F32)
        upad_ref[pl.ds(0, PAD), :] = zeros
        upad_ref[pl.ds(PAD + T, PAD), :] = zeros
        upad_ref[pl.ds(PAD, T), :] = up_ref[...]
        sp = _softplus_neg(lam_ref[...])

        def gate_chunk(c, carry):
            t0 = pl.multiple_of(c * REC_CHUNK, REC_CHUNK)
            rows = pl.ds(t0, REC_CHUNK)
            y = yb_ref[rows, :]
            dgv = dg_ref[rows, :].astype(F32)
            dh_ref[rows, :] = dgv * _gelu(y)
            dyb_ref[rows, :] = (dgv * (hf_ref[rows, :] + hr_ref[rows, :]) * _gelu_grad(y)).astype(BF16)
            return carry

        lax.fori_loop(0, N_CHUNK, gate_chunk, 0)

        sub = lax.broadcasted_iota(jnp.int32, (8, cb), 0)

        def tile(k, carry):
            cf, cr = carry
            kf = N_TILE - 1 - k
            tf = pl.multiple_of(kf * 8, 8)
            tnext = pl.multiple_of(jnp.minimum(kf + 1, N_TILE - 1) * 8, 8)
            tprev = pl.multiple_of(jnp.maximum(kf - 1, 0) * 8, 8)
            a_t = af_ref[pl.ds(tf, 8), :]
            a_n = jnp.where(kf < N_TILE - 1, af_ref[pl.ds(tnext, 8), :], 0.0)
            a_sh = jnp.where(sub == 7, pltpu.roll(a_n, 7, 0), pltpu.roll(a_t, 7, 0))
            ca, cbb = _tile_scan(a_sh, dh_ref[pl.ds(tf, 8), :], sub, True)
            gf = ca * cf + cbb
            h_t = hf_ref[pl.ds(tf, 8), :]
            h_p = jnp.where(kf > 0, hf_ref[pl.ds(tprev, 8), :], 0.0)
            h_sh = jnp.where(sub == 0, pltpu.roll(h_p, 1, 0), pltpu.roll(h_t, 1, 0))
            gf_ref[pl.ds(tf, 8), :] = gf
            daf_ref[pl.ds(tf, 8), :] = gf * h_sh
            tr = pl.multiple_of(k * 8, 8)
            rnext = pl.multiple_of(jnp.minimum(k + 1, N_TILE - 1) * 8, 8)
            rprev = pl.multiple_of(jnp.maximum(k - 1, 0) * 8, 8)
            b_t = ar_ref[pl.ds(tr, 8), :]
            b_p = jnp.where(k > 0, ar_ref[pl.ds(rprev, 8), :], 0.0)
            b_sh = jnp.where(sub == 0, pltpu.roll(b_p, 1, 0), pltpu.roll(b_t, 1, 0))
            ra, rb = _tile_scan(b_sh, dh_ref[pl.ds(tr, 8), :], sub, False)
            gr = ra * cr + rb
            hr_t = hr_ref[pl.ds(tr, 8), :]
            hr_n = jnp.where(k < N_TILE - 1, hr_ref[pl.ds(rnext, 8), :], 0.0)
            hr_sh = jnp.where(sub == 7, pltpu.roll(hr_n, 7, 0), pltpu.roll(hr_t, 7, 0))
            gr_ref[pl.ds(tr, 8), :] = gr
            dar_ref[pl.ds(tr, 8), :] = gr * hr_sh
            return _last_row(gf, 0), _last_row(gr, 7)

        z8 = jnp.zeros((8, cb), F32)
        lax.fori_loop(0, N_TILE, tile, (z8, z8))

        dupad_ref[pl.ds(0, PAD), :] = zeros
        dupad_ref[pl.ds(PAD + T, PAD), :] = zeros
        dwa_ref[...] = jnp.zeros_like(dwa_ref)
        dwi_ref[...] = jnp.zeros_like(dwi_ref)
        dba_ref[...] = jnp.zeros_like(dba_ref)
        dbi_ref[...] = jnp.zeros_like(dbi_ref)
        dlam_ref[...] = jnp.zeros_like(dlam_ref)

        def grad_chunk(c, carry):
            t0 = pl.multiple_of(c * REC_CHUNK, REC_CHUNK)
            rows = pl.ds(t0, REC_CHUNK)
            u = u_ref[rows, :]
            ub = u.astype(BF16)
            du = jnp.zeros((REC_CHUNK, cb), F32)
            for d in range(2):
                r, i, a, mult, inv_mult = _gates(u, wa_ref[d], wi_ref[d], ba_ref[d:d + 1, :], bi_ref[d:d + 1, :],
                                                 sp[d:d + 1, :])
                dbx = g_refs[d][rows, :]
                dmult = dbx * (i * u)
                diu = dbx * mult
                a2 = a * a
                dlog = da_refs[d][rows, :] * a - dmult * (a2 * inv_mult)
                dpa = (dlog * (-LRU_C) * sp[d:d + 1, :]) * r * (1.0 - r)
                dpi = (diu * u) * i * (1.0 - i)
                dpab, dpib = dpa.astype(BF16), dpi.astype(BF16)
                du = du + diu * i + _dot(dpab, wa_ref[d], NT) + _dot(dpib, wi_ref[d], NT)
                dwa_ref[d] += _dot(ub, dpab, TN)
                dwi_ref[d] += _dot(ub, dpib, TN)
                dba_ref[d:d + 1, :] += jnp.sum(dpa, axis=0, keepdims=True)
                dbi_ref[d:d + 1, :] += jnp.sum(dpi, axis=0, keepdims=True)
                dlam_ref[d:d + 1, :] += jnp.sum(dlog * r, axis=0, keepdims=True)
            dupad_ref[pl.ds(PAD + t0, REC_CHUNK), :] = du
            return carry

        lax.fori_loop(0, N_CHUNK, grad_chunk, 0)
        dlam_ref[...] = dlam_ref[...] * (LRU_C * _sigmoid(-lam_ref[...]))

        cw = cw_ref[...]
        dcb = jnp.zeros((1, cb), F32)
        dcw = [jnp.zeros((1, cb), F32) for _ in range(4)]
        for c in range(N_CHUNK):
            t0 = c * REC_CHUNK
            du = dupad_ref[pl.ds(PAD + t0, REC_CHUNK), :]
            dcb = dcb + jnp.sum(du, axis=0, keepdims=True)
            for j in range(4):
                dcw[j] = dcw[j] + jnp.sum(du * upad_ref[pl.ds(PAD + t0 + j - 2, REC_CHUNK), :], axis=0, keepdims=True)
            dup_ref[pl.ds(t0, REC_CHUNK), :] = _conv_taps(dupad_ref, t0, cw, -1).astype(BF16)
        dcb_ref[...] = dcb
        dcw_ref[...] = jnp.concatenate(dcw, axis=0)

    full = pltpu.VMEM((T, REC_CB), F32)
    padded = pltpu.VMEM((T + 2 * PAD, REC_CB), F32)
    return pl.pallas_call(
        body, name="rec_bwd", grid=(N_CB,),
        in_specs=[up, yb] + [col] * 6 + [cw, cbias, wbd, wbd, vec2, vec2, vec2] + [_ANY] * len(after),
        out_specs=[col, col, cw, cbias, wbd, wbd, vec2, vec2, vec2],
        out_shape=[_sds((T, D), BF16), _sds((T, D), BF16), _sds((4, D), F32), _sds((1, D), F32),
                   _sds((2, N_CB, REC_CB, REC_CB), F32), _sds((2, N_CB, REC_CB, REC_CB), F32),
                   _sds((2, D), F32), _sds((2, D), F32), _sds((2, D), F32)],
        scratch_shapes=[padded, full, full, full, full, full, padded],
        compiler_params=_params(("parallel",)))(z, z, dg, *saved, conv_w, conv_b, wa, wi, ba, bi, lam, *after)


class _NoReducer:
    def begin(self, tag, grads):
        return ()

    def advance(self, tag, after):
        return ()


def _local_step(x, target, p, late=None, reducer=_NoReducer()):
    x = x.reshape(T, D)
    target = target.reshape(T, D)
    tb = _bias_pairs(p["rpb"])
    wa, wi = _block_diag(p["w_rg_a"]), _block_diag(p["w_rg_i"])

    h1 = _rms_fwd("rms1_fwd", x, p["ln1_g"], after=late[0] if late else ())
    if late:
        p = {**p, **late[1]((h1, tb, wa, wi))}
    rec_params = (p["conv_w"], p["conv_b"], wa, wi, p["b_rg_a"], p["b_rg_i"], p["lru_lambda"])
    (z,) = _mm_nn_cols("mm_z", h1, p["w_in"], F32, bias=p["b_in"])
    att = _attn_fwd(z, tb)
    g, rec_saved = _rec_fwd(z, *rec_params)
    if late:
        p = {**p, **late[2](g)}
    y_att, y_rec, mixed, x1, h2 = _branches_fwd(att, g, z, x, p["w_att_o"], p["w_rec_o"], p["w_out"], p["ln2_g"])

    def relu2(r, ex, outs):
        rp = jnp.maximum(r, 0.0)
        outs[0][...] = (rp * rp).astype(BF16)

    (s,) = _mm_nn_cols("mm_ff1", h2, p["w_ff1"], BF16, epilogue=relu2)
    loss, dx2, dx2_b, g_lnf = _mm_x2_loss_head(s, p["w_ff2"], x1, target, p["lnf_g"])

    def relu2_bwd(r, ex, outs):
        outs[0][...] = (r * 2.0 * jnp.sqrt(ex[0][...].astype(F32))).astype(BF16)

    (df,) = _mm_nt_rows("mm_df", dx2_b, p["w_ff2"], BF16, tn=D, extras=[s],
                        extra_specs=[pl.BlockSpec((TJ, D), lambda j, i, k: (i, j))], epilogue=relu2_bwd)
    (g_w_ff2,) = _mm_tn_rows("mm_g_ff2", s, dx2_b, tm=D)
    (g_w_ff1,) = _mm_tn_cols("mm_g_ff1", h2, df, D)
    tok = reducer.begin("ff", dict(w_ff2=g_w_ff2, w_ff1=g_w_ff1))
    dx1, dx1_b, g_ln2 = _mm_nt_cols_rms_bwd("mm_dh2_rms2_bwd", df, p["w_ff1"], x1, p["ln2_g"], dx2, after=tok,
                                            bf16_copy=True)

    dy_att, dy_rec, dg_att, dg_rec, d_att, d_g = _branches_bwd(dx1_b, y_att, y_rec, z, p["w_att_o"], p["w_rec_o"],
                                                               p["w_out"])
    (g_w_out,) = _mm_tn_rows("mm_g_out", mixed, dx1_b, tm=D)
    (g_w_att_o,) = _mm_tn_cols("mm_g_att_o", att, dy_att, D // N_CHIPS)
    (g_w_rec_o,) = _mm_tn_rows("mm_g_rec_o", g, dy_rec, tm=D)
    tok = reducer.advance("ff", g_w_rec_o) + reducer.begin("proj", dict(w_out=g_w_out, w_att_o=g_w_att_o, w_rec_o=g_w_rec_o))

    dq, dk, dv, ds_acc = _attn_bwd(z, tb, d_att, after=tok)
    g_rpb = _rpb_grad(ds_acc)
    tok = reducer.advance("proj", dq)
    d_up, d_yb, g_conv_w, g_conv_b, g_wa, g_wi, g_ba, g_bi, g_lam = _rec_bwd(z, d_g, rec_saved, *rec_params, after=tok)
    dz = jnp.concatenate([dq, dk, dv, d_up, d_yb, dg_att, dg_rec], axis=1)

    g_w_in, g_b_in = _mm_tn_cols("mm_g_in", h1, dz, D_IN // N_CHIPS, colsum=True)
    tok = reducer.begin("in", dict(w_in=g_w_in))
    grad_x, g_ln1 = _mm_nt_cols_rms_bwd("mm_dh1_rms1_bwd", dz, p["w_in"], x, p["ln1_g"], dx1, after=tok)
    reducer.advance("in", grad_x)

    grads = dict(ln1_g=g_ln1, w_in=g_w_in, b_in=g_b_in, rpb=g_rpb, w_att_o=g_w_att_o, conv_w=g_conv_w,
                 conv_b=g_conv_b, w_rg_a=_block_diag_grad(g_wa), b_rg_a=g_ba, w_rg_i=_block_diag_grad(g_wi),
                 b_rg_i=g_bi, lru_lambda=g_lam, w_rec_o=g_w_rec_o, w_out=g_w_out, ln2_g=g_ln2,
                 w_ff1=g_w_ff1, w_ff2=g_w_ff2, lnf_g=g_lnf)
    return loss, grad_x.reshape(1, T, D), grads


_ANY = pl.BlockSpec(memory_space=pl.ANY)
N_PEERS = N_CHIPS - 1


def _place():
    x, y, c = lax.axis_index("x"), lax.axis_index("y"), lax.axis_index("c")
    peers = [(1 - x, y), (x, 1 - y), (1 - x, 1 - y)]
    return x, y, c, 2 * x + y, peers


def _remote(src, dst, send_sem, recv_sem, dev):
    return pltpu.make_async_remote_copy(src_ref=src, dst_ref=dst, send_sem=send_sem, recv_sem=recv_sem,
                                        device_id=dev, device_id_type=MESH)


def _prefetch_call(body, name, ids, grid, in_specs, out_specs, out_shape, args, semantics=None):
    spec = pltpu.PrefetchScalarGridSpec(num_scalar_prefetch=1, grid=grid, in_specs=in_specs, out_specs=out_specs)
    return pl.pallas_call(body, name=name, grid_spec=spec, out_shape=out_shape,
                          compiler_params=_params(semantics or ("parallel",) * len(grid)))(ids, *args)


def _cast_bf16(name, w, chip_id, after=()):
    rows, cols = w.shape
    rb = min(rows, 256)

    def body(ids_ref, w_ref, *rest):
        rest[-1][...] = w_ref[...].astype(BF16)

    return _prefetch_call(body, name, chip_id, (rows // rb,),
                          [pl.BlockSpec((rb, cols), lambda i, ids: (i, 0))] + [_ANY] * len(after),
                          pl.BlockSpec((None, rb, cols), lambda i, ids: (ids[0], i, 0)),
                          _sds((N_CHIPS, rows, cols), BF16), (w, *after))


def _dma_sems(*counts):
    return [pltpu.SemaphoreType.DMA((k,)) for k in counts]


_HBM = pl.BlockSpec(memory_space=pltpu.HBM)
_SEM = pl.BlockSpec(memory_space=pltpu.SEMAPHORE)
_SPLIT_COPY = pltpu.CompilerParams(has_side_effects=pltpu.SideEffectType.DATAFLOW_SIDE_EFFECTING)
SIBLING_ID = 0
_SPLIT_COPY_SIBLING = pltpu.CompilerParams(has_side_effects=pltpu.SideEffectType.DATAFLOW_SIDE_EFFECTING,
                                           collective_id=SIBLING_ID)


def _sibling_handshake():
    x, y, c = lax.axis_index("x"), lax.axis_index("y"), lax.axis_index("c")
    barrier = pltpu.get_barrier_semaphore()
    pl.semaphore_signal(barrier, inc=1, device_id=(x, y, 1 - c), device_id_type=MESH)
    pl.semaphore_wait(barrier, 1)


def _hbm(arrays):
    return [pltpu.with_memory_space_constraint(a, pltpu.HBM) for a in arrays]


def _hbm_like(arrays):
    return [pltpu.HBM(a.shape, a.dtype) for a in arrays]


def _halves(buf, c):
    half = buf.shape[1] // 2
    return pl.ds(c * half, half), pl.ds((1 - c) * half, half)


def _gather_start(name, slots):
    n = len(slots)
    nk = n * N_PEERS

    def body(*refs):
        bufs = refs[n:2 * n]
        send_sems, recv_sems, token = refs[2 * n:]
        x, y, c, chip, peers = _place()
        for t in range(n):
            mine, _ = _halves(bufs[t], c)
            for r, (px, py) in enumerate(peers):
                k = t * N_PEERS + r
                own = bufs[t].at[chip, mine]
                _remote(own, own, send_sems.at[k], recv_sems.at[k], (px, py, c)).start()
        token[...] = jnp.zeros_like(token)

    res = pl.pallas_call(
        body, name=name, in_specs=[_HBM] * n, out_specs=[_HBM] * n + [_SEM, _SEM, pl.BlockSpec(memory_space=pltpu.VMEM)],
        out_shape=_hbm_like(slots) + [pltpu.SemaphoreType.DMA((nk,)), pltpu.SemaphoreType.DMA((nk,)),
                                      _sds((8, 128), F32)],
        input_output_aliases={t: t for t in range(n)}, compiler_params=_SPLIT_COPY)(*_hbm(slots))
    return res[:n], (res[n], res[n + 1]), res[n + 2]


def _gather_wait(name, bufs, sems, after):
    n = len(bufs)
    after = tuple(after) if isinstance(after, (tuple, list)) else (after,)

    def body(*refs):
        ins = refs[:n]
        send_sems, recv_sems = refs[n], refs[n + 1]
        x, y, c, chip, peers = _place()
        for t in range(n):
            mine, _ = _halves(ins[t], c)
            for r, (px, py) in enumerate(peers):
                k = t * N_PEERS + r
                cp = _remote(ins[t].at[chip, mine], ins[t].at[2 * px + py, mine], send_sems.at[k], recv_sems.at[k],
                             (px, py, c))
                cp.wait_send()
                cp.wait_recv()

    return pl.pallas_call(
        body, name=name, in_specs=[_HBM] * n + [_SEM, _SEM] + [_ANY] * len(after), out_specs=[_HBM] * n,
        out_shape=_hbm_like(bufs), input_output_aliases={t: t for t in range(n)},
        compiler_params=_SPLIT_COPY)(*bufs, *sems, *after)


def _gather_forward(name, bufs):
    n = len(bufs)
    nk = n * N_PEERS

    def body(*refs):
        _sibling_handshake()
        outs = refs[n:2 * n]
        send_sems, recv_sems = refs[2 * n:]
        x, y, c, chip, peers = _place()
        sibling = (x, y, 1 - c)
        sends = []
        for t in range(n):
            mine, _ = _halves(outs[t], c)
            for r, (px, py) in enumerate(peers):
                k = t * N_PEERS + r
                landed = outs[t].at[2 * px + py, mine]
                sends.append(_remote(landed, landed, send_sems.at[k], recv_sems.at[k], sibling))
                sends[-1].start()
        for t in range(n):
            _, theirs = _halves(outs[t], c)
            for r, (px, py) in enumerate(peers):
                k = t * N_PEERS + r
                landed = outs[t].at[2 * px + py, theirs]
                _remote(landed, landed, send_sems.at[k], recv_sems.at[k], sibling).wait_recv()
        for cp in sends:
            cp.wait_send()

    return pl.pallas_call(
        body, name=name, in_specs=[_ANY] * n, out_specs=[_ANY] * n, out_shape=[_sds(b.shape, b.dtype) for b in bufs],
        input_output_aliases={t: t for t in range(n)}, scratch_shapes=_dma_sems(nk, nk),
        compiler_params=pltpu.CompilerParams(collective_id=SIBLING_ID))(*bufs)


def _pair_copies(n, srcs, lands, send_sems, recv_sems):
    x, y, c, _, _ = _place()
    sibling = (x, y, 1 - c)
    copies = []
    for t in range(n):
        half = srcs[t].shape[1] // 2
        for j in range(N_CHIPS):
            k = t * N_CHIPS + j
            copies.append(_remote(srcs[t].at[j, pl.ds((1 - c) * half, half)], lands[t].at[j],
                                  send_sems.at[k], recv_sems.at[k], sibling))
    for t in range(n, len(srcs)):
        k = n * N_CHIPS + t - n
        copies.append(_remote(srcs[t], lands[t], send_sems.at[k], recv_sems.at[k], sibling))
    return copies


def _pair_start(name, grads, wholes=()):
    n = len(grads)
    srcs = list(grads) + list(wholes)
    m = len(srcs)
    lands = [pltpu.HBM((N_CHIPS, g.shape[1] // 2, g.shape[2]), F32) for g in grads] + _hbm_like(wholes)
    ns = n * N_CHIPS + len(wholes)

    def body(*refs):
        _sibling_handshake()
        src_refs, land_refs = refs[m:2 * m], refs[2 * m:3 * m]
        send_sems, recv_sems, token = refs[3 * m:]
        for cp in _pair_copies(n, src_refs, land_refs, send_sems, recv_sems):
            cp.start()
        token[...] = jnp.zeros_like(token)

    res = pl.pallas_call(
        body, name=name, in_specs=[_HBM] * m,
        out_specs=[_HBM] * (2 * m) + [_SEM, _SEM, pl.BlockSpec(memory_space=pltpu.VMEM)],
        out_shape=_hbm_like(srcs) + lands + [pltpu.SemaphoreType.DMA((ns,)), pltpu.SemaphoreType.DMA((ns,)),
                                             _sds((8, 128), F32)],
        input_output_aliases={t: t for t in range(m)}, compiler_params=_SPLIT_COPY_SIBLING)(*_hbm(srcs))
    return (res[:m], res[m:2 * m], (res[2 * m], res[2 * m + 1])), res[2 * m + 2]


def _pair_wait(name, flight, n, after):
    srcs, lands, sems = flight
    m = len(srcs)

    def body(*refs):
        for cp in _pair_copies(n, refs[:m], refs[m:2 * m], refs[2 * m], refs[2 * m + 1]):
            cp.wait_send()
            cp.wait_recv()

    res = pl.pallas_call(
        body, name=name, in_specs=[_HBM] * (2 * m) + [_SEM, _SEM, _ANY], out_specs=[_HBM] * (2 * m),
        out_shape=_hbm_like(srcs) + _hbm_like(lands), input_output_aliases={t: t for t in range(2 * m)},
        compiler_params=_SPLIT_COPY)(*srcs, *lands, *sems, after)
    return res[:m], res[m:]


def _chip_copies(srcs, lands, small_src, small_land, send_sems, recv_sems):
    x, y, c, chip, peers = _place()
    n = len(srcs)
    copies = []
    for r, (px, py) in enumerate(peers):
        for t in range(n):
            k = t * N_PEERS + r
            copies.append(_remote(srcs[t].at[2 * px + py], lands[t].at[r], send_sems.at[k], recv_sems.at[k], (px, py, c)))
        if small_src is not None:
            k = n * N_PEERS + r
            half_s = small_src.shape[0] // 2
            copies.append(_remote(small_src.at[pl.ds(c * half_s, half_s)], small_land.at[r],
                                  send_sems.at[k], recv_sems.at[k], (px, py, c)))
    return copies


def _chip_start(name, sums_bf16, small=None):
    n = len(sums_bf16)
    srcs = list(sums_bf16) + ([small] if small is not None else [])
    m = len(srcs)
    lands = [pltpu.HBM((N_PEERS,) + s.shape[1:], BF16) for s in sums_bf16]
    if small is not None:
        lands.append(pltpu.HBM((N_PEERS, small.shape[0] // 2, 128), F32))
    nk = m * N_PEERS

    def body(*refs):
        src_refs, land_refs = refs[m:2 * m], refs[2 * m:3 * m]
        send_sems, recv_sems, token = refs[3 * m:]
        small_src, small_land = (src_refs[n], land_refs[n]) if small is not None else (None, None)
        for cp in _chip_copies(src_refs[:n], land_refs[:n], small_src, small_land, send_sems, recv_sems):
            cp.start()
        token[...] = jnp.zeros_like(token)

    res = pl.pallas_call(
        body, name=name, in_specs=[_HBM] * m,
        out_specs=[_HBM] * (2 * m) + [_SEM, _SEM, pl.BlockSpec(memory_space=pltpu.VMEM)],
        out_shape=_hbm_like(srcs) + lands + [pltpu.SemaphoreType.DMA((nk,)), pltpu.SemaphoreType.DMA((nk,)),
                                             _sds((8, 128), F32)],
        input_output_aliases={t: t for t in range(m)}, compiler_params=_SPLIT_COPY)(*_hbm(srcs))
    return (res[:m], res[m:2 * m], (res[2 * m], res[2 * m + 1])), res[2 * m + 2]


def _chip_wait(name, flight, with_small, after):
    srcs, lands, sems = flight
    m = len(srcs)
    n = m - 1 if with_small else m

    def body(*refs):
        src_refs, land_refs = refs[:m], refs[m:2 * m]
        send_sems, recv_sems = refs[2 * m], refs[2 * m + 1]
        small_src, small_land = (src_refs[n], land_refs[n]) if with_small else (None, None)
        for cp in _chip_copies(src_refs[:n], land_refs[:n], small_src, small_land, send_sems, recv_sems):
            cp.wait_send()
            cp.wait_recv()

    res = pl.pallas_call(
        body, name=name, in_specs=[_HBM] * (2 * m) + [_SEM, _SEM, _ANY], out_specs=[_HBM] * (2 * m),
        out_shape=_hbm_like(srcs) + _hbm_like(lands), input_output_aliases={t: t for t in range(2 * m)},
        compiler_params=_SPLIT_COPY)(*srcs, *lands, *sems, after)
    return res[:m], res[m:]


def _swap_start(name, bufs):
    n = len(bufs)

    def body(*refs):
        _sibling_handshake()
        outs = refs[n:2 * n]
        send_sems, recv_sems, token = refs[2 * n:]
        x, y, c, _, _ = _place()
        for t in range(n):
            h = outs[t].shape[0] // 2
            mine = outs[t].at[pl.ds(c * h, h)]
            _remote(mine, mine, send_sems.at[t], recv_sems.at[t], (x, y, 1 - c)).start()
        token[...] = jnp.zeros_like(token)

    res = pl.pallas_call(
        body, name=name, in_specs=[_HBM] * n, out_specs=[_HBM] * n + [_SEM, _SEM, pl.BlockSpec(memory_space=pltpu.VMEM)],
        out_shape=_hbm_like(bufs) + [pltpu.SemaphoreType.DMA((n,)), pltpu.SemaphoreType.DMA((n,)), _sds((8, 128), F32)],
        input_output_aliases={t: t for t in range(n)}, compiler_params=_SPLIT_COPY_SIBLING)(*_hbm(bufs))
    return (res[:n], (res[n], res[n + 1])), res[n + 2]


def _swap_wait(name, flight, after):
    bufs, sems = flight
    n = len(bufs)

    def body(*refs):
        ins = refs[:n]
        send_sems, recv_sems = refs[n], refs[n + 1]
        x, y, c, _, _ = _place()
        for t in range(n):
            h = ins[t].shape[0] // 2
            cp = _remote(ins[t].at[pl.ds(c * h, h)], ins[t].at[pl.ds((1 - c) * h, h)], send_sems.at[t],
                         recv_sems.at[t], (x, y, 1 - c))
            cp.wait_send()
            cp.wait_recv()

    return pl.pallas_call(
        body, name=name, in_specs=[_HBM] * n + [_SEM, _SEM, _ANY], out_specs=[_HBM] * n, out_shape=_hbm_like(bufs),
        input_output_aliases={t: t for t in range(n)}, compiler_params=_SPLIT_COPY)(*bufs, *sems, after)


def _pair_sum(name, grad, got, ids):
    _, rows, cols = got.shape
    rb = min(rows, 256)
    nb = rows // rb
    blk = pl.BlockSpec((None, rb, cols), lambda i, j, ids: (j, i, 0))
    mine = pl.BlockSpec((None, rb, cols), lambda i, j, ids: (j, ids[1] * nb + i, 0))
    own = pl.BlockSpec((rb, cols), lambda i, j, ids: (i, 0))

    def body(ids_ref, a_ref, b_ref, s_ref, sb_ref):
        s = a_ref[...] + b_ref[...]
        sb_ref[...] = s.astype(BF16)

        @pl.when(pl.program_id(1) == ids_ref[0])
        def _():
            s_ref[...] = s

    return _prefetch_call(body, name, ids, (nb, N_CHIPS), [mine, blk], [own, blk],
                          [_sds((rows, cols), F32), _sds(got.shape, BF16)], (grad, got),
                          semantics=("parallel", "arbitrary"))


def _chip_sum(name, own_sum, got, ids):
    rows, cols = own_sum.shape
    rb = min(rows, 256)
    nb = rows // rb
    own = pl.BlockSpec((rb, cols), lambda i, ids: (i, 0))
    blk3 = pl.BlockSpec((N_PEERS, rb, cols), lambda i, ids: (0, i, 0))
    out = pl.BlockSpec((rb, cols), lambda i, ids: (ids[1] * nb + i, 0))

    def body(ids_ref, a_ref, b_ref, o_ref):
        o_ref[...] = ((a_ref[...] + b_ref[0].astype(F32)) + b_ref[1].astype(F32)) + b_ref[2].astype(F32)

    return _prefetch_call(body, name, ids, (nb,), [own, blk3], out, _sds((2 * rows, cols), F32), (own_sum, got))


SMALL_RB = 280


def _small_pair_sum(own, got):
    blk = pl.BlockSpec((SMALL_RB, 128), lambda i: (i, 0))

    def body(a_ref, b_ref, o_ref):
        o_ref[...] = a_ref[...] + b_ref[...]

    return pl.pallas_call(body, name="small_pair_sum", grid=(own.shape[0] // SMALL_RB,), in_specs=[blk, blk],
                          out_specs=blk, out_shape=_sds(own.shape, F32),
                          compiler_params=_params(("parallel",)))(own, got)


def _small_chip_sum(pair, got, ids):
    nb = pair.shape[0] // 2 // SMALL_RB
    half = pl.BlockSpec((SMALL_RB, 128), lambda i, ids: (ids[1] * nb + i, 0))
    blk3 = pl.BlockSpec((N_PEERS, SMALL_RB, 128), lambda i, ids: (0, i, 0))

    def body(ids_ref, a_ref, b_ref, o_ref):
        o_ref[...] = (a_ref[...] + b_ref[1]) + (b_ref[0] + b_ref[2])

    return _prefetch_call(body, "small_chip_sum", ids, (nb,), [half, blk3], half, _sds(pair.shape, F32), (pair, got))


def _adamw_math(w, g, m, v):
    m = ADAM_B1 * m + (1.0 - ADAM_B1) * g
    v = ADAM_B2 * v + (1.0 - ADAM_B2) * (g * g)
    m_hat = m / (1.0 - ADAM_B1 ** ADAM_STEP)
    v_hat = v / (1.0 - ADAM_B2 ** ADAM_STEP)
    delta = -ADAM_LR * (m_hat / (jnp.sqrt(v_hat) + ADAM_EPS) + ADAM_WD * w)
    return delta, m, v


def _adamw(name, w, g, m, v, rb=None):
    rows, cols = w.shape
    rb = rows if rb is None else rb
    blk = pl.BlockSpec((rb, cols), lambda i: (i, 0))

    def body(w_ref, g_ref, m_ref, v_ref, d_ref, nm_ref, nv_ref):
        d, nm, nv = _adamw_math(w_ref[...], g_ref[...], m_ref[...], v_ref[...])
        d_ref[...] = d
        nm_ref[...] = nm
        nv_ref[...] = nv

    return pl.pallas_call(body, name=name, grid=(rows // rb,), in_specs=[blk] * 4, out_specs=[blk] * 3,
                          out_shape=[_sds(w.shape, F32)] * 3, compiler_params=_params(("parallel",)))(w, g, m, v)


def _adamw_small(ws, gs, ms, vs):
    n = len(ws)

    def body(*refs):
        for t in range(n):
            w_ref, g_ref, m_ref, v_ref = (refs[k * n + t] for k in range(4))
            d, nm, nv = _adamw_math(w_ref[...], g_ref[...], m_ref[...], v_ref[...])
            for k, val in enumerate((d, nm, nv)):
                refs[(4 + k) * n + t][...] = val

    res = pl.pallas_call(body, name="adamw_small", out_shape=[_sds(a.shape, F32) for a in ws] * 3,
                         compiler_params=_params())(*ws, *gs, *ms, *vs)
    return [(res[t], res[n + t], res[2 * n + t]) for t in range(n)]


BIG = ("w_in", "w_att_o", "w_rec_o", "w_out", "w_ff1", "w_ff2")
SHARDED_VECS = ("conv_w", "b_rg_a", "b_rg_i", "lru_lambda")
SMALL = ("ln1_g", "b_in", "rpb", "conv_w", "conv_b", "w_rg_a", "b_rg_a", "w_rg_i", "b_rg_i", "lru_lambda",
         "ln2_g", "lnf_g")
SMALL_ROWS = 2240
ORDER = ("ln1_g", "w_in", "b_in", "rpb", "w_att_o", "conv_w", "conv_b", "w_rg_a", "b_rg_a", "w_rg_i", "b_rg_i",
         "lru_lambda", "w_rec_o", "w_out", "ln2_g", "w_ff1", "w_ff2", "lnf_g")


def _pack_small(grads, loss):
    parts, sizes = [], {}
    for n in SMALL:
        flat = grads[n].reshape(-1)
        pad = (-flat.shape[0]) % 128
        sizes[n] = (flat.shape[0], flat.shape[0] + pad)
        parts.append(jnp.pad(flat, (0, pad)))
    total = sum(s[1] for s in sizes.values())
    parts.append(jnp.pad(loss.reshape(1), (0, SMALL_ROWS * 128 - total - 1)))
    return jnp.concatenate(parts).reshape(SMALL_ROWS, 128), sizes


def _unpack_small(buf, sizes, shapes):
    flat = buf.reshape(-1)
    out, pos = {}, 0
    for n in SMALL:
        size, padded = sizes[n]
        out[n] = flat[pos:pos + size].reshape(shapes[n])
        pos += padded
    return out, flat[pos]


def _gather_weights(w, chip):
    chip_id = chip.astype(jnp.int32).reshape(1)
    vec_rows = [w[n][0] for n in SHARDED_VECS]
    vec_shard = jnp.concatenate(vec_rows + [jnp.zeros((16 - 10, D // N_CHIPS), F32)], axis=0)
    vec_slots = lax.dynamic_update_slice(jnp.zeros((N_CHIPS, 16, D // N_CHIPS), F32), vec_shard[None], (chip, 0, 0))
    bufs_a, sems_a, token_a = _gather_start("gather_start_first", [_cast_bf16("cast_w_in", w["w_in"][0], chip_id), vec_slots])
    rest_names = BIG[1:]
    bufs_b, sems_b, token_b = _gather_start(
        "gather_start_rest", [_cast_bf16("cast_" + n, w[n][0], chip_id, after=(token_a,)) for n in rest_names])

    def first(after):
        w_in_full, vec_full = _gather_forward("gather_forward_first", _gather_wait("gather_wait_first", bufs_a, sems_a, after))
        vecs = vec_full.transpose(1, 0, 2).reshape(16, D)
        return dict(w_in=w_in_full, conv_w=vecs[0:4], b_rg_a=vecs[4:6], b_rg_i=vecs[6:8], lru_lambda=vecs[8:10])

    def rest(after):
        full = dict(zip(rest_names, _gather_forward("gather_forward_rest",
                                                    _gather_wait("gather_wait_rest", bufs_b, sems_b, after))))
        return dict(w_att_o=full["w_att_o"], w_ff1=full["w_ff1"], w_rec_o=full["w_rec_o"].reshape(D, D),
                    w_out=full["w_out"].reshape(D, D), w_ff2=full["w_ff2"].reshape(D_FF, D))

    p = dict(ln1_g=w["ln1_g"], b_in=w["b_in"], rpb=w["rpb"][0], conv_b=w["conv_b"], w_rg_a=w["w_rg_a"][0],
             w_rg_i=w["w_rg_i"][0], ln2_g=w["ln2_g"], lnf_g=w["lnf_g"].reshape(1, D))
    return p, ((token_b,), first, rest)


class _Reducer:
    def __init__(self, ids):
        self.ids = ids
        self.groups = {}

    def begin(self, tag, grads, small=None):
        names = list(grads)
        big = [grads[n].reshape(N_CHIPS, -1, grads[n].shape[-1]) for n in names]
        flight, token = _pair_start("pair_start_" + tag, big, [] if small is None else [small])
        self.groups[tag] = dict(names=names, pair=flight, small=small is not None)
        return (token,)

    def advance(self, tag, after):
        grp = self.groups[tag]
        n = len(grp["names"])
        mine, got = _pair_wait("pair_wait_" + tag, grp["pair"], n, after)
        sums = [_pair_sum("pair_sum_" + name, a, b, self.ids) for name, a, b in zip(grp["names"], mine, got)]
        small_sum = _small_pair_sum(mine[n], got[n]) if grp["small"] else None
        grp["chip"], token = _chip_start("chip_start_" + tag, [s[1] for s in sums], small_sum)
        grp["sums"] = [s[0] for s in sums]
        self.last_token = token
        return (token,)

    def finish(self, tag, after):
        grp = self.groups[tag]
        srcs, lands = _chip_wait("chip_wait_" + tag, grp["chip"], grp["small"], after)
        halves = [_chip_sum("chip_sum_" + name, s, b, self.ids) for name, s, b in zip(grp["names"], grp["sums"], lands)]
        if grp["small"]:
            halves.append(_small_chip_sum(srcs[-1], lands[-1], self.ids))
        grp["swap"], token = _swap_start("swap_start_" + tag, halves)
        return token

    def result(self, tag, after):
        return _swap_wait("swap_wait_" + tag, self.groups[tag]["swap"], after)


def kernel(x, ln1_g, w_in, b_in, rpb, w_att_o, conv_w, conv_b, w_rg_a, b_rg_a, w_rg_i, b_rg_i, lru_lambda, w_rec_o, w_out, ln2_g, w_ff1, w_ff2, lnf_g, loss_target, m_ln1_g, m_w_in, m_b_in, m_rpb, m_w_att_o, m_conv_w, m_conv_b, m_w_rg_a, m_b_rg_a, m_w_rg_i, m_b_rg_i, m_lru_lambda, m_w_rec_o, m_w_out, m_ln2_g, m_w_ff1, m_w_ff2, m_lnf_g, v_ln1_g, v_w_in, v_b_in, v_rpb, v_w_att_o, v_conv_w, v_conv_b, v_w_rg_a, v_b_rg_a, v_w_rg_i, v_b_rg_i, v_lru_lambda, v_w_rec_o, v_w_out, v_ln2_g, v_w_ff1, v_w_ff2, v_lnf_g):
    w = dict(ln1_g=ln1_g, w_in=w_in, b_in=b_in, rpb=rpb, w_att_o=w_att_o, conv_w=conv_w, conv_b=conv_b,
             w_rg_a=w_rg_a, b_rg_a=b_rg_a, w_rg_i=w_rg_i, b_rg_i=b_rg_i, lru_lambda=lru_lambda, w_rec_o=w_rec_o,
             w_out=w_out, ln2_g=ln2_g, w_ff1=w_ff1, w_ff2=w_ff2, lnf_g=lnf_g)
    m = dict(ln1_g=m_ln1_g, w_in=m_w_in, b_in=m_b_in, rpb=m_rpb, w_att_o=m_w_att_o, conv_w=m_conv_w,
             conv_b=m_conv_b, w_rg_a=m_w_rg_a, b_rg_a=m_b_rg_a, w_rg_i=m_w_rg_i, b_rg_i=m_b_rg_i,
             lru_lambda=m_lru_lambda, w_rec_o=m_w_rec_o, w_out=m_w_out, ln2_g=m_ln2_g, w_ff1=m_w_ff1,
             w_ff2=m_w_ff2, lnf_g=m_lnf_g)
    v = dict(ln1_g=v_ln1_g, w_in=v_w_in, b_in=v_b_in, rpb=v_rpb, w_att_o=v_w_att_o, conv_w=v_conv_w,
             conv_b=v_conv_b, w_rg_a=v_w_rg_a, b_rg_a=v_b_rg_a, w_rg_i=v_w_rg_i, b_rg_i=v_b_rg_i,
             lru_lambda=v_lru_lambda, w_rec_o=v_w_rec_o, w_out=v_w_out, ln2_g=v_ln2_g, w_ff1=v_w_ff1,
             w_ff2=v_w_ff2, lnf_g=v_lnf_g)
    chip = 2 * lax.axis_index("x") + lax.axis_index("y")
    ids = jnp.stack([chip, lax.axis_index("c")]).astype(jnp.int32)

    out_grad, out_delta, out_m, out_v = {}, {}, {}, {}

    def update(n, gn):
        shape, two_d = w[n].shape, gn.shape
        d, nm, nv = _adamw("adamw_" + n, w[n].reshape(two_d), gn, m[n].reshape(two_d), v[n].reshape(two_d), 256)
        out_grad[n], out_delta[n], out_m[n], out_v[n] = (gn.reshape(shape), d.reshape(shape), nm.reshape(shape),
                                                         nv.reshape(shape))
        return d

    reducer = _Reducer(ids)
    p, late = _gather_weights(w, chip)
    loss, grad_x, g = _local_step(x, loss_target, p, late, reducer)
    small, sizes = _pack_small(g, loss + reducer.last_token[:1, :1])
    after = reducer.begin("small", {}, small)[0]
    after = reducer.finish("ff", after)
    after = reducer.advance("small", after)[0]
    for tag, following in (("ff", "proj"), ("proj", "in"), ("in", None)):
        if following:
            after = reducer.finish(following, after)
        for n, red in zip(reducer.groups[tag]["names"], reducer.result(tag, after)):
            after = update(n, red)
    (small_red,) = reducer.result("small", reducer.finish("small", after))
    gsmall, loss = _unpack_small(small_red, sizes, {n: g[n].shape for n in SMALL})
    two_d = {n: (int(np.prod(w[n].shape[:-1])), w[n].shape[-1]) for n in SMALL}
    for n in SHARDED_VECS:
        gsmall[n] = lax.dynamic_slice_in_dim(gsmall[n], chip * (D // N_CHIPS), D // N_CHIPS, axis=1)
    gs = [gsmall[n].reshape(two_d[n]) for n in SMALL]
    updates = _adamw_small([w[n].reshape(two_d[n]) for n in SMALL], gs, [m[n].reshape(two_d[n]) for n in SMALL],
                           [v[n].reshape(two_d[n]) for n in SMALL])
    for n, gn, (d, nm, nv) in zip(SMALL, gs, updates):
        shape = w[n].shape
        out_grad[n], out_delta[n], out_m[n], out_v[n] = (gn.reshape(shape), d.reshape(shape), nm.reshape(shape),
                                                         nv.reshape(shape))
    return (loss, grad_x, *[out_grad[n] for n in ORDER], *[out_delta[n] for n in ORDER],
            *[out_m[n] for n in ORDER], *[out_v[n] for n in ORDER])
```

```python
import numpy as np
import jax
import jax.numpy as jnp
from jax import lax
from jax.experimental import pallas as pl
from jax.experimental.pallas import tpu as pltpu

F32 = jnp.float32
BF16 = jnp.bfloat16

T = 2048
D = 1024
D_ATT = 512
D_IN = 5632
D_FF = 4096
N_HEADS = 8
HEAD_DIM = 64
GRID_W = 64
N_ROWS = T // GRID_W
WIN_H = 8
WIN_W = 16
KEYS = WIN_H * GRID_W
N_CHIPS = 4
EPS = 1e-6
LRU_C = 8.0
SCALE = HEAD_DIM ** -0.5
REC_CB = 256
REC_CHUNK = 256
PAD = 8

ADAM_LR = 0.001
ADAM_B1 = 0.9
ADAM_B2 = 0.999
ADAM_EPS = 1e-08
ADAM_WD = 0.01
ADAM_STEP = 10

VMEM_LIMIT = 56 * 1024 * 1024

NN = (((1,), (0,)), ((), ()))
NT = (((1,), (1,)), ((), ()))
TN = (((0,), (0,)), ((), ()))
MESH = pl.DeviceIdType.MESH


def _params(sem=None):
    return pltpu.CompilerParams(dimension_semantics=sem, vmem_limit_bytes=VMEM_LIMIT)


def _dot(a, b, dims):
    return lax.dot_general(a, b, dims, preferred_element_type=F32)


def _sigmoid(x):
    return 0.5 * jnp.tanh(0.5 * x) + 0.5


def _matmul(name, a, b, *, dims, grid, a_spec, b_spec, out_shapes, out_specs, acc_shape,
            extras=(), extra_specs=(), epilogue=None, colsum_spec=None, colsum_shape=None, after=(),
            semantics=("parallel", "parallel", "arbitrary"), epilogue_takes_first=False):
    nk = grid[2]
    n_extra = len(extras)
    n_out = len(out_shapes)
    with_colsum = colsum_spec is not None

    def body(a_ref, b_ref, *rest):
        ex = rest[:n_extra]
        rest = rest[:n_extra] + rest[n_extra + len(after):]
        outs = rest[n_extra:n_extra + n_out]
        pos = n_extra + n_out
        cs_out = rest[pos] if with_colsum else None
        pos += 1 if with_colsum else 0
        acc = rest[pos]
        cs_acc = rest[pos + 1] if with_colsum else None
        k = pl.program_id(2)
        first_tile = pl.program_id(0) == 0

        @pl.when(k == 0)
        def _():
            acc[...] = jnp.zeros_like(acc)
            if with_colsum:
                cs_acc[...] = jnp.zeros_like(cs_acc)

        bv = b_ref[...]
        acc[...] += _dot(a_ref[...].astype(BF16), bv.astype(BF16), dims)
        if with_colsum:
            cs_acc[...] += jnp.sum(bv.astype(F32), axis=0, keepdims=True)

        @pl.when(k == nk - 1)
        def _():
            r = acc[...]
            if epilogue is None:
                outs[0][...] = r.astype(outs[0].dtype)
            elif epilogue_takes_first:
                epilogue(r, ex, outs, first_tile)
            else:
                epilogue(r, ex, outs)
            if with_colsum:
                cs_out[...] = cs_acc[...]

    shapes = list(out_shapes)
    specs = list(out_specs)
    scratch = [pltpu.VMEM(acc_shape, F32)]
    if with_colsum:
        shapes.append(colsum_shape)
        specs.append(colsum_spec)
        scratch.append(pltpu.VMEM((1, acc_shape[1]), F32))
    res = pl.pallas_call(
        body, name=name, grid=grid,
        in_specs=[a_spec, b_spec, *extra_specs] + [_ANY] * len(after),
        out_specs=specs, out_shape=shapes, scratch_shapes=scratch,
        compiler_params=_params(semantics),
    )(a, b, *extras, *after)
    return res


def _sds(shape, dtype):
    return jax.ShapeDtypeStruct(shape, dtype)


TM = 1024
NI = T // TM
TJ = T
NJ = T // TJ


def _mm_nn_cols(name, a, wg, out_dtype, *, bias=None, extras=(), extra_specs=(), epilogue=None,
                out_shapes=None, out_specs=None):
    k_dim, n4 = wg.shape[1], wg.shape[2]
    ex, exs = list(extras), list(extra_specs)
    if bias is not None:
        ex = [bias] + ex
        exs = [pl.BlockSpec((1, n4), lambda j, i, k: (0, j))] + exs
        user_ep = epilogue

        def epilogue(r, e, outs):
            r = r + e[0][...]
            if user_ep is None:
                outs[0][...] = r.astype(outs[0].dtype)
            else:
                user_ep(r, e[1:], outs)
    if out_shapes is None:
        out_shapes = [_sds((T, N_CHIPS * n4), out_dtype)]
        out_specs = [pl.BlockSpec((TJ, n4), lambda j, i, k: (i, j))]
    return _matmul(
        name, a, wg, dims=NN, grid=(N_CHIPS, NJ, 1),
        a_spec=pl.BlockSpec((TJ, k_dim), lambda j, i, k: (i, 0)),
        b_spec=pl.BlockSpec((None, k_dim, n4), lambda j, i, k: (j, 0, 0)),
        out_shapes=out_shapes, out_specs=out_specs, acc_shape=(TJ, n4),
        extras=ex, extra_specs=exs, epilogue=epilogue)


def _mm_nt_cols_rms_bwd(name, a, wg, x, g, dres, after=(), bf16_copy=False):
    n4 = wg.shape[2]
    row = pl.BlockSpec((TM, D), lambda i, j, k: (i, 0))
    vec = pl.BlockSpec((1, D), lambda i, j, k: (0, 0))

    def epilogue(dhv, ex, outs, first):
        x_ref, g_ref, dres_ref = ex
        dx_ref, dg_ref = outs[0], outs[-1]
        xv = x_ref[...]
        rstd = lax.rsqrt(jnp.mean(xv * xv, axis=-1, keepdims=True) + EPS)
        xhat = xv * rstd
        dy = dhv * g_ref[...]
        dx = dres_ref[...] + rstd * (dy - xhat * jnp.mean(dy * xhat, axis=-1, keepdims=True))
        dx_ref[...] = dx
        if bf16_copy:
            outs[1][...] = dx.astype(BF16)
        part = jnp.sum(dhv * xhat, axis=0, keepdims=True)

        @pl.when(first)
        def _():
            dg_ref[...] = part

        @pl.when(jnp.logical_not(first))
        def _():
            dg_ref[...] += part

    return _matmul(
        name, a, wg, dims=NT, grid=(NI, 1, N_CHIPS),
        a_spec=pl.BlockSpec((TM, n4), lambda i, j, k: (i, k)),
        b_spec=pl.BlockSpec((None, D, n4), lambda i, j, k: (k, 0, 0)),
        out_shapes=[_sds((T, D), F32)] + [_sds((T, D), BF16)] * bf16_copy + [_sds((1, D), F32)],
        out_specs=[row] + [row] * bf16_copy + [vec], acc_shape=(TM, D),
        extras=[x, g, dres], extra_specs=[row, vec, row], epilogue=epilogue, after=after,
        semantics=("arbitrary", "arbitrary", "arbitrary"), epilogue_takes_first=True)


def _mm_nt_rows(name, a, w, out_dtype, *, tn, extras=(), extra_specs=(), epilogue=None):
    k_dim, n = w.shape
    return _matmul(
        name, a, w, dims=NT, grid=(k_dim // tn, NJ, 1),
        a_spec=pl.BlockSpec((TJ, n), lambda j, i, k: (i, 0)),
        b_spec=pl.BlockSpec((tn, n), lambda j, i, k: (j, 0)),
        out_shapes=[_sds((T, k_dim), out_dtype)],
        out_specs=[pl.BlockSpec((TJ, tn), lambda j, i, k: (i, j))], acc_shape=(TJ, tn),
        extras=extras, extra_specs=extra_specs, epilogue=epilogue)


def _mm_tn_cols(name, a, g, n4, *, colsum=False):
    k_dim = a.shape[1]
    kw = {}
    if colsum:
        kw = dict(colsum_spec=pl.BlockSpec((1, n4), lambda j, i, k: (0, j)),
                  colsum_shape=_sds((1, N_CHIPS * n4), F32))
    return _matmul(
        name, a, g, dims=TN, grid=(N_CHIPS, 1, NJ),
        a_spec=pl.BlockSpec((TJ, k_dim), lambda j, i, k: (k, 0)),
        b_spec=pl.BlockSpec((TJ, n4), lambda j, i, k: (k, j)),
        out_shapes=[_sds((N_CHIPS, k_dim, n4), F32)],
        out_specs=[pl.BlockSpec((None, k_dim, n4), lambda j, i, k: (j, 0, 0))],
        acc_shape=(k_dim, n4), **kw)


def _mm_tn_rows(name, a, g, *, tm):
    k_dim, n = a.shape[1], g.shape[1]
    return _matmul(
        name, a, g, dims=TN, grid=(k_dim // tm, 1, NJ),
        a_spec=pl.BlockSpec((TJ, tm), lambda j, i, k: (k, j)),
        b_spec=pl.BlockSpec((TJ, n), lambda j, i, k: (k, 0)),
        out_shapes=[_sds((k_dim, n), F32)],
        out_specs=[pl.BlockSpec((tm, n), lambda j, i, k: (j, 0))], acc_shape=(tm, n))


TE = 256
NE = T // TE
_ROW = pl.BlockSpec((TE, D), lambda i: (i, 0))
_VEC = pl.BlockSpec((1, D), lambda i: (0, 0))


def _rms_fwd(name, x, g, after=()):
    def body(x_ref, g_ref, *rest):
        h_ref = rest[-1]
        xv = x_ref[...]
        rstd = lax.rsqrt(jnp.mean(xv * xv, axis=-1, keepdims=True) + EPS)
        h_ref[...] = (xv * rstd * g_ref[...]).astype(BF16)

    return pl.pallas_call(body, name=name, grid=(NE,), in_specs=[_ROW, _VEC] + [_ANY] * len(after), out_specs=_ROW,
                          out_shape=_sds((T, D), BF16), compiler_params=_params(("parallel",)))(x, g, *after)


def _mm_x2_loss_head(s, w_ff2, x1, target, g):
    k_dim = w_ff2.shape[0]
    row = pl.BlockSpec((TM, D), lambda i, j, k: (i, 0))
    vec = pl.BlockSpec((1, D), lambda i, j, k: (0, 0))

    def epilogue(r, ex, outs, first):
        x1_ref, t_ref, g_ref = ex
        loss_ref, dx_ref, dxb_ref, dg_ref = outs
        xv = x1_ref[...] + r
        rstd = lax.rsqrt(jnp.mean(xv * xv, axis=-1, keepdims=True) + EPS)
        xhat = xv * rstd
        gv = g_ref[...]
        err = xhat * gv - t_ref[...]
        dy = err * (1.0 / D)
        dxh = dy * gv
        dx = rstd * (dxh - xhat * jnp.mean(dxh * xhat, axis=-1, keepdims=True))
        dx_ref[...] = dx
        dxb_ref[...] = dx.astype(BF16)
        dg_part = jnp.sum(dy * xhat, axis=0, keepdims=True)
        loss_part = (0.5 / D) * jnp.sum(jnp.sum(err * err, axis=1, keepdims=True), axis=0, keepdims=True)

        @pl.when(first)
        def _():
            dg_ref[...] = dg_part
            loss_ref[...] = loss_part

        @pl.when(jnp.logical_not(first))
        def _():
            dg_ref[...] += dg_part
            loss_ref[...] += loss_part

    return _matmul(
        "mm_x2_loss_head", s, w_ff2, dims=NN, grid=(NI, 1, k_dim // D),
        a_spec=pl.BlockSpec((TM, D), lambda i, j, k: (i, k)), b_spec=pl.BlockSpec((D, D), lambda i, j, k: (k, 0)),
        out_shapes=[_sds((1, 1), F32), _sds((T, D), F32), _sds((T, D), BF16), _sds((1, D), F32)],
        out_specs=[pl.BlockSpec((1, 1), lambda i, j, k: (0, 0)), row, row, vec], acc_shape=(TM, D),
        extras=[x1, target, g], extra_specs=[row, row, vec], epilogue=epilogue,
        semantics=("arbitrary", "arbitrary", "arbitrary"), epilogue_takes_first=True)


MW = 512
_G_ATT_BLK = 3584 // MW
_G_REC_BLK = 4608 // MW


TB = 512


def _branch_specs():
    def row(cols):
        return pl.BlockSpec((TB, cols), lambda i: (i, 0))

    ga = pl.BlockSpec((TB, MW), lambda i: (i, _G_ATT_BLK))
    ga2 = pl.BlockSpec((TB, MW), lambda i: (i, _G_ATT_BLK + 1))
    gr = pl.BlockSpec((TB, MW), lambda i: (i, _G_REC_BLK))
    gr2 = pl.BlockSpec((TB, MW), lambda i: (i, _G_REC_BLK + 1))
    w_att = pl.BlockSpec((N_CHIPS, D_ATT, D // N_CHIPS), lambda i: (0, 0, 0))
    w_sq = pl.BlockSpec((D, D), lambda i: (0, 0))
    return row, (ga, ga2, gr, gr2), w_att, w_sq


def _gate_values(gate_refs):
    ga, ga2, gr, gr2 = (r[...] for r in gate_refs)
    return _sigmoid(jnp.concatenate([ga, ga2], axis=1)), _sigmoid(jnp.concatenate([gr, gr2], axis=1))


def _branches_fwd(att, g, z, x, w_att_o, w_rec_o, w_out, ln2_g):
    row, gate_specs, w_att, w_sq = _branch_specs()

    def body(att_ref, g_ref, ga_ref, ga2_ref, gr_ref, gr2_ref, x_ref, wa_ref, wr_ref, wo_ref, g2_ref,
             ya_ref, yr_ref, m_ref, x1_ref, h2_ref):
        attv = att_ref[...]
        ya = jnp.concatenate([_dot(attv, wa_ref[j], NN) for j in range(N_CHIPS)], axis=1)
        yr = _dot(g_ref[...], wr_ref[...], NN)
        sa, sr = _gate_values((ga_ref, ga2_ref, gr_ref, gr2_ref))
        mixed = (sa * ya + sr * yr).astype(BF16)
        ya_ref[...] = ya
        yr_ref[...] = yr
        m_ref[...] = mixed
        x1 = x_ref[...] + _dot(mixed, wo_ref[...], NN)
        x1_ref[...] = x1
        rstd = lax.rsqrt(jnp.mean(x1 * x1, axis=-1, keepdims=True) + EPS)
        h2_ref[...] = (x1 * rstd * g2_ref[...]).astype(BF16)

    return pl.pallas_call(
        body, name="branches_fwd", grid=(T // TB,),
        in_specs=[row(D_ATT), row(D), *gate_specs, row(D), w_att, w_sq, w_sq, pl.BlockSpec((1, D), lambda i: (0, 0))],
        out_specs=[row(D)] * 5,
        out_shape=[_sds((T, D), F32), _sds((T, D), F32), _sds((T, D), BF16), _sds((T, D), F32), _sds((T, D), BF16)],
        compiler_params=_params(("parallel",)))(att, g, z, z, z, z, x, w_att_o, w_rec_o, w_out, ln2_g)


def _branches_bwd(dx1_b, y_att, y_rec, z, w_att_o, w_rec_o, w_out):
    row, gate_specs, w_att, w_sq = _branch_specs()
    n4 = D // N_CHIPS

    def body(dx_ref, ya_ref, yr_ref, ga_ref, ga2_ref, gr_ref, gr2_ref, wa_ref, wr_ref, wo_ref,
             dya_ref, dyr_ref, dga_ref, dgr_ref, datt_ref, dg_ref):
        dm = _dot(dx_ref[...], wo_ref[...], NT)
        sa, sr = _gate_values((ga_ref, ga2_ref, gr_ref, gr2_ref))
        dya = (dm * sa).astype(BF16)
        dyr = (dm * sr).astype(BF16)
        dya_ref[...] = dya
        dyr_ref[...] = dyr
        dga_ref[...] = (dm * ya_ref[...] * sa * (1.0 - sa)).astype(BF16)
        dgr_ref[...] = (dm * yr_ref[...] * sr * (1.0 - sr)).astype(BF16)
        datt = _dot(dya[:, 0:n4], wa_ref[0], NT)
        for j in range(1, N_CHIPS):
            datt = datt + _dot(dya[:, j * n4:(j + 1) * n4], wa_ref[j], NT)
        datt_ref[...] = datt.astype(BF16)
        dg_ref[...] = _dot(dyr, wr_ref[...], NT).astype(BF16)

    return pl.pallas_call(
        body, name="branches_bwd", grid=(T // TB,),
        in_specs=[row(D), row(D), row(D), *gate_specs, w_att, w_sq, w_sq],
        out_specs=[row(D)] * 4 + [row(D_ATT), row(D)],
        out_shape=[_sds((T, D), BF16)] * 4 + [_sds((T, D_ATT), BF16), _sds((T, D), BF16)],
        compiler_params=_params(("parallel",)))(dx1_b, y_att, y_rec, z, z, z, z, w_att_o, w_rec_o, w_out)


HP = 2 * HEAD_DIM
N_HP = N_HEADS // 2
ATT_UNROLL_FWD = 16
ATT_UNROLL_BWD = 8
DIAG_ROWS = 32


def _window_maps():
    diag = np.zeros((GRID_W * GRID_W, 128), np.float32)
    for qc in range(GRID_W):
        w0 = min(max(qc - WIN_W // 2, 0), GRID_W - WIN_W)
        for kc in range(w0, w0 + WIN_W):
            diag[qc * GRID_W + kc, kc - qc + WIN_W - 1] = 1.0
    return diag, diag.sum(axis=1)[None, :]


def _split3(x):
    a = x.astype(BF16)
    r = x - a.astype(F32)
    b = r.astype(BF16)
    c = (r - b.astype(F32)).astype(BF16)
    return a, b, c


N_DROW = 2 * WIN_H - 1
N_DPAIR = N_DROW - 1


def _bias_pairs(rpb):
    diag, valid = _window_maps()
    r2 = jnp.pad(rpb.reshape(N_HEADS * N_DROW, 2 * WIN_W - 1),
                 ((0, 128 - N_HEADS * N_DROW), (0, 128 - (2 * WIN_W - 1))))

    def body(r_ref, d_ref, v_ref, o_ref):
        dv = d_ref[...]
        t = sum(_dot(part, dv, NN) for part in _split3(r_ref[...]))
        o_ref[...] = jnp.where(v_ref[...] > 0.0, t, -1e30)

    t = pl.pallas_call(body, name="rpb_expand", out_shape=_sds((128, GRID_W * GRID_W), F32),
                       compiler_params=_params())(r2, jnp.asarray(diag.T, BF16), jnp.asarray(valid, F32))
    t = t[:N_HEADS * N_DROW].reshape(N_HEADS, N_DROW, GRID_W, GRID_W)
    return jnp.concatenate([t[:, :N_DPAIR], t[:, 1:]], axis=-1)


def _row_bias(tb_ref, hh, d0):
    return jnp.concatenate([tb_ref[hh, d0 + 2 * ii] for ii in range(WIN_H // 2)], axis=1)


def _row_window(r):
    rs = jnp.clip(r - WIN_H // 2, 0, N_ROWS - WIN_H)
    return pl.multiple_of(r * GRID_W, GRID_W), pl.multiple_of(rs * GRID_W, GRID_W), rs - r + (WIN_H - 1)


def _split_heads(src_ref, dst_ref, scale=None):
    for hh in range(2):
        v = src_ref[:, hh * HEAD_DIM:(hh + 1) * HEAD_DIM]
        dst_ref[hh] = (v if scale is None else v * scale).astype(BF16)


def _attn_items(qb_ref, kb_ref, vb_ref, tb_ref, first_row, n_rows):
    wins = [_row_window(first_row + u) for u in range(n_rows)]
    items = [(u, hh) for u in range(n_rows) for hh in range(2)]
    q = [qb_ref[hh, pl.ds(wins[u][0], GRID_W), :] for u, hh in items]
    k = [kb_ref[hh, pl.ds(wins[u][1], KEYS), :] for u, hh in items]
    v = [vb_ref[hh, pl.ds(wins[u][1], KEYS), :] for u, hh in items]
    s = [_dot(qi, ki, NT) + _row_bias(tb_ref, hh, wins[u][2]) for qi, ki, (u, hh) in zip(q, k, items)]
    m = [jnp.max(si, axis=-1, keepdims=True) for si in s]
    e = [jnp.exp(si - mi) for si, mi in zip(s, m)]
    inv = [1.0 / jnp.sum(ei, axis=-1, keepdims=True) for ei in e]
    p = [ei * li for ei, li in zip(e, inv)]
    return wins, items, q, k, v, p


def _attn_in_specs():
    q = pl.BlockSpec((T, HP), lambda p: (0, p))
    k = pl.BlockSpec((T, HP), lambda p: (0, N_HP + p))
    v = pl.BlockSpec((T, HP), lambda p: (0, 2 * N_HP + p))
    tb = pl.BlockSpec((2, N_DPAIR, GRID_W, HP), lambda p: (p, 0, 0, 0))
    return q, k, v, tb


_HEAD_SCRATCH = pltpu.VMEM((2, T, HEAD_DIM), BF16)


_PROBS = pl.BlockSpec((T, 2 * KEYS), lambda p: (0, p))


def _attn_fwd(z, tb):
    def body(q_ref, k_ref, v_ref, tb_ref, o_ref, p_ref, qb_ref, kb_ref, vb_ref):
        _split_heads(q_ref, qb_ref, SCALE)
        _split_heads(k_ref, kb_ref)
        _split_heads(v_ref, vb_ref)

        def rows(it, carry):
            wins, items, _, _, v, p = _attn_items(qb_ref, kb_ref, vb_ref, tb_ref, it * ATT_UNROLL_FWD, ATT_UNROLL_FWD)
            pb = [pi.astype(BF16) for pi in p]
            o = [_dot(pi, vi, NN) for pi, vi in zip(pb, v)]
            for u, (q0, _, _) in enumerate(wins):
                o_ref[pl.ds(q0, GRID_W), :] = jnp.concatenate(o[2 * u:2 * u + 2], axis=1).astype(BF16)
                p_ref[pl.ds(q0, GRID_W), :] = jnp.concatenate(pb[2 * u:2 * u + 2], axis=1)
            return carry

        lax.fori_loop(0, N_ROWS // ATT_UNROLL_FWD, rows, 0)

    blk = pl.BlockSpec((T, HP), lambda p: (0, p))
    return pl.pallas_call(
        body, name="attn_fwd", grid=(N_HP,), in_specs=list(_attn_in_specs()), out_specs=[blk, _PROBS],
        out_shape=[_sds((T, D_ATT), BF16), _sds((T, N_HEADS * KEYS), BF16)], scratch_shapes=[_HEAD_SCRATCH] * 3,
        compiler_params=_params(("parallel",)))(z, z, z, tb)


def _attn_bwd(z, probs, d_att, after=()):
    def body(q_ref, k_ref, v_ref, p_ref, do_ref, flip_ref, *rest):
        (dq_ref, dk_ref, dv_ref, diag_ref, qb_ref, kb_ref, vb_ref, dob_ref, dka_ref, dva_ref,
         ds_ref) = rest[len(after):]
        _split_heads(q_ref, qb_ref, SCALE)
        _split_heads(k_ref, kb_ref)
        _split_heads(v_ref, vb_ref)
        _split_heads(do_ref, dob_ref)
        dka_ref[...] = jnp.zeros_like(dka_ref)
        dva_ref[...] = jnp.zeros_like(dva_ref)
        ds_ref[...] = jnp.zeros_like(ds_ref)

        def rows(it, carry):
            wins = [_row_window(it * ATT_UNROLL_BWD + u) for u in range(ATT_UNROLL_BWD)]
            items = [(u, hh) for u in range(ATT_UNROLL_BWD) for hh in range(2)]
            q = [qb_ref[hh, pl.ds(wins[u][0], GRID_W), :] for u, hh in items]
            k = [kb_ref[hh, pl.ds(wins[u][1], KEYS), :] for u, hh in items]
            v = [vb_ref[hh, pl.ds(wins[u][1], KEYS), :] for u, hh in items]
            pb = [p_ref[pl.ds(wins[u][0], GRID_W), hh * KEYS:(hh + 1) * KEYS] for u, hh in items]
            p = [pi.astype(F32) for pi in pb]
            do = [dob_ref[hh, pl.ds(wins[u][0], GRID_W), :] for u, hh in items]
            dv = [_dot(pi, di, TN) for pi, di in zip(pb, do)]
            dp = [_dot(di, vi, NT) for di, vi in zip(do, v)]
            ds = [pi * (dpi - jnp.sum(dpi * pi, axis=-1, keepdims=True)) for pi, dpi in zip(p, dp)]
            dsb = [d.astype(BF16) for d in ds]
            dq = [_dot(d, ki, NN) * SCALE for d, ki in zip(dsb, k)]
            dk = [_dot(d, qi, TN) for d, qi in zip(dsb, q)]
            for d, (u, hh) in zip(ds, items):
                for ii in range(WIN_H // 2):
                    ds_ref[hh, wins[u][2] + 2 * ii] += d[:, ii * HP:(ii + 1) * HP]
            for u, (q0, k0, _) in enumerate(wins):
                dq_ref[pl.ds(q0, GRID_W), :] = jnp.concatenate(dq[2 * u:2 * u + 2], axis=1).astype(BF16)
                dka_ref[pl.ds(k0, KEYS), :] += jnp.concatenate(dk[2 * u:2 * u + 2], axis=1)
                dva_ref[pl.ds(k0, KEYS), :] += jnp.concatenate(dv[2 * u:2 * u + 2], axis=1)
            return carry

        lax.fori_loop(0, N_ROWS // ATT_UNROLL_BWD, rows, 0)
        dk_ref[...] = dka_ref[...].astype(BF16)
        dv_ref[...] = dva_ref[...].astype(BF16)
        _diag_sums(ds_ref, flip_ref, diag_ref)

    blk = pl.BlockSpec((T, HP), lambda p: (0, p))
    q, k, v, _ = _attn_in_specs()
    flip =jnp.asarray(np.eye(HP, dtype=np.float32)[::-1], BF16)
    return pl.pallas_call(
        body, name="attn_bwd", grid=(N_HP,),
        in_specs=[q, k, v, _PROBS, blk, pl.BlockSpec((HP, HP), lambda p: (0, 0))] + [_ANY] * len(after),
        out_specs=[blk, blk, blk, pl.BlockSpec((None, DIAG_ROWS, HP), lambda p: (p, 0, 0))],
        out_shape=[_sds((T, D_ATT), BF16)] * 3 + [_sds((N_HP, DIAG_ROWS, HP), F32)],
        scratch_shapes=[_HEAD_SCRATCH] * 4 + [pltpu.VMEM((T, HP), F32), pltpu.VMEM((T, HP), F32),
                                              pltpu.VMEM((2, N_DPAIR, GRID_W, HP), F32)],
        compiler_params=_params(("parallel",)))(z, z, z, probs, d_att, flip, *after)


def _diag_sums(acc_ref, flip_ref, out_ref):
    flip = flip_ref[...]
    rows = []
    for hh in range(2):
        for pair in range(N_DPAIR):
            reversed_lanes = sum(_dot(part, flip, NN) for part in _split3(acc_ref[hh, pair]))
            skewed = pltpu.roll(reversed_lanes, 0, 1, stride=1, stride_axis=0)
            rows.append(jnp.sum(skewed, axis=0, keepdims=True))
    rows.append(jnp.zeros((DIAG_ROWS - len(rows), HP), F32))
    out_ref[...] = jnp.concatenate(rows, axis=0)


def _rpb_grad(diag_sums):
    g = diag_sums.reshape(N_HP * DIAG_ROWS, HP)
    sel = np.zeros((2, 128, N_HP * DIAG_ROWS), np.float32)
    lane = np.zeros((2, HP, 128), np.float32)
    for h in range(N_HEADS):
        for pair in range(N_DPAIR):
            for half in range(2):
                sel[half, h * N_DROW + pair + half, (h // 2) * DIAG_ROWS + (h % 2) * N_DPAIR + pair] = 1.0
    for j in range(2 * WIN_W - 1):
        for half in range(2):
            lane[half, (HP - 1 - GRID_W * half - (j - (WIN_W - 1))) % HP, j] = 1.0

    def body(g_ref, sel_ref, lane_ref, o_ref):
        parts = _split3(g_ref[...])
        total = None
        for half in range(2):
            picked = sum(_dot(sel_ref[half], part, NN) for part in parts)
            term = sum(_dot(part, lane_ref[half], NN) for part in _split3(picked))
            total = term if total is None else total + term
        o_ref[...] = total

    out = pl.pallas_call(body, name="rpb_grad", out_shape=_sds((128, 128), F32),
                         compiler_params=_params())(g, jnp.asarray(sel, BF16), jnp.asarray(lane, BF16))
    return out[:N_HEADS * N_DROW, :2 * WIN_W - 1].reshape(N_HEADS, N_DROW, 2 * WIN_W - 1)


N_CB = D // REC_CB
N_CHUNK = T // REC_CHUNK
N_TILE = T // 8
_U_BLK = 1536 // REC_CB
_Y_BLK = 2560 // REC_CB


def _block_diag(w):
    per = REC_CB // 64
    wt = w.reshape(2, N_CB, per, 64, 64)
    eye = jnp.eye(per, dtype=w.dtype)
    full = wt[:, :, :, :, None, :] * eye[None, None, :, None, :, None]
    return full.reshape(2, N_CB, REC_CB, REC_CB).astype(BF16)


def _block_diag_grad(g):
    per = REC_CB // 64
    g6 = g.reshape(2, N_CB, per, 64, per, 64)
    return jnp.stack([g6[:, :, p, :, p, :] for p in range(per)], axis=2).reshape(2, 16, 64, 64)


def _gelu(x):
    c = 0.7978845608028654
    return 0.5 * x * (1.0 + jnp.tanh(c * (x + 0.044715 * x * x * x)))


def _gelu_grad(x):
    c = 0.7978845608028654
    th = jnp.tanh(c * (x + 0.044715 * x * x * x))
    return 0.5 * (1.0 + th) + 0.5 * x * (1.0 - th * th) * c * (1.0 + 3.0 * 0.044715 * x * x)


def _softplus_neg(lam):
    x = -lam
    e = jnp.exp(-jnp.abs(x))
    w = 1.0 + e
    l1p = jnp.where(w == 1.0, e, jnp.log(w) * e / (w - 1.0))
    return jnp.maximum(x, 0.0) + l1p


def _one_minus_exp(x):
    poly = x * (1.0 + x * (1 / 2 + x * (1 / 6 + x * (1 / 24 + x * (1 / 120 + x * (1 / 720))))))
    return jnp.where(x > -0.125, -poly, 1.0 - jnp.exp(x))


def _conv_taps(pad_ref, t0, w, sign):
    out = None
    for j in range(4):
        term = w[j:j + 1, :] * pad_ref[pl.ds(PAD + t0 + sign * (j - 2), REC_CHUNK), :]
        out = term if out is None else out + term
    return out


def _gates(u, wa, wi, ba, bi, sp):
    ub = u.astype(BF16)
    r = _sigmoid(_dot(ub, wa, NN) + ba)
    i = _sigmoid(_dot(ub, wi, NN) + bi)
    log_a = -LRU_C * r * sp
    a = jnp.exp(log_a)
    x = jnp.maximum(_one_minus_exp(2.0 * log_a), 0.0)
    positive = x > 0.0
    inv = lax.rsqrt(jnp.where(positive, x, 1.0))
    mult = jnp.where(positive, x * inv, 0.0)
    return r, i, a, mult, jnp.where(positive, inv, 0.0)


def _tile_scan(a, b, sub, reverse):
    for s in (1, 2, 4):
        if reverse:
            a_s, b_s, m = pltpu.roll(a, 8 - s, 0), pltpu.roll(b, 8 - s, 0), sub < 8 - s
        else:
            a_s, b_s, m = pltpu.roll(a, s, 0), pltpu.roll(b, s, 0), sub >= s
        b = jnp.where(m, a * b_s + b, b)
        a = jnp.where(m, a * a_s, a)
    return a, b


def _last_row(x, row):
    return jnp.broadcast_to(x[row:row + 1, :], x.shape)


def _rec_prologue(up_ref, cw_ref, cb_ref, wa_ref, wi_ref, ba_ref, bi_ref, lam_ref,
                  upad_ref, u_ref, a_refs, h_refs):
    cb = up_ref.shape[1]
    zeros = jnp.zeros((PAD, cb), F32)
    upad_ref[pl.ds(0, PAD), :] = zeros
    upad_ref[pl.ds(PAD + T, PAD), :] = zeros
    upad_ref[pl.ds(PAD, T), :] = up_ref[...]
    cw = cw_ref[...]
    sp = _softplus_neg(lam_ref[...])
    for c in range(N_CHUNK):
        t0 = c * REC_CHUNK
        u = cb_ref[...] + _conv_taps(upad_ref, t0, cw, 1)
        u_ref[pl.ds(t0, REC_CHUNK), :] = u
        for d in range(2):
            _, i, a, mult, _ = _gates(u, wa_ref[d], wi_ref[d], ba_ref[d:d + 1, :], bi_ref[d:d + 1, :], sp[d:d + 1, :])
            a_refs[d][pl.ds(t0, REC_CHUNK), :] = a
            h_refs[d][pl.ds(t0, REC_CHUNK), :] = mult * (i * u)

    sub = lax.broadcasted_iota(jnp.int32, (8, cb), 0)

    def tile(k, carry):
        cf, cr = carry
        tf = pl.multiple_of(k * 8, 8)
        tr = pl.multiple_of((N_TILE - 1 - k) * 8, 8)
        af, bf = _tile_scan(a_refs[0][pl.ds(tf, 8), :], h_refs[0][pl.ds(tf, 8), :], sub, False)
        hf = af * cf + bf
        h_refs[0][pl.ds(tf, 8), :] = hf
        ar, br = _tile_scan(a_refs[1][pl.ds(tr, 8), :], h_refs[1][pl.ds(tr, 8), :], sub, True)
        hr = ar * cr + br
        h_refs[1][pl.ds(tr, 8), :] = hr
        return _last_row(af, 7) * cf + _last_row(bf, 7), _last_row(ar, 0) * cr + _last_row(br, 0)

    z8 = jnp.zeros((8, cb), F32)
    lax.fori_loop(0, N_TILE, tile, (z8, z8))
    return sp


def _rec_specs():
    up = pl.BlockSpec((T, REC_CB), lambda c: (0, _U_BLK + c))
    yb = pl.BlockSpec((T, REC_CB), lambda c: (0, _Y_BLK + c))
    cw = pl.BlockSpec((4, REC_CB), lambda c: (0, c))
    cbias = pl.BlockSpec((1, REC_CB), lambda c: (0, c))
    wbd = pl.BlockSpec((2, None, REC_CB, REC_CB), lambda c: (0, c, 0, 0))
    vec2 = pl.BlockSpec((2, REC_CB), lambda c: (0, c))
    col = pl.BlockSpec((T, REC_CB), lambda c: (0, c))
    return up, yb, cw, cbias, wbd, vec2, col


def _rec_fwd(z, conv_w, conv_b, wa, wi, ba, bi, lam):
    up, yb, cw, cbias, wbd, vec2, col = _rec_specs()

    def body(up_ref, yb_ref, cw_ref, cb_ref, wa_ref, wi_ref, ba_ref, bi_ref, lam_ref, g_ref,
             u_ref, af_ref, ar_ref, hf_ref, hr_ref, upad_ref):
        _rec_prologue(up_ref, cw_ref, cb_ref, wa_ref, wi_ref, ba_ref, bi_ref, lam_ref,
                      upad_ref, u_ref, (af_ref, ar_ref), (hf_ref, hr_ref))

        def chunk(c, carry):
            t0 = pl.multiple_of(c * REC_CHUNK, REC_CHUNK)
            rows = pl.ds(t0, REC_CHUNK)
            g_ref[rows, :] = ((hf_ref[rows, :] + hr_ref[rows, :]) * _gelu(yb_ref[rows, :])).astype(BF16)
            return carry

        lax.fori_loop(0, N_CHUNK, chunk, 0)

    res = pl.pallas_call(
        body, name="rec_fwd", grid=(N_CB,),
        in_specs=[up, yb, cw, cbias, wbd, wbd, vec2, vec2, vec2], out_specs=[col] * 6,
        out_shape=[_sds((T, D), BF16)] + [_sds((T, D), F32)] * 5,
        scratch_shapes=[pltpu.VMEM((T + 2 * PAD, REC_CB), F32)],
        compiler_params=_params(("parallel",)))(z, z, conv_w, conv_b, wa, wi, ba, bi, lam)
    return res[0], tuple(res[1:])


def _rec_bwd(z, dg, saved, conv_w, conv_b, wa, wi, ba, bi, lam, after=()):
    up, yb, cw, cbias, wbd, vec2, col = _rec_specs()

    def body(up_ref, yb_ref, dg_ref, u_ref, af_ref, ar_ref, hf_ref, hr_ref,
             cw_ref, cb_ref, wa_ref, wi_ref, ba_ref, bi_ref, lam_ref, *rest):
        (dup_ref, dyb_ref, dcw_ref, dcb_ref, dwa_ref, dwi_ref, dba_ref, dbi_ref, dlam_ref,
         upad_ref, dh_ref, gf_ref, gr_ref, daf_ref, dar_ref, dupad_ref) = rest[len(after):]
        g_refs, da_refs = (gf_ref, gr_ref), (daf_ref, dar_ref)
        cb = up_ref.shape[1]
        zeros = jnp.zeros((PAD, cb), F32)
        upad_ref[pl.ds(0, PAD), :] = zeros
        upad_ref[pl.ds(PAD + T, PAD), :] = zeros
        upad_ref[pl.ds(PAD, T), :] = up_ref[...]
        sp = _softplus_neg(lam_ref[...])

        def gate_chunk(c, carry):
            t0 = pl.multiple_of(c * REC_CHUNK, REC_CHUNK)
            rows = pl.ds(t0, REC_CHUNK)
            y = yb_ref[rows, :]
            dgv = dg_ref[rows, :].astype(F32)
            dh_ref[rows, :] = dgv * _gelu(y)
            dyb_ref[rows, :] = (dgv * (hf_ref[rows, :] + hr_ref[rows, :]) * _gelu_grad(y)).astype(BF16)
            return carry

        lax.fori_loop(0, N_CHUNK, gate_chunk, 0)

        sub = lax.broadcasted_iota(jnp.int32, (8, cb), 0)

        def tile(k, carry):
            cf, cr = carry
            kf = N_TILE - 1 - k
            tf = pl.multiple_of(kf * 8, 8)
            tnext = pl.multiple_of(jnp.minimum(kf + 1, N_TILE - 1) * 8, 8)
            tprev = pl.multiple_of(jnp.maximum(kf - 1, 0) * 8, 8)
            a_t = af_ref[pl.ds(tf, 8), :]
            a_n = jnp.where(kf < N_TILE - 1, af_ref[pl.ds(tnext, 8), :], 0.0)
            a_sh = jnp.where(sub == 7, pltpu.roll(a_n, 7, 0), pltpu.roll(a_t, 7, 0))
            ca, cbb = _tile_scan(a_sh, dh_ref[pl.ds(tf, 8), :], sub, True)
            gf = ca * cf + cbb
            h_t = hf_ref[pl.ds(tf, 8), :]
            h_p = jnp.where(kf > 0, hf_ref[pl.ds(tprev, 8), :], 0.0)
            h_sh = jnp.where(sub == 0, pltpu.roll(h_p, 1, 0), pltpu.roll(h_t, 1, 0))
            gf_ref[pl.ds(tf, 8), :] = gf
            daf_ref[pl.ds(tf, 8), :] = gf * h_sh
            tr = pl.multiple_of(k * 8, 8)
            rnext = pl.multiple_of(jnp.minimum(k + 1, N_TILE - 1) * 8, 8)
            rprev = pl.multiple_of(jnp.maximum(k - 1, 0) * 8, 8)
            b_t = ar_ref[pl.ds(tr, 8), :]
            b_p = jnp.where(k > 0, ar_ref[pl.ds(rprev, 8), :], 0.0)
            b_sh = jnp.where(sub == 0, pltpu.roll(b_p, 1, 0), pltpu.roll(b_t, 1, 0))
            ra, rb = _tile_scan(b_sh, dh_ref[pl.ds(tr, 8), :], sub, False)
            gr = ra * cr + rb
            hr_t = hr_ref[pl.ds(tr, 8), :]
            hr_n = jnp.where(k < N_TILE - 1, hr_ref[pl.ds(rnext, 8), :], 0.0)
            hr_sh = jnp.where(sub == 7, pltpu.roll(hr_n, 7, 0), pltpu.roll(hr_t, 7, 0))
            gr_ref[pl.ds(tr, 8), :] = gr
            dar_ref[pl.ds(tr, 8), :] = gr * hr_sh
            return _last_row(gf, 0), _last_row(gr, 7)

        z8 = jnp.zeros((8, cb), F32)
        lax.fori_loop(0, N_TILE, tile, (z8, z8))

        dupad_ref[pl.ds(0, PAD), :] = zeros
        dupad_ref[pl.ds(PAD + T, PAD), :] = zeros
        dwa_ref[...] = jnp.zeros_like(dwa_ref)
        dwi_ref[...] = jnp.zeros_like(dwi_ref)
        dba_ref[...] = jnp.zeros_like(dba_ref)
        dbi_ref[...] = jnp.zeros_like(dbi_ref)
        dlam_ref[...] = jnp.zeros_like(dlam_ref)

        def grad_chunk(c, carry):
            t0 = pl.multiple_of(c * REC_CHUNK, REC_CHUNK)
            rows = pl.ds(t0, REC_CHUNK)
            u = u_ref[rows, :]
            ub = u.astype(BF16)
            du = jnp.zeros((REC_CHUNK, cb), F32)
            for d in range(2):
                r, i, a, mult, inv_mult = _gates(u, wa_ref[d], wi_ref[d], ba_ref[d:d + 1, :], bi_ref[d:d + 1, :],
                                                 sp[d:d + 1, :])
                dbx = g_refs[d][rows, :]
                dmult = dbx * (i * u)
                diu = dbx * mult
                a2 = a * a
                dlog = da_refs[d][rows, :] * a - dmult * (a2 * inv_mult)
                dpa = (dlog * (-LRU_C) * sp[d:d + 1, :]) * r * (1.0 - r)
                dpi = (diu * u) * i * (1.0 - i)
                dpab, dpib = dpa.astype(BF16), dpi.astype(BF16)
                du = du + diu * i + _dot(dpab, wa_ref[d], NT) + _dot(dpib, wi_ref[d], NT)
                dwa_ref[d] += _dot(ub, dpab, TN)
                dwi_ref[d] += _dot(ub, dpib, TN)
                dba_ref[d:d + 1, :] += jnp.sum(dpa, axis=0, keepdims=True)
                dbi_ref[d:d + 1, :] += jnp.sum(dpi, axis=0, keepdims=True)
                dlam_ref[d:d + 1, :] += jnp.sum(dlog * r, axis=0, keepdims=True)
            dupad_ref[pl.ds(PAD + t0, REC_CHUNK), :] = du
            return carry

        lax.fori_loop(0, N_CHUNK, grad_chunk, 0)
        dlam_ref[...] = dlam_ref[...] * (LRU_C * _sigmoid(-lam_ref[...]))

        cw = cw_ref[...]
        dcb = jnp.zeros((1, cb), F32)
        dcw = [jnp.zeros((1, cb), F32) for _ in range(4)]
        for c in range(N_CHUNK):
            t0 = c * REC_CHUNK
            du = dupad_ref[pl.ds(PAD + t0, REC_CHUNK), :]
            dcb = dcb + jnp.sum(du, axis=0, keepdims=True)
            for j in range(4):
                dcw[j] = dcw[j] + jnp.sum(du * upad_ref[pl.ds(PAD + t0 + j - 2, REC_CHUNK), :], axis=0, keepdims=True)
            dup_ref[pl.ds(t0, REC_CHUNK), :] = _conv_taps(dupad_ref, t0, cw, -1).astype(BF16)
        dcb_ref[...] = dcb
        dcw_ref[...] = jnp.concatenate(dcw, axis=0)

    full = pltpu.VMEM((T, REC_CB), F32)
    padded = pltpu.VMEM((T + 2 * PAD, REC_CB), F32)
    return pl.pallas_call(
        body, name="rec_bwd", grid=(N_CB,),
        in_specs=[up, yb] + [col] * 6 + [cw, cbias, wbd, wbd, vec2, vec2, vec2] + [_ANY] * len(after),
        out_specs=[col, col, cw, cbias, wbd, wbd, vec2, vec2, vec2],
        out_shape=[_sds((T, D), BF16), _sds((T, D), BF16), _sds((4, D), F32), _sds((1, D), F32),
                   _sds((2, N_CB, REC_CB, REC_CB), F32), _sds((2, N_CB, REC_CB, REC_CB), F32),
                   _sds((2, D), F32), _sds((2, D), F32), _sds((2, D), F32)],
        scratch_shapes=[padded, full, full, full, full, full, padded],
        compiler_params=_params(("parallel",)))(z, z, dg, *saved, conv_w, conv_b, wa, wi, ba, bi, lam, *after)


class _NoReducer:
    def begin(self, tag, grads):
        return ()

    def advance(self, tag, after):
        return ()


def _local_step(x, target, p, late=None, reducer=_NoReducer()):
    x = x.reshape(T, D)
    target = target.reshape(T, D)
    tb = _bias_pairs(p["rpb"])
    wa, wi = _block_diag(p["w_rg_a"]), _block_diag(p["w_rg_i"])

    h1 = _rms_fwd("rms1_fwd", x, p["ln1_g"], after=late[0] if late else ())
    if late:
        p = {**p, **late[1]((h1, tb, wa, wi))}
    rec_params = (p["conv_w"], p["conv_b"], wa, wi, p["b_rg_a"], p["b_rg_i"], p["lru_lambda"])
    (z,) = _mm_nn_cols("mm_z", h1, p["w_in"], F32, bias=p["b_in"])
    att, probs = _attn_fwd(z, tb)
    g, rec_saved = _rec_fwd(z, *rec_params)
    if late:
        p = {**p, **late[2](g)}
    y_att, y_rec, mixed, x1, h2 = _branches_fwd(att, g, z, x, p["w_att_o"], p["w_rec_o"], p["w_out"], p["ln2_g"])

    def relu2(r, ex, outs):
        rp = jnp.maximum(r, 0.0)
        outs[0][...] = (rp * rp).astype(BF16)

    (s,) = _mm_nn_cols("mm_ff1", h2, p["w_ff1"], BF16, epilogue=relu2)
    loss, dx2, dx2_b, g_lnf = _mm_x2_loss_head(s, p["w_ff2"], x1, target, p["lnf_g"])

    def relu2_bwd(r, ex, outs):
        outs[0][...] = (r * 2.0 * jnp.sqrt(ex[0][...].astype(F32))).astype(BF16)

    (df,) = _mm_nt_rows("mm_df", dx2_b, p["w_ff2"], BF16, tn=D, extras=[s],
                        extra_specs=[pl.BlockSpec((TJ, D), lambda j, i, k: (i, j))], epilogue=relu2_bwd)
    (g_w_ff2,) = _mm_tn_rows("mm_g_ff2", s, dx2_b, tm=D)
    (g_w_ff1,) = _mm_tn_cols("mm_g_ff1", h2, df, D)
    tok = reducer.begin("ff", dict(w_ff2=g_w_ff2, w_ff1=g_w_ff1))
    dx1, dx1_b, g_ln2 = _mm_nt_cols_rms_bwd("mm_dh2_rms2_bwd", df, p["w_ff1"], x1, p["ln2_g"], dx2, after=tok,
                                            bf16_copy=True)

    dy_att, dy_rec, dg_att, dg_rec, d_att, d_g = _branches_bwd(dx1_b, y_att, y_rec, z, p["w_att_o"], p["w_rec_o"],
                                                               p["w_out"])
    (g_w_out,) = _mm_tn_rows("mm_g_out", mixed, dx1_b, tm=D)
    (g_w_att_o,) = _mm_tn_cols("mm_g_att_o", att, dy_att, D // N_CHIPS)
    (g_w_rec_o,) = _mm_tn_rows("mm_g_rec_o", g, dy_rec, tm=D)
    tok = reducer.advance("ff", g_w_rec_o) + reducer.begin("proj", dict(w_out=g_w_out, w_att_o=g_w_att_o, w_rec_o=g_w_rec_o))

    dq, dk, dv, ds_acc = _attn_bwd(z, probs, d_att, after=tok)
    g_rpb = _rpb_grad(ds_acc)
    tok = reducer.advance("proj", dq)
    d_up, d_yb, g_conv_w, g_conv_b, g_wa, g_wi, g_ba, g_bi, g_lam = _rec_bwd(z, d_g, rec_saved, *rec_params, after=tok)
    dz = jnp.concatenate([dq, dk, dv, d_up, d_yb, dg_att, dg_rec], axis=1)

    g_w_in, g_b_in = _mm_tn_cols("mm_g_in", h1, dz, D_IN // N_CHIPS, colsum=True)
    tok = reducer.begin("in", dict(w_in=g_w_in))
    grad_x, g_ln1 = _mm_nt_cols_rms_bwd("mm_dh1_rms1_bwd", dz, p["w_in"], x, p["ln1_g"], dx1, after=tok)
    reducer.advance("in", grad_x)

    grads = dict(ln1_g=g_ln1, w_in=g_w_in, b_in=g_b_in, rpb=g_rpb, w_att_o=g_w_att_o, conv_w=g_conv_w,
                 conv_b=g_conv_b, w_rg_a=_block_diag_grad(g_wa), b_rg_a=g_ba, w_rg_i=_block_diag_grad(g_wi),
                 b_rg_i=g_bi, lru_lambda=g_lam, w_rec_o=g_w_rec_o, w_out=g_w_out, ln2_g=g_ln2,
                 w_ff1=g_w_ff1, w_ff2=g_w_ff2, lnf_g=g_lnf)
    return loss, grad_x.reshape(1, T, D), grads


_ANY = pl.BlockSpec(memory_space=pl.ANY)
N_PEERS = N_CHIPS - 1


def _place():
    x, y, c = lax.axis_index("x"), lax.axis_index("y"), lax.axis_index("c")
    peers = [(1 - x, y), (x, 1 - y), (1 - x, 1 - y)]
    return x, y, c, 2 * x + y, peers


def _remote(src, dst, send_sem, recv_sem, dev):
    return pltpu.make_async_remote_copy(src_ref=src, dst_ref=dst, send_sem=send_sem, recv_sem=recv_sem,
                                        device_id=dev, device_id_type=MESH)


def _prefetch_call(body, name, ids, grid, in_specs, out_specs, out_shape, args, semantics=None):
    spec = pltpu.PrefetchScalarGridSpec(num_scalar_prefetch=1, grid=grid, in_specs=in_specs, out_specs=out_specs)
    return pl.pallas_call(body, name=name, grid_spec=spec, out_shape=out_shape,
                          compiler_params=_params(semantics or ("parallel",) * len(grid)))(ids, *args)


def _cast_bf16(name, w, chip_id, after=()):
    rows, cols = w.shape
    rb = min(rows, 256)

    def body(ids_ref, w_ref, *rest):
        rest[-1][...] = w_ref[...].astype(BF16)

    return _prefetch_call(body, name, chip_id, (rows // rb,),
                          [pl.BlockSpec((rb, cols), lambda i, ids: (i, 0))] + [_ANY] * len(after),
                          pl.BlockSpec((None, rb, cols), lambda i, ids: (ids[0], i, 0)),
                          _sds((N_CHIPS, rows, cols), BF16), (w, *after))


def _dma_sems(*counts):
    return [pltpu.SemaphoreType.DMA((k,)) for k in counts]


_HBM = pl.BlockSpec(memory_space=pltpu.HBM)
_SEM = pl.BlockSpec(memory_space=pltpu.SEMAPHORE)
_SPLIT_COPY = pltpu.CompilerParams(has_side_effects=pltpu.SideEffectType.DATAFLOW_SIDE_EFFECTING)
SIBLING_ID = 0
_SPLIT_COPY_SIBLING = pltpu.CompilerParams(has_side_effects=pltpu.SideEffectType.DATAFLOW_SIDE_EFFECTING,
                                           collective_id=SIBLING_ID)


def _sibling_handshake():
    x, y, c = lax.axis_index("x"), lax.axis_index("y"), lax.axis_index("c")
    barrier = pltpu.get_barrier_semaphore()
    pl.semaphore_signal(barrier, inc=1, device_id=(x, y, 1 - c), device_id_type=MESH)
    pl.semaphore_wait(barrier, 1)


def _hbm(arrays):
    return [pltpu.with_memory_space_constraint(a, pltpu.HBM) for a in arrays]


def _hbm_like(arrays):
    return [pltpu.HBM(a.shape, a.dtype) for a in arrays]


def _halves(buf, c):
    half = buf.shape[1] // 2
    return pl.ds(c * half, half), pl.ds((1 - c) * half, half)


def _gather_start(name, slots):
    n = len(slots)
    nk = n * N_PEERS

    def body(*refs):
        bufs = refs[n:2 * n]
        send_sems, recv_sems, token = refs[2 * n:]
        x, y, c, chip, peers = _place()
        for t in range(n):
            mine, _ = _halves(bufs[t], c)
            for r, (px, py) in enumerate(peers):
                k = t * N_PEERS + r
                own = bufs[t].at[chip, mine]
                _remote(own, own, send_sems.at[k], recv_sems.at[k], (px, py, c)).start()
        token[...] = jnp.zeros_like(token)

    res = pl.pallas_call(
        body, name=name, in_specs=[_HBM] * n, out_specs=[_HBM] * n + [_SEM, _SEM, pl.BlockSpec(memory_space=pltpu.VMEM)],
        out_shape=_hbm_like(slots) + [pltpu.SemaphoreType.DMA((nk,)), pltpu.SemaphoreType.DMA((nk,)),
                                      _sds((8, 128), F32)],
        input_output_aliases={t: t for t in range(n)}, compiler_params=_SPLIT_COPY)(*_hbm(slots))
    return res[:n], (res[n], res[n + 1]), res[n + 2]


def _gather_wait(name, bufs, sems, after):
    n = len(bufs)
    after = tuple(after) if isinstance(after, (tuple, list)) else (after,)

    def body(*refs):
        ins = refs[:n]
        send_sems, recv_sems = refs[n], refs[n + 1]
        x, y, c, chip, peers = _place()
        for t in range(n):
            mine, _ = _halves(ins[t], c)
            for r, (px, py) in enumerate(peers):
                k = t * N_PEERS + r
                cp = _remote(ins[t].at[chip, mine], ins[t].at[2 * px + py, mine], send_sems.at[k], recv_sems.at[k],
                             (px, py, c))
                cp.wait_send()
                cp.wait_recv()

    return pl.pallas_call(
        body, name=name, in_specs=[_HBM] * n + [_SEM, _SEM] + [_ANY] * len(after), out_specs=[_HBM] * n,
        out_shape=_hbm_like(bufs), input_output_aliases={t: t for t in range(n)},
        compiler_params=_SPLIT_COPY)(*bufs, *sems, *after)


def _gather_forward(name, bufs):
    n = len(bufs)
    nk = n * N_PEERS

    def body(*refs):
        _sibling_handshake()
        outs = refs[n:2 * n]
        send_sems, recv_sems = refs[2 * n:]
        x, y, c, chip, peers = _place()
        sibling = (x, y, 1 - c)
        sends = []
        for t in range(n):
            mine, _ = _halves(outs[t], c)
            for r, (px, py) in enumerate(peers):
                k = t * N_PEERS + r
                landed = outs[t].at[2 * px + py, mine]
                sends.append(_remote(landed, landed, send_sems.at[k], recv_sems.at[k], sibling))
                sends[-1].start()
        for t in range(n):
            _, theirs = _halves(outs[t], c)
            for r, (px, py) in enumerate(peers):
                k = t * N_PEERS + r
                landed = outs[t].at[2 * px + py, theirs]
                _remote(landed, landed, send_sems.at[k], recv_sems.at[k], sibling).wait_recv()
        for cp in sends:
            cp.wait_send()

    return pl.pallas_call(
        body, name=name, in_specs=[_ANY] * n, out_specs=[_ANY] * n, out_shape=[_sds(b.shape, b.dtype) for b in bufs],
        input_output_aliases={t: t for t in range(n)}, scratch_shapes=_dma_sems(nk, nk),
        compiler_params=pltpu.CompilerParams(collective_id=SIBLING_ID))(*bufs)


def _pair_copies(n, srcs, lands, send_sems, recv_sems):
    x, y, c, _, _ = _place()
    sibling = (x, y, 1 - c)
    copies = []
    for t in range(n):
        half = srcs[t].shape[1] // 2
        for j in range(N_CHIPS):
            k = t * N_CHIPS + j
            copies.append(_remote(srcs[t].at[j, pl.ds((1 - c) * half, half)], lands[t].at[j],
                                  send_sems.at[k], recv_sems.at[k], sibling))
    for t in range(n, len(srcs)):
        k = n * N_CHIPS + t - n
        copies.append(_remote(srcs[t], lands[t], send_sems.at[k], recv_sems.at[k], sibling))
    return copies


def _pair_start(name, grads, wholes=()):
    n = len(grads)
    srcs = list(grads) + list(wholes)
    m = len(srcs)
    lands = [pltpu.HBM((N_CHIPS, g.shape[1] // 2, g.shape[2]), F32) for g in grads] + _hbm_like(wholes)
    ns = n * N_CHIPS + len(wholes)

    def body(*refs):
        _sibling_handshake()
        src_refs, land_refs = refs[m:2 * m], refs[2 * m:3 * m]
        send_sems, recv_sems, token = refs[3 * m:]
        for cp in _pair_copies(n, src_refs, land_refs, send_sems, recv_sems):
            cp.start()
        token[...] = jnp.zeros_like(token)

    res = pl.pallas_call(
        body, name=name, in_specs=[_HBM] * m,
        out_specs=[_HBM] * (2 * m) + [_SEM, _SEM, pl.BlockSpec(memory_space=pltpu.VMEM)],
        out_shape=_hbm_like(srcs) + lands + [pltpu.SemaphoreType.DMA((ns,)), pltpu.SemaphoreType.DMA((ns,)),
                                             _sds((8, 128), F32)],
        input_output_aliases={t: t for t in range(m)}, compiler_params=_SPLIT_COPY_SIBLING)(*_hbm(srcs))
    return (res[:m], res[m:2 * m], (res[2 * m], res[2 * m + 1])), res[2 * m + 2]


def _pair_wait(name, flight, n, after):
    srcs, lands, sems = flight
    m = len(srcs)

    def body(*refs):
        for cp in _pair_copies(n, refs[:m], refs[m:2 * m], refs[2 * m], refs[2 * m + 1]):
            cp.wait_send()
            cp.wait_recv()

    res = pl.pallas_call(
        body, name=name, in_specs=[_HBM] * (2 * m) + [_SEM, _SEM, _ANY], out_specs=[_HBM] * (2 * m),
        out_shape=_hbm_like(srcs) + _hbm_like(lands), input_output_aliases={t: t for t in range(2 * m)},
        compiler_params=_SPLIT_COPY)(*srcs, *lands, *sems, after)
    return res[:m], res[m:]


def _chip_copies(srcs, lands, small_src, small_land, send_sems, recv_sems):
    x, y, c, chip, peers = _place()
    n = len(srcs)
    copies = []
    for r, (px, py) in enumerate(peers):
        for t in range(n):
            k = t * N_PEERS + r
            copies.append(_remote(srcs[t].at[2 * px + py], lands[t].at[r], send_sems.at[k], recv_sems.at[k], (px, py, c)))
        if small_src is not None:
            k = n * N_PEERS + r
            half_s = small_src.shape[0] // 2
            copies.append(_remote(small_src.at[pl.ds(c * half_s, half_s)], small_land.at[r],
                                  send_sems.at[k], recv_sems.at[k], (px, py, c)))
    return copies


def _chip_start(name, sums_bf16, small=None):
    n = len(sums_bf16)
    srcs = list(sums_bf16) + ([small] if small is not None else [])
    m = len(srcs)
    lands = [pltpu.HBM((N_PEERS,) + s.shape[1:], BF16) for s in sums_bf16]
    if small is not None:
        lands.append(pltpu.HBM((N_PEERS, small.shape[0] // 2, 128), F32))
    nk = m * N_PEERS

    def body(*refs):
        src_refs, land_refs = refs[m:2 * m], refs[2 * m:3 * m]
        send_sems, recv_sems, token = refs[3 * m:]
        small_src, small_land = (src_refs[n], land_refs[n]) if small is not None else (None, None)
        for cp in _chip_copies(src_refs[:n], land_refs[:n], small_src, small_land, send_sems, recv_sems):
            cp.start()
        token[...] = jnp.zeros_like(token)

    res = pl.pallas_call(
        body, name=name, in_specs=[_HBM] * m,
        out_specs=[_HBM] * (2 * m) + [_SEM, _SEM, pl.BlockSpec(memory_space=pltpu.VMEM)],
        out_shape=_hbm_like(srcs) + lands + [pltpu.SemaphoreType.DMA((nk,)), pltpu.SemaphoreType.DMA((nk,)),
                                             _sds((8, 128), F32)],
        input_output_aliases={t: t for t in range(m)}, compiler_params=_SPLIT_COPY)(*_hbm(srcs))
    return (res[:m], res[m:2 * m], (res[2 * m], res[2 * m + 1])), res[2 * m + 2]


def _chip_wait(name, flight, with_small, after):
    srcs, lands, sems = flight
    m = len(srcs)
    n = m - 1 if with_small else m

    def body(*refs):
        src_refs, land_refs = refs[:m], refs[m:2 * m]
        send_sems, recv_sems = refs[2 * m], refs[2 * m + 1]
        small_src, small_land = (src_refs[n], land_refs[n]) if with_small else (None, None)
        for cp in _chip_copies(src_refs[:n], land_refs[:n], small_src, small_land, send_sems, recv_sems):
            cp.wait_send()
            cp.wait_recv()

    res = pl.pallas_call(
        body, name=name, in_specs=[_HBM] * (2 * m) + [_SEM, _SEM, _ANY], out_specs=[_HBM] * (2 * m),
        out_shape=_hbm_like(srcs) + _hbm_like(lands), input_output_aliases={t: t for t in range(2 * m)},
        compiler_params=_SPLIT_COPY)(*srcs, *lands, *sems, after)
    return res[:m], res[m:]


def _swap_start(name, bufs):
    n = len(bufs)

    def body(*refs):
        _sibling_handshake()
        outs = refs[n:2 * n]
        send_sems, recv_sems, token = refs[2 * n:]
        x, y, c, _, _ = _place()
        for t in range(n):
            h = outs[t].shape[0] // 2
            mine = outs[t].at[pl.ds(c * h, h)]
            _remote(mine, mine, send_sems.at[t], recv_sems.at[t], (x, y, 1 - c)).start()
        token[...] = jnp.zeros_like(token)

    res = pl.pallas_call(
        body, name=name, in_specs=[_HBM] * n, out_specs=[_HBM] * n + [_SEM, _SEM, pl.BlockSpec(memory_space=pltpu.VMEM)],
        out_shape=_hbm_like(bufs) + [pltpu.SemaphoreType.DMA((n,)), pltpu.SemaphoreType.DMA((n,)), _sds((8, 128), F32)],
        input_output_aliases={t: t for t in range(n)}, compiler_params=_SPLIT_COPY_SIBLING)(*_hbm(bufs))
    return (res[:n], (res[n], res[n + 1])), res[n + 2]


def _swap_wait(name, flight, after):
    bufs, sems = flight
    n = len(bufs)

    def body(*refs):
        ins = refs[:n]
        send_sems, recv_sems = refs[n], refs[n + 1]
        x, y, c, _, _ = _place()
        for t in range(n):
            h = ins[t].shape[0] // 2
            cp = _remote(ins[t].at[pl.ds(c * h, h)], ins[t].at[pl.ds((1 - c) * h, h)], send_sems.at[t],
                         recv_sems.at[t], (x, y, 1 - c))
            cp.wait_send()
            cp.wait_recv()

    return pl.pallas_call(
        body, name=name, in_specs=[_HBM] * n + [_SEM, _SEM, _ANY], out_specs=[_HBM] * n, out_shape=_hbm_like(bufs),
        input_output_aliases={t: t for t in range(n)}, compiler_params=_SPLIT_COPY)(*bufs, *sems, after)


def _pair_sum(name, grad, got, ids):
    _, rows, cols = got.shape
    rb = min(rows, 256)
    nb = rows // rb
    blk = pl.BlockSpec((None, rb, cols), lambda i, j, ids: (j, i, 0))
    mine = pl.BlockSpec((None, rb, cols), lambda i, j, ids: (j, ids[1] * nb + i, 0))
    own = pl.BlockSpec((rb, cols), lambda i, j, ids: (i, 0))

    def body(ids_ref, a_ref, b_ref, s_ref, sb_ref):
        s = a_ref[...] + b_ref[...]
        sb_ref[...] = s.astype(BF16)

        @pl.when(pl.program_id(1) == ids_ref[0])
        def _():
            s_ref[...] = s

    return _prefetch_call(body, name, ids, (nb, N_CHIPS), [mine, blk], [own, blk],
                          [_sds((rows, cols), F32), _sds(got.shape, BF16)], (grad, got),
                          semantics=("parallel", "arbitrary"))


def _chip_sum(name, own_sum, got, ids):
    rows, cols = own_sum.shape
    rb = min(rows, 256)
    nb = rows // rb
    own = pl.BlockSpec((rb, cols), lambda i, ids: (i, 0))
    blk3 = pl.BlockSpec((N_PEERS, rb, cols), lambda i, ids: (0, i, 0))
    out = pl.BlockSpec((rb, cols), lambda i, ids: (ids[1] * nb + i, 0))

    def body(ids_ref, a_ref, b_ref, o_ref):
        o_ref[...] = ((a_ref[...] + b_ref[0].astype(F32)) + b_ref[1].astype(F32)) + b_ref[2].astype(F32)

    return _prefetch_call(body, name, ids, (nb,), [own, blk3], out, _sds((2 * rows, cols), F32), (own_sum, got))


SMALL_RB = 280


def _small_pair_sum(own, got):
    blk = pl.BlockSpec((SMALL_RB, 128), lambda i: (i, 0))

    def body(a_ref, b_ref, o_ref):
        o_ref[...] = a_ref[...] + b_ref[...]

    return pl.pallas_call(body, name="small_pair_sum", grid=(own.shape[0] // SMALL_RB,), in_specs=[blk, blk],
                          out_specs=blk, out_shape=_sds(own.shape, F32),
                          compiler_params=_params(("parallel",)))(own, got)


def _small_chip_sum(pair, got, ids):
    nb = pair.shape[0] // 2 // SMALL_RB
    half = pl.BlockSpec((SMALL_RB, 128), lambda i, ids: (ids[1] * nb + i, 0))
    blk3 = pl.BlockSpec((N_PEERS, SMALL_RB, 128), lambda i, ids: (0, i, 0))

    def body(ids_ref, a_ref, b_ref, o_ref):
        o_ref[...] = (a_ref[...] + b_ref[1]) + (b_ref[0] + b_ref[2])

    return _prefetch_call(body, "small_chip_sum", ids, (nb,), [half, blk3], half, _sds(pair.shape, F32), (pair, got))


def _adamw_math(w, g, m, v):
    m = ADAM_B1 * m + (1.0 - ADAM_B1) * g
    v = ADAM_B2 * v + (1.0 - ADAM_B2) * (g * g)
    m_hat = m / (1.0 - ADAM_B1 ** ADAM_STEP)
    v_hat = v / (1.0 - ADAM_B2 ** ADAM_STEP)
    delta = -ADAM_LR * (m_hat / (jnp.sqrt(v_hat) + ADAM_EPS) + ADAM_WD * w)
    return delta, m, v


def _adamw(name, w, g, m, v, rb=None):
    rows, cols = w.shape
    rb = rows if rb is None else rb
    blk = pl.BlockSpec((rb, cols), lambda i: (i, 0))

    def body(w_ref, g_ref, m_ref, v_ref, d_ref, nm_ref, nv_ref):
        d, nm, nv = _adamw_math(w_ref[...], g_ref[...], m_ref[...], v_ref[...])
        d_ref[...] = d
        nm_ref[...] = nm
        nv_ref[...] = nv

    return pl.pallas_call(body, name=name, grid=(rows // rb,), in_specs=[blk] * 4, out_specs=[blk] * 3,
                          out_shape=[_sds(w.shape, F32)] * 3, compiler_params=_params(("parallel",)))(w, g, m, v)


def _adamw_small(ws, gs, ms, vs):
    n = len(ws)

    def body(*refs):
        for t in range(n):
            w_ref, g_ref, m_ref, v_ref = (refs[k * n + t] for k in range(4))
            d, nm, nv = _adamw_math(w_ref[...], g_ref[...], m_ref[...], v_ref[...])
            for k, val in enumerate((d, nm, nv)):
                refs[(4 + k) * n + t][...] = val

    res = pl.pallas_call(body, name="adamw_small", out_shape=[_sds(a.shape, F32) for a in ws] * 3,
                         compiler_params=_params())(*ws, *gs, *ms, *vs)
    return [(res[t], res[n + t], res[2 * n + t]) for t in range(n)]


BIG = ("w_in", "w_att_o", "w_rec_o", "w_out", "w_ff1", "w_ff2")
SHARDED_VECS = ("conv_w", "b_rg_a", "b_rg_i", "lru_lambda")
SMALL = ("ln1_g", "b_in", "rpb", "conv_w", "conv_b", "w_rg_a", "b_rg_a", "w_rg_i", "b_rg_i", "lru_lambda",
         "ln2_g", "lnf_g")
SMALL_ROWS = 2240
ORDER = ("ln1_g", "w_in", "b_in", "rpb", "w_att_o", "conv_w", "conv_b", "w_rg_a", "b_rg_a", "w_rg_i", "b_rg_i",
         "lru_lambda", "w_rec_o", "w_out", "ln2_g", "w_ff1", "w_ff2", "lnf_g")


def _pack_small(grads, loss):
    parts, sizes = [], {}
    for n in SMALL:
        flat = grads[n].reshape(-1)
        pad = (-flat.shape[0]) % 128
        sizes[n] = (flat.shape[0], flat.shape[0] + pad)
        parts.append(jnp.pad(flat, (0, pad)))
    total = sum(s[1] for s in sizes.values())
    parts.append(jnp.pad(loss.reshape(1), (0, SMALL_ROWS * 128 - total - 1)))
    return jnp.concatenate(parts).reshape(SMALL_ROWS, 128), sizes


def _unpack_small(buf, sizes, shapes):
    flat = buf.reshape(-1)
    out, pos = {}, 0
    for n in SMALL:
        size, padded = sizes[n]
        out[n] = flat[pos:pos + size].reshape(shapes[n])
        pos += padded
    return out, flat[pos]


def _gather_weights(w, chip):
    chip_id = chip.astype(jnp.int32).reshape(1)
    vec_rows = [w[n][0] for n in SHARDED_VECS]
    vec_shard = jnp.concatenate(vec_rows + [jnp.zeros((16 - 10, D // N_CHIPS), F32)], axis=0)
    vec_slots = lax.dynamic_update_slice(jnp.zeros((N_CHIPS, 16, D // N_CHIPS), F32), vec_shard[None], (chip, 0, 0))
    bufs_a, sems_a, token_a = _gather_start("gather_start_first", [_cast_bf16("cast_w_in", w["w_in"][0], chip_id), vec_slots])
    rest_names = BIG[1:]
    bufs_b, sems_b, token_b = _gather_start(
        "gather_start_rest", [_cast_bf16("cast_" + n, w[n][0], chip_id, after=(token_a,)) for n in rest_names])

    def first(after):
        w_in_full, vec_full = _gather_forward("gather_forward_first", _gather_wait("gather_wait_first", bufs_a, sems_a, after))
        vecs = vec_full.transpose(1, 0, 2).reshape(16, D)
        return dict(w_in=w_in_full, conv_w=vecs[0:4], b_rg_a=vecs[4:6], b_rg_i=vecs[6:8], lru_lambda=vecs[8:10])

    def rest(after):
        full = dict(zip(rest_names, _gather_forward("gather_forward_rest",
                                                    _gather_wait("gather_wait_rest", bufs_b, sems_b, after))))
        return dict(w_att_o=full["w_att_o"], w_ff1=full["w_ff1"], w_rec_o=full["w_rec_o"].reshape(D, D),
                    w_out=full["w_out"].reshape(D, D), w_ff2=full["w_ff2"].reshape(D_FF, D))

    p = dict(ln1_g=w["ln1_g"], b_in=w["b_in"], rpb=w["rpb"][0], conv_b=w["conv_b"], w_rg_a=w["w_rg_a"][0],
             w_rg_i=w["w_rg_i"][0], ln2_g=w["ln2_g"], lnf_g=w["lnf_g"].reshape(1, D))
    return p, ((token_b,), first, rest)


class _Reducer:
    def __init__(self, ids):
        self.ids = ids
        self.groups = {}

    def begin(self, tag, grads, small=None):
        names = list(grads)
        big = [grads[n].reshape(N_CHIPS, -1, grads[n].shape[-1]) for n in names]
        flight, token = _pair_start("pair_start_" + tag, big, [] if small is None else [small])
        self.groups[tag] = dict(names=names, pair=flight, small=small is not None)
        return (token,)

    def advance(self, tag, after):
        grp = self.groups[tag]
        n = len(grp["names"])
        mine, got = _pair_wait("pair_wait_" + tag, grp["pair"], n, after)
        sums = [_pair_sum("pair_sum_" + name, a, b, self.ids) for name, a, b in zip(grp["names"], mine, got)]
        small_sum = _small_pair_sum(mine[n], got[n]) if grp["small"] else None
        grp["chip"], token = _chip_start("chip_start_" + tag, [s[1] for s in sums], small_sum)
        grp["sums"] = [s[0] for s in sums]
        self.last_token = token
        return (token,)

    def finish(self, tag, after):
        grp = self.groups[tag]
        srcs, lands = _chip_wait("chip_wait_" + tag, grp["chip"], grp["small"], after)
        halves = [_chip_sum("chip_sum_" + name, s, b, self.ids) for name, s, b in zip(grp["names"], grp["sums"], lands)]
        if grp["small"]:
            halves.append(_small_chip_sum(srcs[-1], lands[-1], self.ids))
        grp["swap"], token = _swap_start("swap_start_" + tag, halves)
        return token

    def result(self, tag, after):
        return _swap_wait("swap_wait_" + tag, self.groups[tag]["swap"], after)


def kernel(x, ln1_g, w_in, b_in, rpb, w_att_o, conv_w, conv_b, w_rg_a, b_rg_a, w_rg_i, b_rg_i, lru_lambda, w_rec_o, w_out, ln2_g, w_ff1, w_ff2, lnf_g, loss_target, m_ln1_g, m_w_in, m_b_in, m_rpb, m_w_att_o, m_conv_w, m_conv_b, m_w_rg_a, m_b_rg_a, m_w_rg_i, m_b_rg_i, m_lru_lambda, m_w_rec_o, m_w_out, m_ln2_g, m_w_ff1, m_w_ff2, m_lnf_g, v_ln1_g, v_w_in, v_b_in, v_rpb, v_w_att_o, v_conv_w, v_conv_b, v_w_rg_a, v_b_rg_a, v_w_rg_i, v_b_rg_i, v_lru_lambda, v_w_rec_o, v_w_out, v_ln2_g, v_w_ff1, v_w_ff2, v_lnf_g):
    w = dict(ln1_g=ln1_g, w_in=w_in, b_in=b_in, rpb=rpb, w_att_o=w_att_o, conv_w=conv_w, conv_b=conv_b,
             w_rg_a=w_rg_a, b_rg_a=b_rg_a, w_rg_i=w_rg_i, b_rg_i=b_rg_i, lru_lambda=lru_lambda, w_rec_o=w_rec_o,
             w_out=w_out, ln2_g=ln2_g, w_ff1=w_ff1, w_ff2=w_ff2, lnf_g=lnf_g)
    m = dict(ln1_g=m_ln1_g, w_in=m_w_in, b_in=m_b_in, rpb=m_rpb, w_att_o=m_w_att_o, conv_w=m_conv_w,
             conv_b=m_conv_b, w_rg_a=m_w_rg_a, b_rg_a=m_b_rg_a, w_rg_i=m_w_rg_i, b_rg_i=m_b_rg_i,
             lru_lambda=m_lru_lambda, w_rec_o=m_w_rec_o, w_out=m_w_out, ln2_g=m_ln2_g, w_ff1=m_w_ff1,
             w_ff2=m_w_ff2, lnf_g=m_lnf_g)
    v = dict(ln1_g=v_ln1_g, w_in=v_w_in, b_in=v_b_in, rpb=v_rpb, w_att_o=v_w_att_o, conv_w=v_conv_w,
             conv_b=v_conv_b, w_rg_a=v_w_rg_a, b_rg_a=v_b_rg_a, w_rg_i=v_w_rg_i, b_rg_i=v_b_rg_i,
             lru_lambda=v_lru_lambda, w_rec_o=v_w_rec_o, w_out=v_w_out, ln2_g=v_ln2_g, w_ff1=v_w_ff1,
             w_ff2=v_w_ff2, lnf_g=v_lnf_g)
    chip = 2 * lax.axis_index("x") + lax.axis_index("y")
    ids = jnp.stack([chip, lax.axis_index("c")]).astype(jnp.int32)

    out_grad, out_delta, out_m, out_v = {}, {}, {}, {}

    def update(n, gn):
        shape, two_d = w[n].shape, gn.shape
        d, nm, nv = _adamw("adamw_" + n, w[n].reshape(two_d), gn, m[n].reshape(two_d), v[n].reshape(two_d), 256)
        out_grad[n], out_delta[n], out_m[n], out_v[n] = (gn.reshape(shape), d.reshape(shape), nm.reshape(shape),
                                                         nv.reshape(shape))
        return d

    reducer = _Reducer(ids)
    p, late = _gather_weights(w, chip)
    loss, grad_x, g = _local_step(x, loss_target, p, late, reducer)
    small, sizes = _pack_small(g, loss + reducer.last_token[:1, :1])
    after = reducer.begin("small", {}, small)[0]
    after = reducer.finish("ff", after)
    after = reducer.advance("small", after)[0]
    for tag, following in (("ff", "proj"), ("proj", "in"), ("in", None)):
        if following:
            after = reducer.finish(following, after)
        for n, red in zip(reducer.groups[tag]["names"], reducer.result(tag, after)):
            after = update(n, red)
    (small_red,) = reducer.result("small", reducer.finish("small", after))
    gsmall, loss = _unpack_small(small_red, sizes, {n: g[n].shape for n in SMALL})
    two_d = {n: (int(np.prod(w[n].shape[:-1])), w[n].shape[-1]) for n in SMALL}
    for n in SHARDED_VECS:
        gsmall[n] = lax.dynamic_slice_in_dim(gsmall[n], chip * (D // N_CHIPS), D // N_CHIPS, axis=1)
    gs = [gsmall[n].reshape(two_d[n]) for n in SMALL]
    updates = _adamw_small([w[n].reshape(two_d[n]) for n in SMALL], gs, [m[n].reshape(two_d[n]) for n in SMALL],
                           [v[n].reshape(two_d[n]) for n in SMALL])
    for n, gn, (d, nm, nv) in zip(SMALL, gs, updates):
        shape = w[n].shape
        out_grad[n], out_delta[n], out_m[n], out_v[n] = (gn.reshape(shape), d.reshape(shape), nm.reshape(shape),
                                                         nv.reshape(shape))
    return (loss, grad_x, *[out_grad[n] for n in ORDER], *[out_delta[n] for n in ORDER],
            *[out_m[n] for n in ORDER], *[out_v[n] for n in ORDER])
```

```python
import numpy as np
import jax
import jax.numpy as jnp
from jax import lax
from jax.experimental import pallas as pl
from jax.experimental.pallas import tpu as pltpu

F32 = jnp.float32
BF16 = jnp.bfloat16

T = 2048
D = 1024
D_ATT = 512
D_IN = 5632
D_FF = 4096
N_HEADS = 8
HEAD_DIM = 64
GRID_W = 64
N_ROWS = T // GRID_W
WIN_H = 8
WIN_W = 16
KEYS = WIN_H * GRID_W
N_CHIPS = 4
EPS = 1e-6
LRU_C = 8.0
SCALE = HEAD_DIM ** -0.5
REC_CB = 256
REC_CHUNK = 256
PAD = 8

ADAM_LR = 0.001
ADAM_B1 = 0.9
ADAM_B2 = 0.999
ADAM_EPS = 1e-08
ADAM_WD = 0.01
ADAM_STEP = 10

VMEM_LIMIT = 56 * 1024 * 1024

NN = (((1,), (0,)), ((), ()))
NT = (((1,), (1,)), ((), ()))
TN = (((0,), (0,)), ((), ()))
MESH = pl.DeviceIdType.MESH


def _params(sem=None):
    return pltpu.CompilerParams(dimension_semantics=sem, vmem_limit_bytes=VMEM_LIMIT)


def _dot(a, b, dims):
    return lax.dot_general(a, b, dims, preferred_element_type=F32)


def _sigmoid(x):
    return 0.5 * jnp.tanh(0.5 * x) + 0.5


def _matmul(name, a, b, *, dims, grid, a_spec, b_spec, out_shapes, out_specs, acc_shape,
            extras=(), extra_specs=(), epilogue=None, colsum_spec=None, colsum_shape=None, after=(),
            semantics=("parallel", "parallel", "arbitrary"), epilogue_takes_first=False):
    nk = grid[2]
    n_extra = len(extras)
    n_out = len(out_shapes)
    with_colsum = colsum_spec is not None

    def body(a_ref, b_ref, *rest):
        ex = rest[:n_extra]
        rest = rest[:n_extra] + rest[n_extra + len(after):]
        outs = rest[n_extra:n_extra + n_out]
        pos = n_extra + n_out
        cs_out = rest[pos] if with_colsum else None
        pos += 1 if with_colsum else 0
        acc = rest[pos]
        cs_acc = rest[pos + 1] if with_colsum else None
        k = pl.program_id(2)
        first_tile = pl.program_id(0) == 0

        @pl.when(k == 0)
        def _():
            acc[...] = jnp.zeros_like(acc)
            if with_colsum:
                cs_acc[...] = jnp.zeros_like(cs_acc)

        bv = b_ref[...]
        acc[...] += _dot(a_ref[...].astype(BF16), bv.astype(BF16), dims)
        if with_colsum:
            cs_acc[...] += jnp.sum(bv.astype(F32), axis=0, keepdims=True)

        @pl.when(k == nk - 1)
        def _():
            r = acc[...]
            if epilogue is None:
                outs[0][...] = r.astype(outs[0].dtype)
            elif epilogue_takes_first:
                epilogue(r, ex, outs, first_tile)
            else:
                epilogue(r, ex, outs)
            if with_colsum:
                cs_out[...] = cs_acc[...]

    shapes = list(out_shapes)
    specs = list(out_specs)
    scratch = [pltpu.VMEM(acc_shape, F32)]
    if with_colsum:
        shapes.append(colsum_shape)
        specs.append(colsum_spec)
        scratch.append(pltpu.VMEM((1, acc_shape[1]), F32))
    res = pl.pallas_call(
        body, name=name, grid=grid,
        in_specs=[a_spec, b_spec, *extra_specs] + [_ANY] * len(after),
        out_specs=specs, out_shape=shapes, scratch_shapes=scratch,
        compiler_params=_params(semantics),
    )(a, b, *extras, *after)
    return res


def _sds(shape, dtype):
    return jax.ShapeDtypeStruct(shape, dtype)


TM = 1024
NI = T // TM
TJ = T
NJ = T // TJ


def _mm_nn_cols(name, a, wg, out_dtype, *, bias=None, extras=(), extra_specs=(), epilogue=None,
                out_shapes=None, out_specs=None):
    k_dim, n4 = wg.shape[1], wg.shape[2]
    ex, exs = list(extras), list(extra_specs)
    if bias is not None:
        ex = [bias] + ex
        exs = [pl.BlockSpec((1, n4), lambda j, i, k: (0, j))] + exs
        user_ep = epilogue

        def epilogue(r, e, outs):
            r = r + e[0][...]
            if user_ep is None:
                outs[0][...] = r.astype(outs[0].dtype)
            else:
                user_ep(r, e[1:], outs)
    if out_shapes is None:
        out_shapes = [_sds((T, N_CHIPS * n4), out_dtype)]
        out_specs = [pl.BlockSpec((TJ, n4), lambda j, i, k: (i, j))]
    return _matmul(
        name, a, wg, dims=NN, grid=(N_CHIPS, NJ, 1),
        a_spec=pl.BlockSpec((TJ, k_dim), lambda j, i, k: (i, 0)),
        b_spec=pl.BlockSpec((None, k_dim, n4), lambda j, i, k: (j, 0, 0)),
        out_shapes=out_shapes, out_specs=out_specs, acc_shape=(TJ, n4),
        extras=ex, extra_specs=exs, epilogue=epilogue)


def _mm_nt_cols_rms_bwd(name, a, wg, x, g, dres, after=(), bf16_copy=False):
    n4 = wg.shape[2]
    row = pl.BlockSpec((TM, D), lambda i, j, k: (i, 0))
    vec = pl.BlockSpec((1, D), lambda i, j, k: (0, 0))

    def epilogue(dhv, ex, outs, first):
        x_ref, g_ref, dres_ref = ex
        dx_ref, dg_ref = outs[0], outs[-1]
        xv = x_ref[...]
        rstd = lax.rsqrt(jnp.mean(xv * xv, axis=-1, keepdims=True) + EPS)
        xhat = xv * rstd
        dy = dhv * g_ref[...]
        dx = dres_ref[...] + rstd * (dy - xhat * jnp.mean(dy * xhat, axis=-1, keepdims=True))
        dx_ref[...] = dx
        if bf16_copy:
            outs[1][...] = dx.astype(BF16)
        part = jnp.sum(dhv * xhat, axis=0, keepdims=True)

        @pl.when(first)
        def _():
            dg_ref[...] = part

        @pl.when(jnp.logical_not(first))
        def _():
            dg_ref[...] += part

    return _matmul(
        name, a, wg, dims=NT, grid=(NI, 1, N_CHIPS),
        a_spec=pl.BlockSpec((TM, n4), lambda i, j, k: (i, k)),
        b_spec=pl.BlockSpec((None, D, n4), lambda i, j, k: (k, 0, 0)),
        out_shapes=[_sds((T, D), F32)] + [_sds((T, D), BF16)] * bf16_copy + [_sds((1, D), F32)],
        out_specs=[row] + [row] * bf16_copy + [vec], acc_shape=(TM, D),
        extras=[x, g, dres], extra_specs=[row, vec, row], epilogue=epilogue, after=after,
        semantics=("arbitrary", "arbitrary", "arbitrary"), epilogue_takes_first=True)


def _mm_nt_rows(name, a, w, out_dtype, *, tn, extras=(), extra_specs=(), epilogue=None):
    k_dim, n = w.shape
    return _matmul(
        name, a, w, dims=NT, grid=(k_dim // tn, NJ, 1),
        a_spec=pl.BlockSpec((TJ, n), lambda j, i, k: (i, 0)),
        b_spec=pl.BlockSpec((tn, n), lambda j, i, k: (j, 0)),
        out_shapes=[_sds((T, k_dim), out_dtype)],
        out_specs=[pl.BlockSpec((TJ, tn), lambda j, i, k: (i, j))], acc_shape=(TJ, tn),
        extras=extras, extra_specs=extra_specs, epilogue=epilogue)


def _mm_tn_cols(name, a, g, n4, *, colsum=False):
    k_dim = a.shape[1]
    kw = {}
    if colsum:
        kw = dict(colsum_spec=pl.BlockSpec((1, n4), lambda j, i, k: (0, j)),
                  colsum_shape=_sds((1, N_CHIPS * n4), F32))
    return _matmul(
        name, a, g, dims=TN, grid=(N_CHIPS, 1, NJ),
        a_spec=pl.BlockSpec((TJ, k_dim), lambda j, i, k: (k, 0)),
        b_spec=pl.BlockSpec((TJ, n4), lambda j, i, k: (k, j)),
        out_shapes=[_sds((N_CHIPS, k_dim, n4), F32)],
        out_specs=[pl.BlockSpec((None, k_dim, n4), lambda j, i, k: (j, 0, 0))],
        acc_shape=(k_dim, n4), **kw)


def _mm_tn_rows(name, a, g, *, tm):
    k_dim, n = a.shape[1], g.shape[1]
    return _matmul(
        name, a, g, dims=TN, grid=(k_dim // tm, 1, NJ),
        a_spec=pl.BlockSpec((TJ, tm), lambda j, i, k: (k, j)),
        b_spec=pl.BlockSpec((TJ, n), lambda j, i, k: (k, 0)),
        out_shapes=[_sds((k_dim, n), F32)],
        out_specs=[pl.BlockSpec((tm, n), lambda j, i, k: (j, 0))], acc_shape=(tm, n))


TE = 256
NE = T // TE
_ROW = pl.BlockSpec((TE, D), lambda i: (i, 0))
_VEC = pl.BlockSpec((1, D), lambda i: (0, 0))


def _rms_fwd(name, x, g, after=()):
    def body(x_ref, g_ref, *rest):
        h_ref = rest[-1]
        xv = x_ref[...]
        rstd = lax.rsqrt(jnp.mean(xv * xv, axis=-1, keepdims=True) + EPS)
        h_ref[...] = (xv * rstd * g_ref[...]).astype(BF16)

    return pl.pallas_call(body, name=name, grid=(NE,), in_specs=[_ROW, _VEC] + [_ANY] * len(after), out_specs=_ROW,
                          out_shape=_sds((T, D), BF16), compiler_params=_params(("parallel",)))(x, g, *after)


def _mm_x2_loss_head(s, w_ff2, x1, target, g):
    k_dim = w_ff2.shape[0]
    row = pl.BlockSpec((TM, D), lambda i, j, k: (i, 0))
    vec = pl.BlockSpec((1, D), lambda i, j, k: (0, 0))

    def epilogue(r, ex, outs, first):
        x1_ref, t_ref, g_ref = ex
        loss_ref, dx_ref, dxb_ref, dg_ref = outs
        xv = x1_ref[...] + r
        rstd = lax.rsqrt(jnp.mean(xv * xv, axis=-1, keepdims=True) + EPS)
        xhat = xv * rstd
        gv = g_ref[...]
        err = xhat * gv - t_ref[...]
        dy = err * (1.0 / D)
        dxh = dy * gv
        dx = rstd * (dxh - xhat * jnp.mean(dxh * xhat, axis=-1, keepdims=True))
        dx_ref[...] = dx
        dxb_ref[...] = dx.astype(BF16)
        dg_part = jnp.sum(dy * xhat, axis=0, keepdims=True)
        loss_part = (0.5 / D) * jnp.sum(jnp.sum(err * err, axis=1, keepdims=True), axis=0, keepdims=True)

        @pl.when(first)
        def _():
            dg_ref[...] = dg_part
            loss_ref[...] = loss_part

        @pl.when(jnp.logical_not(first))
        def _():
            dg_ref[...] += dg_part
            loss_ref[...] += loss_part

    return _matmul(
        "mm_x2_loss_head", s, w_ff2, dims=NN, grid=(NI, 1, k_dim // D),
        a_spec=pl.BlockSpec((TM, D), lambda i, j, k: (i, k)), b_spec=pl.BlockSpec((D, D), lambda i, j, k: (k, 0)),
        out_shapes=[_sds((1, 1), F32), _sds((T, D), F32), _sds((T, D), BF16), _sds((1, D), F32)],
        out_specs=[pl.BlockSpec((1, 1), lambda i, j, k: (0, 0)), row, row, vec], acc_shape=(TM, D),
        extras=[x1, target, g], extra_specs=[row, row, vec], epilogue=epilogue,
        semantics=("arbitrary", "arbitrary", "arbitrary"), epilogue_takes_first=True)


MW = 512
_G_ATT_BLK = 3584 // MW
_G_REC_BLK = 4608 // MW


TB = 512


def _branch_specs():
    def row(cols):
        return pl.BlockSpec((TB, cols), lambda i: (i, 0))

    ga = pl.BlockSpec((TB, MW), lambda i: (i, _G_ATT_BLK))
    ga2 = pl.BlockSpec((TB, MW), lambda i: (i, _G_ATT_BLK + 1))
    gr = pl.BlockSpec((TB, MW), lambda i: (i, _G_REC_BLK))
    gr2 = pl.BlockSpec((TB, MW), lambda i: (i, _G_REC_BLK + 1))
    w_att = pl.BlockSpec((N_CHIPS, D_ATT, D // N_CHIPS), lambda i: (0, 0, 0))
    w_sq = pl.BlockSpec((D, D), lambda i: (0, 0))
    return row, (ga, ga2, gr, gr2), w_att, w_sq


def _gate_values(gate_refs):
    ga, ga2, gr, gr2 = (r[...] for r in gate_refs)
    return _sigmoid(jnp.concatenate([ga, ga2], axis=1)), _sigmoid(jnp.concatenate([gr, gr2], axis=1))


def _branches_fwd(att, g, z, x, w_att_o, w_rec_o, w_out, ln2_g):
    row, gate_specs, w_att, w_sq = _branch_specs()

    def body(att_ref, g_ref, ga_ref, ga2_ref, gr_ref, gr2_ref, x_ref, wa_ref, wr_ref, wo_ref, g2_ref,
             ya_ref, yr_ref, m_ref, x1_ref, h2_ref):
        attv = att_ref[...]
        ya = jnp.concatenate([_dot(attv, wa_ref[j], NN) for j in range(N_CHIPS)], axis=1)
        yr = _dot(g_ref[...], wr_ref[...], NN)
        sa, sr = _gate_values((ga_ref, ga2_ref, gr_ref, gr2_ref))
        mixed = (sa * ya + sr * yr).astype(BF16)
        ya_ref[...] = ya
        yr_ref[...] = yr
        m_ref[...] = mixed
        x1 = x_ref[...] + _dot(mixed, wo_ref[...], NN)
        x1_ref[...] = x1
        rstd = lax.rsqrt(jnp.mean(x1 * x1, axis=-1, keepdims=True) + EPS)
        h2_ref[...] = (x1 * rstd * g2_ref[...]).astype(BF16)

    return pl.pallas_call(
        body, name="branches_fwd", grid=(T // TB,),
        in_specs=[row(D_ATT), row(D), *gate_specs, row(D), w_att, w_sq, w_sq, pl.BlockSpec((1, D), lambda i: (0, 0))],
        out_specs=[row(D)] * 5,
        out_shape=[_sds((T, D), F32), _sds((T, D), F32), _sds((T, D), BF16), _sds((T, D), F32), _sds((T, D), BF16)],
        compiler_params=_params(("parallel",)))(att, g, z, z, z, z, x, w_att_o, w_rec_o, w_out, ln2_g)


def _branches_bwd(dx1_b, y_att, y_rec, z, w_att_o, w_rec_o, w_out):
    row, gate_specs, w_att, w_sq = _branch_specs()
    n4 = D // N_CHIPS

    def body(dx_ref, ya_ref, yr_ref, ga_ref, ga2_ref, gr_ref, gr2_ref, wa_ref, wr_ref, wo_ref,
             dya_ref, dyr_ref, dga_ref, dgr_ref, datt_ref, dg_ref):
        dm = _dot(dx_ref[...], wo_ref[...], NT)
        sa, sr = _gate_values((ga_ref, ga2_ref, gr_ref, gr2_ref))
        dya = (dm * sa).astype(BF16)
        dyr = (dm * sr).astype(BF16)
        dya_ref[...] = dya
        dyr_ref[...] = dyr
        dga_ref[...] = (dm * ya_ref[...] * sa * (1.0 - sa)).astype(BF16)
        dgr_ref[...] = (dm * yr_ref[...] * sr * (1.0 - sr)).astype(BF16)
        datt = _dot(dya[:, 0:n4], wa_ref[0], NT)
        for j in range(1, N_CHIPS):
            datt = datt + _dot(dya[:, j * n4:(j + 1) * n4], wa_ref[j], NT)
        datt_ref[...] = datt.astype(BF16)
        dg_ref[...] = _dot(dyr, wr_ref[...], NT).astype(BF16)

    return pl.pallas_call(
        body, name="branches_bwd", grid=(T // TB,),
        in_specs=[row(D), row(D), row(D), *gate_specs, w_att, w_sq, w_sq],
        out_specs=[row(D)] * 4 + [row(D_ATT), row(D)],
        out_shape=[_sds((T, D), BF16)] * 4 + [_sds((T, D_ATT), BF16), _sds((T, D), BF16)],
        compiler_params=_params(("parallel",)))(dx1_b, y_att, y_rec, z, z, z, z, w_att_o, w_rec_o, w_out)


HP = 2 * HEAD_DIM
N_HP = N_HEADS // 2
ATT_UNROLL_FWD = 16
ATT_UNROLL_BWD = 8
DIAG_ROWS = 32


def _window_maps():
    diag = np.zeros((GRID_W * GRID_W, 128), np.float32)
    for qc in range(GRID_W):
        w0 = min(max(qc - WIN_W // 2, 0), GRID_W - WIN_W)
        for kc in range(w0, w0 + WIN_W):
            diag[qc * GRID_W + kc, kc - qc + WIN_W - 1] = 1.0
    return diag, diag.sum(axis=1)[None, :]


def _split3(x):
    a = x.astype(BF16)
    r = x - a.astype(F32)
    b = r.astype(BF16)
    c = (r - b.astype(F32)).astype(BF16)
    return a, b, c


N_DROW = 2 * WIN_H - 1
N_DPAIR = N_DROW - 1


def _bias_pairs(rpb):
    diag, valid = _window_maps()
    r2 = jnp.pad(rpb.reshape(N_HEADS * N_DROW, 2 * WIN_W - 1),
                 ((0, 128 - N_HEADS * N_DROW), (0, 128 - (2 * WIN_W - 1))))

    def body(r_ref, d_ref, v_ref, o_ref):
        dv = d_ref[...]
        t = sum(_dot(part, dv, NN) for part in _split3(r_ref[...]))
        o_ref[...] = jnp.where(v_ref[...] > 0.0, t, -1e30)

    t = pl.pallas_call(body, name="rpb_expand", out_shape=_sds((128, GRID_W * GRID_W), F32),
                       compiler_params=_params())(r2, jnp.asarray(diag.T, BF16), jnp.asarray(valid, F32))
    t = t[:N_HEADS * N_DROW].reshape(N_HEADS, N_DROW, GRID_W, GRID_W)
    return jnp.concatenate([t[:, :N_DPAIR], t[:, 1:]], axis=-1)


def _row_bias(tb_ref, hh, d0):
    return jnp.concatenate([tb_ref[hh, d0 + 2 * ii] for ii in range(WIN_H // 2)], axis=1)


def _row_window(r):
    rs = jnp.clip(r - WIN_H // 2, 0, N_ROWS - WIN_H)
    return pl.multiple_of(r * GRID_W, GRID_W), pl.multiple_of(rs * GRID_W, GRID_W), rs - r + (WIN_H - 1)


def _split_heads(src_ref, dst_ref, scale=None):
    for hh in range(2):
        v = src_ref[:, hh * HEAD_DIM:(hh + 1) * HEAD_DIM]
        dst_ref[hh] = (v if scale is None else v * scale).astype(BF16)


def _attn_items(qb_ref, kb_ref, vb_ref, tb_ref, first_row, n_rows):
    wins = [_row_window(first_row + u) for u in range(n_rows)]
    items = [(u, hh) for u in range(n_rows) for hh in range(2)]
    q = [qb_ref[hh, pl.ds(wins[u][0], GRID_W), :] for u, hh in items]
    k = [kb_ref[hh, pl.ds(wins[u][1], KEYS), :] for u, hh in items]
    v = [vb_ref[hh, pl.ds(wins[u][1], KEYS), :] for u, hh in items]
    s = [_dot(qi, ki, NT) + _row_bias(tb_ref, hh, wins[u][2]) for qi, ki, (u, hh) in zip(q, k, items)]
    m = [jnp.max(si, axis=-1, keepdims=True) for si in s]
    e = [jnp.exp(si - mi) for si, mi in zip(s, m)]
    inv = [1.0 / jnp.sum(ei, axis=-1, keepdims=True) for ei in e]
    p = [ei * li for ei, li in zip(e, inv)]
    return wins, items, q, k, v, p


def _attn_in_specs():
    q = pl.BlockSpec((T, HP), lambda p: (0, p))
    k = pl.BlockSpec((T, HP), lambda p: (0, N_HP + p))
    v = pl.BlockSpec((T, HP), lambda p: (0, 2 * N_HP + p))
    tb = pl.BlockSpec((2, N_DPAIR, GRID_W, HP), lambda p: (p, 0, 0, 0))
    return q, k, v, tb


_HEAD_SCRATCH = pltpu.VMEM((2, T, HEAD_DIM), BF16)


_PROBS = pl.BlockSpec((T, 2 * KEYS), lambda p: (0, p))


def _attn_fwd(z, tb):
    def body(q_ref, k_ref, v_ref, tb_ref, o_ref, p_ref, qb_ref, kb_ref, vb_ref):
        _split_heads(q_ref, qb_ref, SCALE)
        _split_heads(k_ref, kb_ref)
        _split_heads(v_ref, vb_ref)

        def rows(it, carry):
            wins, items, _, _, v, p = _attn_items(qb_ref, kb_ref, vb_ref, tb_ref, it * ATT_UNROLL_FWD, ATT_UNROLL_FWD)
            pb = [pi.astype(BF16) for pi in p]
            o = [_dot(pi, vi, NN) for pi, vi in zip(pb, v)]
            for u, (q0, _, _) in enumerate(wins):
                o_ref[pl.ds(q0, GRID_W), :] = jnp.concatenate(o[2 * u:2 * u + 2], axis=1).astype(BF16)
                p_ref[pl.ds(q0, GRID_W), :] = jnp.concatenate(pb[2 * u:2 * u + 2], axis=1)
            return carry

        lax.fori_loop(0, N_ROWS // ATT_UNROLL_FWD, rows, 0)

    blk = pl.BlockSpec((T, HP), lambda p: (0, p))
    return pl.pallas_call(
        body, name="attn_fwd", grid=(N_HP,), in_specs=list(_attn_in_specs()), out_specs=[blk, _PROBS],
        out_shape=[_sds((T, D_ATT), BF16), _sds((T, N_HEADS * KEYS), BF16)], scratch_shapes=[_HEAD_SCRATCH] * 3,
        compiler_params=_params(("parallel",)))(z, z, z, tb)


def _attn_bwd(z, probs, d_att, after=()):
    def body(q_ref, k_ref, v_ref, p_ref, do_ref, flip_ref, *rest):
        (dq_ref, dk_ref, dv_ref, diag_ref, qb_ref, kb_ref, vb_ref, dob_ref, dka_ref, dva_ref,
         ds_ref) = rest[len(after):]
        _split_heads(q_ref, qb_ref, SCALE)
        _split_heads(k_ref, kb_ref)
        _split_heads(v_ref, vb_ref)
        _split_heads(do_ref, dob_ref)
        dka_ref[...] = jnp.zeros_like(dka_ref)
        dva_ref[...] = jnp.zeros_like(dva_ref)
        ds_ref[...] = jnp.zeros_like(ds_ref)

        def rows(it, carry):
            wins = [_row_window(it * ATT_UNROLL_BWD + u) for u in range(ATT_UNROLL_BWD)]
            items = [(u, hh) for u in range(ATT_UNROLL_BWD) for hh in range(2)]
            q = [qb_ref[hh, pl.ds(wins[u][0], GRID_W), :] for u, hh in items]
            k = [kb_ref[hh, pl.ds(wins[u][1], KEYS), :] for u, hh in items]
            v = [vb_ref[hh, pl.ds(wins[u][1], KEYS), :] for u, hh in items]
            pb = [p_ref[pl.ds(wins[u][0], GRID_W), hh * KEYS:(hh + 1) * KEYS] for u, hh in items]
            p = [pi.astype(F32) for pi in pb]
            do = [dob_ref[hh, pl.ds(wins[u][0], GRID_W), :] for u, hh in items]
            dv = [_dot(pi, di, TN) for pi, di in zip(pb, do)]
            dp = [_dot(di, vi, NT) for di, vi in zip(do, v)]
            ds = [pi * (dpi - jnp.sum(dpi * pi, axis=-1, keepdims=True)) for pi, dpi in zip(p, dp)]
            dsb = [d.astype(BF16) for d in ds]
            dq = [_dot(d, ki, NN) * SCALE for d, ki in zip(dsb, k)]
            dk = [_dot(d, qi, TN) for d, qi in zip(dsb, q)]
            for d, (u, hh) in zip(ds, items):
                for ii in range(WIN_H // 2):
                    ds_ref[hh, wins[u][2] + 2 * ii] += d[:, ii * HP:(ii + 1) * HP]
            for u, (q0, k0, _) in enumerate(wins):
                dq_ref[pl.ds(q0, GRID_W), :] = jnp.concatenate(dq[2 * u:2 * u + 2], axis=1).astype(BF16)
                dka_ref[pl.ds(k0, KEYS), :] += jnp.concatenate(dk[2 * u:2 * u + 2], axis=1)
                dva_ref[pl.ds(k0, KEYS), :] += jnp.concatenate(dv[2 * u:2 * u + 2], axis=1)
            return carry

        lax.fori_loop(0, N_ROWS // ATT_UNROLL_BWD, rows, 0)
        dk_ref[...] = dka_ref[...].astype(BF16)
        dv_ref[...] = dva_ref[...].astype(BF16)
        _diag_sums(ds_ref, flip_ref, diag_ref)

    blk = pl.BlockSpec((T, HP), lambda p: (0, p))
    q, k, v, _ = _attn_in_specs()
    flip =jnp.asarray(np.eye(HP, dtype=np.float32)[::-1], BF16)
    return pl.pallas_call(
        body, name="attn_bwd", grid=(N_HP,),
        in_specs=[q, k, v, _PROBS, blk, pl.BlockSpec((HP, HP), lambda p: (0, 0))] + [_ANY] * len(after),
        out_specs=[blk, blk, blk, pl.BlockSpec((None, DIAG_ROWS, HP), lambda p: (p, 0, 0))],
        out_shape=[_sds((T, D_ATT), BF16)] * 3 + [_sds((N_HP, DIAG_ROWS, HP), F32)],
        scratch_shapes=[_HEAD_SCRATCH] * 4 + [pltpu.VMEM((T, HP), F32), pltpu.VMEM((T, HP), F32),
                                              pltpu.VMEM((2, N_DPAIR, GRID_W, HP), F32)],
        compiler_params=_params(("parallel",)))(z, z, z, probs, d_att, flip, *after)


def _diag_sums(acc_ref, flip_ref, out_ref):
    flip = flip_ref[...]
    rows = []
    for hh in range(2):
        for pair in range(N_DPAIR):
            reversed_lanes = sum(_dot(part, flip, NN) for part in _split3(acc_ref[hh, pair]))
            skewed = pltpu.roll(reversed_lanes, 0, 1, stride=1, stride_axis=0)
            rows.append(jnp.sum(skewed, axis=0, keepdims=True))
    rows.append(jnp.zeros((DIAG_ROWS - len(rows), HP), F32))
    out_ref[...] = jnp.concatenate(rows, axis=0)


def _rpb_grad(diag_sums):
    g = diag_sums.reshape(N_HP * DIAG_ROWS, HP)
    sel = np.zeros((2, 128, N_HP * DIAG_ROWS), np.float32)
    lane = np.zeros((2, HP, 128), np.float32)
    for h in range(N_HEADS):
        for pair in range(N_DPAIR):
            for half in range(2):
                sel[half, h * N_DROW + pair + half, (h // 2) * DIAG_ROWS + (h % 2) * N_DPAIR + pair] = 1.0
    for j in range(2 * WIN_W - 1):
        for half in range(2):
            lane[half, (HP - 1 - GRID_W * half - (j - (WIN_W - 1))) % HP, j] = 1.0

    def body(g_ref, sel_ref, lane_ref, o_ref):
        parts = _split3(g_ref[...])
        total = None
        for half in range(2):
            picked = sum(_dot(sel_ref[half], part, NN) for part in parts)
            term = sum(_dot(part, lane_ref[half], NN) for part in _split3(picked))
            total = term if total is None else total + term
        o_ref[...] = total

    out = pl.pallas_call(body, name="rpb_grad", out_shape=_sds((128, 128), F32),
                         compiler_params=_params())(g, jnp.asarray(sel, BF16), jnp.asarray(lane, BF16))
    return out[:N_HEADS * N_DROW, :2 * WIN_W - 1].reshape(N_HEADS, N_DROW, 2 * WIN_W - 1)


N_CB = D // REC_CB
N_CHUNK = T // REC_CHUNK
N_TILE = T // 8
_U_BLK = 1536 // REC_CB
_Y_BLK = 2560 // REC_CB


def _block_diag(w):
    per = REC_CB // 64
    wt = w.reshape(2, N_CB, per, 64, 64)
    eye = jnp.eye(per, dtype=w.dtype)
    full = wt[:, :, :, :, None, :] * eye[None, None, :, None, :, None]
    return full.reshape(2, N_CB, REC_CB, REC_CB).astype(BF16)


def _block_diag_grad(g):
    per = REC_CB // 64
    g6 = g.reshape(2, N_CB, per, 64, per, 64)
    return jnp.stack([g6[:, :, p, :, p, :] for p in range(per)], axis=2).reshape(2, 16, 64, 64)


def _gelu(x):
    c = 0.7978845608028654
    return 0.5 * x * (1.0 + jnp.tanh(c * (x + 0.044715 * x * x * x)))


def _gelu_grad(x):
    c = 0.7978845608028654
    th = jnp.tanh(c * (x + 0.044715 * x * x * x))
    return 0.5 * (1.0 + th) + 0.5 * x * (1.0 - th * th) * c * (1.0 + 3.0 * 0.044715 * x * x)


def _softplus_neg(lam):
    x = -lam
    e = jnp.exp(-jnp.abs(x))
    w = 1.0 + e
    l1p = jnp.where(w == 1.0, e, jnp.log(w) * e / (w - 1.0))
    return jnp.maximum(x, 0.0) + l1p


def _one_minus_exp(x):
    poly = x * (1.0 + x * (1 / 2 + x * (1 / 6 + x * (1 / 24 + x * (1 / 120 + x * (1 / 720))))))
    return jnp.where(x > -0.125, -poly, 1.0 - jnp.exp(x))


def _conv_taps(pad_ref, t0, w, sign):
    out = None
    for j in range(4):
        term = w[j:j + 1, :] * pad_ref[pl.ds(PAD + t0 + sign * (j - 2), REC_CHUNK), :]
        out = term if out is None else out + term
    return out


def _gates(u, wa, wi, ba, bi, sp):
    ub = u.astype(BF16)
    r = _sigmoid(_dot(ub, wa, NN) + ba)
    i = _sigmoid(_dot(ub, wi, NN) + bi)
    log_a = -LRU_C * r * sp
    a = jnp.exp(log_a)
    x = jnp.maximum(_one_minus_exp(2.0 * log_a), 0.0)
    positive = x > 0.0
    inv = lax.rsqrt(jnp.where(positive, x, 1.0))
    mult = jnp.where(positive, x * inv, 0.0)
    return r, i, a, mult, jnp.where(positive, inv, 0.0)


def _tile_scan(a, b, sub, reverse):
    for s in (1, 2, 4):
        if reverse:
            a_s, b_s, m = pltpu.roll(a, 8 - s, 0), pltpu.roll(b, 8 - s, 0), sub < 8 - s
        else:
            a_s, b_s, m = pltpu.roll(a, s, 0), pltpu.roll(b, s, 0), sub >= s
        b = jnp.where(m, a * b_s + b, b)
        a = jnp.where(m, a * a_s, a)
    return a, b


def _last_row(x, row):
    return jnp.broadcast_to(x[row:row + 1, :], x.shape)


def _rec_prologue(up_ref, cw_ref, cb_ref, wa_ref, wi_ref, ba_ref, bi_ref, lam_ref,
                  upad_ref, u_ref, a_refs, h_refs):
    cb = up_ref.shape[1]
    zeros = jnp.zeros((PAD, cb), F32)
    upad_ref[pl.ds(0, PAD), :] = zeros
    upad_ref[pl.ds(PAD + T, PAD), :] = zeros
    upad_ref[pl.ds(PAD, T), :] = up_ref[...]
    cw = cw_ref[...]
    sp = _softplus_neg(lam_ref[...])
    for c in range(N_CHUNK):
        t0 = c * REC_CHUNK
        u = cb_ref[...] + _conv_taps(upad_ref, t0, cw, 1)
        u_ref[pl.ds(t0, REC_CHUNK), :] = u
        for d in range(2):
            _, i, a, mult, _ = _gates(u, wa_ref[d], wi_ref[d], ba_ref[d:d + 1, :], bi_ref[d:d + 1, :], sp[d:d + 1, :])
            a_refs[d][pl.ds(t0, REC_CHUNK), :] = a
            h_refs[d][pl.ds(t0, REC_CHUNK), :] = mult * (i * u)

    sub = lax.broadcasted_iota(jnp.int32, (8, cb), 0)

    def tile(k, carry):
        cf, cr = carry
        tf = pl.multiple_of(k * 8, 8)
        tr = pl.multiple_of((N_TILE - 1 - k) * 8, 8)
        af, bf = _tile_scan(a_refs[0][pl.ds(tf, 8), :], h_refs[0][pl.ds(tf, 8), :], sub, False)
        hf = af * cf + bf
        h_refs[0][pl.ds(tf, 8), :] = hf
        ar, br = _tile_scan(a_refs[1][pl.ds(tr, 8), :], h_refs[1][pl.ds(tr, 8), :], sub, True)
        hr = ar * cr + br
        h_refs[1][pl.ds(tr, 8), :] = hr
        return _last_row(af, 7) * cf + _last_row(bf, 7), _last_row(ar, 0) * cr + _last_row(br, 0)

    z8 = jnp.zeros((8, cb), F32)
    lax.fori_loop(0, N_TILE, tile, (z8, z8))
    return sp


def _rec_specs():
    up = pl.BlockSpec((T, REC_CB), lambda c: (0, _U_BLK + c))
    yb = pl.BlockSpec((T, REC_CB), lambda c: (0, _Y_BLK + c))
    cw = pl.BlockSpec((4, REC_CB), lambda c: (0, c))
    cbias = pl.BlockSpec((1, REC_CB), lambda c: (0, c))
    wbd = pl.BlockSpec((2, None, REC_CB, REC_CB), lambda c: (0, c, 0, 0))
    vec2 = pl.BlockSpec((2, REC_CB), lambda c: (0, c))
    col = pl.BlockSpec((T, REC_CB), lambda c: (0, c))
    return up, yb, cw, cbias, wbd, vec2, col


def _rec_fwd(z, conv_w, conv_b, wa, wi, ba, bi, lam):
    up, yb, cw, cbias, wbd, vec2, col = _rec_specs()

    def body(up_ref, yb_ref, cw_ref, cb_ref, wa_ref, wi_ref, ba_ref, bi_ref, lam_ref, g_ref,
             u_ref, af_ref, ar_ref, hf_ref, hr_ref, upad_ref):
        _rec_prologue(up_ref, cw_ref, cb_ref, wa_ref, wi_ref, ba_ref, bi_ref, lam_ref,
                      upad_ref, u_ref, (af_ref, ar_ref), (hf_ref, hr_ref))

        def chunk(c, carry):
            t0 = pl.multiple_of(c * REC_CHUNK, REC_CHUNK)
            rows = pl.ds(t0, REC_CHUNK)
            g_ref[rows, :] = ((hf_ref[rows, :] + hr_ref[rows, :]) * _gelu(yb_ref[rows, :])).astype(BF16)
            return carry

        lax.fori_loop(0, N_CHUNK, chunk, 0)

    res = pl.pallas_call(
        body, name="rec_fwd", grid=(N_CB,),
        in_specs=[up, yb, cw, cbias, wbd, wbd, vec2, vec2, vec2], out_specs=[col] * 6,
        out_shape=[_sds((T, D), BF16)] + [_sds((T, D), F32)] * 5,
        scratch_shapes=[pltpu.VMEM((T + 2 * PAD, REC_CB), F32)],
        compiler_params=_params(("parallel",)))(z, z, conv_w, conv_b, wa, wi, ba, bi, lam)
    return res[0], tuple(res[1:])


def _rec_bwd(z, dg, saved, conv_w, conv_b, wa, wi, ba, bi, lam, after=()):
    up, yb, cw, cbias, wbd, vec2, col = _rec_specs()

    def body(up_ref, yb_ref, dg_ref, u_ref, af_ref, ar_ref, hf_ref, hr_ref,
             cw_ref, cb_ref, wa_ref, wi_ref, ba_ref, bi_ref, lam_ref, *rest):
        (dup_ref, dyb_ref, dcw_ref, dcb_ref, dwa_ref, dwi_ref, dba_ref, dbi_ref, dlam_ref,
         upad_ref, dh_ref, gf_ref, gr_ref, daf_ref, dar_ref, dupad_ref) = rest[len(after):]
        g_refs, da_refs = (gf_ref, gr_ref), (daf_ref, dar_ref)
        cb = up_ref.shape[1]
        zeros = jnp.zeros((PAD, cb), F32)
        upad_ref[pl.ds(0, PAD), :] = zeros
        upad_ref[pl.ds(PAD + T, PAD), :] = zeros
        upad_ref[pl.ds(PAD, T), :] = up_ref[...]
        sp = _softplus_neg(lam_ref[...])

        def gate_chunk(c, carry):
            t0 = pl.multiple_of(c * REC_CHUNK, REC_CHUNK)
            rows = pl.ds(t0, REC_CHUNK)
            y = yb_ref[rows, :]
            dgv = dg_ref[rows, :].astype(F32)
            dh_ref[rows, :] = dgv * _gelu(y)
            dyb_ref[rows, :] = (dgv * (hf_ref[rows, :] + hr_ref[rows, :]) * _gelu_grad(y)).astype(BF16)
            return carry

        lax.fori_loop(0, N_CHUNK, gate_chunk, 0)

        sub = lax.broadcasted_iota(jnp.int32, (8, cb), 0)

        def tile(k, carry):
            cf, cr = carry
            kf = N_TILE - 1 - k
            tf = pl.multiple_of(kf * 8, 8)
            tnext = pl.multiple_of(jnp.minimum(kf + 1, N_TILE - 1) * 8, 8)
            tprev = pl.multiple_of(jnp.maximum(kf - 1, 0) * 8, 8)
            a_t = af_ref[pl.ds(tf, 8), :]
            a_n = jnp.where(kf < N_TILE - 1, af_ref[pl.ds(tnext, 8), :], 0.0)
            a_sh = jnp.where(sub == 7, pltpu.roll(a_n, 7, 0), pltpu.roll(a_t, 7, 0))
            ca, cbb = _tile_scan(a_sh, dh_ref[pl.ds(tf, 8), :], sub, True)
            gf = ca * cf + cbb
            h_t = hf_ref[pl.ds(tf, 8), :]
            h_p = jnp.where(kf > 0, hf_ref[pl.ds(tprev, 8), :], 0.0)
            h_sh = jnp.where(sub == 0, pltpu.roll(h_p, 1, 0), pltpu.roll(h_t, 1, 0))
            gf_ref[pl.ds(tf, 8), :] = gf
            daf_ref[pl.ds(tf, 8), :] = gf * h_sh
            tr = pl.multiple_of(k * 8, 8)
            rnext = pl.multiple_of(jnp.minimum(k + 1, N_TILE - 1) * 8, 8)
            rprev = pl.multiple_of(jnp.maximum(k - 1, 0) * 8, 8)
            b_t = ar_ref[pl.ds(tr, 8), :]
            b_p = jnp.where(k > 0, ar_ref[pl.ds(rprev, 8), :], 0.0)
            b_sh = jnp.where(sub == 0, pltpu.roll(b_p, 1, 0), pltpu.roll(b_t, 1, 0))
            ra, rb = _tile_scan(b_sh, dh_ref[pl.ds(tr, 8), :], sub, False)
            gr = ra * cr + rb
            hr_t = hr_ref[pl.ds(tr, 8), :]
            hr_n = jnp.where(k < N_TILE - 1, hr_ref[pl.ds(rnext, 8), :], 0.0)
            hr_sh = jnp.where(sub == 7, pltpu.roll(hr_n, 7, 0), pltpu.roll(hr_t, 7, 0))
            gr_ref[pl.ds(tr, 8), :] = gr
            dar_ref[pl.ds(tr, 8), :] = gr * hr_sh
            return _last_row(gf, 0), _last_row(gr, 7)

        z8 = jnp.zeros((8, cb), F32)
        lax.fori_loop(0, N_TILE, tile, (z8, z8))

        dupad_ref[pl.ds(0, PAD), :] = zeros
        dupad_ref[pl.ds(PAD + T, PAD), :] = zeros
        dwa_ref[...] = jnp.zeros_like(dwa_ref)
        dwi_ref[...] = jnp.zeros_like(dwi_ref)
        dba_ref[...] = jnp.zeros_like(dba_ref)
        dbi_ref[...] = jnp.zeros_like(dbi_ref)
        dlam_ref[...] = jnp.zeros_like(dlam_ref)

        def grad_chunk(c, carry):
            t0 = pl.multiple_of(c * REC_CHUNK, REC_CHUNK)
            rows = pl.ds(t0, REC_CHUNK)
            u = u_ref[rows, :]
            ub = u.astype(BF16)
            du = jnp.zeros((REC_CHUNK, cb), F32)
            for d in range(2):
                r, i, a, mult, inv_mult = _gates(u, wa_ref[d], wi_ref[d], ba_ref[d:d + 1, :], bi_ref[d:d + 1, :],
                                                 sp[d:d + 1, :])
                dbx = g_refs[d][rows, :]
                dmult = dbx * (i * u)
                diu = dbx * mult
                a2 = a * a
                dlog = da_refs[d][rows, :] * a - dmult * (a2 * inv_mult)
                dpa = (dlog * (-LRU_C) * sp[d:d + 1, :]) * r * (1.0 - r)
                dpi = (diu * u) * i * (1.0 - i)
                dpab, dpib = dpa.astype(BF16), dpi.astype(BF16)
                du = du + diu * i + _dot(dpab, wa_ref[d], NT) + _dot(dpib, wi_ref[d], NT)
                dwa_ref[d] += _dot(ub, dpab, TN)
                dwi_ref[d] += _dot(ub, dpib, TN)
                dba_ref[d:d + 1, :] += jnp.sum(dpa, axis=0, keepdims=True)
                dbi_ref[d:d + 1, :] += jnp.sum(dpi, axis=0, keepdims=True)
                dlam_ref[d:d + 1, :] += jnp.sum(dlog * r, axis=0, keepdims=True)
            dupad_ref[pl.ds(PAD + t0, REC_CHUNK), :] = du
            return carry

        lax.fori_loop(0, N_CHUNK, grad_chunk, 0)
        dlam_ref[...] = dlam_ref[...] * (LRU_C * _sigmoid(-lam_ref[...]))

        cw = cw_ref[...]
        dcb = jnp.zeros((1, cb), F32)
        dcw = [jnp.zeros((1, cb), F32) for _ in range(4)]
        for c in range(N_CHUNK):
            t0 = c * REC_CHUNK
            du = dupad_ref[pl.ds(PAD + t0, REC_CHUNK), :]
            dcb = dcb + jnp.sum(du, axis=0, keepdims=True)
            for j in range(4):
                dcw[j] = dcw[j] + jnp.sum(du * upad_ref[pl.ds(PAD + t0 + j - 2, REC_CHUNK), :], axis=0, keepdims=True)
            dup_ref[pl.ds(t0, REC_CHUNK), :] = _conv_taps(dupad_ref, t0, cw, -1).astype(BF16)
        dcb_ref[...] = dcb
        dcw_ref[...] = jnp.concatenate(dcw, axis=0)

    full = pltpu.VMEM((T, REC_CB), F32)
    padded = pltpu.VMEM((T + 2 * PAD, REC_CB), F32)
    return pl.pallas_call(
        body, name="rec_bwd", grid=(N_CB,),
        in_specs=[up, yb] + [col] * 6 + [cw, cbias, wbd, wbd, vec2, vec2, vec2] + [_ANY] * len(after),
        out_specs=[col, col, cw, cbias, wbd, wbd, vec2, vec2, vec2],
        out_shape=[_sds((T, D), BF16), _sds((T, D), BF16), _sds((4, D), F32), _sds((1, D), F32),
                   _sds((2, N_CB, REC_CB, REC_CB), F32), _sds((2, N_CB, REC_CB, REC_CB), F32),
                   _sds((2, D), F32), _sds((2, D), F32), _sds((2, D), F32)],
        scratch_shapes=[padded, full, full, full, full, full, padded],
        compiler_params=_params(("parallel",)))(z, z, dg, *saved, conv_w, conv_b, wa, wi, ba, bi, lam, *after)


class _NoReducer:
    def begin(self, tag, grads):
        return ()

    def advance(self, tag, after):
        return ()

    def interlude(self, tokens):
        return None


def _local_step(x, target, p, late=None, reducer=_NoReducer()):
    x = x.reshape(T, D)
    target = target.reshape(T, D)
    tb = _bias_pairs(p["rpb"])
    wa, wi = _block_diag(p["w_rg_a"]), _block_diag(p["w_rg_i"])

    h1 = _rms_fwd("rms1_fwd", x, p["ln1_g"], after=late[0] if late else ())
    if late:
        p = {**p, **late[1]((h1, tb, wa, wi))}
    rec_params = (p["conv_w"], p["conv_b"], wa, wi, p["b_rg_a"], p["b_rg_i"], p["lru_lambda"])
    (z,) = _mm_nn_cols("mm_z", h1, p["w_in"], F32, bias=p["b_in"])
    att, probs = _attn_fwd(z, tb)
    g, rec_saved = _rec_fwd(z, *rec_params)
    if late:
        p = {**p, **late[2](g)}
    y_att, y_rec, mixed, x1, h2 = _branches_fwd(att, g, z, x, p["w_att_o"], p["w_rec_o"], p["w_out"], p["ln2_g"])

    def relu2(r, ex, outs):
        rp = jnp.maximum(r, 0.0)
        outs[0][...] = (rp * rp).astype(BF16)

    (s,) = _mm_nn_cols("mm_ff1", h2, p["w_ff1"], BF16, epilogue=relu2)
    loss, dx2, dx2_b, g_lnf = _mm_x2_loss_head(s, p["w_ff2"], x1, target, p["lnf_g"])

    def relu2_bwd(r, ex, outs):
        outs[0][...] = (r * 2.0 * jnp.sqrt(ex[0][...].astype(F32))).astype(BF16)

    (df,) = _mm_nt_rows("mm_df", dx2_b, p["w_ff2"], BF16, tn=D, extras=[s],
                        extra_specs=[pl.BlockSpec((TJ, D), lambda j, i, k: (i, j))], epilogue=relu2_bwd)
    (g_w_ff2,) = _mm_tn_rows("mm_g_ff2", s, dx2_b, tm=D)
    (g_w_ff1,) = _mm_tn_cols("mm_g_ff1", h2, df, D)
    tok = reducer.begin("ff", dict(w_ff2=g_w_ff2, w_ff1=g_w_ff1))
    dx1, dx1_b, g_ln2 = _mm_nt_cols_rms_bwd("mm_dh2_rms2_bwd", df, p["w_ff1"], x1, p["ln2_g"], dx2, after=tok,
                                            bf16_copy=True)

    dy_att, dy_rec, dg_att, dg_rec, d_att, d_g = _branches_bwd(dx1_b, y_att, y_rec, z, p["w_att_o"], p["w_rec_o"],
                                                               p["w_out"])
    (g_w_out,) = _mm_tn_rows("mm_g_out", mixed, dx1_b, tm=D)
    (g_w_att_o,) = _mm_tn_cols("mm_g_att_o", att, dy_att, D // N_CHIPS)
    (g_w_rec_o,) = _mm_tn_rows("mm_g_rec_o", g, dy_rec, tm=D)
    tok = reducer.advance("ff", g_w_rec_o) + reducer.begin("proj", dict(w_out=g_w_out, w_att_o=g_w_att_o, w_rec_o=g_w_rec_o))

    dq, dk, dv, ds_acc = _attn_bwd(z, probs, d_att, after=tok)
    g_rpb = _rpb_grad(ds_acc)
    tok = reducer.advance("proj", dq)
    d_up, d_yb, g_conv_w, g_conv_b, g_wa, g_wi, g_ba, g_bi, g_lam = _rec_bwd(z, d_g, rec_saved, *rec_params, after=tok)
    dz = jnp.concatenate([dq, dk, dv, d_up, d_yb, dg_att, dg_rec], axis=1)

    g_w_in, g_b_in = _mm_tn_cols("mm_g_in", h1, dz, D_IN // N_CHIPS, colsum=True)
    tok = reducer.advance("in", reducer.interlude(reducer.begin("in", dict(w_in=g_w_in))))
    grad_x, g_ln1 = _mm_nt_cols_rms_bwd("mm_dh1_rms1_bwd", dz, p["w_in"], x, p["ln1_g"], dx1, after=tok)

    grads = dict(ln1_g=g_ln1, w_in=g_w_in, b_in=g_b_in, rpb=g_rpb, w_att_o=g_w_att_o, conv_w=g_conv_w,
                 conv_b=g_conv_b, w_rg_a=_block_diag_grad(g_wa), b_rg_a=g_ba, w_rg_i=_block_diag_grad(g_wi),
                 b_rg_i=g_bi, lru_lambda=g_lam, w_rec_o=g_w_rec_o, w_out=g_w_out, ln2_g=g_ln2,
                 w_ff1=g_w_ff1, w_ff2=g_w_ff2, lnf_g=g_lnf)
    return loss, grad_x.reshape(1, T, D), grads


_ANY = pl.BlockSpec(memory_space=pl.ANY)
N_PEERS = N_CHIPS - 1


def _place():
    x, y, c = lax.axis_index("x"), lax.axis_index("y"), lax.axis_index("c")
    peers = [(1 - x, y), (x, 1 - y), (1 - x, 1 - y)]
    return x, y, c, 2 * x + y, peers


def _remote(src, dst, send_sem, recv_sem, dev):
    return pltpu.make_async_remote_copy(src_ref=src, dst_ref=dst, send_sem=send_sem, recv_sem=recv_sem,
                                        device_id=dev, device_id_type=MESH)


def _prefetch_call(body, name, ids, grid, in_specs, out_specs, out_shape, args, semantics=None):
    spec = pltpu.PrefetchScalarGridSpec(num_scalar_prefetch=1, grid=grid, in_specs=in_specs, out_specs=out_specs)
    return pl.pallas_call(body, name=name, grid_spec=spec, out_shape=out_shape,
                          compiler_params=_params(semantics or ("parallel",) * len(grid)))(ids, *args)


def _cast_bf16(name, w, chip_id, after=()):
    rows, cols = w.shape
    rb = min(rows, 256)

    def body(ids_ref, w_ref, *rest):
        rest[-1][...] = w_ref[...].astype(BF16)

    return _prefetch_call(body, name, chip_id, (rows // rb,),
                          [pl.BlockSpec((rb, cols), lambda i, ids: (i, 0))] + [_ANY] * len(after),
                          pl.BlockSpec((None, rb, cols), lambda i, ids: (ids[0], i, 0)),
                          _sds((N_CHIPS, rows, cols), BF16), (w, *after))


def _dma_sems(*counts):
    return [pltpu.SemaphoreType.DMA((k,)) for k in counts]


_HBM = pl.BlockSpec(memory_space=pltpu.HBM)
_SEM = pl.BlockSpec(memory_space=pltpu.SEMAPHORE)
_SPLIT_COPY = pltpu.CompilerParams(has_side_effects=pltpu.SideEffectType.DATAFLOW_SIDE_EFFECTING)
SIBLING_ID = 0
_SPLIT_COPY_SIBLING = pltpu.CompilerParams(has_side_effects=pltpu.SideEffectType.DATAFLOW_SIDE_EFFECTING,
                                           collective_id=SIBLING_ID)


def _sibling_handshake():
    x, y, c = lax.axis_index("x"), lax.axis_index("y"), lax.axis_index("c")
    barrier = pltpu.get_barrier_semaphore()
    pl.semaphore_signal(barrier, inc=1, device_id=(x, y, 1 - c), device_id_type=MESH)
    pl.semaphore_wait(barrier, 1)


def _hbm(arrays):
    return [pltpu.with_memory_space_constraint(a, pltpu.HBM) for a in arrays]


def _hbm_like(arrays):
    return [pltpu.HBM(a.shape, a.dtype) for a in arrays]


def _halves(buf, c):
    half = buf.shape[1] // 2
    return pl.ds(c * half, half), pl.ds((1 - c) * half, half)


def _gather_start(name, slots):
    n = len(slots)
    nk = n * N_PEERS

    def body(*refs):
        bufs = refs[n:2 * n]
        send_sems, recv_sems, token = refs[2 * n:]
        x, y, c, chip, peers = _place()
        for t in range(n):
            mine, _ = _halves(bufs[t], c)
            for r, (px, py) in enumerate(peers):
                k = t * N_PEERS + r
                own = bufs[t].at[chip, mine]
                _remote(own, own, send_sems.at[k], recv_sems.at[k], (px, py, c)).start()
        token[...] = jnp.zeros_like(token)

    res = pl.pallas_call(
        body, name=name, in_specs=[_HBM] * n, out_specs=[_HBM] * n + [_SEM, _SEM, pl.BlockSpec(memory_space=pltpu.VMEM)],
        out_shape=_hbm_like(slots) + [pltpu.SemaphoreType.DMA((nk,)), pltpu.SemaphoreType.DMA((nk,)),
                                      _sds((8, 128), F32)],
        input_output_aliases={t: t for t in range(n)}, compiler_params=_SPLIT_COPY)(*_hbm(slots))
    return res[:n], (res[n], res[n + 1]), res[n + 2]


def _gather_wait(name, bufs, sems, after):
    n = len(bufs)
    after = tuple(after) if isinstance(after, (tuple, list)) else (after,)

    def body(*refs):
        ins = refs[:n]
        send_sems, recv_sems = refs[n], refs[n + 1]
        x, y, c, chip, peers = _place()
        for t in range(n):
            mine, _ = _halves(ins[t], c)
            for r, (px, py) in enumerate(peers):
                k = t * N_PEERS + r
                cp = _remote(ins[t].at[chip, mine], ins[t].at[2 * px + py, mine], send_sems.at[k], recv_sems.at[k],
                             (px, py, c))
                cp.wait_send()
                cp.wait_recv()

    return pl.pallas_call(
        body, name=name, in_specs=[_HBM] * n + [_SEM, _SEM] + [_ANY] * len(after), out_specs=[_HBM] * n,
        out_shape=_hbm_like(bufs), input_output_aliases={t: t for t in range(n)},
        compiler_params=_SPLIT_COPY)(*bufs, *sems, *after)


def _gather_forward(name, bufs):
    n = len(bufs)
    nk = n * N_PEERS

    def body(*refs):
        _sibling_handshake()
        outs = refs[n:2 * n]
        send_sems, recv_sems = refs[2 * n:]
        x, y, c, chip, peers = _place()
        sibling = (x, y, 1 - c)
        sends = []
        for t in range(n):
            mine, _ = _halves(outs[t], c)
            for r, (px, py) in enumerate(peers):
                k = t * N_PEERS + r
                landed = outs[t].at[2 * px + py, mine]
                sends.append(_remote(landed, landed, send_sems.at[k], recv_sems.at[k], sibling))
                sends[-1].start()
        for t in range(n):
            _, theirs = _halves(outs[t], c)
            for r, (px, py) in enumerate(peers):
                k = t * N_PEERS + r
                landed = outs[t].at[2 * px + py, theirs]
                _remote(landed, landed, send_sems.at[k], recv_sems.at[k], sibling).wait_recv()
        for cp in sends:
            cp.wait_send()

    return pl.pallas_call(
        body, name=name, in_specs=[_ANY] * n, out_specs=[_ANY] * n, out_shape=[_sds(b.shape, b.dtype) for b in bufs],
        input_output_aliases={t: t for t in range(n)}, scratch_shapes=_dma_sems(nk, nk),
        compiler_params=pltpu.CompilerParams(collective_id=SIBLING_ID))(*bufs)


def _pair_copies(n, srcs, lands, send_sems, recv_sems):
    x, y, c, _, _ = _place()
    sibling = (x, y, 1 - c)
    copies = []
    for t in range(n):
        half = srcs[t].shape[1] // 2
        for j in range(N_CHIPS):
            k = t * N_CHIPS + j
            copies.append(_remote(srcs[t].at[j, pl.ds((1 - c) * half, half)], lands[t].at[j],
                                  send_sems.at[k], recv_sems.at[k], sibling))
    for t in range(n, len(srcs)):
        k = n * N_CHIPS + t - n
        copies.append(_remote(srcs[t], lands[t], send_sems.at[k], recv_sems.at[k], sibling))
    return copies


def _pair_start(name, grads, wholes=()):
    n = len(grads)
    srcs = list(grads) + list(wholes)
    m = len(srcs)
    lands = [pltpu.HBM((N_CHIPS, g.shape[1] // 2, g.shape[2]), F32) for g in grads] + _hbm_like(wholes)
    ns = n * N_CHIPS + len(wholes)

    def body(*refs):
        _sibling_handshake()
        src_refs, land_refs = refs[m:2 * m], refs[2 * m:3 * m]
        send_sems, recv_sems, token = refs[3 * m:]
        for cp in _pair_copies(n, src_refs, land_refs, send_sems, recv_sems):
            cp.start()
        token[...] = jnp.zeros_like(token)

    res = pl.pallas_call(
        body, name=name, in_specs=[_HBM] * m,
        out_specs=[_HBM] * (2 * m) + [_SEM, _SEM, pl.BlockSpec(memory_space=pltpu.VMEM)],
        out_shape=_hbm_like(srcs) + lands + [pltpu.SemaphoreType.DMA((ns,)), pltpu.SemaphoreType.DMA((ns,)),
                                             _sds((8, 128), F32)],
        input_output_aliases={t: t for t in range(m)}, compiler_params=_SPLIT_COPY_SIBLING)(*_hbm(srcs))
    return (res[:m], res[m:2 * m], (res[2 * m], res[2 * m + 1])), res[2 * m + 2]


def _pair_wait(name, flight, n, after):
    srcs, lands, sems = flight
    m = len(srcs)

    def body(*refs):
        for cp in _pair_copies(n, refs[:m], refs[m:2 * m], refs[2 * m], refs[2 * m + 1]):
            cp.wait_send()
            cp.wait_recv()

    res = pl.pallas_call(
        body, name=name, in_specs=[_HBM] * (2 * m) + [_SEM, _SEM, _ANY], out_specs=[_HBM] * (2 * m),
        out_shape=_hbm_like(srcs) + _hbm_like(lands), input_output_aliases={t: t for t in range(2 * m)},
        compiler_params=_SPLIT_COPY)(*srcs, *lands, *sems, after)
    return res[:m], res[m:]


def _chip_copies(srcs, lands, small_src, small_land, send_sems, recv_sems):
    x, y, c, chip, peers = _place()
    n = len(srcs)
    copies = []
    for r, (px, py) in enumerate(peers):
        for t in range(n):
            k = t * N_PEERS + r
            copies.append(_remote(srcs[t].at[2 * px + py], lands[t].at[r], send_sems.at[k], recv_sems.at[k], (px, py, c)))
        if small_src is not None:
            k = n * N_PEERS + r
            half_s = small_src.shape[0] // 2
            copies.append(_remote(small_src.at[pl.ds(c * half_s, half_s)], small_land.at[r],
                                  send_sems.at[k], recv_sems.at[k], (px, py, c)))
    return copies


def _chip_start(name, sums_bf16, small=None):
    n = len(sums_bf16)
    srcs = list(sums_bf16) + ([small] if small is not None else [])
    m = len(srcs)
    lands = [pltpu.HBM((N_PEERS,) + s.shape[1:], BF16) for s in sums_bf16]
    if small is not None:
        lands.append(pltpu.HBM((N_PEERS, small.shape[0] // 2, 128), F32))
    nk = m * N_PEERS

    def body(*refs):
        src_refs, land_refs = refs[m:2 * m], refs[2 * m:3 * m]
        send_sems, recv_sems, token = refs[3 * m:]
        small_src, small_land = (src_refs[n], land_refs[n]) if small is not None else (None, None)
        for cp in _chip_copies(src_refs[:n], land_refs[:n], small_src, small_land, send_sems, recv_sems):
            cp.start()
        token[...] = jnp.zeros_like(token)

    res = pl.pallas_call(
        body, name=name, in_specs=[_HBM] * m,
        out_specs=[_HBM] * (2 * m) + [_SEM, _SEM, pl.BlockSpec(memory_space=pltpu.VMEM)],
        out_shape=_hbm_like(srcs) + lands + [pltpu.SemaphoreType.DMA((nk,)), pltpu.SemaphoreType.DMA((nk,)),
                                             _sds((8, 128), F32)],
        input_output_aliases={t: t for t in range(m)}, compiler_params=_SPLIT_COPY)(*_hbm(srcs))
    return (res[:m], res[m:2 * m], (res[2 * m], res[2 * m + 1])), res[2 * m + 2]


def _chip_wait(name, flight, with_small, after):
    srcs, lands, sems = flight
    m = len(srcs)
    n = m - 1 if with_small else m

    def body(*refs):
        src_refs, land_refs = refs[:m], refs[m:2 * m]
        send_sems, recv_sems = refs[2 * m], refs[2 * m + 1]
        small_src, small_land = (src_refs[n], land_refs[n]) if with_small else (None, None)
        for cp in _chip_copies(src_refs[:n], land_refs[:n], small_src, small_land, send_sems, recv_sems):
            cp.wait_send()
            cp.wait_recv()

    res = pl.pallas_call(
        body, name=name, in_specs=[_HBM] * (2 * m) + [_SEM, _SEM, _ANY], out_specs=[_HBM] * (2 * m),
        out_shape=_hbm_like(srcs) + _hbm_like(lands), input_output_aliases={t: t for t in range(2 * m)},
        compiler_params=_SPLIT_COPY)(*srcs, *lands, *sems, after)
    return res[:m], res[m:]


def _swap_start(name, bufs):
    n = len(bufs)

    def body(*refs):
        _sibling_handshake()
        outs = refs[n:2 * n]
        send_sems, recv_sems, token = refs[2 * n:]
        x, y, c, _, _ = _place()
        for t in range(n):
            h = outs[t].shape[0] // 2
            mine = outs[t].at[pl.ds(c * h, h)]
            _remote(mine, mine, send_sems.at[t], recv_sems.at[t], (x, y, 1 - c)).start()
        token[...] = jnp.zeros_like(token)

    res = pl.pallas_call(
        body, name=name, in_specs=[_HBM] * n, out_specs=[_HBM] * n + [_SEM, _SEM, pl.BlockSpec(memory_space=pltpu.VMEM)],
        out_shape=_hbm_like(bufs) + [pltpu.SemaphoreType.DMA((n,)), pltpu.SemaphoreType.DMA((n,)), _sds((8, 128), F32)],
        input_output_aliases={t: t for t in range(n)}, compiler_params=_SPLIT_COPY_SIBLING)(*_hbm(bufs))
    return (res[:n], (res[n], res[n + 1])), res[n + 2]


def _swap_wait(name, flight, after):
    bufs, sems = flight
    n = len(bufs)

    def body(*refs):
        ins = refs[:n]
        send_sems, recv_sems = refs[n], refs[n + 1]
        x, y, c, _, _ = _place()
        for t in range(n):
            h = ins[t].shape[0] // 2
            cp = _remote(ins[t].at[pl.ds(c * h, h)], ins[t].at[pl.ds((1 - c) * h, h)], send_sems.at[t],
                         recv_sems.at[t], (x, y, 1 - c))
            cp.wait_send()
            cp.wait_recv()

    return pl.pallas_call(
        body, name=name, in_specs=[_HBM] * n + [_SEM, _SEM, _ANY], out_specs=[_HBM] * n, out_shape=_hbm_like(bufs),
        input_output_aliases={t: t for t in range(n)}, compiler_params=_SPLIT_COPY)(*bufs, *sems, after)


def _pair_sum(name, grad, got, ids):
    _, rows, cols = got.shape
    rb = min(rows, 256)
    nb = rows // rb
    blk = pl.BlockSpec((None, rb, cols), lambda i, j, ids: (j, i, 0))
    mine = pl.BlockSpec((None, rb, cols), lambda i, j, ids: (j, ids[1] * nb + i, 0))
    own = pl.BlockSpec((rb, cols), lambda i, j, ids: (i, 0))

    def body(ids_ref, a_ref, b_ref, s_ref, sb_ref):
        s = a_ref[...] + b_ref[...]
        sb_ref[...] = s.astype(BF16)

        @pl.when(pl.program_id(1) == ids_ref[0])
        def _():
            s_ref[...] = s

    return _prefetch_call(body, name, ids, (nb, N_CHIPS), [mine, blk], [own, blk],
                          [_sds((rows, cols), F32), _sds(got.shape, BF16)], (grad, got),
                          semantics=("parallel", "arbitrary"))


def _chip_sum(name, own_sum, got, ids):
    rows, cols = own_sum.shape
    rb = min(rows, 256)
    nb = rows // rb
    own = pl.BlockSpec((rb, cols), lambda i, ids: (i, 0))
    blk3 = pl.BlockSpec((N_PEERS, rb, cols), lambda i, ids: (0, i, 0))
    out = pl.BlockSpec((rb, cols), lambda i, ids: (ids[1] * nb + i, 0))

    def body(ids_ref, a_ref, b_ref, o_ref):
        o_ref[...] = ((a_ref[...] + b_ref[0].astype(F32)) + b_ref[1].astype(F32)) + b_ref[2].astype(F32)

    return _prefetch_call(body, name, ids, (nb,), [own, blk3], out, _sds((2 * rows, cols), F32), (own_sum, got))


SMALL_RB = 280


def _small_pair_sum(own, got):
    blk = pl.BlockSpec((SMALL_RB, 128), lambda i: (i, 0))

    def body(a_ref, b_ref, o_ref):
        o_ref[...] = a_ref[...] + b_ref[...]

    return pl.pallas_call(body, name="small_pair_sum", grid=(own.shape[0] // SMALL_RB,), in_specs=[blk, blk],
                          out_specs=blk, out_shape=_sds(own.shape, F32),
                          compiler_params=_params(("parallel",)))(own, got)


def _small_chip_sum(pair, got, ids):
    nb = pair.shape[0] // 2 // SMALL_RB
    half = pl.BlockSpec((SMALL_RB, 128), lambda i, ids: (ids[1] * nb + i, 0))
    blk3 = pl.BlockSpec((N_PEERS, SMALL_RB, 128), lambda i, ids: (0, i, 0))

    def body(ids_ref, a_ref, b_ref, o_ref):
        o_ref[...] = (a_ref[...] + b_ref[1]) + (b_ref[0] + b_ref[2])

    return _prefetch_call(body, "small_chip_sum", ids, (nb,), [half, blk3], half, _sds(pair.shape, F32), (pair, got))


def _adamw_math(w, g, m, v):
    m = ADAM_B1 * m + (1.0 - ADAM_B1) * g
    v = ADAM_B2 * v + (1.0 - ADAM_B2) * (g * g)
    m_hat = m / (1.0 - ADAM_B1 ** ADAM_STEP)
    v_hat = v / (1.0 - ADAM_B2 ** ADAM_STEP)
    delta = -ADAM_LR * (m_hat / (jnp.sqrt(v_hat) + ADAM_EPS) + ADAM_WD * w)
    return delta, m, v


def _adamw(name, w, g, m, v, rb=None):
    rows, cols = w.shape
    rb = rows if rb is None else rb
    blk = pl.BlockSpec((rb, cols), lambda i: (i, 0))

    def body(w_ref, g_ref, m_ref, v_ref, d_ref, nm_ref, nv_ref):
        d, nm, nv = _adamw_math(w_ref[...], g_ref[...], m_ref[...], v_ref[...])
        d_ref[...] = d
        nm_ref[...] = nm
        nv_ref[...] = nv

    return pl.pallas_call(body, name=name, grid=(rows // rb,), in_specs=[blk] * 4, out_specs=[blk] * 3,
                          out_shape=[_sds(w.shape, F32)] * 3, compiler_params=_params(("parallel",)))(w, g, m, v)


def _adamw_small(ws, gs, ms, vs):
    n = len(ws)

    def body(*refs):
        for t in range(n):
            w_ref, g_ref, m_ref, v_ref = (refs[k * n + t] for k in range(4))
            d, nm, nv = _adamw_math(w_ref[...], g_ref[...], m_ref[...], v_ref[...])
            for k, val in enumerate((d, nm, nv)):
                refs[(4 + k) * n + t][...] = val

    res = pl.pallas_call(body, name="adamw_small", out_shape=[_sds(a.shape, F32) for a in ws] * 3,
                         compiler_params=_params())(*ws, *gs, *ms, *vs)
    return [(res[t], res[n + t], res[2 * n + t]) for t in range(n)]


BIG = ("w_in", "w_att_o", "w_rec_o", "w_out", "w_ff1", "w_ff2")
SHARDED_VECS = ("conv_w", "b_rg_a", "b_rg_i", "lru_lambda")
SMALL = ("ln1_g", "b_in", "rpb", "conv_w", "conv_b", "w_rg_a", "b_rg_a", "w_rg_i", "b_rg_i", "lru_lambda",
         "ln2_g", "lnf_g")
SMALL_ROWS = 2240
ORDER = ("ln1_g", "w_in", "b_in", "rpb", "w_att_o", "conv_w", "conv_b", "w_rg_a", "b_rg_a", "w_rg_i", "b_rg_i",
         "lru_lambda", "w_rec_o", "w_out", "ln2_g", "w_ff1", "w_ff2", "lnf_g")


def _pack_small(grads, loss):
    parts, sizes = [], {}
    for n in SMALL:
        flat = grads[n].reshape(-1)
        pad = (-flat.shape[0]) % 128
        sizes[n] = (flat.shape[0], flat.shape[0] + pad)
        parts.append(jnp.pad(flat, (0, pad)))
    total = sum(s[1] for s in sizes.values())
    parts.append(jnp.pad(loss.reshape(1), (0, SMALL_ROWS * 128 - total - 1)))
    return jnp.concatenate(parts).reshape(SMALL_ROWS, 128), sizes


def _unpack_small(buf, sizes, shapes):
    flat = buf.reshape(-1)
    out, pos = {}, 0
    for n in SMALL:
        size, padded = sizes[n]
        out[n] = flat[pos:pos + size].reshape(shapes[n])
        pos += padded
    return out, flat[pos]


def _gather_weights(w, chip):
    chip_id = chip.astype(jnp.int32).reshape(1)
    vec_rows = [w[n][0] for n in SHARDED_VECS]
    vec_shard = jnp.concatenate(vec_rows + [jnp.zeros((16 - 10, D // N_CHIPS), F32)], axis=0)
    vec_slots = lax.dynamic_update_slice(jnp.zeros((N_CHIPS, 16, D // N_CHIPS), F32), vec_shard[None], (chip, 0, 0))
    bufs_a, sems_a, token_a = _gather_start("gather_start_first", [_cast_bf16("cast_w_in", w["w_in"][0], chip_id), vec_slots])
    rest_names = BIG[1:]
    bufs_b, sems_b, token_b = _gather_start(
        "gather_start_rest", [_cast_bf16("cast_" + n, w[n][0], chip_id, after=(token_a,)) for n in rest_names])

    def first(after):
        w_in_full, vec_full = _gather_forward("gather_forward_first", _gather_wait("gather_wait_first", bufs_a, sems_a, after))
        vecs = vec_full.transpose(1, 0, 2).reshape(16, D)
        return dict(w_in=w_in_full, conv_w=vecs[0:4], b_rg_a=vecs[4:6], b_rg_i=vecs[6:8], lru_lambda=vecs[8:10])

    def rest(after):
        full = dict(zip(rest_names, _gather_forward("gather_forward_rest",
                                                    _gather_wait("gather_wait_rest", bufs_b, sems_b, after))))
        return dict(w_att_o=full["w_att_o"], w_ff1=full["w_ff1"], w_rec_o=full["w_rec_o"].reshape(D, D),
                    w_out=full["w_out"].reshape(D, D), w_ff2=full["w_ff2"].reshape(D_FF, D))

    p = dict(ln1_g=w["ln1_g"], b_in=w["b_in"], rpb=w["rpb"][0], conv_b=w["conv_b"], w_rg_a=w["w_rg_a"][0],
             w_rg_i=w["w_rg_i"][0], ln2_g=w["ln2_g"], lnf_g=w["lnf_g"].reshape(1, D))
    return p, ((token_b,), first, rest)


class _Reducer:
    def __init__(self, ids):
        self.ids = ids
        self.groups = {}

    def begin(self, tag, grads, small=None):
        names = list(grads)
        big = [grads[n].reshape(N_CHIPS, -1, grads[n].shape[-1]) for n in names]
        flight, token = _pair_start("pair_start_" + tag, big, [] if small is None else [small])
        self.groups[tag] = dict(names=names, pair=flight, small=small is not None)
        return (token,)

    def advance(self, tag, after):
        grp = self.groups[tag]
        n = len(grp["names"])
        mine, got = _pair_wait("pair_wait_" + tag, grp["pair"], n, after)
        sums = [_pair_sum("pair_sum_" + name, a, b, self.ids) for name, a, b in zip(grp["names"], mine, got)]
        small_sum = _small_pair_sum(mine[n], got[n]) if grp["small"] else None
        grp["chip"], token = _chip_start("chip_start_" + tag, [s[1] for s in sums], small_sum)
        grp["sums"] = [s[0] for s in sums]
        self.last_token = token
        return (token,)

    def finish(self, tag, after):
        grp = self.groups[tag]
        srcs, lands = _chip_wait("chip_wait_" + tag, grp["chip"], grp["small"], after)
        halves = [_chip_sum("chip_sum_" + name, s, b, self.ids) for name, s, b in zip(grp["names"], grp["sums"], lands)]
        if grp["small"]:
            halves.append(_small_chip_sum(srcs[-1], lands[-1], self.ids))
        grp["swap"], token = _swap_start("swap_start_" + tag, halves)
        return token

    def interlude(self, tokens):
        after = tokens[0]
        for tag in list(self.groups)[:-1]:
            after = self.finish(tag, after)
        return after

    def result(self, tag, after):
        return _swap_wait("swap_wait_" + tag, self.groups[tag]["swap"], after)


def kernel(x, ln1_g, w_in, b_in, rpb, w_att_o, conv_w, conv_b, w_rg_a, b_rg_a, w_rg_i, b_rg_i, lru_lambda, w_rec_o, w_out, ln2_g, w_ff1, w_ff2, lnf_g, loss_target, m_ln1_g, m_w_in, m_b_in, m_rpb, m_w_att_o, m_conv_w, m_conv_b, m_w_rg_a, m_b_rg_a, m_w_rg_i, m_b_rg_i, m_lru_lambda, m_w_rec_o, m_w_out, m_ln2_g, m_w_ff1, m_w_ff2, m_lnf_g, v_ln1_g, v_w_in, v_b_in, v_rpb, v_w_att_o, v_conv_w, v_conv_b, v_w_rg_a, v_b_rg_a, v_w_rg_i, v_b_rg_i, v_lru_lambda, v_w_rec_o, v_w_out, v_ln2_g, v_w_ff1, v_w_ff2, v_lnf_g):
    w = dict(ln1_g=ln1_g, w_in=w_in, b_in=b_in, rpb=rpb, w_att_o=w_att_o, conv_w=conv_w, conv_b=conv_b,
             w_rg_a=w_rg_a, b_rg_a=b_rg_a, w_rg_i=w_rg_i, b_rg_i=b_rg_i, lru_lambda=lru_lambda, w_rec_o=w_rec_o,
             w_out=w_out, ln2_g=ln2_g, w_ff1=w_ff1, w_ff2=w_ff2, lnf_g=lnf_g)
    m = dict(ln1_g=m_ln1_g, w_in=m_w_in, b_in=m_b_in, rpb=m_rpb, w_att_o=m_w_att_o, conv_w=m_conv_w,
             conv_b=m_conv_b, w_rg_a=m_w_rg_a, b_rg_a=m_b_rg_a, w_rg_i=m_w_rg_i, b_rg_i=m_b_rg_i,
             lru_lambda=m_lru_lambda, w_rec_o=m_w_rec_o, w_out=m_w_out, ln2_g=m_ln2_g, w_ff1=m_w_ff1,
             w_ff2=m_w_ff2, lnf_g=m_lnf_g)
    v = dict(ln1_g=v_ln1_g, w_in=v_w_in, b_in=v_b_in, rpb=v_rpb, w_att_o=v_w_att_o, conv_w=v_conv_w,
             conv_b=v_conv_b, w_rg_a=v_w_rg_a, b_rg_a=v_b_rg_a, w_rg_i=v_w_rg_i, b_rg_i=v_b_rg_i,
             lru_lambda=v_lru_lambda, w_rec_o=v_w_rec_o, w_out=v_w_out, ln2_g=v_ln2_g, w_ff1=v_w_ff1,
             w_ff2=v_w_ff2, lnf_g=v_lnf_g)
    chip = 2 * lax.axis_index("x") + lax.axis_index("y")
    ids = jnp.stack([chip, lax.axis_index("c")]).astype(jnp.int32)

    out_grad, out_delta, out_m, out_v = {}, {}, {}, {}

    def update(n, gn):
        shape, two_d = w[n].shape, gn.shape
        d, nm, nv = _adamw("adamw_" + n, w[n].reshape(two_d), gn, m[n].reshape(two_d), v[n].reshape(two_d), 256)
        out_grad[n], out_delta[n], out_m[n], out_v[n] = (gn.reshape(shape), d.reshape(shape), nm.reshape(shape),
                                                         nv.reshape(shape))
        return d

    reducer = _Reducer(ids)
    p, late = _gather_weights(w, chip)
    loss, grad_x, g = _local_step(x, loss_target, p, late, reducer)
    small, sizes = _pack_small(g, loss + reducer.last_token[:1, :1])
    after = reducer.begin("small", {}, small)[0]
    after = reducer.advance("small", after)[0]
    for tag in ("ff", "proj", "in"):
        if tag == "in":
            after = reducer.finish("in", after)
        for n, red in zip(reducer.groups[tag]["names"], reducer.result(tag, after)):
            after = update(n, red)
    (small_red,) = reducer.result("small", reducer.finish("small", after))
    gsmall, loss = _unpack_small(small_red, sizes, {n: g[n].shape for n in SMALL})
    two_d = {n: (int(np.prod(w[n].shape[:-1])), w[n].shape[-1]) for n in SMALL}
    for n in SHARDED_VECS:
        gsmall[n] = lax.dynamic_slice_in_dim(gsmall[n], chip * (D // N_CHIPS), D // N_CHIPS, axis=1)
    gs = [gsmall[n].reshape(two_d[n]) for n in SMALL]
    updates = _adamw_small([w[n].reshape(two_d[n]) for n in SMALL], gs, [m[n].reshape(two_d[n]) for n in SMALL],
                           [v[n].reshape(two_d[n]) for n in SMALL])
    for n, gn, (d, nm, nv) in zip(SMALL, gs, updates):
        shape = w[n].shape
        out_grad[n], out_delta[n], out_m[n], out_v[n] = (gn.reshape(shape), d.reshape(shape), nm.reshape(shape),
                                                         nv.reshape(shape))
    return (loss, grad_x, *[out_grad[n] for n in ORDER], *[out_delta[n] for n in ORDER],
            *[out_m[n] for n in ORDER], *[out_v[n] for n in ORDER])
```

```python
import numpy as np
import jax
import jax.numpy as jnp
from jax import lax
from jax.experimental import pallas as pl
from jax.experimental.pallas import tpu as pltpu

F32 = jnp.float32
BF16 = jnp.bfloat16

T = 2048
D = 1024
D_ATT = 512
D_IN = 5632
D_FF = 4096
N_HEADS = 8
HEAD_DIM = 64
GRID_W = 64
N_ROWS = T // GRID_W
WIN_H = 8
WIN_W = 16
KEYS = WIN_H * GRID_W
N_CHIPS = 4
EPS = 1e-6
LRU_C = 8.0
SCALE = HEAD_DIM ** -0.5
REC_CB = 256
REC_CHUNK = 256
PAD = 8

ADAM_LR = 0.001
ADAM_B1 = 0.9
ADAM_B2 = 0.999
ADAM_EPS = 1e-08
ADAM_WD = 0.01
ADAM_STEP = 10

VMEM_LIMIT = 56 * 1024 * 1024

NN = (((1,), (0,)), ((), ()))
NT = (((1,), (1,)), ((), ()))
TN = (((0,), (0,)), ((), ()))
MESH = pl.DeviceIdType.MESH


def _params(sem=None):
    return pltpu.CompilerParams(dimension_semantics=sem, vmem_limit_bytes=VMEM_LIMIT)


def _dot(a, b, dims):
    return lax.dot_general(a, b, dims, preferred_element_type=F32)


def _sigmoid(x):
    return 0.5 * jnp.tanh(0.5 * x) + 0.5


def _matmul(name, a, b, *, dims, grid, a_spec, b_spec, out_shapes, out_specs, acc_shape,
            extras=(), extra_specs=(), epilogue=None, colsum_spec=None, colsum_shape=None, after=(),
            semantics=("parallel", "parallel", "arbitrary"), epilogue_takes_first=False):
    nk = grid[2]
    n_extra = len(extras)
    n_out = len(out_shapes)
    with_colsum = colsum_spec is not None

    def body(a_ref, b_ref, *rest):
        ex = rest[:n_extra]
        rest = rest[:n_extra] + rest[n_extra + len(after):]
        outs = rest[n_extra:n_extra + n_out]
        pos = n_extra + n_out
        cs_out = rest[pos] if with_colsum else None
        pos += 1 if with_colsum else 0
        acc = rest[pos]
        cs_acc = rest[pos + 1] if with_colsum else None
        k = pl.program_id(2)
        first_tile = pl.program_id(0) == 0

        @pl.when(k == 0)
        def _():
            acc[...] = jnp.zeros_like(acc)
            if with_colsum:
                cs_acc[...] = jnp.zeros_like(cs_acc)

        bv = b_ref[...]
        acc[...] += _dot(a_ref[...].astype(BF16), bv.astype(BF16), dims)
        if with_colsum:
            cs_acc[...] += jnp.sum(bv.astype(F32), axis=0, keepdims=True)

        @pl.when(k == nk - 1)
        def _():
            r = acc[...]
            if epilogue is None:
                outs[0][...] = r.astype(outs[0].dtype)
            elif epilogue_takes_first:
                epilogue(r, ex, outs, first_tile)
            else:
                epilogue(r, ex, outs)
            if with_colsum:
                cs_out[...] = cs_acc[...]

    shapes = list(out_shapes)
    specs = list(out_specs)
    scratch = [pltpu.VMEM(acc_shape, F32)]
    if with_colsum:
        shapes.append(colsum_shape)
        specs.append(colsum_spec)
        scratch.append(pltpu.VMEM((1, acc_shape[1]), F32))
    res = pl.pallas_call(
        body, name=name, grid=grid,
        in_specs=[a_spec, b_spec, *extra_specs] + [_ANY] * len(after),
        out_specs=specs, out_shape=shapes, scratch_shapes=scratch,
        compiler_params=_params(semantics),
    )(a, b, *extras, *after)
    return res


def _sds(shape, dtype):
    return jax.ShapeDtypeStruct(shape, dtype)


TM = 1024
NI = T // TM
TJ = T
NJ = T // TJ


def _mm_nn_cols(name, a, wg, out_dtype, *, bias=None, extras=(), extra_specs=(), epilogue=None,
                out_shapes=None, out_specs=None):
    k_dim, n4 = wg.shape[1], wg.shape[2]
    ex, exs = list(extras), list(extra_specs)
    if bias is not None:
        ex = [bias] + ex
        exs = [pl.BlockSpec((1, n4), lambda j, i, k: (0, j))] + exs
        user_ep = epilogue

        def epilogue(r, e, outs):
            r = r + e[0][...]
            if user_ep is None:
                outs[0][...] = r.astype(outs[0].dtype)
            else:
                user_ep(r, e[1:], outs)
    if out_shapes is None:
        out_shapes = [_sds((T, N_CHIPS * n4), out_dtype)]
        out_specs = [pl.BlockSpec((TJ, n4), lambda j, i, k: (i, j))]
    return _matmul(
        name, a, wg, dims=NN, grid=(N_CHIPS, NJ, 1),
        a_spec=pl.BlockSpec((TJ, k_dim), lambda j, i, k: (i, 0)),
        b_spec=pl.BlockSpec((None, k_dim, n4), lambda j, i, k: (j, 0, 0)),
        out_shapes=out_shapes, out_specs=out_specs, acc_shape=(TJ, n4),
        extras=ex, extra_specs=exs, epilogue=epilogue)


def _mm_nt_cols_rms_bwd(name, a, wg, x, g, dres, after=(), bf16_copy=False):
    n4 = wg.shape[2]
    row = pl.BlockSpec((TM, D), lambda i, j, k: (i, 0))
    vec = pl.BlockSpec((1, D), lambda i, j, k: (0, 0))

    def epilogue(dhv, ex, outs, first):
        x_ref, g_ref, dres_ref = ex
        dx_ref, dg_ref = outs[0], outs[-1]
        xv = x_ref[...]
        rstd = lax.rsqrt(jnp.mean(xv * xv, axis=-1, keepdims=True) + EPS)
        xhat = xv * rstd
        dy = dhv * g_ref[...]
        dx = dres_ref[...] + rstd * (dy - xhat * jnp.mean(dy * xhat, axis=-1, keepdims=True))
        dx_ref[...] = dx
        if bf16_copy:
            outs[1][...] = dx.astype(BF16)
        part = jnp.sum(dhv * xhat, axis=0, keepdims=True)

        @pl.when(first)
        def _():
            dg_ref[...] = part

        @pl.when(jnp.logical_not(first))
        def _():
            dg_ref[...] += part

    return _matmul(
        name, a, wg, dims=NT, grid=(NI, 1, N_CHIPS),
        a_spec=pl.BlockSpec((TM, n4), lambda i, j, k: (i, k)),
        b_spec=pl.BlockSpec((None, D, n4), lambda i, j, k: (k, 0, 0)),
        out_shapes=[_sds((T, D), F32)] + [_sds((T, D), BF16)] * bf16_copy + [_sds((1, D), F32)],
        out_specs=[row] + [row] * bf16_copy + [vec], acc_shape=(TM, D),
        extras=[x, g, dres], extra_specs=[row, vec, row], epilogue=epilogue, after=after,
        semantics=("arbitrary", "arbitrary", "arbitrary"), epilogue_takes_first=True)


def _mm_nt_rows(name, a, w, out_dtype, *, tn, extras=(), extra_specs=(), epilogue=None):
    k_dim, n = w.shape
    return _matmul(
        name, a, w, dims=NT, grid=(k_dim // tn, NJ, 1),
        a_spec=pl.BlockSpec((TJ, n), lambda j, i, k: (i, 0)),
        b_spec=pl.BlockSpec((tn, n), lambda j, i, k: (j, 0)),
        out_shapes=[_sds((T, k_dim), out_dtype)],
        out_specs=[pl.BlockSpec((TJ, tn), lambda j, i, k: (i, j))], acc_shape=(TJ, tn),
        extras=extras, extra_specs=extra_specs, epilogue=epilogue)


def _mm_tn_cols(name, a, g, n4, *, colsum=False):
    k_dim = a.shape[1]
    kw = {}
    if colsum:
        kw = dict(colsum_spec=pl.BlockSpec((1, n4), lambda j, i, k: (0, j)),
                  colsum_shape=_sds((1, N_CHIPS * n4), F32))
    return _matmul(
        name, a, g, dims=TN, grid=(N_CHIPS, 1, NJ),
        a_spec=pl.BlockSpec((TJ, k_dim), lambda j, i, k: (k, 0)),
        b_spec=pl.BlockSpec((TJ, n4), lambda j, i, k: (k, j)),
        out_shapes=[_sds((N_CHIPS, k_dim, n4), F32)],
        out_specs=[pl.BlockSpec((None, k_dim, n4), lambda j, i, k: (j, 0, 0))],
        acc_shape=(k_dim, n4), **kw)


def _mm_tn_rows(name, a, g, *, tm):
    k_dim, n = a.shape[1], g.shape[1]
    return _matmul(
        name, a, g, dims=TN, grid=(k_dim // tm, 1, NJ),
        a_spec=pl.BlockSpec((TJ, tm), lambda j, i, k: (k, j)),
        b_spec=pl.BlockSpec((TJ, n), lambda j, i, k: (k, 0)),
        out_shapes=[_sds((k_dim, n), F32)],
        out_specs=[pl.BlockSpec((tm, n), lambda j, i, k: (j, 0))], acc_shape=(tm, n))


TE = 256
NE = T // TE
_ROW = pl.BlockSpec((TE, D), lambda i: (i, 0))
_VEC = pl.BlockSpec((1, D), lambda i: (0, 0))


def _rms_fwd(name, x, g, after=()):
    def body(x_ref, g_ref, *rest):
        h_ref = rest[-1]
        xv = x_ref[...]
        rstd = lax.rsqrt(jnp.mean(xv * xv, axis=-1, keepdims=True) + EPS)
        h_ref[...] = (xv * rstd * g_ref[...]).astype(BF16)

    return pl.pallas_call(body, name=name, grid=(NE,), in_specs=[_ROW, _VEC] + [_ANY] * len(after), out_specs=_ROW,
                          out_shape=_sds((T, D), BF16), compiler_params=_params(("parallel",)))(x, g, *after)


def _mm_x2_loss_head(s, w_ff2, x1, target, g):
    k_dim = w_ff2.shape[0]
    row = pl.BlockSpec((TM, D), lambda i, j, k: (i, 0))
    vec = pl.BlockSpec((1, D), lambda i, j, k: (0, 0))

    def epilogue(r, ex, outs, first):
        x1_ref, t_ref, g_ref = ex
        loss_ref, dx_ref, dxb_ref, dg_ref = outs
        xv = x1_ref[...] + r
        rstd = lax.rsqrt(jnp.mean(xv * xv, axis=-1, keepdims=True) + EPS)
        xhat = xv * rstd
        gv = g_ref[...]
        err = xhat * gv - t_ref[...]
        dy = err * (1.0 / D)
        dxh = dy * gv
        dx = rstd * (dxh - xhat * jnp.mean(dxh * xhat, axis=-1, keepdims=True))
        dx_ref[...] = dx
        dxb_ref[...] = dx.astype(BF16)
        dg_part = jnp.sum(dy * xhat, axis=0, keepdims=True)
        loss_part = (0.5 / D) * jnp.sum(jnp.sum(err * err, axis=1, keepdims=True), axis=0, keepdims=True)

        @pl.when(first)
        def _():
            dg_ref[...] = dg_part
            loss_ref[...] = loss_part

        @pl.when(jnp.logical_not(first))
        def _():
            dg_ref[...] += dg_part
            loss_ref[...] += loss_part

    return _matmul(
        "mm_x2_loss_head", s, w_ff2, dims=NN, grid=(NI, 1, k_dim // D),
        a_spec=pl.BlockSpec((TM, D), lambda i, j, k: (i, k)), b_spec=pl.BlockSpec((D, D), lambda i, j, k: (k, 0)),
        out_shapes=[_sds((1, 1), F32), _sds((T, D), F32), _sds((T, D), BF16), _sds((1, D), F32)],
        out_specs=[pl.BlockSpec((1, 1), lambda i, j, k: (0, 0)), row, row, vec], acc_shape=(TM, D),
        extras=[x1, target, g], extra_specs=[row, row, vec], epilogue=epilogue,
        semantics=("arbitrary", "arbitrary", "arbitrary"), epilogue_takes_first=True)


MW = 512
_G_ATT_BLK = 3584 // MW
_G_REC_BLK = 4608 // MW


TB = 512


def _branch_specs():
    def row(cols):
        return pl.BlockSpec((TB, cols), lambda i: (i, 0))

    ga = pl.BlockSpec((TB, MW), lambda i: (i, _G_ATT_BLK))
    ga2 = pl.BlockSpec((TB, MW), lambda i: (i, _G_ATT_BLK + 1))
    gr = pl.BlockSpec((TB, MW), lambda i: (i, _G_REC_BLK))
    gr2 = pl.BlockSpec((TB, MW), lambda i: (i, _G_REC_BLK + 1))
    w_att = pl.BlockSpec((N_CHIPS, D_ATT, D // N_CHIPS), lambda i: (0, 0, 0))
    w_sq = pl.BlockSpec((D, D), lambda i: (0, 0))
    return row, (ga, ga2, gr, gr2), w_att, w_sq


def _gate_values(gate_refs):
    ga, ga2, gr, gr2 = (r[...] for r in gate_refs)
    return _sigmoid(jnp.concatenate([ga, ga2], axis=1)), _sigmoid(jnp.concatenate([gr, gr2], axis=1))


def _branches_fwd(att, g, z, x, w_att_o, w_rec_o, w_out, ln2_g):
    row, gate_specs, w_att, w_sq = _branch_specs()

    def body(att_ref, g_ref, ga_ref, ga2_ref, gr_ref, gr2_ref, x_ref, wa_ref, wr_ref, wo_ref, g2_ref,
             ya_ref, yr_ref, m_ref, x1_ref, h2_ref):
        attv = att_ref[...]
        ya = jnp.concatenate([_dot(attv, wa_ref[j], NN) for j in range(N_CHIPS)], axis=1)
        yr = _dot(g_ref[...], wr_ref[...], NN)
        sa, sr = _gate_values((ga_ref, ga2_ref, gr_ref, gr2_ref))
        mixed = (sa * ya + sr * yr).astype(BF16)
        ya_ref[...] = ya
        yr_ref[...] = yr
        m_ref[...] = mixed
        x1 = x_ref[...] + _dot(mixed, wo_ref[...], NN)
        x1_ref[...] = x1
        rstd = lax.rsqrt(jnp.mean(x1 * x1, axis=-1, keepdims=True) + EPS)
        h2_ref[...] = (x1 * rstd * g2_ref[...]).astype(BF16)

    return pl.pallas_call(
        body, name="branches_fwd", grid=(T // TB,),
        in_specs=[row(D_ATT), row(D), *gate_specs, row(D), w_att, w_sq, w_sq, pl.BlockSpec((1, D), lambda i: (0, 0))],
        out_specs=[row(D)] * 5,
        out_shape=[_sds((T, D), F32), _sds((T, D), F32), _sds((T, D), BF16), _sds((T, D), F32), _sds((T, D), BF16)],
        compiler_params=_params(("parallel",)))(att, g, z, z, z, z, x, w_att_o, w_rec_o, w_out, ln2_g)


def _branches_bwd(dx1_b, y_att, y_rec, z, w_att_o, w_rec_o, w_out):
    row, gate_specs, w_att, w_sq = _branch_specs()
    n4 = D // N_CHIPS

    def body(dx_ref, ya_ref, yr_ref, ga_ref, ga2_ref, gr_ref, gr2_ref, wa_ref, wr_ref, wo_ref,
             dya_ref, dyr_ref, dga_ref, dgr_ref, datt_ref, dg_ref):
        dm = _dot(dx_ref[...], wo_ref[...], NT)
        sa, sr = _gate_values((ga_ref, ga2_ref, gr_ref, gr2_ref))
        dya = (dm * sa).astype(BF16)
        dyr = (dm * sr).astype(BF16)
        dya_ref[...] = dya
        dyr_ref[...] = dyr
        dga_ref[...] = (dm * ya_ref[...] * sa * (1.0 - sa)).astype(BF16)
        dgr_ref[...] = (dm * yr_ref[...] * sr * (1.0 - sr)).astype(BF16)
        datt = _dot(dya[:, 0:n4], wa_ref[0], NT)
        for j in range(1, N_CHIPS):
            datt = datt + _dot(dya[:, j * n4:(j + 1) * n4], wa_ref[j], NT)
        datt_ref[...] = datt.astype(BF16)
        dg_ref[...] = _dot(dyr, wr_ref[...], NT).astype(BF16)

    return pl.pallas_call(
        body, name="branches_bwd", grid=(T // TB,),
        in_specs=[row(D), row(D), row(D), *gate_specs, w_att, w_sq, w_sq],
        out_specs=[row(D)] * 4 + [row(D_ATT), row(D)],
        out_shape=[_sds((T, D), BF16)] * 4 + [_sds((T, D_ATT), BF16), _sds((T, D), BF16)],
        compiler_params=_params(("parallel",)))(dx1_b, y_att, y_rec, z, z, z, z, w_att_o, w_rec_o, w_out)


HP = 2 * HEAD_DIM
N_HP = N_HEADS // 2
ATT_UNROLL_FWD = 16
ATT_UNROLL_BWD = 8
DIAG_ROWS = 32


def _window_maps():
    diag = np.zeros((GRID_W * GRID_W, 128), np.float32)
    for qc in range(GRID_W):
        w0 = min(max(qc - WIN_W // 2, 0), GRID_W - WIN_W)
        for kc in range(w0, w0 + WIN_W):
            diag[qc * GRID_W + kc, kc - qc + WIN_W - 1] = 1.0
    return diag, diag.sum(axis=1)[None, :]


def _split3(x):
    a = x.astype(BF16)
    r = x - a.astype(F32)
    b = r.astype(BF16)
    c = (r - b.astype(F32)).astype(BF16)
    return a, b, c


N_DROW = 2 * WIN_H - 1
N_DPAIR = N_DROW - 1


def _bias_pairs(rpb):
    diag, valid = _window_maps()
    r2 = jnp.pad(rpb.reshape(N_HEADS * N_DROW, 2 * WIN_W - 1),
                 ((0, 128 - N_HEADS * N_DROW), (0, 128 - (2 * WIN_W - 1))))

    def body(r_ref, d_ref, v_ref, o_ref):
        dv = d_ref[...]
        t = sum(_dot(part, dv, NN) for part in _split3(r_ref[...]))
        o_ref[...] = jnp.where(v_ref[...] > 0.0, t, -1e30)

    t = pl.pallas_call(body, name="rpb_expand", out_shape=_sds((128, GRID_W * GRID_W), F32),
                       compiler_params=_params())(r2, jnp.asarray(diag.T, BF16), jnp.asarray(valid, F32))
    t = t[:N_HEADS * N_DROW].reshape(N_HEADS, N_DROW, GRID_W, GRID_W)
    return jnp.concatenate([t[:, :N_DPAIR], t[:, 1:]], axis=-1)


def _row_bias(tb_ref, hh, d0):
    return jnp.concatenate([tb_ref[hh, d0 + 2 * ii] for ii in range(WIN_H // 2)], axis=1)


def _row_window(r):
    rs = jnp.clip(r - WIN_H // 2, 0, N_ROWS - WIN_H)
    return pl.multiple_of(r * GRID_W, GRID_W), pl.multiple_of(rs * GRID_W, GRID_W), rs - r + (WIN_H - 1)


def _split_heads(src_ref, dst_ref, scale=None):
    for hh in range(2):
        v = src_ref[:, hh * HEAD_DIM:(hh + 1) * HEAD_DIM]
        dst_ref[hh] = (v if scale is None else v * scale).astype(BF16)


def _attn_items(qb_ref, kb_ref, vb_ref, tb_ref, first_row, n_rows):
    wins = [_row_window(first_row + u) for u in range(n_rows)]
    items = [(u, hh) for u in range(n_rows) for hh in range(2)]
    q = [qb_ref[hh, pl.ds(wins[u][0], GRID_W), :] for u, hh in items]
    k = [kb_ref[hh, pl.ds(wins[u][1], KEYS), :] for u, hh in items]
    v = [vb_ref[hh, pl.ds(wins[u][1], KEYS), :] for u, hh in items]
    s = [_dot(qi, ki, NT) + _row_bias(tb_ref, hh, wins[u][2]) for qi, ki, (u, hh) in zip(q, k, items)]
    m = [jnp.max(si, axis=-1, keepdims=True) for si in s]
    e = [jnp.exp(si - mi) for si, mi in zip(s, m)]
    inv = [1.0 / jnp.sum(ei, axis=-1, keepdims=True) for ei in e]
    p = [ei * li for ei, li in zip(e, inv)]
    return wins, items, q, k, v, p


def _attn_in_specs():
    q = pl.BlockSpec((T, HP), lambda p: (0, p))
    k = pl.BlockSpec((T, HP), lambda p: (0, N_HP + p))
    v = pl.BlockSpec((T, HP), lambda p: (0, 2 * N_HP + p))
    tb = pl.BlockSpec((2, N_DPAIR, GRID_W, HP), lambda p: (p, 0, 0, 0))
    return q, k, v, tb


_HEAD_SCRATCH = pltpu.VMEM((2, T, HEAD_DIM), BF16)


_PROBS = pl.BlockSpec((T, 2 * KEYS), lambda p: (0, p))


def _attn_fwd(z, tb):
    def body(q_ref, k_ref, v_ref, tb_ref, o_ref, p_ref, qb_ref, kb_ref, vb_ref):
        _split_heads(q_ref, qb_ref, SCALE)
        _split_heads(k_ref, kb_ref)
        _split_heads(v_ref, vb_ref)

        def rows(it, carry):
            wins, items, _, _, v, p = _attn_items(qb_ref, kb_ref, vb_ref, tb_ref, it * ATT_UNROLL_FWD, ATT_UNROLL_FWD)
            pb = [pi.astype(BF16) for pi in p]
            o = [_dot(pi, vi, NN) for pi, vi in zip(pb, v)]
            for u, (q0, _, _) in enumerate(wins):
                o_ref[pl.ds(q0, GRID_W), :] = jnp.concatenate(o[2 * u:2 * u + 2], axis=1).astype(BF16)
                p_ref[pl.ds(q0, GRID_W), :] = jnp.concatenate(pb[2 * u:2 * u + 2], axis=1)
            return carry

        lax.fori_loop(0, N_ROWS // ATT_UNROLL_FWD, rows, 0)

    blk = pl.BlockSpec((T, HP), lambda p: (0, p))
    return pl.pallas_call(
        body, name="attn_fwd", grid=(N_HP,), in_specs=list(_attn_in_specs()), out_specs=[blk, _PROBS],
        out_shape=[_sds((T, D_ATT), BF16), _sds((T, N_HEADS * KEYS), BF16)], scratch_shapes=[_HEAD_SCRATCH] * 3,
        compiler_params=_params(("parallel",)))(z, z, z, tb)


def _attn_bwd(z, probs, d_att, after=()):
    def body(q_ref, k_ref, v_ref, p_ref, do_ref, flip_ref, *rest):
        (dq_ref, dk_ref, dv_ref, diag_ref, qb_ref, kb_ref, vb_ref, dob_ref, dka_ref, dva_ref,
         ds_ref) = rest[len(after):]
        _split_heads(q_ref, qb_ref, SCALE)
        _split_heads(k_ref, kb_ref)
        _split_heads(v_ref, vb_ref)
        _split_heads(do_ref, dob_ref)
        dka_ref[...] = jnp.zeros_like(dka_ref)
        dva_ref[...] = jnp.zeros_like(dva_ref)
        ds_ref[...] = jnp.zeros_like(ds_ref)

        def rows(it, carry):
            wins = [_row_window(it * ATT_UNROLL_BWD + u) for u in range(ATT_UNROLL_BWD)]
            items = [(u, hh) for u in range(ATT_UNROLL_BWD) for hh in range(2)]
            q = [qb_ref[hh, pl.ds(wins[u][0], GRID_W), :] for u, hh in items]
            k = [kb_ref[hh, pl.ds(wins[u][1], KEYS), :] for u, hh in items]
            v = [vb_ref[hh, pl.ds(wins[u][1], KEYS), :] for u, hh in items]
            pb = [p_ref[pl.ds(wins[u][0], GRID_W), hh * KEYS:(hh + 1) * KEYS] for u, hh in items]
            p = [pi.astype(F32) for pi in pb]
            do = [dob_ref[hh, pl.ds(wins[u][0], GRID_W), :] for u, hh in items]
            dv = [_dot(pi, di, TN) for pi, di in zip(pb, do)]
            dp = [_dot(di, vi, NT) for di, vi in zip(do, v)]
            ds = [pi * (dpi - jnp.sum(dpi * pi, axis=-1, keepdims=True)) for pi, dpi in zip(p, dp)]
            dsb = [d.astype(BF16) for d in ds]
            dq = [_dot(d, ki, NN) * SCALE for d, ki in zip(dsb, k)]
            dk = [_dot(d, qi, TN) for d, qi in zip(dsb, q)]
            for d, (u, hh) in zip(ds, items):
                for ii in range(WIN_H // 2):
                    ds_ref[hh, wins[u][2] + 2 * ii] += d[:, ii * HP:(ii + 1) * HP]
            for u, (q0, k0, _) in enumerate(wins):
                dq_ref[pl.ds(q0, GRID_W), :] = jnp.concatenate(dq[2 * u:2 * u + 2], axis=1).astype(BF16)
                dka_ref[pl.ds(k0, KEYS), :] += jnp.concatenate(dk[2 * u:2 * u + 2], axis=1)
                dva_ref[pl.ds(k0, KEYS), :] += jnp.concatenate(dv[2 * u:2 * u + 2], axis=1)
            return carry

        lax.fori_loop(0, N_ROWS // ATT_UNROLL_BWD, rows, 0)
        dk_ref[...] = dka_ref[...].astype(BF16)
        dv_ref[...] = dva_ref[...].astype(BF16)
        _diag_sums(ds_ref, flip_ref, diag_ref)

    blk = pl.BlockSpec((T, HP), lambda p: (0, p))
    q, k, v, _ = _attn_in_specs()
    flip =jnp.asarray(np.eye(HP, dtype=np.float32)[::-1], BF16)
    return pl.pallas_call(
        body, name="attn_bwd", grid=(N_HP,),
        in_specs=[q, k, v, _PROBS, blk, pl.BlockSpec((HP, HP), lambda p: (0, 0))] + [_ANY] * len(after),
        out_specs=[blk, blk, blk, pl.BlockSpec((None, DIAG_ROWS, HP), lambda p: (p, 0, 0))],
        out_shape=[_sds((T, D_ATT), BF16)] * 3 + [_sds((N_HP, DIAG_ROWS, HP), F32)],
        scratch_shapes=[_HEAD_SCRATCH] * 4 + [pltpu.VMEM((T, HP), F32), pltpu.VMEM((T, HP), F32),
                                              pltpu.VMEM((2, N_DPAIR, GRID_W, HP), F32)],
        compiler_params=_params(("parallel",)))(z, z, z, probs, d_att, flip, *after)


def _diag_sums(acc_ref, flip_ref, out_ref):
    flip = flip_ref[...]
    rows = []
    for hh in range(2):
        for pair in range(N_DPAIR):
            reversed_lanes = sum(_dot(part, flip, NN) for part in _split3(acc_ref[hh, pair]))
            skewed = pltpu.roll(reversed_lanes, 0, 1, stride=1, stride_axis=0)
            rows.append(jnp.sum(skewed, axis=0, keepdims=True))
    rows.append(jnp.zeros((DIAG_ROWS - len(rows), HP), F32))
    out_ref[...] = jnp.concatenate(rows, axis=0)


def _rpb_grad(diag_sums):
    g = diag_sums.reshape(N_HP * DIAG_ROWS, HP)
    sel = np.zeros((2, 128, N_HP * DIAG_ROWS), np.float32)
    lane = np.zeros((2, HP, 128), np.float32)
    for h in range(N_HEADS):
        for pair in range(N_DPAIR):
            for half in range(2):
                sel[half, h * N_DROW + pair + half, (h // 2) * DIAG_ROWS + (h % 2) * N_DPAIR + pair] = 1.0
    for j in range(2 * WIN_W - 1):
        for half in range(2):
            lane[half, (HP - 1 - GRID_W * half - (j - (WIN_W - 1))) % HP, j] = 1.0

    def body(g_ref, sel_ref, lane_ref, o_ref):
        parts = _split3(g_ref[...])
        total = None
        for half in range(2):
            picked = sum(_dot(sel_ref[half], part, NN) for part in parts)
            term = sum(_dot(part, lane_ref[half], NN) for part in _split3(picked))
            total = term if total is None else total + term
        o_ref[...] = total

    out = pl.pallas_call(body, name="rpb_grad", out_shape=_sds((128, 128), F32),
                         compiler_params=_params())(g, jnp.asarray(sel, BF16), jnp.asarray(lane, BF16))
    return out[:N_HEADS * N_DROW, :2 * WIN_W - 1].reshape(N_HEADS, N_DROW, 2 * WIN_W - 1)


N_CB = D // REC_CB
N_CHUNK = T // REC_CHUNK
N_TILE = T // 8
_U_BLK = 1536 // REC_CB
_Y_BLK = 2560 // REC_CB


def _block_diag(w):
    per = REC_CB // 64
    wt = w.reshape(2, N_CB, per, 64, 64)
    eye = jnp.eye(per, dtype=w.dtype)
    full = wt[:, :, :, :, None, :] * eye[None, None, :, None, :, None]
    return full.reshape(2, N_CB, REC_CB, REC_CB).astype(BF16)


def _block_diag_grad(g):
    per = REC_CB // 64
    g6 = g.reshape(2, N_CB, per, 64, per, 64)
    return jnp.stack([g6[:, :, p, :, p, :] for p in range(per)], axis=2).reshape(2, 16, 64, 64)


def _gelu(x):
    c = 0.7978845608028654
    return 0.5 * x * (1.0 + jnp.tanh(c * (x + 0.044715 * x * x * x)))


def _gelu_grad(x):
    c = 0.7978845608028654
    th = jnp.tanh(c * (x + 0.044715 * x * x * x))
    return 0.5 * (1.0 + th) + 0.5 * x * (1.0 - th * th) * c * (1.0 + 3.0 * 0.044715 * x * x)


def _softplus_neg(lam):
    x = -lam
    e = jnp.exp(-jnp.abs(x))
    w = 1.0 + e
    l1p = jnp.where(w == 1.0, e, jnp.log(w) * e / (w - 1.0))
    return jnp.maximum(x, 0.0) + l1p


def _one_minus_exp(x):
    poly = x * (1.0 + x * (1 / 2 + x * (1 / 6 + x * (1 / 24 + x * (1 / 120 + x * (1 / 720))))))
    return jnp.where(x > -0.125, -poly, 1.0 - jnp.exp(x))


def _conv_taps(pad_ref, t0, w, sign):
    out = None
    for j in range(4):
        term = w[j:j + 1, :] * pad_ref[pl.ds(PAD + t0 + sign * (j - 2), REC_CHUNK), :]
        out = term if out is None else out + term
    return out


def _gates(u, wa, wi, ba, bi, sp):
    ub = u.astype(BF16)
    r = _sigmoid(_dot(ub, wa, NN) + ba)
    i = _sigmoid(_dot(ub, wi, NN) + bi)
    log_a = -LRU_C * r * sp
    a = jnp.exp(log_a)
    x = jnp.maximum(_one_minus_exp(2.0 * log_a), 0.0)
    positive = x > 0.0
    inv = lax.rsqrt(jnp.where(positive, x, 1.0))
    mult = jnp.where(positive, x * inv, 0.0)
    return r, i, a, mult, jnp.where(positive, inv, 0.0)


def _tile_scan(a, b, sub, reverse):
    for s in (1, 2, 4):
        if reverse:
            a_s, b_s, m = pltpu.roll(a, 8 - s, 0), pltpu.roll(b, 8 - s, 0), sub < 8 - s
        else:
            a_s, b_s, m = pltpu.roll(a, s, 0), pltpu.roll(b, s, 0), sub >= s
        b = jnp.where(m, a * b_s + b, b)
        a = jnp.where(m, a * a_s, a)
    return a, b


def _last_row(x, row):
    return jnp.broadcast_to(x[row:row + 1, :], x.shape)


def _rec_prologue(up_ref, cw_ref, cb_ref, wa_ref, wi_ref, ba_ref, bi_ref, lam_ref,
                  upad_ref, u_ref, a_refs, h_refs):
    cb = up_ref.shape[1]
    zeros = jnp.zeros((PAD, cb), F32)
    upad_ref[pl.ds(0, PAD), :] = zeros
    upad_ref[pl.ds(PAD + T, PAD), :] = zeros
    upad_ref[pl.ds(PAD, T), :] = up_ref[...]
    cw = cw_ref[...]
    sp = _softplus_neg(lam_ref[...])
    for c in range(N_CHUNK):
        t0 = c * REC_CHUNK
        u = cb_ref[...] + _conv_taps(upad_ref, t0, cw, 1)
        u_ref[pl.ds(t0, REC_CHUNK), :] = u
        for d in range(2):
            _, i, a, mult, _ = _gates(u, wa_ref[d], wi_ref[d], ba_ref[d:d + 1, :], bi_ref[d:d + 1, :], sp[d:d + 1, :])
            a_refs[d][pl.ds(t0, REC_CHUNK), :] = a
            h_refs[d][pl.ds(t0, REC_CHUNK), :] = mult * (i * u)

    sub = lax.broadcasted_iota(jnp.int32, (8, cb), 0)

    def tile(k, carry):
        cf, cr = carry
        tf = pl.multiple_of(k * 8, 8)
        tr = pl.multiple_of((N_TILE - 1 - k) * 8, 8)
        af, bf = _tile_scan(a_refs[0][pl.ds(tf, 8), :], h_refs[0][pl.ds(tf, 8), :], sub, False)
        hf = af * cf + bf
        h_refs[0][pl.ds(tf, 8), :] = hf
        ar, br = _tile_scan(a_refs[1][pl.ds(tr, 8), :], h_refs[1][pl.ds(tr, 8), :], sub, True)
        hr = ar * cr + br
        h_refs[1][pl.ds(tr, 8), :] = hr
        return _last_row(af, 7) * cf + _last_row(bf, 7), _last_row(ar, 0) * cr + _last_row(br, 0)

    z8 = jnp.zeros((8, cb), F32)
    lax.fori_loop(0, N_TILE, tile, (z8, z8))
    return sp


def _rec_specs():
    up = pl.BlockSpec((T, REC_CB), lambda c: (0, _U_BLK + c))
    yb = pl.BlockSpec((T, REC_CB), lambda c: (0, _Y_BLK + c))
    cw = pl.BlockSpec((4, REC_CB), lambda c: (0, c))
    cbias = pl.BlockSpec((1, REC_CB), lambda c: (0, c))
    wbd = pl.BlockSpec((2, None, REC_CB, REC_CB), lambda c: (0, c, 0, 0))
    vec2 = pl.BlockSpec((2, REC_CB), lambda c: (0, c))
    col = pl.BlockSpec((T, REC_CB), lambda c: (0, c))
    return up, yb, cw, cbias, wbd, vec2, col


def _rec_fwd(z, conv_w, conv_b, wa, wi, ba, bi, lam):
    up, yb, cw, cbias, wbd, vec2, col = _rec_specs()

    def body(up_ref, yb_ref, cw_ref, cb_ref, wa_ref, wi_ref, ba_ref, bi_ref, lam_ref, g_ref,
             u_ref, af_ref, ar_ref, hf_ref, hr_ref, upad_ref):
        _rec_prologue(up_ref, cw_ref, cb_ref, wa_ref, wi_ref, ba_ref, bi_ref, lam_ref,
                      upad_ref, u_ref, (af_ref, ar_ref), (hf_ref, hr_ref))

        def chunk(c, carry):
            t0 = pl.multiple_of(c * REC_CHUNK, REC_CHUNK)
            rows = pl.ds(t0, REC_CHUNK)
            g_ref[rows, :] = ((hf_ref[rows, :] + hr_ref[rows, :]) * _gelu(yb_ref[rows, :])).astype(BF16)
            return carry

        lax.fori_loop(0, N_CHUNK, chunk, 0)

    res = pl.pallas_call(
        body, name="rec_fwd", grid=(N_CB,),
        in_specs=[up, yb, cw, cbias, wbd, wbd, vec2, vec2, vec2], out_specs=[col] * 6,
        out_shape=[_sds((T, D), BF16)] + [_sds((T, D), F32)] * 5,
        scratch_shapes=[pltpu.VMEM((T + 2 * PAD, REC_CB), F32)],
        compiler_params=_params(("parallel",)))(z, z, conv_w, conv_b, wa, wi, ba, bi, lam)
    return res[0], tuple(res[1:])


def _rec_bwd(z, dg, saved, conv_w, conv_b, wa, wi, ba, bi, lam, after=()):
    up, yb, cw, cbias, wbd, vec2, col = _rec_specs()

    def body(up_ref, yb_ref, dg_ref, u_ref, af_ref, ar_ref, hf_ref, hr_ref,
             cw_ref, cb_ref, wa_ref, wi_ref, ba_ref, bi_ref, lam_ref, *rest):
        (dup_ref, dyb_ref, dcw_ref, dcb_ref, dwa_ref, dwi_ref, dba_ref, dbi_ref, dlam_ref,
         upad_ref, dh_ref, gf_ref, gr_ref, daf_ref, dar_ref, dupad_ref) = rest[len(after):]
        g_refs, da_refs = (gf_ref, gr_ref), (daf_ref, dar_ref)
        cb = up_ref.shape[1]
        zeros = jnp.zeros((PAD, cb), F32)
        upad_ref[pl.ds(0, PAD), :] = zeros
        upad_ref[pl.ds(PAD + T, PAD), :] = zeros
        upad_ref[pl.ds(PAD, T), :] = up_ref[...]
        sp = _softplus_neg(lam_ref[...])

        def gate_chunk(c, carry):
            t0 = pl.multiple_of(c * REC_CHUNK, REC_CHUNK)
            rows = pl.ds(t0, REC_CHUNK)
            y = yb_ref[rows, :]
            dgv = dg_ref[rows, :].astype(F32)
            dh_ref[rows, :] = dgv * _gelu(y)
            dyb_ref[rows, :] = (dgv * (hf_ref[rows, :] + hr_ref[rows, :]) * _gelu_grad(y)).astype(BF16)
            return carry

        lax.fori_loop(0, N_CHUNK, gate_chunk, 0)

        sub = lax.broadcasted_iota(jnp.int32, (8, cb), 0)

        def tile(k, carry):
            cf, cr = carry
            kf = N_TILE - 1 - k
            tf = pl.multiple_of(kf * 8, 8)
            tnext = pl.multiple_of(jnp.minimum(kf + 1, N_TILE - 1) * 8, 8)
            tprev = pl.multiple_of(jnp.maximum(kf - 1, 0) * 8, 8)
            a_t = af_ref[pl.ds(tf, 8), :]
            a_n = jnp.where(kf < N_TILE - 1, af_ref[pl.ds(tnext, 8), :], 0.0)
            a_sh = jnp.where(sub == 7, pltpu.roll(a_n, 7, 0), pltpu.roll(a_t, 7, 0))
            ca, cbb = _tile_scan(a_sh, dh_ref[pl.ds(tf, 8), :], sub, True)
            gf = ca * cf + cbb
            h_t = hf_ref[pl.ds(tf, 8), :]
            h_p = jnp.where(kf > 0, hf_ref[pl.ds(tprev, 8), :], 0.0)
            h_sh = jnp.where(sub == 0, pltpu.roll(h_p, 1, 0), pltpu.roll(h_t, 1, 0))
            gf_ref[pl.ds(tf, 8), :] = gf
            daf_ref[pl.ds(tf, 8), :] = gf * h_sh
            tr = pl.multiple_of(k * 8, 8)
            rnext = pl.multiple_of(jnp.minimum(k + 1, N_TILE - 1) * 8, 8)
            rprev = pl.multiple_of(jnp.maximum(k - 1, 0) * 8, 8)
            b_t = ar_ref[pl.ds(tr, 8), :]
            b_p = jnp.where(k > 0, ar_ref[pl.ds(rprev, 8), :], 0.0)
            b_sh = jnp.where(sub == 0, pltpu.roll(b_p, 1, 0), pltpu.roll(b_t, 1, 0))
            ra, rb = _tile_scan(b_sh, dh_ref[pl.ds(tr, 8), :], sub, False)
            gr = ra * cr + rb
            hr_t = hr_ref[pl.ds(tr, 8), :]
            hr_n = jnp.where(k < N_TILE - 1, hr_ref[pl.ds(rnext, 8), :], 0.0)
            hr_sh = jnp.where(sub == 7, pltpu.roll(hr_n, 7, 0), pltpu.roll(hr_t, 7, 0))
            gr_ref[pl.ds(tr, 8), :] = gr
            dar_ref[pl.ds(tr, 8), :] = gr * hr_sh
            return _last_row(gf, 0), _last_row(gr, 7)

        z8 = jnp.zeros((8, cb), F32)
        lax.fori_loop(0, N_TILE, tile, (z8, z8))

        dupad_ref[pl.ds(0, PAD), :] = zeros
        dupad_ref[pl.ds(PAD + T, PAD), :] = zeros
        dwa_ref[...] = jnp.zeros_like(dwa_ref)
        dwi_ref[...] = jnp.zeros_like(dwi_ref)
        dba_ref[...] = jnp.zeros_like(dba_ref)
        dbi_ref[...] = jnp.zeros_like(dbi_ref)
        dlam_ref[...] = jnp.zeros_like(dlam_ref)

        def grad_chunk(c, carry):
            t0 = pl.multiple_of(c * REC_CHUNK, REC_CHUNK)
            rows = pl.ds(t0, REC_CHUNK)
            u = u_ref[rows, :]
            ub = u.astype(BF16)
            du = jnp.zeros((REC_CHUNK, cb), F32)
            for d in range(2):
                r, i, a, mult, inv_mult = _gates(u, wa_ref[d], wi_ref[d], ba_ref[d:d + 1, :], bi_ref[d:d + 1, :],
                                                 sp[d:d + 1, :])
                dbx = g_refs[d][rows, :]
                dmult = dbx * (i * u)
                diu = dbx * mult
                a2 = a * a
                dlog = da_refs[d][rows, :] * a - dmult * (a2 * inv_mult)
                dpa = (dlog * (-LRU_C) * sp[d:d + 1, :]) * r * (1.0 - r)
                dpi = (diu * u) * i * (1.0 - i)
                dpab, dpib = dpa.astype(BF16), dpi.astype(BF16)
                du = du + diu * i + _dot(dpab, wa_ref[d], NT) + _dot(dpib, wi_ref[d], NT)
                dwa_ref[d] += _dot(ub, dpab, TN)
                dwi_ref[d] += _dot(ub, dpib, TN)
                dba_ref[d:d + 1, :] += jnp.sum(dpa, axis=0, keepdims=True)
                dbi_ref[d:d + 1, :] += jnp.sum(dpi, axis=0, keepdims=True)
                dlam_ref[d:d + 1, :] += jnp.sum(dlog * r, axis=0, keepdims=True)
            dupad_ref[pl.ds(PAD + t0, REC_CHUNK), :] = du
            return carry

        lax.fori_loop(0, N_CHUNK, grad_chunk, 0)
        dlam_ref[...] = dlam_ref[...] * (LRU_C * _sigmoid(-lam_ref[...]))

        cw = cw_ref[...]
        dcb = jnp.zeros((1, cb), F32)
        dcw = [jnp.zeros((1, cb), F32) for _ in range(4)]
        for c in range(N_CHUNK):
            t0 = c * REC_CHUNK
            du = dupad_ref[pl.ds(PAD + t0, REC_CHUNK), :]
            dcb = dcb + jnp.sum(du, axis=0, keepdims=True)
            for j in range(4):
                dcw[j] = dcw[j] + jnp.sum(du * upad_ref[pl.ds(PAD + t0 + j - 2, REC_CHUNK), :], axis=0, keepdims=True)
            dup_ref[pl.ds(t0, REC_CHUNK), :] = _conv_taps(dupad_ref, t0, cw, -1).astype(BF16)
        dcb_ref[...] = dcb
        dcw_ref[...] = jnp.concatenate(dcw, axis=0)

    full = pltpu.VMEM((T, REC_CB), F32)
    padded = pltpu.VMEM((T + 2 * PAD, REC_CB), F32)
    return pl.pallas_call(
        body, name="rec_bwd", grid=(N_CB,),
        in_specs=[up, yb] + [col] * 6 + [cw, cbias, wbd, wbd, vec2, vec2, vec2] + [_ANY] * len(after),
        out_specs=[col, col, cw, cbias, wbd, wbd, vec2, vec2, vec2],
        out_shape=[_sds((T, D), BF16), _sds((T, D), BF16), _sds((4, D), F32), _sds((1, D), F32),
                   _sds((2, N_CB, REC_CB, REC_CB), F32), _sds((2, N_CB, REC_CB, REC_CB), F32),
                   _sds((2, D), F32), _sds((2, D), F32), _sds((2, D), F32)],
        scratch_shapes=[padded, full, full, full, full, full, padded],
        compiler_params=_params(("parallel",)))(z, z, dg, *saved, conv_w, conv_b, wa, wi, ba, bi, lam, *after)


class _NoReducer:
    def begin(self, tag, grads):
        return ()

    def advance(self, tag, after):
        return ()

    def interlude(self, tokens):
        return None


def _local_step(x, target, p, late=None, reducer=_NoReducer()):
    x = x.reshape(T, D)
    target = target.reshape(T, D)
    tb = _bias_pairs(p["rpb"])
    wa, wi = _block_diag(p["w_rg_a"]), _block_diag(p["w_rg_i"])

    h1 = _rms_fwd("rms1_fwd", x, p["ln1_g"], after=late[0] if late else ())
    if late:
        p = {**p, **late[1]((h1, tb, wa, wi))}
    rec_params = (p["conv_w"], p["conv_b"], wa, wi, p["b_rg_a"], p["b_rg_i"], p["lru_lambda"])
    (z,) = _mm_nn_cols("mm_z", h1, p["w_in"], F32, bias=p["b_in"])
    att, probs = _attn_fwd(z, tb)
    g, rec_saved = _rec_fwd(z, *rec_params)
    if late:
        p = {**p, **late[2](g)}
    y_att, y_rec, mixed, x1, h2 = _branches_fwd(att, g, z, x, p["w_att_o"], p["w_rec_o"], p["w_out"], p["ln2_g"])

    def relu2(r, ex, outs):
        rp = jnp.maximum(r, 0.0)
        outs[0][...] = (rp * rp).astype(BF16)

    (s,) = _mm_nn_cols("mm_ff1", h2, p["w_ff1"], BF16, epilogue=relu2)
    loss, dx2, dx2_b, g_lnf = _mm_x2_loss_head(s, p["w_ff2"], x1, target, p["lnf_g"])

    def relu2_bwd(r, ex, outs):
        outs[0][...] = (r * 2.0 * jnp.sqrt(ex[0][...].astype(F32))).astype(BF16)

    (df,) = _mm_nt_rows("mm_df", dx2_b, p["w_ff2"], BF16, tn=D, extras=[s],
                        extra_specs=[pl.BlockSpec((TJ, D), lambda j, i, k: (i, j))], epilogue=relu2_bwd)
    (g_w_ff2,) = _mm_tn_rows("mm_g_ff2", s, dx2_b, tm=D)
    (g_w_ff1,) = _mm_tn_cols("mm_g_ff1", h2, df, D)
    tok = reducer.begin("ff", dict(w_ff2=g_w_ff2, w_ff1=g_w_ff1))
    dx1, dx1_b, g_ln2 = _mm_nt_cols_rms_bwd("mm_dh2_rms2_bwd", df, p["w_ff1"], x1, p["ln2_g"], dx2, after=tok,
                                            bf16_copy=True)

    dy_att, dy_rec, dg_att, dg_rec, d_att, d_g = _branches_bwd(dx1_b, y_att, y_rec, z, p["w_att_o"], p["w_rec_o"],
                                                               p["w_out"])
    (g_w_out,) = _mm_tn_rows("mm_g_out", mixed, dx1_b, tm=D)
    (g_w_att_o,) = _mm_tn_cols("mm_g_att_o", att, dy_att, D // N_CHIPS)
    (g_w_rec_o,) = _mm_tn_rows("mm_g_rec_o", g, dy_rec, tm=D)
    tok = reducer.advance("ff", g_w_rec_o) + reducer.begin("proj", dict(w_out=g_w_out, w_att_o=g_w_att_o, w_rec_o=g_w_rec_o))

    dq, dk, dv, ds_acc = _attn_bwd(z, probs, d_att, after=tok)
    g_rpb = _rpb_grad(ds_acc)
    tok = reducer.advance("proj", dq)
    d_up, d_yb, g_conv_w, g_conv_b, g_wa, g_wi, g_ba, g_bi, g_lam = _rec_bwd(z, d_g, rec_saved, *rec_params, after=tok)
    dz = jnp.concatenate([dq, dk, dv, d_up, d_yb, dg_att, dg_rec], axis=1)

    g_w_in, g_b_in = _mm_tn_cols("mm_g_in", h1, dz, D_IN // N_CHIPS, colsum=True)
    tok = reducer.advance("in", reducer.interlude(reducer.begin("in", dict(w_in=g_w_in))))
    grad_x, g_ln1 = _mm_nt_cols_rms_bwd("mm_dh1_rms1_bwd", dz, p["w_in"], x, p["ln1_g"], dx1, after=tok)

    grads = dict(ln1_g=g_ln1, w_in=g_w_in, b_in=g_b_in, rpb=g_rpb, w_att_o=g_w_att_o, conv_w=g_conv_w,
                 conv_b=g_conv_b, w_rg_a=_block_diag_grad(g_wa), b_rg_a=g_ba, w_rg_i=_block_diag_grad(g_wi),
                 b_rg_i=g_bi, lru_lambda=g_lam, w_rec_o=g_w_rec_o, w_out=g_w_out, ln2_g=g_ln2,
                 w_ff1=g_w_ff1, w_ff2=g_w_ff2, lnf_g=g_lnf)
    return loss, grad_x.reshape(1, T, D), grads


_ANY = pl.BlockSpec(memory_space=pl.ANY)
N_PEERS = N_CHIPS - 1


def _place():
    x, y, c = lax.axis_index("x"), lax.axis_index("y"), lax.axis_index("c")
    peers = [(1 - x, y), (x, 1 - y), (1 - x, 1 - y)]
    return x, y, c, 2 * x + y, peers


def _remote(src, dst, send_sem, recv_sem, dev):
    return pltpu.make_async_remote_copy(src_ref=src, dst_ref=dst, send_sem=send_sem, recv_sem=recv_sem,
                                        device_id=dev, device_id_type=MESH)


def _prefetch_call(body, name, ids, grid, in_specs, out_specs, out_shape, args, semantics=None):
    spec = pltpu.PrefetchScalarGridSpec(num_scalar_prefetch=1, grid=grid, in_specs=in_specs, out_specs=out_specs)
    return pl.pallas_call(body, name=name, grid_spec=spec, out_shape=out_shape,
                          compiler_params=_params(semantics or ("parallel",) * len(grid)))(ids, *args)


def _cast_bf16(name, w, chip_id, after=()):
    rows, cols = w.shape
    rb = min(rows, 256)

    def body(ids_ref, w_ref, *rest):
        rest[-1][...] = w_ref[...].astype(BF16)

    return _prefetch_call(body, name, chip_id, (rows // rb,),
                          [pl.BlockSpec((rb, cols), lambda i, ids: (i, 0))] + [_ANY] * len(after),
                          pl.BlockSpec((None, rb, cols), lambda i, ids: (ids[0], i, 0)),
                          _sds((N_CHIPS, rows, cols), BF16), (w, *after))


def _dma_sems(*counts):
    return [pltpu.SemaphoreType.DMA((k,)) for k in counts]


_HBM = pl.BlockSpec(memory_space=pltpu.HBM)
_SEM = pl.BlockSpec(memory_space=pltpu.SEMAPHORE)
_SPLIT_COPY = pltpu.CompilerParams(has_side_effects=pltpu.SideEffectType.DATAFLOW_SIDE_EFFECTING)
SIBLING_ID = 0
_SPLIT_COPY_SIBLING = pltpu.CompilerParams(has_side_effects=pltpu.SideEffectType.DATAFLOW_SIDE_EFFECTING,
                                           collective_id=SIBLING_ID)


def _sibling_handshake():
    x, y, c = lax.axis_index("x"), lax.axis_index("y"), lax.axis_index("c")
    barrier = pltpu.get_barrier_semaphore()
    pl.semaphore_signal(barrier, inc=1, device_id=(x, y, 1 - c), device_id_type=MESH)
    pl.semaphore_wait(barrier, 1)


def _hbm(arrays):
    return [pltpu.with_memory_space_constraint(a, pltpu.HBM) for a in arrays]


def _hbm_like(arrays):
    return [pltpu.HBM(a.shape, a.dtype) for a in arrays]


def _halves(buf, c):
    half = buf.shape[1] // 2
    return pl.ds(c * half, half), pl.ds((1 - c) * half, half)


def _gather_start(name, slots):
    n = len(slots)
    nk = n * N_PEERS

    def body(*refs):
        bufs = refs[n:2 * n]
        send_sems, recv_sems, token = refs[2 * n:]
        x, y, c, chip, peers = _place()
        for t in range(n):
            mine, _ = _halves(bufs[t], c)
            for r, (px, py) in enumerate(peers):
                k = t * N_PEERS + r
                own = bufs[t].at[chip, mine]
                _remote(own, own, send_sems.at[k], recv_sems.at[k], (px, py, c)).start()
        token[...] = jnp.zeros_like(token)

    res = pl.pallas_call(
        body, name=name, in_specs=[_HBM] * n, out_specs=[_HBM] * n + [_SEM, _SEM, pl.BlockSpec(memory_space=pltpu.VMEM)],
        out_shape=_hbm_like(slots) + [pltpu.SemaphoreType.DMA((nk,)), pltpu.SemaphoreType.DMA((nk,)),
                                      _sds((8, 128), F32)],
        input_output_aliases={t: t for t in range(n)}, compiler_params=_SPLIT_COPY)(*_hbm(slots))
    return res[:n], (res[n], res[n + 1]), res[n + 2]


def _gather_wait(name, bufs, sems, after):
    n = len(bufs)
    after = tuple(after) if isinstance(after, (tuple, list)) else (after,)

    def body(*refs):
        ins = refs[:n]
        send_sems, recv_sems = refs[n], refs[n + 1]
        x, y, c, chip, peers = _place()
        for t in range(n):
            mine, _ = _halves(ins[t], c)
            for r, (px, py) in enumerate(peers):
                k = t * N_PEERS + r
                cp = _remote(ins[t].at[chip, mine], ins[t].at[2 * px + py, mine], send_sems.at[k], recv_sems.at[k],
                             (px, py, c))
                cp.wait_send()
                cp.wait_recv()

    return pl.pallas_call(
        body, name=name, in_specs=[_HBM] * n + [_SEM, _SEM] + [_ANY] * len(after), out_specs=[_HBM] * n,
        out_shape=_hbm_like(bufs), input_output_aliases={t: t for t in range(n)},
        compiler_params=_SPLIT_COPY)(*bufs, *sems, *after)


def _gather_forward(name, bufs):
    n = len(bufs)
    nk = n * N_PEERS

    def body(*refs):
        _sibling_handshake()
        outs = refs[n:2 * n]
        send_sems, recv_sems = refs[2 * n:]
        x, y, c, chip, peers = _place()
        sibling = (x, y, 1 - c)
        sends = []
        for t in range(n):
            mine, _ = _halves(outs[t], c)
            for r, (px, py) in enumerate(peers):
                k = t * N_PEERS + r
                landed = outs[t].at[2 * px + py, mine]
                sends.append(_remote(landed, landed, send_sems.at[k], recv_sems.at[k], sibling))
                sends[-1].start()
        for t in range(n):
            _, theirs = _halves(outs[t], c)
            for r, (px, py) in enumerate(peers):
                k = t * N_PEERS + r
                landed = outs[t].at[2 * px + py, theirs]
                _remote(landed, landed, send_sems.at[k], recv_sems.at[k], sibling).wait_recv()
        for cp in sends:
            cp.wait_send()

    return pl.pallas_call(
        body, name=name, in_specs=[_ANY] * n, out_specs=[_ANY] * n, out_shape=[_sds(b.shape, b.dtype) for b in bufs],
        input_output_aliases={t: t for t in range(n)}, scratch_shapes=_dma_sems(nk, nk),
        compiler_params=pltpu.CompilerParams(collective_id=SIBLING_ID))(*bufs)


def _pair_copies(n, srcs, lands, send_sems, recv_sems):
    x, y, c, _, _ = _place()
    sibling = (x, y, 1 - c)
    copies = []
    for t in range(n):
        half = srcs[t].shape[1] // 2
        for j in range(N_CHIPS):
            k = t * N_CHIPS + j
            copies.append(_remote(srcs[t].at[j, pl.ds((1 - c) * half, half)], lands[t].at[j],
                                  send_sems.at[k], recv_sems.at[k], sibling))
    for t in range(n, len(srcs)):
        k = n * N_CHIPS + t - n
        copies.append(_remote(srcs[t], lands[t], send_sems.at[k], recv_sems.at[k], sibling))
    return copies


def _pair_start(name, grads, wholes=()):
    n = len(grads)
    srcs = list(grads) + list(wholes)
    m = len(srcs)
    lands = [pltpu.HBM((N_CHIPS, g.shape[1] // 2, g.shape[2]), F32) for g in grads] + _hbm_like(wholes)
    ns = n * N_CHIPS + len(wholes)

    def body(*refs):
        _sibling_handshake()
        src_refs, land_refs = refs[m:2 * m], refs[2 * m:3 * m]
        send_sems, recv_sems, token = refs[3 * m:]
        for cp in _pair_copies(n, src_refs, land_refs, send_sems, recv_sems):
            cp.start()
        token[...] = jnp.zeros_like(token)

    res = pl.pallas_call(
        body, name=name, in_specs=[_HBM] * m,
        out_specs=[_HBM] * (2 * m) + [_SEM, _SEM, pl.BlockSpec(memory_space=pltpu.VMEM)],
        out_shape=_hbm_like(srcs) + lands + [pltpu.SemaphoreType.DMA((ns,)), pltpu.SemaphoreType.DMA((ns,)),
                                             _sds((8, 128), F32)],
        input_output_aliases={t: t for t in range(m)}, compiler_params=_SPLIT_COPY_SIBLING)(*_hbm(srcs))
    return (res[:m], res[m:2 * m], (res[2 * m], res[2 * m + 1])), res[2 * m + 2]


def _pair_wait(name, flight, n, after):
    srcs, lands, sems = flight
    m = len(srcs)

    def body(*refs):
        for cp in _pair_copies(n, refs[:m], refs[m:2 * m], refs[2 * m], refs[2 * m + 1]):
            cp.wait_send()
            cp.wait_recv()

    res = pl.pallas_call(
        body, name=name, in_specs=[_HBM] * (2 * m) + [_SEM, _SEM, _ANY], out_specs=[_HBM] * (2 * m),
        out_shape=_hbm_like(srcs) + _hbm_like(lands), input_output_aliases={t: t for t in range(2 * m)},
        compiler_params=_SPLIT_COPY)(*srcs, *lands, *sems, after)
    return res[:m], res[m:]


def _chip_copies(srcs, lands, small_src, small_land, send_sems, recv_sems):
    x, y, c, chip, peers = _place()
    n = len(srcs)
    copies = []
    for r, (px, py) in enumerate(peers):
        for t in range(n):
            k = t * N_PEERS + r
            copies.append(_remote(srcs[t].at[2 * px + py], lands[t].at[r], send_sems.at[k], recv_sems.at[k], (px, py, c)))
        if small_src is not None:
            k = n * N_PEERS + r
            half_s = small_src.shape[0] // 2
            copies.append(_remote(small_src.at[pl.ds(c * half_s, half_s)], small_land.at[r],
                                  send_sems.at[k], recv_sems.at[k], (px, py, c)))
    return copies


def _chip_start(name, sums_bf16, small=None):
    n = len(sums_bf16)
    srcs = list(sums_bf16) + ([small] if small is not None else [])
    m = len(srcs)
    lands = [pltpu.HBM((N_PEERS,) + s.shape[1:], BF16) for s in sums_bf16]
    if small is not None:
        lands.append(pltpu.HBM((N_PEERS, small.shape[0] // 2, 128), F32))
    nk = m * N_PEERS

    def body(*refs):
        src_refs, land_refs = refs[m:2 * m], refs[2 * m:3 * m]
        send_sems, recv_sems, token = refs[3 * m:]
        small_src, small_land = (src_refs[n], land_refs[n]) if small is not None else (None, None)
        for cp in _chip_copies(src_refs[:n], land_refs[:n], small_src, small_land, send_sems, recv_sems):
            cp.start()
        token[...] = jnp.zeros_like(token)

    res = pl.pallas_call(
        body, name=name, in_specs=[_HBM] * m,
        out_specs=[_HBM] * (2 * m) + [_SEM, _SEM, pl.BlockSpec(memory_space=pltpu.VMEM)],
        out_shape=_hbm_like(srcs) + lands + [pltpu.SemaphoreType.DMA((nk,)), pltpu.SemaphoreType.DMA((nk,)),
                                             _sds((8, 128), F32)],
        input_output_aliases={t: t for t in range(m)}, compiler_params=_SPLIT_COPY)(*_hbm(srcs))
    return (res[:m], res[m:2 * m], (res[2 * m], res[2 * m + 1])), res[2 * m + 2]


def _chip_wait(name, flight, with_small, after):
    srcs, lands, sems = flight
    m = len(srcs)
    n = m - 1 if with_small else m

    def body(*refs):
        src_refs, land_refs = refs[:m], refs[m:2 * m]
        send_sems, recv_sems = refs[2 * m], refs[2 * m + 1]
        small_src, small_land = (src_refs[n], land_refs[n]) if with_small else (None, None)
        for cp in _chip_copies(src_refs[:n], land_refs[:n], small_src, small_land, send_sems, recv_sems):
            cp.wait_send()
            cp.wait_recv()

    res = pl.pallas_call(
        body, name=name, in_specs=[_HBM] * (2 * m) + [_SEM, _SEM, _ANY], out_specs=[_HBM] * (2 * m),
        out_shape=_hbm_like(srcs) + _hbm_like(lands), input_output_aliases={t: t for t in range(2 * m)},
        compiler_params=_SPLIT_COPY)(*srcs, *lands, *sems, after)
    return res[:m], res[m:]


def _swap_start(name, bufs):
    n = len(bufs)

    def body(*refs):
        _sibling_handshake()
        outs = refs[n:2 * n]
        send_sems, recv_sems, token = refs[2 * n:]
        x, y, c, _, _ = _place()
        for t in range(n):
            h = outs[t].shape[0] // 2
            mine = outs[t].at[pl.ds(c * h, h)]
            _remote(mine, mine, send_sems.at[t], recv_sems.at[t], (x, y, 1 - c)).start()
        token[...] = jnp.zeros_like(token)

    res = pl.pallas_call(
        body, name=name, in_specs=[_HBM] * n, out_specs=[_HBM] * n + [_SEM, _SEM, pl.BlockSpec(memory_space=pltpu.VMEM)],
        out_shape=_hbm_like(bufs) + [pltpu.SemaphoreType.DMA((n,)), pltpu.SemaphoreType.DMA((n,)), _sds((8, 128), F32)],
        input_output_aliases={t: t for t in range(n)}, compiler_params=_SPLIT_COPY_SIBLING)(*_hbm(bufs))
    return (res[:n], (res[n], res[n + 1])), res[n + 2]


def _swap_wait(name, flight, after):
    bufs, sems = flight
    n = len(bufs)

    def body(*refs):
        ins = refs[:n]
        send_sems, recv_sems = refs[n], refs[n + 1]
        x, y, c, _, _ = _place()
        for t in range(n):
            h = ins[t].shape[0] // 2
            cp = _remote(ins[t].at[pl.ds(c * h, h)], ins[t].at[pl.ds((1 - c) * h, h)], send_sems.at[t],
                         recv_sems.at[t], (x, y, 1 - c))
            cp.wait_send()
            cp.wait_recv()

    return pl.pallas_call(
        body, name=name, in_specs=[_HBM] * n + [_SEM, _SEM, _ANY], out_specs=[_HBM] * n, out_shape=_hbm_like(bufs),
        input_output_aliases={t: t for t in range(n)}, compiler_params=_SPLIT_COPY)(*bufs, *sems, after)


def _pair_sum(name, grad, got, ids):
    _, rows, cols = got.shape
    rb = min(rows, 256)
    nb = rows // rb
    blk = pl.BlockSpec((None, rb, cols), lambda i, j, ids: (j, i, 0))
    mine = pl.BlockSpec((None, rb, cols), lambda i, j, ids: (j, ids[1] * nb + i, 0))
    own = pl.BlockSpec((rb, cols), lambda i, j, ids: (i, 0))

    def body(ids_ref, a_ref, b_ref, s_ref, sb_ref):
        s = a_ref[...] + b_ref[...]
        sb_ref[...] = s.astype(BF16)

        @pl.when(pl.program_id(1) == ids_ref[0])
        def _():
            s_ref[...] = s

    return _prefetch_call(body, name, ids, (nb, N_CHIPS), [mine, blk], [own, blk],
                          [_sds((rows, cols), F32), _sds(got.shape, BF16)], (grad, got),
                          semantics=("parallel", "arbitrary"))


def _chip_sum(name, own_sum, got, ids):
    rows, cols = own_sum.shape
    rb = min(rows, 256)
    nb = rows // rb
    own = pl.BlockSpec((rb, cols), lambda i, ids: (i, 0))
    blk3 = pl.BlockSpec((N_PEERS, rb, cols), lambda i, ids: (0, i, 0))
    out = pl.BlockSpec((rb, cols), lambda i, ids: (ids[1] * nb + i, 0))

    def body(ids_ref, a_ref, b_ref, o_ref):
        o_ref[...] = ((a_ref[...] + b_ref[0].astype(F32)) + b_ref[1].astype(F32)) + b_ref[2].astype(F32)

    return _prefetch_call(body, name, ids, (nb,), [own, blk3], out, _sds((2 * rows, cols), F32), (own_sum, got))


SMALL_RB = 280


def _small_pair_sum(own, got):
    blk = pl.BlockSpec((SMALL_RB, 128), lambda i: (i, 0))

    def body(a_ref, b_ref, o_ref):
        o_ref[...] = a_ref[...] + b_ref[...]

    return pl.pallas_call(body, name="small_pair_sum", grid=(own.shape[0] // SMALL_RB,), in_specs=[blk, blk],
                          out_specs=blk, out_shape=_sds(own.shape, F32),
                          compiler_params=_params(("parallel",)))(own, got)


def _small_chip_sum(pair, got, ids):
    nb = pair.shape[0] // 2 // SMALL_RB
    half = pl.BlockSpec((SMALL_RB, 128), lambda i, ids: (ids[1] * nb + i, 0))
    blk3 = pl.BlockSpec((N_PEERS, SMALL_RB, 128), lambda i, ids: (0, i, 0))

    def body(ids_ref, a_ref, b_ref, o_ref):
        o_ref[...] = (a_ref[...] + b_ref[1]) + (b_ref[0] + b_ref[2])

    return _prefetch_call(body, "small_chip_sum", ids, (nb,), [half, blk3], half, _sds(pair.shape, F32), (pair, got))


def _adamw_math(w, g, m, v):
    m = ADAM_B1 * m + (1.0 - ADAM_B1) * g
    v = ADAM_B2 * v + (1.0 - ADAM_B2) * (g * g)
    m_hat = m / (1.0 - ADAM_B1 ** ADAM_STEP)
    v_hat = v / (1.0 - ADAM_B2 ** ADAM_STEP)
    delta = -ADAM_LR * (m_hat / (jnp.sqrt(v_hat) + ADAM_EPS) + ADAM_WD * w)
    return delta, m, v


def _adamw(name, w, g, m, v, rb=None):
    rows, cols = w.shape
    rb = rows if rb is None else rb
    blk = pl.BlockSpec((rb, cols), lambda i: (i, 0))

    def body(w_ref, g_ref, m_ref, v_ref, d_ref, nm_ref, nv_ref):
        d, nm, nv = _adamw_math(w_ref[...], g_ref[...], m_ref[...], v_ref[...])
        d_ref[...] = d
        nm_ref[...] = nm
        nv_ref[...] = nv

    return pl.pallas_call(body, name=name, grid=(rows // rb,), in_specs=[blk] * 4, out_specs=[blk] * 3,
                          out_shape=[_sds(w.shape, F32)] * 3, compiler_params=_params(("parallel",)))(w, g, m, v)


def _adamw_small(ws, gs, ms, vs):
    n = len(ws)

    def body(*refs):
        for t in range(n):
            w_ref, g_ref, m_ref, v_ref = (refs[k * n + t] for k in range(4))
            d, nm, nv = _adamw_math(w_ref[...], g_ref[...], m_ref[...], v_ref[...])
            for k, val in enumerate((d, nm, nv)):
                refs[(4 + k) * n + t][...] = val

    res = pl.pallas_call(body, name="adamw_small", out_shape=[_sds(a.shape, F32) for a in ws] * 3,
                         compiler_params=_params())(*ws, *gs, *ms, *vs)
    return [(res[t], res[n + t], res[2 * n + t]) for t in range(n)]


BIG = ("w_in", "w_att_o", "w_rec_o", "w_out", "w_ff1", "w_ff2")
SHARDED_VECS = ("conv_w", "b_rg_a", "b_rg_i", "lru_lambda")
SMALL = ("ln1_g", "b_in", "rpb", "conv_w", "conv_b", "w_rg_a", "b_rg_a", "w_rg_i", "b_rg_i", "lru_lambda",
         "ln2_g", "lnf_g")
SMALL_ROWS = 2240
ORDER = ("ln1_g", "w_in", "b_in", "rpb", "w_att_o", "conv_w", "conv_b", "w_rg_a", "b_rg_a", "w_rg_i", "b_rg_i",
         "lru_lambda", "w_rec_o", "w_out", "ln2_g", "w_ff1", "w_ff2", "lnf_g")


def _pack_small(grads, loss):
    parts, sizes = [], {}
    for n in SMALL:
        flat = grads[n].reshape(-1)
        pad = (-flat.shape[0]) % 128
        sizes[n] = (flat.shape[0], flat.shape[0] + pad)
        parts.append(jnp.pad(flat, (0, pad)))
    total = sum(s[1] for s in sizes.values())
    parts.append(jnp.pad(loss.reshape(1), (0, SMALL_ROWS * 128 - total - 1)))
    return jnp.concatenate(parts).reshape(SMALL_ROWS, 128), sizes


def _unpack_small(buf, sizes, shapes):
    flat = buf.reshape(-1)
    out, pos = {}, 0
    for n in SMALL:
        size, padded = sizes[n]
        out[n] = flat[pos:pos + size].reshape(shapes[n])
        pos += padded
    return out, flat[pos]


def _gather_weights(w, chip):
    chip_id = chip.astype(jnp.int32).reshape(1)
    vec_rows = [w[n][0] for n in SHARDED_VECS]
    vec_shard = jnp.concatenate(vec_rows + [jnp.zeros((16 - 10, D // N_CHIPS), F32)], axis=0)
    vec_slots = lax.dynamic_update_slice(jnp.zeros((N_CHIPS, 16, D // N_CHIPS), F32), vec_shard[None], (chip, 0, 0))
    bufs_a, sems_a, token_a = _gather_start("gather_start_first", [_cast_bf16("cast_w_in", w["w_in"][0], chip_id), vec_slots])
    rest_names = BIG[1:]
    bufs_b, sems_b, token_b = _gather_start(
        "gather_start_rest", [_cast_bf16("cast_" + n, w[n][0], chip_id, after=(token_a,)) for n in rest_names])

    def first(after):
        w_in_full, vec_full = _gather_forward("gather_forward_first", _gather_wait("gather_wait_first", bufs_a, sems_a, after))
        vecs = vec_full.transpose(1, 0, 2).reshape(16, D)
        return dict(w_in=w_in_full, conv_w=vecs[0:4], b_rg_a=vecs[4:6], b_rg_i=vecs[6:8], lru_lambda=vecs[8:10])

    def rest(after):
        full = dict(zip(rest_names, _gather_forward("gather_forward_rest",
                                                    _gather_wait("gather_wait_rest", bufs_b, sems_b, after))))
        return dict(w_att_o=full["w_att_o"], w_ff1=full["w_ff1"], w_rec_o=full["w_rec_o"].reshape(D, D),
                    w_out=full["w_out"].reshape(D, D), w_ff2=full["w_ff2"].reshape(D_FF, D))

    p = dict(ln1_g=w["ln1_g"], b_in=w["b_in"], rpb=w["rpb"][0], conv_b=w["conv_b"], w_rg_a=w["w_rg_a"][0],
             w_rg_i=w["w_rg_i"][0], ln2_g=w["ln2_g"], lnf_g=w["lnf_g"].reshape(1, D))
    return p, ((token_b,), first, rest)


class _Reducer:
    def __init__(self, ids):
        self.ids = ids
        self.groups = {}

    def begin(self, tag, grads, small=None):
        names = list(grads)
        big = [grads[n].reshape(N_CHIPS, -1, grads[n].shape[-1]) for n in names]
        flight, token = _pair_start("pair_start_" + tag, big, [] if small is None else [small])
        self.groups[tag] = dict(names=names, pair=flight, small=small is not None)
        return (token,)

    def advance(self, tag, after):
        grp = self.groups[tag]
        n = len(grp["names"])
        mine, got = _pair_wait("pair_wait_" + tag, grp["pair"], n, after)
        sums = [_pair_sum("pair_sum_" + name, a, b, self.ids) for name, a, b in zip(grp["names"], mine, got)]
        small_sum = _small_pair_sum(mine[n], got[n]) if grp["small"] else None
        grp["chip"], token = _chip_start("chip_start_" + tag, [s[1] for s in sums], small_sum)
        grp["sums"] = [s[0] for s in sums]
        self.last_token = token
        return (token,)

    def finish(self, tag, after):
        grp = self.groups[tag]
        srcs, lands = _chip_wait("chip_wait_" + tag, grp["chip"], grp["small"], after)
        halves = [_chip_sum("chip_sum_" + name, s, b, self.ids) for name, s, b in zip(grp["names"], grp["sums"], lands)]
        if grp["small"]:
            halves.append(_small_chip_sum(srcs[-1], lands[-1], self.ids))
        grp["swap"], token = _swap_start("swap_start_" + tag, halves)
        return token

    def interlude(self, tokens):
        after = tokens[0]
        for tag in list(self.groups)[:-1]:
            after = self.finish(tag, after)
        return after

    def result(self, tag, after):
        return _swap_wait("swap_wait_" + tag, self.groups[tag]["swap"], after)


def kernel(x, ln1_g, w_in, b_in, rpb, w_att_o, conv_w, conv_b, w_rg_a, b_rg_a, w_rg_i, b_rg_i, lru_lambda, w_rec_o, w_out, ln2_g, w_ff1, w_ff2, lnf_g, loss_target, m_ln1_g, m_w_in, m_b_in, m_rpb, m_w_att_o, m_conv_w, m_conv_b, m_w_rg_a, m_b_rg_a, m_w_rg_i, m_b_rg_i, m_lru_lambda, m_w_rec_o, m_w_out, m_ln2_g, m_w_ff1, m_w_ff2, m_lnf_g, v_ln1_g, v_w_in, v_b_in, v_rpb, v_w_att_o, v_conv_w, v_conv_b, v_w_rg_a, v_b_rg_a, v_w_rg_i, v_b_rg_i, v_lru_lambda, v_w_rec_o, v_w_out, v_ln2_g, v_w_ff1, v_w_ff2, v_lnf_g):
    w = dict(ln1_g=ln1_g, w_in=w_in, b_in=b_in, rpb=rpb, w_att_o=w_att_o, conv_w=conv_w, conv_b=conv_b,
             w_rg_a=w_rg_a, b_rg_a=b_rg_a, w_rg_i=w_rg_i, b_rg_i=b_rg_i, lru_lambda=lru_lambda, w_rec_o=w_rec_o,
             w_out=w_out, ln2_g=ln2_g, w_ff1=w_ff1, w_ff2=w_ff2, lnf_g=lnf_g)
    m = dict(ln1_g=m_ln1_g, w_in=m_w_in, b_in=m_b_in, rpb=m_rpb, w_att_o=m_w_att_o, conv_w=m_conv_w,
             conv_b=m_conv_b, w_rg_a=m_w_rg_a, b_rg_a=m_b_rg_a, w_rg_i=m_w_rg_i, b_rg_i=m_b_rg_i,
             lru_lambda=m_lru_lambda, w_rec_o=m_w_rec_o, w_out=m_w_out, ln2_g=m_ln2_g, w_ff1=m_w_ff1,
             w_ff2=m_w_ff2, lnf_g=m_lnf_g)
    v = dict(ln1_g=v_ln1_g, w_in=v_w_in, b_in=v_b_in, rpb=v_rpb, w_att_o=v_w_att_o, conv_w=v_conv_w,
             conv_b=v_conv_b, w_rg_a=v_w_rg_a, b_rg_a=v_b_rg_a, w_rg_i=v_w_rg_i, b_rg_i=v_b_rg_i,
             lru_lambda=v_lru_lambda, w_rec_o=v_w_rec_o, w_out=v_w_out, ln2_g=v_ln2_g, w_ff1=v_w_ff1,
             w_ff2=v_w_ff2, lnf_g=v_lnf_g)
    chip = 2 * lax.axis_index("x") + lax.axis_index("y")
    ids = jnp.stack([chip, lax.axis_index("c")]).astype(jnp.int32)

    out_grad, out_delta, out_m, out_v = {}, {}, {}, {}

    def update(n, gn):
        shape, two_d = w[n].shape, gn.shape
        d, nm, nv = _adamw("adamw_" + n, w[n].reshape(two_d), gn, m[n].reshape(two_d), v[n].reshape(two_d), 256)
        out_grad[n], out_delta[n], out_m[n], out_v[n] = (gn.reshape(shape), d.reshape(shape), nm.reshape(shape),
                                                         nv.reshape(shape))
        return d

    reducer = _Reducer(ids)
    p, late = _gather_weights(w, chip)
    loss, grad_x, g = _local_step(x, loss_target, p, late, reducer)
    small, sizes = _pack_small(g, loss + reducer.last_token[:1, :1])
    after = reducer.begin("small", {}, small)[0]
    for tag in ("ff", "proj", "in"):
        for n, red in zip(reducer.groups[tag]["names"], reducer.result(tag, after)):
            after = update(n, red)
        if tag == "ff":
            after = reducer.finish("in", reducer.advance("small", after)[0])
    (small_red,) = reducer.result("small", reducer.finish("small", after))
    gsmall, loss = _unpack_small(small_red, sizes, {n: g[n].shape for n in SMALL})
    two_d = {n: (int(np.prod(w[n].shape[:-1])), w[n].shape[-1]) for n in SMALL}
    for n in SHARDED_VECS:
        gsmall[n] = lax.dynamic_slice_in_dim(gsmall[n], chip * (D // N_CHIPS), D // N_CHIPS, axis=1)
    gs = [gsmall[n].reshape(two_d[n]) for n in SMALL]
    updates = _adamw_small([w[n].reshape(two_d[n]) for n in SMALL], gs, [m[n].reshape(two_d[n]) for n in SMALL],
                           [v[n].reshape(two_d[n]) for n in SMALL])
    for n, gn, (d, nm, nv) in zip(SMALL, gs, updates):
        shape = w[n].shape
        out_grad[n], out_delta[n], out_m[n], out_v[n] = (gn.reshape(shape), d.reshape(shape), nm.reshape(shape),
                                                         nv.reshape(shape))
    return (loss, grad_x, *[out_grad[n] for n in ORDER], *[out_delta[n] for n in ORDER],
            *[out_m[n] for n in ORDER], *[out_v[n] for n in ORDER])
```

```python
import numpy as np
import jax
import jax.numpy as jnp
from jax import lax
from jax.experimental import pallas as pl
from jax.experimental.pallas import tpu as pltpu

F32 = jnp.float32
BF16 = jnp.bfloat16

T = 2048
D = 1024
D_ATT = 512
D_IN = 5632
D_FF = 4096
N_HEADS = 8
HEAD_DIM = 64
GRID_W = 64
N_ROWS = T // GRID_W
WIN_H = 8
WIN_W = 16
KEYS = WIN_H * GRID_W
N_CHIPS = 4
EPS = 1e-6
LRU_C = 8.0
SCALE = HEAD_DIM ** -0.5
REC_CB = 256
REC_CHUNK = 256
PAD = 8

ADAM_LR = 0.001
ADAM_B1 = 0.9
ADAM_B2 = 0.999
ADAM_EPS = 1e-08
ADAM_WD = 0.01
ADAM_STEP = 10

VMEM_LIMIT = 56 * 1024 * 1024

NN = (((1,), (0,)), ((), ()))
NT = (((1,), (1,)), ((), ()))
TN = (((0,), (0,)), ((), ()))
MESH = pl.DeviceIdType.MESH


def _params(sem=None):
    return pltpu.CompilerParams(dimension_semantics=sem, vmem_limit_bytes=VMEM_LIMIT)


def _dot(a, b, dims):
    return lax.dot_general(a, b, dims, preferred_element_type=F32)


def _sigmoid(x):
    return 0.5 * jnp.tanh(0.5 * x) + 0.5


def _matmul(name, a, b, *, dims, grid, a_spec, b_spec, out_shapes, out_specs, acc_shape,
            extras=(), extra_specs=(), epilogue=None, colsum_spec=None, colsum_shape=None, after=(),
            semantics=("parallel", "parallel", "arbitrary"), epilogue_takes_first=False):
    nk = grid[2]
    n_extra = len(extras)
    n_out = len(out_shapes)
    with_colsum = colsum_spec is not None

    def body(a_ref, b_ref, *rest):
        ex = rest[:n_extra]
        rest = rest[:n_extra] + rest[n_extra + len(after):]
        outs = rest[n_extra:n_extra + n_out]
        pos = n_extra + n_out
        cs_out = rest[pos] if with_colsum else None
        pos += 1 if with_colsum else 0
        acc = rest[pos]
        cs_acc = rest[pos + 1] if with_colsum else None
        k = pl.program_id(2)
        first_tile = pl.program_id(0) == 0

        @pl.when(k == 0)
        def _():
            acc[...] = jnp.zeros_like(acc)
            if with_colsum:
                cs_acc[...] = jnp.zeros_like(cs_acc)

        bv = b_ref[...]
        acc[...] += _dot(a_ref[...].astype(BF16), bv.astype(BF16), dims)
        if with_colsum:
            cs_acc[...] += jnp.sum(bv.astype(F32), axis=0, keepdims=True)

        @pl.when(k == nk - 1)
        def _():
            r = acc[...]
            if epilogue is None:
                outs[0][...] = r.astype(outs[0].dtype)
            elif epilogue_takes_first:
                epilogue(r, ex, outs, first_tile)
            else:
                epilogue(r, ex, outs)
            if with_colsum:
                cs_out[...] = cs_acc[...]

    shapes = list(out_shapes)
    specs = list(out_specs)
    scratch = [pltpu.VMEM(acc_shape, F32)]
    if with_colsum:
        shapes.append(colsum_shape)
        specs.append(colsum_spec)
        scratch.append(pltpu.VMEM((1, acc_shape[1]), F32))
    res = pl.pallas_call(
        body, name=name, grid=grid,
        in_specs=[a_spec, b_spec, *extra_specs] + [_ANY] * len(after),
        out_specs=specs, out_shape=shapes, scratch_shapes=scratch,
        compiler_params=_params(semantics),
    )(a, b, *extras, *after)
    return res


def _sds(shape, dtype):
    return jax.ShapeDtypeStruct(shape, dtype)


TM = 1024
NI = T // TM
TJ = T
NJ = T // TJ


def _mm_nn_cols(name, a, wg, out_dtype, *, bias=None, extras=(), extra_specs=(), epilogue=None,
                out_shapes=None, out_specs=None):
    k_dim, n4 = wg.shape[1], wg.shape[2]
    ex, exs = list(extras), list(extra_specs)
    if bias is not None:
        ex = [bias] + ex
        exs = [pl.BlockSpec((1, n4), lambda j, i, k: (0, j))] + exs
        user_ep = epilogue

        def epilogue(r, e, outs):
            r = r + e[0][...]
            if user_ep is None:
                outs[0][...] = r.astype(outs[0].dtype)
            else:
                user_ep(r, e[1:], outs)
    if out_shapes is None:
        out_shapes = [_sds((T, N_CHIPS * n4), out_dtype)]
        out_specs = [pl.BlockSpec((TJ, n4), lambda j, i, k: (i, j))]
    return _matmul(
        name, a, wg, dims=NN, grid=(N_CHIPS, NJ, 1),
        a_spec=pl.BlockSpec((TJ, k_dim), lambda j, i, k: (i, 0)),
        b_spec=pl.BlockSpec((None, k_dim, n4), lambda j, i, k: (j, 0, 0)),
        out_shapes=out_shapes, out_specs=out_specs, acc_shape=(TJ, n4),
        extras=ex, extra_specs=exs, epilogue=epilogue)


def _mm_nt_cols_rms_bwd(name, a, wg, x, g, dres, after=(), bf16_copy=False):
    n4 = wg.shape[2]
    row = pl.BlockSpec((TM, D), lambda i, j, k: (i, 0))
    vec = pl.BlockSpec((1, D), lambda i, j, k: (0, 0))

    def epilogue(dhv, ex, outs, first):
        x_ref, g_ref, dres_ref = ex
        dx_ref, dg_ref = outs[0], outs[-1]
        xv = x_ref[...]
        rstd = lax.rsqrt(jnp.mean(xv * xv, axis=-1, keepdims=True) + EPS)
        xhat = xv * rstd
        dy = dhv * g_ref[...]
        dx = dres_ref[...] + rstd * (dy - xhat * jnp.mean(dy * xhat, axis=-1, keepdims=True))
        dx_ref[...] = dx
        if bf16_copy:
            outs[1][...] = dx.astype(BF16)
        part = jnp.sum(dhv * xhat, axis=0, keepdims=True)

        @pl.when(first)
        def _():
            dg_ref[...] = part

        @pl.when(jnp.logical_not(first))
        def _():
            dg_ref[...] += part

    return _matmul(
        name, a, wg, dims=NT, grid=(NI, 1, N_CHIPS),
        a_spec=pl.BlockSpec((TM, n4), lambda i, j, k: (i, k)),
        b_spec=pl.BlockSpec((None, D, n4), lambda i, j, k: (k, 0, 0)),
        out_shapes=[_sds((T, D), F32)] + [_sds((T, D), BF16)] * bf16_copy + [_sds((1, D), F32)],
        out_specs=[row] + [row] * bf16_copy + [vec], acc_shape=(TM, D),
        extras=[x, g, dres], extra_specs=[row, vec, row], epilogue=epilogue, after=after,
        semantics=("arbitrary", "arbitrary", "arbitrary"), epilogue_takes_first=True)


def _mm_nt_rows(name, a, w, out_dtype, *, tn, extras=(), extra_specs=(), epilogue=None):
    k_dim, n = w.shape
    return _matmul(
        name, a, w, dims=NT, grid=(k_dim // tn, NJ, 1),
        a_spec=pl.BlockSpec((TJ, n), lambda j, i, k: (i, 0)),
        b_spec=pl.BlockSpec((tn, n), lambda j, i, k: (j, 0)),
        out_shapes=[_sds((T, k_dim), out_dtype)],
        out_specs=[pl.BlockSpec((TJ, tn), lambda j, i, k: (i, j))], acc_shape=(TJ, tn),
        extras=extras, extra_specs=extra_specs, epilogue=epilogue)


def _mm_tn_cols(name, a, g, n4, *, colsum=False):
    k_dim = a.shape[1]
    kw = {}
    if colsum:
        kw = dict(colsum_spec=pl.BlockSpec((1, n4), lambda j, i, k: (0, j)),
                  colsum_shape=_sds((1, N_CHIPS * n4), F32))
    return _matmul(
        name, a, g, dims=TN, grid=(N_CHIPS, 1, NJ),
        a_spec=pl.BlockSpec((TJ, k_dim), lambda j, i, k: (k, 0)),
        b_spec=pl.BlockSpec((TJ, n4), lambda j, i, k: (k, j)),
        out_shapes=[_sds((N_CHIPS, k_dim, n4), F32)],
        out_specs=[pl.BlockSpec((None, k_dim, n4), lambda j, i, k: (j, 0, 0))],
        acc_shape=(k_dim, n4), **kw)


def _mm_tn_rows(name, a, g, *, tm):
    k_dim, n = a.shape[1], g.shape[1]
    return _matmul(
        name, a, g, dims=TN, grid=(k_dim // tm, 1, NJ),
        a_spec=pl.BlockSpec((TJ, tm), lambda j, i, k: (k, j)),
        b_spec=pl.BlockSpec((TJ, n), lambda j, i, k: (k, 0)),
        out_shapes=[_sds((k_dim, n), F32)],
        out_specs=[pl.BlockSpec((tm, n), lambda j, i, k: (j, 0))], acc_shape=(tm, n))


TE = 256
NE = T // TE
_ROW = pl.BlockSpec((TE, D), lambda i: (i, 0))
_VEC = pl.BlockSpec((1, D), lambda i: (0, 0))


def _rms_fwd(name, x, g, after=()):
    def body(x_ref, g_ref, *rest):
        h_ref = rest[-1]
        xv = x_ref[...]
        rstd = lax.rsqrt(jnp.mean(xv * xv, axis=-1, keepdims=True) + EPS)
        h_ref[...] = (xv * rstd * g_ref[...]).astype(BF16)

    return pl.pallas_call(body, name=name, grid=(NE,), in_specs=[_ROW, _VEC] + [_ANY] * len(after), out_specs=_ROW,
                          out_shape=_sds((T, D), BF16), compiler_params=_params(("parallel",)))(x, g, *after)


def _mm_x2_loss_head(s, w_ff2, x1, target, g):
    k_dim = w_ff2.shape[0]
    row = pl.BlockSpec((TM, D), lambda i, j, k: (i, 0))
    vec = pl.BlockSpec((1, D), lambda i, j, k: (0, 0))

    def epilogue(r, ex, outs, first):
        x1_ref, t_ref, g_ref = ex
        loss_ref, dx_ref, dxb_ref, dg_ref = outs
        xv = x1_ref[...] + r
        rstd = lax.rsqrt(jnp.mean(xv * xv, axis=-1, keepdims=True) + EPS)
        xhat = xv * rstd
        gv = g_ref[...]
        err = xhat * gv - t_ref[...]
        dy = err * (1.0 / D)
        dxh = dy * gv
        dx = rstd * (dxh - xhat * jnp.mean(dxh * xhat, axis=-1, keepdims=True))
        dx_ref[...] = dx
        dxb_ref[...] = dx.astype(BF16)
        dg_part = jnp.sum(dy * xhat, axis=0, keepdims=True)
        loss_part = (0.5 / D) * jnp.sum(jnp.sum(err * err, axis=1, keepdims=True), axis=0, keepdims=True)

        @pl.when(first)
        def _():
            dg_ref[...] = dg_part
            loss_ref[...] = loss_part

        @pl.when(jnp.logical_not(first))
        def _():
            dg_ref[...] += dg_part
            loss_ref[...] += loss_part

    return _matmul(
        "mm_x2_loss_head", s, w_ff2, dims=NN, grid=(NI, 1, k_dim // D),
        a_spec=pl.BlockSpec((TM, D), lambda i, j, k: (i, k)), b_spec=pl.BlockSpec((D, D), lambda i, j, k: (k, 0)),
        out_shapes=[_sds((1, 1), F32), _sds((T, D), F32), _sds((T, D), BF16), _sds((1, D), F32)],
        out_specs=[pl.BlockSpec((1, 1), lambda i, j, k: (0, 0)), row, row, vec], acc_shape=(TM, D),
        extras=[x1, target, g], extra_specs=[row, row, vec], epilogue=epilogue,
        semantics=("arbitrary", "arbitrary", "arbitrary"), epilogue_takes_first=True)


MW = 512
_G_ATT_BLK = 3584 // MW
_G_REC_BLK = 4608 // MW


TB = 512


def _branch_specs():
    def row(cols):
        return pl.BlockSpec((TB, cols), lambda i: (i, 0))

    ga = pl.BlockSpec((TB, MW), lambda i: (i, _G_ATT_BLK))
    ga2 = pl.BlockSpec((TB, MW), lambda i: (i, _G_ATT_BLK + 1))
    gr = pl.BlockSpec((TB, MW), lambda i: (i, _G_REC_BLK))
    gr2 = pl.BlockSpec((TB, MW), lambda i: (i, _G_REC_BLK + 1))
    w_att = pl.BlockSpec((N_CHIPS, D_ATT, D // N_CHIPS), lambda i: (0, 0, 0))
    w_sq = pl.BlockSpec((D, D), lambda i: (0, 0))
    return row, (ga, ga2, gr, gr2), w_att, w_sq


def _gate_values(gate_refs):
    ga, ga2, gr, gr2 = (r[...].astype(F32) for r in gate_refs)
    return _sigmoid(jnp.concatenate([ga, ga2], axis=1)), _sigmoid(jnp.concatenate([gr, gr2], axis=1))


def _branches_fwd(att, g, z, x, w_att_o, w_rec_o, w_out, ln2_g):
    row, gate_specs, w_att, w_sq = _branch_specs()

    def body(att_ref, g_ref, ga_ref, ga2_ref, gr_ref, gr2_ref, x_ref, wa_ref, wr_ref, wo_ref, g2_ref,
             ya_ref, yr_ref, m_ref, x1_ref, h2_ref):
        attv = att_ref[...]
        ya = jnp.concatenate([_dot(attv, wa_ref[j], NN) for j in range(N_CHIPS)], axis=1)
        yr = _dot(g_ref[...], wr_ref[...], NN)
        sa, sr = _gate_values((ga_ref, ga2_ref, gr_ref, gr2_ref))
        mixed = (sa * ya + sr * yr).astype(BF16)
        ya_ref[...] = ya
        yr_ref[...] = yr
        m_ref[...] = mixed
        x1 = x_ref[...] + _dot(mixed, wo_ref[...], NN)
        x1_ref[...] = x1
        rstd = lax.rsqrt(jnp.mean(x1 * x1, axis=-1, keepdims=True) + EPS)
        h2_ref[...] = (x1 * rstd * g2_ref[...]).astype(BF16)

    return pl.pallas_call(
        body, name="branches_fwd", grid=(T // TB,),
        in_specs=[row(D_ATT), row(D), *gate_specs, row(D), w_att, w_sq, w_sq, pl.BlockSpec((1, D), lambda i: (0, 0))],
        out_specs=[row(D)] * 5,
        out_shape=[_sds((T, D), F32), _sds((T, D), F32), _sds((T, D), BF16), _sds((T, D), F32), _sds((T, D), BF16)],
        compiler_params=_params(("parallel",)))(att, g, z, z, z, z, x, w_att_o, w_rec_o, w_out, ln2_g)


def _branches_bwd(dx1_b, y_att, y_rec, z, w_att_o, w_rec_o, w_out):
    row, gate_specs, w_att, w_sq = _branch_specs()
    n4 = D // N_CHIPS

    def body(dx_ref, ya_ref, yr_ref, ga_ref, ga2_ref, gr_ref, gr2_ref, wa_ref, wr_ref, wo_ref,
             dya_ref, dyr_ref, dga_ref, dgr_ref, datt_ref, dg_ref):
        dm = _dot(dx_ref[...], wo_ref[...], NT)
        sa, sr = _gate_values((ga_ref, ga2_ref, gr_ref, gr2_ref))
        dya = (dm * sa).astype(BF16)
        dyr = (dm * sr).astype(BF16)
        dya_ref[...] = dya
        dyr_ref[...] = dyr
        dga_ref[...] = (dm * ya_ref[...] * sa * (1.0 - sa)).astype(BF16)
        dgr_ref[...] = (dm * yr_ref[...] * sr * (1.0 - sr)).astype(BF16)
        datt = _dot(dya[:, 0:n4], wa_ref[0], NT)
        for j in range(1, N_CHIPS):
            datt = datt + _dot(dya[:, j * n4:(j + 1) * n4], wa_ref[j], NT)
        datt_ref[...] = datt.astype(BF16)
        dg_ref[...] = _dot(dyr, wr_ref[...], NT).astype(BF16)

    return pl.pallas_call(
        body, name="branches_bwd", grid=(T // TB,),
        in_specs=[row(D), row(D), row(D), *gate_specs, w_att, w_sq, w_sq],
        out_specs=[row(D)] * 4 + [row(D_ATT), row(D)],
        out_shape=[_sds((T, D), BF16)] * 4 + [_sds((T, D_ATT), BF16), _sds((T, D), BF16)],
        compiler_params=_params(("parallel",)))(dx1_b, y_att, y_rec, z, z, z, z, w_att_o, w_rec_o, w_out)


HP = 2 * HEAD_DIM
N_HP = N_HEADS // 2
ATT_UNROLL_FWD = 16
ATT_UNROLL_BWD = 8
DIAG_ROWS = 32


def _window_maps():
    diag = np.zeros((GRID_W * GRID_W, 128), np.float32)
    for qc in range(GRID_W):
        w0 = min(max(qc - WIN_W // 2, 0), GRID_W - WIN_W)
        for kc in range(w0, w0 + WIN_W):
            diag[qc * GRID_W + kc, kc - qc + WIN_W - 1] = 1.0
    return diag, diag.sum(axis=1)[None, :]


def _split3(x):
    a = x.astype(BF16)
    r = x - a.astype(F32)
    b = r.astype(BF16)
    c = (r - b.astype(F32)).astype(BF16)
    return a, b, c


N_DROW = 2 * WIN_H - 1
N_DPAIR = N_DROW - 1


def _bias_pairs(rpb):
    diag, valid = _window_maps()
    r2 = jnp.pad(rpb.reshape(N_HEADS * N_DROW, 2 * WIN_W - 1),
                 ((0, 128 - N_HEADS * N_DROW), (0, 128 - (2 * WIN_W - 1))))

    def body(r_ref, d_ref, v_ref, o_ref):
        dv = d_ref[...]
        t = sum(_dot(part, dv, NN) for part in _split3(r_ref[...]))
        o_ref[...] = jnp.where(v_ref[...] > 0.0, t, -1e30)

    t = pl.pallas_call(body, name="rpb_expand", out_shape=_sds((128, GRID_W * GRID_W), F32),
                       compiler_params=_params())(r2, jnp.asarray(diag.T, BF16), jnp.asarray(valid, F32))
    t = t[:N_HEADS * N_DROW].reshape(N_HEADS, N_DROW, GRID_W, GRID_W)
    return jnp.concatenate([t[:, :N_DPAIR], t[:, 1:]], axis=-1)


def _row_bias(tb_ref, hh, d0):
    return jnp.concatenate([tb_ref[hh, d0 + 2 * ii] for ii in range(WIN_H // 2)], axis=1)


def _row_window(r):
    rs = jnp.clip(r - WIN_H // 2, 0, N_ROWS - WIN_H)
    return pl.multiple_of(r * GRID_W, GRID_W), pl.multiple_of(rs * GRID_W, GRID_W), rs - r + (WIN_H - 1)


def _split_heads(src_ref, dst_ref, scale=None):
    for hh in range(2):
        v = src_ref[:, hh * HEAD_DIM:(hh + 1) * HEAD_DIM]
        dst_ref[hh] = (v if scale is None else v * scale).astype(BF16)


def _attn_items(qb_ref, kb_ref, vb_ref, tb_ref, first_row, n_rows):
    wins = [_row_window(first_row + u) for u in range(n_rows)]
    items = [(u, hh) for u in range(n_rows) for hh in range(2)]
    q = [qb_ref[hh, pl.ds(wins[u][0], GRID_W), :] for u, hh in items]
    k = [kb_ref[hh, pl.ds(wins[u][1], KEYS), :] for u, hh in items]
    v = [vb_ref[hh, pl.ds(wins[u][1], KEYS), :] for u, hh in items]
    s = [_dot(qi, ki, NT) + _row_bias(tb_ref, hh, wins[u][2]) for qi, ki, (u, hh) in zip(q, k, items)]
    m = [jnp.max(si, axis=-1, keepdims=True) for si in s]
    e = [jnp.exp(si - mi) for si, mi in zip(s, m)]
    inv = [1.0 / jnp.sum(ei, axis=-1, keepdims=True) for ei in e]
    p = [ei * li for ei, li in zip(e, inv)]
    return wins, items, q, k, v, p


def _attn_in_specs():
    q = pl.BlockSpec((T, HP), lambda p: (0, p))
    k = pl.BlockSpec((T, HP), lambda p: (0, N_HP + p))
    v = pl.BlockSpec((T, HP), lambda p: (0, 2 * N_HP + p))
    tb = pl.BlockSpec((2, N_DPAIR, GRID_W, HP), lambda p: (p, 0, 0, 0))
    return q, k, v, tb


_HEAD_SCRATCH = pltpu.VMEM((2, T, HEAD_DIM), BF16)


_PROBS = pl.BlockSpec((T, 2 * KEYS), lambda p: (0, p))


def _attn_fwd(z, tb):
    def body(q_ref, k_ref, v_ref, tb_ref, o_ref, p_ref, qb_ref, kb_ref, vb_ref):
        _split_heads(q_ref, qb_ref, SCALE)
        _split_heads(k_ref, kb_ref)
        _split_heads(v_ref, vb_ref)

        def rows(it, carry):
            wins, items, _, _, v, p = _attn_items(qb_ref, kb_ref, vb_ref, tb_ref, it * ATT_UNROLL_FWD, ATT_UNROLL_FWD)
            pb = [pi.astype(BF16) for pi in p]
            o = [_dot(pi, vi, NN) for pi, vi in zip(pb, v)]
            for u, (q0, _, _) in enumerate(wins):
                o_ref[pl.ds(q0, GRID_W), :] = jnp.concatenate(o[2 * u:2 * u + 2], axis=1).astype(BF16)
                p_ref[pl.ds(q0, GRID_W), :] = jnp.concatenate(pb[2 * u:2 * u + 2], axis=1)
            return carry

        lax.fori_loop(0, N_ROWS // ATT_UNROLL_FWD, rows, 0)

    blk = pl.BlockSpec((T, HP), lambda p: (0, p))
    return pl.pallas_call(
        body, name="attn_fwd", grid=(N_HP,), in_specs=list(_attn_in_specs()), out_specs=[blk, _PROBS],
        out_shape=[_sds((T, D_ATT), BF16), _sds((T, N_HEADS * KEYS), BF16)], scratch_shapes=[_HEAD_SCRATCH] * 3,
        compiler_params=_params(("parallel",)))(z, z, z, tb)


def _attn_bwd(z, probs, d_att, after=()):
    def body(q_ref, k_ref, v_ref, p_ref, do_ref, flip_ref, *rest):
        (dq_ref, dk_ref, dv_ref, diag_ref, qb_ref, kb_ref, vb_ref, dob_ref, dka_ref, dva_ref,
         ds_ref) = rest[len(after):]
        _split_heads(q_ref, qb_ref, SCALE)
        _split_heads(k_ref, kb_ref)
        _split_heads(v_ref, vb_ref)
        _split_heads(do_ref, dob_ref)
        dka_ref[...] = jnp.zeros_like(dka_ref)
        dva_ref[...] = jnp.zeros_like(dva_ref)
        ds_ref[...] = jnp.zeros_like(ds_ref)

        def rows(it, carry):
            wins = [_row_window(it * ATT_UNROLL_BWD + u) for u in range(ATT_UNROLL_BWD)]
            items = [(u, hh) for u in range(ATT_UNROLL_BWD) for hh in range(2)]
            q = [qb_ref[hh, pl.ds(wins[u][0], GRID_W), :] for u, hh in items]
            k = [kb_ref[hh, pl.ds(wins[u][1], KEYS), :] for u, hh in items]
            v = [vb_ref[hh, pl.ds(wins[u][1], KEYS), :] for u, hh in items]
            pb = [p_ref[pl.ds(wins[u][0], GRID_W), hh * KEYS:(hh + 1) * KEYS] for u, hh in items]
            p = [pi.astype(F32) for pi in pb]
            do = [dob_ref[hh, pl.ds(wins[u][0], GRID_W), :] for u, hh in items]
            dv = [_dot(pi, di, TN) for pi, di in zip(pb, do)]
            dp = [_dot(di, vi, NT) for di, vi in zip(do, v)]
            ds = [pi * (dpi - jnp.sum(dpi * pi, axis=-1, keepdims=True)) for pi, dpi in zip(p, dp)]
            dsb = [d.astype(BF16) for d in ds]
            dq = [_dot(d, ki, NN) * SCALE for d, ki in zip(dsb, k)]
            dk = [_dot(d, qi, TN) for d, qi in zip(dsb, q)]
            for d, (u, hh) in zip(ds, items):
                for ii in range(WIN_H // 2):
                    ds_ref[hh, wins[u][2] + 2 * ii] += d[:, ii * HP:(ii + 1) * HP]
            for u, (q0, k0, _) in enumerate(wins):
                dq_ref[pl.ds(q0, GRID_W), :] = jnp.concatenate(dq[2 * u:2 * u + 2], axis=1).astype(BF16)
                dka_ref[pl.ds(k0, KEYS), :] += jnp.concatenate(dk[2 * u:2 * u + 2], axis=1)
                dva_ref[pl.ds(k0, KEYS), :] += jnp.concatenate(dv[2 * u:2 * u + 2], axis=1)
            return carry

        lax.fori_loop(0, N_ROWS // ATT_UNROLL_BWD, rows, 0)
        dk_ref[...] = dka_ref[...].astype(BF16)
        dv_ref[...] = dva_ref[...].astype(BF16)
        _diag_sums(ds_ref, flip_ref, diag_ref)

    blk = pl.BlockSpec((T, HP), lambda p: (0, p))
    q, k, v, _ = _attn_in_specs()
    flip =jnp.asarray(np.eye(HP, dtype=np.float32)[::-1], BF16)
    return pl.pallas_call(
        body, name="attn_bwd", grid=(N_HP,),
        in_specs=[q, k, v, _PROBS, blk, pl.BlockSpec((HP, HP), lambda p: (0, 0))] + [_ANY] * len(after),
        out_specs=[blk, blk, blk, pl.BlockSpec((None, DIAG_ROWS, HP), lambda p: (p, 0, 0))],
        out_shape=[_sds((T, D_ATT), BF16)] * 3 + [_sds((N_HP, DIAG_ROWS, HP), F32)],
        scratch_shapes=[_HEAD_SCRATCH] * 4 + [pltpu.VMEM((T, HP), F32), pltpu.VMEM((T, HP), F32),
                                              pltpu.VMEM((2, N_DPAIR, GRID_W, HP), F32)],
        compiler_params=_params(("parallel",)))(z, z, z, probs, d_att, flip, *after)


def _diag_sums(acc_ref, flip_ref, out_ref):
    flip = flip_ref[...]
    rows = []
    for hh in range(2):
        for pair in range(N_DPAIR):
            reversed_lanes = sum(_dot(part, flip, NN) for part in _split3(acc_ref[hh, pair]))
            skewed = pltpu.roll(reversed_lanes, 0, 1, stride=1, stride_axis=0)
            rows.append(jnp.sum(skewed, axis=0, keepdims=True))
    rows.append(jnp.zeros((DIAG_ROWS - len(rows), HP), F32))
    out_ref[...] = jnp.concatenate(rows, axis=0)


def _rpb_grad(diag_sums):
    g = diag_sums.reshape(N_HP * DIAG_ROWS, HP)
    sel = np.zeros((2, 128, N_HP * DIAG_ROWS), np.float32)
    lane = np.zeros((2, HP, 128), np.float32)
    for h in range(N_HEADS):
        for pair in range(N_DPAIR):
            for half in range(2):
                sel[half, h * N_DROW + pair + half, (h // 2) * DIAG_ROWS + (h % 2) * N_DPAIR + pair] = 1.0
    for j in range(2 * WIN_W - 1):
        for half in range(2):
            lane[half, (HP - 1 - GRID_W * half - (j - (WIN_W - 1))) % HP, j] = 1.0

    def body(g_ref, sel_ref, lane_ref, o_ref):
        parts = _split3(g_ref[...])
        total = None
        for half in range(2):
            picked = sum(_dot(sel_ref[half], part, NN) for part in parts)
            term = sum(_dot(part, lane_ref[half], NN) for part in _split3(picked))
            total = term if total is None else total + term
        o_ref[...] = total

    out = pl.pallas_call(body, name="rpb_grad", out_shape=_sds((128, 128), F32),
                         compiler_params=_params())(g, jnp.asarray(sel, BF16), jnp.asarray(lane, BF16))
    return out[:N_HEADS * N_DROW, :2 * WIN_W - 1].reshape(N_HEADS, N_DROW, 2 * WIN_W - 1)


N_CB = D // REC_CB
N_CHUNK = T // REC_CHUNK
N_TILE = T // 8
_U_BLK = 1536 // REC_CB
_Y_BLK = 2560 // REC_CB


def _block_diag(w):
    per = REC_CB // 64
    wt = w.reshape(2, N_CB, per, 64, 64)
    eye = jnp.eye(per, dtype=w.dtype)
    full = wt[:, :, :, :, None, :] * eye[None, None, :, None, :, None]
    return full.reshape(2, N_CB, REC_CB, REC_CB).astype(BF16)


def _block_diag_grad(g):
    per = REC_CB // 64
    g6 = g.reshape(2, N_CB, per, 64, per, 64)
    return jnp.stack([g6[:, :, p, :, p, :] for p in range(per)], axis=2).reshape(2, 16, 64, 64)


def _gelu(x):
    c = 0.7978845608028654
    return 0.5 * x * (1.0 + jnp.tanh(c * (x + 0.044715 * x * x * x)))


def _gelu_grad(x):
    c = 0.7978845608028654
    th = jnp.tanh(c * (x + 0.044715 * x * x * x))
    return 0.5 * (1.0 + th) + 0.5 * x * (1.0 - th * th) * c * (1.0 + 3.0 * 0.044715 * x * x)


def _softplus_neg(lam):
    x = -lam
    e = jnp.exp(-jnp.abs(x))
    w = 1.0 + e
    l1p = jnp.where(w == 1.0, e, jnp.log(w) * e / (w - 1.0))
    return jnp.maximum(x, 0.0) + l1p


def _one_minus_exp(x):
    poly = x * (1.0 + x * (1 / 2 + x * (1 / 6 + x * (1 / 24 + x * (1 / 120 + x * (1 / 720))))))
    return jnp.where(x > -0.125, -poly, 1.0 - jnp.exp(x))


def _conv_taps(pad_ref, t0, w, sign):
    out = None
    for j in range(4):
        term = w[j:j + 1, :] * pad_ref[pl.ds(PAD + t0 + sign * (j - 2), REC_CHUNK), :]
        out = term if out is None else out + term
    return out


def _gates(u, wa, wi, ba, bi, sp):
    ub = u.astype(BF16)
    r = _sigmoid(_dot(ub, wa, NN) + ba)
    i = _sigmoid(_dot(ub, wi, NN) + bi)
    log_a = -LRU_C * r * sp
    a = jnp.exp(log_a)
    x = jnp.maximum(_one_minus_exp(2.0 * log_a), 0.0)
    positive = x > 0.0
    inv = lax.rsqrt(jnp.where(positive, x, 1.0))
    mult = jnp.where(positive, x * inv, 0.0)
    return r, i, a, mult, jnp.where(positive, inv, 0.0)


def _tile_scan(a, b, sub, reverse):
    for s in (1, 2, 4):
        if reverse:
            a_s, b_s, m = pltpu.roll(a, 8 - s, 0), pltpu.roll(b, 8 - s, 0), sub < 8 - s
        else:
            a_s, b_s, m = pltpu.roll(a, s, 0), pltpu.roll(b, s, 0), sub >= s
        b = jnp.where(m, a * b_s + b, b)
        a = jnp.where(m, a * a_s, a)
    return a, b


def _last_row(x, row):
    return jnp.broadcast_to(x[row:row + 1, :], x.shape)


def _rec_prologue(up_ref, cw_ref, cb_ref, wa_ref, wi_ref, ba_ref, bi_ref, lam_ref,
                  upad_ref, u_ref, a_refs, h_refs):
    cb = up_ref.shape[1]
    zeros = jnp.zeros((PAD, cb), F32)
    upad_ref[pl.ds(0, PAD), :] = zeros
    upad_ref[pl.ds(PAD + T, PAD), :] = zeros
    upad_ref[pl.ds(PAD, T), :] = up_ref[...].astype(F32)
    cw = cw_ref[...]
    sp = _softplus_neg(lam_ref[...])
    for c in range(N_CHUNK):
        t0 = c * REC_CHUNK
        u = cb_ref[...] + _conv_taps(upad_ref, t0, cw, 1)
        u_ref[pl.ds(t0, REC_CHUNK), :] = u
        for d in range(2):
            _, i, a, mult, _ = _gates(u, wa_ref[d], wi_ref[d], ba_ref[d:d + 1, :], bi_ref[d:d + 1, :], sp[d:d + 1, :])
            a_refs[d][pl.ds(t0, REC_CHUNK), :] = a
            h_refs[d][pl.ds(t0, REC_CHUNK), :] = mult * (i * u)

    sub = lax.broadcasted_iota(jnp.int32, (8, cb), 0)

    def tile(k, carry):
        cf, cr = carry
        tf = pl.multiple_of(k * 8, 8)
        tr = pl.multiple_of((N_TILE - 1 - k) * 8, 8)
        af, bf = _tile_scan(a_refs[0][pl.ds(tf, 8), :], h_refs[0][pl.ds(tf, 8), :], sub, False)
        hf = af * cf + bf
        h_refs[0][pl.ds(tf, 8), :] = hf
        ar, br = _tile_scan(a_refs[1][pl.ds(tr, 8), :], h_refs[1][pl.ds(tr, 8), :], sub, True)
        hr = ar * cr + br
        h_refs[1][pl.ds(tr, 8), :] = hr
        return _last_row(af, 7) * cf + _last_row(bf, 7), _last_row(ar, 0) * cr + _last_row(br, 0)

    z8 = jnp.zeros((8, cb), F32)
    lax.fori_loop(0, N_TILE, tile, (z8, z8))
    return sp


def _rec_specs():
    up = pl.BlockSpec((T, REC_CB), lambda c: (0, _U_BLK + c))
    yb = pl.BlockSpec((T, REC_CB), lambda c: (0, _Y_BLK + c))
    cw = pl.BlockSpec((4, REC_CB), lambda c: (0, c))
    cbias = pl.BlockSpec((1, REC_CB), lambda c: (0, c))
    wbd = pl.BlockSpec((2, None, REC_CB, REC_CB), lambda c: (0, c, 0, 0))
    vec2 = pl.BlockSpec((2, REC_CB), lambda c: (0, c))
    col = pl.BlockSpec((T, REC_CB), lambda c: (0, c))
    return up, yb, cw, cbias, wbd, vec2, col


def _rec_fwd(z, conv_w, conv_b, wa, wi, ba, bi, lam):
    up, yb, cw, cbias, wbd, vec2, col = _rec_specs()

    def body(up_ref, yb_ref, cw_ref, cb_ref, wa_ref, wi_ref, ba_ref, bi_ref, lam_ref, g_ref,
             u_ref, af_ref, ar_ref, hf_ref, hr_ref, upad_ref):
        _rec_prologue(up_ref, cw_ref, cb_ref, wa_ref, wi_ref, ba_ref, bi_ref, lam_ref,
                      upad_ref, u_ref, (af_ref, ar_ref), (hf_ref, hr_ref))

        def chunk(c, carry):
            t0 = pl.multiple_of(c * REC_CHUNK, REC_CHUNK)
            rows = pl.ds(t0, REC_CHUNK)
            g_ref[rows, :] = ((hf_ref[rows, :] + hr_ref[rows, :]) * _gelu(yb_ref[rows, :].astype(F32))).astype(BF16)
            return carry

        lax.fori_loop(0, N_CHUNK, chunk, 0)

    res = pl.pallas_call(
        body, name="rec_fwd", grid=(N_CB,),
        in_specs=[up, yb, cw, cbias, wbd, wbd, vec2, vec2, vec2], out_specs=[col] * 6,
        out_shape=[_sds((T, D), BF16)] + [_sds((T, D), F32)] * 5,
        scratch_shapes=[pltpu.VMEM((T + 2 * PAD, REC_CB), F32)],
        compiler_params=_params(("parallel",)))(z, z, conv_w, conv_b, wa, wi, ba, bi, lam)
    return res[0], tuple(res[1:])


def _rec_bwd(z, dg, saved, conv_w, conv_b, wa, wi, ba, bi, lam, after=()):
    up, yb, cw, cbias, wbd, vec2, col = _rec_specs()

    def body(up_ref, yb_ref, dg_ref, u_ref, af_ref, ar_ref, hf_ref, hr_ref,
             cw_ref, cb_ref, wa_ref, wi_ref, ba_ref, bi_ref, lam_ref, *rest):
        (dup_ref, dyb_ref, dcw_ref, dcb_ref, dwa_ref, dwi_ref, dba_ref, dbi_ref, dlam_ref,
         upad_ref, dh_ref, gf_ref, gr_ref, daf_ref, dar_ref, dupad_ref) = rest[len(after):]
        g_refs, da_refs = (gf_ref, gr_ref), (daf_ref, dar_ref)
        cb = up_ref.shape[1]
        zeros = jnp.zeros((PAD, cb), F32)
        upad_ref[pl.ds(0, PAD), :] = zeros
        upad_ref[pl.ds(PAD + T, PAD), :] = zeros
        upad_ref[pl.ds(PAD, T), :] = up_ref[...].astype(F32)
        sp = _softplus_neg(lam_ref[...])

        def gate_chunk(c, carry):
            t0 = pl.multiple_of(c * REC_CHUNK, REC_CHUNK)
            rows = pl.ds(t0, REC_CHUNK)
            y = yb_ref[rows, :].astype(F32)
            dgv = dg_ref[rows, :].astype(F32)
            dh_ref[rows, :] = dgv * _gelu(y)
            dyb_ref[rows, :] = (dgv * (hf_ref[rows, :] + hr_ref[rows, :]) * _gelu_grad(y)).astype(BF16)
            return carry

        lax.fori_loop(0, N_CHUNK, gate_chunk, 0)

        sub = lax.broadcasted_iota(jnp.int32, (8, cb), 0)

        def tile(k, carry):
            cf, cr = carry
            kf = N_TILE - 1 - k
            tf = pl.multiple_of(kf * 8, 8)
            tnext = pl.multiple_of(jnp.minimum(kf + 1, N_TILE - 1) * 8, 8)
            tprev = pl.multiple_of(jnp.maximum(kf - 1, 0) * 8, 8)
            a_t = af_ref[pl.ds(tf, 8), :]
            a_n = jnp.where(kf < N_TILE - 1, af_ref[pl.ds(tnext, 8), :], 0.0)
            a_sh = jnp.where(sub == 7, pltpu.roll(a_n, 7, 0), pltpu.roll(a_t, 7, 0))
            ca, cbb = _tile_scan(a_sh, dh_ref[pl.ds(tf, 8), :], sub, True)
            gf = ca * cf + cbb
            h_t = hf_ref[pl.ds(tf, 8), :]
            h_p = jnp.where(kf > 0, hf_ref[pl.ds(tprev, 8), :], 0.0)
            h_sh = jnp.where(sub == 0, pltpu.roll(h_p, 1, 0), pltpu.roll(h_t, 1, 0))
            gf_ref[pl.ds(tf, 8), :] = gf
            daf_ref[pl.ds(tf, 8), :] = gf * h_sh
            tr = pl.multiple_of(k * 8, 8)
            rnext = pl.multiple_of(jnp.minimum(k + 1, N_TILE - 1) * 8, 8)
            rprev = pl.multiple_of(jnp.maximum(k - 1, 0) * 8, 8)
            b_t = ar_ref[pl.ds(tr, 8), :]
            b_p = jnp.where(k > 0, ar_ref[pl.ds(rprev, 8), :], 0.0)
            b_sh = jnp.where(sub == 0, pltpu.roll(b_p, 1, 0), pltpu.roll(b_t, 1, 0))
            ra, rb = _tile_scan(b_sh, dh_ref[pl.ds(tr, 8), :], sub, False)
            gr = ra * cr + rb
            hr_t = hr_ref[pl.ds(tr, 8), :]
            hr_n = jnp.where(k < N_TILE - 1, hr_ref[pl.ds(rnext, 8), :], 0.0)
            hr_sh = jnp.where(sub == 7, pltpu.roll(hr_n, 7, 0), pltpu.roll(hr_t, 7, 0))
            gr_ref[pl.ds(tr, 8), :] = gr
            dar_ref[pl.ds(tr, 8), :] = gr * hr_sh
            return _last_row(gf, 0), _last_row(gr, 7)

        z8 = jnp.zeros((8, cb), F32)
        lax.fori_loop(0, N_TILE, tile, (z8, z8))

        dupad_ref[pl.ds(0, PAD), :] = zeros
        dupad_ref[pl.ds(PAD + T, PAD), :] = zeros
        dwa_ref[...] = jnp.zeros_like(dwa_ref)
        dwi_ref[...] = jnp.zeros_like(dwi_ref)
        dba_ref[...] = jnp.zeros_like(dba_ref)
        dbi_ref[...] = jnp.zeros_like(dbi_ref)
        dlam_ref[...] = jnp.zeros_like(dlam_ref)

        def grad_chunk(c, carry):
            t0 = pl.multiple_of(c * REC_CHUNK, REC_CHUNK)
            rows = pl.ds(t0, REC_CHUNK)
            u = u_ref[rows, :]
            ub = u.astype(BF16)
            du = jnp.zeros((REC_CHUNK, cb), F32)
            for d in range(2):
                r, i, a, mult, inv_mult = _gates(u, wa_ref[d], wi_ref[d], ba_ref[d:d + 1, :], bi_ref[d:d + 1, :],
                                                 sp[d:d + 1, :])
                dbx = g_refs[d][rows, :]
                dmult = dbx * (i * u)
                diu = dbx * mult
                a2 = a * a
                dlog = da_refs[d][rows, :] * a - dmult * (a2 * inv_mult)
                dpa = (dlog * (-LRU_C) * sp[d:d + 1, :]) * r * (1.0 - r)
                dpi = (diu * u) * i * (1.0 - i)
                dpab, dpib = dpa.astype(BF16), dpi.astype(BF16)
                du = du + diu * i + _dot(dpab, wa_ref[d], NT) + _dot(dpib, wi_ref[d], NT)
                dwa_ref[d] += _dot(ub, dpab, TN)
                dwi_ref[d] += _dot(ub, dpib, TN)
                dba_ref[d:d + 1, :] += jnp.sum(dpa, axis=0, keepdims=True)
                dbi_ref[d:d + 1, :] += jnp.sum(dpi, axis=0, keepdims=True)
                dlam_ref[d:d + 1, :] += jnp.sum(dlog * r, axis=0, keepdims=True)
            dupad_ref[pl.ds(PAD + t0, REC_CHUNK), :] = du
            return carry

        lax.fori_loop(0, N_CHUNK, grad_chunk, 0)
        dlam_ref[...] = dlam_ref[...] * (LRU_C * _sigmoid(-lam_ref[...]))

        cw = cw_ref[...]
        dcb = jnp.zeros((1, cb), F32)
        dcw = [jnp.zeros((1, cb), F32) for _ in range(4)]
        for c in range(N_CHUNK):
            t0 = c * REC_CHUNK
            du = dupad_ref[pl.ds(PAD + t0, REC_CHUNK), :]
            dcb = dcb + jnp.sum(du, axis=0, keepdims=True)
            for j in range(4):
                dcw[j] = dcw[j] + jnp.sum(du * upad_ref[pl.ds(PAD + t0 + j - 2, REC_CHUNK), :], axis=0, keepdims=True)
            dup_ref[pl.ds(t0, REC_CHUNK), :] = _conv_taps(dupad_ref, t0, cw, -1).astype(BF16)
        dcb_ref[...] = dcb
        dcw_ref[...] = jnp.concatenate(dcw, axis=0)

    full = pltpu.VMEM((T, REC_CB), F32)
    padded = pltpu.VMEM((T + 2 * PAD, REC_CB), F32)
    return pl.pallas_call(
        body, name="rec_bwd", grid=(N_CB,),
        in_specs=[up, yb] + [col] * 6 + [cw, cbias, wbd, wbd, vec2, vec2, vec2] + [_ANY] * len(after),
        out_specs=[col, col, cw, cbias, wbd, wbd, vec2, vec2, vec2],
        out_shape=[_sds((T, D), BF16), _sds((T, D), BF16), _sds((4, D), F32), _sds((1, D), F32),
                   _sds((2, N_CB, REC_CB, REC_CB), F32), _sds((2, N_CB, REC_CB, REC_CB), F32),
                   _sds((2, D), F32), _sds((2, D), F32), _sds((2, D), F32)],
        scratch_shapes=[padded, full, full, full, full, full, padded],
        compiler_params=_params(("parallel",)))(z, z, dg, *saved, conv_w, conv_b, wa, wi, ba, bi, lam, *after)


class _NoReducer:
    def begin(self, tag, grads):
        return ()

    def advance(self, tag, after):
        return ()

    def interlude(self, tokens):
        return None


def _local_step(x, target, p, late=None, reducer=_NoReducer()):
    x = x.reshape(T, D)
    target = target.reshape(T, D)
    tb = _bias_pairs(p["rpb"])
    wa, wi = _block_diag(p["w_rg_a"]), _block_diag(p["w_rg_i"])

    h1 = _rms_fwd("rms1_fwd", x, p["ln1_g"], after=late[0] if late else ())
    if late:
        p = {**p, **late[1]((h1, tb, wa, wi))}
    rec_params = (p["conv_w"], p["conv_b"], wa, wi, p["b_rg_a"], p["b_rg_i"], p["lru_lambda"])
    (z,) = _mm_nn_cols("mm_z", h1, p["w_in"], BF16, bias=p["b_in"])
    att, probs = _attn_fwd(z, tb)
    g, rec_saved = _rec_fwd(z, *rec_params)
    if late:
        p = {**p, **late[2](g)}
    y_att, y_rec, mixed, x1, h2 = _branches_fwd(att, g, z, x, p["w_att_o"], p["w_rec_o"], p["w_out"], p["ln2_g"])

    def relu2(r, ex, outs):
        rp = jnp.maximum(r, 0.0)
        outs[0][...] = (rp * rp).astype(BF16)

    (s,) = _mm_nn_cols("mm_ff1", h2, p["w_ff1"], BF16, epilogue=relu2)
    loss, dx2, dx2_b, g_lnf = _mm_x2_loss_head(s, p["w_ff2"], x1, target, p["lnf_g"])

    def relu2_bwd(r, ex, outs):
        outs[0][...] = (r * 2.0 * jnp.sqrt(ex[0][...].astype(F32))).astype(BF16)

    (df,) = _mm_nt_rows("mm_df", dx2_b, p["w_ff2"], BF16, tn=D, extras=[s],
                        extra_specs=[pl.BlockSpec((TJ, D), lambda j, i, k: (i, j))], epilogue=relu2_bwd)
    (g_w_ff2,) = _mm_tn_rows("mm_g_ff2", s, dx2_b, tm=D)
    (g_w_ff1,) = _mm_tn_cols("mm_g_ff1", h2, df, D)
    tok = reducer.begin("ff", dict(w_ff2=g_w_ff2, w_ff1=g_w_ff1))
    dx1, dx1_b, g_ln2 = _mm_nt_cols_rms_bwd("mm_dh2_rms2_bwd", df, p["w_ff1"], x1, p["ln2_g"], dx2, after=tok,
                                            bf16_copy=True)

    dy_att, dy_rec, dg_att, dg_rec, d_att, d_g = _branches_bwd(dx1_b, y_att, y_rec, z, p["w_att_o"], p["w_rec_o"],
                                                               p["w_out"])
    (g_w_out,) = _mm_tn_rows("mm_g_out", mixed, dx1_b, tm=D)
    (g_w_att_o,) = _mm_tn_cols("mm_g_att_o", att, dy_att, D // N_CHIPS)
    (g_w_rec_o,) = _mm_tn_rows("mm_g_rec_o", g, dy_rec, tm=D)
    tok = reducer.advance("ff", g_w_rec_o) + reducer.begin("proj", dict(w_out=g_w_out, w_att_o=g_w_att_o, w_rec_o=g_w_rec_o))

    dq, dk, dv, ds_acc = _attn_bwd(z, probs, d_att, after=tok)
    g_rpb = _rpb_grad(ds_acc)
    tok = reducer.advance("proj", dq)
    d_up, d_yb, g_conv_w, g_conv_b, g_wa, g_wi, g_ba, g_bi, g_lam = _rec_bwd(z, d_g, rec_saved, *rec_params, after=tok)
    dz = jnp.concatenate([dq, dk, dv, d_up, d_yb, dg_att, dg_rec], axis=1)

    g_w_in, g_b_in = _mm_tn_cols("mm_g_in", h1, dz, D_IN // N_CHIPS, colsum=True)
    tok = reducer.advance("in", reducer.interlude(reducer.begin("in", dict(w_in=g_w_in))))
    grad_x, g_ln1 = _mm_nt_cols_rms_bwd("mm_dh1_rms1_bwd", dz, p["w_in"], x, p["ln1_g"], dx1, after=tok)

    grads = dict(ln1_g=g_ln1, w_in=g_w_in, b_in=g_b_in, rpb=g_rpb, w_att_o=g_w_att_o, conv_w=g_conv_w,
                 conv_b=g_conv_b, w_rg_a=_block_diag_grad(g_wa), b_rg_a=g_ba, w_rg_i=_block_diag_grad(g_wi),
                 b_rg_i=g_bi, lru_lambda=g_lam, w_rec_o=g_w_rec_o, w_out=g_w_out, ln2_g=g_ln2,
                 w_ff1=g_w_ff1, w_ff2=g_w_ff2, lnf_g=g_lnf)
    return loss, grad_x.reshape(1, T, D), grads


_ANY = pl.BlockSpec(memory_space=pl.ANY)
N_PEERS = N_CHIPS - 1


def _place():
    x, y, c = lax.axis_index("x"), lax.axis_index("y"), lax.axis_index("c")
    peers = [(1 - x, y), (x, 1 - y), (1 - x, 1 - y)]
    return x, y, c, 2 * x + y, peers


def _remote(src, dst, send_sem, recv_sem, dev):
    return pltpu.make_async_remote_copy(src_ref=src, dst_ref=dst, send_sem=send_sem, recv_sem=recv_sem,
                                        device_id=dev, device_id_type=MESH)


def _prefetch_call(body, name, ids, grid, in_specs, out_specs, out_shape, args, semantics=None):
    spec = pltpu.PrefetchScalarGridSpec(num_scalar_prefetch=1, grid=grid, in_specs=in_specs, out_specs=out_specs)
    return pl.pallas_call(body, name=name, grid_spec=spec, out_shape=out_shape,
                          compiler_params=_params(semantics or ("parallel",) * len(grid)))(ids, *args)


def _cast_bf16(name, w, chip_id, after=()):
    rows, cols = w.shape
    rb = min(rows, 256)

    def body(ids_ref, w_ref, *rest):
        rest[-1][...] = w_ref[...].astype(BF16)

    return _prefetch_call(body, name, chip_id, (rows // rb,),
                          [pl.BlockSpec((rb, cols), lambda i, ids: (i, 0))] + [_ANY] * len(after),
                          pl.BlockSpec((None, rb, cols), lambda i, ids: (ids[0], i, 0)),
                          _sds((N_CHIPS, rows, cols), BF16), (w, *after))


def _dma_sems(*counts):
    return [pltpu.SemaphoreType.DMA((k,)) for k in counts]


_HBM = pl.BlockSpec(memory_space=pltpu.HBM)
_SEM = pl.BlockSpec(memory_space=pltpu.SEMAPHORE)
_SPLIT_COPY = pltpu.CompilerParams(has_side_effects=pltpu.SideEffectType.DATAFLOW_SIDE_EFFECTING)
SIBLING_ID = 0
_SPLIT_COPY_SIBLING = pltpu.CompilerParams(has_side_effects=pltpu.SideEffectType.DATAFLOW_SIDE_EFFECTING,
                                           collective_id=SIBLING_ID)


def _sibling_handshake():
    x, y, c = lax.axis_index("x"), lax.axis_index("y"), lax.axis_index("c")
    barrier = pltpu.get_barrier_semaphore()
    pl.semaphore_signal(barrier, inc=1, device_id=(x, y, 1 - c), device_id_type=MESH)
    pl.semaphore_wait(barrier, 1)


def _hbm(arrays):
    return [pltpu.with_memory_space_constraint(a, pltpu.HBM) for a in arrays]


def _hbm_like(arrays):
    return [pltpu.HBM(a.shape, a.dtype) for a in arrays]


def _halves(buf, c):
    half = buf.shape[1] // 2
    return pl.ds(c * half, half), pl.ds((1 - c) * half, half)


def _gather_start(name, slots):
    n = len(slots)
    nk = n * N_PEERS

    def body(*refs):
        bufs = refs[n:2 * n]
        send_sems, recv_sems, token = refs[2 * n:]
        x, y, c, chip, peers = _place()
        for t in range(n):
            mine, _ = _halves(bufs[t], c)
            for r, (px, py) in enumerate(peers):
                k = t * N_PEERS + r
                own = bufs[t].at[chip, mine]
                _remote(own, own, send_sems.at[k], recv_sems.at[k], (px, py, c)).start()
        token[...] = jnp.zeros_like(token)

    res = pl.pallas_call(
        body, name=name, in_specs=[_HBM] * n, out_specs=[_HBM] * n + [_SEM, _SEM, pl.BlockSpec(memory_space=pltpu.VMEM)],
        out_shape=_hbm_like(slots) + [pltpu.SemaphoreType.DMA((nk,)), pltpu.SemaphoreType.DMA((nk,)),
                                      _sds((8, 128), F32)],
        input_output_aliases={t: t for t in range(n)}, compiler_params=_SPLIT_COPY)(*_hbm(slots))
    return res[:n], (res[n], res[n + 1]), res[n + 2]


def _gather_wait(name, bufs, sems, after):
    n = len(bufs)
    after = tuple(after) if isinstance(after, (tuple, list)) else (after,)

    def body(*refs):
        ins = refs[:n]
        send_sems, recv_sems = refs[n], refs[n + 1]
        x, y, c, chip, peers = _place()
        for t in range(n):
            mine, _ = _halves(ins[t], c)
            for r, (px, py) in enumerate(peers):
                k = t * N_PEERS + r
                cp = _remote(ins[t].at[chip, mine], ins[t].at[2 * px + py, mine], send_sems.at[k], recv_sems.at[k],
                             (px, py, c))
                cp.wait_send()
                cp.wait_recv()

    return pl.pallas_call(
        body, name=name, in_specs=[_HBM] * n + [_SEM, _SEM] + [_ANY] * len(after), out_specs=[_HBM] * n,
        out_shape=_hbm_like(bufs), input_output_aliases={t: t for t in range(n)},
        compiler_params=_SPLIT_COPY)(*bufs, *sems, *after)


def _gather_forward(name, bufs):
    n = len(bufs)
    nk = n * N_PEERS

    def body(*refs):
        _sibling_handshake()
        outs = refs[n:2 * n]
        send_sems, recv_sems = refs[2 * n:]
        x, y, c, chip, peers = _place()
        sibling = (x, y, 1 - c)
        sends = []
        for t in range(n):
            mine, _ = _halves(outs[t], c)
            for r, (px, py) in enumerate(peers):
                k = t * N_PEERS + r
                landed = outs[t].at[2 * px + py, mine]
                sends.append(_remote(landed, landed, send_sems.at[k], recv_sems.at[k], sibling))
                sends[-1].start()
        for t in range(n):
            _, theirs = _halves(outs[t], c)
            for r, (px, py) in enumerate(peers):
                k = t * N_PEERS + r
                landed = outs[t].at[2 * px + py, theirs]
                _remote(landed, landed, send_sems.at[k], recv_sems.at[k], sibling).wait_recv()
        for cp in sends:
            cp.wait_send()

    return pl.pallas_call(
        body, name=name, in_specs=[_ANY] * n, out_specs=[_ANY] * n, out_shape=[_sds(b.shape, b.dtype) for b in bufs],
        input_output_aliases={t: t for t in range(n)}, scratch_shapes=_dma_sems(nk, nk),
        compiler_params=pltpu.CompilerParams(collective_id=SIBLING_ID))(*bufs)


def _pair_copies(n, srcs, lands, send_sems, recv_sems):
    x, y, c, _, _ = _place()
    sibling = (x, y, 1 - c)
    copies = []
    for t in range(n):
        half = srcs[t].shape[1] // 2
        for j in range(N_CHIPS):
            k = t * N_CHIPS + j
            copies.append(_remote(srcs[t].at[j, pl.ds((1 - c) * half, half)], lands[t].at[j],
                                  send_sems.at[k], recv_sems.at[k], sibling))
    for t in range(n, len(srcs)):
        k = n * N_CHIPS + t - n
        copies.append(_remote(srcs[t], lands[t], send_sems.at[k], recv_sems.at[k], sibling))
    return copies


def _pair_start(name, grads, wholes=()):
    n = len(grads)
    srcs = list(grads) + list(wholes)
    m = len(srcs)
    lands = [pltpu.HBM((N_CHIPS, g.shape[1] // 2, g.shape[2]), F32) for g in grads] + _hbm_like(wholes)
    ns = n * N_CHIPS + len(wholes)

    def body(*refs):
        _sibling_handshake()
        src_refs, land_refs = refs[m:2 * m], refs[2 * m:3 * m]
        send_sems, recv_sems, token = refs[3 * m:]
        for cp in _pair_copies(n, src_refs, land_refs, send_sems, recv_sems):
            cp.start()
        token[...] = jnp.zeros_like(token)

    res = pl.pallas_call(
        body, name=name, in_specs=[_HBM] * m,
        out_specs=[_HBM] * (2 * m) + [_SEM, _SEM, pl.BlockSpec(memory_space=pltpu.VMEM)],
        out_shape=_hbm_like(srcs) + lands + [pltpu.SemaphoreType.DMA((ns,)), pltpu.SemaphoreType.DMA((ns,)),
                                             _sds((8, 128), F32)],
        input_output_aliases={t: t for t in range(m)}, compiler_params=_SPLIT_COPY_SIBLING)(*_hbm(srcs))
    return (res[:m], res[m:2 * m], (res[2 * m], res[2 * m + 1])), res[2 * m + 2]


def _pair_wait(name, flight, n, after):
    srcs, lands, sems = flight
    m = len(srcs)

    def body(*refs):
        for cp in _pair_copies(n, refs[:m], refs[m:2 * m], refs[2 * m], refs[2 * m + 1]):
            cp.wait_send()
            cp.wait_recv()

    res = pl.pallas_call(
        body, name=name, in_specs=[_HBM] * (2 * m) + [_SEM, _SEM, _ANY], out_specs=[_HBM] * (2 * m),
        out_shape=_hbm_like(srcs) + _hbm_like(lands), input_output_aliases={t: t for t in range(2 * m)},
        compiler_params=_SPLIT_COPY)(*srcs, *lands, *sems, after)
    return res[:m], res[m:]


def _chip_copies(srcs, lands, small_src, small_land, send_sems, recv_sems):
    x, y, c, chip, peers = _place()
    n = len(srcs)
    copies = []
    for r, (px, py) in enumerate(peers):
        for t in range(n):
            k = t * N_PEERS + r
            copies.append(_remote(srcs[t].at[2 * px + py], lands[t].at[r], send_sems.at[k], recv_sems.at[k], (px, py, c)))
        if small_src is not None:
            k = n * N_PEERS + r
            half_s = small_src.shape[0] // 2
            copies.append(_remote(small_src.at[pl.ds(c * half_s, half_s)], small_land.at[r],
                                  send_sems.at[k], recv_sems.at[k], (px, py, c)))
    return copies


def _chip_start(name, sums_bf16, small=None):
    n = len(sums_bf16)
    srcs = list(sums_bf16) + ([small] if small is not None else [])
    m = len(srcs)
    lands = [pltpu.HBM((N_PEERS,) + s.shape[1:], BF16) for s in sums_bf16]
    if small is not None:
        lands.append(pltpu.HBM((N_PEERS, small.shape[0] // 2, 128), F32))
    nk = m * N_PEERS

    def body(*refs):
        src_refs, land_refs = refs[m:2 * m], refs[2 * m:3 * m]
        send_sems, recv_sems, token = refs[3 * m:]
        small_src, small_land = (src_refs[n], land_refs[n]) if small is not None else (None, None)
        for cp in _chip_copies(src_refs[:n], land_refs[:n], small_src, small_land, send_sems, recv_sems):
            cp.start()
        token[...] = jnp.zeros_like(token)

    res = pl.pallas_call(
        body, name=name, in_specs=[_HBM] * m,
        out_specs=[_HBM] * (2 * m) + [_SEM, _SEM, pl.BlockSpec(memory_space=pltpu.VMEM)],
        out_shape=_hbm_like(srcs) + lands + [pltpu.SemaphoreType.DMA((nk,)), pltpu.SemaphoreType.DMA((nk,)),
                                             _sds((8, 128), F32)],
        input_output_aliases={t: t for t in range(m)}, compiler_params=_SPLIT_COPY)(*_hbm(srcs))
    return (res[:m], res[m:2 * m], (res[2 * m], res[2 * m + 1])), res[2 * m + 2]


def _chip_wait(name, flight, with_small, after):
    srcs, lands, sems = flight
    m = len(srcs)
    n = m - 1 if with_small else m

    def body(*refs):
        src_refs, land_refs = refs[:m], refs[m:2 * m]
        send_sems, recv_sems = refs[2 * m], refs[2 * m + 1]
        small_src, small_land = (src_refs[n], land_refs[n]) if with_small else (None, None)
        for cp in _chip_copies(src_refs[:n], land_refs[:n], small_src, small_land, send_sems, recv_sems):
            cp.wait_send()
            cp.wait_recv()

    res = pl.pallas_call(
        body, name=name, in_specs=[_HBM] * (2 * m) + [_SEM, _SEM, _ANY], out_specs=[_HBM] * (2 * m),
        out_shape=_hbm_like(srcs) + _hbm_like(lands), input_output_aliases={t: t for t in range(2 * m)},
        compiler_params=_SPLIT_COPY)(*srcs, *lands, *sems, after)
    return res[:m], res[m:]


def _swap_start(name, bufs):
    n = len(bufs)

    def body(*refs):
        _sibling_handshake()
        outs = refs[n:2 * n]
        send_sems, recv_sems, token = refs[2 * n:]
        x, y, c, _, _ = _place()
        for t in range(n):
            h = outs[t].shape[0] // 2
            mine = outs[t].at[pl.ds(c * h, h)]
            _remote(mine, mine, send_sems.at[t], recv_sems.at[t], (x, y, 1 - c)).start()
        token[...] = jnp.zeros_like(token)

    res = pl.pallas_call(
        body, name=name, in_specs=[_HBM] * n, out_specs=[_HBM] * n + [_SEM, _SEM, pl.BlockSpec(memory_space=pltpu.VMEM)],
        out_shape=_hbm_like(bufs) + [pltpu.SemaphoreType.DMA((n,)), pltpu.SemaphoreType.DMA((n,)), _sds((8, 128), F32)],
        input_output_aliases={t: t for t in range(n)}, compiler_params=_SPLIT_COPY_SIBLING)(*_hbm(bufs))
    return (res[:n], (res[n], res[n + 1])), res[n + 2]


def _swap_wait(name, flight, after):
    bufs, sems = flight
    n = len(bufs)

    def body(*refs):
        ins = refs[:n]
        send_sems, recv_sems = refs[n], refs[n + 1]
        x, y, c, _, _ = _place()
        for t in range(n):
            h = ins[t].shape[0] // 2
            cp = _remote(ins[t].at[pl.ds(c * h, h)], ins[t].at[pl.ds((1 - c) * h, h)], send_sems.at[t],
                         recv_sems.at[t], (x, y, 1 - c))
            cp.wait_send()
            cp.wait_recv()

    return pl.pallas_call(
        body, name=name, in_specs=[_HBM] * n + [_SEM, _SEM, _ANY], out_specs=[_HBM] * n, out_shape=_hbm_like(bufs),
        input_output_aliases={t: t for t in range(n)}, compiler_params=_SPLIT_COPY)(*bufs, *sems, after)


def _pair_sum(name, grad, got, ids):
    _, rows, cols = got.shape
    rb = min(rows, 256)
    nb = rows // rb
    blk = pl.BlockSpec((None, rb, cols), lambda i, j, ids: (j, i, 0))
    mine = pl.BlockSpec((None, rb, cols), lambda i, j, ids: (j, ids[1] * nb + i, 0))
    own = pl.BlockSpec((rb, cols), lambda i, j, ids: (i, 0))

    def body(ids_ref, a_ref, b_ref, s_ref, sb_ref):
        s = a_ref[...] + b_ref[...]
        sb_ref[...] = s.astype(BF16)

        @pl.when(pl.program_id(1) == ids_ref[0])
        def _():
            s_ref[...] = s

    return _prefetch_call(body, name, ids, (nb, N_CHIPS), [mine, blk], [own, blk],
                          [_sds((rows, cols), F32), _sds(got.shape, BF16)], (grad, got),
                          semantics=("parallel", "arbitrary"))


def _chip_sum(name, own_sum, got, ids):
    rows, cols = own_sum.shape
    rb = min(rows, 256)
    nb = rows // rb
    own = pl.BlockSpec((rb, cols), lambda i, ids: (i, 0))
    blk3 = pl.BlockSpec((N_PEERS, rb, cols), lambda i, ids: (0, i, 0))
    out = pl.BlockSpec((rb, cols), lambda i, ids: (ids[1] * nb + i, 0))

    def body(ids_ref, a_ref, b_ref, o_ref):
        o_ref[...] = ((a_ref[...] + b_ref[0].astype(F32)) + b_ref[1].astype(F32)) + b_ref[2].astype(F32)

    return _prefetch_call(body, name, ids, (nb,), [own, blk3], out, _sds((2 * rows, cols), F32), (own_sum, got))


SMALL_RB = 280


def _small_pair_sum(own, got):
    blk = pl.BlockSpec((SMALL_RB, 128), lambda i: (i, 0))

    def body(a_ref, b_ref, o_ref):
        o_ref[...] = a_ref[...] + b_ref[...]

    return pl.pallas_call(body, name="small_pair_sum", grid=(own.shape[0] // SMALL_RB,), in_specs=[blk, blk],
                          out_specs=blk, out_shape=_sds(own.shape, F32),
                          compiler_params=_params(("parallel",)))(own, got)


def _small_chip_sum(pair, got, ids):
    nb = pair.shape[0] // 2 // SMALL_RB
    half = pl.BlockSpec((SMALL_RB, 128), lambda i, ids: (ids[1] * nb + i, 0))
    blk3 = pl.BlockSpec((N_PEERS, SMALL_RB, 128), lambda i, ids: (0, i, 0))

    def body(ids_ref, a_ref, b_ref, o_ref):
        o_ref[...] = (a_ref[...] + b_ref[1]) + (b_ref[0] + b_ref[2])

    return _prefetch_call(body, "small_chip_sum", ids, (nb,), [half, blk3], half, _sds(pair.shape, F32), (pair, got))


def _adamw_math(w, g, m, v):
    m = ADAM_B1 * m + (1.0 - ADAM_B1) * g
    v = ADAM_B2 * v + (1.0 - ADAM_B2) * (g * g)
    m_hat = m / (1.0 - ADAM_B1 ** ADAM_STEP)
    v_hat = v / (1.0 - ADAM_B2 ** ADAM_STEP)
    delta = -ADAM_LR * (m_hat / (jnp.sqrt(v_hat) + ADAM_EPS) + ADAM_WD * w)
    return delta, m, v


def _adamw(name, w, g, m, v, rb=None):
    rows, cols = w.shape
    rb = rows if rb is None else rb
    blk = pl.BlockSpec((rb, cols), lambda i: (i, 0))

    def body(w_ref, g_ref, m_ref, v_ref, d_ref, nm_ref, nv_ref):
        d, nm, nv = _adamw_math(w_ref[...], g_ref[...], m_ref[...], v_ref[...])
        d_ref[...] = d
        nm_ref[...] = nm
        nv_ref[...] = nv

    return pl.pallas_call(body, name=name, grid=(rows // rb,), in_specs=[blk] * 4, out_specs=[blk] * 3,
                          out_shape=[_sds(w.shape, F32)] * 3, compiler_params=_params(("parallel",)))(w, g, m, v)


def _adamw_small(ws, gs, ms, vs):
    n = len(ws)

    def body(*refs):
        for t in range(n):
            w_ref, g_ref, m_ref, v_ref = (refs[k * n + t] for k in range(4))
            d, nm, nv = _adamw_math(w_ref[...], g_ref[...], m_ref[...], v_ref[...])
            for k, val in enumerate((d, nm, nv)):
                refs[(4 + k) * n + t][...] = val

    res = pl.pallas_call(body, name="adamw_small", out_shape=[_sds(a.shape, F32) for a in ws] * 3,
                         compiler_params=_params())(*ws, *gs, *ms, *vs)
    return [(res[t], res[n + t], res[2 * n + t]) for t in range(n)]


BIG = ("w_in", "w_att_o", "w_rec_o", "w_out", "w_ff1", "w_ff2")
SHARDED_VECS = ("conv_w", "b_rg_a", "b_rg_i", "lru_lambda")
SMALL = ("ln1_g", "b_in", "rpb", "conv_w", "conv_b", "w_rg_a", "b_rg_a", "w_rg_i", "b_rg_i", "lru_lambda",
         "ln2_g", "lnf_g")
SMALL_ROWS = 2240
ORDER = ("ln1_g", "w_in", "b_in", "rpb", "w_att_o", "conv_w", "conv_b", "w_rg_a", "b_rg_a", "w_rg_i", "b_rg_i",
         "lru_lambda", "w_rec_o", "w_out", "ln2_g", "w_ff1", "w_ff2", "lnf_g")


def _pack_small(grads, loss):
    parts, sizes = [], {}
    for n in SMALL:
        flat = grads[n].reshape(-1)
        pad = (-flat.shape[0]) % 128
        sizes[n] = (flat.shape[0], flat.shape[0] + pad)
        parts.append(jnp.pad(flat, (0, pad)))
    total = sum(s[1] for s in sizes.values())
    parts.append(jnp.pad(loss.reshape(1), (0, SMALL_ROWS * 128 - total - 1)))
    return jnp.concatenate(parts).reshape(SMALL_ROWS, 128), sizes


def _unpack_small(buf, sizes, shapes):
    flat = buf.reshape(-1)
    out, pos = {}, 0
    for n in SMALL:
        size, padded = sizes[n]
        out[n] = flat[pos:pos + size].reshape(shapes[n])
        pos += padded
    return out, flat[pos]


def _gather_weights(w, chip):
    chip_id = chip.astype(jnp.int32).reshape(1)
    vec_rows = [w[n][0] for n in SHARDED_VECS]
    vec_shard = jnp.concatenate(vec_rows + [jnp.zeros((16 - 10, D // N_CHIPS), F32)], axis=0)
    vec_slots = lax.dynamic_update_slice(jnp.zeros((N_CHIPS, 16, D // N_CHIPS), F32), vec_shard[None], (chip, 0, 0))
    bufs_a, sems_a, token_a = _gather_start("gather_start_first", [_cast_bf16("cast_w_in", w["w_in"][0], chip_id), vec_slots])
    rest_names = BIG[1:]
    bufs_b, sems_b, token_b = _gather_start(
        "gather_start_rest", [_cast_bf16("cast_" + n, w[n][0], chip_id, after=(token_a,)) for n in rest_names])

    def first(after):
        w_in_full, vec_full = _gather_forward("gather_forward_first", _gather_wait("gather_wait_first", bufs_a, sems_a, after))
        vecs = vec_full.transpose(1, 0, 2).reshape(16, D)
        return dict(w_in=w_in_full, conv_w=vecs[0:4], b_rg_a=vecs[4:6], b_rg_i=vecs[6:8], lru_lambda=vecs[8:10])

    def rest(after):
        full = dict(zip(rest_names, _gather_forward("gather_forward_rest",
                                                    _gather_wait("gather_wait_rest", bufs_b, sems_b, after))))
        return dict(w_att_o=full["w_att_o"], w_ff1=full["w_ff1"], w_rec_o=full["w_rec_o"].reshape(D, D),
                    w_out=full["w_out"].reshape(D, D), w_ff2=full["w_ff2"].reshape(D_FF, D))

    p = dict(ln1_g=w["ln1_g"], b_in=w["b_in"], rpb=w["rpb"][0], conv_b=w["conv_b"], w_rg_a=w["w_rg_a"][0],
             w_rg_i=w["w_rg_i"][0], ln2_g=w["ln2_g"], lnf_g=w["lnf_g"].reshape(1, D))
    return p, ((token_b,), first, rest)


class _Reducer:
    def __init__(self, ids):
        self.ids = ids
        self.groups = {}

    def begin(self, tag, grads, small=None):
        names = list(grads)
        big = [grads[n].reshape(N_CHIPS, -1, grads[n].shape[-1]) for n in names]
        flight, token = _pair_start("pair_start_" + tag, big, [] if small is None else [small])
        self.groups[tag] = dict(names=names, pair=flight, small=small is not None)
        return (token,)

    def advance(self, tag, after):
        grp = self.groups[tag]
        n = len(grp["names"])
        mine, got = _pair_wait("pair_wait_" + tag, grp["pair"], n, after)
        sums = [_pair_sum("pair_sum_" + name, a, b, self.ids) for name, a, b in zip(grp["names"], mine, got)]
        small_sum = _small_pair_sum(mine[n], got[n]) if grp["small"] else None
        grp["chip"], token = _chip_start("chip_start_" + tag, [s[1] for s in sums], small_sum)
        grp["sums"] = [s[0] for s in sums]
        self.last_token = token
        return (token,)

    def finish(self, tag, after):
        grp = self.groups[tag]
        srcs, lands = _chip_wait("chip_wait_" + tag, grp["chip"], grp["small"], after)
        halves = [_chip_sum("chip_sum_" + name, s, b, self.ids) for name, s, b in zip(grp["names"], grp["sums"], lands)]
        if grp["small"]:
            halves.append(_small_chip_sum(srcs[-1], lands[-1], self.ids))
        grp["swap"], token = _swap_start("swap_start_" + tag, halves)
        return token

    def interlude(self, tokens):
        after = tokens[0]
        for tag in list(self.groups)[:-1]:
            after = self.finish(tag, after)
        return after

    def result(self, tag, after):
        return _swap_wait("swap_wait_" + tag, self.groups[tag]["swap"], after)


def kernel(x, ln1_g, w_in, b_in, rpb, w_att_o, conv_w, conv_b, w_rg_a, b_rg_a, w_rg_i, b_rg_i, lru_lambda, w_rec_o, w_out, ln2_g, w_ff1, w_ff2, lnf_g, loss_target, m_ln1_g, m_w_in, m_b_in, m_rpb, m_w_att_o, m_conv_w, m_conv_b, m_w_rg_a, m_b_rg_a, m_w_rg_i, m_b_rg_i, m_lru_lambda, m_w_rec_o, m_w_out, m_ln2_g, m_w_ff1, m_w_ff2, m_lnf_g, v_ln1_g, v_w_in, v_b_in, v_rpb, v_w_att_o, v_conv_w, v_conv_b, v_w_rg_a, v_b_rg_a, v_w_rg_i, v_b_rg_i, v_lru_lambda, v_w_rec_o, v_w_out, v_ln2_g, v_w_ff1, v_w_ff2, v_lnf_g):
    w = dict(ln1_g=ln1_g, w_in=w_in, b_in=b_in, rpb=rpb, w_att_o=w_att_o, conv_w=conv_w, conv_b=conv_b,
             w_rg_a=w_rg_a, b_rg_a=b_rg_a, w_rg_i=w_rg_i, b_rg_i=b_rg_i, lru_lambda=lru_lambda, w_rec_o=w_rec_o,
             w_out=w_out, ln2_g=ln2_g, w_ff1=w_ff1, w_ff2=w_ff2, lnf_g=lnf_g)
    m = dict(ln1_g=m_ln1_g, w_in=m_w_in, b_in=m_b_in, rpb=m_rpb, w_att_o=m_w_att_o, conv_w=m_conv_w,
             conv_b=m_conv_b, w_rg_a=m_w_rg_a, b_rg_a=m_b_rg_a, w_rg_i=m_w_rg_i, b_rg_i=m_b_rg_i,
             lru_lambda=m_lru_lambda, w_rec_o=m_w_rec_o, w_out=m_w_out, ln2_g=m_ln2_g, w_ff1=m_w_ff1,
             w_ff2=m_w_ff2, lnf_g=m_lnf_g)
    v = dict(ln1_g=v_ln1_g, w_in=v_w_in, b_in=v_b_in, rpb=v_rpb, w_att_o=v_w_att_o, conv_w=v_conv_w,
             conv_b=v_conv_b, w_rg_a=v_w_rg_a, b_rg_a=v_b_rg_a, w_rg_i=v_w_rg_i, b_rg_i=v_b_rg_i,
             lru_lambda=v_lru_lambda, w_rec_o=v_w_rec_o, w_out=v_w_out, ln2_g=v_ln2_g, w_ff1=v_w_ff1,
             w_ff2=v_w_ff2, lnf_g=v_lnf_g)
    chip = 2 * lax.axis_index("x") + lax.axis_index("y")
    ids = jnp.stack([chip, lax.axis_index("c")]).astype(jnp.int32)

    out_grad, out_delta, out_m, out_v = {}, {}, {}, {}

    def update(n, gn):
        shape, two_d = w[n].shape, gn.shape
        d, nm, nv = _adamw("adamw_" + n, w[n].reshape(two_d), gn, m[n].reshape(two_d), v[n].reshape(two_d), 256)
        out_grad[n], out_delta[n], out_m[n], out_v[n] = (gn.reshape(shape), d.reshape(shape), nm.reshape(shape),
                                                         nv.reshape(shape))
        return d

    reducer = _Reducer(ids)
    p, late = _gather_weights(w, chip)
    loss, grad_x, g = _local_step(x, loss_target, p, late, reducer)
    small, sizes = _pack_small(g, loss + reducer.last_token[:1, :1])
    after = reducer.begin("small", {}, small)[0]
    for tag in ("ff", "proj", "in"):
        for n, red in zip(reducer.groups[tag]["names"], reducer.result(tag, after)):
            after = update(n, red)
        if tag == "ff":
            after = reducer.finish("in", reducer.advance("small", after)[0])
    (small_red,) = reducer.result("small", reducer.finish("small", after))
    gsmall, loss = _unpack_small(small_red, sizes, {n: g[n].shape for n in SMALL})
    two_d = {n: (int(np.prod(w[n].shape[:-1])), w[n].shape[-1]) for n in SMALL}
    for n in SHARDED_VECS:
        gsmall[n] = lax.dynamic_slice_in_dim(gsmall[n], chip * (D // N_CHIPS), D // N_CHIPS, axis=1)
    gs = [gsmall[n].reshape(two_d[n]) for n in SMALL]
    updates = _adamw_small([w[n].reshape(two_d[n]) for n in SMALL], gs, [m[n].reshape(two_d[n]) for n in SMALL],
                           [v[n].reshape(two_d[n]) for n in SMALL])
    for n, gn, (d, nm, nv) in zip(SMALL, gs, updates):
        shape = w[n].shape
        out_grad[n], out_delta[n], out_m[n], out_v[n] = (gn.reshape(shape), d.reshape(shape), nm.reshape(shape),
                                                         nv.reshape(shape))
    return (loss, grad_x, *[out_grad[n] for n in ORDER], *[out_delta[n] for n in ORDER],
            *[out_m[n] for n in ORDER], *[out_v[n] for n in ORDER])
```

```python
import numpy as np
import jax
import jax.numpy as jnp
from jax import lax
from jax.experimental import pallas as pl
from jax.experimental.pallas import tpu as pltpu

F32 = jnp.float32
BF16 = jnp.bfloat16

T = 2048
D = 1024
D_ATT = 512
D_IN = 5632
D_FF = 4096
N_HEADS = 8
HEAD_DIM = 64
GRID_W = 64
N_ROWS = T // GRID_W
WIN_H = 8
WIN_W = 16
KEYS = WIN_H * GRID_W
N_CHIPS = 4
EPS = 1e-6
LRU_C = 8.0
SCALE = HEAD_DIM ** -0.5
REC_CB = 256
REC_BLOCK = 64
REC_CHUNK = 256
PAD = 8

ADAM_LR = 0.001
ADAM_B1 = 0.9
ADAM_B2 = 0.999
ADAM_EPS = 1e-08
ADAM_WD = 0.01
ADAM_STEP = 10

VMEM_LIMIT = 56 * 1024 * 1024

NN = (((1,), (0,)), ((), ()))
NT = (((1,), (1,)), ((), ()))
TN = (((0,), (0,)), ((), ()))
MESH = pl.DeviceIdType.MESH


def _params(sem=None):
    return pltpu.CompilerParams(dimension_semantics=sem, vmem_limit_bytes=VMEM_LIMIT)


def _dot(a, b, dims):
    return lax.dot_general(a, b, dims, preferred_element_type=F32)


def _sigmoid(x):
    return 0.5 * jnp.tanh(0.5 * x) + 0.5


def _matmul(name, a, b, *, dims, grid, a_spec, b_spec, out_shapes, out_specs, acc_shape,
            extras=(), extra_specs=(), epilogue=None, colsum_spec=None, colsum_shape=None, after=(),
            semantics=("parallel", "parallel", "arbitrary"), epilogue_takes_first=False):
    nk = grid[2]
    n_extra = len(extras)
    n_out = len(out_shapes)
    with_colsum = colsum_spec is not None

    def body(a_ref, b_ref, *rest):
        ex = rest[:n_extra]
        rest = rest[:n_extra] + rest[n_extra + len(after):]
        outs = rest[n_extra:n_extra + n_out]
        pos = n_extra + n_out
        cs_out = rest[pos] if with_colsum else None
        pos += 1 if with_colsum else 0
        acc = rest[pos]
        cs_acc = rest[pos + 1] if with_colsum else None
        k = pl.program_id(2)
        first_tile = pl.program_id(0) == 0

        @pl.when(k == 0)
        def _():
            acc[...] = jnp.zeros_like(acc)
            if with_colsum:
                cs_acc[...] = jnp.zeros_like(cs_acc)

        bv = b_ref[...]
        acc[...] += _dot(a_ref[...].astype(BF16), bv.astype(BF16), dims)
        if with_colsum:
            cs_acc[...] += jnp.sum(bv.astype(F32), axis=0, keepdims=True)

        @pl.when(k == nk - 1)
        def _():
            r = acc[...]
            if epilogue is None:
                outs[0][...] = r.astype(outs[0].dtype)
            elif epilogue_takes_first:
                epilogue(r, ex, outs, first_tile)
            else:
                epilogue(r, ex, outs)
            if with_colsum:
                cs_out[...] = cs_acc[...]

    shapes = list(out_shapes)
    specs = list(out_specs)
    scratch = [pltpu.VMEM(acc_shape, F32)]
    if with_colsum:
        shapes.append(colsum_shape)
        specs.append(colsum_spec)
        scratch.append(pltpu.VMEM((1, acc_shape[1]), F32))
    res = pl.pallas_call(
        body, name=name, grid=grid,
        in_specs=[a_spec, b_spec, *extra_specs] + [_ANY] * len(after),
        out_specs=specs, out_shape=shapes, scratch_shapes=scratch,
        compiler_params=_params(semantics),
    )(a, b, *extras, *after)
    return res


def _sds(shape, dtype):
    return jax.ShapeDtypeStruct(shape, dtype)


TM = 1024
NI = T // TM
TJ = T
NJ = T // TJ


def _mm_nn_cols(name, a, wg, out_dtype, *, bias=None, extras=(), extra_specs=(), epilogue=None,
                out_shapes=None, out_specs=None):
    k_dim, n4 = wg.shape[1], wg.shape[2]
    ex, exs = list(extras), list(extra_specs)
    if bias is not None:
        ex = [bias] + ex
        exs = [pl.BlockSpec((1, n4), lambda j, i, k: (0, j))] + exs
        user_ep = epilogue

        def epilogue(r, e, outs):
            r = r + e[0][...]
            if user_ep is None:
                outs[0][...] = r.astype(outs[0].dtype)
            else:
                user_ep(r, e[1:], outs)
    if out_shapes is None:
        out_shapes = [_sds((T, N_CHIPS * n4), out_dtype)]
        out_specs = [pl.BlockSpec((TJ, n4), lambda j, i, k: (i, j))]
    return _matmul(
        name, a, wg, dims=NN, grid=(N_CHIPS, NJ, 1),
        a_spec=pl.BlockSpec((TJ, k_dim), lambda j, i, k: (i, 0)),
        b_spec=pl.BlockSpec((None, k_dim, n4), lambda j, i, k: (j, 0, 0)),
        out_shapes=out_shapes, out_specs=out_specs, acc_shape=(TJ, n4),
        extras=ex, extra_specs=exs, epilogue=epilogue)


def _mm_nt_cols_rms_bwd(name, a, wg, x, g, dres, after=(), bf16_copy=False):
    n4 = wg.shape[2]
    row = pl.BlockSpec((TM, D), lambda i, j, k: (i, 0))
    vec = pl.BlockSpec((1, D), lambda i, j, k: (0, 0))

    def epilogue(dhv, ex, outs, first):
        x_ref, g_ref, dres_ref = ex
        dx_ref, dg_ref = outs[0], outs[-1]
        xv = x_ref[...]
        rstd = lax.rsqrt(jnp.mean(xv * xv, axis=-1, keepdims=True) + EPS)
        xhat = xv * rstd
        dy = dhv * g_ref[...]
        dx = dres_ref[...] + rstd * (dy - xhat * jnp.mean(dy * xhat, axis=-1, keepdims=True))
        dx_ref[...] = dx
        if bf16_copy:
            outs[1][...] = dx.astype(BF16)
        part = jnp.sum(dhv * xhat, axis=0, keepdims=True)

        @pl.when(first)
        def _():
            dg_ref[...] = part

        @pl.when(jnp.logical_not(first))
        def _():
            dg_ref[...] += part

    return _matmul(
        name, a, wg, dims=NT, grid=(NI, 1, N_CHIPS),
        a_spec=pl.BlockSpec((TM, n4), lambda i, j, k: (i, k)),
        b_spec=pl.BlockSpec((None, D, n4), lambda i, j, k: (k, 0, 0)),
        out_shapes=[_sds((T, D), F32)] + [_sds((T, D), BF16)] * bf16_copy + [_sds((1, D), F32)],
        out_specs=[row] + [row] * bf16_copy + [vec], acc_shape=(TM, D),
        extras=[x, g, dres], extra_specs=[row, vec, row], epilogue=epilogue, after=after,
        semantics=("arbitrary", "arbitrary", "arbitrary"), epilogue_takes_first=True)


def _mm_nt_rows(name, a, w, out_dtype, *, tn, extras=(), extra_specs=(), epilogue=None):
    k_dim, n = w.shape
    return _matmul(
        name, a, w, dims=NT, grid=(k_dim // tn, NJ, 1),
        a_spec=pl.BlockSpec((TJ, n), lambda j, i, k: (i, 0)),
        b_spec=pl.BlockSpec((tn, n), lambda j, i, k: (j, 0)),
        out_shapes=[_sds((T, k_dim), out_dtype)],
        out_specs=[pl.BlockSpec((TJ, tn), lambda j, i, k: (i, j))], acc_shape=(TJ, tn),
        extras=extras, extra_specs=extra_specs, epilogue=epilogue)


def _mm_tn_cols(name, a, g, n4, *, colsum=False):
    k_dim = a.shape[1]
    kw = {}
    if colsum:
        kw = dict(colsum_spec=pl.BlockSpec((1, n4), lambda j, i, k: (0, j)),
                  colsum_shape=_sds((1, N_CHIPS * n4), F32))
    return _matmul(
        name, a, g, dims=TN, grid=(N_CHIPS, 1, NJ),
        a_spec=pl.BlockSpec((TJ, k_dim), lambda j, i, k: (k, 0)),
        b_spec=pl.BlockSpec((TJ, n4), lambda j, i, k: (k, j)),
        out_shapes=[_sds((N_CHIPS, k_dim, n4), F32)],
        out_specs=[pl.BlockSpec((None, k_dim, n4), lambda j, i, k: (j, 0, 0))],
        acc_shape=(k_dim, n4), **kw)


def _mm_tn_rows(name, a, g, *, tm):
    k_dim, n = a.shape[1], g.shape[1]
    return _matmul(
        name, a, g, dims=TN, grid=(k_dim // tm, 1, NJ),
        a_spec=pl.BlockSpec((TJ, tm), lambda j, i, k: (k, j)),
        b_spec=pl.BlockSpec((TJ, n), lambda j, i, k: (k, 0)),
        out_shapes=[_sds((k_dim, n), F32)],
        out_specs=[pl.BlockSpec((tm, n), lambda j, i, k: (j, 0))], acc_shape=(tm, n))


TE = 256
NE = T // TE
_ROW = pl.BlockSpec((TE, D), lambda i: (i, 0))
_VEC = pl.BlockSpec((1, D), lambda i: (0, 0))


def _rms_fwd(name, x, g, after=()):
    def body(x_ref, g_ref, *rest):
        h_ref = rest[-1]
        xv = x_ref[...]
        rstd = lax.rsqrt(jnp.mean(xv * xv, axis=-1, keepdims=True) + EPS)
        h_ref[...] = (xv * rstd * g_ref[...]).astype(BF16)

    return pl.pallas_call(body, name=name, grid=(NE,), in_specs=[_ROW, _VEC] + [_ANY] * len(after), out_specs=_ROW,
                          out_shape=_sds((T, D), BF16), compiler_params=_params(("parallel",)))(x, g, *after)


def _mm_x2_loss_head(s, w_ff2, x1, target, g):
    k_dim = w_ff2.shape[0]
    row = pl.BlockSpec((TM, D), lambda i, j, k: (i, 0))
    vec = pl.BlockSpec((1, D), lambda i, j, k: (0, 0))

    def epilogue(r, ex, outs, first):
        x1_ref, t_ref, g_ref = ex
        loss_ref, dx_ref, dxb_ref, dg_ref = outs
        xv = x1_ref[...] + r
        rstd = lax.rsqrt(jnp.mean(xv * xv, axis=-1, keepdims=True) + EPS)
        xhat = xv * rstd
        gv = g_ref[...]
        err = xhat * gv - t_ref[...]
        dy = err * (1.0 / D)
        dxh = dy * gv
        dx = rstd * (dxh - xhat * jnp.mean(dxh * xhat, axis=-1, keepdims=True))
        dx_ref[...] = dx
        dxb_ref[...] = dx.astype(BF16)
        dg_part = jnp.sum(dy * xhat, axis=0, keepdims=True)
        loss_part = (0.5 / D) * jnp.sum(jnp.sum(err * err, axis=1, keepdims=True), axis=0, keepdims=True)

        @pl.when(first)
        def _():
            dg_ref[...] = dg_part
            loss_ref[...] = loss_part

        @pl.when(jnp.logical_not(first))
        def _():
            dg_ref[...] += dg_part
            loss_ref[...] += loss_part

    return _matmul(
        "mm_x2_loss_head", s, w_ff2, dims=NN, grid=(NI, 1, k_dim // D),
        a_spec=pl.BlockSpec((TM, D), lambda i, j, k: (i, k)), b_spec=pl.BlockSpec((D, D), lambda i, j, k: (k, 0)),
        out_shapes=[_sds((1, 1), F32), _sds((T, D), F32), _sds((T, D), BF16), _sds((1, D), F32)],
        out_specs=[pl.BlockSpec((1, 1), lambda i, j, k: (0, 0)), row, row, vec], acc_shape=(TM, D),
        extras=[x1, target, g], extra_specs=[row, row, vec], epilogue=epilogue,
        semantics=("arbitrary", "arbitrary", "arbitrary"), epilogue_takes_first=True)


MW = 512
_G_ATT_BLK = 3584 // MW
_G_REC_BLK = 4608 // MW


TB = 512


def _branch_specs():
    def row(cols):
        return pl.BlockSpec((TB, cols), lambda i: (i, 0))

    ga = pl.BlockSpec((TB, MW), lambda i: (i, _G_ATT_BLK))
    ga2 = pl.BlockSpec((TB, MW), lambda i: (i, _G_ATT_BLK + 1))
    gr = pl.BlockSpec((TB, MW), lambda i: (i, _G_REC_BLK))
    gr2 = pl.BlockSpec((TB, MW), lambda i: (i, _G_REC_BLK + 1))
    w_att = pl.BlockSpec((N_CHIPS, D_ATT, D // N_CHIPS), lambda i: (0, 0, 0))
    w_sq = pl.BlockSpec((D, D), lambda i: (0, 0))
    return row, (ga, ga2, gr, gr2), w_att, w_sq


def _gate_values(gate_refs):
    ga, ga2, gr, gr2 = (r[...].astype(F32) for r in gate_refs)
    return _sigmoid(jnp.concatenate([ga, ga2], axis=1)), _sigmoid(jnp.concatenate([gr, gr2], axis=1))


def _branches_fwd(att, g, z, x, w_att_o, w_rec_o, w_out, ln2_g):
    row, gate_specs, w_att, w_sq = _branch_specs()

    def body(att_ref, g_ref, ga_ref, ga2_ref, gr_ref, gr2_ref, x_ref, wa_ref, wr_ref, wo_ref, g2_ref,
             ya_ref, yr_ref, m_ref, x1_ref, h2_ref):
        attv = att_ref[...]
        ya = jnp.concatenate([_dot(attv, wa_ref[j], NN) for j in range(N_CHIPS)], axis=1)
        yr = _dot(g_ref[...], wr_ref[...], NN)
        sa, sr = _gate_values((ga_ref, ga2_ref, gr_ref, gr2_ref))
        mixed = (sa * ya + sr * yr).astype(BF16)
        ya_ref[...] = ya
        yr_ref[...] = yr
        m_ref[...] = mixed
        x1 = x_ref[...] + _dot(mixed, wo_ref[...], NN)
        x1_ref[...] = x1
        rstd = lax.rsqrt(jnp.mean(x1 * x1, axis=-1, keepdims=True) + EPS)
        h2_ref[...] = (x1 * rstd * g2_ref[...]).astype(BF16)

    return pl.pallas_call(
        body, name="branches_fwd", grid=(T // TB,),
        in_specs=[row(D_ATT), row(D), *gate_specs, row(D), w_att, w_sq, w_sq, pl.BlockSpec((1, D), lambda i: (0, 0))],
        out_specs=[row(D)] * 5,
        out_shape=[_sds((T, D), F32), _sds((T, D), F32), _sds((T, D), BF16), _sds((T, D), F32), _sds((T, D), BF16)],
        compiler_params=_params(("parallel",)))(att, g, z, z, z, z, x, w_att_o, w_rec_o, w_out, ln2_g)


def _branches_bwd(dx1_b, y_att, y_rec, z, w_att_o, w_rec_o, w_out):
    row, gate_specs, w_att, w_sq = _branch_specs()
    n4 = D // N_CHIPS

    def body(dx_ref, ya_ref, yr_ref, ga_ref, ga2_ref, gr_ref, gr2_ref, wa_ref, wr_ref, wo_ref,
             dya_ref, dyr_ref, dga_ref, dgr_ref, datt_ref, dg_ref):
        dm = _dot(dx_ref[...], wo_ref[...], NT)
        sa, sr = _gate_values((ga_ref, ga2_ref, gr_ref, gr2_ref))
        dya = (dm * sa).astype(BF16)
        dyr = (dm * sr).astype(BF16)
        dya_ref[...] = dya
        dyr_ref[...] = dyr
        dga_ref[...] = (dm * ya_ref[...] * sa * (1.0 - sa)).astype(BF16)
        dgr_ref[...] = (dm * yr_ref[...] * sr * (1.0 - sr)).astype(BF16)
        datt = _dot(dya[:, 0:n4], wa_ref[0], NT)
        for j in range(1, N_CHIPS):
            datt = datt + _dot(dya[:, j * n4:(j + 1) * n4], wa_ref[j], NT)
        datt_ref[...] = datt.astype(BF16)
        dg_ref[...] = _dot(dyr, wr_ref[...], NT).astype(BF16)

    return pl.pallas_call(
        body, name="branches_bwd", grid=(T // TB,),
        in_specs=[row(D), row(D), row(D), *gate_specs, w_att, w_sq, w_sq],
        out_specs=[row(D)] * 4 + [row(D_ATT), row(D)],
        out_shape=[_sds((T, D), BF16)] * 4 + [_sds((T, D_ATT), BF16), _sds((T, D), BF16)],
        compiler_params=_params(("parallel",)))(dx1_b, y_att, y_rec, z, z, z, z, w_att_o, w_rec_o, w_out)


HP = 2 * HEAD_DIM
N_HP = N_HEADS // 2
ATT_UNROLL_FWD = 16
ATT_UNROLL_BWD = 8
DIAG_ROWS = 32


def _window_maps():
    diag = np.zeros((GRID_W * GRID_W, 128), np.float32)
    for qc in range(GRID_W):
        w0 = min(max(qc - WIN_W // 2, 0), GRID_W - WIN_W)
        for kc in range(w0, w0 + WIN_W):
            diag[qc * GRID_W + kc, kc - qc + WIN_W - 1] = 1.0
    return diag, diag.sum(axis=1)[None, :]


def _split3(x):
    a = x.astype(BF16)
    r = x - a.astype(F32)
    b = r.astype(BF16)
    c = (r - b.astype(F32)).astype(BF16)
    return a, b, c


N_DROW = 2 * WIN_H - 1
N_DPAIR = N_DROW - 1


def _bias_pairs(rpb):
    diag, valid = _window_maps()
    r2 = jnp.pad(rpb.reshape(N_HEADS * N_DROW, 2 * WIN_W - 1),
                 ((0, 128 - N_HEADS * N_DROW), (0, 128 - (2 * WIN_W - 1))))

    def body(r_ref, d_ref, v_ref, o_ref):
        dv = d_ref[...]
        t = sum(_dot(part, dv, NN) for part in _split3(r_ref[...]))
        o_ref[...] = jnp.where(v_ref[...] > 0.0, t, -1e30)

    t = pl.pallas_call(body, name="rpb_expand", out_shape=_sds((128, GRID_W * GRID_W), F32),
                       compiler_params=_params())(r2, jnp.asarray(diag.T, BF16), jnp.asarray(valid, F32))
    t = t[:N_HEADS * N_DROW].reshape(N_HEADS, N_DROW, GRID_W, GRID_W)
    return jnp.concatenate([t[:, :N_DPAIR], t[:, 1:]], axis=-1)


def _row_bias(tb_ref, hh, d0):
    return jnp.concatenate([tb_ref[hh, d0 + 2 * ii] for ii in range(WIN_H // 2)], axis=1)


def _row_window(r):
    rs = jnp.clip(r - WIN_H // 2, 0, N_ROWS - WIN_H)
    return pl.multiple_of(r * GRID_W, GRID_W), pl.multiple_of(rs * GRID_W, GRID_W), rs - r + (WIN_H - 1)


def _split_heads(src_ref, dst_ref, scale=None):
    for hh in range(2):
        v = src_ref[:, hh * HEAD_DIM:(hh + 1) * HEAD_DIM]
        dst_ref[hh] = (v if scale is None else v * scale).astype(BF16)


def _attn_items(qb_ref, kb_ref, vb_ref, tb_ref, first_row, n_rows):
    wins = [_row_window(first_row + u) for u in range(n_rows)]
    items = [(u, hh) for u in range(n_rows) for hh in range(2)]
    q = [qb_ref[hh, pl.ds(wins[u][0], GRID_W), :] for u, hh in items]
    k = [kb_ref[hh, pl.ds(wins[u][1], KEYS), :] for u, hh in items]
    v = [vb_ref[hh, pl.ds(wins[u][1], KEYS), :] for u, hh in items]
    s = [_dot(qi, ki, NT) + _row_bias(tb_ref, hh, wins[u][2]) for qi, ki, (u, hh) in zip(q, k, items)]
    m = [jnp.max(si, axis=-1, keepdims=True) for si in s]
    e = [jnp.exp(si - mi) for si, mi in zip(s, m)]
    inv = [1.0 / jnp.sum(ei, axis=-1, keepdims=True) for ei in e]
    p = [ei * li for ei, li in zip(e, inv)]
    return wins, items, q, k, v, p


def _attn_in_specs():
    q = pl.BlockSpec((T, HP), lambda p: (0, p))
    k = pl.BlockSpec((T, HP), lambda p: (0, N_HP + p))
    v = pl.BlockSpec((T, HP), lambda p: (0, 2 * N_HP + p))
    tb = pl.BlockSpec((2, N_DPAIR, GRID_W, HP), lambda p: (p, 0, 0, 0))
    return q, k, v, tb


_HEAD_SCRATCH = pltpu.VMEM((2, T, HEAD_DIM), BF16)


_PROBS = pl.BlockSpec((T, 2 * KEYS), lambda p: (0, p))


def _attn_fwd(z, tb):
    def body(q_ref, k_ref, v_ref, tb_ref, o_ref, p_ref, qb_ref, kb_ref, vb_ref):
        _split_heads(q_ref, qb_ref, SCALE)
        _split_heads(k_ref, kb_ref)
        _split_heads(v_ref, vb_ref)

        def rows(it, carry):
            wins, items, _, _, v, p = _attn_items(qb_ref, kb_ref, vb_ref, tb_ref, it * ATT_UNROLL_FWD, ATT_UNROLL_FWD)
            pb = [pi.astype(BF16) for pi in p]
            o = [_dot(pi, vi, NN) for pi, vi in zip(pb, v)]
            for u, (q0, _, _) in enumerate(wins):
                o_ref[pl.ds(q0, GRID_W), :] = jnp.concatenate(o[2 * u:2 * u + 2], axis=1).astype(BF16)
                p_ref[pl.ds(q0, GRID_W), :] = jnp.concatenate(pb[2 * u:2 * u + 2], axis=1)
            return carry

        lax.fori_loop(0, N_ROWS // ATT_UNROLL_FWD, rows, 0)

    blk = pl.BlockSpec((T, HP), lambda p: (0, p))
    return pl.pallas_call(
        body, name="attn_fwd", grid=(N_HP,), in_specs=list(_attn_in_specs()), out_specs=[blk, _PROBS],
        out_shape=[_sds((T, D_ATT), BF16), _sds((T, N_HEADS * KEYS), BF16)], scratch_shapes=[_HEAD_SCRATCH] * 3,
        compiler_params=_params(("parallel",)))(z, z, z, tb)


def _attn_bwd(z, probs, d_att, after=()):
    def body(q_ref, k_ref, v_ref, p_ref, do_ref, flip_ref, *rest):
        (dq_ref, dk_ref, dv_ref, diag_ref, qb_ref, kb_ref, vb_ref, dob_ref, dka_ref, dva_ref,
         ds_ref) = rest[len(after):]
        _split_heads(q_ref, qb_ref, SCALE)
        _split_heads(k_ref, kb_ref)
        _split_heads(v_ref, vb_ref)
        _split_heads(do_ref, dob_ref)
        dka_ref[...] = jnp.zeros_like(dka_ref)
        dva_ref[...] = jnp.zeros_like(dva_ref)
        ds_ref[...] = jnp.zeros_like(ds_ref)

        def rows(it, carry):
            wins = [_row_window(it * ATT_UNROLL_BWD + u) for u in range(ATT_UNROLL_BWD)]
            items = [(u, hh) for u in range(ATT_UNROLL_BWD) for hh in range(2)]
            q = [qb_ref[hh, pl.ds(wins[u][0], GRID_W), :] for u, hh in items]
            k = [kb_ref[hh, pl.ds(wins[u][1], KEYS), :] for u, hh in items]
            v = [vb_ref[hh, pl.ds(wins[u][1], KEYS), :] for u, hh in items]
            pb = [p_ref[pl.ds(wins[u][0], GRID_W), hh * KEYS:(hh + 1) * KEYS] for u, hh in items]
            p = [pi.astype(F32) for pi in pb]
            do = [dob_ref[hh, pl.ds(wins[u][0], GRID_W), :] for u, hh in items]
            dv = [_dot(pi, di, TN) for pi, di in zip(pb, do)]
            dp = [_dot(di, vi, NT) for di, vi in zip(do, v)]
            ds = [pi * (dpi - jnp.sum(dpi * pi, axis=-1, keepdims=True)) for pi, dpi in zip(p, dp)]
            dsb = [d.astype(BF16) for d in ds]
            dq = [_dot(d, ki, NN) * SCALE for d, ki in zip(dsb, k)]
            dk = [_dot(d, qi, TN) for d, qi in zip(dsb, q)]
            for d, (u, hh) in zip(ds, items):
                for ii in range(WIN_H // 2):
                    ds_ref[hh, wins[u][2] + 2 * ii] += d[:, ii * HP:(ii + 1) * HP]
            for u, (q0, k0, _) in enumerate(wins):
                dq_ref[pl.ds(q0, GRID_W), :] = jnp.concatenate(dq[2 * u:2 * u + 2], axis=1).astype(BF16)
                dka_ref[pl.ds(k0, KEYS), :] += jnp.concatenate(dk[2 * u:2 * u + 2], axis=1)
                dva_ref[pl.ds(k0, KEYS), :] += jnp.concatenate(dv[2 * u:2 * u + 2], axis=1)
            return carry

        lax.fori_loop(0, N_ROWS // ATT_UNROLL_BWD, rows, 0)
        dk_ref[...] = dka_ref[...].astype(BF16)
        dv_ref[...] = dva_ref[...].astype(BF16)
        _diag_sums(ds_ref, flip_ref, diag_ref)

    blk = pl.BlockSpec((T, HP), lambda p: (0, p))
    q, k, v, _ = _attn_in_specs()
    flip =jnp.asarray(np.eye(HP, dtype=np.float32)[::-1], BF16)
    return pl.pallas_call(
        body, name="attn_bwd", grid=(N_HP,),
        in_specs=[q, k, v, _PROBS, blk, pl.BlockSpec((HP, HP), lambda p: (0, 0))] + [_ANY] * len(after),
        out_specs=[blk, blk, blk, pl.BlockSpec((None, DIAG_ROWS, HP), lambda p: (p, 0, 0))],
        out_shape=[_sds((T, D_ATT), BF16)] * 3 + [_sds((N_HP, DIAG_ROWS, HP), F32)],
        scratch_shapes=[_HEAD_SCRATCH] * 4 + [pltpu.VMEM((T, HP), F32), pltpu.VMEM((T, HP), F32),
                                              pltpu.VMEM((2, N_DPAIR, GRID_W, HP), F32)],
        compiler_params=_params(("parallel",)))(z, z, z, probs, d_att, flip, *after)


def _diag_sums(acc_ref, flip_ref, out_ref):
    flip = flip_ref[...]
    rows = []
    for hh in range(2):
        for pair in range(N_DPAIR):
            reversed_lanes = sum(_dot(part, flip, NN) for part in _split3(acc_ref[hh, pair]))
            skewed = pltpu.roll(reversed_lanes, 0, 1, stride=1, stride_axis=0)
            rows.append(jnp.sum(skewed, axis=0, keepdims=True))
    rows.append(jnp.zeros((DIAG_ROWS - len(rows), HP), F32))
    out_ref[...] = jnp.concatenate(rows, axis=0)


def _rpb_grad(diag_sums):
    g = diag_sums.reshape(N_HP * DIAG_ROWS, HP)
    sel = np.zeros((2, 128, N_HP * DIAG_ROWS), np.float32)
    lane = np.zeros((2, HP, 128), np.float32)
    for h in range(N_HEADS):
        for pair in range(N_DPAIR):
            for half in range(2):
                sel[half, h * N_DROW + pair + half, (h // 2) * DIAG_ROWS + (h % 2) * N_DPAIR + pair] = 1.0
    for j in range(2 * WIN_W - 1):
        for half in range(2):
            lane[half, (HP - 1 - GRID_W * half - (j - (WIN_W - 1))) % HP, j] = 1.0

    def body(g_ref, sel_ref, lane_ref, o_ref):
        parts = _split3(g_ref[...])
        total = None
        for half in range(2):
            picked = sum(_dot(sel_ref[half], part, NN) for part in parts)
            term = sum(_dot(part, lane_ref[half], NN) for part in _split3(picked))
            total = term if total is None else total + term
        o_ref[...] = total

    out = pl.pallas_call(body, name="rpb_grad", out_shape=_sds((128, 128), F32),
                         compiler_params=_params())(g, jnp.asarray(sel, BF16), jnp.asarray(lane, BF16))
    return out[:N_HEADS * N_DROW, :2 * WIN_W - 1].reshape(N_HEADS, N_DROW, 2 * WIN_W - 1)


N_CB = D // REC_CB
N_CHUNK = T // REC_CHUNK
N_TILE = T // 8
_U_BLK = 1536 // REC_CB
_Y_BLK = 2560 // REC_CB


def _block_diag(w):
    per = REC_CB // 64
    wt = w.reshape(2, N_CB, per, 64, 64)
    eye = jnp.eye(per, dtype=w.dtype)
    full = wt[:, :, :, :, None, :] * eye[None, None, :, None, :, None]
    return full.reshape(2, N_CB, REC_CB, REC_CB).astype(BF16)


def _gelu(x):
    c = 0.7978845608028654
    return 0.5 * x * (1.0 + jnp.tanh(c * (x + 0.044715 * x * x * x)))


def _gelu_grad(x):
    c = 0.7978845608028654
    th = jnp.tanh(c * (x + 0.044715 * x * x * x))
    return 0.5 * (1.0 + th) + 0.5 * x * (1.0 - th * th) * c * (1.0 + 3.0 * 0.044715 * x * x)


def _softplus_neg(lam):
    x = -lam
    e = jnp.exp(-jnp.abs(x))
    w = 1.0 + e
    l1p = jnp.where(w == 1.0, e, jnp.log(w) * e / (w - 1.0))
    return jnp.maximum(x, 0.0) + l1p


def _one_minus_exp(x):
    poly = x * (1.0 + x * (1 / 2 + x * (1 / 6 + x * (1 / 24 + x * (1 / 120 + x * (1 / 720))))))
    return jnp.where(x > -0.125, -poly, 1.0 - jnp.exp(x))


def _conv_taps(pad_ref, t0, w, sign):
    out = None
    for j in range(4):
        term = w[j:j + 1, :] * pad_ref[pl.ds(PAD + t0 + sign * (j - 2), REC_CHUNK), :]
        out = term if out is None else out + term
    return out


def _gates(u, wa, wi, ba, bi, sp):
    ub = u.astype(BF16)
    r = _sigmoid(_dot(ub, wa, NN) + ba)
    i = _sigmoid(_dot(ub, wi, NN) + bi)
    log_a = -LRU_C * r * sp
    a = jnp.exp(log_a)
    x = jnp.maximum(_one_minus_exp(2.0 * log_a), 0.0)
    positive = x > 0.0
    inv = lax.rsqrt(jnp.where(positive, x, 1.0))
    mult = jnp.where(positive, x * inv, 0.0)
    return r, i, a, mult, jnp.where(positive, inv, 0.0)


def _tile_scan(a, b, sub, reverse):
    for s in (1, 2, 4):
        if reverse:
            a_s, b_s, m = pltpu.roll(a, 8 - s, 0), pltpu.roll(b, 8 - s, 0), sub < 8 - s
        else:
            a_s, b_s, m = pltpu.roll(a, s, 0), pltpu.roll(b, s, 0), sub >= s
        b = jnp.where(m, a * b_s + b, b)
        a = jnp.where(m, a * a_s, a)
    return a, b


def _last_row(x, row):
    return jnp.broadcast_to(x[row:row + 1, :], x.shape)


def _rec_prologue(up_ref, cw_ref, cb_ref, wa_ref, wi_ref, ba_ref, bi_ref, lam_ref,
                  upad_ref, u_ref, a_refs, h_refs):
    cb = up_ref.shape[1]
    zeros = jnp.zeros((PAD, cb), F32)
    upad_ref[pl.ds(0, PAD), :] = zeros
    upad_ref[pl.ds(PAD + T, PAD), :] = zeros
    upad_ref[pl.ds(PAD, T), :] = up_ref[...].astype(F32)
    cw = cw_ref[...]
    sp = _softplus_neg(lam_ref[...])
    for c in range(N_CHUNK):
        t0 = c * REC_CHUNK
        u = cb_ref[...] + _conv_taps(upad_ref, t0, cw, 1)
        u_ref[pl.ds(t0, REC_CHUNK), :] = u
        for d in range(2):
            _, i, a, mult, _ = _gates(u, wa_ref[d], wi_ref[d], ba_ref[d:d + 1, :], bi_ref[d:d + 1, :], sp[d:d + 1, :])
            a_refs[d][pl.ds(t0, REC_CHUNK), :] = a
            h_refs[d][pl.ds(t0, REC_CHUNK), :] = mult * (i * u)

    sub = lax.broadcasted_iota(jnp.int32, (8, cb), 0)

    def tile(k, carry):
        cf, cr = carry
        tf = pl.multiple_of(k * 8, 8)
        tr = pl.multiple_of((N_TILE - 1 - k) * 8, 8)
        af, bf = _tile_scan(a_refs[0][pl.ds(tf, 8), :], h_refs[0][pl.ds(tf, 8), :], sub, False)
        hf = af * cf + bf
        h_refs[0][pl.ds(tf, 8), :] = hf
        ar, br = _tile_scan(a_refs[1][pl.ds(tr, 8), :], h_refs[1][pl.ds(tr, 8), :], sub, True)
        hr = ar * cr + br
        h_refs[1][pl.ds(tr, 8), :] = hr
        return _last_row(af, 7) * cf + _last_row(bf, 7), _last_row(ar, 0) * cr + _last_row(br, 0)

    z8 = jnp.zeros((8, cb), F32)
    lax.fori_loop(0, N_TILE, tile, (z8, z8))
    return sp


def _rec_specs():
    up = pl.BlockSpec((T, REC_CB), lambda c: (0, _U_BLK + c))
    yb = pl.BlockSpec((T, REC_CB), lambda c: (0, _Y_BLK + c))
    cw = pl.BlockSpec((4, REC_CB), lambda c: (0, c))
    cbias = pl.BlockSpec((1, REC_CB), lambda c: (0, c))
    wbd = pl.BlockSpec((2, None, REC_CB, REC_CB), lambda c: (0, c, 0, 0))
    vec2 = pl.BlockSpec((2, REC_CB), lambda c: (0, c))
    col = pl.BlockSpec((T, REC_CB), lambda c: (0, c))
    return up, yb, cw, cbias, wbd, vec2, col


def _rec_fwd(z, conv_w, conv_b, wa, wi, ba, bi, lam):
    up, yb, cw, cbias, wbd, vec2, col = _rec_specs()

    def body(up_ref, yb_ref, cw_ref, cb_ref, wa_ref, wi_ref, ba_ref, bi_ref, lam_ref, g_ref,
             u_ref, af_ref, ar_ref, hf_ref, hr_ref, upad_ref):
        _rec_prologue(up_ref, cw_ref, cb_ref, wa_ref, wi_ref, ba_ref, bi_ref, lam_ref,
                      upad_ref, u_ref, (af_ref, ar_ref), (hf_ref, hr_ref))

        def chunk(c, carry):
            t0 = pl.multiple_of(c * REC_CHUNK, REC_CHUNK)
            rows = pl.ds(t0, REC_CHUNK)
            g_ref[rows, :] = ((hf_ref[rows, :] + hr_ref[rows, :]) * _gelu(yb_ref[rows, :].astype(F32))).astype(BF16)
            return carry

        lax.fori_loop(0, N_CHUNK, chunk, 0)

    res = pl.pallas_call(
        body, name="rec_fwd", grid=(N_CB,),
        in_specs=[up, yb, cw, cbias, wbd, wbd, vec2, vec2, vec2], out_specs=[col] * 6,
        out_shape=[_sds((T, D), BF16)] + [_sds((T, D), F32)] * 5,
        scratch_shapes=[pltpu.VMEM((T + 2 * PAD, REC_CB), F32)],
        compiler_params=_params(("parallel",)))(z, z, conv_w, conv_b, wa, wi, ba, bi, lam)
    return res[0], tuple(res[1:])


def _rec_bwd(z, dg, saved, conv_w, conv_b, wa, wi, ba, bi, lam, after=()):
    up, yb, cw, cbias, wbd, vec2, col = _rec_specs()

    def body(up_ref, yb_ref, dg_ref, u_ref, af_ref, ar_ref, hf_ref, hr_ref,
             cw_ref, cb_ref, wa_ref, wi_ref, ba_ref, bi_ref, lam_ref, *rest):
        (dup_ref, dyb_ref, dcw_ref, dcb_ref, dwa_out, dwi_out, dba_ref, dbi_ref, dlam_ref,
         upad_ref, dh_ref, gf_ref, gr_ref, daf_ref, dar_ref, dupad_ref, dwa_ref, dwi_ref) = rest[len(after):]
        g_refs, da_refs = (gf_ref, gr_ref), (daf_ref, dar_ref)
        cb = up_ref.shape[1]
        zeros = jnp.zeros((PAD, cb), F32)
        upad_ref[pl.ds(0, PAD), :] = zeros
        upad_ref[pl.ds(PAD + T, PAD), :] = zeros
        upad_ref[pl.ds(PAD, T), :] = up_ref[...].astype(F32)
        sp = _softplus_neg(lam_ref[...])

        def gate_chunk(c, carry):
            t0 = pl.multiple_of(c * REC_CHUNK, REC_CHUNK)
            rows = pl.ds(t0, REC_CHUNK)
            y = yb_ref[rows, :].astype(F32)
            dgv = dg_ref[rows, :].astype(F32)
            dh_ref[rows, :] = dgv * _gelu(y)
            dyb_ref[rows, :] = (dgv * (hf_ref[rows, :] + hr_ref[rows, :]) * _gelu_grad(y)).astype(BF16)
            return carry

        lax.fori_loop(0, N_CHUNK, gate_chunk, 0)

        sub = lax.broadcasted_iota(jnp.int32, (8, cb), 0)

        def tile(k, carry):
            cf, cr = carry
            kf = N_TILE - 1 - k
            tf = pl.multiple_of(kf * 8, 8)
            tnext = pl.multiple_of(jnp.minimum(kf + 1, N_TILE - 1) * 8, 8)
            tprev = pl.multiple_of(jnp.maximum(kf - 1, 0) * 8, 8)
            a_t = af_ref[pl.ds(tf, 8), :]
            a_n = jnp.where(kf < N_TILE - 1, af_ref[pl.ds(tnext, 8), :], 0.0)
            a_sh = jnp.where(sub == 7, pltpu.roll(a_n, 7, 0), pltpu.roll(a_t, 7, 0))
            ca, cbb = _tile_scan(a_sh, dh_ref[pl.ds(tf, 8), :], sub, True)
            gf = ca * cf + cbb
            h_t = hf_ref[pl.ds(tf, 8), :]
            h_p = jnp.where(kf > 0, hf_ref[pl.ds(tprev, 8), :], 0.0)
            h_sh = jnp.where(sub == 0, pltpu.roll(h_p, 1, 0), pltpu.roll(h_t, 1, 0))
            gf_ref[pl.ds(tf, 8), :] = gf
            daf_ref[pl.ds(tf, 8), :] = gf * h_sh
            tr = pl.multiple_of(k * 8, 8)
            rnext = pl.multiple_of(jnp.minimum(k + 1, N_TILE - 1) * 8, 8)
            rprev = pl.multiple_of(jnp.maximum(k - 1, 0) * 8, 8)
            b_t = ar_ref[pl.ds(tr, 8), :]
            b_p = jnp.where(k > 0, ar_ref[pl.ds(rprev, 8), :], 0.0)
            b_sh = jnp.where(sub == 0, pltpu.roll(b_p, 1, 0), pltpu.roll(b_t, 1, 0))
            ra, rb = _tile_scan(b_sh, dh_ref[pl.ds(tr, 8), :], sub, False)
            gr = ra * cr + rb
            hr_t = hr_ref[pl.ds(tr, 8), :]
            hr_n = jnp.where(k < N_TILE - 1, hr_ref[pl.ds(rnext, 8), :], 0.0)
            hr_sh = jnp.where(sub == 7, pltpu.roll(hr_n, 7, 0), pltpu.roll(hr_t, 7, 0))
            gr_ref[pl.ds(tr, 8), :] = gr
            dar_ref[pl.ds(tr, 8), :] = gr * hr_sh
            return _last_row(gf, 0), _last_row(gr, 7)

        z8 = jnp.zeros((8, cb), F32)
        lax.fori_loop(0, N_TILE, tile, (z8, z8))

        dupad_ref[pl.ds(0, PAD), :] = zeros
        dupad_ref[pl.ds(PAD + T, PAD), :] = zeros
        dwa_ref[...] = jnp.zeros_like(dwa_ref)
        dwi_ref[...] = jnp.zeros_like(dwi_ref)
        dba_ref[...] = jnp.zeros_like(dba_ref)
        dbi_ref[...] = jnp.zeros_like(dbi_ref)
        dlam_ref[...] = jnp.zeros_like(dlam_ref)

        def grad_chunk(c, carry):
            t0 = pl.multiple_of(c * REC_CHUNK, REC_CHUNK)
            rows = pl.ds(t0, REC_CHUNK)
            u = u_ref[rows, :]
            ub = u.astype(BF16)
            du = jnp.zeros((REC_CHUNK, cb), F32)
            for d in range(2):
                r, i, a, mult, inv_mult = _gates(u, wa_ref[d], wi_ref[d], ba_ref[d:d + 1, :], bi_ref[d:d + 1, :],
                                                 sp[d:d + 1, :])
                dbx = g_refs[d][rows, :]
                dmult = dbx * (i * u)
                diu = dbx * mult
                a2 = a * a
                dlog = da_refs[d][rows, :] * a - dmult * (a2 * inv_mult)
                dpa = (dlog * (-LRU_C) * sp[d:d + 1, :]) * r * (1.0 - r)
                dpi = (diu * u) * i * (1.0 - i)
                dpab, dpib = dpa.astype(BF16), dpi.astype(BF16)
                du = du + diu * i + _dot(dpab, wa_ref[d], NT) + _dot(dpib, wi_ref[d], NT)
                dwa_ref[d] += _dot(ub, dpab, TN)
                dwi_ref[d] += _dot(ub, dpib, TN)
                dba_ref[d:d + 1, :] += jnp.sum(dpa, axis=0, keepdims=True)
                dbi_ref[d:d + 1, :] += jnp.sum(dpi, axis=0, keepdims=True)
                dlam_ref[d:d + 1, :] += jnp.sum(dlog * r, axis=0, keepdims=True)
            dupad_ref[pl.ds(PAD + t0, REC_CHUNK), :] = du
            return carry

        lax.fori_loop(0, N_CHUNK, grad_chunk, 0)
        dlam_ref[...] = dlam_ref[...] * (LRU_C * _sigmoid(-lam_ref[...]))
        for d in range(2):
            for blk in range(REC_CB // REC_BLOCK):
                lo, hi = blk * REC_BLOCK, (blk + 1) * REC_BLOCK
                dwa_out[d, blk] = dwa_ref[d, lo:hi, lo:hi]
                dwi_out[d, blk] = dwi_ref[d, lo:hi, lo:hi]

        cw = cw_ref[...]
        dcb = jnp.zeros((1, cb), F32)
        dcw = [jnp.zeros((1, cb), F32) for _ in range(4)]
        for c in range(N_CHUNK):
            t0 = c * REC_CHUNK
            du = dupad_ref[pl.ds(PAD + t0, REC_CHUNK), :]
            dcb = dcb + jnp.sum(du, axis=0, keepdims=True)
            for j in range(4):
                dcw[j] = dcw[j] + jnp.sum(du * upad_ref[pl.ds(PAD + t0 + j - 2, REC_CHUNK), :], axis=0, keepdims=True)
            dup_ref[pl.ds(t0, REC_CHUNK), :] = _conv_taps(dupad_ref, t0, cw, -1).astype(BF16)
        dcb_ref[...] = dcb
        dcw_ref[...] = jnp.concatenate(dcw, axis=0)

    full = pltpu.VMEM((T, REC_CB), F32)
    padded = pltpu.VMEM((T + 2 * PAD, REC_CB), F32)
    per = REC_CB // REC_BLOCK
    diag = pl.BlockSpec((2, per, REC_BLOCK, REC_BLOCK), lambda c: (0, c, 0, 0))
    return pl.pallas_call(
        body, name="rec_bwd", grid=(N_CB,),
        in_specs=[up, yb] + [col] * 6 + [cw, cbias, wbd, wbd, vec2, vec2, vec2] + [_ANY] * len(after),
        out_specs=[col, col, cw, cbias, diag, diag, vec2, vec2, vec2],
        out_shape=[_sds((T, D), BF16), _sds((T, D), BF16), _sds((4, D), F32), _sds((1, D), F32),
                   _sds((2, D // REC_BLOCK, REC_BLOCK, REC_BLOCK), F32), _sds((2, D // REC_BLOCK, REC_BLOCK, REC_BLOCK), F32),
                   _sds((2, D), F32), _sds((2, D), F32), _sds((2, D), F32)],
        scratch_shapes=[padded, full, full, full, full, full, padded,
                        pltpu.VMEM((2, REC_CB, REC_CB), F32), pltpu.VMEM((2, REC_CB, REC_CB), F32)],
        compiler_params=_params(("parallel",)))(z, z, dg, *saved, conv_w, conv_b, wa, wi, ba, bi, lam, *after)


class _NoReducer:
    def begin(self, tag, grads):
        return ()

    def advance(self, tag, after):
        return ()

    def interlude(self, tokens):
        return None


def _local_step(x, target, p, late=None, reducer=_NoReducer()):
    x = x.reshape(T, D)
    target = target.reshape(T, D)
    tb = _bias_pairs(p["rpb"])
    wa, wi = _block_diag(p["w_rg_a"]), _block_diag(p["w_rg_i"])

    h1 = _rms_fwd("rms1_fwd", x, p["ln1_g"], after=late[0] if late else ())
    if late:
        p = {**p, **late[1]((h1, tb, wa, wi))}
    rec_params = (p["conv_w"], p["conv_b"], wa, wi, p["b_rg_a"], p["b_rg_i"], p["lru_lambda"])
    (z,) = _mm_nn_cols("mm_z", h1, p["w_in"], BF16, bias=p["b_in"])
    att, probs = _attn_fwd(z, tb)
    g, rec_saved = _rec_fwd(z, *rec_params)
    if late:
        p = {**p, **late[2](g)}
    y_att, y_rec, mixed, x1, h2 = _branches_fwd(att, g, z, x, p["w_att_o"], p["w_rec_o"], p["w_out"], p["ln2_g"])

    def relu2(r, ex, outs):
        rp = jnp.maximum(r, 0.0)
        outs[0][...] = (rp * rp).astype(BF16)

    (s,) = _mm_nn_cols("mm_ff1", h2, p["w_ff1"], BF16, epilogue=relu2)
    loss, dx2, dx2_b, g_lnf = _mm_x2_loss_head(s, p["w_ff2"], x1, target, p["lnf_g"])

    def relu2_bwd(r, ex, outs):
        outs[0][...] = (r * 2.0 * jnp.sqrt(ex[0][...].astype(F32))).astype(BF16)

    (df,) = _mm_nt_rows("mm_df", dx2_b, p["w_ff2"], BF16, tn=D, extras=[s],
                        extra_specs=[pl.BlockSpec((TJ, D), lambda j, i, k: (i, j))], epilogue=relu2_bwd)
    (g_w_ff2,) = _mm_tn_rows("mm_g_ff2", s, dx2_b, tm=D)
    (g_w_ff1,) = _mm_tn_cols("mm_g_ff1", h2, df, D)
    tok = reducer.begin("ff", dict(w_ff2=g_w_ff2, w_ff1=g_w_ff1))
    dx1, dx1_b, g_ln2 = _mm_nt_cols_rms_bwd("mm_dh2_rms2_bwd", df, p["w_ff1"], x1, p["ln2_g"], dx2, after=tok,
                                            bf16_copy=True)

    dy_att, dy_rec, dg_att, dg_rec, d_att, d_g = _branches_bwd(dx1_b, y_att, y_rec, z, p["w_att_o"], p["w_rec_o"],
                                                               p["w_out"])
    (g_w_out,) = _mm_tn_rows("mm_g_out", mixed, dx1_b, tm=D)
    (g_w_att_o,) = _mm_tn_cols("mm_g_att_o", att, dy_att, D // N_CHIPS)
    (g_w_rec_o,) = _mm_tn_rows("mm_g_rec_o", g, dy_rec, tm=D)
    tok = reducer.advance("ff", g_w_rec_o) + reducer.begin("proj", dict(w_out=g_w_out, w_att_o=g_w_att_o, w_rec_o=g_w_rec_o))

    dq, dk, dv, ds_acc = _attn_bwd(z, probs, d_att, after=tok)
    g_rpb = _rpb_grad(ds_acc)
    tok = reducer.advance("proj", dq)
    d_up, d_yb, g_conv_w, g_conv_b, g_wa, g_wi, g_ba, g_bi, g_lam = _rec_bwd(z, d_g, rec_saved, *rec_params, after=tok)
    dz = jnp.concatenate([dq, dk, dv, d_up, d_yb, dg_att, dg_rec], axis=1)

    g_w_in, g_b_in = _mm_tn_cols("mm_g_in", h1, dz, D_IN // N_CHIPS, colsum=True)
    tok = reducer.advance("in", reducer.interlude(reducer.begin("in", dict(w_in=g_w_in))))
    grad_x, g_ln1 = _mm_nt_cols_rms_bwd("mm_dh1_rms1_bwd", dz, p["w_in"], x, p["ln1_g"], dx1, after=tok)

    grads = dict(ln1_g=g_ln1, w_in=g_w_in, b_in=g_b_in, rpb=g_rpb, w_att_o=g_w_att_o, conv_w=g_conv_w,
                 conv_b=g_conv_b, w_rg_a=g_wa, b_rg_a=g_ba, w_rg_i=g_wi,
                 b_rg_i=g_bi, lru_lambda=g_lam, w_rec_o=g_w_rec_o, w_out=g_w_out, ln2_g=g_ln2,
                 w_ff1=g_w_ff1, w_ff2=g_w_ff2, lnf_g=g_lnf)
    return loss, grad_x.reshape(1, T, D), grads


_ANY = pl.BlockSpec(memory_space=pl.ANY)
N_PEERS = N_CHIPS - 1


def _place():
    x, y, c = lax.axis_index("x"), lax.axis_index("y"), lax.axis_index("c")
    peers = [(1 - x, y), (x, 1 - y), (1 - x, 1 - y)]
    return x, y, c, 2 * x + y, peers


def _remote(src, dst, send_sem, recv_sem, dev):
    return pltpu.make_async_remote_copy(src_ref=src, dst_ref=dst, send_sem=send_sem, recv_sem=recv_sem,
                                        device_id=dev, device_id_type=MESH)


def _prefetch_call(body, name, ids, grid, in_specs, out_specs, out_shape, args, semantics=None):
    spec = pltpu.PrefetchScalarGridSpec(num_scalar_prefetch=1, grid=grid, in_specs=in_specs, out_specs=out_specs)
    return pl.pallas_call(body, name=name, grid_spec=spec, out_shape=out_shape,
                          compiler_params=_params(semantics or ("parallel",) * len(grid)))(ids, *args)


def _cast_bf16(name, w, chip_id, after=()):
    rows, cols = w.shape
    rb = min(rows, 256)

    def body(ids_ref, w_ref, *rest):
        rest[-1][...] = w_ref[...].astype(BF16)

    return _prefetch_call(body, name, chip_id, (rows // rb,),
                          [pl.BlockSpec((rb, cols), lambda i, ids: (i, 0))] + [_ANY] * len(after),
                          pl.BlockSpec((None, rb, cols), lambda i, ids: (ids[0], i, 0)),
                          _sds((N_CHIPS, rows, cols), BF16), (w, *after))


def _dma_sems(*counts):
    return [pltpu.SemaphoreType.DMA((k,)) for k in counts]


_HBM = pl.BlockSpec(memory_space=pltpu.HBM)
_SEM = pl.BlockSpec(memory_space=pltpu.SEMAPHORE)
_SPLIT_COPY = pltpu.CompilerParams(has_side_effects=pltpu.SideEffectType.DATAFLOW_SIDE_EFFECTING)
SIBLING_ID = 0
_SPLIT_COPY_SIBLING = pltpu.CompilerParams(has_side_effects=pltpu.SideEffectType.DATAFLOW_SIDE_EFFECTING,
                                           collective_id=SIBLING_ID)


def _sibling_handshake():
    x, y, c = lax.axis_index("x"), lax.axis_index("y"), lax.axis_index("c")
    barrier = pltpu.get_barrier_semaphore()
    pl.semaphore_signal(barrier, inc=1, device_id=(x, y, 1 - c), device_id_type=MESH)
    pl.semaphore_wait(barrier, 1)


def _hbm(arrays):
    return [pltpu.with_memory_space_constraint(a, pltpu.HBM) for a in arrays]


def _hbm_like(arrays):
    return [pltpu.HBM(a.shape, a.dtype) for a in arrays]


def _halves(buf, c):
    half = buf.shape[1] // 2
    return pl.ds(c * half, half), pl.ds((1 - c) * half, half)


def _gather_start(name, slots):
    n = len(slots)
    nk = n * N_PEERS

    def body(*refs):
        bufs = refs[n:2 * n]
        send_sems, recv_sems, token = refs[2 * n:]
        x, y, c, chip, peers = _place()
        for t in range(n):
            mine, _ = _halves(bufs[t], c)
            for r, (px, py) in enumerate(peers):
                k = t * N_PEERS + r
                own = bufs[t].at[chip, mine]
                _remote(own, own, send_sems.at[k], recv_sems.at[k], (px, py, c)).start()
        token[...] = jnp.zeros_like(token)

    res = pl.pallas_call(
        body, name=name, in_specs=[_HBM] * n, out_specs=[_HBM] * n + [_SEM, _SEM, pl.BlockSpec(memory_space=pltpu.VMEM)],
        out_shape=_hbm_like(slots) + [pltpu.SemaphoreType.DMA((nk,)), pltpu.SemaphoreType.DMA((nk,)),
                                      _sds((8, 128), F32)],
        input_output_aliases={t: t for t in range(n)}, compiler_params=_SPLIT_COPY)(*_hbm(slots))
    return res[:n], (res[n], res[n + 1]), res[n + 2]


def _gather_wait(name, bufs, sems, after):
    n = len(bufs)
    after = tuple(after) if isinstance(after, (tuple, list)) else (after,)

    def body(*refs):
        ins = refs[:n]
        send_sems, recv_sems = refs[n], refs[n + 1]
        x, y, c, chip, peers = _place()
        for t in range(n):
            mine, _ = _halves(ins[t], c)
            for r, (px, py) in enumerate(peers):
                k = t * N_PEERS + r
                cp = _remote(ins[t].at[chip, mine], ins[t].at[2 * px + py, mine], send_sems.at[k], recv_sems.at[k],
                             (px, py, c))
                cp.wait_send()
                cp.wait_recv()

    return pl.pallas_call(
        body, name=name, in_specs=[_HBM] * n + [_SEM, _SEM] + [_ANY] * len(after), out_specs=[_HBM] * n,
        out_shape=_hbm_like(bufs), input_output_aliases={t: t for t in range(n)},
        compiler_params=_SPLIT_COPY)(*bufs, *sems, *after)


def _gather_forward(name, bufs):
    n = len(bufs)
    nk = n * N_PEERS

    def body(*refs):
        _sibling_handshake()
        outs = refs[n:2 * n]
        send_sems, recv_sems = refs[2 * n:]
        x, y, c, chip, peers = _place()
        sibling = (x, y, 1 - c)
        sends = []
        for t in range(n):
            mine, _ = _halves(outs[t], c)
            for r, (px, py) in enumerate(peers):
                k = t * N_PEERS + r
                landed = outs[t].at[2 * px + py, mine]
                sends.append(_remote(landed, landed, send_sems.at[k], recv_sems.at[k], sibling))
                sends[-1].start()
        for t in range(n):
            _, theirs = _halves(outs[t], c)
            for r, (px, py) in enumerate(peers):
                k = t * N_PEERS + r
                landed = outs[t].at[2 * px + py, theirs]
                _remote(landed, landed, send_sems.at[k], recv_sems.at[k], sibling).wait_recv()
        for cp in sends:
            cp.wait_send()

    return pl.pallas_call(
        body, name=name, in_specs=[_ANY] * n, out_specs=[_ANY] * n, out_shape=[_sds(b.shape, b.dtype) for b in bufs],
        input_output_aliases={t: t for t in range(n)}, scratch_shapes=_dma_sems(nk, nk),
        compiler_params=pltpu.CompilerParams(collective_id=SIBLING_ID))(*bufs)


def _pair_copies(n, srcs, lands, send_sems, recv_sems):
    x, y, c, _, _ = _place()
    sibling = (x, y, 1 - c)
    copies = []
    for t in range(n):
        half = srcs[t].shape[1] // 2
        for j in range(N_CHIPS):
            k = t * N_CHIPS + j
            copies.append(_remote(srcs[t].at[j, pl.ds((1 - c) * half, half)], lands[t].at[j],
                                  send_sems.at[k], recv_sems.at[k], sibling))
    for t in range(n, len(srcs)):
        k = n * N_CHIPS + t - n
        copies.append(_remote(srcs[t], lands[t], send_sems.at[k], recv_sems.at[k], sibling))
    return copies


def _pair_start(name, grads, wholes=()):
    n = len(grads)
    srcs = list(grads) + list(wholes)
    m = len(srcs)
    lands = [pltpu.HBM((N_CHIPS, g.shape[1] // 2, g.shape[2]), F32) for g in grads] + _hbm_like(wholes)
    ns = n * N_CHIPS + len(wholes)

    def body(*refs):
        _sibling_handshake()
        src_refs, land_refs = refs[m:2 * m], refs[2 * m:3 * m]
        send_sems, recv_sems, token = refs[3 * m:]
        for cp in _pair_copies(n, src_refs, land_refs, send_sems, recv_sems):
            cp.start()
        token[...] = jnp.zeros_like(token)

    res = pl.pallas_call(
        body, name=name, in_specs=[_HBM] * m,
        out_specs=[_HBM] * (2 * m) + [_SEM, _SEM, pl.BlockSpec(memory_space=pltpu.VMEM)],
        out_shape=_hbm_like(srcs) + lands + [pltpu.SemaphoreType.DMA((ns,)), pltpu.SemaphoreType.DMA((ns,)),
                                             _sds((8, 128), F32)],
        input_output_aliases={t: t for t in range(m)}, compiler_params=_SPLIT_COPY_SIBLING)(*_hbm(srcs))
    return (res[:m], res[m:2 * m], (res[2 * m], res[2 * m + 1])), res[2 * m + 2]


def _pair_wait(name, flight, n, after):
    srcs, lands, sems = flight
    m = len(srcs)

    def body(*refs):
        for cp in _pair_copies(n, refs[:m], refs[m:2 * m], refs[2 * m], refs[2 * m + 1]):
            cp.wait_send()
            cp.wait_recv()

    res = pl.pallas_call(
        body, name=name, in_specs=[_HBM] * (2 * m) + [_SEM, _SEM, _ANY], out_specs=[_HBM] * (2 * m),
        out_shape=_hbm_like(srcs) + _hbm_like(lands), input_output_aliases={t: t for t in range(2 * m)},
        compiler_params=_SPLIT_COPY)(*srcs, *lands, *sems, after)
    return res[:m], res[m:]


def _chip_copies(srcs, lands, small_src, small_land, send_sems, recv_sems):
    x, y, c, chip, peers = _place()
    n = len(srcs)
    copies = []
    for r, (px, py) in enumerate(peers):
        for t in range(n):
            k = t * N_PEERS + r
            copies.append(_remote(srcs[t].at[2 * px + py], lands[t].at[r], send_sems.at[k], recv_sems.at[k], (px, py, c)))
        if small_src is not None:
            k = n * N_PEERS + r
            half_s = small_src.shape[0] // 2
            copies.append(_remote(small_src.at[pl.ds(c * half_s, half_s)], small_land.at[r],
                                  send_sems.at[k], recv_sems.at[k], (px, py, c)))
    return copies


def _chip_start(name, sums_bf16, small=None):
    n = len(sums_bf16)
    srcs = list(sums_bf16) + ([small] if small is not None else [])
    m = len(srcs)
    lands = [pltpu.HBM((N_PEERS,) + s.shape[1:], BF16) for s in sums_bf16]
    if small is not None:
        lands.append(pltpu.HBM((N_PEERS, small.shape[0] // 2, 128), F32))
    nk = m * N_PEERS

    def body(*refs):
        src_refs, land_refs = refs[m:2 * m], refs[2 * m:3 * m]
        send_sems, recv_sems, token = refs[3 * m:]
        small_src, small_land = (src_refs[n], land_refs[n]) if small is not None else (None, None)
        for cp in _chip_copies(src_refs[:n], land_refs[:n], small_src, small_land, send_sems, recv_sems):
            cp.start()
        token[...] = jnp.zeros_like(token)

    res = pl.pallas_call(
        body, name=name, in_specs=[_HBM] * m,
        out_specs=[_HBM] * (2 * m) + [_SEM, _SEM, pl.BlockSpec(memory_space=pltpu.VMEM)],
        out_shape=_hbm_like(srcs) + lands + [pltpu.SemaphoreType.DMA((nk,)), pltpu.SemaphoreType.DMA((nk,)),
                                             _sds((8, 128), F32)],
        input_output_aliases={t: t for t in range(m)}, compiler_params=_SPLIT_COPY)(*_hbm(srcs))
    return (res[:m], res[m:2 * m], (res[2 * m], res[2 * m + 1])), res[2 * m + 2]


def _chip_wait(name, flight, with_small, after):
    srcs, lands, sems = flight
    m = len(srcs)
    n = m - 1 if with_small else m

    def body(*refs):
        src_refs, land_refs = refs[:m], refs[m:2 * m]
        send_sems, recv_sems = refs[2 * m], refs[2 * m + 1]
        small_src, small_land = (src_refs[n], land_refs[n]) if with_small else (None, None)
        for cp in _chip_copies(src_refs[:n], land_refs[:n], small_src, small_land, send_sems, recv_sems):
            cp.wait_send()
            cp.wait_recv()

    res = pl.pallas_call(
        body, name=name, in_specs=[_HBM] * (2 * m) + [_SEM, _SEM, _ANY], out_specs=[_HBM] * (2 * m),
        out_shape=_hbm_like(srcs) + _hbm_like(lands), input_output_aliases={t: t for t in range(2 * m)},
        compiler_params=_SPLIT_COPY)(*srcs, *lands, *sems, after)
    return res[:m], res[m:]


def _swap_start(name, bufs):
    n = len(bufs)

    def body(*refs):
        _sibling_handshake()
        outs = refs[n:2 * n]
        send_sems, recv_sems, token = refs[2 * n:]
        x, y, c, _, _ = _place()
        for t in range(n):
            h = outs[t].shape[0] // 2
            mine = outs[t].at[pl.ds(c * h, h)]
            _remote(mine, mine, send_sems.at[t], recv_sems.at[t], (x, y, 1 - c)).start()
        token[...] = jnp.zeros_like(token)

    res = pl.pallas_call(
        body, name=name, in_specs=[_HBM] * n, out_specs=[_HBM] * n + [_SEM, _SEM, pl.BlockSpec(memory_space=pltpu.VMEM)],
        out_shape=_hbm_like(bufs) + [pltpu.SemaphoreType.DMA((n,)), pltpu.SemaphoreType.DMA((n,)), _sds((8, 128), F32)],
        input_output_aliases={t: t for t in range(n)}, compiler_params=_SPLIT_COPY_SIBLING)(*_hbm(bufs))
    return (res[:n], (res[n], res[n + 1])), res[n + 2]


def _swap_wait(name, flight, after):
    bufs, sems = flight
    n = len(bufs)

    def body(*refs):
        ins = refs[:n]
        send_sems, recv_sems = refs[n], refs[n + 1]
        x, y, c, _, _ = _place()
        for t in range(n):
            h = ins[t].shape[0] // 2
            cp = _remote(ins[t].at[pl.ds(c * h, h)], ins[t].at[pl.ds((1 - c) * h, h)], send_sems.at[t],
                         recv_sems.at[t], (x, y, 1 - c))
            cp.wait_send()
            cp.wait_recv()

    return pl.pallas_call(
        body, name=name, in_specs=[_HBM] * n + [_SEM, _SEM, _ANY], out_specs=[_HBM] * n, out_shape=_hbm_like(bufs),
        input_output_aliases={t: t for t in range(n)}, compiler_params=_SPLIT_COPY)(*bufs, *sems, after)


def _pair_sum(name, grad, got, ids):
    _, rows, cols = got.shape
    rb = min(rows, 256)
    nb = rows // rb
    blk = pl.BlockSpec((None, rb, cols), lambda i, j, ids: (j, i, 0))
    mine = pl.BlockSpec((None, rb, cols), lambda i, j, ids: (j, ids[1] * nb + i, 0))
    own = pl.BlockSpec((rb, cols), lambda i, j, ids: (i, 0))

    def body(ids_ref, a_ref, b_ref, s_ref, sb_ref):
        s = a_ref[...] + b_ref[...]
        sb_ref[...] = s.astype(BF16)

        @pl.when(pl.program_id(1) == ids_ref[0])
        def _():
            s_ref[...] = s

    return _prefetch_call(body, name, ids, (nb, N_CHIPS), [mine, blk], [own, blk],
                          [_sds((rows, cols), F32), _sds(got.shape, BF16)], (grad, got),
                          semantics=("parallel", "arbitrary"))


def _chip_sum(name, own_sum, got, ids):
    rows, cols = own_sum.shape
    rb = min(rows, 256)
    nb = rows // rb
    own = pl.BlockSpec((rb, cols), lambda i, ids: (i, 0))
    blk3 = pl.BlockSpec((N_PEERS, rb, cols), lambda i, ids: (0, i, 0))
    out = pl.BlockSpec((rb, cols), lambda i, ids: (ids[1] * nb + i, 0))

    def body(ids_ref, a_ref, b_ref, o_ref):
        o_ref[...] = ((a_ref[...] + b_ref[0].astype(F32)) + b_ref[1].astype(F32)) + b_ref[2].astype(F32)

    return _prefetch_call(body, name, ids, (nb,), [own, blk3], out, _sds((2 * rows, cols), F32), (own_sum, got))


SMALL_RB = 280


def _small_pair_sum(own, got):
    blk = pl.BlockSpec((SMALL_RB, 128), lambda i: (i, 0))

    def body(a_ref, b_ref, o_ref):
        o_ref[...] = a_ref[...] + b_ref[...]

    return pl.pallas_call(body, name="small_pair_sum", grid=(own.shape[0] // SMALL_RB,), in_specs=[blk, blk],
                          out_specs=blk, out_shape=_sds(own.shape, F32),
                          compiler_params=_params(("parallel",)))(own, got)


def _small_chip_sum(pair, got, ids):
    nb = pair.shape[0] // 2 // SMALL_RB
    half = pl.BlockSpec((SMALL_RB, 128), lambda i, ids: (ids[1] * nb + i, 0))
    blk3 = pl.BlockSpec((N_PEERS, SMALL_RB, 128), lambda i, ids: (0, i, 0))

    def body(ids_ref, a_ref, b_ref, o_ref):
        o_ref[...] = (a_ref[...] + b_ref[1]) + (b_ref[0] + b_ref[2])

    return _prefetch_call(body, "small_chip_sum", ids, (nb,), [half, blk3], half, _sds(pair.shape, F32), (pair, got))


def _adamw_math(w, g, m, v):
    m = ADAM_B1 * m + (1.0 - ADAM_B1) * g
    v = ADAM_B2 * v + (1.0 - ADAM_B2) * (g * g)
    m_hat = m / (1.0 - ADAM_B1 ** ADAM_STEP)
    v_hat = v / (1.0 - ADAM_B2 ** ADAM_STEP)
    delta = -ADAM_LR * (m_hat / (jnp.sqrt(v_hat) + ADAM_EPS) + ADAM_WD * w)
    return delta, m, v


def _adamw(name, w, g, m, v, rb=None):
    rows, cols = w.shape
    rb = rows if rb is None else rb
    blk = pl.BlockSpec((rb, cols), lambda i: (i, 0))

    def body(w_ref, g_ref, m_ref, v_ref, d_ref, nm_ref, nv_ref):
        d, nm, nv = _adamw_math(w_ref[...], g_ref[...], m_ref[...], v_ref[...])
        d_ref[...] = d
        nm_ref[...] = nm
        nv_ref[...] = nv

    return pl.pallas_call(body, name=name, grid=(rows // rb,), in_specs=[blk] * 4, out_specs=[blk] * 3,
                          out_shape=[_sds(w.shape, F32)] * 3, compiler_params=_params(("parallel",)))(w, g, m, v)


def _adamw_small(ws, gs, ms, vs):
    n = len(ws)

    def body(*refs):
        for t in range(n):
            w_ref, g_ref, m_ref, v_ref = (refs[k * n + t] for k in range(4))
            d, nm, nv = _adamw_math(w_ref[...], g_ref[...], m_ref[...], v_ref[...])
            for k, val in enumerate((d, nm, nv)):
                refs[(4 + k) * n + t][...] = val

    res = pl.pallas_call(body, name="adamw_small", out_shape=[_sds(a.shape, F32) for a in ws] * 3,
                         compiler_params=_params())(*ws, *gs, *ms, *vs)
    return [(res[t], res[n + t], res[2 * n + t]) for t in range(n)]


BIG = ("w_in", "w_att_o", "w_rec_o", "w_out", "w_ff1", "w_ff2")
SHARDED_VECS = ("conv_w", "b_rg_a", "b_rg_i", "lru_lambda")
SMALL = ("ln1_g", "b_in", "rpb", "conv_w", "conv_b", "w_rg_a", "b_rg_a", "w_rg_i", "b_rg_i", "lru_lambda",
         "ln2_g", "lnf_g")
SMALL_ROWS = 2240
ORDER = ("ln1_g", "w_in", "b_in", "rpb", "w_att_o", "conv_w", "conv_b", "w_rg_a", "b_rg_a", "w_rg_i", "b_rg_i",
         "lru_lambda", "w_rec_o", "w_out", "ln2_g", "w_ff1", "w_ff2", "lnf_g")


def _pack_small(grads, loss):
    parts, sizes = [], {}
    for n in SMALL:
        flat = grads[n].reshape(-1)
        pad = (-flat.shape[0]) % 128
        sizes[n] = (flat.shape[0], flat.shape[0] + pad)
        parts.append(jnp.pad(flat, (0, pad)))
    total = sum(s[1] for s in sizes.values())
    parts.append(jnp.pad(loss.reshape(1), (0, SMALL_ROWS * 128 - total - 1)))
    return jnp.concatenate(parts).reshape(SMALL_ROWS, 128), sizes


def _unpack_small(buf, sizes, shapes):
    flat = buf.reshape(-1)
    out, pos = {}, 0
    for n in SMALL:
        size, padded = sizes[n]
        out[n] = flat[pos:pos + size].reshape(shapes[n])
        pos += padded
    return out, flat[pos]


def _gather_weights(w, chip):
    chip_id = chip.astype(jnp.int32).reshape(1)
    vec_rows = [w[n][0] for n in SHARDED_VECS]
    vec_shard = jnp.concatenate(vec_rows + [jnp.zeros((16 - 10, D // N_CHIPS), F32)], axis=0)
    vec_slots = lax.dynamic_update_slice(jnp.zeros((N_CHIPS, 16, D // N_CHIPS), F32), vec_shard[None], (chip, 0, 0))
    bufs_a, sems_a, token_a = _gather_start("gather_start_first", [_cast_bf16("cast_w_in", w["w_in"][0], chip_id), vec_slots])
    rest_names = BIG[1:]
    bufs_b, sems_b, token_b = _gather_start(
        "gather_start_rest", [_cast_bf16("cast_" + n, w[n][0], chip_id, after=(token_a,)) for n in rest_names])

    def first(after):
        w_in_full, vec_full = _gather_forward("gather_forward_first", _gather_wait("gather_wait_first", bufs_a, sems_a, after))
        vecs = vec_full.transpose(1, 0, 2).reshape(16, D)
        return dict(w_in=w_in_full, conv_w=vecs[0:4], b_rg_a=vecs[4:6], b_rg_i=vecs[6:8], lru_lambda=vecs[8:10])

    def rest(after):
        full = dict(zip(rest_names, _gather_forward("gather_forward_rest",
                                                    _gather_wait("gather_wait_rest", bufs_b, sems_b, after))))
        return dict(w_att_o=full["w_att_o"], w_ff1=full["w_ff1"], w_rec_o=full["w_rec_o"].reshape(D, D),
                    w_out=full["w_out"].reshape(D, D), w_ff2=full["w_ff2"].reshape(D_FF, D))

    p = dict(ln1_g=w["ln1_g"], b_in=w["b_in"], rpb=w["rpb"][0], conv_b=w["conv_b"], w_rg_a=w["w_rg_a"][0],
             w_rg_i=w["w_rg_i"][0], ln2_g=w["ln2_g"], lnf_g=w["lnf_g"].reshape(1, D))
    return p, ((token_b,), first, rest)


class _Reducer:
    def __init__(self, ids):
        self.ids = ids
        self.groups = {}

    def begin(self, tag, grads, small=None):
        names = list(grads)
        big = [grads[n].reshape(N_CHIPS, -1, grads[n].shape[-1]) for n in names]
        flight, token = _pair_start("pair_start_" + tag, big, [] if small is None else [small])
        self.groups[tag] = dict(names=names, pair=flight, small=small is not None)
        return (token,)

    def advance(self, tag, after):
        grp = self.groups[tag]
        n = len(grp["names"])
        mine, got = _pair_wait("pair_wait_" + tag, grp["pair"], n, after)
        sums = [_pair_sum("pair_sum_" + name, a, b, self.ids) for name, a, b in zip(grp["names"], mine, got)]
        small_sum = _small_pair_sum(mine[n], got[n]) if grp["small"] else None
        grp["chip"], token = _chip_start("chip_start_" + tag, [s[1] for s in sums], small_sum)
        grp["sums"] = [s[0] for s in sums]
        self.last_token = token
        return (token,)

    def finish(self, tag, after):
        grp = self.groups[tag]
        srcs, lands = _chip_wait("chip_wait_" + tag, grp["chip"], grp["small"], after)
        halves = [_chip_sum("chip_sum_" + name, s, b, self.ids) for name, s, b in zip(grp["names"], grp["sums"], lands)]
        if grp["small"]:
            halves.append(_small_chip_sum(srcs[-1], lands[-1], self.ids))
        grp["swap"], token = _swap_start("swap_start_" + tag, halves)
        return token

    def interlude(self, tokens):
        after = tokens[0]
        for tag in list(self.groups)[:-1]:
            after = self.finish(tag, after)
        return after

    def result(self, tag, after):
        return _swap_wait("swap_wait_" + tag, self.groups[tag]["swap"], after)


def kernel(x, ln1_g, w_in, b_in, rpb, w_att_o, conv_w, conv_b, w_rg_a, b_rg_a, w_rg_i, b_rg_i, lru_lambda, w_rec_o, w_out, ln2_g, w_ff1, w_ff2, lnf_g, loss_target, m_ln1_g, m_w_in, m_b_in, m_rpb, m_w_att_o, m_conv_w, m_conv_b, m_w_rg_a, m_b_rg_a, m_w_rg_i, m_b_rg_i, m_lru_lambda, m_w_rec_o, m_w_out, m_ln2_g, m_w_ff1, m_w_ff2, m_lnf_g, v_ln1_g, v_w_in, v_b_in, v_rpb, v_w_att_o, v_conv_w, v_conv_b, v_w_rg_a, v_b_rg_a, v_w_rg_i, v_b_rg_i, v_lru_lambda, v_w_rec_o, v_w_out, v_ln2_g, v_w_ff1, v_w_ff2, v_lnf_g):
    w = dict(ln1_g=ln1_g, w_in=w_in, b_in=b_in, rpb=rpb, w_att_o=w_att_o, conv_w=conv_w, conv_b=conv_b,
             w_rg_a=w_rg_a, b_rg_a=b_rg_a, w_rg_i=w_rg_i, b_rg_i=b_rg_i, lru_lambda=lru_lambda, w_rec_o=w_rec_o,
             w_out=w_out, ln2_g=ln2_g, w_ff1=w_ff1, w_ff2=w_ff2, lnf_g=lnf_g)
    m = dict(ln1_g=m_ln1_g, w_in=m_w_in, b_in=m_b_in, rpb=m_rpb, w_att_o=m_w_att_o, conv_w=m_conv_w,
             conv_b=m_conv_b, w_rg_a=m_w_rg_a, b_rg_a=m_b_rg_a, w_rg_i=m_w_rg_i, b_rg_i=m_b_rg_i,
             lru_lambda=m_lru_lambda, w_rec_o=m_w_rec_o, w_out=m_w_out, ln2_g=m_ln2_g, w_ff1=m_w_ff1,
             w_ff2=m_w_ff2, lnf_g=m_lnf_g)
    v = dict(ln1_g=v_ln1_g, w_in=v_w_in, b_in=v_b_in, rpb=v_rpb, w_att_o=v_w_att_o, conv_w=v_conv_w,
             conv_b=v_conv_b, w_rg_a=v_w_rg_a, b_rg_a=v_b_rg_a, w_rg_i=v_w_rg_i, b_rg_i=v_b_rg_i,
             lru_lambda=v_lru_lambda, w_rec_o=v_w_rec_o, w_out=v_w_out, ln2_g=v_ln2_g, w_ff1=v_w_ff1,
             w_ff2=v_w_ff2, lnf_g=v_lnf_g)
    chip = 2 * lax.axis_index("x") + lax.axis_index("y")
    ids = jnp.stack([chip, lax.axis_index("c")]).astype(jnp.int32)

    out_grad, out_delta, out_m, out_v = {}, {}, {}, {}

    def update(n, gn):
        shape, two_d = w[n].shape, gn.shape
        d, nm, nv = _adamw("adamw_" + n, w[n].reshape(two_d), gn, m[n].reshape(two_d), v[n].reshape(two_d), 256)
        out_grad[n], out_delta[n], out_m[n], out_v[n] = (gn.reshape(shape), d.reshape(shape), nm.reshape(shape),
                                                         nv.reshape(shape))
        return d

    reducer = _Reducer(ids)
    p, late = _gather_weights(w, chip)
    loss, grad_x, g = _local_step(x, loss_target, p, late, reducer)
    small, sizes = _pack_small(g, loss + reducer.last_token[:1, :1])
    after = reducer.begin("small", {}, small)[0]
    for tag in ("ff", "proj", "in"):
        for n, red in zip(reducer.groups[tag]["names"], reducer.result(tag, after)):
            after = update(n, red)
        if tag == "ff":
            after = reducer.finish("in", reducer.advance("small", after)[0])
    (small_red,) = reducer.result("small", reducer.finish("small", after))
    gsmall, loss = _unpack_small(small_red, sizes, {n: g[n].shape for n in SMALL})
    two_d = {n: (int(np.prod(w[n].shape[:-1])), w[n].shape[-1]) for n in SMALL}
    for n in SHARDED_VECS:
        gsmall[n] = lax.dynamic_slice_in_dim(gsmall[n], chip * (D // N_CHIPS), D // N_CHIPS, axis=1)
    gs = [gsmall[n].reshape(two_d[n]) for n in SMALL]
    updates = _adamw_small([w[n].reshape(two_d[n]) for n in SMALL], gs, [m[n].reshape(two_d[n]) for n in SMALL],
                           [v[n].reshape(two_d[n]) for n in SMALL])
    for n, gn, (d, nm, nv) in zip(SMALL, gs, updates):
        shape = w[n].shape
        out_grad[n], out_delta[n], out_m[n], out_v[n] = (gn.reshape(shape), d.reshape(shape), nm.reshape(shape),
                                                         nv.reshape(shape))
    return (loss, grad_x, *[out_grad[n] for n in ORDER], *[out_delta[n] for n in ORDER],
            *[out_m[n] for n in ORDER], *[out_v[n] for n in ORDER])
```

```python
import numpy as np
import jax
import jax.numpy as jnp
from jax import lax
from jax.experimental import pallas as pl
from jax.experimental.pallas import tpu as pltpu

F32 = jnp.float32
BF16 = jnp.bfloat16

T = 2048
D = 1024
D_ATT = 512
D_IN = 5632
D_FF = 4096
N_HEADS = 8
HEAD_DIM = 64
GRID_W = 64
N_ROWS = T // GRID_W
WIN_H = 8
WIN_W = 16
KEYS = WIN_H * GRID_W
N_CHIPS = 4
EPS = 1e-6
LRU_C = 8.0
SCALE = HEAD_DIM ** -0.5
REC_CB = 256
REC_BLOCK = 64
REC_CHUNK = 256
PAD = 8

ADAM_LR = 0.001
ADAM_B1 = 0.9
ADAM_B2 = 0.999
ADAM_EPS = 1e-08
ADAM_WD = 0.01
ADAM_STEP = 10

VMEM_LIMIT = 56 * 1024 * 1024

NN = (((1,), (0,)), ((), ()))
NT = (((1,), (1,)), ((), ()))
TN = (((0,), (0,)), ((), ()))
MESH = pl.DeviceIdType.MESH


def _params(sem=None):
    return pltpu.CompilerParams(dimension_semantics=sem, vmem_limit_bytes=VMEM_LIMIT)


def _dot(a, b, dims):
    return lax.dot_general(a, b, dims, preferred_element_type=F32)


def _sigmoid(x):
    return 0.5 * jnp.tanh(0.5 * x) + 0.5


def _matmul(name, a, b, *, dims, grid, a_spec, b_spec, out_shapes, out_specs, acc_shape,
            extras=(), extra_specs=(), epilogue=None, colsum_spec=None, colsum_shape=None, after=(),
            semantics=("parallel", "parallel", "arbitrary"), epilogue_takes_first=False):
    nk = grid[2]
    n_extra = len(extras)
    n_out = len(out_shapes)
    with_colsum = colsum_spec is not None

    def body(a_ref, b_ref, *rest):
        ex = rest[:n_extra]
        rest = rest[:n_extra] + rest[n_extra + len(after):]
        outs = rest[n_extra:n_extra + n_out]
        pos = n_extra + n_out
        cs_out = rest[pos] if with_colsum else None
        pos += 1 if with_colsum else 0
        acc = rest[pos]
        cs_acc = rest[pos + 1] if with_colsum else None
        k = pl.program_id(2)
        first_tile = pl.program_id(0) == 0

        @pl.when(k == 0)
        def _():
            acc[...] = jnp.zeros_like(acc)
            if with_colsum:
                cs_acc[...] = jnp.zeros_like(cs_acc)

        bv = b_ref[...]
        acc[...] += _dot(a_ref[...].astype(BF16), bv.astype(BF16), dims)
        if with_colsum:
            cs_acc[...] += jnp.sum(bv.astype(F32), axis=0, keepdims=True)

        @pl.when(k == nk - 1)
        def _():
            r = acc[...]
            if epilogue is None:
                outs[0][...] = r.astype(outs[0].dtype)
            elif epilogue_takes_first:
                epilogue(r, ex, outs, first_tile)
            else:
                epilogue(r, ex, outs)
            if with_colsum:
                cs_out[...] = cs_acc[...]

    shapes = list(out_shapes)
    specs = list(out_specs)
    scratch = [pltpu.VMEM(acc_shape, F32)]
    if with_colsum:
        shapes.append(colsum_shape)
        specs.append(colsum_spec)
        scratch.append(pltpu.VMEM((1, acc_shape[1]), F32))
    res = pl.pallas_call(
        body, name=name, grid=grid,
        in_specs=[a_spec, b_spec, *extra_specs] + [_ANY] * len(after),
        out_specs=specs, out_shape=shapes, scratch_shapes=scratch,
        compiler_params=_params(semantics),
    )(a, b, *extras, *after)
    return res


def _sds(shape, dtype):
    return jax.ShapeDtypeStruct(shape, dtype)


TM = 1024
NI = T // TM
TJ = T
NJ = T // TJ


def _mm_nn_cols(name, a, wg, out_dtype, *, bias=None, extras=(), extra_specs=(), epilogue=None,
                out_shapes=None, out_specs=None):
    k_dim, n4 = wg.shape[1], wg.shape[2]
    ex, exs = list(extras), list(extra_specs)
    if bias is not None:
        ex = [bias] + ex
        exs = [pl.BlockSpec((1, n4), lambda j, i, k: (0, j))] + exs
        user_ep = epilogue

        def epilogue(r, e, outs):
            r = r + e[0][...]
            if user_ep is None:
                outs[0][...] = r.astype(outs[0].dtype)
            else:
                user_ep(r, e[1:], outs)
    if out_shapes is None:
        out_shapes = [_sds((T, N_CHIPS * n4), out_dtype)]
        out_specs = [pl.BlockSpec((TJ, n4), lambda j, i, k: (i, j))]
    return _matmul(
        name, a, wg, dims=NN, grid=(N_CHIPS, NJ, 1),
        a_spec=pl.BlockSpec((TJ, k_dim), lambda j, i, k: (i, 0)),
        b_spec=pl.BlockSpec((None, k_dim, n4), lambda j, i, k: (j, 0, 0)),
        out_shapes=out_shapes, out_specs=out_specs, acc_shape=(TJ, n4),
        extras=ex, extra_specs=exs, epilogue=epilogue)


def _mm_nt_cols_rms_bwd(name, a, wg, x, g, dres, after=(), bf16_copy=False):
    n4 = wg.shape[2]
    row = pl.BlockSpec((TM, D), lambda i, j, k: (i, 0))
    vec = pl.BlockSpec((1, D), lambda i, j, k: (0, 0))

    def epilogue(dhv, ex, outs, first):
        x_ref, g_ref, dres_ref = ex
        dx_ref, dg_ref = outs[0], outs[-1]
        xv = x_ref[...]
        rstd = lax.rsqrt(jnp.mean(xv * xv, axis=-1, keepdims=True) + EPS)
        xhat = xv * rstd
        dy = dhv * g_ref[...]
        dx = dres_ref[...] + rstd * (dy - xhat * jnp.mean(dy * xhat, axis=-1, keepdims=True))
        dx_ref[...] = dx
        if bf16_copy:
            outs[1][...] = dx.astype(BF16)
        part = jnp.sum(dhv * xhat, axis=0, keepdims=True)

        @pl.when(first)
        def _():
            dg_ref[...] = part

        @pl.when(jnp.logical_not(first))
        def _():
            dg_ref[...] += part

    return _matmul(
        name, a, wg, dims=NT, grid=(NI, 1, N_CHIPS),
        a_spec=pl.BlockSpec((TM, n4), lambda i, j, k: (i, k)),
        b_spec=pl.BlockSpec((None, D, n4), lambda i, j, k: (k, 0, 0)),
        out_shapes=[_sds((T, D), F32)] + [_sds((T, D), BF16)] * bf16_copy + [_sds((1, D), F32)],
        out_specs=[row] + [row] * bf16_copy + [vec], acc_shape=(TM, D),
        extras=[x, g, dres], extra_specs=[row, vec, row], epilogue=epilogue, after=after,
        semantics=("arbitrary", "arbitrary", "arbitrary"), epilogue_takes_first=True)


def _mm_nt_rows(name, a, w, out_dtype, *, tn, extras=(), extra_specs=(), epilogue=None):
    k_dim, n = w.shape
    return _matmul(
        name, a, w, dims=NT, grid=(k_dim // tn, NJ, 1),
        a_spec=pl.BlockSpec((TJ, n), lambda j, i, k: (i, 0)),
        b_spec=pl.BlockSpec((tn, n), lambda j, i, k: (j, 0)),
        out_shapes=[_sds((T, k_dim), out_dtype)],
        out_specs=[pl.BlockSpec((TJ, tn), lambda j, i, k: (i, j))], acc_shape=(TJ, tn),
        extras=extras, extra_specs=extra_specs, epilogue=epilogue)


def _mm_tn_cols(name, a, g, n4, *, colsum=False):
    k_dim = a.shape[1]
    kw = {}
    if colsum:
        kw = dict(colsum_spec=pl.BlockSpec((1, n4), lambda j, i, k: (0, j)),
                  colsum_shape=_sds((1, N_CHIPS * n4), F32))
    return _matmul(
        name, a, g, dims=TN, grid=(N_CHIPS, 1, NJ),
        a_spec=pl.BlockSpec((TJ, k_dim), lambda j, i, k: (k, 0)),
        b_spec=pl.BlockSpec((TJ, n4), lambda j, i, k: (k, j)),
        out_shapes=[_sds((N_CHIPS, k_dim, n4), F32)],
        out_specs=[pl.BlockSpec((None, k_dim, n4), lambda j, i, k: (j, 0, 0))],
        acc_shape=(k_dim, n4), **kw)


def _mm_tn_rows(name, a, g, *, tm):
    k_dim, n = a.shape[1], g.shape[1]
    return _matmul(
        name, a, g, dims=TN, grid=(k_dim // tm, 1, NJ),
        a_spec=pl.BlockSpec((TJ, tm), lambda j, i, k: (k, j)),
        b_spec=pl.BlockSpec((TJ, n), lambda j, i, k: (k, 0)),
        out_shapes=[_sds((k_dim, n), F32)],
        out_specs=[pl.BlockSpec((tm, n), lambda j, i, k: (j, 0))], acc_shape=(tm, n))


TE = 256
NE = T // TE
_ROW = pl.BlockSpec((TE, D), lambda i: (i, 0))
_VEC = pl.BlockSpec((1, D), lambda i: (0, 0))


def _rms_fwd(name, x, g, after=()):
    def body(x_ref, g_ref, *rest):
        h_ref = rest[-1]
        xv = x_ref[...]
        rstd = lax.rsqrt(jnp.mean(xv * xv, axis=-1, keepdims=True) + EPS)
        h_ref[...] = (xv * rstd * g_ref[...]).astype(BF16)

    return pl.pallas_call(body, name=name, grid=(NE,), in_specs=[_ROW, _VEC] + [_ANY] * len(after), out_specs=_ROW,
                          out_shape=_sds((T, D), BF16), compiler_params=_params(("parallel",)))(x, g, *after)


def _mm_x2_loss_head(s, w_ff2, x1, target, g):
    k_dim = w_ff2.shape[0]
    row = pl.BlockSpec((TM, D), lambda i, j, k: (i, 0))
    vec = pl.BlockSpec((1, D), lambda i, j, k: (0, 0))

    def epilogue(r, ex, outs, first):
        x1_ref, t_ref, g_ref = ex
        loss_ref, dx_ref, dxb_ref, dg_ref = outs
        xv = x1_ref[...] + r
        rstd = lax.rsqrt(jnp.mean(xv * xv, axis=-1, keepdims=True) + EPS)
        xhat = xv * rstd
        gv = g_ref[...]
        err = xhat * gv - t_ref[...]
        dy = err * (1.0 / D)
        dxh = dy * gv
        dx = rstd * (dxh - xhat * jnp.mean(dxh * xhat, axis=-1, keepdims=True))
        dx_ref[...] = dx
        dxb_ref[...] = dx.astype(BF16)
        dg_part = jnp.sum(dy * xhat, axis=0, keepdims=True)
        loss_part = (0.5 / D) * jnp.sum(jnp.sum(err * err, axis=1, keepdims=True), axis=0, keepdims=True)

        @pl.when(first)
        def _():
            dg_ref[...] = dg_part
            loss_ref[...] = loss_part

        @pl.when(jnp.logical_not(first))
        def _():
            dg_ref[...] += dg_part
            loss_ref[...] += loss_part

    return _matmul(
        "mm_x2_loss_head", s, w_ff2, dims=NN, grid=(NI, 1, k_dim // D),
        a_spec=pl.BlockSpec((TM, D), lambda i, j, k: (i, k)), b_spec=pl.BlockSpec((D, D), lambda i, j, k: (k, 0)),
        out_shapes=[_sds((1, 1), F32), _sds((T, D), F32), _sds((T, D), BF16), _sds((1, D), F32)],
        out_specs=[pl.BlockSpec((1, 1), lambda i, j, k: (0, 0)), row, row, vec], acc_shape=(TM, D),
        extras=[x1, target, g], extra_specs=[row, row, vec], epilogue=epilogue,
        semantics=("arbitrary", "arbitrary", "arbitrary"), epilogue_takes_first=True)


MW = 512
_G_ATT_BLK = 3584 // MW
_G_REC_BLK = 4608 // MW


TB = 512


def _branch_specs():
    def row(cols):
        return pl.BlockSpec((TB, cols), lambda i: (i, 0))

    ga = pl.BlockSpec((TB, MW), lambda i: (i, _G_ATT_BLK))
    ga2 = pl.BlockSpec((TB, MW), lambda i: (i, _G_ATT_BLK + 1))
    gr = pl.BlockSpec((TB, MW), lambda i: (i, _G_REC_BLK))
    gr2 = pl.BlockSpec((TB, MW), lambda i: (i, _G_REC_BLK + 1))
    w_att = pl.BlockSpec((N_CHIPS, D_ATT, D // N_CHIPS), lambda i: (0, 0, 0))
    w_sq = pl.BlockSpec((D, D), lambda i: (0, 0))
    return row, (ga, ga2, gr, gr2), w_att, w_sq


def _gate_values(gate_refs):
    ga, ga2, gr, gr2 = (r[...].astype(F32) for r in gate_refs)
    return _sigmoid(jnp.concatenate([ga, ga2], axis=1)), _sigmoid(jnp.concatenate([gr, gr2], axis=1))


def _branches_fwd(att, g, z, x, w_att_o, w_rec_o, w_out, ln2_g):
    row, gate_specs, w_att, w_sq = _branch_specs()

    def body(att_ref, g_ref, ga_ref, ga2_ref, gr_ref, gr2_ref, x_ref, wa_ref, wr_ref, wo_ref, g2_ref,
             ya_ref, yr_ref, m_ref, x1_ref, h2_ref):
        attv = att_ref[...]
        ya = jnp.concatenate([_dot(attv, wa_ref[j], NN) for j in range(N_CHIPS)], axis=1)
        yr = _dot(g_ref[...], wr_ref[...], NN)
        sa, sr = _gate_values((ga_ref, ga2_ref, gr_ref, gr2_ref))
        mixed = (sa * ya + sr * yr).astype(BF16)
        ya_ref[...] = ya
        yr_ref[...] = yr
        m_ref[...] = mixed
        x1 = x_ref[...] + _dot(mixed, wo_ref[...], NN)
        x1_ref[...] = x1
        rstd = lax.rsqrt(jnp.mean(x1 * x1, axis=-1, keepdims=True) + EPS)
        h2_ref[...] = (x1 * rstd * g2_ref[...]).astype(BF16)

    return pl.pallas_call(
        body, name="branches_fwd", grid=(T // TB,),
        in_specs=[row(D_ATT), row(D), *gate_specs, row(D), w_att, w_sq, w_sq, pl.BlockSpec((1, D), lambda i: (0, 0))],
        out_specs=[row(D)] * 5,
        out_shape=[_sds((T, D), F32), _sds((T, D), F32), _sds((T, D), BF16), _sds((T, D), F32), _sds((T, D), BF16)],
        compiler_params=_params(("parallel",)))(att, g, z, z, z, z, x, w_att_o, w_rec_o, w_out, ln2_g)


def _branches_bwd(dx1_b, y_att, y_rec, z, w_att_o, w_rec_o, w_out):
    row, gate_specs, w_att, w_sq = _branch_specs()
    n4 = D // N_CHIPS

    def body(dx_ref, ya_ref, yr_ref, ga_ref, ga2_ref, gr_ref, gr2_ref, wa_ref, wr_ref, wo_ref,
             dya_ref, dyr_ref, dga_ref, dgr_ref, datt_ref, dg_ref):
        dm = _dot(dx_ref[...], wo_ref[...], NT)
        sa, sr = _gate_values((ga_ref, ga2_ref, gr_ref, gr2_ref))
        dya = (dm * sa).astype(BF16)
        dyr = (dm * sr).astype(BF16)
        dya_ref[...] = dya
        dyr_ref[...] = dyr
        dga_ref[...] = (dm * ya_ref[...] * sa * (1.0 - sa)).astype(BF16)
        dgr_ref[...] = (dm * yr_ref[...] * sr * (1.0 - sr)).astype(BF16)
        datt = _dot(dya[:, 0:n4], wa_ref[0], NT)
        for j in range(1, N_CHIPS):
            datt = datt + _dot(dya[:, j * n4:(j + 1) * n4], wa_ref[j], NT)
        datt_ref[...] = datt.astype(BF16)
        dg_ref[...] = _dot(dyr, wr_ref[...], NT).astype(BF16)

    return pl.pallas_call(
        body, name="branches_bwd", grid=(T // TB,),
        in_specs=[row(D), row(D), row(D), *gate_specs, w_att, w_sq, w_sq],
        out_specs=[row(D)] * 4 + [row(D_ATT), row(D)],
        out_shape=[_sds((T, D), BF16)] * 4 + [_sds((T, D_ATT), BF16), _sds((T, D), BF16)],
        compiler_params=_params(("parallel",)))(dx1_b, y_att, y_rec, z, z, z, z, w_att_o, w_rec_o, w_out)


HP = 2 * HEAD_DIM
N_HP = N_HEADS // 2
ATT_UNROLL_FWD = 16
ATT_UNROLL_BWD = 8
DIAG_ROWS = 32


def _window_maps():
    diag = np.zeros((GRID_W * GRID_W, 128), np.float32)
    for qc in range(GRID_W):
        w0 = min(max(qc - WIN_W // 2, 0), GRID_W - WIN_W)
        for kc in range(w0, w0 + WIN_W):
            diag[qc * GRID_W + kc, kc - qc + WIN_W - 1] = 1.0
    return diag, diag.sum(axis=1)[None, :]


def _split3(x):
    a = x.astype(BF16)
    r = x - a.astype(F32)
    b = r.astype(BF16)
    c = (r - b.astype(F32)).astype(BF16)
    return a, b, c


N_DROW = 2 * WIN_H - 1
N_DPAIR = N_DROW - 1


def _bias_pairs(rpb):
    diag, valid = _window_maps()
    r2 = jnp.pad(rpb.reshape(N_HEADS * N_DROW, 2 * WIN_W - 1),
                 ((0, 128 - N_HEADS * N_DROW), (0, 128 - (2 * WIN_W - 1))))

    def body(r_ref, d_ref, v_ref, o_ref):
        dv = d_ref[...]
        t = sum(_dot(part, dv, NN) for part in _split3(r_ref[...]))
        o_ref[...] = jnp.where(v_ref[...] > 0.0, t, -1e30)

    t = pl.pallas_call(body, name="rpb_expand", out_shape=_sds((128, GRID_W * GRID_W), F32),
                       compiler_params=_params())(r2, jnp.asarray(diag.T, BF16), jnp.asarray(valid, F32))
    t = t[:N_HEADS * N_DROW].reshape(N_HEADS, N_DROW, GRID_W, GRID_W)
    return jnp.concatenate([t[:, :N_DPAIR], t[:, 1:]], axis=-1)


def _row_bias(tb_ref, hh, d0):
    return jnp.concatenate([tb_ref[hh, d0 + 2 * ii] for ii in range(WIN_H // 2)], axis=1)


def _row_window(r):
    rs = jnp.clip(r - WIN_H // 2, 0, N_ROWS - WIN_H)
    return pl.multiple_of(r * GRID_W, GRID_W), pl.multiple_of(rs * GRID_W, GRID_W), rs - r + (WIN_H - 1)


def _split_heads(src_ref, dst_ref, scale=None):
    for hh in range(2):
        v = src_ref[:, hh * HEAD_DIM:(hh + 1) * HEAD_DIM]
        dst_ref[hh] = (v if scale is None else v * scale).astype(BF16)


def _attn_items(qb_ref, kb_ref, vb_ref, tb_ref, first_row, n_rows):
    wins = [_row_window(first_row + u) for u in range(n_rows)]
    items = [(u, hh) for u in range(n_rows) for hh in range(2)]
    q = [qb_ref[hh, pl.ds(wins[u][0], GRID_W), :] for u, hh in items]
    k = [kb_ref[hh, pl.ds(wins[u][1], KEYS), :] for u, hh in items]
    v = [vb_ref[hh, pl.ds(wins[u][1], KEYS), :] for u, hh in items]
    s = [_dot(qi, ki, NT) + _row_bias(tb_ref, hh, wins[u][2]) for qi, ki, (u, hh) in zip(q, k, items)]
    m = [jnp.max(si, axis=-1, keepdims=True) for si in s]
    e = [jnp.exp(si - mi) for si, mi in zip(s, m)]
    inv = [1.0 / jnp.sum(ei, axis=-1, keepdims=True) for ei in e]
    p = [ei * li for ei, li in zip(e, inv)]
    return wins, items, q, k, v, p


def _attn_in_specs():
    q = pl.BlockSpec((T, HP), lambda p: (0, p))
    k = pl.BlockSpec((T, HP), lambda p: (0, N_HP + p))
    v = pl.BlockSpec((T, HP), lambda p: (0, 2 * N_HP + p))
    tb = pl.BlockSpec((2, N_DPAIR, GRID_W, HP), lambda p: (p, 0, 0, 0))
    return q, k, v, tb


_HEAD_SCRATCH = pltpu.VMEM((2, T, HEAD_DIM), BF16)


_PROBS = pl.BlockSpec((T, 2 * KEYS), lambda p: (0, p))


def _attn_fwd(z, tb):
    def body(q_ref, k_ref, v_ref, tb_ref, o_ref, p_ref, qb_ref, kb_ref, vb_ref):
        _split_heads(q_ref, qb_ref, SCALE)
        _split_heads(k_ref, kb_ref)
        _split_heads(v_ref, vb_ref)

        def rows(it, carry):
            wins, items, _, _, v, p = _attn_items(qb_ref, kb_ref, vb_ref, tb_ref, it * ATT_UNROLL_FWD, ATT_UNROLL_FWD)
            pb = [pi.astype(BF16) for pi in p]
            o = [_dot(pi, vi, NN) for pi, vi in zip(pb, v)]
            for u, (q0, _, _) in enumerate(wins):
                o_ref[pl.ds(q0, GRID_W), :] = jnp.concatenate(o[2 * u:2 * u + 2], axis=1).astype(BF16)
                p_ref[pl.ds(q0, GRID_W), :] = jnp.concatenate(pb[2 * u:2 * u + 2], axis=1)
            return carry

        lax.fori_loop(0, N_ROWS // ATT_UNROLL_FWD, rows, 0)

    blk = pl.BlockSpec((T, HP), lambda p: (0, p))
    return pl.pallas_call(
        body, name="attn_fwd", grid=(N_HP,), in_specs=list(_attn_in_specs()), out_specs=[blk, _PROBS],
        out_shape=[_sds((T, D_ATT), BF16), _sds((T, N_HEADS * KEYS), BF16)], scratch_shapes=[_HEAD_SCRATCH] * 3,
        compiler_params=_params(("parallel",)))(z, z, z, tb)


def _attn_bwd(z, probs, d_att, after=()):
    def body(q_ref, k_ref, v_ref, p_ref, do_ref, flip_ref, *rest):
        (dq_ref, dk_ref, dv_ref, diag_ref, qb_ref, kb_ref, vb_ref, dob_ref, dka_ref, dva_ref,
         ds_ref) = rest[len(after):]
        _split_heads(q_ref, qb_ref, SCALE)
        _split_heads(k_ref, kb_ref)
        _split_heads(v_ref, vb_ref)
        _split_heads(do_ref, dob_ref)
        dka_ref[...] = jnp.zeros_like(dka_ref)
        dva_ref[...] = jnp.zeros_like(dva_ref)
        ds_ref[...] = jnp.zeros_like(ds_ref)

        def rows(it, carry):
            wins = [_row_window(it * ATT_UNROLL_BWD + u) for u in range(ATT_UNROLL_BWD)]
            items = [(u, hh) for u in range(ATT_UNROLL_BWD) for hh in range(2)]
            q = [qb_ref[hh, pl.ds(wins[u][0], GRID_W), :] for u, hh in items]
            k = [kb_ref[hh, pl.ds(wins[u][1], KEYS), :] for u, hh in items]
            v = [vb_ref[hh, pl.ds(wins[u][1], KEYS), :] for u, hh in items]
            pb = [p_ref[pl.ds(wins[u][0], GRID_W), hh * KEYS:(hh + 1) * KEYS] for u, hh in items]
            p = [pi.astype(F32) for pi in pb]
            do = [dob_ref[hh, pl.ds(wins[u][0], GRID_W), :] for u, hh in items]
            dv = [_dot(pi, di, TN) for pi, di in zip(pb, do)]
            dp = [_dot(di, vi, NT) for di, vi in zip(do, v)]
            ds = [pi * (dpi - jnp.sum(dpi * pi, axis=-1, keepdims=True)) for pi, dpi in zip(p, dp)]
            dsb = [d.astype(BF16) for d in ds]
            dq = [_dot(d, ki, NN) * SCALE for d, ki in zip(dsb, k)]
            dk = [_dot(d, qi, TN) for d, qi in zip(dsb, q)]
            for d, (u, hh) in zip(ds, items):
                for ii in range(WIN_H // 2):
                    ds_ref[hh, wins[u][2] + 2 * ii] += d[:, ii * HP:(ii + 1) * HP]
            for u, (q0, k0, _) in enumerate(wins):
                dq_ref[pl.ds(q0, GRID_W), :] = jnp.concatenate(dq[2 * u:2 * u + 2], axis=1).astype(BF16)
                dka_ref[pl.ds(k0, KEYS), :] += jnp.concatenate(dk[2 * u:2 * u + 2], axis=1)
                dva_ref[pl.ds(k0, KEYS), :] += jnp.concatenate(dv[2 * u:2 * u + 2], axis=1)
            return carry

        lax.fori_loop(0, N_ROWS // ATT_UNROLL_BWD, rows, 0)
        dk_ref[...] = dka_ref[...].astype(BF16)
        dv_ref[...] = dva_ref[...].astype(BF16)
        _diag_sums(ds_ref, flip_ref, diag_ref)

    blk = pl.BlockSpec((T, HP), lambda p: (0, p))
    q, k, v, _ = _attn_in_specs()
    flip =jnp.asarray(np.eye(HP, dtype=np.float32)[::-1], BF16)
    return pl.pallas_call(
        body, name="attn_bwd", grid=(N_HP,),
        in_specs=[q, k, v, _PROBS, blk, pl.BlockSpec((HP, HP), lambda p: (0, 0))] + [_ANY] * len(after),
        out_specs=[blk, blk, blk, pl.BlockSpec((None, DIAG_ROWS, HP), lambda p: (p, 0, 0))],
        out_shape=[_sds((T, D_ATT), BF16)] * 3 + [_sds((N_HP, DIAG_ROWS, HP), F32)],
        scratch_shapes=[_HEAD_SCRATCH] * 4 + [pltpu.VMEM((T, HP), F32), pltpu.VMEM((T, HP), F32),
                                              pltpu.VMEM((2, N_DPAIR, GRID_W, HP), F32)],
        compiler_params=_params(("parallel",)))(z, z, z, probs, d_att, flip, *after)


def _diag_sums(acc_ref, flip_ref, out_ref):
    flip = flip_ref[...]
    rows = []
    for hh in range(2):
        for pair in range(N_DPAIR):
            reversed_lanes = sum(_dot(part, flip, NN) for part in _split3(acc_ref[hh, pair]))
            skewed = pltpu.roll(reversed_lanes, 0, 1, stride=1, stride_axis=0)
            rows.append(jnp.sum(skewed, axis=0, keepdims=True))
    rows.append(jnp.zeros((DIAG_ROWS - len(rows), HP), F32))
    out_ref[...] = jnp.concatenate(rows, axis=0)


def _rpb_grad(diag_sums):
    g = diag_sums.reshape(N_HP * DIAG_ROWS, HP)
    sel = np.zeros((2, 128, N_HP * DIAG_ROWS), np.float32)
    lane = np.zeros((2, HP, 128), np.float32)
    for h in range(N_HEADS):
        for pair in range(N_DPAIR):
            for half in range(2):
                sel[half, h * N_DROW + pair + half, (h // 2) * DIAG_ROWS + (h % 2) * N_DPAIR + pair] = 1.0
    for j in range(2 * WIN_W - 1):
        for half in range(2):
            lane[half, (HP - 1 - GRID_W * half - (j - (WIN_W - 1))) % HP, j] = 1.0

    def body(g_ref, sel_ref, lane_ref, o_ref):
        parts = _split3(g_ref[...])
        total = None
        for half in range(2):
            picked = sum(_dot(sel_ref[half], part, NN) for part in parts)
            term = sum(_dot(part, lane_ref[half], NN) for part in _split3(picked))
            total = term if total is None else total + term
        o_ref[...] = total

    out = pl.pallas_call(body, name="rpb_grad", out_shape=_sds((128, 128), F32),
                         compiler_params=_params())(g, jnp.asarray(sel, BF16), jnp.asarray(lane, BF16))
    return out[:N_HEADS * N_DROW, :2 * WIN_W - 1].reshape(N_HEADS, N_DROW, 2 * WIN_W - 1)


N_CB = D // REC_CB
N_CHUNK = T // REC_CHUNK
N_TILE = T // 8
_U_BLK = 1536 // REC_CB
_Y_BLK = 2560 // REC_CB


def _block_diag(w):
    per = REC_CB // 64
    wt = w.reshape(2, N_CB, per, 64, 64)
    eye = jnp.eye(per, dtype=w.dtype)
    full = wt[:, :, :, :, None, :] * eye[None, None, :, None, :, None]
    return full.reshape(2, N_CB, REC_CB, REC_CB).astype(BF16)


def _gelu(x):
    c = 0.7978845608028654
    return 0.5 * x * (1.0 + jnp.tanh(c * (x + 0.044715 * x * x * x)))


def _gelu_grad(x):
    c = 0.7978845608028654
    th = jnp.tanh(c * (x + 0.044715 * x * x * x))
    return 0.5 * (1.0 + th) + 0.5 * x * (1.0 - th * th) * c * (1.0 + 3.0 * 0.044715 * x * x)


def _softplus_neg(lam):
    x = -lam
    e = jnp.exp(-jnp.abs(x))
    w = 1.0 + e
    l1p = jnp.where(w == 1.0, e, jnp.log(w) * e / (w - 1.0))
    return jnp.maximum(x, 0.0) + l1p


def _one_minus_exp(x):
    poly = x * (1.0 + x * (1 / 2 + x * (1 / 6 + x * (1 / 24 + x * (1 / 120 + x * (1 / 720))))))
    return jnp.where(x > -0.125, -poly, 1.0 - jnp.exp(x))


def _conv_taps(pad_ref, t0, w, sign):
    out = None
    for j in range(4):
        term = w[j:j + 1, :] * pad_ref[pl.ds(PAD + t0 + sign * (j - 2), REC_CHUNK), :]
        out = term if out is None else out + term
    return out


def _gates(u, wa, wi, ba, bi, sp):
    ub = u.astype(BF16)
    r = _sigmoid(_dot(ub, wa, NN) + ba)
    i = _sigmoid(_dot(ub, wi, NN) + bi)
    log_a = -LRU_C * r * sp
    a = jnp.exp(log_a)
    x = jnp.maximum(_one_minus_exp(2.0 * log_a), 0.0)
    positive = x > 0.0
    inv = lax.rsqrt(jnp.where(positive, x, 1.0))
    mult = jnp.where(positive, x * inv, 0.0)
    return r, i, a, mult, jnp.where(positive, inv, 0.0)


def _tile_scan(a, b, sub, reverse):
    for s in (1, 2, 4):
        if reverse:
            a_s, b_s, m = pltpu.roll(a, 8 - s, 0), pltpu.roll(b, 8 - s, 0), sub < 8 - s
        else:
            a_s, b_s, m = pltpu.roll(a, s, 0), pltpu.roll(b, s, 0), sub >= s
        b = jnp.where(m, a * b_s + b, b)
        a = jnp.where(m, a * a_s, a)
    return a, b


def _last_row(x, row):
    return jnp.broadcast_to(x[row:row + 1, :], x.shape)


def _rec_prologue(up_ref, cw_ref, cb_ref, wa_ref, wi_ref, ba_ref, bi_ref, lam_ref,
                  upad_ref, u_ref, a_refs, h_refs):
    cb = up_ref.shape[1]
    zeros = jnp.zeros((PAD, cb), F32)
    upad_ref[pl.ds(0, PAD), :] = zeros
    upad_ref[pl.ds(PAD + T, PAD), :] = zeros
    upad_ref[pl.ds(PAD, T), :] = up_ref[...].astype(F32)
    cw = cw_ref[...]
    sp = _softplus_neg(lam_ref[...])
    for c in range(N_CHUNK):
        t0 = c * REC_CHUNK
        u = cb_ref[...] + _conv_taps(upad_ref, t0, cw, 1)
        u_ref[pl.ds(t0, REC_CHUNK), :] = u
        for d in range(2):
            _, i, a, mult, _ = _gates(u, wa_ref[d], wi_ref[d], ba_ref[d:d + 1, :], bi_ref[d:d + 1, :], sp[d:d + 1, :])
            a_refs[d][pl.ds(t0, REC_CHUNK), :] = a
            h_refs[d][pl.ds(t0, REC_CHUNK), :] = mult * (i * u)

    sub = lax.broadcasted_iota(jnp.int32, (8, cb), 0)

    def tile(k, carry):
        cf, cr = carry
        tf = pl.multiple_of(k * 8, 8)
        tr = pl.multiple_of((N_TILE - 1 - k) * 8, 8)
        af, bf = _tile_scan(a_refs[0][pl.ds(tf, 8), :], h_refs[0][pl.ds(tf, 8), :], sub, False)
        hf = af * cf + bf
        h_refs[0][pl.ds(tf, 8), :] = hf
        ar, br = _tile_scan(a_refs[1][pl.ds(tr, 8), :], h_refs[1][pl.ds(tr, 8), :], sub, True)
        hr = ar * cr + br
        h_refs[1][pl.ds(tr, 8), :] = hr
        return _last_row(af, 7) * cf + _last_row(bf, 7), _last_row(ar, 0) * cr + _last_row(br, 0)

    z8 = jnp.zeros((8, cb), F32)
    lax.fori_loop(0, N_TILE, tile, (z8, z8))
    return sp


def _rec_specs():
    up = pl.BlockSpec((T, REC_CB), lambda c: (0, _U_BLK + c))
    yb = pl.BlockSpec((T, REC_CB), lambda c: (0, _Y_BLK + c))
    cw = pl.BlockSpec((4, REC_CB), lambda c: (0, c))
    cbias = pl.BlockSpec((1, REC_CB), lambda c: (0, c))
    wbd = pl.BlockSpec((2, None, REC_CB, REC_CB), lambda c: (0, c, 0, 0))
    vec2 = pl.BlockSpec((2, REC_CB), lambda c: (0, c))
    col = pl.BlockSpec((T, REC_CB), lambda c: (0, c))
    return up, yb, cw, cbias, wbd, vec2, col


def _rec_fwd(z, conv_w, conv_b, wa, wi, ba, bi, lam):
    up, yb, cw, cbias, wbd, vec2, col = _rec_specs()

    def body(up_ref, yb_ref, cw_ref, cb_ref, wa_ref, wi_ref, ba_ref, bi_ref, lam_ref, g_ref,
             u_ref, af_ref, ar_ref, hf_ref, hr_ref, upad_ref):
        _rec_prologue(up_ref, cw_ref, cb_ref, wa_ref, wi_ref, ba_ref, bi_ref, lam_ref,
                      upad_ref, u_ref, (af_ref, ar_ref), (hf_ref, hr_ref))

        def chunk(c, carry):
            t0 = pl.multiple_of(c * REC_CHUNK, REC_CHUNK)
            rows = pl.ds(t0, REC_CHUNK)
            g_ref[rows, :] = ((hf_ref[rows, :] + hr_ref[rows, :]) * _gelu(yb_ref[rows, :].astype(F32))).astype(BF16)
            return carry

        lax.fori_loop(0, N_CHUNK, chunk, 0)

    res = pl.pallas_call(
        body, name="rec_fwd", grid=(N_CB,),
        in_specs=[up, yb, cw, cbias, wbd, wbd, vec2, vec2, vec2], out_specs=[col] * 6,
        out_shape=[_sds((T, D), BF16)] + [_sds((T, D), F32)] * 5,
        scratch_shapes=[pltpu.VMEM((T + 2 * PAD, REC_CB), F32)],
        compiler_params=_params(("parallel",)))(z, z, conv_w, conv_b, wa, wi, ba, bi, lam)
    return res[0], tuple(res[1:])


def _rec_bwd(z, dg, saved, conv_w, conv_b, wa, wi, ba, bi, lam, after=()):
    up, yb, cw, cbias, wbd, vec2, col = _rec_specs()

    def body(up_ref, yb_ref, dg_ref, u_ref, af_ref, ar_ref, hf_ref, hr_ref,
             cw_ref, cb_ref, wa_ref, wi_ref, ba_ref, bi_ref, lam_ref, *rest):
        (dup_ref, dyb_ref, dcw_ref, dcb_ref, dwa_out, dwi_out, dba_ref, dbi_ref, dlam_ref,
         upad_ref, dh_ref, gf_ref, gr_ref, daf_ref, dar_ref, dupad_ref, dwa_ref, dwi_ref) = rest[len(after):]
        g_refs, da_refs = (gf_ref, gr_ref), (daf_ref, dar_ref)
        cb = up_ref.shape[1]
        zeros = jnp.zeros((PAD, cb), F32)
        upad_ref[pl.ds(0, PAD), :] = zeros
        upad_ref[pl.ds(PAD + T, PAD), :] = zeros
        upad_ref[pl.ds(PAD, T), :] = up_ref[...].astype(F32)
        sp = _softplus_neg(lam_ref[...])

        def gate_chunk(c, carry):
            t0 = pl.multiple_of(c * REC_CHUNK, REC_CHUNK)
            rows = pl.ds(t0, REC_CHUNK)
            y = yb_ref[rows, :].astype(F32)
            dgv = dg_ref[rows, :].astype(F32)
            dh_ref[rows, :] = dgv * _gelu(y)
            dyb_ref[rows, :] = (dgv * (hf_ref[rows, :] + hr_ref[rows, :]) * _gelu_grad(y)).astype(BF16)
            return carry

        lax.fori_loop(0, N_CHUNK, gate_chunk, 0)

        sub = lax.broadcasted_iota(jnp.int32, (8, cb), 0)

        def tile(k, carry):
            cf, cr = carry
            kf = N_TILE - 1 - k
            tf = pl.multiple_of(kf * 8, 8)
            tnext = pl.multiple_of(jnp.minimum(kf + 1, N_TILE - 1) * 8, 8)
            tprev = pl.multiple_of(jnp.maximum(kf - 1, 0) * 8, 8)
            a_t = af_ref[pl.ds(tf, 8), :]
            a_n = jnp.where(kf < N_TILE - 1, af_ref[pl.ds(tnext, 8), :], 0.0)
            a_sh = jnp.where(sub == 7, pltpu.roll(a_n, 7, 0), pltpu.roll(a_t, 7, 0))
            ca, cbb = _tile_scan(a_sh, dh_ref[pl.ds(tf, 8), :], sub, True)
            gf = ca * cf + cbb
            h_t = hf_ref[pl.ds(tf, 8), :]
            h_p = jnp.where(kf > 0, hf_ref[pl.ds(tprev, 8), :], 0.0)
            h_sh = jnp.where(sub == 0, pltpu.roll(h_p, 1, 0), pltpu.roll(h_t, 1, 0))
            gf_ref[pl.ds(tf, 8), :] = gf
            daf_ref[pl.ds(tf, 8), :] = gf * h_sh
            tr = pl.multiple_of(k * 8, 8)
            rnext = pl.multiple_of(jnp.minimum(k + 1, N_TILE - 1) * 8, 8)
            rprev = pl.multiple_of(jnp.maximum(k - 1, 0) * 8, 8)
            b_t = ar_ref[pl.ds(tr, 8), :]
            b_p = jnp.where(k > 0, ar_ref[pl.ds(rprev, 8), :], 0.0)
            b_sh = jnp.where(sub == 0, pltpu.roll(b_p, 1, 0), pltpu.roll(b_t, 1, 0))
            ra, rb = _tile_scan(b_sh, dh_ref[pl.ds(tr, 8), :], sub, False)
            gr = ra * cr + rb
            hr_t = hr_ref[pl.ds(tr, 8), :]
            hr_n = jnp.where(k < N_TILE - 1, hr_ref[pl.ds(rnext, 8), :], 0.0)
            hr_sh = jnp.where(sub == 7, pltpu.roll(hr_n, 7, 0), pltpu.roll(hr_t, 7, 0))
            gr_ref[pl.ds(tr, 8), :] = gr
            dar_ref[pl.ds(tr, 8), :] = gr * hr_sh
            return _last_row(gf, 0), _last_row(gr, 7)

        z8 = jnp.zeros((8, cb), F32)
        lax.fori_loop(0, N_TILE, tile, (z8, z8))

        dupad_ref[pl.ds(0, PAD), :] = zeros
        dupad_ref[pl.ds(PAD + T, PAD), :] = zeros
        dwa_ref[...] = jnp.zeros_like(dwa_ref)
        dwi_ref[...] = jnp.zeros_like(dwi_ref)
        dba_ref[...] = jnp.zeros_like(dba_ref)
        dbi_ref[...] = jnp.zeros_like(dbi_ref)
        dlam_ref[...] = jnp.zeros_like(dlam_ref)

        def grad_chunk(c, carry):
            t0 = pl.multiple_of(c * REC_CHUNK, REC_CHUNK)
            rows = pl.ds(t0, REC_CHUNK)
            u = u_ref[rows, :]
            ub = u.astype(BF16)
            du = jnp.zeros((REC_CHUNK, cb), F32)
            for d in range(2):
                r, i, a, mult, inv_mult = _gates(u, wa_ref[d], wi_ref[d], ba_ref[d:d + 1, :], bi_ref[d:d + 1, :],
                                                 sp[d:d + 1, :])
                dbx = g_refs[d][rows, :]
                dmult = dbx * (i * u)
                diu = dbx * mult
                a2 = a * a
                dlog = da_refs[d][rows, :] * a - dmult * (a2 * inv_mult)
                dpa = (dlog * (-LRU_C) * sp[d:d + 1, :]) * r * (1.0 - r)
                dpi = (diu * u) * i * (1.0 - i)
                dpab, dpib = dpa.astype(BF16), dpi.astype(BF16)
                du = du + diu * i + _dot(dpab, wa_ref[d], NT) + _dot(dpib, wi_ref[d], NT)
                dwa_ref[d] += _dot(ub, dpab, TN)
                dwi_ref[d] += _dot(ub, dpib, TN)
                dba_ref[d:d + 1, :] += jnp.sum(dpa, axis=0, keepdims=True)
                dbi_ref[d:d + 1, :] += jnp.sum(dpi, axis=0, keepdims=True)
                dlam_ref[d:d + 1, :] += jnp.sum(dlog * r, axis=0, keepdims=True)
            dupad_ref[pl.ds(PAD + t0, REC_CHUNK), :] = du
            return carry

        lax.fori_loop(0, N_CHUNK, grad_chunk, 0)
        dlam_ref[...] = dlam_ref[...] * (LRU_C * _sigmoid(-lam_ref[...]))
        for d in range(2):
            for blk in range(REC_CB // REC_BLOCK):
                lo, hi = blk * REC_BLOCK, (blk + 1) * REC_BLOCK
                dwa_out[d, blk] = dwa_ref[d, lo:hi, lo:hi]
                dwi_out[d, blk] = dwi_ref[d, lo:hi, lo:hi]

        cw = cw_ref[...]
        dcb = jnp.zeros((1, cb), F32)
        dcw = [jnp.zeros((1, cb), F32) for _ in range(4)]
        for c in range(N_CHUNK):
            t0 = c * REC_CHUNK
            du = dupad_ref[pl.ds(PAD + t0, REC_CHUNK), :]
            dcb = dcb + jnp.sum(du, axis=0, keepdims=True)
            for j in range(4):
                dcw[j] = dcw[j] + jnp.sum(du * upad_ref[pl.ds(PAD + t0 + j - 2, REC_CHUNK), :], axis=0, keepdims=True)
            dup_ref[pl.ds(t0, REC_CHUNK), :] = _conv_taps(dupad_ref, t0, cw, -1).astype(BF16)
        dcb_ref[...] = dcb
        dcw_ref[...] = jnp.concatenate(dcw, axis=0)

    full = pltpu.VMEM((T, REC_CB), F32)
    padded = pltpu.VMEM((T + 2 * PAD, REC_CB), F32)
    per = REC_CB // REC_BLOCK
    diag = pl.BlockSpec((2, per, REC_BLOCK, REC_BLOCK), lambda c: (0, c, 0, 0))
    return pl.pallas_call(
        body, name="rec_bwd", grid=(N_CB,),
        in_specs=[up, yb] + [col] * 6 + [cw, cbias, wbd, wbd, vec2, vec2, vec2] + [_ANY] * len(after),
        out_specs=[col, col, cw, cbias, diag, diag, vec2, vec2, vec2],
        out_shape=[_sds((T, D), BF16), _sds((T, D), BF16), _sds((4, D), F32), _sds((1, D), F32),
                   _sds((2, D // REC_BLOCK, REC_BLOCK, REC_BLOCK), F32), _sds((2, D // REC_BLOCK, REC_BLOCK, REC_BLOCK), F32),
                   _sds((2, D), F32), _sds((2, D), F32), _sds((2, D), F32)],
        scratch_shapes=[padded, full, full, full, full, full, padded,
                        pltpu.VMEM((2, REC_CB, REC_CB), F32), pltpu.VMEM((2, REC_CB, REC_CB), F32)],
        compiler_params=_params(("parallel",)))(z, z, dg, *saved, conv_w, conv_b, wa, wi, ba, bi, lam, *after)


class _NoReducer:
    def begin(self, tag, grads):
        return ()

    def advance(self, tag, after):
        return ()

    def interlude(self, tokens):
        return None


def _local_step(x, target, p, late=None, reducer=_NoReducer()):
    x = x.reshape(T, D)
    target = target.reshape(T, D)
    tb = _bias_pairs(p["rpb"])
    wa, wi = _block_diag(p["w_rg_a"]), _block_diag(p["w_rg_i"])

    h1 = _rms_fwd("rms1_fwd", x, p["ln1_g"], after=late[0] if late else ())
    if late:
        p = {**p, **late[1]((h1, tb, wa, wi))}
    rec_params = (p["conv_w"], p["conv_b"], wa, wi, p["b_rg_a"], p["b_rg_i"], p["lru_lambda"])
    (z,) = _mm_nn_cols("mm_z", h1, p["w_in"], BF16, bias=p["b_in"])
    att, probs = _attn_fwd(z, tb)
    g, rec_saved = _rec_fwd(z, *rec_params)
    if late:
        p = {**p, **late[2](g)}
    y_att, y_rec, mixed, x1, h2 = _branches_fwd(att, g, z, x, p["w_att_o"], p["w_rec_o"], p["w_out"], p["ln2_g"])

    def relu2(r, ex, outs):
        rp = jnp.maximum(r, 0.0)
        outs[0][...] = (rp * rp).astype(BF16)

    (s,) = _mm_nn_cols("mm_ff1", h2, p["w_ff1"], BF16, epilogue=relu2)
    loss, dx2, dx2_b, g_lnf = _mm_x2_loss_head(s, p["w_ff2"], x1, target, p["lnf_g"])

    def relu2_bwd(r, ex, outs):
        outs[0][...] = (r * 2.0 * jnp.sqrt(ex[0][...].astype(F32))).astype(BF16)

    (df,) = _mm_nt_rows("mm_df", dx2_b, p["w_ff2"], BF16, tn=D, extras=[s],
                        extra_specs=[pl.BlockSpec((TJ, D), lambda j, i, k: (i, j))], epilogue=relu2_bwd)
    (g_w_ff2,) = _mm_tn_rows("mm_g_ff2", s, dx2_b, tm=D)
    (g_w_ff1,) = _mm_tn_cols("mm_g_ff1", h2, df, D)
    tok = reducer.begin("ff", dict(w_ff2=g_w_ff2, w_ff1=g_w_ff1))
    dx1, dx1_b, g_ln2 = _mm_nt_cols_rms_bwd("mm_dh2_rms2_bwd", df, p["w_ff1"], x1, p["ln2_g"], dx2, after=tok,
                                            bf16_copy=True)

    dy_att, dy_rec, dg_att, dg_rec, d_att, d_g = _branches_bwd(dx1_b, y_att, y_rec, z, p["w_att_o"], p["w_rec_o"],
                                                               p["w_out"])
    (g_w_out,) = _mm_tn_rows("mm_g_out", mixed, dx1_b, tm=D)
    (g_w_att_o,) = _mm_tn_cols("mm_g_att_o", att, dy_att, D // N_CHIPS)
    (g_w_rec_o,) = _mm_tn_rows("mm_g_rec_o", g, dy_rec, tm=D)
    tok = reducer.advance("ff", g_w_rec_o) + reducer.begin("proj", dict(w_out=g_w_out, w_att_o=g_w_att_o, w_rec_o=g_w_rec_o))

    dq, dk, dv, ds_acc = _attn_bwd(z, probs, d_att, after=tok)
    g_rpb = _rpb_grad(ds_acc)
    tok = reducer.advance("proj", dq)
    d_up, d_yb, g_conv_w, g_conv_b, g_wa, g_wi, g_ba, g_bi, g_lam = _rec_bwd(z, d_g, rec_saved, *rec_params, after=tok)
    dz = jnp.concatenate([dq, dk, dv, d_up, d_yb, dg_att, dg_rec], axis=1)

    g_w_in, g_b_in = _mm_tn_cols("mm_g_in", h1, dz, D_IN // N_CHIPS, colsum=True)
    tok = reducer.advance("in", reducer.interlude(reducer.begin("in", dict(w_in=g_w_in))))
    grad_x, g_ln1 = _mm_nt_cols_rms_bwd("mm_dh1_rms1_bwd", dz, p["w_in"], x, p["ln1_g"], dx1, after=tok)

    grads = dict(ln1_g=g_ln1, w_in=g_w_in, b_in=g_b_in, rpb=g_rpb, w_att_o=g_w_att_o, conv_w=g_conv_w,
                 conv_b=g_conv_b, w_rg_a=g_wa, b_rg_a=g_ba, w_rg_i=g_wi,
                 b_rg_i=g_bi, lru_lambda=g_lam, w_rec_o=g_w_rec_o, w_out=g_w_out, ln2_g=g_ln2,
                 w_ff1=g_w_ff1, w_ff2=g_w_ff2, lnf_g=g_lnf)
    return loss, grad_x.reshape(1, T, D), grads


_ANY = pl.BlockSpec(memory_space=pl.ANY)
N_PEERS = N_CHIPS - 1


def _place():
    x, y, c = lax.axis_index("x"), lax.axis_index("y"), lax.axis_index("c")
    peers = [(1 - x, y), (x, 1 - y), (1 - x, 1 - y)]
    return x, y, c, 2 * x + y, peers


def _remote(src, dst, send_sem, recv_sem, dev):
    return pltpu.make_async_remote_copy(src_ref=src, dst_ref=dst, send_sem=send_sem, recv_sem=recv_sem,
                                        device_id=dev, device_id_type=MESH)


def _prefetch_call(body, name, ids, grid, in_specs, out_specs, out_shape, args, semantics=None):
    spec = pltpu.PrefetchScalarGridSpec(num_scalar_prefetch=1, grid=grid, in_specs=in_specs, out_specs=out_specs)
    return pl.pallas_call(body, name=name, grid_spec=spec, out_shape=out_shape,
                          compiler_params=_params(semantics or ("parallel",) * len(grid)))(ids, *args)


def _cast_bf16(name, w, chip_id, after=()):
    rows, cols = w.shape
    rb = min(rows, 256)

    def body(ids_ref, w_ref, *rest):
        rest[-1][...] = w_ref[...].astype(BF16)

    return _prefetch_call(body, name, chip_id, (rows // rb,),
                          [pl.BlockSpec((rb, cols), lambda i, ids: (i, 0))] + [_ANY] * len(after),
                          pl.BlockSpec((None, rb, cols), lambda i, ids: (ids[0], i, 0)),
                          _sds((N_CHIPS, rows, cols), BF16), (w, *after))


def _dma_sems(*counts):
    return [pltpu.SemaphoreType.DMA((k,)) for k in counts]


_HBM = pl.BlockSpec(memory_space=pltpu.HBM)
_SEM = pl.BlockSpec(memory_space=pltpu.SEMAPHORE)
_SPLIT_COPY = pltpu.CompilerParams(has_side_effects=pltpu.SideEffectType.DATAFLOW_SIDE_EFFECTING)
SIBLING_ID = 0
_SPLIT_COPY_SIBLING = pltpu.CompilerParams(has_side_effects=pltpu.SideEffectType.DATAFLOW_SIDE_EFFECTING,
                                           collective_id=SIBLING_ID)


def _sibling_handshake():
    x, y, c = lax.axis_index("x"), lax.axis_index("y"), lax.axis_index("c")
    barrier = pltpu.get_barrier_semaphore()
    pl.semaphore_signal(barrier, inc=1, device_id=(x, y, 1 - c), device_id_type=MESH)
    pl.semaphore_wait(barrier, 1)


def _hbm(arrays):
    return [pltpu.with_memory_space_constraint(a, pltpu.HBM) for a in arrays]


def _hbm_like(arrays):
    return [pltpu.HBM(a.shape, a.dtype) for a in arrays]


def _halves(buf, c):
    half = buf.shape[1] // 2
    return pl.ds(c * half, half), pl.ds((1 - c) * half, half)


def _gather_start(name, slots):
    n = len(slots)
    nk = n * N_PEERS

    def body(*refs):
        bufs = refs[n:2 * n]
        send_sems, recv_sems, token = refs[2 * n:]
        x, y, c, chip, peers = _place()
        for t in range(n):
            mine, _ = _halves(bufs[t], c)
            for r, (px, py) in enumerate(peers):
                k = t * N_PEERS + r
                own = bufs[t].at[chip, mine]
                _remote(own, own, send_sems.at[k], recv_sems.at[k], (px, py, c)).start()
        token[...] = jnp.zeros_like(token)

    res = pl.pallas_call(
        body, name=name, in_specs=[_HBM] * n, out_specs=[_HBM] * n + [_SEM, _SEM, pl.BlockSpec(memory_space=pltpu.VMEM)],
        out_shape=_hbm_like(slots) + [pltpu.SemaphoreType.DMA((nk,)), pltpu.SemaphoreType.DMA((nk,)),
                                      _sds((8, 128), F32)],
        input_output_aliases={t: t for t in range(n)}, compiler_params=_SPLIT_COPY)(*_hbm(slots))
    return res[:n], (res[n], res[n + 1]), res[n + 2]


def _gather_wait(name, bufs, sems, after):
    n = len(bufs)
    after = tuple(after) if isinstance(after, (tuple, list)) else (after,)

    def body(*refs):
        ins = refs[:n]
        send_sems, recv_sems = refs[n], refs[n + 1]
        x, y, c, chip, peers = _place()
        for t in range(n):
            mine, _ = _halves(ins[t], c)
            for r, (px, py) in enumerate(peers):
                k = t * N_PEERS + r
                cp = _remote(ins[t].at[chip, mine], ins[t].at[2 * px + py, mine], send_sems.at[k], recv_sems.at[k],
                             (px, py, c))
                cp.wait_send()
                cp.wait_recv()

    return pl.pallas_call(
        body, name=name, in_specs=[_HBM] * n + [_SEM, _SEM] + [_ANY] * len(after), out_specs=[_HBM] * n,
        out_shape=_hbm_like(bufs), input_output_aliases={t: t for t in range(n)},
        compiler_params=_SPLIT_COPY)(*bufs, *sems, *after)


def _gather_forward(name, bufs):
    n = len(bufs)
    nk = n * N_PEERS

    def body(*refs):
        _sibling_handshake()
        outs = refs[n:2 * n]
        send_sems, recv_sems = refs[2 * n:]
        x, y, c, chip, peers = _place()
        sibling = (x, y, 1 - c)
        sends = []
        for t in range(n):
            mine, _ = _halves(outs[t], c)
            for r, (px, py) in enumerate(peers):
                k = t * N_PEERS + r
                landed = outs[t].at[2 * px + py, mine]
                sends.append(_remote(landed, landed, send_sems.at[k], recv_sems.at[k], sibling))
                sends[-1].start()
        for t in range(n):
            _, theirs = _halves(outs[t], c)
            for r, (px, py) in enumerate(peers):
                k = t * N_PEERS + r
                landed = outs[t].at[2 * px + py, theirs]
                _remote(landed, landed, send_sems.at[k], recv_sems.at[k], sibling).wait_recv()
        for cp in sends:
            cp.wait_send()

    return pl.pallas_call(
        body, name=name, in_specs=[_ANY] * n, out_specs=[_ANY] * n, out_shape=[_sds(b.shape, b.dtype) for b in bufs],
        input_output_aliases={t: t for t in range(n)}, scratch_shapes=_dma_sems(nk, nk),
        compiler_params=pltpu.CompilerParams(collective_id=SIBLING_ID))(*bufs)


def _pair_copies(n, srcs, lands, send_sems, recv_sems):
    x, y, c, _, _ = _place()
    sibling = (x, y, 1 - c)
    copies = []
    for t in range(n):
        half = srcs[t].shape[1] // 2
        for j in range(N_CHIPS):
            k = t * N_CHIPS + j
            copies.append(_remote(srcs[t].at[j, pl.ds((1 - c) * half, half)], lands[t].at[j],
                                  send_sems.at[k], recv_sems.at[k], sibling))
    for t in range(n, len(srcs)):
        k = n * N_CHIPS + t - n
        copies.append(_remote(srcs[t], lands[t], send_sems.at[k], recv_sems.at[k], sibling))
    return copies


def _pair_start(name, grads, wholes=()):
    n = len(grads)
    srcs = list(grads) + list(wholes)
    m = len(srcs)
    lands = [pltpu.HBM((N_CHIPS, g.shape[1] // 2, g.shape[2]), F32) for g in grads] + _hbm_like(wholes)
    ns = n * N_CHIPS + len(wholes)

    def body(*refs):
        _sibling_handshake()
        src_refs, land_refs = refs[m:2 * m], refs[2 * m:3 * m]
        send_sems, recv_sems, token = refs[3 * m:]
        for cp in _pair_copies(n, src_refs, land_refs, send_sems, recv_sems):
            cp.start()
        token[...] = jnp.zeros_like(token)

    res = pl.pallas_call(
        body, name=name, in_specs=[_HBM] * m,
        out_specs=[_HBM] * (2 * m) + [_SEM, _SEM, pl.BlockSpec(memory_space=pltpu.VMEM)],
        out_shape=_hbm_like(srcs) + lands + [pltpu.SemaphoreType.DMA((ns,)), pltpu.SemaphoreType.DMA((ns,)),
                                             _sds((8, 128), F32)],
        input_output_aliases={t: t for t in range(m)}, compiler_params=_SPLIT_COPY_SIBLING)(*_hbm(srcs))
    return (res[:m], res[m:2 * m], (res[2 * m], res[2 * m + 1])), res[2 * m + 2]


def _pair_wait(name, flight, n, after):
    srcs, lands, sems = flight
    m = len(srcs)

    def body(*refs):
        for cp in _pair_copies(n, refs[:m], refs[m:2 * m], refs[2 * m], refs[2 * m + 1]):
            cp.wait_send()
            cp.wait_recv()

    res = pl.pallas_call(
        body, name=name, in_specs=[_HBM] * (2 * m) + [_SEM, _SEM, _ANY], out_specs=[_HBM] * (2 * m),
        out_shape=_hbm_like(srcs) + _hbm_like(lands), input_output_aliases={t: t for t in range(2 * m)},
        compiler_params=_SPLIT_COPY)(*srcs, *lands, *sems, after)
    return res[:m], res[m:]


def _chip_copies(srcs, lands, small_src, small_land, send_sems, recv_sems):
    x, y, c, chip, peers = _place()
    n = len(srcs)
    copies = []
    for r, (px, py) in enumerate(peers):
        for t in range(n):
            k = t * N_PEERS + r
            copies.append(_remote(srcs[t].at[2 * px + py], lands[t].at[r], send_sems.at[k], recv_sems.at[k], (px, py, c)))
        if small_src is not None:
            k = n * N_PEERS + r
            half_s = small_src.shape[0] // 2
            copies.append(_remote(small_src.at[pl.ds(c * half_s, half_s)], small_land.at[r],
                                  send_sems.at[k], recv_sems.at[k], (px, py, c)))
    return copies


def _chip_start(name, sums_bf16, small=None):
    n = len(sums_bf16)
    srcs = list(sums_bf16) + ([small] if small is not None else [])
    m = len(srcs)
    lands = [pltpu.HBM((N_PEERS,) + s.shape[1:], BF16) for s in sums_bf16]
    if small is not None:
        lands.append(pltpu.HBM((N_PEERS, small.shape[0] // 2, 128), F32))
    nk = m * N_PEERS

    def body(*refs):
        src_refs, land_refs = refs[m:2 * m], refs[2 * m:3 * m]
        send_sems, recv_sems, token = refs[3 * m:]
        small_src, small_land = (src_refs[n], land_refs[n]) if small is not None else (None, None)
        for cp in _chip_copies(src_refs[:n], land_refs[:n], small_src, small_land, send_sems, recv_sems):
            cp.start()
        token[...] = jnp.zeros_like(token)

    res = pl.pallas_call(
        body, name=name, in_specs=[_HBM] * m,
        out_specs=[_HBM] * (2 * m) + [_SEM, _SEM, pl.BlockSpec(memory_space=pltpu.VMEM)],
        out_shape=_hbm_like(srcs) + lands + [pltpu.SemaphoreType.DMA((nk,)), pltpu.SemaphoreType.DMA((nk,)),
                                             _sds((8, 128), F32)],
        input_output_aliases={t: t for t in range(m)}, compiler_params=_SPLIT_COPY)(*_hbm(srcs))
    return (res[:m], res[m:2 * m], (res[2 * m], res[2 * m + 1])), res[2 * m + 2]


def _chip_wait(name, flight, with_small, after):
    srcs, lands, sems = flight
    m = len(srcs)
    n = m - 1 if with_small else m

    def body(*refs):
        src_refs, land_refs = refs[:m], refs[m:2 * m]
        send_sems, recv_sems = refs[2 * m], refs[2 * m + 1]
        small_src, small_land = (src_refs[n], land_refs[n]) if with_small else (None, None)
        for cp in _chip_copies(src_refs[:n], land_refs[:n], small_src, small_land, send_sems, recv_sems):
            cp.wait_send()
            cp.wait_recv()

    res = pl.pallas_call(
        body, name=name, in_specs=[_HBM] * (2 * m) + [_SEM, _SEM, _ANY], out_specs=[_HBM] * (2 * m),
        out_shape=_hbm_like(srcs) + _hbm_like(lands), input_output_aliases={t: t for t in range(2 * m)},
        compiler_params=_SPLIT_COPY)(*srcs, *lands, *sems, after)
    return res[:m], res[m:]


def _swap_start(name, bufs):
    n = len(bufs)

    def body(*refs):
        _sibling_handshake()
        outs = refs[n:2 * n]
        send_sems, recv_sems, token = refs[2 * n:]
        x, y, c, _, _ = _place()
        for t in range(n):
            h = outs[t].shape[0] // 2
            mine = outs[t].at[pl.ds(c * h, h)]
            _remote(mine, mine, send_sems.at[t], recv_sems.at[t], (x, y, 1 - c)).start()
        token[...] = jnp.zeros_like(token)

    res = pl.pallas_call(
        body, name=name, in_specs=[_HBM] * n, out_specs=[_HBM] * n + [_SEM, _SEM, pl.BlockSpec(memory_space=pltpu.VMEM)],
        out_shape=_hbm_like(bufs) + [pltpu.SemaphoreType.DMA((n,)), pltpu.SemaphoreType.DMA((n,)), _sds((8, 128), F32)],
        input_output_aliases={t: t for t in range(n)}, compiler_params=_SPLIT_COPY_SIBLING)(*_hbm(bufs))
    return (res[:n], (res[n], res[n + 1])), res[n + 2]


def _swap_wait(name, flight, after):
    bufs, sems = flight
    n = len(bufs)

    def body(*refs):
        ins = refs[:n]
        send_sems, recv_sems = refs[n], refs[n + 1]
        x, y, c, _, _ = _place()
        for t in range(n):
            h = ins[t].shape[0] // 2
            cp = _remote(ins[t].at[pl.ds(c * h, h)], ins[t].at[pl.ds((1 - c) * h, h)], send_sems.at[t],
                         recv_sems.at[t], (x, y, 1 - c))
            cp.wait_send()
            cp.wait_recv()

    return pl.pallas_call(
        body, name=name, in_specs=[_HBM] * n + [_SEM, _SEM, _ANY], out_specs=[_HBM] * n, out_shape=_hbm_like(bufs),
        input_output_aliases={t: t for t in range(n)}, compiler_params=_SPLIT_COPY)(*bufs, *sems, after)


def _pair_sum(name, grad, got, ids):
    _, rows, cols = got.shape
    rb = min(rows, 256)
    nb = rows // rb
    blk = pl.BlockSpec((None, rb, cols), lambda i, j, ids: (j, i, 0))
    mine = pl.BlockSpec((None, rb, cols), lambda i, j, ids: (j, ids[1] * nb + i, 0))
    own = pl.BlockSpec((rb, cols), lambda i, j, ids: (i, 0))

    def body(ids_ref, a_ref, b_ref, s_ref, sb_ref):
        s = a_ref[...] + b_ref[...]
        sb_ref[...] = s.astype(BF16)

        @pl.when(pl.program_id(1) == ids_ref[0])
        def _():
            s_ref[...] = s

    return _prefetch_call(body, name, ids, (nb, N_CHIPS), [mine, blk], [own, blk],
                          [_sds((rows, cols), F32), _sds(got.shape, BF16)], (grad, got),
                          semantics=("parallel", "arbitrary"))


def _chip_sum(name, own_sum, got, ids):
    rows, cols = own_sum.shape
    rb = min(rows, 256)
    nb = rows // rb
    own = pl.BlockSpec((rb, cols), lambda i, ids: (i, 0))
    blk3 = pl.BlockSpec((N_PEERS, rb, cols), lambda i, ids: (0, i, 0))
    out = pl.BlockSpec((rb, cols), lambda i, ids: (ids[1] * nb + i, 0))

    def body(ids_ref, a_ref, b_ref, o_ref):
        o_ref[...] = ((a_ref[...] + b_ref[0].astype(F32)) + b_ref[1].astype(F32)) + b_ref[2].astype(F32)

    return _prefetch_call(body, name, ids, (nb,), [own, blk3], out, _sds((2 * rows, cols), F32), (own_sum, got))


SMALL_RB = 280


def _small_pair_sum(own, got):
    blk = pl.BlockSpec((SMALL_RB, 128), lambda i: (i, 0))

    def body(a_ref, b_ref, o_ref):
        o_ref[...] = a_ref[...] + b_ref[...]

    return pl.pallas_call(body, name="small_pair_sum", grid=(own.shape[0] // SMALL_RB,), in_specs=[blk, blk],
                          out_specs=blk, out_shape=_sds(own.shape, F32),
                          compiler_params=_params(("parallel",)))(own, got)


def _small_chip_sum(pair, got, ids):
    nb = pair.shape[0] // 2 // SMALL_RB
    half = pl.BlockSpec((SMALL_RB, 128), lambda i, ids: (ids[1] * nb + i, 0))
    blk3 = pl.BlockSpec((N_PEERS, SMALL_RB, 128), lambda i, ids: (0, i, 0))

    def body(ids_ref, a_ref, b_ref, o_ref):
        o_ref[...] = (a_ref[...] + b_ref[1]) + (b_ref[0] + b_ref[2])

    return _prefetch_call(body, "small_chip_sum", ids, (nb,), [half, blk3], half, _sds(pair.shape, F32), (pair, got))


def _adamw_math(w, g, m, v):
    m = ADAM_B1 * m + (1.0 - ADAM_B1) * g
    v = ADAM_B2 * v + (1.0 - ADAM_B2) * (g * g)
    m_hat = m / (1.0 - ADAM_B1 ** ADAM_STEP)
    v_hat = v / (1.0 - ADAM_B2 ** ADAM_STEP)
    delta = -ADAM_LR * (m_hat / (jnp.sqrt(v_hat) + ADAM_EPS) + ADAM_WD * w)
    return delta, m, v


def _adamw(name, w, g, m, v, rb):
    rows, cols = w.shape
    blk = pl.BlockSpec((rb, cols), lambda i: (i, 0))

    def body(w_ref, g_ref, m_ref, v_ref, g_out, d_ref, nm_ref, nv_ref):
        gv = g_ref[...]
        d, nm, nv = _adamw_math(w_ref[...], gv, m_ref[...], v_ref[...])
        g_out[...] = gv
        d_ref[...] = d
        nm_ref[...] = nm
        nv_ref[...] = nv

    return pl.pallas_call(body, name=name, grid=(rows // rb,), in_specs=[blk] * 4, out_specs=[blk] * 4,
                          out_shape=[_sds(w.shape, F32)] * 4, compiler_params=_params(("parallel",)))(w, g, m, v)


def _adamw_small(ws, gs, ms, vs):
    n = len(ws)

    def body(*refs):
        for t in range(n):
            w_ref, g_ref, m_ref, v_ref = (refs[k * n + t] for k in range(4))
            d, nm, nv = _adamw_math(w_ref[...], g_ref[...], m_ref[...], v_ref[...])
            for k, val in enumerate((d, nm, nv)):
                refs[(4 + k) * n + t][...] = val

    res = pl.pallas_call(body, name="adamw_small", out_shape=[_sds(a.shape, F32) for a in ws] * 3,
                         compiler_params=_params())(*ws, *gs, *ms, *vs)
    return [(res[t], res[n + t], res[2 * n + t]) for t in range(n)]


BIG = ("w_in", "w_att_o", "w_rec_o", "w_out", "w_ff1", "w_ff2")
SHARDED_VECS = ("conv_w", "b_rg_a", "b_rg_i", "lru_lambda")
SMALL = ("ln1_g", "b_in", "rpb", "conv_w", "conv_b", "w_rg_a", "b_rg_a", "w_rg_i", "b_rg_i", "lru_lambda",
         "ln2_g", "lnf_g")
SMALL_ROWS = 2240
ORDER = ("ln1_g", "w_in", "b_in", "rpb", "w_att_o", "conv_w", "conv_b", "w_rg_a", "b_rg_a", "w_rg_i", "b_rg_i",
         "lru_lambda", "w_rec_o", "w_out", "ln2_g", "w_ff1", "w_ff2", "lnf_g")


def _pack_small(grads, loss):
    parts, sizes = [], {}
    for n in SMALL:
        flat = grads[n].reshape(-1)
        pad = (-flat.shape[0]) % 128
        sizes[n] = (flat.shape[0], flat.shape[0] + pad)
        parts.append(jnp.pad(flat, (0, pad)))
    total = sum(s[1] for s in sizes.values())
    parts.append(jnp.pad(loss.reshape(1), (0, SMALL_ROWS * 128 - total - 1)))
    return jnp.concatenate(parts).reshape(SMALL_ROWS, 128), sizes


def _unpack_small(buf, sizes, shapes):
    flat = buf.reshape(-1)
    out, pos = {}, 0
    for n in SMALL:
        size, padded = sizes[n]
        out[n] = flat[pos:pos + size].reshape(shapes[n])
        pos += padded
    return out, flat[pos]


def _gather_weights(w, chip):
    chip_id = chip.astype(jnp.int32).reshape(1)
    vec_rows = [w[n][0] for n in SHARDED_VECS]
    vec_shard = jnp.concatenate(vec_rows + [jnp.zeros((16 - 10, D // N_CHIPS), F32)], axis=0)
    vec_slots = lax.dynamic_update_slice(jnp.zeros((N_CHIPS, 16, D // N_CHIPS), F32), vec_shard[None], (chip, 0, 0))
    bufs_a, sems_a, token_a = _gather_start("gather_start_first", [_cast_bf16("cast_w_in", w["w_in"][0], chip_id), vec_slots])
    rest_names = BIG[1:]
    bufs_b, sems_b, token_b = _gather_start(
        "gather_start_rest", [_cast_bf16("cast_" + n, w[n][0], chip_id, after=(token_a,)) for n in rest_names])

    def first(after):
        w_in_full, vec_full = _gather_forward("gather_forward_first", _gather_wait("gather_wait_first", bufs_a, sems_a, after))
        vecs = vec_full.transpose(1, 0, 2).reshape(16, D)
        return dict(w_in=w_in_full, conv_w=vecs[0:4], b_rg_a=vecs[4:6], b_rg_i=vecs[6:8], lru_lambda=vecs[8:10])

    def rest(after):
        full = dict(zip(rest_names, _gather_forward("gather_forward_rest",
                                                    _gather_wait("gather_wait_rest", bufs_b, sems_b, after))))
        return dict(w_att_o=full["w_att_o"], w_ff1=full["w_ff1"], w_rec_o=full["w_rec_o"].reshape(D, D),
                    w_out=full["w_out"].reshape(D, D), w_ff2=full["w_ff2"].reshape(D_FF, D))

    p = dict(ln1_g=w["ln1_g"], b_in=w["b_in"], rpb=w["rpb"][0], conv_b=w["conv_b"], w_rg_a=w["w_rg_a"][0],
             w_rg_i=w["w_rg_i"][0], ln2_g=w["ln2_g"], lnf_g=w["lnf_g"].reshape(1, D))
    return p, ((token_b,), first, rest)


class _Reducer:
    def __init__(self, ids):
        self.ids = ids
        self.groups = {}

    def begin(self, tag, grads, small=None):
        names = list(grads)
        big = [grads[n].reshape(N_CHIPS, -1, grads[n].shape[-1]) for n in names]
        flight, token = _pair_start("pair_start_" + tag, big, [] if small is None else [small])
        self.groups[tag] = dict(names=names, pair=flight, small=small is not None)
        return (token,)

    def advance(self, tag, after):
        grp = self.groups[tag]
        n = len(grp["names"])
        mine, got = _pair_wait("pair_wait_" + tag, grp["pair"], n, after)
        sums = [_pair_sum("pair_sum_" + name, a, b, self.ids) for name, a, b in zip(grp["names"], mine, got)]
        small_sum = _small_pair_sum(mine[n], got[n]) if grp["small"] else None
        grp["chip"], token = _chip_start("chip_start_" + tag, [s[1] for s in sums], small_sum)
        grp["sums"] = [s[0] for s in sums]
        self.last_token = token
        return (token,)

    def finish(self, tag, after):
        grp = self.groups[tag]
        srcs, lands = _chip_wait("chip_wait_" + tag, grp["chip"], grp["small"], after)
        halves = [_chip_sum("chip_sum_" + name, s, b, self.ids) for name, s, b in zip(grp["names"], grp["sums"], lands)]
        if grp["small"]:
            halves.append(_small_chip_sum(srcs[-1], lands[-1], self.ids))
        grp["swap"], token = _swap_start("swap_start_" + tag, halves)
        return token

    def interlude(self, tokens):
        after = tokens[0]
        for tag in list(self.groups)[:-1]:
            after = self.finish(tag, after)
        return after

    def result(self, tag, after):
        return _swap_wait("swap_wait_" + tag, self.groups[tag]["swap"], after)


def kernel(x, ln1_g, w_in, b_in, rpb, w_att_o, conv_w, conv_b, w_rg_a, b_rg_a, w_rg_i, b_rg_i, lru_lambda, w_rec_o, w_out, ln2_g, w_ff1, w_ff2, lnf_g, loss_target, m_ln1_g, m_w_in, m_b_in, m_rpb, m_w_att_o, m_conv_w, m_conv_b, m_w_rg_a, m_b_rg_a, m_w_rg_i, m_b_rg_i, m_lru_lambda, m_w_rec_o, m_w_out, m_ln2_g, m_w_ff1, m_w_ff2, m_lnf_g, v_ln1_g, v_w_in, v_b_in, v_rpb, v_w_att_o, v_conv_w, v_conv_b, v_w_rg_a, v_b_rg_a, v_w_rg_i, v_b_rg_i, v_lru_lambda, v_w_rec_o, v_w_out, v_ln2_g, v_w_ff1, v_w_ff2, v_lnf_g):
    w = dict(ln1_g=ln1_g, w_in=w_in, b_in=b_in, rpb=rpb, w_att_o=w_att_o, conv_w=conv_w, conv_b=conv_b,
             w_rg_a=w_rg_a, b_rg_a=b_rg_a, w_rg_i=w_rg_i, b_rg_i=b_rg_i, lru_lambda=lru_lambda, w_rec_o=w_rec_o,
             w_out=w_out, ln2_g=ln2_g, w_ff1=w_ff1, w_ff2=w_ff2, lnf_g=lnf_g)
    m = dict(ln1_g=m_ln1_g, w_in=m_w_in, b_in=m_b_in, rpb=m_rpb, w_att_o=m_w_att_o, conv_w=m_conv_w,
             conv_b=m_conv_b, w_rg_a=m_w_rg_a, b_rg_a=m_b_rg_a, w_rg_i=m_w_rg_i, b_rg_i=m_b_rg_i,
             lru_lambda=m_lru_lambda, w_rec_o=m_w_rec_o, w_out=m_w_out, ln2_g=m_ln2_g, w_ff1=m_w_ff1,
             w_ff2=m_w_ff2, lnf_g=m_lnf_g)
    v = dict(ln1_g=v_ln1_g, w_in=v_w_in, b_in=v_b_in, rpb=v_rpb, w_att_o=v_w_att_o, conv_w=v_conv_w,
             conv_b=v_conv_b, w_rg_a=v_w_rg_a, b_rg_a=v_b_rg_a, w_rg_i=v_w_rg_i, b_rg_i=v_b_rg_i,
             lru_lambda=v_lru_lambda, w_rec_o=v_w_rec_o, w_out=v_w_out, ln2_g=v_ln2_g, w_ff1=v_w_ff1,
             w_ff2=v_w_ff2, lnf_g=v_lnf_g)
    chip = 2 * lax.axis_index("x") + lax.axis_index("y")
    ids = jnp.stack([chip, lax.axis_index("c")]).astype(jnp.int32)

    out_grad, out_delta, out_m, out_v = {}, {}, {}, {}

    def update(n, gn):
        shape, two_d = w[n].shape, gn.shape
        gn, d, nm, nv = _adamw("adamw_" + n, w[n].reshape(two_d), gn, m[n].reshape(two_d), v[n].reshape(two_d), 256)
        out_grad[n], out_delta[n], out_m[n], out_v[n] = (gn.reshape(shape), d.reshape(shape), nm.reshape(shape),
                                                         nv.reshape(shape))
        return d

    reducer = _Reducer(ids)
    p, late = _gather_weights(w, chip)
    loss, grad_x, g = _local_step(x, loss_target, p, late, reducer)
    small, sizes = _pack_small(g, loss + reducer.last_token[:1, :1])
    after = reducer.begin("small", {}, small)[0]
    for tag in ("ff", "proj", "in"):
        for n, red in zip(reducer.groups[tag]["names"], reducer.result(tag, after)):
            after = update(n, red)
        if tag == "ff":
            after = reducer.finish("in", reducer.advance("small", after)[0])
    (small_red,) = reducer.result("small", reducer.finish("small", after))
    gsmall, loss = _unpack_small(small_red, sizes, {n: g[n].shape for n in SMALL})
    two_d = {n: (int(np.prod(w[n].shape[:-1])), w[n].shape[-1]) for n in SMALL}
    for n in SHARDED_VECS:
        gsmall[n] = lax.dynamic_slice_in_dim(gsmall[n], chip * (D // N_CHIPS), D // N_CHIPS, axis=1)
    gs = [gsmall[n].reshape(two_d[n]) for n in SMALL]
    updates = _adamw_small([w[n].reshape(two_d[n]) for n in SMALL], gs, [m[n].reshape(two_d[n]) for n in SMALL],
                           [v[n].reshape(two_d[n]) for n in SMALL])
    for n, gn, (d, nm, nv) in zip(SMALL, gs, updates):
        shape = w[n].shape
        out_grad[n], out_delta[n], out_m[n], out_v[n] = (gn.reshape(shape), d.reshape(shape), nm.reshape(shape),
                                                         nv.reshape(shape))
    return (loss, grad_x, *[out_grad[n] for n in ORDER], *[out_delta[n] for n in ORDER],
            *[out_m[n] for n in ORDER], *[out_v[n] for n in ORDER])
```

```python
import numpy as np
import jax
import jax.numpy as jnp
from jax import lax
from jax.experimental import pallas as pl
from jax.experimental.pallas import tpu as pltpu

F32 = jnp.float32
BF16 = jnp.bfloat16

T = 2048
D = 1024
D_ATT = 512
D_IN = 5632
D_FF = 4096
N_HEADS = 8
HEAD_DIM = 64
GRID_W = 64
N_ROWS = T // GRID_W
WIN_H = 8
WIN_W = 16
KEYS = WIN_H * GRID_W
N_CHIPS = 4
EPS = 1e-6
LRU_C = 8.0
SCALE = HEAD_DIM ** -0.5
REC_CB = 256
REC_BLOCK = 64
REC_CHUNK = 256
PAD = 8

ADAM_LR = 0.001
ADAM_B1 = 0.9
ADAM_B2 = 0.999
ADAM_EPS = 1e-08
ADAM_WD = 0.01
ADAM_STEP = 10

VMEM_LIMIT = 56 * 1024 * 1024

NN = (((1,), (0,)), ((), ()))
NT = (((1,), (1,)), ((), ()))
TN = (((0,), (0,)), ((), ()))
MESH = pl.DeviceIdType.MESH


def _params(sem=None):
    return pltpu.CompilerParams(dimension_semantics=sem, vmem_limit_bytes=VMEM_LIMIT)


def _dot(a, b, dims):
    return lax.dot_general(a, b, dims, preferred_element_type=F32)


def _sigmoid(x):
    return 0.5 * jnp.tanh(0.5 * x) + 0.5


def _matmul(name, a, b, *, dims, grid, a_spec, b_spec, out_shapes, out_specs, acc_shape,
            extras=(), extra_specs=(), epilogue=None, colsum_spec=None, colsum_shape=None, after=(),
            semantics=("parallel", "parallel", "arbitrary"), epilogue_takes_first=False):
    nk = grid[2]
    n_extra = len(extras)
    n_out = len(out_shapes)
    with_colsum = colsum_spec is not None

    def body(a_ref, b_ref, *rest):
        ex = rest[:n_extra]
        rest = rest[:n_extra] + rest[n_extra + len(after):]
        outs = rest[n_extra:n_extra + n_out]
        pos = n_extra + n_out
        cs_out = rest[pos] if with_colsum else None
        pos += 1 if with_colsum else 0
        acc = rest[pos]
        cs_acc = rest[pos + 1] if with_colsum else None
        k = pl.program_id(2)
        first_tile = pl.program_id(0) == 0

        @pl.when(k == 0)
        def _():
            acc[...] = jnp.zeros_like(acc)
            if with_colsum:
                cs_acc[...] = jnp.zeros_like(cs_acc)

        bv = b_ref[...]
        acc[...] += _dot(a_ref[...].astype(BF16), bv.astype(BF16), dims)
        if with_colsum:
            cs_acc[...] += jnp.sum(bv.astype(F32), axis=0, keepdims=True)

        @pl.when(k == nk - 1)
        def _():
            r = acc[...]
            if epilogue is None:
                outs[0][...] = r.astype(outs[0].dtype)
            elif epilogue_takes_first:
                epilogue(r, ex, outs, first_tile)
            else:
                epilogue(r, ex, outs)
            if with_colsum:
                cs_out[...] = cs_acc[...]

    shapes = list(out_shapes)
    specs = list(out_specs)
    scratch = [pltpu.VMEM(acc_shape, F32)]
    if with_colsum:
        shapes.append(colsum_shape)
        specs.append(colsum_spec)
        scratch.append(pltpu.VMEM((1, acc_shape[1]), F32))
    res = pl.pallas_call(
        body, name=name, grid=grid,
        in_specs=[a_spec, b_spec, *extra_specs] + [_ANY] * len(after),
        out_specs=specs, out_shape=shapes, scratch_shapes=scratch,
        compiler_params=_params(semantics),
    )(a, b, *extras, *after)
    return res


def _sds(shape, dtype):
    return jax.ShapeDtypeStruct(shape, dtype)


TM = 1024
NI = T // TM
TJ = T
NJ = T // TJ


def _mm_nn_cols(name, a, wg, out_dtype, *, bias=None, extras=(), extra_specs=(), epilogue=None,
                out_shapes=None, out_specs=None):
    k_dim, n4 = wg.shape[1], wg.shape[2]
    ex, exs = list(extras), list(extra_specs)
    if bias is not None:
        ex = [bias] + ex
        exs = [pl.BlockSpec((1, n4), lambda j, i, k: (0, j))] + exs
        user_ep = epilogue

        def epilogue(r, e, outs):
            r = r + e[0][...]
            if user_ep is None:
                outs[0][...] = r.astype(outs[0].dtype)
            else:
                user_ep(r, e[1:], outs)
    if out_shapes is None:
        out_shapes = [_sds((T, N_CHIPS * n4), out_dtype)]
        out_specs = [pl.BlockSpec((TJ, n4), lambda j, i, k: (i, j))]
    return _matmul(
        name, a, wg, dims=NN, grid=(N_CHIPS, NJ, 1),
        a_spec=pl.BlockSpec((TJ, k_dim), lambda j, i, k: (i, 0)),
        b_spec=pl.BlockSpec((None, k_dim, n4), lambda j, i, k: (j, 0, 0)),
        out_shapes=out_shapes, out_specs=out_specs, acc_shape=(TJ, n4),
        extras=ex, extra_specs=exs, epilogue=epilogue)


def _mm_nt_cols_rms_bwd(name, a, wg, x, g, dres, after=(), bf16_copy=False):
    n4 = wg.shape[2]
    row = pl.BlockSpec((TM, D), lambda i, j, k: (i, 0))
    vec = pl.BlockSpec((1, D), lambda i, j, k: (0, 0))

    def epilogue(dhv, ex, outs, first):
        x_ref, g_ref, dres_ref = ex
        dx_ref, dg_ref = outs[0], outs[-1]
        xv = x_ref[...]
        rstd = lax.rsqrt(jnp.mean(xv * xv, axis=-1, keepdims=True) + EPS)
        xhat = xv * rstd
        dy = dhv * g_ref[...]
        dx = dres_ref[...] + rstd * (dy - xhat * jnp.mean(dy * xhat, axis=-1, keepdims=True))
        dx_ref[...] = dx
        if bf16_copy:
            outs[1][...] = dx.astype(BF16)
        part = jnp.sum(dhv * xhat, axis=0, keepdims=True)

        @pl.when(first)
        def _():
            dg_ref[...] = part

        @pl.when(jnp.logical_not(first))
        def _():
            dg_ref[...] += part

    return _matmul(
        name, a, wg, dims=NT, grid=(NI, 1, N_CHIPS),
        a_spec=pl.BlockSpec((TM, n4), lambda i, j, k: (i, k)),
        b_spec=pl.BlockSpec((None, D, n4), lambda i, j, k: (k, 0, 0)),
        out_shapes=[_sds((T, D), F32)] + [_sds((T, D), BF16)] * bf16_copy + [_sds((1, D), F32)],
        out_specs=[row] + [row] * bf16_copy + [vec], acc_shape=(TM, D),
        extras=[x, g, dres], extra_specs=[row, vec, row], epilogue=epilogue, after=after,
        semantics=("arbitrary", "arbitrary", "arbitrary"), epilogue_takes_first=True)


def _mm_nt_rows(name, a, w, out_dtype, *, tn, extras=(), extra_specs=(), epilogue=None):
    k_dim, n = w.shape
    return _matmul(
        name, a, w, dims=NT, grid=(k_dim // tn, NJ, 1),
        a_spec=pl.BlockSpec((TJ, n), lambda j, i, k: (i, 0)),
        b_spec=pl.BlockSpec((tn, n), lambda j, i, k: (j, 0)),
        out_shapes=[_sds((T, k_dim), out_dtype)],
        out_specs=[pl.BlockSpec((TJ, tn), lambda j, i, k: (i, j))], acc_shape=(TJ, tn),
        extras=extras, extra_specs=extra_specs, epilogue=epilogue)


def _mm_tn_cols(name, a, g, n4, *, colsum=False):
    k_dim = a.shape[1]
    kw = {}
    if colsum:
        kw = dict(colsum_spec=pl.BlockSpec((1, n4), lambda j, i, k: (0, j)),
                  colsum_shape=_sds((1, N_CHIPS * n4), F32))
    return _matmul(
        name, a, g, dims=TN, grid=(N_CHIPS, 1, NJ),
        a_spec=pl.BlockSpec((TJ, k_dim), lambda j, i, k: (k, 0)),
        b_spec=pl.BlockSpec((TJ, n4), lambda j, i, k: (k, j)),
        out_shapes=[_sds((N_CHIPS, k_dim, n4), F32)],
        out_specs=[pl.BlockSpec((None, k_dim, n4), lambda j, i, k: (j, 0, 0))],
        acc_shape=(k_dim, n4), **kw)


def _mm_tn_rows(name, a, g, *, tm):
    k_dim, n = a.shape[1], g.shape[1]
    return _matmul(
        name, a, g, dims=TN, grid=(k_dim // tm, 1, NJ),
        a_spec=pl.BlockSpec((TJ, tm), lambda j, i, k: (k, j)),
        b_spec=pl.BlockSpec((TJ, n), lambda j, i, k: (k, 0)),
        out_shapes=[_sds((k_dim, n), F32)],
        out_specs=[pl.BlockSpec((tm, n), lambda j, i, k: (j, 0))], acc_shape=(tm, n))


TE = 256
NE = T // TE
_ROW = pl.BlockSpec((TE, D), lambda i: (i, 0))
_VEC = pl.BlockSpec((1, D), lambda i: (0, 0))


def _rms_fwd(name, x, g, after=()):
    def body(x_ref, g_ref, *rest):
        h_ref = rest[-1]
        xv = x_ref[...]
        rstd = lax.rsqrt(jnp.mean(xv * xv, axis=-1, keepdims=True) + EPS)
        h_ref[...] = (xv * rstd * g_ref[...]).astype(BF16)

    return pl.pallas_call(body, name=name, grid=(NE,), in_specs=[_ROW, _VEC] + [_ANY] * len(after), out_specs=_ROW,
                          out_shape=_sds((T, D), BF16), compiler_params=_params(("parallel",)))(x, g, *after)


def _mm_x2_loss_head(s, w_ff2, x1, target, g):
    k_dim = w_ff2.shape[0]
    row = pl.BlockSpec((TM, D), lambda i, j, k: (i, 0))
    vec = pl.BlockSpec((1, D), lambda i, j, k: (0, 0))

    def epilogue(r, ex, outs, first):
        x1_ref, t_ref, g_ref = ex
        loss_ref, dx_ref, dxb_ref, dg_ref = outs
        xv = x1_ref[...] + r
        rstd = lax.rsqrt(jnp.mean(xv * xv, axis=-1, keepdims=True) + EPS)
        xhat = xv * rstd
        gv = g_ref[...]
        err = xhat * gv - t_ref[...]
        dy = err * (1.0 / D)
        dxh = dy * gv
        dx = rstd * (dxh - xhat * jnp.mean(dxh * xhat, axis=-1, keepdims=True))
        dx_ref[...] = dx
        dxb_ref[...] = dx.astype(BF16)
        dg_part = jnp.sum(dy * xhat, axis=0, keepdims=True)
        loss_part = (0.5 / D) * jnp.sum(jnp.sum(err * err, axis=1, keepdims=True), axis=0, keepdims=True)

        @pl.when(first)
        def _():
            dg_ref[...] = dg_part
            loss_ref[...] = loss_part

        @pl.when(jnp.logical_not(first))
        def _():
            dg_ref[...] += dg_part
            loss_ref[...] += loss_part

    return _matmul(
        "mm_x2_loss_head", s, w_ff2, dims=NN, grid=(NI, 1, k_dim // D),
        a_spec=pl.BlockSpec((TM, D), lambda i, j, k: (i, k)), b_spec=pl.BlockSpec((D, D), lambda i, j, k: (k, 0)),
        out_shapes=[_sds((1, 1), F32), _sds((T, D), F32), _sds((T, D), BF16), _sds((1, D), F32)],
        out_specs=[pl.BlockSpec((1, 1), lambda i, j, k: (0, 0)), row, row, vec], acc_shape=(TM, D),
        extras=[x1, target, g], extra_specs=[row, row, vec], epilogue=epilogue,
        semantics=("arbitrary", "arbitrary", "arbitrary"), epilogue_takes_first=True)


MW = 512
_G_ATT_BLK = 3584 // MW
_G_REC_BLK = 4608 // MW


TB = 512


def _branch_specs():
    def row(cols):
        return pl.BlockSpec((TB, cols), lambda i: (i, 0))

    ga = pl.BlockSpec((TB, MW), lambda i: (i, _G_ATT_BLK))
    ga2 = pl.BlockSpec((TB, MW), lambda i: (i, _G_ATT_BLK + 1))
    gr = pl.BlockSpec((TB, MW), lambda i: (i, _G_REC_BLK))
    gr2 = pl.BlockSpec((TB, MW), lambda i: (i, _G_REC_BLK + 1))
    w_att = pl.BlockSpec((N_CHIPS, D_ATT, D // N_CHIPS), lambda i: (0, 0, 0))
    w_sq = pl.BlockSpec((D, D), lambda i: (0, 0))
    return row, (ga, ga2, gr, gr2), w_att, w_sq


def _gate_values(gate_refs):
    ga, ga2, gr, gr2 = (r[...].astype(F32) for r in gate_refs)
    return _sigmoid(jnp.concatenate([ga, ga2], axis=1)), _sigmoid(jnp.concatenate([gr, gr2], axis=1))


def _branches_fwd(att, g, z, x, w_att_o, w_rec_o, w_out, ln2_g):
    row, gate_specs, w_att, w_sq = _branch_specs()

    def body(att_ref, g_ref, ga_ref, ga2_ref, gr_ref, gr2_ref, x_ref, wa_ref, wr_ref, wo_ref, g2_ref,
             ya_ref, yr_ref, m_ref, x1_ref, h2_ref):
        attv = att_ref[...]
        ya = jnp.concatenate([_dot(attv, wa_ref[j], NN) for j in range(N_CHIPS)], axis=1)
        yr = _dot(g_ref[...], wr_ref[...], NN)
        sa, sr = _gate_values((ga_ref, ga2_ref, gr_ref, gr2_ref))
        mixed = (sa * ya + sr * yr).astype(BF16)
        ya_ref[...] = ya
        yr_ref[...] = yr
        m_ref[...] = mixed
        x1 = x_ref[...] + _dot(mixed, wo_ref[...], NN)
        x1_ref[...] = x1
        rstd = lax.rsqrt(jnp.mean(x1 * x1, axis=-1, keepdims=True) + EPS)
        h2_ref[...] = (x1 * rstd * g2_ref[...]).astype(BF16)

    return pl.pallas_call(
        body, name="branches_fwd", grid=(T // TB,),
        in_specs=[row(D_ATT), row(D), *gate_specs, row(D), w_att, w_sq, w_sq, pl.BlockSpec((1, D), lambda i: (0, 0))],
        out_specs=[row(D)] * 5,
        out_shape=[_sds((T, D), F32), _sds((T, D), F32), _sds((T, D), BF16), _sds((T, D), F32), _sds((T, D), BF16)],
        compiler_params=_params(("parallel",)))(att, g, z, z, z, z, x, w_att_o, w_rec_o, w_out, ln2_g)


def _branches_bwd(dx1_b, y_att, y_rec, z, w_att_o, w_rec_o, w_out):
    row, gate_specs, w_att, w_sq = _branch_specs()
    n4 = D // N_CHIPS

    def body(dx_ref, ya_ref, yr_ref, ga_ref, ga2_ref, gr_ref, gr2_ref, wa_ref, wr_ref, wo_ref,
             dya_ref, dyr_ref, dga_ref, dgr_ref, datt_ref, dg_ref):
        dm = _dot(dx_ref[...], wo_ref[...], NT)
        sa, sr = _gate_values((ga_ref, ga2_ref, gr_ref, gr2_ref))
        dya = (dm * sa).astype(BF16)
        dyr = (dm * sr).astype(BF16)
        dya_ref[...] = dya
        dyr_ref[...] = dyr
        dga_ref[...] = (dm * ya_ref[...] * sa * (1.0 - sa)).astype(BF16)
        dgr_ref[...] = (dm * yr_ref[...] * sr * (1.0 - sr)).astype(BF16)
        datt = _dot(dya[:, 0:n4], wa_ref[0], NT)
        for j in range(1, N_CHIPS):
            datt = datt + _dot(dya[:, j * n4:(j + 1) * n4], wa_ref[j], NT)
        datt_ref[...] = datt.astype(BF16)
        dg_ref[...] = _dot(dyr, wr_ref[...], NT).astype(BF16)

    return pl.pallas_call(
        body, name="branches_bwd", grid=(T // TB,),
        in_specs=[row(D), row(D), row(D), *gate_specs, w_att, w_sq, w_sq],
        out_specs=[row(D)] * 4 + [row(D_ATT), row(D)],
        out_shape=[_sds((T, D), BF16)] * 4 + [_sds((T, D_ATT), BF16), _sds((T, D), BF16)],
        compiler_params=_params(("parallel",)))(dx1_b, y_att, y_rec, z, z, z, z, w_att_o, w_rec_o, w_out)


HP = 2 * HEAD_DIM
N_HP = N_HEADS // 2
ATT_UNROLL_FWD = 16
ATT_UNROLL_BWD = 8
DIAG_ROWS = 32


def _window_maps():
    diag = np.zeros((GRID_W * GRID_W, 128), np.float32)
    for qc in range(GRID_W):
        w0 = min(max(qc - WIN_W // 2, 0), GRID_W - WIN_W)
        for kc in range(w0, w0 + WIN_W):
            diag[qc * GRID_W + kc, kc - qc + WIN_W - 1] = 1.0
    return diag, diag.sum(axis=1)[None, :]


def _split3(x):
    a = x.astype(BF16)
    r = x - a.astype(F32)
    b = r.astype(BF16)
    c = (r - b.astype(F32)).astype(BF16)
    return a, b, c


N_DROW = 2 * WIN_H - 1
N_DPAIR = N_DROW - 1


def _bias_pairs(rpb):
    diag, valid = _window_maps()
    r2 = jnp.pad(rpb.reshape(N_HEADS * N_DROW, 2 * WIN_W - 1),
                 ((0, 128 - N_HEADS * N_DROW), (0, 128 - (2 * WIN_W - 1))))

    def body(r_ref, d_ref, v_ref, o_ref):
        dv = d_ref[...]
        t = sum(_dot(part, dv, NN) for part in _split3(r_ref[...]))
        o_ref[...] = jnp.where(v_ref[...] > 0.0, t, -1e30)

    t = pl.pallas_call(body, name="rpb_expand", out_shape=_sds((128, GRID_W * GRID_W), F32),
                       compiler_params=_params())(r2, jnp.asarray(diag.T, BF16), jnp.asarray(valid, F32))
    t = t[:N_HEADS * N_DROW].reshape(N_HEADS, N_DROW, GRID_W, GRID_W)
    return jnp.concatenate([t[:, :N_DPAIR], t[:, 1:]], axis=-1)


def _row_bias(tb_ref, hh, d0):
    return jnp.concatenate([tb_ref[hh, d0 + 2 * ii] for ii in range(WIN_H // 2)], axis=1)


def _row_window(r):
    rs = jnp.clip(r - WIN_H // 2, 0, N_ROWS - WIN_H)
    return pl.multiple_of(r * GRID_W, GRID_W), pl.multiple_of(rs * GRID_W, GRID_W), rs - r + (WIN_H - 1)


def _split_heads(src_ref, dst_ref, scale=None):
    for hh in range(2):
        v = src_ref[:, hh * HEAD_DIM:(hh + 1) * HEAD_DIM]
        dst_ref[hh] = (v if scale is None else v * scale).astype(BF16)


def _attn_items(qb_ref, kb_ref, vb_ref, tb_ref, first_row, n_rows):
    wins = [_row_window(first_row + u) for u in range(n_rows)]
    items = [(u, hh) for u in range(n_rows) for hh in range(2)]
    q = [qb_ref[hh, pl.ds(wins[u][0], GRID_W), :] for u, hh in items]
    k = [kb_ref[hh, pl.ds(wins[u][1], KEYS), :] for u, hh in items]
    v = [vb_ref[hh, pl.ds(wins[u][1], KEYS), :] for u, hh in items]
    s = [_dot(qi, ki, NT) + _row_bias(tb_ref, hh, wins[u][2]) for qi, ki, (u, hh) in zip(q, k, items)]
    m = [jnp.max(si, axis=-1, keepdims=True) for si in s]
    e = [jnp.exp(si - mi) for si, mi in zip(s, m)]
    inv = [1.0 / jnp.sum(ei, axis=-1, keepdims=True) for ei in e]
    p = [ei * li for ei, li in zip(e, inv)]
    return wins, items, q, k, v, p


def _attn_in_specs():
    q = pl.BlockSpec((T, HP), lambda p: (0, p))
    k = pl.BlockSpec((T, HP), lambda p: (0, N_HP + p))
    v = pl.BlockSpec((T, HP), lambda p: (0, 2 * N_HP + p))
    tb = pl.BlockSpec((2, N_DPAIR, GRID_W, HP), lambda p: (p, 0, 0, 0))
    return q, k, v, tb


_HEAD_SCRATCH = pltpu.VMEM((2, T, HEAD_DIM), BF16)


_PROBS = pl.BlockSpec((T, 2 * KEYS), lambda p: (0, p))


def _attn_fwd(z, tb):
    def body(q_ref, k_ref, v_ref, tb_ref, o_ref, p_ref, qb_ref, kb_ref, vb_ref):
        _split_heads(q_ref, qb_ref, SCALE)
        _split_heads(k_ref, kb_ref)
        _split_heads(v_ref, vb_ref)

        def rows(it, carry):
            wins, items, _, _, v, p = _attn_items(qb_ref, kb_ref, vb_ref, tb_ref, it * ATT_UNROLL_FWD, ATT_UNROLL_FWD)
            pb = [pi.astype(BF16) for pi in p]
            o = [_dot(pi, vi, NN) for pi, vi in zip(pb, v)]
            for u, (q0, _, _) in enumerate(wins):
                o_ref[pl.ds(q0, GRID_W), :] = jnp.concatenate(o[2 * u:2 * u + 2], axis=1).astype(BF16)
                p_ref[pl.ds(q0, GRID_W), :] = jnp.concatenate(pb[2 * u:2 * u + 2], axis=1)
            return carry

        lax.fori_loop(0, N_ROWS // ATT_UNROLL_FWD, rows, 0)

    blk = pl.BlockSpec((T, HP), lambda p: (0, p))
    return pl.pallas_call(
        body, name="attn_fwd", grid=(N_HP,), in_specs=list(_attn_in_specs()), out_specs=[blk, _PROBS],
        out_shape=[_sds((T, D_ATT), BF16), _sds((T, N_HEADS * KEYS), BF16)], scratch_shapes=[_HEAD_SCRATCH] * 3,
        compiler_params=_params(("parallel",)))(z, z, z, tb)


def _attn_bwd(z, probs, d_att, after=()):
    def body(q_ref, k_ref, v_ref, p_ref, do_ref, flip_ref, *rest):
        (dq_ref, dk_ref, dv_ref, diag_ref, qb_ref, kb_ref, vb_ref, dob_ref, dka_ref, dva_ref,
         ds_ref) = rest[len(after):]
        _split_heads(q_ref, qb_ref, SCALE)
        _split_heads(k_ref, kb_ref)
        _split_heads(v_ref, vb_ref)
        _split_heads(do_ref, dob_ref)
        dka_ref[...] = jnp.zeros_like(dka_ref)
        dva_ref[...] = jnp.zeros_like(dva_ref)
        ds_ref[...] = jnp.zeros_like(ds_ref)

        def rows(it, carry):
            wins = [_row_window(it * ATT_UNROLL_BWD + u) for u in range(ATT_UNROLL_BWD)]
            items = [(u, hh) for u in range(ATT_UNROLL_BWD) for hh in range(2)]
            q = [qb_ref[hh, pl.ds(wins[u][0], GRID_W), :] for u, hh in items]
            k = [kb_ref[hh, pl.ds(wins[u][1], KEYS), :] for u, hh in items]
            v = [vb_ref[hh, pl.ds(wins[u][1], KEYS), :] for u, hh in items]
            pb = [p_ref[pl.ds(wins[u][0], GRID_W), hh * KEYS:(hh + 1) * KEYS] for u, hh in items]
            p = [pi.astype(F32) for pi in pb]
            do = [dob_ref[hh, pl.ds(wins[u][0], GRID_W), :] for u, hh in items]
            dv = [_dot(pi, di, TN) for pi, di in zip(pb, do)]
            dp = [_dot(di, vi, NT) for di, vi in zip(do, v)]
            ds = [pi * (dpi - jnp.sum(dpi * pi, axis=-1, keepdims=True)) for pi, dpi in zip(p, dp)]
            dsb = [d.astype(BF16) for d in ds]
            dq = [_dot(d, ki, NN) * SCALE for d, ki in zip(dsb, k)]
            dk = [_dot(d, qi, TN) for d, qi in zip(dsb, q)]
            for d, (u, hh) in zip(ds, items):
                for ii in range(WIN_H // 2):
                    ds_ref[hh, wins[u][2] + 2 * ii] += d[:, ii * HP:(ii + 1) * HP]
            for u, (q0, _, _) in enumerate(wins):
                dq_ref[pl.ds(q0, GRID_W), :] = jnp.concatenate(dq[2 * u:2 * u + 2], axis=1).astype(BF16)
            for dki, dvi, (u, hh) in zip(dk, dv, items):
                dka_ref[hh, pl.ds(wins[u][1], KEYS), :] += dki
                dva_ref[hh, pl.ds(wins[u][1], KEYS), :] += dvi
            return carry

        lax.fori_loop(0, N_ROWS // ATT_UNROLL_BWD, rows, 0)
        dk_ref[...] = jnp.concatenate([dka_ref[0], dka_ref[1]], axis=1).astype(BF16)
        dv_ref[...] = jnp.concatenate([dva_ref[0], dva_ref[1]], axis=1).astype(BF16)
        _diag_sums(ds_ref, flip_ref, diag_ref)

    blk = pl.BlockSpec((T, HP), lambda p: (0, p))
    q, k, v, _ = _attn_in_specs()
    flip =jnp.asarray(np.eye(HP, dtype=np.float32)[::-1], BF16)
    return pl.pallas_call(
        body, name="attn_bwd", grid=(N_HP,),
        in_specs=[q, k, v, _PROBS, blk, pl.BlockSpec((HP, HP), lambda p: (0, 0))] + [_ANY] * len(after),
        out_specs=[blk, blk, blk, pl.BlockSpec((None, DIAG_ROWS, HP), lambda p: (p, 0, 0))],
        out_shape=[_sds((T, D_ATT), BF16)] * 3 + [_sds((N_HP, DIAG_ROWS, HP), F32)],
        scratch_shapes=[_HEAD_SCRATCH] * 4 + [pltpu.VMEM((2, T, HEAD_DIM), F32), pltpu.VMEM((2, T, HEAD_DIM), F32),
                                              pltpu.VMEM((2, N_DPAIR, GRID_W, HP), F32)],
        compiler_params=_params(("parallel",)))(z, z, z, probs, d_att, flip, *after)


def _diag_sums(acc_ref, flip_ref, out_ref):
    flip = flip_ref[...]
    rows = []
    for hh in range(2):
        for pair in range(N_DPAIR):
            reversed_lanes = sum(_dot(part, flip, NN) for part in _split3(acc_ref[hh, pair]))
            skewed = pltpu.roll(reversed_lanes, 0, 1, stride=1, stride_axis=0)
            rows.append(jnp.sum(skewed, axis=0, keepdims=True))
    rows.append(jnp.zeros((DIAG_ROWS - len(rows), HP), F32))
    out_ref[...] = jnp.concatenate(rows, axis=0)


def _rpb_grad(diag_sums):
    g = diag_sums.reshape(N_HP * DIAG_ROWS, HP)
    sel = np.zeros((2, 128, N_HP * DIAG_ROWS), np.float32)
    lane = np.zeros((2, HP, 128), np.float32)
    for h in range(N_HEADS):
        for pair in range(N_DPAIR):
            for half in range(2):
                sel[half, h * N_DROW + pair + half, (h // 2) * DIAG_ROWS + (h % 2) * N_DPAIR + pair] = 1.0
    for j in range(2 * WIN_W - 1):
        for half in range(2):
            lane[half, (HP - 1 - GRID_W * half - (j - (WIN_W - 1))) % HP, j] = 1.0

    def body(g_ref, sel_ref, lane_ref, o_ref):
        parts = _split3(g_ref[...])
        total = None
        for half in range(2):
            picked = sum(_dot(sel_ref[half], part, NN) for part in parts)
            term = sum(_dot(part, lane_ref[half], NN) for part in _split3(picked))
            total = term if total is None else total + term
        o_ref[...] = total

    out = pl.pallas_call(body, name="rpb_grad", out_shape=_sds((128, 128), F32),
                         compiler_params=_params())(g, jnp.asarray(sel, BF16), jnp.asarray(lane, BF16))
    return out[:N_HEADS * N_DROW, :2 * WIN_W - 1].reshape(N_HEADS, N_DROW, 2 * WIN_W - 1)


N_CB = D // REC_CB
N_CHUNK = T // REC_CHUNK
N_TILE = T // 8
_U_BLK = 1536 // REC_CB
_Y_BLK = 2560 // REC_CB


def _block_diag(w):
    per = REC_CB // 64
    wt = w.reshape(2, N_CB, per, 64, 64)
    eye = jnp.eye(per, dtype=w.dtype)
    full = wt[:, :, :, :, None, :] * eye[None, None, :, None, :, None]
    return full.reshape(2, N_CB, REC_CB, REC_CB).astype(BF16)


def _gelu(x):
    c = 0.7978845608028654
    return 0.5 * x * (1.0 + jnp.tanh(c * (x + 0.044715 * x * x * x)))


def _gelu_grad(x):
    c = 0.7978845608028654
    th = jnp.tanh(c * (x + 0.044715 * x * x * x))
    return 0.5 * (1.0 + th) + 0.5 * x * (1.0 - th * th) * c * (1.0 + 3.0 * 0.044715 * x * x)


def _softplus_neg(lam):
    x = -lam
    e = jnp.exp(-jnp.abs(x))
    w = 1.0 + e
    l1p = jnp.where(w == 1.0, e, jnp.log(w) * e / (w - 1.0))
    return jnp.maximum(x, 0.0) + l1p


def _one_minus_exp(x):
    poly = x * (1.0 + x * (1 / 2 + x * (1 / 6 + x * (1 / 24 + x * (1 / 120 + x * (1 / 720))))))
    return jnp.where(x > -0.125, -poly, 1.0 - jnp.exp(x))


def _conv_taps(pad_ref, t0, w, sign):
    out = None
    for j in range(4):
        term = w[j:j + 1, :] * pad_ref[pl.ds(PAD + t0 + sign * (j - 2), REC_CHUNK), :]
        out = term if out is None else out + term
    return out


def _gates(u, wa, wi, ba, bi, sp):
    ub = u.astype(BF16)
    r = _sigmoid(_dot(ub, wa, NN) + ba)
    i = _sigmoid(_dot(ub, wi, NN) + bi)
    log_a = -LRU_C * r * sp
    a = jnp.exp(log_a)
    x = jnp.maximum(_one_minus_exp(2.0 * log_a), 0.0)
    positive = x > 0.0
    inv = lax.rsqrt(jnp.where(positive, x, 1.0))
    mult = jnp.where(positive, x * inv, 0.0)
    return r, i, a, mult, jnp.where(positive, inv, 0.0)


def _tile_scan(a, b, sub, reverse):
    for s in (1, 2, 4):
        if reverse:
            a_s, b_s, m = pltpu.roll(a, 8 - s, 0), pltpu.roll(b, 8 - s, 0), sub < 8 - s
        else:
            a_s, b_s, m = pltpu.roll(a, s, 0), pltpu.roll(b, s, 0), sub >= s
        b = jnp.where(m, a * b_s + b, b)
        a = jnp.where(m, a * a_s, a)
    return a, b


def _last_row(x, row):
    return jnp.broadcast_to(x[row:row + 1, :], x.shape)


def _rec_prologue(up_ref, cw_ref, cb_ref, wa_ref, wi_ref, ba_ref, bi_ref, lam_ref,
                  upad_ref, u_ref, a_refs, h_refs):
    cb = up_ref.shape[1]
    zeros = jnp.zeros((PAD, cb), F32)
    upad_ref[pl.ds(0, PAD), :] = zeros
    upad_ref[pl.ds(PAD + T, PAD), :] = zeros
    upad_ref[pl.ds(PAD, T), :] = up_ref[...].astype(F32)
    cw = cw_ref[...]
    sp = _softplus_neg(lam_ref[...])
    for c in range(N_CHUNK):
        t0 = c * REC_CHUNK
        u = cb_ref[...] + _conv_taps(upad_ref, t0, cw, 1)
        u_ref[pl.ds(t0, REC_CHUNK), :] = u
        for d in range(2):
            _, i, a, mult, _ = _gates(u, wa_ref[d], wi_ref[d], ba_ref[d:d + 1, :], bi_ref[d:d + 1, :], sp[d:d + 1, :])
            a_refs[d][pl.ds(t0, REC_CHUNK), :] = a
            h_refs[d][pl.ds(t0, REC_CHUNK), :] = mult * (i * u)

    sub = lax.broadcasted_iota(jnp.int32, (8, cb), 0)

    def tile(k, carry):
        cf, cr = carry
        tf = pl.multiple_of(k * 8, 8)
        tr = pl.multiple_of((N_TILE - 1 - k) * 8, 8)
        af, bf = _tile_scan(a_refs[0][pl.ds(tf, 8), :], h_refs[0][pl.ds(tf, 8), :], sub, False)
        hf = af * cf + bf
        h_refs[0][pl.ds(tf, 8), :] = hf
        ar, br = _tile_scan(a_refs[1][pl.ds(tr, 8), :], h_refs[1][pl.ds(tr, 8), :], sub, True)
        hr = ar * cr + br
        h_refs[1][pl.ds(tr, 8), :] = hr
        return _last_row(af, 7) * cf + _last_row(bf, 7), _last_row(ar, 0) * cr + _last_row(br, 0)

    z8 = jnp.zeros((8, cb), F32)
    lax.fori_loop(0, N_TILE, tile, (z8, z8))
    return sp


def _rec_specs():
    up = pl.BlockSpec((T, REC_CB), lambda c: (0, _U_BLK + c))
    yb = pl.BlockSpec((T, REC_CB), lambda c: (0, _Y_BLK + c))
    cw = pl.BlockSpec((4, REC_CB), lambda c: (0, c))
    cbias = pl.BlockSpec((1, REC_CB), lambda c: (0, c))
    wbd = pl.BlockSpec((2, None, REC_CB, REC_CB), lambda c: (0, c, 0, 0))
    vec2 = pl.BlockSpec((2, REC_CB), lambda c: (0, c))
    col = pl.BlockSpec((T, REC_CB), lambda c: (0, c))
    return up, yb, cw, cbias, wbd, vec2, col


def _rec_fwd(z, conv_w, conv_b, wa, wi, ba, bi, lam):
    up, yb, cw, cbias, wbd, vec2, col = _rec_specs()

    def body(up_ref, yb_ref, cw_ref, cb_ref, wa_ref, wi_ref, ba_ref, bi_ref, lam_ref, g_ref,
             u_ref, af_ref, ar_ref, hf_ref, hr_ref, upad_ref):
        _rec_prologue(up_ref, cw_ref, cb_ref, wa_ref, wi_ref, ba_ref, bi_ref, lam_ref,
                      upad_ref, u_ref, (af_ref, ar_ref), (hf_ref, hr_ref))

        def chunk(c, carry):
            t0 = pl.multiple_of(c * REC_CHUNK, REC_CHUNK)
            rows = pl.ds(t0, REC_CHUNK)
            g_ref[rows, :] = ((hf_ref[rows, :] + hr_ref[rows, :]) * _gelu(yb_ref[rows, :].astype(F32))).astype(BF16)
            return carry

        lax.fori_loop(0, N_CHUNK, chunk, 0)

    res = pl.pallas_call(
        body, name="rec_fwd", grid=(N_CB,),
        in_specs=[up, yb, cw, cbias, wbd, wbd, vec2, vec2, vec2], out_specs=[col] * 6,
        out_shape=[_sds((T, D), BF16)] + [_sds((T, D), F32)] * 5,
        scratch_shapes=[pltpu.VMEM((T + 2 * PAD, REC_CB), F32)],
        compiler_params=_params(("parallel",)))(z, z, conv_w, conv_b, wa, wi, ba, bi, lam)
    return res[0], tuple(res[1:])


def _rec_bwd(z, dg, saved, conv_w, conv_b, wa, wi, ba, bi, lam, after=()):
    up, yb, cw, cbias, wbd, vec2, col = _rec_specs()

    def body(up_ref, yb_ref, dg_ref, u_ref, af_ref, ar_ref, hf_ref, hr_ref,
             cw_ref, cb_ref, wa_ref, wi_ref, ba_ref, bi_ref, lam_ref, *rest):
        (dup_ref, dyb_ref, dcw_ref, dcb_ref, dwa_out, dwi_out, dba_ref, dbi_ref, dlam_ref,
         upad_ref, dh_ref, gf_ref, gr_ref, daf_ref, dar_ref, dupad_ref, dwa_ref, dwi_ref) = rest[len(after):]
        g_refs, da_refs = (gf_ref, gr_ref), (daf_ref, dar_ref)
        cb = up_ref.shape[1]
        zeros = jnp.zeros((PAD, cb), F32)
        upad_ref[pl.ds(0, PAD), :] = zeros
        upad_ref[pl.ds(PAD + T, PAD), :] = zeros
        upad_ref[pl.ds(PAD, T), :] = up_ref[...].astype(F32)
        sp = _softplus_neg(lam_ref[...])

        def gate_chunk(c, carry):
            t0 = pl.multiple_of(c * REC_CHUNK, REC_CHUNK)
            rows = pl.ds(t0, REC_CHUNK)
            y = yb_ref[rows, :].astype(F32)
            dgv = dg_ref[rows, :].astype(F32)
            dh_ref[rows, :] = dgv * _gelu(y)
            dyb_ref[rows, :] = (dgv * (hf_ref[rows, :] + hr_ref[rows, :]) * _gelu_grad(y)).astype(BF16)
            return carry

        lax.fori_loop(0, N_CHUNK, gate_chunk, 0)

        sub = lax.broadcasted_iota(jnp.int32, (8, cb), 0)

        def tile(k, carry):
            cf, cr = carry
            kf = N_TILE - 1 - k
            tf = pl.multiple_of(kf * 8, 8)
            tnext = pl.multiple_of(jnp.minimum(kf + 1, N_TILE - 1) * 8, 8)
            tprev = pl.multiple_of(jnp.maximum(kf - 1, 0) * 8, 8)
            a_t = af_ref[pl.ds(tf, 8), :]
            a_n = jnp.where(kf < N_TILE - 1, af_ref[pl.ds(tnext, 8), :], 0.0)
            a_sh = jnp.where(sub == 7, pltpu.roll(a_n, 7, 0), pltpu.roll(a_t, 7, 0))
            ca, cbb = _tile_scan(a_sh, dh_ref[pl.ds(tf, 8), :], sub, True)
            gf = ca * cf + cbb
            h_t = hf_ref[pl.ds(tf, 8), :]
            h_p = jnp.where(kf > 0, hf_ref[pl.ds(tprev, 8), :], 0.0)
            h_sh = jnp.where(sub == 0, pltpu.roll(h_p, 1, 0), pltpu.roll(h_t, 1, 0))
            gf_ref[pl.ds(tf, 8), :] = gf
            daf_ref[pl.ds(tf, 8), :] = gf * h_sh
            tr = pl.multiple_of(k * 8, 8)
            rnext = pl.multiple_of(jnp.minimum(k + 1, N_TILE - 1) * 8, 8)
            rprev = pl.multiple_of(jnp.maximum(k - 1, 0) * 8, 8)
            b_t = ar_ref[pl.ds(tr, 8), :]
            b_p = jnp.where(k > 0, ar_ref[pl.ds(rprev, 8), :], 0.0)
            b_sh = jnp.where(sub == 0, pltpu.roll(b_p, 1, 0), pltpu.roll(b_t, 1, 0))
            ra, rb = _tile_scan(b_sh, dh_ref[pl.ds(tr, 8), :], sub, False)
            gr = ra * cr + rb
            hr_t = hr_ref[pl.ds(tr, 8), :]
            hr_n = jnp.where(k < N_TILE - 1, hr_ref[pl.ds(rnext, 8), :], 0.0)
            hr_sh = jnp.where(sub == 7, pltpu.roll(hr_n, 7, 0), pltpu.roll(hr_t, 7, 0))
            gr_ref[pl.ds(tr, 8), :] = gr
            dar_ref[pl.ds(tr, 8), :] = gr * hr_sh
            return _last_row(gf, 0), _last_row(gr, 7)

        z8 = jnp.zeros((8, cb), F32)
        lax.fori_loop(0, N_TILE, tile, (z8, z8))

        dupad_ref[pl.ds(0, PAD), :] = zeros
        dupad_ref[pl.ds(PAD + T, PAD), :] = zeros
        dwa_ref[...] = jnp.zeros_like(dwa_ref)
        dwi_ref[...] = jnp.zeros_like(dwi_ref)
        dba_ref[...] = jnp.zeros_like(dba_ref)
        dbi_ref[...] = jnp.zeros_like(dbi_ref)
        dlam_ref[...] = jnp.zeros_like(dlam_ref)

        def grad_chunk(c, carry):
            t0 = pl.multiple_of(c * REC_CHUNK, REC_CHUNK)
            rows = pl.ds(t0, REC_CHUNK)
            u = u_ref[rows, :]
            ub = u.astype(BF16)
            du = jnp.zeros((REC_CHUNK, cb), F32)
            for d in range(2):
                r, i, a, mult, inv_mult = _gates(u, wa_ref[d], wi_ref[d], ba_ref[d:d + 1, :], bi_ref[d:d + 1, :],
                                                 sp[d:d + 1, :])
                dbx = g_refs[d][rows, :]
                dmult = dbx * (i * u)
                diu = dbx * mult
                a2 = a * a
                dlog = da_refs[d][rows, :] * a - dmult * (a2 * inv_mult)
                dpa = (dlog * (-LRU_C) * sp[d:d + 1, :]) * r * (1.0 - r)
                dpi = (diu * u) * i * (1.0 - i)
                dpab, dpib = dpa.astype(BF16), dpi.astype(BF16)
                du = du + diu * i + _dot(dpab, wa_ref[d], NT) + _dot(dpib, wi_ref[d], NT)
                dwa_ref[d] += _dot(ub, dpab, TN)
                dwi_ref[d] += _dot(ub, dpib, TN)
                dba_ref[d:d + 1, :] += jnp.sum(dpa, axis=0, keepdims=True)
                dbi_ref[d:d + 1, :] += jnp.sum(dpi, axis=0, keepdims=True)
                dlam_ref[d:d + 1, :] += jnp.sum(dlog * r, axis=0, keepdims=True)
            dupad_ref[pl.ds(PAD + t0, REC_CHUNK), :] = du
            return carry

        lax.fori_loop(0, N_CHUNK, grad_chunk, 0)
        dlam_ref[...] = dlam_ref[...] * (LRU_C * _sigmoid(-lam_ref[...]))
        for d in range(2):
            for blk in range(REC_CB // REC_BLOCK):
                lo, hi = blk * REC_BLOCK, (blk + 1) * REC_BLOCK
                dwa_out[d, blk] = dwa_ref[d, lo:hi, lo:hi]
                dwi_out[d, blk] = dwi_ref[d, lo:hi, lo:hi]

        cw = cw_ref[...]
        dcb = jnp.zeros((1, cb), F32)
        dcw = [jnp.zeros((1, cb), F32) for _ in range(4)]
        for c in range(N_CHUNK):
            t0 = c * REC_CHUNK
            du = dupad_ref[pl.ds(PAD + t0, REC_CHUNK), :]
            dcb = dcb + jnp.sum(du, axis=0, keepdims=True)
            for j in range(4):
                dcw[j] = dcw[j] + jnp.sum(du * upad_ref[pl.ds(PAD + t0 + j - 2, REC_CHUNK), :], axis=0, keepdims=True)
            dup_ref[pl.ds(t0, REC_CHUNK), :] = _conv_taps(dupad_ref, t0, cw, -1).astype(BF16)
        dcb_ref[...] = dcb
        dcw_ref[...] = jnp.concatenate(dcw, axis=0)

    full = pltpu.VMEM((T, REC_CB), F32)
    padded = pltpu.VMEM((T + 2 * PAD, REC_CB), F32)
    per = REC_CB // REC_BLOCK
    diag = pl.BlockSpec((2, per, REC_BLOCK, REC_BLOCK), lambda c: (0, c, 0, 0))
    return pl.pallas_call(
        body, name="rec_bwd", grid=(N_CB,),
        in_specs=[up, yb] + [col] * 6 + [cw, cbias, wbd, wbd, vec2, vec2, vec2] + [_ANY] * len(after),
        out_specs=[col, col, cw, cbias, diag, diag, vec2, vec2, vec2],
        out_shape=[_sds((T, D), BF16), _sds((T, D), BF16), _sds((4, D), F32), _sds((1, D), F32),
                   _sds((2, D // REC_BLOCK, REC_BLOCK, REC_BLOCK), F32), _sds((2, D // REC_BLOCK, REC_BLOCK, REC_BLOCK), F32),
                   _sds((2, D), F32), _sds((2, D), F32), _sds((2, D), F32)],
        scratch_shapes=[padded, full, full, full, full, full, padded,
                        pltpu.VMEM((2, REC_CB, REC_CB), F32), pltpu.VMEM((2, REC_CB, REC_CB), F32)],
        compiler_params=_params(("parallel",)))(z, z, dg, *saved, conv_w, conv_b, wa, wi, ba, bi, lam, *after)


class _NoReducer:
    def begin(self, tag, grads):
        return ()

    def advance(self, tag, after):
        return ()

    def interlude(self, tokens):
        return None


def _local_step(x, target, p, late=None, reducer=_NoReducer()):
    x = x.reshape(T, D)
    target = target.reshape(T, D)
    tb = _bias_pairs(p["rpb"])
    wa, wi = _block_diag(p["w_rg_a"]), _block_diag(p["w_rg_i"])

    h1 = _rms_fwd("rms1_fwd", x, p["ln1_g"], after=late[0] if late else ())
    if late:
        p = {**p, **late[1]((h1, tb, wa, wi))}
    rec_params = (p["conv_w"], p["conv_b"], wa, wi, p["b_rg_a"], p["b_rg_i"], p["lru_lambda"])
    (z,) = _mm_nn_cols("mm_z", h1, p["w_in"], BF16, bias=p["b_in"])
    att, probs = _attn_fwd(z, tb)
    g, rec_saved = _rec_fwd(z, *rec_params)
    if late:
        p = {**p, **late[2](g)}
    y_att, y_rec, mixed, x1, h2 = _branches_fwd(att, g, z, x, p["w_att_o"], p["w_rec_o"], p["w_out"], p["ln2_g"])

    def relu2(r, ex, outs):
        rp = jnp.maximum(r, 0.0)
        outs[0][...] = (rp * rp).astype(BF16)

    (s,) = _mm_nn_cols("mm_ff1", h2, p["w_ff1"], BF16, epilogue=relu2)
    loss, dx2, dx2_b, g_lnf = _mm_x2_loss_head(s, p["w_ff2"], x1, target, p["lnf_g"])

    def relu2_bwd(r, ex, outs):
        outs[0][...] = (r * 2.0 * jnp.sqrt(ex[0][...].astype(F32))).astype(BF16)

    (df,) = _mm_nt_rows("mm_df", dx2_b, p["w_ff2"], BF16, tn=D, extras=[s],
                        extra_specs=[pl.BlockSpec((TJ, D), lambda j, i, k: (i, j))], epilogue=relu2_bwd)
    (g_w_ff2,) = _mm_tn_rows("mm_g_ff2", s, dx2_b, tm=D)
    (g_w_ff1,) = _mm_tn_cols("mm_g_ff1", h2, df, D)
    tok = reducer.begin("ff", dict(w_ff2=g_w_ff2, w_ff1=g_w_ff1))
    dx1, dx1_b, g_ln2 = _mm_nt_cols_rms_bwd("mm_dh2_rms2_bwd", df, p["w_ff1"], x1, p["ln2_g"], dx2, after=tok,
                                            bf16_copy=True)

    dy_att, dy_rec, dg_att, dg_rec, d_att, d_g = _branches_bwd(dx1_b, y_att, y_rec, z, p["w_att_o"], p["w_rec_o"],
                                                               p["w_out"])
    (g_w_out,) = _mm_tn_rows("mm_g_out", mixed, dx1_b, tm=D)
    (g_w_att_o,) = _mm_tn_cols("mm_g_att_o", att, dy_att, D // N_CHIPS)
    (g_w_rec_o,) = _mm_tn_rows("mm_g_rec_o", g, dy_rec, tm=D)
    tok = reducer.advance("ff", g_w_rec_o) + reducer.begin("proj", dict(w_out=g_w_out, w_att_o=g_w_att_o, w_rec_o=g_w_rec_o))

    dq, dk, dv, ds_acc = _attn_bwd(z, probs, d_att, after=tok)
    g_rpb = _rpb_grad(ds_acc)
    tok = reducer.advance("proj", dq)
    d_up, d_yb, g_conv_w, g_conv_b, g_wa, g_wi, g_ba, g_bi, g_lam = _rec_bwd(z, d_g, rec_saved, *rec_params, after=tok)
    dz = jnp.concatenate([dq, dk, dv, d_up, d_yb, dg_att, dg_rec], axis=1)

    g_w_in, g_b_in = _mm_tn_cols("mm_g_in", h1, dz, D_IN // N_CHIPS, colsum=True)
    tok = reducer.advance("in", reducer.interlude(reducer.begin("in", dict(w_in=g_w_in))))
    grad_x, g_ln1 = _mm_nt_cols_rms_bwd("mm_dh1_rms1_bwd", dz, p["w_in"], x, p["ln1_g"], dx1, after=tok)

    grads = dict(ln1_g=g_ln1, w_in=g_w_in, b_in=g_b_in, rpb=g_rpb, w_att_o=g_w_att_o, conv_w=g_conv_w,
                 conv_b=g_conv_b, w_rg_a=g_wa, b_rg_a=g_ba, w_rg_i=g_wi,
                 b_rg_i=g_bi, lru_lambda=g_lam, w_rec_o=g_w_rec_o, w_out=g_w_out, ln2_g=g_ln2,
                 w_ff1=g_w_ff1, w_ff2=g_w_ff2, lnf_g=g_lnf)
    return loss, grad_x.reshape(1, T, D), grads


_ANY = pl.BlockSpec(memory_space=pl.ANY)
N_PEERS = N_CHIPS - 1


def _place():
    x, y, c = lax.axis_index("x"), lax.axis_index("y"), lax.axis_index("c")
    peers = [(1 - x, y), (x, 1 - y), (1 - x, 1 - y)]
    return x, y, c, 2 * x + y, peers


def _remote(src, dst, send_sem, recv_sem, dev):
    return pltpu.make_async_remote_copy(src_ref=src, dst_ref=dst, send_sem=send_sem, recv_sem=recv_sem,
                                        device_id=dev, device_id_type=MESH)


def _prefetch_call(body, name, ids, grid, in_specs, out_specs, out_shape, args, semantics=None):
    spec = pltpu.PrefetchScalarGridSpec(num_scalar_prefetch=1, grid=grid, in_specs=in_specs, out_specs=out_specs)
    return pl.pallas_call(body, name=name, grid_spec=spec, out_shape=out_shape,
                          compiler_params=_params(semantics or ("parallel",) * len(grid)))(ids, *args)


def _cast_bf16(name, w, chip_id, after=()):
    rows, cols = w.shape
    rb = min(rows, 256)

    def body(ids_ref, w_ref, *rest):
        rest[-1][...] = w_ref[...].astype(BF16)

    return _prefetch_call(body, name, chip_id, (rows // rb,),
                          [pl.BlockSpec((rb, cols), lambda i, ids: (i, 0))] + [_ANY] * len(after),
                          pl.BlockSpec((None, rb, cols), lambda i, ids: (ids[0], i, 0)),
                          _sds((N_CHIPS, rows, cols), BF16), (w, *after))


def _dma_sems(*counts):
    return [pltpu.SemaphoreType.DMA((k,)) for k in counts]


_HBM = pl.BlockSpec(memory_space=pltpu.HBM)
_SEM = pl.BlockSpec(memory_space=pltpu.SEMAPHORE)
_SPLIT_COPY = pltpu.CompilerParams(has_side_effects=pltpu.SideEffectType.DATAFLOW_SIDE_EFFECTING)
SIBLING_ID = 0
_SPLIT_COPY_SIBLING = pltpu.CompilerParams(has_side_effects=pltpu.SideEffectType.DATAFLOW_SIDE_EFFECTING,
                                           collective_id=SIBLING_ID)


def _sibling_handshake():
    x, y, c = lax.axis_index("x"), lax.axis_index("y"), lax.axis_index("c")
    barrier = pltpu.get_barrier_semaphore()
    pl.semaphore_signal(barrier, inc=1, device_id=(x, y, 1 - c), device_id_type=MESH)
    pl.semaphore_wait(barrier, 1)


def _hbm(arrays):
    return [pltpu.with_memory_space_constraint(a, pltpu.HBM) for a in arrays]


def _hbm_like(arrays):
    return [pltpu.HBM(a.shape, a.dtype) for a in arrays]


def _halves(buf, c):
    half = buf.shape[1] // 2
    return pl.ds(c * half, half), pl.ds((1 - c) * half, half)


def _gather_start(name, slots):
    n = len(slots)
    nk = n * N_PEERS

    def body(*refs):
        bufs = refs[n:2 * n]
        send_sems, recv_sems, token = refs[2 * n:]
        x, y, c, chip, peers = _place()
        for t in range(n):
            mine, _ = _halves(bufs[t], c)
            for r, (px, py) in enumerate(peers):
                k = t * N_PEERS + r
                own = bufs[t].at[chip, mine]
                _remote(own, own, send_sems.at[k], recv_sems.at[k], (px, py, c)).start()
        token[...] = jnp.zeros_like(token)

    res = pl.pallas_call(
        body, name=name, in_specs=[_HBM] * n, out_specs=[_HBM] * n + [_SEM, _SEM, pl.BlockSpec(memory_space=pltpu.VMEM)],
        out_shape=_hbm_like(slots) + [pltpu.SemaphoreType.DMA((nk,)), pltpu.SemaphoreType.DMA((nk,)),
                                      _sds((8, 128), F32)],
        input_output_aliases={t: t for t in range(n)}, compiler_params=_SPLIT_COPY)(*_hbm(slots))
    return res[:n], (res[n], res[n + 1]), res[n + 2]


def _gather_wait(name, bufs, sems, after):
    n = len(bufs)
    after = tuple(after) if isinstance(after, (tuple, list)) else (after,)

    def body(*refs):
        ins = refs[:n]
        send_sems, recv_sems = refs[n], refs[n + 1]
        x, y, c, chip, peers = _place()
        for t in range(n):
            mine, _ = _halves(ins[t], c)
            for r, (px, py) in enumerate(peers):
                k = t * N_PEERS + r
                cp = _remote(ins[t].at[chip, mine], ins[t].at[2 * px + py, mine], send_sems.at[k], recv_sems.at[k],
                             (px, py, c))
                cp.wait_send()
                cp.wait_recv()

    return pl.pallas_call(
        body, name=name, in_specs=[_HBM] * n + [_SEM, _SEM] + [_ANY] * len(after), out_specs=[_HBM] * n,
        out_shape=_hbm_like(bufs), input_output_aliases={t: t for t in range(n)},
        compiler_params=_SPLIT_COPY)(*bufs, *sems, *after)


def _gather_forward(name, bufs):
    n = len(bufs)
    nk = n * N_PEERS

    def body(*refs):
        _sibling_handshake()
        outs = refs[n:2 * n]
        send_sems, recv_sems = refs[2 * n:]
        x, y, c, chip, peers = _place()
        sibling = (x, y, 1 - c)
        sends = []
        for t in range(n):
            mine, _ = _halves(outs[t], c)
            for r, (px, py) in enumerate(peers):
                k = t * N_PEERS + r
                landed = outs[t].at[2 * px + py, mine]
                sends.append(_remote(landed, landed, send_sems.at[k], recv_sems.at[k], sibling))
                sends[-1].start()
        for t in range(n):
            _, theirs = _halves(outs[t], c)
            for r, (px, py) in enumerate(peers):
                k = t * N_PEERS + r
                landed = outs[t].at[2 * px + py, theirs]
                _remote(landed, landed, send_sems.at[k], recv_sems.at[k], sibling).wait_recv()
        for cp in sends:
            cp.wait_send()

    return pl.pallas_call(
        body, name=name, in_specs=[_ANY] * n, out_specs=[_ANY] * n, out_shape=[_sds(b.shape, b.dtype) for b in bufs],
        input_output_aliases={t: t for t in range(n)}, scratch_shapes=_dma_sems(nk, nk),
        compiler_params=pltpu.CompilerParams(collective_id=SIBLING_ID))(*bufs)


def _pair_copies(n, srcs, lands, send_sems, recv_sems):
    x, y, c, _, _ = _place()
    sibling = (x, y, 1 - c)
    copies = []
    for t in range(n):
        half = srcs[t].shape[1] // 2
        for j in range(N_CHIPS):
            k = t * N_CHIPS + j
            copies.append(_remote(srcs[t].at[j, pl.ds((1 - c) * half, half)], lands[t].at[j],
                                  send_sems.at[k], recv_sems.at[k], sibling))
    for t in range(n, len(srcs)):
        k = n * N_CHIPS + t - n
        copies.append(_remote(srcs[t], lands[t], send_sems.at[k], recv_sems.at[k], sibling))
    return copies


def _pair_start(name, grads, wholes=()):
    n = len(grads)
    srcs = list(grads) + list(wholes)
    m = len(srcs)
    lands = [pltpu.HBM((N_CHIPS, g.shape[1] // 2, g.shape[2]), F32) for g in grads] + _hbm_like(wholes)
    ns = n * N_CHIPS + len(wholes)

    def body(*refs):
        _sibling_handshake()
        src_refs, land_refs = refs[m:2 * m], refs[2 * m:3 * m]
        send_sems, recv_sems, token = refs[3 * m:]
        for cp in _pair_copies(n, src_refs, land_refs, send_sems, recv_sems):
            cp.start()
        token[...] = jnp.zeros_like(token)

    res = pl.pallas_call(
        body, name=name, in_specs=[_HBM] * m,
        out_specs=[_HBM] * (2 * m) + [_SEM, _SEM, pl.BlockSpec(memory_space=pltpu.VMEM)],
        out_shape=_hbm_like(srcs) + lands + [pltpu.SemaphoreType.DMA((ns,)), pltpu.SemaphoreType.DMA((ns,)),
                                             _sds((8, 128), F32)],
        input_output_aliases={t: t for t in range(m)}, compiler_params=_SPLIT_COPY_SIBLING)(*_hbm(srcs))
    return (res[:m], res[m:2 * m], (res[2 * m], res[2 * m + 1])), res[2 * m + 2]


def _pair_wait(name, flight, n, after):
    srcs, lands, sems = flight
    m = len(srcs)

    def body(*refs):
        for cp in _pair_copies(n, refs[:m], refs[m:2 * m], refs[2 * m], refs[2 * m + 1]):
            cp.wait_send()
            cp.wait_recv()

    res = pl.pallas_call(
        body, name=name, in_specs=[_HBM] * (2 * m) + [_SEM, _SEM, _ANY], out_specs=[_HBM] * (2 * m),
        out_shape=_hbm_like(srcs) + _hbm_like(lands), input_output_aliases={t: t for t in range(2 * m)},
        compiler_params=_SPLIT_COPY)(*srcs, *lands, *sems, after)
    return res[:m], res[m:]


def _chip_copies(srcs, lands, small_src, small_land, send_sems, recv_sems):
    x, y, c, chip, peers = _place()
    n = len(srcs)
    copies = []
    for r, (px, py) in enumerate(peers):
        for t in range(n):
            k = t * N_PEERS + r
            copies.append(_remote(srcs[t].at[2 * px + py], lands[t].at[r], send_sems.at[k], recv_sems.at[k], (px, py, c)))
        if small_src is not None:
            k = n * N_PEERS + r
            half_s = small_src.shape[0] // 2
            copies.append(_remote(small_src.at[pl.ds(c * half_s, half_s)], small_land.at[r],
                                  send_sems.at[k], recv_sems.at[k], (px, py, c)))
    return copies


def _chip_start(name, sums_bf16, small=None):
    n = len(sums_bf16)
    srcs = list(sums_bf16) + ([small] if small is not None else [])
    m = len(srcs)
    lands = [pltpu.HBM((N_PEERS,) + s.shape[1:], BF16) for s in sums_bf16]
    if small is not None:
        lands.append(pltpu.HBM((N_PEERS, small.shape[0] // 2, 128), F32))
    nk = m * N_PEERS

    def body(*refs):
        src_refs, land_refs = refs[m:2 * m], refs[2 * m:3 * m]
        send_sems, recv_sems, token = refs[3 * m:]
        small_src, small_land = (src_refs[n], land_refs[n]) if small is not None else (None, None)
        for cp in _chip_copies(src_refs[:n], land_refs[:n], small_src, small_land, send_sems, recv_sems):
            cp.start()
        token[...] = jnp.zeros_like(token)

    res = pl.pallas_call(
        body, name=name, in_specs=[_HBM] * m,
        out_specs=[_HBM] * (2 * m) + [_SEM, _SEM, pl.BlockSpec(memory_space=pltpu.VMEM)],
        out_shape=_hbm_like(srcs) + lands + [pltpu.SemaphoreType.DMA((nk,)), pltpu.SemaphoreType.DMA((nk,)),
                                             _sds((8, 128), F32)],
        input_output_aliases={t: t for t in range(m)}, compiler_params=_SPLIT_COPY)(*_hbm(srcs))
    return (res[:m], res[m:2 * m], (res[2 * m], res[2 * m + 1])), res[2 * m + 2]


def _chip_wait(name, flight, with_small, after):
    srcs, lands, sems = flight
    m = len(srcs)
    n = m - 1 if with_small else m

    def body(*refs):
        src_refs, land_refs = refs[:m], refs[m:2 * m]
        send_sems, recv_sems = refs[2 * m], refs[2 * m + 1]
        small_src, small_land = (src_refs[n], land_refs[n]) if with_small else (None, None)
        for cp in _chip_copies(src_refs[:n], land_refs[:n], small_src, small_land, send_sems, recv_sems):
            cp.wait_send()
            cp.wait_recv()

    res = pl.pallas_call(
        body, name=name, in_specs=[_HBM] * (2 * m) + [_SEM, _SEM, _ANY], out_specs=[_HBM] * (2 * m),
        out_shape=_hbm_like(srcs) + _hbm_like(lands), input_output_aliases={t: t for t in range(2 * m)},
        compiler_params=_SPLIT_COPY)(*srcs, *lands, *sems, after)
    return res[:m], res[m:]


def _swap_start(name, bufs):
    n = len(bufs)

    def body(*refs):
        _sibling_handshake()
        outs = refs[n:2 * n]
        send_sems, recv_sems, token = refs[2 * n:]
        x, y, c, _, _ = _place()
        for t in range(n):
            h = outs[t].shape[0] // 2
            mine = outs[t].at[pl.ds(c * h, h)]
            _remote(mine, mine, send_sems.at[t], recv_sems.at[t], (x, y, 1 - c)).start()
        token[...] = jnp.zeros_like(token)

    res = pl.pallas_call(
        body, name=name, in_specs=[_HBM] * n, out_specs=[_HBM] * n + [_SEM, _SEM, pl.BlockSpec(memory_space=pltpu.VMEM)],
        out_shape=_hbm_like(bufs) + [pltpu.SemaphoreType.DMA((n,)), pltpu.SemaphoreType.DMA((n,)), _sds((8, 128), F32)],
        input_output_aliases={t: t for t in range(n)}, compiler_params=_SPLIT_COPY_SIBLING)(*_hbm(bufs))
    return (res[:n], (res[n], res[n + 1])), res[n + 2]


def _swap_wait(name, flight, after):
    bufs, sems = flight
    n = len(bufs)

    def body(*refs):
        ins = refs[:n]
        send_sems, recv_sems = refs[n], refs[n + 1]
        x, y, c, _, _ = _place()
        for t in range(n):
            h = ins[t].shape[0] // 2
            cp = _remote(ins[t].at[pl.ds(c * h, h)], ins[t].at[pl.ds((1 - c) * h, h)], send_sems.at[t],
                         recv_sems.at[t], (x, y, 1 - c))
            cp.wait_send()
            cp.wait_recv()

    return pl.pallas_call(
        body, name=name, in_specs=[_HBM] * n + [_SEM, _SEM, _ANY], out_specs=[_HBM] * n, out_shape=_hbm_like(bufs),
        input_output_aliases={t: t for t in range(n)}, compiler_params=_SPLIT_COPY)(*bufs, *sems, after)


def _pair_sum(name, grad, got, ids):
    _, rows, cols = got.shape
    rb = min(rows, 256)
    nb = rows // rb
    blk = pl.BlockSpec((None, rb, cols), lambda i, j, ids: (j, i, 0))
    mine = pl.BlockSpec((None, rb, cols), lambda i, j, ids: (j, ids[1] * nb + i, 0))
    own = pl.BlockSpec((rb, cols), lambda i, j, ids: (i, 0))

    def body(ids_ref, a_ref, b_ref, s_ref, sb_ref):
        s = a_ref[...] + b_ref[...]
        sb_ref[...] = s.astype(BF16)

        @pl.when(pl.program_id(1) == ids_ref[0])
        def _():
            s_ref[...] = s

    return _prefetch_call(body, name, ids, (nb, N_CHIPS), [mine, blk], [own, blk],
                          [_sds((rows, cols), F32), _sds(got.shape, BF16)], (grad, got),
                          semantics=("parallel", "arbitrary"))


def _chip_sum(name, own_sum, got, ids):
    rows, cols = own_sum.shape
    rb = min(rows, 256)
    nb = rows // rb
    own = pl.BlockSpec((rb, cols), lambda i, ids: (i, 0))
    blk3 = pl.BlockSpec((N_PEERS, rb, cols), lambda i, ids: (0, i, 0))
    out = pl.BlockSpec((rb, cols), lambda i, ids: (ids[1] * nb + i, 0))

    def body(ids_ref, a_ref, b_ref, o_ref):
        o_ref[...] = ((a_ref[...] + b_ref[0].astype(F32)) + b_ref[1].astype(F32)) + b_ref[2].astype(F32)

    return _prefetch_call(body, name, ids, (nb,), [own, blk3], out, _sds((2 * rows, cols), F32), (own_sum, got))


SMALL_RB = 280


def _small_pair_sum(own, got):
    blk = pl.BlockSpec((SMALL_RB, 128), lambda i: (i, 0))

    def body(a_ref, b_ref, o_ref):
        o_ref[...] = a_ref[...] + b_ref[...]

    return pl.pallas_call(body, name="small_pair_sum", grid=(own.shape[0] // SMALL_RB,), in_specs=[blk, blk],
                          out_specs=blk, out_shape=_sds(own.shape, F32),
                          compiler_params=_params(("parallel",)))(own, got)


def _small_chip_sum(pair, got, ids):
    nb = pair.shape[0] // 2 // SMALL_RB
    half = pl.BlockSpec((SMALL_RB, 128), lambda i, ids: (ids[1] * nb + i, 0))
    blk3 = pl.BlockSpec((N_PEERS, SMALL_RB, 128), lambda i, ids: (0, i, 0))

    def body(ids_ref, a_ref, b_ref, o_ref):
        o_ref[...] = (a_ref[...] + b_ref[1]) + (b_ref[0] + b_ref[2])

    return _prefetch_call(body, "small_chip_sum", ids, (nb,), [half, blk3], half, _sds(pair.shape, F32), (pair, got))


def _adamw_math(w, g, m, v):
    m = ADAM_B1 * m + (1.0 - ADAM_B1) * g
    v = ADAM_B2 * v + (1.0 - ADAM_B2) * (g * g)
    m_hat = m / (1.0 - ADAM_B1 ** ADAM_STEP)
    v_hat = v / (1.0 - ADAM_B2 ** ADAM_STEP)
    delta = -ADAM_LR * (m_hat / (jnp.sqrt(v_hat) + ADAM_EPS) + ADAM_WD * w)
    return delta, m, v


def _adamw(name, w, g, m, v, rb=None):
    rows, cols = w.shape
    rb = rows if rb is None else rb
    blk = pl.BlockSpec((rb, cols), lambda i: (i, 0))

    def body(w_ref, g_ref, m_ref, v_ref, d_ref, nm_ref, nv_ref):
        d, nm, nv = _adamw_math(w_ref[...], g_ref[...], m_ref[...], v_ref[...])
        d_ref[...] = d
        nm_ref[...] = nm
        nv_ref[...] = nv

    return pl.pallas_call(body, name=name, grid=(rows // rb,), in_specs=[blk] * 4, out_specs=[blk] * 3,
                          out_shape=[_sds(w.shape, F32)] * 3, compiler_params=_params(("parallel",)))(w, g, m, v)


def _adamw_small(ws, gs, ms, vs):
    n = len(ws)

    def body(*refs):
        for t in range(n):
            w_ref, g_ref, m_ref, v_ref = (refs[k * n + t] for k in range(4))
            d, nm, nv = _adamw_math(w_ref[...], g_ref[...], m_ref[...], v_ref[...])
            for k, val in enumerate((d, nm, nv)):
                refs[(4 + k) * n + t][...] = val

    res = pl.pallas_call(body, name="adamw_small", out_shape=[_sds(a.shape, F32) for a in ws] * 3,
                         compiler_params=_params())(*ws, *gs, *ms, *vs)
    return [(res[t], res[n + t], res[2 * n + t]) for t in range(n)]


BIG = ("w_in", "w_att_o", "w_rec_o", "w_out", "w_ff1", "w_ff2")
SHARDED_VECS = ("conv_w", "b_rg_a", "b_rg_i", "lru_lambda")
SMALL = ("ln1_g", "b_in", "rpb", "conv_w", "conv_b", "w_rg_a", "b_rg_a", "w_rg_i", "b_rg_i", "lru_lambda",
         "ln2_g", "lnf_g")
SMALL_ROWS = 2240
ORDER = ("ln1_g", "w_in", "b_in", "rpb", "w_att_o", "conv_w", "conv_b", "w_rg_a", "b_rg_a", "w_rg_i", "b_rg_i",
         "lru_lambda", "w_rec_o", "w_out", "ln2_g", "w_ff1", "w_ff2", "lnf_g")


def _pack_small(grads, loss):
    parts, sizes = [], {}
    for n in SMALL:
        flat = grads[n].reshape(-1)
        pad = (-flat.shape[0]) % 128
        sizes[n] = (flat.shape[0], flat.shape[0] + pad)
        parts.append(jnp.pad(flat, (0, pad)))
    total = sum(s[1] for s in sizes.values())
    parts.append(jnp.pad(loss.reshape(1), (0, SMALL_ROWS * 128 - total - 1)))
    return jnp.concatenate(parts).reshape(SMALL_ROWS, 128), sizes


def _unpack_small(buf, sizes, shapes):
    flat = buf.reshape(-1)
    out, pos = {}, 0
    for n in SMALL:
        size, padded = sizes[n]
        out[n] = flat[pos:pos + size].reshape(shapes[n])
        pos += padded
    return out, flat[pos]


def _gather_weights(w, chip):
    chip_id = chip.astype(jnp.int32).reshape(1)
    vec_rows = [w[n][0] for n in SHARDED_VECS]
    vec_shard = jnp.concatenate(vec_rows + [jnp.zeros((16 - 10, D // N_CHIPS), F32)], axis=0)
    vec_slots = lax.dynamic_update_slice(jnp.zeros((N_CHIPS, 16, D // N_CHIPS), F32), vec_shard[None], (chip, 0, 0))
    bufs_a, sems_a, token_a = _gather_start("gather_start_first", [_cast_bf16("cast_w_in", w["w_in"][0], chip_id), vec_slots])
    rest_names = BIG[1:]
    bufs_b, sems_b, token_b = _gather_start(
        "gather_start_rest", [_cast_bf16("cast_" + n, w[n][0], chip_id, after=(token_a,)) for n in rest_names])

    def first(after):
        w_in_full, vec_full = _gather_forward("gather_forward_first", _gather_wait("gather_wait_first", bufs_a, sems_a, after))
        vecs = vec_full.transpose(1, 0, 2).reshape(16, D)
        return dict(w_in=w_in_full, conv_w=vecs[0:4], b_rg_a=vecs[4:6], b_rg_i=vecs[6:8], lru_lambda=vecs[8:10])

    def rest(after):
        full = dict(zip(rest_names, _gather_forward("gather_forward_rest",
                                                    _gather_wait("gather_wait_rest", bufs_b, sems_b, after))))
        return dict(w_att_o=full["w_att_o"], w_ff1=full["w_ff1"], w_rec_o=full["w_rec_o"].reshape(D, D),
                    w_out=full["w_out"].reshape(D, D), w_ff2=full["w_ff2"].reshape(D_FF, D))

    p = dict(ln1_g=w["ln1_g"], b_in=w["b_in"], rpb=w["rpb"][0], conv_b=w["conv_b"], w_rg_a=w["w_rg_a"][0],
             w_rg_i=w["w_rg_i"][0], ln2_g=w["ln2_g"], lnf_g=w["lnf_g"].reshape(1, D))
    return p, ((token_b,), first, rest)


class _Reducer:
    def __init__(self, ids):
        self.ids = ids
        self.groups = {}

    def begin(self, tag, grads, small=None):
        names = list(grads)
        big = [grads[n].reshape(N_CHIPS, -1, grads[n].shape[-1]) for n in names]
        flight, token = _pair_start("pair_start_" + tag, big, [] if small is None else [small])
        self.groups[tag] = dict(names=names, pair=flight, small=small is not None)
        return (token,)

    def advance(self, tag, after):
        grp = self.groups[tag]
        n = len(grp["names"])
        mine, got = _pair_wait("pair_wait_" + tag, grp["pair"], n, after)
        sums = [_pair_sum("pair_sum_" + name, a, b, self.ids) for name, a, b in zip(grp["names"], mine, got)]
        small_sum = _small_pair_sum(mine[n], got[n]) if grp["small"] else None
        grp["chip"], token = _chip_start("chip_start_" + tag, [s[1] for s in sums], small_sum)
        grp["sums"] = [s[0] for s in sums]
        self.last_token = token
        return (token,)

    def finish(self, tag, after):
        grp = self.groups[tag]
        srcs, lands = _chip_wait("chip_wait_" + tag, grp["chip"], grp["small"], after)
        halves = [_chip_sum("chip_sum_" + name, s, b, self.ids) for name, s, b in zip(grp["names"], grp["sums"], lands)]
        if grp["small"]:
            halves.append(_small_chip_sum(srcs[-1], lands[-1], self.ids))
        grp["swap"], token = _swap_start("swap_start_" + tag, halves)
        return token

    def interlude(self, tokens):
        after = tokens[0]
        for tag in list(self.groups)[:-1]:
            after = self.finish(tag, after)
        return after

    def result(self, tag, after):
        return _swap_wait("swap_wait_" + tag, self.groups[tag]["swap"], after)


def kernel(x, ln1_g, w_in, b_in, rpb, w_att_o, conv_w, conv_b, w_rg_a, b_rg_a, w_rg_i, b_rg_i, lru_lambda, w_rec_o, w_out, ln2_g, w_ff1, w_ff2, lnf_g, loss_target, m_ln1_g, m_w_in, m_b_in, m_rpb, m_w_att_o, m_conv_w, m_conv_b, m_w_rg_a, m_b_rg_a, m_w_rg_i, m_b_rg_i, m_lru_lambda, m_w_rec_o, m_w_out, m_ln2_g, m_w_ff1, m_w_ff2, m_lnf_g, v_ln1_g, v_w_in, v_b_in, v_rpb, v_w_att_o, v_conv_w, v_conv_b, v_w_rg_a, v_b_rg_a, v_w_rg_i, v_b_rg_i, v_lru_lambda, v_w_rec_o, v_w_out, v_ln2_g, v_w_ff1, v_w_ff2, v_lnf_g):
    w = dict(ln1_g=ln1_g, w_in=w_in, b_in=b_in, rpb=rpb, w_att_o=w_att_o, conv_w=conv_w, conv_b=conv_b,
             w_rg_a=w_rg_a, b_rg_a=b_rg_a, w_rg_i=w_rg_i, b_rg_i=b_rg_i, lru_lambda=lru_lambda, w_rec_o=w_rec_o,
             w_out=w_out, ln2_g=ln2_g, w_ff1=w_ff1, w_ff2=w_ff2, lnf_g=lnf_g)
    m = dict(ln1_g=m_ln1_g, w_in=m_w_in, b_in=m_b_in, rpb=m_rpb, w_att_o=m_w_att_o, conv_w=m_conv_w,
             conv_b=m_conv_b, w_rg_a=m_w_rg_a, b_rg_a=m_b_rg_a, w_rg_i=m_w_rg_i, b_rg_i=m_b_rg_i,
             lru_lambda=m_lru_lambda, w_rec_o=m_w_rec_o, w_out=m_w_out, ln2_g=m_ln2_g, w_ff1=m_w_ff1,
             w_ff2=m_w_ff2, lnf_g=m_lnf_g)
    v = dict(ln1_g=v_ln1_g, w_in=v_w_in, b_in=v_b_in, rpb=v_rpb, w_att_o=v_w_att_o, conv_w=v_conv_w,
             conv_b=v_conv_b, w_rg_a=v_w_rg_a, b_rg_a=v_b_rg_a, w_rg_i=v_w_rg_i, b_rg_i=v_b_rg_i,
             lru_lambda=v_lru_lambda, w_rec_o=v_w_rec_o, w_out=v_w_out, ln2_g=v_ln2_g, w_ff1=v_w_ff1,
             w_ff2=v_w_ff2, lnf_g=v_lnf_g)
    chip = 2 * lax.axis_index("x") + lax.axis_index("y")
    ids = jnp.stack([chip, lax.axis_index("c")]).astype(jnp.int32)

    out_grad, out_delta, out_m, out_v = {}, {}, {}, {}

    def update(n, gn):
        shape, two_d = w[n].shape, gn.shape
        d, nm, nv = _adamw("adamw_" + n, w[n].reshape(two_d), gn, m[n].reshape(two_d), v[n].reshape(two_d), 256)
        out_grad[n], out_delta[n], out_m[n], out_v[n] = (gn.reshape(shape), d.reshape(shape), nm.reshape(shape),
                                                         nv.reshape(shape))
        return d

    reducer = _Reducer(ids)
    p, late = _gather_weights(w, chip)
    loss, grad_x, g = _local_step(x, loss_target, p, late, reducer)
    small, sizes = _pack_small(g, loss + reducer.last_token[:1, :1])
    after = reducer.begin("small", {}, small)[0]
    for tag in ("ff", "proj", "in"):
        for n, red in zip(reducer.groups[tag]["names"], reducer.result(tag, after)):
            after = update(n, red)
        if tag == "ff":
            after = reducer.finish("in", reducer.advance("small", after)[0])
    (small_red,) = reducer.result("small", reducer.finish("small", after))
    gsmall, loss = _unpack_small(small_red, sizes, {n: g[n].shape for n in SMALL})
    two_d = {n: (int(np.prod(w[n].shape[:-1])), w[n].shape[-1]) for n in SMALL}
    for n in SHARDED_VECS:
        gsmall[n] = lax.dynamic_slice_in_dim(gsmall[n], chip * (D // N_CHIPS), D // N_CHIPS, axis=1)
    gs = [gsmall[n].reshape(two_d[n]) for n in SMALL]
    updates = _adamw_small([w[n].reshape(two_d[n]) for n in SMALL], gs, [m[n].reshape(two_d[n]) for n in SMALL],
                           [v[n].reshape(two_d[n]) for n in SMALL])
    for n, gn, (d, nm, nv) in zip(SMALL, gs, updates):
        shape = w[n].shape
        out_grad[n], out_delta[n], out_m[n], out_v[n] = (gn.reshape(shape), d.reshape(shape), nm.reshape(shape),
                                                         nv.reshape(shape))
    return (loss, grad_x, *[out_grad[n] for n in ORDER], *[out_delta[n] for n in ORDER],
            *[out_m[n] for n in ORDER], *[out_v[n] for n in ORDER])
```

```python
import numpy as np
import jax
import jax.numpy as jnp
from jax import lax
from jax.experimental import pallas as pl
from jax.experimental.pallas import tpu as pltpu

F32 = jnp.float32
BF16 = jnp.bfloat16

T = 2048
D = 1024
D_ATT = 512
D_IN = 5632
D_FF = 4096
N_HEADS = 8
HEAD_DIM = 64
GRID_W = 64
N_ROWS = T // GRID_W
WIN_H = 8
WIN_W = 16
KEYS = WIN_H * GRID_W
N_CHIPS = 4
EPS = 1e-6
LRU_C = 8.0
SCALE = HEAD_DIM ** -0.5
REC_CB = 256
REC_BLOCK = 64
REC_CHUNK = 256
PAD = 8

ADAM_LR = 0.001
ADAM_B1 = 0.9
ADAM_B2 = 0.999
ADAM_EPS = 1e-08
ADAM_WD = 0.01
ADAM_STEP = 10

VMEM_LIMIT = 56 * 1024 * 1024

NN = (((1,), (0,)), ((), ()))
NT = (((1,), (1,)), ((), ()))
TN = (((0,), (0,)), ((), ()))
MESH = pl.DeviceIdType.MESH


def _params(sem=None):
    return pltpu.CompilerParams(dimension_semantics=sem, vmem_limit_bytes=VMEM_LIMIT)


def _dot(a, b, dims):
    return lax.dot_general(a, b, dims, preferred_element_type=F32)


def _sigmoid(x):
    return 0.5 * jnp.tanh(0.5 * x) + 0.5


def _matmul(name, a, b, *, dims, grid, a_spec, b_spec, out_shapes, out_specs, acc_shape,
            extras=(), extra_specs=(), epilogue=None, colsum_spec=None, colsum_shape=None, after=(),
            semantics=("parallel", "parallel", "arbitrary"), epilogue_takes_first=False):
    nk = grid[2]
    n_extra = len(extras)
    n_out = len(out_shapes)
    with_colsum = colsum_spec is not None

    def body(a_ref, b_ref, *rest):
        ex = rest[:n_extra]
        rest = rest[:n_extra] + rest[n_extra + len(after):]
        outs = rest[n_extra:n_extra + n_out]
        pos = n_extra + n_out
        cs_out = rest[pos] if with_colsum else None
        pos += 1 if with_colsum else 0
        acc = rest[pos]
        cs_acc = rest[pos + 1] if with_colsum else None
        k = pl.program_id(2)
        first_tile = pl.program_id(0) == 0

        @pl.when(k == 0)
        def _():
            acc[...] = jnp.zeros_like(acc)
            if with_colsum:
                cs_acc[...] = jnp.zeros_like(cs_acc)

        bv = b_ref[...]
        acc[...] += _dot(a_ref[...].astype(BF16), bv.astype(BF16), dims)
        if with_colsum:
            cs_acc[...] += jnp.sum(bv.astype(F32), axis=0, keepdims=True)

        @pl.when(k == nk - 1)
        def _():
            r = acc[...]
            if epilogue is None:
                outs[0][...] = r.astype(outs[0].dtype)
            elif epilogue_takes_first:
                epilogue(r, ex, outs, first_tile)
            else:
                epilogue(r, ex, outs)
            if with_colsum:
                cs_out[...] = cs_acc[...]

    shapes = list(out_shapes)
    specs = list(out_specs)
    scratch = [pltpu.VMEM(acc_shape, F32)]
    if with_colsum:
        shapes.append(colsum_shape)
        specs.append(colsum_spec)
        scratch.append(pltpu.VMEM((1, acc_shape[1]), F32))
    res = pl.pallas_call(
        body, name=name, grid=grid,
        in_specs=[a_spec, b_spec, *extra_specs] + [_ANY] * len(after),
        out_specs=specs, out_shape=shapes, scratch_shapes=scratch,
        compiler_params=_params(semantics),
    )(a, b, *extras, *after)
    return res


def _sds(shape, dtype):
    return jax.ShapeDtypeStruct(shape, dtype)


TM = 1024
NI = T // TM
TJ = T
NJ = T // TJ


def _mm_nn_cols(name, a, wg, out_dtype, *, bias=None, extras=(), extra_specs=(), epilogue=None,
                out_shapes=None, out_specs=None):
    k_dim, n4 = wg.shape[1], wg.shape[2]
    ex, exs = list(extras), list(extra_specs)
    if bias is not None:
        ex = [bias] + ex
        exs = [pl.BlockSpec((1, n4), lambda j, i, k: (0, j))] + exs
        user_ep = epilogue

        def epilogue(r, e, outs):
            r = r + e[0][...]
            if user_ep is None:
                outs[0][...] = r.astype(outs[0].dtype)
            else:
                user_ep(r, e[1:], outs)
    if out_shapes is None:
        out_shapes = [_sds((T, N_CHIPS * n4), out_dtype)]
        out_specs = [pl.BlockSpec((TJ, n4), lambda j, i, k: (i, j))]
    return _matmul(
        name, a, wg, dims=NN, grid=(N_CHIPS, NJ, 1),
        a_spec=pl.BlockSpec((TJ, k_dim), lambda j, i, k: (i, 0)),
        b_spec=pl.BlockSpec((None, k_dim, n4), lambda j, i, k: (j, 0, 0)),
        out_shapes=out_shapes, out_specs=out_specs, acc_shape=(TJ, n4),
        extras=ex, extra_specs=exs, epilogue=epilogue)


def _mm_nt_cols_rms_bwd(name, a, wg, x, g, dres, after=(), bf16_copy=False):
    n4 = wg.shape[2]
    row = pl.BlockSpec((TM, D), lambda i, j, k: (i, 0))
    vec = pl.BlockSpec((1, D), lambda i, j, k: (0, 0))

    def epilogue(dhv, ex, outs, first):
        x_ref, g_ref, dres_ref = ex
        dx_ref, dg_ref = outs[0], outs[-1]
        xv = x_ref[...]
        rstd = lax.rsqrt(jnp.mean(xv * xv, axis=-1, keepdims=True) + EPS)
        xhat = xv * rstd
        dy = dhv * g_ref[...]
        dx = dres_ref[...] + rstd * (dy - xhat * jnp.mean(dy * xhat, axis=-1, keepdims=True))
        dx_ref[...] = dx
        if bf16_copy:
            outs[1][...] = dx.astype(BF16)
        part = jnp.sum(dhv * xhat, axis=0, keepdims=True)

        @pl.when(first)
        def _():
            dg_ref[...] = part

        @pl.when(jnp.logical_not(first))
        def _():
            dg_ref[...] += part

    return _matmul(
        name, a, wg, dims=NT, grid=(NI, 1, N_CHIPS),
        a_spec=pl.BlockSpec((TM, n4), lambda i, j, k: (i, k)),
        b_spec=pl.BlockSpec((None, D, n4), lambda i, j, k: (k, 0, 0)),
        out_shapes=[_sds((T, D), F32)] + [_sds((T, D), BF16)] * bf16_copy + [_sds((1, D), F32)],
        out_specs=[row] + [row] * bf16_copy + [vec], acc_shape=(TM, D),
        extras=[x, g, dres], extra_specs=[row, vec, row], epilogue=epilogue, after=after,
        semantics=("arbitrary", "arbitrary", "arbitrary"), epilogue_takes_first=True)


def _mm_nt_rows(name, a, w, out_dtype, *, tn, extras=(), extra_specs=(), epilogue=None):
    k_dim, n = w.shape
    return _matmul(
        name, a, w, dims=NT, grid=(k_dim // tn, NJ, 1),
        a_spec=pl.BlockSpec((TJ, n), lambda j, i, k: (i, 0)),
        b_spec=pl.BlockSpec((tn, n), lambda j, i, k: (j, 0)),
        out_shapes=[_sds((T, k_dim), out_dtype)],
        out_specs=[pl.BlockSpec((TJ, tn), lambda j, i, k: (i, j))], acc_shape=(TJ, tn),
        extras=extras, extra_specs=extra_specs, epilogue=epilogue)


def _mm_tn_cols(name, a, g, n4, *, colsum=False):
    k_dim = a.shape[1]
    kw = {}
    if colsum:
        kw = dict(colsum_spec=pl.BlockSpec((1, n4), lambda j, i, k: (0, j)),
                  colsum_shape=_sds((1, N_CHIPS * n4), F32))
    return _matmul(
        name, a, g, dims=TN, grid=(N_CHIPS, 1, NJ),
        a_spec=pl.BlockSpec((TJ, k_dim), lambda j, i, k: (k, 0)),
        b_spec=pl.BlockSpec((TJ, n4), lambda j, i, k: (k, j)),
        out_shapes=[_sds((N_CHIPS, k_dim, n4), F32)],
        out_specs=[pl.BlockSpec((None, k_dim, n4), lambda j, i, k: (j, 0, 0))],
        acc_shape=(k_dim, n4), **kw)


def _mm_tn_rows(name, a, g, *, tm):
    k_dim, n = a.shape[1], g.shape[1]
    return _matmul(
        name, a, g, dims=TN, grid=(k_dim // tm, 1, NJ),
        a_spec=pl.BlockSpec((TJ, tm), lambda j, i, k: (k, j)),
        b_spec=pl.BlockSpec((TJ, n), lambda j, i, k: (k, 0)),
        out_shapes=[_sds((k_dim, n), F32)],
        out_specs=[pl.BlockSpec((tm, n), lambda j, i, k: (j, 0))], acc_shape=(tm, n))


TE = 256
NE = T // TE
_ROW = pl.BlockSpec((TE, D), lambda i: (i, 0))
_VEC = pl.BlockSpec((1, D), lambda i: (0, 0))


def _rms_fwd(name, x, g, after=()):
    def body(x_ref, g_ref, *rest):
        h_ref = rest[-1]
        xv = x_ref[...]
        rstd = lax.rsqrt(jnp.mean(xv * xv, axis=-1, keepdims=True) + EPS)
        h_ref[...] = (xv * rstd * g_ref[...]).astype(BF16)

    return pl.pallas_call(body, name=name, grid=(NE,), in_specs=[_ROW, _VEC] + [_ANY] * len(after), out_specs=_ROW,
                          out_shape=_sds((T, D), BF16), compiler_params=_params(("parallel",)))(x, g, *after)


def _mm_x2_loss_head(s, w_ff2, x1, target, g):
    k_dim = w_ff2.shape[0]
    row = pl.BlockSpec((TM, D), lambda i, j, k: (i, 0))
    vec = pl.BlockSpec((1, D), lambda i, j, k: (0, 0))

    def epilogue(r, ex, outs, first):
        x1_ref, t_ref, g_ref = ex
        loss_ref, dx_ref, dxb_ref, dg_ref = outs
        xv = x1_ref[...] + r
        rstd = lax.rsqrt(jnp.mean(xv * xv, axis=-1, keepdims=True) + EPS)
        xhat = xv * rstd
        gv = g_ref[...]
        err = xhat * gv - t_ref[...]
        dy = err * (1.0 / D)
        dxh = dy * gv
        dx = rstd * (dxh - xhat * jnp.mean(dxh * xhat, axis=-1, keepdims=True))
        dx_ref[...] = dx
        dxb_ref[...] = dx.astype(BF16)
        dg_part = jnp.sum(dy * xhat, axis=0, keepdims=True)
        loss_part = (0.5 / D) * jnp.sum(jnp.sum(err * err, axis=1, keepdims=True), axis=0, keepdims=True)

        @pl.when(first)
        def _():
            dg_ref[...] = dg_part
            loss_ref[...] = loss_part

        @pl.when(jnp.logical_not(first))
        def _():
            dg_ref[...] += dg_part
            loss_ref[...] += loss_part

    return _matmul(
        "mm_x2_loss_head", s, w_ff2, dims=NN, grid=(NI, 1, k_dim // D),
        a_spec=pl.BlockSpec((TM, D), lambda i, j, k: (i, k)), b_spec=pl.BlockSpec((D, D), lambda i, j, k: (k, 0)),
        out_shapes=[_sds((1, 1), F32), _sds((T, D), F32), _sds((T, D), BF16), _sds((1, D), F32)],
        out_specs=[pl.BlockSpec((1, 1), lambda i, j, k: (0, 0)), row, row, vec], acc_shape=(TM, D),
        extras=[x1, target, g], extra_specs=[row, row, vec], epilogue=epilogue,
        semantics=("arbitrary", "arbitrary", "arbitrary"), epilogue_takes_first=True)


DZ_Q, DZ_K, DZ_V, DZ_U, DZ_Y, DZ_G_ATT, DZ_G_REC = 0, 512, 1024, 1536, 2560, 3584, 4608
MW = 512
_G_ATT_BLK = 3584 // MW
_G_REC_BLK = 4608 // MW


TB = 512


def _branch_specs():
    def row(cols):
        return pl.BlockSpec((TB, cols), lambda i: (i, 0))

    ga = pl.BlockSpec((TB, MW), lambda i: (i, _G_ATT_BLK))
    ga2 = pl.BlockSpec((TB, MW), lambda i: (i, _G_ATT_BLK + 1))
    gr = pl.BlockSpec((TB, MW), lambda i: (i, _G_REC_BLK))
    gr2 = pl.BlockSpec((TB, MW), lambda i: (i, _G_REC_BLK + 1))
    w_att = pl.BlockSpec((N_CHIPS, D_ATT, D // N_CHIPS), lambda i: (0, 0, 0))
    w_sq = pl.BlockSpec((D, D), lambda i: (0, 0))
    return row, (ga, ga2, gr, gr2), w_att, w_sq


def _gate_values(gate_refs):
    ga, ga2, gr, gr2 = (r[...].astype(F32) for r in gate_refs)
    return _sigmoid(jnp.concatenate([ga, ga2], axis=1)), _sigmoid(jnp.concatenate([gr, gr2], axis=1))


def _branches_fwd(att, g, z, x, w_att_o, w_rec_o, w_out, ln2_g):
    row, gate_specs, w_att, w_sq = _branch_specs()

    def body(att_ref, g_ref, ga_ref, ga2_ref, gr_ref, gr2_ref, x_ref, wa_ref, wr_ref, wo_ref, g2_ref,
             ya_ref, yr_ref, m_ref, x1_ref, h2_ref):
        attv = att_ref[...]
        ya = jnp.concatenate([_dot(attv, wa_ref[j], NN) for j in range(N_CHIPS)], axis=1)
        yr = _dot(g_ref[...], wr_ref[...], NN)
        sa, sr = _gate_values((ga_ref, ga2_ref, gr_ref, gr2_ref))
        mixed = (sa * ya + sr * yr).astype(BF16)
        ya_ref[...] = ya
        yr_ref[...] = yr
        m_ref[...] = mixed
        x1 = x_ref[...] + _dot(mixed, wo_ref[...], NN)
        x1_ref[...] = x1
        rstd = lax.rsqrt(jnp.mean(x1 * x1, axis=-1, keepdims=True) + EPS)
        h2_ref[...] = (x1 * rstd * g2_ref[...]).astype(BF16)

    return pl.pallas_call(
        body, name="branches_fwd", grid=(T // TB,),
        in_specs=[row(D_ATT), row(D), *gate_specs, row(D), w_att, w_sq, w_sq, pl.BlockSpec((1, D), lambda i: (0, 0))],
        out_specs=[row(D)] * 5,
        out_shape=[_sds((T, D), F32), _sds((T, D), F32), _sds((T, D), BF16), _sds((T, D), F32), _sds((T, D), BF16)],
        compiler_params=_params(("parallel",)))(att, g, z, z, z, z, x, w_att_o, w_rec_o, w_out, ln2_g)


def _branches_bwd(dx1_b, y_att, y_rec, z, w_att_o, w_rec_o, w_out):
    row, gate_specs, w_att, w_sq = _branch_specs()
    n4 = D // N_CHIPS

    def body(dx_ref, ya_ref, yr_ref, ga_ref, ga2_ref, gr_ref, gr2_ref, wa_ref, wr_ref, wo_ref,
             dya_ref, dyr_ref, datt_ref, dg_ref, dz_ref, dga_ref, dgr_ref, sems):
        rows = pl.ds(pl.multiple_of(pl.program_id(0) * TB, TB), TB)
        dm = _dot(dx_ref[...], wo_ref[...], NT)
        sa, sr = _gate_values((ga_ref, ga2_ref, gr_ref, gr2_ref))
        dya = (dm * sa).astype(BF16)
        dyr = (dm * sr).astype(BF16)
        dya_ref[...] = dya
        dyr_ref[...] = dyr
        dga_ref[...] = (dm * ya_ref[...] * sa * (1.0 - sa)).astype(BF16)
        dgr_ref[...] = (dm * yr_ref[...] * sr * (1.0 - sr)).astype(BF16)
        copies = [pltpu.make_async_copy(dga_ref, dz_ref.at[rows, pl.ds(DZ_G_ATT, D)], sems.at[0]),
                  pltpu.make_async_copy(dgr_ref, dz_ref.at[rows, pl.ds(DZ_G_REC, D)], sems.at[1])]
        for cp in copies:
            cp.start()
        datt = _dot(dya[:, 0:n4], wa_ref[0], NT)
        for j in range(1, N_CHIPS):
            datt = datt + _dot(dya[:, j * n4:(j + 1) * n4], wa_ref[j], NT)
        datt_ref[...] = datt.astype(BF16)
        dg_ref[...] = _dot(dyr, wr_ref[...], NT).astype(BF16)
        for cp in copies:
            cp.wait()

    return pl.pallas_call(
        body, name="branches_bwd", grid=(T // TB,),
        in_specs=[row(D), row(D), row(D), *gate_specs, w_att, w_sq, w_sq],
        out_specs=[row(D)] * 2 + [row(D_ATT), row(D), _ANY],
        out_shape=[_sds((T, D), BF16)] * 2 + [_sds((T, D_ATT), BF16), _sds((T, D), BF16), _sds((T, D_IN), BF16)],
        scratch_shapes=[pltpu.VMEM((TB, D), BF16), pltpu.VMEM((TB, D), BF16), pltpu.SemaphoreType.DMA((2,))],
        compiler_params=_params(("parallel",)))(dx1_b, y_att, y_rec, z, z, z, z, w_att_o, w_rec_o, w_out)


HP = 2 * HEAD_DIM
N_HP = N_HEADS // 2
ATT_UNROLL_FWD = 16
ATT_UNROLL_BWD = 8
DIAG_ROWS = 32


def _window_maps():
    diag = np.zeros((GRID_W * GRID_W, 128), np.float32)
    for qc in range(GRID_W):
        w0 = min(max(qc - WIN_W // 2, 0), GRID_W - WIN_W)
        for kc in range(w0, w0 + WIN_W):
            diag[qc * GRID_W + kc, kc - qc + WIN_W - 1] = 1.0
    return diag, diag.sum(axis=1)[None, :]


def _split3(x):
    a = x.astype(BF16)
    r = x - a.astype(F32)
    b = r.astype(BF16)
    c = (r - b.astype(F32)).astype(BF16)
    return a, b, c


N_DROW = 2 * WIN_H - 1
N_DPAIR = N_DROW - 1


def _bias_pairs(rpb):
    diag, valid = _window_maps()
    r2 = jnp.pad(rpb.reshape(N_HEADS * N_DROW, 2 * WIN_W - 1),
                 ((0, 128 - N_HEADS * N_DROW), (0, 128 - (2 * WIN_W - 1))))

    def body(r_ref, d_ref, v_ref, o_ref):
        dv = d_ref[...]
        t = sum(_dot(part, dv, NN) for part in _split3(r_ref[...]))
        o_ref[...] = jnp.where(v_ref[...] > 0.0, t, -1e30)

    t = pl.pallas_call(body, name="rpb_expand", out_shape=_sds((128, GRID_W * GRID_W), F32),
                       compiler_params=_params())(r2, jnp.asarray(diag.T, BF16), jnp.asarray(valid, F32))
    t = t[:N_HEADS * N_DROW].reshape(N_HEADS, N_DROW, GRID_W, GRID_W)
    return jnp.concatenate([t[:, :N_DPAIR], t[:, 1:]], axis=-1)


def _row_bias(tb_ref, hh, d0):
    return jnp.concatenate([tb_ref[hh, d0 + 2 * ii] for ii in range(WIN_H // 2)], axis=1)


def _row_window(r):
    rs = jnp.clip(r - WIN_H // 2, 0, N_ROWS - WIN_H)
    return pl.multiple_of(r * GRID_W, GRID_W), pl.multiple_of(rs * GRID_W, GRID_W), rs - r + (WIN_H - 1)


def _split_heads(src_ref, dst_ref, scale=None):
    for hh in range(2):
        v = src_ref[:, hh * HEAD_DIM:(hh + 1) * HEAD_DIM]
        dst_ref[hh] = (v if scale is None else v * scale).astype(BF16)


def _attn_items(qb_ref, kb_ref, vb_ref, tb_ref, first_row, n_rows):
    wins = [_row_window(first_row + u) for u in range(n_rows)]
    items = [(u, hh) for u in range(n_rows) for hh in range(2)]
    q = [qb_ref[hh, pl.ds(wins[u][0], GRID_W), :] for u, hh in items]
    k = [kb_ref[hh, pl.ds(wins[u][1], KEYS), :] for u, hh in items]
    v = [vb_ref[hh, pl.ds(wins[u][1], KEYS), :] for u, hh in items]
    s = [_dot(qi, ki, NT) + _row_bias(tb_ref, hh, wins[u][2]) for qi, ki, (u, hh) in zip(q, k, items)]
    m = [jnp.max(si, axis=-1, keepdims=True) for si in s]
    e = [jnp.exp(si - mi) for si, mi in zip(s, m)]
    inv = [1.0 / jnp.sum(ei, axis=-1, keepdims=True) for ei in e]
    p = [ei * li for ei, li in zip(e, inv)]
    return wins, items, q, k, v, p


def _attn_in_specs():
    q = pl.BlockSpec((T, HP), lambda p: (0, p))
    k = pl.BlockSpec((T, HP), lambda p: (0, N_HP + p))
    v = pl.BlockSpec((T, HP), lambda p: (0, 2 * N_HP + p))
    tb = pl.BlockSpec((2, N_DPAIR, GRID_W, HP), lambda p: (p, 0, 0, 0))
    return q, k, v, tb


_HEAD_SCRATCH = pltpu.VMEM((2, T, HEAD_DIM), BF16)


_PROBS = pl.BlockSpec((T, 2 * KEYS), lambda p: (0, p))


def _attn_fwd(z, tb):
    def body(q_ref, k_ref, v_ref, tb_ref, o_ref, p_ref, qb_ref, kb_ref, vb_ref):
        _split_heads(q_ref, qb_ref, SCALE)
        _split_heads(k_ref, kb_ref)
        _split_heads(v_ref, vb_ref)

        def rows(it, carry):
            wins, items, _, _, v, p = _attn_items(qb_ref, kb_ref, vb_ref, tb_ref, it * ATT_UNROLL_FWD, ATT_UNROLL_FWD)
            pb = [pi.astype(BF16) for pi in p]
            o = [_dot(pi, vi, NN) for pi, vi in zip(pb, v)]
            for u, (q0, _, _) in enumerate(wins):
                o_ref[pl.ds(q0, GRID_W), :] = jnp.concatenate(o[2 * u:2 * u + 2], axis=1).astype(BF16)
                p_ref[pl.ds(q0, GRID_W), :] = jnp.concatenate(pb[2 * u:2 * u + 2], axis=1)
            return carry

        lax.fori_loop(0, N_ROWS // ATT_UNROLL_FWD, rows, 0)

    blk = pl.BlockSpec((T, HP), lambda p: (0, p))
    return pl.pallas_call(
        body, name="attn_fwd", grid=(N_HP,), in_specs=list(_attn_in_specs()), out_specs=[blk, _PROBS],
        out_shape=[_sds((T, D_ATT), BF16), _sds((T, N_HEADS * KEYS), BF16)], scratch_shapes=[_HEAD_SCRATCH] * 3,
        compiler_params=_params(("parallel",)))(z, z, z, tb)


def _attn_bwd(z, probs, d_att, dz, after=()):
    def body(q_ref, k_ref, v_ref, p_ref, do_ref, flip_ref, dz_in_ref, *rest):
        (dz_ref, diag_ref, qb_ref, kb_ref, vb_ref, dob_ref, dka_ref, dva_ref, ds_ref,
         dq_ref, dk_ref, dv_ref, sems) = rest[len(after):]
        _split_heads(q_ref, qb_ref, SCALE)
        _split_heads(k_ref, kb_ref)
        _split_heads(v_ref, vb_ref)
        _split_heads(do_ref, dob_ref)
        dka_ref[...] = jnp.zeros_like(dka_ref)
        dva_ref[...] = jnp.zeros_like(dva_ref)
        ds_ref[...] = jnp.zeros_like(ds_ref)

        def rows(it, carry):
            wins = [_row_window(it * ATT_UNROLL_BWD + u) for u in range(ATT_UNROLL_BWD)]
            items = [(u, hh) for u in range(ATT_UNROLL_BWD) for hh in range(2)]
            q = [qb_ref[hh, pl.ds(wins[u][0], GRID_W), :] for u, hh in items]
            k = [kb_ref[hh, pl.ds(wins[u][1], KEYS), :] for u, hh in items]
            v = [vb_ref[hh, pl.ds(wins[u][1], KEYS), :] for u, hh in items]
            pb = [p_ref[pl.ds(wins[u][0], GRID_W), hh * KEYS:(hh + 1) * KEYS] for u, hh in items]
            p = [pi.astype(F32) for pi in pb]
            do = [dob_ref[hh, pl.ds(wins[u][0], GRID_W), :] for u, hh in items]
            dv = [_dot(pi, di, TN) for pi, di in zip(pb, do)]
            dp = [_dot(di, vi, NT) for di, vi in zip(do, v)]
            ds = [pi * (dpi - jnp.sum(dpi * pi, axis=-1, keepdims=True)) for pi, dpi in zip(p, dp)]
            dsb = [d.astype(BF16) for d in ds]
            dq = [_dot(d, ki, NN) * SCALE for d, ki in zip(dsb, k)]
            dk = [_dot(d, qi, TN) for d, qi in zip(dsb, q)]
            for d, (u, hh) in zip(ds, items):
                for ii in range(WIN_H // 2):
                    ds_ref[hh, wins[u][2] + 2 * ii] += d[:, ii * HP:(ii + 1) * HP]
            for u, (q0, _, _) in enumerate(wins):
                dq_ref[pl.ds(q0, GRID_W), :] = jnp.concatenate(dq[2 * u:2 * u + 2], axis=1).astype(BF16)
            for dki, dvi, (u, hh) in zip(dk, dv, items):
                dka_ref[hh, pl.ds(wins[u][1], KEYS), :] += dki
                dva_ref[hh, pl.ds(wins[u][1], KEYS), :] += dvi
            return carry

        lax.fori_loop(0, N_ROWS // ATT_UNROLL_BWD, rows, 0)
        dk_ref[...] = jnp.concatenate([dka_ref[0], dka_ref[1]], axis=1).astype(BF16)
        dv_ref[...] = jnp.concatenate([dva_ref[0], dva_ref[1]], axis=1).astype(BF16)
        cols = pl.multiple_of(pl.program_id(0) * HP, HP)
        copies = [pltpu.make_async_copy(src, dz_ref.at[:, pl.ds(base + cols, HP)], sems.at[t])
                  for t, (src, base) in enumerate(((dq_ref, DZ_Q), (dk_ref, DZ_K), (dv_ref, DZ_V)))]
        for cp in copies:
            cp.start()
        _diag_sums(ds_ref, flip_ref, diag_ref)
        for cp in copies:
            cp.wait()

    blk = pl.BlockSpec((T, HP), lambda p: (0, p))
    q, k, v, _ = _attn_in_specs()
    flip =jnp.asarray(np.eye(HP, dtype=np.float32)[::-1], BF16)
    return pl.pallas_call(
        body, name="attn_bwd", grid=(N_HP,),
        in_specs=[q, k, v, _PROBS, blk, pl.BlockSpec((HP, HP), lambda p: (0, 0)), _ANY] + [_ANY] * len(after),
        out_specs=[_ANY, pl.BlockSpec((None, DIAG_ROWS, HP), lambda p: (p, 0, 0))],
        out_shape=[_sds(dz.shape, dz.dtype), _sds((N_HP, DIAG_ROWS, HP), F32)], input_output_aliases={6: 0},
        scratch_shapes=[_HEAD_SCRATCH] * 4 + [pltpu.VMEM((2, T, HEAD_DIM), F32), pltpu.VMEM((2, T, HEAD_DIM), F32),
                                              pltpu.VMEM((2, N_DPAIR, GRID_W, HP), F32)]
        + [pltpu.VMEM((T, HP), BF16)] * 3 + [pltpu.SemaphoreType.DMA((3,))],
        compiler_params=_params(("parallel",)))(z, z, z, probs, d_att, flip, dz, *after)


def _diag_sums(acc_ref, flip_ref, out_ref):
    flip = flip_ref[...]
    rows = []
    for hh in range(2):
        for pair in range(N_DPAIR):
            reversed_lanes = sum(_dot(part, flip, NN) for part in _split3(acc_ref[hh, pair]))
            skewed = pltpu.roll(reversed_lanes, 0, 1, stride=1, stride_axis=0)
            rows.append(jnp.sum(skewed, axis=0, keepdims=True))
    rows.append(jnp.zeros((DIAG_ROWS - len(rows), HP), F32))
    out_ref[...] = jnp.concatenate(rows, axis=0)


def _rpb_grad(diag_sums):
    g = diag_sums.reshape(N_HP * DIAG_ROWS, HP)
    sel = np.zeros((2, 128, N_HP * DIAG_ROWS), np.float32)
    lane = np.zeros((2, HP, 128), np.float32)
    for h in range(N_HEADS):
        for pair in range(N_DPAIR):
            for half in range(2):
                sel[half, h * N_DROW + pair + half, (h // 2) * DIAG_ROWS + (h % 2) * N_DPAIR + pair] = 1.0
    for j in range(2 * WIN_W - 1):
        for half in range(2):
            lane[half, (HP - 1 - GRID_W * half - (j - (WIN_W - 1))) % HP, j] = 1.0

    def body(g_ref, sel_ref, lane_ref, o_ref):
        parts = _split3(g_ref[...])
        total = None
        for half in range(2):
            picked = sum(_dot(sel_ref[half], part, NN) for part in parts)
            term = sum(_dot(part, lane_ref[half], NN) for part in _split3(picked))
            total = term if total is None else total + term
        o_ref[...] = total

    out = pl.pallas_call(body, name="rpb_grad", out_shape=_sds((128, 128), F32),
                         compiler_params=_params())(g, jnp.asarray(sel, BF16), jnp.asarray(lane, BF16))
    return out[:N_HEADS * N_DROW, :2 * WIN_W - 1].reshape(N_HEADS, N_DROW, 2 * WIN_W - 1)


N_CB = D // REC_CB
N_CHUNK = T // REC_CHUNK
N_TILE = T // 8
_U_BLK = 1536 // REC_CB
_Y_BLK = 2560 // REC_CB


def _block_diag(w):
    per = REC_CB // 64
    wt = w.reshape(2, N_CB, per, 64, 64)
    eye = jnp.eye(per, dtype=w.dtype)
    full = wt[:, :, :, :, None, :] * eye[None, None, :, None, :, None]
    return full.reshape(2, N_CB, REC_CB, REC_CB).astype(BF16)


def _gelu(x):
    c = 0.7978845608028654
    return 0.5 * x * (1.0 + jnp.tanh(c * (x + 0.044715 * x * x * x)))


def _gelu_grad(x):
    c = 0.7978845608028654
    th = jnp.tanh(c * (x + 0.044715 * x * x * x))
    return 0.5 * (1.0 + th) + 0.5 * x * (1.0 - th * th) * c * (1.0 + 3.0 * 0.044715 * x * x)


def _softplus_neg(lam):
    x = -lam
    e = jnp.exp(-jnp.abs(x))
    w = 1.0 + e
    l1p = jnp.where(w == 1.0, e, jnp.log(w) * e / (w - 1.0))
    return jnp.maximum(x, 0.0) + l1p


def _one_minus_exp(x):
    poly = x * (1.0 + x * (1 / 2 + x * (1 / 6 + x * (1 / 24 + x * (1 / 120 + x * (1 / 720))))))
    return jnp.where(x > -0.125, -poly, 1.0 - jnp.exp(x))


def _conv_taps(pad_ref, t0, w, sign):
    out = None
    for j in range(4):
        term = w[j:j + 1, :] * pad_ref[pl.ds(PAD + t0 + sign * (j - 2), REC_CHUNK), :]
        out = term if out is None else out + term
    return out


def _gates(u, wa, wi, ba, bi, sp):
    ub = u.astype(BF16)
    r = _sigmoid(_dot(ub, wa, NN) + ba)
    i = _sigmoid(_dot(ub, wi, NN) + bi)
    log_a = -LRU_C * r * sp
    a = jnp.exp(log_a)
    x = jnp.maximum(_one_minus_exp(2.0 * log_a), 0.0)
    positive = x > 0.0
    inv = lax.rsqrt(jnp.where(positive, x, 1.0))
    mult = jnp.where(positive, x * inv, 0.0)
    return r, i, a, mult, jnp.where(positive, inv, 0.0)


def _tile_scan(a, b, sub, reverse):
    for s in (1, 2, 4):
        if reverse:
            a_s, b_s, m = pltpu.roll(a, 8 - s, 0), pltpu.roll(b, 8 - s, 0), sub < 8 - s
        else:
            a_s, b_s, m = pltpu.roll(a, s, 0), pltpu.roll(b, s, 0), sub >= s
        b = jnp.where(m, a * b_s + b, b)
        a = jnp.where(m, a * a_s, a)
    return a, b


def _last_row(x, row):
    return jnp.broadcast_to(x[row:row + 1, :], x.shape)


def _rec_prologue(up_ref, cw_ref, cb_ref, wa_ref, wi_ref, ba_ref, bi_ref, lam_ref,
                  upad_ref, u_ref, a_refs, h_refs):
    cb = up_ref.shape[1]
    zeros = jnp.zeros((PAD, cb), F32)
    upad_ref[pl.ds(0, PAD), :] = zeros
    upad_ref[pl.ds(PAD + T, PAD), :] = zeros
    upad_ref[pl.ds(PAD, T), :] = up_ref[...].astype(F32)
    cw = cw_ref[...]
    sp = _softplus_neg(lam_ref[...])
    for c in range(N_CHUNK):
        t0 = c * REC_CHUNK
        u = cb_ref[...] + _conv_taps(upad_ref, t0, cw, 1)
        u_ref[pl.ds(t0, REC_CHUNK), :] = u
        for d in range(2):
            _, i, a, mult, _ = _gates(u, wa_ref[d], wi_ref[d], ba_ref[d:d + 1, :], bi_ref[d:d + 1, :], sp[d:d + 1, :])
            a_refs[d][pl.ds(t0, REC_CHUNK), :] = a
            h_refs[d][pl.ds(t0, REC_CHUNK), :] = mult * (i * u)

    sub = lax.broadcasted_iota(jnp.int32, (8, cb), 0)

    def tile(k, carry):
        cf, cr = carry
        tf = pl.multiple_of(k * 8, 8)
        tr = pl.multiple_of((N_TILE - 1 - k) * 8, 8)
        af, bf = _tile_scan(a_refs[0][pl.ds(tf, 8), :], h_refs[0][pl.ds(tf, 8), :], sub, False)
        hf = af * cf + bf
        h_refs[0][pl.ds(tf, 8), :] = hf
        ar, br = _tile_scan(a_refs[1][pl.ds(tr, 8), :], h_refs[1][pl.ds(tr, 8), :], sub, True)
        hr = ar * cr + br
        h_refs[1][pl.ds(tr, 8), :] = hr
        return _last_row(af, 7) * cf + _last_row(bf, 7), _last_row(ar, 0) * cr + _last_row(br, 0)

    z8 = jnp.zeros((8, cb), F32)
    lax.fori_loop(0, N_TILE, tile, (z8, z8))
    return sp


def _rec_specs():
    up = pl.BlockSpec((T, REC_CB), lambda c: (0, _U_BLK + c))
    yb = pl.BlockSpec((T, REC_CB), lambda c: (0, _Y_BLK + c))
    cw = pl.BlockSpec((4, REC_CB), lambda c: (0, c))
    cbias = pl.BlockSpec((1, REC_CB), lambda c: (0, c))
    wbd = pl.BlockSpec((2, None, REC_CB, REC_CB), lambda c: (0, c, 0, 0))
    vec2 = pl.BlockSpec((2, REC_CB), lambda c: (0, c))
    col = pl.BlockSpec((T, REC_CB), lambda c: (0, c))
    return up, yb, cw, cbias, wbd, vec2, col


def _rec_fwd(z, conv_w, conv_b, wa, wi, ba, bi, lam):
    up, yb, cw, cbias, wbd, vec2, col = _rec_specs()

    def body(up_ref, yb_ref, cw_ref, cb_ref, wa_ref, wi_ref, ba_ref, bi_ref, lam_ref, g_ref,
             u_ref, af_ref, ar_ref, hf_ref, hr_ref, upad_ref):
        _rec_prologue(up_ref, cw_ref, cb_ref, wa_ref, wi_ref, ba_ref, bi_ref, lam_ref,
                      upad_ref, u_ref, (af_ref, ar_ref), (hf_ref, hr_ref))

        def chunk(c, carry):
            t0 = pl.multiple_of(c * REC_CHUNK, REC_CHUNK)
            rows = pl.ds(t0, REC_CHUNK)
            g_ref[rows, :] = ((hf_ref[rows, :] + hr_ref[rows, :]) * _gelu(yb_ref[rows, :].astype(F32))).astype(BF16)
            return carry

        lax.fori_loop(0, N_CHUNK, chunk, 0)

    res = pl.pallas_call(
        body, name="rec_fwd", grid=(N_CB,),
        in_specs=[up, yb, cw, cbias, wbd, wbd, vec2, vec2, vec2], out_specs=[col] * 6,
        out_shape=[_sds((T, D), BF16)] + [_sds((T, D), F32)] * 5,
        scratch_shapes=[pltpu.VMEM((T + 2 * PAD, REC_CB), F32)],
        compiler_params=_params(("parallel",)))(z, z, conv_w, conv_b, wa, wi, ba, bi, lam)
    return res[0], tuple(res[1:])


def _rec_bwd(z, dg, saved, dz, conv_w, conv_b, wa, wi, ba, bi, lam, after=()):
    up, yb, cw, cbias, wbd, vec2, col = _rec_specs()

    def body(up_ref, yb_ref, dg_ref, u_ref, af_ref, ar_ref, hf_ref, hr_ref,
             cw_ref, cb_ref, wa_ref, wi_ref, ba_ref, bi_ref, lam_ref, dz_in_ref, *rest):
        (dz_ref, dcw_ref, dcb_ref, dwa_out, dwi_out, dba_ref, dbi_ref, dlam_ref,
         upad_ref, dh_ref, gf_ref, gr_ref, daf_ref, dar_ref, dupad_ref, dwa_ref, dwi_ref,
         dup_ref, dyb_ref, sems) = rest[len(after):]
        g_refs, da_refs = (gf_ref, gr_ref), (daf_ref, dar_ref)
        cb = up_ref.shape[1]
        zeros = jnp.zeros((PAD, cb), F32)
        upad_ref[pl.ds(0, PAD), :] = zeros
        upad_ref[pl.ds(PAD + T, PAD), :] = zeros
        upad_ref[pl.ds(PAD, T), :] = up_ref[...].astype(F32)
        sp = _softplus_neg(lam_ref[...])

        def gate_chunk(c, carry):
            t0 = pl.multiple_of(c * REC_CHUNK, REC_CHUNK)
            rows = pl.ds(t0, REC_CHUNK)
            y = yb_ref[rows, :].astype(F32)
            dgv = dg_ref[rows, :].astype(F32)
            dh_ref[rows, :] = dgv * _gelu(y)
            dyb_ref[rows, :] = (dgv * (hf_ref[rows, :] + hr_ref[rows, :]) * _gelu_grad(y)).astype(BF16)
            return carry

        lax.fori_loop(0, N_CHUNK, gate_chunk, 0)

        sub = lax.broadcasted_iota(jnp.int32, (8, cb), 0)

        def tile(k, carry):
            cf, cr = carry
            kf = N_TILE - 1 - k
            tf = pl.multiple_of(kf * 8, 8)
            tnext = pl.multiple_of(jnp.minimum(kf + 1, N_TILE - 1) * 8, 8)
            tprev = pl.multiple_of(jnp.maximum(kf - 1, 0) * 8, 8)
            a_t = af_ref[pl.ds(tf, 8), :]
            a_n = jnp.where(kf < N_TILE - 1, af_ref[pl.ds(tnext, 8), :], 0.0)
            a_sh = jnp.where(sub == 7, pltpu.roll(a_n, 7, 0), pltpu.roll(a_t, 7, 0))
            ca, cbb = _tile_scan(a_sh, dh_ref[pl.ds(tf, 8), :], sub, True)
            gf = ca * cf + cbb
            h_t = hf_ref[pl.ds(tf, 8), :]
            h_p = jnp.where(kf > 0, hf_ref[pl.ds(tprev, 8), :], 0.0)
            h_sh = jnp.where(sub == 0, pltpu.roll(h_p, 1, 0), pltpu.roll(h_t, 1, 0))
            gf_ref[pl.ds(tf, 8), :] = gf
            daf_ref[pl.ds(tf, 8), :] = gf * h_sh
            tr = pl.multiple_of(k * 8, 8)
            rnext = pl.multiple_of(jnp.minimum(k + 1, N_TILE - 1) * 8, 8)
            rprev = pl.multiple_of(jnp.maximum(k - 1, 0) * 8, 8)
            b_t = ar_ref[pl.ds(tr, 8), :]
            b_p = jnp.where(k > 0, ar_ref[pl.ds(rprev, 8), :], 0.0)
            b_sh = jnp.where(sub == 0, pltpu.roll(b_p, 1, 0), pltpu.roll(b_t, 1, 0))
            ra, rb = _tile_scan(b_sh, dh_ref[pl.ds(tr, 8), :], sub, False)
            gr = ra * cr + rb
            hr_t = hr_ref[pl.ds(tr, 8), :]
            hr_n = jnp.where(k < N_TILE - 1, hr_ref[pl.ds(rnext, 8), :], 0.0)
            hr_sh = jnp.where(sub == 7, pltpu.roll(hr_n, 7, 0), pltpu.roll(hr_t, 7, 0))
            gr_ref[pl.ds(tr, 8), :] = gr
            dar_ref[pl.ds(tr, 8), :] = gr * hr_sh
            return _last_row(gf, 0), _last_row(gr, 7)

        z8 = jnp.zeros((8, cb), F32)
        lax.fori_loop(0, N_TILE, tile, (z8, z8))

        dupad_ref[pl.ds(0, PAD), :] = zeros
        dupad_ref[pl.ds(PAD + T, PAD), :] = zeros
        dwa_ref[...] = jnp.zeros_like(dwa_ref)
        dwi_ref[...] = jnp.zeros_like(dwi_ref)
        dba_ref[...] = jnp.zeros_like(dba_ref)
        dbi_ref[...] = jnp.zeros_like(dbi_ref)
        dlam_ref[...] = jnp.zeros_like(dlam_ref)

        def grad_chunk(c, carry):
            t0 = pl.multiple_of(c * REC_CHUNK, REC_CHUNK)
            rows = pl.ds(t0, REC_CHUNK)
            u = u_ref[rows, :]
            ub = u.astype(BF16)
            du = jnp.zeros((REC_CHUNK, cb), F32)
            for d in range(2):
                r, i, a, mult, inv_mult = _gates(u, wa_ref[d], wi_ref[d], ba_ref[d:d + 1, :], bi_ref[d:d + 1, :],
                                                 sp[d:d + 1, :])
                dbx = g_refs[d][rows, :]
                dmult = dbx * (i * u)
                diu = dbx * mult
                a2 = a * a
                dlog = da_refs[d][rows, :] * a - dmult * (a2 * inv_mult)
                dpa = (dlog * (-LRU_C) * sp[d:d + 1, :]) * r * (1.0 - r)
                dpi = (diu * u) * i * (1.0 - i)
                dpab, dpib = dpa.astype(BF16), dpi.astype(BF16)
                du = du + diu * i + _dot(dpab, wa_ref[d], NT) + _dot(dpib, wi_ref[d], NT)
                dwa_ref[d] += _dot(ub, dpab, TN)
                dwi_ref[d] += _dot(ub, dpib, TN)
                dba_ref[d:d + 1, :] += jnp.sum(dpa, axis=0, keepdims=True)
                dbi_ref[d:d + 1, :] += jnp.sum(dpi, axis=0, keepdims=True)
                dlam_ref[d:d + 1, :] += jnp.sum(dlog * r, axis=0, keepdims=True)
            dupad_ref[pl.ds(PAD + t0, REC_CHUNK), :] = du
            return carry

        lax.fori_loop(0, N_CHUNK, grad_chunk, 0)
        dlam_ref[...] = dlam_ref[...] * (LRU_C * _sigmoid(-lam_ref[...]))
        for d in range(2):
            for blk in range(REC_CB // REC_BLOCK):
                lo, hi = blk * REC_BLOCK, (blk + 1) * REC_BLOCK
                dwa_out[d, blk] = dwa_ref[d, lo:hi, lo:hi]
                dwi_out[d, blk] = dwi_ref[d, lo:hi, lo:hi]

        cw = cw_ref[...]
        dcb = jnp.zeros((1, cb), F32)
        dcw = [jnp.zeros((1, cb), F32) for _ in range(4)]
        for c in range(N_CHUNK):
            t0 = c * REC_CHUNK
            du = dupad_ref[pl.ds(PAD + t0, REC_CHUNK), :]
            dcb = dcb + jnp.sum(du, axis=0, keepdims=True)
            for j in range(4):
                dcw[j] = dcw[j] + jnp.sum(du * upad_ref[pl.ds(PAD + t0 + j - 2, REC_CHUNK), :], axis=0, keepdims=True)
            dup_ref[pl.ds(t0, REC_CHUNK), :] = _conv_taps(dupad_ref, t0, cw, -1).astype(BF16)
        dcb_ref[...] = dcb
        dcw_ref[...] = jnp.concatenate(dcw, axis=0)
        cols = pl.multiple_of(pl.program_id(0) * REC_CB, REC_CB)
        copies = [pltpu.make_async_copy(dup_ref, dz_ref.at[:, pl.ds(DZ_U + cols, REC_CB)], sems.at[0]),
                  pltpu.make_async_copy(dyb_ref, dz_ref.at[:, pl.ds(DZ_Y + cols, REC_CB)], sems.at[1])]
        for cp in copies:
            cp.start()
        for cp in copies:
            cp.wait()

    full = pltpu.VMEM((T, REC_CB), F32)
    padded = pltpu.VMEM((T + 2 * PAD, REC_CB), F32)
    per = REC_CB // REC_BLOCK
    diag = pl.BlockSpec((2, per, REC_BLOCK, REC_BLOCK), lambda c: (0, c, 0, 0))
    return pl.pallas_call(
        body, name="rec_bwd", grid=(N_CB,),
        in_specs=[up, yb] + [col] * 6 + [cw, cbias, wbd, wbd, vec2, vec2, vec2, _ANY] + [_ANY] * len(after),
        out_specs=[_ANY, cw, cbias, diag, diag, vec2, vec2, vec2],
        out_shape=[_sds(dz.shape, dz.dtype), _sds((4, D), F32), _sds((1, D), F32),
                   _sds((2, D // REC_BLOCK, REC_BLOCK, REC_BLOCK), F32), _sds((2, D // REC_BLOCK, REC_BLOCK, REC_BLOCK), F32),
                   _sds((2, D), F32), _sds((2, D), F32), _sds((2, D), F32)],
        input_output_aliases={15: 0},
        scratch_shapes=[padded, full, full, full, full, full, padded,
                        pltpu.VMEM((2, REC_CB, REC_CB), F32), pltpu.VMEM((2, REC_CB, REC_CB), F32),
                        pltpu.VMEM((T, REC_CB), BF16), pltpu.VMEM((T, REC_CB), BF16), pltpu.SemaphoreType.DMA((2,))],
        compiler_params=_params(("parallel",)))(z, z, dg, *saved, conv_w, conv_b, wa, wi, ba, bi, lam, dz, *after)


class _NoReducer:
    def begin(self, tag, grads):
        return ()

    def advance(self, tag, after):
        return ()

    def interlude(self, tokens):
        return None


def _local_step(x, target, p, late=None, reducer=_NoReducer()):
    x = x.reshape(T, D)
    target = target.reshape(T, D)
    tb = _bias_pairs(p["rpb"])
    wa, wi = _block_diag(p["w_rg_a"]), _block_diag(p["w_rg_i"])

    h1 = _rms_fwd("rms1_fwd", x, p["ln1_g"], after=late[0] if late else ())
    if late:
        p = {**p, **late[1]((h1, tb, wa, wi))}
    rec_params = (p["conv_w"], p["conv_b"], wa, wi, p["b_rg_a"], p["b_rg_i"], p["lru_lambda"])
    (z,) = _mm_nn_cols("mm_z", h1, p["w_in"], BF16, bias=p["b_in"])
    att, probs = _attn_fwd(z, tb)
    g, rec_saved = _rec_fwd(z, *rec_params)
    if late:
        p = {**p, **late[2](g)}
    y_att, y_rec, mixed, x1, h2 = _branches_fwd(att, g, z, x, p["w_att_o"], p["w_rec_o"], p["w_out"], p["ln2_g"])

    def relu2(r, ex, outs):
        rp = jnp.maximum(r, 0.0)
        outs[0][...] = (rp * rp).astype(BF16)

    (s,) = _mm_nn_cols("mm_ff1", h2, p["w_ff1"], BF16, epilogue=relu2)
    loss, dx2, dx2_b, g_lnf = _mm_x2_loss_head(s, p["w_ff2"], x1, target, p["lnf_g"])

    def relu2_bwd(r, ex, outs):
        outs[0][...] = (r * 2.0 * jnp.sqrt(ex[0][...].astype(F32))).astype(BF16)

    (df,) = _mm_nt_rows("mm_df", dx2_b, p["w_ff2"], BF16, tn=D, extras=[s],
                        extra_specs=[pl.BlockSpec((TJ, D), lambda j, i, k: (i, j))], epilogue=relu2_bwd)
    (g_w_ff2,) = _mm_tn_rows("mm_g_ff2", s, dx2_b, tm=D)
    (g_w_ff1,) = _mm_tn_cols("mm_g_ff1", h2, df, D)
    tok = reducer.begin("ff", dict(w_ff2=g_w_ff2, w_ff1=g_w_ff1))
    dx1, dx1_b, g_ln2 = _mm_nt_cols_rms_bwd("mm_dh2_rms2_bwd", df, p["w_ff1"], x1, p["ln2_g"], dx2, after=tok,
                                            bf16_copy=True)

    dy_att, dy_rec, d_att, d_g, dz = _branches_bwd(dx1_b, y_att, y_rec, z, p["w_att_o"], p["w_rec_o"], p["w_out"])
    (g_w_out,) = _mm_tn_rows("mm_g_out", mixed, dx1_b, tm=D)
    (g_w_att_o,) = _mm_tn_cols("mm_g_att_o", att, dy_att, D // N_CHIPS)
    (g_w_rec_o,) = _mm_tn_rows("mm_g_rec_o", g, dy_rec, tm=D)
    tok = reducer.advance("ff", g_w_rec_o) + reducer.begin("proj", dict(w_out=g_w_out, w_att_o=g_w_att_o, w_rec_o=g_w_rec_o))

    dz, ds_acc = _attn_bwd(z, probs, d_att, dz, after=tok)
    g_rpb = _rpb_grad(ds_acc)
    tok = reducer.advance("proj", ds_acc)
    dz, g_conv_w, g_conv_b, g_wa, g_wi, g_ba, g_bi, g_lam = _rec_bwd(z, d_g, rec_saved, dz, *rec_params, after=tok)

    g_w_in, g_b_in = _mm_tn_cols("mm_g_in", h1, dz, D_IN // N_CHIPS, colsum=True)
    tok = reducer.advance("in", reducer.interlude(reducer.begin("in", dict(w_in=g_w_in))))
    grad_x, g_ln1 = _mm_nt_cols_rms_bwd("mm_dh1_rms1_bwd", dz, p["w_in"], x, p["ln1_g"], dx1, after=tok)

    grads = dict(ln1_g=g_ln1, w_in=g_w_in, b_in=g_b_in, rpb=g_rpb, w_att_o=g_w_att_o, conv_w=g_conv_w,
                 conv_b=g_conv_b, w_rg_a=g_wa, b_rg_a=g_ba, w_rg_i=g_wi,
                 b_rg_i=g_bi, lru_lambda=g_lam, w_rec_o=g_w_rec_o, w_out=g_w_out, ln2_g=g_ln2,
                 w_ff1=g_w_ff1, w_ff2=g_w_ff2, lnf_g=g_lnf)
    return loss, grad_x.reshape(1, T, D), grads


_ANY = pl.BlockSpec(memory_space=pl.ANY)
N_PEERS = N_CHIPS - 1


def _place():
    x, y, c = lax.axis_index("x"), lax.axis_index("y"), lax.axis_index("c")
    peers = [(1 - x, y), (x, 1 - y), (1 - x, 1 - y)]
    return x, y, c, 2 * x + y, peers


def _remote(src, dst, send_sem, recv_sem, dev):
    return pltpu.make_async_remote_copy(src_ref=src, dst_ref=dst, send_sem=send_sem, recv_sem=recv_sem,
                                        device_id=dev, device_id_type=MESH)


def _prefetch_call(body, name, ids, grid, in_specs, out_specs, out_shape, args, semantics=None):
    spec = pltpu.PrefetchScalarGridSpec(num_scalar_prefetch=1, grid=grid, in_specs=in_specs, out_specs=out_specs)
    return pl.pallas_call(body, name=name, grid_spec=spec, out_shape=out_shape,
                          compiler_params=_params(semantics or ("parallel",) * len(grid)))(ids, *args)


def _cast_bf16(name, w, chip_id, after=()):
    rows, cols = w.shape
    rb = min(rows, 256)

    def body(ids_ref, w_ref, *rest):
        rest[-1][...] = w_ref[...].astype(BF16)

    return _prefetch_call(body, name, chip_id, (rows // rb,),
                          [pl.BlockSpec((rb, cols), lambda i, ids: (i, 0))] + [_ANY] * len(after),
                          pl.BlockSpec((None, rb, cols), lambda i, ids: (ids[0], i, 0)),
                          _sds((N_CHIPS, rows, cols), BF16), (w, *after))


def _dma_sems(*counts):
    return [pltpu.SemaphoreType.DMA((k,)) for k in counts]


_HBM = pl.BlockSpec(memory_space=pltpu.HBM)
_SEM = pl.BlockSpec(memory_space=pltpu.SEMAPHORE)
_SPLIT_COPY = pltpu.CompilerParams(has_side_effects=pltpu.SideEffectType.DATAFLOW_SIDE_EFFECTING)
SIBLING_ID = 0
_SPLIT_COPY_SIBLING = pltpu.CompilerParams(has_side_effects=pltpu.SideEffectType.DATAFLOW_SIDE_EFFECTING,
                                           collective_id=SIBLING_ID)


def _sibling_handshake():
    x, y, c = lax.axis_index("x"), lax.axis_index("y"), lax.axis_index("c")
    barrier = pltpu.get_barrier_semaphore()
    pl.semaphore_signal(barrier, inc=1, device_id=(x, y, 1 - c), device_id_type=MESH)
    pl.semaphore_wait(barrier, 1)


def _hbm(arrays):
    return [pltpu.with_memory_space_constraint(a, pltpu.HBM) for a in arrays]


def _hbm_like(arrays):
    return [pltpu.HBM(a.shape, a.dtype) for a in arrays]


def _halves(buf, c):
    half = buf.shape[1] // 2
    return pl.ds(c * half, half), pl.ds((1 - c) * half, half)


def _gather_start(name, slots):
    n = len(slots)
    nk = n * N_PEERS

    def body(*refs):
        bufs = refs[n:2 * n]
        send_sems, recv_sems, token = refs[2 * n:]
        x, y, c, chip, peers = _place()
        for t in range(n):
            mine, _ = _halves(bufs[t], c)
            for r, (px, py) in enumerate(peers):
                k = t * N_PEERS + r
                own = bufs[t].at[chip, mine]
                _remote(own, own, send_sems.at[k], recv_sems.at[k], (px, py, c)).start()
        token[...] = jnp.zeros_like(token)

    res = pl.pallas_call(
        body, name=name, in_specs=[_HBM] * n, out_specs=[_HBM] * n + [_SEM, _SEM, pl.BlockSpec(memory_space=pltpu.VMEM)],
        out_shape=_hbm_like(slots) + [pltpu.SemaphoreType.DMA((nk,)), pltpu.SemaphoreType.DMA((nk,)),
                                      _sds((8, 128), F32)],
        input_output_aliases={t: t for t in range(n)}, compiler_params=_SPLIT_COPY)(*_hbm(slots))
    return res[:n], (res[n], res[n + 1]), res[n + 2]


def _gather_wait(name, bufs, sems, after):
    n = len(bufs)
    after = tuple(after) if isinstance(after, (tuple, list)) else (after,)

    def body(*refs):
        ins = refs[:n]
        send_sems, recv_sems = refs[n], refs[n + 1]
        x, y, c, chip, peers = _place()
        for t in range(n):
            mine, _ = _halves(ins[t], c)
            for r, (px, py) in enumerate(peers):
                k = t * N_PEERS + r
                cp = _remote(ins[t].at[chip, mine], ins[t].at[2 * px + py, mine], send_sems.at[k], recv_sems.at[k],
                             (px, py, c))
                cp.wait_send()
                cp.wait_recv()

    return pl.pallas_call(
        body, name=name, in_specs=[_HBM] * n + [_SEM, _SEM] + [_ANY] * len(after), out_specs=[_HBM] * n,
        out_shape=_hbm_like(bufs), input_output_aliases={t: t for t in range(n)},
        compiler_params=_SPLIT_COPY)(*bufs, *sems, *after)


def _gather_forward(name, bufs):
    n = len(bufs)
    nk = n * N_PEERS

    def body(*refs):
        _sibling_handshake()
        outs = refs[n:2 * n]
        send_sems, recv_sems = refs[2 * n:]
        x, y, c, chip, peers = _place()
        sibling = (x, y, 1 - c)
        sends = []
        for t in range(n):
            mine, _ = _halves(outs[t], c)
            for r, (px, py) in enumerate(peers):
                k = t * N_PEERS + r
                landed = outs[t].at[2 * px + py, mine]
                sends.append(_remote(landed, landed, send_sems.at[k], recv_sems.at[k], sibling))
                sends[-1].start()
        for t in range(n):
            _, theirs = _halves(outs[t], c)
            for r, (px, py) in enumerate(peers):
                k = t * N_PEERS + r
                landed = outs[t].at[2 * px + py, theirs]
                _remote(landed, landed, send_sems.at[k], recv_sems.at[k], sibling).wait_recv()
        for cp in sends:
            cp.wait_send()

    return pl.pallas_call(
        body, name=name, in_specs=[_ANY] * n, out_specs=[_ANY] * n, out_shape=[_sds(b.shape, b.dtype) for b in bufs],
        input_output_aliases={t: t for t in range(n)}, scratch_shapes=_dma_sems(nk, nk),
        compiler_params=pltpu.CompilerParams(collective_id=SIBLING_ID))(*bufs)


def _pair_copies(n, srcs, lands, send_sems, recv_sems):
    x, y, c, _, _ = _place()
    sibling = (x, y, 1 - c)
    copies = []
    for t in range(n):
        half = srcs[t].shape[1] // 2
        for j in range(N_CHIPS):
            k = t * N_CHIPS + j
            copies.append(_remote(srcs[t].at[j, pl.ds((1 - c) * half, half)], lands[t].at[j],
                                  send_sems.at[k], recv_sems.at[k], sibling))
    for t in range(n, len(srcs)):
        k = n * N_CHIPS + t - n
        copies.append(_remote(srcs[t], lands[t], send_sems.at[k], recv_sems.at[k], sibling))
    return copies


def _pair_start(name, grads, wholes=()):
    n = len(grads)
    srcs = list(grads) + list(wholes)
    m = len(srcs)
    lands = [pltpu.HBM((N_CHIPS, g.shape[1] // 2, g.shape[2]), F32) for g in grads] + _hbm_like(wholes)
    ns = n * N_CHIPS + len(wholes)

    def body(*refs):
        _sibling_handshake()
        src_refs, land_refs = refs[m:2 * m], refs[2 * m:3 * m]
        send_sems, recv_sems, token = refs[3 * m:]
        for cp in _pair_copies(n, src_refs, land_refs, send_sems, recv_sems):
            cp.start()
        token[...] = jnp.zeros_like(token)

    res = pl.pallas_call(
        body, name=name, in_specs=[_HBM] * m,
        out_specs=[_HBM] * (2 * m) + [_SEM, _SEM, pl.BlockSpec(memory_space=pltpu.VMEM)],
        out_shape=_hbm_like(srcs) + lands + [pltpu.SemaphoreType.DMA((ns,)), pltpu.SemaphoreType.DMA((ns,)),
                                             _sds((8, 128), F32)],
        input_output_aliases={t: t for t in range(m)}, compiler_params=_SPLIT_COPY_SIBLING)(*_hbm(srcs))
    return (res[:m], res[m:2 * m], (res[2 * m], res[2 * m + 1])), res[2 * m + 2]


def _pair_wait(name, flight, n, after):
    srcs, lands, sems = flight
    m = len(srcs)

    def body(*refs):
        for cp in _pair_copies(n, refs[:m], refs[m:2 * m], refs[2 * m], refs[2 * m + 1]):
            cp.wait_send()
            cp.wait_recv()

    res = pl.pallas_call(
        body, name=name, in_specs=[_HBM] * (2 * m) + [_SEM, _SEM, _ANY], out_specs=[_HBM] * (2 * m),
        out_shape=_hbm_like(srcs) + _hbm_like(lands), input_output_aliases={t: t for t in range(2 * m)},
        compiler_params=_SPLIT_COPY)(*srcs, *lands, *sems, after)
    return res[:m], res[m:]


def _chip_copies(srcs, lands, small_src, small_land, send_sems, recv_sems):
    x, y, c, chip, peers = _place()
    n = len(srcs)
    copies = []
    for r, (px, py) in enumerate(peers):
        for t in range(n):
            k = t * N_PEERS + r
            copies.append(_remote(srcs[t].at[2 * px + py], lands[t].at[r], send_sems.at[k], recv_sems.at[k], (px, py, c)))
        if small_src is not None:
            k = n * N_PEERS + r
            half_s = small_src.shape[0] // 2
            copies.append(_remote(small_src.at[pl.ds(c * half_s, half_s)], small_land.at[r],
                                  send_sems.at[k], recv_sems.at[k], (px, py, c)))
    return copies


def _chip_start(name, sums_bf16, small=None):
    n = len(sums_bf16)
    srcs = list(sums_bf16) + ([small] if small is not None else [])
    m = len(srcs)
    lands = [pltpu.HBM((N_PEERS,) + s.shape[1:], BF16) for s in sums_bf16]
    if small is not None:
        lands.append(pltpu.HBM((N_PEERS, small.shape[0] // 2, 128), F32))
    nk = m * N_PEERS

    def body(*refs):
        src_refs, land_refs = refs[m:2 * m], refs[2 * m:3 * m]
        send_sems, recv_sems, token = refs[3 * m:]
        small_src, small_land = (src_refs[n], land_refs[n]) if small is not None else (None, None)
        for cp in _chip_copies(src_refs[:n], land_refs[:n], small_src, small_land, send_sems, recv_sems):
            cp.start()
        token[...] = jnp.zeros_like(token)

    res = pl.pallas_call(
        body, name=name, in_specs=[_HBM] * m,
        out_specs=[_HBM] * (2 * m) + [_SEM, _SEM, pl.BlockSpec(memory_space=pltpu.VMEM)],
        out_shape=_hbm_like(srcs) + lands + [pltpu.SemaphoreType.DMA((nk,)), pltpu.SemaphoreType.DMA((nk,)),
                                             _sds((8, 128), F32)],
        input_output_aliases={t: t for t in range(m)}, compiler_params=_SPLIT_COPY)(*_hbm(srcs))
    return (res[:m], res[m:2 * m], (res[2 * m], res[2 * m + 1])), res[2 * m + 2]


def _chip_wait(name, flight, with_small, after):
    srcs, lands, sems = flight
    m = len(srcs)
    n = m - 1 if with_small else m

    def body(*refs):
        src_refs, land_refs = refs[:m], refs[m:2 * m]
        send_sems, recv_sems = refs[2 * m], refs[2 * m + 1]
        small_src, small_land = (src_refs[n], land_refs[n]) if with_small else (None, None)
        for cp in _chip_copies(src_refs[:n], land_refs[:n], small_src, small_land, send_sems, recv_sems):
            cp.wait_send()
            cp.wait_recv()

    res = pl.pallas_call(
        body, name=name, in_specs=[_HBM] * (2 * m) + [_SEM, _SEM, _ANY], out_specs=[_HBM] * (2 * m),
        out_shape=_hbm_like(srcs) + _hbm_like(lands), input_output_aliases={t: t for t in range(2 * m)},
        compiler_params=_SPLIT_COPY)(*srcs, *lands, *sems, after)
    return res[:m], res[m:]


def _swap_start(name, bufs):
    n = len(bufs)

    def body(*refs):
        _sibling_handshake()
        outs = refs[n:2 * n]
        send_sems, recv_sems, token = refs[2 * n:]
        x, y, c, _, _ = _place()
        for t in range(n):
            h = outs[t].shape[0] // 2
            mine = outs[t].at[pl.ds(c * h, h)]
            _remote(mine, mine, send_sems.at[t], recv_sems.at[t], (x, y, 1 - c)).start()
        token[...] = jnp.zeros_like(token)

    res = pl.pallas_call(
        body, name=name, in_specs=[_HBM] * n, out_specs=[_HBM] * n + [_SEM, _SEM, pl.BlockSpec(memory_space=pltpu.VMEM)],
        out_shape=_hbm_like(bufs) + [pltpu.SemaphoreType.DMA((n,)), pltpu.SemaphoreType.DMA((n,)), _sds((8, 128), F32)],
        input_output_aliases={t: t for t in range(n)}, compiler_params=_SPLIT_COPY_SIBLING)(*_hbm(bufs))
    return (res[:n], (res[n], res[n + 1])), res[n + 2]


def _swap_wait(name, flight, after):
    bufs, sems = flight
    n = len(bufs)

    def body(*refs):
        ins = refs[:n]
        send_sems, recv_sems = refs[n], refs[n + 1]
        x, y, c, _, _ = _place()
        for t in range(n):
            h = ins[t].shape[0] // 2
            cp = _remote(ins[t].at[pl.ds(c * h, h)], ins[t].at[pl.ds((1 - c) * h, h)], send_sems.at[t],
                         recv_sems.at[t], (x, y, 1 - c))
            cp.wait_send()
            cp.wait_recv()

    return pl.pallas_call(
        body, name=name, in_specs=[_HBM] * n + [_SEM, _SEM, _ANY], out_specs=[_HBM] * n, out_shape=_hbm_like(bufs),
        input_output_aliases={t: t for t in range(n)}, compiler_params=_SPLIT_COPY)(*bufs, *sems, after)


def _pair_sum(name, grad, got, ids):
    _, rows, cols = got.shape
    rb = min(rows, 256)
    nb = rows // rb
    blk = pl.BlockSpec((None, rb, cols), lambda i, j, ids: (j, i, 0))
    mine = pl.BlockSpec((None, rb, cols), lambda i, j, ids: (j, ids[1] * nb + i, 0))
    own = pl.BlockSpec((rb, cols), lambda i, j, ids: (i, 0))

    def body(ids_ref, a_ref, b_ref, s_ref, sb_ref):
        s = a_ref[...] + b_ref[...]
        sb_ref[...] = s.astype(BF16)

        @pl.when(pl.program_id(1) == ids_ref[0])
        def _():
            s_ref[...] = s

    return _prefetch_call(body, name, ids, (nb, N_CHIPS), [mine, blk], [own, blk],
                          [_sds((rows, cols), F32), _sds(got.shape, BF16)], (grad, got),
                          semantics=("parallel", "arbitrary"))


def _chip_sum(name, own_sum, got, ids):
    rows, cols = own_sum.shape
    rb = min(rows, 256)
    nb = rows // rb
    own = pl.BlockSpec((rb, cols), lambda i, ids: (i, 0))
    blk3 = pl.BlockSpec((N_PEERS, rb, cols), lambda i, ids: (0, i, 0))
    out = pl.BlockSpec((rb, cols), lambda i, ids: (ids[1] * nb + i, 0))

    def body(ids_ref, a_ref, b_ref, o_ref):
        o_ref[...] = ((a_ref[...] + b_ref[0].astype(F32)) + b_ref[1].astype(F32)) + b_ref[2].astype(F32)

    return _prefetch_call(body, name, ids, (nb,), [own, blk3], out, _sds((2 * rows, cols), F32), (own_sum, got))


SMALL_RB = 280


def _small_pair_sum(own, got):
    blk = pl.BlockSpec((SMALL_RB, 128), lambda i: (i, 0))

    def body(a_ref, b_ref, o_ref):
        o_ref[...] = a_ref[...] + b_ref[...]

    return pl.pallas_call(body, name="small_pair_sum", grid=(own.shape[0] // SMALL_RB,), in_specs=[blk, blk],
                          out_specs=blk, out_shape=_sds(own.shape, F32),
                          compiler_params=_params(("parallel",)))(own, got)


def _small_chip_sum(pair, got, ids):
    nb = pair.shape[0] // 2 // SMALL_RB
    half = pl.BlockSpec((SMALL_RB, 128), lambda i, ids: (ids[1] * nb + i, 0))
    blk3 = pl.BlockSpec((N_PEERS, SMALL_RB, 128), lambda i, ids: (0, i, 0))

    def body(ids_ref, a_ref, b_ref, o_ref):
        o_ref[...] = (a_ref[...] + b_ref[1]) + (b_ref[0] + b_ref[2])

    return _prefetch_call(body, "small_chip_sum", ids, (nb,), [half, blk3], half, _sds(pair.shape, F32), (pair, got))


def _adamw_math(w, g, m, v):
    m = ADAM_B1 * m + (1.0 - ADAM_B1) * g
    v = ADAM_B2 * v + (1.0 - ADAM_B2) * (g * g)
    m_hat = m / (1.0 - ADAM_B1 ** ADAM_STEP)
    v_hat = v / (1.0 - ADAM_B2 ** ADAM_STEP)
    delta = -ADAM_LR * (m_hat / (jnp.sqrt(v_hat) + ADAM_EPS) + ADAM_WD * w)
    return delta, m, v


def _adamw(name, w, g, m, v, rb=None):
    rows, cols = w.shape
    rb = rows if rb is None else rb
    blk = pl.BlockSpec((rb, cols), lambda i: (i, 0))

    def body(w_ref, g_ref, m_ref, v_ref, d_ref, nm_ref, nv_ref):
        d, nm, nv = _adamw_math(w_ref[...], g_ref[...], m_ref[...], v_ref[...])
        d_ref[...] = d
        nm_ref[...] = nm
        nv_ref[...] = nv

    return pl.pallas_call(body, name=name, grid=(rows // rb,), in_specs=[blk] * 4, out_specs=[blk] * 3,
                          out_shape=[_sds(w.shape, F32)] * 3, compiler_params=_params(("parallel",)))(w, g, m, v)


def _adamw_small(ws, gs, ms, vs):
    n = len(ws)

    def body(*refs):
        for t in range(n):
            w_ref, g_ref, m_ref, v_ref = (refs[k * n + t] for k in range(4))
            d, nm, nv = _adamw_math(w_ref[...], g_ref[...], m_ref[...], v_ref[...])
            for k, val in enumerate((d, nm, nv)):
                refs[(4 + k) * n + t][...] = val

    res = pl.pallas_call(body, name="adamw_small", out_shape=[_sds(a.shape, F32) for a in ws] * 3,
                         compiler_params=_params())(*ws, *gs, *ms, *vs)
    return [(res[t], res[n + t], res[2 * n + t]) for t in range(n)]


BIG = ("w_in", "w_att_o", "w_rec_o", "w_out", "w_ff1", "w_ff2")
SHARDED_VECS = ("conv_w", "b_rg_a", "b_rg_i", "lru_lambda")
SMALL = ("ln1_g", "b_in", "rpb", "conv_w", "conv_b", "w_rg_a", "b_rg_a", "w_rg_i", "b_rg_i", "lru_lambda",
         "ln2_g", "lnf_g")
SMALL_ROWS = 2240
ORDER = ("ln1_g", "w_in", "b_in", "rpb", "w_att_o", "conv_w", "conv_b", "w_rg_a", "b_rg_a", "w_rg_i", "b_rg_i",
         "lru_lambda", "w_rec_o", "w_out", "ln2_g", "w_ff1", "w_ff2", "lnf_g")


def _pack_small(grads, loss):
    parts, sizes = [], {}
    for n in SMALL:
        flat = grads[n].reshape(-1)
        pad = (-flat.shape[0]) % 128
        sizes[n] = (flat.shape[0], flat.shape[0] + pad)
        parts.append(jnp.pad(flat, (0, pad)))
    total = sum(s[1] for s in sizes.values())
    parts.append(jnp.pad(loss.reshape(1), (0, SMALL_ROWS * 128 - total - 1)))
    return jnp.concatenate(parts).reshape(SMALL_ROWS, 128), sizes


def _unpack_small(buf, sizes, shapes):
    flat = buf.reshape(-1)
    out, pos = {}, 0
    for n in SMALL:
        size, padded = sizes[n]
        out[n] = flat[pos:pos + size].reshape(shapes[n])
        pos += padded
    return out, flat[pos]


def _gather_weights(w, chip):
    chip_id = chip.astype(jnp.int32).reshape(1)
    vec_rows = [w[n][0] for n in SHARDED_VECS]
    vec_shard = jnp.concatenate(vec_rows + [jnp.zeros((16 - 10, D // N_CHIPS), F32)], axis=0)
    vec_slots = lax.dynamic_update_slice(jnp.zeros((N_CHIPS, 16, D // N_CHIPS), F32), vec_shard[None], (chip, 0, 0))
    bufs_a, sems_a, token_a = _gather_start("gather_start_first", [_cast_bf16("cast_w_in", w["w_in"][0], chip_id), vec_slots])
    rest_names = BIG[1:]
    bufs_b, sems_b, token_b = _gather_start(
        "gather_start_rest", [_cast_bf16("cast_" + n, w[n][0], chip_id, after=(token_a,)) for n in rest_names])

    def first(after):
        w_in_full, vec_full = _gather_forward("gather_forward_first", _gather_wait("gather_wait_first", bufs_a, sems_a, after))
        vecs = vec_full.transpose(1, 0, 2).reshape(16, D)
        return dict(w_in=w_in_full, conv_w=vecs[0:4], b_rg_a=vecs[4:6], b_rg_i=vecs[6:8], lru_lambda=vecs[8:10])

    def rest(after):
        full = dict(zip(rest_names, _gather_forward("gather_forward_rest",
                                                    _gather_wait("gather_wait_rest", bufs_b, sems_b, after))))
        return dict(w_att_o=full["w_att_o"], w_ff1=full["w_ff1"], w_rec_o=full["w_rec_o"].reshape(D, D),
                    w_out=full["w_out"].reshape(D, D), w_ff2=full["w_ff2"].reshape(D_FF, D))

    p = dict(ln1_g=w["ln1_g"], b_in=w["b_in"], rpb=w["rpb"][0], conv_b=w["conv_b"], w_rg_a=w["w_rg_a"][0],
             w_rg_i=w["w_rg_i"][0], ln2_g=w["ln2_g"], lnf_g=w["lnf_g"].reshape(1, D))
    return p, ((token_b,), first, rest)


class _Reducer:
    def __init__(self, ids):
        self.ids = ids
        self.groups = {}

    def begin(self, tag, grads, small=None):
        names = list(grads)
        big = [grads[n].reshape(N_CHIPS, -1, grads[n].shape[-1]) for n in names]
        flight, token = _pair_start("pair_start_" + tag, big, [] if small is None else [small])
        self.groups[tag] = dict(names=names, pair=flight, small=small is not None)
        return (token,)

    def advance(self, tag, after):
        grp = self.groups[tag]
        n = len(grp["names"])
        mine, got = _pair_wait("pair_wait_" + tag, grp["pair"], n, after)
        sums = [_pair_sum("pair_sum_" + name, a, b, self.ids) for name, a, b in zip(grp["names"], mine, got)]
        small_sum = _small_pair_sum(mine[n], got[n]) if grp["small"] else None
        grp["chip"], token = _chip_start("chip_start_" + tag, [s[1] for s in sums], small_sum)
        grp["sums"] = [s[0] for s in sums]
        self.last_token = token
        return (token,)

    def finish(self, tag, after):
        grp = self.groups[tag]
        srcs, lands = _chip_wait("chip_wait_" + tag, grp["chip"], grp["small"], after)
        halves = [_chip_sum("chip_sum_" + name, s, b, self.ids) for name, s, b in zip(grp["names"], grp["sums"], lands)]
        if grp["small"]:
            halves.append(_small_chip_sum(srcs[-1], lands[-1], self.ids))
        grp["swap"], token = _swap_start("swap_start_" + tag, halves)
        return token

    def interlude(self, tokens):
        after = tokens[0]
        for tag in list(self.groups)[:-1]:
            after = self.finish(tag, after)
        return after

    def result(self, tag, after):
        return _swap_wait("swap_wait_" + tag, self.groups[tag]["swap"], after)


def kernel(x, ln1_g, w_in, b_in, rpb, w_att_o, conv_w, conv_b, w_rg_a, b_rg_a, w_rg_i, b_rg_i, lru_lambda, w_rec_o, w_out, ln2_g, w_ff1, w_ff2, lnf_g, loss_target, m_ln1_g, m_w_in, m_b_in, m_rpb, m_w_att_o, m_conv_w, m_conv_b, m_w_rg_a, m_b_rg_a, m_w_rg_i, m_b_rg_i, m_lru_lambda, m_w_rec_o, m_w_out, m_ln2_g, m_w_ff1, m_w_ff2, m_lnf_g, v_ln1_g, v_w_in, v_b_in, v_rpb, v_w_att_o, v_conv_w, v_conv_b, v_w_rg_a, v_b_rg_a, v_w_rg_i, v_b_rg_i, v_lru_lambda, v_w_rec_o, v_w_out, v_ln2_g, v_w_ff1, v_w_ff2, v_lnf_g):
    w = dict(ln1_g=ln1_g, w_in=w_in, b_in=b_in, rpb=rpb, w_att_o=w_att_o, conv_w=conv_w, conv_b=conv_b,
             w_rg_a=w_rg_a, b_rg_a=b_rg_a, w_rg_i=w_rg_i, b_rg_i=b_rg_i, lru_lambda=lru_lambda, w_rec_o=w_rec_o,
             w_out=w_out, ln2_g=ln2_g, w_ff1=w_ff1, w_ff2=w_ff2, lnf_g=lnf_g)
    m = dict(ln1_g=m_ln1_g, w_in=m_w_in, b_in=m_b_in, rpb=m_rpb, w_att_o=m_w_att_o, conv_w=m_conv_w,
             conv_b=m_conv_b, w_rg_a=m_w_rg_a, b_rg_a=m_b_rg_a, w_rg_i=m_w_rg_i, b_rg_i=m_b_rg_i,
             lru_lambda=m_lru_lambda, w_rec_o=m_w_rec_o, w_out=m_w_out, ln2_g=m_ln2_g, w_ff1=m_w_ff1,
             w_ff2=m_w_ff2, lnf_g=m_lnf_g)
    v = dict(ln1_g=v_ln1_g, w_in=v_w_in, b_in=v_b_in, rpb=v_rpb, w_att_o=v_w_att_o, conv_w=v_conv_w,
             conv_b=v_conv_b, w_rg_a=v_w_rg_a, b_rg_a=v_b_rg_a, w_rg_i=v_w_rg_i, b_rg_i=v_b_rg_i,
             lru_lambda=v_lru_lambda, w_rec_o=v_w_rec_o, w_out=v_w_out, ln2_g=v_ln2_g, w_ff1=v_w_ff1,
             w_ff2=v_w_ff2, lnf_g=v_lnf_g)
    chip = 2 * lax.axis_index("x") + lax.axis_index("y")
    ids = jnp.stack([chip, lax.axis_index("c")]).astype(jnp.int32)

    out_grad, out_delta, out_m, out_v = {}, {}, {}, {}

    def update(n, gn):
        shape, two_d = w[n].shape, gn.shape
        d, nm, nv = _adamw("adamw_" + n, w[n].reshape(two_d), gn, m[n].reshape(two_d), v[n].reshape(two_d), 256)
        out_grad[n], out_delta[n], out_m[n], out_v[n] = (gn.reshape(shape), d.reshape(shape), nm.reshape(shape),
                                                         nv.reshape(shape))
        return d

    reducer = _Reducer(ids)
    p, late = _gather_weights(w, chip)
    loss, grad_x, g = _local_step(x, loss_target, p, late, reducer)
    small, sizes = _pack_small(g, loss + reducer.last_token[:1, :1])
    after = reducer.begin("small", {}, small)[0]
    for tag in ("ff", "proj", "in"):
        for n, red in zip(reducer.groups[tag]["names"], reducer.result(tag, after)):
            after = update(n, red)
        if tag == "ff":
            after = reducer.finish("in", reducer.advance("small", after)[0])
    (small_red,) = reducer.result("small", reducer.finish("small", after))
    gsmall, loss = _unpack_small(small_red, sizes, {n: g[n].shape for n in SMALL})
    two_d = {n: (int(np.prod(w[n].shape[:-1])), w[n].shape[-1]) for n in SMALL}
    for n in SHARDED_VECS:
        gsmall[n] = lax.dynamic_slice_in_dim(gsmall[n], chip * (D // N_CHIPS), D // N_CHIPS, axis=1)
    gs = [gsmall[n].reshape(two_d[n]) for n in SMALL]
    updates = _adamw_small([w[n].reshape(two_d[n]) for n in SMALL], gs, [m[n].reshape(two_d[n]) for n in SMALL],
                           [v[n].reshape(two_d[n]) for n in SMALL])
    for n, gn, (d, nm, nv) in zip(SMALL, gs, updates):
        shape = w[n].shape
        out_grad[n], out_delta[n], out_m[n], out_v[n] = (gn.reshape(shape), d.reshape(shape), nm.reshape(shape),
                                                         nv.reshape(shape))
    return (loss, grad_x, *[out_grad[n] for n in ORDER], *[out_delta[n] for n in ORDER],
            *[out_m[n] for n in ORDER], *[out_v[n] for n in ORDER])
```

```python
import numpy as np
import jax
import jax.numpy as jnp
from jax import lax
from jax.experimental import pallas as pl
from jax.experimental.pallas import tpu as pltpu

F32 = jnp.float32
BF16 = jnp.bfloat16

T = 2048
D = 1024
D_ATT = 512
D_IN = 5632
D_FF = 4096
N_HEADS = 8
HEAD_DIM = 64
GRID_W = 64
N_ROWS = T // GRID_W
WIN_H = 8
WIN_W = 16
KEYS = WIN_H * GRID_W
N_CHIPS = 4
EPS = 1e-6
LRU_C = 8.0
SCALE = HEAD_DIM ** -0.5
REC_CB = 256
REC_BLOCK = 64
REC_CHUNK = 256
PAD = 8

ADAM_LR = 0.001
ADAM_B1 = 0.9
ADAM_B2 = 0.999
ADAM_EPS = 1e-08
ADAM_WD = 0.01
ADAM_STEP = 10

VMEM_LIMIT = 56 * 1024 * 1024

NN = (((1,), (0,)), ((), ()))
NT = (((1,), (1,)), ((), ()))
TN = (((0,), (0,)), ((), ()))
MESH = pl.DeviceIdType.MESH


def _params(sem=None):
    return pltpu.CompilerParams(dimension_semantics=sem, vmem_limit_bytes=VMEM_LIMIT)


def _dot(a, b, dims):
    return lax.dot_general(a, b, dims, preferred_element_type=F32)


def _sigmoid(x):
    return 0.5 * jnp.tanh(0.5 * x) + 0.5


def _matmul(name, a, b, *, dims, grid, a_spec, b_spec, out_shapes, out_specs, acc_shape,
            extras=(), extra_specs=(), epilogue=None, colsum_spec=None, colsum_shape=None, after=(),
            semantics=("parallel", "parallel", "arbitrary"), epilogue_takes_first=False):
    nk = grid[2]
    n_extra = len(extras)
    n_out = len(out_shapes)
    with_colsum = colsum_spec is not None

    def body(a_ref, b_ref, *rest):
        ex = rest[:n_extra]
        rest = rest[:n_extra] + rest[n_extra + len(after):]
        outs = rest[n_extra:n_extra + n_out]
        pos = n_extra + n_out
        cs_out = rest[pos] if with_colsum else None
        pos += 1 if with_colsum else 0
        acc = rest[pos]
        cs_acc = rest[pos + 1] if with_colsum else None
        k = pl.program_id(2)
        first_tile = pl.program_id(0) == 0

        @pl.when(k == 0)
        def _():
            acc[...] = jnp.zeros_like(acc)
            if with_colsum:
                cs_acc[...] = jnp.zeros_like(cs_acc)

        bv = b_ref[...]
        acc[...] += _dot(a_ref[...].astype(BF16), bv.astype(BF16), dims)
        if with_colsum:
            cs_acc[...] += jnp.sum(bv.astype(F32), axis=0, keepdims=True)

        @pl.when(k == nk - 1)
        def _():
            r = acc[...]
            if epilogue is None:
                outs[0][...] = r.astype(outs[0].dtype)
            elif epilogue_takes_first:
                epilogue(r, ex, outs, first_tile)
            else:
                epilogue(r, ex, outs)
            if with_colsum:
                cs_out[...] = cs_acc[...]

    shapes = list(out_shapes)
    specs = list(out_specs)
    scratch = [pltpu.VMEM(acc_shape, F32)]
    if with_colsum:
        shapes.append(colsum_shape)
        specs.append(colsum_spec)
        scratch.append(pltpu.VMEM((1, acc_shape[1]), F32))
    res = pl.pallas_call(
        body, name=name, grid=grid,
        in_specs=[a_spec, b_spec, *extra_specs] + [_ANY] * len(after),
        out_specs=specs, out_shape=shapes, scratch_shapes=scratch,
        compiler_params=_params(semantics),
    )(a, b, *extras, *after)
    return res


def _sds(shape, dtype):
    return jax.ShapeDtypeStruct(shape, dtype)


TM = 1024
NI = T // TM
TJ = T
NJ = T // TJ


def _mm_nn_cols(name, a, wg, out_dtype, *, bias=None, extras=(), extra_specs=(), epilogue=None,
                out_shapes=None, out_specs=None):
    k_dim, n4 = wg.shape[1], wg.shape[2]
    ex, exs = list(extras), list(extra_specs)
    if bias is not None:
        ex = [bias] + ex
        exs = [pl.BlockSpec((1, n4), lambda j, i, k: (0, j))] + exs
        user_ep = epilogue

        def epilogue(r, e, outs):
            r = r + e[0][...]
            if user_ep is None:
                outs[0][...] = r.astype(outs[0].dtype)
            else:
                user_ep(r, e[1:], outs)
    if out_shapes is None:
        out_shapes = [_sds((T, N_CHIPS * n4), out_dtype)]
        out_specs = [pl.BlockSpec((TJ, n4), lambda j, i, k: (i, j))]
    return _matmul(
        name, a, wg, dims=NN, grid=(N_CHIPS, NJ, 1),
        a_spec=pl.BlockSpec((TJ, k_dim), lambda j, i, k: (i, 0)),
        b_spec=pl.BlockSpec((None, k_dim, n4), lambda j, i, k: (j, 0, 0)),
        out_shapes=out_shapes, out_specs=out_specs, acc_shape=(TJ, n4),
        extras=ex, extra_specs=exs, epilogue=epilogue)


def _mm_nt_cols_rms_bwd(name, a, wg, x, g, dres, after=(), bf16_copy=False):
    n4 = wg.shape[2]
    row = pl.BlockSpec((TM, D), lambda i, j, k: (i, 0))
    vec = pl.BlockSpec((1, D), lambda i, j, k: (0, 0))

    def epilogue(dhv, ex, outs, first):
        x_ref, g_ref, dres_ref = ex
        dx_ref, dg_ref = outs[0], outs[-1]
        xv = x_ref[...]
        rstd = lax.rsqrt(jnp.mean(xv * xv, axis=-1, keepdims=True) + EPS)
        xhat = xv * rstd
        dy = dhv * g_ref[...]
        dx = dres_ref[...] + rstd * (dy - xhat * jnp.mean(dy * xhat, axis=-1, keepdims=True))
        dx_ref[...] = dx
        if bf16_copy:
            outs[1][...] = dx.astype(BF16)
        part = jnp.sum(dhv * xhat, axis=0, keepdims=True)

        @pl.when(first)
        def _():
            dg_ref[...] = part

        @pl.when(jnp.logical_not(first))
        def _():
            dg_ref[...] += part

    return _matmul(
        name, a, wg, dims=NT, grid=(NI, 1, N_CHIPS),
        a_spec=pl.BlockSpec((TM, n4), lambda i, j, k: (i, k)),
        b_spec=pl.BlockSpec((None, D, n4), lambda i, j, k: (k, 0, 0)),
        out_shapes=[_sds((T, D), F32)] + [_sds((T, D), BF16)] * bf16_copy + [_sds((1, D), F32)],
        out_specs=[row] + [row] * bf16_copy + [vec], acc_shape=(TM, D),
        extras=[x, g, dres], extra_specs=[row, vec, row], epilogue=epilogue, after=after,
        semantics=("arbitrary", "arbitrary", "arbitrary"), epilogue_takes_first=True)


def _mm_nt_rows(name, a, w, out_dtype, *, tn, extras=(), extra_specs=(), epilogue=None):
    k_dim, n = w.shape
    return _matmul(
        name, a, w, dims=NT, grid=(k_dim // tn, NJ, 1),
        a_spec=pl.BlockSpec((TJ, n), lambda j, i, k: (i, 0)),
        b_spec=pl.BlockSpec((tn, n), lambda j, i, k: (j, 0)),
        out_shapes=[_sds((T, k_dim), out_dtype)],
        out_specs=[pl.BlockSpec((TJ, tn), lambda j, i, k: (i, j))], acc_shape=(TJ, tn),
        extras=extras, extra_specs=extra_specs, epilogue=epilogue)


def _mm_tn_cols(name, a, g, n4, *, colsum=False):
    k_dim = a.shape[1]
    kw = {}
    if colsum:
        kw = dict(colsum_spec=pl.BlockSpec((1, n4), lambda j, i, k: (0, j)),
                  colsum_shape=_sds((1, N_CHIPS * n4), F32))
    return _matmul(
        name, a, g, dims=TN, grid=(N_CHIPS, 1, NJ),
        a_spec=pl.BlockSpec((TJ, k_dim), lambda j, i, k: (k, 0)),
        b_spec=pl.BlockSpec((TJ, n4), lambda j, i, k: (k, j)),
        out_shapes=[_sds((N_CHIPS, k_dim, n4), F32)],
        out_specs=[pl.BlockSpec((None, k_dim, n4), lambda j, i, k: (j, 0, 0))],
        acc_shape=(k_dim, n4), **kw)


def _mm_tn_rows(name, a, g, *, tm):
    k_dim, n = a.shape[1], g.shape[1]
    return _matmul(
        name, a, g, dims=TN, grid=(k_dim // tm, 1, NJ),
        a_spec=pl.BlockSpec((TJ, tm), lambda j, i, k: (k, j)),
        b_spec=pl.BlockSpec((TJ, n), lambda j, i, k: (k, 0)),
        out_shapes=[_sds((k_dim, n), F32)],
        out_specs=[pl.BlockSpec((tm, n), lambda j, i, k: (j, 0))], acc_shape=(tm, n))


TE = 256
NE = T // TE
_ROW = pl.BlockSpec((TE, D), lambda i: (i, 0))
_VEC = pl.BlockSpec((1, D), lambda i: (0, 0))


def _rms_fwd(name, x, g, after=()):
    def body(x_ref, g_ref, *rest):
        h_ref = rest[-1]
        xv = x_ref[...]
        rstd = lax.rsqrt(jnp.mean(xv * xv, axis=-1, keepdims=True) + EPS)
        h_ref[...] = (xv * rstd * g_ref[...]).astype(BF16)

    return pl.pallas_call(body, name=name, grid=(NE,), in_specs=[_ROW, _VEC] + [_ANY] * len(after), out_specs=_ROW,
                          out_shape=_sds((T, D), BF16), compiler_params=_params(("parallel",)))(x, g, *after)


def _mm_x2_loss_head(s, w_ff2, x1, target, g):
    k_dim = w_ff2.shape[0]
    row = pl.BlockSpec((TM, D), lambda i, j, k: (i, 0))
    vec = pl.BlockSpec((1, D), lambda i, j, k: (0, 0))

    def epilogue(r, ex, outs, first):
        x1_ref, t_ref, g_ref = ex
        loss_ref, dx_ref, dxb_ref, dg_ref = outs
        xv = x1_ref[...] + r
        rstd = lax.rsqrt(jnp.mean(xv * xv, axis=-1, keepdims=True) + EPS)
        xhat = xv * rstd
        gv = g_ref[...]
        err = xhat * gv - t_ref[...]
        dy = err * (1.0 / D)
        dxh = dy * gv
        dx = rstd * (dxh - xhat * jnp.mean(dxh * xhat, axis=-1, keepdims=True))
        dx_ref[...] = dx
        dxb_ref[...] = dx.astype(BF16)
        dg_part = jnp.sum(dy * xhat, axis=0, keepdims=True)
        loss_part = (0.5 / D) * jnp.sum(jnp.sum(err * err, axis=1, keepdims=True), axis=0, keepdims=True)

        @pl.when(first)
        def _():
            dg_ref[...] = dg_part
            loss_ref[...] = loss_part

        @pl.when(jnp.logical_not(first))
        def _():
            dg_ref[...] += dg_part
            loss_ref[...] += loss_part

    return _matmul(
        "mm_x2_loss_head", s, w_ff2, dims=NN, grid=(NI, 1, k_dim // D),
        a_spec=pl.BlockSpec((TM, D), lambda i, j, k: (i, k)), b_spec=pl.BlockSpec((D, D), lambda i, j, k: (k, 0)),
        out_shapes=[_sds((1, 1), F32), _sds((T, D), F32), _sds((T, D), BF16), _sds((1, D), F32)],
        out_specs=[pl.BlockSpec((1, 1), lambda i, j, k: (0, 0)), row, row, vec], acc_shape=(TM, D),
        extras=[x1, target, g], extra_specs=[row, row, vec], epilogue=epilogue,
        semantics=("arbitrary", "arbitrary", "arbitrary"), epilogue_takes_first=True)


DZ_Q, DZ_K, DZ_V, DZ_U, DZ_Y, DZ_G_ATT, DZ_G_REC = 0, 512, 1024, 1536, 2560, 3584, 4608
MW = 512
_G_ATT_BLK = 3584 // MW
_G_REC_BLK = 4608 // MW


TB = 512


def _branch_specs():
    def row(cols):
        return pl.BlockSpec((TB, cols), lambda i: (i, 0))

    ga = pl.BlockSpec((TB, MW), lambda i: (i, _G_ATT_BLK))
    ga2 = pl.BlockSpec((TB, MW), lambda i: (i, _G_ATT_BLK + 1))
    gr = pl.BlockSpec((TB, MW), lambda i: (i, _G_REC_BLK))
    gr2 = pl.BlockSpec((TB, MW), lambda i: (i, _G_REC_BLK + 1))
    w_att = pl.BlockSpec((N_CHIPS, D_ATT, D // N_CHIPS), lambda i: (0, 0, 0))
    w_sq = pl.BlockSpec((D, D), lambda i: (0, 0))
    return row, (ga, ga2, gr, gr2), w_att, w_sq


def _gate_values(gate_refs):
    ga, ga2, gr, gr2 = (r[...].astype(F32) for r in gate_refs)
    return _sigmoid(jnp.concatenate([ga, ga2], axis=1)), _sigmoid(jnp.concatenate([gr, gr2], axis=1))


def _branches_fwd(att, g, z, x, w_att_o, w_rec_o, w_out, ln2_g):
    row, gate_specs, w_att, w_sq = _branch_specs()

    def body(att_ref, g_ref, ga_ref, ga2_ref, gr_ref, gr2_ref, x_ref, wa_ref, wr_ref, wo_ref, g2_ref,
             ya_ref, yr_ref, m_ref, x1_ref, h2_ref):
        attv = att_ref[...]
        ya = jnp.concatenate([_dot(attv, wa_ref[j], NN) for j in range(N_CHIPS)], axis=1)
        yr = _dot(g_ref[...], wr_ref[...], NN)
        sa, sr = _gate_values((ga_ref, ga2_ref, gr_ref, gr2_ref))
        mixed = (sa * ya + sr * yr).astype(BF16)
        ya_ref[...] = ya
        yr_ref[...] = yr
        m_ref[...] = mixed
        x1 = x_ref[...] + _dot(mixed, wo_ref[...], NN)
        x1_ref[...] = x1
        rstd = lax.rsqrt(jnp.mean(x1 * x1, axis=-1, keepdims=True) + EPS)
        h2_ref[...] = (x1 * rstd * g2_ref[...]).astype(BF16)

    return pl.pallas_call(
        body, name="branches_fwd", grid=(T // TB,),
        in_specs=[row(D_ATT), row(D), *gate_specs, row(D), w_att, w_sq, w_sq, pl.BlockSpec((1, D), lambda i: (0, 0))],
        out_specs=[row(D)] * 5,
        out_shape=[_sds((T, D), F32), _sds((T, D), F32), _sds((T, D), BF16), _sds((T, D), F32), _sds((T, D), BF16)],
        compiler_params=_params(("parallel",)))(att, g, z, z, z, z, x, w_att_o, w_rec_o, w_out, ln2_g)


def _branches_bwd(dx1_b, y_att, y_rec, z, w_att_o, w_rec_o, w_out):
    row, gate_specs, w_att, w_sq = _branch_specs()
    n4 = D // N_CHIPS

    def body(dx_ref, ya_ref, yr_ref, ga_ref, ga2_ref, gr_ref, gr2_ref, wa_ref, wr_ref, wo_ref,
             dya_ref, dyr_ref, datt_ref, dg_ref, dz_ref, dga_ref, dgr_ref, sems):
        rows = pl.ds(pl.multiple_of(pl.program_id(0) * TB, TB), TB)
        dm = _dot(dx_ref[...], wo_ref[...], NT)
        sa, sr = _gate_values((ga_ref, ga2_ref, gr_ref, gr2_ref))
        dya = (dm * sa).astype(BF16)
        dyr = (dm * sr).astype(BF16)
        dya_ref[...] = dya
        dyr_ref[...] = dyr
        dga_ref[...] = (dm * ya_ref[...] * sa * (1.0 - sa)).astype(BF16)
        dgr_ref[...] = (dm * yr_ref[...] * sr * (1.0 - sr)).astype(BF16)
        copies = [pltpu.make_async_copy(dga_ref, dz_ref.at[rows, pl.ds(DZ_G_ATT, D)], sems.at[0]),
                  pltpu.make_async_copy(dgr_ref, dz_ref.at[rows, pl.ds(DZ_G_REC, D)], sems.at[1])]
        for cp in copies:
            cp.start()
        datt = _dot(dya[:, 0:n4], wa_ref[0], NT)
        for j in range(1, N_CHIPS):
            datt = datt + _dot(dya[:, j * n4:(j + 1) * n4], wa_ref[j], NT)
        datt_ref[...] = datt.astype(BF16)
        dg_ref[...] = _dot(dyr, wr_ref[...], NT).astype(BF16)
        for cp in copies:
            cp.wait()

    return pl.pallas_call(
        body, name="branches_bwd", grid=(T // TB,),
        in_specs=[row(D), row(D), row(D), *gate_specs, w_att, w_sq, w_sq],
        out_specs=[row(D)] * 2 + [row(D_ATT), row(D), _ANY],
        out_shape=[_sds((T, D), BF16)] * 2 + [_sds((T, D_ATT), BF16), _sds((T, D), BF16), _sds((T, D_IN), BF16)],
        scratch_shapes=[pltpu.VMEM((TB, D), BF16), pltpu.VMEM((TB, D), BF16), pltpu.SemaphoreType.DMA((2,))],
        compiler_params=_params(("parallel",)))(dx1_b, y_att, y_rec, z, z, z, z, w_att_o, w_rec_o, w_out)


HP = 2 * HEAD_DIM
N_HP = N_HEADS // 2
ATT_UNROLL_FWD = 16
ATT_UNROLL_BWD = 16
DIAG_ROWS = 32


def _window_maps():
    diag = np.zeros((GRID_W * GRID_W, 128), np.float32)
    for qc in range(GRID_W):
        w0 = min(max(qc - WIN_W // 2, 0), GRID_W - WIN_W)
        for kc in range(w0, w0 + WIN_W):
            diag[qc * GRID_W + kc, kc - qc + WIN_W - 1] = 1.0
    return diag, diag.sum(axis=1)[None, :]


def _split3(x):
    a = x.astype(BF16)
    r = x - a.astype(F32)
    b = r.astype(BF16)
    c = (r - b.astype(F32)).astype(BF16)
    return a, b, c


N_DROW = 2 * WIN_H - 1
N_DPAIR = N_DROW - 1


def _bias_pairs(rpb):
    diag, valid = _window_maps()
    r2 = jnp.pad(rpb.reshape(N_HEADS * N_DROW, 2 * WIN_W - 1),
                 ((0, 128 - N_HEADS * N_DROW), (0, 128 - (2 * WIN_W - 1))))

    def body(r_ref, d_ref, v_ref, o_ref):
        dv = d_ref[...]
        t = sum(_dot(part, dv, NN) for part in _split3(r_ref[...]))
        o_ref[...] = jnp.where(v_ref[...] > 0.0, t, -1e30)

    t = pl.pallas_call(body, name="rpb_expand", out_shape=_sds((128, GRID_W * GRID_W), F32),
                       compiler_params=_params())(r2, jnp.asarray(diag.T, BF16), jnp.asarray(valid, F32))
    t = t[:N_HEADS * N_DROW].reshape(N_HEADS, N_DROW, GRID_W, GRID_W)
    return jnp.concatenate([t[:, :N_DPAIR], t[:, 1:]], axis=-1)


def _row_bias(tb_ref, hh, d0):
    return jnp.concatenate([tb_ref[hh, d0 + 2 * ii] for ii in range(WIN_H // 2)], axis=1)


def _row_window(r):
    rs = jnp.clip(r - WIN_H // 2, 0, N_ROWS - WIN_H)
    return pl.multiple_of(r * GRID_W, GRID_W), pl.multiple_of(rs * GRID_W, GRID_W), rs - r + (WIN_H - 1)


def _split_heads(src_ref, dst_ref, scale=None):
    for hh in range(2):
        v = src_ref[:, hh * HEAD_DIM:(hh + 1) * HEAD_DIM]
        dst_ref[hh] = (v if scale is None else v * scale).astype(BF16)


def _attn_items(qb_ref, kb_ref, vb_ref, tb_ref, first_row, n_rows):
    wins = [_row_window(first_row + u) for u in range(n_rows)]
    items = [(u, hh) for u in range(n_rows) for hh in range(2)]
    q = [qb_ref[hh, pl.ds(wins[u][0], GRID_W), :] for u, hh in items]
    k = [kb_ref[hh, pl.ds(wins[u][1], KEYS), :] for u, hh in items]
    v = [vb_ref[hh, pl.ds(wins[u][1], KEYS), :] for u, hh in items]
    s = [_dot(qi, ki, NT) + _row_bias(tb_ref, hh, wins[u][2]) for qi, ki, (u, hh) in zip(q, k, items)]
    m = [jnp.max(si, axis=-1, keepdims=True) for si in s]
    e = [jnp.exp(si - mi) for si, mi in zip(s, m)]
    inv = [1.0 / jnp.sum(ei, axis=-1, keepdims=True) for ei in e]
    p = [ei * li for ei, li in zip(e, inv)]
    return wins, items, q, k, v, p


def _attn_in_specs():
    q = pl.BlockSpec((T, HP), lambda p: (0, p))
    k = pl.BlockSpec((T, HP), lambda p: (0, N_HP + p))
    v = pl.BlockSpec((T, HP), lambda p: (0, 2 * N_HP + p))
    tb = pl.BlockSpec((2, N_DPAIR, GRID_W, HP), lambda p: (p, 0, 0, 0))
    return q, k, v, tb


_HEAD_SCRATCH = pltpu.VMEM((2, T, HEAD_DIM), BF16)


_PROBS = pl.BlockSpec((T, 2 * KEYS), lambda p: (0, p))


def _attn_fwd(z, tb):
    def body(q_ref, k_ref, v_ref, tb_ref, o_ref, p_ref, qb_ref, kb_ref, vb_ref):
        _split_heads(q_ref, qb_ref, SCALE)
        _split_heads(k_ref, kb_ref)
        _split_heads(v_ref, vb_ref)

        def rows(it, carry):
            wins, items, _, _, v, p = _attn_items(qb_ref, kb_ref, vb_ref, tb_ref, it * ATT_UNROLL_FWD, ATT_UNROLL_FWD)
            pb = [pi.astype(BF16) for pi in p]
            o = [_dot(pi, vi, NN) for pi, vi in zip(pb, v)]
            for u, (q0, _, _) in enumerate(wins):
                o_ref[pl.ds(q0, GRID_W), :] = jnp.concatenate(o[2 * u:2 * u + 2], axis=1).astype(BF16)
                p_ref[pl.ds(q0, GRID_W), :] = jnp.concatenate(pb[2 * u:2 * u + 2], axis=1)
            return carry

        lax.fori_loop(0, N_ROWS // ATT_UNROLL_FWD, rows, 0)

    blk = pl.BlockSpec((T, HP), lambda p: (0, p))
    return pl.pallas_call(
        body, name="attn_fwd", grid=(N_HP,), in_specs=list(_attn_in_specs()), out_specs=[blk, _PROBS],
        out_shape=[_sds((T, D_ATT), BF16), _sds((T, N_HEADS * KEYS), BF16)], scratch_shapes=[_HEAD_SCRATCH] * 3,
        compiler_params=_params(("parallel",)))(z, z, z, tb)


def _attn_bwd(z, probs, d_att, dz, after=()):
    def body(q_ref, k_ref, v_ref, p_ref, do_ref, flip_ref, dz_in_ref, *rest):
        (dz_ref, diag_ref, qb_ref, kb_ref, vb_ref, dob_ref, dka_ref, dva_ref, ds_ref,
         dq_ref, dk_ref, dv_ref, sems) = rest[len(after):]
        _split_heads(q_ref, qb_ref, SCALE)
        _split_heads(k_ref, kb_ref)
        _split_heads(v_ref, vb_ref)
        _split_heads(do_ref, dob_ref)
        dka_ref[...] = jnp.zeros_like(dka_ref)
        dva_ref[...] = jnp.zeros_like(dva_ref)
        ds_ref[...] = jnp.zeros_like(ds_ref)

        def rows(it, carry):
            wins = [_row_window(it * ATT_UNROLL_BWD + u) for u in range(ATT_UNROLL_BWD)]
            items = [(u, hh) for u in range(ATT_UNROLL_BWD) for hh in range(2)]
            q = [qb_ref[hh, pl.ds(wins[u][0], GRID_W), :] for u, hh in items]
            k = [kb_ref[hh, pl.ds(wins[u][1], KEYS), :] for u, hh in items]
            v = [vb_ref[hh, pl.ds(wins[u][1], KEYS), :] for u, hh in items]
            pb = [p_ref[pl.ds(wins[u][0], GRID_W), hh * KEYS:(hh + 1) * KEYS] for u, hh in items]
            p = [pi.astype(F32) for pi in pb]
            do = [dob_ref[hh, pl.ds(wins[u][0], GRID_W), :] for u, hh in items]
            dv = [_dot(pi, di, TN) for pi, di in zip(pb, do)]
            dp = [_dot(di, vi, NT) for di, vi in zip(do, v)]
            ds = [pi * (dpi - jnp.sum(dpi * pi, axis=-1, keepdims=True)) for pi, dpi in zip(p, dp)]
            dsb = [d.astype(BF16) for d in ds]
            dq = [_dot(d, ki, NN) * SCALE for d, ki in zip(dsb, k)]
            dk = [_dot(d, qi, TN) for d, qi in zip(dsb, q)]
            for d, (u, hh) in zip(ds, items):
                for ii in range(WIN_H // 2):
                    ds_ref[hh, wins[u][2] + 2 * ii] += d[:, ii * HP:(ii + 1) * HP]
            for u, (q0, _, _) in enumerate(wins):
                dq_ref[pl.ds(q0, GRID_W), :] = jnp.concatenate(dq[2 * u:2 * u + 2], axis=1).astype(BF16)
            for dki, dvi, (u, hh) in zip(dk, dv, items):
                dka_ref[hh, pl.ds(wins[u][1], KEYS), :] += dki
                dva_ref[hh, pl.ds(wins[u][1], KEYS), :] += dvi
            return carry

        lax.fori_loop(0, N_ROWS // ATT_UNROLL_BWD, rows, 0)
        dk_ref[...] = jnp.concatenate([dka_ref[0], dka_ref[1]], axis=1).astype(BF16)
        dv_ref[...] = jnp.concatenate([dva_ref[0], dva_ref[1]], axis=1).astype(BF16)
        cols = pl.multiple_of(pl.program_id(0) * HP, HP)
        copies = [pltpu.make_async_copy(src, dz_ref.at[:, pl.ds(base + cols, HP)], sems.at[t])
                  for t, (src, base) in enumerate(((dq_ref, DZ_Q), (dk_ref, DZ_K), (dv_ref, DZ_V)))]
        for cp in copies:
            cp.start()
        _diag_sums(ds_ref, flip_ref, diag_ref)
        for cp in copies:
            cp.wait()

    blk = pl.BlockSpec((T, HP), lambda p: (0, p))
    q, k, v, _ = _attn_in_specs()
    flip =jnp.asarray(np.eye(HP, dtype=np.float32)[::-1], BF16)
    return pl.pallas_call(
        body, name="attn_bwd", grid=(N_HP,),
        in_specs=[q, k, v, _PROBS, blk, pl.BlockSpec((HP, HP), lambda p: (0, 0)), _ANY] + [_ANY] * len(after),
        out_specs=[_ANY, pl.BlockSpec((None, DIAG_ROWS, HP), lambda p: (p, 0, 0))],
        out_shape=[_sds(dz.shape, dz.dtype), _sds((N_HP, DIAG_ROWS, HP), F32)], input_output_aliases={6: 0},
        scratch_shapes=[_HEAD_SCRATCH] * 4 + [pltpu.VMEM((2, T, HEAD_DIM), F32), pltpu.VMEM((2, T, HEAD_DIM), F32),
                                              pltpu.VMEM((2, N_DPAIR, GRID_W, HP), F32)]
        + [pltpu.VMEM((T, HP), BF16)] * 3 + [pltpu.SemaphoreType.DMA((3,))],
        compiler_params=_params(("parallel",)))(z, z, z, probs, d_att, flip, dz, *after)


def _diag_sums(acc_ref, flip_ref, out_ref):
    flip = flip_ref[...]
    rows = []
    for hh in range(2):
        for pair in range(N_DPAIR):
            reversed_lanes = sum(_dot(part, flip, NN) for part in _split3(acc_ref[hh, pair]))
            skewed = pltpu.roll(reversed_lanes, 0, 1, stride=1, stride_axis=0)
            rows.append(jnp.sum(skewed, axis=0, keepdims=True))
    rows.append(jnp.zeros((DIAG_ROWS - len(rows), HP), F32))
    out_ref[...] = jnp.concatenate(rows, axis=0)


def _rpb_grad(diag_sums):
    g = diag_sums.reshape(N_HP * DIAG_ROWS, HP)
    sel = np.zeros((2, 128, N_HP * DIAG_ROWS), np.float32)
    lane = np.zeros((2, HP, 128), np.float32)
    for h in range(N_HEADS):
        for pair in range(N_DPAIR):
            for half in range(2):
                sel[half, h * N_DROW + pair + half, (h // 2) * DIAG_ROWS + (h % 2) * N_DPAIR + pair] = 1.0
    for j in range(2 * WIN_W - 1):
        for half in range(2):
            lane[half, (HP - 1 - GRID_W * half - (j - (WIN_W - 1))) % HP, j] = 1.0

    def body(g_ref, sel_ref, lane_ref, o_ref):
        parts = _split3(g_ref[...])
        total = None
        for half in range(2):
            picked = sum(_dot(sel_ref[half], part, NN) for part in parts)
            term = sum(_dot(part, lane_ref[half], NN) for part in _split3(picked))
            total = term if total is None else total + term
        o_ref[...] = total

    out = pl.pallas_call(body, name="rpb_grad", out_shape=_sds((128, 128), F32),
                         compiler_params=_params())(g, jnp.asarray(sel, BF16), jnp.asarray(lane, BF16))
    return out[:N_HEADS * N_DROW, :2 * WIN_W - 1].reshape(N_HEADS, N_DROW, 2 * WIN_W - 1)


N_CB = D // REC_CB
N_CHUNK = T // REC_CHUNK
N_TILE = T // 8
_U_BLK = 1536 // REC_CB
_Y_BLK = 2560 // REC_CB


def _block_diag(w):
    per = REC_CB // 64
    wt = w.reshape(2, N_CB, per, 64, 64)
    eye = jnp.eye(per, dtype=w.dtype)
    full = wt[:, :, :, :, None, :] * eye[None, None, :, None, :, None]
    return full.reshape(2, N_CB, REC_CB, REC_CB).astype(BF16)


def _gelu(x):
    c = 0.7978845608028654
    return 0.5 * x * (1.0 + jnp.tanh(c * (x + 0.044715 * x * x * x)))


def _gelu_grad(x):
    c = 0.7978845608028654
    th = jnp.tanh(c * (x + 0.044715 * x * x * x))
    return 0.5 * (1.0 + th) + 0.5 * x * (1.0 - th * th) * c * (1.0 + 3.0 * 0.044715 * x * x)


def _softplus_neg(lam):
    x = -lam
    e = jnp.exp(-jnp.abs(x))
    w = 1.0 + e
    l1p = jnp.where(w == 1.0, e, jnp.log(w) * e / (w - 1.0))
    return jnp.maximum(x, 0.0) + l1p


def _one_minus_exp(x, exp_x):
    poly = x * (1.0 + x * (1 / 2 + x * (1 / 6 + x * (1 / 24 + x * (1 / 120 + x * (1 / 720))))))
    return jnp.where(x > -0.125, -poly, 1.0 - exp_x)


def _conv_taps(pad_ref, t0, w, sign):
    out = None
    for j in range(4):
        term = w[j:j + 1, :] * pad_ref[pl.ds(PAD + t0 + sign * (j - 2), REC_CHUNK), :]
        out = term if out is None else out + term
    return out


def _gates(u, wa, wi, ba, bi, sp):
    ub = u.astype(BF16)
    r = _sigmoid(_dot(ub, wa, NN) + ba)
    i = _sigmoid(_dot(ub, wi, NN) + bi)
    log_a = (-LRU_C * sp) * r
    a = jnp.exp(log_a)
    x = jnp.maximum(_one_minus_exp(2.0 * log_a, a * a), 0.0)
    positive = x > 0.0
    inv = lax.rsqrt(jnp.where(positive, x, 1.0))
    mult = jnp.where(positive, x * inv, 0.0)
    return r, i, a, mult, jnp.where(positive, inv, 0.0)


def _tile_scan(a, b, sub, reverse):
    for s in (1, 2, 4):
        if reverse:
            a_s, b_s, m = pltpu.roll(a, 8 - s, 0), pltpu.roll(b, 8 - s, 0), sub < 8 - s
        else:
            a_s, b_s, m = pltpu.roll(a, s, 0), pltpu.roll(b, s, 0), sub >= s
        b = jnp.where(m, a * b_s + b, b)
        a = jnp.where(m, a * a_s, a)
    return a, b


def _last_row(x, row):
    return jnp.broadcast_to(x[row:row + 1, :], x.shape)


def _rec_prologue(up_ref, cw_ref, cb_ref, wa_ref, wi_ref, ba_ref, bi_ref, lam_ref,
                  upad_ref, u_ref, a_refs, h_refs):
    cb = up_ref.shape[1]
    zeros = jnp.zeros((PAD, cb), F32)
    upad_ref[pl.ds(0, PAD), :] = zeros
    upad_ref[pl.ds(PAD + T, PAD), :] = zeros
    upad_ref[pl.ds(PAD, T), :] = up_ref[...].astype(F32)
    cw = cw_ref[...]
    sp = _softplus_neg(lam_ref[...])
    for c in range(N_CHUNK):
        t0 = c * REC_CHUNK
        u = cb_ref[...] + _conv_taps(upad_ref, t0, cw, 1)
        u_ref[pl.ds(t0, REC_CHUNK), :] = u
        for d in range(2):
            _, i, a, mult, _ = _gates(u, wa_ref[d], wi_ref[d], ba_ref[d:d + 1, :], bi_ref[d:d + 1, :], sp[d:d + 1, :])
            a_refs[d][pl.ds(t0, REC_CHUNK), :] = a
            h_refs[d][pl.ds(t0, REC_CHUNK), :] = mult * (i * u)

    sub = lax.broadcasted_iota(jnp.int32, (8, cb), 0)

    def tile(k, carry):
        cf, cr = carry
        tf = pl.multiple_of(k * 8, 8)
        tr = pl.multiple_of((N_TILE - 1 - k) * 8, 8)
        af, bf = _tile_scan(a_refs[0][pl.ds(tf, 8), :], h_refs[0][pl.ds(tf, 8), :], sub, False)
        hf = af * cf + bf
        h_refs[0][pl.ds(tf, 8), :] = hf
        ar, br = _tile_scan(a_refs[1][pl.ds(tr, 8), :], h_refs[1][pl.ds(tr, 8), :], sub, True)
        hr = ar * cr + br
        h_refs[1][pl.ds(tr, 8), :] = hr
        return _last_row(af, 7) * cf + _last_row(bf, 7), _last_row(ar, 0) * cr + _last_row(br, 0)

    z8 = jnp.zeros((8, cb), F32)
    lax.fori_loop(0, N_TILE, tile, (z8, z8))
    return sp


def _rec_specs():
    up = pl.BlockSpec((T, REC_CB), lambda c: (0, _U_BLK + c))
    yb = pl.BlockSpec((T, REC_CB), lambda c: (0, _Y_BLK + c))
    cw = pl.BlockSpec((4, REC_CB), lambda c: (0, c))
    cbias = pl.BlockSpec((1, REC_CB), lambda c: (0, c))
    wbd = pl.BlockSpec((2, None, REC_CB, REC_CB), lambda c: (0, c, 0, 0))
    vec2 = pl.BlockSpec((2, REC_CB), lambda c: (0, c))
    col = pl.BlockSpec((T, REC_CB), lambda c: (0, c))
    return up, yb, cw, cbias, wbd, vec2, col


def _rec_fwd(z, conv_w, conv_b, wa, wi, ba, bi, lam):
    up, yb, cw, cbias, wbd, vec2, col = _rec_specs()

    def body(up_ref, yb_ref, cw_ref, cb_ref, wa_ref, wi_ref, ba_ref, bi_ref, lam_ref, g_ref,
             u_ref, af_ref, ar_ref, hf_ref, hr_ref, upad_ref):
        _rec_prologue(up_ref, cw_ref, cb_ref, wa_ref, wi_ref, ba_ref, bi_ref, lam_ref,
                      upad_ref, u_ref, (af_ref, ar_ref), (hf_ref, hr_ref))

        def chunk(c, carry):
            t0 = pl.multiple_of(c * REC_CHUNK, REC_CHUNK)
            rows = pl.ds(t0, REC_CHUNK)
            g_ref[rows, :] = ((hf_ref[rows, :] + hr_ref[rows, :]) * _gelu(yb_ref[rows, :].astype(F32))).astype(BF16)
            return carry

        lax.fori_loop(0, N_CHUNK, chunk, 0)

    res = pl.pallas_call(
        body, name="rec_fwd", grid=(N_CB,),
        in_specs=[up, yb, cw, cbias, wbd, wbd, vec2, vec2, vec2], out_specs=[col] * 6,
        out_shape=[_sds((T, D), BF16)] + [_sds((T, D), F32)] * 5,
        scratch_shapes=[pltpu.VMEM((T + 2 * PAD, REC_CB), F32)],
        compiler_params=_params(("parallel",)))(z, z, conv_w, conv_b, wa, wi, ba, bi, lam)
    return res[0], tuple(res[1:])


def _rec_bwd(z, dg, saved, dz, conv_w, conv_b, wa, wi, ba, bi, lam, after=()):
    up, yb, cw, cbias, wbd, vec2, col = _rec_specs()

    def body(up_ref, yb_ref, dg_ref, u_ref, af_ref, ar_ref, hf_ref, hr_ref,
             cw_ref, cb_ref, wa_ref, wi_ref, ba_ref, bi_ref, lam_ref, dz_in_ref, *rest):
        (dz_ref, dcw_ref, dcb_ref, dwa_out, dwi_out, dba_ref, dbi_ref, dlam_ref,
         upad_ref, dh_ref, gf_ref, gr_ref, daf_ref, dar_ref, dupad_ref, dwa_ref, dwi_ref,
         dup_ref, dyb_ref, sems) = rest[len(after):]
        g_refs, da_refs = (gf_ref, gr_ref), (daf_ref, dar_ref)
        cb = up_ref.shape[1]
        zeros = jnp.zeros((PAD, cb), F32)
        upad_ref[pl.ds(0, PAD), :] = zeros
        upad_ref[pl.ds(PAD + T, PAD), :] = zeros
        upad_ref[pl.ds(PAD, T), :] = up_ref[...].astype(F32)
        sp = _softplus_neg(lam_ref[...])

        def gate_chunk(c, carry):
            t0 = pl.multiple_of(c * REC_CHUNK, REC_CHUNK)
            rows = pl.ds(t0, REC_CHUNK)
            y = yb_ref[rows, :].astype(F32)
            dgv = dg_ref[rows, :].astype(F32)
            dh_ref[rows, :] = dgv * _gelu(y)
            dyb_ref[rows, :] = (dgv * (hf_ref[rows, :] + hr_ref[rows, :]) * _gelu_grad(y)).astype(BF16)
            return carry

        lax.fori_loop(0, N_CHUNK, gate_chunk, 0)
        cols = pl.multiple_of(pl.program_id(0) * REC_CB, REC_CB)
        dyb_copy = pltpu.make_async_copy(dyb_ref, dz_ref.at[:, pl.ds(DZ_Y + cols, REC_CB)], sems.at[1])
        dyb_copy.start()

        sub = lax.broadcasted_iota(jnp.int32, (8, cb), 0)

        def tile(k, carry):
            cf, cr = carry
            kf = N_TILE - 1 - k
            tf = pl.multiple_of(kf * 8, 8)
            tnext = pl.multiple_of(jnp.minimum(kf + 1, N_TILE - 1) * 8, 8)
            tprev = pl.multiple_of(jnp.maximum(kf - 1, 0) * 8, 8)
            a_t = af_ref[pl.ds(tf, 8), :]
            a_n = jnp.where(kf < N_TILE - 1, af_ref[pl.ds(tnext, 8), :], 0.0)
            a_sh = jnp.where(sub == 7, pltpu.roll(a_n, 7, 0), pltpu.roll(a_t, 7, 0))
            ca, cbb = _tile_scan(a_sh, dh_ref[pl.ds(tf, 8), :], sub, True)
            gf = ca * cf + cbb
            h_t = hf_ref[pl.ds(tf, 8), :]
            h_p = jnp.where(kf > 0, hf_ref[pl.ds(tprev, 8), :], 0.0)
            h_sh = jnp.where(sub == 0, pltpu.roll(h_p, 1, 0), pltpu.roll(h_t, 1, 0))
            gf_ref[pl.ds(tf, 8), :] = gf
            daf_ref[pl.ds(tf, 8), :] = gf * h_sh
            tr = pl.multiple_of(k * 8, 8)
            rnext = pl.multiple_of(jnp.minimum(k + 1, N_TILE - 1) * 8, 8)
            rprev = pl.multiple_of(jnp.maximum(k - 1, 0) * 8, 8)
            b_t = ar_ref[pl.ds(tr, 8), :]
            b_p = jnp.where(k > 0, ar_ref[pl.ds(rprev, 8), :], 0.0)
            b_sh = jnp.where(sub == 0, pltpu.roll(b_p, 1, 0), pltpu.roll(b_t, 1, 0))
            ra, rb = _tile_scan(b_sh, dh_ref[pl.ds(tr, 8), :], sub, False)
            gr = ra * cr + rb
            hr_t = hr_ref[pl.ds(tr, 8), :]
            hr_n = jnp.where(k < N_TILE - 1, hr_ref[pl.ds(rnext, 8), :], 0.0)
            hr_sh = jnp.where(sub == 7, pltpu.roll(hr_n, 7, 0), pltpu.roll(hr_t, 7, 0))
            gr_ref[pl.ds(tr, 8), :] = gr
            dar_ref[pl.ds(tr, 8), :] = gr * hr_sh
            return _last_row(gf, 0), _last_row(gr, 7)

        z8 = jnp.zeros((8, cb), F32)
        lax.fori_loop(0, N_TILE, tile, (z8, z8))

        dupad_ref[pl.ds(0, PAD), :] = zeros
        dupad_ref[pl.ds(PAD + T, PAD), :] = zeros
        dwa_ref[...] = jnp.zeros_like(dwa_ref)
        dwi_ref[...] = jnp.zeros_like(dwi_ref)
        dba_ref[...] = jnp.zeros_like(dba_ref)
        dbi_ref[...] = jnp.zeros_like(dbi_ref)
        dlam_ref[...] = jnp.zeros_like(dlam_ref)

        def grad_chunk(c, carry):
            t0 = pl.multiple_of(c * REC_CHUNK, REC_CHUNK)
            rows = pl.ds(t0, REC_CHUNK)
            u = u_ref[rows, :]
            ub = u.astype(BF16)
            du = jnp.zeros((REC_CHUNK, cb), F32)
            for d in range(2):
                r, i, a, mult, inv_mult = _gates(u, wa_ref[d], wi_ref[d], ba_ref[d:d + 1, :], bi_ref[d:d + 1, :],
                                                 sp[d:d + 1, :])
                dbx = g_refs[d][rows, :]
                dmult = dbx * (i * u)
                diu = dbx * mult
                a2 = a * a
                dlog = da_refs[d][rows, :] * a - dmult * (a2 * inv_mult)
                dpa = (dlog * (-LRU_C) * sp[d:d + 1, :]) * r * (1.0 - r)
                dpi = (diu * u) * i * (1.0 - i)
                dpab, dpib = dpa.astype(BF16), dpi.astype(BF16)
                du = du + diu * i + _dot(dpab, wa_ref[d], NT) + _dot(dpib, wi_ref[d], NT)
                dwa_ref[d] += _dot(ub, dpab, TN)
                dwi_ref[d] += _dot(ub, dpib, TN)
                dba_ref[d:d + 1, :] += jnp.sum(dpa, axis=0, keepdims=True)
                dbi_ref[d:d + 1, :] += jnp.sum(dpi, axis=0, keepdims=True)
                dlam_ref[d:d + 1, :] += jnp.sum(dlog * r, axis=0, keepdims=True)
            dupad_ref[pl.ds(PAD + t0, REC_CHUNK), :] = du
            return carry

        lax.fori_loop(0, N_CHUNK, grad_chunk, 0)
        dlam_ref[...] = dlam_ref[...] * (LRU_C * _sigmoid(-lam_ref[...]))
        for d in range(2):
            for blk in range(REC_CB // REC_BLOCK):
                lo, hi = blk * REC_BLOCK, (blk + 1) * REC_BLOCK
                dwa_out[d, blk] = dwa_ref[d, lo:hi, lo:hi]
                dwi_out[d, blk] = dwi_ref[d, lo:hi, lo:hi]

        cw = cw_ref[...]
        dcb = jnp.zeros((1, cb), F32)
        dcw = [jnp.zeros((1, cb), F32) for _ in range(4)]
        for c in range(N_CHUNK):
            t0 = c * REC_CHUNK
            du = dupad_ref[pl.ds(PAD + t0, REC_CHUNK), :]
            dcb = dcb + jnp.sum(du, axis=0, keepdims=True)
            for j in range(4):
                dcw[j] = dcw[j] + jnp.sum(du * upad_ref[pl.ds(PAD + t0 + j - 2, REC_CHUNK), :], axis=0, keepdims=True)
            dup_ref[pl.ds(t0, REC_CHUNK), :] = _conv_taps(dupad_ref, t0, cw, -1).astype(BF16)
        dcb_ref[...] = dcb
        dcw_ref[...] = jnp.concatenate(dcw, axis=0)
        dup_copy = pltpu.make_async_copy(dup_ref, dz_ref.at[:, pl.ds(DZ_U + cols, REC_CB)], sems.at[0])
        dup_copy.start()
        dup_copy.wait()
        dyb_copy.wait()

    full = pltpu.VMEM((T, REC_CB), F32)
    padded = pltpu.VMEM((T + 2 * PAD, REC_CB), F32)
    per = REC_CB // REC_BLOCK
    diag = pl.BlockSpec((2, per, REC_BLOCK, REC_BLOCK), lambda c: (0, c, 0, 0))
    return pl.pallas_call(
        body, name="rec_bwd", grid=(N_CB,),
        in_specs=[up, yb] + [col] * 6 + [cw, cbias, wbd, wbd, vec2, vec2, vec2, _ANY] + [_ANY] * len(after),
        out_specs=[_ANY, cw, cbias, diag, diag, vec2, vec2, vec2],
        out_shape=[_sds(dz.shape, dz.dtype), _sds((4, D), F32), _sds((1, D), F32),
                   _sds((2, D // REC_BLOCK, REC_BLOCK, REC_BLOCK), F32), _sds((2, D // REC_BLOCK, REC_BLOCK, REC_BLOCK), F32),
                   _sds((2, D), F32), _sds((2, D), F32), _sds((2, D), F32)],
        input_output_aliases={15: 0},
        scratch_shapes=[padded, full, full, full, full, full, padded,
                        pltpu.VMEM((2, REC_CB, REC_CB), F32), pltpu.VMEM((2, REC_CB, REC_CB), F32),
                        pltpu.VMEM((T, REC_CB), BF16), pltpu.VMEM((T, REC_CB), BF16), pltpu.SemaphoreType.DMA((2,))],
        compiler_params=_params(("parallel",)))(z, z, dg, *saved, conv_w, conv_b, wa, wi, ba, bi, lam, dz, *after)


class _NoReducer:
    def begin(self, tag, grads):
        return ()

    def advance(self, tag, after):
        return ()

    def interlude(self, tokens):
        return None


def _local_step(x, target, p, late=None, reducer=_NoReducer()):
    x = x.reshape(T, D)
    target = target.reshape(T, D)
    tb = _bias_pairs(p["rpb"])
    wa, wi = _block_diag(p["w_rg_a"]), _block_diag(p["w_rg_i"])

    h1 = _rms_fwd("rms1_fwd", x, p["ln1_g"], after=late[0] if late else ())
    if late:
        p = {**p, **late[1]((h1, tb, wa, wi))}
    rec_params = (p["conv_w"], p["conv_b"], wa, wi, p["b_rg_a"], p["b_rg_i"], p["lru_lambda"])
    (z,) = _mm_nn_cols("mm_z", h1, p["w_in"], BF16, bias=p["b_in"])
    att, probs = _attn_fwd(z, tb)
    g, rec_saved = _rec_fwd(z, *rec_params)
    if late:
        p = {**p, **late[2](g)}
    y_att, y_rec, mixed, x1, h2 = _branches_fwd(att, g, z, x, p["w_att_o"], p["w_rec_o"], p["w_out"], p["ln2_g"])

    def relu2(r, ex, outs):
        rp = jnp.maximum(r, 0.0)
        outs[0][...] = (rp * rp).astype(BF16)

    (s,) = _mm_nn_cols("mm_ff1", h2, p["w_ff1"], BF16, epilogue=relu2)
    loss, dx2, dx2_b, g_lnf = _mm_x2_loss_head(s, p["w_ff2"], x1, target, p["lnf_g"])

    def relu2_bwd(r, ex, outs):
        outs[0][...] = (r * 2.0 * jnp.sqrt(ex[0][...].astype(F32))).astype(BF16)

    (df,) = _mm_nt_rows("mm_df", dx2_b, p["w_ff2"], BF16, tn=D, extras=[s],
                        extra_specs=[pl.BlockSpec((TJ, D), lambda j, i, k: (i, j))], epilogue=relu2_bwd)
    (g_w_ff2,) = _mm_tn_rows("mm_g_ff2", s, dx2_b, tm=D)
    (g_w_ff1,) = _mm_tn_cols("mm_g_ff1", h2, df, D)
    tok = reducer.begin("ff", dict(w_ff2=g_w_ff2, w_ff1=g_w_ff1))
    dx1, dx1_b, g_ln2 = _mm_nt_cols_rms_bwd("mm_dh2_rms2_bwd", df, p["w_ff1"], x1, p["ln2_g"], dx2, after=tok,
                                            bf16_copy=True)

    dy_att, dy_rec, d_att, d_g, dz = _branches_bwd(dx1_b, y_att, y_rec, z, p["w_att_o"], p["w_rec_o"], p["w_out"])
    (g_w_out,) = _mm_tn_rows("mm_g_out", mixed, dx1_b, tm=D)
    (g_w_att_o,) = _mm_tn_cols("mm_g_att_o", att, dy_att, D // N_CHIPS)
    (g_w_rec_o,) = _mm_tn_rows("mm_g_rec_o", g, dy_rec, tm=D)
    tok = reducer.advance("ff", g_w_rec_o) + reducer.begin("proj", dict(w_out=g_w_out, w_att_o=g_w_att_o, w_rec_o=g_w_rec_o))

    dz, ds_acc = _attn_bwd(z, probs, d_att, dz, after=tok)
    g_rpb = _rpb_grad(ds_acc)
    tok = reducer.advance("proj", ds_acc)
    dz, g_conv_w, g_conv_b, g_wa, g_wi, g_ba, g_bi, g_lam = _rec_bwd(z, d_g, rec_saved, dz, *rec_params, after=tok)

    g_w_in, g_b_in = _mm_tn_cols("mm_g_in", h1, dz, D_IN // N_CHIPS, colsum=True)
    tok = reducer.advance("in", reducer.interlude(reducer.begin("in", dict(w_in=g_w_in))))
    grad_x, g_ln1 = _mm_nt_cols_rms_bwd("mm_dh1_rms1_bwd", dz, p["w_in"], x, p["ln1_g"], dx1, after=tok)

    grads = dict(ln1_g=g_ln1, w_in=g_w_in, b_in=g_b_in, rpb=g_rpb, w_att_o=g_w_att_o, conv_w=g_conv_w,
                 conv_b=g_conv_b, w_rg_a=g_wa, b_rg_a=g_ba, w_rg_i=g_wi,
                 b_rg_i=g_bi, lru_lambda=g_lam, w_rec_o=g_w_rec_o, w_out=g_w_out, ln2_g=g_ln2,
                 w_ff1=g_w_ff1, w_ff2=g_w_ff2, lnf_g=g_lnf)
    return loss, grad_x.reshape(1, T, D), grads


_ANY = pl.BlockSpec(memory_space=pl.ANY)
N_PEERS = N_CHIPS - 1


def _place():
    x, y, c = lax.axis_index("x"), lax.axis_index("y"), lax.axis_index("c")
    peers = [(1 - x, y), (x, 1 - y), (1 - x, 1 - y)]
    return x, y, c, 2 * x + y, peers


def _remote(src, dst, send_sem, recv_sem, dev):
    return pltpu.make_async_remote_copy(src_ref=src, dst_ref=dst, send_sem=send_sem, recv_sem=recv_sem,
                                        device_id=dev, device_id_type=MESH)


def _prefetch_call(body, name, ids, grid, in_specs, out_specs, out_shape, args, semantics=None):
    spec = pltpu.PrefetchScalarGridSpec(num_scalar_prefetch=1, grid=grid, in_specs=in_specs, out_specs=out_specs)
    return pl.pallas_call(body, name=name, grid_spec=spec, out_shape=out_shape,
                          compiler_params=_params(semantics or ("parallel",) * len(grid)))(ids, *args)


def _cast_bf16(name, w, chip_id, after=()):
    rows, cols = w.shape
    rb = min(rows, 256)

    def body(ids_ref, w_ref, *rest):
        rest[-1][...] = w_ref[...].astype(BF16)

    return _prefetch_call(body, name, chip_id, (rows // rb,),
                          [pl.BlockSpec((rb, cols), lambda i, ids: (i, 0))] + [_ANY] * len(after),
                          pl.BlockSpec((None, rb, cols), lambda i, ids: (ids[0], i, 0)),
                          _sds((N_CHIPS, rows, cols), BF16), (w, *after))


def _dma_sems(*counts):
    return [pltpu.SemaphoreType.DMA((k,)) for k in counts]


_HBM = pl.BlockSpec(memory_space=pltpu.HBM)
_SEM = pl.BlockSpec(memory_space=pltpu.SEMAPHORE)
_SPLIT_COPY = pltpu.CompilerParams(has_side_effects=pltpu.SideEffectType.DATAFLOW_SIDE_EFFECTING)
SIBLING_ID = 0
_SPLIT_COPY_SIBLING = pltpu.CompilerParams(has_side_effects=pltpu.SideEffectType.DATAFLOW_SIDE_EFFECTING,
                                           collective_id=SIBLING_ID)


def _sibling_handshake():
    x, y, c = lax.axis_index("x"), lax.axis_index("y"), lax.axis_index("c")
    barrier = pltpu.get_barrier_semaphore()
    pl.semaphore_signal(barrier, inc=1, device_id=(x, y, 1 - c), device_id_type=MESH)
    pl.semaphore_wait(barrier, 1)


def _hbm(arrays):
    return [pltpu.with_memory_space_constraint(a, pltpu.HBM) for a in arrays]


def _hbm_like(arrays):
    return [pltpu.HBM(a.shape, a.dtype) for a in arrays]


def _halves(buf, c):
    half = buf.shape[1] // 2
    return pl.ds(c * half, half), pl.ds((1 - c) * half, half)


def _gather_start(name, slots):
    n = len(slots)
    nk = n * N_PEERS

    def body(*refs):
        bufs = refs[n:2 * n]
        send_sems, recv_sems, token = refs[2 * n:]
        x, y, c, chip, peers = _place()
        for t in range(n):
            mine, _ = _halves(bufs[t], c)
            for r, (px, py) in enumerate(peers):
                k = t * N_PEERS + r
                own = bufs[t].at[chip, mine]
                _remote(own, own, send_sems.at[k], recv_sems.at[k], (px, py, c)).start()
        token[...] = jnp.zeros_like(token)

    res = pl.pallas_call(
        body, name=name, in_specs=[_HBM] * n, out_specs=[_HBM] * n + [_SEM, _SEM, pl.BlockSpec(memory_space=pltpu.VMEM)],
        out_shape=_hbm_like(slots) + [pltpu.SemaphoreType.DMA((nk,)), pltpu.SemaphoreType.DMA((nk,)),
                                      _sds((8, 128), F32)],
        input_output_aliases={t: t for t in range(n)}, compiler_params=_SPLIT_COPY)(*_hbm(slots))
    return res[:n], (res[n], res[n + 1]), res[n + 2]


def _gather_wait(name, bufs, sems, after):
    n = len(bufs)
    after = tuple(after) if isinstance(after, (tuple, list)) else (after,)

    def body(*refs):
        ins = refs[:n]
        send_sems, recv_sems = refs[n], refs[n + 1]
        x, y, c, chip, peers = _place()
        for t in range(n):
            mine, _ = _halves(ins[t], c)
            for r, (px, py) in enumerate(peers):
                k = t * N_PEERS + r
                cp = _remote(ins[t].at[chip, mine], ins[t].at[2 * px + py, mine], send_sems.at[k], recv_sems.at[k],
                             (px, py, c))
                cp.wait_send()
                cp.wait_recv()

    return pl.pallas_call(
        body, name=name, in_specs=[_HBM] * n + [_SEM, _SEM] + [_ANY] * len(after), out_specs=[_HBM] * n,
        out_shape=_hbm_like(bufs), input_output_aliases={t: t for t in range(n)},
        compiler_params=_SPLIT_COPY)(*bufs, *sems, *after)


def _gather_forward(name, bufs):
    n = len(bufs)
    nk = n * N_PEERS

    def body(*refs):
        _sibling_handshake()
        outs = refs[n:2 * n]
        send_sems, recv_sems = refs[2 * n:]
        x, y, c, chip, peers = _place()
        sibling = (x, y, 1 - c)
        sends = []
        for t in range(n):
            mine, _ = _halves(outs[t], c)
            for r, (px, py) in enumerate(peers):
                k = t * N_PEERS + r
                landed = outs[t].at[2 * px + py, mine]
                sends.append(_remote(landed, landed, send_sems.at[k], recv_sems.at[k], sibling))
                sends[-1].start()
        for t in range(n):
            _, theirs = _halves(outs[t], c)
            for r, (px, py) in enumerate(peers):
                k = t * N_PEERS + r
                landed = outs[t].at[2 * px + py, theirs]
                _remote(landed, landed, send_sems.at[k], recv_sems.at[k], sibling).wait_recv()
        for cp in sends:
            cp.wait_send()

    return pl.pallas_call(
        body, name=name, in_specs=[_ANY] * n, out_specs=[_ANY] * n, out_shape=[_sds(b.shape, b.dtype) for b in bufs],
        input_output_aliases={t: t for t in range(n)}, scratch_shapes=_dma_sems(nk, nk),
        compiler_params=pltpu.CompilerParams(collective_id=SIBLING_ID))(*bufs)


def _pair_copies(n, srcs, lands, send_sems, recv_sems):
    x, y, c, _, _ = _place()
    sibling = (x, y, 1 - c)
    copies = []
    for t in range(n):
        half = srcs[t].shape[1] // 2
        for j in range(N_CHIPS):
            k = t * N_CHIPS + j
            copies.append(_remote(srcs[t].at[j, pl.ds((1 - c) * half, half)], lands[t].at[j],
                                  send_sems.at[k], recv_sems.at[k], sibling))
    for t in range(n, len(srcs)):
        k = n * N_CHIPS + t - n
        copies.append(_remote(srcs[t], lands[t], send_sems.at[k], recv_sems.at[k], sibling))
    return copies


def _pair_start(name, grads, wholes=()):
    n = len(grads)
    srcs = list(grads) + list(wholes)
    m = len(srcs)
    lands = [pltpu.HBM((N_CHIPS, g.shape[1] // 2, g.shape[2]), F32) for g in grads] + _hbm_like(wholes)
    ns = n * N_CHIPS + len(wholes)

    def body(*refs):
        _sibling_handshake()
        src_refs, land_refs = refs[m:2 * m], refs[2 * m:3 * m]
        send_sems, recv_sems, token = refs[3 * m:]
        for cp in _pair_copies(n, src_refs, land_refs, send_sems, recv_sems):
            cp.start()
        token[...] = jnp.zeros_like(token)

    res = pl.pallas_call(
        body, name=name, in_specs=[_HBM] * m,
        out_specs=[_HBM] * (2 * m) + [_SEM, _SEM, pl.BlockSpec(memory_space=pltpu.VMEM)],
        out_shape=_hbm_like(srcs) + lands + [pltpu.SemaphoreType.DMA((ns,)), pltpu.SemaphoreType.DMA((ns,)),
                                             _sds((8, 128), F32)],
        input_output_aliases={t: t for t in range(m)}, compiler_params=_SPLIT_COPY_SIBLING)(*_hbm(srcs))
    return (res[:m], res[m:2 * m], (res[2 * m], res[2 * m + 1])), res[2 * m + 2]


def _pair_wait(name, flight, n, after):
    srcs, lands, sems = flight
    m = len(srcs)

    def body(*refs):
        for cp in _pair_copies(n, refs[:m], refs[m:2 * m], refs[2 * m], refs[2 * m + 1]):
            cp.wait_send()
            cp.wait_recv()

    res = pl.pallas_call(
        body, name=name, in_specs=[_HBM] * (2 * m) + [_SEM, _SEM, _ANY], out_specs=[_HBM] * (2 * m),
        out_shape=_hbm_like(srcs) + _hbm_like(lands), input_output_aliases={t: t for t in range(2 * m)},
        compiler_params=_SPLIT_COPY)(*srcs, *lands, *sems, after)
    return res[:m], res[m:]


def _chip_copies(srcs, lands, small_src, small_land, send_sems, recv_sems):
    x, y, c, chip, peers = _place()
    n = len(srcs)
    copies = []
    for r, (px, py) in enumerate(peers):
        for t in range(n):
            k = t * N_PEERS + r
            copies.append(_remote(srcs[t].at[2 * px + py], lands[t].at[r], send_sems.at[k], recv_sems.at[k], (px, py, c)))
        if small_src is not None:
            k = n * N_PEERS + r
            half_s = small_src.shape[0] // 2
            copies.append(_remote(small_src.at[pl.ds(c * half_s, half_s)], small_land.at[r],
                                  send_sems.at[k], recv_sems.at[k], (px, py, c)))
    return copies


def _chip_start(name, sums_bf16, small=None):
    n = len(sums_bf16)
    srcs = list(sums_bf16) + ([small] if small is not None else [])
    m = len(srcs)
    lands = [pltpu.HBM((N_PEERS,) + s.shape[1:], BF16) for s in sums_bf16]
    if small is not None:
        lands.append(pltpu.HBM((N_PEERS, small.shape[0] // 2, 128), F32))
    nk = m * N_PEERS

    def body(*refs):
        src_refs, land_refs = refs[m:2 * m], refs[2 * m:3 * m]
        send_sems, recv_sems, token = refs[3 * m:]
        small_src, small_land = (src_refs[n], land_refs[n]) if small is not None else (None, None)
        for cp in _chip_copies(src_refs[:n], land_refs[:n], small_src, small_land, send_sems, recv_sems):
            cp.start()
        token[...] = jnp.zeros_like(token)

    res = pl.pallas_call(
        body, name=name, in_specs=[_HBM] * m,
        out_specs=[_HBM] * (2 * m) + [_SEM, _SEM, pl.BlockSpec(memory_space=pltpu.VMEM)],
        out_shape=_hbm_like(srcs) + lands + [pltpu.SemaphoreType.DMA((nk,)), pltpu.SemaphoreType.DMA((nk,)),
                                             _sds((8, 128), F32)],
        input_output_aliases={t: t for t in range(m)}, compiler_params=_SPLIT_COPY)(*_hbm(srcs))
    return (res[:m], res[m:2 * m], (res[2 * m], res[2 * m + 1])), res[2 * m + 2]


def _chip_wait(name, flight, with_small, after):
    srcs, lands, sems = flight
    m = len(srcs)
    n = m - 1 if with_small else m

    def body(*refs):
        src_refs, land_refs = refs[:m], refs[m:2 * m]
        send_sems, recv_sems = refs[2 * m], refs[2 * m + 1]
        small_src, small_land = (src_refs[n], land_refs[n]) if with_small else (None, None)
        for cp in _chip_copies(src_refs[:n], land_refs[:n], small_src, small_land, send_sems, recv_sems):
            cp.wait_send()
            cp.wait_recv()

    res = pl.pallas_call(
        body, name=name, in_specs=[_HBM] * (2 * m) + [_SEM, _SEM, _ANY], out_specs=[_HBM] * (2 * m),
        out_shape=_hbm_like(srcs) + _hbm_like(lands), input_output_aliases={t: t for t in range(2 * m)},
        compiler_params=_SPLIT_COPY)(*srcs, *lands, *sems, after)
    return res[:m], res[m:]


def _swap_start(name, bufs):
    n = len(bufs)

    def body(*refs):
        _sibling_handshake()
        outs = refs[n:2 * n]
        send_sems, recv_sems, token = refs[2 * n:]
        x, y, c, _, _ = _place()
        for t in range(n):
            h = outs[t].shape[0] // 2
            mine = outs[t].at[pl.ds(c * h, h)]
            _remote(mine, mine, send_sems.at[t], recv_sems.at[t], (x, y, 1 - c)).start()
        token[...] = jnp.zeros_like(token)

    res = pl.pallas_call(
        body, name=name, in_specs=[_HBM] * n, out_specs=[_HBM] * n + [_SEM, _SEM, pl.BlockSpec(memory_space=pltpu.VMEM)],
        out_shape=_hbm_like(bufs) + [pltpu.SemaphoreType.DMA((n,)), pltpu.SemaphoreType.DMA((n,)), _sds((8, 128), F32)],
        input_output_aliases={t: t for t in range(n)}, compiler_params=_SPLIT_COPY_SIBLING)(*_hbm(bufs))
    return (res[:n], (res[n], res[n + 1])), res[n + 2]


def _swap_wait(name, flight, after):
    bufs, sems = flight
    n = len(bufs)

    def body(*refs):
        ins = refs[:n]
        send_sems, recv_sems = refs[n], refs[n + 1]
        x, y, c, _, _ = _place()
        for t in range(n):
            h = ins[t].shape[0] // 2
            cp = _remote(ins[t].at[pl.ds(c * h, h)], ins[t].at[pl.ds((1 - c) * h, h)], send_sems.at[t],
                         recv_sems.at[t], (x, y, 1 - c))
            cp.wait_send()
            cp.wait_recv()

    return pl.pallas_call(
        body, name=name, in_specs=[_HBM] * n + [_SEM, _SEM, _ANY], out_specs=[_HBM] * n, out_shape=_hbm_like(bufs),
        input_output_aliases={t: t for t in range(n)}, compiler_params=_SPLIT_COPY)(*bufs, *sems, after)


def _pair_sum(name, grad, got, ids):
    _, rows, cols = got.shape
    rb = min(rows, 256)
    nb = rows // rb
    blk = pl.BlockSpec((None, rb, cols), lambda i, j, ids: (j, i, 0))
    mine = pl.BlockSpec((None, rb, cols), lambda i, j, ids: (j, ids[1] * nb + i, 0))
    own = pl.BlockSpec((rb, cols), lambda i, j, ids: (i, 0))

    def body(ids_ref, a_ref, b_ref, s_ref, sb_ref):
        s = a_ref[...] + b_ref[...]
        sb_ref[...] = s.astype(BF16)

        @pl.when(pl.program_id(1) == ids_ref[0])
        def _():
            s_ref[...] = s

    return _prefetch_call(body, name, ids, (nb, N_CHIPS), [mine, blk], [own, blk],
                          [_sds((rows, cols), F32), _sds(got.shape, BF16)], (grad, got),
                          semantics=("parallel", "arbitrary"))


def _chip_sum(name, own_sum, got, ids):
    rows, cols = own_sum.shape
    rb = min(rows, 256)
    nb = rows // rb
    own = pl.BlockSpec((rb, cols), lambda i, ids: (i, 0))
    blk3 = pl.BlockSpec((N_PEERS, rb, cols), lambda i, ids: (0, i, 0))
    out = pl.BlockSpec((rb, cols), lambda i, ids: (ids[1] * nb + i, 0))

    def body(ids_ref, a_ref, b_ref, o_ref):
        o_ref[...] = ((a_ref[...] + b_ref[0].astype(F32)) + b_ref[1].astype(F32)) + b_ref[2].astype(F32)

    return _prefetch_call(body, name, ids, (nb,), [own, blk3], out, _sds((2 * rows, cols), F32), (own_sum, got))


SMALL_RB = 280


def _small_pair_sum(own, got):
    blk = pl.BlockSpec((SMALL_RB, 128), lambda i: (i, 0))

    def body(a_ref, b_ref, o_ref):
        o_ref[...] = a_ref[...] + b_ref[...]

    return pl.pallas_call(body, name="small_pair_sum", grid=(own.shape[0] // SMALL_RB,), in_specs=[blk, blk],
                          out_specs=blk, out_shape=_sds(own.shape, F32),
                          compiler_params=_params(("parallel",)))(own, got)


def _small_chip_sum(pair, got, ids):
    nb = pair.shape[0] // 2 // SMALL_RB
    half = pl.BlockSpec((SMALL_RB, 128), lambda i, ids: (ids[1] * nb + i, 0))
    blk3 = pl.BlockSpec((N_PEERS, SMALL_RB, 128), lambda i, ids: (0, i, 0))

    def body(ids_ref, a_ref, b_ref, o_ref):
        o_ref[...] = (a_ref[...] + b_ref[1]) + (b_ref[0] + b_ref[2])

    return _prefetch_call(body, "small_chip_sum", ids, (nb,), [half, blk3], half, _sds(pair.shape, F32), (pair, got))


def _adamw_math(w, g, m, v):
    m = ADAM_B1 * m + (1.0 - ADAM_B1) * g
    v = ADAM_B2 * v + (1.0 - ADAM_B2) * (g * g)
    m_hat = m / (1.0 - ADAM_B1 ** ADAM_STEP)
    v_hat = v / (1.0 - ADAM_B2 ** ADAM_STEP)
    delta = -ADAM_LR * (m_hat / (jnp.sqrt(v_hat) + ADAM_EPS) + ADAM_WD * w)
    return delta, m, v


def _adamw(name, w, g, m, v, rb=None):
    rows, cols = w.shape
    rb = rows if rb is None else rb
    blk = pl.BlockSpec((rb, cols), lambda i: (i, 0))

    def body(w_ref, g_ref, m_ref, v_ref, d_ref, nm_ref, nv_ref):
        d, nm, nv = _adamw_math(w_ref[...], g_ref[...], m_ref[...], v_ref[...])
        d_ref[...] = d
        nm_ref[...] = nm
        nv_ref[...] = nv

    return pl.pallas_call(body, name=name, grid=(rows // rb,), in_specs=[blk] * 4, out_specs=[blk] * 3,
                          out_shape=[_sds(w.shape, F32)] * 3, compiler_params=_params(("parallel",)))(w, g, m, v)


def _adamw_small(ws, gs, ms, vs):
    n = len(ws)

    def body(*refs):
        for t in range(n):
            w_ref, g_ref, m_ref, v_ref = (refs[k * n + t] for k in range(4))
            d, nm, nv = _adamw_math(w_ref[...], g_ref[...], m_ref[...], v_ref[...])
            for k, val in enumerate((d, nm, nv)):
                refs[(4 + k) * n + t][...] = val

    res = pl.pallas_call(body, name="adamw_small", out_shape=[_sds(a.shape, F32) for a in ws] * 3,
                         compiler_params=_params())(*ws, *gs, *ms, *vs)
    return [(res[t], res[n + t], res[2 * n + t]) for t in range(n)]


BIG = ("w_in", "w_att_o", "w_rec_o", "w_out", "w_ff1", "w_ff2")
SHARDED_VECS = ("conv_w", "b_rg_a", "b_rg_i", "lru_lambda")
SMALL = ("ln1_g", "b_in", "rpb", "conv_w", "conv_b", "w_rg_a", "b_rg_a", "w_rg_i", "b_rg_i", "lru_lambda",
         "ln2_g", "lnf_g")
SMALL_ROWS = 2240
ORDER = ("ln1_g", "w_in", "b_in", "rpb", "w_att_o", "conv_w", "conv_b", "w_rg_a", "b_rg_a", "w_rg_i", "b_rg_i",
         "lru_lambda", "w_rec_o", "w_out", "ln2_g", "w_ff1", "w_ff2", "lnf_g")


def _pack_small(grads, loss):
    parts, sizes = [], {}
    for n in SMALL:
        flat = grads[n].reshape(-1)
        pad = (-flat.shape[0]) % 128
        sizes[n] = (flat.shape[0], flat.shape[0] + pad)
        parts.append(jnp.pad(flat, (0, pad)))
    total = sum(s[1] for s in sizes.values())
    parts.append(jnp.pad(loss.reshape(1), (0, SMALL_ROWS * 128 - total - 1)))
    return jnp.concatenate(parts).reshape(SMALL_ROWS, 128), sizes


def _unpack_small(buf, sizes, shapes):
    flat = buf.reshape(-1)
    out, pos = {}, 0
    for n in SMALL:
        size, padded = sizes[n]
        out[n] = flat[pos:pos + size].reshape(shapes[n])
        pos += padded
    return out, flat[pos]


def _gather_weights(w, chip):
    chip_id = chip.astype(jnp.int32).reshape(1)
    vec_rows = [w[n][0] for n in SHARDED_VECS]
    vec_shard = jnp.concatenate(vec_rows + [jnp.zeros((16 - 10, D // N_CHIPS), F32)], axis=0)
    vec_slots = lax.dynamic_update_slice(jnp.zeros((N_CHIPS, 16, D // N_CHIPS), F32), vec_shard[None], (chip, 0, 0))
    bufs_a, sems_a, token_a = _gather_start("gather_start_first", [_cast_bf16("cast_w_in", w["w_in"][0], chip_id), vec_slots])
    rest_names = BIG[1:]
    bufs_b, sems_b, token_b = _gather_start(
        "gather_start_rest", [_cast_bf16("cast_" + n, w[n][0], chip_id, after=(token_a,)) for n in rest_names])

    def first(after):
        w_in_full, vec_full = _gather_forward("gather_forward_first", _gather_wait("gather_wait_first", bufs_a, sems_a, after))
        vecs = vec_full.transpose(1, 0, 2).reshape(16, D)
        return dict(w_in=w_in_full, conv_w=vecs[0:4], b_rg_a=vecs[4:6], b_rg_i=vecs[6:8], lru_lambda=vecs[8:10])

    def rest(after):
        full = dict(zip(rest_names, _gather_forward("gather_forward_rest",
                                                    _gather_wait("gather_wait_rest", bufs_b, sems_b, after))))
        return dict(w_att_o=full["w_att_o"], w_ff1=full["w_ff1"], w_rec_o=full["w_rec_o"].reshape(D, D),
                    w_out=full["w_out"].reshape(D, D), w_ff2=full["w_ff2"].reshape(D_FF, D))

    p = dict(ln1_g=w["ln1_g"], b_in=w["b_in"], rpb=w["rpb"][0], conv_b=w["conv_b"], w_rg_a=w["w_rg_a"][0],
             w_rg_i=w["w_rg_i"][0], ln2_g=w["ln2_g"], lnf_g=w["lnf_g"].reshape(1, D))
    return p, ((token_b,), first, rest)


class _Reducer:
    def __init__(self, ids):
        self.ids = ids
        self.groups = {}

    def begin(self, tag, grads, small=None):
        names = list(grads)
        big = [grads[n].reshape(N_CHIPS, -1, grads[n].shape[-1]) for n in names]
        flight, token = _pair_start("pair_start_" + tag, big, [] if small is None else [small])
        self.groups[tag] = dict(names=names, pair=flight, small=small is not None)
        return (token,)

    def advance(self, tag, after):
        grp = self.groups[tag]
        n = len(grp["names"])
        mine, got = _pair_wait("pair_wait_" + tag, grp["pair"], n, after)
        sums = [_pair_sum("pair_sum_" + name, a, b, self.ids) for name, a, b in zip(grp["names"], mine, got)]
        small_sum = _small_pair_sum(mine[n], got[n]) if grp["small"] else None
        grp["chip"], token = _chip_start("chip_start_" + tag, [s[1] for s in sums], small_sum)
        grp["sums"] = [s[0] for s in sums]
        self.last_token = token
        return (token,)

    def finish(self, tag, after):
        grp = self.groups[tag]
        srcs, lands = _chip_wait("chip_wait_" + tag, grp["chip"], grp["small"], after)
        halves = [_chip_sum("chip_sum_" + name, s, b, self.ids) for name, s, b in zip(grp["names"], grp["sums"], lands)]
        if grp["small"]:
            halves.append(_small_chip_sum(srcs[-1], lands[-1], self.ids))
        grp["swap"], token = _swap_start("swap_start_" + tag, halves)
        return token

    def interlude(self, tokens):
        after = tokens[0]
        for tag in list(self.groups)[:-1]:
            after = self.finish(tag, after)
        return after

    def result(self, tag, after):
        return _swap_wait("swap_wait_" + tag, self.groups[tag]["swap"], after)


def kernel(x, ln1_g, w_in, b_in, rpb, w_att_o, conv_w, conv_b, w_rg_a, b_rg_a, w_rg_i, b_rg_i, lru_lambda, w_rec_o, w_out, ln2_g, w_ff1, w_ff2, lnf_g, loss_target, m_ln1_g, m_w_in, m_b_in, m_rpb, m_w_att_o, m_conv_w, m_conv_b, m_w_rg_a, m_b_rg_a, m_w_rg_i, m_b_rg_i, m_lru_lambda, m_w_rec_o, m_w_out, m_ln2_g, m_w_ff1, m_w_ff2, m_lnf_g, v_ln1_g, v_w_in, v_b_in, v_rpb, v_w_att_o, v_conv_w, v_conv_b, v_w_rg_a, v_b_rg_a, v_w_rg_i, v_b_rg_i, v_lru_lambda, v_w_rec_o, v_w_out, v_ln2_g, v_w_ff1, v_w_ff2, v_lnf_g):
    w = dict(ln1_g=ln1_g, w_in=w_in, b_in=b_in, rpb=rpb, w_att_o=w_att_o, conv_w=conv_w, conv_b=conv_b,
             w_rg_a=w_rg_a, b_rg_a=b_rg_a, w_rg_i=w_rg_i, b_rg_i=b_rg_i, lru_lambda=lru_lambda, w_rec_o=w_rec_o,
             w_out=w_out, ln2_g=ln2_g, w_ff1=w_ff1, w_ff2=w_ff2, lnf_g=lnf_g)
    m = dict(ln1_g=m_ln1_g, w_in=m_w_in, b_in=m_b_in, rpb=m_rpb, w_att_o=m_w_att_o, conv_w=m_conv_w,
             conv_b=m_conv_b, w_rg_a=m_w_rg_a, b_rg_a=m_b_rg_a, w_rg_i=m_w_rg_i, b_rg_i=m_b_rg_i,
             lru_lambda=m_lru_lambda, w_rec_o=m_w_rec_o, w_out=m_w_out, ln2_g=m_ln2_g, w_ff1=m_w_ff1,
             w_ff2=m_w_ff2, lnf_g=m_lnf_g)
    v = dict(ln1_g=v_ln1_g, w_in=v_w_in, b_in=v_b_in, rpb=v_rpb, w_att_o=v_w_att_o, conv_w=v_conv_w,
             conv_b=v_conv_b, w_rg_a=v_w_rg_a, b_rg_a=v_b_rg_a, w_rg_i=v_w_rg_i, b_rg_i=v_b_rg_i,
             lru_lambda=v_lru_lambda, w_rec_o=v_w_rec_o, w_out=v_w_out, ln2_g=v_ln2_g, w_ff1=v_w_ff1,
             w_ff2=v_w_ff2, lnf_g=v_lnf_g)
    chip = 2 * lax.axis_index("x") + lax.axis_index("y")
    ids = jnp.stack([chip, lax.axis_index("c")]).astype(jnp.int32)

    out_grad, out_delta, out_m, out_v = {}, {}, {}, {}

    def update(n, gn):
        shape, two_d = w[n].shape, gn.shape
        d, nm, nv = _adamw("adamw_" + n, w[n].reshape(two_d), gn, m[n].reshape(two_d), v[n].reshape(two_d), 256)
        out_grad[n], out_delta[n], out_m[n], out_v[n] = (gn.reshape(shape), d.reshape(shape), nm.reshape(shape),
                                                         nv.reshape(shape))
        return d

    reducer = _Reducer(ids)
    p, late = _gather_weights(w, chip)
    loss, grad_x, g = _local_step(x, loss_target, p, late, reducer)
    small, sizes = _pack_small(g, loss + reducer.last_token[:1, :1])
    after = reducer.begin("small", {}, small)[0]
    for tag in ("ff", "proj", "in"):
        for n, red in zip(reducer.groups[tag]["names"], reducer.result(tag, after)):
            after = update(n, red)
        if tag == "ff":
            after = reducer.finish("in", reducer.advance("small", after)[0])
    (small_red,) = reducer.result("small", reducer.finish("small", after))
    gsmall, loss = _unpack_small(small_red, sizes, {n: g[n].shape for n in SMALL})
    two_d = {n: (int(np.prod(w[n].shape[:-1])), w[n].shape[-1]) for n in SMALL}
    for n in SHARDED_VECS:
        gsmall[n] = lax.dynamic_slice_in_dim(gsmall[n], chip * (D // N_CHIPS), D // N_CHIPS, axis=1)
    gs = [gsmall[n].reshape(two_d[n]) for n in SMALL]
    updates = _adamw_small([w[n].reshape(two_d[n]) for n in SMALL], gs, [m[n].reshape(two_d[n]) for n in SMALL],
                           [v[n].reshape(two_d[n]) for n in SMALL])
    for n, gn, (d, nm, nv) in zip(SMALL, gs, updates):
        shape = w[n].shape
        out_grad[n], out_delta[n], out_m[n], out_v[n] = (gn.reshape(shape), d.reshape(shape), nm.reshape(shape),
                                                         nv.reshape(shape))
    return (loss, grad_x, *[out_grad[n] for n in ORDER], *[out_delta[n] for n in ORDER],
            *[out_m[n] for n in ORDER], *[out_v[n] for n in ORDER])
```

```python
import numpy as np
import jax
import jax.numpy as jnp
from jax import lax
from jax.experimental import pallas as pl
from jax.experimental.pallas import tpu as pltpu

F32 = jnp.float32
BF16 = jnp.bfloat16

T = 2048
D = 1024
D_ATT = 512
D_IN = 5632
D_FF = 4096
N_HEADS = 8
HEAD_DIM = 64
GRID_W = 64
N_ROWS = T // GRID_W
WIN_H = 8
WIN_W = 16
KEYS = WIN_H * GRID_W
N_CHIPS = 4
EPS = 1e-6
LRU_C = 8.0
SCALE = HEAD_DIM ** -0.5
REC_CB = 256
REC_BLOCK = 64
REC_CHUNK = 256
PAD = 8

ADAM_LR = 0.001
ADAM_B1 = 0.9
ADAM_B2 = 0.999
ADAM_EPS = 1e-08
ADAM_WD = 0.01
ADAM_STEP = 10

VMEM_LIMIT = 56 * 1024 * 1024

NN = (((1,), (0,)), ((), ()))
NT = (((1,), (1,)), ((), ()))
TN = (((0,), (0,)), ((), ()))
MESH = pl.DeviceIdType.MESH


def _params(sem=None):
    return pltpu.CompilerParams(dimension_semantics=sem, vmem_limit_bytes=VMEM_LIMIT)


def _dot(a, b, dims):
    return lax.dot_general(a, b, dims, preferred_element_type=F32)


def _sigmoid(x):
    return 0.5 * jnp.tanh(0.5 * x) + 0.5


def _matmul(name, a, b, *, dims, grid, a_spec, b_spec, out_shapes, out_specs, acc_shape,
            extras=(), extra_specs=(), epilogue=None, colsum_spec=None, colsum_shape=None, after=(),
            semantics=("parallel", "parallel", "arbitrary"), epilogue_takes_first=False):
    nk = grid[2]
    n_extra = len(extras)
    n_out = len(out_shapes)
    with_colsum = colsum_spec is not None

    def body(a_ref, b_ref, *rest):
        ex = rest[:n_extra]
        rest = rest[:n_extra] + rest[n_extra + len(after):]
        outs = rest[n_extra:n_extra + n_out]
        pos = n_extra + n_out
        cs_out = rest[pos] if with_colsum else None
        pos += 1 if with_colsum else 0
        acc = rest[pos]
        cs_acc = rest[pos + 1] if with_colsum else None
        k = pl.program_id(2)
        first_tile = pl.program_id(0) == 0

        @pl.when(k == 0)
        def _():
            acc[...] = jnp.zeros_like(acc)
            if with_colsum:
                cs_acc[...] = jnp.zeros_like(cs_acc)

        bv = b_ref[...]
        acc[...] += _dot(a_ref[...].astype(BF16), bv.astype(BF16), dims)
        if with_colsum:
            cs_acc[...] += jnp.sum(bv.astype(F32), axis=0, keepdims=True)

        @pl.when(k == nk - 1)
        def _():
            r = acc[...]
            if epilogue is None:
                outs[0][...] = r.astype(outs[0].dtype)
            elif epilogue_takes_first:
                epilogue(r, ex, outs, first_tile)
            else:
                epilogue(r, ex, outs)
            if with_colsum:
                cs_out[...] = cs_acc[...]

    shapes = list(out_shapes)
    specs = list(out_specs)
    scratch = [pltpu.VMEM(acc_shape, F32)]
    if with_colsum:
        shapes.append(colsum_shape)
        specs.append(colsum_spec)
        scratch.append(pltpu.VMEM((1, acc_shape[1]), F32))
    res = pl.pallas_call(
        body, name=name, grid=grid,
        in_specs=[a_spec, b_spec, *extra_specs] + [_ANY] * len(after),
        out_specs=specs, out_shape=shapes, scratch_shapes=scratch,
        compiler_params=_params(semantics),
    )(a, b, *extras, *after)
    return res


def _sds(shape, dtype):
    return jax.ShapeDtypeStruct(shape, dtype)


TM = 1024
NI = T // TM
TJ = T
NJ = T // TJ


def _mm_nn_cols(name, a, wg, out_dtype, *, bias=None, extras=(), extra_specs=(), epilogue=None,
                out_shapes=None, out_specs=None):
    k_dim, n4 = wg.shape[1], wg.shape[2]
    ex, exs = list(extras), list(extra_specs)
    if bias is not None:
        ex = [bias] + ex
        exs = [pl.BlockSpec((1, n4), lambda j, i, k: (0, j))] + exs
        user_ep = epilogue

        def epilogue(r, e, outs):
            r = r + e[0][...]
            if user_ep is None:
                outs[0][...] = r.astype(outs[0].dtype)
            else:
                user_ep(r, e[1:], outs)
    if out_shapes is None:
        out_shapes = [_sds((T, N_CHIPS * n4), out_dtype)]
        out_specs = [pl.BlockSpec((TJ, n4), lambda j, i, k: (i, j))]
    return _matmul(
        name, a, wg, dims=NN, grid=(N_CHIPS, NJ, 1),
        a_spec=pl.BlockSpec((TJ, k_dim), lambda j, i, k: (i, 0)),
        b_spec=pl.BlockSpec((None, k_dim, n4), lambda j, i, k: (j, 0, 0)),
        out_shapes=out_shapes, out_specs=out_specs, acc_shape=(TJ, n4),
        extras=ex, extra_specs=exs, epilogue=epilogue)


def _mm_nt_cols_rms_bwd(name, a, wg, x, g, dres, after=(), bf16_copy=False):
    n4 = wg.shape[2]
    row = pl.BlockSpec((TM, D), lambda i, j, k: (i, 0))
    vec = pl.BlockSpec((1, D), lambda i, j, k: (0, 0))

    def epilogue(dhv, ex, outs, first):
        x_ref, g_ref, dres_ref = ex
        dx_ref, dg_ref = outs[0], outs[-1]
        xv = x_ref[...]
        rstd = lax.rsqrt(jnp.mean(xv * xv, axis=-1, keepdims=True) + EPS)
        xhat = xv * rstd
        dy = dhv * g_ref[...]
        dx = dres_ref[...] + rstd * (dy - xhat * jnp.mean(dy * xhat, axis=-1, keepdims=True))
        dx_ref[...] = dx
        if bf16_copy:
            outs[1][...] = dx.astype(BF16)
        part = jnp.sum(dhv * xhat, axis=0, keepdims=True)

        @pl.when(first)
        def _():
            dg_ref[...] = part

        @pl.when(jnp.logical_not(first))
        def _():
            dg_ref[...] += part

    return _matmul(
        name, a, wg, dims=NT, grid=(NI, 1, N_CHIPS),
        a_spec=pl.BlockSpec((TM, n4), lambda i, j, k: (i, k)),
        b_spec=pl.BlockSpec((None, D, n4), lambda i, j, k: (k, 0, 0)),
        out_shapes=[_sds((T, D), F32)] + [_sds((T, D), BF16)] * bf16_copy + [_sds((1, D), F32)],
        out_specs=[row] + [row] * bf16_copy + [vec], acc_shape=(TM, D),
        extras=[x, g, dres], extra_specs=[row, vec, row], epilogue=epilogue, after=after,
        semantics=("arbitrary", "arbitrary", "arbitrary"), epilogue_takes_first=True)


def _mm_nt_rows(name, a, w, out_dtype, *, tn, extras=(), extra_specs=(), epilogue=None):
    k_dim, n = w.shape
    return _matmul(
        name, a, w, dims=NT, grid=(k_dim // tn, NJ, 1),
        a_spec=pl.BlockSpec((TJ, n), lambda j, i, k: (i, 0)),
        b_spec=pl.BlockSpec((tn, n), lambda j, i, k: (j, 0)),
        out_shapes=[_sds((T, k_dim), out_dtype)],
        out_specs=[pl.BlockSpec((TJ, tn), lambda j, i, k: (i, j))], acc_shape=(TJ, tn),
        extras=extras, extra_specs=extra_specs, epilogue=epilogue)


def _mm_tn_cols(name, a, g, n4, *, colsum=False):
    k_dim = a.shape[1]
    kw = {}
    if colsum:
        kw = dict(colsum_spec=pl.BlockSpec((1, n4), lambda j, i, k: (0, j)),
                  colsum_shape=_sds((1, N_CHIPS * n4), F32))
    return _matmul(
        name, a, g, dims=TN, grid=(N_CHIPS, 1, NJ),
        a_spec=pl.BlockSpec((TJ, k_dim), lambda j, i, k: (k, 0)),
        b_spec=pl.BlockSpec((TJ, n4), lambda j, i, k: (k, j)),
        out_shapes=[_sds((N_CHIPS, k_dim, n4), F32)],
        out_specs=[pl.BlockSpec((None, k_dim, n4), lambda j, i, k: (j, 0, 0))],
        acc_shape=(k_dim, n4), **kw)


def _mm_tn_rows(name, a, g, *, tm):
    k_dim, n = a.shape[1], g.shape[1]
    return _matmul(
        name, a, g, dims=TN, grid=(k_dim // tm, 1, NJ),
        a_spec=pl.BlockSpec((TJ, tm), lambda j, i, k: (k, j)),
        b_spec=pl.BlockSpec((TJ, n), lambda j, i, k: (k, 0)),
        out_shapes=[_sds((k_dim, n), F32)],
        out_specs=[pl.BlockSpec((tm, n), lambda j, i, k: (j, 0))], acc_shape=(tm, n))


TE = 256
NE = T // TE
_ROW = pl.BlockSpec((TE, D), lambda i: (i, 0))
_VEC = pl.BlockSpec((1, D), lambda i: (0, 0))


def _rms_fwd(name, x, g, after=()):
    def body(x_ref, g_ref, *rest):
        h_ref = rest[-1]
        xv = x_ref[...]
        rstd = lax.rsqrt(jnp.mean(xv * xv, axis=-1, keepdims=True) + EPS)
        h_ref[...] = (xv * rstd * g_ref[...]).astype(BF16)

    return pl.pallas_call(body, name=name, grid=(NE,), in_specs=[_ROW, _VEC] + [_ANY] * len(after), out_specs=_ROW,
                          out_shape=_sds((T, D), BF16), compiler_params=_params(("parallel",)))(x, g, *after)


def _mm_x2_loss_head(s, w_ff2, x1, target, g):
    k_dim = w_ff2.shape[0]
    row = pl.BlockSpec((TM, D), lambda i, j, k: (i, 0))
    vec = pl.BlockSpec((1, D), lambda i, j, k: (0, 0))

    def epilogue(r, ex, outs, first):
        x1_ref, t_ref, g_ref = ex
        loss_ref, dx_ref, dxb_ref, dg_ref = outs
        xv = x1_ref[...] + r
        rstd = lax.rsqrt(jnp.mean(xv * xv, axis=-1, keepdims=True) + EPS)
        xhat = xv * rstd
        gv = g_ref[...]
        err = xhat * gv - t_ref[...]
        dy = err * (1.0 / D)
        dxh = dy * gv
        dx = rstd * (dxh - xhat * jnp.mean(dxh * xhat, axis=-1, keepdims=True))
        dx_ref[...] = dx
        dxb_ref[...] = dx.astype(BF16)
        dg_part = jnp.sum(dy * xhat, axis=0, keepdims=True)
        loss_part = (0.5 / D) * jnp.sum(jnp.sum(err * err, axis=1, keepdims=True), axis=0, keepdims=True)

        @pl.when(first)
        def _():
            dg_ref[...] = dg_part
            loss_ref[...] = loss_part

        @pl.when(jnp.logical_not(first))
        def _():
            dg_ref[...] += dg_part
            loss_ref[...] += loss_part

    return _matmul(
        "mm_x2_loss_head", s, w_ff2, dims=NN, grid=(NI, 1, k_dim // D),
        a_spec=pl.BlockSpec((TM, D), lambda i, j, k: (i, k)), b_spec=pl.BlockSpec((D, D), lambda i, j, k: (k, 0)),
        out_shapes=[_sds((1, 1), F32), _sds((T, D), F32), _sds((T, D), BF16), _sds((1, D), F32)],
        out_specs=[pl.BlockSpec((1, 1), lambda i, j, k: (0, 0)), row, row, vec], acc_shape=(TM, D),
        extras=[x1, target, g], extra_specs=[row, row, vec], epilogue=epilogue,
        semantics=("arbitrary", "arbitrary", "arbitrary"), epilogue_takes_first=True)


DZ_Q, DZ_K, DZ_V, DZ_U, DZ_Y, DZ_G_ATT, DZ_G_REC = 0, 512, 1024, 1536, 2560, 3584, 4608
MW = 512
_G_ATT_BLK = 3584 // MW
_G_REC_BLK = 4608 // MW


TB = 512


def _branch_specs():
    def row(cols):
        return pl.BlockSpec((TB, cols), lambda i: (i, 0))

    ga = pl.BlockSpec((TB, MW), lambda i: (i, _G_ATT_BLK))
    ga2 = pl.BlockSpec((TB, MW), lambda i: (i, _G_ATT_BLK + 1))
    gr = pl.BlockSpec((TB, MW), lambda i: (i, _G_REC_BLK))
    gr2 = pl.BlockSpec((TB, MW), lambda i: (i, _G_REC_BLK + 1))
    w_att = pl.BlockSpec((N_CHIPS, D_ATT, D // N_CHIPS), lambda i: (0, 0, 0))
    w_sq = pl.BlockSpec((D, D), lambda i: (0, 0))
    return row, (ga, ga2, gr, gr2), w_att, w_sq


def _gate_values(gate_refs):
    ga, ga2, gr, gr2 = (r[...].astype(F32) for r in gate_refs)
    return _sigmoid(jnp.concatenate([ga, ga2], axis=1)), _sigmoid(jnp.concatenate([gr, gr2], axis=1))


def _branches_fwd(att, g, z, x, w_att_o, w_rec_o, w_out, ln2_g):
    row, gate_specs, w_att, w_sq = _branch_specs()

    def body(att_ref, g_ref, ga_ref, ga2_ref, gr_ref, gr2_ref, x_ref, wa_ref, wr_ref, wo_ref, g2_ref,
             ya_ref, yr_ref, m_ref, x1_ref, h2_ref):
        attv = att_ref[...]
        ya = jnp.concatenate([_dot(attv, wa_ref[j], NN) for j in range(N_CHIPS)], axis=1)
        yr = _dot(g_ref[...], wr_ref[...], NN)
        sa, sr = _gate_values((ga_ref, ga2_ref, gr_ref, gr2_ref))
        mixed = (sa * ya + sr * yr).astype(BF16)
        ya_ref[...] = ya
        yr_ref[...] = yr
        m_ref[...] = mixed
        x1 = x_ref[...] + _dot(mixed, wo_ref[...], NN)
        x1_ref[...] = x1
        rstd = lax.rsqrt(jnp.mean(x1 * x1, axis=-1, keepdims=True) + EPS)
        h2_ref[...] = (x1 * rstd * g2_ref[...]).astype(BF16)

    return pl.pallas_call(
        body, name="branches_fwd", grid=(T // TB,),
        in_specs=[row(D_ATT), row(D), *gate_specs, row(D), w_att, w_sq, w_sq, pl.BlockSpec((1, D), lambda i: (0, 0))],
        out_specs=[row(D)] * 5,
        out_shape=[_sds((T, D), F32), _sds((T, D), F32), _sds((T, D), BF16), _sds((T, D), F32), _sds((T, D), BF16)],
        compiler_params=_params(("parallel",)))(att, g, z, z, z, z, x, w_att_o, w_rec_o, w_out, ln2_g)


def _branches_bwd(dx1_b, y_att, y_rec, z, w_att_o, w_rec_o, w_out):
    row, gate_specs, w_att, w_sq = _branch_specs()
    n4 = D // N_CHIPS

    def body(dx_ref, ya_ref, yr_ref, ga_ref, ga2_ref, gr_ref, gr2_ref, wa_ref, wr_ref, wo_ref,
             dya_ref, dyr_ref, datt_ref, dg_ref, dz_ref, dga_ref, dgr_ref, sems):
        rows = pl.ds(pl.multiple_of(pl.program_id(0) * TB, TB), TB)
        dm = _dot(dx_ref[...], wo_ref[...], NT)
        sa, sr = _gate_values((ga_ref, ga2_ref, gr_ref, gr2_ref))
        dya = (dm * sa).astype(BF16)
        dyr = (dm * sr).astype(BF16)
        dya_ref[...] = dya
        dyr_ref[...] = dyr
        dga_ref[...] = (dm * ya_ref[...] * sa * (1.0 - sa)).astype(BF16)
        dgr_ref[...] = (dm * yr_ref[...] * sr * (1.0 - sr)).astype(BF16)
        copies = [pltpu.make_async_copy(dga_ref, dz_ref.at[rows, pl.ds(DZ_G_ATT, D)], sems.at[0]),
                  pltpu.make_async_copy(dgr_ref, dz_ref.at[rows, pl.ds(DZ_G_REC, D)], sems.at[1])]
        for cp in copies:
            cp.start()
        datt = _dot(dya[:, 0:n4], wa_ref[0], NT)
        for j in range(1, N_CHIPS):
            datt = datt + _dot(dya[:, j * n4:(j + 1) * n4], wa_ref[j], NT)
        datt_ref[...] = datt.astype(BF16)
        dg_ref[...] = _dot(dyr, wr_ref[...], NT).astype(BF16)
        for cp in copies:
            cp.wait()

    return pl.pallas_call(
        body, name="branches_bwd", grid=(T // TB,),
        in_specs=[row(D), row(D), row(D), *gate_specs, w_att, w_sq, w_sq],
        out_specs=[row(D)] * 2 + [row(D_ATT), row(D), _ANY],
        out_shape=[_sds((T, D), BF16)] * 2 + [_sds((T, D_ATT), BF16), _sds((T, D), BF16), _sds((T, D_IN), BF16)],
        scratch_shapes=[pltpu.VMEM((TB, D), BF16), pltpu.VMEM((TB, D), BF16), pltpu.SemaphoreType.DMA((2,))],
        compiler_params=_params(("parallel",)))(dx1_b, y_att, y_rec, z, z, z, z, w_att_o, w_rec_o, w_out)


HP = 2 * HEAD_DIM
N_HP = N_HEADS // 2
ATT_UNROLL_FWD = 16
ATT_UNROLL_BWD = 16
DIAG_ROWS = 32


def _window_maps():
    diag = np.zeros((GRID_W * GRID_W, 128), np.float32)
    for qc in range(GRID_W):
        w0 = min(max(qc - WIN_W // 2, 0), GRID_W - WIN_W)
        for kc in range(w0, w0 + WIN_W):
            diag[qc * GRID_W + kc, kc - qc + WIN_W - 1] = 1.0
    return diag, diag.sum(axis=1)[None, :]


def _split3(x):
    a = x.astype(BF16)
    r = x - a.astype(F32)
    b = r.astype(BF16)
    c = (r - b.astype(F32)).astype(BF16)
    return a, b, c


N_DROW = 2 * WIN_H - 1
N_DPAIR = N_DROW - 1


def _bias_pairs(rpb):
    diag, valid = _window_maps()
    r2 = jnp.pad(rpb.reshape(N_HEADS * N_DROW, 2 * WIN_W - 1),
                 ((0, 128 - N_HEADS * N_DROW), (0, 128 - (2 * WIN_W - 1))))

    def body(r_ref, d_ref, v_ref, o_ref):
        dv = d_ref[...]
        t = sum(_dot(part, dv, NN) for part in _split3(r_ref[...]))
        o_ref[...] = jnp.where(v_ref[...] > 0.0, t, -1e30)

    t = pl.pallas_call(body, name="rpb_expand", out_shape=_sds((128, GRID_W * GRID_W), F32),
                       compiler_params=_params())(r2, jnp.asarray(diag.T, BF16), jnp.asarray(valid, F32))
    t = t[:N_HEADS * N_DROW].reshape(N_HEADS, N_DROW, GRID_W, GRID_W)
    return jnp.concatenate([t[:, :N_DPAIR], t[:, 1:]], axis=-1)


def _row_bias(tb_ref, hh, d0):
    return jnp.concatenate([tb_ref[hh, d0 + 2 * ii] for ii in range(WIN_H // 2)], axis=1)


def _row_window(r):
    rs = jnp.clip(r - WIN_H // 2, 0, N_ROWS - WIN_H)
    return pl.multiple_of(r * GRID_W, GRID_W), pl.multiple_of(rs * GRID_W, GRID_W), rs - r + (WIN_H - 1)


def _split_heads(src_ref, dst_ref, scale=None):
    for hh in range(2):
        v = src_ref[:, hh * HEAD_DIM:(hh + 1) * HEAD_DIM]
        dst_ref[hh] = (v if scale is None else v * scale).astype(BF16)


def _attn_items(qb_ref, kb_ref, vb_ref, tb_ref, first_row, n_rows):
    wins = [_row_window(first_row + u) for u in range(n_rows)]
    items = [(u, hh) for u in range(n_rows) for hh in range(2)]
    q = [qb_ref[hh, pl.ds(wins[u][0], GRID_W), :] for u, hh in items]
    k = [kb_ref[hh, pl.ds(wins[u][1], KEYS), :] for u, hh in items]
    v = [vb_ref[hh, pl.ds(wins[u][1], KEYS), :] for u, hh in items]
    s = [_dot(qi, ki, NT) + _row_bias(tb_ref, hh, wins[u][2]) for qi, ki, (u, hh) in zip(q, k, items)]
    m = [jnp.max(si, axis=-1, keepdims=True) for si in s]
    e = [jnp.exp(si - mi) for si, mi in zip(s, m)]
    inv = [1.0 / jnp.sum(ei, axis=-1, keepdims=True) for ei in e]
    p = [ei * li for ei, li in zip(e, inv)]
    return wins, items, q, k, v, p


def _attn_in_specs():
    q = pl.BlockSpec((T, HP), lambda p: (0, p))
    k = pl.BlockSpec((T, HP), lambda p: (0, N_HP + p))
    v = pl.BlockSpec((T, HP), lambda p: (0, 2 * N_HP + p))
    tb = pl.BlockSpec((2, N_DPAIR, GRID_W, HP), lambda p: (p, 0, 0, 0))
    return q, k, v, tb


_HEAD_SCRATCH = pltpu.VMEM((2, T, HEAD_DIM), BF16)


_PROBS = pl.BlockSpec((T, 2 * KEYS), lambda p: (0, p))


def _attn_fwd(z, tb):
    def body(q_ref, k_ref, v_ref, tb_ref, o_ref, p_ref, qb_ref, kb_ref, vb_ref):
        _split_heads(q_ref, qb_ref, SCALE)
        _split_heads(k_ref, kb_ref)
        _split_heads(v_ref, vb_ref)

        def rows(it, carry):
            wins, items, _, _, v, p = _attn_items(qb_ref, kb_ref, vb_ref, tb_ref, it * ATT_UNROLL_FWD, ATT_UNROLL_FWD)
            pb = [pi.astype(BF16) for pi in p]
            o = [_dot(pi, vi, NN) for pi, vi in zip(pb, v)]
            for u, (q0, _, _) in enumerate(wins):
                o_ref[pl.ds(q0, GRID_W), :] = jnp.concatenate(o[2 * u:2 * u + 2], axis=1).astype(BF16)
                p_ref[pl.ds(q0, GRID_W), :] = jnp.concatenate(pb[2 * u:2 * u + 2], axis=1)
            return carry

        lax.fori_loop(0, N_ROWS // ATT_UNROLL_FWD, rows, 0)

    blk = pl.BlockSpec((T, HP), lambda p: (0, p))
    return pl.pallas_call(
        body, name="attn_fwd", grid=(N_HP,), in_specs=list(_attn_in_specs()), out_specs=[blk, _PROBS],
        out_shape=[_sds((T, D_ATT), BF16), _sds((T, N_HEADS * KEYS), BF16)], scratch_shapes=[_HEAD_SCRATCH] * 3,
        compiler_params=_params(("parallel",)))(z, z, z, tb)


def _attn_bwd(z, probs, d_att, dz, after=()):
    def body(q_ref, k_ref, v_ref, p_ref, do_ref, flip_ref, dz_in_ref, *rest):
        (dz_ref, diag_ref, qb_ref, kb_ref, vb_ref, dob_ref, dka_ref, dva_ref, ds_ref,
         dq_ref, dk_ref, dv_ref, sems) = rest[len(after):]
        _split_heads(q_ref, qb_ref, SCALE)
        _split_heads(k_ref, kb_ref)
        _split_heads(v_ref, vb_ref)
        _split_heads(do_ref, dob_ref)
        dka_ref[...] = jnp.zeros_like(dka_ref)
        dva_ref[...] = jnp.zeros_like(dva_ref)
        ds_ref[...] = jnp.zeros_like(ds_ref)

        def rows(it, carry):
            wins = [_row_window(it * ATT_UNROLL_BWD + u) for u in range(ATT_UNROLL_BWD)]
            items = [(u, hh) for u in range(ATT_UNROLL_BWD) for hh in range(2)]
            q = [qb_ref[hh, pl.ds(wins[u][0], GRID_W), :] for u, hh in items]
            k = [kb_ref[hh, pl.ds(wins[u][1], KEYS), :] for u, hh in items]
            v = [vb_ref[hh, pl.ds(wins[u][1], KEYS), :] for u, hh in items]
            pb = [p_ref[pl.ds(wins[u][0], GRID_W), hh * KEYS:(hh + 1) * KEYS] for u, hh in items]
            p = [pi.astype(F32) for pi in pb]
            do = [dob_ref[hh, pl.ds(wins[u][0], GRID_W), :] for u, hh in items]
            dv = [_dot(pi, di, TN) for pi, di in zip(pb, do)]
            dp = [_dot(di, vi, NT) for di, vi in zip(do, v)]
            ds = [pi * (dpi - jnp.sum(dpi * pi, axis=-1, keepdims=True)) for pi, dpi in zip(p, dp)]
            dsb = [d.astype(BF16) for d in ds]
            dq = [_dot(d, ki, NN) * SCALE for d, ki in zip(dsb, k)]
            dk = [_dot(d, qi, TN) for d, qi in zip(dsb, q)]
            for d, (u, hh) in zip(ds, items):
                for ii in range(WIN_H // 2):
                    ds_ref[hh, wins[u][2] + 2 * ii] += d[:, ii * HP:(ii + 1) * HP]
            for u, (q0, _, _) in enumerate(wins):
                dq_ref[pl.ds(q0, GRID_W), :] = jnp.concatenate(dq[2 * u:2 * u + 2], axis=1).astype(BF16)
            for dki, dvi, (u, hh) in zip(dk, dv, items):
                dka_ref[hh, pl.ds(wins[u][1], KEYS), :] += dki
                dva_ref[hh, pl.ds(wins[u][1], KEYS), :] += dvi
            return carry

        lax.fori_loop(0, N_ROWS // ATT_UNROLL_BWD, rows, 0)
        dk_ref[...] = jnp.concatenate([dka_ref[0], dka_ref[1]], axis=1).astype(BF16)
        dv_ref[...] = jnp.concatenate([dva_ref[0], dva_ref[1]], axis=1).astype(BF16)
        cols = pl.multiple_of(pl.program_id(0) * HP, HP)
        copies = [pltpu.make_async_copy(src, dz_ref.at[:, pl.ds(base + cols, HP)], sems.at[t])
                  for t, (src, base) in enumerate(((dq_ref, DZ_Q), (dk_ref, DZ_K), (dv_ref, DZ_V)))]
        for cp in copies:
            cp.start()
        _diag_sums(ds_ref, flip_ref, diag_ref)
        for cp in copies:
            cp.wait()

    blk = pl.BlockSpec((T, HP), lambda p: (0, p))
    q, k, v, _ = _attn_in_specs()
    flip =jnp.asarray(np.eye(HP, dtype=np.float32)[::-1], BF16)
    return pl.pallas_call(
        body, name="attn_bwd", grid=(N_HP,),
        in_specs=[q, k, v, _PROBS, blk, pl.BlockSpec((HP, HP), lambda p: (0, 0)), _ANY] + [_ANY] * len(after),
        out_specs=[_ANY, pl.BlockSpec((None, DIAG_ROWS, HP), lambda p: (p, 0, 0))],
        out_shape=[_sds(dz.shape, dz.dtype), _sds((N_HP, DIAG_ROWS, HP), F32)], input_output_aliases={6: 0},
        scratch_shapes=[_HEAD_SCRATCH] * 4 + [pltpu.VMEM((2, T, HEAD_DIM), F32), pltpu.VMEM((2, T, HEAD_DIM), F32),
                                              pltpu.VMEM((2, N_DPAIR, GRID_W, HP), F32)]
        + [pltpu.VMEM((T, HP), BF16)] * 3 + [pltpu.SemaphoreType.DMA((3,))],
        compiler_params=_params(("parallel",)))(z, z, z, probs, d_att, flip, dz, *after)


def _diag_sums(acc_ref, flip_ref, out_ref):
    flip = flip_ref[...]
    rows = []
    for hh in range(2):
        for pair in range(N_DPAIR):
            reversed_lanes = sum(_dot(part, flip, NN) for part in _split3(acc_ref[hh, pair]))
            skewed = pltpu.roll(reversed_lanes, 0, 1, stride=1, stride_axis=0)
            rows.append(jnp.sum(skewed, axis=0, keepdims=True))
    rows.append(jnp.zeros((DIAG_ROWS - len(rows), HP), F32))
    out_ref[...] = jnp.concatenate(rows, axis=0)


def _rpb_grad(diag_sums):
    g = diag_sums.reshape(N_HP * DIAG_ROWS, HP)
    sel = np.zeros((2, 128, N_HP * DIAG_ROWS), np.float32)
    lane = np.zeros((2, HP, 128), np.float32)
    for h in range(N_HEADS):
        for pair in range(N_DPAIR):
            for half in range(2):
                sel[half, h * N_DROW + pair + half, (h // 2) * DIAG_ROWS + (h % 2) * N_DPAIR + pair] = 1.0
    for j in range(2 * WIN_W - 1):
        for half in range(2):
            lane[half, (HP - 1 - GRID_W * half - (j - (WIN_W - 1))) % HP, j] = 1.0

    def body(g_ref, sel_ref, lane_ref, o_ref):
        parts = _split3(g_ref[...])
        total = None
        for half in range(2):
            picked = sum(_dot(sel_ref[half], part, NN) for part in parts)
            term = sum(_dot(part, lane_ref[half], NN) for part in _split3(picked))
            total = term if total is None else total + term
        o_ref[...] = total

    out = pl.pallas_call(body, name="rpb_grad", out_shape=_sds((128, 128), F32),
                         compiler_params=_params())(g, jnp.asarray(sel, BF16), jnp.asarray(lane, BF16))
    return out[:N_HEADS * N_DROW, :2 * WIN_W - 1].reshape(N_HEADS, N_DROW, 2 * WIN_W - 1)


N_CB = D // REC_CB
N_CHUNK = T // REC_CHUNK
DUP_PIECES = 4
N_TILE = T // 8
_U_BLK = 1536 // REC_CB
_Y_BLK = 2560 // REC_CB


def _block_diag(w):
    per = REC_CB // 64
    wt = w.reshape(2, N_CB, per, 64, 64)
    eye = jnp.eye(per, dtype=w.dtype)
    full = wt[:, :, :, :, None, :] * eye[None, None, :, None, :, None]
    return full.reshape(2, N_CB, REC_CB, REC_CB).astype(BF16)


def _gelu(x):
    c = 0.7978845608028654
    return 0.5 * x * (1.0 + jnp.tanh(c * (x + 0.044715 * x * x * x)))


def _gelu_grad(x):
    c = 0.7978845608028654
    th = jnp.tanh(c * (x + 0.044715 * x * x * x))
    return 0.5 * (1.0 + th) + 0.5 * x * (1.0 - th * th) * c * (1.0 + 3.0 * 0.044715 * x * x)


def _softplus_neg(lam):
    x = -lam
    e = jnp.exp(-jnp.abs(x))
    w = 1.0 + e
    l1p = jnp.where(w == 1.0, e, jnp.log(w) * e / (w - 1.0))
    return jnp.maximum(x, 0.0) + l1p


def _one_minus_exp(x, exp_x):
    poly = x * (1.0 + x * (1 / 2 + x * (1 / 6 + x * (1 / 24 + x * (1 / 120 + x * (1 / 720))))))
    return jnp.where(x > -0.125, -poly, 1.0 - exp_x)


def _conv_taps(pad_ref, t0, w, sign):
    out = None
    for j in range(4):
        term = w[j:j + 1, :] * pad_ref[pl.ds(PAD + t0 + sign * (j - 2), REC_CHUNK), :]
        out = term if out is None else out + term
    return out


def _gates(u, wa, wi, ba, bi, sp):
    ub = u.astype(BF16)
    r = _sigmoid(_dot(ub, wa, NN) + ba)
    i = _sigmoid(_dot(ub, wi, NN) + bi)
    log_a = (-LRU_C * sp) * r
    a = jnp.exp(log_a)
    x = jnp.maximum(_one_minus_exp(2.0 * log_a, a * a), 0.0)
    positive = x > 0.0
    inv = lax.rsqrt(jnp.where(positive, x, 1.0))
    mult = jnp.where(positive, x * inv, 0.0)
    return r, i, a, mult, jnp.where(positive, inv, 0.0)


def _tile_scan(a, b, sub, reverse):
    for s in (1, 2, 4):
        if reverse:
            a_s, b_s, m = pltpu.roll(a, 8 - s, 0), pltpu.roll(b, 8 - s, 0), sub < 8 - s
        else:
            a_s, b_s, m = pltpu.roll(a, s, 0), pltpu.roll(b, s, 0), sub >= s
        b = jnp.where(m, a * b_s + b, b)
        a = jnp.where(m, a * a_s, a)
    return a, b


def _last_row(x, row):
    return jnp.broadcast_to(x[row:row + 1, :], x.shape)


def _rec_prologue(up_ref, cw_ref, cb_ref, wa_ref, wi_ref, ba_ref, bi_ref, lam_ref,
                  upad_ref, u_ref, a_refs, h_refs):
    cb = up_ref.shape[1]
    zeros = jnp.zeros((PAD, cb), F32)
    upad_ref[pl.ds(0, PAD), :] = zeros
    upad_ref[pl.ds(PAD + T, PAD), :] = zeros
    upad_ref[pl.ds(PAD, T), :] = up_ref[...].astype(F32)
    cw = cw_ref[...]
    sp = _softplus_neg(lam_ref[...])
    for c in range(N_CHUNK):
        t0 = c * REC_CHUNK
        u = cb_ref[...] + _conv_taps(upad_ref, t0, cw, 1)
        u_ref[pl.ds(t0, REC_CHUNK), :] = u
        for d in range(2):
            _, i, a, mult, _ = _gates(u, wa_ref[d], wi_ref[d], ba_ref[d:d + 1, :], bi_ref[d:d + 1, :], sp[d:d + 1, :])
            a_refs[d][pl.ds(t0, REC_CHUNK), :] = a
            h_refs[d][pl.ds(t0, REC_CHUNK), :] = mult * (i * u)

    sub = lax.broadcasted_iota(jnp.int32, (8, cb), 0)

    def tile(k, carry):
        cf, cr = carry
        tf = pl.multiple_of(k * 8, 8)
        tr = pl.multiple_of((N_TILE - 1 - k) * 8, 8)
        af, bf = _tile_scan(a_refs[0][pl.ds(tf, 8), :], h_refs[0][pl.ds(tf, 8), :], sub, False)
        hf = af * cf + bf
        h_refs[0][pl.ds(tf, 8), :] = hf
        ar, br = _tile_scan(a_refs[1][pl.ds(tr, 8), :], h_refs[1][pl.ds(tr, 8), :], sub, True)
        hr = ar * cr + br
        h_refs[1][pl.ds(tr, 8), :] = hr
        return _last_row(af, 7) * cf + _last_row(bf, 7), _last_row(ar, 0) * cr + _last_row(br, 0)

    z8 = jnp.zeros((8, cb), F32)
    lax.fori_loop(0, N_TILE, tile, (z8, z8))
    return sp


def _rec_specs():
    up = pl.BlockSpec((T, REC_CB), lambda c: (0, _U_BLK + c))
    yb = pl.BlockSpec((T, REC_CB), lambda c: (0, _Y_BLK + c))
    cw = pl.BlockSpec((4, REC_CB), lambda c: (0, c))
    cbias = pl.BlockSpec((1, REC_CB), lambda c: (0, c))
    wbd = pl.BlockSpec((2, None, REC_CB, REC_CB), lambda c: (0, c, 0, 0))
    vec2 = pl.BlockSpec((2, REC_CB), lambda c: (0, c))
    col = pl.BlockSpec((T, REC_CB), lambda c: (0, c))
    return up, yb, cw, cbias, wbd, vec2, col


def _rec_fwd(z, conv_w, conv_b, wa, wi, ba, bi, lam):
    up, yb, cw, cbias, wbd, vec2, col = _rec_specs()

    def body(up_ref, yb_ref, cw_ref, cb_ref, wa_ref, wi_ref, ba_ref, bi_ref, lam_ref, g_ref,
             u_ref, af_ref, ar_ref, hf_ref, hr_ref, upad_ref):
        _rec_prologue(up_ref, cw_ref, cb_ref, wa_ref, wi_ref, ba_ref, bi_ref, lam_ref,
                      upad_ref, u_ref, (af_ref, ar_ref), (hf_ref, hr_ref))

        def chunk(c, carry):
            t0 = pl.multiple_of(c * REC_CHUNK, REC_CHUNK)
            rows = pl.ds(t0, REC_CHUNK)
            g_ref[rows, :] = ((hf_ref[rows, :] + hr_ref[rows, :]) * _gelu(yb_ref[rows, :].astype(F32))).astype(BF16)
            return carry

        lax.fori_loop(0, N_CHUNK, chunk, 0)

    res = pl.pallas_call(
        body, name="rec_fwd", grid=(N_CB,),
        in_specs=[up, yb, cw, cbias, wbd, wbd, vec2, vec2, vec2], out_specs=[col] * 6,
        out_shape=[_sds((T, D), BF16)] + [_sds((T, D), F32)] * 5,
        scratch_shapes=[pltpu.VMEM((T + 2 * PAD, REC_CB), F32)],
        compiler_params=_params(("parallel",)))(z, z, conv_w, conv_b, wa, wi, ba, bi, lam)
    return res[0], tuple(res[1:])


def _rec_bwd(z, dg, saved, dz, conv_w, conv_b, wa, wi, ba, bi, lam, after=()):
    up, yb, cw, cbias, wbd, vec2, col = _rec_specs()

    def body(up_ref, yb_ref, dg_ref, u_ref, af_ref, ar_ref, hf_ref, hr_ref,
             cw_ref, cb_ref, wa_ref, wi_ref, ba_ref, bi_ref, lam_ref, dz_in_ref, *rest):
        (dz_ref, dcw_ref, dcb_ref, dwa_out, dwi_out, dba_ref, dbi_ref, dlam_ref,
         upad_ref, dh_ref, gf_ref, gr_ref, daf_ref, dar_ref, dupad_ref, dwa_ref, dwi_ref,
         dup_ref, dyb_ref, sems) = rest[len(after):]
        g_refs, da_refs = (gf_ref, gr_ref), (daf_ref, dar_ref)
        cb = up_ref.shape[1]
        zeros = jnp.zeros((PAD, cb), F32)
        upad_ref[pl.ds(0, PAD), :] = zeros
        upad_ref[pl.ds(PAD + T, PAD), :] = zeros
        upad_ref[pl.ds(PAD, T), :] = up_ref[...].astype(F32)
        sp = _softplus_neg(lam_ref[...])

        def gate_chunk(c, carry):
            t0 = pl.multiple_of(c * REC_CHUNK, REC_CHUNK)
            rows = pl.ds(t0, REC_CHUNK)
            y = yb_ref[rows, :].astype(F32)
            dgv = dg_ref[rows, :].astype(F32)
            dh_ref[rows, :] = dgv * _gelu(y)
            dyb_ref[rows, :] = (dgv * (hf_ref[rows, :] + hr_ref[rows, :]) * _gelu_grad(y)).astype(BF16)
            return carry

        lax.fori_loop(0, N_CHUNK, gate_chunk, 0)
        cols = pl.multiple_of(pl.program_id(0) * REC_CB, REC_CB)
        dyb_copy = pltpu.make_async_copy(dyb_ref, dz_ref.at[:, pl.ds(DZ_Y + cols, REC_CB)], sems.at[1])
        dyb_copy.start()

        sub = lax.broadcasted_iota(jnp.int32, (8, cb), 0)

        def tile(k, carry):
            cf, cr = carry
            kf = N_TILE - 1 - k
            tf = pl.multiple_of(kf * 8, 8)
            tnext = pl.multiple_of(jnp.minimum(kf + 1, N_TILE - 1) * 8, 8)
            tprev = pl.multiple_of(jnp.maximum(kf - 1, 0) * 8, 8)
            a_t = af_ref[pl.ds(tf, 8), :]
            a_n = jnp.where(kf < N_TILE - 1, af_ref[pl.ds(tnext, 8), :], 0.0)
            a_sh = jnp.where(sub == 7, pltpu.roll(a_n, 7, 0), pltpu.roll(a_t, 7, 0))
            ca, cbb = _tile_scan(a_sh, dh_ref[pl.ds(tf, 8), :], sub, True)
            gf = ca * cf + cbb
            h_t = hf_ref[pl.ds(tf, 8), :]
            h_p = jnp.where(kf > 0, hf_ref[pl.ds(tprev, 8), :], 0.0)
            h_sh = jnp.where(sub == 0, pltpu.roll(h_p, 1, 0), pltpu.roll(h_t, 1, 0))
            gf_ref[pl.ds(tf, 8), :] = gf
            daf_ref[pl.ds(tf, 8), :] = gf * h_sh
            tr = pl.multiple_of(k * 8, 8)
            rnext = pl.multiple_of(jnp.minimum(k + 1, N_TILE - 1) * 8, 8)
            rprev = pl.multiple_of(jnp.maximum(k - 1, 0) * 8, 8)
            b_t = ar_ref[pl.ds(tr, 8), :]
            b_p = jnp.where(k > 0, ar_ref[pl.ds(rprev, 8), :], 0.0)
            b_sh = jnp.where(sub == 0, pltpu.roll(b_p, 1, 0), pltpu.roll(b_t, 1, 0))
            ra, rb = _tile_scan(b_sh, dh_ref[pl.ds(tr, 8), :], sub, False)
            gr = ra * cr + rb
            hr_t = hr_ref[pl.ds(tr, 8), :]
            hr_n = jnp.where(k < N_TILE - 1, hr_ref[pl.ds(rnext, 8), :], 0.0)
            hr_sh = jnp.where(sub == 7, pltpu.roll(hr_n, 7, 0), pltpu.roll(hr_t, 7, 0))
            gr_ref[pl.ds(tr, 8), :] = gr
            dar_ref[pl.ds(tr, 8), :] = gr * hr_sh
            return _last_row(gf, 0), _last_row(gr, 7)

        z8 = jnp.zeros((8, cb), F32)
        lax.fori_loop(0, N_TILE, tile, (z8, z8))

        dupad_ref[pl.ds(0, PAD), :] = zeros
        dupad_ref[pl.ds(PAD + T, PAD), :] = zeros
        dwa_ref[...] = jnp.zeros_like(dwa_ref)
        dwi_ref[...] = jnp.zeros_like(dwi_ref)
        dba_ref[...] = jnp.zeros_like(dba_ref)
        dbi_ref[...] = jnp.zeros_like(dbi_ref)
        dlam_ref[...] = jnp.zeros_like(dlam_ref)

        def grad_chunk(c, carry):
            t0 = pl.multiple_of(c * REC_CHUNK, REC_CHUNK)
            rows = pl.ds(t0, REC_CHUNK)
            u = u_ref[rows, :]
            ub = u.astype(BF16)
            du = jnp.zeros((REC_CHUNK, cb), F32)
            for d in range(2):
                r, i, a, mult, inv_mult = _gates(u, wa_ref[d], wi_ref[d], ba_ref[d:d + 1, :], bi_ref[d:d + 1, :],
                                                 sp[d:d + 1, :])
                dbx = g_refs[d][rows, :]
                dmult = dbx * (i * u)
                diu = dbx * mult
                a2 = a * a
                dlog = da_refs[d][rows, :] * a - dmult * (a2 * inv_mult)
                dpa = (dlog * (-LRU_C) * sp[d:d + 1, :]) * r * (1.0 - r)
                dpi = (diu * u) * i * (1.0 - i)
                dpab, dpib = dpa.astype(BF16), dpi.astype(BF16)
                du = du + diu * i + _dot(dpab, wa_ref[d], NT) + _dot(dpib, wi_ref[d], NT)
                dwa_ref[d] += _dot(ub, dpab, TN)
                dwi_ref[d] += _dot(ub, dpib, TN)
                dba_ref[d:d + 1, :] += jnp.sum(dpa, axis=0, keepdims=True)
                dbi_ref[d:d + 1, :] += jnp.sum(dpi, axis=0, keepdims=True)
                dlam_ref[d:d + 1, :] += jnp.sum(dlog * r, axis=0, keepdims=True)
            dupad_ref[pl.ds(PAD + t0, REC_CHUNK), :] = du
            return carry

        lax.fori_loop(0, N_CHUNK, grad_chunk, 0)
        cw = cw_ref[...]
        dcb = jnp.zeros((1, cb), F32)
        dcw = [jnp.zeros((1, cb), F32) for _ in range(4)]
        dup_copies = []
        per_piece = N_CHUNK // DUP_PIECES
        for c in range(N_CHUNK):
            t0 = c * REC_CHUNK
            du = dupad_ref[pl.ds(PAD + t0, REC_CHUNK), :]
            dcb = dcb + jnp.sum(du, axis=0, keepdims=True)
            for j in range(4):
                dcw[j] = dcw[j] + jnp.sum(du * upad_ref[pl.ds(PAD + t0 + j - 2, REC_CHUNK), :], axis=0, keepdims=True)
            dup_ref[pl.ds(t0, REC_CHUNK), :] = _conv_taps(dupad_ref, t0, cw, -1).astype(BF16)
            if (c + 1) % per_piece == 0:
                piece = c // per_piece
                rows = pl.ds(piece * per_piece * REC_CHUNK, per_piece * REC_CHUNK)
                dup_copies.append(pltpu.make_async_copy(
                    dup_ref.at[rows, :], dz_ref.at[rows, pl.ds(DZ_U + cols, REC_CB)], sems.at[2 + piece]))
                dup_copies[-1].start()
        dcb_ref[...] = dcb
        dcw_ref[...] = jnp.concatenate(dcw, axis=0)
        dlam_ref[...] = dlam_ref[...] * (LRU_C * _sigmoid(-lam_ref[...]))
        for d in range(2):
            for blk in range(REC_CB // REC_BLOCK):
                lo, hi = blk * REC_BLOCK, (blk + 1) * REC_BLOCK
                dwa_out[d, blk] = dwa_ref[d, lo:hi, lo:hi]
                dwi_out[d, blk] = dwi_ref[d, lo:hi, lo:hi]
        for cp in dup_copies:
            cp.wait()
        dyb_copy.wait()

    full = pltpu.VMEM((T, REC_CB), F32)
    padded = pltpu.VMEM((T + 2 * PAD, REC_CB), F32)
    per = REC_CB // REC_BLOCK
    diag = pl.BlockSpec((2, per, REC_BLOCK, REC_BLOCK), lambda c: (0, c, 0, 0))
    return pl.pallas_call(
        body, name="rec_bwd", grid=(N_CB,),
        in_specs=[up, yb] + [col] * 6 + [cw, cbias, wbd, wbd, vec2, vec2, vec2, _ANY] + [_ANY] * len(after),
        out_specs=[_ANY, cw, cbias, diag, diag, vec2, vec2, vec2],
        out_shape=[_sds(dz.shape, dz.dtype), _sds((4, D), F32), _sds((1, D), F32),
                   _sds((2, D // REC_BLOCK, REC_BLOCK, REC_BLOCK), F32), _sds((2, D // REC_BLOCK, REC_BLOCK, REC_BLOCK), F32),
                   _sds((2, D), F32), _sds((2, D), F32), _sds((2, D), F32)],
        input_output_aliases={15: 0},
        scratch_shapes=[padded, full, full, full, full, full, padded,
                        pltpu.VMEM((2, REC_CB, REC_CB), F32), pltpu.VMEM((2, REC_CB, REC_CB), F32),
                        pltpu.VMEM((T, REC_CB), BF16), pltpu.VMEM((T, REC_CB), BF16),
                        pltpu.SemaphoreType.DMA((2 + DUP_PIECES,))],
        compiler_params=_params(("parallel",)))(z, z, dg, *saved, conv_w, conv_b, wa, wi, ba, bi, lam, dz, *after)


class _NoReducer:
    def begin(self, tag, grads):
        return ()

    def advance(self, tag, after):
        return ()

    def interlude(self, tokens):
        return None


def _local_step(x, target, p, late=None, reducer=_NoReducer()):
    x = x.reshape(T, D)
    target = target.reshape(T, D)
    tb = _bias_pairs(p["rpb"])
    wa, wi = _block_diag(p["w_rg_a"]), _block_diag(p["w_rg_i"])

    h1 = _rms_fwd("rms1_fwd", x, p["ln1_g"], after=late[0] if late else ())
    if late:
        p = {**p, **late[1]((h1, tb, wa, wi))}
    rec_params = (p["conv_w"], p["conv_b"], wa, wi, p["b_rg_a"], p["b_rg_i"], p["lru_lambda"])
    (z,) = _mm_nn_cols("mm_z", h1, p["w_in"], BF16, bias=p["b_in"])
    att, probs = _attn_fwd(z, tb)
    g, rec_saved = _rec_fwd(z, *rec_params)
    if late:
        p = {**p, **late[2](g)}
    y_att, y_rec, mixed, x1, h2 = _branches_fwd(att, g, z, x, p["w_att_o"], p["w_rec_o"], p["w_out"], p["ln2_g"])

    def relu2(r, ex, outs):
        rp = jnp.maximum(r, 0.0)
        outs[0][...] = (rp * rp).astype(BF16)

    (s,) = _mm_nn_cols("mm_ff1", h2, p["w_ff1"], BF16, epilogue=relu2)
    loss, dx2, dx2_b, g_lnf = _mm_x2_loss_head(s, p["w_ff2"], x1, target, p["lnf_g"])

    def relu2_bwd(r, ex, outs):
        outs[0][...] = (r * 2.0 * jnp.sqrt(ex[0][...].astype(F32))).astype(BF16)

    (df,) = _mm_nt_rows("mm_df", dx2_b, p["w_ff2"], BF16, tn=D, extras=[s],
                        extra_specs=[pl.BlockSpec((TJ, D), lambda j, i, k: (i, j))], epilogue=relu2_bwd)
    (g_w_ff2,) = _mm_tn_rows("mm_g_ff2", s, dx2_b, tm=D)
    (g_w_ff1,) = _mm_tn_cols("mm_g_ff1", h2, df, D)
    tok = reducer.begin("ff", dict(w_ff2=g_w_ff2, w_ff1=g_w_ff1))
    dx1, dx1_b, g_ln2 = _mm_nt_cols_rms_bwd("mm_dh2_rms2_bwd", df, p["w_ff1"], x1, p["ln2_g"], dx2, after=tok,
                                            bf16_copy=True)

    dy_att, dy_rec, d_att, d_g, dz = _branches_bwd(dx1_b, y_att, y_rec, z, p["w_att_o"], p["w_rec_o"], p["w_out"])
    (g_w_out,) = _mm_tn_rows("mm_g_out", mixed, dx1_b, tm=D)
    (g_w_att_o,) = _mm_tn_cols("mm_g_att_o", att, dy_att, D // N_CHIPS)
    (g_w_rec_o,) = _mm_tn_rows("mm_g_rec_o", g, dy_rec, tm=D)
    tok = reducer.advance("ff", g_w_rec_o) + reducer.begin("proj", dict(w_out=g_w_out, w_att_o=g_w_att_o, w_rec_o=g_w_rec_o))

    dz, ds_acc = _attn_bwd(z, probs, d_att, dz, after=tok)
    g_rpb = _rpb_grad(ds_acc)
    tok = reducer.advance("proj", ds_acc)
    dz, g_conv_w, g_conv_b, g_wa, g_wi, g_ba, g_bi, g_lam = _rec_bwd(z, d_g, rec_saved, dz, *rec_params, after=tok)

    g_w_in, g_b_in = _mm_tn_cols("mm_g_in", h1, dz, D_IN // N_CHIPS, colsum=True)
    tok = reducer.advance("in", reducer.interlude(reducer.begin("in", dict(w_in=g_w_in))))
    grad_x, g_ln1 = _mm_nt_cols_rms_bwd("mm_dh1_rms1_bwd", dz, p["w_in"], x, p["ln1_g"], dx1, after=tok)

    grads = dict(ln1_g=g_ln1, w_in=g_w_in, b_in=g_b_in, rpb=g_rpb, w_att_o=g_w_att_o, conv_w=g_conv_w,
                 conv_b=g_conv_b, w_rg_a=g_wa, b_rg_a=g_ba, w_rg_i=g_wi,
                 b_rg_i=g_bi, lru_lambda=g_lam, w_rec_o=g_w_rec_o, w_out=g_w_out, ln2_g=g_ln2,
                 w_ff1=g_w_ff1, w_ff2=g_w_ff2, lnf_g=g_lnf)
    return loss, grad_x.reshape(1, T, D), grads


_ANY = pl.BlockSpec(memory_space=pl.ANY)
N_PEERS = N_CHIPS - 1


def _place():
    x, y, c = lax.axis_index("x"), lax.axis_index("y"), lax.axis_index("c")
    peers = [(1 - x, y), (x, 1 - y), (1 - x, 1 - y)]
    return x, y, c, 2 * x + y, peers


def _remote(src, dst, send_sem, recv_sem, dev):
    return pltpu.make_async_remote_copy(src_ref=src, dst_ref=dst, send_sem=send_sem, recv_sem=recv_sem,
                                        device_id=dev, device_id_type=MESH)


def _prefetch_call(body, name, ids, grid, in_specs, out_specs, out_shape, args, semantics=None):
    spec = pltpu.PrefetchScalarGridSpec(num_scalar_prefetch=1, grid=grid, in_specs=in_specs, out_specs=out_specs)
    return pl.pallas_call(body, name=name, grid_spec=spec, out_shape=out_shape,
                          compiler_params=_params(semantics or ("parallel",) * len(grid)))(ids, *args)


def _cast_bf16(name, w, chip_id, after=()):
    rows, cols = w.shape
    rb = min(rows, 256)

    def body(ids_ref, w_ref, *rest):
        rest[-1][...] = w_ref[...].astype(BF16)

    return _prefetch_call(body, name, chip_id, (rows // rb,),
                          [pl.BlockSpec((rb, cols), lambda i, ids: (i, 0))] + [_ANY] * len(after),
                          pl.BlockSpec((None, rb, cols), lambda i, ids: (ids[0], i, 0)),
                          _sds((N_CHIPS, rows, cols), BF16), (w, *after))


def _dma_sems(*counts):
    return [pltpu.SemaphoreType.DMA((k,)) for k in counts]


_HBM = pl.BlockSpec(memory_space=pltpu.HBM)
_SEM = pl.BlockSpec(memory_space=pltpu.SEMAPHORE)
_SPLIT_COPY = pltpu.CompilerParams(has_side_effects=pltpu.SideEffectType.DATAFLOW_SIDE_EFFECTING)
SIBLING_ID = 0
_SPLIT_COPY_SIBLING = pltpu.CompilerParams(has_side_effects=pltpu.SideEffectType.DATAFLOW_SIDE_EFFECTING,
                                           collective_id=SIBLING_ID)


def _sibling_handshake():
    x, y, c = lax.axis_index("x"), lax.axis_index("y"), lax.axis_index("c")
    barrier = pltpu.get_barrier_semaphore()
    pl.semaphore_signal(barrier, inc=1, device_id=(x, y, 1 - c), device_id_type=MESH)
    pl.semaphore_wait(barrier, 1)


def _hbm(arrays):
    return [pltpu.with_memory_space_constraint(a, pltpu.HBM) for a in arrays]


def _hbm_like(arrays):
    return [pltpu.HBM(a.shape, a.dtype) for a in arrays]


def _halves(buf, c):
    half = buf.shape[1] // 2
    return pl.ds(c * half, half), pl.ds((1 - c) * half, half)


def _gather_start(name, slots):
    n = len(slots)
    nk = n * N_PEERS

    def body(*refs):
        bufs = refs[n:2 * n]
        send_sems, recv_sems, token = refs[2 * n:]
        x, y, c, chip, peers = _place()
        for t in range(n):
            mine, _ = _halves(bufs[t], c)
            for r, (px, py) in enumerate(peers):
                k = t * N_PEERS + r
                own = bufs[t].at[chip, mine]
                _remote(own, own, send_sems.at[k], recv_sems.at[k], (px, py, c)).start()
        token[...] = jnp.zeros_like(token)

    res = pl.pallas_call(
        body, name=name, in_specs=[_HBM] * n, out_specs=[_HBM] * n + [_SEM, _SEM, pl.BlockSpec(memory_space=pltpu.VMEM)],
        out_shape=_hbm_like(slots) + [pltpu.SemaphoreType.DMA((nk,)), pltpu.SemaphoreType.DMA((nk,)),
                                      _sds((8, 128), F32)],
        input_output_aliases={t: t for t in range(n)}, compiler_params=_SPLIT_COPY)(*_hbm(slots))
    return res[:n], (res[n], res[n + 1]), res[n + 2]


def _gather_wait(name, bufs, sems, after):
    n = len(bufs)
    after = tuple(after) if isinstance(after, (tuple, list)) else (after,)

    def body(*refs):
        ins = refs[:n]
        send_sems, recv_sems = refs[n], refs[n + 1]
        x, y, c, chip, peers = _place()
        for t in range(n):
            mine, _ = _halves(ins[t], c)
            for r, (px, py) in enumerate(peers):
                k = t * N_PEERS + r
                cp = _remote(ins[t].at[chip, mine], ins[t].at[2 * px + py, mine], send_sems.at[k], recv_sems.at[k],
                             (px, py, c))
                cp.wait_send()
                cp.wait_recv()

    return pl.pallas_call(
        body, name=name, in_specs=[_HBM] * n + [_SEM, _SEM] + [_ANY] * len(after), out_specs=[_HBM] * n,
        out_shape=_hbm_like(bufs), input_output_aliases={t: t for t in range(n)},
        compiler_params=_SPLIT_COPY)(*bufs, *sems, *after)


def _gather_forward(name, bufs):
    n = len(bufs)
    nk = n * N_PEERS

    def body(*refs):
        _sibling_handshake()
        outs = refs[n:2 * n]
        send_sems, recv_sems = refs[2 * n:]
        x, y, c, chip, peers = _place()
        sibling = (x, y, 1 - c)
        sends = []
        for t in range(n):
            mine, _ = _halves(outs[t], c)
            for r, (px, py) in enumerate(peers):
                k = t * N_PEERS + r
                landed = outs[t].at[2 * px + py, mine]
                sends.append(_remote(landed, landed, send_sems.at[k], recv_sems.at[k], sibling))
                sends[-1].start()
        for t in range(n):
            _, theirs = _halves(outs[t], c)
            for r, (px, py) in enumerate(peers):
                k = t * N_PEERS + r
                landed = outs[t].at[2 * px + py, theirs]
                _remote(landed, landed, send_sems.at[k], recv_sems.at[k], sibling).wait_recv()
        for cp in sends:
            cp.wait_send()

    return pl.pallas_call(
        body, name=name, in_specs=[_ANY] * n, out_specs=[_ANY] * n, out_shape=[_sds(b.shape, b.dtype) for b in bufs],
        input_output_aliases={t: t for t in range(n)}, scratch_shapes=_dma_sems(nk, nk),
        compiler_params=pltpu.CompilerParams(collective_id=SIBLING_ID))(*bufs)


def _pair_copies(n, srcs, lands, send_sems, recv_sems):
    x, y, c, _, _ = _place()
    sibling = (x, y, 1 - c)
    copies = []
    for t in range(n):
        half = srcs[t].shape[1] // 2
        for j in range(N_CHIPS):
            k = t * N_CHIPS + j
            copies.append(_remote(srcs[t].at[j, pl.ds((1 - c) * half, half)], lands[t].at[j],
                                  send_sems.at[k], recv_sems.at[k], sibling))
    for t in range(n, len(srcs)):
        k = n * N_CHIPS + t - n
        copies.append(_remote(srcs[t], lands[t], send_sems.at[k], recv_sems.at[k], sibling))
    return copies


def _pair_start(name, grads, wholes=()):
    n = len(grads)
    srcs = list(grads) + list(wholes)
    m = len(srcs)
    lands = [pltpu.HBM((N_CHIPS, g.shape[1] // 2, g.shape[2]), F32) for g in grads] + _hbm_like(wholes)
    ns = n * N_CHIPS + len(wholes)

    def body(*refs):
        _sibling_handshake()
        src_refs, land_refs = refs[m:2 * m], refs[2 * m:3 * m]
        send_sems, recv_sems, token = refs[3 * m:]
        for cp in _pair_copies(n, src_refs, land_refs, send_sems, recv_sems):
            cp.start()
        token[...] = jnp.zeros_like(token)

    res = pl.pallas_call(
        body, name=name, in_specs=[_HBM] * m,
        out_specs=[_HBM] * (2 * m) + [_SEM, _SEM, pl.BlockSpec(memory_space=pltpu.VMEM)],
        out_shape=_hbm_like(srcs) + lands + [pltpu.SemaphoreType.DMA((ns,)), pltpu.SemaphoreType.DMA((ns,)),
                                             _sds((8, 128), F32)],
        input_output_aliases={t: t for t in range(m)}, compiler_params=_SPLIT_COPY_SIBLING)(*_hbm(srcs))
    return (res[:m], res[m:2 * m], (res[2 * m], res[2 * m + 1])), res[2 * m + 2]


def _pair_wait(name, flight, n, after):
    srcs, lands, sems = flight
    m = len(srcs)

    def body(*refs):
        for cp in _pair_copies(n, refs[:m], refs[m:2 * m], refs[2 * m], refs[2 * m + 1]):
            cp.wait_send()
            cp.wait_recv()

    res = pl.pallas_call(
        body, name=name, in_specs=[_HBM] * (2 * m) + [_SEM, _SEM, _ANY], out_specs=[_HBM] * (2 * m),
        out_shape=_hbm_like(srcs) + _hbm_like(lands), input_output_aliases={t: t for t in range(2 * m)},
        compiler_params=_SPLIT_COPY)(*srcs, *lands, *sems, after)
    return res[:m], res[m:]


def _chip_copies(srcs, lands, small_src, small_land, send_sems, recv_sems):
    x, y, c, chip, peers = _place()
    n = len(srcs)
    copies = []
    for r, (px, py) in enumerate(peers):
        for t in range(n):
            k = t * N_PEERS + r
            copies.append(_remote(srcs[t].at[2 * px + py], lands[t].at[r], send_sems.at[k], recv_sems.at[k], (px, py, c)))
        if small_src is not None:
            k = n * N_PEERS + r
            half_s = small_src.shape[0] // 2
            copies.append(_remote(small_src.at[pl.ds(c * half_s, half_s)], small_land.at[r],
                                  send_sems.at[k], recv_sems.at[k], (px, py, c)))
    return copies


def _chip_start(name, sums_bf16, small=None):
    n = len(sums_bf16)
    srcs = list(sums_bf16) + ([small] if small is not None else [])
    m = len(srcs)
    lands = [pltpu.HBM((N_PEERS,) + s.shape[1:], BF16) for s in sums_bf16]
    if small is not None:
        lands.append(pltpu.HBM((N_PEERS, small.shape[0] // 2, 128), F32))
    nk = m * N_PEERS

    def body(*refs):
        src_refs, land_refs = refs[m:2 * m], refs[2 * m:3 * m]
        send_sems, recv_sems, token = refs[3 * m:]
        small_src, small_land = (src_refs[n], land_refs[n]) if small is not None else (None, None)
        for cp in _chip_copies(src_refs[:n], land_refs[:n], small_src, small_land, send_sems, recv_sems):
            cp.start()
        token[...] = jnp.zeros_like(token)

    res = pl.pallas_call(
        body, name=name, in_specs=[_HBM] * m,
        out_specs=[_HBM] * (2 * m) + [_SEM, _SEM, pl.BlockSpec(memory_space=pltpu.VMEM)],
        out_shape=_hbm_like(srcs) + lands + [pltpu.SemaphoreType.DMA((nk,)), pltpu.SemaphoreType.DMA((nk,)),
                                             _sds((8, 128), F32)],
        input_output_aliases={t: t for t in range(m)}, compiler_params=_SPLIT_COPY)(*_hbm(srcs))
    return (res[:m], res[m:2 * m], (res[2 * m], res[2 * m + 1])), res[2 * m + 2]


def _chip_wait(name, flight, with_small, after):
    srcs, lands, sems = flight
    m = len(srcs)
    n = m - 1 if with_small else m

    def body(*refs):
        src_refs, land_refs = refs[:m], refs[m:2 * m]
        send_sems, recv_sems = refs[2 * m], refs[2 * m + 1]
        small_src, small_land = (src_refs[n], land_refs[n]) if with_small else (None, None)
        for cp in _chip_copies(src_refs[:n], land_refs[:n], small_src, small_land, send_sems, recv_sems):
            cp.wait_send()
            cp.wait_recv()

    res = pl.pallas_call(
        body, name=name, in_specs=[_HBM] * (2 * m) + [_SEM, _SEM, _ANY], out_specs=[_HBM] * (2 * m),
        out_shape=_hbm_like(srcs) + _hbm_like(lands), input_output_aliases={t: t for t in range(2 * m)},
        compiler_params=_SPLIT_COPY)(*srcs, *lands, *sems, after)
    return res[:m], res[m:]


def _swap_start(name, bufs):
    n = len(bufs)

    def body(*refs):
        _sibling_handshake()
        outs = refs[n:2 * n]
        send_sems, recv_sems, token = refs[2 * n:]
        x, y, c, _, _ = _place()
        for t in range(n):
            h = outs[t].shape[0] // 2
            mine = outs[t].at[pl.ds(c * h, h)]
            _remote(mine, mine, send_sems.at[t], recv_sems.at[t], (x, y, 1 - c)).start()
        token[...] = jnp.zeros_like(token)

    res = pl.pallas_call(
        body, name=name, in_specs=[_HBM] * n, out_specs=[_HBM] * n + [_SEM, _SEM, pl.BlockSpec(memory_space=pltpu.VMEM)],
        out_shape=_hbm_like(bufs) + [pltpu.SemaphoreType.DMA((n,)), pltpu.SemaphoreType.DMA((n,)), _sds((8, 128), F32)],
        input_output_aliases={t: t for t in range(n)}, compiler_params=_SPLIT_COPY_SIBLING)(*_hbm(bufs))
    return (res[:n], (res[n], res[n + 1])), res[n + 2]


def _swap_wait(name, flight, after):
    bufs, sems = flight
    n = len(bufs)

    def body(*refs):
        ins = refs[:n]
        send_sems, recv_sems = refs[n], refs[n + 1]
        x, y, c, _, _ = _place()
        for t in range(n):
            h = ins[t].shape[0] // 2
            cp = _remote(ins[t].at[pl.ds(c * h, h)], ins[t].at[pl.ds((1 - c) * h, h)], send_sems.at[t],
                         recv_sems.at[t], (x, y, 1 - c))
            cp.wait_send()
            cp.wait_recv()

    return pl.pallas_call(
        body, name=name, in_specs=[_HBM] * n + [_SEM, _SEM, _ANY], out_specs=[_HBM] * n, out_shape=_hbm_like(bufs),
        input_output_aliases={t: t for t in range(n)}, compiler_params=_SPLIT_COPY)(*bufs, *sems, after)


def _pair_sum(name, grad, got, ids):
    _, rows, cols = got.shape
    rb = min(rows, 256)
    nb = rows // rb
    blk = pl.BlockSpec((None, rb, cols), lambda i, j, ids: (j, i, 0))
    mine = pl.BlockSpec((None, rb, cols), lambda i, j, ids: (j, ids[1] * nb + i, 0))
    own = pl.BlockSpec((rb, cols), lambda i, j, ids: (i, 0))

    def body(ids_ref, a_ref, b_ref, s_ref, sb_ref):
        s = a_ref[...] + b_ref[...]
        sb_ref[...] = s.astype(BF16)

        @pl.when(pl.program_id(1) == ids_ref[0])
        def _():
            s_ref[...] = s

    return _prefetch_call(body, name, ids, (nb, N_CHIPS), [mine, blk], [own, blk],
                          [_sds((rows, cols), F32), _sds(got.shape, BF16)], (grad, got),
                          semantics=("parallel", "arbitrary"))


def _chip_sum(name, own_sum, got, ids):
    rows, cols = own_sum.shape
    rb = min(rows, 256)
    nb = rows // rb
    own = pl.BlockSpec((rb, cols), lambda i, ids: (i, 0))
    blk3 = pl.BlockSpec((N_PEERS, rb, cols), lambda i, ids: (0, i, 0))
    out = pl.BlockSpec((rb, cols), lambda i, ids: (ids[1] * nb + i, 0))

    def body(ids_ref, a_ref, b_ref, o_ref):
        o_ref[...] = ((a_ref[...] + b_ref[0].astype(F32)) + b_ref[1].astype(F32)) + b_ref[2].astype(F32)

    return _prefetch_call(body, name, ids, (nb,), [own, blk3], out, _sds((2 * rows, cols), F32), (own_sum, got))


SMALL_RB = 280


def _small_pair_sum(own, got):
    blk = pl.BlockSpec((SMALL_RB, 128), lambda i: (i, 0))

    def body(a_ref, b_ref, o_ref):
        o_ref[...] = a_ref[...] + b_ref[...]

    return pl.pallas_call(body, name="small_pair_sum", grid=(own.shape[0] // SMALL_RB,), in_specs=[blk, blk],
                          out_specs=blk, out_shape=_sds(own.shape, F32),
                          compiler_params=_params(("parallel",)))(own, got)


def _small_chip_sum(pair, got, ids):
    nb = pair.shape[0] // 2 // SMALL_RB
    half = pl.BlockSpec((SMALL_RB, 128), lambda i, ids: (ids[1] * nb + i, 0))
    blk3 = pl.BlockSpec((N_PEERS, SMALL_RB, 128), lambda i, ids: (0, i, 0))

    def body(ids_ref, a_ref, b_ref, o_ref):
        o_ref[...] = (a_ref[...] + b_ref[1]) + (b_ref[0] + b_ref[2])

    return _prefetch_call(body, "small_chip_sum", ids, (nb,), [half, blk3], half, _sds(pair.shape, F32), (pair, got))


def _adamw_math(w, g, m, v):
    m = ADAM_B1 * m + (1.0 - ADAM_B1) * g
    v = ADAM_B2 * v + (1.0 - ADAM_B2) * (g * g)
    m_hat = m / (1.0 - ADAM_B1 ** ADAM_STEP)
    v_hat = v / (1.0 - ADAM_B2 ** ADAM_STEP)
    delta = -ADAM_LR * (m_hat / (jnp.sqrt(v_hat) + ADAM_EPS) + ADAM_WD * w)
    return delta, m, v


def _adamw(name, w, g, m, v, rb=None):
    rows, cols = w.shape
    rb = rows if rb is None else rb
    blk = pl.BlockSpec((rb, cols), lambda i: (i, 0))

    def body(w_ref, g_ref, m_ref, v_ref, d_ref, nm_ref, nv_ref):
        d, nm, nv = _adamw_math(w_ref[...], g_ref[...], m_ref[...], v_ref[...])
        d_ref[...] = d
        nm_ref[...] = nm
        nv_ref[...] = nv

    return pl.pallas_call(body, name=name, grid=(rows // rb,), in_specs=[blk] * 4, out_specs=[blk] * 3,
                          out_shape=[_sds(w.shape, F32)] * 3, compiler_params=_params(("parallel",)))(w, g, m, v)


def _adamw_small(ws, gs, ms, vs):
    n = len(ws)

    def body(*refs):
        for t in range(n):
            w_ref, g_ref, m_ref, v_ref = (refs[k * n + t] for k in range(4))
            d, nm, nv = _adamw_math(w_ref[...], g_ref[...], m_ref[...], v_ref[...])
            for k, val in enumerate((d, nm, nv)):
                refs[(4 + k) * n + t][...] = val

    res = pl.pallas_call(body, name="adamw_small", out_shape=[_sds(a.shape, F32) for a in ws] * 3,
                         compiler_params=_params())(*ws, *gs, *ms, *vs)
    return [(res[t], res[n + t], res[2 * n + t]) for t in range(n)]


BIG = ("w_in", "w_att_o", "w_rec_o", "w_out", "w_ff1", "w_ff2")
SHARDED_VECS = ("conv_w", "b_rg_a", "b_rg_i", "lru_lambda")
SMALL = ("ln1_g", "b_in", "rpb", "conv_w", "conv_b", "w_rg_a", "b_rg_a", "w_rg_i", "b_rg_i", "lru_lambda",
         "ln2_g", "lnf_g")
SMALL_ROWS = 2240
ORDER = ("ln1_g", "w_in", "b_in", "rpb", "w_att_o", "conv_w", "conv_b", "w_rg_a", "b_rg_a", "w_rg_i", "b_rg_i",
         "lru_lambda", "w_rec_o", "w_out", "ln2_g", "w_ff1", "w_ff2", "lnf_g")


def _pack_small(grads, loss):
    parts, sizes = [], {}
    for n in SMALL:
        flat = grads[n].reshape(-1)
        pad = (-flat.shape[0]) % 128
        sizes[n] = (flat.shape[0], flat.shape[0] + pad)
        parts.append(jnp.pad(flat, (0, pad)))
    total = sum(s[1] for s in sizes.values())
    parts.append(jnp.pad(loss.reshape(1), (0, SMALL_ROWS * 128 - total - 1)))
    return jnp.concatenate(parts).reshape(SMALL_ROWS, 128), sizes


def _unpack_small(buf, sizes, shapes):
    flat = buf.reshape(-1)
    out, pos = {}, 0
    for n in SMALL:
        size, padded = sizes[n]
        out[n] = flat[pos:pos + size].reshape(shapes[n])
        pos += padded
    return out, flat[pos]


def _gather_weights(w, chip):
    chip_id = chip.astype(jnp.int32).reshape(1)
    vec_rows = [w[n][0] for n in SHARDED_VECS]
    vec_shard = jnp.concatenate(vec_rows + [jnp.zeros((16 - 10, D // N_CHIPS), F32)], axis=0)
    vec_slots = lax.dynamic_update_slice(jnp.zeros((N_CHIPS, 16, D // N_CHIPS), F32), vec_shard[None], (chip, 0, 0))
    bufs_a, sems_a, token_a = _gather_start("gather_start_first", [_cast_bf16("cast_w_in", w["w_in"][0], chip_id), vec_slots])
    rest_names = BIG[1:]
    bufs_b, sems_b, token_b = _gather_start(
        "gather_start_rest", [_cast_bf16("cast_" + n, w[n][0], chip_id, after=(token_a,)) for n in rest_names])

    def first(after):
        w_in_full, vec_full = _gather_forward("gather_forward_first", _gather_wait("gather_wait_first", bufs_a, sems_a, after))
        vecs = vec_full.transpose(1, 0, 2).reshape(16, D)
        return dict(w_in=w_in_full, conv_w=vecs[0:4], b_rg_a=vecs[4:6], b_rg_i=vecs[6:8], lru_lambda=vecs[8:10])

    def rest(after):
        full = dict(zip(rest_names, _gather_forward("gather_forward_rest",
                                                    _gather_wait("gather_wait_rest", bufs_b, sems_b, after))))
        return dict(w_att_o=full["w_att_o"], w_ff1=full["w_ff1"], w_rec_o=full["w_rec_o"].reshape(D, D),
                    w_out=full["w_out"].reshape(D, D), w_ff2=full["w_ff2"].reshape(D_FF, D))

    p = dict(ln1_g=w["ln1_g"], b_in=w["b_in"], rpb=w["rpb"][0], conv_b=w["conv_b"], w_rg_a=w["w_rg_a"][0],
             w_rg_i=w["w_rg_i"][0], ln2_g=w["ln2_g"], lnf_g=w["lnf_g"].reshape(1, D))
    return p, ((token_b,), first, rest)


class _Reducer:
    def __init__(self, ids):
        self.ids = ids
        self.groups = {}

    def begin(self, tag, grads, small=None):
        names = list(grads)
        big = [grads[n].reshape(N_CHIPS, -1, grads[n].shape[-1]) for n in names]
        flight, token = _pair_start("pair_start_" + tag, big, [] if small is None else [small])
        self.groups[tag] = dict(names=names, pair=flight, small=small is not None)
        return (token,)

    def advance(self, tag, after):
        grp = self.groups[tag]
        n = len(grp["names"])
        mine, got = _pair_wait("pair_wait_" + tag, grp["pair"], n, after)
        sums = [_pair_sum("pair_sum_" + name, a, b, self.ids) for name, a, b in zip(grp["names"], mine, got)]
        small_sum = _small_pair_sum(mine[n], got[n]) if grp["small"] else None
        grp["chip"], token = _chip_start("chip_start_" + tag, [s[1] for s in sums], small_sum)
        grp["sums"] = [s[0] for s in sums]
        self.last_token = token
        return (token,)

    def finish(self, tag, after):
        grp = self.groups[tag]
        srcs, lands = _chip_wait("chip_wait_" + tag, grp["chip"], grp["small"], after)
        halves = [_chip_sum("chip_sum_" + name, s, b, self.ids) for name, s, b in zip(grp["names"], grp["sums"], lands)]
        if grp["small"]:
            halves.append(_small_chip_sum(srcs[-1], lands[-1], self.ids))
        grp["swap"], token = _swap_start("swap_start_" + tag, halves)
        return token

    def interlude(self, tokens):
        after = tokens[0]
        for tag in list(self.groups)[:-1]:
            after = self.finish(tag, after)
        return after

    def result(self, tag, after):
        return _swap_wait("swap_wait_" + tag, self.groups[tag]["swap"], after)


def kernel(x, ln1_g, w_in, b_in, rpb, w_att_o, conv_w, conv_b, w_rg_a, b_rg_a, w_rg_i, b_rg_i, lru_lambda, w_rec_o, w_out, ln2_g, w_ff1, w_ff2, lnf_g, loss_target, m_ln1_g, m_w_in, m_b_in, m_rpb, m_w_att_o, m_conv_w, m_conv_b, m_w_rg_a, m_b_rg_a, m_w_rg_i, m_b_rg_i, m_lru_lambda, m_w_rec_o, m_w_out, m_ln2_g, m_w_ff1, m_w_ff2, m_lnf_g, v_ln1_g, v_w_in, v_b_in, v_rpb, v_w_att_o, v_conv_w, v_conv_b, v_w_rg_a, v_b_rg_a, v_w_rg_i, v_b_rg_i, v_lru_lambda, v_w_rec_o, v_w_out, v_ln2_g, v_w_ff1, v_w_ff2, v_lnf_g):
    w = dict(ln1_g=ln1_g, w_in=w_in, b_in=b_in, rpb=rpb, w_att_o=w_att_o, conv_w=conv_w, conv_b=conv_b,
             w_rg_a=w_rg_a, b_rg_a=b_rg_a, w_rg_i=w_rg_i, b_rg_i=b_rg_i, lru_lambda=lru_lambda, w_rec_o=w_rec_o,
             w_out=w_out, ln2_g=ln2_g, w_ff1=w_ff1, w_ff2=w_ff2, lnf_g=lnf_g)
    m = dict(ln1_g=m_ln1_g, w_in=m_w_in, b_in=m_b_in, rpb=m_rpb, w_att_o=m_w_att_o, conv_w=m_conv_w,
             conv_b=m_conv_b, w_rg_a=m_w_rg_a, b_rg_a=m_b_rg_a, w_rg_i=m_w_rg_i, b_rg_i=m_b_rg_i,
             lru_lambda=m_lru_lambda, w_rec_o=m_w_rec_o, w_out=m_w_out, ln2_g=m_ln2_g, w_ff1=m_w_ff1,
             w_ff2=m_w_ff2, lnf_g=m_lnf_g)
    v = dict(ln1_g=v_ln1_g, w_in=v_w_in, b_in=v_b_in, rpb=v_rpb, w_att_o=v_w_att_o, conv_w=v_conv_w,
             conv_b=v_conv_b, w_rg_a=v_w_rg_a, b_rg_a=v_b_rg_a, w_rg_i=v_w_rg_i, b_rg_i=v_b_rg_i,
             lru_lambda=v_lru_lambda, w_rec_o=v_w_rec_o, w_out=v_w_out, ln2_g=v_ln2_g, w_ff1=v_w_ff1,
             w_ff2=v_w_ff2, lnf_g=v_lnf_g)
    chip = 2 * lax.axis_index("x") + lax.axis_index("y")
    ids = jnp.stack([chip, lax.axis_index("c")]).astype(jnp.int32)

    out_grad, out_delta, out_m, out_v = {}, {}, {}, {}

    def update(n, gn):
        shape, two_d = w[n].shape, gn.shape
        d, nm, nv = _adamw("adamw_" + n, w[n].reshape(two_d), gn, m[n].reshape(two_d), v[n].reshape(two_d), 256)
        out_grad[n], out_delta[n], out_m[n], out_v[n] = (gn.reshape(shape), d.reshape(shape), nm.reshape(shape),
                                                         nv.reshape(shape))
        return d

    reducer = _Reducer(ids)
    p, late = _gather_weights(w, chip)
    loss, grad_x, g = _local_step(x, loss_target, p, late, reducer)
    small, sizes = _pack_small(g, loss + reducer.last_token[:1, :1])
    after = reducer.begin("small", {}, small)[0]
    for tag in ("ff", "proj", "in"):
        for n, red in zip(reducer.groups[tag]["names"], reducer.result(tag, after)):
            after = update(n, red)
        if tag == "ff":
            after = reducer.finish("in", reducer.advance("small", after)[0])
    (small_red,) = reducer.result("small", reducer.finish("small", after))
    gsmall, loss = _unpack_small(small_red, sizes, {n: g[n].shape for n in SMALL})
    two_d = {n: (int(np.prod(w[n].shape[:-1])), w[n].shape[-1]) for n in SMALL}
    for n in SHARDED_VECS:
        gsmall[n] = lax.dynamic_slice_in_dim(gsmall[n], chip * (D // N_CHIPS), D // N_CHIPS, axis=1)
    gs = [gsmall[n].reshape(two_d[n]) for n in SMALL]
    updates = _adamw_small([w[n].reshape(two_d[n]) for n in SMALL], gs, [m[n].reshape(two_d[n]) for n in SMALL],
                           [v[n].reshape(two_d[n]) for n in SMALL])
    for n, gn, (d, nm, nv) in zip(SMALL, gs, updates):
        shape = w[n].shape
        out_grad[n], out_delta[n], out_m[n], out_v[n] = (gn.reshape(shape), d.reshape(shape), nm.reshape(shape),
                                                         nv.reshape(shape))
    return (loss, grad_x, *[out_grad[n] for n in ORDER], *[out_delta[n] for n in ORDER],
            *[out_m[n] for n in ORDER], *[out_v[n] for n in ORDER])
```

```python
import numpy as np
import jax
import jax.numpy as jnp
from jax import lax
from jax.experimental import pallas as pl
from jax.experimental.pallas import tpu as pltpu

F32 = jnp.float32
BF16 = jnp.bfloat16

T = 2048
D = 1024
D_ATT = 512
D_IN = 5632
D_FF = 4096
N_HEADS = 8
HEAD_DIM = 64
GRID_W = 64
N_ROWS = T // GRID_W
WIN_H = 8
WIN_W = 16
KEYS = WIN_H * GRID_W
N_CHIPS = 4
EPS = 1e-6
LRU_C = 8.0
SCALE = HEAD_DIM ** -0.5
REC_CB = 256
REC_BLOCK = 64
REC_CHUNK = 256
PAD = 8

ADAM_LR = 0.001
ADAM_B1 = 0.9
ADAM_B2 = 0.999
ADAM_EPS = 1e-08
ADAM_WD = 0.01
ADAM_STEP = 10

VMEM_LIMIT = 56 * 1024 * 1024

NN = (((1,), (0,)), ((), ()))
NT = (((1,), (1,)), ((), ()))
TN = (((0,), (0,)), ((), ()))
MESH = pl.DeviceIdType.MESH


def _params(sem=None):
    return pltpu.CompilerParams(dimension_semantics=sem, vmem_limit_bytes=VMEM_LIMIT)


def _dot(a, b, dims):
    return lax.dot_general(a, b, dims, preferred_element_type=F32)


def _sigmoid(x):
    return 0.5 * jnp.tanh(0.5 * x) + 0.5


def _matmul(name, a, b, *, dims, grid, a_spec, b_spec, out_shapes, out_specs, acc_shape,
            extras=(), extra_specs=(), epilogue=None, colsum_spec=None, colsum_shape=None, after=(),
            semantics=("parallel", "parallel", "arbitrary"), epilogue_takes_first=False):
    nk = grid[2]
    n_extra = len(extras)
    n_out = len(out_shapes)
    with_colsum = colsum_spec is not None

    def body(a_ref, b_ref, *rest):
        ex = rest[:n_extra]
        rest = rest[:n_extra] + rest[n_extra + len(after):]
        outs = rest[n_extra:n_extra + n_out]
        pos = n_extra + n_out
        cs_out = rest[pos] if with_colsum else None
        pos += 1 if with_colsum else 0
        acc = rest[pos]
        cs_acc = rest[pos + 1] if with_colsum else None
        k = pl.program_id(2)
        first_tile = pl.program_id(0) == 0

        @pl.when(k == 0)
        def _():
            acc[...] = jnp.zeros_like(acc)
            if with_colsum:
                cs_acc[...] = jnp.zeros_like(cs_acc)

        bv = b_ref[...]
        acc[...] += _dot(a_ref[...].astype(BF16), bv.astype(BF16), dims)
        if with_colsum:
            cs_acc[...] += jnp.sum(bv.astype(F32), axis=0, keepdims=True)

        @pl.when(k == nk - 1)
        def _():
            r = acc[...]
            if epilogue is None:
                outs[0][...] = r.astype(outs[0].dtype)
            elif epilogue_takes_first:
                epilogue(r, ex, outs, first_tile)
            else:
                epilogue(r, ex, outs)
            if with_colsum:
                cs_out[...] = cs_acc[...]

    shapes = list(out_shapes)
    specs = list(out_specs)
    scratch = [pltpu.VMEM(acc_shape, F32)]
    if with_colsum:
        shapes.append(colsum_shape)
        specs.append(colsum_spec)
        scratch.append(pltpu.VMEM((1, acc_shape[1]), F32))
    res = pl.pallas_call(
        body, name=name, grid=grid,
        in_specs=[a_spec, b_spec, *extra_specs] + [_ANY] * len(after),
        out_specs=specs, out_shape=shapes, scratch_shapes=scratch,
        compiler_params=_params(semantics),
    )(a, b, *extras, *after)
    return res


def _sds(shape, dtype):
    return jax.ShapeDtypeStruct(shape, dtype)


TM = 1024
NI = T // TM
TJ = T
NJ = T // TJ


def _mm_nn_cols(name, a, wg, out_dtype, *, bias=None, extras=(), extra_specs=(), epilogue=None,
                out_shapes=None, out_specs=None):
    k_dim, n4 = wg.shape[1], wg.shape[2]
    ex, exs = list(extras), list(extra_specs)
    if bias is not None:
        ex = [bias] + ex
        exs = [pl.BlockSpec((1, n4), lambda j, i, k: (0, j))] + exs
        user_ep = epilogue

        def epilogue(r, e, outs):
            r = r + e[0][...]
            if user_ep is None:
                outs[0][...] = r.astype(outs[0].dtype)
            else:
                user_ep(r, e[1:], outs)
    if out_shapes is None:
        out_shapes = [_sds((T, N_CHIPS * n4), out_dtype)]
        out_specs = [pl.BlockSpec((TJ, n4), lambda j, i, k: (i, j))]
    return _matmul(
        name, a, wg, dims=NN, grid=(N_CHIPS, NJ, 1),
        a_spec=pl.BlockSpec((TJ, k_dim), lambda j, i, k: (i, 0)),
        b_spec=pl.BlockSpec((None, k_dim, n4), lambda j, i, k: (j, 0, 0)),
        out_shapes=out_shapes, out_specs=out_specs, acc_shape=(TJ, n4),
        extras=ex, extra_specs=exs, epilogue=epilogue)


def _mm_nt_cols_rms_bwd(name, a, wg, x, g, dres, after=(), bf16_copy=False):
    n4 = wg.shape[2]
    row = pl.BlockSpec((TM, D), lambda i, j, k: (i, 0))
    vec = pl.BlockSpec((1, D), lambda i, j, k: (0, 0))

    def epilogue(dhv, ex, outs, first):
        x_ref, g_ref, dres_ref = ex
        dx_ref, dg_ref = outs[0], outs[-1]
        xv = x_ref[...]
        rstd = lax.rsqrt(jnp.mean(xv * xv, axis=-1, keepdims=True) + EPS)
        xhat = xv * rstd
        dy = dhv * g_ref[...]
        dx = dres_ref[...] + rstd * (dy - xhat * jnp.mean(dy * xhat, axis=-1, keepdims=True))
        dx_ref[...] = dx
        if bf16_copy:
            outs[1][...] = dx.astype(BF16)
        part = jnp.sum(dhv * xhat, axis=0, keepdims=True)

        @pl.when(first)
        def _():
            dg_ref[...] = part

        @pl.when(jnp.logical_not(first))
        def _():
            dg_ref[...] += part

    return _matmul(
        name, a, wg, dims=NT, grid=(NI, 1, N_CHIPS),
        a_spec=pl.BlockSpec((TM, n4), lambda i, j, k: (i, k)),
        b_spec=pl.BlockSpec((None, D, n4), lambda i, j, k: (k, 0, 0)),
        out_shapes=[_sds((T, D), F32)] + [_sds((T, D), BF16)] * bf16_copy + [_sds((1, D), F32)],
        out_specs=[row] + [row] * bf16_copy + [vec], acc_shape=(TM, D),
        extras=[x, g, dres], extra_specs=[row, vec, row], epilogue=epilogue, after=after,
        semantics=("arbitrary", "arbitrary", "arbitrary"), epilogue_takes_first=True)


def _mm_nt_rows(name, a, w, out_dtype, *, tn, extras=(), extra_specs=(), epilogue=None):
    k_dim, n = w.shape
    return _matmul(
        name, a, w, dims=NT, grid=(k_dim // tn, NJ, 1),
        a_spec=pl.BlockSpec((TJ, n), lambda j, i, k: (i, 0)),
        b_spec=pl.BlockSpec((tn, n), lambda j, i, k: (j, 0)),
        out_shapes=[_sds((T, k_dim), out_dtype)],
        out_specs=[pl.BlockSpec((TJ, tn), lambda j, i, k: (i, j))], acc_shape=(TJ, tn),
        extras=extras, extra_specs=extra_specs, epilogue=epilogue)


def _mm_tn_cols(name, a, g, n4, *, colsum=False):
    k_dim = a.shape[1]
    kw = {}
    if colsum:
        kw = dict(colsum_spec=pl.BlockSpec((1, n4), lambda j, i, k: (0, j)),
                  colsum_shape=_sds((1, N_CHIPS * n4), F32))
    return _matmul(
        name, a, g, dims=TN, grid=(N_CHIPS, 1, NJ),
        a_spec=pl.BlockSpec((TJ, k_dim), lambda j, i, k: (k, 0)),
        b_spec=pl.BlockSpec((TJ, n4), lambda j, i, k: (k, j)),
        out_shapes=[_sds((N_CHIPS, k_dim, n4), F32)],
        out_specs=[pl.BlockSpec((None, k_dim, n4), lambda j, i, k: (j, 0, 0))],
        acc_shape=(k_dim, n4), **kw)


def _mm_tn_rows(name, a, g, *, tm):
    k_dim, n = a.shape[1], g.shape[1]
    return _matmul(
        name, a, g, dims=TN, grid=(k_dim // tm, 1, NJ),
        a_spec=pl.BlockSpec((TJ, tm), lambda j, i, k: (k, j)),
        b_spec=pl.BlockSpec((TJ, n), lambda j, i, k: (k, 0)),
        out_shapes=[_sds((k_dim, n), F32)],
        out_specs=[pl.BlockSpec((tm, n), lambda j, i, k: (j, 0))], acc_shape=(tm, n))


TE = 256
NE = T // TE
_ROW = pl.BlockSpec((TE, D), lambda i: (i, 0))
_VEC = pl.BlockSpec((1, D), lambda i: (0, 0))


def _rms_fwd(name, x, g, after=()):
    def body(x_ref, g_ref, *rest):
        h_ref = rest[-1]
        xv = x_ref[...]
        rstd = lax.rsqrt(jnp.mean(xv * xv, axis=-1, keepdims=True) + EPS)
        h_ref[...] = (xv * rstd * g_ref[...]).astype(BF16)

    return pl.pallas_call(body, name=name, grid=(NE,), in_specs=[_ROW, _VEC] + [_ANY] * len(after), out_specs=_ROW,
                          out_shape=_sds((T, D), BF16), compiler_params=_params(("parallel",)))(x, g, *after)


def _mm_x2_loss_head(s, w_ff2, x1, target, g):
    k_dim = w_ff2.shape[0]
    row = pl.BlockSpec((TM, D), lambda i, j, k: (i, 0))
    vec = pl.BlockSpec((1, D), lambda i, j, k: (0, 0))

    def epilogue(r, ex, outs, first):
        x1_ref, t_ref, g_ref = ex
        loss_ref, dx_ref, dxb_ref, dg_ref = outs
        xv = x1_ref[...] + r
        rstd = lax.rsqrt(jnp.mean(xv * xv, axis=-1, keepdims=True) + EPS)
        xhat = xv * rstd
        gv = g_ref[...]
        err = xhat * gv - t_ref[...]
        dy = err * (1.0 / D)
        dxh = dy * gv
        dx = rstd * (dxh - xhat * jnp.mean(dxh * xhat, axis=-1, keepdims=True))
        dx_ref[...] = dx
        dxb_ref[...] = dx.astype(BF16)
        dg_part = jnp.sum(dy * xhat, axis=0, keepdims=True)
        loss_part = (0.5 / D) * jnp.sum(jnp.sum(err * err, axis=1, keepdims=True), axis=0, keepdims=True)

        @pl.when(first)
        def _():
            dg_ref[...] = dg_part
            loss_ref[...] = loss_part

        @pl.when(jnp.logical_not(first))
        def _():
            dg_ref[...] += dg_part
            loss_ref[...] += loss_part

    return _matmul(
        "mm_x2_loss_head", s, w_ff2, dims=NN, grid=(NI, 1, k_dim // D),
        a_spec=pl.BlockSpec((TM, D), lambda i, j, k: (i, k)), b_spec=pl.BlockSpec((D, D), lambda i, j, k: (k, 0)),
        out_shapes=[_sds((1, 1), F32), _sds((T, D), F32), _sds((T, D), BF16), _sds((1, D), F32)],
        out_specs=[pl.BlockSpec((1, 1), lambda i, j, k: (0, 0)), row, row, vec], acc_shape=(TM, D),
        extras=[x1, target, g], extra_specs=[row, row, vec], epilogue=epilogue,
        semantics=("arbitrary", "arbitrary", "arbitrary"), epilogue_takes_first=True)


DZ_Q, DZ_K, DZ_V, DZ_U, DZ_Y, DZ_G_ATT, DZ_G_REC = 0, 512, 1024, 1536, 2560, 3584, 4608
MW = 512
_G_ATT_BLK = 3584 // MW
_G_REC_BLK = 4608 // MW


TB = 512


def _branch_specs():
    def row(cols):
        return pl.BlockSpec((TB, cols), lambda i: (i, 0))

    ga = pl.BlockSpec((TB, MW), lambda i: (i, _G_ATT_BLK))
    ga2 = pl.BlockSpec((TB, MW), lambda i: (i, _G_ATT_BLK + 1))
    gr = pl.BlockSpec((TB, MW), lambda i: (i, _G_REC_BLK))
    gr2 = pl.BlockSpec((TB, MW), lambda i: (i, _G_REC_BLK + 1))
    w_att = pl.BlockSpec((N_CHIPS, D_ATT, D // N_CHIPS), lambda i: (0, 0, 0))
    w_sq = pl.BlockSpec((D, D), lambda i: (0, 0))
    return row, (ga, ga2, gr, gr2), w_att, w_sq


def _gate_values(gate_refs):
    ga, ga2, gr, gr2 = (r[...].astype(F32) for r in gate_refs)
    return _sigmoid(jnp.concatenate([ga, ga2], axis=1)), _sigmoid(jnp.concatenate([gr, gr2], axis=1))


def _branches_fwd(att, g, z, x, w_att_o, w_rec_o, w_out, ln2_g):
    row, gate_specs, w_att, w_sq = _branch_specs()

    def body(att_ref, g_ref, ga_ref, ga2_ref, gr_ref, gr2_ref, x_ref, wa_ref, wr_ref, wo_ref, g2_ref,
             ya_ref, yr_ref, m_ref, x1_ref, h2_ref):
        attv = att_ref[...]
        ya = jnp.concatenate([_dot(attv, wa_ref[j], NN) for j in range(N_CHIPS)], axis=1)
        yr = _dot(g_ref[...], wr_ref[...], NN)
        sa, sr = _gate_values((ga_ref, ga2_ref, gr_ref, gr2_ref))
        mixed = (sa * ya + sr * yr).astype(BF16)
        ya_ref[...] = ya
        yr_ref[...] = yr
        m_ref[...] = mixed
        x1 = x_ref[...] + _dot(mixed, wo_ref[...], NN)
        x1_ref[...] = x1
        rstd = lax.rsqrt(jnp.mean(x1 * x1, axis=-1, keepdims=True) + EPS)
        h2_ref[...] = (x1 * rstd * g2_ref[...]).astype(BF16)

    return pl.pallas_call(
        body, name="branches_fwd", grid=(T // TB,),
        in_specs=[row(D_ATT), row(D), *gate_specs, row(D), w_att, w_sq, w_sq, pl.BlockSpec((1, D), lambda i: (0, 0))],
        out_specs=[row(D)] * 5,
        out_shape=[_sds((T, D), F32), _sds((T, D), F32), _sds((T, D), BF16), _sds((T, D), F32), _sds((T, D), BF16)],
        compiler_params=_params(("parallel",)))(att, g, z, z, z, z, x, w_att_o, w_rec_o, w_out, ln2_g)


def _branches_bwd(dx1_b, y_att, y_rec, z, w_att_o, w_rec_o, w_out):
    row, gate_specs, w_att, w_sq = _branch_specs()
    n4 = D // N_CHIPS

    def body(dx_ref, ya_ref, yr_ref, ga_ref, ga2_ref, gr_ref, gr2_ref, wa_ref, wr_ref, wo_ref,
             dya_ref, dyr_ref, datt_ref, dg_ref, dz_ref, dga_ref, dgr_ref, sems):
        rows = pl.ds(pl.multiple_of(pl.program_id(0) * TB, TB), TB)
        dm = _dot(dx_ref[...], wo_ref[...], NT)
        sa, sr = _gate_values((ga_ref, ga2_ref, gr_ref, gr2_ref))
        dya = (dm * sa).astype(BF16)
        dyr = (dm * sr).astype(BF16)
        dya_ref[...] = dya
        dyr_ref[...] = dyr
        dga_ref[...] = (dm * ya_ref[...] * sa * (1.0 - sa)).astype(BF16)
        dgr_ref[...] = (dm * yr_ref[...] * sr * (1.0 - sr)).astype(BF16)
        copies = [pltpu.make_async_copy(dga_ref, dz_ref.at[rows, pl.ds(DZ_G_ATT, D)], sems.at[0]),
                  pltpu.make_async_copy(dgr_ref, dz_ref.at[rows, pl.ds(DZ_G_REC, D)], sems.at[1])]
        for cp in copies:
            cp.start()
        datt = _dot(dya[:, 0:n4], wa_ref[0], NT)
        for j in range(1, N_CHIPS):
            datt = datt + _dot(dya[:, j * n4:(j + 1) * n4], wa_ref[j], NT)
        datt_ref[...] = datt.astype(BF16)
        dg_ref[...] = _dot(dyr, wr_ref[...], NT).astype(BF16)
        for cp in copies:
            cp.wait()

    return pl.pallas_call(
        body, name="branches_bwd", grid=(T // TB,),
        in_specs=[row(D), row(D), row(D), *gate_specs, w_att, w_sq, w_sq],
        out_specs=[row(D)] * 2 + [row(D_ATT), row(D), _ANY],
        out_shape=[_sds((T, D), BF16)] * 2 + [_sds((T, D_ATT), BF16), _sds((T, D), BF16), _sds((T, D_IN), BF16)],
        scratch_shapes=[pltpu.VMEM((TB, D), BF16), pltpu.VMEM((TB, D), BF16), pltpu.SemaphoreType.DMA((2,))],
        compiler_params=_params(("parallel",)))(dx1_b, y_att, y_rec, z, z, z, z, w_att_o, w_rec_o, w_out)


HP = 2 * HEAD_DIM
N_HP = N_HEADS // 2
ATT_UNROLL_FWD = 16
ATT_UNROLL_BWD = 16
DIAG_ROWS = 32


def _window_maps():
    diag = np.zeros((GRID_W * GRID_W, 128), np.float32)
    for qc in range(GRID_W):
        w0 = min(max(qc - WIN_W // 2, 0), GRID_W - WIN_W)
        for kc in range(w0, w0 + WIN_W):
            diag[qc * GRID_W + kc, kc - qc + WIN_W - 1] = 1.0
    return diag, diag.sum(axis=1)[None, :]


def _split3(x):
    a = x.astype(BF16)
    r = x - a.astype(F32)
    b = r.astype(BF16)
    c = (r - b.astype(F32)).astype(BF16)
    return a, b, c


N_DROW = 2 * WIN_H - 1
N_DPAIR = N_DROW - 1


def _bias_pairs(rpb):
    diag, valid = _window_maps()
    r2 = jnp.pad(rpb.reshape(N_HEADS * N_DROW, 2 * WIN_W - 1),
                 ((0, 128 - N_HEADS * N_DROW), (0, 128 - (2 * WIN_W - 1))))

    def body(r_ref, d_ref, v_ref, o_ref):
        dv = d_ref[...]
        t = sum(_dot(part, dv, NN) for part in _split3(r_ref[...]))
        o_ref[...] = jnp.where(v_ref[...] > 0.0, t, -1e30)

    t = pl.pallas_call(body, name="rpb_expand", out_shape=_sds((128, GRID_W * GRID_W), F32),
                       compiler_params=_params())(r2, jnp.asarray(diag.T, BF16), jnp.asarray(valid, F32))
    t = t[:N_HEADS * N_DROW].reshape(N_HEADS, N_DROW, GRID_W, GRID_W)
    return jnp.concatenate([t[:, :N_DPAIR], t[:, 1:]], axis=-1)


def _row_bias(tb_ref, hh, d0):
    return jnp.concatenate([tb_ref[hh, d0 + 2 * ii] for ii in range(WIN_H // 2)], axis=1)


def _row_window(r):
    rs = jnp.clip(r - WIN_H // 2, 0, N_ROWS - WIN_H)
    return pl.multiple_of(r * GRID_W, GRID_W), pl.multiple_of(rs * GRID_W, GRID_W), rs - r + (WIN_H - 1)


def _split_heads(src_ref, dst_ref, scale=None):
    for hh in range(2):
        v = src_ref[:, hh * HEAD_DIM:(hh + 1) * HEAD_DIM]
        dst_ref[hh] = (v if scale is None else v * scale).astype(BF16)


def _attn_items(qb_ref, kb_ref, vb_ref, tb_ref, first_row, n_rows):
    wins = [_row_window(first_row + u) for u in range(n_rows)]
    items = [(u, hh) for u in range(n_rows) for hh in range(2)]
    q = [qb_ref[hh, pl.ds(wins[u][0], GRID_W), :] for u, hh in items]
    k = [kb_ref[hh, pl.ds(wins[u][1], KEYS), :] for u, hh in items]
    v = [vb_ref[hh, pl.ds(wins[u][1], KEYS), :] for u, hh in items]
    s = [_dot(qi, ki, NT) + _row_bias(tb_ref, hh, wins[u][2]) for qi, ki, (u, hh) in zip(q, k, items)]
    m = [jnp.max(si, axis=-1, keepdims=True) for si in s]
    e = [jnp.exp(si - mi) for si, mi in zip(s, m)]
    inv = [1.0 / jnp.sum(ei, axis=-1, keepdims=True) for ei in e]
    p = [ei * li for ei, li in zip(e, inv)]
    return wins, items, q, k, v, p


def _attn_in_specs():
    q = pl.BlockSpec((T, HP), lambda p: (0, p))
    k = pl.BlockSpec((T, HP), lambda p: (0, N_HP + p))
    v = pl.BlockSpec((T, HP), lambda p: (0, 2 * N_HP + p))
    tb = pl.BlockSpec((2, N_DPAIR, GRID_W, HP), lambda p: (p, 0, 0, 0))
    return q, k, v, tb


_HEAD_SCRATCH = pltpu.VMEM((2, T, HEAD_DIM), BF16)


_PROBS = pl.BlockSpec((T, 2 * KEYS), lambda p: (0, p))


def _attn_fwd(z, tb):
    def body(q_ref, k_ref, v_ref, tb_ref, o_ref, p_ref, qb_ref, kb_ref, vb_ref):
        _split_heads(q_ref, qb_ref, SCALE)
        _split_heads(k_ref, kb_ref)
        _split_heads(v_ref, vb_ref)

        def rows(it, carry):
            wins, items, _, _, v, p = _attn_items(qb_ref, kb_ref, vb_ref, tb_ref, it * ATT_UNROLL_FWD, ATT_UNROLL_FWD)
            pb = [pi.astype(BF16) for pi in p]
            o = [_dot(pi, vi, NN) for pi, vi in zip(pb, v)]
            for u, (q0, _, _) in enumerate(wins):
                o_ref[pl.ds(q0, GRID_W), :] = jnp.concatenate(o[2 * u:2 * u + 2], axis=1).astype(BF16)
                p_ref[pl.ds(q0, GRID_W), :] = jnp.concatenate(pb[2 * u:2 * u + 2], axis=1)
            return carry

        lax.fori_loop(0, N_ROWS // ATT_UNROLL_FWD, rows, 0)

    blk = pl.BlockSpec((T, HP), lambda p: (0, p))
    return pl.pallas_call(
        body, name="attn_fwd", grid=(N_HP,), in_specs=list(_attn_in_specs()), out_specs=[blk, _PROBS],
        out_shape=[_sds((T, D_ATT), BF16), _sds((T, N_HEADS * KEYS), BF16)], scratch_shapes=[_HEAD_SCRATCH] * 3,
        compiler_params=_params(("parallel",)))(z, z, z, tb)


def _attn_bwd(z, probs, d_att, dz, after=()):
    def body(q_ref, k_ref, v_ref, p_ref, do_ref, flip_ref, dz_in_ref, *rest):
        (dz_ref, diag_ref, qb_ref, kb_ref, vb_ref, dob_ref, dka_ref, dva_ref, ds_ref,
         dq_ref, dk_ref, dv_ref, sems) = rest[len(after):]
        _split_heads(q_ref, qb_ref, SCALE)
        _split_heads(k_ref, kb_ref)
        _split_heads(v_ref, vb_ref)
        _split_heads(do_ref, dob_ref)
        dka_ref[...] = jnp.zeros_like(dka_ref)
        dva_ref[...] = jnp.zeros_like(dva_ref)
        ds_ref[...] = jnp.zeros_like(ds_ref)

        def rows(it, carry):
            wins = [_row_window(it * ATT_UNROLL_BWD + u) for u in range(ATT_UNROLL_BWD)]
            items = [(u, hh) for u in range(ATT_UNROLL_BWD) for hh in range(2)]
            q = [qb_ref[hh, pl.ds(wins[u][0], GRID_W), :] for u, hh in items]
            k = [kb_ref[hh, pl.ds(wins[u][1], KEYS), :] for u, hh in items]
            v = [vb_ref[hh, pl.ds(wins[u][1], KEYS), :] for u, hh in items]
            pb = [p_ref[pl.ds(wins[u][0], GRID_W), hh * KEYS:(hh + 1) * KEYS] for u, hh in items]
            p = [pi.astype(F32) for pi in pb]
            do = [dob_ref[hh, pl.ds(wins[u][0], GRID_W), :] for u, hh in items]
            dv = [_dot(pi, di, TN) for pi, di in zip(pb, do)]
            dp = [_dot(di, vi, NT) for di, vi in zip(do, v)]
            ds = [pi * (dpi - jnp.sum(dpi * pi, axis=-1, keepdims=True)) for pi, dpi in zip(p, dp)]
            dsb = [d.astype(BF16) for d in ds]
            dq = [_dot(d, ki, NN) * SCALE for d, ki in zip(dsb, k)]
            dk = [_dot(d, qi, TN) for d, qi in zip(dsb, q)]
            for d, (u, hh) in zip(ds, items):
                for ii in range(WIN_H // 2):
                    ds_ref[hh, wins[u][2] + 2 * ii] += d[:, ii * HP:(ii + 1) * HP]
            for u, (q0, _, _) in enumerate(wins):
                dq_ref[pl.ds(q0, GRID_W), :] = jnp.concatenate(dq[2 * u:2 * u + 2], axis=1).astype(BF16)
            for dki, dvi, (u, hh) in zip(dk, dv, items):
                dka_ref[hh, pl.ds(wins[u][1], KEYS), :] += dki
                dva_ref[hh, pl.ds(wins[u][1], KEYS), :] += dvi
            return carry

        lax.fori_loop(0, N_ROWS // ATT_UNROLL_BWD, rows, 0)
        dk_ref[...] = jnp.concatenate([dka_ref[0], dka_ref[1]], axis=1).astype(BF16)
        dv_ref[...] = jnp.concatenate([dva_ref[0], dva_ref[1]], axis=1).astype(BF16)
        cols = pl.multiple_of(pl.program_id(0) * HP, HP)
        copies = [pltpu.make_async_copy(src, dz_ref.at[:, pl.ds(base + cols, HP)], sems.at[t])
                  for t, (src, base) in enumerate(((dq_ref, DZ_Q), (dk_ref, DZ_K), (dv_ref, DZ_V)))]
        for cp in copies:
            cp.start()
        _diag_sums(ds_ref, flip_ref, diag_ref)
        for cp in copies:
            cp.wait()

    blk = pl.BlockSpec((T, HP), lambda p: (0, p))
    q, k, v, _ = _attn_in_specs()
    flip =jnp.asarray(np.eye(HP, dtype=np.float32)[::-1], BF16)
    return pl.pallas_call(
        body, name="attn_bwd", grid=(N_HP,),
        in_specs=[q, k, v, _PROBS, blk, pl.BlockSpec((HP, HP), lambda p: (0, 0)), _ANY] + [_ANY] * len(after),
        out_specs=[_ANY, pl.BlockSpec((None, DIAG_ROWS, HP), lambda p: (p, 0, 0))],
        out_shape=[_sds(dz.shape, dz.dtype), _sds((N_HP, DIAG_ROWS, HP), F32)], input_output_aliases={6: 0},
        scratch_shapes=[_HEAD_SCRATCH] * 4 + [pltpu.VMEM((2, T, HEAD_DIM), F32), pltpu.VMEM((2, T, HEAD_DIM), F32),
                                              pltpu.VMEM((2, N_DPAIR, GRID_W, HP), F32)]
        + [pltpu.VMEM((T, HP), BF16)] * 3 + [pltpu.SemaphoreType.DMA((3,))],
        compiler_params=_params(("parallel",)))(z, z, z, probs, d_att, flip, dz, *after)


def _diag_sums(acc_ref, flip_ref, out_ref):
    flip = flip_ref[...]
    rows = []
    for hh in range(2):
        for pair in range(N_DPAIR):
            reversed_lanes = sum(_dot(part, flip, NN) for part in _split3(acc_ref[hh, pair]))
            skewed = pltpu.roll(reversed_lanes, 0, 1, stride=1, stride_axis=0)
            rows.append(jnp.sum(skewed, axis=0, keepdims=True))
    rows.append(jnp.zeros((DIAG_ROWS - len(rows), HP), F32))
    out_ref[...] = jnp.concatenate(rows, axis=0)


def _rpb_grad(diag_sums):
    g = diag_sums.reshape(N_HP * DIAG_ROWS, HP)
    sel = np.zeros((2, 128, N_HP * DIAG_ROWS), np.float32)
    lane = np.zeros((2, HP, 128), np.float32)
    for h in range(N_HEADS):
        for pair in range(N_DPAIR):
            for half in range(2):
                sel[half, h * N_DROW + pair + half, (h // 2) * DIAG_ROWS + (h % 2) * N_DPAIR + pair] = 1.0
    for j in range(2 * WIN_W - 1):
        for half in range(2):
            lane[half, (HP - 1 - GRID_W * half - (j - (WIN_W - 1))) % HP, j] = 1.0

    def body(g_ref, sel_ref, lane_ref, o_ref):
        parts = _split3(g_ref[...])
        total = None
        for half in range(2):
            picked = sum(_dot(sel_ref[half], part, NN) for part in parts)
            term = sum(_dot(part, lane_ref[half], NN) for part in _split3(picked))
            total = term if total is None else total + term
        o_ref[...] = total

    out = pl.pallas_call(body, name="rpb_grad", out_shape=_sds((128, 128), F32),
                         compiler_params=_params())(g, jnp.asarray(sel, BF16), jnp.asarray(lane, BF16))
    return out[:N_HEADS * N_DROW, :2 * WIN_W - 1].reshape(N_HEADS, N_DROW, 2 * WIN_W - 1)


N_CB = D // REC_CB
N_CHUNK = T // REC_CHUNK
DUP_PIECES = 4
N_TILE = T // 8
_U_BLK = 1536 // REC_CB
_Y_BLK = 2560 // REC_CB


def _block_diag(w):
    per = REC_CB // 64
    wt = w.reshape(2, N_CB, per, 64, 64)
    eye = jnp.eye(per, dtype=w.dtype)
    full = wt[:, :, :, :, None, :] * eye[None, None, :, None, :, None]
    return full.reshape(2, N_CB, REC_CB, REC_CB).astype(BF16)


def _gelu(x):
    c = 0.7978845608028654
    return 0.5 * x * (1.0 + jnp.tanh(c * (x + 0.044715 * x * x * x)))


def _gelu_grad(x):
    c = 0.7978845608028654
    th = jnp.tanh(c * (x + 0.044715 * x * x * x))
    return 0.5 * (1.0 + th) + 0.5 * x * (1.0 - th * th) * c * (1.0 + 3.0 * 0.044715 * x * x)


def _softplus_neg(lam):
    x = -lam
    e = jnp.exp(-jnp.abs(x))
    w = 1.0 + e
    l1p = jnp.where(w == 1.0, e, jnp.log(w) * e / (w - 1.0))
    return jnp.maximum(x, 0.0) + l1p


def _one_minus_exp(x, exp_x):
    poly = x * (1.0 + x * (1 / 2 + x * (1 / 6 + x * (1 / 24 + x * (1 / 120 + x * (1 / 720))))))
    return jnp.where(x > -0.125, -poly, 1.0 - exp_x)


def _conv_taps(pad_ref, t0, w, sign):
    out = None
    for j in range(4):
        term = w[j:j + 1, :] * pad_ref[pl.ds(PAD + t0 + sign * (j - 2), REC_CHUNK), :]
        out = term if out is None else out + term
    return out


def _gates(u, wa, wi, ba, bi, sp):
    ub = u.astype(BF16)
    r = _sigmoid(_dot(ub, wa, NN) + ba)
    i = _sigmoid(_dot(ub, wi, NN) + bi)
    log_a = (-LRU_C * sp) * r
    a = jnp.exp(log_a)
    x = jnp.maximum(_one_minus_exp(2.0 * log_a, a * a), 0.0)
    positive = x > 0.0
    inv = lax.rsqrt(jnp.where(positive, x, 1.0))
    mult = jnp.where(positive, x * inv, 0.0)
    return r, i, a, mult, jnp.where(positive, inv, 0.0)


def _tile_scan(a, b, sub, reverse):
    for s in (1, 2, 4):
        if reverse:
            a_s, b_s, m = pltpu.roll(a, 8 - s, 0), pltpu.roll(b, 8 - s, 0), sub < 8 - s
        else:
            a_s, b_s, m = pltpu.roll(a, s, 0), pltpu.roll(b, s, 0), sub >= s
        b = jnp.where(m, a * b_s + b, b)
        a = jnp.where(m, a * a_s, a)
    return a, b


def _last_row(x, row):
    return jnp.broadcast_to(x[row:row + 1, :], x.shape)


def _rec_prologue(up_ref, cw_ref, cb_ref, wa_ref, wi_ref, ba_ref, bi_ref, lam_ref,
                  upad_ref, u_ref, a_refs, h_refs):
    cb = up_ref.shape[1]
    zeros = jnp.zeros((PAD, cb), F32)
    upad_ref[pl.ds(0, PAD), :] = zeros
    upad_ref[pl.ds(PAD + T, PAD), :] = zeros
    upad_ref[pl.ds(PAD, T), :] = up_ref[...].astype(F32)
    cw = cw_ref[...]
    sp = _softplus_neg(lam_ref[...])
    for c in range(N_CHUNK):
        t0 = c * REC_CHUNK
        u = cb_ref[...] + _conv_taps(upad_ref, t0, cw, 1)
        u_ref[pl.ds(t0, REC_CHUNK), :] = u
        for d in range(2):
            _, i, a, mult, _ = _gates(u, wa_ref[d], wi_ref[d], ba_ref[d:d + 1, :], bi_ref[d:d + 1, :], sp[d:d + 1, :])
            a_refs[d][pl.ds(t0, REC_CHUNK), :] = a
            h_refs[d][pl.ds(t0, REC_CHUNK), :] = mult * (i * u)

    sub = lax.broadcasted_iota(jnp.int32, (8, cb), 0)

    def tile(k, carry):
        cf, cr = carry
        tf = pl.multiple_of(k * 8, 8)
        tr = pl.multiple_of((N_TILE - 1 - k) * 8, 8)
        af, bf = _tile_scan(a_refs[0][pl.ds(tf, 8), :], h_refs[0][pl.ds(tf, 8), :], sub, False)
        hf = af * cf + bf
        h_refs[0][pl.ds(tf, 8), :] = hf
        ar, br = _tile_scan(a_refs[1][pl.ds(tr, 8), :], h_refs[1][pl.ds(tr, 8), :], sub, True)
        hr = ar * cr + br
        h_refs[1][pl.ds(tr, 8), :] = hr
        return _last_row(af, 7) * cf + _last_row(bf, 7), _last_row(ar, 0) * cr + _last_row(br, 0)

    z8 = jnp.zeros((8, cb), F32)
    lax.fori_loop(0, N_TILE, tile, (z8, z8), unroll=2)
    return sp


def _rec_specs():
    up = pl.BlockSpec((T, REC_CB), lambda c: (0, _U_BLK + c))
    yb = pl.BlockSpec((T, REC_CB), lambda c: (0, _Y_BLK + c))
    cw = pl.BlockSpec((4, REC_CB), lambda c: (0, c))
    cbias = pl.BlockSpec((1, REC_CB), lambda c: (0, c))
    wbd = pl.BlockSpec((2, None, REC_CB, REC_CB), lambda c: (0, c, 0, 0))
    vec2 = pl.BlockSpec((2, REC_CB), lambda c: (0, c))
    col = pl.BlockSpec((T, REC_CB), lambda c: (0, c))
    return up, yb, cw, cbias, wbd, vec2, col


def _rec_fwd(z, conv_w, conv_b, wa, wi, ba, bi, lam):
    up, yb, cw, cbias, wbd, vec2, col = _rec_specs()

    def body(up_ref, yb_ref, cw_ref, cb_ref, wa_ref, wi_ref, ba_ref, bi_ref, lam_ref, g_ref,
             u_ref, af_ref, ar_ref, hf_ref, hr_ref, upad_ref):
        _rec_prologue(up_ref, cw_ref, cb_ref, wa_ref, wi_ref, ba_ref, bi_ref, lam_ref,
                      upad_ref, u_ref, (af_ref, ar_ref), (hf_ref, hr_ref))

        def chunk(c, carry):
            t0 = pl.multiple_of(c * REC_CHUNK, REC_CHUNK)
            rows = pl.ds(t0, REC_CHUNK)
            g_ref[rows, :] = ((hf_ref[rows, :] + hr_ref[rows, :]) * _gelu(yb_ref[rows, :].astype(F32))).astype(BF16)
            return carry

        lax.fori_loop(0, N_CHUNK, chunk, 0)

    res = pl.pallas_call(
        body, name="rec_fwd", grid=(N_CB,),
        in_specs=[up, yb, cw, cbias, wbd, wbd, vec2, vec2, vec2], out_specs=[col] * 6,
        out_shape=[_sds((T, D), BF16)] + [_sds((T, D), F32)] * 5,
        scratch_shapes=[pltpu.VMEM((T + 2 * PAD, REC_CB), F32)],
        compiler_params=_params(("parallel",)))(z, z, conv_w, conv_b, wa, wi, ba, bi, lam)
    return res[0], tuple(res[1:])


def _rec_bwd(z, dg, saved, dz, conv_w, conv_b, wa, wi, ba, bi, lam, after=()):
    up, yb, cw, cbias, wbd, vec2, col = _rec_specs()

    def body(up_ref, yb_ref, dg_ref, u_ref, af_ref, ar_ref, hf_ref, hr_ref,
             cw_ref, cb_ref, wa_ref, wi_ref, ba_ref, bi_ref, lam_ref, dz_in_ref, *rest):
        (dz_ref, dcw_ref, dcb_ref, dwa_out, dwi_out, dba_ref, dbi_ref, dlam_ref,
         upad_ref, dh_ref, gf_ref, gr_ref, daf_ref, dar_ref, dupad_ref, dwa_ref, dwi_ref,
         dup_ref, dyb_ref, sems) = rest[len(after):]
        g_refs, da_refs = (gf_ref, gr_ref), (daf_ref, dar_ref)
        cb = up_ref.shape[1]
        zeros = jnp.zeros((PAD, cb), F32)
        upad_ref[pl.ds(0, PAD), :] = zeros
        upad_ref[pl.ds(PAD + T, PAD), :] = zeros
        upad_ref[pl.ds(PAD, T), :] = up_ref[...].astype(F32)
        sp = _softplus_neg(lam_ref[...])

        def gate_chunk(c, carry):
            t0 = pl.multiple_of(c * REC_CHUNK, REC_CHUNK)
            rows = pl.ds(t0, REC_CHUNK)
            y = yb_ref[rows, :].astype(F32)
            dgv = dg_ref[rows, :].astype(F32)
            dh_ref[rows, :] = dgv * _gelu(y)
            dyb_ref[rows, :] = (dgv * (hf_ref[rows, :] + hr_ref[rows, :]) * _gelu_grad(y)).astype(BF16)
            return carry

        lax.fori_loop(0, N_CHUNK, gate_chunk, 0)
        cols = pl.multiple_of(pl.program_id(0) * REC_CB, REC_CB)
        dyb_copy = pltpu.make_async_copy(dyb_ref, dz_ref.at[:, pl.ds(DZ_Y + cols, REC_CB)], sems.at[1])
        dyb_copy.start()

        sub = lax.broadcasted_iota(jnp.int32, (8, cb), 0)

        def tile(k, carry):
            cf, cr = carry
            kf = N_TILE - 1 - k
            tf = pl.multiple_of(kf * 8, 8)
            tnext = pl.multiple_of(jnp.minimum(kf + 1, N_TILE - 1) * 8, 8)
            tprev = pl.multiple_of(jnp.maximum(kf - 1, 0) * 8, 8)
            a_t = af_ref[pl.ds(tf, 8), :]
            a_n = jnp.where(kf < N_TILE - 1, af_ref[pl.ds(tnext, 8), :], 0.0)
            a_sh = jnp.where(sub == 7, pltpu.roll(a_n, 7, 0), pltpu.roll(a_t, 7, 0))
            ca, cbb = _tile_scan(a_sh, dh_ref[pl.ds(tf, 8), :], sub, True)
            gf = ca * cf + cbb
            h_t = hf_ref[pl.ds(tf, 8), :]
            h_p = jnp.where(kf > 0, hf_ref[pl.ds(tprev, 8), :], 0.0)
            h_sh = jnp.where(sub == 0, pltpu.roll(h_p, 1, 0), pltpu.roll(h_t, 1, 0))
            gf_ref[pl.ds(tf, 8), :] = gf
            daf_ref[pl.ds(tf, 8), :] = gf * h_sh
            tr = pl.multiple_of(k * 8, 8)
            rnext = pl.multiple_of(jnp.minimum(k + 1, N_TILE - 1) * 8, 8)
            rprev = pl.multiple_of(jnp.maximum(k - 1, 0) * 8, 8)
            b_t = ar_ref[pl.ds(tr, 8), :]
            b_p = jnp.where(k > 0, ar_ref[pl.ds(rprev, 8), :], 0.0)
            b_sh = jnp.where(sub == 0, pltpu.roll(b_p, 1, 0), pltpu.roll(b_t, 1, 0))
            ra, rb = _tile_scan(b_sh, dh_ref[pl.ds(tr, 8), :], sub, False)
            gr = ra * cr + rb
            hr_t = hr_ref[pl.ds(tr, 8), :]
            hr_n = jnp.where(k < N_TILE - 1, hr_ref[pl.ds(rnext, 8), :], 0.0)
            hr_sh = jnp.where(sub == 7, pltpu.roll(hr_n, 7, 0), pltpu.roll(hr_t, 7, 0))
            gr_ref[pl.ds(tr, 8), :] = gr
            dar_ref[pl.ds(tr, 8), :] = gr * hr_sh
            return _last_row(gf, 0), _last_row(gr, 7)

        z8 = jnp.zeros((8, cb), F32)
        lax.fori_loop(0, N_TILE, tile, (z8, z8), unroll=2)

        dupad_ref[pl.ds(0, PAD), :] = zeros
        dupad_ref[pl.ds(PAD + T, PAD), :] = zeros
        dwa_ref[...] = jnp.zeros_like(dwa_ref)
        dwi_ref[...] = jnp.zeros_like(dwi_ref)
        dba_ref[...] = jnp.zeros_like(dba_ref)
        dbi_ref[...] = jnp.zeros_like(dbi_ref)
        dlam_ref[...] = jnp.zeros_like(dlam_ref)

        def grad_chunk(c, carry):
            t0 = pl.multiple_of(c * REC_CHUNK, REC_CHUNK)
            rows = pl.ds(t0, REC_CHUNK)
            u = u_ref[rows, :]
            ub = u.astype(BF16)
            du = jnp.zeros((REC_CHUNK, cb), F32)
            for d in range(2):
                r, i, a, mult, inv_mult = _gates(u, wa_ref[d], wi_ref[d], ba_ref[d:d + 1, :], bi_ref[d:d + 1, :],
                                                 sp[d:d + 1, :])
                dbx = g_refs[d][rows, :]
                dmult = dbx * (i * u)
                diu = dbx * mult
                a2 = a * a
                dlog = da_refs[d][rows, :] * a - dmult * (a2 * inv_mult)
                dpa = (dlog * (-LRU_C) * sp[d:d + 1, :]) * r * (1.0 - r)
                dpi = (diu * u) * i * (1.0 - i)
                dpab, dpib = dpa.astype(BF16), dpi.astype(BF16)
                du = du + diu * i + _dot(dpab, wa_ref[d], NT) + _dot(dpib, wi_ref[d], NT)
                dwa_ref[d] += _dot(ub, dpab, TN)
                dwi_ref[d] += _dot(ub, dpib, TN)
                dba_ref[d:d + 1, :] += jnp.sum(dpa, axis=0, keepdims=True)
                dbi_ref[d:d + 1, :] += jnp.sum(dpi, axis=0, keepdims=True)
                dlam_ref[d:d + 1, :] += jnp.sum(dlog * r, axis=0, keepdims=True)
            dupad_ref[pl.ds(PAD + t0, REC_CHUNK), :] = du
            return carry

        lax.fori_loop(0, N_CHUNK, grad_chunk, 0)
        cw = cw_ref[...]
        dcb = jnp.zeros((1, cb), F32)
        dcw = [jnp.zeros((1, cb), F32) for _ in range(4)]
        dup_copies = []
        per_piece = N_CHUNK // DUP_PIECES
        for c in range(N_CHUNK):
            t0 = c * REC_CHUNK
            du = dupad_ref[pl.ds(PAD + t0, REC_CHUNK), :]
            dcb = dcb + jnp.sum(du, axis=0, keepdims=True)
            for j in range(4):
                dcw[j] = dcw[j] + jnp.sum(du * upad_ref[pl.ds(PAD + t0 + j - 2, REC_CHUNK), :], axis=0, keepdims=True)
            dup_ref[pl.ds(t0, REC_CHUNK), :] = _conv_taps(dupad_ref, t0, cw, -1).astype(BF16)
            if (c + 1) % per_piece == 0:
                piece = c // per_piece
                rows = pl.ds(piece * per_piece * REC_CHUNK, per_piece * REC_CHUNK)
                dup_copies.append(pltpu.make_async_copy(
                    dup_ref.at[rows, :], dz_ref.at[rows, pl.ds(DZ_U + cols, REC_CB)], sems.at[2 + piece]))
                dup_copies[-1].start()
        dcb_ref[...] = dcb
        dcw_ref[...] = jnp.concatenate(dcw, axis=0)
        dlam_ref[...] = dlam_ref[...] * (LRU_C * _sigmoid(-lam_ref[...]))
        for d in range(2):
            for blk in range(REC_CB // REC_BLOCK):
                lo, hi = blk * REC_BLOCK, (blk + 1) * REC_BLOCK
                dwa_out[d, blk] = dwa_ref[d, lo:hi, lo:hi]
                dwi_out[d, blk] = dwi_ref[d, lo:hi, lo:hi]
        for cp in dup_copies:
            cp.wait()
        dyb_copy.wait()

    full = pltpu.VMEM((T, REC_CB), F32)
    padded = pltpu.VMEM((T + 2 * PAD, REC_CB), F32)
    per = REC_CB // REC_BLOCK
    diag = pl.BlockSpec((2, per, REC_BLOCK, REC_BLOCK), lambda c: (0, c, 0, 0))
    return pl.pallas_call(
        body, name="rec_bwd", grid=(N_CB,),
        in_specs=[up, yb] + [col] * 6 + [cw, cbias, wbd, wbd, vec2, vec2, vec2, _ANY] + [_ANY] * len(after),
        out_specs=[_ANY, cw, cbias, diag, diag, vec2, vec2, vec2],
        out_shape=[_sds(dz.shape, dz.dtype), _sds((4, D), F32), _sds((1, D), F32),
                   _sds((2, D // REC_BLOCK, REC_BLOCK, REC_BLOCK), F32), _sds((2, D // REC_BLOCK, REC_BLOCK, REC_BLOCK), F32),
                   _sds((2, D), F32), _sds((2, D), F32), _sds((2, D), F32)],
        input_output_aliases={15: 0},
        scratch_shapes=[padded, full, full, full, full, full, padded,
                        pltpu.VMEM((2, REC_CB, REC_CB), F32), pltpu.VMEM((2, REC_CB, REC_CB), F32),
                        pltpu.VMEM((T, REC_CB), BF16), pltpu.VMEM((T, REC_CB), BF16),
                        pltpu.SemaphoreType.DMA((2 + DUP_PIECES,))],
        compiler_params=_params(("parallel",)))(z, z, dg, *saved, conv_w, conv_b, wa, wi, ba, bi, lam, dz, *after)


class _NoReducer:
    def begin(self, tag, grads):
        return ()

    def advance(self, tag, after):
        return ()

    def interlude(self, tokens):
        return None


def _local_step(x, target, p, late=None, reducer=_NoReducer()):
    x = x.reshape(T, D)
    target = target.reshape(T, D)
    tb = _bias_pairs(p["rpb"])
    wa, wi = _block_diag(p["w_rg_a"]), _block_diag(p["w_rg_i"])

    h1 = _rms_fwd("rms1_fwd", x, p["ln1_g"], after=late[0] if late else ())
    if late:
        p = {**p, **late[1]((h1, tb, wa, wi))}
    rec_params = (p["conv_w"], p["conv_b"], wa, wi, p["b_rg_a"], p["b_rg_i"], p["lru_lambda"])
    (z,) = _mm_nn_cols("mm_z", h1, p["w_in"], BF16, bias=p["b_in"])
    att, probs = _attn_fwd(z, tb)
    g, rec_saved = _rec_fwd(z, *rec_params)
    if late:
        p = {**p, **late[2](g)}
    y_att, y_rec, mixed, x1, h2 = _branches_fwd(att, g, z, x, p["w_att_o"], p["w_rec_o"], p["w_out"], p["ln2_g"])

    def relu2(r, ex, outs):
        rp = jnp.maximum(r, 0.0)
        outs[0][...] = (rp * rp).astype(BF16)

    (s,) = _mm_nn_cols("mm_ff1", h2, p["w_ff1"], BF16, epilogue=relu2)
    loss, dx2, dx2_b, g_lnf = _mm_x2_loss_head(s, p["w_ff2"], x1, target, p["lnf_g"])

    def relu2_bwd(r, ex, outs):
        outs[0][...] = (r * 2.0 * jnp.sqrt(ex[0][...].astype(F32))).astype(BF16)

    (df,) = _mm_nt_rows("mm_df", dx2_b, p["w_ff2"], BF16, tn=D, extras=[s],
                        extra_specs=[pl.BlockSpec((TJ, D), lambda j, i, k: (i, j))], epilogue=relu2_bwd)
    (g_w_ff2,) = _mm_tn_rows("mm_g_ff2", s, dx2_b, tm=D)
    (g_w_ff1,) = _mm_tn_cols("mm_g_ff1", h2, df, D)
    tok = reducer.begin("ff", dict(w_ff2=g_w_ff2, w_ff1=g_w_ff1))
    dx1, dx1_b, g_ln2 = _mm_nt_cols_rms_bwd("mm_dh2_rms2_bwd", df, p["w_ff1"], x1, p["ln2_g"], dx2, after=tok,
                                            bf16_copy=True)

    dy_att, dy_rec, d_att, d_g, dz = _branches_bwd(dx1_b, y_att, y_rec, z, p["w_att_o"], p["w_rec_o"], p["w_out"])
    (g_w_out,) = _mm_tn_rows("mm_g_out", mixed, dx1_b, tm=D)
    (g_w_att_o,) = _mm_tn_cols("mm_g_att_o", att, dy_att, D // N_CHIPS)
    (g_w_rec_o,) = _mm_tn_rows("mm_g_rec_o", g, dy_rec, tm=D)
    tok = reducer.advance("ff", g_w_rec_o) + reducer.begin("proj", dict(w_out=g_w_out, w_att_o=g_w_att_o, w_rec_o=g_w_rec_o))

    dz, ds_acc = _attn_bwd(z, probs, d_att, dz, after=tok)
    g_rpb = _rpb_grad(ds_acc)
    tok = reducer.advance("proj", ds_acc)
    dz, g_conv_w, g_conv_b, g_wa, g_wi, g_ba, g_bi, g_lam = _rec_bwd(z, d_g, rec_saved, dz, *rec_params, after=tok)

    g_w_in, g_b_in = _mm_tn_cols("mm_g_in", h1, dz, D_IN // N_CHIPS, colsum=True)
    tok = reducer.advance("in", reducer.interlude(reducer.begin("in", dict(w_in=g_w_in))))
    grad_x, g_ln1 = _mm_nt_cols_rms_bwd("mm_dh1_rms1_bwd", dz, p["w_in"], x, p["ln1_g"], dx1, after=tok)

    grads = dict(ln1_g=g_ln1, w_in=g_w_in, b_in=g_b_in, rpb=g_rpb, w_att_o=g_w_att_o, conv_w=g_conv_w,
                 conv_b=g_conv_b, w_rg_a=g_wa, b_rg_a=g_ba, w_rg_i=g_wi,
                 b_rg_i=g_bi, lru_lambda=g_lam, w_rec_o=g_w_rec_o, w_out=g_w_out, ln2_g=g_ln2,
                 w_ff1=g_w_ff1, w_ff2=g_w_ff2, lnf_g=g_lnf)
    return loss, grad_x.reshape(1, T, D), grads


_ANY = pl.BlockSpec(memory_space=pl.ANY)
N_PEERS = N_CHIPS - 1


def _place():
    x, y, c = lax.axis_index("x"), lax.axis_index("y"), lax.axis_index("c")
    peers = [(1 - x, y), (x, 1 - y), (1 - x, 1 - y)]
    return x, y, c, 2 * x + y, peers


def _remote(src, dst, send_sem, recv_sem, dev):
    return pltpu.make_async_remote_copy(src_ref=src, dst_ref=dst, send_sem=send_sem, recv_sem=recv_sem,
                                        device_id=dev, device_id_type=MESH)


def _prefetch_call(body, name, ids, grid, in_specs, out_specs, out_shape, args, semantics=None):
    spec = pltpu.PrefetchScalarGridSpec(num_scalar_prefetch=1, grid=grid, in_specs=in_specs, out_specs=out_specs)
    return pl.pallas_call(body, name=name, grid_spec=spec, out_shape=out_shape,
                          compiler_params=_params(semantics or ("parallel",) * len(grid)))(ids, *args)


def _cast_bf16(name, w, chip_id, after=()):
    rows, cols = w.shape
    rb = min(rows, 256)

    def body(ids_ref, w_ref, *rest):
        rest[-1][...] = w_ref[...].astype(BF16)

    return _prefetch_call(body, name, chip_id, (rows // rb,),
                          [pl.BlockSpec((rb, cols), lambda i, ids: (i, 0))] + [_ANY] * len(after),
                          pl.BlockSpec((None, rb, cols), lambda i, ids: (ids[0], i, 0)),
                          _sds((N_CHIPS, rows, cols), BF16), (w, *after))


def _dma_sems(*counts):
    return [pltpu.SemaphoreType.DMA((k,)) for k in counts]


_HBM = pl.BlockSpec(memory_space=pltpu.HBM)
_SEM = pl.BlockSpec(memory_space=pltpu.SEMAPHORE)
_SPLIT_COPY = pltpu.CompilerParams(has_side_effects=pltpu.SideEffectType.DATAFLOW_SIDE_EFFECTING)
SIBLING_ID = 0
_SPLIT_COPY_SIBLING = pltpu.CompilerParams(has_side_effects=pltpu.SideEffectType.DATAFLOW_SIDE_EFFECTING,
                                           collective_id=SIBLING_ID)


def _sibling_handshake():
    x, y, c = lax.axis_index("x"), lax.axis_index("y"), lax.axis_index("c")
    barrier = pltpu.get_barrier_semaphore()
    pl.semaphore_signal(barrier, inc=1, device_id=(x, y, 1 - c), device_id_type=MESH)
    pl.semaphore_wait(barrier, 1)


def _hbm(arrays):
    return [pltpu.with_memory_space_constraint(a, pltpu.HBM) for a in arrays]


def _hbm_like(arrays):
    return [pltpu.HBM(a.shape, a.dtype) for a in arrays]


def _halves(buf, c):
    half = buf.shape[1] // 2
    return pl.ds(c * half, half), pl.ds((1 - c) * half, half)


def _gather_start(name, slots):
    n = len(slots)
    nk = n * N_PEERS

    def body(*refs):
        bufs = refs[n:2 * n]
        send_sems, recv_sems, token = refs[2 * n:]
        x, y, c, chip, peers = _place()
        for t in range(n):
            mine, _ = _halves(bufs[t], c)
            for r, (px, py) in enumerate(peers):
                k = t * N_PEERS + r
                own = bufs[t].at[chip, mine]
                _remote(own, own, send_sems.at[k], recv_sems.at[k], (px, py, c)).start()
        token[...] = jnp.zeros_like(token)

    res = pl.pallas_call(
        body, name=name, in_specs=[_HBM] * n, out_specs=[_HBM] * n + [_SEM, _SEM, pl.BlockSpec(memory_space=pltpu.VMEM)],
        out_shape=_hbm_like(slots) + [pltpu.SemaphoreType.DMA((nk,)), pltpu.SemaphoreType.DMA((nk,)),
                                      _sds((8, 128), F32)],
        input_output_aliases={t: t for t in range(n)}, compiler_params=_SPLIT_COPY)(*_hbm(slots))
    return res[:n], (res[n], res[n + 1]), res[n + 2]


def _gather_wait(name, bufs, sems, after):
    n = len(bufs)
    after = tuple(after) if isinstance(after, (tuple, list)) else (after,)

    def body(*refs):
        ins = refs[:n]
        send_sems, recv_sems = refs[n], refs[n + 1]
        x, y, c, chip, peers = _place()
        for t in range(n):
            mine, _ = _halves(ins[t], c)
            for r, (px, py) in enumerate(peers):
                k = t * N_PEERS + r
                cp = _remote(ins[t].at[chip, mine], ins[t].at[2 * px + py, mine], send_sems.at[k], recv_sems.at[k],
                             (px, py, c))
                cp.wait_send()
                cp.wait_recv()

    return pl.pallas_call(
        body, name=name, in_specs=[_HBM] * n + [_SEM, _SEM] + [_ANY] * len(after), out_specs=[_HBM] * n,
        out_shape=_hbm_like(bufs), input_output_aliases={t: t for t in range(n)},
        compiler_params=_SPLIT_COPY)(*bufs, *sems, *after)


def _gather_forward(name, bufs):
    n = len(bufs)
    nk = n * N_PEERS

    def body(*refs):
        _sibling_handshake()
        outs = refs[n:2 * n]
        send_sems, recv_sems = refs[2 * n:]
        x, y, c, chip, peers = _place()
        sibling = (x, y, 1 - c)
        sends = []
        for t in range(n):
            mine, _ = _halves(outs[t], c)
            for r, (px, py) in enumerate(peers):
                k = t * N_PEERS + r
                landed = outs[t].at[2 * px + py, mine]
                sends.append(_remote(landed, landed, send_sems.at[k], recv_sems.at[k], sibling))
                sends[-1].start()
        for t in range(n):
            _, theirs = _halves(outs[t], c)
            for r, (px, py) in enumerate(peers):
                k = t * N_PEERS + r
                landed = outs[t].at[2 * px + py, theirs]
                _remote(landed, landed, send_sems.at[k], recv_sems.at[k], sibling).wait_recv()
        for cp in sends:
            cp.wait_send()

    return pl.pallas_call(
        body, name=name, in_specs=[_ANY] * n, out_specs=[_ANY] * n, out_shape=[_sds(b.shape, b.dtype) for b in bufs],
        input_output_aliases={t: t for t in range(n)}, scratch_shapes=_dma_sems(nk, nk),
        compiler_params=pltpu.CompilerParams(collective_id=SIBLING_ID))(*bufs)


def _pair_copies(n, srcs, lands, send_sems, recv_sems):
    x, y, c, _, _ = _place()
    sibling = (x, y, 1 - c)
    copies = []
    for t in range(n):
        half = srcs[t].shape[1] // 2
        for j in range(N_CHIPS):
            k = t * N_CHIPS + j
            copies.append(_remote(srcs[t].at[j, pl.ds((1 - c) * half, half)], lands[t].at[j],
                                  send_sems.at[k], recv_sems.at[k], sibling))
    for t in range(n, len(srcs)):
        k = n * N_CHIPS + t - n
        copies.append(_remote(srcs[t], lands[t], send_sems.at[k], recv_sems.at[k], sibling))
    return copies


def _pair_start(name, grads, wholes=()):
    n = len(grads)
    srcs = list(grads) + list(wholes)
    m = len(srcs)
    lands = [pltpu.HBM((N_CHIPS, g.shape[1] // 2, g.shape[2]), F32) for g in grads] + _hbm_like(wholes)
    ns = n * N_CHIPS + len(wholes)

    def body(*refs):
        _sibling_handshake()
        src_refs, land_refs = refs[m:2 * m], refs[2 * m:3 * m]
        send_sems, recv_sems, token = refs[3 * m:]
        for cp in _pair_copies(n, src_refs, land_refs, send_sems, recv_sems):
            cp.start()
        token[...] = jnp.zeros_like(token)

    res = pl.pallas_call(
        body, name=name, in_specs=[_HBM] * m,
        out_specs=[_HBM] * (2 * m) + [_SEM, _SEM, pl.BlockSpec(memory_space=pltpu.VMEM)],
        out_shape=_hbm_like(srcs) + lands + [pltpu.SemaphoreType.DMA((ns,)), pltpu.SemaphoreType.DMA((ns,)),
                                             _sds((8, 128), F32)],
        input_output_aliases={t: t for t in range(m)}, compiler_params=_SPLIT_COPY_SIBLING)(*_hbm(srcs))
    return (res[:m], res[m:2 * m], (res[2 * m], res[2 * m + 1])), res[2 * m + 2]


def _pair_wait(name, flight, n, after):
    srcs, lands, sems = flight
    m = len(srcs)

    def body(*refs):
        for cp in _pair_copies(n, refs[:m], refs[m:2 * m], refs[2 * m], refs[2 * m + 1]):
            cp.wait_send()
            cp.wait_recv()

    res = pl.pallas_call(
        body, name=name, in_specs=[_HBM] * (2 * m) + [_SEM, _SEM, _ANY], out_specs=[_HBM] * (2 * m),
        out_shape=_hbm_like(srcs) + _hbm_like(lands), input_output_aliases={t: t for t in range(2 * m)},
        compiler_params=_SPLIT_COPY)(*srcs, *lands, *sems, after)
    return res[:m], res[m:]


def _chip_copies(srcs, lands, small_src, small_land, send_sems, recv_sems):
    x, y, c, chip, peers = _place()
    n = len(srcs)
    copies = []
    for r, (px, py) in enumerate(peers):
        for t in range(n):
            k = t * N_PEERS + r
            copies.append(_remote(srcs[t].at[2 * px + py], lands[t].at[r], send_sems.at[k], recv_sems.at[k], (px, py, c)))
        if small_src is not None:
            k = n * N_PEERS + r
            half_s = small_src.shape[0] // 2
            copies.append(_remote(small_src.at[pl.ds(c * half_s, half_s)], small_land.at[r],
                                  send_sems.at[k], recv_sems.at[k], (px, py, c)))
    return copies


def _chip_start(name, sums_bf16, small=None):
    n = len(sums_bf16)
    srcs = list(sums_bf16) + ([small] if small is not None else [])
    m = len(srcs)
    lands = [pltpu.HBM((N_PEERS,) + s.shape[1:], BF16) for s in sums_bf16]
    if small is not None:
        lands.append(pltpu.HBM((N_PEERS, small.shape[0] // 2, 128), F32))
    nk = m * N_PEERS

    def body(*refs):
        src_refs, land_refs = refs[m:2 * m], refs[2 * m:3 * m]
        send_sems, recv_sems, token = refs[3 * m:]
        small_src, small_land = (src_refs[n], land_refs[n]) if small is not None else (None, None)
        for cp in _chip_copies(src_refs[:n], land_refs[:n], small_src, small_land, send_sems, recv_sems):
            cp.start()
        token[...] = jnp.zeros_like(token)

    res = pl.pallas_call(
        body, name=name, in_specs=[_HBM] * m,
        out_specs=[_HBM] * (2 * m) + [_SEM, _SEM, pl.BlockSpec(memory_space=pltpu.VMEM)],
        out_shape=_hbm_like(srcs) + lands + [pltpu.SemaphoreType.DMA((nk,)), pltpu.SemaphoreType.DMA((nk,)),
                                             _sds((8, 128), F32)],
        input_output_aliases={t: t for t in range(m)}, compiler_params=_SPLIT_COPY)(*_hbm(srcs))
    return (res[:m], res[m:2 * m], (res[2 * m], res[2 * m + 1])), res[2 * m + 2]


def _chip_wait(name, flight, with_small, after):
    srcs, lands, sems = flight
    m = len(srcs)
    n = m - 1 if with_small else m

    def body(*refs):
        src_refs, land_refs = refs[:m], refs[m:2 * m]
        send_sems, recv_sems = refs[2 * m], refs[2 * m + 1]
        small_src, small_land = (src_refs[n], land_refs[n]) if with_small else (None, None)
        for cp in _chip_copies(src_refs[:n], land_refs[:n], small_src, small_land, send_sems, recv_sems):
            cp.wait_send()
            cp.wait_recv()

    res = pl.pallas_call(
        body, name=name, in_specs=[_HBM] * (2 * m) + [_SEM, _SEM, _ANY], out_specs=[_HBM] * (2 * m),
        out_shape=_hbm_like(srcs) + _hbm_like(lands), input_output_aliases={t: t for t in range(2 * m)},
        compiler_params=_SPLIT_COPY)(*srcs, *lands, *sems, after)
    return res[:m], res[m:]


def _swap_start(name, bufs):
    n = len(bufs)

    def body(*refs):
        _sibling_handshake()
        outs = refs[n:2 * n]
        send_sems, recv_sems, token = refs[2 * n:]
        x, y, c, _, _ = _place()
        for t in range(n):
            h = outs[t].shape[0] // 2
            mine = outs[t].at[pl.ds(c * h, h)]
            _remote(mine, mine, send_sems.at[t], recv_sems.at[t], (x, y, 1 - c)).start()
        token[...] = jnp.zeros_like(token)

    res = pl.pallas_call(
        body, name=name, in_specs=[_HBM] * n, out_specs=[_HBM] * n + [_SEM, _SEM, pl.BlockSpec(memory_space=pltpu.VMEM)],
        out_shape=_hbm_like(bufs) + [pltpu.SemaphoreType.DMA((n,)), pltpu.SemaphoreType.DMA((n,)), _sds((8, 128), F32)],
        input_output_aliases={t: t for t in range(n)}, compiler_params=_SPLIT_COPY_SIBLING)(*_hbm(bufs))
    return (res[:n], (res[n], res[n + 1])), res[n + 2]


def _swap_wait(name, flight, after):
    bufs, sems = flight
    n = len(bufs)

    def body(*refs):
        ins = refs[:n]
        send_sems, recv_sems = refs[n], refs[n + 1]
        x, y, c, _, _ = _place()
        for t in range(n):
            h = ins[t].shape[0] // 2
            cp = _remote(ins[t].at[pl.ds(c * h, h)], ins[t].at[pl.ds((1 - c) * h, h)], send_sems.at[t],
                         recv_sems.at[t], (x, y, 1 - c))
            cp.wait_send()
            cp.wait_recv()

    return pl.pallas_call(
        body, name=name, in_specs=[_HBM] * n + [_SEM, _SEM, _ANY], out_specs=[_HBM] * n, out_shape=_hbm_like(bufs),
        input_output_aliases={t: t for t in range(n)}, compiler_params=_SPLIT_COPY)(*bufs, *sems, after)


def _pair_sum(name, grad, got, ids):
    _, rows, cols = got.shape
    rb = min(rows, 256)
    nb = rows // rb
    blk = pl.BlockSpec((None, rb, cols), lambda i, j, ids: (j, i, 0))
    mine = pl.BlockSpec((None, rb, cols), lambda i, j, ids: (j, ids[1] * nb + i, 0))
    own = pl.BlockSpec((rb, cols), lambda i, j, ids: (i, 0))

    def body(ids_ref, a_ref, b_ref, s_ref, sb_ref):
        s = a_ref[...] + b_ref[...]
        sb_ref[...] = s.astype(BF16)

        @pl.when(pl.program_id(1) == ids_ref[0])
        def _():
            s_ref[...] = s

    return _prefetch_call(body, name, ids, (nb, N_CHIPS), [mine, blk], [own, blk],
                          [_sds((rows, cols), F32), _sds(got.shape, BF16)], (grad, got),
                          semantics=("parallel", "arbitrary"))


def _chip_sum(name, own_sum, got, ids):
    rows, cols = own_sum.shape
    rb = min(rows, 256)
    nb = rows // rb
    own = pl.BlockSpec((rb, cols), lambda i, ids: (i, 0))
    blk3 = pl.BlockSpec((N_PEERS, rb, cols), lambda i, ids: (0, i, 0))
    out = pl.BlockSpec((rb, cols), lambda i, ids: (ids[1] * nb + i, 0))

    def body(ids_ref, a_ref, b_ref, o_ref):
        o_ref[...] = ((a_ref[...] + b_ref[0].astype(F32)) + b_ref[1].astype(F32)) + b_ref[2].astype(F32)

    return _prefetch_call(body, name, ids, (nb,), [own, blk3], out, _sds((2 * rows, cols), F32), (own_sum, got))


SMALL_RB = 280


def _small_pair_sum(own, got):
    blk = pl.BlockSpec((SMALL_RB, 128), lambda i: (i, 0))

    def body(a_ref, b_ref, o_ref):
        o_ref[...] = a_ref[...] + b_ref[...]

    return pl.pallas_call(body, name="small_pair_sum", grid=(own.shape[0] // SMALL_RB,), in_specs=[blk, blk],
                          out_specs=blk, out_shape=_sds(own.shape, F32),
                          compiler_params=_params(("parallel",)))(own, got)


def _small_chip_sum(pair, got, ids):
    nb = pair.shape[0] // 2 // SMALL_RB
    half = pl.BlockSpec((SMALL_RB, 128), lambda i, ids: (ids[1] * nb + i, 0))
    blk3 = pl.BlockSpec((N_PEERS, SMALL_RB, 128), lambda i, ids: (0, i, 0))

    def body(ids_ref, a_ref, b_ref, o_ref):
        o_ref[...] = (a_ref[...] + b_ref[1]) + (b_ref[0] + b_ref[2])

    return _prefetch_call(body, "small_chip_sum", ids, (nb,), [half, blk3], half, _sds(pair.shape, F32), (pair, got))


def _adamw_math(w, g, m, v):
    m = ADAM_B1 * m + (1.0 - ADAM_B1) * g
    v = ADAM_B2 * v + (1.0 - ADAM_B2) * (g * g)
    m_hat = m / (1.0 - ADAM_B1 ** ADAM_STEP)
    v_hat = v / (1.0 - ADAM_B2 ** ADAM_STEP)
    delta = -ADAM_LR * (m_hat / (jnp.sqrt(v_hat) + ADAM_EPS) + ADAM_WD * w)
    return delta, m, v


def _adamw(name, w, g, m, v, rb=None):
    rows, cols = w.shape
    rb = rows if rb is None else rb
    blk = pl.BlockSpec((rb, cols), lambda i: (i, 0))

    def body(w_ref, g_ref, m_ref, v_ref, d_ref, nm_ref, nv_ref):
        d, nm, nv = _adamw_math(w_ref[...], g_ref[...], m_ref[...], v_ref[...])
        d_ref[...] = d
        nm_ref[...] = nm
        nv_ref[...] = nv

    return pl.pallas_call(body, name=name, grid=(rows // rb,), in_specs=[blk] * 4, out_specs=[blk] * 3,
                          out_shape=[_sds(w.shape, F32)] * 3, compiler_params=_params(("parallel",)))(w, g, m, v)


def _adamw_small(ws, gs, ms, vs):
    n = len(ws)

    def body(*refs):
        for t in range(n):
            w_ref, g_ref, m_ref, v_ref = (refs[k * n + t] for k in range(4))
            d, nm, nv = _adamw_math(w_ref[...], g_ref[...], m_ref[...], v_ref[...])
            for k, val in enumerate((d, nm, nv)):
                refs[(4 + k) * n + t][...] = val

    res = pl.pallas_call(body, name="adamw_small", out_shape=[_sds(a.shape, F32) for a in ws] * 3,
                         compiler_params=_params())(*ws, *gs, *ms, *vs)
    return [(res[t], res[n + t], res[2 * n + t]) for t in range(n)]


BIG = ("w_in", "w_att_o", "w_rec_o", "w_out", "w_ff1", "w_ff2")
SHARDED_VECS = ("conv_w", "b_rg_a", "b_rg_i", "lru_lambda")
SMALL = ("ln1_g", "b_in", "rpb", "conv_w", "conv_b", "w_rg_a", "b_rg_a", "w_rg_i", "b_rg_i", "lru_lambda",
         "ln2_g", "lnf_g")
SMALL_ROWS = 2240
ORDER = ("ln1_g", "w_in", "b_in", "rpb", "w_att_o", "conv_w", "conv_b", "w_rg_a", "b_rg_a", "w_rg_i", "b_rg_i",
         "lru_lambda", "w_rec_o", "w_out", "ln2_g", "w_ff1", "w_ff2", "lnf_g")


def _pack_small(grads, loss):
    parts, sizes = [], {}
    for n in SMALL:
        flat = grads[n].reshape(-1)
        pad = (-flat.shape[0]) % 128
        sizes[n] = (flat.shape[0], flat.shape[0] + pad)
        parts.append(jnp.pad(flat, (0, pad)))
    total = sum(s[1] for s in sizes.values())
    parts.append(jnp.pad(loss.reshape(1), (0, SMALL_ROWS * 128 - total - 1)))
    return jnp.concatenate(parts).reshape(SMALL_ROWS, 128), sizes


def _unpack_small(buf, sizes, shapes):
    flat = buf.reshape(-1)
    out, pos = {}, 0
    for n in SMALL:
        size, padded = sizes[n]
        out[n] = flat[pos:pos + size].reshape(shapes[n])
        pos += padded
    return out, flat[pos]


def _gather_weights(w, chip):
    chip_id = chip.astype(jnp.int32).reshape(1)
    vec_rows = [w[n][0] for n in SHARDED_VECS]
    vec_shard = jnp.concatenate(vec_rows + [jnp.zeros((16 - 10, D // N_CHIPS), F32)], axis=0)
    vec_slots = lax.dynamic_update_slice(jnp.zeros((N_CHIPS, 16, D // N_CHIPS), F32), vec_shard[None], (chip, 0, 0))
    bufs_a, sems_a, token_a = _gather_start("gather_start_first", [_cast_bf16("cast_w_in", w["w_in"][0], chip_id), vec_slots])
    rest_names = BIG[1:]
    bufs_b, sems_b, token_b = _gather_start(
        "gather_start_rest", [_cast_bf16("cast_" + n, w[n][0], chip_id, after=(token_a,)) for n in rest_names])

    def first(after):
        w_in_full, vec_full = _gather_forward("gather_forward_first", _gather_wait("gather_wait_first", bufs_a, sems_a, after))
        vecs = vec_full.transpose(1, 0, 2).reshape(16, D)
        return dict(w_in=w_in_full, conv_w=vecs[0:4], b_rg_a=vecs[4:6], b_rg_i=vecs[6:8], lru_lambda=vecs[8:10])

    def rest(after):
        full = dict(zip(rest_names, _gather_forward("gather_forward_rest",
                                                    _gather_wait("gather_wait_rest", bufs_b, sems_b, after))))
        return dict(w_att_o=full["w_att_o"], w_ff1=full["w_ff1"], w_rec_o=full["w_rec_o"].reshape(D, D),
                    w_out=full["w_out"].reshape(D, D), w_ff2=full["w_ff2"].reshape(D_FF, D))

    p = dict(ln1_g=w["ln1_g"], b_in=w["b_in"], rpb=w["rpb"][0], conv_b=w["conv_b"], w_rg_a=w["w_rg_a"][0],
             w_rg_i=w["w_rg_i"][0], ln2_g=w["ln2_g"], lnf_g=w["lnf_g"].reshape(1, D))
    return p, ((token_b,), first, rest)


class _Reducer:
    def __init__(self, ids):
        self.ids = ids
        self.groups = {}

    def begin(self, tag, grads, small=None):
        names = list(grads)
        big = [grads[n].reshape(N_CHIPS, -1, grads[n].shape[-1]) for n in names]
        flight, token = _pair_start("pair_start_" + tag, big, [] if small is None else [small])
        self.groups[tag] = dict(names=names, pair=flight, small=small is not None)
        return (token,)

    def advance(self, tag, after):
        grp = self.groups[tag]
        n = len(grp["names"])
        mine, got = _pair_wait("pair_wait_" + tag, grp["pair"], n, after)
        sums = [_pair_sum("pair_sum_" + name, a, b, self.ids) for name, a, b in zip(grp["names"], mine, got)]
        small_sum = _small_pair_sum(mine[n], got[n]) if grp["small"] else None
        grp["chip"], token = _chip_start("chip_start_" + tag, [s[1] for s in sums], small_sum)
        grp["sums"] = [s[0] for s in sums]
        self.last_token = token
        return (token,)

    def finish(self, tag, after):
        grp = self.groups[tag]
        srcs, lands = _chip_wait("chip_wait_" + tag, grp["chip"], grp["small"], after)
        halves = [_chip_sum("chip_sum_" + name, s, b, self.ids) for name, s, b in zip(grp["names"], grp["sums"], lands)]
        if grp["small"]:
            halves.append(_small_chip_sum(srcs[-1], lands[-1], self.ids))
        grp["swap"], token = _swap_start("swap_start_" + tag, halves)
        return token

    def interlude(self, tokens):
        after = tokens[0]
        for tag in list(self.groups)[:-1]:
            after = self.finish(tag, after)
        return after

    def result(self, tag, after):
        return _swap_wait("swap_wait_" + tag, self.groups[tag]["swap"], after)


def kernel(x, ln1_g, w_in, b_in, rpb, w_att_o, conv_w, conv_b, w_rg_a, b_rg_a, w_rg_i, b_rg_i, lru_lambda, w_rec_o, w_out, ln2_g, w_ff1, w_ff2, lnf_g, loss_target, m_ln1_g, m_w_in, m_b_in, m_rpb, m_w_att_o, m_conv_w, m_conv_b, m_w_rg_a, m_b_rg_a, m_w_rg_i, m_b_rg_i, m_lru_lambda, m_w_rec_o, m_w_out, m_ln2_g, m_w_ff1, m_w_ff2, m_lnf_g, v_ln1_g, v_w_in, v_b_in, v_rpb, v_w_att_o, v_conv_w, v_conv_b, v_w_rg_a, v_b_rg_a, v_w_rg_i, v_b_rg_i, v_lru_lambda, v_w_rec_o, v_w_out, v_ln2_g, v_w_ff1, v_w_ff2, v_lnf_g):
    w = dict(ln1_g=ln1_g, w_in=w_in, b_in=b_in, rpb=rpb, w_att_o=w_att_o, conv_w=conv_w, conv_b=conv_b,
             w_rg_a=w_rg_a, b_rg_a=b_rg_a, w_rg_i=w_rg_i, b_rg_i=b_rg_i, lru_lambda=lru_lambda, w_rec_o=w_rec_o,
             w_out=w_out, ln2_g=ln2_g, w_ff1=w_ff1, w_ff2=w_ff2, lnf_g=lnf_g)
    m = dict(ln1_g=m_ln1_g, w_in=m_w_in, b_in=m_b_in, rpb=m_rpb, w_att_o=m_w_att_o, conv_w=m_conv_w,
             conv_b=m_conv_b, w_rg_a=m_w_rg_a, b_rg_a=m_b_rg_a, w_rg_i=m_w_rg_i, b_rg_i=m_b_rg_i,
             lru_lambda=m_lru_lambda, w_rec_o=m_w_rec_o, w_out=m_w_out, ln2_g=m_ln2_g, w_ff1=m_w_ff1,
             w_ff2=m_w_ff2, lnf_g=m_lnf_g)
    v = dict(ln1_g=v_ln1_g, w_in=v_w_in, b_in=v_b_in, rpb=v_rpb, w_att_o=v_w_att_o, conv_w=v_conv_w,
             conv_b=v_conv_b, w_rg_a=v_w_rg_a, b_rg_a=v_b_rg_a, w_rg_i=v_w_rg_i, b_rg_i=v_b_rg_i,
             lru_lambda=v_lru_lambda, w_rec_o=v_w_rec_o, w_out=v_w_out, ln2_g=v_ln2_g, w_ff1=v_w_ff1,
             w_ff2=v_w_ff2, lnf_g=v_lnf_g)
    chip = 2 * lax.axis_index("x") + lax.axis_index("y")
    ids = jnp.stack([chip, lax.axis_index("c")]).astype(jnp.int32)

    out_grad, out_delta, out_m, out_v = {}, {}, {}, {}

    def update(n, gn):
        shape, two_d = w[n].shape, gn.shape
        d, nm, nv = _adamw("adamw_" + n, w[n].reshape(two_d), gn, m[n].reshape(two_d), v[n].reshape(two_d), 256)
        out_grad[n], out_delta[n], out_m[n], out_v[n] = (gn.reshape(shape), d.reshape(shape), nm.reshape(shape),
                                                         nv.reshape(shape))
        return d

    reducer = _Reducer(ids)
    p, late = _gather_weights(w, chip)
    loss, grad_x, g = _local_step(x, loss_target, p, late, reducer)
    small, sizes = _pack_small(g, loss + reducer.last_token[:1, :1])
    after = reducer.begin("small", {}, small)[0]
    for tag in ("ff", "proj", "in"):
        for n, red in zip(reducer.groups[tag]["names"], reducer.result(tag, after)):
            after = update(n, red)
        if tag == "ff":
            after = reducer.finish("in", reducer.advance("small", after)[0])
    (small_red,) = reducer.result("small", reducer.finish("small", after))
    gsmall, loss = _unpack_small(small_red, sizes, {n: g[n].shape for n in SMALL})
    two_d = {n: (int(np.prod(w[n].shape[:-1])), w[n].shape[-1]) for n in SMALL}
    for n in SHARDED_VECS:
        gsmall[n] = lax.dynamic_slice_in_dim(gsmall[n], chip * (D // N_CHIPS), D // N_CHIPS, axis=1)
    gs = [gsmall[n].reshape(two_d[n]) for n in SMALL]
    updates = _adamw_small([w[n].reshape(two_d[n]) for n in SMALL], gs, [m[n].reshape(two_d[n]) for n in SMALL],
                           [v[n].reshape(two_d[n]) for n in SMALL])
    for n, gn, (d, nm, nv) in zip(SMALL, gs, updates):
        shape = w[n].shape
        out_grad[n], out_delta[n], out_m[n], out_v[n] = (gn.reshape(shape), d.reshape(shape), nm.reshape(shape),
                                                         nv.reshape(shape))
    return (loss, grad_x, *[out_grad[n] for n in ORDER], *[out_delta[n] for n in ORDER],
            *[out_m[n] for n in ORDER], *[out_v[n] for n in ORDER])
```

```python
import numpy as np
import jax
import jax.numpy as jnp
from jax import lax
from jax.experimental import pallas as pl
from jax.experimental.pallas import tpu as pltpu

F32 = jnp.float32
BF16 = jnp.bfloat16

T = 2048
D = 1024
D_ATT = 512
D_IN = 5632
D_FF = 4096
N_HEADS = 8
HEAD_DIM = 64
GRID_W = 64
N_ROWS = T // GRID_W
WIN_H = 8
WIN_W = 16
KEYS = WIN_H * GRID_W
N_CHIPS = 4
EPS = 1e-6
LRU_C = 8.0
SCALE = HEAD_DIM ** -0.5
REC_CB = 256
REC_BLOCK = 64
REC_CHUNK = 256
PAD = 8

ADAM_LR = 0.001
ADAM_B1 = 0.9
ADAM_B2 = 0.999
ADAM_EPS = 1e-08
ADAM_WD = 0.01
ADAM_STEP = 10

VMEM_LIMIT = 56 * 1024 * 1024

NN = (((1,), (0,)), ((), ()))
NT = (((1,), (1,)), ((), ()))
TN = (((0,), (0,)), ((), ()))
MESH = pl.DeviceIdType.MESH


def _params(sem=None):
    return pltpu.CompilerParams(dimension_semantics=sem, vmem_limit_bytes=VMEM_LIMIT)


def _dot(a, b, dims):
    return lax.dot_general(a, b, dims, preferred_element_type=F32)


def _sigmoid(x):
    return 0.5 * jnp.tanh(0.5 * x) + 0.5


def _matmul(name, a, b, *, dims, grid, a_spec, b_spec, out_shapes, out_specs, acc_shape,
            extras=(), extra_specs=(), epilogue=None, colsum_spec=None, colsum_shape=None, after=(),
            semantics=("parallel", "parallel", "arbitrary"), epilogue_takes_first=False):
    nk = grid[2]
    n_extra = len(extras)
    n_out = len(out_shapes)
    with_colsum = colsum_spec is not None

    def body(a_ref, b_ref, *rest):
        ex = rest[:n_extra]
        rest = rest[:n_extra] + rest[n_extra + len(after):]
        outs = rest[n_extra:n_extra + n_out]
        pos = n_extra + n_out
        cs_out = rest[pos] if with_colsum else None
        pos += 1 if with_colsum else 0
        acc = rest[pos]
        cs_acc = rest[pos + 1] if with_colsum else None
        k = pl.program_id(2)
        first_tile = pl.program_id(0) == 0

        @pl.when(k == 0)
        def _():
            acc[...] = jnp.zeros_like(acc)
            if with_colsum:
                cs_acc[...] = jnp.zeros_like(cs_acc)

        bv = b_ref[...]
        acc[...] += _dot(a_ref[...].astype(BF16), bv.astype(BF16), dims)
        if with_colsum:
            cs_acc[...] += jnp.sum(bv.astype(F32), axis=0, keepdims=True)

        @pl.when(k == nk - 1)
        def _():
            r = acc[...]
            if epilogue is None:
                outs[0][...] = r.astype(outs[0].dtype)
            elif epilogue_takes_first:
                epilogue(r, ex, outs, first_tile)
            else:
                epilogue(r, ex, outs)
            if with_colsum:
                cs_out[...] = cs_acc[...]

    shapes = list(out_shapes)
    specs = list(out_specs)
    scratch = [pltpu.VMEM(acc_shape, F32)]
    if with_colsum:
        shapes.append(colsum_shape)
        specs.append(colsum_spec)
        scratch.append(pltpu.VMEM((1, acc_shape[1]), F32))
    res = pl.pallas_call(
        body, name=name, grid=grid,
        in_specs=[a_spec, b_spec, *extra_specs] + [_ANY] * len(after),
        out_specs=specs, out_shape=shapes, scratch_shapes=scratch,
        compiler_params=_params(semantics),
    )(a, b, *extras, *after)
    return res


def _sds(shape, dtype):
    return jax.ShapeDtypeStruct(shape, dtype)


TM = 1024
NI = T // TM
TJ = T
NJ = T // TJ


def _mm_nn_cols(name, a, wg, out_dtype, *, bias=None, extras=(), extra_specs=(), epilogue=None,
                out_shapes=None, out_specs=None):
    k_dim, n4 = wg.shape[1], wg.shape[2]
    ex, exs = list(extras), list(extra_specs)
    if bias is not None:
        ex = [bias] + ex
        exs = [pl.BlockSpec((1, n4), lambda j, i, k: (0, j))] + exs
        user_ep = epilogue

        def epilogue(r, e, outs):
            r = r + e[0][...]
            if user_ep is None:
                outs[0][...] = r.astype(outs[0].dtype)
            else:
                user_ep(r, e[1:], outs)
    if out_shapes is None:
        out_shapes = [_sds((T, N_CHIPS * n4), out_dtype)]
        out_specs = [pl.BlockSpec((TJ, n4), lambda j, i, k: (i, j))]
    return _matmul(
        name, a, wg, dims=NN, grid=(N_CHIPS, NJ, 1),
        a_spec=pl.BlockSpec((TJ, k_dim), lambda j, i, k: (i, 0)),
        b_spec=pl.BlockSpec((None, k_dim, n4), lambda j, i, k: (j, 0, 0)),
        out_shapes=out_shapes, out_specs=out_specs, acc_shape=(TJ, n4),
        extras=ex, extra_specs=exs, epilogue=epilogue)


def _mm_nt_cols_rms_bwd(name, a, wg, x, g, dres, after=(), bf16_copy=False):
    n4 = wg.shape[2]
    row = pl.BlockSpec((TM, D), lambda i, j, k: (i, 0))
    vec = pl.BlockSpec((1, D), lambda i, j, k: (0, 0))

    def epilogue(dhv, ex, outs, first):
        x_ref, g_ref, dres_ref = ex
        dx_ref, dg_ref = outs[0], outs[-1]
        xv = x_ref[...]
        rstd = lax.rsqrt(jnp.mean(xv * xv, axis=-1, keepdims=True) + EPS)
        xhat = xv * rstd
        dy = dhv * g_ref[...]
        dx = dres_ref[...] + rstd * (dy - xhat * jnp.mean(dy * xhat, axis=-1, keepdims=True))
        dx_ref[...] = dx
        if bf16_copy:
            outs[1][...] = dx.astype(BF16)
        part = jnp.sum(dhv * xhat, axis=0, keepdims=True)

        @pl.when(first)
        def _():
            dg_ref[...] = part

        @pl.when(jnp.logical_not(first))
        def _():
            dg_ref[...] += part

    return _matmul(
        name, a, wg, dims=NT, grid=(NI, 1, N_CHIPS),
        a_spec=pl.BlockSpec((TM, n4), lambda i, j, k: (i, k)),
        b_spec=pl.BlockSpec((None, D, n4), lambda i, j, k: (k, 0, 0)),
        out_shapes=[_sds((T, D), F32)] + [_sds((T, D), BF16)] * bf16_copy + [_sds((1, D), F32)],
        out_specs=[row] + [row] * bf16_copy + [vec], acc_shape=(TM, D),
        extras=[x, g, dres], extra_specs=[row, vec, row], epilogue=epilogue, after=after,
        semantics=("arbitrary", "arbitrary", "arbitrary"), epilogue_takes_first=True)


def _mm_nt_rows(name, a, w, out_dtype, *, tn, extras=(), extra_specs=(), epilogue=None):
    k_dim, n = w.shape
    return _matmul(
        name, a, w, dims=NT, grid=(k_dim // tn, NJ, 1),
        a_spec=pl.BlockSpec((TJ, n), lambda j, i, k: (i, 0)),
        b_spec=pl.BlockSpec((tn, n), lambda j, i, k: (j, 0)),
        out_shapes=[_sds((T, k_dim), out_dtype)],
        out_specs=[pl.BlockSpec((TJ, tn), lambda j, i, k: (i, j))], acc_shape=(TJ, tn),
        extras=extras, extra_specs=extra_specs, epilogue=epilogue)


def _mm_tn_cols(name, a, g, n4, *, colsum=False):
    k_dim = a.shape[1]
    kw = {}
    if colsum:
        kw = dict(colsum_spec=pl.BlockSpec((1, n4), lambda j, i, k: (0, j)),
                  colsum_shape=_sds((1, N_CHIPS * n4), F32))
    return _matmul(
        name, a, g, dims=TN, grid=(N_CHIPS, 1, NJ),
        a_spec=pl.BlockSpec((TJ, k_dim), lambda j, i, k: (k, 0)),
        b_spec=pl.BlockSpec((TJ, n4), lambda j, i, k: (k, j)),
        out_shapes=[_sds((N_CHIPS, k_dim, n4), F32)],
        out_specs=[pl.BlockSpec((None, k_dim, n4), lambda j, i, k: (j, 0, 0))],
        acc_shape=(k_dim, n4), **kw)


def _mm_tn_rows(name, a, g, *, tm):
    k_dim, n = a.shape[1], g.shape[1]
    return _matmul(
        name, a, g, dims=TN, grid=(k_dim // tm, 1, NJ),
        a_spec=pl.BlockSpec((TJ, tm), lambda j, i, k: (k, j)),
        b_spec=pl.BlockSpec((TJ, n), lambda j, i, k: (k, 0)),
        out_shapes=[_sds((k_dim, n), F32)],
        out_specs=[pl.BlockSpec((tm, n), lambda j, i, k: (j, 0))], acc_shape=(tm, n))


TE = 256
NE = T // TE
_ROW = pl.BlockSpec((TE, D), lambda i: (i, 0))
_VEC = pl.BlockSpec((1, D), lambda i: (0, 0))


def _rms_fwd(name, x, g, after=()):
    def body(x_ref, g_ref, *rest):
        h_ref = rest[-1]
        xv = x_ref[...]
        rstd = lax.rsqrt(jnp.mean(xv * xv, axis=-1, keepdims=True) + EPS)
        h_ref[...] = (xv * rstd * g_ref[...]).astype(BF16)

    return pl.pallas_call(body, name=name, grid=(NE,), in_specs=[_ROW, _VEC] + [_ANY] * len(after), out_specs=_ROW,
                          out_shape=_sds((T, D), BF16), compiler_params=_params(("parallel",)))(x, g, *after)


def _mm_x2_loss_head(s, w_ff2, x1, target, g):
    k_dim = w_ff2.shape[0]
    row = pl.BlockSpec((TM, D), lambda i, j, k: (i, 0))
    vec = pl.BlockSpec((1, D), lambda i, j, k: (0, 0))

    def epilogue(r, ex, outs, first):
        x1_ref, t_ref, g_ref = ex
        loss_ref, dx_ref, dxb_ref, dg_ref = outs
        xv = x1_ref[...] + r
        rstd = lax.rsqrt(jnp.mean(xv * xv, axis=-1, keepdims=True) + EPS)
        xhat = xv * rstd
        gv = g_ref[...]
        err = xhat * gv - t_ref[...]
        dy = err * (1.0 / D)
        dxh = dy * gv
        dx = rstd * (dxh - xhat * jnp.mean(dxh * xhat, axis=-1, keepdims=True))
        dx_ref[...] = dx
        dxb_ref[...] = dx.astype(BF16)
        dg_part = jnp.sum(dy * xhat, axis=0, keepdims=True)
        loss_part = (0.5 / D) * jnp.sum(jnp.sum(err * err, axis=1, keepdims=True), axis=0, keepdims=True)

        @pl.when(first)
        def _():
            dg_ref[...] = dg_part
            loss_ref[...] = loss_part

        @pl.when(jnp.logical_not(first))
        def _():
            dg_ref[...] += dg_part
            loss_ref[...] += loss_part

    return _matmul(
        "mm_x2_loss_head", s, w_ff2, dims=NN, grid=(NI, 1, k_dim // D),
        a_spec=pl.BlockSpec((TM, D), lambda i, j, k: (i, k)), b_spec=pl.BlockSpec((D, D), lambda i, j, k: (k, 0)),
        out_shapes=[_sds((1, 1), F32), _sds((T, D), F32), _sds((T, D), BF16), _sds((1, D), F32)],
        out_specs=[pl.BlockSpec((1, 1), lambda i, j, k: (0, 0)), row, row, vec], acc_shape=(TM, D),
        extras=[x1, target, g], extra_specs=[row, row, vec], epilogue=epilogue,
        semantics=("arbitrary", "arbitrary", "arbitrary"), epilogue_takes_first=True)


DZ_Q, DZ_K, DZ_V, DZ_U, DZ_Y, DZ_G_ATT, DZ_G_REC = 0, 512, 1024, 1536, 2560, 3584, 4608
MW = 512
_G_ATT_BLK = 3584 // MW
_G_REC_BLK = 4608 // MW


TB = 512


def _branch_specs():
    def row(cols):
        return pl.BlockSpec((TB, cols), lambda i: (i, 0))

    ga = pl.BlockSpec((TB, MW), lambda i: (i, _G_ATT_BLK))
    ga2 = pl.BlockSpec((TB, MW), lambda i: (i, _G_ATT_BLK + 1))
    gr = pl.BlockSpec((TB, MW), lambda i: (i, _G_REC_BLK))
    gr2 = pl.BlockSpec((TB, MW), lambda i: (i, _G_REC_BLK + 1))
    w_att = pl.BlockSpec((N_CHIPS, D_ATT, D // N_CHIPS), lambda i: (0, 0, 0))
    w_sq = pl.BlockSpec((D, D), lambda i: (0, 0))
    return row, (ga, ga2, gr, gr2), w_att, w_sq


def _gate_values(gate_refs):
    ga, ga2, gr, gr2 = (r[...].astype(F32) for r in gate_refs)
    return _sigmoid(jnp.concatenate([ga, ga2], axis=1)), _sigmoid(jnp.concatenate([gr, gr2], axis=1))


def _branches_fwd(att, g, z, x, w_att_o, w_rec_o, w_out, ln2_g):
    row, gate_specs, w_att, w_sq = _branch_specs()

    def body(att_ref, g_ref, ga_ref, ga2_ref, gr_ref, gr2_ref, x_ref, wa_ref, wr_ref, wo_ref, g2_ref,
             ya_ref, yr_ref, m_ref, x1_ref, h2_ref):
        attv = att_ref[...]
        ya = jnp.concatenate([_dot(attv, wa_ref[j], NN) for j in range(N_CHIPS)], axis=1)
        yr = _dot(g_ref[...], wr_ref[...], NN)
        sa, sr = _gate_values((ga_ref, ga2_ref, gr_ref, gr2_ref))
        mixed = (sa * ya + sr * yr).astype(BF16)
        ya_ref[...] = ya
        yr_ref[...] = yr
        m_ref[...] = mixed
        x1 = x_ref[...] + _dot(mixed, wo_ref[...], NN)
        x1_ref[...] = x1
        rstd = lax.rsqrt(jnp.mean(x1 * x1, axis=-1, keepdims=True) + EPS)
        h2_ref[...] = (x1 * rstd * g2_ref[...]).astype(BF16)

    return pl.pallas_call(
        body, name="branches_fwd", grid=(T // TB,),
        in_specs=[row(D_ATT), row(D), *gate_specs, row(D), w_att, w_sq, w_sq, pl.BlockSpec((1, D), lambda i: (0, 0))],
        out_specs=[row(D)] * 5,
        out_shape=[_sds((T, D), F32), _sds((T, D), F32), _sds((T, D), BF16), _sds((T, D), F32), _sds((T, D), BF16)],
        compiler_params=_params(("parallel",)))(att, g, z, z, z, z, x, w_att_o, w_rec_o, w_out, ln2_g)


def _branches_bwd(dx1_b, y_att, y_rec, z, w_att_o, w_rec_o, w_out):
    row, gate_specs, w_att, w_sq = _branch_specs()
    n4 = D // N_CHIPS

    def body(dx_ref, ya_ref, yr_ref, ga_ref, ga2_ref, gr_ref, gr2_ref, wa_ref, wr_ref, wo_ref,
             dya_ref, dyr_ref, datt_ref, dg_ref, dz_ref, dga_ref, dgr_ref, sems):
        rows = pl.ds(pl.multiple_of(pl.program_id(0) * TB, TB), TB)
        dm = _dot(dx_ref[...], wo_ref[...], NT)
        sa, sr = _gate_values((ga_ref, ga2_ref, gr_ref, gr2_ref))
        dya = (dm * sa).astype(BF16)
        dyr = (dm * sr).astype(BF16)
        dya_ref[...] = dya
        dyr_ref[...] = dyr
        dga_ref[...] = (dm * ya_ref[...] * sa * (1.0 - sa)).astype(BF16)
        dgr_ref[...] = (dm * yr_ref[...] * sr * (1.0 - sr)).astype(BF16)
        copies = [pltpu.make_async_copy(dga_ref, dz_ref.at[rows, pl.ds(DZ_G_ATT, D)], sems.at[0]),
                  pltpu.make_async_copy(dgr_ref, dz_ref.at[rows, pl.ds(DZ_G_REC, D)], sems.at[1])]
        for cp in copies:
            cp.start()
        datt = _dot(dya[:, 0:n4], wa_ref[0], NT)
        for j in range(1, N_CHIPS):
            datt = datt + _dot(dya[:, j * n4:(j + 1) * n4], wa_ref[j], NT)
        datt_ref[...] = datt.astype(BF16)
        dg_ref[...] = _dot(dyr, wr_ref[...], NT).astype(BF16)
        for cp in copies:
            cp.wait()

    return pl.pallas_call(
        body, name="branches_bwd", grid=(T // TB,),
        in_specs=[row(D), row(D), row(D), *gate_specs, w_att, w_sq, w_sq],
        out_specs=[row(D)] * 2 + [row(D_ATT), row(D), _ANY],
        out_shape=[_sds((T, D), BF16)] * 2 + [_sds((T, D_ATT), BF16), _sds((T, D), BF16), _sds((T, D_IN), BF16)],
        scratch_shapes=[pltpu.VMEM((TB, D), BF16), pltpu.VMEM((TB, D), BF16), pltpu.SemaphoreType.DMA((2,))],
        compiler_params=_params(("parallel",)))(dx1_b, y_att, y_rec, z, z, z, z, w_att_o, w_rec_o, w_out)


HP = 2 * HEAD_DIM
N_HP = N_HEADS // 2
ATT_UNROLL_FWD = 32
ATT_UNROLL_BWD = 16
DIAG_ROWS = 32


def _window_maps():
    diag = np.zeros((GRID_W * GRID_W, 128), np.float32)
    for qc in range(GRID_W):
        w0 = min(max(qc - WIN_W // 2, 0), GRID_W - WIN_W)
        for kc in range(w0, w0 + WIN_W):
            diag[qc * GRID_W + kc, kc - qc + WIN_W - 1] = 1.0
    return diag, diag.sum(axis=1)[None, :]


def _split3(x):
    a = x.astype(BF16)
    r = x - a.astype(F32)
    b = r.astype(BF16)
    c = (r - b.astype(F32)).astype(BF16)
    return a, b, c


N_DROW = 2 * WIN_H - 1
N_DPAIR = N_DROW - 1


def _bias_pairs(rpb):
    diag, valid = _window_maps()
    r2 = jnp.pad(rpb.reshape(N_HEADS * N_DROW, 2 * WIN_W - 1),
                 ((0, 128 - N_HEADS * N_DROW), (0, 128 - (2 * WIN_W - 1))))

    def body(r_ref, d_ref, v_ref, o_ref):
        dv = d_ref[...]
        t = sum(_dot(part, dv, NN) for part in _split3(r_ref[...]))
        o_ref[...] = jnp.where(v_ref[...] > 0.0, t, -1e30)

    t = pl.pallas_call(body, name="rpb_expand", out_shape=_sds((128, GRID_W * GRID_W), F32),
                       compiler_params=_params())(r2, jnp.asarray(diag.T, BF16), jnp.asarray(valid, F32))
    t = t[:N_HEADS * N_DROW].reshape(N_HEADS, N_DROW, GRID_W, GRID_W)
    return jnp.concatenate([t[:, :N_DPAIR], t[:, 1:]], axis=-1)


def _row_bias(tb_ref, hh, d0):
    return jnp.concatenate([tb_ref[hh, d0 + 2 * ii] for ii in range(WIN_H // 2)], axis=1)


def _row_window(r):
    rs = jnp.clip(r - WIN_H // 2, 0, N_ROWS - WIN_H)
    return pl.multiple_of(r * GRID_W, GRID_W), pl.multiple_of(rs * GRID_W, GRID_W), rs - r + (WIN_H - 1)


def _split_heads(src_ref, dst_ref, scale=None):
    for hh in range(2):
        v = src_ref[:, hh * HEAD_DIM:(hh + 1) * HEAD_DIM]
        dst_ref[hh] = (v if scale is None else v * scale).astype(BF16)


def _attn_items(qb_ref, kb_ref, vb_ref, tb_ref, first_row, n_rows):
    wins = [_row_window(first_row + u) for u in range(n_rows)]
    items = [(u, hh) for u in range(n_rows) for hh in range(2)]
    q = [qb_ref[hh, pl.ds(wins[u][0], GRID_W), :] for u, hh in items]
    k = [kb_ref[hh, pl.ds(wins[u][1], KEYS), :] for u, hh in items]
    v = [vb_ref[hh, pl.ds(wins[u][1], KEYS), :] for u, hh in items]
    s = [_dot(qi, ki, NT) + _row_bias(tb_ref, hh, wins[u][2]) for qi, ki, (u, hh) in zip(q, k, items)]
    m = [jnp.max(si, axis=-1, keepdims=True) for si in s]
    e = [jnp.exp(si - mi) for si, mi in zip(s, m)]
    inv = [1.0 / jnp.sum(ei, axis=-1, keepdims=True) for ei in e]
    p = [ei * li for ei, li in zip(e, inv)]
    return wins, items, q, k, v, p


def _attn_in_specs():
    q = pl.BlockSpec((T, HP), lambda p: (0, p))
    k = pl.BlockSpec((T, HP), lambda p: (0, N_HP + p))
    v = pl.BlockSpec((T, HP), lambda p: (0, 2 * N_HP + p))
    tb = pl.BlockSpec((2, N_DPAIR, GRID_W, HP), lambda p: (p, 0, 0, 0))
    return q, k, v, tb


_HEAD_SCRATCH = pltpu.VMEM((2, T, HEAD_DIM), BF16)


_PROBS = pl.BlockSpec((T, 2 * KEYS), lambda p: (0, p))


def _attn_fwd(z, tb):
    def body(q_ref, k_ref, v_ref, tb_ref, o_ref, p_ref, qb_ref, kb_ref, vb_ref):
        _split_heads(q_ref, qb_ref, SCALE)
        _split_heads(k_ref, kb_ref)
        _split_heads(v_ref, vb_ref)

        def rows(it, carry):
            wins, items, _, _, v, p = _attn_items(qb_ref, kb_ref, vb_ref, tb_ref, it * ATT_UNROLL_FWD, ATT_UNROLL_FWD)
            pb = [pi.astype(BF16) for pi in p]
            o = [_dot(pi, vi, NN) for pi, vi in zip(pb, v)]
            for u, (q0, _, _) in enumerate(wins):
                o_ref[pl.ds(q0, GRID_W), :] = jnp.concatenate(o[2 * u:2 * u + 2], axis=1).astype(BF16)
                p_ref[pl.ds(q0, GRID_W), :] = jnp.concatenate(pb[2 * u:2 * u + 2], axis=1)
            return carry

        lax.fori_loop(0, N_ROWS // ATT_UNROLL_FWD, rows, 0)

    blk = pl.BlockSpec((T, HP), lambda p: (0, p))
    return pl.pallas_call(
        body, name="attn_fwd", grid=(N_HP,), in_specs=list(_attn_in_specs()), out_specs=[blk, _PROBS],
        out_shape=[_sds((T, D_ATT), BF16), _sds((T, N_HEADS * KEYS), BF16)], scratch_shapes=[_HEAD_SCRATCH] * 3,
        compiler_params=_params(("parallel",)))(z, z, z, tb)


def _attn_bwd(z, probs, d_att, dz, after=()):
    def body(q_ref, k_ref, v_ref, p_ref, do_ref, flip_ref, dz_in_ref, *rest):
        (dz_ref, diag_ref, qb_ref, kb_ref, vb_ref, dob_ref, dka_ref, dva_ref, ds_ref,
         dq_ref, dk_ref, dv_ref, sems) = rest[len(after):]
        _split_heads(q_ref, qb_ref, SCALE)
        _split_heads(k_ref, kb_ref)
        _split_heads(v_ref, vb_ref)
        _split_heads(do_ref, dob_ref)
        dka_ref[...] = jnp.zeros_like(dka_ref)
        dva_ref[...] = jnp.zeros_like(dva_ref)
        ds_ref[...] = jnp.zeros_like(ds_ref)

        def rows(it, carry):
            wins = [_row_window(it * ATT_UNROLL_BWD + u) for u in range(ATT_UNROLL_BWD)]
            items = [(u, hh) for u in range(ATT_UNROLL_BWD) for hh in range(2)]
            q = [qb_ref[hh, pl.ds(wins[u][0], GRID_W), :] for u, hh in items]
            k = [kb_ref[hh, pl.ds(wins[u][1], KEYS), :] for u, hh in items]
            v = [vb_ref[hh, pl.ds(wins[u][1], KEYS), :] for u, hh in items]
            pb = [p_ref[pl.ds(wins[u][0], GRID_W), hh * KEYS:(hh + 1) * KEYS] for u, hh in items]
            p = [pi.astype(F32) for pi in pb]
            do = [dob_ref[hh, pl.ds(wins[u][0], GRID_W), :] for u, hh in items]
            dv = [_dot(pi, di, TN) for pi, di in zip(pb, do)]
            dp = [_dot(di, vi, NT) for di, vi in zip(do, v)]
            ds = [pi * (dpi - jnp.sum(dpi * pi, axis=-1, keepdims=True)) for pi, dpi in zip(p, dp)]
            dsb = [d.astype(BF16) for d in ds]
            dq = [_dot(d, ki, NN) * SCALE for d, ki in zip(dsb, k)]
            dk = [_dot(d, qi, TN) for d, qi in zip(dsb, q)]
            for d, (u, hh) in zip(ds, items):
                for ii in range(WIN_H // 2):
                    ds_ref[hh, wins[u][2] + 2 * ii] += d[:, ii * HP:(ii + 1) * HP]
            for u, (q0, _, _) in enumerate(wins):
                dq_ref[pl.ds(q0, GRID_W), :] = jnp.concatenate(dq[2 * u:2 * u + 2], axis=1).astype(BF16)
            for dki, dvi, (u, hh) in zip(dk, dv, items):
                dka_ref[hh, pl.ds(wins[u][1], KEYS), :] += dki
                dva_ref[hh, pl.ds(wins[u][1], KEYS), :] += dvi
            return carry

        lax.fori_loop(0, N_ROWS // ATT_UNROLL_BWD, rows, 0)
        dk_ref[...] = jnp.concatenate([dka_ref[0], dka_ref[1]], axis=1).astype(BF16)
        dv_ref[...] = jnp.concatenate([dva_ref[0], dva_ref[1]], axis=1).astype(BF16)
        cols = pl.multiple_of(pl.program_id(0) * HP, HP)
        copies = [pltpu.make_async_copy(src, dz_ref.at[:, pl.ds(base + cols, HP)], sems.at[t])
                  for t, (src, base) in enumerate(((dq_ref, DZ_Q), (dk_ref, DZ_K), (dv_ref, DZ_V)))]
        for cp in copies:
            cp.start()
        _diag_sums(ds_ref, flip_ref, diag_ref)
        for cp in copies:
            cp.wait()

    blk = pl.BlockSpec((T, HP), lambda p: (0, p))
    q, k, v, _ = _attn_in_specs()
    flip =jnp.asarray(np.eye(HP, dtype=np.float32)[::-1], BF16)
    return pl.pallas_call(
        body, name="attn_bwd", grid=(N_HP,),
        in_specs=[q, k, v, _PROBS, blk, pl.BlockSpec((HP, HP), lambda p: (0, 0)), _ANY] + [_ANY] * len(after),
        out_specs=[_ANY, pl.BlockSpec((None, DIAG_ROWS, HP), lambda p: (p, 0, 0))],
        out_shape=[_sds(dz.shape, dz.dtype), _sds((N_HP, DIAG_ROWS, HP), F32)], input_output_aliases={6: 0},
        scratch_shapes=[_HEAD_SCRATCH] * 4 + [pltpu.VMEM((2, T, HEAD_DIM), F32), pltpu.VMEM((2, T, HEAD_DIM), F32),
                                              pltpu.VMEM((2, N_DPAIR, GRID_W, HP), F32)]
        + [pltpu.VMEM((T, HP), BF16)] * 3 + [pltpu.SemaphoreType.DMA((3,))],
        compiler_params=_params(("parallel",)))(z, z, z, probs, d_att, flip, dz, *after)


def _diag_sums(acc_ref, flip_ref, out_ref):
    flip = flip_ref[...]
    rows = []
    for hh in range(2):
        for pair in range(N_DPAIR):
            reversed_lanes = sum(_dot(part, flip, NN) for part in _split3(acc_ref[hh, pair]))
            skewed = pltpu.roll(reversed_lanes, 0, 1, stride=1, stride_axis=0)
            rows.append(jnp.sum(skewed, axis=0, keepdims=True))
    rows.append(jnp.zeros((DIAG_ROWS - len(rows), HP), F32))
    out_ref[...] = jnp.concatenate(rows, axis=0)


def _rpb_grad(diag_sums):
    g = diag_sums.reshape(N_HP * DIAG_ROWS, HP)
    sel = np.zeros((2, 128, N_HP * DIAG_ROWS), np.float32)
    lane = np.zeros((2, HP, 128), np.float32)
    for h in range(N_HEADS):
        for pair in range(N_DPAIR):
            for half in range(2):
                sel[half, h * N_DROW + pair + half, (h // 2) * DIAG_ROWS + (h % 2) * N_DPAIR + pair] = 1.0
    for j in range(2 * WIN_W - 1):
        for half in range(2):
            lane[half, (HP - 1 - GRID_W * half - (j - (WIN_W - 1))) % HP, j] = 1.0

    def body(g_ref, sel_ref, lane_ref, o_ref):
        parts = _split3(g_ref[...])
        total = None
        for half in range(2):
            picked = sum(_dot(sel_ref[half], part, NN) for part in parts)
            term = sum(_dot(part, lane_ref[half], NN) for part in _split3(picked))
            total = term if total is None else total + term
        o_ref[...] = total

    out = pl.pallas_call(body, name="rpb_grad", out_shape=_sds((128, 128), F32),
                         compiler_params=_params())(g, jnp.asarray(sel, BF16), jnp.asarray(lane, BF16))
    return out[:N_HEADS * N_DROW, :2 * WIN_W - 1].reshape(N_HEADS, N_DROW, 2 * WIN_W - 1)


N_CB = D // REC_CB
N_CHUNK = T // REC_CHUNK
DUP_PIECES = 4
N_TILE = T // 8
_U_BLK = 1536 // REC_CB
_Y_BLK = 2560 // REC_CB


def _block_diag(w):
    per = REC_CB // 64
    wt = w.reshape(2, N_CB, per, 64, 64)
    eye = jnp.eye(per, dtype=w.dtype)
    full = wt[:, :, :, :, None, :] * eye[None, None, :, None, :, None]
    return full.reshape(2, N_CB, REC_CB, REC_CB).astype(BF16)


def _gelu(x):
    c = 0.7978845608028654
    return 0.5 * x * (1.0 + jnp.tanh(c * (x + 0.044715 * x * x * x)))


def _gelu_grad(x):
    c = 0.7978845608028654
    th = jnp.tanh(c * (x + 0.044715 * x * x * x))
    return 0.5 * (1.0 + th) + 0.5 * x * (1.0 - th * th) * c * (1.0 + 3.0 * 0.044715 * x * x)


def _softplus_neg(lam):
    x = -lam
    e = jnp.exp(-jnp.abs(x))
    w = 1.0 + e
    l1p = jnp.where(w == 1.0, e, jnp.log(w) * e / (w - 1.0))
    return jnp.maximum(x, 0.0) + l1p


def _one_minus_exp(x, exp_x):
    poly = x * (1.0 + x * (1 / 2 + x * (1 / 6 + x * (1 / 24 + x * (1 / 120 + x * (1 / 720))))))
    return jnp.where(x > -0.125, -poly, 1.0 - exp_x)


def _conv_taps(pad_ref, t0, w, sign):
    out = None
    for j in range(4):
        term = w[j:j + 1, :] * pad_ref[pl.ds(PAD + t0 + sign * (j - 2), REC_CHUNK), :]
        out = term if out is None else out + term
    return out


def _gates(u, wa, wi, ba, bi, sp):
    ub = u.astype(BF16)
    r = _sigmoid(_dot(ub, wa, NN) + ba)
    i = _sigmoid(_dot(ub, wi, NN) + bi)
    log_a = (-LRU_C * sp) * r
    a = jnp.exp(log_a)
    x = jnp.maximum(_one_minus_exp(2.0 * log_a, a * a), 0.0)
    positive = x > 0.0
    inv = lax.rsqrt(jnp.where(positive, x, 1.0))
    mult = jnp.where(positive, x * inv, 0.0)
    return r, i, a, mult, jnp.where(positive, inv, 0.0)


def _tile_scan(a, b, sub, reverse):
    for s in (1, 2, 4):
        if reverse:
            a_s, b_s, m = pltpu.roll(a, 8 - s, 0), pltpu.roll(b, 8 - s, 0), sub < 8 - s
        else:
            a_s, b_s, m = pltpu.roll(a, s, 0), pltpu.roll(b, s, 0), sub >= s
        b = jnp.where(m, a * b_s + b, b)
        a = jnp.where(m, a * a_s, a)
    return a, b


def _last_row(x, row):
    return jnp.broadcast_to(x[row:row + 1, :], x.shape)


def _rec_prologue(up_ref, cw_ref, cb_ref, wa_ref, wi_ref, ba_ref, bi_ref, lam_ref,
                  upad_ref, u_ref, a_refs, h_refs):
    cb = up_ref.shape[1]
    zeros = jnp.zeros((PAD, cb), F32)
    upad_ref[pl.ds(0, PAD), :] = zeros
    upad_ref[pl.ds(PAD + T, PAD), :] = zeros
    upad_ref[pl.ds(PAD, T), :] = up_ref[...].astype(F32)
    cw = cw_ref[...]
    sp = _softplus_neg(lam_ref[...])
    for c in range(N_CHUNK):
        t0 = c * REC_CHUNK
        u = cb_ref[...] + _conv_taps(upad_ref, t0, cw, 1)
        u_ref[pl.ds(t0, REC_CHUNK), :] = u
        for d in range(2):
            _, i, a, mult, _ = _gates(u, wa_ref[d], wi_ref[d], ba_ref[d:d + 1, :], bi_ref[d:d + 1, :], sp[d:d + 1, :])
            a_refs[d][pl.ds(t0, REC_CHUNK), :] = a
            h_refs[d][pl.ds(t0, REC_CHUNK), :] = mult * (i * u)

    sub = lax.broadcasted_iota(jnp.int32, (8, cb), 0)

    def tile(k, carry):
        cf, cr = carry
        tf = pl.multiple_of(k * 8, 8)
        tr = pl.multiple_of((N_TILE - 1 - k) * 8, 8)
        af, bf = _tile_scan(a_refs[0][pl.ds(tf, 8), :], h_refs[0][pl.ds(tf, 8), :], sub, False)
        hf = af * cf + bf
        h_refs[0][pl.ds(tf, 8), :] = hf
        ar, br = _tile_scan(a_refs[1][pl.ds(tr, 8), :], h_refs[1][pl.ds(tr, 8), :], sub, True)
        hr = ar * cr + br
        h_refs[1][pl.ds(tr, 8), :] = hr
        return _last_row(af, 7) * cf + _last_row(bf, 7), _last_row(ar, 0) * cr + _last_row(br, 0)

    z8 = jnp.zeros((8, cb), F32)
    lax.fori_loop(0, N_TILE, tile, (z8, z8), unroll=2)
    return sp


def _rec_specs():
    up = pl.BlockSpec((T, REC_CB), lambda c: (0, _U_BLK + c))
    yb = pl.BlockSpec((T, REC_CB), lambda c: (0, _Y_BLK + c))
    cw = pl.BlockSpec((4, REC_CB), lambda c: (0, c))
    cbias = pl.BlockSpec((1, REC_CB), lambda c: (0, c))
    wbd = pl.BlockSpec((2, None, REC_CB, REC_CB), lambda c: (0, c, 0, 0))
    vec2 = pl.BlockSpec((2, REC_CB), lambda c: (0, c))
    col = pl.BlockSpec((T, REC_CB), lambda c: (0, c))
    return up, yb, cw, cbias, wbd, vec2, col


def _rec_fwd(z, conv_w, conv_b, wa, wi, ba, bi, lam):
    up, yb, cw, cbias, wbd, vec2, col = _rec_specs()

    def body(up_ref, yb_ref, cw_ref, cb_ref, wa_ref, wi_ref, ba_ref, bi_ref, lam_ref, g_ref,
             u_ref, af_ref, ar_ref, hf_ref, hr_ref, upad_ref):
        _rec_prologue(up_ref, cw_ref, cb_ref, wa_ref, wi_ref, ba_ref, bi_ref, lam_ref,
                      upad_ref, u_ref, (af_ref, ar_ref), (hf_ref, hr_ref))

        def chunk(c, carry):
            t0 = pl.multiple_of(c * REC_CHUNK, REC_CHUNK)
            rows = pl.ds(t0, REC_CHUNK)
            g_ref[rows, :] = ((hf_ref[rows, :] + hr_ref[rows, :]) * _gelu(yb_ref[rows, :].astype(F32))).astype(BF16)
            return carry

        lax.fori_loop(0, N_CHUNK, chunk, 0)

    res = pl.pallas_call(
        body, name="rec_fwd", grid=(N_CB,),
        in_specs=[up, yb, cw, cbias, wbd, wbd, vec2, vec2, vec2], out_specs=[col] * 6,
        out_shape=[_sds((T, D), BF16)] + [_sds((T, D), F32)] * 5,
        scratch_shapes=[pltpu.VMEM((T + 2 * PAD, REC_CB), F32)],
        compiler_params=_params(("parallel",)))(z, z, conv_w, conv_b, wa, wi, ba, bi, lam)
    return res[0], tuple(res[1:])


def _rec_bwd(z, dg, saved, dz, conv_w, conv_b, wa, wi, ba, bi, lam, after=()):
    up, yb, cw, cbias, wbd, vec2, col = _rec_specs()

    def body(up_ref, yb_ref, dg_ref, u_ref, af_ref, ar_ref, hf_ref, hr_ref,
             cw_ref, cb_ref, wa_ref, wi_ref, ba_ref, bi_ref, lam_ref, dz_in_ref, *rest):
        (dz_ref, dcw_ref, dcb_ref, dwa_out, dwi_out, dba_ref, dbi_ref, dlam_ref,
         upad_ref, dh_ref, gf_ref, gr_ref, daf_ref, dar_ref, dupad_ref, dwa_ref, dwi_ref,
         dup_ref, dyb_ref, sems) = rest[len(after):]
        g_refs, da_refs = (gf_ref, gr_ref), (daf_ref, dar_ref)
        cb = up_ref.shape[1]
        zeros = jnp.zeros((PAD, cb), F32)
        upad_ref[pl.ds(0, PAD), :] = zeros
        upad_ref[pl.ds(PAD + T, PAD), :] = zeros
        upad_ref[pl.ds(PAD, T), :] = up_ref[...].astype(F32)
        sp = _softplus_neg(lam_ref[...])

        def gate_chunk(c, carry):
            t0 = pl.multiple_of(c * REC_CHUNK, REC_CHUNK)
            rows = pl.ds(t0, REC_CHUNK)
            y = yb_ref[rows, :].astype(F32)
            dgv = dg_ref[rows, :].astype(F32)
            dh_ref[rows, :] = dgv * _gelu(y)
            dyb_ref[rows, :] = (dgv * (hf_ref[rows, :] + hr_ref[rows, :]) * _gelu_grad(y)).astype(BF16)
            return carry

        lax.fori_loop(0, N_CHUNK, gate_chunk, 0)
        cols = pl.multiple_of(pl.program_id(0) * REC_CB, REC_CB)
        dyb_copy = pltpu.make_async_copy(dyb_ref, dz_ref.at[:, pl.ds(DZ_Y + cols, REC_CB)], sems.at[1])
        dyb_copy.start()

        sub = lax.broadcasted_iota(jnp.int32, (8, cb), 0)

        def tile(k, carry):
            cf, cr = carry
            kf = N_TILE - 1 - k
            tf = pl.multiple_of(kf * 8, 8)
            tnext = pl.multiple_of(jnp.minimum(kf + 1, N_TILE - 1) * 8, 8)
            tprev = pl.multiple_of(jnp.maximum(kf - 1, 0) * 8, 8)
            a_t = af_ref[pl.ds(tf, 8), :]
            a_n = jnp.where(kf < N_TILE - 1, af_ref[pl.ds(tnext, 8), :], 0.0)
            a_sh = jnp.where(sub == 7, pltpu.roll(a_n, 7, 0), pltpu.roll(a_t, 7, 0))
            ca, cbb = _tile_scan(a_sh, dh_ref[pl.ds(tf, 8), :], sub, True)
            gf = ca * cf + cbb
            h_t = hf_ref[pl.ds(tf, 8), :]
            h_p = jnp.where(kf > 0, hf_ref[pl.ds(tprev, 8), :], 0.0)
            h_sh = jnp.where(sub == 0, pltpu.roll(h_p, 1, 0), pltpu.roll(h_t, 1, 0))
            gf_ref[pl.ds(tf, 8), :] = gf
            daf_ref[pl.ds(tf, 8), :] = gf * h_sh
            tr = pl.multiple_of(k * 8, 8)
            rnext = pl.multiple_of(jnp.minimum(k + 1, N_TILE - 1) * 8, 8)
            rprev = pl.multiple_of(jnp.maximum(k - 1, 0) * 8, 8)
            b_t = ar_ref[pl.ds(tr, 8), :]
            b_p = jnp.where(k > 0, ar_ref[pl.ds(rprev, 8), :], 0.0)
            b_sh = jnp.where(sub == 0, pltpu.roll(b_p, 1, 0), pltpu.roll(b_t, 1, 0))
            ra, rb = _tile_scan(b_sh, dh_ref[pl.ds(tr, 8), :], sub, False)
            gr = ra * cr + rb
            hr_t = hr_ref[pl.ds(tr, 8), :]
            hr_n = jnp.where(k < N_TILE - 1, hr_ref[pl.ds(rnext, 8), :], 0.0)
            hr_sh = jnp.where(sub == 7, pltpu.roll(hr_n, 7, 0), pltpu.roll(hr_t, 7, 0))
            gr_ref[pl.ds(tr, 8), :] = gr
            dar_ref[pl.ds(tr, 8), :] = gr * hr_sh
            return _last_row(gf, 0), _last_row(gr, 7)

        z8 = jnp.zeros((8, cb), F32)
        lax.fori_loop(0, N_TILE, tile, (z8, z8), unroll=2)

        dupad_ref[pl.ds(0, PAD), :] = zeros
        dupad_ref[pl.ds(PAD + T, PAD), :] = zeros
        dwa_ref[...] = jnp.zeros_like(dwa_ref)
        dwi_ref[...] = jnp.zeros_like(dwi_ref)
        dba_ref[...] = jnp.zeros_like(dba_ref)
        dbi_ref[...] = jnp.zeros_like(dbi_ref)
        dlam_ref[...] = jnp.zeros_like(dlam_ref)

        def grad_chunk(c, carry):
            t0 = pl.multiple_of(c * REC_CHUNK, REC_CHUNK)
            rows = pl.ds(t0, REC_CHUNK)
            u = u_ref[rows, :]
            ub = u.astype(BF16)
            du = jnp.zeros((REC_CHUNK, cb), F32)
            for d in range(2):
                r, i, a, mult, inv_mult = _gates(u, wa_ref[d], wi_ref[d], ba_ref[d:d + 1, :], bi_ref[d:d + 1, :],
                                                 sp[d:d + 1, :])
                dbx = g_refs[d][rows, :]
                dmult = dbx * (i * u)
                diu = dbx * mult
                a2 = a * a
                dlog = da_refs[d][rows, :] * a - dmult * (a2 * inv_mult)
                dpa = (dlog * (-LRU_C) * sp[d:d + 1, :]) * r * (1.0 - r)
                dpi = (diu * u) * i * (1.0 - i)
                dpab, dpib = dpa.astype(BF16), dpi.astype(BF16)
                du = du + diu * i + _dot(dpab, wa_ref[d], NT) + _dot(dpib, wi_ref[d], NT)
                dwa_ref[d] += _dot(ub, dpab, TN)
                dwi_ref[d] += _dot(ub, dpib, TN)
                dba_ref[d:d + 1, :] += jnp.sum(dpa, axis=0, keepdims=True)
                dbi_ref[d:d + 1, :] += jnp.sum(dpi, axis=0, keepdims=True)
                dlam_ref[d:d + 1, :] += jnp.sum(dlog * r, axis=0, keepdims=True)
            dupad_ref[pl.ds(PAD + t0, REC_CHUNK), :] = du
            return carry

        lax.fori_loop(0, N_CHUNK, grad_chunk, 0)
        cw = cw_ref[...]
        dcb = jnp.zeros((1, cb), F32)
        dcw = [jnp.zeros((1, cb), F32) for _ in range(4)]
        dup_copies = []
        per_piece = N_CHUNK // DUP_PIECES
        for c in range(N_CHUNK):
            t0 = c * REC_CHUNK
            du = dupad_ref[pl.ds(PAD + t0, REC_CHUNK), :]
            dcb = dcb + jnp.sum(du, axis=0, keepdims=True)
            for j in range(4):
                dcw[j] = dcw[j] + jnp.sum(du * upad_ref[pl.ds(PAD + t0 + j - 2, REC_CHUNK), :], axis=0, keepdims=True)
            dup_ref[pl.ds(t0, REC_CHUNK), :] = _conv_taps(dupad_ref, t0, cw, -1).astype(BF16)
            if (c + 1) % per_piece == 0:
                piece = c // per_piece
                rows = pl.ds(piece * per_piece * REC_CHUNK, per_piece * REC_CHUNK)
                dup_copies.append(pltpu.make_async_copy(
                    dup_ref.at[rows, :], dz_ref.at[rows, pl.ds(DZ_U + cols, REC_CB)], sems.at[2 + piece]))
                dup_copies[-1].start()
        dcb_ref[...] = dcb
        dcw_ref[...] = jnp.concatenate(dcw, axis=0)
        dlam_ref[...] = dlam_ref[...] * (LRU_C * _sigmoid(-lam_ref[...]))
        for d in range(2):
            for blk in range(REC_CB // REC_BLOCK):
                lo, hi = blk * REC_BLOCK, (blk + 1) * REC_BLOCK
                dwa_out[d, blk] = dwa_ref[d, lo:hi, lo:hi]
                dwi_out[d, blk] = dwi_ref[d, lo:hi, lo:hi]
        for cp in dup_copies:
            cp.wait()
        dyb_copy.wait()

    full = pltpu.VMEM((T, REC_CB), F32)
    padded = pltpu.VMEM((T + 2 * PAD, REC_CB), F32)
    per = REC_CB // REC_BLOCK
    diag = pl.BlockSpec((2, per, REC_BLOCK, REC_BLOCK), lambda c: (0, c, 0, 0))
    return pl.pallas_call(
        body, name="rec_bwd", grid=(N_CB,),
        in_specs=[up, yb] + [col] * 6 + [cw, cbias, wbd, wbd, vec2, vec2, vec2, _ANY] + [_ANY] * len(after),
        out_specs=[_ANY, cw, cbias, diag, diag, vec2, vec2, vec2],
        out_shape=[_sds(dz.shape, dz.dtype), _sds((4, D), F32), _sds((1, D), F32),
                   _sds((2, D // REC_BLOCK, REC_BLOCK, REC_BLOCK), F32), _sds((2, D // REC_BLOCK, REC_BLOCK, REC_BLOCK), F32),
                   _sds((2, D), F32), _sds((2, D), F32), _sds((2, D), F32)],
        input_output_aliases={15: 0},
        scratch_shapes=[padded, full, full, full, full, full, padded,
                        pltpu.VMEM((2, REC_CB, REC_CB), F32), pltpu.VMEM((2, REC_CB, REC_CB), F32),
                        pltpu.VMEM((T, REC_CB), BF16), pltpu.VMEM((T, REC_CB), BF16),
                        pltpu.SemaphoreType.DMA((2 + DUP_PIECES,))],
        compiler_params=_params(("parallel",)))(z, z, dg, *saved, conv_w, conv_b, wa, wi, ba, bi, lam, dz, *after)


class _NoReducer:
    def begin(self, tag, grads):
        return ()

    def advance(self, tag, after):
        return ()

    def interlude(self, tokens):
        return None


def _local_step(x, target, p, late=None, reducer=_NoReducer()):
    x = x.reshape(T, D)
    target = target.reshape(T, D)
    tb = _bias_pairs(p["rpb"])
    wa, wi = _block_diag(p["w_rg_a"]), _block_diag(p["w_rg_i"])

    h1 = _rms_fwd("rms1_fwd", x, p["ln1_g"], after=late[0] if late else ())
    if late:
        p = {**p, **late[1]((h1, tb, wa, wi))}
    rec_params = (p["conv_w"], p["conv_b"], wa, wi, p["b_rg_a"], p["b_rg_i"], p["lru_lambda"])
    (z,) = _mm_nn_cols("mm_z", h1, p["w_in"], BF16, bias=p["b_in"])
    att, probs = _attn_fwd(z, tb)
    g, rec_saved = _rec_fwd(z, *rec_params)
    if late:
        p = {**p, **late[2](g)}
    y_att, y_rec, mixed, x1, h2 = _branches_fwd(att, g, z, x, p["w_att_o"], p["w_rec_o"], p["w_out"], p["ln2_g"])

    def relu2(r, ex, outs):
        rp = jnp.maximum(r, 0.0)
        outs[0][...] = (rp * rp).astype(BF16)

    (s,) = _mm_nn_cols("mm_ff1", h2, p["w_ff1"], BF16, epilogue=relu2)
    loss, dx2, dx2_b, g_lnf = _mm_x2_loss_head(s, p["w_ff2"], x1, target, p["lnf_g"])

    def relu2_bwd(r, ex, outs):
        outs[0][...] = (r * 2.0 * jnp.sqrt(ex[0][...].astype(F32))).astype(BF16)

    (df,) = _mm_nt_rows("mm_df", dx2_b, p["w_ff2"], BF16, tn=D, extras=[s],
                        extra_specs=[pl.BlockSpec((TJ, D), lambda j, i, k: (i, j))], epilogue=relu2_bwd)
    (g_w_ff2,) = _mm_tn_rows("mm_g_ff2", s, dx2_b, tm=D)
    (g_w_ff1,) = _mm_tn_cols("mm_g_ff1", h2, df, D)
    tok = reducer.begin("ff", dict(w_ff2=g_w_ff2, w_ff1=g_w_ff1))
    dx1, dx1_b, g_ln2 = _mm_nt_cols_rms_bwd("mm_dh2_rms2_bwd", df, p["w_ff1"], x1, p["ln2_g"], dx2, after=tok,
                                            bf16_copy=True)

    dy_att, dy_rec, d_att, d_g, dz = _branches_bwd(dx1_b, y_att, y_rec, z, p["w_att_o"], p["w_rec_o"], p["w_out"])
    (g_w_out,) = _mm_tn_rows("mm_g_out", mixed, dx1_b, tm=D)
    (g_w_att_o,) = _mm_tn_cols("mm_g_att_o", att, dy_att, D // N_CHIPS)
    (g_w_rec_o,) = _mm_tn_rows("mm_g_rec_o", g, dy_rec, tm=D)
    tok = reducer.advance("ff", g_w_rec_o) + reducer.begin("proj", dict(w_out=g_w_out, w_att_o=g_w_att_o, w_rec_o=g_w_rec_o))

    dz, ds_acc = _attn_bwd(z, probs, d_att, dz, after=tok)
    g_rpb = _rpb_grad(ds_acc)
    tok = reducer.advance("proj", ds_acc)
    dz, g_conv_w, g_conv_b, g_wa, g_wi, g_ba, g_bi, g_lam = _rec_bwd(z, d_g, rec_saved, dz, *rec_params, after=tok)

    g_w_in, g_b_in = _mm_tn_cols("mm_g_in", h1, dz, D_IN // N_CHIPS, colsum=True)
    tok = reducer.advance("in", reducer.interlude(reducer.begin("in", dict(w_in=g_w_in))))
    grad_x, g_ln1 = _mm_nt_cols_rms_bwd("mm_dh1_rms1_bwd", dz, p["w_in"], x, p["ln1_g"], dx1, after=tok)

    grads = dict(ln1_g=g_ln1, w_in=g_w_in, b_in=g_b_in, rpb=g_rpb, w_att_o=g_w_att_o, conv_w=g_conv_w,
                 conv_b=g_conv_b, w_rg_a=g_wa, b_rg_a=g_ba, w_rg_i=g_wi,
                 b_rg_i=g_bi, lru_lambda=g_lam, w_rec_o=g_w_rec_o, w_out=g_w_out, ln2_g=g_ln2,
                 w_ff1=g_w_ff1, w_ff2=g_w_ff2, lnf_g=g_lnf)
    return loss, grad_x.reshape(1, T, D), grads


_ANY = pl.BlockSpec(memory_space=pl.ANY)
N_PEERS = N_CHIPS - 1


def _place():
    x, y, c = lax.axis_index("x"), lax.axis_index("y"), lax.axis_index("c")
    peers = [(1 - x, y), (x, 1 - y), (1 - x, 1 - y)]
    return x, y, c, 2 * x + y, peers


def _remote(src, dst, send_sem, recv_sem, dev):
    return pltpu.make_async_remote_copy(src_ref=src, dst_ref=dst, send_sem=send_sem, recv_sem=recv_sem,
                                        device_id=dev, device_id_type=MESH)


def _prefetch_call(body, name, ids, grid, in_specs, out_specs, out_shape, args, semantics=None):
    spec = pltpu.PrefetchScalarGridSpec(num_scalar_prefetch=1, grid=grid, in_specs=in_specs, out_specs=out_specs)
    return pl.pallas_call(body, name=name, grid_spec=spec, out_shape=out_shape,
                          compiler_params=_params(semantics or ("parallel",) * len(grid)))(ids, *args)


def _cast_bf16(name, w, chip_id, after=()):
    rows, cols = w.shape
    rb = min(rows, 256)

    def body(ids_ref, w_ref, *rest):
        rest[-1][...] = w_ref[...].astype(BF16)

    return _prefetch_call(body, name, chip_id, (rows // rb,),
                          [pl.BlockSpec((rb, cols), lambda i, ids: (i, 0))] + [_ANY] * len(after),
                          pl.BlockSpec((None, rb, cols), lambda i, ids: (ids[0], i, 0)),
                          _sds((N_CHIPS, rows, cols), BF16), (w, *after))


def _dma_sems(*counts):
    return [pltpu.SemaphoreType.DMA((k,)) for k in counts]


_HBM = pl.BlockSpec(memory_space=pltpu.HBM)
_SEM = pl.BlockSpec(memory_space=pltpu.SEMAPHORE)
_SPLIT_COPY = pltpu.CompilerParams(has_side_effects=pltpu.SideEffectType.DATAFLOW_SIDE_EFFECTING)
SIBLING_ID = 0
_SPLIT_COPY_SIBLING = pltpu.CompilerParams(has_side_effects=pltpu.SideEffectType.DATAFLOW_SIDE_EFFECTING,
                                           collective_id=SIBLING_ID)


def _sibling_handshake():
    x, y, c = lax.axis_index("x"), lax.axis_index("y"), lax.axis_index("c")
    barrier = pltpu.get_barrier_semaphore()
    pl.semaphore_signal(barrier, inc=1, device_id=(x, y, 1 - c), device_id_type=MESH)
    pl.semaphore_wait(barrier, 1)


def _hbm(arrays):
    return [pltpu.with_memory_space_constraint(a, pltpu.HBM) for a in arrays]


def _hbm_like(arrays):
    return [pltpu.HBM(a.shape, a.dtype) for a in arrays]


def _halves(buf, c):
    half = buf.shape[1] // 2
    return pl.ds(c * half, half), pl.ds((1 - c) * half, half)


def _gather_start(name, slots):
    n = len(slots)
    nk = n * N_PEERS

    def body(*refs):
        bufs = refs[n:2 * n]
        send_sems, recv_sems, token = refs[2 * n:]
        x, y, c, chip, peers = _place()
        for t in range(n):
            mine, _ = _halves(bufs[t], c)
            for r, (px, py) in enumerate(peers):
                k = t * N_PEERS + r
                own = bufs[t].at[chip, mine]
                _remote(own, own, send_sems.at[k], recv_sems.at[k], (px, py, c)).start()
        token[...] = jnp.zeros_like(token)

    res = pl.pallas_call(
        body, name=name, in_specs=[_HBM] * n, out_specs=[_HBM] * n + [_SEM, _SEM, pl.BlockSpec(memory_space=pltpu.VMEM)],
        out_shape=_hbm_like(slots) + [pltpu.SemaphoreType.DMA((nk,)), pltpu.SemaphoreType.DMA((nk,)),
                                      _sds((8, 128), F32)],
        input_output_aliases={t: t for t in range(n)}, compiler_params=_SPLIT_COPY)(*_hbm(slots))
    return res[:n], (res[n], res[n + 1]), res[n + 2]


def _gather_wait(name, bufs, sems, after):
    n = len(bufs)
    after = tuple(after) if isinstance(after, (tuple, list)) else (after,)

    def body(*refs):
        ins = refs[:n]
        send_sems, recv_sems = refs[n], refs[n + 1]
        x, y, c, chip, peers = _place()
        for t in range(n):
            mine, _ = _halves(ins[t], c)
            for r, (px, py) in enumerate(peers):
                k = t * N_PEERS + r
                cp = _remote(ins[t].at[chip, mine], ins[t].at[2 * px + py, mine], send_sems.at[k], recv_sems.at[k],
                             (px, py, c))
                cp.wait_send()
                cp.wait_recv()

    return pl.pallas_call(
        body, name=name, in_specs=[_HBM] * n + [_SEM, _SEM] + [_ANY] * len(after), out_specs=[_HBM] * n,
        out_shape=_hbm_like(bufs), input_output_aliases={t: t for t in range(n)},
        compiler_params=_SPLIT_COPY)(*bufs, *sems, *after)


def _gather_forward(name, bufs):
    n = len(bufs)
    nk = n * N_PEERS

    def body(*refs):
        _sibling_handshake()
        outs = refs[n:2 * n]
        send_sems, recv_sems = refs[2 * n:]
        x, y, c, chip, peers = _place()
        sibling = (x, y, 1 - c)
        sends = []
        for t in range(n):
            mine, _ = _halves(outs[t], c)
            for r, (px, py) in enumerate(peers):
                k = t * N_PEERS + r
                landed = outs[t].at[2 * px + py, mine]
                sends.append(_remote(landed, landed, send_sems.at[k], recv_sems.at[k], sibling))
                sends[-1].start()
        for t in range(n):
            _, theirs = _halves(outs[t], c)
            for r, (px, py) in enumerate(peers):
                k = t * N_PEERS + r
                landed = outs[t].at[2 * px + py, theirs]
                _remote(landed, landed, send_sems.at[k], recv_sems.at[k], sibling).wait_recv()
        for cp in sends:
            cp.wait_send()

    return pl.pallas_call(
        body, name=name, in_specs=[_ANY] * n, out_specs=[_ANY] * n, out_shape=[_sds(b.shape, b.dtype) for b in bufs],
        input_output_aliases={t: t for t in range(n)}, scratch_shapes=_dma_sems(nk, nk),
        compiler_params=pltpu.CompilerParams(collective_id=SIBLING_ID))(*bufs)


def _pair_copies(n, srcs, lands, send_sems, recv_sems):
    x, y, c, _, _ = _place()
    sibling = (x, y, 1 - c)
    copies = []
    for t in range(n):
        half = srcs[t].shape[1] // 2
        for j in range(N_CHIPS):
            k = t * N_CHIPS + j
            copies.append(_remote(srcs[t].at[j, pl.ds((1 - c) * half, half)], lands[t].at[j],
                                  send_sems.at[k], recv_sems.at[k], sibling))
    for t in range(n, len(srcs)):
        k = n * N_CHIPS + t - n
        copies.append(_remote(srcs[t], lands[t], send_sems.at[k], recv_sems.at[k], sibling))
    return copies


def _pair_start(name, grads, wholes=()):
    n = len(grads)
    srcs = list(grads) + list(wholes)
    m = len(srcs)
    lands = [pltpu.HBM((N_CHIPS, g.shape[1] // 2, g.shape[2]), F32) for g in grads] + _hbm_like(wholes)
    ns = n * N_CHIPS + len(wholes)

    def body(*refs):
        _sibling_handshake()
        src_refs, land_refs = refs[m:2 * m], refs[2 * m:3 * m]
        send_sems, recv_sems, token = refs[3 * m:]
        for cp in _pair_copies(n, src_refs, land_refs, send_sems, recv_sems):
            cp.start()
        token[...] = jnp.zeros_like(token)

    res = pl.pallas_call(
        body, name=name, in_specs=[_HBM] * m,
        out_specs=[_HBM] * (2 * m) + [_SEM, _SEM, pl.BlockSpec(memory_space=pltpu.VMEM)],
        out_shape=_hbm_like(srcs) + lands + [pltpu.SemaphoreType.DMA((ns,)), pltpu.SemaphoreType.DMA((ns,)),
                                             _sds((8, 128), F32)],
        input_output_aliases={t: t for t in range(m)}, compiler_params=_SPLIT_COPY_SIBLING)(*_hbm(srcs))
    return (res[:m], res[m:2 * m], (res[2 * m], res[2 * m + 1])), res[2 * m + 2]


def _pair_wait(name, flight, n, after):
    srcs, lands, sems = flight
    m = len(srcs)

    def body(*refs):
        for cp in _pair_copies(n, refs[:m], refs[m:2 * m], refs[2 * m], refs[2 * m + 1]):
            cp.wait_send()
            cp.wait_recv()

    res = pl.pallas_call(
        body, name=name, in_specs=[_HBM] * (2 * m) + [_SEM, _SEM, _ANY], out_specs=[_HBM] * (2 * m),
        out_shape=_hbm_like(srcs) + _hbm_like(lands), input_output_aliases={t: t for t in range(2 * m)},
        compiler_params=_SPLIT_COPY)(*srcs, *lands, *sems, after)
    return res[:m], res[m:]


def _chip_copies(srcs, lands, small_src, small_land, send_sems, recv_sems):
    x, y, c, chip, peers = _place()
    n = len(srcs)
    copies = []
    for r, (px, py) in enumerate(peers):
        for t in range(n):
            k = t * N_PEERS + r
            copies.append(_remote(srcs[t].at[2 * px + py], lands[t].at[r], send_sems.at[k], recv_sems.at[k], (px, py, c)))
        if small_src is not None:
            k = n * N_PEERS + r
            half_s = small_src.shape[0] // 2
            copies.append(_remote(small_src.at[pl.ds(c * half_s, half_s)], small_land.at[r],
                                  send_sems.at[k], recv_sems.at[k], (px, py, c)))
    return copies


def _chip_start(name, sums_bf16, small=None):
    n = len(sums_bf16)
    srcs = list(sums_bf16) + ([small] if small is not None else [])
    m = len(srcs)
    lands = [pltpu.HBM((N_PEERS,) + s.shape[1:], BF16) for s in sums_bf16]
    if small is not None:
        lands.append(pltpu.HBM((N_PEERS, small.shape[0] // 2, 128), F32))
    nk = m * N_PEERS

    def body(*refs):
        src_refs, land_refs = refs[m:2 * m], refs[2 * m:3 * m]
        send_sems, recv_sems, token = refs[3 * m:]
        small_src, small_land = (src_refs[n], land_refs[n]) if small is not None else (None, None)
        for cp in _chip_copies(src_refs[:n], land_refs[:n], small_src, small_land, send_sems, recv_sems):
            cp.start()
        token[...] = jnp.zeros_like(token)

    res = pl.pallas_call(
        body, name=name, in_specs=[_HBM] * m,
        out_specs=[_HBM] * (2 * m) + [_SEM, _SEM, pl.BlockSpec(memory_space=pltpu.VMEM)],
        out_shape=_hbm_like(srcs) + lands + [pltpu.SemaphoreType.DMA((nk,)), pltpu.SemaphoreType.DMA((nk,)),
                                             _sds((8, 128), F32)],
        input_output_aliases={t: t for t in range(m)}, compiler_params=_SPLIT_COPY)(*_hbm(srcs))
    return (res[:m], res[m:2 * m], (res[2 * m], res[2 * m + 1])), res[2 * m + 2]


def _chip_wait(name, flight, with_small, after):
    srcs, lands, sems = flight
    m = len(srcs)
    n = m - 1 if with_small else m

    def body(*refs):
        src_refs, land_refs = refs[:m], refs[m:2 * m]
        send_sems, recv_sems = refs[2 * m], refs[2 * m + 1]
        small_src, small_land = (src_refs[n], land_refs[n]) if with_small else (None, None)
        for cp in _chip_copies(src_refs[:n], land_refs[:n], small_src, small_land, send_sems, recv_sems):
            cp.wait_send()
            cp.wait_recv()

    res = pl.pallas_call(
        body, name=name, in_specs=[_HBM] * (2 * m) + [_SEM, _SEM, _ANY], out_specs=[_HBM] * (2 * m),
        out_shape=_hbm_like(srcs) + _hbm_like(lands), input_output_aliases={t: t for t in range(2 * m)},
        compiler_params=_SPLIT_COPY)(*srcs, *lands, *sems, after)
    return res[:m], res[m:]


def _swap_start(name, bufs):
    n = len(bufs)

    def body(*refs):
        _sibling_handshake()
        outs = refs[n:2 * n]
        send_sems, recv_sems, token = refs[2 * n:]
        x, y, c, _, _ = _place()
        for t in range(n):
            h = outs[t].shape[0] // 2
            mine = outs[t].at[pl.ds(c * h, h)]
            _remote(mine, mine, send_sems.at[t], recv_sems.at[t], (x, y, 1 - c)).start()
        token[...] = jnp.zeros_like(token)

    res = pl.pallas_call(
        body, name=name, in_specs=[_HBM] * n, out_specs=[_HBM] * n + [_SEM, _SEM, pl.BlockSpec(memory_space=pltpu.VMEM)],
        out_shape=_hbm_like(bufs) + [pltpu.SemaphoreType.DMA((n,)), pltpu.SemaphoreType.DMA((n,)), _sds((8, 128), F32)],
        input_output_aliases={t: t for t in range(n)}, compiler_params=_SPLIT_COPY_SIBLING)(*_hbm(bufs))
    return (res[:n], (res[n], res[n + 1])), res[n + 2]


def _swap_wait(name, flight, after):
    bufs, sems = flight
    n = len(bufs)

    def body(*refs):
        ins = refs[:n]
        send_sems, recv_sems = refs[n], refs[n + 1]
        x, y, c, _, _ = _place()
        for t in range(n):
            h = ins[t].shape[0] // 2
            cp = _remote(ins[t].at[pl.ds(c * h, h)], ins[t].at[pl.ds((1 - c) * h, h)], send_sems.at[t],
                         recv_sems.at[t], (x, y, 1 - c))
            cp.wait_send()
            cp.wait_recv()

    return pl.pallas_call(
        body, name=name, in_specs=[_HBM] * n + [_SEM, _SEM, _ANY], out_specs=[_HBM] * n, out_shape=_hbm_like(bufs),
        input_output_aliases={t: t for t in range(n)}, compiler_params=_SPLIT_COPY)(*bufs, *sems, after)


def _pair_sum(name, grad, got, ids):
    _, rows, cols = got.shape
    rb = min(rows, 256)
    nb = rows // rb
    blk = pl.BlockSpec((None, rb, cols), lambda i, j, ids: (j, i, 0))
    mine = pl.BlockSpec((None, rb, cols), lambda i, j, ids: (j, ids[1] * nb + i, 0))
    own = pl.BlockSpec((rb, cols), lambda i, j, ids: (i, 0))

    def body(ids_ref, a_ref, b_ref, s_ref, sb_ref):
        s = a_ref[...] + b_ref[...]
        sb_ref[...] = s.astype(BF16)

        @pl.when(pl.program_id(1) == ids_ref[0])
        def _():
            s_ref[...] = s

    return _prefetch_call(body, name, ids, (nb, N_CHIPS), [mine, blk], [own, blk],
                          [_sds((rows, cols), F32), _sds(got.shape, BF16)], (grad, got),
                          semantics=("parallel", "arbitrary"))


def _chip_sum(name, own_sum, got, ids):
    rows, cols = own_sum.shape
    rb = min(rows, 256)
    nb = rows // rb
    own = pl.BlockSpec((rb, cols), lambda i, ids: (i, 0))
    blk3 = pl.BlockSpec((N_PEERS, rb, cols), lambda i, ids: (0, i, 0))
    out = pl.BlockSpec((rb, cols), lambda i, ids: (ids[1] * nb + i, 0))

    def body(ids_ref, a_ref, b_ref, o_ref):
        o_ref[...] = ((a_ref[...] + b_ref[0].astype(F32)) + b_ref[1].astype(F32)) + b_ref[2].astype(F32)

    return _prefetch_call(body, name, ids, (nb,), [own, blk3], out, _sds((2 * rows, cols), F32), (own_sum, got))


SMALL_RB = 280


def _small_pair_sum(own, got):
    blk = pl.BlockSpec((SMALL_RB, 128), lambda i: (i, 0))

    def body(a_ref, b_ref, o_ref):
        o_ref[...] = a_ref[...] + b_ref[...]

    return pl.pallas_call(body, name="small_pair_sum", grid=(own.shape[0] // SMALL_RB,), in_specs=[blk, blk],
                          out_specs=blk, out_shape=_sds(own.shape, F32),
                          compiler_params=_params(("parallel",)))(own, got)


def _small_chip_sum(pair, got, ids):
    nb = pair.shape[0] // 2 // SMALL_RB
    half = pl.BlockSpec((SMALL_RB, 128), lambda i, ids: (ids[1] * nb + i, 0))
    blk3 = pl.BlockSpec((N_PEERS, SMALL_RB, 128), lambda i, ids: (0, i, 0))

    def body(ids_ref, a_ref, b_ref, o_ref):
        o_ref[...] = (a_ref[...] + b_ref[1]) + (b_ref[0] + b_ref[2])

    return _prefetch_call(body, "small_chip_sum", ids, (nb,), [half, blk3], half, _sds(pair.shape, F32), (pair, got))


def _adamw_math(w, g, m, v):
    m = ADAM_B1 * m + (1.0 - ADAM_B1) * g
    v = ADAM_B2 * v + (1.0 - ADAM_B2) * (g * g)
    m_hat = m / (1.0 - ADAM_B1 ** ADAM_STEP)
    v_hat = v / (1.0 - ADAM_B2 ** ADAM_STEP)
    delta = -ADAM_LR * (m_hat / (jnp.sqrt(v_hat) + ADAM_EPS) + ADAM_WD * w)
    return delta, m, v


def _adamw(name, w, g, m, v, rb=None):
    rows, cols = w.shape
    rb = rows if rb is None else rb
    blk = pl.BlockSpec((rb, cols), lambda i: (i, 0))

    def body(w_ref, g_ref, m_ref, v_ref, d_ref, nm_ref, nv_ref):
        d, nm, nv = _adamw_math(w_ref[...], g_ref[...], m_ref[...], v_ref[...])
        d_ref[...] = d
        nm_ref[...] = nm
        nv_ref[...] = nv

    return pl.pallas_call(body, name=name, grid=(rows // rb,), in_specs=[blk] * 4, out_specs=[blk] * 3,
                          out_shape=[_sds(w.shape, F32)] * 3, compiler_params=_params(("parallel",)))(w, g, m, v)


def _adamw_small(ws, gs, ms, vs):
    n = len(ws)

    def body(*refs):
        for t in range(n):
            w_ref, g_ref, m_ref, v_ref = (refs[k * n + t] for k in range(4))
            d, nm, nv = _adamw_math(w_ref[...], g_ref[...], m_ref[...], v_ref[...])
            for k, val in enumerate((d, nm, nv)):
                refs[(4 + k) * n + t][...] = val

    res = pl.pallas_call(body, name="adamw_small", out_shape=[_sds(a.shape, F32) for a in ws] * 3,
                         compiler_params=_params())(*ws, *gs, *ms, *vs)
    return [(res[t], res[n + t], res[2 * n + t]) for t in range(n)]


BIG = ("w_in", "w_att_o", "w_rec_o", "w_out", "w_ff1", "w_ff2")
SHARDED_VECS = ("conv_w", "b_rg_a", "b_rg_i", "lru_lambda")
SMALL = ("ln1_g", "b_in", "rpb", "conv_w", "conv_b", "w_rg_a", "b_rg_a", "w_rg_i", "b_rg_i", "lru_lambda",
         "ln2_g", "lnf_g")
SMALL_ROWS = 2240
ORDER = ("ln1_g", "w_in", "b_in", "rpb", "w_att_o", "conv_w", "conv_b", "w_rg_a", "b_rg_a", "w_rg_i", "b_rg_i",
         "lru_lambda", "w_rec_o", "w_out", "ln2_g", "w_ff1", "w_ff2", "lnf_g")


def _pack_small(grads, loss):
    parts, sizes = [], {}
    for n in SMALL:
        flat = grads[n].reshape(-1)
        pad = (-flat.shape[0]) % 128
        sizes[n] = (flat.shape[0], flat.shape[0] + pad)
        parts.append(jnp.pad(flat, (0, pad)))
    total = sum(s[1] for s in sizes.values())
    parts.append(jnp.pad(loss.reshape(1), (0, SMALL_ROWS * 128 - total - 1)))
    return jnp.concatenate(parts).reshape(SMALL_ROWS, 128), sizes


def _unpack_small(buf, sizes, shapes):
    flat = buf.reshape(-1)
    out, pos = {}, 0
    for n in SMALL:
        size, padded = sizes[n]
        out[n] = flat[pos:pos + size].reshape(shapes[n])
        pos += padded
    return out, flat[pos]


def _gather_weights(w, chip):
    chip_id = chip.astype(jnp.int32).reshape(1)
    vec_rows = [w[n][0] for n in SHARDED_VECS]
    vec_shard = jnp.concatenate(vec_rows + [jnp.zeros((16 - 10, D // N_CHIPS), F32)], axis=0)
    vec_slots = lax.dynamic_update_slice(jnp.zeros((N_CHIPS, 16, D // N_CHIPS), F32), vec_shard[None], (chip, 0, 0))
    bufs_a, sems_a, token_a = _gather_start("gather_start_first", [_cast_bf16("cast_w_in", w["w_in"][0], chip_id), vec_slots])
    rest_names = BIG[1:]
    bufs_b, sems_b, token_b = _gather_start(
        "gather_start_rest", [_cast_bf16("cast_" + n, w[n][0], chip_id, after=(token_a,)) for n in rest_names])

    def first(after):
        w_in_full, vec_full = _gather_forward("gather_forward_first", _gather_wait("gather_wait_first", bufs_a, sems_a, after))
        vecs = vec_full.transpose(1, 0, 2).reshape(16, D)
        return dict(w_in=w_in_full, conv_w=vecs[0:4], b_rg_a=vecs[4:6], b_rg_i=vecs[6:8], lru_lambda=vecs[8:10])

    def rest(after):
        full = dict(zip(rest_names, _gather_forward("gather_forward_rest",
                                                    _gather_wait("gather_wait_rest", bufs_b, sems_b, after))))
        return dict(w_att_o=full["w_att_o"], w_ff1=full["w_ff1"], w_rec_o=full["w_rec_o"].reshape(D, D),
                    w_out=full["w_out"].reshape(D, D), w_ff2=full["w_ff2"].reshape(D_FF, D))

    p = dict(ln1_g=w["ln1_g"], b_in=w["b_in"], rpb=w["rpb"][0], conv_b=w["conv_b"], w_rg_a=w["w_rg_a"][0],
             w_rg_i=w["w_rg_i"][0], ln2_g=w["ln2_g"], lnf_g=w["lnf_g"].reshape(1, D))
    return p, ((token_b,), first, rest)


class _Reducer:
    def __init__(self, ids):
        self.ids = ids
        self.groups = {}

    def begin(self, tag, grads, small=None):
        names = list(grads)
        big = [grads[n].reshape(N_CHIPS, -1, grads[n].shape[-1]) for n in names]
        flight, token = _pair_start("pair_start_" + tag, big, [] if small is None else [small])
        self.groups[tag] = dict(names=names, pair=flight, small=small is not None)
        return (token,)

    def advance(self, tag, after):
        grp = self.groups[tag]
        n = len(grp["names"])
        mine, got = _pair_wait("pair_wait_" + tag, grp["pair"], n, after)
        sums = [_pair_sum("pair_sum_" + name, a, b, self.ids) for name, a, b in zip(grp["names"], mine, got)]
        small_sum = _small_pair_sum(mine[n], got[n]) if grp["small"] else None
        grp["chip"], token = _chip_start("chip_start_" + tag, [s[1] for s in sums], small_sum)
        grp["sums"] = [s[0] for s in sums]
        self.last_token = token
        return (token,)

    def finish(self, tag, after):
        grp = self.groups[tag]
        srcs, lands = _chip_wait("chip_wait_" + tag, grp["chip"], grp["small"], after)
        halves = [_chip_sum("chip_sum_" + name, s, b, self.ids) for name, s, b in zip(grp["names"], grp["sums"], lands)]
        if grp["small"]:
            halves.append(_small_chip_sum(srcs[-1], lands[-1], self.ids))
        grp["swap"], token = _swap_start("swap_start_" + tag, halves)
        return token

    def interlude(self, tokens):
        after = tokens[0]
        for tag in list(self.groups)[:-1]:
            after = self.finish(tag, after)
        return after

    def result(self, tag, after):
        return _swap_wait("swap_wait_" + tag, self.groups[tag]["swap"], after)


def kernel(x, ln1_g, w_in, b_in, rpb, w_att_o, conv_w, conv_b, w_rg_a, b_rg_a, w_rg_i, b_rg_i, lru_lambda, w_rec_o, w_out, ln2_g, w_ff1, w_ff2, lnf_g, loss_target, m_ln1_g, m_w_in, m_b_in, m_rpb, m_w_att_o, m_conv_w, m_conv_b, m_w_rg_a, m_b_rg_a, m_w_rg_i, m_b_rg_i, m_lru_lambda, m_w_rec_o, m_w_out, m_ln2_g, m_w_ff1, m_w_ff2, m_lnf_g, v_ln1_g, v_w_in, v_b_in, v_rpb, v_w_att_o, v_conv_w, v_conv_b, v_w_rg_a, v_b_rg_a, v_w_rg_i, v_b_rg_i, v_lru_lambda, v_w_rec_o, v_w_out, v_ln2_g, v_w_ff1, v_w_ff2, v_lnf_g):
    w = dict(ln1_g=ln1_g, w_in=w_in, b_in=b_in, rpb=rpb, w_att_o=w_att_o, conv_w=conv_w, conv_b=conv_b,
             w_rg_a=w_rg_a, b_rg_a=b_rg_a, w_rg_i=w_rg_i, b_rg_i=b_rg_i, lru_lambda=lru_lambda, w_rec_o=w_rec_o,
             w_out=w_out, ln2_g=ln2_g, w_ff1=w_ff1, w_ff2=w_ff2, lnf_g=lnf_g)
    m = dict(ln1_g=m_ln1_g, w_in=m_w_in, b_in=m_b_in, rpb=m_rpb, w_att_o=m_w_att_o, conv_w=m_conv_w,
             conv_b=m_conv_b, w_rg_a=m_w_rg_a, b_rg_a=m_b_rg_a, w_rg_i=m_w_rg_i, b_rg_i=m_b_rg_i,
             lru_lambda=m_lru_lambda, w_rec_o=m_w_rec_o, w_out=m_w_out, ln2_g=m_ln2_g, w_ff1=m_w_ff1,
             w_ff2=m_w_ff2, lnf_g=m_lnf_g)
    v = dict(ln1_g=v_ln1_g, w_in=v_w_in, b_in=v_b_in, rpb=v_rpb, w_att_o=v_w_att_o, conv_w=v_conv_w,
             conv_b=v_conv_b, w_rg_a=v_w_rg_a, b_rg_a=v_b_rg_a, w_rg_i=v_w_rg_i, b_rg_i=v_b_rg_i,
             lru_lambda=v_lru_lambda, w_rec_o=v_w_rec_o, w_out=v_w_out, ln2_g=v_ln2_g, w_ff1=v_w_ff1,
             w_ff2=v_w_ff2, lnf_g=v_lnf_g)
    chip = 2 * lax.axis_index("x") + lax.axis_index("y")
    ids = jnp.stack([chip, lax.axis_index("c")]).astype(jnp.int32)

    out_grad, out_delta, out_m, out_v = {}, {}, {}, {}

    def update(n, gn):
        shape, two_d = w[n].shape, gn.shape
        d, nm, nv = _adamw("adamw_" + n, w[n].reshape(two_d), gn, m[n].reshape(two_d), v[n].reshape(two_d), 256)
        out_grad[n], out_delta[n], out_m[n], out_v[n] = (gn.reshape(shape), d.reshape(shape), nm.reshape(shape),
                                                         nv.reshape(shape))
        return d

    reducer = _Reducer(ids)
    p, late = _gather_weights(w, chip)
    loss, grad_x, g = _local_step(x, loss_target, p, late, reducer)
    small, sizes = _pack_small(g, loss + reducer.last_token[:1, :1])
    after = reducer.begin("small", {}, small)[0]
    for tag in ("ff", "proj", "in"):
        for n, red in zip(reducer.groups[tag]["names"], reducer.result(tag, after)):
            after = update(n, red)
        if tag == "ff":
            after = reducer.finish("in", reducer.advance("small", after)[0])
    (small_red,) = reducer.result("small", reducer.finish("small", after))
    gsmall, loss = _unpack_small(small_red, sizes, {n: g[n].shape for n in SMALL})
    two_d = {n: (int(np.prod(w[n].shape[:-1])), w[n].shape[-1]) for n in SMALL}
    for n in SHARDED_VECS:
        gsmall[n] = lax.dynamic_slice_in_dim(gsmall[n], chip * (D // N_CHIPS), D // N_CHIPS, axis=1)
    gs = [gsmall[n].reshape(two_d[n]) for n in SMALL]
    updates = _adamw_small([w[n].reshape(two_d[n]) for n in SMALL], gs, [m[n].reshape(two_d[n]) for n in SMALL],
                           [v[n].reshape(two_d[n]) for n in SMALL])
    for n, gn, (d, nm, nv) in zip(SMALL, gs, updates):
        shape = w[n].shape
        out_grad[n], out_delta[n], out_m[n], out_v[n] = (gn.reshape(shape), d.reshape(shape), nm.reshape(shape),
                                                         nv.reshape(shape))
    return (loss, grad_x, *[out_grad[n] for n in ORDER], *[out_delta[n] for n in ORDER],
            *[out_m[n] for n in ORDER], *[out_v[n] for n in ORDER])
```

```python
import numpy as np
import jax
import jax.numpy as jnp
from jax import lax
from jax.experimental import pallas as pl
from jax.experimental.pallas import tpu as pltpu

F32 = jnp.float32
BF16 = jnp.bfloat16

T = 2048
D = 1024
D_ATT = 512
D_IN = 5632
D_FF = 4096
N_HEADS = 8
HEAD_DIM = 64
GRID_W = 64
N_ROWS = T // GRID_W
WIN_H = 8
WIN_W = 16
KEYS = WIN_H * GRID_W
N_CHIPS = 4
EPS = 1e-6
LRU_C = 8.0
SCALE = HEAD_DIM ** -0.5
REC_CB = 256
REC_BLOCK = 64
REC_CHUNK = 256
PAD = 8

ADAM_LR = 0.001
ADAM_B1 = 0.9
ADAM_B2 = 0.999
ADAM_EPS = 1e-08
ADAM_WD = 0.01
ADAM_STEP = 10

VMEM_LIMIT = 56 * 1024 * 1024

NN = (((1,), (0,)), ((), ()))
NT = (((1,), (1,)), ((), ()))
TN = (((0,), (0,)), ((), ()))
MESH = pl.DeviceIdType.MESH


def _params(sem=None):
    return pltpu.CompilerParams(dimension_semantics=sem, vmem_limit_bytes=VMEM_LIMIT)


def _dot(a, b, dims):
    return lax.dot_general(a, b, dims, preferred_element_type=F32)


def _sigmoid(x):
    return 0.5 * jnp.tanh(0.5 * x) + 0.5


def _matmul(name, a, b, *, dims, grid, a_spec, b_spec, out_shapes, out_specs, acc_shape,
            extras=(), extra_specs=(), epilogue=None, colsum_spec=None, colsum_shape=None, after=(),
            semantics=("parallel", "parallel", "arbitrary"), epilogue_takes_first=False):
    nk = grid[2]
    n_extra = len(extras)
    n_out = len(out_shapes)
    with_colsum = colsum_spec is not None

    def body(a_ref, b_ref, *rest):
        ex = rest[:n_extra]
        rest = rest[:n_extra] + rest[n_extra + len(after):]
        outs = rest[n_extra:n_extra + n_out]
        pos = n_extra + n_out
        cs_out = rest[pos] if with_colsum else None
        pos += 1 if with_colsum else 0
        acc = rest[pos]
        cs_acc = rest[pos + 1] if with_colsum else None
        k = pl.program_id(2)
        first_tile = pl.program_id(0) == 0

        @pl.when(k == 0)
        def _():
            acc[...] = jnp.zeros_like(acc)
            if with_colsum:
                cs_acc[...] = jnp.zeros_like(cs_acc)

        bv = b_ref[...]
        acc[...] += _dot(a_ref[...].astype(BF16), bv.astype(BF16), dims)
        if with_colsum:
            cs_acc[...] += jnp.sum(bv.astype(F32), axis=0, keepdims=True)

        @pl.when(k == nk - 1)
        def _():
            r = acc[...]
            if epilogue is None:
                outs[0][...] = r.astype(outs[0].dtype)
            elif epilogue_takes_first:
                epilogue(r, ex, outs, first_tile)
            else:
                epilogue(r, ex, outs)
            if with_colsum:
                cs_out[...] = cs_acc[...]

    shapes = list(out_shapes)
    specs = list(out_specs)
    scratch = [pltpu.VMEM(acc_shape, F32)]
    if with_colsum:
        shapes.append(colsum_shape)
        specs.append(colsum_spec)
        scratch.append(pltpu.VMEM((1, acc_shape[1]), F32))
    res = pl.pallas_call(
        body, name=name, grid=grid,
        in_specs=[a_spec, b_spec, *extra_specs] + [_ANY] * len(after),
        out_specs=specs, out_shape=shapes, scratch_shapes=scratch,
        compiler_params=_params(semantics),
    )(a, b, *extras, *after)
    return res


def _sds(shape, dtype):
    return jax.ShapeDtypeStruct(shape, dtype)


TM = 1024
NI = T // TM
TJ = T
NJ = T // TJ


def _mm_nn_cols(name, a, wg, out_dtype, *, bias=None, extras=(), extra_specs=(), epilogue=None,
                out_shapes=None, out_specs=None):
    k_dim, n4 = wg.shape[1], wg.shape[2]
    ex, exs = list(extras), list(extra_specs)
    if bias is not None:
        ex = [bias] + ex
        exs = [pl.BlockSpec((1, n4), lambda j, i, k: (0, j))] + exs
        user_ep = epilogue

        def epilogue(r, e, outs):
            r = r + e[0][...]
            if user_ep is None:
                outs[0][...] = r.astype(outs[0].dtype)
            else:
                user_ep(r, e[1:], outs)
    if out_shapes is None:
        out_shapes = [_sds((T, N_CHIPS * n4), out_dtype)]
        out_specs = [pl.BlockSpec((TJ, n4), lambda j, i, k: (i, j))]
    return _matmul(
        name, a, wg, dims=NN, grid=(N_CHIPS, NJ, 1),
        a_spec=pl.BlockSpec((TJ, k_dim), lambda j, i, k: (i, 0)),
        b_spec=pl.BlockSpec((None, k_dim, n4), lambda j, i, k: (j, 0, 0)),
        out_shapes=out_shapes, out_specs=out_specs, acc_shape=(TJ, n4),
        extras=ex, extra_specs=exs, epilogue=epilogue)


def _mm_nt_cols_rms_bwd(name, a, wg, x, g, dres, after=(), bf16_copy=False):
    n4 = wg.shape[2]
    row = pl.BlockSpec((TM, D), lambda i, j, k: (i, 0))
    vec = pl.BlockSpec((1, D), lambda i, j, k: (0, 0))

    def epilogue(dhv, ex, outs, first):
        x_ref, g_ref, dres_ref = ex
        dx_ref, dg_ref = outs[0], outs[-1]
        xv = x_ref[...]
        rstd = lax.rsqrt(jnp.mean(xv * xv, axis=-1, keepdims=True) + EPS)
        xhat = xv * rstd
        dy = dhv * g_ref[...]
        dx = dres_ref[...] + rstd * (dy - xhat * jnp.mean(dy * xhat, axis=-1, keepdims=True))
        dx_ref[...] = dx
        if bf16_copy:
            outs[1][...] = dx.astype(BF16)
        part = jnp.sum(dhv * xhat, axis=0, keepdims=True)

        @pl.when(first)
        def _():
            dg_ref[...] = part

        @pl.when(jnp.logical_not(first))
        def _():
            dg_ref[...] += part

    return _matmul(
        name, a, wg, dims=NT, grid=(NI, 1, N_CHIPS),
        a_spec=pl.BlockSpec((TM, n4), lambda i, j, k: (i, k)),
        b_spec=pl.BlockSpec((None, D, n4), lambda i, j, k: (k, 0, 0)),
        out_shapes=[_sds((T, D), F32)] + [_sds((T, D), BF16)] * bf16_copy + [_sds((1, D), F32)],
        out_specs=[row] + [row] * bf16_copy + [vec], acc_shape=(TM, D),
        extras=[x, g, dres], extra_specs=[row, vec, row], epilogue=epilogue, after=after,
        semantics=("arbitrary", "arbitrary", "arbitrary"), epilogue_takes_first=True)


def _mm_nt_rows(name, a, w, out_dtype, *, tn, extras=(), extra_specs=(), epilogue=None):
    k_dim, n = w.shape
    return _matmul(
        name, a, w, dims=NT, grid=(k_dim // tn, NJ, 1),
        a_spec=pl.BlockSpec((TJ, n), lambda j, i, k: (i, 0)),
        b_spec=pl.BlockSpec((tn, n), lambda j, i, k: (j, 0)),
        out_shapes=[_sds((T, k_dim), out_dtype)],
        out_specs=[pl.BlockSpec((TJ, tn), lambda j, i, k: (i, j))], acc_shape=(TJ, tn),
        extras=extras, extra_specs=extra_specs, epilogue=epilogue)


def _mm_tn_cols(name, a, g, n4, *, colsum=False):
    k_dim = a.shape[1]
    kw = {}
    if colsum:
        kw = dict(colsum_spec=pl.BlockSpec((1, n4), lambda j, i, k: (0, j)),
                  colsum_shape=_sds((1, N_CHIPS * n4), F32))
    return _matmul(
        name, a, g, dims=TN, grid=(N_CHIPS, 1, NJ),
        a_spec=pl.BlockSpec((TJ, k_dim), lambda j, i, k: (k, 0)),
        b_spec=pl.BlockSpec((TJ, n4), lambda j, i, k: (k, j)),
        out_shapes=[_sds((N_CHIPS, k_dim, n4), F32)],
        out_specs=[pl.BlockSpec((None, k_dim, n4), lambda j, i, k: (j, 0, 0))],
        acc_shape=(k_dim, n4), **kw)


def _mm_tn_rows(name, a, g, *, tm):
    k_dim, n = a.shape[1], g.shape[1]
    return _matmul(
        name, a, g, dims=TN, grid=(k_dim // tm, 1, NJ),
        a_spec=pl.BlockSpec((TJ, tm), lambda j, i, k: (k, j)),
        b_spec=pl.BlockSpec((TJ, n), lambda j, i, k: (k, 0)),
        out_shapes=[_sds((k_dim, n), F32)],
        out_specs=[pl.BlockSpec((tm, n), lambda j, i, k: (j, 0))], acc_shape=(tm, n))


TE = 256
NE = T // TE
_ROW = pl.BlockSpec((TE, D), lambda i: (i, 0))
_VEC = pl.BlockSpec((1, D), lambda i: (0, 0))


def _rms_fwd(name, x, g, after=()):
    def body(x_ref, g_ref, *rest):
        h_ref = rest[-1]
        xv = x_ref[...]
        rstd = lax.rsqrt(jnp.mean(xv * xv, axis=-1, keepdims=True) + EPS)
        h_ref[...] = (xv * rstd * g_ref[...]).astype(BF16)

    return pl.pallas_call(body, name=name, grid=(NE,), in_specs=[_ROW, _VEC] + [_ANY] * len(after), out_specs=_ROW,
                          out_shape=_sds((T, D), BF16), compiler_params=_params(("parallel",)))(x, g, *after)


def _mm_x2_loss_head(s, w_ff2, x1, target, g):
    k_dim = w_ff2.shape[0]
    row = pl.BlockSpec((TM, D), lambda i, j, k: (i, 0))
    vec = pl.BlockSpec((1, D), lambda i, j, k: (0, 0))

    def epilogue(r, ex, outs, first):
        x1_ref, t_ref, g_ref = ex
        loss_ref, dx_ref, dxb_ref, dg_ref = outs
        xv = x1_ref[...] + r
        rstd = lax.rsqrt(jnp.mean(xv * xv, axis=-1, keepdims=True) + EPS)
        xhat = xv * rstd
        gv = g_ref[...]
        err = xhat * gv - t_ref[...]
        dy = err * (1.0 / D)
        dxh = dy * gv
        dx = rstd * (dxh - xhat * jnp.mean(dxh * xhat, axis=-1, keepdims=True))
        dx_ref[...] = dx
        dxb_ref[...] = dx.astype(BF16)
        dg_part = jnp.sum(dy * xhat, axis=0, keepdims=True)
        loss_part = (0.5 / D) * jnp.sum(jnp.sum(err * err, axis=1, keepdims=True), axis=0, keepdims=True)

        @pl.when(first)
        def _():
            dg_ref[...] = dg_part
            loss_ref[...] = loss_part

        @pl.when(jnp.logical_not(first))
        def _():
            dg_ref[...] += dg_part
            loss_ref[...] += loss_part

    return _matmul(
        "mm_x2_loss_head", s, w_ff2, dims=NN, grid=(NI, 1, k_dim // D),
        a_spec=pl.BlockSpec((TM, D), lambda i, j, k: (i, k)), b_spec=pl.BlockSpec((D, D), lambda i, j, k: (k, 0)),
        out_shapes=[_sds((1, 1), F32), _sds((T, D), F32), _sds((T, D), BF16), _sds((1, D), F32)],
        out_specs=[pl.BlockSpec((1, 1), lambda i, j, k: (0, 0)), row, row, vec], acc_shape=(TM, D),
        extras=[x1, target, g], extra_specs=[row, row, vec], epilogue=epilogue,
        semantics=("arbitrary", "arbitrary", "arbitrary"), epilogue_takes_first=True)


DZ_Q, DZ_K, DZ_V, DZ_U, DZ_Y, DZ_G_ATT, DZ_G_REC = 0, 512, 1024, 1536, 2560, 3584, 4608
MW = 512
_G_ATT_BLK = 3584 // MW
_G_REC_BLK = 4608 // MW


TB = 512


def _branch_specs():
    def row(cols):
        return pl.BlockSpec((TB, cols), lambda i: (i, 0))

    ga = pl.BlockSpec((TB, MW), lambda i: (i, _G_ATT_BLK))
    ga2 = pl.BlockSpec((TB, MW), lambda i: (i, _G_ATT_BLK + 1))
    gr = pl.BlockSpec((TB, MW), lambda i: (i, _G_REC_BLK))
    gr2 = pl.BlockSpec((TB, MW), lambda i: (i, _G_REC_BLK + 1))
    w_att = pl.BlockSpec((N_CHIPS, D_ATT, D // N_CHIPS), lambda i: (0, 0, 0))
    w_sq = pl.BlockSpec((D, D), lambda i: (0, 0))
    return row, (ga, ga2, gr, gr2), w_att, w_sq


def _gate_values(gate_refs):
    ga, ga2, gr, gr2 = (r[...].astype(F32) for r in gate_refs)
    return _sigmoid(jnp.concatenate([ga, ga2], axis=1)), _sigmoid(jnp.concatenate([gr, gr2], axis=1))


def _branches_fwd(att, g, z, x, w_att_o, w_rec_o, w_out, ln2_g):
    row, gate_specs, w_att, w_sq = _branch_specs()

    def body(att_ref, g_ref, ga_ref, ga2_ref, gr_ref, gr2_ref, x_ref, wa_ref, wr_ref, wo_ref, g2_ref,
             ya_ref, yr_ref, m_ref, x1_ref, h2_ref):
        attv = att_ref[...]
        ya = jnp.concatenate([_dot(attv, wa_ref[j], NN) for j in range(N_CHIPS)], axis=1)
        yr = _dot(g_ref[...], wr_ref[...], NN)
        sa, sr = _gate_values((ga_ref, ga2_ref, gr_ref, gr2_ref))
        mixed = (sa * ya + sr * yr).astype(BF16)
        ya_ref[...] = ya
        yr_ref[...] = yr
        m_ref[...] = mixed
        x1 = x_ref[...] + _dot(mixed, wo_ref[...], NN)
        x1_ref[...] = x1
        rstd = lax.rsqrt(jnp.mean(x1 * x1, axis=-1, keepdims=True) + EPS)
        h2_ref[...] = (x1 * rstd * g2_ref[...]).astype(BF16)

    return pl.pallas_call(
        body, name="branches_fwd", grid=(T // TB,),
        in_specs=[row(D_ATT), row(D), *gate_specs, row(D), w_att, w_sq, w_sq, pl.BlockSpec((1, D), lambda i: (0, 0))],
        out_specs=[row(D)] * 5,
        out_shape=[_sds((T, D), F32), _sds((T, D), F32), _sds((T, D), BF16), _sds((T, D), F32), _sds((T, D), BF16)],
        compiler_params=_params(("parallel",)))(att, g, z, z, z, z, x, w_att_o, w_rec_o, w_out, ln2_g)


def _branches_bwd(dx1_b, y_att, y_rec, z, w_att_o, w_rec_o, w_out):
    row, gate_specs, w_att, w_sq = _branch_specs()
    n4 = D // N_CHIPS

    def body(dx_ref, ya_ref, yr_ref, ga_ref, ga2_ref, gr_ref, gr2_ref, wa_ref, wr_ref, wo_ref,
             dya_ref, dyr_ref, datt_ref, dg_ref, dz_ref, dga_ref, dgr_ref, sems):
        rows = pl.ds(pl.multiple_of(pl.program_id(0) * TB, TB), TB)
        dm = _dot(dx_ref[...], wo_ref[...], NT)
        sa, sr = _gate_values((ga_ref, ga2_ref, gr_ref, gr2_ref))
        dya = (dm * sa).astype(BF16)
        dyr = (dm * sr).astype(BF16)
        dya_ref[...] = dya
        dyr_ref[...] = dyr
        dga_ref[...] = (dm * ya_ref[...] * sa * (1.0 - sa)).astype(BF16)
        dgr_ref[...] = (dm * yr_ref[...] * sr * (1.0 - sr)).astype(BF16)
        copies = [pltpu.make_async_copy(dga_ref, dz_ref.at[rows, pl.ds(DZ_G_ATT, D)], sems.at[0]),
                  pltpu.make_async_copy(dgr_ref, dz_ref.at[rows, pl.ds(DZ_G_REC, D)], sems.at[1])]
        for cp in copies:
            cp.start()
        datt = _dot(dya[:, 0:n4], wa_ref[0], NT)
        for j in range(1, N_CHIPS):
            datt = datt + _dot(dya[:, j * n4:(j + 1) * n4], wa_ref[j], NT)
        datt_ref[...] = datt.astype(BF16)
        dg_ref[...] = _dot(dyr, wr_ref[...], NT).astype(BF16)
        for cp in copies:
            cp.wait()

    return pl.pallas_call(
        body, name="branches_bwd", grid=(T // TB,),
        in_specs=[row(D), row(D), row(D), *gate_specs, w_att, w_sq, w_sq],
        out_specs=[row(D)] * 2 + [row(D_ATT), row(D), _ANY],
        out_shape=[_sds((T, D), BF16)] * 2 + [_sds((T, D_ATT), BF16), _sds((T, D), BF16), _sds((T, D_IN), BF16)],
        scratch_shapes=[pltpu.VMEM((TB, D), BF16), pltpu.VMEM((TB, D), BF16), pltpu.SemaphoreType.DMA((2,))],
        compiler_params=_params(("parallel",)))(dx1_b, y_att, y_rec, z, z, z, z, w_att_o, w_rec_o, w_out)


HP = 2 * HEAD_DIM
N_HP = N_HEADS // 2
ATT_UNROLL_FWD = 32
ATT_UNROLL_BWD = 32
DIAG_ROWS = 32


def _window_maps():
    diag = np.zeros((GRID_W * GRID_W, 128), np.float32)
    for qc in range(GRID_W):
        w0 = min(max(qc - WIN_W // 2, 0), GRID_W - WIN_W)
        for kc in range(w0, w0 + WIN_W):
            diag[qc * GRID_W + kc, kc - qc + WIN_W - 1] = 1.0
    return diag, diag.sum(axis=1)[None, :]


def _split3(x):
    a = x.astype(BF16)
    r = x - a.astype(F32)
    b = r.astype(BF16)
    c = (r - b.astype(F32)).astype(BF16)
    return a, b, c


N_DROW = 2 * WIN_H - 1
N_DPAIR = N_DROW - 1


def _bias_pairs(rpb):
    diag, valid = _window_maps()
    r2 = jnp.pad(rpb.reshape(N_HEADS * N_DROW, 2 * WIN_W - 1),
                 ((0, 128 - N_HEADS * N_DROW), (0, 128 - (2 * WIN_W - 1))))

    def body(r_ref, d_ref, v_ref, o_ref):
        dv = d_ref[...]
        t = sum(_dot(part, dv, NN) for part in _split3(r_ref[...]))
        o_ref[...] = jnp.where(v_ref[...] > 0.0, t, -1e30)

    t = pl.pallas_call(body, name="rpb_expand", out_shape=_sds((128, GRID_W * GRID_W), F32),
                       compiler_params=_params())(r2, jnp.asarray(diag.T, BF16), jnp.asarray(valid, F32))
    t = t[:N_HEADS * N_DROW].reshape(N_HEADS, N_DROW, GRID_W, GRID_W)
    return jnp.concatenate([t[:, :N_DPAIR], t[:, 1:]], axis=-1)


def _row_bias(tb_ref, hh, d0):
    return jnp.concatenate([tb_ref[hh, d0 + 2 * ii] for ii in range(WIN_H // 2)], axis=1)


def _row_window(r):
    rs = jnp.clip(r - WIN_H // 2, 0, N_ROWS - WIN_H)
    return pl.multiple_of(r * GRID_W, GRID_W), pl.multiple_of(rs * GRID_W, GRID_W), rs - r + (WIN_H - 1)


def _split_heads(src_ref, dst_ref, scale=None):
    for hh in range(2):
        v = src_ref[:, hh * HEAD_DIM:(hh + 1) * HEAD_DIM]
        dst_ref[hh] = (v if scale is None else v * scale).astype(BF16)


def _attn_items(qb_ref, kb_ref, vb_ref, tb_ref, first_row, n_rows):
    wins = [_row_window(first_row + u) for u in range(n_rows)]
    items = [(u, hh) for u in range(n_rows) for hh in range(2)]
    q = [qb_ref[hh, pl.ds(wins[u][0], GRID_W), :] for u, hh in items]
    k = [kb_ref[hh, pl.ds(wins[u][1], KEYS), :] for u, hh in items]
    v = [vb_ref[hh, pl.ds(wins[u][1], KEYS), :] for u, hh in items]
    s = [_dot(qi, ki, NT) + _row_bias(tb_ref, hh, wins[u][2]) for qi, ki, (u, hh) in zip(q, k, items)]
    m = [jnp.max(si, axis=-1, keepdims=True) for si in s]
    e = [jnp.exp(si - mi) for si, mi in zip(s, m)]
    inv = [1.0 / jnp.sum(ei, axis=-1, keepdims=True) for ei in e]
    p = [ei * li for ei, li in zip(e, inv)]
    return wins, items, q, k, v, p


def _attn_in_specs():
    q = pl.BlockSpec((T, HP), lambda p: (0, p))
    k = pl.BlockSpec((T, HP), lambda p: (0, N_HP + p))
    v = pl.BlockSpec((T, HP), lambda p: (0, 2 * N_HP + p))
    tb = pl.BlockSpec((2, N_DPAIR, GRID_W, HP), lambda p: (p, 0, 0, 0))
    return q, k, v, tb


_HEAD_SCRATCH = pltpu.VMEM((2, T, HEAD_DIM), BF16)


_PROBS = pl.BlockSpec((T, 2 * KEYS), lambda p: (0, p))


def _attn_fwd(z, tb):
    def body(q_ref, k_ref, v_ref, tb_ref, o_ref, p_ref, qb_ref, kb_ref, vb_ref):
        _split_heads(q_ref, qb_ref, SCALE)
        _split_heads(k_ref, kb_ref)
        _split_heads(v_ref, vb_ref)

        def rows(it, carry):
            wins, items, _, _, v, p = _attn_items(qb_ref, kb_ref, vb_ref, tb_ref, it * ATT_UNROLL_FWD, ATT_UNROLL_FWD)
            pb = [pi.astype(BF16) for pi in p]
            o = [_dot(pi, vi, NN) for pi, vi in zip(pb, v)]
            for u, (q0, _, _) in enumerate(wins):
                o_ref[pl.ds(q0, GRID_W), :] = jnp.concatenate(o[2 * u:2 * u + 2], axis=1).astype(BF16)
                p_ref[pl.ds(q0, GRID_W), :] = jnp.concatenate(pb[2 * u:2 * u + 2], axis=1)
            return carry

        lax.fori_loop(0, N_ROWS // ATT_UNROLL_FWD, rows, 0)

    blk = pl.BlockSpec((T, HP), lambda p: (0, p))
    return pl.pallas_call(
        body, name="attn_fwd", grid=(N_HP,), in_specs=list(_attn_in_specs()), out_specs=[blk, _PROBS],
        out_shape=[_sds((T, D_ATT), BF16), _sds((T, N_HEADS * KEYS), BF16)], scratch_shapes=[_HEAD_SCRATCH] * 3,
        compiler_params=_params(("parallel",)))(z, z, z, tb)


def _attn_bwd(z, probs, d_att, dz, after=()):
    def body(q_ref, k_ref, v_ref, p_ref, do_ref, flip_ref, dz_in_ref, *rest):
        (dz_ref, diag_ref, qb_ref, kb_ref, vb_ref, dob_ref, dka_ref, dva_ref, ds_ref,
         dq_ref, dk_ref, dv_ref, sems) = rest[len(after):]
        _split_heads(q_ref, qb_ref, SCALE)
        _split_heads(k_ref, kb_ref)
        _split_heads(v_ref, vb_ref)
        _split_heads(do_ref, dob_ref)
        dka_ref[...] = jnp.zeros_like(dka_ref)
        dva_ref[...] = jnp.zeros_like(dva_ref)
        ds_ref[...] = jnp.zeros_like(ds_ref)

        def rows(it, carry):
            wins = [_row_window(it * ATT_UNROLL_BWD + u) for u in range(ATT_UNROLL_BWD)]
            items = [(u, hh) for u in range(ATT_UNROLL_BWD) for hh in range(2)]
            q = [qb_ref[hh, pl.ds(wins[u][0], GRID_W), :] for u, hh in items]
            k = [kb_ref[hh, pl.ds(wins[u][1], KEYS), :] for u, hh in items]
            v = [vb_ref[hh, pl.ds(wins[u][1], KEYS), :] for u, hh in items]
            pb = [p_ref[pl.ds(wins[u][0], GRID_W), hh * KEYS:(hh + 1) * KEYS] for u, hh in items]
            p = [pi.astype(F32) for pi in pb]
            do = [dob_ref[hh, pl.ds(wins[u][0], GRID_W), :] for u, hh in items]
            dv = [_dot(pi, di, TN) for pi, di in zip(pb, do)]
            dp = [_dot(di, vi, NT) for di, vi in zip(do, v)]
            ds = [pi * (dpi - jnp.sum(dpi * pi, axis=-1, keepdims=True)) for pi, dpi in zip(p, dp)]
            dsb = [d.astype(BF16) for d in ds]
            dq = [_dot(d, ki, NN) * SCALE for d, ki in zip(dsb, k)]
            dk = [_dot(d, qi, TN) for d, qi in zip(dsb, q)]
            for d, (u, hh) in zip(ds, items):
                for ii in range(WIN_H // 2):
                    ds_ref[hh, wins[u][2] + 2 * ii] += d[:, ii * HP:(ii + 1) * HP]
            for u, (q0, _, _) in enumerate(wins):
                dq_ref[pl.ds(q0, GRID_W), :] = jnp.concatenate(dq[2 * u:2 * u + 2], axis=1).astype(BF16)
            for dki, dvi, (u, hh) in zip(dk, dv, items):
                dka_ref[hh, pl.ds(wins[u][1], KEYS), :] += dki
                dva_ref[hh, pl.ds(wins[u][1], KEYS), :] += dvi
            return carry

        lax.fori_loop(0, N_ROWS // ATT_UNROLL_BWD, rows, 0)
        dk_ref[...] = jnp.concatenate([dka_ref[0], dka_ref[1]], axis=1).astype(BF16)
        dv_ref[...] = jnp.concatenate([dva_ref[0], dva_ref[1]], axis=1).astype(BF16)
        cols = pl.multiple_of(pl.program_id(0) * HP, HP)
        copies = [pltpu.make_async_copy(src, dz_ref.at[:, pl.ds(base + cols, HP)], sems.at[t])
                  for t, (src, base) in enumerate(((dq_ref, DZ_Q), (dk_ref, DZ_K), (dv_ref, DZ_V)))]
        for cp in copies:
            cp.start()
        _diag_sums(ds_ref, flip_ref, diag_ref)
        for cp in copies:
            cp.wait()

    blk = pl.BlockSpec((T, HP), lambda p: (0, p))
    q, k, v, _ = _attn_in_specs()
    flip =jnp.asarray(np.eye(HP, dtype=np.float32)[::-1], BF16)
    return pl.pallas_call(
        body, name="attn_bwd", grid=(N_HP,),
        in_specs=[q, k, v, _PROBS, blk, pl.BlockSpec((HP, HP), lambda p: (0, 0)), _ANY] + [_ANY] * len(after),
        out_specs=[_ANY, pl.BlockSpec((None, DIAG_ROWS, HP), lambda p: (p, 0, 0))],
        out_shape=[_sds(dz.shape, dz.dtype), _sds((N_HP, DIAG_ROWS, HP), F32)], input_output_aliases={6: 0},
        scratch_shapes=[_HEAD_SCRATCH] * 4 + [pltpu.VMEM((2, T, HEAD_DIM), F32), pltpu.VMEM((2, T, HEAD_DIM), F32),
                                              pltpu.VMEM((2, N_DPAIR, GRID_W, HP), F32)]
        + [pltpu.VMEM((T, HP), BF16)] * 3 + [pltpu.SemaphoreType.DMA((3,))],
        compiler_params=_params(("parallel",)))(z, z, z, probs, d_att, flip, dz, *after)


def _diag_sums(acc_ref, flip_ref, out_ref):
    flip = flip_ref[...]
    rows = []
    for hh in range(2):
        for pair in range(N_DPAIR):
            reversed_lanes = sum(_dot(part, flip, NN) for part in _split3(acc_ref[hh, pair]))
            skewed = pltpu.roll(reversed_lanes, 0, 1, stride=1, stride_axis=0)
            rows.append(jnp.sum(skewed, axis=0, keepdims=True))
    rows.append(jnp.zeros((DIAG_ROWS - len(rows), HP), F32))
    out_ref[...] = jnp.concatenate(rows, axis=0)


def _rpb_grad(diag_sums):
    g = diag_sums.reshape(N_HP * DIAG_ROWS, HP)
    sel = np.zeros((2, 128, N_HP * DIAG_ROWS), np.float32)
    lane = np.zeros((2, HP, 128), np.float32)
    for h in range(N_HEADS):
        for pair in range(N_DPAIR):
            for half in range(2):
                sel[half, h * N_DROW + pair + half, (h // 2) * DIAG_ROWS + (h % 2) * N_DPAIR + pair] = 1.0
    for j in range(2 * WIN_W - 1):
        for half in range(2):
            lane[half, (HP - 1 - GRID_W * half - (j - (WIN_W - 1))) % HP, j] = 1.0

    def body(g_ref, sel_ref, lane_ref, o_ref):
        parts = _split3(g_ref[...])
        total = None
        for half in range(2):
            picked = sum(_dot(sel_ref[half], part, NN) for part in parts)
            term = sum(_dot(part, lane_ref[half], NN) for part in _split3(picked))
            total = term if total is None else total + term
        o_ref[...] = total

    out = pl.pallas_call(body, name="rpb_grad", out_shape=_sds((128, 128), F32),
                         compiler_params=_params())(g, jnp.asarray(sel, BF16), jnp.asarray(lane, BF16))
    return out[:N_HEADS * N_DROW, :2 * WIN_W - 1].reshape(N_HEADS, N_DROW, 2 * WIN_W - 1)


N_CB = D // REC_CB
N_CHUNK = T // REC_CHUNK
DUP_PIECES = 4
N_TILE = T // 8
_U_BLK = 1536 // REC_CB
_Y_BLK = 2560 // REC_CB


def _block_diag(w):
    per = REC_CB // 64
    wt = w.reshape(2, N_CB, per, 64, 64)
    eye = jnp.eye(per, dtype=w.dtype)
    full = wt[:, :, :, :, None, :] * eye[None, None, :, None, :, None]
    return full.reshape(2, N_CB, REC_CB, REC_CB).astype(BF16)


def _gelu(x):
    c = 0.7978845608028654
    return 0.5 * x * (1.0 + jnp.tanh(c * (x + 0.044715 * x * x * x)))


def _gelu_grad(x):
    c = 0.7978845608028654
    th = jnp.tanh(c * (x + 0.044715 * x * x * x))
    return 0.5 * (1.0 + th) + 0.5 * x * (1.0 - th * th) * c * (1.0 + 3.0 * 0.044715 * x * x)


def _softplus_neg(lam):
    x = -lam
    e = jnp.exp(-jnp.abs(x))
    w = 1.0 + e
    l1p = jnp.where(w == 1.0, e, jnp.log(w) * e / (w - 1.0))
    return jnp.maximum(x, 0.0) + l1p


def _one_minus_exp(x, exp_x):
    poly = x * (1.0 + x * (1 / 2 + x * (1 / 6 + x * (1 / 24 + x * (1 / 120 + x * (1 / 720))))))
    return jnp.where(x > -0.125, -poly, 1.0 - exp_x)


def _conv_taps(pad_ref, t0, w, sign):
    out = None
    for j in range(4):
        term = w[j:j + 1, :] * pad_ref[pl.ds(PAD + t0 + sign * (j - 2), REC_CHUNK), :]
        out = term if out is None else out + term
    return out


def _gates(u, wa, wi, ba, bi, sp):
    ub = u.astype(BF16)
    r = _sigmoid(_dot(ub, wa, NN) + ba)
    i = _sigmoid(_dot(ub, wi, NN) + bi)
    log_a = (-LRU_C * sp) * r
    a = jnp.exp(log_a)
    x = jnp.maximum(_one_minus_exp(2.0 * log_a, a * a), 0.0)
    positive = x > 0.0
    inv = lax.rsqrt(jnp.where(positive, x, 1.0))
    mult = jnp.where(positive, x * inv, 0.0)
    return r, i, a, mult, jnp.where(positive, inv, 0.0)


def _tile_scan(a, b, sub, reverse):
    for s in (1, 2, 4):
        if reverse:
            a_s, b_s, m = pltpu.roll(a, 8 - s, 0), pltpu.roll(b, 8 - s, 0), sub < 8 - s
        else:
            a_s, b_s, m = pltpu.roll(a, s, 0), pltpu.roll(b, s, 0), sub >= s
        b = jnp.where(m, a * b_s + b, b)
        a = jnp.where(m, a * a_s, a)
    return a, b


def _last_row(x, row):
    return jnp.broadcast_to(x[row:row + 1, :], x.shape)


def _rec_prologue(up_ref, cw_ref, cb_ref, wa_ref, wi_ref, ba_ref, bi_ref, lam_ref,
                  upad_ref, u_ref, a_refs, h_refs):
    cb = up_ref.shape[1]
    zeros = jnp.zeros((PAD, cb), F32)
    upad_ref[pl.ds(0, PAD), :] = zeros
    upad_ref[pl.ds(PAD + T, PAD), :] = zeros
    upad_ref[pl.ds(PAD, T), :] = up_ref[...].astype(F32)
    cw = cw_ref[...]
    sp = _softplus_neg(lam_ref[...])
    for c in range(N_CHUNK):
        t0 = c * REC_CHUNK
        u = cb_ref[...] + _conv_taps(upad_ref, t0, cw, 1)
        u_ref[pl.ds(t0, REC_CHUNK), :] = u
        for d in range(2):
            _, i, a, mult, _ = _gates(u, wa_ref[d], wi_ref[d], ba_ref[d:d + 1, :], bi_ref[d:d + 1, :], sp[d:d + 1, :])
            a_refs[d][pl.ds(t0, REC_CHUNK), :] = a
            h_refs[d][pl.ds(t0, REC_CHUNK), :] = mult * (i * u)

    sub = lax.broadcasted_iota(jnp.int32, (8, cb), 0)

    def tile(k, carry):
        cf, cr = carry
        tf = pl.multiple_of(k * 8, 8)
        tr = pl.multiple_of((N_TILE - 1 - k) * 8, 8)
        af, bf = _tile_scan(a_refs[0][pl.ds(tf, 8), :], h_refs[0][pl.ds(tf, 8), :], sub, False)
        hf = af * cf + bf
        h_refs[0][pl.ds(tf, 8), :] = hf
        ar, br = _tile_scan(a_refs[1][pl.ds(tr, 8), :], h_refs[1][pl.ds(tr, 8), :], sub, True)
        hr = ar * cr + br
        h_refs[1][pl.ds(tr, 8), :] = hr
        return _last_row(af, 7) * cf + _last_row(bf, 7), _last_row(ar, 0) * cr + _last_row(br, 0)

    z8 = jnp.zeros((8, cb), F32)
    lax.fori_loop(0, N_TILE, tile, (z8, z8), unroll=2)
    return sp


def _rec_specs():
    up = pl.BlockSpec((T, REC_CB), lambda c: (0, _U_BLK + c))
    yb = pl.BlockSpec((T, REC_CB), lambda c: (0, _Y_BLK + c))
    cw = pl.BlockSpec((4, REC_CB), lambda c: (0, c))
    cbias = pl.BlockSpec((1, REC_CB), lambda c: (0, c))
    wbd = pl.BlockSpec((2, None, REC_CB, REC_CB), lambda c: (0, c, 0, 0))
    vec2 = pl.BlockSpec((2, REC_CB), lambda c: (0, c))
    col = pl.BlockSpec((T, REC_CB), lambda c: (0, c))
    return up, yb, cw, cbias, wbd, vec2, col


def _rec_fwd(z, conv_w, conv_b, wa, wi, ba, bi, lam):
    up, yb, cw, cbias, wbd, vec2, col = _rec_specs()

    def body(up_ref, yb_ref, cw_ref, cb_ref, wa_ref, wi_ref, ba_ref, bi_ref, lam_ref, g_ref,
             u_ref, af_ref, ar_ref, hf_ref, hr_ref, upad_ref):
        _rec_prologue(up_ref, cw_ref, cb_ref, wa_ref, wi_ref, ba_ref, bi_ref, lam_ref,
                      upad_ref, u_ref, (af_ref, ar_ref), (hf_ref, hr_ref))

        def chunk(c, carry):
            t0 = pl.multiple_of(c * REC_CHUNK, REC_CHUNK)
            rows = pl.ds(t0, REC_CHUNK)
            g_ref[rows, :] = ((hf_ref[rows, :] + hr_ref[rows, :]) * _gelu(yb_ref[rows, :].astype(F32))).astype(BF16)
            return carry

        lax.fori_loop(0, N_CHUNK, chunk, 0)

    res = pl.pallas_call(
        body, name="rec_fwd", grid=(N_CB,),
        in_specs=[up, yb, cw, cbias, wbd, wbd, vec2, vec2, vec2], out_specs=[col] * 6,
        out_shape=[_sds((T, D), BF16)] + [_sds((T, D), F32)] * 5,
        scratch_shapes=[pltpu.VMEM((T + 2 * PAD, REC_CB), F32)],
        compiler_params=_params(("parallel",)))(z, z, conv_w, conv_b, wa, wi, ba, bi, lam)
    return res[0], tuple(res[1:])


def _rec_bwd(z, dg, saved, dz, conv_w, conv_b, wa, wi, ba, bi, lam, after=()):
    up, yb, cw, cbias, wbd, vec2, col = _rec_specs()

    def body(up_ref, yb_ref, dg_ref, u_ref, af_ref, ar_ref, hf_ref, hr_ref,
             cw_ref, cb_ref, wa_ref, wi_ref, ba_ref, bi_ref, lam_ref, dz_in_ref, *rest):
        (dz_ref, dcw_ref, dcb_ref, dwa_out, dwi_out, dba_ref, dbi_ref, dlam_ref,
         upad_ref, dh_ref, gf_ref, gr_ref, daf_ref, dar_ref, dupad_ref, dwa_ref, dwi_ref,
         dup_ref, dyb_ref, sems) = rest[len(after):]
        g_refs, da_refs = (gf_ref, gr_ref), (daf_ref, dar_ref)
        cb = up_ref.shape[1]
        zeros = jnp.zeros((PAD, cb), F32)
        upad_ref[pl.ds(0, PAD), :] = zeros
        upad_ref[pl.ds(PAD + T, PAD), :] = zeros
        upad_ref[pl.ds(PAD, T), :] = up_ref[...].astype(F32)
        sp = _softplus_neg(lam_ref[...])

        def gate_chunk(c, carry):
            t0 = pl.multiple_of(c * REC_CHUNK, REC_CHUNK)
            rows = pl.ds(t0, REC_CHUNK)
            y = yb_ref[rows, :].astype(F32)
            dgv = dg_ref[rows, :].astype(F32)
            dh_ref[rows, :] = dgv * _gelu(y)
            dyb_ref[rows, :] = (dgv * (hf_ref[rows, :] + hr_ref[rows, :]) * _gelu_grad(y)).astype(BF16)
            return carry

        lax.fori_loop(0, N_CHUNK, gate_chunk, 0)
        cols = pl.multiple_of(pl.program_id(0) * REC_CB, REC_CB)
        dyb_copy = pltpu.make_async_copy(dyb_ref, dz_ref.at[:, pl.ds(DZ_Y + cols, REC_CB)], sems.at[1])
        dyb_copy.start()

        sub = lax.broadcasted_iota(jnp.int32, (8, cb), 0)

        def tile(k, carry):
            cf, cr = carry
            kf = N_TILE - 1 - k
            tf = pl.multiple_of(kf * 8, 8)
            tnext = pl.multiple_of(jnp.minimum(kf + 1, N_TILE - 1) * 8, 8)
            tprev = pl.multiple_of(jnp.maximum(kf - 1, 0) * 8, 8)
            a_t = af_ref[pl.ds(tf, 8), :]
            a_n = jnp.where(kf < N_TILE - 1, af_ref[pl.ds(tnext, 8), :], 0.0)
            a_sh = jnp.where(sub == 7, pltpu.roll(a_n, 7, 0), pltpu.roll(a_t, 7, 0))
            ca, cbb = _tile_scan(a_sh, dh_ref[pl.ds(tf, 8), :], sub, True)
            gf = ca * cf + cbb
            h_t = hf_ref[pl.ds(tf, 8), :]
            h_p = jnp.where(kf > 0, hf_ref[pl.ds(tprev, 8), :], 0.0)
            h_sh = jnp.where(sub == 0, pltpu.roll(h_p, 1, 0), pltpu.roll(h_t, 1, 0))
            gf_ref[pl.ds(tf, 8), :] = gf
            daf_ref[pl.ds(tf, 8), :] = gf * h_sh
            tr = pl.multiple_of(k * 8, 8)
            rnext = pl.multiple_of(jnp.minimum(k + 1, N_TILE - 1) * 8, 8)
            rprev = pl.multiple_of(jnp.maximum(k - 1, 0) * 8, 8)
            b_t = ar_ref[pl.ds(tr, 8), :]
            b_p = jnp.where(k > 0, ar_ref[pl.ds(rprev, 8), :], 0.0)
            b_sh = jnp.where(sub == 0, pltpu.roll(b_p, 1, 0), pltpu.roll(b_t, 1, 0))
            ra, rb = _tile_scan(b_sh, dh_ref[pl.ds(tr, 8), :], sub, False)
            gr = ra * cr + rb
            hr_t = hr_ref[pl.ds(tr, 8), :]
            hr_n = jnp.where(k < N_TILE - 1, hr_ref[pl.ds(rnext, 8), :], 0.0)
            hr_sh = jnp.where(sub == 7, pltpu.roll(hr_n, 7, 0), pltpu.roll(hr_t, 7, 0))
            gr_ref[pl.ds(tr, 8), :] = gr
            dar_ref[pl.ds(tr, 8), :] = gr * hr_sh
            return _last_row(gf, 0), _last_row(gr, 7)

        z8 = jnp.zeros((8, cb), F32)
        lax.fori_loop(0, N_TILE, tile, (z8, z8), unroll=2)

        dupad_ref[pl.ds(0, PAD), :] = zeros
        dupad_ref[pl.ds(PAD + T, PAD), :] = zeros
        dwa_ref[...] = jnp.zeros_like(dwa_ref)
        dwi_ref[...] = jnp.zeros_like(dwi_ref)
        dba_ref[...] = jnp.zeros_like(dba_ref)
        dbi_ref[...] = jnp.zeros_like(dbi_ref)
        dlam_ref[...] = jnp.zeros_like(dlam_ref)

        def grad_chunk(c, carry):
            t0 = pl.multiple_of(c * REC_CHUNK, REC_CHUNK)
            rows = pl.ds(t0, REC_CHUNK)
            u = u_ref[rows, :]
            ub = u.astype(BF16)
            du = jnp.zeros((REC_CHUNK, cb), F32)
            for d in range(2):
                r, i, a, mult, inv_mult = _gates(u, wa_ref[d], wi_ref[d], ba_ref[d:d + 1, :], bi_ref[d:d + 1, :],
                                                 sp[d:d + 1, :])
                dbx = g_refs[d][rows, :]
                dmult = dbx * (i * u)
                diu = dbx * mult
                a2 = a * a
                dlog = da_refs[d][rows, :] * a - dmult * (a2 * inv_mult)
                dpa = (dlog * (-LRU_C) * sp[d:d + 1, :]) * r * (1.0 - r)
                dpi = (diu * u) * i * (1.0 - i)
                dpab, dpib = dpa.astype(BF16), dpi.astype(BF16)
                du = du + diu * i + _dot(dpab, wa_ref[d], NT) + _dot(dpib, wi_ref[d], NT)
                dwa_ref[d] += _dot(ub, dpab, TN)
                dwi_ref[d] += _dot(ub, dpib, TN)
                dba_ref[d:d + 1, :] += jnp.sum(dpa, axis=0, keepdims=True)
                dbi_ref[d:d + 1, :] += jnp.sum(dpi, axis=0, keepdims=True)
                dlam_ref[d:d + 1, :] += jnp.sum(dlog * r, axis=0, keepdims=True)
            dupad_ref[pl.ds(PAD + t0, REC_CHUNK), :] = du
            return carry

        lax.fori_loop(0, N_CHUNK, grad_chunk, 0)
        cw = cw_ref[...]
        dcb = jnp.zeros((1, cb), F32)
        dcw = [jnp.zeros((1, cb), F32) for _ in range(4)]
        dup_copies = []
        per_piece = N_CHUNK // DUP_PIECES
        for c in range(N_CHUNK):
            t0 = c * REC_CHUNK
            du = dupad_ref[pl.ds(PAD + t0, REC_CHUNK), :]
            dcb = dcb + jnp.sum(du, axis=0, keepdims=True)
            for j in range(4):
                dcw[j] = dcw[j] + jnp.sum(du * upad_ref[pl.ds(PAD + t0 + j - 2, REC_CHUNK), :], axis=0, keepdims=True)
            dup_ref[pl.ds(t0, REC_CHUNK), :] = _conv_taps(dupad_ref, t0, cw, -1).astype(BF16)
            if (c + 1) % per_piece == 0:
                piece = c // per_piece
                rows = pl.ds(piece * per_piece * REC_CHUNK, per_piece * REC_CHUNK)
                dup_copies.append(pltpu.make_async_copy(
                    dup_ref.at[rows, :], dz_ref.at[rows, pl.ds(DZ_U + cols, REC_CB)], sems.at[2 + piece]))
                dup_copies[-1].start()
        dcb_ref[...] = dcb
        dcw_ref[...] = jnp.concatenate(dcw, axis=0)
        dlam_ref[...] = dlam_ref[...] * (LRU_C * _sigmoid(-lam_ref[...]))
        for d in range(2):
            for blk in range(REC_CB // REC_BLOCK):
                lo, hi = blk * REC_BLOCK, (blk + 1) * REC_BLOCK
                dwa_out[d, blk] = dwa_ref[d, lo:hi, lo:hi]
                dwi_out[d, blk] = dwi_ref[d, lo:hi, lo:hi]
        for cp in dup_copies:
            cp.wait()
        dyb_copy.wait()

    full = pltpu.VMEM((T, REC_CB), F32)
    padded = pltpu.VMEM((T + 2 * PAD, REC_CB), F32)
    per = REC_CB // REC_BLOCK
    diag = pl.BlockSpec((2, per, REC_BLOCK, REC_BLOCK), lambda c: (0, c, 0, 0))
    return pl.pallas_call(
        body, name="rec_bwd", grid=(N_CB,),
        in_specs=[up, yb] + [col] * 6 + [cw, cbias, wbd, wbd, vec2, vec2, vec2, _ANY] + [_ANY] * len(after),
        out_specs=[_ANY, cw, cbias, diag, diag, vec2, vec2, vec2],
        out_shape=[_sds(dz.shape, dz.dtype), _sds((4, D), F32), _sds((1, D), F32),
                   _sds((2, D // REC_BLOCK, REC_BLOCK, REC_BLOCK), F32), _sds((2, D // REC_BLOCK, REC_BLOCK, REC_BLOCK), F32),
                   _sds((2, D), F32), _sds((2, D), F32), _sds((2, D), F32)],
        input_output_aliases={15: 0},
        scratch_shapes=[padded, full, full, full, full, full, padded,
                        pltpu.VMEM((2, REC_CB, REC_CB), F32), pltpu.VMEM((2, REC_CB, REC_CB), F32),
                        pltpu.VMEM((T, REC_CB), BF16), pltpu.VMEM((T, REC_CB), BF16),
                        pltpu.SemaphoreType.DMA((2 + DUP_PIECES,))],
        compiler_params=_params(("parallel",)))(z, z, dg, *saved, conv_w, conv_b, wa, wi, ba, bi, lam, dz, *after)


class _NoReducer:
    def begin(self, tag, grads):
        return ()

    def advance(self, tag, after):
        return ()

    def interlude(self, tokens):
        return None


def _local_step(x, target, p, late=None, reducer=_NoReducer()):
    x = x.reshape(T, D)
    target = target.reshape(T, D)
    tb = _bias_pairs(p["rpb"])
    wa, wi = _block_diag(p["w_rg_a"]), _block_diag(p["w_rg_i"])

    h1 = _rms_fwd("rms1_fwd", x, p["ln1_g"], after=late[0] if late else ())
    if late:
        p = {**p, **late[1]((h1, tb, wa, wi))}
    rec_params = (p["conv_w"], p["conv_b"], wa, wi, p["b_rg_a"], p["b_rg_i"], p["lru_lambda"])
    (z,) = _mm_nn_cols("mm_z", h1, p["w_in"], BF16, bias=p["b_in"])
    att, probs = _attn_fwd(z, tb)
    g, rec_saved = _rec_fwd(z, *rec_params)
    if late:
        p = {**p, **late[2](g)}
    y_att, y_rec, mixed, x1, h2 = _branches_fwd(att, g, z, x, p["w_att_o"], p["w_rec_o"], p["w_out"], p["ln2_g"])

    def relu2(r, ex, outs):
        rp = jnp.maximum(r, 0.0)
        outs[0][...] = (rp * rp).astype(BF16)

    (s,) = _mm_nn_cols("mm_ff1", h2, p["w_ff1"], BF16, epilogue=relu2)
    loss, dx2, dx2_b, g_lnf = _mm_x2_loss_head(s, p["w_ff2"], x1, target, p["lnf_g"])

    def relu2_bwd(r, ex, outs):
        outs[0][...] = (r * 2.0 * jnp.sqrt(ex[0][...].astype(F32))).astype(BF16)

    (df,) = _mm_nt_rows("mm_df", dx2_b, p["w_ff2"], BF16, tn=D, extras=[s],
                        extra_specs=[pl.BlockSpec((TJ, D), lambda j, i, k: (i, j))], epilogue=relu2_bwd)
    (g_w_ff2,) = _mm_tn_rows("mm_g_ff2", s, dx2_b, tm=D)
    (g_w_ff1,) = _mm_tn_cols("mm_g_ff1", h2, df, D)
    tok = reducer.begin("ff", dict(w_ff2=g_w_ff2, w_ff1=g_w_ff1))
    dx1, dx1_b, g_ln2 = _mm_nt_cols_rms_bwd("mm_dh2_rms2_bwd", df, p["w_ff1"], x1, p["ln2_g"], dx2, after=tok,
                                            bf16_copy=True)

    dy_att, dy_rec, d_att, d_g, dz = _branches_bwd(dx1_b, y_att, y_rec, z, p["w_att_o"], p["w_rec_o"], p["w_out"])
    (g_w_out,) = _mm_tn_rows("mm_g_out", mixed, dx1_b, tm=D)
    (g_w_att_o,) = _mm_tn_cols("mm_g_att_o", att, dy_att, D // N_CHIPS)
    (g_w_rec_o,) = _mm_tn_rows("mm_g_rec_o", g, dy_rec, tm=D)
    tok = reducer.advance("ff", g_w_rec_o) + reducer.begin("proj", dict(w_out=g_w_out, w_att_o=g_w_att_o, w_rec_o=g_w_rec_o))

    dz, ds_acc = _attn_bwd(z, probs, d_att, dz, after=tok)
    g_rpb = _rpb_grad(ds_acc)
    tok = reducer.advance("proj", ds_acc)
    dz, g_conv_w, g_conv_b, g_wa, g_wi, g_ba, g_bi, g_lam = _rec_bwd(z, d_g, rec_saved, dz, *rec_params, after=tok)

    g_w_in, g_b_in = _mm_tn_cols("mm_g_in", h1, dz, D_IN // N_CHIPS, colsum=True)
    tok = reducer.advance("in", reducer.interlude(reducer.begin("in", dict(w_in=g_w_in))))
    grad_x, g_ln1 = _mm_nt_cols_rms_bwd("mm_dh1_rms1_bwd", dz, p["w_in"], x, p["ln1_g"], dx1, after=tok)

    grads = dict(ln1_g=g_ln1, w_in=g_w_in, b_in=g_b_in, rpb=g_rpb, w_att_o=g_w_att_o, conv_w=g_conv_w,
                 conv_b=g_conv_b, w_rg_a=g_wa, b_rg_a=g_ba, w_rg_i=g_wi,
                 b_rg_i=g_bi, lru_lambda=g_lam, w_rec_o=g_w_rec_o, w_out=g_w_out, ln2_g=g_ln2,
                 w_ff1=g_w_ff1, w_ff2=g_w_ff2, lnf_g=g_lnf)
    return loss, grad_x.reshape(1, T, D), grads


_ANY = pl.BlockSpec(memory_space=pl.ANY)
N_PEERS = N_CHIPS - 1


def _place():
    x, y, c = lax.axis_index("x"), lax.axis_index("y"), lax.axis_index("c")
    peers = [(1 - x, y), (x, 1 - y), (1 - x, 1 - y)]
    return x, y, c, 2 * x + y, peers


def _remote(src, dst, send_sem, recv_sem, dev):
    return pltpu.make_async_remote_copy(src_ref=src, dst_ref=dst, send_sem=send_sem, recv_sem=recv_sem,
                                        device_id=dev, device_id_type=MESH)


def _prefetch_call(body, name, ids, grid, in_specs, out_specs, out_shape, args, semantics=None):
    spec = pltpu.PrefetchScalarGridSpec(num_scalar_prefetch=1, grid=grid, in_specs=in_specs, out_specs=out_specs)
    return pl.pallas_call(body, name=name, grid_spec=spec, out_shape=out_shape,
                          compiler_params=_params(semantics or ("parallel",) * len(grid)))(ids, *args)


def _cast_bf16(name, w, chip_id, after=()):
    rows, cols = w.shape
    rb = min(rows, 256)

    def body(ids_ref, w_ref, *rest):
        rest[-1][...] = w_ref[...].astype(BF16)

    return _prefetch_call(body, name, chip_id, (rows // rb,),
                          [pl.BlockSpec((rb, cols), lambda i, ids: (i, 0))] + [_ANY] * len(after),
                          pl.BlockSpec((None, rb, cols), lambda i, ids: (ids[0], i, 0)),
                          _sds((N_CHIPS, rows, cols), BF16), (w, *after))


def _dma_sems(*counts):
    return [pltpu.SemaphoreType.DMA((k,)) for k in counts]


_HBM = pl.BlockSpec(memory_space=pltpu.HBM)
_SEM = pl.BlockSpec(memory_space=pltpu.SEMAPHORE)
_SPLIT_COPY = pltpu.CompilerParams(has_side_effects=pltpu.SideEffectType.DATAFLOW_SIDE_EFFECTING)
SIBLING_ID = 0
_SPLIT_COPY_SIBLING = pltpu.CompilerParams(has_side_effects=pltpu.SideEffectType.DATAFLOW_SIDE_EFFECTING,
                                           collective_id=SIBLING_ID)


def _sibling_handshake():
    x, y, c = lax.axis_index("x"), lax.axis_index("y"), lax.axis_index("c")
    barrier = pltpu.get_barrier_semaphore()
    pl.semaphore_signal(barrier, inc=1, device_id=(x, y, 1 - c), device_id_type=MESH)
    pl.semaphore_wait(barrier, 1)


def _hbm(arrays):
    return [pltpu.with_memory_space_constraint(a, pltpu.HBM) for a in arrays]


def _hbm_like(arrays):
    return [pltpu.HBM(a.shape, a.dtype) for a in arrays]


def _halves(buf, c):
    half = buf.shape[1] // 2
    return pl.ds(c * half, half), pl.ds((1 - c) * half, half)


def _gather_start(name, slots):
    n = len(slots)
    nk = n * N_PEERS

    def body(*refs):
        bufs = refs[n:2 * n]
        send_sems, recv_sems, token = refs[2 * n:]
        x, y, c, chip, peers = _place()
        for t in range(n):
            mine, _ = _halves(bufs[t], c)
            for r, (px, py) in enumerate(peers):
                k = t * N_PEERS + r
                own = bufs[t].at[chip, mine]
                _remote(own, own, send_sems.at[k], recv_sems.at[k], (px, py, c)).start()
        token[...] = jnp.zeros_like(token)

    res = pl.pallas_call(
        body, name=name, in_specs=[_HBM] * n, out_specs=[_HBM] * n + [_SEM, _SEM, pl.BlockSpec(memory_space=pltpu.VMEM)],
        out_shape=_hbm_like(slots) + [pltpu.SemaphoreType.DMA((nk,)), pltpu.SemaphoreType.DMA((nk,)),
                                      _sds((8, 128), F32)],
        input_output_aliases={t: t for t in range(n)}, compiler_params=_SPLIT_COPY)(*_hbm(slots))
    return res[:n], (res[n], res[n + 1]), res[n + 2]


def _gather_wait(name, bufs, sems, after):
    n = len(bufs)
    after = tuple(after) if isinstance(after, (tuple, list)) else (after,)

    def body(*refs):
        ins = refs[:n]
        send_sems, recv_sems = refs[n], refs[n + 1]
        x, y, c, chip, peers = _place()
        for t in range(n):
            mine, _ = _halves(ins[t], c)
            for r, (px, py) in enumerate(peers):
                k = t * N_PEERS + r
                cp = _remote(ins[t].at[chip, mine], ins[t].at[2 * px + py, mine], send_sems.at[k], recv_sems.at[k],
                             (px, py, c))
                cp.wait_send()
                cp.wait_recv()

    return pl.pallas_call(
        body, name=name, in_specs=[_HBM] * n + [_SEM, _SEM] + [_ANY] * len(after), out_specs=[_HBM] * n,
        out_shape=_hbm_like(bufs), input_output_aliases={t: t for t in range(n)},
        compiler_params=_SPLIT_COPY)(*bufs, *sems, *after)


def _gather_forward(name, bufs):
    n = len(bufs)
    nk = n * N_PEERS

    def body(*refs):
        _sibling_handshake()
        outs = refs[n:2 * n]
        send_sems, recv_sems = refs[2 * n:]
        x, y, c, chip, peers = _place()
        sibling = (x, y, 1 - c)
        sends = []
        for t in range(n):
            mine, _ = _halves(outs[t], c)
            for r, (px, py) in enumerate(peers):
                k = t * N_PEERS + r
                landed = outs[t].at[2 * px + py, mine]
                sends.append(_remote(landed, landed, send_sems.at[k], recv_sems.at[k], sibling))
                sends[-1].start()
        for t in range(n):
            _, theirs = _halves(outs[t], c)
            for r, (px, py) in enumerate(peers):
                k = t * N_PEERS + r
                landed = outs[t].at[2 * px + py, theirs]
                _remote(landed, landed, send_sems.at[k], recv_sems.at[k], sibling).wait_recv()
        for cp in sends:
            cp.wait_send()

    return pl.pallas_call(
        body, name=name, in_specs=[_ANY] * n, out_specs=[_ANY] * n, out_shape=[_sds(b.shape, b.dtype) for b in bufs],
        input_output_aliases={t: t for t in range(n)}, scratch_shapes=_dma_sems(nk, nk),
        compiler_params=pltpu.CompilerParams(collective_id=SIBLING_ID))(*bufs)


def _pair_copies(n, srcs, lands, send_sems, recv_sems):
    x, y, c, _, _ = _place()
    sibling = (x, y, 1 - c)
    copies = []
    for t in range(n):
        half = srcs[t].shape[1] // 2
        for j in range(N_CHIPS):
            k = t * N_CHIPS + j
            copies.append(_remote(srcs[t].at[j, pl.ds((1 - c) * half, half)], lands[t].at[j],
                                  send_sems.at[k], recv_sems.at[k], sibling))
    for t in range(n, len(srcs)):
        k = n * N_CHIPS + t - n
        copies.append(_remote(srcs[t], lands[t], send_sems.at[k], recv_sems.at[k], sibling))
    return copies


def _pair_start(name, grads, wholes=()):
    n = len(grads)
    srcs = list(grads) + list(wholes)
    m = len(srcs)
    lands = [pltpu.HBM((N_CHIPS, g.shape[1] // 2, g.shape[2]), F32) for g in grads] + _hbm_like(wholes)
    ns = n * N_CHIPS + len(wholes)

    def body(*refs):
        _sibling_handshake()
        src_refs, land_refs = refs[m:2 * m], refs[2 * m:3 * m]
        send_sems, recv_sems, token = refs[3 * m:]
        for cp in _pair_copies(n, src_refs, land_refs, send_sems, recv_sems):
            cp.start()
        token[...] = jnp.zeros_like(token)

    res = pl.pallas_call(
        body, name=name, in_specs=[_HBM] * m,
        out_specs=[_HBM] * (2 * m) + [_SEM, _SEM, pl.BlockSpec(memory_space=pltpu.VMEM)],
        out_shape=_hbm_like(srcs) + lands + [pltpu.SemaphoreType.DMA((ns,)), pltpu.SemaphoreType.DMA((ns,)),
                                             _sds((8, 128), F32)],
        input_output_aliases={t: t for t in range(m)}, compiler_params=_SPLIT_COPY_SIBLING)(*_hbm(srcs))
    return (res[:m], res[m:2 * m], (res[2 * m], res[2 * m + 1])), res[2 * m + 2]


def _pair_wait(name, flight, n, after):
    srcs, lands, sems = flight
    m = len(srcs)

    def body(*refs):
        for cp in _pair_copies(n, refs[:m], refs[m:2 * m], refs[2 * m], refs[2 * m + 1]):
            cp.wait_send()
            cp.wait_recv()

    res = pl.pallas_call(
        body, name=name, in_specs=[_HBM] * (2 * m) + [_SEM, _SEM, _ANY], out_specs=[_HBM] * (2 * m),
        out_shape=_hbm_like(srcs) + _hbm_like(lands), input_output_aliases={t: t for t in range(2 * m)},
        compiler_params=_SPLIT_COPY)(*srcs, *lands, *sems, after)
    return res[:m], res[m:]


def _chip_copies(srcs, lands, small_src, small_land, send_sems, recv_sems):
    x, y, c, chip, peers = _place()
    n = len(srcs)
    copies = []
    for r, (px, py) in enumerate(peers):
        for t in range(n):
            k = t * N_PEERS + r
            copies.append(_remote(srcs[t].at[2 * px + py], lands[t].at[r], send_sems.at[k], recv_sems.at[k], (px, py, c)))
        if small_src is not None:
            k = n * N_PEERS + r
            half_s = small_src.shape[0] // 2
            copies.append(_remote(small_src.at[pl.ds(c * half_s, half_s)], small_land.at[r],
                                  send_sems.at[k], recv_sems.at[k], (px, py, c)))
    return copies


def _chip_start(name, sums_bf16, small=None):
    n = len(sums_bf16)
    srcs = list(sums_bf16) + ([small] if small is not None else [])
    m = len(srcs)
    lands = [pltpu.HBM((N_PEERS,) + s.shape[1:], BF16) for s in sums_bf16]
    if small is not None:
        lands.append(pltpu.HBM((N_PEERS, small.shape[0] // 2, 128), F32))
    nk = m * N_PEERS

    def body(*refs):
        src_refs, land_refs = refs[m:2 * m], refs[2 * m:3 * m]
        send_sems, recv_sems, token = refs[3 * m:]
        small_src, small_land = (src_refs[n], land_refs[n]) if small is not None else (None, None)
        for cp in _chip_copies(src_refs[:n], land_refs[:n], small_src, small_land, send_sems, recv_sems):
            cp.start()
        token[...] = jnp.zeros_like(token)

    res = pl.pallas_call(
        body, name=name, in_specs=[_HBM] * m,
        out_specs=[_HBM] * (2 * m) + [_SEM, _SEM, pl.BlockSpec(memory_space=pltpu.VMEM)],
        out_shape=_hbm_like(srcs) + lands + [pltpu.SemaphoreType.DMA((nk,)), pltpu.SemaphoreType.DMA((nk,)),
                                             _sds((8, 128), F32)],
        input_output_aliases={t: t for t in range(m)}, compiler_params=_SPLIT_COPY)(*_hbm(srcs))
    return (res[:m], res[m:2 * m], (res[2 * m], res[2 * m + 1])), res[2 * m + 2]


def _chip_wait(name, flight, with_small, after):
    srcs, lands, sems = flight
    m = len(srcs)
    n = m - 1 if with_small else m

    def body(*refs):
        src_refs, land_refs = refs[:m], refs[m:2 * m]
        send_sems, recv_sems = refs[2 * m], refs[2 * m + 1]
        small_src, small_land = (src_refs[n], land_refs[n]) if with_small else (None, None)
        for cp in _chip_copies(src_refs[:n], land_refs[:n], small_src, small_land, send_sems, recv_sems):
            cp.wait_send()
            cp.wait_recv()

    res = pl.pallas_call(
        body, name=name, in_specs=[_HBM] * (2 * m) + [_SEM, _SEM, _ANY], out_specs=[_HBM] * (2 * m),
        out_shape=_hbm_like(srcs) + _hbm_like(lands), input_output_aliases={t: t for t in range(2 * m)},
        compiler_params=_SPLIT_COPY)(*srcs, *lands, *sems, after)
    return res[:m], res[m:]


def _swap_start(name, bufs):
    n = len(bufs)

    def body(*refs):
        _sibling_handshake()
        outs = refs[n:2 * n]
        send_sems, recv_sems, token = refs[2 * n:]
        x, y, c, _, _ = _place()
        for t in range(n):
            h = outs[t].shape[0] // 2
            mine = outs[t].at[pl.ds(c * h, h)]
            _remote(mine, mine, send_sems.at[t], recv_sems.at[t], (x, y, 1 - c)).start()
        token[...] = jnp.zeros_like(token)

    res = pl.pallas_call(
        body, name=name, in_specs=[_HBM] * n, out_specs=[_HBM] * n + [_SEM, _SEM, pl.BlockSpec(memory_space=pltpu.VMEM)],
        out_shape=_hbm_like(bufs) + [pltpu.SemaphoreType.DMA((n,)), pltpu.SemaphoreType.DMA((n,)), _sds((8, 128), F32)],
        input_output_aliases={t: t for t in range(n)}, compiler_params=_SPLIT_COPY_SIBLING)(*_hbm(bufs))
    return (res[:n], (res[n], res[n + 1])), res[n + 2]


def _swap_wait(name, flight, after):
    bufs, sems = flight
    n = len(bufs)

    def body(*refs):
        ins = refs[:n]
        send_sems, recv_sems = refs[n], refs[n + 1]
        x, y, c, _, _ = _place()
        for t in range(n):
            h = ins[t].shape[0] // 2
            cp = _remote(ins[t].at[pl.ds(c * h, h)], ins[t].at[pl.ds((1 - c) * h, h)], send_sems.at[t],
                         recv_sems.at[t], (x, y, 1 - c))
            cp.wait_send()
            cp.wait_recv()

    return pl.pallas_call(
        body, name=name, in_specs=[_HBM] * n + [_SEM, _SEM, _ANY], out_specs=[_HBM] * n, out_shape=_hbm_like(bufs),
        input_output_aliases={t: t for t in range(n)}, compiler_params=_SPLIT_COPY)(*bufs, *sems, after)


def _pair_sum(name, grad, got, ids):
    _, rows, cols = got.shape
    rb = min(rows, 256)
    nb = rows // rb
    blk = pl.BlockSpec((None, rb, cols), lambda i, j, ids: (j, i, 0))
    mine = pl.BlockSpec((None, rb, cols), lambda i, j, ids: (j, ids[1] * nb + i, 0))
    own = pl.BlockSpec((rb, cols), lambda i, j, ids: (i, 0))

    def body(ids_ref, a_ref, b_ref, s_ref, sb_ref):
        s = a_ref[...] + b_ref[...]
        sb_ref[...] = s.astype(BF16)

        @pl.when(pl.program_id(1) == ids_ref[0])
        def _():
            s_ref[...] = s

    return _prefetch_call(body, name, ids, (nb, N_CHIPS), [mine, blk], [own, blk],
                          [_sds((rows, cols), F32), _sds(got.shape, BF16)], (grad, got),
                          semantics=("parallel", "arbitrary"))


def _chip_sum(name, own_sum, got, ids):
    rows, cols = own_sum.shape
    rb = min(rows, 256)
    nb = rows // rb
    own = pl.BlockSpec((rb, cols), lambda i, ids: (i, 0))
    blk3 = pl.BlockSpec((N_PEERS, rb, cols), lambda i, ids: (0, i, 0))
    out = pl.BlockSpec((rb, cols), lambda i, ids: (ids[1] * nb + i, 0))

    def body(ids_ref, a_ref, b_ref, o_ref):
        o_ref[...] = ((a_ref[...] + b_ref[0].astype(F32)) + b_ref[1].astype(F32)) + b_ref[2].astype(F32)

    return _prefetch_call(body, name, ids, (nb,), [own, blk3], out, _sds((2 * rows, cols), F32), (own_sum, got))


SMALL_RB = 280


def _small_pair_sum(own, got):
    blk = pl.BlockSpec((SMALL_RB, 128), lambda i: (i, 0))

    def body(a_ref, b_ref, o_ref):
        o_ref[...] = a_ref[...] + b_ref[...]

    return pl.pallas_call(body, name="small_pair_sum", grid=(own.shape[0] // SMALL_RB,), in_specs=[blk, blk],
                          out_specs=blk, out_shape=_sds(own.shape, F32),
                          compiler_params=_params(("parallel",)))(own, got)


def _small_chip_sum(pair, got, ids):
    nb = pair.shape[0] // 2 // SMALL_RB
    half = pl.BlockSpec((SMALL_RB, 128), lambda i, ids: (ids[1] * nb + i, 0))
    blk3 = pl.BlockSpec((N_PEERS, SMALL_RB, 128), lambda i, ids: (0, i, 0))

    def body(ids_ref, a_ref, b_ref, o_ref):
        o_ref[...] = (a_ref[...] + b_ref[1]) + (b_ref[0] + b_ref[2])

    return _prefetch_call(body, "small_chip_sum", ids, (nb,), [half, blk3], half, _sds(pair.shape, F32), (pair, got))


def _adamw_math(w, g, m, v):
    m = ADAM_B1 * m + (1.0 - ADAM_B1) * g
    v = ADAM_B2 * v + (1.0 - ADAM_B2) * (g * g)
    m_hat = m / (1.0 - ADAM_B1 ** ADAM_STEP)
    v_hat = v / (1.0 - ADAM_B2 ** ADAM_STEP)
    delta = -ADAM_LR * (m_hat / (jnp.sqrt(v_hat) + ADAM_EPS) + ADAM_WD * w)
    return delta, m, v


def _adamw(name, w, g, m, v, rb=None):
    rows, cols = w.shape
    rb = rows if rb is None else rb
    blk = pl.BlockSpec((rb, cols), lambda i: (i, 0))

    def body(w_ref, g_ref, m_ref, v_ref, d_ref, nm_ref, nv_ref):
        d, nm, nv = _adamw_math(w_ref[...], g_ref[...], m_ref[...], v_ref[...])
        d_ref[...] = d
        nm_ref[...] = nm
        nv_ref[...] = nv

    return pl.pallas_call(body, name=name, grid=(rows // rb,), in_specs=[blk] * 4, out_specs=[blk] * 3,
                          out_shape=[_sds(w.shape, F32)] * 3, compiler_params=_params(("parallel",)))(w, g, m, v)


def _adamw_small(ws, gs, ms, vs):
    n = len(ws)

    def body(*refs):
        for t in range(n):
            w_ref, g_ref, m_ref, v_ref = (refs[k * n + t] for k in range(4))
            d, nm, nv = _adamw_math(w_ref[...], g_ref[...], m_ref[...], v_ref[...])
            for k, val in enumerate((d, nm, nv)):
                refs[(4 + k) * n + t][...] = val

    res = pl.pallas_call(body, name="adamw_small", out_shape=[_sds(a.shape, F32) for a in ws] * 3,
                         compiler_params=_params())(*ws, *gs, *ms, *vs)
    return [(res[t], res[n + t], res[2 * n + t]) for t in range(n)]


BIG = ("w_in", "w_att_o", "w_rec_o", "w_out", "w_ff1", "w_ff2")
SHARDED_VECS = ("conv_w", "b_rg_a", "b_rg_i", "lru_lambda")
SMALL = ("ln1_g", "b_in", "rpb", "conv_w", "conv_b", "w_rg_a", "b_rg_a", "w_rg_i", "b_rg_i", "lru_lambda",
         "ln2_g", "lnf_g")
SMALL_ROWS = 2240
ORDER = ("ln1_g", "w_in", "b_in", "rpb", "w_att_o", "conv_w", "conv_b", "w_rg_a", "b_rg_a", "w_rg_i", "b_rg_i",
         "lru_lambda", "w_rec_o", "w_out", "ln2_g", "w_ff1", "w_ff2", "lnf_g")


def _pack_small(grads, loss):
    parts, sizes = [], {}
    for n in SMALL:
        flat = grads[n].reshape(-1)
        pad = (-flat.shape[0]) % 128
        sizes[n] = (flat.shape[0], flat.shape[0] + pad)
        parts.append(jnp.pad(flat, (0, pad)))
    total = sum(s[1] for s in sizes.values())
    parts.append(jnp.pad(loss.reshape(1), (0, SMALL_ROWS * 128 - total - 1)))
    return jnp.concatenate(parts).reshape(SMALL_ROWS, 128), sizes


def _unpack_small(buf, sizes, shapes):
    flat = buf.reshape(-1)
    out, pos = {}, 0
    for n in SMALL:
        size, padded = sizes[n]
        out[n] = flat[pos:pos + size].reshape(shapes[n])
        pos += padded
    return out, flat[pos]


def _gather_weights(w, chip):
    chip_id = chip.astype(jnp.int32).reshape(1)
    vec_rows = [w[n][0] for n in SHARDED_VECS]
    vec_shard = jnp.concatenate(vec_rows + [jnp.zeros((16 - 10, D // N_CHIPS), F32)], axis=0)
    vec_slots = lax.dynamic_update_slice(jnp.zeros((N_CHIPS, 16, D // N_CHIPS), F32), vec_shard[None], (chip, 0, 0))
    bufs_a, sems_a, token_a = _gather_start("gather_start_first", [_cast_bf16("cast_w_in", w["w_in"][0], chip_id), vec_slots])
    rest_names = BIG[1:]
    bufs_b, sems_b, token_b = _gather_start(
        "gather_start_rest", [_cast_bf16("cast_" + n, w[n][0], chip_id, after=(token_a,)) for n in rest_names])

    def first(after):
        w_in_full, vec_full = _gather_forward("gather_forward_first", _gather_wait("gather_wait_first", bufs_a, sems_a, after))
        vecs = vec_full.transpose(1, 0, 2).reshape(16, D)
        return dict(w_in=w_in_full, conv_w=vecs[0:4], b_rg_a=vecs[4:6], b_rg_i=vecs[6:8], lru_lambda=vecs[8:10])

    def rest(after):
        full = dict(zip(rest_names, _gather_forward("gather_forward_rest",
                                                    _gather_wait("gather_wait_rest", bufs_b, sems_b, after))))
        return dict(w_att_o=full["w_att_o"], w_ff1=full["w_ff1"], w_rec_o=full["w_rec_o"].reshape(D, D),
                    w_out=full["w_out"].reshape(D, D), w_ff2=full["w_ff2"].reshape(D_FF, D))

    p = dict(ln1_g=w["ln1_g"], b_in=w["b_in"], rpb=w["rpb"][0], conv_b=w["conv_b"], w_rg_a=w["w_rg_a"][0],
             w_rg_i=w["w_rg_i"][0], ln2_g=w["ln2_g"], lnf_g=w["lnf_g"].reshape(1, D))
    return p, ((token_b,), first, rest)


class _Reducer:
    def __init__(self, ids):
        self.ids = ids
        self.groups = {}

    def begin(self, tag, grads, small=None):
        names = list(grads)
        big = [grads[n].reshape(N_CHIPS, -1, grads[n].shape[-1]) for n in names]
        flight, token = _pair_start("pair_start_" + tag, big, [] if small is None else [small])
        self.groups[tag] = dict(names=names, pair=flight, small=small is not None)
        return (token,)

    def advance(self, tag, after):
        grp = self.groups[tag]
        n = len(grp["names"])
        mine, got = _pair_wait("pair_wait_" + tag, grp["pair"], n, after)
        sums = [_pair_sum("pair_sum_" + name, a, b, self.ids) for name, a, b in zip(grp["names"], mine, got)]
        small_sum = _small_pair_sum(mine[n], got[n]) if grp["small"] else None
        grp["chip"], token = _chip_start("chip_start_" + tag, [s[1] for s in sums], small_sum)
        grp["sums"] = [s[0] for s in sums]
        self.last_token = token
        return (token,)

    def finish(self, tag, after):
        grp = self.groups[tag]
        srcs, lands = _chip_wait("chip_wait_" + tag, grp["chip"], grp["small"], after)
        halves = [_chip_sum("chip_sum_" + name, s, b, self.ids) for name, s, b in zip(grp["names"], grp["sums"], lands)]
        if grp["small"]:
            halves.append(_small_chip_sum(srcs[-1], lands[-1], self.ids))
        grp["swap"], token = _swap_start("swap_start_" + tag, halves)
        return token

    def interlude(self, tokens):
        after = tokens[0]
        for tag in list(self.groups)[:-1]:
            after = self.finish(tag, after)
        return after

    def result(self, tag, after):
        return _swap_wait("swap_wait_" + tag, self.groups[tag]["swap"], after)


def kernel(x, ln1_g, w_in, b_in, rpb, w_att_o, conv_w, conv_b, w_rg_a, b_rg_a, w_rg_i, b_rg_i, lru_lambda, w_rec_o, w_out, ln2_g, w_ff1, w_ff2, lnf_g, loss_target, m_ln1_g, m_w_in, m_b_in, m_rpb, m_w_att_o, m_conv_w, m_conv_b, m_w_rg_a, m_b_rg_a, m_w_rg_i, m_b_rg_i, m_lru_lambda, m_w_rec_o, m_w_out, m_ln2_g, m_w_ff1, m_w_ff2, m_lnf_g, v_ln1_g, v_w_in, v_b_in, v_rpb, v_w_att_o, v_conv_w, v_conv_b, v_w_rg_a, v_b_rg_a, v_w_rg_i, v_b_rg_i, v_lru_lambda, v_w_rec_o, v_w_out, v_ln2_g, v_w_ff1, v_w_ff2, v_lnf_g):
    w = dict(ln1_g=ln1_g, w_in=w_in, b_in=b_in, rpb=rpb, w_att_o=w_att_o, conv_w=conv_w, conv_b=conv_b,
             w_rg_a=w_rg_a, b_rg_a=b_rg_a, w_rg_i=w_rg_i, b_rg_i=b_rg_i, lru_lambda=lru_lambda, w_rec_o=w_rec_o,
             w_out=w_out, ln2_g=ln2_g, w_ff1=w_ff1, w_ff2=w_ff2, lnf_g=lnf_g)
    m = dict(ln1_g=m_ln1_g, w_in=m_w_in, b_in=m_b_in, rpb=m_rpb, w_att_o=m_w_att_o, conv_w=m_conv_w,
             conv_b=m_conv_b, w_rg_a=m_w_rg_a, b_rg_a=m_b_rg_a, w_rg_i=m_w_rg_i, b_rg_i=m_b_rg_i,
             lru_lambda=m_lru_lambda, w_rec_o=m_w_rec_o, w_out=m_w_out, ln2_g=m_ln2_g, w_ff1=m_w_ff1,
             w_ff2=m_w_ff2, lnf_g=m_lnf_g)
    v = dict(ln1_g=v_ln1_g, w_in=v_w_in, b_in=v_b_in, rpb=v_rpb, w_att_o=v_w_att_o, conv_w=v_conv_w,
             conv_b=v_conv_b, w_rg_a=v_w_rg_a, b_rg_a=v_b_rg_a, w_rg_i=v_w_rg_i, b_rg_i=v_b_rg_i,
             lru_lambda=v_lru_lambda, w_rec_o=v_w_rec_o, w_out=v_w_out, ln2_g=v_ln2_g, w_ff1=v_w_ff1,
             w_ff2=v_w_ff2, lnf_g=v_lnf_g)
    chip = 2 * lax.axis_index("x") + lax.axis_index("y")
    ids = jnp.stack([chip, lax.axis_index("c")]).astype(jnp.int32)

    out_grad, out_delta, out_m, out_v = {}, {}, {}, {}

    def update(n, gn):
        shape, two_d = w[n].shape, gn.shape
        d, nm, nv = _adamw("adamw_" + n, w[n].reshape(two_d), gn, m[n].reshape(two_d), v[n].reshape(two_d), 256)
        out_grad[n], out_delta[n], out_m[n], out_v[n] = (gn.reshape(shape), d.reshape(shape), nm.reshape(shape),
                                                         nv.reshape(shape))
        return d

    reducer = _Reducer(ids)
    p, late = _gather_weights(w, chip)
    loss, grad_x, g = _local_step(x, loss_target, p, late, reducer)
    small, sizes = _pack_small(g, loss + reducer.last_token[:1, :1])
    after = reducer.begin("small", {}, small)[0]
    for tag in ("ff", "proj", "in"):
        for n, red in zip(reducer.groups[tag]["names"], reducer.result(tag, after)):
            after = update(n, red)
        if tag == "ff":
            after = reducer.finish("in", reducer.advance("small", after)[0])
    (small_red,) = reducer.result("small", reducer.finish("small", after))
    gsmall, loss = _unpack_small(small_red, sizes, {n: g[n].shape for n in SMALL})
    two_d = {n: (int(np.prod(w[n].shape[:-1])), w[n].shape[-1]) for n in SMALL}
    for n in SHARDED_VECS:
        gsmall[n] = lax.dynamic_slice_in_dim(gsmall[n], chip * (D // N_CHIPS), D // N_CHIPS, axis=1)
    gs = [gsmall[n].reshape(two_d[n]) for n in SMALL]
    updates = _adamw_small([w[n].reshape(two_d[n]) for n in SMALL], gs, [m[n].reshape(two_d[n]) for n in SMALL],
                           [v[n].reshape(two_d[n]) for n in SMALL])
    for n, gn, (d, nm, nv) in zip(SMALL, gs, updates):
        shape = w[n].shape
        out_grad[n], out_delta[n], out_m[n], out_v[n] = (gn.reshape(shape), d.reshape(shape), nm.reshape(shape),
                                                         nv.reshape(shape))
    return (loss, grad_x, *[out_grad[n] for n in ORDER], *[out_delta[n] for n in ORDER],
            *[out_m[n] for n in ORDER], *[out_v[n] for n in ORDER])
```
